```python
import jax, jax.numpy as jnp
from jax import lax
import numpy as np

D_MODEL = 1024
BATCH = 8
SEQ = 4096
DEPTH = 2

N_EVEN = (DEPTH + 1) // 2
N_ODD = DEPTH // 2
RMS_EPS = 1e-5
LN_EPS = 1e-5
D_FF = 4 * D_MODEL

GM_WIDTH = D_MODEL
GM_GROUPS = 8
GM_CH = GM_WIDTH // GM_GROUPS
GM_CHUNK = 128

SSM_D_INNER = D_MODEL
SSM_HEADDIM = 64
SSM_HEADS = SSM_D_INNER // SSM_HEADDIM
SSM_GROUPS = 4
SSM_STATE = 128
SSM_CONV = 4
SSD_CHUNK = 128
SSM_CONV_DIM = SSM_D_INNER + 2 * SSM_GROUPS * SSM_STATE

IN_EVEN = 2 * GM_WIDTH + SSM_D_INNER + SSM_CONV_DIM + SSM_HEADS
MIX_EVEN = GM_WIDTH + SSM_D_INNER

ATTN_HEADS = 16
ATTN_KV_HEADS = 2
ATTN_HEAD_DIM = 64
ATTN_WINDOW = 128
ATTN_BLOCK = ATTN_WINDOW
QKV_DIM = (ATTN_HEADS + 2 * ATTN_KV_HEADS) * ATTN_HEAD_DIM

kernel_name = "hybrid_gmlp_ssd_swa_sinks"


def rmsnorm(x, g, eps=RMS_EPS):
    xf = x.astype(jnp.float32)
    y = xf * lax.rsqrt(jnp.mean(xf * xf, axis=-1, keepdims=True) + eps)
    return (y * g.astype(jnp.float32)).astype(x.dtype)


def layernorm(x, g, b, eps=LN_EPS):
    xf = x.astype(jnp.float32)
    mu = jnp.mean(xf, axis=-1, keepdims=True)
    var = jnp.mean(jnp.square(xf - mu), axis=-1, keepdims=True)
    y = (xf - mu) * lax.rsqrt(var + eps)
    return (y * g.astype(jnp.float32) + b.astype(jnp.float32)).astype(x.dtype)


def gmlp_spatial_gating(u, v, ln_g, ln_b, w_s, b_s):
    b, s, _ = v.shape
    nc = s // GM_CHUNK
    vn = layernorm(v, ln_g, ln_b).reshape(b, nc, GM_CHUNK, GM_GROUPS, GM_CH)
    causal = jnp.tril(jnp.ones((GM_CHUNK, GM_CHUNK), dtype=bool))
    w = jnp.where(causal, w_s, 0.0).astype(v.dtype)
    mixed = jnp.einsum('gts,bcsgd->bctgd', w, vn) + b_s.T[None, None, :, :, None]
    return u * mixed.reshape(b, s, GM_WIDTH)


def causal_depthwise_conv(x, w, bias):
    k = w.shape[0]
    s = x.shape[1]
    xp = jnp.pad(x, ((0, 0), (k - 1, 0), (0, 0)))
    return sum(xp[:, i:i + s] * w[i] for i in range(k)) + bias


def ssd_chunked(x, dt, a_neg, bm, cm):
    b, s, h, p = x.shape
    g, n = bm.shape[2], bm.shape[3]
    r = h // g
    l = SSD_CHUNK
    c = s // l
    xc = x.reshape(b, c, l, g, r, p)
    bc = bm.reshape(b, c, l, g, n)
    cc = cm.reshape(b, c, l, g, n)
    dtc = dt.reshape(b, c, l, g, r)
    a_cum = jnp.cumsum(dtc * a_neg.reshape(g, r), axis=2)
    causal = jnp.tril(jnp.ones((l, l), dtype=bool))[:, :, None, None]
    seg = a_cum[:, :, :, None] - a_cum[:, :, None, :]
    decay = jnp.exp(jnp.where(causal, seg, -jnp.inf))
    cb = jnp.einsum('bclgn,bcsgn->bclsg', cc, bc)
    w_intra = (cb[..., None] * decay * dtc[:, :, None]).astype(x.dtype)
    y_diag = jnp.einsum('bclsgr,bcsgrp->bclgrp', w_intra, xc)
    to_end = (jnp.exp(a_cum[:, :, -1:] - a_cum) * dtc).astype(x.dtype)
    states = jnp.einsum('bclgn,bclgr,bclgrp->bcgrpn', bc, to_end, xc)
    chunk_decay = jnp.exp(a_cum[:, :, -1]).astype(x.dtype)

    def step(hstate, inp):
        dec, st = inp
        return dec[..., None, None] * hstate + st, hstate

    h0 = jnp.zeros((b, g, r, p, n), x.dtype)
    _, prev = lax.scan(step, h0, (jnp.moveaxis(chunk_decay, 1, 0), jnp.moveaxis(states, 1, 0)))
    prev = jnp.moveaxis(prev, 0, 1)
    y_off = jnp.einsum('bclgn,bcgrpn,bclgr->bclgrp', cc, prev, jnp.exp(a_cum).astype(x.dtype))
    return (y_diag + y_off).reshape(b, s, h, p)


def ssd_mixer(z, xbc, dt_raw, conv_w, conv_b, dt_bias, a_log, d_skip, norm_g):
    b, s, _ = z.shape
    xbc = jax.nn.silu(causal_depthwise_conv(xbc, conv_w, conv_b))
    xs, bm, cm = jnp.split(xbc, [SSM_D_INNER, SSM_D_INNER + SSM_GROUPS * SSM_STATE], axis=-1)
    xs = xs.reshape(b, s, SSM_HEADS, SSM_HEADDIM)
    bm = bm.reshape(b, s, SSM_GROUPS, SSM_STATE)
    cm = cm.reshape(b, s, SSM_GROUPS, SSM_STATE)
    dt = jax.nn.softplus(dt_raw.astype(jnp.float32) + dt_bias.astype(jnp.float32))
    a_neg = -jnp.exp(a_log.astype(jnp.float32))
    y = ssd_chunked(xs, dt, a_neg, bm, cm) + xs * d_skip[:, None]
    y = y.reshape(b, s, SSM_D_INNER) * jax.nn.silu(z)
    y = rmsnorm(y.reshape(b, s, SSM_GROUPS, -1), norm_g.reshape(SSM_GROUPS, -1))
    return y.reshape(b, s, SSM_D_INNER)


def even_mixer(h, w_in, w_out, gm_ln_g, gm_ln_b, gm_w_s, gm_b_s,
               conv_w, conv_b, dt_bias, a_log, d_skip, ssm_norm_g):
    proj = h @ w_in
    u, v, z, xbc, dt_raw = jnp.split(
        proj, [GM_WIDTH, 2 * GM_WIDTH, 2 * GM_WIDTH + SSM_D_INNER,
               2 * GM_WIDTH + SSM_D_INNER + SSM_CONV_DIM], axis=-1)
    a_out = gmlp_spatial_gating(jax.nn.gelu(u), jax.nn.gelu(v), gm_ln_g, gm_ln_b, gm_w_s, gm_b_s)
    b_out = ssd_mixer(z, xbc, dt_raw, conv_w, conv_b, dt_bias, a_log, d_skip, ssm_norm_g)
    return jnp.concatenate([a_out, b_out], axis=-1) @ w_out


def sliding_window_sink_attention(q, k, v, sinks):
    b, s, _, d = q.shape
    blk = ATTN_BLOCK
    nb = s // blk
    grp = ATTN_HEADS // ATTN_KV_HEADS
    qb = q.reshape(b, nb, blk, ATTN_KV_HEADS, grp, d)
    kp = jnp.pad(k, ((0, 0), (blk, 0), (0, 0), (0, 0))).reshape(b, nb + 1, blk, ATTN_KV_HEADS, d)
    vp = jnp.pad(v, ((0, 0), (blk, 0), (0, 0), (0, 0))).reshape(b, nb + 1, blk, ATTN_KV_HEADS, d)
    kband = jnp.concatenate([kp[:, :-1], kp[:, 1:]], axis=2)
    vband = jnp.concatenate([vp[:, :-1], vp[:, 1:]], axis=2)
    scores = jnp.einsum('bnqkgd,bnskd->bnkgqs', qb, kband).astype(jnp.float32) * (d ** -0.5)
    qpos = jnp.arange(nb)[:, None, None] * blk + jnp.arange(blk)[None, :, None]
    kpos = jnp.arange(nb)[:, None, None] * blk - blk + jnp.arange(2 * blk)[None, None, :]
    rel = qpos - kpos
    valid = (rel >= 0) & (rel < ATTN_WINDOW) & (kpos >= 0)
    scores = jnp.where(valid[None, :, None, None], scores, -jnp.inf)
    sink = sinks.astype(jnp.float32).reshape(ATTN_KV_HEADS, grp)[None, None, :, :, None, None]
    m = jnp.maximum(jnp.max(scores, axis=-1, keepdims=True), sink)
    pexp = jnp.exp(scores - m)
    denom = jnp.sum(pexp, axis=-1, keepdims=True) + jnp.exp(sink - m)
    probs = (pexp / denom).astype(v.dtype)
    out = jnp.einsum('bnkgqs,bnskd->bnqkgd', probs, vband)
    return out.reshape(b, s, ATTN_HEADS * d)


def odd_mixer(h, w_qkv, b_qkv, w_o, b_o, sinks):
    b, s, _ = h.shape
    qkv = h @ w_qkv + b_qkv
    q, k, v = jnp.split(qkv, [ATTN_HEADS * ATTN_HEAD_DIM, (ATTN_HEADS + ATTN_KV_HEADS) * ATTN_HEAD_DIM], axis=-1)
    q = q.reshape(b, s, ATTN_HEADS, ATTN_HEAD_DIM)
    k = k.reshape(b, s, ATTN_KV_HEADS, ATTN_HEAD_DIM)
    v = v.reshape(b, s, ATTN_KV_HEADS, ATTN_HEAD_DIM)
    return sliding_window_sink_attention(q, k, v, sinks) @ w_o + b_o


def squared_relu_mlp(h, w_up, w_down):
    return jnp.square(jax.nn.relu(h @ w_up)) @ w_down


def _fwd_setup_inputs(seed: int = 0) -> dict:
    key = jax.random.key(seed)
    ks = jax.random.split(key, 24)
    nrm = jax.random.normal
    f32 = jnp.float32
    dt0 = jnp.exp(jax.random.uniform(ks[10], (N_EVEN, SSM_HEADS), f32, np.log(1e-3), np.log(1e-1)))
    return {
        "x": nrm(ks[0], (BATCH, SEQ, D_MODEL), f32),
        "norm_mix_g": 1.0 + 0.02 * nrm(ks[1], (DEPTH, D_MODEL), f32),
        "norm_mlp_g": 1.0 + 0.02 * nrm(ks[2], (DEPTH, D_MODEL), f32),
        "final_norm_g": 1.0 + 0.02 * nrm(ks[3], (D_MODEL,), f32),
        "w_in_even": nrm(ks[4], (N_EVEN, D_MODEL, IN_EVEN), f32) * D_MODEL ** -0.5,
        "w_out_even": nrm(ks[5], (N_EVEN, MIX_EVEN, D_MODEL), f32) * MIX_EVEN ** -0.5,
        "gm_ln_g": 1.0 + 0.02 * nrm(ks[6], (N_EVEN, GM_WIDTH), f32),
        "gm_ln_b": 0.02 * nrm(ks[7], (N_EVEN, GM_WIDTH), f32),
        "gm_w_s": nrm(ks[8], (N_EVEN, GM_GROUPS, GM_CHUNK, GM_CHUNK), f32) * 0.5 * GM_CHUNK ** -0.5,
        "gm_b_s": 1.0 + 0.02 * nrm(ks[9], (N_EVEN, GM_GROUPS, GM_CHUNK), f32),
        "ssm_conv_w": nrm(ks[11], (N_EVEN, SSM_CONV, SSM_CONV_DIM), f32) * SSM_CONV ** -0.5,
        "ssm_conv_b": 0.02 * nrm(ks[12], (N_EVEN, SSM_CONV_DIM), f32),
        "ssm_dt_bias": dt0 + jnp.log(-jnp.expm1(-dt0)),
        "ssm_a_log": jnp.log(jax.random.uniform(ks[13], (N_EVEN, SSM_HEADS), f32, 1.0, 16.0)),
        "ssm_d": 1.0 + 0.1 * nrm(ks[14], (N_EVEN, SSM_HEADS), f32),
        "ssm_norm_g": 1.0 + 0.02 * nrm(ks[15], (N_EVEN, SSM_D_INNER), f32),
        "w_qkv": nrm(ks[16], (N_ODD, D_MODEL, QKV_DIM), f32) * D_MODEL ** -0.5,
        "b_qkv": 0.02 * nrm(ks[17], (N_ODD, QKV_DIM), f32),
        "w_o": nrm(ks[18], (N_ODD, ATTN_HEADS * ATTN_HEAD_DIM, D_MODEL), f32) * (ATTN_HEADS * ATTN_HEAD_DIM) ** -0.5,
        "b_o": 0.02 * nrm(ks[19], (N_ODD, D_MODEL), f32),
        "attn_sinks": 0.5 * nrm(ks[20], (N_ODD, ATTN_HEADS), f32),
        "w_up": nrm(ks[21], (DEPTH, D_MODEL, D_FF), f32) * D_MODEL ** -0.5,
        "w_down": nrm(ks[22], (DEPTH, D_FF, D_MODEL), f32) * D_FF ** -0.5,
    }


def _fwd_reference(x, norm_mix_g, norm_mlp_g, final_norm_g, w_in_even, w_out_even,
              gm_ln_g, gm_ln_b, gm_w_s, gm_b_s, ssm_conv_w, ssm_conv_b,
              ssm_dt_bias, ssm_a_log, ssm_d, ssm_norm_g,
              w_qkv, b_qkv, w_o, b_o, attn_sinks, w_up, w_down):
    h = x
    for i in range(DEPTH):
        j = i // 2
        y = rmsnorm(h, norm_mix_g[i])
        if i % 2 == 0:
            h = h + even_mixer(y, w_in_even[j], w_out_even[j], gm_ln_g[j], gm_ln_b[j],
                               gm_w_s[j], gm_b_s[j], ssm_conv_w[j], ssm_conv_b[j],
                               ssm_dt_bias[j], ssm_a_log[j], ssm_d[j], ssm_norm_g[j])
        else:
            h = h + odd_mixer(y, w_qkv[j], b_qkv[j], w_o[j], b_o[j], attn_sinks[j])
        y = rmsnorm(h, norm_mlp_g[i])
        h = h + squared_relu_mlp(y, w_up[i], w_down[i])
    return rmsnorm(h, final_norm_g)


import jax as _jax
import jax.numpy as _jnp

TWIN_FORMAT = 'train_step'
FWD_PARAMS = ['x', 'norm_mix_g', 'norm_mlp_g', 'final_norm_g', 'w_in_even', 'w_out_even', 'gm_ln_g', 'gm_ln_b', 'gm_w_s', 'gm_b_s', 'ssm_conv_w', 'ssm_conv_b', 'ssm_dt_bias', 'ssm_a_log', 'ssm_d', 'ssm_norm_g', 'w_qkv', 'b_qkv', 'w_o', 'b_o', 'attn_sinks', 'w_up', 'w_down']
TWIN_WEIGHTS = ['norm_mix_g', 'norm_mlp_g', 'final_norm_g', 'w_in_even', 'w_out_even', 'gm_ln_g', 'gm_ln_b', 'gm_w_s', 'gm_b_s', 'ssm_conv_w', 'ssm_conv_b', 'ssm_dt_bias', 'ssm_a_log', 'ssm_d', 'ssm_norm_g', 'w_qkv', 'b_qkv', 'w_o', 'b_o', 'attn_sinks', 'w_up', 'w_down']
TWIN_DIFF_INPUT = 'x'
TWIN_INPUTS = ['x', 'norm_mix_g', 'norm_mlp_g', 'final_norm_g', 'w_in_even', 'w_out_even', 'gm_ln_g', 'gm_ln_b', 'gm_w_s', 'gm_b_s', 'ssm_conv_w', 'ssm_conv_b', 'ssm_dt_bias', 'ssm_a_log', 'ssm_d', 'ssm_norm_g', 'w_qkv', 'b_qkv', 'w_o', 'b_o', 'attn_sinks', 'w_up', 'w_down', 'loss_target', 'm_norm_mix_g', 'm_norm_mlp_g', 'm_final_norm_g', 'm_w_in_even', 'm_w_out_even', 'm_gm_ln_g', 'm_gm_ln_b', 'm_gm_w_s', 'm_gm_b_s', 'm_ssm_conv_w', 'm_ssm_conv_b', 'm_ssm_dt_bias', 'm_ssm_a_log', 'm_ssm_d', 'm_ssm_norm_g', 'm_w_qkv', 'm_b_qkv', 'm_w_o', 'm_b_o', 'm_attn_sinks', 'm_w_up', 'm_w_down', 'v_norm_mix_g', 'v_norm_mlp_g', 'v_final_norm_g', 'v_w_in_even', 'v_w_out_even', 'v_gm_ln_g', 'v_gm_ln_b', 'v_gm_w_s', 'v_gm_b_s', 'v_ssm_conv_w', 'v_ssm_conv_b', 'v_ssm_dt_bias', 'v_ssm_a_log', 'v_ssm_d', 'v_ssm_norm_g', 'v_w_qkv', 'v_b_qkv', 'v_w_o', 'v_b_o', 'v_attn_sinks', 'v_w_up', 'v_w_down']
TWIN_OUTPUTS = ['loss', 'grad_x', 'grad_norm_mix_g', 'grad_norm_mlp_g', 'grad_final_norm_g', 'grad_w_in_even', 'grad_w_out_even', 'grad_gm_ln_g', 'grad_gm_ln_b', 'grad_gm_w_s', 'grad_gm_b_s', 'grad_ssm_conv_w', 'grad_ssm_conv_b', 'grad_ssm_dt_bias', 'grad_ssm_a_log', 'grad_ssm_d', 'grad_ssm_norm_g', 'grad_w_qkv', 'grad_b_qkv', 'grad_w_o', 'grad_b_o', 'grad_attn_sinks', 'grad_w_up', 'grad_w_down', 'delta_norm_mix_g', 'delta_norm_mlp_g', 'delta_final_norm_g', 'delta_w_in_even', 'delta_w_out_even', 'delta_gm_ln_g', 'delta_gm_ln_b', 'delta_gm_w_s', 'delta_gm_b_s', 'delta_ssm_conv_w', 'delta_ssm_conv_b', 'delta_ssm_dt_bias', 'delta_ssm_a_log', 'delta_ssm_d', 'delta_ssm_norm_g', 'delta_w_qkv', 'delta_b_qkv', 'delta_w_o', 'delta_b_o', 'delta_attn_sinks', 'delta_w_up', 'delta_w_down', 'new_m_norm_mix_g', 'new_m_norm_mlp_g', 'new_m_final_norm_g', 'new_m_w_in_even', 'new_m_w_out_even', 'new_m_gm_ln_g', 'new_m_gm_ln_b', 'new_m_gm_w_s', 'new_m_gm_b_s', 'new_m_ssm_conv_w', 'new_m_ssm_conv_b', 'new_m_ssm_dt_bias', 'new_m_ssm_a_log', 'new_m_ssm_d', 'new_m_ssm_norm_g', 'new_m_w_qkv', 'new_m_b_qkv', 'new_m_w_o', 'new_m_b_o', 'new_m_attn_sinks', 'new_m_w_up', 'new_m_w_down', 'new_v_norm_mix_g', 'new_v_norm_mlp_g', 'new_v_final_norm_g', 'new_v_w_in_even', 'new_v_w_out_even', 'new_v_gm_ln_g', 'new_v_gm_ln_b', 'new_v_gm_w_s', 'new_v_gm_b_s', 'new_v_ssm_conv_w', 'new_v_ssm_conv_b', 'new_v_ssm_dt_bias', 'new_v_ssm_a_log', 'new_v_ssm_d', 'new_v_ssm_norm_g', 'new_v_w_qkv', 'new_v_b_qkv', 'new_v_w_o', 'new_v_b_o', 'new_v_attn_sinks', 'new_v_w_up', 'new_v_w_down']
TWIN_LEAF_KINDS = {'loss': 'loss', 'grad_x': 'grad_x', 'grad_norm_mix_g': 'grad_w', 'grad_norm_mlp_g': 'grad_w', 'grad_final_norm_g': 'grad_w', 'grad_w_in_even': 'grad_w', 'grad_w_out_even': 'grad_w', 'grad_gm_ln_g': 'grad_w', 'grad_gm_ln_b': 'grad_w', 'grad_gm_w_s': 'grad_w', 'grad_gm_b_s': 'grad_w', 'grad_ssm_conv_w': 'grad_w', 'grad_ssm_conv_b': 'grad_w', 'grad_ssm_dt_bias': 'grad_w', 'grad_ssm_a_log': 'grad_w', 'grad_ssm_d': 'grad_w', 'grad_ssm_norm_g': 'grad_w', 'grad_w_qkv': 'grad_w', 'grad_b_qkv': 'grad_w', 'grad_w_o': 'grad_w', 'grad_b_o': 'grad_w', 'grad_attn_sinks': 'grad_w', 'grad_w_up': 'grad_w', 'grad_w_down': 'grad_w', 'delta_norm_mix_g': 'delta_w', 'delta_norm_mlp_g': 'delta_w', 'delta_final_norm_g': 'delta_w', 'delta_w_in_even': 'delta_w', 'delta_w_out_even': 'delta_w', 'delta_gm_ln_g': 'delta_w', 'delta_gm_ln_b': 'delta_w', 'delta_gm_w_s': 'delta_w', 'delta_gm_b_s': 'delta_w', 'delta_ssm_conv_w': 'delta_w', 'delta_ssm_conv_b': 'delta_w', 'delta_ssm_dt_bias': 'delta_w', 'delta_ssm_a_log': 'delta_w', 'delta_ssm_d': 'delta_w', 'delta_ssm_norm_g': 'delta_w', 'delta_w_qkv': 'delta_w', 'delta_b_qkv': 'delta_w', 'delta_w_o': 'delta_w', 'delta_b_o': 'delta_w', 'delta_attn_sinks': 'delta_w', 'delta_w_up': 'delta_w', 'delta_w_down': 'delta_w', 'new_m_norm_mix_g': 'new_m', 'new_m_norm_mlp_g': 'new_m', 'new_m_final_norm_g': 'new_m', 'new_m_w_in_even': 'new_m', 'new_m_w_out_even': 'new_m', 'new_m_gm_ln_g': 'new_m', 'new_m_gm_ln_b': 'new_m', 'new_m_gm_w_s': 'new_m', 'new_m_gm_b_s': 'new_m', 'new_m_ssm_conv_w': 'new_m', 'new_m_ssm_conv_b': 'new_m', 'new_m_ssm_dt_bias': 'new_m', 'new_m_ssm_a_log': 'new_m', 'new_m_ssm_d': 'new_m', 'new_m_ssm_norm_g': 'new_m', 'new_m_w_qkv': 'new_m', 'new_m_b_qkv': 'new_m', 'new_m_w_o': 'new_m', 'new_m_b_o': 'new_m', 'new_m_attn_sinks': 'new_m', 'new_m_w_up': 'new_m', 'new_m_w_down': 'new_m', 'new_v_norm_mix_g': 'new_v', 'new_v_norm_mlp_g': 'new_v', 'new_v_final_norm_g': 'new_v', 'new_v_w_in_even': 'new_v', 'new_v_w_out_even': 'new_v', 'new_v_gm_ln_g': 'new_v', 'new_v_gm_ln_b': 'new_v', 'new_v_gm_w_s': 'new_v', 'new_v_gm_b_s': 'new_v', 'new_v_ssm_conv_w': 'new_v', 'new_v_ssm_conv_b': 'new_v', 'new_v_ssm_dt_bias': 'new_v', 'new_v_ssm_a_log': 'new_v', 'new_v_ssm_d': 'new_v', 'new_v_ssm_norm_g': 'new_v', 'new_v_w_qkv': 'new_v', 'new_v_b_qkv': 'new_v', 'new_v_w_o': 'new_v', 'new_v_b_o': 'new_v', 'new_v_attn_sinks': 'new_v', 'new_v_w_up': 'new_v', 'new_v_w_down': 'new_v'}


def _forward(args):
    return _fwd_reference(*[args[k] for k in FWD_PARAMS])


def _output_shape():
    out = _jax.eval_shape(lambda: _forward(_fwd_setup_inputs(0)))
    return out.shape, out.dtype

N_MICROBATCH = 1
ADAM_LR = 0.001
ADAM_B1 = 0.9
ADAM_B2 = 0.999
ADAM_EPS = 1e-08
ADAM_WD = 0.01
ADAM_STEP = 10
PER_EXAMPLE_BATCH_AXIS = {'x': 0, 'loss_target': 0}
SHARED_INPUTS = []
_WEIGHT_DTYPES = {'norm_mix_g': _jnp.float32, 'norm_mlp_g': _jnp.float32, 'final_norm_g': _jnp.float32, 'w_in_even': _jnp.float32, 'w_out_even': _jnp.float32, 'gm_ln_g': _jnp.float32, 'gm_ln_b': _jnp.float32, 'gm_w_s': _jnp.float32, 'gm_b_s': _jnp.float32, 'ssm_conv_w': _jnp.float32, 'ssm_conv_b': _jnp.float32, 'ssm_dt_bias': _jnp.float32, 'ssm_a_log': _jnp.float32, 'ssm_d': _jnp.float32, 'ssm_norm_g': _jnp.float32, 'w_qkv': _jnp.float32, 'b_qkv': _jnp.float32, 'w_o': _jnp.float32, 'b_o': _jnp.float32, 'attn_sinks': _jnp.float32, 'w_up': _jnp.float32, 'w_down': _jnp.float32}
MOMENT_SCALE = {'norm_mix_g': 1.324498e-01, 'norm_mlp_g': 1.306682e-01, 'final_norm_g': 3.283323e+01, 'w_in_even': 8.117101e-02, 'w_out_even': 1.348168e-01, 'gm_ln_g': 2.444950e-02, 'gm_ln_b': 2.624709e-02, 'gm_w_s': 4.872808e-02, 'gm_b_s': 7.042277e-02, 'ssm_conv_w': 8.323188e-02, 'ssm_conv_b': 1.058691e-01, 'ssm_dt_bias': 1.845934e-01, 'ssm_a_log': 3.338288e-01, 'ssm_d': 5.945961e-01, 'ssm_norm_g': 1.073944e-01, 'w_qkv': 5.466234e-02, 'b_qkv': 1.948423e-01, 'w_o': 5.213790e-02, 'b_o': 1.744706e-01, 'attn_sinks': 2.090606e-02, 'w_up': 6.661486e-02, 'w_down': 1.365244e-01}


def _to_microbatches(a, axis):
    t = _jnp.moveaxis(a, axis, 0)
    t = t.reshape((N_MICROBATCH, t.shape[0] // N_MICROBATCH) + t.shape[1:])
    return _jnp.moveaxis(t, 1, axis + 1)


def setup_inputs(seed: int = 0) -> dict:
    inp = _fwd_setup_inputs(seed)
    key = _jax.random.fold_in(_jax.random.key(seed), 7919)
    shape, _ = _output_shape()
    out = dict(inp)
    out["loss_target"] = _jax.random.normal(_jax.random.fold_in(key, 0), shape, _jnp.float32)
    for i, name in enumerate(TWIN_WEIGHTS):
        w = inp[name].astype(_jnp.float32)
        if MOMENT_SCALE is None:
            s = _jnp.sqrt(_jnp.mean(_jnp.square(w)) + 1e-30)
        else:
            s = MOMENT_SCALE[name]
        km, kv = _jax.random.split(_jax.random.fold_in(key, i + 1))
        out[name] = w
        out["m_" + name] = s * _jax.random.normal(km, w.shape, _jnp.float32)
        out["v_" + name] = (s * s) * _jax.random.uniform(kv, w.shape, _jnp.float32, 0.5, 1.5)
    if N_MICROBATCH > 1:
        for name, axis in PER_EXAMPLE_BATCH_AXIS.items():
            out[name] = _to_microbatches(out[name], axis)
    return {'x': out['x'], 'norm_mix_g': out['norm_mix_g'], 'norm_mlp_g': out['norm_mlp_g'], 'final_norm_g': out['final_norm_g'], 'w_in_even': out['w_in_even'], 'w_out_even': out['w_out_even'], 'gm_ln_g': out['gm_ln_g'], 'gm_ln_b': out['gm_ln_b'], 'gm_w_s': out['gm_w_s'], 'gm_b_s': out['gm_b_s'], 'ssm_conv_w': out['ssm_conv_w'], 'ssm_conv_b': out['ssm_conv_b'], 'ssm_dt_bias': out['ssm_dt_bias'], 'ssm_a_log': out['ssm_a_log'], 'ssm_d': out['ssm_d'], 'ssm_norm_g': out['ssm_norm_g'], 'w_qkv': out['w_qkv'], 'b_qkv': out['b_qkv'], 'w_o': out['w_o'], 'b_o': out['b_o'], 'attn_sinks': out['attn_sinks'], 'w_up': out['w_up'], 'w_down': out['w_down'], 'loss_target': out['loss_target'], 'm_norm_mix_g': out['m_norm_mix_g'], 'm_norm_mlp_g': out['m_norm_mlp_g'], 'm_final_norm_g': out['m_final_norm_g'], 'm_w_in_even': out['m_w_in_even'], 'm_w_out_even': out['m_w_out_even'], 'm_gm_ln_g': out['m_gm_ln_g'], 'm_gm_ln_b': out['m_gm_ln_b'], 'm_gm_w_s': out['m_gm_w_s'], 'm_gm_b_s': out['m_gm_b_s'], 'm_ssm_conv_w': out['m_ssm_conv_w'], 'm_ssm_conv_b': out['m_ssm_conv_b'], 'm_ssm_dt_bias': out['m_ssm_dt_bias'], 'm_ssm_a_log': out['m_ssm_a_log'], 'm_ssm_d': out['m_ssm_d'], 'm_ssm_norm_g': out['m_ssm_norm_g'], 'm_w_qkv': out['m_w_qkv'], 'm_b_qkv': out['m_b_qkv'], 'm_w_o': out['m_w_o'], 'm_b_o': out['m_b_o'], 'm_attn_sinks': out['m_attn_sinks'], 'm_w_up': out['m_w_up'], 'm_w_down': out['m_w_down'], 'v_norm_mix_g': out['v_norm_mix_g'], 'v_norm_mlp_g': out['v_norm_mlp_g'], 'v_final_norm_g': out['v_final_norm_g'], 'v_w_in_even': out['v_w_in_even'], 'v_w_out_even': out['v_w_out_even'], 'v_gm_ln_g': out['v_gm_ln_g'], 'v_gm_ln_b': out['v_gm_ln_b'], 'v_gm_w_s': out['v_gm_w_s'], 'v_gm_b_s': out['v_gm_b_s'], 'v_ssm_conv_w': out['v_ssm_conv_w'], 'v_ssm_conv_b': out['v_ssm_conv_b'], 'v_ssm_dt_bias': out['v_ssm_dt_bias'], 'v_ssm_a_log': out['v_ssm_a_log'], 'v_ssm_d': out['v_ssm_d'], 'v_ssm_norm_g': out['v_ssm_norm_g'], 'v_w_qkv': out['v_w_qkv'], 'v_b_qkv': out['v_b_qkv'], 'v_w_o': out['v_w_o'], 'v_b_o': out['v_b_o'], 'v_attn_sinks': out['v_attn_sinks'], 'v_w_up': out['v_w_up'], 'v_w_down': out['v_w_down']}


def _loss(weights, diff, rest, loss_target):
    with _jax.named_scope("forward"):
        args = {**rest, TWIN_DIFF_INPUT: diff, **{k: w.astype(_WEIGHT_DTYPES[k]) for k, w in weights.items()}}
        y = _forward(args)
    with _jax.named_scope("loss_head"):
        err = _jnp.square(y.astype(_jnp.float32) - loss_target)
        return 0.5 * _jnp.sum(_jnp.mean(err, axis=-1)) if err.ndim else 0.5 * err


def _adamw(w, g, m, v):
    m = ADAM_B1 * m + (1.0 - ADAM_B1) * g
    v = ADAM_B2 * v + (1.0 - ADAM_B2) * _jnp.square(g)
    m_hat = m / (1.0 - ADAM_B1 ** ADAM_STEP)
    v_hat = v / (1.0 - ADAM_B2 ** ADAM_STEP)
    delta = -ADAM_LR * (m_hat / (_jnp.sqrt(v_hat) + ADAM_EPS) + ADAM_WD * w)
    return delta, m, v


def reference(x, norm_mix_g, norm_mlp_g, final_norm_g, w_in_even, w_out_even, gm_ln_g, gm_ln_b, gm_w_s, gm_b_s, ssm_conv_w, ssm_conv_b, ssm_dt_bias, ssm_a_log, ssm_d, ssm_norm_g, w_qkv, b_qkv, w_o, b_o, attn_sinks, w_up, w_down, loss_target, m_norm_mix_g, m_norm_mlp_g, m_final_norm_g, m_w_in_even, m_w_out_even, m_gm_ln_g, m_gm_ln_b, m_gm_w_s, m_gm_b_s, m_ssm_conv_w, m_ssm_conv_b, m_ssm_dt_bias, m_ssm_a_log, m_ssm_d, m_ssm_norm_g, m_w_qkv, m_b_qkv, m_w_o, m_b_o, m_attn_sinks, m_w_up, m_w_down, v_norm_mix_g, v_norm_mlp_g, v_final_norm_g, v_w_in_even, v_w_out_even, v_gm_ln_g, v_gm_ln_b, v_gm_w_s, v_gm_b_s, v_ssm_conv_w, v_ssm_conv_b, v_ssm_dt_bias, v_ssm_a_log, v_ssm_d, v_ssm_norm_g, v_w_qkv, v_b_qkv, v_w_o, v_b_o, v_attn_sinks, v_w_up, v_w_down):
    given = dict(x=x, norm_mix_g=norm_mix_g, norm_mlp_g=norm_mlp_g, final_norm_g=final_norm_g, w_in_even=w_in_even, w_out_even=w_out_even, gm_ln_g=gm_ln_g, gm_ln_b=gm_ln_b, gm_w_s=gm_w_s, gm_b_s=gm_b_s, ssm_conv_w=ssm_conv_w, ssm_conv_b=ssm_conv_b, ssm_dt_bias=ssm_dt_bias, ssm_a_log=ssm_a_log, ssm_d=ssm_d, ssm_norm_g=ssm_norm_g, w_qkv=w_qkv, b_qkv=b_qkv, w_o=w_o, b_o=b_o, attn_sinks=attn_sinks, w_up=w_up, w_down=w_down, loss_target=loss_target, m_norm_mix_g=m_norm_mix_g, m_norm_mlp_g=m_norm_mlp_g, m_final_norm_g=m_final_norm_g, m_w_in_even=m_w_in_even, m_w_out_even=m_w_out_even, m_gm_ln_g=m_gm_ln_g, m_gm_ln_b=m_gm_ln_b, m_gm_w_s=m_gm_w_s, m_gm_b_s=m_gm_b_s, m_ssm_conv_w=m_ssm_conv_w, m_ssm_conv_b=m_ssm_conv_b, m_ssm_dt_bias=m_ssm_dt_bias, m_ssm_a_log=m_ssm_a_log, m_ssm_d=m_ssm_d, m_ssm_norm_g=m_ssm_norm_g, m_w_qkv=m_w_qkv, m_b_qkv=m_b_qkv, m_w_o=m_w_o, m_b_o=m_b_o, m_attn_sinks=m_attn_sinks, m_w_up=m_w_up, m_w_down=m_w_down, v_norm_mix_g=v_norm_mix_g, v_norm_mlp_g=v_norm_mlp_g, v_final_norm_g=v_final_norm_g, v_w_in_even=v_w_in_even, v_w_out_even=v_w_out_even, v_gm_ln_g=v_gm_ln_g, v_gm_ln_b=v_gm_ln_b, v_gm_w_s=v_gm_w_s, v_gm_b_s=v_gm_b_s, v_ssm_conv_w=v_ssm_conv_w, v_ssm_conv_b=v_ssm_conv_b, v_ssm_dt_bias=v_ssm_dt_bias, v_ssm_a_log=v_ssm_a_log, v_ssm_d=v_ssm_d, v_ssm_norm_g=v_ssm_norm_g, v_w_qkv=v_w_qkv, v_b_qkv=v_b_qkv, v_w_o=v_w_o, v_b_o=v_b_o, v_attn_sinks=v_attn_sinks, v_w_up=v_w_up, v_w_down=v_w_down)
    weights = {n: given[n] for n in TWIN_WEIGHTS}
    shared = {n: given[n] for n in SHARED_INPUTS}
    per_example = {n: given[n] for n in ['x']}
    grad_fn = _jax.value_and_grad(_loss, argnums=(0, 1))

    def one_microbatch(ex, loss_target):
        ex = dict(ex)
        diff = ex.pop(TWIN_DIFF_INPUT)
        return grad_fn(weights, diff, {**shared, **ex}, loss_target)

    if N_MICROBATCH == 1:
        loss, (grad_w, grad_x) = one_microbatch(per_example, given["loss_target"])
    else:
        def body(carry, xs):
            loss_sum, grad_sum = carry
            l_k, (gw_k, gx_k) = one_microbatch(xs[0], xs[1])
            with _jax.named_scope("update"):
                return (loss_sum + l_k, _jax.tree.map(_jnp.add, grad_sum, gw_k)), gx_k

        init = (_jnp.zeros((), _jnp.float32), _jax.tree.map(_jnp.zeros_like, weights))
        (loss, grad_w), grad_x = _jax.lax.scan(body, init, (per_example, given["loss_target"]))
    with _jax.named_scope("update"):
        delta_w, new_m, new_v = {}, {}, {}
        for n in TWIN_WEIGHTS:
            delta_w[n], new_m[n], new_v[n] = _adamw(weights[n], grad_w[n], given["m_" + n], given["v_" + n])
    return (loss, grad_x, *[grad_w[n] for n in TWIN_WEIGHTS], *[delta_w[n] for n in TWIN_WEIGHTS],
            *[new_m[n] for n in TWIN_WEIGHTS], *[new_v[n] for n in TWIN_WEIGHTS])
```

```python
import functools

import jax
import jax.numpy as jnp
from jax import lax
from jax.experimental import pallas as pl
from jax.experimental.pallas import tpu as pltpu

f32 = jnp.float32
bf16 = jnp.bfloat16
MXU_DTYPE = bf16

RMS_EPS = 1e-5
LN_EPS = 1e-5
D_MODEL = 1024
D_FF = 4096
CH = 128
N_BLK = 8
SSM_HEADS = 16
IN_EVEN = 5136
NP_IN = 5376
OFF_U, OFF_V, OFF_Z, OFF_X, OFF_DT = 0, 1024, 2048, 3072, 5120
XBC_BLKS = 16
QKV_DIM = 1280
ATT_SCALE = 64 ** -0.5

ADAM_LR = 0.001
ADAM_B1 = 0.9
ADAM_B2 = 0.999
ADAM_EPS = 1e-08
ADAM_WD = 0.01
ADAM_STEP = 10

VMEM_LIMIT_BYTES = 48 * 1024 * 1024
N_CHIPS = 4
PACK_ROWS = 6656
HALF_ROWS = PACK_ROWS // 2
SMALL_ROWS = 160

NN = ((1,), (0,))
NT = ((1,), (1,))
TN = ((0,), (0,))


def _mm(a, b, dims):
    return lax.dot_general(a.astype(MXU_DTYPE), b.astype(MXU_DTYPE), (dims, ((), ())),
                           preferred_element_type=f32)


def _mm_exact(a, b):
    return jnp.dot(a, b, preferred_element_type=f32, precision=lax.Precision.HIGHEST)


def _cparams(sem=None):
    return pltpu.CompilerParams(dimension_semantics=sem, vmem_limit_bytes=VMEM_LIMIT_BYTES)


@jax.custom_vjp
def _swap64(x):
    return pltpu.roll(x, 64, axis=1)


_swap64.defvjp(lambda x: (pltpu.roll(x, 64, axis=1), None), lambda _, g: (pltpu.roll(g, 64, axis=1),))


def _make_delay(k):
    @jax.custom_vjp
    def delay(ext):
        return pltpu.roll(ext, k, axis=0)[8:, :]

    def fwd(ext):
        return delay(ext), None

    def bwd(_, g):
        gp = jnp.concatenate([jnp.zeros((8, g.shape[1]), g.dtype), g], axis=0)
        return (pltpu.roll(gp, gp.shape[0] - k, axis=0),)

    delay.defvjp(fwd, bwd)
    return delay


_DELAYS = {k: _make_delay(k) for k in (1, 2, 3)}


def _col(m, lane, h):
    return jnp.sum(jnp.where(lane == h, m, 0.0), axis=1, keepdims=True)


def _row(m, sub, h):
    return jnp.sum(jnp.where(sub == h, m, 0.0), axis=0, keepdims=True)


def _mixer_chunk(us, vs, zs, xbcs, halos, dtblk, hps, prm):
    lane = lax.broadcasted_iota(jnp.int32, (CH, CH), 1)
    sub = lax.broadcasted_iota(jnp.int32, (CH, CH), 0)
    left = lane < 64
    top = sub < 64
    causal = sub >= lane

    gus = [jax.nn.gelu(u) for u in us]
    gvs = [jax.nn.gelu(v) for v in vs]
    mu = sum(jnp.sum(g, axis=1, keepdims=True) for g in gvs) / D_MODEL
    cen = [g - mu for g in gvs]
    var = sum(jnp.sum(c * c, axis=1, keepdims=True) for c in cen) / D_MODEL
    rstd = lax.rsqrt(var + LN_EPS)
    a_out = []
    for g in range(N_BLK):
        vn = cen[g] * rstd * prm["ln_g"][g] + prm["ln_b"][g]
        w = jnp.where(causal, prm["wm"][g], 0.0)
        mixed = _mm(w, vn, NN) + _col(prm["bs_t"], lane, g)
        a_out.append(gus[g] * mixed)

    act = []
    for b in range(XBC_BLKS):
        w8 = prm["conv_w"][b]
        sub8 = lax.broadcasted_iota(jnp.int32, w8.shape, 0)
        ext = jnp.concatenate([halos[b], xbcs[b]], axis=0)
        conv = xbcs[b] * _row(w8, sub8, 3) + prm["conv_b"][b]
        for k in (1, 2, 3):
            conv = conv + _DELAYS[k](ext) * _row(w8, sub8, 3 - k)
        act.append(jax.nn.silu(conv))

    dt = jax.nn.softplus(dtblk + prm["dt_bias"])
    a_neg = -jnp.exp(prm["a_log"])
    tri = causal.astype(f32)
    acum = _mm_exact(tri, dt * a_neg)
    acum_t = acum.T
    dt_t = dt.T
    last = sub == CH - 1
    ys, h_out = [], []
    for grp in range(4):
        bm = act[8 + grp]
        cm = act[12 + grp]
        cb = _mm(cm, bm, NT)
        for p in (2 * grp, 2 * grp + 1):
            h0, h1 = 2 * p, 2 * p + 1
            xp = act[p]
            hp = hps[p]
            wis = []
            for h in (h0, h1):
                seg = _col(acum, lane, h) - _row(acum_t, sub, h)
                decay = jnp.exp(jnp.where(causal, seg, -jnp.inf))
                wis.append(cb * decay * _row(dt_t, sub, h))
            wcat = jnp.concatenate(wis, axis=1)
            xbd = jnp.concatenate([jnp.where(left, xp, 0.0), jnp.where(left, 0.0, xp)], axis=0)
            y_diag = _mm(wcat, xbd, NN)
            a_end = [jnp.sum(jnp.where(last & (lane == h), acum, 0.0), keepdims=True) for h in (h0, h1)]
            a_col = jnp.where(left, _col(acum, lane, h0), _col(acum, lane, h1))
            dt_col = jnp.where(left, _col(dt, lane, h0), _col(dt, lane, h1))
            to_end = jnp.exp(jnp.where(left, a_end[0], a_end[1]) - a_col) * dt_col
            states = _mm(xp * to_end, bm, TN)
            chunk_decay = jnp.where(top, jnp.exp(a_end[0]), jnp.exp(a_end[1]))
            h_out.append(chunk_decay * hp + states)
            y_off = jnp.exp(a_col) * _mm(cm, hp, NT)
            d_skip = jnp.where(left[:1], _col(prm["d_heads"], lane[:1], h0), _col(prm["d_heads"], lane[:1], h1))
            ys.append((y_diag + y_off + xp * d_skip) * jax.nn.silu(zs[p]))

    b_out = []
    for grp in range(4):
        pair = (ys[2 * grp], ys[2 * grp + 1])
        ms = sum(jnp.sum(y * y, axis=1, keepdims=True) for y in pair) / 256.0
        r = lax.rsqrt(ms + RMS_EPS)
        for j, y in enumerate(pair):
            b_out.append(y * r * prm["norm_g"][2 * grp + j])
    return a_out, b_out, h_out


def _attn_block(qps, kprev, kcur, vprev, vcur, sink_row, first):
    lane = lax.broadcasted_iota(jnp.int32, (CH, CH), 1)
    left = lane < 64
    kband = jnp.concatenate([kprev, kcur], axis=0)
    vband = jnp.concatenate([vprev, vcur], axis=0)
    left2 = lax.broadcasted_iota(jnp.int32, kband.shape, 1) < 64
    ksw, vsw = _swap64(kband), _swap64(vband)
    kdup = [jnp.where(left2, kband, ksw), jnp.where(left2, ksw, kband)]
    vdup = [jnp.where(left2, vband, vsw), jnp.where(left2, vsw, vband)]
    qi = lax.broadcasted_iota(jnp.int32, (CH, 2 * CH), 0)
    si = lax.broadcasted_iota(jnp.int32, (CH, 2 * CH), 1)
    rel = qi + CH - si
    valid = (rel >= 0) & (rel < CH) & (jnp.logical_not(first) | (si >= CH))
    outs = []
    for p in range(N_BLK):
        j = p // 4
        halves = []
        for side, h in ((0, 2 * p), (1, 2 * p + 1)):
            qh = jnp.where(left, qps[p], 0.0) if side == 0 else jnp.where(left, 0.0, qps[p])
            s = _mm(qh, kdup[j], NT) * ATT_SCALE
            s = jnp.where(valid, s, -jnp.inf)
            sink = _col(sink_row, lane[:1], h)
            m = lax.stop_gradient(jnp.maximum(jnp.max(s, axis=1, keepdims=True), sink))
            pexp = jnp.exp(s - m)
            denom = jnp.sum(pexp, axis=1, keepdims=True) + jnp.exp(sink - m)
            halves.append(_mm(pexp / denom, vdup[j], NN))
        outs.append(jnp.where(left, halves[0], halves[1]))
    return outs


def _rmsnorm(x, g):
    r = lax.rsqrt(jnp.mean(x * x, axis=-1, keepdims=True) + RMS_EPS)
    return x * r * g


def rmsnorm_fwd(x, g_row, name):
    s, d = x.shape
    tm = min(512, s)

    def body(x_ref, g_ref, y_ref):
        y_ref[...] = _rmsnorm(x_ref[...], g_ref[...]).astype(bf16)

    return pl.pallas_call(
        body, name=name, grid=(s // tm,),
        in_specs=[pl.BlockSpec((tm, d), lambda i: (i, 0)), pl.BlockSpec((1, d), lambda i: (0, 0))],
        out_specs=pl.BlockSpec((tm, d), lambda i: (i, 0)),
        out_shape=jax.ShapeDtypeStruct((s, d), bf16),
        compiler_params=_cparams(("parallel",)),
    )(x, g_row)


def rmsnorm_bwd(x, g_row, dy, res, name):
    s, d = x.shape
    tm = min(512, s)

    def body(x_ref, g_ref, dy_ref, res_ref, dx_ref, dg_ref):
        @pl.when(pl.program_id(0) == 0)
        def _():
            dg_ref[...] = jnp.zeros_like(dg_ref)

        _, vjp = jax.vjp(_rmsnorm, x_ref[...], g_ref[...])
        dx, dg = vjp(dy_ref[...])
        dx_ref[...] = res_ref[...] + dx
        dg_ref[...] += dg

    tile = pl.BlockSpec((tm, d), lambda i: (i, 0))
    row = pl.BlockSpec((1, d), lambda i: (0, 0))
    return pl.pallas_call(
        body, name=name, grid=(s // tm,),
        in_specs=[tile, row, tile, tile], out_specs=[tile, row],
        out_shape=[jax.ShapeDtypeStruct((s, d), f32), jax.ShapeDtypeStruct((1, d), f32)],
        compiler_params=_cparams(("arbitrary",)),
    )(x, g_row, dy, res)


def final_loss(h, g_row, target, name):
    s, d = h.shape
    tm = min(512, s)

    def body(h_ref, g_ref, t_ref, loss_ref, dh_ref, dg_ref):
        @pl.when(pl.program_id(0) == 0)
        def _():
            dg_ref[...] = jnp.zeros_like(dg_ref)
            loss_ref[...] = jnp.zeros_like(loss_ref)

        def f(hv, gv):
            err = jnp.square(_rmsnorm(hv, gv) - t_ref[...])
            return 0.5 * jnp.sum(jnp.mean(err, axis=-1, keepdims=True), axis=0, keepdims=True)

        loss, vjp = jax.vjp(f, h_ref[...], g_ref[...])
        dh, dg = vjp(jnp.ones_like(loss))
        dh_ref[...] = dh
        dg_ref[...] += dg
        loss_ref[...] += jnp.broadcast_to(loss, loss_ref.shape)

    tile = pl.BlockSpec((tm, d), lambda i: (i, 0))
    row = pl.BlockSpec((1, d), lambda i: (0, 0))
    return pl.pallas_call(
        body, name=name, grid=(s // tm,),
        in_specs=[tile, row, tile],
        out_specs=[pl.BlockSpec((1, 128), lambda i: (0, 0)), tile, row],
        out_shape=[jax.ShapeDtypeStruct((1, 128), f32), jax.ShapeDtypeStruct((s, d), f32),
                   jax.ShapeDtypeStruct((1, d), f32)],
        compiler_params=_cparams(("arbitrary",)),
    )(h, g_row, target)


def colsum(x, name):
    s, n = x.shape
    tm = min(512, s)

    def body(x_ref, o_ref):
        @pl.when(pl.program_id(0) == 0)
        def _():
            o_ref[...] = jnp.zeros_like(o_ref)

        o_ref[...] += jnp.sum(x_ref[...].astype(f32), axis=0, keepdims=True)

    return pl.pallas_call(
        body, name=name, grid=(s // tm,),
        in_specs=[pl.BlockSpec((tm, n), lambda i: (i, 0))],
        out_specs=pl.BlockSpec((1, n), lambda i: (0, 0)),
        out_shape=jax.ShapeDtypeStruct((1, n), f32),
        compiler_params=_cparams(("arbitrary",)),
    )(x)


def _fit(dim, want):
    if dim <= want:
        return dim
    t = want
    while dim % t:
        t -= 128
    return t


def matmul(a, b, *, dims, name, out_dtype=f32, tm=1024, tn=512, tk=1024, a_pro=None, epi=None, epi_args=()):
    if dims == "nn":
        (m, k), n = a.shape, b.shape[1]
    elif dims == "nt":
        (m, k), n = a.shape, b.shape[0]
    else:
        (k, m), n = a.shape, b.shape[1]
    tm, tn, tk = _fit(m, tm), _fit(n, tn), _fit(k, tk)
    nk = k // tk
    if dims == "nn":
        a_spec = pl.BlockSpec((tm, tk), lambda i, j, kk: (i, kk))
        b_spec = pl.BlockSpec((tk, tn), lambda i, j, kk: (kk, j))
        dn = NN
    elif dims == "nt":
        a_spec = pl.BlockSpec((tm, tk), lambda i, j, kk: (i, kk))
        b_spec = pl.BlockSpec((tn, tk), lambda i, j, kk: (j, kk))
        dn = NT
    else:
        a_spec = pl.BlockSpec((tk, tm), lambda i, j, kk: (kk, i))
        b_spec = pl.BlockSpec((tk, tn), lambda i, j, kk: (kk, j))
        dn = TN
    e_specs = [pl.BlockSpec((tm, tn), lambda i, j, kk: (i, j)) if kind == "tile"
               else pl.BlockSpec((1, tn), lambda i, j, kk: (0, j)) for kind, _ in epi_args]
    n_epi = len(epi_args)

    def body(*refs):
        a_ref, b_ref = refs[0], refs[1]
        e_refs = refs[2:2 + n_epi]
        o_ref = refs[2 + n_epi]
        av = a_ref[...]
        if a_pro is not None:
            av = a_pro(av)
        part = _mm(av, b_ref[...], dn)

        def finish(acc):
            if epi is not None:
                acc = epi(acc, *[r[...] for r in e_refs])
            o_ref[...] = acc.astype(out_dtype)

        if nk == 1:
            finish(part)
        else:
            acc_ref = refs[3 + n_epi]
            kk = pl.program_id(2)

            @pl.when(kk == 0)
            def _():
                acc_ref[...] = part

            @pl.when(kk > 0)
            def _():
                acc_ref[...] += part

            @pl.when(kk == nk - 1)
            def _():
                finish(acc_ref[...])

    return pl.pallas_call(
        body, name=name, grid=(m // tm, n // tn, nk),
        in_specs=[a_spec, b_spec] + e_specs,
        out_specs=pl.BlockSpec((tm, tn), lambda i, j, kk: (i, j)),
        out_shape=jax.ShapeDtypeStruct((m, n), out_dtype),
        scratch_shapes=[pltpu.VMEM((tm, tn), f32)] if nk > 1 else [],
        compiler_params=_cparams(("parallel", "parallel", "arbitrary")),
    )(a, b, *[arr for _, arr in epi_args])


def _relu2(a):
    r = jnp.maximum(a.astype(f32), 0.0)
    return r * r


def _add(acc, t):
    return acc + t


def _add_bias(acc, t):
    return acc + t


def _add_bias_res(acc, bias, res):
    return acc + bias + res


def _times_relu2_grad(acc, a):
    return acc * (2.0 * jnp.maximum(a.astype(f32), 0.0))


_MIXER_PARAM_SHAPES = (
    ("ln_g", (1, D_MODEL)), ("ln_b", (1, D_MODEL)), ("wm", (N_BLK, CH, CH)), ("bs_t", (CH, CH)),
    ("conv_w", (8, 2048)), ("conv_b", (1, 2048)), ("dt_bias", (1, CH)), ("a_log", (1, CH)),
    ("d_heads", (1, CH)), ("norm_g", (1, D_MODEL)),
)


def _blocks(v, n, off=0):
    return [v[:, off + i * CH: off + (i + 1) * CH] for i in range(n)]


def _split_mixer_params(vals):
    p = dict(vals)
    return {
        "ln_g": _blocks(p["ln_g"], N_BLK), "ln_b": _blocks(p["ln_b"], N_BLK),
        "wm": [p["wm"][g] for g in range(N_BLK)], "bs_t": p["bs_t"],
        "conv_w": _blocks(p["conv_w"], XBC_BLKS), "conv_b": _blocks(p["conv_b"], XBC_BLKS),
        "dt_bias": p["dt_bias"], "a_log": p["a_log"], "d_heads": p["d_heads"],
        "norm_g": _blocks(p["norm_g"], N_BLK),
    }


def _mixer_leaves(proj_ref, halo_ref, keep_halo):
    pv = proj_ref
    us = [pv[:, OFF_U + i * CH: OFF_U + (i + 1) * CH] for i in range(N_BLK)]
    vs = [pv[:, OFF_V + i * CH: OFF_V + (i + 1) * CH] for i in range(N_BLK)]
    zs = [pv[:, OFF_Z + i * CH: OFF_Z + (i + 1) * CH] for i in range(N_BLK)]
    xbcs = [pv[:, OFF_X + i * CH: OFF_X + (i + 1) * CH] for i in range(XBC_BLKS)]
    halos = [halo_ref[:, OFF_X + i * CH: OFF_X + (i + 1) * CH] * keep_halo for i in range(XBC_BLKS)]
    dtblk = pv[:, OFF_DT: OFF_DT + CH]
    return us, vs, zs, xbcs, halos, dtblk


def mixer_fwd(proj, prm):
    s = proj.shape[0]
    nc = s // CH
    names = [n for n, _ in _MIXER_PARAM_SHAPES]

    def body(proj_ref, halo_ref, *rest):
        p_refs = rest[:len(names)]
        ab_ref, hs_ref, h_ref = rest[len(names):]
        c = pl.program_id(0)

        @pl.when(c == 0)
        def _():
            h_ref[...] = jnp.zeros_like(h_ref)

        hs_ref[...] = h_ref[...]
        keep = (c > 0).astype(f32)
        us, vs, zs, xbcs, halos, dtblk = _mixer_leaves(proj_ref, halo_ref, keep)
        hps = [h_ref[i * CH:(i + 1) * CH, :] for i in range(N_BLK)]
        p = _split_mixer_params({n: r[...] for n, r in zip(names, p_refs)})
        a_out, b_out, h_out = _mixer_chunk(us, vs, zs, xbcs, halos, dtblk, hps, p)
        for i in range(N_BLK):
            ab_ref[:, i * CH:(i + 1) * CH] = a_out[i].astype(bf16)
            ab_ref[:, D_MODEL + i * CH: D_MODEL + (i + 1) * CH] = b_out[i].astype(bf16)
            h_ref[i * CH:(i + 1) * CH, :] = h_out[i]

    def const(shape):
        return pl.BlockSpec(shape, lambda c: (0,) * len(shape))

    return pl.pallas_call(
        body, name="mixer_fwd", grid=(nc,),
        in_specs=[pl.BlockSpec((CH, NP_IN), lambda c: (c, 0)),
                  pl.BlockSpec((8, NP_IN), lambda c: (jnp.maximum(c * (CH // 8) - 1, 0), 0))]
                 + [const(shp) for _, shp in _MIXER_PARAM_SHAPES],
        out_specs=[pl.BlockSpec((CH, 2 * D_MODEL), lambda c: (c, 0)),
                   pl.BlockSpec((None, D_MODEL, CH), lambda c: (c, 0, 0))],
        out_shape=[jax.ShapeDtypeStruct((s, 2 * D_MODEL), bf16), jax.ShapeDtypeStruct((nc, D_MODEL, CH), f32)],
        scratch_shapes=[pltpu.VMEM((D_MODEL, CH), f32)],
        compiler_params=_cparams(("arbitrary",)),
    )(proj, proj, *[prm[n] for n in names])


def mixer_bwd(proj, hstates, dab, prm):
    s = proj.shape[0]
    nc = s // CH
    names = [n for n, _ in _MIXER_PARAM_SHAPES]
    npar = len(names)

    def body(proj_ref, halo_ref, hs_ref, dab_ref, *rest):
        p_refs = rest[:npar]
        dproj_ref = rest[npar]
        g_refs = rest[npar + 1: 2 * npar + 1]
        dh_ref, dhalo_ref = rest[2 * npar + 1:]
        i = pl.program_id(0)
        c = nc - 1 - i

        @pl.when(i == 0)
        def _():
            dh_ref[...] = jnp.zeros_like(dh_ref)
            dhalo_ref[...] = jnp.zeros_like(dhalo_ref)
            for r in g_refs:
                r[...] = jnp.zeros_like(r)

        keep = (c > 0).astype(f32)
        us, vs, zs, xbcs, halos, dtblk = _mixer_leaves(proj_ref, halo_ref, keep)
        hps = [hs_ref[j * CH:(j + 1) * CH, :] for j in range(N_BLK)]
        pvals = {n: r[...] for n, r in zip(names, p_refs)}

        def fn(us, vs, zs, xbcs, halos, dtblk, hps, pvals):
            return _mixer_chunk(us, vs, zs, xbcs, halos, dtblk, hps, _split_mixer_params(pvals))

        _, vjp = jax.vjp(fn, us, vs, zs, xbcs, halos, dtblk, hps, pvals)
        da = [dab_ref[:, j * CH:(j + 1) * CH].astype(f32) for j in range(N_BLK)]
        db = [dab_ref[:, D_MODEL + j * CH: D_MODEL + (j + 1) * CH].astype(f32) for j in range(N_BLK)]
        dh = [dh_ref[j * CH:(j + 1) * CH, :] for j in range(N_BLK)]
        dus, dvs, dzs, dxbcs, dhalos, ddt, dhps, dp = vjp((da, db, dh))

        for j in range(N_BLK):
            dproj_ref[:, OFF_U + j * CH: OFF_U + (j + 1) * CH] = dus[j].astype(bf16)
            dproj_ref[:, OFF_V + j * CH: OFF_V + (j + 1) * CH] = dvs[j].astype(bf16)
            dproj_ref[:, OFF_Z + j * CH: OFF_Z + (j + 1) * CH] = dzs[j].astype(bf16)
            dh_ref[j * CH:(j + 1) * CH, :] = dhps[j]
        zeros_top = jnp.zeros((CH - 8, CH), f32)
        for j in range(XBC_BLKS):
            late = jnp.concatenate([zeros_top, dhalo_ref[:, j * CH:(j + 1) * CH]], axis=0)
            dproj_ref[:, OFF_X + j * CH: OFF_X + (j + 1) * CH] = (dxbcs[j] + late).astype(bf16)
        for j in range(XBC_BLKS):
            dhalo_ref[:, j * CH:(j + 1) * CH] = dhalos[j] * keep
        lane = lax.broadcasted_iota(jnp.int32, (CH, CH), 1)
        dproj_ref[:, OFF_DT: OFF_DT + CH] = jnp.where(lane < SSM_HEADS, ddt, 0.0).astype(bf16)
        dproj_ref[:, OFF_DT + CH:] = jnp.zeros((CH, NP_IN - OFF_DT - CH), bf16)
        for n, r in zip(names, g_refs):
            r[...] += dp[n]

    def const(shape):
        return pl.BlockSpec(shape, lambda i: (0,) * len(shape))

    outs = pl.pallas_call(
        body, name="mixer_bwd", grid=(nc,),
        in_specs=[pl.BlockSpec((CH, NP_IN), lambda i: (nc - 1 - i, 0)),
                  pl.BlockSpec((8, NP_IN), lambda i: (jnp.maximum((nc - 1 - i) * (CH // 8) - 1, 0), 0)),
                  pl.BlockSpec((None, D_MODEL, CH), lambda i: (nc - 1 - i, 0, 0)),
                  pl.BlockSpec((CH, 2 * D_MODEL), lambda i: (nc - 1 - i, 0))]
                 + [const(shp) for _, shp in _MIXER_PARAM_SHAPES],
        out_specs=[pl.BlockSpec((CH, NP_IN), lambda i: (nc - 1 - i, 0))]
                  + [const(shp) for _, shp in _MIXER_PARAM_SHAPES],
        out_shape=[jax.ShapeDtypeStruct((s, NP_IN), bf16)]
                  + [jax.ShapeDtypeStruct(shp, f32) for _, shp in _MIXER_PARAM_SHAPES],
        scratch_shapes=[pltpu.VMEM((D_MODEL, CH), f32), pltpu.VMEM((8, 2048), f32)],
        compiler_params=_cparams(("arbitrary",)),
    )(proj, proj, hstates, dab, *[prm[n] for n in names])
    return outs[0], dict(zip(names, outs[1:]))


_K_BLK = D_MODEL // CH
_V_BLK = _K_BLK + 1


def _attn_specs(rev, nb):
    def blk(i):
        return nb - 1 - i if rev else i

    q_spec = pl.BlockSpec((CH, D_MODEL), lambda i: (blk(i), 0))
    kv = lambda col, prev: pl.BlockSpec(
        (CH, CH), lambda i: (jnp.maximum(blk(i) - 1, 0) if prev else blk(i), col))
    return q_spec, [kv(_K_BLK, True), kv(_K_BLK, False), kv(_V_BLK, True), kv(_V_BLK, False)]


def attn_fwd(qkv, sink_row):
    s = qkv.shape[0]
    nb = s // CH

    def body(q_ref, kp_ref, kc_ref, vp_ref, vc_ref, sink_ref, o_ref):
        qps = [q_ref[:, p * CH:(p + 1) * CH] for p in range(N_BLK)]
        outs = _attn_block(qps, kp_ref[...], kc_ref[...], vp_ref[...], vc_ref[...], sink_ref[...],
                           pl.program_id(0) == 0)
        for p in range(N_BLK):
            o_ref[:, p * CH:(p + 1) * CH] = outs[p].astype(bf16)

    q_spec, kv_specs = _attn_specs(False, nb)
    return pl.pallas_call(
        body, name="attn_fwd", grid=(nb,),
        in_specs=[q_spec] + kv_specs + [pl.BlockSpec((1, CH), lambda i: (0, 0))],
        out_specs=pl.BlockSpec((CH, D_MODEL), lambda i: (i, 0)),
        out_shape=jax.ShapeDtypeStruct((s, D_MODEL), bf16),
        compiler_params=_cparams(("parallel",)),
    )(qkv, qkv, qkv, qkv, qkv, sink_row)


def attn_bwd(qkv, sink_row, dout):
    s = qkv.shape[0]
    nb = s // CH

    def body(q_ref, kp_ref, kc_ref, vp_ref, vc_ref, sink_ref, do_ref, dqkv_ref, dsink_ref, carry_ref):
        i = pl.program_id(0)
        blk = nb - 1 - i

        @pl.when(i == 0)
        def _():
            dsink_ref[...] = jnp.zeros_like(dsink_ref)
            carry_ref[...] = jnp.zeros_like(carry_ref)

        qps = [q_ref[:, p * CH:(p + 1) * CH] for p in range(N_BLK)]
        first = blk == 0
        _, vjp = jax.vjp(lambda *a: _attn_block(*a, first), qps, kp_ref[...], kc_ref[...], vp_ref[...],
                         vc_ref[...], sink_ref[...])
        dos = [do_ref[:, p * CH:(p + 1) * CH].astype(f32) for p in range(N_BLK)]
        dqs, dkp, dkc, dvp, dvc, dsink = vjp(dos)
        for p in range(N_BLK):
            dqkv_ref[:, p * CH:(p + 1) * CH] = dqs[p].astype(bf16)
        dqkv_ref[:, D_MODEL: D_MODEL + CH] = (dkc + carry_ref[0]).astype(bf16)
        dqkv_ref[:, D_MODEL + CH:] = (dvc + carry_ref[1]).astype(bf16)
        keep = jnp.logical_not(first).astype(f32)
        carry_ref[0] = dkp * keep
        carry_ref[1] = dvp * keep
        dsink_ref[...] += dsink

    q_spec, kv_specs = _attn_specs(True, nb)
    return pl.pallas_call(
        body, name="attn_bwd", grid=(nb,),
        in_specs=[q_spec] + kv_specs + [pl.BlockSpec((1, CH), lambda i: (0, 0)),
                                        pl.BlockSpec((CH, D_MODEL), lambda i: (nb - 1 - i, 0))],
        out_specs=[pl.BlockSpec((CH, QKV_DIM), lambda i: (nb - 1 - i, 0)), pl.BlockSpec((1, CH), lambda i: (0, 0))],
        out_shape=[jax.ShapeDtypeStruct((s, QKV_DIM), bf16), jax.ShapeDtypeStruct((1, CH), f32)],
        scratch_shapes=[pltpu.VMEM((2, CH, CH), f32)],
        compiler_params=_cparams(("arbitrary",)),
    )(qkv, qkv, qkv, qkv, qkv, sink_row, dout)


def adamw(w, g, m, v, name):
    r, c = w.shape
    tr = _fit(r, 256) if r % 8 == 0 else r

    def body(w_ref, g_ref, m_ref, v_ref, d_ref, nm_ref, nv_ref):
        gv = g_ref[...]
        nm = ADAM_B1 * m_ref[...] + (1.0 - ADAM_B1) * gv
        nv = ADAM_B2 * v_ref[...] + (1.0 - ADAM_B2) * jnp.square(gv)
        m_hat = nm / (1.0 - ADAM_B1 ** ADAM_STEP)
        v_hat = nv / (1.0 - ADAM_B2 ** ADAM_STEP)
        d_ref[...] = -ADAM_LR * (m_hat / (jnp.sqrt(v_hat) + ADAM_EPS) + ADAM_WD * w_ref[...])
        nm_ref[...] = nm
        nv_ref[...] = nv

    tile = pl.BlockSpec((tr, c), lambda i: (i, 0))
    return pl.pallas_call(
        body, name=name, grid=(r // tr,),
        in_specs=[tile] * 4, out_specs=[tile] * 3,
        out_shape=[jax.ShapeDtypeStruct((r, c), f32)] * 3,
        compiler_params=_cparams(("parallel",)),
    )(w, g, m, v)


_MESH = pl.DeviceIdType.MESH
_ANY = pl.BlockSpec(memory_space=pl.ANY)
_VMEM = pl.BlockSpec(memory_space=pltpu.VMEM)


def _place():
    x, y, c = lax.axis_index("x"), lax.axis_index("y"), lax.axis_index("c")
    chips = [(1 - x, y), (x, 1 - y), (1 - x, 1 - y)]
    return x, y, c, 2 * x + y, chips, [2 * cx + cy for cx, cy in chips]


def _half(c):
    return pl.ds(pl.multiple_of(c * HALF_ROWS, HALF_ROWS), HALF_ROWS)


def allgather_weights(wp):
    def body(w_ref, out_ref, send_sems, recv_sems, local_sem):
        x, y, c, me, chips, chip_idx = _place()
        sibling = (x, y, 1 - c)
        mine = pltpu.make_async_copy(w_ref, out_ref.at[me], local_sem)
        mine.start()

        def copy(k, shard, half, to, src=None):
            dst = out_ref.at[shard, _half(half)]
            return pltpu.make_async_remote_copy(
                src_ref=dst if src is None else src, dst_ref=dst, send_sem=send_sems.at[k],
                recv_sem=recv_sems.at[k], device_id=to, device_id_type=_MESH)

        first = [copy(j, me, c, (*chips[j], c), src=w_ref.at[_half(c)]) for j in range(3)]
        for cp in first:
            cp.start()
        passed = [copy(3 + j, chip_idx[j], c, sibling) for j in range(3)]
        for j in range(3):
            copy(j, chip_idx[j], c, sibling).wait_recv()
            passed[j].start()
        for j in range(3):
            copy(3 + j, chip_idx[j], 1 - c, sibling).wait_recv()
        for cp in first + passed:
            cp.wait_send()
        mine.wait()

    return pl.pallas_call(
        body, name="allgather_weights",
        out_shape=jax.ShapeDtypeStruct((N_CHIPS,) + wp.shape, wp.dtype),
        in_specs=[_ANY], out_specs=_ANY,
        scratch_shapes=[pltpu.SemaphoreType.DMA((6,)), pltpu.SemaphoreType.DMA((6,)), pltpu.SemaphoreType.DMA],
    )(wp)


def exchange_halves(g):
    def body(g_ref, out_ref, send_sem, recv_sem):
        x, y, c, *_ = _place()
        cp = pltpu.make_async_remote_copy(
            src_ref=g_ref.at[:, _half(1 - c)], dst_ref=out_ref, send_sem=send_sem, recv_sem=recv_sem,
            device_id=(x, y, 1 - c), device_id_type=_MESH)
        cp.start()
        cp.wait()

    return pl.pallas_call(
        body, name="exchange_halves",
        out_shape=jax.ShapeDtypeStruct((N_CHIPS, HALF_ROWS, g.shape[2]), g.dtype),
        in_specs=[_ANY], out_specs=_ANY,
        scratch_shapes=[pltpu.SemaphoreType.DMA, pltpu.SemaphoreType.DMA],
    )(g)


_SUM_ROWS = 416


def add_halves(g, got, c_idx):
    steps = HALF_ROWS // _SUM_ROWS

    def body(c_ref, g_ref, got_ref, o_ref):
        o_ref[...] = (g_ref[...].astype(f32) + got_ref[...].astype(f32)).astype(bf16)

    return pl.pallas_call(
        body, name="add_halves",
        grid_spec=pltpu.PrefetchScalarGridSpec(
            num_scalar_prefetch=1, grid=(N_CHIPS, steps),
            in_specs=[pl.BlockSpec((None, _SUM_ROWS, 1024), lambda s, i, c: (s, c[0] * steps + i, 0)),
                      pl.BlockSpec((None, _SUM_ROWS, 1024), lambda s, i, c: (s, i, 0))],
            out_specs=pl.BlockSpec((None, _SUM_ROWS, 1024), lambda s, i, c: (s, i, 0))),
        out_shape=jax.ShapeDtypeStruct((N_CHIPS, HALF_ROWS, 1024), bf16),
        compiler_params=_cparams(("parallel", "parallel")),
    )(c_idx, g, got)


def scatter_chip_sums(t):
    def body(t_ref, out_ref, send_sems, recv_sems):
        x, y, c, me, chips, chip_idx = _place()
        cps = [pltpu.make_async_remote_copy(
            src_ref=t_ref.at[chip_idx[j]], dst_ref=out_ref.at[j], send_sem=send_sems.at[j],
            recv_sem=recv_sems.at[j], device_id=(*chips[j], c), device_id_type=_MESH) for j in range(3)]
        for cp in cps:
            cp.start()
        for cp in cps:
            cp.wait_recv()
        for cp in cps:
            cp.wait_send()

    return pl.pallas_call(
        body, name="scatter_chip_sums",
        out_shape=jax.ShapeDtypeStruct((3, HALF_ROWS, t.shape[2]), t.dtype),
        in_specs=[_ANY], out_specs=_ANY,
        scratch_shapes=[pltpu.SemaphoreType.DMA((3,)), pltpu.SemaphoreType.DMA((3,))],
    )(t)


def sum_chips(t, got, me_idx):
    steps = HALF_ROWS // _SUM_ROWS

    def body(me_ref, t_ref, got_ref, o_ref):
        acc = t_ref[...].astype(f32)
        for j in range(3):
            acc = acc + got_ref[j].astype(f32)
        o_ref[...] = acc

    return pl.pallas_call(
        body, name="sum_chips",
        grid_spec=pltpu.PrefetchScalarGridSpec(
            num_scalar_prefetch=1, grid=(steps,),
            in_specs=[pl.BlockSpec((None, _SUM_ROWS, 1024), lambda i, me: (me[0], i, 0)),
                      pl.BlockSpec((3, _SUM_ROWS, 1024), lambda i, me: (0, i, 0))],
            out_specs=pl.BlockSpec((_SUM_ROWS, 1024), lambda i, me: (i, 0))),
        out_shape=jax.ShapeDtypeStruct((HALF_ROWS, 1024), f32),
        compiler_params=_cparams(("parallel",)),
    )(me_idx, t, got)


def share_halves(r):
    def body(r_ref, out_ref, send_sem, recv_sem, local_sem):
        x, y, c, *_ = _place()
        mine = pltpu.make_async_copy(r_ref, out_ref.at[_half(c)], local_sem)
        mine.start()
        cp = pltpu.make_async_remote_copy(
            src_ref=r_ref, dst_ref=out_ref.at[_half(c)], send_sem=send_sem, recv_sem=recv_sem,
            device_id=(x, y, 1 - c), device_id_type=_MESH)
        cp.start()
        pltpu.make_async_remote_copy(
            src_ref=r_ref, dst_ref=out_ref.at[_half(1 - c)], send_sem=send_sem, recv_sem=recv_sem,
            device_id=(x, y, 1 - c), device_id_type=_MESH).wait_recv()
        cp.wait_send()
        mine.wait()

    return pl.pallas_call(
        body, name="share_halves",
        out_shape=jax.ShapeDtypeStruct((PACK_ROWS, r.shape[1]), r.dtype),
        in_specs=[_ANY], out_specs=_ANY,
        scratch_shapes=[pltpu.SemaphoreType.DMA, pltpu.SemaphoreType.DMA, pltpu.SemaphoreType.DMA],
    )(r)


def allreduce_small(sp):
    def body(s_ref, out_ref, gather_ref, send_sems, recv_sems):
        x, y, c, me, chips, chip_idx = _place()
        sibling = (x, y, 1 - c)

        def copy(k, chip, core, to, src=None):
            dst = gather_ref.at[2 * chip + core]
            return pltpu.make_async_remote_copy(
                src_ref=dst if src is None else src, dst_ref=dst, send_sem=send_sems.at[k],
                recv_sem=recv_sems.at[k], device_id=to, device_id_type=_MESH)

        first = [copy(0, me, c, sibling, src=s_ref)]
        first += [copy(1 + j, me, c, (*chips[j], c), src=s_ref) for j in range(3)]
        for cp in first:
            cp.start()
        gather_ref[2 * me + c] = s_ref[...]
        passed = [copy(4 + j, chip_idx[j], c, sibling) for j in range(3)]
        for j in range(3):
            copy(1 + j, chip_idx[j], c, sibling).wait_recv()
            passed[j].start()
        copy(0, me, 1 - c, sibling).wait_recv()
        for j in range(3):
            copy(4 + j, chip_idx[j], 1 - c, sibling).wait_recv()
        for cp in first + passed:
            cp.wait_send()
        acc = gather_ref[0]
        for d in range(1, 2 * N_CHIPS):
            acc = acc + gather_ref[d]
        out_ref[...] = acc

    return pl.pallas_call(
        body, name="allreduce_small",
        out_shape=jax.ShapeDtypeStruct(sp.shape, sp.dtype),
        in_specs=[_VMEM], out_specs=_VMEM,
        scratch_shapes=[pltpu.VMEM((2 * N_CHIPS,) + sp.shape, sp.dtype),
                        pltpu.SemaphoreType.DMA((7,)), pltpu.SemaphoreType.DMA((7,))],
        compiler_params=_cparams(),
    )(sp)


def _rows(a):
    flat = a.reshape(-1)
    pad = (-flat.shape[0]) % 1024
    if pad:
        flat = jnp.pad(flat, (0, pad))
    return flat.reshape(-1, 1024)


def _pack(arrays, total_rows):
    rows = jnp.concatenate([_rows(a) for a in arrays], axis=0)
    return jnp.pad(rows, ((0, total_rows - rows.shape[0]), (0, 0)))


def _unpack(packed, shapes):
    out, r = [], 0
    for shp in shapes:
        n = 1
        for d in shp:
            n *= d
        nr = -(-n // 1024)
        out.append(packed[r:r + nr].reshape(-1)[:n].reshape(shp))
        r += nr
    return out


_BIG_SHARD_SHAPES = ((1024, 1284), (512, 1024), (1024, 320), (256, 1024), (2, 1024, 1024), (2, 1024, 1024))
_BIG_NAMES = ("w_in_even", "w_out_even", "w_qkv", "w_o", "w_up", "w_down")

_SMALL_SHAPES = (
    ("norm_mix_g", (2, 1024)), ("norm_mlp_g", (2, 1024)), ("final_norm_g", (1024,)), ("gm_ln_g", (1, 1024)),
    ("gm_ln_b", (1, 1024)), ("gm_w_s", (1, 8, 128, 128)), ("gm_b_s", (1, 8, 128)), ("ssm_conv_b", (1, 2048)),
    ("ssm_dt_bias", (1, 16)), ("ssm_a_log", (1, 16)), ("ssm_d", (1, 16)), ("ssm_norm_g", (1, 1024)),
    ("attn_sinks", (1, 16)), ("ssm_conv_w", (1, 4, 2048)), ("b_qkv", (1, 1280)), ("b_o", (1, 1024)),
)
_N_REPLICATED = 13
_SHARDED_SMALL = (("ssm_conv_w", 2, 512), ("b_qkv", 1, 320), ("b_o", 1, 256))
_GATHER_ROWS = 16
_SHARD_PACK_ROWS = 8


def _full_big_grads_to_shards(dw_in_p, dw_out, dw_qkv, dw_o, dw_up, dw_down):
    shards = []
    for s in range(N_CHIPS):
        shards.append(_pack([
            dw_in_p[:, 1284 * s: 1284 * (s + 1)], dw_out[512 * s: 512 * (s + 1)],
            dw_qkv[:, 320 * s: 320 * (s + 1)], dw_o[256 * s: 256 * (s + 1)],
            jnp.stack([w[:, 1024 * s: 1024 * (s + 1)] for w in dw_up]),
            jnp.stack([w[1024 * s: 1024 * (s + 1)] for w in dw_down]),
        ], PACK_ROWS).astype(bf16))
    return jnp.stack(shards)


def _gathered_to_full_weights(gathered):
    parts = [_unpack(gathered[s], _BIG_SHARD_SHAPES) for s in range(N_CHIPS)]
    w_in = jnp.concatenate([p[0] for p in parts], axis=1)
    w_in_p = jnp.pad(w_in, ((0, 0), (0, NP_IN - IN_EVEN)))
    w_out = jnp.concatenate([p[1] for p in parts], axis=0)
    w_qkv = jnp.concatenate([p[2] for p in parts], axis=1)
    w_o = jnp.concatenate([p[3] for p in parts], axis=0)
    w_up = [jnp.concatenate([p[4][l] for p in parts], axis=1) for l in range(2)]
    w_down = [jnp.concatenate([p[5][l] for p in parts], axis=0) for l in range(2)]
    return w_in_p, w_out, w_qkv, w_o, w_up, w_down


def _row2(v):
    return v.reshape(1, -1)


def _lane_pad(v):
    return jnp.pad(v, ((0, 0), (0, CH - v.shape[1])))


def _mlp_fwd(h, g_row, w_up, w_down, tag):
    y = rmsnorm_fwd(h, g_row, f"mlp_norm{tag}")
    a = matmul(y, w_up, dims="nn", name=f"mlp_up{tag}", out_dtype=bf16, tn=1024)
    out = matmul(a, w_down, dims="nn", name=f"mlp_down{tag}", a_pro=_relu2, epi=_add, epi_args=(("tile", h),))
    return out, y, a


def _mlp_bwd(dh_out, h, g_row, y, a, w_up, w_down, tag):
    da = matmul(dh_out, w_down, dims="nt", name=f"mlp_da{tag}", out_dtype=bf16, tn=1024,
                epi=_times_relu2_grad, epi_args=(("tile", a),))
    dw_down = matmul(a, dh_out, dims="tn", name=f"mlp_dwdown{tag}", a_pro=_relu2)
    dw_up = matmul(y, da, dims="tn", name=f"mlp_dwup{tag}", tn=1024)
    dy = matmul(da, w_up, dims="nt", name=f"mlp_dy{tag}")
    dh, dg = rmsnorm_bwd(h, g_row, dy, dh_out, f"mlp_dnorm{tag}")
    return dh, dg, dw_up, dw_down


def _local_step(x, target, big, sm):
    w_in_p, w_out, w_qkv, w_o, w_up, w_down = big
    mix_g = [_row2(sm["norm_mix_g"][i]) for i in range(2)]
    mlp_g = [_row2(sm["norm_mlp_g"][i]) for i in range(2)]
    mixer_prm = {
        "ln_g": sm["gm_ln_g"], "ln_b": sm["gm_ln_b"], "wm": sm["gm_w_s"][0],
        "bs_t": jnp.pad(sm["gm_b_s"][0].T, ((0, 0), (0, CH - N_BLK))),
        "conv_w": jnp.pad(sm["ssm_conv_w"][0], ((0, 4), (0, 0))), "conv_b": sm["ssm_conv_b"],
        "dt_bias": _lane_pad(sm["ssm_dt_bias"]), "a_log": _lane_pad(sm["ssm_a_log"]),
        "d_heads": _lane_pad(sm["ssm_d"]), "norm_g": sm["ssm_norm_g"],
    }
    sink_row = _lane_pad(sm["attn_sinks"])

    y0 = rmsnorm_fwd(x, mix_g[0], "mix_norm0")
    proj = matmul(y0, w_in_p, dims="nn", name="in_proj", tn=768)
    ab, hstates = mixer_fwd(proj, mixer_prm)
    h1 = matmul(ab, w_out, dims="nn", name="out_proj", epi=_add, epi_args=(("tile", x),))
    h2, y1, a1 = _mlp_fwd(h1, mlp_g[0], w_up[0], w_down[0], 0)
    y2 = rmsnorm_fwd(h2, mix_g[1], "mix_norm1")
    qkv = matmul(y2, w_qkv, dims="nn", name="qkv_proj", tn=QKV_DIM, epi=_add_bias, epi_args=(("row", sm["b_qkv"]),))
    att = attn_fwd(qkv, sink_row)
    h3 = matmul(att, w_o, dims="nn", name="o_proj", epi=_add_bias_res,
                epi_args=(("row", sm["b_o"]), ("tile", h2)))
    h4, y3, a3 = _mlp_fwd(h3, mlp_g[1], w_up[1], w_down[1], 1)
    loss, dh4, dg_final = final_loss(h4, _row2(sm["final_norm_g"]), target, "final_loss")

    dh3, dg_mlp1, dw_up1, dw_down1 = _mlp_bwd(dh4, h3, mlp_g[1], y3, a3, w_up[1], w_down[1], 1)
    db_o = colsum(dh3, "db_o")
    datt = matmul(dh3, w_o, dims="nt", name="attn_dout", out_dtype=bf16)
    dw_o = matmul(att, dh3, dims="tn", name="dw_o")
    dqkv, dsink = attn_bwd(qkv, sink_row, datt)
    db_qkv = colsum(dqkv, "db_qkv")
    dw_qkv = matmul(y2, dqkv, dims="tn", name="dw_qkv", tn=QKV_DIM)
    dy2 = matmul(dqkv, w_qkv, dims="nt", name="dy_qkv", tk=QKV_DIM)
    dh2, dg_mix1 = rmsnorm_bwd(h2, mix_g[1], dy2, dh3, "mix_dnorm1")
    dh1, dg_mlp0, dw_up0, dw_down0 = _mlp_bwd(dh2, h1, mlp_g[0], y1, a1, w_up[0], w_down[0], 0)
    dab = matmul(dh1, w_out, dims="nt", name="mixer_dout", tn=1024)
    dw_out = matmul(ab, dh1, dims="tn", name="dw_out")
    dproj, dmix = mixer_bwd(proj, hstates, dab, mixer_prm)
    dw_in_p = matmul(y0, dproj, dims="tn", name="dw_in", tn=768)
    dy0 = matmul(dproj, w_in_p, dims="nt", name="dy_in", tk=768)
    dx, dg_mix0 = rmsnorm_bwd(x, mix_g[0], dy0, dh1, "mix_dnorm0")

    small_grads = {
        "norm_mix_g": jnp.concatenate([dg_mix0, dg_mix1], axis=0),
        "norm_mlp_g": jnp.concatenate([dg_mlp0, dg_mlp1], axis=0),
        "final_norm_g": dg_final[0], "gm_ln_g": dmix["ln_g"], "gm_ln_b": dmix["ln_b"],
        "gm_w_s": dmix["wm"][None], "gm_b_s": dmix["bs_t"][:, :N_BLK].T[None],
        "ssm_conv_b": dmix["conv_b"], "ssm_dt_bias": dmix["dt_bias"][:, :SSM_HEADS],
        "ssm_a_log": dmix["a_log"][:, :SSM_HEADS], "ssm_d": dmix["d_heads"][:, :SSM_HEADS],
        "ssm_norm_g": dmix["norm_g"], "attn_sinks": dsink[:, :SSM_HEADS],
        "ssm_conv_w": dmix["conv_w"][None, :4], "b_qkv": db_qkv, "b_o": db_o,
    }
    big_grads = (dw_in_p, dw_out, dw_qkv, dw_o, (dw_up0, dw_up1), (dw_down0, dw_down1))
    return loss, dx, big_grads, small_grads


def kernel(x, norm_mix_g, norm_mlp_g, final_norm_g, w_in_even, w_out_even, gm_ln_g, gm_ln_b, gm_w_s, gm_b_s, ssm_conv_w, ssm_conv_b, ssm_dt_bias, ssm_a_log, ssm_d, ssm_norm_g, w_qkv, b_qkv, w_o, b_o, attn_sinks, w_up, w_down, loss_target, m_norm_mix_g, m_norm_mlp_g, m_final_norm_g, m_w_in_even, m_w_out_even, m_gm_ln_g, m_gm_ln_b, m_gm_w_s, m_gm_b_s, m_ssm_conv_w, m_ssm_conv_b, m_ssm_dt_bias, m_ssm_a_log, m_ssm_d, m_ssm_norm_g, m_w_qkv, m_b_qkv, m_w_o, m_b_o, m_attn_sinks, m_w_up, m_w_down, v_norm_mix_g, v_norm_mlp_g, v_final_norm_g, v_w_in_even, v_w_out_even, v_gm_ln_g, v_gm_ln_b, v_gm_w_s, v_gm_b_s, v_ssm_conv_w, v_ssm_conv_b, v_ssm_dt_bias, v_ssm_a_log, v_ssm_d, v_ssm_norm_g, v_w_qkv, v_b_qkv, v_w_o, v_b_o, v_attn_sinks, v_w_up, v_w_down):
    w = dict(norm_mix_g=norm_mix_g, norm_mlp_g=norm_mlp_g, final_norm_g=final_norm_g, w_in_even=w_in_even,
             w_out_even=w_out_even, gm_ln_g=gm_ln_g, gm_ln_b=gm_ln_b, gm_w_s=gm_w_s, gm_b_s=gm_b_s,
             ssm_conv_w=ssm_conv_w, ssm_conv_b=ssm_conv_b, ssm_dt_bias=ssm_dt_bias, ssm_a_log=ssm_a_log,
             ssm_d=ssm_d, ssm_norm_g=ssm_norm_g, w_qkv=w_qkv, b_qkv=b_qkv, w_o=w_o, b_o=b_o,
             attn_sinks=attn_sinks, w_up=w_up, w_down=w_down)
    m = dict(norm_mix_g=m_norm_mix_g, norm_mlp_g=m_norm_mlp_g, final_norm_g=m_final_norm_g,
             w_in_even=m_w_in_even, w_out_even=m_w_out_even, gm_ln_g=m_gm_ln_g, gm_ln_b=m_gm_ln_b,
             gm_w_s=m_gm_w_s, gm_b_s=m_gm_b_s, ssm_conv_w=m_ssm_conv_w, ssm_conv_b=m_ssm_conv_b,
             ssm_dt_bias=m_ssm_dt_bias, ssm_a_log=m_ssm_a_log, ssm_d=m_ssm_d, ssm_norm_g=m_ssm_norm_g,
             w_qkv=m_w_qkv, b_qkv=m_b_qkv, w_o=m_w_o, b_o=m_b_o, attn_sinks=m_attn_sinks, w_up=m_w_up,
             w_down=m_w_down)
    v = dict(norm_mix_g=v_norm_mix_g, norm_mlp_g=v_norm_mlp_g, final_norm_g=v_final_norm_g,
             w_in_even=v_w_in_even, w_out_even=v_w_out_even, gm_ln_g=v_gm_ln_g, gm_ln_b=v_gm_ln_b,
             gm_w_s=v_gm_w_s, gm_b_s=v_gm_b_s, ssm_conv_w=v_ssm_conv_w, ssm_conv_b=v_ssm_conv_b,
             ssm_dt_bias=v_ssm_dt_bias, ssm_a_log=v_ssm_a_log, ssm_d=v_ssm_d, ssm_norm_g=v_ssm_norm_g,
             w_qkv=v_w_qkv, b_qkv=v_b_qkv, w_o=v_w_o, b_o=v_b_o, attn_sinks=v_attn_sinks, w_up=v_w_up,
             w_down=v_w_down)
    names = ("norm_mix_g", "norm_mlp_g", "final_norm_g", "w_in_even", "w_out_even", "gm_ln_g", "gm_ln_b",
             "gm_w_s", "gm_b_s", "ssm_conv_w", "ssm_conv_b", "ssm_dt_bias", "ssm_a_log", "ssm_d", "ssm_norm_g",
             "w_qkv", "b_qkv", "w_o", "b_o", "attn_sinks", "w_up", "w_down")

    cx, cy, cc = lax.axis_index("x"), lax.axis_index("y"), lax.axis_index("c")
    chip = 2 * cx + cy
    c_idx = jnp.reshape(cc, (1,)).astype(jnp.int32)
    chip_idx = jnp.reshape(chip, (1,)).astype(jnp.int32)

    def big_shards(d):
        return [d["w_in_even"][0], d["w_out_even"][0], d["w_qkv"][0], d["w_o"][0], d["w_up"], d["w_down"]]

    gathered = allgather_weights(_pack(big_shards(w), PACK_ROWS).astype(bf16))
    big = _gathered_to_full_weights(gathered)

    small_shard_placed = []
    for name, axis, width in _SHARDED_SMALL:
        full_shape = dict(_SMALL_SHAPES)[name]
        placed = lax.dynamic_update_slice_in_dim(jnp.zeros(full_shape, f32), w[name], chip * width, axis)
        small_shard_placed.append(jnp.where(cc == 0, placed, 0.0))
    gathered_small = _unpack(allreduce_small(_pack(small_shard_placed, _GATHER_ROWS)),
                             [dict(_SMALL_SHAPES)[n] for n, _, _ in _SHARDED_SMALL])
    sm = {n: w[n] for n, _ in _SMALL_SHAPES[:_N_REPLICATED]}
    sm.update({n: g for (n, _, _), g in zip(_SHARDED_SMALL, gathered_small)})

    loss_part, dx, big_grads, small_grads = _local_step(x[0], loss_target[0], big, sm)
    loss = lax.psum(loss_part[0, 0], ("x", "y", "c"))

    g_all = _full_big_grads_to_shards(*big_grads)
    t = add_halves(g_all, exchange_halves(g_all), c_idx)
    reduced = share_halves(sum_chips(t, scatter_chip_sums(t), chip_idx))
    grads = dict(zip(_BIG_NAMES, _unpack(reduced, _BIG_SHARD_SHAPES)))
    small_sum = allreduce_small(_pack([small_grads[n] for n, _ in _SMALL_SHAPES], SMALL_ROWS))
    small_full = dict(zip([n for n, _ in _SMALL_SHAPES], _unpack(small_sum, [s for _, s in _SMALL_SHAPES])))
    for n, _ in _SMALL_SHAPES[:_N_REPLICATED]:
        grads[n] = small_full[n]
    for n, axis, width in _SHARDED_SMALL:
        grads[n] = lax.dynamic_slice_in_dim(small_full[n], chip * width, width, axis)
    grads = {n: grads[n].reshape(w[n].shape) for n in names}

    delta, new_m, new_v = {}, {}, {}
    for n in _BIG_NAMES:
        cols = w[n].shape[-1]
        d2, m2, v2 = adamw(w[n].reshape(-1, cols), grads[n].reshape(-1, cols), m[n].reshape(-1, cols),
                           v[n].reshape(-1, cols), f"adamw_{n}")
        delta[n], new_m[n], new_v[n] = (t2.reshape(w[n].shape) for t2 in (d2, m2, v2))
    rep = [n for n, _ in _SMALL_SHAPES[:_N_REPLICATED]]
    shd = [n for n, _, _ in _SHARDED_SMALL]
    for group, rows, tag in ((rep, SMALL_ROWS, "replicated"), (shd, _SHARD_PACK_ROWS, "sharded")):
        packed = [_pack([d[n] for n in group], rows) for d in (w, grads, m, v)]
        outs = adamw(*packed, f"adamw_small_{tag}")
        shapes = [w[n].shape for n in group]
        for res, o in zip((delta, new_m, new_v), outs):
            res.update(dict(zip(group, _unpack(o, shapes))))

    return (loss, dx[None], *[grads[n] for n in names], *[delta[n] for n in names],
            *[new_m[n] for n in names], *[new_v[n] for n in names])
```

```python
import functools

import jax
import jax.numpy as jnp
from jax import lax
from jax.experimental import pallas as pl
from jax.experimental.pallas import tpu as pltpu

f32 = jnp.float32
bf16 = jnp.bfloat16
MXU_DTYPE = bf16

RMS_EPS = 1e-5
LN_EPS = 1e-5
D_MODEL = 1024
D_FF = 4096
CH = 128
N_BLK = 8
SSM_HEADS = 16
IN_EVEN = 5136
NP_IN = 5376
OFF_U, OFF_V, OFF_Z, OFF_X, OFF_DT = 0, 1024, 2048, 3072, 5120
XBC_BLKS = 16
QKV_DIM = 1280
ATT_SCALE = 64 ** -0.5

ADAM_LR = 0.001
ADAM_B1 = 0.9
ADAM_B2 = 0.999
ADAM_EPS = 1e-08
ADAM_WD = 0.01
ADAM_STEP = 10

VMEM_LIMIT_BYTES = 48 * 1024 * 1024
N_CHIPS = 4
SMALL_ROWS = 256

NN = ((1,), (0,))
NT = ((1,), (1,))
TN = ((0,), (0,))


def _mm(a, b, dims):
    return lax.dot_general(a.astype(MXU_DTYPE), b.astype(MXU_DTYPE), (dims, ((), ())),
                           preferred_element_type=f32)


def _mm_exact(a, b):
    return jnp.dot(a, b, preferred_element_type=f32, precision=lax.Precision.HIGHEST)


def _cparams(sem=None):
    return pltpu.CompilerParams(dimension_semantics=sem, vmem_limit_bytes=VMEM_LIMIT_BYTES)


@jax.custom_vjp
def _swap64(x):
    return pltpu.roll(x, 64, axis=1)


_swap64.defvjp(lambda x: (pltpu.roll(x, 64, axis=1), None), lambda _, g: (pltpu.roll(g, 64, axis=1),))


def _make_delay(k):
    @jax.custom_vjp
    def delay(ext):
        return pltpu.roll(ext, k, axis=0)[8:, :]

    def fwd(ext):
        return delay(ext), None

    def bwd(_, g):
        gp = jnp.concatenate([jnp.zeros((8, g.shape[1]), g.dtype), g], axis=0)
        return (pltpu.roll(gp, gp.shape[0] - k, axis=0),)

    delay.defvjp(fwd, bwd)
    return delay


_DELAYS = {k: _make_delay(k) for k in (1, 2, 3)}


def _col(m, lane, h):
    return jnp.sum(jnp.where(lane == h, m, 0.0), axis=1, keepdims=True)


def _row(m, sub, h):
    return jnp.sum(jnp.where(sub == h, m, 0.0), axis=0, keepdims=True)


def _mixer_chunk(us, vs, zs, xbcs, halos, dtblk, hps, prm):
    lane = lax.broadcasted_iota(jnp.int32, (CH, CH), 1)
    sub = lax.broadcasted_iota(jnp.int32, (CH, CH), 0)
    left = lane < 64
    top = sub < 64
    causal = sub >= lane

    gus = [jax.nn.gelu(u) for u in us]
    gvs = [jax.nn.gelu(v) for v in vs]
    mu = sum(jnp.sum(g, axis=1, keepdims=True) for g in gvs) / D_MODEL
    cen = [g - mu for g in gvs]
    var = sum(jnp.sum(c * c, axis=1, keepdims=True) for c in cen) / D_MODEL
    rstd = lax.rsqrt(var + LN_EPS)
    a_out = []
    for g in range(N_BLK):
        vn = cen[g] * rstd * prm["ln_g"][g] + prm["ln_b"][g]
        w = jnp.where(causal, prm["wm"][g], 0.0)
        mixed = _mm(w, vn, NN) + _col(prm["bs_t"], lane, g)
        a_out.append(gus[g] * mixed)

    act = []
    for b in range(XBC_BLKS):
        w8 = prm["conv_w"][b]
        sub8 = lax.broadcasted_iota(jnp.int32, w8.shape, 0)
        ext = jnp.concatenate([halos[b], xbcs[b]], axis=0)
        conv = xbcs[b] * _row(w8, sub8, 3) + prm["conv_b"][b]
        for k in (1, 2, 3):
            conv = conv + _DELAYS[k](ext) * _row(w8, sub8, 3 - k)
        act.append(jax.nn.silu(conv))

    dt = jax.nn.softplus(dtblk + prm["dt_bias"])
    a_neg = -jnp.exp(prm["a_log"])
    tri = causal.astype(f32)
    acum = _mm_exact(tri, dt * a_neg)
    acum_t = acum.T
    dt_t = dt.T
    last = sub == CH - 1
    ys, h_out = [], []
    for grp in range(4):
        bm = act[8 + grp]
        cm = act[12 + grp]
        cb = _mm(cm, bm, NT)
        for p in (2 * grp, 2 * grp + 1):
            h0, h1 = 2 * p, 2 * p + 1
            xp = act[p]
            hp = hps[p]
            wis = []
            for h in (h0, h1):
                seg = _col(acum, lane, h) - _row(acum_t, sub, h)
                decay = jnp.exp(jnp.where(causal, seg, -jnp.inf))
                wis.append(cb * decay * _row(dt_t, sub, h))
            wcat = jnp.concatenate(wis, axis=1)
            xbd = jnp.concatenate([jnp.where(left, xp, 0.0), jnp.where(left, 0.0, xp)], axis=0)
            y_diag = _mm(wcat, xbd, NN)
            a_end = [jnp.sum(jnp.where(last & (lane == h), acum, 0.0), keepdims=True) for h in (h0, h1)]
            a_col = jnp.where(left, _col(acum, lane, h0), _col(acum, lane, h1))
            dt_col = jnp.where(left, _col(dt, lane, h0), _col(dt, lane, h1))
            to_end = jnp.exp(jnp.where(left, a_end[0], a_end[1]) - a_col) * dt_col
            states = _mm(xp * to_end, bm, TN)
            chunk_decay = jnp.where(top, jnp.exp(a_end[0]), jnp.exp(a_end[1]))
            h_out.append(chunk_decay * hp + states)
            y_off = jnp.exp(a_col) * _mm(cm, hp, NT)
            d_skip = jnp.where(left[:1], _col(prm["d_heads"], lane[:1], h0), _col(prm["d_heads"], lane[:1], h1))
            ys.append((y_diag + y_off + xp * d_skip) * jax.nn.silu(zs[p]))

    b_out = []
    for grp in range(4):
        pair = (ys[2 * grp], ys[2 * grp + 1])
        ms = sum(jnp.sum(y * y, axis=1, keepdims=True) for y in pair) / 256.0
        r = lax.rsqrt(ms + RMS_EPS)
        for j, y in enumerate(pair):
            b_out.append(y * r * prm["norm_g"][2 * grp + j])
    return a_out, b_out, h_out


def _attn_block(qps, kprev, kcur, vprev, vcur, sink_row, first):
    lane = lax.broadcasted_iota(jnp.int32, (CH, CH), 1)
    left = lane < 64
    kband = jnp.concatenate([kprev, kcur], axis=0)
    vband = jnp.concatenate([vprev, vcur], axis=0)
    left2 = lax.broadcasted_iota(jnp.int32, kband.shape, 1) < 64
    ksw, vsw = _swap64(kband), _swap64(vband)
    kdup = [jnp.where(left2, kband, ksw), jnp.where(left2, ksw, kband)]
    vdup = [jnp.where(left2, vband, vsw), jnp.where(left2, vsw, vband)]
    qi = lax.broadcasted_iota(jnp.int32, (CH, 2 * CH), 0)
    si = lax.broadcasted_iota(jnp.int32, (CH, 2 * CH), 1)
    rel = qi + CH - si
    valid = (rel >= 0) & (rel < CH) & (jnp.logical_not(first) | (si >= CH))
    outs = []
    for p in range(N_BLK):
        j = p // 4
        halves = []
        for side, h in ((0, 2 * p), (1, 2 * p + 1)):
            qh = jnp.where(left, qps[p], 0.0) if side == 0 else jnp.where(left, 0.0, qps[p])
            s = _mm(qh, kdup[j], NT) * ATT_SCALE
            s = jnp.where(valid, s, -jnp.inf)
            sink = _col(sink_row, lane[:1], h)
            m = lax.stop_gradient(jnp.maximum(jnp.max(s, axis=1, keepdims=True), sink))
            pexp = jnp.exp(s - m)
            denom = jnp.sum(pexp, axis=1, keepdims=True) + jnp.exp(sink - m)
            halves.append(_mm(pexp / denom, vdup[j], NN))
        outs.append(jnp.where(left, halves[0], halves[1]))
    return outs


def _rmsnorm(x, g):
    r = lax.rsqrt(jnp.mean(x * x, axis=-1, keepdims=True) + RMS_EPS)
    return x * r * g


def rmsnorm_fwd(x, g_row, name):
    s, d = x.shape
    tm = min(512, s)

    def body(x_ref, g_ref, y_ref):
        y_ref[...] = _rmsnorm(x_ref[...], g_ref[...]).astype(bf16)

    return pl.pallas_call(
        body, name=name, grid=(s // tm,),
        in_specs=[pl.BlockSpec((tm, d), lambda i: (i, 0)), pl.BlockSpec((1, d), lambda i: (0, 0))],
        out_specs=pl.BlockSpec((tm, d), lambda i: (i, 0)),
        out_shape=jax.ShapeDtypeStruct((s, d), bf16),
        compiler_params=_cparams(("parallel",)),
    )(x, g_row)


def rmsnorm_bwd(x, g_row, dy, res, name):
    s, d = x.shape
    tm = min(512, s)

    def body(x_ref, g_ref, dy_ref, res_ref, dx_ref, dg_ref):
        @pl.when(pl.program_id(0) == 0)
        def _():
            dg_ref[...] = jnp.zeros_like(dg_ref)

        _, vjp = jax.vjp(_rmsnorm, x_ref[...], g_ref[...])
        dx, dg = vjp(dy_ref[...])
        dx_ref[...] = res_ref[...] + dx
        dg_ref[...] += dg

    tile = pl.BlockSpec((tm, d), lambda i: (i, 0))
    row = pl.BlockSpec((1, d), lambda i: (0, 0))
    return pl.pallas_call(
        body, name=name, grid=(s // tm,),
        in_specs=[tile, row, tile, tile], out_specs=[tile, row],
        out_shape=[jax.ShapeDtypeStruct((s, d), f32), jax.ShapeDtypeStruct((1, d), f32)],
        compiler_params=_cparams(("arbitrary",)),
    )(x, g_row, dy, res)


def final_loss(h, g_row, target, name):
    s, d = h.shape
    tm = min(512, s)

    def body(h_ref, g_ref, t_ref, loss_ref, dh_ref, dg_ref):
        @pl.when(pl.program_id(0) == 0)
        def _():
            dg_ref[...] = jnp.zeros_like(dg_ref)
            loss_ref[...] = jnp.zeros_like(loss_ref)

        def f(hv, gv):
            err = jnp.square(_rmsnorm(hv, gv) - t_ref[...])
            return 0.5 * jnp.sum(jnp.mean(err, axis=-1, keepdims=True), axis=0, keepdims=True)

        loss, vjp = jax.vjp(f, h_ref[...], g_ref[...])
        dh, dg = vjp(jnp.ones_like(loss))
        dh_ref[...] = dh
        dg_ref[...] += dg
        loss_ref[...] += jnp.broadcast_to(loss, loss_ref.shape)

    tile = pl.BlockSpec((tm, d), lambda i: (i, 0))
    row = pl.BlockSpec((1, d), lambda i: (0, 0))
    return pl.pallas_call(
        body, name=name, grid=(s // tm,),
        in_specs=[tile, row, tile],
        out_specs=[pl.BlockSpec((1, 128), lambda i: (0, 0)), tile, row],
        out_shape=[jax.ShapeDtypeStruct((1, 128), f32), jax.ShapeDtypeStruct((s, d), f32),
                   jax.ShapeDtypeStruct((1, d), f32)],
        compiler_params=_cparams(("arbitrary",)),
    )(h, g_row, target)


def colsum(x, name):
    s, n = x.shape
    tm = min(512, s)

    def body(x_ref, o_ref):
        @pl.when(pl.program_id(0) == 0)
        def _():
            o_ref[...] = jnp.zeros_like(o_ref)

        o_ref[...] += jnp.sum(x_ref[...].astype(f32), axis=0, keepdims=True)

    return pl.pallas_call(
        body, name=name, grid=(s // tm,),
        in_specs=[pl.BlockSpec((tm, n), lambda i: (i, 0))],
        out_specs=pl.BlockSpec((1, n), lambda i: (0, 0)),
        out_shape=jax.ShapeDtypeStruct((1, n), f32),
        compiler_params=_cparams(("arbitrary",)),
    )(x)


def _fit(dim, want):
    if dim <= want:
        return dim
    t = want
    while dim % t:
        t -= 128
    return t


def matmul(a, b, *, dims, name, out_dtype=f32, tm=1024, tn=512, tk=1024, a_pro=None, epi=None, epi_args=(),
           out_by_col_tile=False):
    if dims == "nn":
        (m, k), n = a.shape, b.shape[1]
    elif dims == "nt":
        (m, k), n = a.shape, b.shape[0]
    else:
        (k, m), n = a.shape, b.shape[1]
    tm, tn, tk = _fit(m, tm), _fit(n, tn), _fit(k, tk)
    nk = k // tk
    if dims == "nn":
        a_spec = pl.BlockSpec((tm, tk), lambda i, j, kk: (i, kk))
        b_spec = pl.BlockSpec((tk, tn), lambda i, j, kk: (kk, j))
        dn = NN
    elif dims == "nt":
        a_spec = pl.BlockSpec((tm, tk), lambda i, j, kk: (i, kk))
        b_spec = pl.BlockSpec((tn, tk), lambda i, j, kk: (j, kk))
        dn = NT
    else:
        a_spec = pl.BlockSpec((tk, tm), lambda i, j, kk: (kk, i))
        b_spec = pl.BlockSpec((tk, tn), lambda i, j, kk: (kk, j))
        dn = TN
    e_specs = [pl.BlockSpec((tm, tn), lambda i, j, kk: (i, j)) if kind == "tile"
               else pl.BlockSpec((1, tn), lambda i, j, kk: (0, j)) for kind, _ in epi_args]
    n_epi = len(epi_args)

    def body(*refs):
        a_ref, b_ref = refs[0], refs[1]
        e_refs = refs[2:2 + n_epi]
        o_ref = refs[2 + n_epi]
        av = a_ref[...]
        if a_pro is not None:
            av = a_pro(av)
        part = _mm(av, b_ref[...], dn)

        def finish(acc):
            if epi is not None:
                acc = epi(acc, *[r[...] for r in e_refs])
            o_ref[...] = acc.astype(out_dtype)

        if nk == 1:
            finish(part)
        else:
            acc_ref = refs[3 + n_epi]
            kk = pl.program_id(2)

            @pl.when(kk == 0)
            def _():
                acc_ref[...] = part

            @pl.when(kk > 0)
            def _():
                acc_ref[...] += part

            @pl.when(kk == nk - 1)
            def _():
                finish(acc_ref[...])

    if out_by_col_tile:
        out_spec = pl.BlockSpec((None, tm, tn), lambda i, j, kk: (j, i, 0))
        out_shape = jax.ShapeDtypeStruct((n // tn, m, tn), out_dtype)
    else:
        out_spec = pl.BlockSpec((tm, tn), lambda i, j, kk: (i, j))
        out_shape = jax.ShapeDtypeStruct((m, n), out_dtype)
    return pl.pallas_call(
        body, name=name, grid=(m // tm, n // tn, nk),
        in_specs=[a_spec, b_spec] + e_specs,
        out_specs=out_spec,
        out_shape=out_shape,
        scratch_shapes=[pltpu.VMEM((tm, tn), f32)] if nk > 1 else [],
        compiler_params=_cparams(("parallel", "parallel", "arbitrary")),
    )(a, b, *[arr for _, arr in epi_args])


def _relu2(a):
    r = jnp.maximum(a.astype(f32), 0.0)
    return r * r


def _add(acc, t):
    return acc + t


def _add_bias(acc, t):
    return acc + t


def _add_bias_res(acc, bias, res):
    return acc + bias + res


def _times_relu2_grad(acc, a):
    return acc * (2.0 * jnp.maximum(a.astype(f32), 0.0))


_MIXER_PARAM_SHAPES = (
    ("ln_g", (1, D_MODEL)), ("ln_b", (1, D_MODEL)), ("wm", (N_BLK, CH, CH)), ("bs_t", (CH, CH)),
    ("conv_w", (8, 2048)), ("conv_b", (1, 2048)), ("dt_bias", (1, CH)), ("a_log", (1, CH)),
    ("d_heads", (1, CH)), ("norm_g", (1, D_MODEL)),
)


def _blocks(v, n, off=0):
    return [v[:, off + i * CH: off + (i + 1) * CH] for i in range(n)]


def _split_mixer_params(vals):
    p = dict(vals)
    return {
        "ln_g": _blocks(p["ln_g"], N_BLK), "ln_b": _blocks(p["ln_b"], N_BLK),
        "wm": [p["wm"][g] for g in range(N_BLK)], "bs_t": p["bs_t"],
        "conv_w": _blocks(p["conv_w"], XBC_BLKS), "conv_b": _blocks(p["conv_b"], XBC_BLKS),
        "dt_bias": p["dt_bias"], "a_log": p["a_log"], "d_heads": p["d_heads"],
        "norm_g": _blocks(p["norm_g"], N_BLK),
    }


def _mixer_leaves(proj_ref, halo_ref, keep_halo):
    pv = proj_ref
    us = [pv[:, OFF_U + i * CH: OFF_U + (i + 1) * CH] for i in range(N_BLK)]
    vs = [pv[:, OFF_V + i * CH: OFF_V + (i + 1) * CH] for i in range(N_BLK)]
    zs = [pv[:, OFF_Z + i * CH: OFF_Z + (i + 1) * CH] for i in range(N_BLK)]
    xbcs = [pv[:, OFF_X + i * CH: OFF_X + (i + 1) * CH] for i in range(XBC_BLKS)]
    halos = [halo_ref[:, OFF_X + i * CH: OFF_X + (i + 1) * CH] * keep_halo for i in range(XBC_BLKS)]
    dtblk = pv[:, OFF_DT: OFF_DT + CH]
    return us, vs, zs, xbcs, halos, dtblk


def mixer_fwd(proj, prm):
    s = proj.shape[0]
    nc = s // CH
    names = [n for n, _ in _MIXER_PARAM_SHAPES]

    def body(proj_ref, halo_ref, *rest):
        p_refs = rest[:len(names)]
        ab_ref, hs_ref, h_ref = rest[len(names):]
        c = pl.program_id(0)

        @pl.when(c == 0)
        def _():
            h_ref[...] = jnp.zeros_like(h_ref)

        hs_ref[...] = h_ref[...]
        keep = (c > 0).astype(f32)
        us, vs, zs, xbcs, halos, dtblk = _mixer_leaves(proj_ref, halo_ref, keep)
        hps = [h_ref[i * CH:(i + 1) * CH, :] for i in range(N_BLK)]
        p = _split_mixer_params({n: r[...] for n, r in zip(names, p_refs)})
        a_out, b_out, h_out = _mixer_chunk(us, vs, zs, xbcs, halos, dtblk, hps, p)
        for i in range(N_BLK):
            ab_ref[:, i * CH:(i + 1) * CH] = a_out[i].astype(bf16)
            ab_ref[:, D_MODEL + i * CH: D_MODEL + (i + 1) * CH] = b_out[i].astype(bf16)
            h_ref[i * CH:(i + 1) * CH, :] = h_out[i]

    def const(shape):
        return pl.BlockSpec(shape, lambda c: (0,) * len(shape))

    return pl.pallas_call(
        body, name="mixer_fwd", grid=(nc,),
        in_specs=[pl.BlockSpec((CH, NP_IN), lambda c: (c, 0)),
                  pl.BlockSpec((8, NP_IN), lambda c: (jnp.maximum(c * (CH // 8) - 1, 0), 0))]
                 + [const(shp) for _, shp in _MIXER_PARAM_SHAPES],
        out_specs=[pl.BlockSpec((CH, 2 * D_MODEL), lambda c: (c, 0)),
                   pl.BlockSpec((None, D_MODEL, CH), lambda c: (c, 0, 0))],
        out_shape=[jax.ShapeDtypeStruct((s, 2 * D_MODEL), bf16), jax.ShapeDtypeStruct((nc, D_MODEL, CH), f32)],
        scratch_shapes=[pltpu.VMEM((D_MODEL, CH), f32)],
        compiler_params=_cparams(("arbitrary",)),
    )(proj, proj, *[prm[n] for n in names])


def mixer_bwd(proj, hstates, dab, prm):
    s = proj.shape[0]
    nc = s // CH
    names = [n for n, _ in _MIXER_PARAM_SHAPES]
    npar = len(names)

    def body(proj_ref, halo_ref, hs_ref, dab_ref, *rest):
        p_refs = rest[:npar]
        dproj_ref = rest[npar]
        g_refs = rest[npar + 1: 2 * npar + 1]
        dh_ref, dhalo_ref = rest[2 * npar + 1:]
        i = pl.program_id(0)
        c = nc - 1 - i

        @pl.when(i == 0)
        def _():
            dh_ref[...] = jnp.zeros_like(dh_ref)
            dhalo_ref[...] = jnp.zeros_like(dhalo_ref)
            for r in g_refs:
                r[...] = jnp.zeros_like(r)

        keep = (c > 0).astype(f32)
        us, vs, zs, xbcs, halos, dtblk = _mixer_leaves(proj_ref, halo_ref, keep)
        hps = [hs_ref[j * CH:(j + 1) * CH, :] for j in range(N_BLK)]
        pvals = {n: r[...] for n, r in zip(names, p_refs)}

        def fn(us, vs, zs, xbcs, halos, dtblk, hps, pvals):
            return _mixer_chunk(us, vs, zs, xbcs, halos, dtblk, hps, _split_mixer_params(pvals))

        _, vjp = jax.vjp(fn, us, vs, zs, xbcs, halos, dtblk, hps, pvals)
        da = [dab_ref[:, j * CH:(j + 1) * CH].astype(f32) for j in range(N_BLK)]
        db = [dab_ref[:, D_MODEL + j * CH: D_MODEL + (j + 1) * CH].astype(f32) for j in range(N_BLK)]
        dh = [dh_ref[j * CH:(j + 1) * CH, :] for j in range(N_BLK)]
        dus, dvs, dzs, dxbcs, dhalos, ddt, dhps, dp = vjp((da, db, dh))

        for j in range(N_BLK):
            dproj_ref[:, OFF_U + j * CH: OFF_U + (j + 1) * CH] = dus[j].astype(bf16)
            dproj_ref[:, OFF_V + j * CH: OFF_V + (j + 1) * CH] = dvs[j].astype(bf16)
            dproj_ref[:, OFF_Z + j * CH: OFF_Z + (j + 1) * CH] = dzs[j].astype(bf16)
            dh_ref[j * CH:(j + 1) * CH, :] = dhps[j]
        zeros_top = jnp.zeros((CH - 8, CH), f32)
        for j in range(XBC_BLKS):
            late = jnp.concatenate([zeros_top, dhalo_ref[:, j * CH:(j + 1) * CH]], axis=0)
            dproj_ref[:, OFF_X + j * CH: OFF_X + (j + 1) * CH] = (dxbcs[j] + late).astype(bf16)
        for j in range(XBC_BLKS):
            dhalo_ref[:, j * CH:(j + 1) * CH] = dhalos[j] * keep
        lane = lax.broadcasted_iota(jnp.int32, (CH, CH), 1)
        dproj_ref[:, OFF_DT: OFF_DT + CH] = jnp.where(lane < SSM_HEADS, ddt, 0.0).astype(bf16)
        dproj_ref[:, OFF_DT + CH:] = jnp.zeros((CH, NP_IN - OFF_DT - CH), bf16)
        for n, r in zip(names, g_refs):
            r[...] += dp[n]

    def const(shape):
        return pl.BlockSpec(shape, lambda i: (0,) * len(shape))

    outs = pl.pallas_call(
        body, name="mixer_bwd", grid=(nc,),
        in_specs=[pl.BlockSpec((CH, NP_IN), lambda i: (nc - 1 - i, 0)),
                  pl.BlockSpec((8, NP_IN), lambda i: (jnp.maximum((nc - 1 - i) * (CH // 8) - 1, 0), 0)),
                  pl.BlockSpec((None, D_MODEL, CH), lambda i: (nc - 1 - i, 0, 0)),
                  pl.BlockSpec((CH, 2 * D_MODEL), lambda i: (nc - 1 - i, 0))]
                 + [const(shp) for _, shp in _MIXER_PARAM_SHAPES],
        out_specs=[pl.BlockSpec((CH, NP_IN), lambda i: (nc - 1 - i, 0))]
                  + [const(shp) for _, shp in _MIXER_PARAM_SHAPES],
        out_shape=[jax.ShapeDtypeStruct((s, NP_IN), bf16)]
                  + [jax.ShapeDtypeStruct(shp, f32) for _, shp in _MIXER_PARAM_SHAPES],
        scratch_shapes=[pltpu.VMEM((D_MODEL, CH), f32), pltpu.VMEM((8, 2048), f32)],
        compiler_params=_cparams(("arbitrary",)),
    )(proj, proj, hstates, dab, *[prm[n] for n in names])
    return outs[0], dict(zip(names, outs[1:]))


_K_BLK = D_MODEL // CH
_V_BLK = _K_BLK + 1


def _attn_specs(rev, nb):
    def blk(i):
        return nb - 1 - i if rev else i

    q_spec = pl.BlockSpec((CH, D_MODEL), lambda i: (blk(i), 0))
    kv = lambda col, prev: pl.BlockSpec(
        (CH, CH), lambda i: (jnp.maximum(blk(i) - 1, 0) if prev else blk(i), col))
    return q_spec, [kv(_K_BLK, True), kv(_K_BLK, False), kv(_V_BLK, True), kv(_V_BLK, False)]


def attn_fwd(qkv, sink_row):
    s = qkv.shape[0]
    nb = s // CH

    def body(q_ref, kp_ref, kc_ref, vp_ref, vc_ref, sink_ref, o_ref):
        qps = [q_ref[:, p * CH:(p + 1) * CH] for p in range(N_BLK)]
        outs = _attn_block(qps, kp_ref[...], kc_ref[...], vp_ref[...], vc_ref[...], sink_ref[...],
                           pl.program_id(0) == 0)
        for p in range(N_BLK):
            o_ref[:, p * CH:(p + 1) * CH] = outs[p].astype(bf16)

    q_spec, kv_specs = _attn_specs(False, nb)
    return pl.pallas_call(
        body, name="attn_fwd", grid=(nb,),
        in_specs=[q_spec] + kv_specs + [pl.BlockSpec((1, CH), lambda i: (0, 0))],
        out_specs=pl.BlockSpec((CH, D_MODEL), lambda i: (i, 0)),
        out_shape=jax.ShapeDtypeStruct((s, D_MODEL), bf16),
        compiler_params=_cparams(("parallel",)),
    )(qkv, qkv, qkv, qkv, qkv, sink_row)


def attn_bwd(qkv, sink_row, dout):
    s = qkv.shape[0]
    nb = s // CH

    def body(q_ref, kp_ref, kc_ref, vp_ref, vc_ref, sink_ref, do_ref, dqkv_ref, dsink_ref, carry_ref):
        i = pl.program_id(0)
        blk = nb - 1 - i

        @pl.when(i == 0)
        def _():
            dsink_ref[...] = jnp.zeros_like(dsink_ref)
            carry_ref[...] = jnp.zeros_like(carry_ref)

        qps = [q_ref[:, p * CH:(p + 1) * CH] for p in range(N_BLK)]
        first = blk == 0
        _, vjp = jax.vjp(lambda *a: _attn_block(*a, first), qps, kp_ref[...], kc_ref[...], vp_ref[...],
                         vc_ref[...], sink_ref[...])
        dos = [do_ref[:, p * CH:(p + 1) * CH].astype(f32) for p in range(N_BLK)]
        dqs, dkp, dkc, dvp, dvc, dsink = vjp(dos)
        for p in range(N_BLK):
            dqkv_ref[:, p * CH:(p + 1) * CH] = dqs[p].astype(bf16)
        dqkv_ref[:, D_MODEL: D_MODEL + CH] = (dkc + carry_ref[0]).astype(bf16)
        dqkv_ref[:, D_MODEL + CH:] = (dvc + carry_ref[1]).astype(bf16)
        keep = jnp.logical_not(first).astype(f32)
        carry_ref[0] = dkp * keep
        carry_ref[1] = dvp * keep
        dsink_ref[...] += dsink

    q_spec, kv_specs = _attn_specs(True, nb)
    return pl.pallas_call(
        body, name="attn_bwd", grid=(nb,),
        in_specs=[q_spec] + kv_specs + [pl.BlockSpec((1, CH), lambda i: (0, 0)),
                                        pl.BlockSpec((CH, D_MODEL), lambda i: (nb - 1 - i, 0))],
        out_specs=[pl.BlockSpec((CH, QKV_DIM), lambda i: (nb - 1 - i, 0)), pl.BlockSpec((1, CH), lambda i: (0, 0))],
        out_shape=[jax.ShapeDtypeStruct((s, QKV_DIM), bf16), jax.ShapeDtypeStruct((1, CH), f32)],
        scratch_shapes=[pltpu.VMEM((2, CH, CH), f32)],
        compiler_params=_cparams(("arbitrary",)),
    )(qkv, qkv, qkv, qkv, qkv, sink_row, dout)


def adamw(w, g, m, v, name):
    r, c = w.shape
    tr = _fit(r, 256) if r % 8 == 0 else r

    def body(w_ref, g_ref, m_ref, v_ref, d_ref, nm_ref, nv_ref):
        gv = g_ref[...]
        nm = ADAM_B1 * m_ref[...] + (1.0 - ADAM_B1) * gv
        nv = ADAM_B2 * v_ref[...] + (1.0 - ADAM_B2) * jnp.square(gv)
        m_hat = nm / (1.0 - ADAM_B1 ** ADAM_STEP)
        v_hat = nv / (1.0 - ADAM_B2 ** ADAM_STEP)
        d_ref[...] = -ADAM_LR * (m_hat / (jnp.sqrt(v_hat) + ADAM_EPS) + ADAM_WD * w_ref[...])
        nm_ref[...] = nm
        nv_ref[...] = nv

    tile = pl.BlockSpec((tr, c), lambda i: (i, 0))
    return pl.pallas_call(
        body, name=name, grid=(r // tr,),
        in_specs=[tile] * 4, out_specs=[tile] * 3,
        out_shape=[jax.ShapeDtypeStruct((r, c), f32)] * 3,
        compiler_params=_cparams(("parallel",)),
    )(w, g, m, v)


_MESH = pl.DeviceIdType.MESH
_ANY = pl.BlockSpec(memory_space=pl.ANY)
_VMEM = pl.BlockSpec(memory_space=pltpu.VMEM)


def _place():
    x, y, c = lax.axis_index("x"), lax.axis_index("y"), lax.axis_index("c")
    chips = [(1 - x, y), (x, 1 - y), (1 - x, 1 - y)]
    return x, y, c, 2 * x + y, chips, [2 * cx + cy for cx, cy in chips]


def _half(c, rows):
    return pl.ds(pl.multiple_of(c * (rows // 2), 16), rows // 2)


def _step_rows(rows):
    return max(t for t in range(16, 641, 16) if rows % t == 0)


def place_shard(b, slot, name):
    r, c = b.shape
    tr = _step_rows(r)

    def body(slot_ref, b_ref, o_ref):
        o_ref[...] = b_ref[...]

    return pl.pallas_call(
        body, name=name,
        grid_spec=pltpu.PrefetchScalarGridSpec(
            num_scalar_prefetch=1, grid=(r // tr,),
            in_specs=[pl.BlockSpec((tr, c), lambda i, s: (i, 0))],
            out_specs=pl.BlockSpec((None, tr, c), lambda i, s: (s[0], i, 0))),
        out_shape=jax.ShapeDtypeStruct((N_CHIPS, r, c), b.dtype),
        compiler_params=_cparams(("parallel",)),
    )(slot, b)


def allgather_weights(bufs):
    n = len(bufs)

    def body(*refs):
        out_refs = refs[n:2 * n]
        send_sems, recv_sems = refs[2 * n:]
        x, y, c, me, chips, chip_idx = _place()
        sibling = (x, y, 1 - c)

        def copy(b, k, shard, half, to):
            ref = out_refs[b]
            part = ref.at[shard, _half(half, ref.shape[1])]
            return pltpu.make_async_remote_copy(
                src_ref=part, dst_ref=part, send_sem=send_sems.at[6 * b + k], recv_sem=recv_sems.at[6 * b + k],
                device_id=to, device_id_type=_MESH)

        first = [copy(b, j, me, c, (*chips[j], c)) for j in range(3) for b in range(n)]
        for cp in first:
            cp.start()
        passed = []
        for j in range(3):
            for b in range(n):
                copy(b, j, chip_idx[j], c, sibling).wait_recv()
                passed.append(copy(b, 3 + j, chip_idx[j], c, sibling))
                passed[-1].start()
        for j in range(3):
            for b in range(n):
                copy(b, 3 + j, chip_idx[j], 1 - c, sibling).wait_recv()
        for cp in first + passed:
            cp.wait_send()

    return pl.pallas_call(
        body, name="allgather_weights",
        out_shape=[jax.ShapeDtypeStruct(b.shape, b.dtype) for b in bufs],
        in_specs=[_ANY] * n, out_specs=[_ANY] * n, input_output_aliases={i: i for i in range(n)},
        scratch_shapes=[pltpu.SemaphoreType.DMA((6 * n,)), pltpu.SemaphoreType.DMA((6 * n,))],
    )(*bufs)


def exchange_halves(bufs):
    n = len(bufs)

    def body(*refs):
        g_refs, out_refs = refs[:n], refs[n:2 * n]
        send_sems, recv_sems = refs[2 * n:]
        x, y, c, *_ = _place()
        cps = [pltpu.make_async_remote_copy(
            src_ref=g_refs[b].at[:, _half(1 - c, g_refs[b].shape[1])], dst_ref=out_refs[b],
            send_sem=send_sems.at[b], recv_sem=recv_sems.at[b], device_id=(x, y, 1 - c), device_id_type=_MESH)
            for b in range(n)]
        for cp in cps:
            cp.start()
        for cp in cps:
            cp.wait()

    return pl.pallas_call(
        body, name="exchange_halves",
        out_shape=[jax.ShapeDtypeStruct((N_CHIPS, b.shape[1] // 2, b.shape[2]), b.dtype) for b in bufs],
        in_specs=[_ANY] * n, out_specs=[_ANY] * n,
        scratch_shapes=[pltpu.SemaphoreType.DMA((n,)), pltpu.SemaphoreType.DMA((n,))],
    )(*bufs)


def add_halves(g, got, c_idx, name):
    hr, cols = got.shape[1], got.shape[2]
    tr = _step_rows(hr)
    steps = hr // tr

    def body(c_ref, g_ref, got_ref, o_ref):
        o_ref[...] = (g_ref[...].astype(f32) + got_ref[...].astype(f32)).astype(bf16)

    return pl.pallas_call(
        body, name=name,
        grid_spec=pltpu.PrefetchScalarGridSpec(
            num_scalar_prefetch=1, grid=(N_CHIPS, steps),
            in_specs=[pl.BlockSpec((None, tr, cols), lambda s, i, c: (s, c[0] * steps + i, 0)),
                      pl.BlockSpec((None, tr, cols), lambda s, i, c: (s, i, 0))],
            out_specs=pl.BlockSpec((None, tr, cols), lambda s, i, c: (s, i, 0))),
        out_shape=jax.ShapeDtypeStruct(got.shape, bf16),
        compiler_params=_cparams(("parallel", "parallel")),
    )(c_idx, g, got)


def scatter_chip_sums(bufs):
    n = len(bufs)

    def body(*refs):
        t_refs, out_refs = refs[:n], refs[n:2 * n]
        send_sems, recv_sems = refs[2 * n:]
        x, y, c, me, chips, chip_idx = _place()
        cps = [pltpu.make_async_remote_copy(
            src_ref=t_refs[b].at[chip_idx[j]], dst_ref=out_refs[b].at[j], send_sem=send_sems.at[3 * b + j],
            recv_sem=recv_sems.at[3 * b + j], device_id=(*chips[j], c), device_id_type=_MESH)
            for j in range(3) for b in range(n)]
        for cp in cps:
            cp.start()
        for cp in cps:
            cp.wait_recv()
        for cp in cps:
            cp.wait_send()

    return pl.pallas_call(
        body, name="scatter_chip_sums",
        out_shape=[jax.ShapeDtypeStruct((3,) + b.shape[1:], b.dtype) for b in bufs],
        in_specs=[_ANY] * n, out_specs=[_ANY] * n,
        scratch_shapes=[pltpu.SemaphoreType.DMA((3 * n,)), pltpu.SemaphoreType.DMA((3 * n,))],
    )(*bufs)


def sum_chips(t, got, place_idx, name):
    hr, cols = t.shape[1], t.shape[2]
    tr = _step_rows(hr)
    steps = hr // tr

    def body(idx_ref, t_ref, got_ref, o_ref):
        acc = t_ref[...].astype(f32)
        for j in range(3):
            acc = acc + got_ref[j].astype(f32)
        o_ref[...] = acc

    return pl.pallas_call(
        body, name=name,
        grid_spec=pltpu.PrefetchScalarGridSpec(
            num_scalar_prefetch=1, grid=(steps,),
            in_specs=[pl.BlockSpec((None, tr, cols), lambda i, idx: (idx[0], i, 0)),
                      pl.BlockSpec((3, tr, cols), lambda i, idx: (0, i, 0))],
            out_specs=pl.BlockSpec((tr, cols), lambda i, idx: (idx[1] * steps + i, 0))),
        out_shape=jax.ShapeDtypeStruct((2 * hr, cols), f32),
        compiler_params=_cparams(("parallel",)),
    )(place_idx, t, got)


def share_halves(bufs):
    n = len(bufs)

    def body(*refs):
        out_refs = refs[n:2 * n]
        send_sems, recv_sems = refs[2 * n:]
        x, y, c, *_ = _place()

        def copy(b, half):
            part = out_refs[b].at[_half(half, out_refs[b].shape[0])]
            return pltpu.make_async_remote_copy(
                src_ref=part, dst_ref=part, send_sem=send_sems.at[b], recv_sem=recv_sems.at[b],
                device_id=(x, y, 1 - c), device_id_type=_MESH)

        for b in range(n):
            copy(b, c).start()
        for b in range(n):
            copy(b, 1 - c).wait_recv()
        for b in range(n):
            copy(b, c).wait_send()

    return pl.pallas_call(
        body, name="share_halves",
        out_shape=[jax.ShapeDtypeStruct(b.shape, b.dtype) for b in bufs],
        in_specs=[_ANY] * n, out_specs=[_ANY] * n, input_output_aliases={i: i for i in range(n)},
        scratch_shapes=[pltpu.SemaphoreType.DMA((n,)), pltpu.SemaphoreType.DMA((n,))],
    )(*bufs)


def allreduce_small(sp):
    def body(s_ref, out_ref, gather_ref, send_sems, recv_sems):
        x, y, c, me, chips, chip_idx = _place()
        sibling = (x, y, 1 - c)

        def copy(k, chip, core, to, src=None):
            dst = gather_ref.at[2 * chip + core]
            return pltpu.make_async_remote_copy(
                src_ref=dst if src is None else src, dst_ref=dst, send_sem=send_sems.at[k],
                recv_sem=recv_sems.at[k], device_id=to, device_id_type=_MESH)

        first = [copy(0, me, c, sibling, src=s_ref)]
        first += [copy(1 + j, me, c, (*chips[j], c), src=s_ref) for j in range(3)]
        for cp in first:
            cp.start()
        gather_ref[2 * me + c] = s_ref[...]
        passed = [copy(4 + j, chip_idx[j], c, sibling) for j in range(3)]
        for j in range(3):
            copy(1 + j, chip_idx[j], c, sibling).wait_recv()
            passed[j].start()
        copy(0, me, 1 - c, sibling).wait_recv()
        for j in range(3):
            copy(4 + j, chip_idx[j], 1 - c, sibling).wait_recv()
        for cp in first + passed:
            cp.wait_send()
        acc = gather_ref[0]
        for d in range(1, 2 * N_CHIPS):
            acc = acc + gather_ref[d]
        out_ref[...] = acc

    return pl.pallas_call(
        body, name="allreduce_small",
        out_shape=jax.ShapeDtypeStruct(sp.shape, sp.dtype),
        in_specs=[_VMEM], out_specs=_VMEM,
        scratch_shapes=[pltpu.VMEM((2 * N_CHIPS,) + sp.shape, sp.dtype),
                        pltpu.SemaphoreType.DMA((7,)), pltpu.SemaphoreType.DMA((7,))],
        compiler_params=_cparams(),
    )(sp)


def _n_rows(shape):
    n = 1
    for d in shape:
        n *= d
    return 8 * (-(-n // 8192))


def _pack(arrays, total_rows):
    parts = []
    for a in arrays:
        flat = a.reshape(-1)
        parts.append(jnp.pad(flat, (0, 1024 * _n_rows(a.shape) - flat.shape[0])).reshape(-1, 1024))
    rows = jnp.concatenate(parts, axis=0)
    return jnp.pad(rows, ((0, total_rows - rows.shape[0]), (0, 0)))


def _unpack(packed, shapes):
    out, r = [], 0
    for shp in shapes:
        n = 1
        for d in shp:
            n *= d
        nr = _n_rows(shp)
        out.append(packed[r:r + nr].reshape(-1)[:n].reshape(shp))
        r += nr
    return out


_BIG_NAMES = ("w_in_even", "w_out_even", "w_qkv", "w_o", "w_up", "w_down")
B1_ROWS = 4864
_B1_UP = 768
_B1_DOWN = 2816
IN_SHARD, IN_PAD = 1284, 1408
QKV_SHARD, QKV_PAD = 320, 384
B2_COLS = IN_PAD + QKV_PAD


def _lane_padded(a, cols):
    return jnp.pad(a, ((0, 0), (0, cols - a.shape[1])))

_SMALL_SHAPES = (
    ("norm_mix_g", (2, 1024)), ("norm_mlp_g", (2, 1024)), ("final_norm_g", (1024,)), ("gm_ln_g", (1, 1024)),
    ("gm_ln_b", (1, 1024)), ("gm_w_s", (1, 8, 128, 128)), ("gm_b_s", (1, 8, 128)), ("ssm_conv_b", (1, 2048)),
    ("ssm_dt_bias", (1, 16)), ("ssm_a_log", (1, 16)), ("ssm_d", (1, 16)), ("ssm_norm_g", (1, 1024)),
    ("attn_sinks", (1, 16)), ("ssm_conv_w", (1, 4, 2048)), ("b_qkv", (1, 1280)), ("b_o", (1, 1024)),
)
_N_REPLICATED = 13
_SHARDED_SMALL = (("ssm_conv_w", 2, 512), ("b_qkv", 1, 320), ("b_o", 1, 256))
_GATHER_ROWS = 24
_SHARD_PACK_ROWS = 24


def _weight_shards_to_buffers(w):
    b1 = jnp.concatenate([w["w_out_even"][0], w["w_o"][0], w["w_up"][0], w["w_up"][1], w["w_down"][0],
                          w["w_down"][1]], axis=0).astype(bf16)
    b2 = jnp.concatenate([_lane_padded(w["w_in_even"][0], IN_PAD), _lane_padded(w["w_qkv"][0], QKV_PAD)],
                         axis=1).astype(bf16)
    return b1, b2


def _full_big_grads_to_buffers(dw_in_p, dw_out, dw_qkv, dw_o, dw_up, dw_down):
    by_owner = lambda a: a.reshape(N_CHIPS, a.shape[0] // N_CHIPS, a.shape[1])
    x1 = jnp.concatenate([by_owner(dw_out), by_owner(dw_o), dw_up[0], dw_up[1], by_owner(dw_down[0]),
                          by_owner(dw_down[1])], axis=1)
    x2 = jnp.stack([jnp.concatenate(
        [_lane_padded(dw_in_p[:, IN_SHARD * s: IN_SHARD * (s + 1)], IN_PAD),
         _lane_padded(dw_qkv[:, QKV_SHARD * s: QKV_SHARD * (s + 1)], QKV_PAD)], axis=1) for s in range(N_CHIPS)])
    return x1, x2


def _reduced_buffers_to_grads(r1, r2):
    return {
        "w_out_even": r1[None, :512], "w_o": r1[None, 512:_B1_UP],
        "w_up": r1[_B1_UP:_B1_DOWN].reshape(2, 1024, 1024), "w_down": r1[_B1_DOWN:].reshape(2, 1024, 1024),
        "w_in_even": r2[None, :, :IN_SHARD], "w_qkv": r2[None, :, IN_PAD:IN_PAD + QKV_SHARD],
    }


def _gathered_to_full_weights(g1, g2):
    cols_by_owner = lambda a: a.transpose(1, 0, 2).reshape(a.shape[1], -1)
    w_in_p = _lane_padded(cols_by_owner(g2[:, :, :IN_SHARD]), NP_IN)
    w_qkv = cols_by_owner(g2[:, :, IN_PAD:IN_PAD + QKV_SHARD])
    w_out = g1[:, :512].reshape(2048, 1024)
    w_o = g1[:, 512:_B1_UP].reshape(1024, 1024)
    w_up = [cols_by_owner(g1[:, _B1_UP + 1024 * l: _B1_UP + 1024 * (l + 1)]) for l in range(2)]
    w_down = [g1[:, _B1_DOWN + 1024 * l: _B1_DOWN + 1024 * (l + 1)].reshape(4096, 1024) for l in range(2)]
    return w_in_p, w_out, w_qkv, w_o, w_up, w_down


def _row2(v):
    return v.reshape(1, -1)


def _lane_pad(v):
    return jnp.pad(v, ((0, 0), (0, CH - v.shape[1])))


def _mlp_fwd(h, g_row, w_up, w_down, tag):
    y = rmsnorm_fwd(h, g_row, f"mlp_norm{tag}")
    a = matmul(y, w_up, dims="nn", name=f"mlp_up{tag}", out_dtype=bf16, tn=1024)
    out = matmul(a, w_down, dims="nn", name=f"mlp_down{tag}", a_pro=_relu2, epi=_add, epi_args=(("tile", h),))
    return out, y, a


def _mlp_bwd(dh_out, h, g_row, y, a, w_up, w_down, tag):
    da = matmul(dh_out, w_down, dims="nt", name=f"mlp_da{tag}", out_dtype=bf16, tn=1024,
                epi=_times_relu2_grad, epi_args=(("tile", a),))
    dw_down = matmul(a, dh_out, dims="tn", name=f"mlp_dwdown{tag}", out_dtype=bf16, a_pro=_relu2)
    dw_up = matmul(y, da, dims="tn", name=f"mlp_dwup{tag}", out_dtype=bf16, tn=1024, out_by_col_tile=True)
    dy = matmul(da, w_up, dims="nt", name=f"mlp_dy{tag}")
    dh, dg = rmsnorm_bwd(h, g_row, dy, dh_out, f"mlp_dnorm{tag}")
    return dh, dg, dw_up, dw_down


def _local_step(x, target, big, sm):
    w_in_p, w_out, w_qkv, w_o, w_up, w_down = big
    mix_g = [_row2(sm["norm_mix_g"][i]) for i in range(2)]
    mlp_g = [_row2(sm["norm_mlp_g"][i]) for i in range(2)]
    mixer_prm = {
        "ln_g": sm["gm_ln_g"], "ln_b": sm["gm_ln_b"], "wm": sm["gm_w_s"][0],
        "bs_t": jnp.pad(sm["gm_b_s"][0].T, ((0, 0), (0, CH - N_BLK))),
        "conv_w": jnp.pad(sm["ssm_conv_w"][0], ((0, 4), (0, 0))), "conv_b": sm["ssm_conv_b"],
        "dt_bias": _lane_pad(sm["ssm_dt_bias"]), "a_log": _lane_pad(sm["ssm_a_log"]),
        "d_heads": _lane_pad(sm["ssm_d"]), "norm_g": sm["ssm_norm_g"],
    }
    sink_row = _lane_pad(sm["attn_sinks"])

    y0 = rmsnorm_fwd(x, mix_g[0], "mix_norm0")
    proj = matmul(y0, w_in_p, dims="nn", name="in_proj", tn=768)
    ab, hstates = mixer_fwd(proj, mixer_prm)
    h1 = matmul(ab, w_out, dims="nn", name="out_proj", epi=_add, epi_args=(("tile", x),))
    h2, y1, a1 = _mlp_fwd(h1, mlp_g[0], w_up[0], w_down[0], 0)
    y2 = rmsnorm_fwd(h2, mix_g[1], "mix_norm1")
    qkv = matmul(y2, w_qkv, dims="nn", name="qkv_proj", tn=QKV_DIM, epi=_add_bias, epi_args=(("row", sm["b_qkv"]),))
    att = attn_fwd(qkv, sink_row)
    h3 = matmul(att, w_o, dims="nn", name="o_proj", epi=_add_bias_res,
                epi_args=(("row", sm["b_o"]), ("tile", h2)))
    h4, y3, a3 = _mlp_fwd(h3, mlp_g[1], w_up[1], w_down[1], 1)
    loss, dh4, dg_final = final_loss(h4, _row2(sm["final_norm_g"]), target, "final_loss")

    dh3, dg_mlp1, dw_up1, dw_down1 = _mlp_bwd(dh4, h3, mlp_g[1], y3, a3, w_up[1], w_down[1], 1)
    db_o = colsum(dh3, "db_o")
    datt = matmul(dh3, w_o, dims="nt", name="attn_dout", out_dtype=bf16)
    dw_o = matmul(att, dh3, dims="tn", name="dw_o", out_dtype=bf16)
    dqkv, dsink = attn_bwd(qkv, sink_row, datt)
    db_qkv = colsum(dqkv, "db_qkv")
    dw_qkv = matmul(y2, dqkv, dims="tn", name="dw_qkv", out_dtype=bf16, tn=QKV_DIM)
    dy2 = matmul(dqkv, w_qkv, dims="nt", name="dy_qkv", tk=QKV_DIM)
    dh2, dg_mix1 = rmsnorm_bwd(h2, mix_g[1], dy2, dh3, "mix_dnorm1")
    dh1, dg_mlp0, dw_up0, dw_down0 = _mlp_bwd(dh2, h1, mlp_g[0], y1, a1, w_up[0], w_down[0], 0)
    dab = matmul(dh1, w_out, dims="nt", name="mixer_dout", tn=1024)
    dw_out = matmul(ab, dh1, dims="tn", name="dw_out", out_dtype=bf16)
    dproj, dmix = mixer_bwd(proj, hstates, dab, mixer_prm)
    dw_in_p = matmul(y0, dproj, dims="tn", name="dw_in", out_dtype=bf16, tn=768)
    dy0 = matmul(dproj, w_in_p, dims="nt", name="dy_in", tk=768)
    dx, dg_mix0 = rmsnorm_bwd(x, mix_g[0], dy0, dh1, "mix_dnorm0")

    small_grads = {
        "norm_mix_g": jnp.concatenate([dg_mix0, dg_mix1], axis=0),
        "norm_mlp_g": jnp.concatenate([dg_mlp0, dg_mlp1], axis=0),
        "final_norm_g": dg_final[0], "gm_ln_g": dmix["ln_g"], "gm_ln_b": dmix["ln_b"],
        "gm_w_s": dmix["wm"][None], "gm_b_s": dmix["bs_t"][:, :N_BLK].T[None],
        "ssm_conv_b": dmix["conv_b"], "ssm_dt_bias": dmix["dt_bias"][:, :SSM_HEADS],
        "ssm_a_log": dmix["a_log"][:, :SSM_HEADS], "ssm_d": dmix["d_heads"][:, :SSM_HEADS],
        "ssm_norm_g": dmix["norm_g"], "attn_sinks": dsink[:, :SSM_HEADS],
        "ssm_conv_w": dmix["conv_w"][None, :4], "b_qkv": db_qkv, "b_o": db_o,
    }
    big_grads = (dw_in_p, dw_out, dw_qkv, dw_o, (dw_up0, dw_up1), (dw_down0, dw_down1))
    return loss, dx, big_grads, small_grads


def kernel(x, norm_mix_g, norm_mlp_g, final_norm_g, w_in_even, w_out_even, gm_ln_g, gm_ln_b, gm_w_s, gm_b_s, ssm_conv_w, ssm_conv_b, ssm_dt_bias, ssm_a_log, ssm_d, ssm_norm_g, w_qkv, b_qkv, w_o, b_o, attn_sinks, w_up, w_down, loss_target, m_norm_mix_g, m_norm_mlp_g, m_final_norm_g, m_w_in_even, m_w_out_even, m_gm_ln_g, m_gm_ln_b, m_gm_w_s, m_gm_b_s, m_ssm_conv_w, m_ssm_conv_b, m_ssm_dt_bias, m_ssm_a_log, m_ssm_d, m_ssm_norm_g, m_w_qkv, m_b_qkv, m_w_o, m_b_o, m_attn_sinks, m_w_up, m_w_down, v_norm_mix_g, v_norm_mlp_g, v_final_norm_g, v_w_in_even, v_w_out_even, v_gm_ln_g, v_gm_ln_b, v_gm_w_s, v_gm_b_s, v_ssm_conv_w, v_ssm_conv_b, v_ssm_dt_bias, v_ssm_a_log, v_ssm_d, v_ssm_norm_g, v_w_qkv, v_b_qkv, v_w_o, v_b_o, v_attn_sinks, v_w_up, v_w_down):
    w = dict(norm_mix_g=norm_mix_g, norm_mlp_g=norm_mlp_g, final_norm_g=final_norm_g, w_in_even=w_in_even,
             w_out_even=w_out_even, gm_ln_g=gm_ln_g, gm_ln_b=gm_ln_b, gm_w_s=gm_w_s, gm_b_s=gm_b_s,
             ssm_conv_w=ssm_conv_w, ssm_conv_b=ssm_conv_b, ssm_dt_bias=ssm_dt_bias, ssm_a_log=ssm_a_log,
             ssm_d=ssm_d, ssm_norm_g=ssm_norm_g, w_qkv=w_qkv, b_qkv=b_qkv, w_o=w_o, b_o=b_o,
             attn_sinks=attn_sinks, w_up=w_up, w_down=w_down)
    m = dict(norm_mix_g=m_norm_mix_g, norm_mlp_g=m_norm_mlp_g, final_norm_g=m_final_norm_g,
             w_in_even=m_w_in_even, w_out_even=m_w_out_even, gm_ln_g=m_gm_ln_g, gm_ln_b=m_gm_ln_b,
             gm_w_s=m_gm_w_s, gm_b_s=m_gm_b_s, ssm_conv_w=m_ssm_conv_w, ssm_conv_b=m_ssm_conv_b,
             ssm_dt_bias=m_ssm_dt_bias, ssm_a_log=m_ssm_a_log, ssm_d=m_ssm_d, ssm_norm_g=m_ssm_norm_g,
             w_qkv=m_w_qkv, b_qkv=m_b_qkv, w_o=m_w_o, b_o=m_b_o, attn_sinks=m_attn_sinks, w_up=m_w_up,
             w_down=m_w_down)
    v = dict(norm_mix_g=v_norm_mix_g, norm_mlp_g=v_norm_mlp_g, final_norm_g=v_final_norm_g,
             w_in_even=v_w_in_even, w_out_even=v_w_out_even, gm_ln_g=v_gm_ln_g, gm_ln_b=v_gm_ln_b,
             gm_w_s=v_gm_w_s, gm_b_s=v_gm_b_s, ssm_conv_w=v_ssm_conv_w, ssm_conv_b=v_ssm_conv_b,
             ssm_dt_bias=v_ssm_dt_bias, ssm_a_log=v_ssm_a_log, ssm_d=v_ssm_d, ssm_norm_g=v_ssm_norm_g,
             w_qkv=v_w_qkv, b_qkv=v_b_qkv, w_o=v_w_o, b_o=v_b_o, attn_sinks=v_attn_sinks, w_up=v_w_up,
             w_down=v_w_down)
    names = ("norm_mix_g", "norm_mlp_g", "final_norm_g", "w_in_even", "w_out_even", "gm_ln_g", "gm_ln_b",
             "gm_w_s", "gm_b_s", "ssm_conv_w", "ssm_conv_b", "ssm_dt_bias", "ssm_a_log", "ssm_d", "ssm_norm_g",
             "w_qkv", "b_qkv", "w_o", "b_o", "attn_sinks", "w_up", "w_down")

    cx, cy, cc = lax.axis_index("x"), lax.axis_index("y"), lax.axis_index("c")
    chip = 2 * cx + cy
    c_idx = jnp.reshape(cc, (1,)).astype(jnp.int32)
    chip_idx = jnp.reshape(chip, (1,)).astype(jnp.int32)

    b1, b2 = _weight_shards_to_buffers(w)
    big = _gathered_to_full_weights(*allgather_weights(
        [place_shard(b1, chip_idx, "place_shard1"), place_shard(b2, chip_idx, "place_shard2")]))

    small_shard_placed = []
    for name, axis, width in _SHARDED_SMALL:
        full_shape = dict(_SMALL_SHAPES)[name]
        placed = lax.dynamic_update_slice_in_dim(jnp.zeros(full_shape, f32), w[name], chip * width, axis)
        small_shard_placed.append(jnp.where(cc == 0, placed, 0.0))
    gathered_small = _unpack(allreduce_small(_pack(small_shard_placed, _GATHER_ROWS)),
                             [dict(_SMALL_SHAPES)[n] for n, _, _ in _SHARDED_SMALL])
    sm = {n: w[n] for n, _ in _SMALL_SHAPES[:_N_REPLICATED]}
    sm.update({n: g for (n, _, _), g in zip(_SHARDED_SMALL, gathered_small)})

    loss_part, dx, big_grads, small_grads = _local_step(x[0], loss_target[0], big, sm)
    loss = lax.psum(loss_part[0, 0], ("x", "y", "c"))

    xs = _full_big_grads_to_buffers(*big_grads)
    ts = [add_halves(xb, got, c_idx, f"add_halves{i}") for i, (xb, got) in enumerate(zip(xs, exchange_halves(xs)))]
    place_idx = jnp.concatenate([chip_idx, c_idx])
    sums = [sum_chips(tb, got, place_idx, f"sum_chips{i}") for i, (tb, got) in enumerate(zip(ts, scatter_chip_sums(ts)))]
    grads = _reduced_buffers_to_grads(*share_halves(sums))
    small_sum = allreduce_small(_pack([small_grads[n] for n, _ in _SMALL_SHAPES], SMALL_ROWS))
    small_full = dict(zip([n for n, _ in _SMALL_SHAPES], _unpack(small_sum, [s for _, s in _SMALL_SHAPES])))
    for n, _ in _SMALL_SHAPES[:_N_REPLICATED]:
        grads[n] = small_full[n]
    for n, axis, width in _SHARDED_SMALL:
        grads[n] = lax.dynamic_slice_in_dim(small_full[n], chip * width, width, axis)
    grads = {n: grads[n].reshape(w[n].shape) for n in names}

    delta, new_m, new_v = {}, {}, {}
    for n in _BIG_NAMES:
        cols = w[n].shape[-1]
        d2, m2, v2 = adamw(w[n].reshape(-1, cols), grads[n].reshape(-1, cols), m[n].reshape(-1, cols),
                           v[n].reshape(-1, cols), f"adamw_{n}")
        delta[n], new_m[n], new_v[n] = (t2.reshape(w[n].shape) for t2 in (d2, m2, v2))
    rep = [n for n, _ in _SMALL_SHAPES[:_N_REPLICATED]]
    shd = [n for n, _, _ in _SHARDED_SMALL]
    for group, rows, tag in ((rep, SMALL_ROWS, "replicated"), (shd, _SHARD_PACK_ROWS, "sharded")):
        packed = [_pack([d[n] for n in group], rows) for d in (w, grads, m, v)]
        outs = adamw(*packed, f"adamw_small_{tag}")
        shapes = [w[n].shape for n in group]
        for res, o in zip((delta, new_m, new_v), outs):
            res.update(dict(zip(group, _unpack(o, shapes))))

    return (loss, dx[None], *[grads[n] for n in names], *[delta[n] for n in names],
            *[new_m[n] for n in names], *[new_v[n] for n in names])
```

```python
import functools

import jax
import jax.numpy as jnp
from jax import lax
from jax.experimental import pallas as pl
from jax.experimental.pallas import tpu as pltpu

f32 = jnp.float32
bf16 = jnp.bfloat16
MXU_DTYPE = bf16

RMS_EPS = 1e-5
LN_EPS = 1e-5
D_MODEL = 1024
D_FF = 4096
CH = 128
N_BLK = 8
SSM_HEADS = 16
IN_EVEN = 5136
NP_IN = 5376
OFF_U, OFF_V, OFF_Z, OFF_X, OFF_DT = 0, 1024, 2048, 3072, 5120
XBC_BLKS = 16
QKV_DIM = 1280
ATT_SCALE = 64 ** -0.5

ADAM_LR = 0.001
ADAM_B1 = 0.9
ADAM_B2 = 0.999
ADAM_EPS = 1e-08
ADAM_WD = 0.01
ADAM_STEP = 10

VMEM_LIMIT_BYTES = 48 * 1024 * 1024
N_CHIPS = 4
SMALL_ROWS = 256

NN = ((1,), (0,))
NT = ((1,), (1,))
TN = ((0,), (0,))


def _mm(a, b, dims):
    return lax.dot_general(a.astype(MXU_DTYPE), b.astype(MXU_DTYPE), (dims, ((), ())),
                           preferred_element_type=f32)


def _mm_exact(a, b):
    return jnp.dot(a, b, preferred_element_type=f32, precision=lax.Precision.HIGHEST)


def _cparams(sem=None):
    return pltpu.CompilerParams(dimension_semantics=sem, vmem_limit_bytes=VMEM_LIMIT_BYTES)


@jax.custom_vjp
def _swap64(x):
    return pltpu.roll(x, 64, axis=1)


_swap64.defvjp(lambda x: (pltpu.roll(x, 64, axis=1), None), lambda _, g: (pltpu.roll(g, 64, axis=1),))


def _make_delay(k):
    @jax.custom_vjp
    def delay(ext):
        return pltpu.roll(ext, k, axis=0)[8:, :]

    def fwd(ext):
        return delay(ext), None

    def bwd(_, g):
        gp = jnp.concatenate([jnp.zeros((8, g.shape[1]), g.dtype), g], axis=0)
        return (pltpu.roll(gp, gp.shape[0] - k, axis=0),)

    delay.defvjp(fwd, bwd)
    return delay


_DELAYS = {k: _make_delay(k) for k in (1, 2, 3)}


def _col(m, lane, h):
    return jnp.sum(jnp.where(lane == h, m, 0.0), axis=1, keepdims=True)


def _row(m, sub, h):
    return jnp.sum(jnp.where(sub == h, m, 0.0), axis=0, keepdims=True)


def _mixer_chunk(us, vs, zs, xbcs, halos, dtblk, hps, prm):
    lane = lax.broadcasted_iota(jnp.int32, (CH, CH), 1)
    sub = lax.broadcasted_iota(jnp.int32, (CH, CH), 0)
    left = lane < 64
    top = sub < 64
    causal = sub >= lane

    gus = [jax.nn.gelu(u) for u in us]
    gvs = [jax.nn.gelu(v) for v in vs]
    mu = sum(jnp.sum(g, axis=1, keepdims=True) for g in gvs) / D_MODEL
    cen = [g - mu for g in gvs]
    var = sum(jnp.sum(c * c, axis=1, keepdims=True) for c in cen) / D_MODEL
    rstd = lax.rsqrt(var + LN_EPS)
    a_out = []
    for g in range(N_BLK):
        vn = cen[g] * rstd * prm["ln_g"][g] + prm["ln_b"][g]
        w = jnp.where(causal, prm["wm"][g], 0.0)
        mixed = _mm(w, vn, NN) + _col(prm["bs_t"], lane, g)
        a_out.append(gus[g] * mixed)

    act = []
    for b in range(XBC_BLKS):
        w8 = prm["conv_w"][b]
        sub8 = lax.broadcasted_iota(jnp.int32, w8.shape, 0)
        ext = jnp.concatenate([halos[b], xbcs[b]], axis=0)
        conv = xbcs[b] * _row(w8, sub8, 3) + prm["conv_b"][b]
        for k in (1, 2, 3):
            conv = conv + _DELAYS[k](ext) * _row(w8, sub8, 3 - k)
        act.append(jax.nn.silu(conv))

    dt = jax.nn.softplus(dtblk + prm["dt_bias"])
    a_neg = -jnp.exp(prm["a_log"])
    tri = causal.astype(f32)
    acum = _mm_exact(tri, dt * a_neg)
    acum_t = acum.T
    dt_t = dt.T
    last = sub == CH - 1
    ys, h_out = [], []
    for grp in range(4):
        bm = act[8 + grp]
        cm = act[12 + grp]
        cb = _mm(cm, bm, NT)
        for p in (2 * grp, 2 * grp + 1):
            h0, h1 = 2 * p, 2 * p + 1
            xp = act[p]
            hp = hps[p]
            wis = []
            for h in (h0, h1):
                seg = _col(acum, lane, h) - _row(acum_t, sub, h)
                decay = jnp.exp(jnp.where(causal, seg, -jnp.inf))
                wis.append(cb * decay * _row(dt_t, sub, h))
            wcat = jnp.concatenate(wis, axis=1)
            xbd = jnp.concatenate([jnp.where(left, xp, 0.0), jnp.where(left, 0.0, xp)], axis=0)
            y_diag = _mm(wcat, xbd, NN)
            a_end = [jnp.sum(jnp.where(last & (lane == h), acum, 0.0), keepdims=True) for h in (h0, h1)]
            a_col = jnp.where(left, _col(acum, lane, h0), _col(acum, lane, h1))
            dt_col = jnp.where(left, _col(dt, lane, h0), _col(dt, lane, h1))
            to_end = jnp.exp(jnp.where(left, a_end[0], a_end[1]) - a_col) * dt_col
            states = _mm(xp * to_end, bm, TN)
            chunk_decay = jnp.where(top, jnp.exp(a_end[0]), jnp.exp(a_end[1]))
            h_out.append(chunk_decay * hp + states)
            y_off = jnp.exp(a_col) * _mm(cm, hp, NT)
            d_skip = jnp.where(left[:1], _col(prm["d_heads"], lane[:1], h0), _col(prm["d_heads"], lane[:1], h1))
            ys.append((y_diag + y_off + xp * d_skip) * jax.nn.silu(zs[p]))

    b_out = []
    for grp in range(4):
        pair = (ys[2 * grp], ys[2 * grp + 1])
        ms = sum(jnp.sum(y * y, axis=1, keepdims=True) for y in pair) / 256.0
        r = lax.rsqrt(ms + RMS_EPS)
        for j, y in enumerate(pair):
            b_out.append(y * r * prm["norm_g"][2 * grp + j])
    return a_out, b_out, h_out


def _attn_block(qps, kprev, kcur, vprev, vcur, sink_row, first):
    lane = lax.broadcasted_iota(jnp.int32, (CH, CH), 1)
    left = lane < 64
    kband = jnp.concatenate([kprev, kcur], axis=0)
    vband = jnp.concatenate([vprev, vcur], axis=0)
    left2 = lax.broadcasted_iota(jnp.int32, kband.shape, 1) < 64
    ksw, vsw = _swap64(kband), _swap64(vband)
    kdup = [jnp.where(left2, kband, ksw), jnp.where(left2, ksw, kband)]
    vdup = [jnp.where(left2, vband, vsw), jnp.where(left2, vsw, vband)]
    qi = lax.broadcasted_iota(jnp.int32, (CH, 2 * CH), 0)
    si = lax.broadcasted_iota(jnp.int32, (CH, 2 * CH), 1)
    rel = qi + CH - si
    valid = (rel >= 0) & (rel < CH) & (jnp.logical_not(first) | (si >= CH))
    outs = []
    for p in range(N_BLK):
        j = p // 4
        halves = []
        for side, h in ((0, 2 * p), (1, 2 * p + 1)):
            qh = jnp.where(left, qps[p], 0.0) if side == 0 else jnp.where(left, 0.0, qps[p])
            s = _mm(qh, kdup[j], NT) * ATT_SCALE
            s = jnp.where(valid, s, -jnp.inf)
            sink = _col(sink_row, lane[:1], h)
            m = lax.stop_gradient(jnp.maximum(jnp.max(s, axis=1, keepdims=True), sink))
            pexp = jnp.exp(s - m)
            denom = jnp.sum(pexp, axis=1, keepdims=True) + jnp.exp(sink - m)
            halves.append(_mm(pexp / denom, vdup[j], NN))
        outs.append(jnp.where(left, halves[0], halves[1]))
    return outs


def _rmsnorm(x, g):
    r = lax.rsqrt(jnp.mean(x * x, axis=-1, keepdims=True) + RMS_EPS)
    return x * r * g


def rmsnorm_fwd(x, g_row, name):
    s, d = x.shape
    tm = min(512, s)

    def body(x_ref, g_ref, y_ref):
        y_ref[...] = _rmsnorm(x_ref[...], g_ref[...]).astype(bf16)

    return pl.pallas_call(
        body, name=name, grid=(s // tm,),
        in_specs=[pl.BlockSpec((tm, d), lambda i: (i, 0)), pl.BlockSpec((1, d), lambda i: (0, 0))],
        out_specs=pl.BlockSpec((tm, d), lambda i: (i, 0)),
        out_shape=jax.ShapeDtypeStruct((s, d), bf16),
        compiler_params=_cparams(("parallel",)),
    )(x, g_row)


def rmsnorm_bwd(x, g_row, dy, res, name):
    s, d = x.shape
    tm = min(512, s)

    def body(x_ref, g_ref, dy_ref, res_ref, dx_ref, dg_ref):
        @pl.when(pl.program_id(0) == 0)
        def _():
            dg_ref[...] = jnp.zeros_like(dg_ref)

        _, vjp = jax.vjp(_rmsnorm, x_ref[...], g_ref[...])
        dx, dg = vjp(dy_ref[...])
        dx_ref[...] = res_ref[...] + dx
        dg_ref[...] += dg

    tile = pl.BlockSpec((tm, d), lambda i: (i, 0))
    row = pl.BlockSpec((1, d), lambda i: (0, 0))
    return pl.pallas_call(
        body, name=name, grid=(s // tm,),
        in_specs=[tile, row, tile, tile], out_specs=[tile, row],
        out_shape=[jax.ShapeDtypeStruct((s, d), f32), jax.ShapeDtypeStruct((1, d), f32)],
        compiler_params=_cparams(("arbitrary",)),
    )(x, g_row, dy, res)


def final_loss(h, g_row, target, name):
    s, d = h.shape
    tm = min(512, s)

    def body(h_ref, g_ref, t_ref, loss_ref, dh_ref, dg_ref):
        @pl.when(pl.program_id(0) == 0)
        def _():
            dg_ref[...] = jnp.zeros_like(dg_ref)
            loss_ref[...] = jnp.zeros_like(loss_ref)

        def f(hv, gv):
            err = jnp.square(_rmsnorm(hv, gv) - t_ref[...])
            return 0.5 * jnp.sum(jnp.mean(err, axis=-1, keepdims=True), axis=0, keepdims=True)

        loss, vjp = jax.vjp(f, h_ref[...], g_ref[...])
        dh, dg = vjp(jnp.ones_like(loss))
        dh_ref[...] = dh
        dg_ref[...] += dg
        loss_ref[...] += jnp.broadcast_to(loss, loss_ref.shape)

    tile = pl.BlockSpec((tm, d), lambda i: (i, 0))
    row = pl.BlockSpec((1, d), lambda i: (0, 0))
    return pl.pallas_call(
        body, name=name, grid=(s // tm,),
        in_specs=[tile, row, tile],
        out_specs=[pl.BlockSpec((1, 128), lambda i: (0, 0)), tile, row],
        out_shape=[jax.ShapeDtypeStruct((1, 128), f32), jax.ShapeDtypeStruct((s, d), f32),
                   jax.ShapeDtypeStruct((1, d), f32)],
        compiler_params=_cparams(("arbitrary",)),
    )(h, g_row, target)


def colsum(x, name):
    s, n = x.shape
    tm = min(512, s)

    def body(x_ref, o_ref):
        @pl.when(pl.program_id(0) == 0)
        def _():
            o_ref[...] = jnp.zeros_like(o_ref)

        o_ref[...] += jnp.sum(x_ref[...].astype(f32), axis=0, keepdims=True)

    return pl.pallas_call(
        body, name=name, grid=(s // tm,),
        in_specs=[pl.BlockSpec((tm, n), lambda i: (i, 0))],
        out_specs=pl.BlockSpec((1, n), lambda i: (0, 0)),
        out_shape=jax.ShapeDtypeStruct((1, n), f32),
        compiler_params=_cparams(("arbitrary",)),
    )(x)


def _fit(dim, want):
    if dim <= want:
        return dim
    t = want
    while dim % t:
        t -= 128
    return t


def matmul(a, b, *, dims, name, out_dtype=f32, tm=1024, tn=512, tk=8192, a_pro=None, epi=None, epi_args=(),
           out_by_col_tile=False):
    if dims == "nn":
        (m, k), n = a.shape, b.shape[1]
    elif dims == "nt":
        (m, k), n = a.shape, b.shape[0]
    else:
        (k, m), n = a.shape, b.shape[1]
    tm, tn, tk = _fit(m, tm), _fit(n, tn), _fit(k, tk)
    nk = k // tk
    if dims == "nn":
        a_spec = pl.BlockSpec((tm, tk), lambda i, j, kk: (i, kk))
        b_spec = pl.BlockSpec((tk, tn), lambda i, j, kk: (kk, j))
        dn = NN
    elif dims == "nt":
        a_spec = pl.BlockSpec((tm, tk), lambda i, j, kk: (i, kk))
        b_spec = pl.BlockSpec((tn, tk), lambda i, j, kk: (j, kk))
        dn = NT
    else:
        a_spec = pl.BlockSpec((tk, tm), lambda i, j, kk: (kk, i))
        b_spec = pl.BlockSpec((tk, tn), lambda i, j, kk: (kk, j))
        dn = TN
    e_specs = [pl.BlockSpec((tm, tn), lambda i, j, kk: (i, j)) if kind == "tile"
               else pl.BlockSpec((1, tn), lambda i, j, kk: (0, j)) for kind, _ in epi_args]
    n_epi = len(epi_args)

    def body(*refs):
        a_ref, b_ref = refs[0], refs[1]
        e_refs = refs[2:2 + n_epi]
        o_ref = refs[2 + n_epi]
        av = a_ref[...]
        if a_pro is not None:
            av = a_pro(av)
        part = _mm(av, b_ref[...], dn)

        def finish(acc):
            if epi is not None:
                acc = epi(acc, *[r[...] for r in e_refs])
            o_ref[...] = acc.astype(out_dtype)

        if nk == 1:
            finish(part)
        else:
            acc_ref = refs[3 + n_epi]
            kk = pl.program_id(2)

            @pl.when(kk == 0)
            def _():
                acc_ref[...] = part

            @pl.when(kk > 0)
            def _():
                acc_ref[...] += part

            @pl.when(kk == nk - 1)
            def _():
                finish(acc_ref[...])

    if out_by_col_tile:
        out_spec = pl.BlockSpec((None, tm, tn), lambda i, j, kk: (j, i, 0))
        out_shape = jax.ShapeDtypeStruct((n // tn, m, tn), out_dtype)
    else:
        out_spec = pl.BlockSpec((tm, tn), lambda i, j, kk: (i, j))
        out_shape = jax.ShapeDtypeStruct((m, n), out_dtype)
    return pl.pallas_call(
        body, name=name, grid=(m // tm, n // tn, nk),
        in_specs=[a_spec, b_spec] + e_specs,
        out_specs=out_spec,
        out_shape=out_shape,
        scratch_shapes=[pltpu.VMEM((tm, tn), f32)] if nk > 1 else [],
        compiler_params=_cparams(("parallel", "parallel", "arbitrary")),
    )(a, b, *[arr for _, arr in epi_args])


def _relu2(a):
    r = jnp.maximum(a.astype(f32), 0.0)
    return r * r


def _add(acc, t):
    return acc + t


def _add_bias(acc, t):
    return acc + t


def _add_bias_res(acc, bias, res):
    return acc + bias + res


def _times_relu2_grad(acc, a):
    return acc * (2.0 * jnp.maximum(a.astype(f32), 0.0))


_MIXER_PARAM_SHAPES = (
    ("ln_g", (1, D_MODEL)), ("ln_b", (1, D_MODEL)), ("wm", (N_BLK, CH, CH)), ("bs_t", (CH, CH)),
    ("conv_w", (8, 2048)), ("conv_b", (1, 2048)), ("dt_bias", (1, CH)), ("a_log", (1, CH)),
    ("d_heads", (1, CH)), ("norm_g", (1, D_MODEL)),
)


def _blocks(v, n, off=0):
    return [v[:, off + i * CH: off + (i + 1) * CH] for i in range(n)]


def _split_mixer_params(vals):
    p = dict(vals)
    return {
        "ln_g": _blocks(p["ln_g"], N_BLK), "ln_b": _blocks(p["ln_b"], N_BLK),
        "wm": [p["wm"][g] for g in range(N_BLK)], "bs_t": p["bs_t"],
        "conv_w": _blocks(p["conv_w"], XBC_BLKS), "conv_b": _blocks(p["conv_b"], XBC_BLKS),
        "dt_bias": p["dt_bias"], "a_log": p["a_log"], "d_heads": p["d_heads"],
        "norm_g": _blocks(p["norm_g"], N_BLK),
    }


def _mixer_leaves(proj_ref, halo_ref, keep_halo):
    pv = proj_ref
    us = [pv[:, OFF_U + i * CH: OFF_U + (i + 1) * CH] for i in range(N_BLK)]
    vs = [pv[:, OFF_V + i * CH: OFF_V + (i + 1) * CH] for i in range(N_BLK)]
    zs = [pv[:, OFF_Z + i * CH: OFF_Z + (i + 1) * CH] for i in range(N_BLK)]
    xbcs = [pv[:, OFF_X + i * CH: OFF_X + (i + 1) * CH] for i in range(XBC_BLKS)]
    halos = [halo_ref[:, OFF_X + i * CH: OFF_X + (i + 1) * CH] * keep_halo for i in range(XBC_BLKS)]
    dtblk = pv[:, OFF_DT: OFF_DT + CH]
    return us, vs, zs, xbcs, halos, dtblk


def mixer_fwd(proj, prm):
    s = proj.shape[0]
    nc = s // CH
    names = [n for n, _ in _MIXER_PARAM_SHAPES]

    def body(proj_ref, halo_ref, *rest):
        p_refs = rest[:len(names)]
        ab_ref, hs_ref, h_ref = rest[len(names):]
        c = pl.program_id(0)

        @pl.when(c == 0)
        def _():
            h_ref[...] = jnp.zeros_like(h_ref)

        hs_ref[...] = h_ref[...]
        keep = (c > 0).astype(f32)
        us, vs, zs, xbcs, halos, dtblk = _mixer_leaves(proj_ref, halo_ref, keep)
        hps = [h_ref[i * CH:(i + 1) * CH, :] for i in range(N_BLK)]
        p = _split_mixer_params({n: r[...] for n, r in zip(names, p_refs)})
        a_out, b_out, h_out = _mixer_chunk(us, vs, zs, xbcs, halos, dtblk, hps, p)
        for i in range(N_BLK):
            ab_ref[:, i * CH:(i + 1) * CH] = a_out[i].astype(bf16)
            ab_ref[:, D_MODEL + i * CH: D_MODEL + (i + 1) * CH] = b_out[i].astype(bf16)
            h_ref[i * CH:(i + 1) * CH, :] = h_out[i]

    def const(shape):
        return pl.BlockSpec(shape, lambda c: (0,) * len(shape))

    return pl.pallas_call(
        body, name="mixer_fwd", grid=(nc,),
        in_specs=[pl.BlockSpec((CH, NP_IN), lambda c: (c, 0)),
                  pl.BlockSpec((8, NP_IN), lambda c: (jnp.maximum(c * (CH // 8) - 1, 0), 0))]
                 + [const(shp) for _, shp in _MIXER_PARAM_SHAPES],
        out_specs=[pl.BlockSpec((CH, 2 * D_MODEL), lambda c: (c, 0)),
                   pl.BlockSpec((None, D_MODEL, CH), lambda c: (c, 0, 0))],
        out_shape=[jax.ShapeDtypeStruct((s, 2 * D_MODEL), bf16), jax.ShapeDtypeStruct((nc, D_MODEL, CH), f32)],
        scratch_shapes=[pltpu.VMEM((D_MODEL, CH), f32)],
        compiler_params=_cparams(("arbitrary",)),
    )(proj, proj, *[prm[n] for n in names])


def mixer_bwd(proj, hstates, dab, prm):
    s = proj.shape[0]
    nc = s // CH
    names = [n for n, _ in _MIXER_PARAM_SHAPES]
    npar = len(names)

    def body(proj_ref, halo_ref, hs_ref, dab_ref, *rest):
        p_refs = rest[:npar]
        dproj_ref = rest[npar]
        g_refs = rest[npar + 1: 2 * npar + 1]
        dh_ref, dhalo_ref = rest[2 * npar + 1:]
        i = pl.program_id(0)
        c = nc - 1 - i

        @pl.when(i == 0)
        def _():
            dh_ref[...] = jnp.zeros_like(dh_ref)
            dhalo_ref[...] = jnp.zeros_like(dhalo_ref)
            for r in g_refs:
                r[...] = jnp.zeros_like(r)

        keep = (c > 0).astype(f32)
        us, vs, zs, xbcs, halos, dtblk = _mixer_leaves(proj_ref, halo_ref, keep)
        hps = [hs_ref[j * CH:(j + 1) * CH, :] for j in range(N_BLK)]
        pvals = {n: r[...] for n, r in zip(names, p_refs)}

        def fn(us, vs, zs, xbcs, halos, dtblk, hps, pvals):
            return _mixer_chunk(us, vs, zs, xbcs, halos, dtblk, hps, _split_mixer_params(pvals))

        _, vjp = jax.vjp(fn, us, vs, zs, xbcs, halos, dtblk, hps, pvals)
        da = [dab_ref[:, j * CH:(j + 1) * CH].astype(f32) for j in range(N_BLK)]
        db = [dab_ref[:, D_MODEL + j * CH: D_MODEL + (j + 1) * CH].astype(f32) for j in range(N_BLK)]
        dh = [dh_ref[j * CH:(j + 1) * CH, :] for j in range(N_BLK)]
        dus, dvs, dzs, dxbcs, dhalos, ddt, dhps, dp = vjp((da, db, dh))

        for j in range(N_BLK):
            dproj_ref[:, OFF_U + j * CH: OFF_U + (j + 1) * CH] = dus[j].astype(bf16)
            dproj_ref[:, OFF_V + j * CH: OFF_V + (j + 1) * CH] = dvs[j].astype(bf16)
            dproj_ref[:, OFF_Z + j * CH: OFF_Z + (j + 1) * CH] = dzs[j].astype(bf16)
            dh_ref[j * CH:(j + 1) * CH, :] = dhps[j]
        zeros_top = jnp.zeros((CH - 8, CH), f32)
        for j in range(XBC_BLKS):
            late = jnp.concatenate([zeros_top, dhalo_ref[:, j * CH:(j + 1) * CH]], axis=0)
            dproj_ref[:, OFF_X + j * CH: OFF_X + (j + 1) * CH] = (dxbcs[j] + late).astype(bf16)
        for j in range(XBC_BLKS):
            dhalo_ref[:, j * CH:(j + 1) * CH] = dhalos[j] * keep
        lane = lax.broadcasted_iota(jnp.int32, (CH, CH), 1)
        dproj_ref[:, OFF_DT: OFF_DT + CH] = jnp.where(lane < SSM_HEADS, ddt, 0.0).astype(bf16)
        dproj_ref[:, OFF_DT + CH:] = jnp.zeros((CH, NP_IN - OFF_DT - CH), bf16)
        for n, r in zip(names, g_refs):
            r[...] += dp[n]

    def const(shape):
        return pl.BlockSpec(shape, lambda i: (0,) * len(shape))

    outs = pl.pallas_call(
        body, name="mixer_bwd", grid=(nc,),
        in_specs=[pl.BlockSpec((CH, NP_IN), lambda i: (nc - 1 - i, 0)),
                  pl.BlockSpec((8, NP_IN), lambda i: (jnp.maximum((nc - 1 - i) * (CH // 8) - 1, 0), 0)),
                  pl.BlockSpec((None, D_MODEL, CH), lambda i: (nc - 1 - i, 0, 0)),
                  pl.BlockSpec((CH, 2 * D_MODEL), lambda i: (nc - 1 - i, 0))]
                 + [const(shp) for _, shp in _MIXER_PARAM_SHAPES],
        out_specs=[pl.BlockSpec((CH, NP_IN), lambda i: (nc - 1 - i, 0))]
                  + [const(shp) for _, shp in _MIXER_PARAM_SHAPES],
        out_shape=[jax.ShapeDtypeStruct((s, NP_IN), bf16)]
                  + [jax.ShapeDtypeStruct(shp, f32) for _, shp in _MIXER_PARAM_SHAPES],
        scratch_shapes=[pltpu.VMEM((D_MODEL, CH), f32), pltpu.VMEM((8, 2048), f32)],
        compiler_params=_cparams(("arbitrary",)),
    )(proj, proj, hstates, dab, *[prm[n] for n in names])
    return outs[0], dict(zip(names, outs[1:]))


_K_BLK = D_MODEL // CH
_V_BLK = _K_BLK + 1


def _attn_specs(rev, nb):
    def blk(i):
        return nb - 1 - i if rev else i

    q_spec = pl.BlockSpec((CH, D_MODEL), lambda i: (blk(i), 0))
    kv = lambda col, prev: pl.BlockSpec(
        (CH, CH), lambda i: (jnp.maximum(blk(i) - 1, 0) if prev else blk(i), col))
    return q_spec, [kv(_K_BLK, True), kv(_K_BLK, False), kv(_V_BLK, True), kv(_V_BLK, False)]


def attn_fwd(qkv, sink_row):
    s = qkv.shape[0]
    nb = s // CH

    def body(q_ref, kp_ref, kc_ref, vp_ref, vc_ref, sink_ref, o_ref):
        qps = [q_ref[:, p * CH:(p + 1) * CH] for p in range(N_BLK)]
        outs = _attn_block(qps, kp_ref[...], kc_ref[...], vp_ref[...], vc_ref[...], sink_ref[...],
                           pl.program_id(0) == 0)
        for p in range(N_BLK):
            o_ref[:, p * CH:(p + 1) * CH] = outs[p].astype(bf16)

    q_spec, kv_specs = _attn_specs(False, nb)
    return pl.pallas_call(
        body, name="attn_fwd", grid=(nb,),
        in_specs=[q_spec] + kv_specs + [pl.BlockSpec((1, CH), lambda i: (0, 0))],
        out_specs=pl.BlockSpec((CH, D_MODEL), lambda i: (i, 0)),
        out_shape=jax.ShapeDtypeStruct((s, D_MODEL), bf16),
        compiler_params=_cparams(("parallel",)),
    )(qkv, qkv, qkv, qkv, qkv, sink_row)


def attn_bwd(qkv, sink_row, dout):
    s = qkv.shape[0]
    nb = s // CH

    def body(q_ref, kp_ref, kc_ref, vp_ref, vc_ref, sink_ref, do_ref, dqkv_ref, dsink_ref, carry_ref):
        i = pl.program_id(0)
        blk = nb - 1 - i

        @pl.when(i == 0)
        def _():
            dsink_ref[...] = jnp.zeros_like(dsink_ref)
            carry_ref[...] = jnp.zeros_like(carry_ref)

        qps = [q_ref[:, p * CH:(p + 1) * CH] for p in range(N_BLK)]
        first = blk == 0
        _, vjp = jax.vjp(lambda *a: _attn_block(*a, first), qps, kp_ref[...], kc_ref[...], vp_ref[...],
                         vc_ref[...], sink_ref[...])
        dos = [do_ref[:, p * CH:(p + 1) * CH].astype(f32) for p in range(N_BLK)]
        dqs, dkp, dkc, dvp, dvc, dsink = vjp(dos)
        for p in range(N_BLK):
            dqkv_ref[:, p * CH:(p + 1) * CH] = dqs[p].astype(bf16)
        dqkv_ref[:, D_MODEL: D_MODEL + CH] = (dkc + carry_ref[0]).astype(bf16)
        dqkv_ref[:, D_MODEL + CH:] = (dvc + carry_ref[1]).astype(bf16)
        keep = jnp.logical_not(first).astype(f32)
        carry_ref[0] = dkp * keep
        carry_ref[1] = dvp * keep
        dsink_ref[...] += dsink

    q_spec, kv_specs = _attn_specs(True, nb)
    return pl.pallas_call(
        body, name="attn_bwd", grid=(nb,),
        in_specs=[q_spec] + kv_specs + [pl.BlockSpec((1, CH), lambda i: (0, 0)),
                                        pl.BlockSpec((CH, D_MODEL), lambda i: (nb - 1 - i, 0))],
        out_specs=[pl.BlockSpec((CH, QKV_DIM), lambda i: (nb - 1 - i, 0)), pl.BlockSpec((1, CH), lambda i: (0, 0))],
        out_shape=[jax.ShapeDtypeStruct((s, QKV_DIM), bf16), jax.ShapeDtypeStruct((1, CH), f32)],
        scratch_shapes=[pltpu.VMEM((2, CH, CH), f32)],
        compiler_params=_cparams(("arbitrary",)),
    )(qkv, qkv, qkv, qkv, qkv, sink_row, dout)


def adamw(w, g, m, v, name):
    def body(w_ref, g_ref, m_ref, v_ref, d_ref, nm_ref, nv_ref):
        gv = g_ref[...]
        nm = ADAM_B1 * m_ref[...] + (1.0 - ADAM_B1) * gv
        nv = ADAM_B2 * v_ref[...] + (1.0 - ADAM_B2) * jnp.square(gv)
        m_hat = nm / (1.0 - ADAM_B1 ** ADAM_STEP)
        v_hat = nv / (1.0 - ADAM_B2 ** ADAM_STEP)
        d_ref[...] = -ADAM_LR * (m_hat / (jnp.sqrt(v_hat) + ADAM_EPS) + ADAM_WD * w_ref[...])
        nm_ref[...] = nm
        nv_ref[...] = nv

    out_shape = [jax.ShapeDtypeStruct(w.shape, f32)] * 3
    if w.ndim == 3 and w.shape[1] % 256 == 0:
        tile = pl.BlockSpec((None, 256, w.shape[2]), lambda l, i: (l, i, 0))
        return pl.pallas_call(
            body, name=name, grid=(w.shape[0], w.shape[1] // 256),
            in_specs=[tile] * 4, out_specs=[tile] * 3, out_shape=out_shape,
            compiler_params=_cparams(("parallel", "parallel")),
        )(w, g, m, v)
    return pl.pallas_call(body, name=name, in_specs=[_VMEM] * 4, out_specs=[_VMEM] * 3, out_shape=out_shape,
                          compiler_params=_cparams())(w, g, m, v)


_MESH = pl.DeviceIdType.MESH
_ANY = pl.BlockSpec(memory_space=pl.ANY)
_VMEM = pl.BlockSpec(memory_space=pltpu.VMEM)


def _place():
    x, y, c = lax.axis_index("x"), lax.axis_index("y"), lax.axis_index("c")
    chips = [(1 - x, y), (x, 1 - y), (1 - x, 1 - y)]
    return x, y, c, 2 * x + y, chips, [2 * cx + cy for cx, cy in chips]


def _half(c, rows):
    return pl.ds(pl.multiple_of(c * (rows // 2), 16), rows // 2)


def _step_rows(rows):
    return max(t for t in range(16, 641, 16) if rows % t == 0)


def place_shard(b, slot, name):
    r, c = b.shape
    tr = _step_rows(r)

    def body(slot_ref, b_ref, o_ref):
        o_ref[...] = b_ref[...]

    return pl.pallas_call(
        body, name=name,
        grid_spec=pltpu.PrefetchScalarGridSpec(
            num_scalar_prefetch=1, grid=(r // tr,),
            in_specs=[pl.BlockSpec((tr, c), lambda i, s: (i, 0))],
            out_specs=pl.BlockSpec((None, tr, c), lambda i, s: (s[0], i, 0))),
        out_shape=jax.ShapeDtypeStruct((N_CHIPS, r, c), b.dtype),
        compiler_params=_cparams(("parallel",)),
    )(slot, b)


def allgather_weights(bufs):
    n = len(bufs)

    def body(*refs):
        out_refs = refs[n:2 * n]
        send_sems, recv_sems = refs[2 * n:]
        x, y, c, me, chips, chip_idx = _place()
        sibling = (x, y, 1 - c)

        def copy(b, k, shard, half, to):
            ref = out_refs[b]
            part = ref.at[shard, _half(half, ref.shape[1])]
            return pltpu.make_async_remote_copy(
                src_ref=part, dst_ref=part, send_sem=send_sems.at[6 * b + k], recv_sem=recv_sems.at[6 * b + k],
                device_id=to, device_id_type=_MESH)

        first = [copy(b, j, me, c, (*chips[j], c)) for j in range(3) for b in range(n)]
        for cp in first:
            cp.start()
        passed = []
        for j in range(3):
            for b in range(n):
                copy(b, j, chip_idx[j], c, sibling).wait_recv()
                passed.append(copy(b, 3 + j, chip_idx[j], c, sibling))
                passed[-1].start()
        for j in range(3):
            for b in range(n):
                copy(b, 3 + j, chip_idx[j], 1 - c, sibling).wait_recv()
        for cp in first + passed:
            cp.wait_send()

    return pl.pallas_call(
        body, name="allgather_weights",
        out_shape=[jax.ShapeDtypeStruct(b.shape, b.dtype) for b in bufs],
        in_specs=[_ANY] * n, out_specs=[_ANY] * n, input_output_aliases={i: i for i in range(n)},
        scratch_shapes=[pltpu.SemaphoreType.DMA((6 * n,)), pltpu.SemaphoreType.DMA((6 * n,))],
    )(*bufs)


def exchange_halves(bufs):
    n = len(bufs)

    def body(*refs):
        g_refs, out_refs = refs[:n], refs[n:2 * n]
        send_sems, recv_sems = refs[2 * n:]
        x, y, c, *_ = _place()
        cps = [pltpu.make_async_remote_copy(
            src_ref=g_refs[b].at[:, _half(1 - c, g_refs[b].shape[1])], dst_ref=out_refs[b],
            send_sem=send_sems.at[b], recv_sem=recv_sems.at[b], device_id=(x, y, 1 - c), device_id_type=_MESH)
            for b in range(n)]
        for cp in cps:
            cp.start()
        for cp in cps:
            cp.wait()

    return pl.pallas_call(
        body, name="exchange_halves",
        out_shape=[jax.ShapeDtypeStruct((N_CHIPS, b.shape[1] // 2, b.shape[2]), b.dtype) for b in bufs],
        in_specs=[_ANY] * n, out_specs=[_ANY] * n,
        scratch_shapes=[pltpu.SemaphoreType.DMA((n,)), pltpu.SemaphoreType.DMA((n,))],
    )(*bufs)


def add_halves(g, got, c_idx, name):
    hr, cols = got.shape[1], got.shape[2]
    tr = _step_rows(hr)
    steps = hr // tr

    def body(c_ref, g_ref, got_ref, o_ref):
        o_ref[...] = (g_ref[...].astype(f32) + got_ref[...].astype(f32)).astype(bf16)

    return pl.pallas_call(
        body, name=name,
        grid_spec=pltpu.PrefetchScalarGridSpec(
            num_scalar_prefetch=1, grid=(N_CHIPS, steps),
            in_specs=[pl.BlockSpec((None, tr, cols), lambda s, i, c: (s, c[0] * steps + i, 0)),
                      pl.BlockSpec((None, tr, cols), lambda s, i, c: (s, i, 0))],
            out_specs=pl.BlockSpec((None, tr, cols), lambda s, i, c: (s, i, 0))),
        out_shape=jax.ShapeDtypeStruct(got.shape, bf16),
        compiler_params=_cparams(("parallel", "parallel")),
    )(c_idx, g, got)


def scatter_chip_sums(bufs):
    n = len(bufs)

    def body(*refs):
        t_refs, out_refs = refs[:n], refs[n:2 * n]
        send_sems, recv_sems = refs[2 * n:]
        x, y, c, me, chips, chip_idx = _place()
        cps = [pltpu.make_async_remote_copy(
            src_ref=t_refs[b].at[chip_idx[j]], dst_ref=out_refs[b].at[j], send_sem=send_sems.at[3 * b + j],
            recv_sem=recv_sems.at[3 * b + j], device_id=(*chips[j], c), device_id_type=_MESH)
            for j in range(3) for b in range(n)]
        for cp in cps:
            cp.start()
        for cp in cps:
            cp.wait_recv()
        for cp in cps:
            cp.wait_send()

    return pl.pallas_call(
        body, name="scatter_chip_sums",
        out_shape=[jax.ShapeDtypeStruct((3,) + b.shape[1:], b.dtype) for b in bufs],
        in_specs=[_ANY] * n, out_specs=[_ANY] * n,
        scratch_shapes=[pltpu.SemaphoreType.DMA((3 * n,)), pltpu.SemaphoreType.DMA((3 * n,))],
    )(*bufs)


def sum_chips(t, got, place_idx, name):
    hr, cols = t.shape[1], t.shape[2]
    tr = _step_rows(hr)
    steps = hr // tr

    def body(idx_ref, t_ref, got_ref, o_ref):
        acc = t_ref[...].astype(f32)
        for j in range(3):
            acc = acc + got_ref[j].astype(f32)
        o_ref[...] = acc

    return pl.pallas_call(
        body, name=name,
        grid_spec=pltpu.PrefetchScalarGridSpec(
            num_scalar_prefetch=1, grid=(steps,),
            in_specs=[pl.BlockSpec((None, tr, cols), lambda i, idx: (idx[0], i, 0)),
                      pl.BlockSpec((3, tr, cols), lambda i, idx: (0, i, 0))],
            out_specs=pl.BlockSpec((tr, cols), lambda i, idx: (idx[1] * steps + i, 0))),
        out_shape=jax.ShapeDtypeStruct((2 * hr, cols), f32),
        compiler_params=_cparams(("parallel",)),
    )(place_idx, t, got)


def share_halves(bufs):
    n = len(bufs)

    def body(*refs):
        out_refs = refs[n:2 * n]
        send_sems, recv_sems = refs[2 * n:]
        x, y, c, *_ = _place()

        def copy(b, half):
            part = out_refs[b].at[_half(half, out_refs[b].shape[0])]
            return pltpu.make_async_remote_copy(
                src_ref=part, dst_ref=part, send_sem=send_sems.at[b], recv_sem=recv_sems.at[b],
                device_id=(x, y, 1 - c), device_id_type=_MESH)

        for b in range(n):
            copy(b, c).start()
        for b in range(n):
            copy(b, 1 - c).wait_recv()
        for b in range(n):
            copy(b, c).wait_send()

    return pl.pallas_call(
        body, name="share_halves",
        out_shape=[jax.ShapeDtypeStruct(b.shape, b.dtype) for b in bufs],
        in_specs=[_ANY] * n, out_specs=[_ANY] * n, input_output_aliases={i: i for i in range(n)},
        scratch_shapes=[pltpu.SemaphoreType.DMA((n,)), pltpu.SemaphoreType.DMA((n,))],
    )(*bufs)


def allreduce_small(sp):
    def body(s_ref, out_ref, gather_ref, send_sems, recv_sems):
        x, y, c, me, chips, chip_idx = _place()
        sibling = (x, y, 1 - c)

        def copy(k, chip, core, to, src=None):
            dst = gather_ref.at[2 * chip + core]
            return pltpu.make_async_remote_copy(
                src_ref=dst if src is None else src, dst_ref=dst, send_sem=send_sems.at[k],
                recv_sem=recv_sems.at[k], device_id=to, device_id_type=_MESH)

        first = [copy(0, me, c, sibling, src=s_ref)]
        first += [copy(1 + j, me, c, (*chips[j], c), src=s_ref) for j in range(3)]
        for cp in first:
            cp.start()
        gather_ref[2 * me + c] = s_ref[...]
        passed = [copy(4 + j, chip_idx[j], c, sibling) for j in range(3)]
        for j in range(3):
            copy(1 + j, chip_idx[j], c, sibling).wait_recv()
            passed[j].start()
        copy(0, me, 1 - c, sibling).wait_recv()
        for j in range(3):
            copy(4 + j, chip_idx[j], 1 - c, sibling).wait_recv()
        for cp in first + passed:
            cp.wait_send()
        acc = gather_ref[0]
        for d in range(1, 2 * N_CHIPS):
            acc = acc + gather_ref[d]
        out_ref[...] = acc

    return pl.pallas_call(
        body, name="allreduce_small",
        out_shape=jax.ShapeDtypeStruct(sp.shape, sp.dtype),
        in_specs=[_VMEM], out_specs=_VMEM,
        scratch_shapes=[pltpu.VMEM((2 * N_CHIPS,) + sp.shape, sp.dtype),
                        pltpu.SemaphoreType.DMA((7,)), pltpu.SemaphoreType.DMA((7,))],
        compiler_params=_cparams(),
    )(sp)


def _n_rows(shape):
    n = 1
    for d in shape:
        n *= d
    return 8 * (-(-n // 8192))


def _pack(arrays, total_rows):
    parts = []
    for a in arrays:
        flat = a.reshape(-1)
        parts.append(jnp.pad(flat, (0, 1024 * _n_rows(a.shape) - flat.shape[0])).reshape(-1, 1024))
    rows = jnp.concatenate(parts, axis=0)
    return jnp.pad(rows, ((0, total_rows - rows.shape[0]), (0, 0)))


def _unpack(packed, shapes):
    out, r = [], 0
    for shp in shapes:
        n = 1
        for d in shp:
            n *= d
        nr = _n_rows(shp)
        out.append(packed[r:r + nr].reshape(-1)[:n].reshape(shp))
        r += nr
    return out


_BIG_NAMES = ("w_in_even", "w_out_even", "w_qkv", "w_o", "w_up", "w_down")
B1_ROWS = 4864
_B1_UP = 768
_B1_DOWN = 2816
IN_SHARD, IN_PAD = 1284, 1408
QKV_SHARD, QKV_PAD = 320, 384
B2_COLS = IN_PAD + QKV_PAD


def _lane_padded(a, cols):
    return jnp.pad(a, ((0, 0), (0, cols - a.shape[1])))

_SMALL_SHAPES = (
    ("norm_mix_g", (2, 1024)), ("norm_mlp_g", (2, 1024)), ("final_norm_g", (1024,)), ("gm_ln_g", (1, 1024)),
    ("gm_ln_b", (1, 1024)), ("gm_w_s", (1, 8, 128, 128)), ("gm_b_s", (1, 8, 128)), ("ssm_conv_b", (1, 2048)),
    ("ssm_dt_bias", (1, 16)), ("ssm_a_log", (1, 16)), ("ssm_d", (1, 16)), ("ssm_norm_g", (1, 1024)),
    ("attn_sinks", (1, 16)), ("ssm_conv_w", (1, 4, 2048)), ("b_qkv", (1, 1280)), ("b_o", (1, 1024)),
)
_N_REPLICATED = 13
_SHARDED_SMALL = (("ssm_conv_w", 2, 512), ("b_qkv", 1, 320), ("b_o", 1, 256))
_GATHER_ROWS = 24
_SHARD_PACK_ROWS = 24


def _weight_shards_to_buffers(w):
    b1 = jnp.concatenate([w["w_out_even"][0], w["w_o"][0], w["w_up"][0], w["w_up"][1], w["w_down"][0],
                          w["w_down"][1]], axis=0).astype(bf16)
    b2 = jnp.concatenate([_lane_padded(w["w_in_even"][0], IN_PAD), _lane_padded(w["w_qkv"][0], QKV_PAD)],
                         axis=1).astype(bf16)
    return b1, b2


def _full_big_grads_to_buffers(dw_in_p, dw_out, dw_qkv, dw_o, dw_up, dw_down):
    by_owner = lambda a: a.reshape(N_CHIPS, a.shape[0] // N_CHIPS, a.shape[1])
    x1 = jnp.concatenate([by_owner(dw_out), by_owner(dw_o), dw_up[0], dw_up[1], by_owner(dw_down[0]),
                          by_owner(dw_down[1])], axis=1)
    x2 = jnp.stack([jnp.concatenate(
        [_lane_padded(dw_in_p[:, IN_SHARD * s: IN_SHARD * (s + 1)], IN_PAD),
         _lane_padded(dw_qkv[:, QKV_SHARD * s: QKV_SHARD * (s + 1)], QKV_PAD)], axis=1) for s in range(N_CHIPS)])
    return x1, x2


def _reduced_buffers_to_grads(r1, r2):
    return {
        "w_out_even": r1[None, :512], "w_o": r1[None, 512:_B1_UP],
        "w_up": r1[_B1_UP:_B1_DOWN].reshape(2, 1024, 1024), "w_down": r1[_B1_DOWN:].reshape(2, 1024, 1024),
        "w_in_even": r2[None, :, :IN_SHARD], "w_qkv": r2[None, :, IN_PAD:IN_PAD + QKV_SHARD],
    }


def _gathered_to_full_weights(g1, g2):
    cols_by_owner = lambda a: a.transpose(1, 0, 2).reshape(a.shape[1], -1)
    w_in_p = _lane_padded(cols_by_owner(g2[:, :, :IN_SHARD]), NP_IN)
    w_qkv = cols_by_owner(g2[:, :, IN_PAD:IN_PAD + QKV_SHARD])
    w_out = g1[:, :512].reshape(2048, 1024)
    w_o = g1[:, 512:_B1_UP].reshape(1024, 1024)
    w_up = [cols_by_owner(g1[:, _B1_UP + 1024 * l: _B1_UP + 1024 * (l + 1)]) for l in range(2)]
    w_down = [g1[:, _B1_DOWN + 1024 * l: _B1_DOWN + 1024 * (l + 1)].reshape(4096, 1024) for l in range(2)]
    return w_in_p, w_out, w_qkv, w_o, w_up, w_down


def _row2(v):
    return v.reshape(1, -1)


def _lane_pad(v):
    return jnp.pad(v, ((0, 0), (0, CH - v.shape[1])))


def _mlp_fwd(h, g_row, w_up, w_down, tag):
    y = rmsnorm_fwd(h, g_row, f"mlp_norm{tag}")
    a = matmul(y, w_up, dims="nn", name=f"mlp_up{tag}", out_dtype=bf16, tn=1024)
    out = matmul(a, w_down, dims="nn", name=f"mlp_down{tag}", a_pro=_relu2, epi=_add, epi_args=(("tile", h),))
    return out, y, a


def _mlp_bwd(dh_out, h, g_row, y, a, w_up, w_down, tag):
    da = matmul(dh_out, w_down, dims="nt", name=f"mlp_da{tag}", out_dtype=bf16, tn=1024,
                epi=_times_relu2_grad, epi_args=(("tile", a),))
    dw_down = matmul(a, dh_out, dims="tn", name=f"mlp_dwdown{tag}", out_dtype=bf16, a_pro=_relu2)
    dw_up = matmul(y, da, dims="tn", name=f"mlp_dwup{tag}", out_dtype=bf16, tn=1024, out_by_col_tile=True)
    dy = matmul(da, w_up, dims="nt", name=f"mlp_dy{tag}")
    dh, dg = rmsnorm_bwd(h, g_row, dy, dh_out, f"mlp_dnorm{tag}")
    return dh, dg, dw_up, dw_down


def _local_step(x, target, big, sm):
    w_in_p, w_out, w_qkv, w_o, w_up, w_down = big
    mix_g = [_row2(sm["norm_mix_g"][i]) for i in range(2)]
    mlp_g = [_row2(sm["norm_mlp_g"][i]) for i in range(2)]
    mixer_prm = {
        "ln_g": sm["gm_ln_g"], "ln_b": sm["gm_ln_b"], "wm": sm["gm_w_s"][0],
        "bs_t": jnp.pad(sm["gm_b_s"][0].T, ((0, 0), (0, CH - N_BLK))),
        "conv_w": jnp.pad(sm["ssm_conv_w"][0], ((0, 4), (0, 0))), "conv_b": sm["ssm_conv_b"],
        "dt_bias": _lane_pad(sm["ssm_dt_bias"]), "a_log": _lane_pad(sm["ssm_a_log"]),
        "d_heads": _lane_pad(sm["ssm_d"]), "norm_g": sm["ssm_norm_g"],
    }
    sink_row = _lane_pad(sm["attn_sinks"])

    y0 = rmsnorm_fwd(x, mix_g[0], "mix_norm0")
    proj = matmul(y0, w_in_p, dims="nn", name="in_proj", tn=768)
    ab, hstates = mixer_fwd(proj, mixer_prm)
    h1 = matmul(ab, w_out, dims="nn", name="out_proj", epi=_add, epi_args=(("tile", x),))
    h2, y1, a1 = _mlp_fwd(h1, mlp_g[0], w_up[0], w_down[0], 0)
    y2 = rmsnorm_fwd(h2, mix_g[1], "mix_norm1")
    qkv = matmul(y2, w_qkv, dims="nn", name="qkv_proj", tn=QKV_DIM, epi=_add_bias, epi_args=(("row", sm["b_qkv"]),))
    att = attn_fwd(qkv, sink_row)
    h3 = matmul(att, w_o, dims="nn", name="o_proj", epi=_add_bias_res,
                epi_args=(("row", sm["b_o"]), ("tile", h2)))
    h4, y3, a3 = _mlp_fwd(h3, mlp_g[1], w_up[1], w_down[1], 1)
    loss, dh4, dg_final = final_loss(h4, _row2(sm["final_norm_g"]), target, "final_loss")

    dh3, dg_mlp1, dw_up1, dw_down1 = _mlp_bwd(dh4, h3, mlp_g[1], y3, a3, w_up[1], w_down[1], 1)
    db_o = colsum(dh3, "db_o")
    datt = matmul(dh3, w_o, dims="nt", name="attn_dout", out_dtype=bf16)
    dw_o = matmul(att, dh3, dims="tn", name="dw_o", out_dtype=bf16)
    dqkv, dsink = attn_bwd(qkv, sink_row, datt)
    db_qkv = colsum(dqkv, "db_qkv")
    dw_qkv = matmul(y2, dqkv, dims="tn", name="dw_qkv", out_dtype=bf16, tn=QKV_DIM)
    dy2 = matmul(dqkv, w_qkv, dims="nt", name="dy_qkv", tk=QKV_DIM)
    dh2, dg_mix1 = rmsnorm_bwd(h2, mix_g[1], dy2, dh3, "mix_dnorm1")
    dh1, dg_mlp0, dw_up0, dw_down0 = _mlp_bwd(dh2, h1, mlp_g[0], y1, a1, w_up[0], w_down[0], 0)
    dab = matmul(dh1, w_out, dims="nt", name="mixer_dout", tn=1024)
    dw_out = matmul(ab, dh1, dims="tn", name="dw_out", out_dtype=bf16)
    dproj, dmix = mixer_bwd(proj, hstates, dab, mixer_prm)
    dw_in_p = matmul(y0, dproj, dims="tn", name="dw_in", out_dtype=bf16, tn=768)
    dy0 = matmul(dproj, w_in_p, dims="nt", name="dy_in")
    dx, dg_mix0 = rmsnorm_bwd(x, mix_g[0], dy0, dh1, "mix_dnorm0")

    small_grads = {
        "norm_mix_g": jnp.concatenate([dg_mix0, dg_mix1], axis=0),
        "norm_mlp_g": jnp.concatenate([dg_mlp0, dg_mlp1], axis=0),
        "final_norm_g": dg_final[0], "gm_ln_g": dmix["ln_g"], "gm_ln_b": dmix["ln_b"],
        "gm_w_s": dmix["wm"][None], "gm_b_s": dmix["bs_t"][:, :N_BLK].T[None],
        "ssm_conv_b": dmix["conv_b"], "ssm_dt_bias": dmix["dt_bias"][:, :SSM_HEADS],
        "ssm_a_log": dmix["a_log"][:, :SSM_HEADS], "ssm_d": dmix["d_heads"][:, :SSM_HEADS],
        "ssm_norm_g": dmix["norm_g"], "attn_sinks": dsink[:, :SSM_HEADS],
        "ssm_conv_w": dmix["conv_w"][None, :4], "b_qkv": db_qkv, "b_o": db_o,
    }
    big_grads = (dw_in_p, dw_out, dw_qkv, dw_o, (dw_up0, dw_up1), (dw_down0, dw_down1))
    return loss, dx, big_grads, small_grads


def kernel(x, norm_mix_g, norm_mlp_g, final_norm_g, w_in_even, w_out_even, gm_ln_g, gm_ln_b, gm_w_s, gm_b_s, ssm_conv_w, ssm_conv_b, ssm_dt_bias, ssm_a_log, ssm_d, ssm_norm_g, w_qkv, b_qkv, w_o, b_o, attn_sinks, w_up, w_down, loss_target, m_norm_mix_g, m_norm_mlp_g, m_final_norm_g, m_w_in_even, m_w_out_even, m_gm_ln_g, m_gm_ln_b, m_gm_w_s, m_gm_b_s, m_ssm_conv_w, m_ssm_conv_b, m_ssm_dt_bias, m_ssm_a_log, m_ssm_d, m_ssm_norm_g, m_w_qkv, m_b_qkv, m_w_o, m_b_o, m_attn_sinks, m_w_up, m_w_down, v_norm_mix_g, v_norm_mlp_g, v_final_norm_g, v_w_in_even, v_w_out_even, v_gm_ln_g, v_gm_ln_b, v_gm_w_s, v_gm_b_s, v_ssm_conv_w, v_ssm_conv_b, v_ssm_dt_bias, v_ssm_a_log, v_ssm_d, v_ssm_norm_g, v_w_qkv, v_b_qkv, v_w_o, v_b_o, v_attn_sinks, v_w_up, v_w_down):
    w = dict(norm_mix_g=norm_mix_g, norm_mlp_g=norm_mlp_g, final_norm_g=final_norm_g, w_in_even=w_in_even,
             w_out_even=w_out_even, gm_ln_g=gm_ln_g, gm_ln_b=gm_ln_b, gm_w_s=gm_w_s, gm_b_s=gm_b_s,
             ssm_conv_w=ssm_conv_w, ssm_conv_b=ssm_conv_b, ssm_dt_bias=ssm_dt_bias, ssm_a_log=ssm_a_log,
             ssm_d=ssm_d, ssm_norm_g=ssm_norm_g, w_qkv=w_qkv, b_qkv=b_qkv, w_o=w_o, b_o=b_o,
             attn_sinks=attn_sinks, w_up=w_up, w_down=w_down)
    m = dict(norm_mix_g=m_norm_mix_g, norm_mlp_g=m_norm_mlp_g, final_norm_g=m_final_norm_g,
             w_in_even=m_w_in_even, w_out_even=m_w_out_even, gm_ln_g=m_gm_ln_g, gm_ln_b=m_gm_ln_b,
             gm_w_s=m_gm_w_s, gm_b_s=m_gm_b_s, ssm_conv_w=m_ssm_conv_w, ssm_conv_b=m_ssm_conv_b,
             ssm_dt_bias=m_ssm_dt_bias, ssm_a_log=m_ssm_a_log, ssm_d=m_ssm_d, ssm_norm_g=m_ssm_norm_g,
             w_qkv=m_w_qkv, b_qkv=m_b_qkv, w_o=m_w_o, b_o=m_b_o, attn_sinks=m_attn_sinks, w_up=m_w_up,
             w_down=m_w_down)
    v = dict(norm_mix_g=v_norm_mix_g, norm_mlp_g=v_norm_mlp_g, final_norm_g=v_final_norm_g,
             w_in_even=v_w_in_even, w_out_even=v_w_out_even, gm_ln_g=v_gm_ln_g, gm_ln_b=v_gm_ln_b,
             gm_w_s=v_gm_w_s, gm_b_s=v_gm_b_s, ssm_conv_w=v_ssm_conv_w, ssm_conv_b=v_ssm_conv_b,
             ssm_dt_bias=v_ssm_dt_bias, ssm_a_log=v_ssm_a_log, ssm_d=v_ssm_d, ssm_norm_g=v_ssm_norm_g,
             w_qkv=v_w_qkv, b_qkv=v_b_qkv, w_o=v_w_o, b_o=v_b_o, attn_sinks=v_attn_sinks, w_up=v_w_up,
             w_down=v_w_down)
    names = ("norm_mix_g", "norm_mlp_g", "final_norm_g", "w_in_even", "w_out_even", "gm_ln_g", "gm_ln_b",
             "gm_w_s", "gm_b_s", "ssm_conv_w", "ssm_conv_b", "ssm_dt_bias", "ssm_a_log", "ssm_d", "ssm_norm_g",
             "w_qkv", "b_qkv", "w_o", "b_o", "attn_sinks", "w_up", "w_down")

    cx, cy, cc = lax.axis_index("x"), lax.axis_index("y"), lax.axis_index("c")
    chip = 2 * cx + cy
    c_idx = jnp.reshape(cc, (1,)).astype(jnp.int32)
    chip_idx = jnp.reshape(chip, (1,)).astype(jnp.int32)

    b1, b2 = _weight_shards_to_buffers(w)
    big = _gathered_to_full_weights(*allgather_weights(
        [place_shard(b1, chip_idx, "place_shard1"), place_shard(b2, chip_idx, "place_shard2")]))

    small_shard_placed = []
    for name, axis, width in _SHARDED_SMALL:
        full_shape = dict(_SMALL_SHAPES)[name]
        placed = lax.dynamic_update_slice_in_dim(jnp.zeros(full_shape, f32), w[name], chip * width, axis)
        small_shard_placed.append(jnp.where(cc == 0, placed, 0.0))
    gathered_small = _unpack(allreduce_small(_pack(small_shard_placed, _GATHER_ROWS)),
                             [dict(_SMALL_SHAPES)[n] for n, _, _ in _SHARDED_SMALL])
    sm = {n: w[n] for n, _ in _SMALL_SHAPES[:_N_REPLICATED]}
    sm.update({n: g for (n, _, _), g in zip(_SHARDED_SMALL, gathered_small)})

    loss_part, dx, big_grads, small_grads = _local_step(x[0], loss_target[0], big, sm)
    loss = lax.psum(loss_part[0, 0], ("x", "y", "c"))

    xs = _full_big_grads_to_buffers(*big_grads)
    ts = [add_halves(xb, got, c_idx, f"add_halves{i}") for i, (xb, got) in enumerate(zip(xs, exchange_halves(xs)))]
    place_idx = jnp.concatenate([chip_idx, c_idx])
    sums = [sum_chips(tb, got, place_idx, f"sum_chips{i}") for i, (tb, got) in enumerate(zip(ts, scatter_chip_sums(ts)))]
    grads = _reduced_buffers_to_grads(*share_halves(sums))
    small_sum = allreduce_small(_pack([small_grads[n] for n, _ in _SMALL_SHAPES], SMALL_ROWS))
    small_full = dict(zip([n for n, _ in _SMALL_SHAPES], _unpack(small_sum, [s for _, s in _SMALL_SHAPES])))
    for n, _ in _SMALL_SHAPES[:_N_REPLICATED]:
        grads[n] = small_full[n]
    for n, axis, width in _SHARDED_SMALL:
        grads[n] = lax.dynamic_slice_in_dim(small_full[n], chip * width, width, axis)
    grads = {n: grads[n].reshape(w[n].shape) for n in names}

    delta, new_m, new_v = {}, {}, {}
    for n in names:
        shape = (1,) + w[n].shape if w[n].ndim == 1 else w[n].shape
        outs = adamw(*[d[n].reshape(shape) for d in (w, grads, m, v)], f"adamw_{n}")
        delta[n], new_m[n], new_v[n] = (o.reshape(w[n].shape) for o in outs)

    return (loss, dx[None], *[grads[n] for n in names], *[delta[n] for n in names],
            *[new_m[n] for n in names], *[new_v[n] for n in names])
```

```python
import functools

import jax
import jax.numpy as jnp
from jax import lax
from jax.experimental import pallas as pl
from jax.experimental.pallas import tpu as pltpu

f32 = jnp.float32
bf16 = jnp.bfloat16
MXU_DTYPE = bf16

RMS_EPS = 1e-5
LN_EPS = 1e-5
D_MODEL = 1024
D_FF = 4096
CH = 128
N_BLK = 8
SSM_HEADS = 16
IN_EVEN = 5136
NP_IN = 5376
OFF_U, OFF_V, OFF_Z, OFF_X, OFF_DT = 0, 1024, 2048, 3072, 5120
XBC_BLKS = 16
QKV_DIM = 1280
ATT_SCALE = 64 ** -0.5

ADAM_LR = 0.001
ADAM_B1 = 0.9
ADAM_B2 = 0.999
ADAM_EPS = 1e-08
ADAM_WD = 0.01
ADAM_STEP = 10

VMEM_LIMIT_BYTES = 48 * 1024 * 1024
N_CHIPS = 4
SMALL_ROWS = 256

NN = ((1,), (0,))
NT = ((1,), (1,))
TN = ((0,), (0,))


def _mm(a, b, dims):
    return lax.dot_general(a.astype(MXU_DTYPE), b.astype(MXU_DTYPE), (dims, ((), ())),
                           preferred_element_type=f32)


def _mm_exact(a, b):
    return jnp.dot(a, b, preferred_element_type=f32, precision=lax.Precision.HIGHEST)


def _cparams(sem=None):
    return pltpu.CompilerParams(dimension_semantics=sem, vmem_limit_bytes=VMEM_LIMIT_BYTES)


@jax.custom_vjp
def _swap64(x):
    return pltpu.roll(x, 64, axis=1)


_swap64.defvjp(lambda x: (pltpu.roll(x, 64, axis=1), None), lambda _, g: (pltpu.roll(g, 64, axis=1),))


def _make_delay(k):
    @jax.custom_vjp
    def delay(ext):
        return pltpu.roll(ext, k, axis=0)[8:, :]

    def fwd(ext):
        return delay(ext), None

    def bwd(_, g):
        gp = jnp.concatenate([jnp.zeros((8, g.shape[1]), g.dtype), g], axis=0)
        return (pltpu.roll(gp, gp.shape[0] - k, axis=0),)

    delay.defvjp(fwd, bwd)
    return delay


_DELAYS = {k: _make_delay(k) for k in (1, 2, 3)}


def _col(m, lane, h):
    return jnp.sum(jnp.where(lane == h, m, 0.0), axis=1, keepdims=True)


def _row(m, sub, h):
    return jnp.sum(jnp.where(sub == h, m, 0.0), axis=0, keepdims=True)


def _mixer_chunk(us, vs, zs, xbcs, halos, dtblk, hps, prm):
    lane = lax.broadcasted_iota(jnp.int32, (CH, CH), 1)
    sub = lax.broadcasted_iota(jnp.int32, (CH, CH), 0)
    left = lane < 64
    top = sub < 64
    causal = sub >= lane

    gus = [jax.nn.gelu(u) for u in us]
    gvs = [jax.nn.gelu(v) for v in vs]
    mu = sum(jnp.sum(g, axis=1, keepdims=True) for g in gvs) / D_MODEL
    cen = [g - mu for g in gvs]
    var = sum(jnp.sum(c * c, axis=1, keepdims=True) for c in cen) / D_MODEL
    rstd = lax.rsqrt(var + LN_EPS)
    a_out = []
    for g in range(N_BLK):
        vn = cen[g] * rstd * prm["ln_g"][g] + prm["ln_b"][g]
        w = jnp.where(causal, prm["wm"][g], 0.0)
        mixed = _mm(w, vn, NN) + _col(prm["bs_t"], lane, g)
        a_out.append(gus[g] * mixed)

    act = []
    for b in range(XBC_BLKS):
        w8 = prm["conv_w"][b]
        sub8 = lax.broadcasted_iota(jnp.int32, w8.shape, 0)
        ext = jnp.concatenate([halos[b], xbcs[b]], axis=0)
        conv = xbcs[b] * _row(w8, sub8, 3) + prm["conv_b"][b]
        for k in (1, 2, 3):
            conv = conv + _DELAYS[k](ext) * _row(w8, sub8, 3 - k)
        act.append(jax.nn.silu(conv))

    dt = jax.nn.softplus(dtblk + prm["dt_bias"])
    a_neg = -jnp.exp(prm["a_log"])
    tri = causal.astype(f32)
    acum = _mm_exact(tri, dt * a_neg)
    acum_t = acum.T
    dt_t = dt.T
    last = sub == CH - 1
    ys, h_out = [], []
    for grp in range(4):
        bm = act[8 + grp]
        cm = act[12 + grp]
        cb = _mm(cm, bm, NT)
        for p in (2 * grp, 2 * grp + 1):
            h0, h1 = 2 * p, 2 * p + 1
            xp = act[p]
            hp = hps[p]
            wis = []
            for h in (h0, h1):
                seg = _col(acum, lane, h) - _row(acum_t, sub, h)
                decay = jnp.exp(jnp.where(causal, seg, -jnp.inf))
                wis.append(cb * decay * _row(dt_t, sub, h))
            wcat = jnp.concatenate(wis, axis=1)
            xbd = jnp.concatenate([jnp.where(left, xp, 0.0), jnp.where(left, 0.0, xp)], axis=0)
            y_diag = _mm(wcat, xbd, NN)
            a_end = [jnp.sum(jnp.where(last & (lane == h), acum, 0.0), keepdims=True) for h in (h0, h1)]
            a_col = jnp.where(left, _col(acum, lane, h0), _col(acum, lane, h1))
            dt_col = jnp.where(left, _col(dt, lane, h0), _col(dt, lane, h1))
            to_end = jnp.exp(jnp.where(left, a_end[0], a_end[1]) - a_col) * dt_col
            states = _mm(xp * to_end, bm, TN)
            chunk_decay = jnp.where(top, jnp.exp(a_end[0]), jnp.exp(a_end[1]))
            h_out.append(chunk_decay * hp + states)
            y_off = jnp.exp(a_col) * _mm(cm, hp, NT)
            d_skip = jnp.where(left[:1], _col(prm["d_heads"], lane[:1], h0), _col(prm["d_heads"], lane[:1], h1))
            ys.append((y_diag + y_off + xp * d_skip) * jax.nn.silu(zs[p]))

    b_out = []
    for grp in range(4):
        pair = (ys[2 * grp], ys[2 * grp + 1])
        ms = sum(jnp.sum(y * y, axis=1, keepdims=True) for y in pair) / 256.0
        r = lax.rsqrt(ms + RMS_EPS)
        for j, y in enumerate(pair):
            b_out.append(y * r * prm["norm_g"][2 * grp + j])
    return a_out, b_out, h_out


def _attn_block(qps, kprev, kcur, vprev, vcur, sink_row, first):
    lane = lax.broadcasted_iota(jnp.int32, (CH, CH), 1)
    left = lane < 64
    kband = jnp.concatenate([kprev, kcur], axis=0)
    vband = jnp.concatenate([vprev, vcur], axis=0)
    left2 = lax.broadcasted_iota(jnp.int32, kband.shape, 1) < 64
    ksw, vsw = _swap64(kband), _swap64(vband)
    kdup = [jnp.where(left2, kband, ksw), jnp.where(left2, ksw, kband)]
    vdup = [jnp.where(left2, vband, vsw), jnp.where(left2, vsw, vband)]
    qi = lax.broadcasted_iota(jnp.int32, (CH, 2 * CH), 0)
    si = lax.broadcasted_iota(jnp.int32, (CH, 2 * CH), 1)
    rel = qi + CH - si
    valid = (rel >= 0) & (rel < CH) & (jnp.logical_not(first) | (si >= CH))
    outs = []
    for p in range(N_BLK):
        j = p // 4
        halves = []
        for side, h in ((0, 2 * p), (1, 2 * p + 1)):
            qh = jnp.where(left, qps[p], 0.0) if side == 0 else jnp.where(left, 0.0, qps[p])
            s = _mm(qh, kdup[j], NT) * ATT_SCALE
            s = jnp.where(valid, s, -jnp.inf)
            sink = _col(sink_row, lane[:1], h)
            m = lax.stop_gradient(jnp.maximum(jnp.max(s, axis=1, keepdims=True), sink))
            pexp = jnp.exp(s - m)
            denom = jnp.sum(pexp, axis=1, keepdims=True) + jnp.exp(sink - m)
            halves.append(_mm(pexp / denom, vdup[j], NN))
        outs.append(jnp.where(left, halves[0], halves[1]))
    return outs


def _rmsnorm(x, g):
    r = lax.rsqrt(jnp.mean(x * x, axis=-1, keepdims=True) + RMS_EPS)
    return x * r * g


def rmsnorm_fwd(x, g_row, name):
    s, d = x.shape
    tm = min(512, s)

    def body(x_ref, g_ref, y_ref):
        y_ref[...] = _rmsnorm(x_ref[...], g_ref[...]).astype(bf16)

    return pl.pallas_call(
        body, name=name, grid=(s // tm,),
        in_specs=[pl.BlockSpec((tm, d), lambda i: (i, 0)), pl.BlockSpec((1, d), lambda i: (0, 0))],
        out_specs=pl.BlockSpec((tm, d), lambda i: (i, 0)),
        out_shape=jax.ShapeDtypeStruct((s, d), bf16),
        compiler_params=_cparams(("parallel",)),
    )(x, g_row)


def rmsnorm_bwd(x, g_row, dy, res, name):
    s, d = x.shape
    tm = min(512, s)

    def body(x_ref, g_ref, dy_ref, res_ref, dx_ref, dg_ref):
        @pl.when(pl.program_id(0) == 0)
        def _():
            dg_ref[...] = jnp.zeros_like(dg_ref)

        _, vjp = jax.vjp(_rmsnorm, x_ref[...], g_ref[...])
        dx, dg = vjp(dy_ref[...])
        dx_ref[...] = res_ref[...] + dx
        dg_ref[...] += dg

    tile = pl.BlockSpec((tm, d), lambda i: (i, 0))
    row = pl.BlockSpec((1, d), lambda i: (0, 0))
    return pl.pallas_call(
        body, name=name, grid=(s // tm,),
        in_specs=[tile, row, tile, tile], out_specs=[tile, row],
        out_shape=[jax.ShapeDtypeStruct((s, d), f32), jax.ShapeDtypeStruct((1, d), f32)],
        compiler_params=_cparams(("arbitrary",)),
    )(x, g_row, dy, res)


def final_loss(h, g_row, target, name):
    s, d = h.shape
    tm = min(512, s)

    def body(h_ref, g_ref, t_ref, loss_ref, dh_ref, dg_ref):
        @pl.when(pl.program_id(0) == 0)
        def _():
            dg_ref[...] = jnp.zeros_like(dg_ref)
            loss_ref[...] = jnp.zeros_like(loss_ref)

        def f(hv, gv):
            err = jnp.square(_rmsnorm(hv, gv) - t_ref[...])
            return 0.5 * jnp.sum(jnp.mean(err, axis=-1, keepdims=True), axis=0, keepdims=True)

        loss, vjp = jax.vjp(f, h_ref[...], g_ref[...])
        dh, dg = vjp(jnp.ones_like(loss))
        dh_ref[...] = dh
        dg_ref[...] += dg
        loss_ref[...] += jnp.broadcast_to(loss, loss_ref.shape)

    tile = pl.BlockSpec((tm, d), lambda i: (i, 0))
    row = pl.BlockSpec((1, d), lambda i: (0, 0))
    return pl.pallas_call(
        body, name=name, grid=(s // tm,),
        in_specs=[tile, row, tile],
        out_specs=[pl.BlockSpec((1, 128), lambda i: (0, 0)), tile, row],
        out_shape=[jax.ShapeDtypeStruct((1, 128), f32), jax.ShapeDtypeStruct((s, d), f32),
                   jax.ShapeDtypeStruct((1, d), f32)],
        compiler_params=_cparams(("arbitrary",)),
    )(h, g_row, target)


def colsum(x, name):
    s, n = x.shape
    tm = min(512, s)

    def body(x_ref, o_ref):
        @pl.when(pl.program_id(0) == 0)
        def _():
            o_ref[...] = jnp.zeros_like(o_ref)

        o_ref[...] += jnp.sum(x_ref[...].astype(f32), axis=0, keepdims=True)

    return pl.pallas_call(
        body, name=name, grid=(s // tm,),
        in_specs=[pl.BlockSpec((tm, n), lambda i: (i, 0))],
        out_specs=pl.BlockSpec((1, n), lambda i: (0, 0)),
        out_shape=jax.ShapeDtypeStruct((1, n), f32),
        compiler_params=_cparams(("arbitrary",)),
    )(x)


def _fit(dim, want):
    if dim <= want:
        return dim
    t = want
    while dim % t:
        t -= 128
    return t


def matmul(a, b, *, dims, name, out_dtype=f32, tm=1024, tn=512, tk=8192, a_pro=None, epi=None, epi_args=(),
           out_by_col_tile=False, after=None):
    if dims == "nn":
        (m, k), n = a.shape, b.shape[1]
    elif dims == "nt":
        (m, k), n = a.shape, b.shape[0]
    else:
        (k, m), n = a.shape, b.shape[1]
    tm, tn, tk = _fit(m, tm), _fit(n, tn), _fit(k, tk)
    nk = k // tk
    if dims == "nn":
        a_spec = pl.BlockSpec((tm, tk), lambda i, j, kk: (i, kk))
        b_spec = pl.BlockSpec((tk, tn), lambda i, j, kk: (kk, j))
        dn = NN
    elif dims == "nt":
        a_spec = pl.BlockSpec((tm, tk), lambda i, j, kk: (i, kk))
        b_spec = pl.BlockSpec((tn, tk), lambda i, j, kk: (j, kk))
        dn = NT
    else:
        a_spec = pl.BlockSpec((tk, tm), lambda i, j, kk: (kk, i))
        b_spec = pl.BlockSpec((tk, tn), lambda i, j, kk: (kk, j))
        dn = TN
    e_specs = [pl.BlockSpec((tm, tn), lambda i, j, kk: (i, j)) if kind == "tile"
               else pl.BlockSpec((1, tn), lambda i, j, kk: (0, j)) for kind, _ in epi_args]
    n_epi = len(epi_args)
    order_specs = [] if after is None else [pl.BlockSpec((8, 128), lambda i, j, kk: (0, 0))]
    order_args = [] if after is None else [after]

    def body(*refs):
        a_ref, b_ref = refs[0], refs[1]
        e_refs = refs[2:2 + n_epi]
        n_in = 2 + n_epi + len(order_args)
        o_ref = refs[n_in]
        av = a_ref[...]
        if a_pro is not None:
            av = a_pro(av)
        part = _mm(av, b_ref[...], dn)

        def finish(acc):
            if epi is not None:
                acc = epi(acc, *[r[...] for r in e_refs])
            o_ref[...] = acc.astype(out_dtype)

        if nk == 1:
            finish(part)
        else:
            acc_ref = refs[n_in + 1]
            kk = pl.program_id(2)

            @pl.when(kk == 0)
            def _():
                acc_ref[...] = part

            @pl.when(kk > 0)
            def _():
                acc_ref[...] += part

            @pl.when(kk == nk - 1)
            def _():
                finish(acc_ref[...])

    if out_by_col_tile:
        out_spec = pl.BlockSpec((None, tm, tn), lambda i, j, kk: (j, i, 0))
        out_shape = jax.ShapeDtypeStruct((n // tn, m, tn), out_dtype)
    else:
        out_spec = pl.BlockSpec((tm, tn), lambda i, j, kk: (i, j))
        out_shape = jax.ShapeDtypeStruct((m, n), out_dtype)
    return pl.pallas_call(
        body, name=name, grid=(m // tm, n // tn, nk),
        in_specs=[a_spec, b_spec] + e_specs + order_specs,
        out_specs=out_spec,
        out_shape=out_shape,
        scratch_shapes=[pltpu.VMEM((tm, tn), f32)] if nk > 1 else [],
        compiler_params=_cparams(("parallel", "parallel", "arbitrary")),
    )(a, b, *[arr for _, arr in epi_args], *order_args)


def _relu2(a):
    r = jnp.maximum(a.astype(f32), 0.0)
    return r * r


def _add(acc, t):
    return acc + t


def _add_bias(acc, t):
    return acc + t


def _add_bias_res(acc, bias, res):
    return acc + bias + res


def _times_relu2_grad(acc, a):
    return acc * (2.0 * jnp.maximum(a.astype(f32), 0.0))


_MIXER_PARAM_SHAPES = (
    ("ln_g", (1, D_MODEL)), ("ln_b", (1, D_MODEL)), ("wm", (N_BLK, CH, CH)), ("bs_t", (CH, CH)),
    ("conv_w", (8, 2048)), ("conv_b", (1, 2048)), ("dt_bias", (1, CH)), ("a_log", (1, CH)),
    ("d_heads", (1, CH)), ("norm_g", (1, D_MODEL)),
)


def _blocks(v, n, off=0):
    return [v[:, off + i * CH: off + (i + 1) * CH] for i in range(n)]


def _split_mixer_params(vals):
    p = dict(vals)
    return {
        "ln_g": _blocks(p["ln_g"], N_BLK), "ln_b": _blocks(p["ln_b"], N_BLK),
        "wm": [p["wm"][g] for g in range(N_BLK)], "bs_t": p["bs_t"],
        "conv_w": _blocks(p["conv_w"], XBC_BLKS), "conv_b": _blocks(p["conv_b"], XBC_BLKS),
        "dt_bias": p["dt_bias"], "a_log": p["a_log"], "d_heads": p["d_heads"],
        "norm_g": _blocks(p["norm_g"], N_BLK),
    }


def _mixer_leaves(proj_ref, halo_ref, keep_halo):
    pv = proj_ref
    us = [pv[:, OFF_U + i * CH: OFF_U + (i + 1) * CH] for i in range(N_BLK)]
    vs = [pv[:, OFF_V + i * CH: OFF_V + (i + 1) * CH] for i in range(N_BLK)]
    zs = [pv[:, OFF_Z + i * CH: OFF_Z + (i + 1) * CH] for i in range(N_BLK)]
    xbcs = [pv[:, OFF_X + i * CH: OFF_X + (i + 1) * CH] for i in range(XBC_BLKS)]
    halos = [halo_ref[:, OFF_X + i * CH: OFF_X + (i + 1) * CH] * keep_halo for i in range(XBC_BLKS)]
    dtblk = pv[:, OFF_DT: OFF_DT + CH]
    return us, vs, zs, xbcs, halos, dtblk


def mixer_fwd(proj, prm):
    s = proj.shape[0]
    nc = s // CH
    names = [n for n, _ in _MIXER_PARAM_SHAPES]

    def body(proj_ref, halo_ref, *rest):
        p_refs = rest[:len(names)]
        ab_ref, hs_ref, h_ref = rest[len(names):]
        c = pl.program_id(0)

        @pl.when(c == 0)
        def _():
            h_ref[...] = jnp.zeros_like(h_ref)

        hs_ref[...] = h_ref[...]
        keep = (c > 0).astype(f32)
        us, vs, zs, xbcs, halos, dtblk = _mixer_leaves(proj_ref, halo_ref, keep)
        hps = [h_ref[i * CH:(i + 1) * CH, :] for i in range(N_BLK)]
        p = _split_mixer_params({n: r[...] for n, r in zip(names, p_refs)})
        a_out, b_out, h_out = _mixer_chunk(us, vs, zs, xbcs, halos, dtblk, hps, p)
        for i in range(N_BLK):
            ab_ref[:, i * CH:(i + 1) * CH] = a_out[i].astype(bf16)
            ab_ref[:, D_MODEL + i * CH: D_MODEL + (i + 1) * CH] = b_out[i].astype(bf16)
            h_ref[i * CH:(i + 1) * CH, :] = h_out[i]

    def const(shape):
        return pl.BlockSpec(shape, lambda c: (0,) * len(shape))

    return pl.pallas_call(
        body, name="mixer_fwd", grid=(nc,),
        in_specs=[pl.BlockSpec((CH, NP_IN), lambda c: (c, 0)),
                  pl.BlockSpec((8, NP_IN), lambda c: (jnp.maximum(c * (CH // 8) - 1, 0), 0))]
                 + [const(shp) for _, shp in _MIXER_PARAM_SHAPES],
        out_specs=[pl.BlockSpec((CH, 2 * D_MODEL), lambda c: (c, 0)),
                   pl.BlockSpec((None, D_MODEL, CH), lambda c: (c, 0, 0))],
        out_shape=[jax.ShapeDtypeStruct((s, 2 * D_MODEL), bf16), jax.ShapeDtypeStruct((nc, D_MODEL, CH), f32)],
        scratch_shapes=[pltpu.VMEM((D_MODEL, CH), f32)],
        compiler_params=_cparams(("arbitrary",)),
    )(proj, proj, *[prm[n] for n in names])


def mixer_bwd(proj, hstates, dab, prm):
    s = proj.shape[0]
    nc = s // CH
    names = [n for n, _ in _MIXER_PARAM_SHAPES]
    npar = len(names)

    def body(proj_ref, halo_ref, hs_ref, dab_ref, *rest):
        p_refs = rest[:npar]
        dproj_ref = rest[npar]
        g_refs = rest[npar + 1: 2 * npar + 1]
        dh_ref, dhalo_ref = rest[2 * npar + 1:]
        i = pl.program_id(0)
        c = nc - 1 - i

        @pl.when(i == 0)
        def _():
            dh_ref[...] = jnp.zeros_like(dh_ref)
            dhalo_ref[...] = jnp.zeros_like(dhalo_ref)
            for r in g_refs:
                r[...] = jnp.zeros_like(r)

        keep = (c > 0).astype(f32)
        us, vs, zs, xbcs, halos, dtblk = _mixer_leaves(proj_ref, halo_ref, keep)
        hps = [hs_ref[j * CH:(j + 1) * CH, :] for j in range(N_BLK)]
        pvals = {n: r[...] for n, r in zip(names, p_refs)}

        def fn(us, vs, zs, xbcs, halos, dtblk, hps, pvals):
            return _mixer_chunk(us, vs, zs, xbcs, halos, dtblk, hps, _split_mixer_params(pvals))

        _, vjp = jax.vjp(fn, us, vs, zs, xbcs, halos, dtblk, hps, pvals)
        da = [dab_ref[:, j * CH:(j + 1) * CH].astype(f32) for j in range(N_BLK)]
        db = [dab_ref[:, D_MODEL + j * CH: D_MODEL + (j + 1) * CH].astype(f32) for j in range(N_BLK)]
        dh = [dh_ref[j * CH:(j + 1) * CH, :] for j in range(N_BLK)]
        dus, dvs, dzs, dxbcs, dhalos, ddt, dhps, dp = vjp((da, db, dh))

        for j in range(N_BLK):
            dproj_ref[:, OFF_U + j * CH: OFF_U + (j + 1) * CH] = dus[j].astype(bf16)
            dproj_ref[:, OFF_V + j * CH: OFF_V + (j + 1) * CH] = dvs[j].astype(bf16)
            dproj_ref[:, OFF_Z + j * CH: OFF_Z + (j + 1) * CH] = dzs[j].astype(bf16)
            dh_ref[j * CH:(j + 1) * CH, :] = dhps[j]
        zeros_top = jnp.zeros((CH - 8, CH), f32)
        for j in range(XBC_BLKS):
            late = jnp.concatenate([zeros_top, dhalo_ref[:, j * CH:(j + 1) * CH]], axis=0)
            dproj_ref[:, OFF_X + j * CH: OFF_X + (j + 1) * CH] = (dxbcs[j] + late).astype(bf16)
        for j in range(XBC_BLKS):
            dhalo_ref[:, j * CH:(j + 1) * CH] = dhalos[j] * keep
        lane = lax.broadcasted_iota(jnp.int32, (CH, CH), 1)
        dproj_ref[:, OFF_DT: OFF_DT + CH] = jnp.where(lane < SSM_HEADS, ddt, 0.0).astype(bf16)
        dproj_ref[:, OFF_DT + CH:] = jnp.zeros((CH, NP_IN - OFF_DT - CH), bf16)
        for n, r in zip(names, g_refs):
            r[...] += dp[n]

    def const(shape):
        return pl.BlockSpec(shape, lambda i: (0,) * len(shape))

    outs = pl.pallas_call(
        body, name="mixer_bwd", grid=(nc,),
        in_specs=[pl.BlockSpec((CH, NP_IN), lambda i: (nc - 1 - i, 0)),
                  pl.BlockSpec((8, NP_IN), lambda i: (jnp.maximum((nc - 1 - i) * (CH // 8) - 1, 0), 0)),
                  pl.BlockSpec((None, D_MODEL, CH), lambda i: (nc - 1 - i, 0, 0)),
                  pl.BlockSpec((CH, 2 * D_MODEL), lambda i: (nc - 1 - i, 0))]
                 + [const(shp) for _, shp in _MIXER_PARAM_SHAPES],
        out_specs=[pl.BlockSpec((CH, NP_IN), lambda i: (nc - 1 - i, 0))]
                  + [const(shp) for _, shp in _MIXER_PARAM_SHAPES],
        out_shape=[jax.ShapeDtypeStruct((s, NP_IN), bf16)]
                  + [jax.ShapeDtypeStruct(shp, f32) for _, shp in _MIXER_PARAM_SHAPES],
        scratch_shapes=[pltpu.VMEM((D_MODEL, CH), f32), pltpu.VMEM((8, 2048), f32)],
        compiler_params=_cparams(("arbitrary",)),
    )(proj, proj, hstates, dab, *[prm[n] for n in names])
    return outs[0], dict(zip(names, outs[1:]))


_K_BLK = D_MODEL // CH
_V_BLK = _K_BLK + 1


def _attn_specs(rev, nb):
    def blk(i):
        return nb - 1 - i if rev else i

    q_spec = pl.BlockSpec((CH, D_MODEL), lambda i: (blk(i), 0))
    kv = lambda col, prev: pl.BlockSpec(
        (CH, CH), lambda i: (jnp.maximum(blk(i) - 1, 0) if prev else blk(i), col))
    return q_spec, [kv(_K_BLK, True), kv(_K_BLK, False), kv(_V_BLK, True), kv(_V_BLK, False)]


def attn_fwd(qkv, sink_row):
    s = qkv.shape[0]
    nb = s // CH

    def body(q_ref, kp_ref, kc_ref, vp_ref, vc_ref, sink_ref, o_ref):
        qps = [q_ref[:, p * CH:(p + 1) * CH] for p in range(N_BLK)]
        outs = _attn_block(qps, kp_ref[...], kc_ref[...], vp_ref[...], vc_ref[...], sink_ref[...],
                           pl.program_id(0) == 0)
        for p in range(N_BLK):
            o_ref[:, p * CH:(p + 1) * CH] = outs[p].astype(bf16)

    q_spec, kv_specs = _attn_specs(False, nb)
    return pl.pallas_call(
        body, name="attn_fwd", grid=(nb,),
        in_specs=[q_spec] + kv_specs + [pl.BlockSpec((1, CH), lambda i: (0, 0))],
        out_specs=pl.BlockSpec((CH, D_MODEL), lambda i: (i, 0)),
        out_shape=jax.ShapeDtypeStruct((s, D_MODEL), bf16),
        compiler_params=_cparams(("parallel",)),
    )(qkv, qkv, qkv, qkv, qkv, sink_row)


def attn_bwd(qkv, sink_row, dout):
    s = qkv.shape[0]
    nb = s // CH

    def body(q_ref, kp_ref, kc_ref, vp_ref, vc_ref, sink_ref, do_ref, dqkv_ref, dsink_ref, carry_ref):
        i = pl.program_id(0)
        blk = nb - 1 - i

        @pl.when(i == 0)
        def _():
            dsink_ref[...] = jnp.zeros_like(dsink_ref)
            carry_ref[...] = jnp.zeros_like(carry_ref)

        qps = [q_ref[:, p * CH:(p + 1) * CH] for p in range(N_BLK)]
        first = blk == 0
        _, vjp = jax.vjp(lambda *a: _attn_block(*a, first), qps, kp_ref[...], kc_ref[...], vp_ref[...],
                         vc_ref[...], sink_ref[...])
        dos = [do_ref[:, p * CH:(p + 1) * CH].astype(f32) for p in range(N_BLK)]
        dqs, dkp, dkc, dvp, dvc, dsink = vjp(dos)
        for p in range(N_BLK):
            dqkv_ref[:, p * CH:(p + 1) * CH] = dqs[p].astype(bf16)
        dqkv_ref[:, D_MODEL: D_MODEL + CH] = (dkc + carry_ref[0]).astype(bf16)
        dqkv_ref[:, D_MODEL + CH:] = (dvc + carry_ref[1]).astype(bf16)
        keep = jnp.logical_not(first).astype(f32)
        carry_ref[0] = dkp * keep
        carry_ref[1] = dvp * keep
        dsink_ref[...] += dsink

    q_spec, kv_specs = _attn_specs(True, nb)
    return pl.pallas_call(
        body, name="attn_bwd", grid=(nb,),
        in_specs=[q_spec] + kv_specs + [pl.BlockSpec((1, CH), lambda i: (0, 0)),
                                        pl.BlockSpec((CH, D_MODEL), lambda i: (nb - 1 - i, 0))],
        out_specs=[pl.BlockSpec((CH, QKV_DIM), lambda i: (nb - 1 - i, 0)), pl.BlockSpec((1, CH), lambda i: (0, 0))],
        out_shape=[jax.ShapeDtypeStruct((s, QKV_DIM), bf16), jax.ShapeDtypeStruct((1, CH), f32)],
        scratch_shapes=[pltpu.VMEM((2, CH, CH), f32)],
        compiler_params=_cparams(("arbitrary",)),
    )(qkv, qkv, qkv, qkv, qkv, sink_row, dout)


def adamw(w, g, m, v, name):
    def body(w_ref, g_ref, m_ref, v_ref, d_ref, nm_ref, nv_ref):
        gv = g_ref[...]
        nm = ADAM_B1 * m_ref[...] + (1.0 - ADAM_B1) * gv
        nv = ADAM_B2 * v_ref[...] + (1.0 - ADAM_B2) * jnp.square(gv)
        m_hat = nm / (1.0 - ADAM_B1 ** ADAM_STEP)
        v_hat = nv / (1.0 - ADAM_B2 ** ADAM_STEP)
        d_ref[...] = -ADAM_LR * (m_hat / (jnp.sqrt(v_hat) + ADAM_EPS) + ADAM_WD * w_ref[...])
        nm_ref[...] = nm
        nv_ref[...] = nv

    out_shape = [jax.ShapeDtypeStruct(w.shape, f32)] * 3
    if w.ndim == 3 and w.shape[1] % 256 == 0:
        tile = pl.BlockSpec((None, 256, w.shape[2]), lambda l, i: (l, i, 0))
        return pl.pallas_call(
            body, name=name, grid=(w.shape[0], w.shape[1] // 256),
            in_specs=[tile] * 4, out_specs=[tile] * 3, out_shape=out_shape,
            compiler_params=_cparams(("parallel", "parallel")),
        )(w, g, m, v)
    return pl.pallas_call(body, name=name, in_specs=[_VMEM] * 4, out_specs=[_VMEM] * 3, out_shape=out_shape,
                          compiler_params=_cparams())(w, g, m, v)


_MESH = pl.DeviceIdType.MESH
_ANY = pl.BlockSpec(memory_space=pl.ANY)
_VMEM = pl.BlockSpec(memory_space=pltpu.VMEM)


def _place():
    x, y, c = lax.axis_index("x"), lax.axis_index("y"), lax.axis_index("c")
    chips = [(1 - x, y), (x, 1 - y), (1 - x, 1 - y)]
    return x, y, c, 2 * x + y, chips, [2 * cx + cy for cx, cy in chips]


def _half(c, rows):
    return pl.ds(pl.multiple_of(c * (rows // 2), 16), rows // 2)


def _step_rows(rows):
    return max(t for t in range(16, 641, 16) if rows % t == 0)


def place_shard(b, slot, name):
    r, c = b.shape
    tr = _step_rows(r)

    def body(slot_ref, b_ref, o_ref):
        o_ref[...] = b_ref[...]

    return pl.pallas_call(
        body, name=name,
        grid_spec=pltpu.PrefetchScalarGridSpec(
            num_scalar_prefetch=1, grid=(r // tr,),
            in_specs=[pl.BlockSpec((tr, c), lambda i, s: (i, 0))],
            out_specs=pl.BlockSpec((None, tr, c), lambda i, s: (s[0], i, 0))),
        out_shape=jax.ShapeDtypeStruct((N_CHIPS, r, c), b.dtype),
        compiler_params=_cparams(("parallel",)),
    )(slot, b)


def allgather_weights(bufs):
    n = len(bufs)

    def body(*refs):
        out_refs = refs[n:2 * n]
        send_sems, recv_sems = refs[2 * n:]
        x, y, c, me, chips, chip_idx = _place()
        sibling = (x, y, 1 - c)

        def copy(b, k, shard, half, to):
            ref = out_refs[b]
            part = ref.at[shard, _half(half, ref.shape[1])]
            return pltpu.make_async_remote_copy(
                src_ref=part, dst_ref=part, send_sem=send_sems.at[6 * b + k], recv_sem=recv_sems.at[6 * b + k],
                device_id=to, device_id_type=_MESH)

        first = [copy(b, j, me, c, (*chips[j], c)) for j in range(3) for b in range(n)]
        for cp in first:
            cp.start()
        passed = []
        for j in range(3):
            for b in range(n):
                copy(b, j, chip_idx[j], c, sibling).wait_recv()
                passed.append(copy(b, 3 + j, chip_idx[j], c, sibling))
                passed[-1].start()
        for j in range(3):
            for b in range(n):
                copy(b, 3 + j, chip_idx[j], 1 - c, sibling).wait_recv()
        for cp in first + passed:
            cp.wait_send()

    return pl.pallas_call(
        body, name="allgather_weights",
        out_shape=[jax.ShapeDtypeStruct(b.shape, b.dtype) for b in bufs],
        in_specs=[_ANY] * n, out_specs=[_ANY] * n, input_output_aliases={i: i for i in range(n)},
        scratch_shapes=[pltpu.SemaphoreType.DMA((6 * n,)), pltpu.SemaphoreType.DMA((6 * n,))],
    )(*bufs)


def exchange_halves(bufs, tag):
    n = len(bufs)

    def body(*refs):
        g_refs, out_refs = refs[:n], refs[n:2 * n]
        send_sems, recv_sems = refs[2 * n:]
        x, y, c, *_ = _place()
        cps = [pltpu.make_async_remote_copy(
            src_ref=g_refs[b].at[:, _half(1 - c, g_refs[b].shape[1])], dst_ref=out_refs[b],
            send_sem=send_sems.at[b], recv_sem=recv_sems.at[b], device_id=(x, y, 1 - c), device_id_type=_MESH)
            for b in range(n)]
        for cp in cps:
            cp.start()
        for cp in cps:
            cp.wait()

    return pl.pallas_call(
        body, name=f"exchange_halves_{tag}",
        out_shape=[jax.ShapeDtypeStruct((N_CHIPS, b.shape[1] // 2, b.shape[2]), b.dtype) for b in bufs],
        in_specs=[_ANY] * n, out_specs=[_ANY] * n,
        scratch_shapes=[pltpu.SemaphoreType.DMA((n,)), pltpu.SemaphoreType.DMA((n,))],
    )(*bufs)


def add_halves(g, got, c_idx, name):
    hr, cols = got.shape[1], got.shape[2]
    tr = _step_rows(hr)
    steps = hr // tr

    def body(c_ref, g_ref, got_ref, o_ref):
        o_ref[...] = (g_ref[...].astype(f32) + got_ref[...].astype(f32)).astype(bf16)

    return pl.pallas_call(
        body, name=name,
        grid_spec=pltpu.PrefetchScalarGridSpec(
            num_scalar_prefetch=1, grid=(N_CHIPS, steps),
            in_specs=[pl.BlockSpec((None, tr, cols), lambda s, i, c: (s, c[0] * steps + i, 0)),
                      pl.BlockSpec((None, tr, cols), lambda s, i, c: (s, i, 0))],
            out_specs=pl.BlockSpec((None, tr, cols), lambda s, i, c: (s, i, 0))),
        out_shape=jax.ShapeDtypeStruct(got.shape, bf16),
        compiler_params=_cparams(("parallel", "parallel")),
    )(c_idx, g, got)


def sum_chips(t, got, place_idx, name):
    hr, cols = t.shape[1], t.shape[2]
    tr = _step_rows(hr)
    steps = hr // tr

    def body(idx_ref, t_ref, got_ref, o_ref):
        acc = t_ref[...].astype(f32)
        for j in range(3):
            acc = acc + got_ref[j].astype(f32)
        o_ref[...] = acc

    return pl.pallas_call(
        body, name=name,
        grid_spec=pltpu.PrefetchScalarGridSpec(
            num_scalar_prefetch=1, grid=(steps,),
            in_specs=[pl.BlockSpec((None, tr, cols), lambda i, idx: (idx[0], i, 0)),
                      pl.BlockSpec((3, tr, cols), lambda i, idx: (0, i, 0))],
            out_specs=pl.BlockSpec((tr, cols), lambda i, idx: (idx[1] * steps + i, 0))),
        out_shape=jax.ShapeDtypeStruct((2 * hr, cols), f32),
        compiler_params=_cparams(("parallel",)),
    )(place_idx, t, got)


def share_halves(bufs, tag):
    n = len(bufs)

    def body(*refs):
        out_refs = refs[n:2 * n]
        send_sems, recv_sems = refs[2 * n:]
        x, y, c, *_ = _place()

        def copy(b, half):
            part = out_refs[b].at[_half(half, out_refs[b].shape[0])]
            return pltpu.make_async_remote_copy(
                src_ref=part, dst_ref=part, send_sem=send_sems.at[b], recv_sem=recv_sems.at[b],
                device_id=(x, y, 1 - c), device_id_type=_MESH)

        for b in range(n):
            copy(b, c).start()
        for b in range(n):
            copy(b, 1 - c).wait_recv()
        for b in range(n):
            copy(b, c).wait_send()

    return pl.pallas_call(
        body, name=f"share_halves_{tag}",
        out_shape=[jax.ShapeDtypeStruct(b.shape, b.dtype) for b in bufs],
        in_specs=[_ANY] * n, out_specs=[_ANY] * n, input_output_aliases={i: i for i in range(n)},
        scratch_shapes=[pltpu.SemaphoreType.DMA((n,)), pltpu.SemaphoreType.DMA((n,))],
    )(*bufs)


_HBM = pl.BlockSpec(memory_space=pltpu.HBM)
_SEM = pl.BlockSpec(memory_space=pltpu.SEMAPHORE)
_EFFECT = pltpu.SideEffectType.DATAFLOW_SIDE_EFFECTING


def _scatter_copies(t_refs, land_refs, send_sems, recv_sems):
    x, y, c, me, chips, chip_idx = _place()
    return [pltpu.make_async_remote_copy(
        src_ref=t_refs[b].at[chip_idx[j]], dst_ref=land_refs[b].at[j], send_sem=send_sems.at[3 * b + j],
        recv_sem=recv_sems.at[3 * b + j], device_id=(*chips[j], c), device_id_type=_MESH)
        for j in range(3) for b in range(len(t_refs))]


def scatter_start(ts, tag):
    n = len(ts)
    lands = [lax.empty((3,) + t.shape[1:], t.dtype) for t in ts]

    def body(*refs):
        for cp in _scatter_copies(refs[:n], refs[n:2 * n], refs[2 * n], refs[2 * n + 1]):
            cp.start()
        token = refs[-1]
        token[...] = jnp.zeros_like(token)

    hbm = [pltpu.HBM(a.shape, a.dtype) for a in (*ts, *lands)]
    outs = pl.pallas_call(
        body, name=f"scatter_start_{tag}",
        out_shape=(pltpu.SemaphoreType.DMA((3 * n,)), pltpu.SemaphoreType.DMA((3 * n,)), *hbm,
                   jax.ShapeDtypeStruct((8, 128), f32)),
        in_specs=[_HBM] * (2 * n), out_specs=(_SEM, _SEM, *[_HBM] * (2 * n), _VMEM),
        input_output_aliases={i: 2 + i for i in range(2 * n)},
        compiler_params=pltpu.CompilerParams(has_side_effects=_EFFECT),
    )(*[pltpu.with_memory_space_constraint(a, pltpu.HBM) for a in (*ts, *lands)])
    return outs[0], outs[1], list(outs[2:2 + n]), list(outs[2 + n:2 + 2 * n]), outs[-1]


def scatter_wait(send_sems, recv_sems, ts, lands, after, tag):
    n = len(ts)

    def body(*refs):
        for cp in _scatter_copies(refs[:n], refs[n:2 * n], refs[2 * n], refs[2 * n + 1]):
            cp.wait_send()
            cp.wait_recv()

    outs = pl.pallas_call(
        body, name=f"scatter_wait_{tag}",
        out_shape=[pltpu.HBM(a.shape, a.dtype) for a in (*ts, *lands)],
        in_specs=[_HBM] * (2 * n) + [_SEM, _SEM, _ANY], out_specs=[_HBM] * (2 * n),
        input_output_aliases={i: i for i in range(2 * n)},
        compiler_params=pltpu.CompilerParams(has_side_effects=_EFFECT),
    )(*ts, *lands, send_sems, recv_sems, after)
    return list(outs[:n]), list(outs[n:])


class GradReducer:
    def __init__(self, c_idx, place_idx):
        self.c_idx, self.place_idx = c_idx, place_idx

    def start(self, bufs, tag):
        got = exchange_halves(bufs, tag)
        ts = [add_halves(b, g, self.c_idx, f"add_halves_{tag}{i}") for i, (b, g) in enumerate(zip(bufs, got))]
        send_sems, recv_sems, ts, lands, token = scatter_start(ts, tag)
        return (send_sems, recv_sems, ts, lands), token

    def finish(self, state, after, tag):
        ts, lands = scatter_wait(*state, after, tag)
        sums = [sum_chips(t, l, self.place_idx, f"sum_chips_{tag}{i}") for i, (t, l) in enumerate(zip(ts, lands))]
        return share_halves(sums, tag)


def allreduce_small(sp):
    def body(s_ref, out_ref, gather_ref, send_sems, recv_sems):
        x, y, c, me, chips, chip_idx = _place()
        sibling = (x, y, 1 - c)

        def copy(k, chip, core, to, src=None):
            dst = gather_ref.at[2 * chip + core]
            return pltpu.make_async_remote_copy(
                src_ref=dst if src is None else src, dst_ref=dst, send_sem=send_sems.at[k],
                recv_sem=recv_sems.at[k], device_id=to, device_id_type=_MESH)

        first = [copy(0, me, c, sibling, src=s_ref)]
        first += [copy(1 + j, me, c, (*chips[j], c), src=s_ref) for j in range(3)]
        for cp in first:
            cp.start()
        gather_ref[2 * me + c] = s_ref[...]
        passed = [copy(4 + j, chip_idx[j], c, sibling) for j in range(3)]
        for j in range(3):
            copy(1 + j, chip_idx[j], c, sibling).wait_recv()
            passed[j].start()
        copy(0, me, 1 - c, sibling).wait_recv()
        for j in range(3):
            copy(4 + j, chip_idx[j], 1 - c, sibling).wait_recv()
        for cp in first + passed:
            cp.wait_send()
        acc = gather_ref[0]
        for d in range(1, 2 * N_CHIPS):
            acc = acc + gather_ref[d]
        out_ref[...] = acc

    return pl.pallas_call(
        body, name="allreduce_small",
        out_shape=jax.ShapeDtypeStruct(sp.shape, sp.dtype),
        in_specs=[_VMEM], out_specs=_VMEM,
        scratch_shapes=[pltpu.VMEM((2 * N_CHIPS,) + sp.shape, sp.dtype),
                        pltpu.SemaphoreType.DMA((7,)), pltpu.SemaphoreType.DMA((7,))],
        compiler_params=_cparams(),
    )(sp)


def _n_rows(shape):
    n = 1
    for d in shape:
        n *= d
    return 8 * (-(-n // 8192))


def _pack(arrays, total_rows):
    parts = []
    for a in arrays:
        flat = a.reshape(-1)
        parts.append(jnp.pad(flat, (0, 1024 * _n_rows(a.shape) - flat.shape[0])).reshape(-1, 1024))
    rows = jnp.concatenate(parts, axis=0)
    return jnp.pad(rows, ((0, total_rows - rows.shape[0]), (0, 0)))


def _unpack(packed, shapes):
    out, r = [], 0
    for shp in shapes:
        n = 1
        for d in shp:
            n *= d
        nr = _n_rows(shp)
        out.append(packed[r:r + nr].reshape(-1)[:n].reshape(shp))
        r += nr
    return out


_BIG_NAMES = ("w_in_even", "w_out_even", "w_qkv", "w_o", "w_up", "w_down")
B1_ROWS = 4864
_B1_UP = 768
_B1_DOWN = 2816
IN_SHARD, IN_PAD = 1284, 1408
QKV_SHARD, QKV_PAD = 320, 384
B2_COLS = IN_PAD + QKV_PAD


def _lane_padded(a, cols):
    return jnp.pad(a, ((0, 0), (0, cols - a.shape[1])))

_SMALL_SHAPES = (
    ("norm_mix_g", (2, 1024)), ("norm_mlp_g", (2, 1024)), ("final_norm_g", (1024,)), ("gm_ln_g", (1, 1024)),
    ("gm_ln_b", (1, 1024)), ("gm_w_s", (1, 8, 128, 128)), ("gm_b_s", (1, 8, 128)), ("ssm_conv_b", (1, 2048)),
    ("ssm_dt_bias", (1, 16)), ("ssm_a_log", (1, 16)), ("ssm_d", (1, 16)), ("ssm_norm_g", (1, 1024)),
    ("attn_sinks", (1, 16)), ("ssm_conv_w", (1, 4, 2048)), ("b_qkv", (1, 1280)), ("b_o", (1, 1024)),
)
_N_REPLICATED = 13
_SHARDED_SMALL = (("ssm_conv_w", 2, 512), ("b_qkv", 1, 320), ("b_o", 1, 256))
_GATHER_ROWS = 24
_SHARD_PACK_ROWS = 24


def _weight_shards_to_buffers(w):
    b1 = jnp.concatenate([w["w_out_even"][0], w["w_o"][0], w["w_up"][0], w["w_up"][1], w["w_down"][0],
                          w["w_down"][1]], axis=0).astype(bf16)
    b2 = jnp.concatenate([_lane_padded(w["w_in_even"][0], IN_PAD), _lane_padded(w["w_qkv"][0], QKV_PAD)],
                         axis=1).astype(bf16)
    return b1, b2


def _gathered_to_full_weights(g1, g2):
    cols_by_owner = lambda a: a.transpose(1, 0, 2).reshape(a.shape[1], -1)
    w_in_p = _lane_padded(cols_by_owner(g2[:, :, :IN_SHARD]), NP_IN)
    w_qkv = cols_by_owner(g2[:, :, IN_PAD:IN_PAD + QKV_SHARD])
    w_out = g1[:, :512].reshape(2048, 1024)
    w_o = g1[:, 512:_B1_UP].reshape(1024, 1024)
    w_up = [cols_by_owner(g1[:, _B1_UP + 1024 * l: _B1_UP + 1024 * (l + 1)]) for l in range(2)]
    w_down = [g1[:, _B1_DOWN + 1024 * l: _B1_DOWN + 1024 * (l + 1)].reshape(4096, 1024) for l in range(2)]
    return w_in_p, w_out, w_qkv, w_o, w_up, w_down


def _row2(v):
    return v.reshape(1, -1)


def _lane_pad(v):
    return jnp.pad(v, ((0, 0), (0, CH - v.shape[1])))


def _mlp_fwd(h, g_row, w_up, w_down, tag):
    y = rmsnorm_fwd(h, g_row, f"mlp_norm{tag}")
    a = matmul(y, w_up, dims="nn", name=f"mlp_up{tag}", out_dtype=bf16, tn=1024)
    out = matmul(a, w_down, dims="nn", name=f"mlp_down{tag}", a_pro=_relu2, epi=_add, epi_args=(("tile", h),))
    return out, y, a


def _mlp_bwd(dh_out, h, g_row, y, a, w_up, w_down, tag, after=None):
    da = matmul(dh_out, w_down, dims="nt", name=f"mlp_da{tag}", out_dtype=bf16, tn=1024,
                epi=_times_relu2_grad, epi_args=(("tile", a),), after=after)
    dw_down = matmul(a, dh_out, dims="tn", name=f"mlp_dwdown{tag}", out_dtype=bf16, a_pro=_relu2)
    dw_up = matmul(y, da, dims="tn", name=f"mlp_dwup{tag}", out_dtype=bf16, tn=1024, out_by_col_tile=True)
    dy = matmul(da, w_up, dims="nt", name=f"mlp_dy{tag}")
    dh, dg = rmsnorm_bwd(h, g_row, dy, dh_out, f"mlp_dnorm{tag}")
    return dh, dg, dw_up, dw_down


def _by_owner(a):
    return a.reshape(N_CHIPS, a.shape[0] // N_CHIPS, a.shape[1])


def _col_shards(a, shard, padded):
    return jnp.stack([_lane_padded(a[:, shard * s: shard * (s + 1)], padded) for s in range(N_CHIPS)])


def _local_step(x, target, big, sm, reducer):
    w_in_p, w_out, w_qkv, w_o, w_up, w_down = big
    mix_g = [_row2(sm["norm_mix_g"][i]) for i in range(2)]
    mlp_g = [_row2(sm["norm_mlp_g"][i]) for i in range(2)]
    mixer_prm = {
        "ln_g": sm["gm_ln_g"], "ln_b": sm["gm_ln_b"], "wm": sm["gm_w_s"][0],
        "bs_t": jnp.pad(sm["gm_b_s"][0].T, ((0, 0), (0, CH - N_BLK))),
        "conv_w": jnp.pad(sm["ssm_conv_w"][0], ((0, 4), (0, 0))), "conv_b": sm["ssm_conv_b"],
        "dt_bias": _lane_pad(sm["ssm_dt_bias"]), "a_log": _lane_pad(sm["ssm_a_log"]),
        "d_heads": _lane_pad(sm["ssm_d"]), "norm_g": sm["ssm_norm_g"],
    }
    sink_row = _lane_pad(sm["attn_sinks"])

    y0 = rmsnorm_fwd(x, mix_g[0], "mix_norm0")
    proj = matmul(y0, w_in_p, dims="nn", name="in_proj", tn=768)
    ab, hstates = mixer_fwd(proj, mixer_prm)
    h1 = matmul(ab, w_out, dims="nn", name="out_proj", epi=_add, epi_args=(("tile", x),))
    h2, y1, a1 = _mlp_fwd(h1, mlp_g[0], w_up[0], w_down[0], 0)
    y2 = rmsnorm_fwd(h2, mix_g[1], "mix_norm1")
    qkv = matmul(y2, w_qkv, dims="nn", name="qkv_proj", tn=QKV_DIM, epi=_add_bias, epi_args=(("row", sm["b_qkv"]),))
    att = attn_fwd(qkv, sink_row)
    h3 = matmul(att, w_o, dims="nn", name="o_proj", epi=_add_bias_res,
                epi_args=(("row", sm["b_o"]), ("tile", h2)))
    h4, y3, a3 = _mlp_fwd(h3, mlp_g[1], w_up[1], w_down[1], 1)
    loss, dh4, dg_final = final_loss(h4, _row2(sm["final_norm_g"]), target, "final_loss")

    dh3, dg_mlp1, dw_up1, dw_down1 = _mlp_bwd(dh4, h3, mlp_g[1], y3, a3, w_up[1], w_down[1], 1)
    db_o = colsum(dh3, "db_o")
    datt = matmul(dh3, w_o, dims="nt", name="attn_dout", out_dtype=bf16)
    dw_o = matmul(att, dh3, dims="tn", name="dw_o", out_dtype=bf16)
    dqkv, dsink = attn_bwd(qkv, sink_row, datt)
    db_qkv = colsum(dqkv, "db_qkv")
    dw_qkv = matmul(y2, dqkv, dims="tn", name="dw_qkv", out_dtype=bf16, tn=QKV_DIM)
    dy2 = matmul(dqkv, w_qkv, dims="nt", name="dy_qkv", tk=QKV_DIM)
    dh2, dg_mix1 = rmsnorm_bwd(h2, mix_g[1], dy2, dh3, "mix_dnorm1")
    layer1 = [jnp.concatenate([_by_owner(dw_o), dw_up1, _by_owner(dw_down1)], axis=1),
              _col_shards(dw_qkv, QKV_SHARD, QKV_PAD)]
    flight1, token1 = reducer.start(layer1, "l1")
    dh1, dg_mlp0, dw_up0, dw_down0 = _mlp_bwd(dh2, h1, mlp_g[0], y1, a1, w_up[0], w_down[0], 0, after=token1)
    r_l1, r_qkv = reducer.finish(flight1, dh1, "l1")
    flight0, token0 = reducer.start([jnp.concatenate([dw_up0, _by_owner(dw_down0)], axis=1)], "l0")
    dab = matmul(dh1, w_out, dims="nt", name="mixer_dout", tn=1024, after=token0)
    dw_out = matmul(ab, dh1, dims="tn", name="dw_out", out_dtype=bf16)
    dproj, dmix = mixer_bwd(proj, hstates, dab, mixer_prm)
    dw_in_p = matmul(y0, dproj, dims="tn", name="dw_in", out_dtype=bf16, tn=768)
    dy0 = matmul(dproj, w_in_p, dims="nt", name="dy_in")
    dx, dg_mix0 = rmsnorm_bwd(x, mix_g[0], dy0, dh1, "mix_dnorm0")
    (r_mlp0,) = reducer.finish(flight0, dx, "l0")
    flight_m, token_m = reducer.start([_by_owner(dw_out), _col_shards(dw_in_p, IN_SHARD, IN_PAD)], "mix")
    r_out, r_in = reducer.finish(flight_m, token_m, "mix")
    reduced = {
        "w_out_even": r_out[None], "w_in_even": r_in[None, :, :IN_SHARD], "w_qkv": r_qkv[None, :, :QKV_SHARD],
        "w_o": r_l1[None, :256], "w_up": jnp.stack([r_mlp0[:1024], r_l1[256:1280]]),
        "w_down": jnp.stack([r_mlp0[1024:], r_l1[1280:]]),
    }

    small_grads = {
        "norm_mix_g": jnp.concatenate([dg_mix0, dg_mix1], axis=0),
        "norm_mlp_g": jnp.concatenate([dg_mlp0, dg_mlp1], axis=0),
        "final_norm_g": dg_final[0], "gm_ln_g": dmix["ln_g"], "gm_ln_b": dmix["ln_b"],
        "gm_w_s": dmix["wm"][None], "gm_b_s": dmix["bs_t"][:, :N_BLK].T[None],
        "ssm_conv_b": dmix["conv_b"], "ssm_dt_bias": dmix["dt_bias"][:, :SSM_HEADS],
        "ssm_a_log": dmix["a_log"][:, :SSM_HEADS], "ssm_d": dmix["d_heads"][:, :SSM_HEADS],
        "ssm_norm_g": dmix["norm_g"], "attn_sinks": dsink[:, :SSM_HEADS],
        "ssm_conv_w": dmix["conv_w"][None, :4], "b_qkv": db_qkv, "b_o": db_o,
    }
    return loss, dx, reduced, small_grads


def kernel(x, norm_mix_g, norm_mlp_g, final_norm_g, w_in_even, w_out_even, gm_ln_g, gm_ln_b, gm_w_s, gm_b_s, ssm_conv_w, ssm_conv_b, ssm_dt_bias, ssm_a_log, ssm_d, ssm_norm_g, w_qkv, b_qkv, w_o, b_o, attn_sinks, w_up, w_down, loss_target, m_norm_mix_g, m_norm_mlp_g, m_final_norm_g, m_w_in_even, m_w_out_even, m_gm_ln_g, m_gm_ln_b, m_gm_w_s, m_gm_b_s, m_ssm_conv_w, m_ssm_conv_b, m_ssm_dt_bias, m_ssm_a_log, m_ssm_d, m_ssm_norm_g, m_w_qkv, m_b_qkv, m_w_o, m_b_o, m_attn_sinks, m_w_up, m_w_down, v_norm_mix_g, v_norm_mlp_g, v_final_norm_g, v_w_in_even, v_w_out_even, v_gm_ln_g, v_gm_ln_b, v_gm_w_s, v_gm_b_s, v_ssm_conv_w, v_ssm_conv_b, v_ssm_dt_bias, v_ssm_a_log, v_ssm_d, v_ssm_norm_g, v_w_qkv, v_b_qkv, v_w_o, v_b_o, v_attn_sinks, v_w_up, v_w_down):
    w = dict(norm_mix_g=norm_mix_g, norm_mlp_g=norm_mlp_g, final_norm_g=final_norm_g, w_in_even=w_in_even,
             w_out_even=w_out_even, gm_ln_g=gm_ln_g, gm_ln_b=gm_ln_b, gm_w_s=gm_w_s, gm_b_s=gm_b_s,
             ssm_conv_w=ssm_conv_w, ssm_conv_b=ssm_conv_b, ssm_dt_bias=ssm_dt_bias, ssm_a_log=ssm_a_log,
             ssm_d=ssm_d, ssm_norm_g=ssm_norm_g, w_qkv=w_qkv, b_qkv=b_qkv, w_o=w_o, b_o=b_o,
             attn_sinks=attn_sinks, w_up=w_up, w_down=w_down)
    m = dict(norm_mix_g=m_norm_mix_g, norm_mlp_g=m_norm_mlp_g, final_norm_g=m_final_norm_g,
             w_in_even=m_w_in_even, w_out_even=m_w_out_even, gm_ln_g=m_gm_ln_g, gm_ln_b=m_gm_ln_b,
             gm_w_s=m_gm_w_s, gm_b_s=m_gm_b_s, ssm_conv_w=m_ssm_conv_w, ssm_conv_b=m_ssm_conv_b,
             ssm_dt_bias=m_ssm_dt_bias, ssm_a_log=m_ssm_a_log, ssm_d=m_ssm_d, ssm_norm_g=m_ssm_norm_g,
             w_qkv=m_w_qkv, b_qkv=m_b_qkv, w_o=m_w_o, b_o=m_b_o, attn_sinks=m_attn_sinks, w_up=m_w_up,
             w_down=m_w_down)
    v = dict(norm_mix_g=v_norm_mix_g, norm_mlp_g=v_norm_mlp_g, final_norm_g=v_final_norm_g,
             w_in_even=v_w_in_even, w_out_even=v_w_out_even, gm_ln_g=v_gm_ln_g, gm_ln_b=v_gm_ln_b,
             gm_w_s=v_gm_w_s, gm_b_s=v_gm_b_s, ssm_conv_w=v_ssm_conv_w, ssm_conv_b=v_ssm_conv_b,
             ssm_dt_bias=v_ssm_dt_bias, ssm_a_log=v_ssm_a_log, ssm_d=v_ssm_d, ssm_norm_g=v_ssm_norm_g,
             w_qkv=v_w_qkv, b_qkv=v_b_qkv, w_o=v_w_o, b_o=v_b_o, attn_sinks=v_attn_sinks, w_up=v_w_up,
             w_down=v_w_down)
    names = ("norm_mix_g", "norm_mlp_g", "final_norm_g", "w_in_even", "w_out_even", "gm_ln_g", "gm_ln_b",
             "gm_w_s", "gm_b_s", "ssm_conv_w", "ssm_conv_b", "ssm_dt_bias", "ssm_a_log", "ssm_d", "ssm_norm_g",
             "w_qkv", "b_qkv", "w_o", "b_o", "attn_sinks", "w_up", "w_down")

    cx, cy, cc = lax.axis_index("x"), lax.axis_index("y"), lax.axis_index("c")
    chip = 2 * cx + cy
    c_idx = jnp.reshape(cc, (1,)).astype(jnp.int32)
    chip_idx = jnp.reshape(chip, (1,)).astype(jnp.int32)

    b1, b2 = _weight_shards_to_buffers(w)
    big = _gathered_to_full_weights(*allgather_weights(
        [place_shard(b1, chip_idx, "place_shard1"), place_shard(b2, chip_idx, "place_shard2")]))

    small_shard_placed = []
    for name, axis, width in _SHARDED_SMALL:
        full_shape = dict(_SMALL_SHAPES)[name]
        placed = lax.dynamic_update_slice_in_dim(jnp.zeros(full_shape, f32), w[name], chip * width, axis)
        small_shard_placed.append(jnp.where(cc == 0, placed, 0.0))
    gathered_small = _unpack(allreduce_small(_pack(small_shard_placed, _GATHER_ROWS)),
                             [dict(_SMALL_SHAPES)[n] for n, _, _ in _SHARDED_SMALL])
    sm = {n: w[n] for n, _ in _SMALL_SHAPES[:_N_REPLICATED]}
    sm.update({n: g for (n, _, _), g in zip(_SHARDED_SMALL, gathered_small)})

    reducer = GradReducer(c_idx, jnp.concatenate([chip_idx, c_idx]))
    loss_part, dx, grads, small_grads = _local_step(x[0], loss_target[0], big, sm, reducer)
    loss = lax.psum(loss_part[0, 0], ("x", "y", "c"))

    small_sum = allreduce_small(_pack([small_grads[n] for n, _ in _SMALL_SHAPES], SMALL_ROWS))
    small_full = dict(zip([n for n, _ in _SMALL_SHAPES], _unpack(small_sum, [s for _, s in _SMALL_SHAPES])))
    for n, _ in _SMALL_SHAPES[:_N_REPLICATED]:
        grads[n] = small_full[n]
    for n, axis, width in _SHARDED_SMALL:
        grads[n] = lax.dynamic_slice_in_dim(small_full[n], chip * width, width, axis)
    grads = {n: grads[n].reshape(w[n].shape) for n in names}

    delta, new_m, new_v = {}, {}, {}
    for n in names:
        shape = (1,) + w[n].shape if w[n].ndim == 1 else w[n].shape
        outs = adamw(*[d[n].reshape(shape) for d in (w, grads, m, v)], f"adamw_{n}")
        delta[n], new_m[n], new_v[n] = (o.reshape(w[n].shape) for o in outs)

    return (loss, dx[None], *[grads[n] for n in names], *[delta[n] for n in names],
            *[new_m[n] for n in names], *[new_v[n] for n in names])
```

```python
import functools

import jax
import jax.numpy as jnp
from jax import lax
from jax.experimental import pallas as pl
from jax.experimental.pallas import tpu as pltpu

f32 = jnp.float32
bf16 = jnp.bfloat16
MXU_DTYPE = bf16

RMS_EPS = 1e-5
LN_EPS = 1e-5
D_MODEL = 1024
D_FF = 4096
CH = 128
N_BLK = 8
SSM_HEADS = 16
IN_EVEN = 5136
NP_IN = 5376
OFF_U, OFF_V, OFF_Z, OFF_X, OFF_DT = 0, 1024, 2048, 3072, 5120
XBC_BLKS = 16
QKV_DIM = 1280
ATT_SCALE = 64 ** -0.5

ADAM_LR = 0.001
ADAM_B1 = 0.9
ADAM_B2 = 0.999
ADAM_EPS = 1e-08
ADAM_WD = 0.01
ADAM_STEP = 10

VMEM_LIMIT_BYTES = 48 * 1024 * 1024
N_CHIPS = 4
SMALL_ROWS = 256

NN = ((1,), (0,))
NT = ((1,), (1,))
TN = ((0,), (0,))


def _mm(a, b, dims):
    return lax.dot_general(a.astype(MXU_DTYPE), b.astype(MXU_DTYPE), (dims, ((), ())),
                           preferred_element_type=f32)


def _mm_exact(a, b):
    return jnp.dot(a, b, preferred_element_type=f32, precision=lax.Precision.HIGHEST)


def _cparams(sem=None):
    return pltpu.CompilerParams(dimension_semantics=sem, vmem_limit_bytes=VMEM_LIMIT_BYTES)


@jax.custom_vjp
def _swap64(x):
    return pltpu.roll(x, 64, axis=1)


_swap64.defvjp(lambda x: (pltpu.roll(x, 64, axis=1), None), lambda _, g: (pltpu.roll(g, 64, axis=1),))


def _make_delay(k):
    @jax.custom_vjp
    def delay(ext):
        return pltpu.roll(ext, k, axis=0)[8:, :]

    def fwd(ext):
        return delay(ext), None

    def bwd(_, g):
        gp = jnp.concatenate([jnp.zeros((8, g.shape[1]), g.dtype), g], axis=0)
        return (pltpu.roll(gp, gp.shape[0] - k, axis=0),)

    delay.defvjp(fwd, bwd)
    return delay


_DELAYS = {k: _make_delay(k) for k in (1, 2, 3)}


def _col(m, lane, h):
    return jnp.sum(jnp.where(lane == h, m, 0.0), axis=1, keepdims=True)


def _row(m, sub, h):
    return jnp.sum(jnp.where(sub == h, m, 0.0), axis=0, keepdims=True)


def _mixer_chunk(us, vs, zs, xbcs, halos, dtblk, hps, prm):
    lane = lax.broadcasted_iota(jnp.int32, (CH, CH), 1)
    sub = lax.broadcasted_iota(jnp.int32, (CH, CH), 0)
    left = lane < 64
    top = sub < 64
    causal = sub >= lane

    gus = [jax.nn.gelu(u) for u in us]
    gvs = [jax.nn.gelu(v) for v in vs]
    mu = sum(jnp.sum(g, axis=1, keepdims=True) for g in gvs) / D_MODEL
    cen = [g - mu for g in gvs]
    var = sum(jnp.sum(c * c, axis=1, keepdims=True) for c in cen) / D_MODEL
    rstd = lax.rsqrt(var + LN_EPS)
    a_out = []
    for g in range(N_BLK):
        vn = cen[g] * rstd * prm["ln_g"][g] + prm["ln_b"][g]
        w = jnp.where(causal, prm["wm"][g], 0.0)
        mixed = _mm(w, vn, NN) + _col(prm["bs_t"], lane, g)
        a_out.append(gus[g] * mixed)

    act = []
    for b in range(XBC_BLKS):
        w8 = prm["conv_w"][b]
        sub8 = lax.broadcasted_iota(jnp.int32, w8.shape, 0)
        ext = jnp.concatenate([halos[b], xbcs[b]], axis=0)
        conv = xbcs[b] * _row(w8, sub8, 3) + prm["conv_b"][b]
        for k in (1, 2, 3):
            conv = conv + _DELAYS[k](ext) * _row(w8, sub8, 3 - k)
        act.append(jax.nn.silu(conv))

    dt = jax.nn.softplus(dtblk + prm["dt_bias"])
    a_neg = -jnp.exp(prm["a_log"])
    tri = causal.astype(f32)
    acum = _mm_exact(tri, dt * a_neg)
    acum_t = acum.T
    dt_t = dt.T
    last = sub == CH - 1
    ys, h_out = [], []
    for grp in range(4):
        bm = act[8 + grp]
        cm = act[12 + grp]
        cb = _mm(cm, bm, NT)
        for p in (2 * grp, 2 * grp + 1):
            h0, h1 = 2 * p, 2 * p + 1
            xp = act[p]
            hp = hps[p]
            wis = []
            for h in (h0, h1):
                seg = _col(acum, lane, h) - _row(acum_t, sub, h)
                decay = jnp.exp(jnp.where(causal, seg, -jnp.inf))
                wis.append(cb * decay * _row(dt_t, sub, h))
            wcat = jnp.concatenate(wis, axis=1)
            xbd = jnp.concatenate([jnp.where(left, xp, 0.0), jnp.where(left, 0.0, xp)], axis=0)
            y_diag = _mm(wcat, xbd, NN)
            a_end = [jnp.sum(jnp.where(last & (lane == h), acum, 0.0), keepdims=True) for h in (h0, h1)]
            a_col = jnp.where(left, _col(acum, lane, h0), _col(acum, lane, h1))
            dt_col = jnp.where(left, _col(dt, lane, h0), _col(dt, lane, h1))
            to_end = jnp.exp(jnp.where(left, a_end[0], a_end[1]) - a_col) * dt_col
            states = _mm(xp * to_end, bm, TN)
            chunk_decay = jnp.where(top, jnp.exp(a_end[0]), jnp.exp(a_end[1]))
            h_out.append(chunk_decay * hp + states)
            y_off = jnp.exp(a_col) * _mm(cm, hp, NT)
            d_skip = jnp.where(left[:1], _col(prm["d_heads"], lane[:1], h0), _col(prm["d_heads"], lane[:1], h1))
            ys.append((y_diag + y_off + xp * d_skip) * jax.nn.silu(zs[p]))

    b_out = []
    for grp in range(4):
        pair = (ys[2 * grp], ys[2 * grp + 1])
        ms = sum(jnp.sum(y * y, axis=1, keepdims=True) for y in pair) / 256.0
        r = lax.rsqrt(ms + RMS_EPS)
        for j, y in enumerate(pair):
            b_out.append(y * r * prm["norm_g"][2 * grp + j])
    return a_out, b_out, h_out


def _attn_block(qps, kprev, kcur, vprev, vcur, sink_row, first):
    lane = lax.broadcasted_iota(jnp.int32, (CH, CH), 1)
    left = lane < 64
    kband = jnp.concatenate([kprev, kcur], axis=0)
    vband = jnp.concatenate([vprev, vcur], axis=0)
    left2 = lax.broadcasted_iota(jnp.int32, kband.shape, 1) < 64
    ksw, vsw = _swap64(kband), _swap64(vband)
    kdup = [jnp.where(left2, kband, ksw), jnp.where(left2, ksw, kband)]
    vdup = [jnp.where(left2, vband, vsw), jnp.where(left2, vsw, vband)]
    qi = lax.broadcasted_iota(jnp.int32, (CH, 2 * CH), 0)
    si = lax.broadcasted_iota(jnp.int32, (CH, 2 * CH), 1)
    rel = qi + CH - si
    valid = (rel >= 0) & (rel < CH) & (jnp.logical_not(first) | (si >= CH))
    outs = []
    for p in range(N_BLK):
        j = p // 4
        halves = []
        for side, h in ((0, 2 * p), (1, 2 * p + 1)):
            qh = jnp.where(left, qps[p], 0.0) if side == 0 else jnp.where(left, 0.0, qps[p])
            s = _mm(qh, kdup[j], NT) * ATT_SCALE
            s = jnp.where(valid, s, -jnp.inf)
            sink = _col(sink_row, lane[:1], h)
            m = lax.stop_gradient(jnp.maximum(jnp.max(s, axis=1, keepdims=True), sink))
            pexp = jnp.exp(s - m)
            denom = jnp.sum(pexp, axis=1, keepdims=True) + jnp.exp(sink - m)
            halves.append(_mm(pexp / denom, vdup[j], NN))
        outs.append(jnp.where(left, halves[0], halves[1]))
    return outs


def _rmsnorm(x, g):
    r = lax.rsqrt(jnp.mean(x * x, axis=-1, keepdims=True) + RMS_EPS)
    return x * r * g


def rmsnorm_fwd(x, g_row, name):
    s, d = x.shape
    tm = min(512, s)

    def body(x_ref, g_ref, y_ref):
        y_ref[...] = _rmsnorm(x_ref[...], g_ref[...]).astype(bf16)

    return pl.pallas_call(
        body, name=name, grid=(s // tm,),
        in_specs=[pl.BlockSpec((tm, d), lambda i: (i, 0)), pl.BlockSpec((1, d), lambda i: (0, 0))],
        out_specs=pl.BlockSpec((tm, d), lambda i: (i, 0)),
        out_shape=jax.ShapeDtypeStruct((s, d), bf16),
        compiler_params=_cparams(("parallel",)),
    )(x, g_row)


def rmsnorm_bwd(x, g_row, dy, res, name):
    s, d = x.shape
    tm = min(512, s)

    def body(x_ref, g_ref, dy_ref, res_ref, dx_ref, dg_ref):
        @pl.when(pl.program_id(0) == 0)
        def _():
            dg_ref[...] = jnp.zeros_like(dg_ref)

        _, vjp = jax.vjp(_rmsnorm, x_ref[...], g_ref[...])
        dx, dg = vjp(dy_ref[...])
        dx_ref[...] = res_ref[...] + dx
        dg_ref[...] += dg

    tile = pl.BlockSpec((tm, d), lambda i: (i, 0))
    row = pl.BlockSpec((1, d), lambda i: (0, 0))
    return pl.pallas_call(
        body, name=name, grid=(s // tm,),
        in_specs=[tile, row, tile, tile], out_specs=[tile, row],
        out_shape=[jax.ShapeDtypeStruct((s, d), f32), jax.ShapeDtypeStruct((1, d), f32)],
        compiler_params=_cparams(("arbitrary",)),
    )(x, g_row, dy, res)


def final_loss(h, g_row, target, name):
    s, d = h.shape
    tm = min(512, s)

    def body(h_ref, g_ref, t_ref, loss_ref, dh_ref, dg_ref):
        @pl.when(pl.program_id(0) == 0)
        def _():
            dg_ref[...] = jnp.zeros_like(dg_ref)
            loss_ref[...] = jnp.zeros_like(loss_ref)

        def f(hv, gv):
            err = jnp.square(_rmsnorm(hv, gv) - t_ref[...])
            return 0.5 * jnp.sum(jnp.mean(err, axis=-1, keepdims=True), axis=0, keepdims=True)

        loss, vjp = jax.vjp(f, h_ref[...], g_ref[...])
        dh, dg = vjp(jnp.ones_like(loss))
        dh_ref[...] = dh
        dg_ref[...] += dg
        loss_ref[...] += jnp.broadcast_to(loss, loss_ref.shape)

    tile = pl.BlockSpec((tm, d), lambda i: (i, 0))
    row = pl.BlockSpec((1, d), lambda i: (0, 0))
    return pl.pallas_call(
        body, name=name, grid=(s // tm,),
        in_specs=[tile, row, tile],
        out_specs=[pl.BlockSpec((1, 128), lambda i: (0, 0)), tile, row],
        out_shape=[jax.ShapeDtypeStruct((1, 128), f32), jax.ShapeDtypeStruct((s, d), f32),
                   jax.ShapeDtypeStruct((1, d), f32)],
        compiler_params=_cparams(("arbitrary",)),
    )(h, g_row, target)


def colsum(x, name):
    s, n = x.shape
    tm = min(512, s)

    def body(x_ref, o_ref):
        @pl.when(pl.program_id(0) == 0)
        def _():
            o_ref[...] = jnp.zeros_like(o_ref)

        o_ref[...] += jnp.sum(x_ref[...].astype(f32), axis=0, keepdims=True)

    return pl.pallas_call(
        body, name=name, grid=(s // tm,),
        in_specs=[pl.BlockSpec((tm, n), lambda i: (i, 0))],
        out_specs=pl.BlockSpec((1, n), lambda i: (0, 0)),
        out_shape=jax.ShapeDtypeStruct((1, n), f32),
        compiler_params=_cparams(("arbitrary",)),
    )(x)


def _fit(dim, want):
    if dim <= want:
        return dim
    t = want
    while dim % t:
        t -= 128
    return t


def matmul(a, b, *, dims, name, out_dtype=f32, tm=1024, tn=512, tk=8192, a_pro=None, epi=None, epi_args=(),
           out_by_col_tile=False, after=None):
    if dims == "nn":
        (m, k), n = a.shape, b.shape[1]
    elif dims == "nt":
        (m, k), n = a.shape, b.shape[0]
    else:
        (k, m), n = a.shape, b.shape[1]
    tm, tn, tk = _fit(m, tm), _fit(n, tn), _fit(k, tk)
    nk = k // tk
    if dims == "nn":
        a_spec = pl.BlockSpec((tm, tk), lambda i, j, kk: (i, kk))
        b_spec = pl.BlockSpec((tk, tn), lambda i, j, kk: (kk, j))
        dn = NN
    elif dims == "nt":
        a_spec = pl.BlockSpec((tm, tk), lambda i, j, kk: (i, kk))
        b_spec = pl.BlockSpec((tn, tk), lambda i, j, kk: (j, kk))
        dn = NT
    else:
        a_spec = pl.BlockSpec((tk, tm), lambda i, j, kk: (kk, i))
        b_spec = pl.BlockSpec((tk, tn), lambda i, j, kk: (kk, j))
        dn = TN
    e_specs = [pl.BlockSpec((tm, tn), lambda i, j, kk: (i, j)) if kind == "tile"
               else pl.BlockSpec((1, tn), lambda i, j, kk: (0, j)) for kind, _ in epi_args]
    n_epi = len(epi_args)
    order_specs = [] if after is None else [pl.BlockSpec((8, 128), lambda i, j, kk: (0, 0))]
    order_args = [] if after is None else [after]

    def body(*refs):
        a_ref, b_ref = refs[0], refs[1]
        e_refs = refs[2:2 + n_epi]
        n_in = 2 + n_epi + len(order_args)
        o_ref = refs[n_in]
        av = a_ref[...]
        if a_pro is not None:
            av = a_pro(av)
        part = _mm(av, b_ref[...], dn)

        def finish(acc):
            if epi is not None:
                acc = epi(acc, *[r[...] for r in e_refs])
            o_ref[...] = acc.astype(out_dtype)

        if nk == 1:
            finish(part)
        else:
            acc_ref = refs[n_in + 1]
            kk = pl.program_id(2)

            @pl.when(kk == 0)
            def _():
                acc_ref[...] = part

            @pl.when(kk > 0)
            def _():
                acc_ref[...] += part

            @pl.when(kk == nk - 1)
            def _():
                finish(acc_ref[...])

    if out_by_col_tile:
        out_spec = pl.BlockSpec((None, tm, tn), lambda i, j, kk: (j, i, 0))
        out_shape = jax.ShapeDtypeStruct((n // tn, m, tn), out_dtype)
    else:
        out_spec = pl.BlockSpec((tm, tn), lambda i, j, kk: (i, j))
        out_shape = jax.ShapeDtypeStruct((m, n), out_dtype)
    return pl.pallas_call(
        body, name=name, grid=(m // tm, n // tn, nk),
        in_specs=[a_spec, b_spec] + e_specs + order_specs,
        out_specs=out_spec,
        out_shape=out_shape,
        scratch_shapes=[pltpu.VMEM((tm, tn), f32)] if nk > 1 else [],
        compiler_params=_cparams(("parallel", "parallel", "arbitrary")),
    )(a, b, *[arr for _, arr in epi_args], *order_args)


def _relu2(a):
    r = jnp.maximum(a.astype(f32), 0.0)
    return r * r


def _add(acc, t):
    return acc + t


def _add_bias(acc, t):
    return acc + t


def _add_bias_res(acc, bias, res):
    return acc + bias + res


def _times_relu2_grad(acc, a):
    return acc * (2.0 * jnp.maximum(a.astype(f32), 0.0))


_MIXER_PARAM_SHAPES = (
    ("ln_g", (1, D_MODEL)), ("ln_b", (1, D_MODEL)), ("wm", (N_BLK, CH, CH)), ("bs_t", (CH, CH)),
    ("conv_w", (8, 2048)), ("conv_b", (1, 2048)), ("dt_bias", (1, CH)), ("a_log", (1, CH)),
    ("d_heads", (1, CH)), ("norm_g", (1, D_MODEL)),
)


def _blocks(v, n, off=0):
    return [v[:, off + i * CH: off + (i + 1) * CH] for i in range(n)]


def _split_mixer_params(vals):
    p = dict(vals)
    return {
        "ln_g": _blocks(p["ln_g"], N_BLK), "ln_b": _blocks(p["ln_b"], N_BLK),
        "wm": [p["wm"][g] for g in range(N_BLK)], "bs_t": p["bs_t"],
        "conv_w": _blocks(p["conv_w"], XBC_BLKS), "conv_b": _blocks(p["conv_b"], XBC_BLKS),
        "dt_bias": p["dt_bias"], "a_log": p["a_log"], "d_heads": p["d_heads"],
        "norm_g": _blocks(p["norm_g"], N_BLK),
    }


def _mixer_leaves(proj_ref, halo_ref, keep_halo):
    pv = proj_ref
    us = [pv[:, OFF_U + i * CH: OFF_U + (i + 1) * CH] for i in range(N_BLK)]
    vs = [pv[:, OFF_V + i * CH: OFF_V + (i + 1) * CH] for i in range(N_BLK)]
    zs = [pv[:, OFF_Z + i * CH: OFF_Z + (i + 1) * CH] for i in range(N_BLK)]
    xbcs = [pv[:, OFF_X + i * CH: OFF_X + (i + 1) * CH] for i in range(XBC_BLKS)]
    halos = [halo_ref[:, OFF_X + i * CH: OFF_X + (i + 1) * CH] * keep_halo for i in range(XBC_BLKS)]
    dtblk = pv[:, OFF_DT: OFF_DT + CH]
    return us, vs, zs, xbcs, halos, dtblk


def mixer_fwd(proj, prm):
    s = proj.shape[0]
    nc = s // CH
    names = [n for n, _ in _MIXER_PARAM_SHAPES]

    def body(proj_ref, halo_ref, *rest):
        p_refs = rest[:len(names)]
        ab_ref, hs_ref, h_ref = rest[len(names):]
        c = pl.program_id(0)

        @pl.when(c == 0)
        def _():
            h_ref[...] = jnp.zeros_like(h_ref)

        hs_ref[...] = h_ref[...]
        keep = (c > 0).astype(f32)
        us, vs, zs, xbcs, halos, dtblk = _mixer_leaves(proj_ref, halo_ref, keep)
        hps = [h_ref[i * CH:(i + 1) * CH, :] for i in range(N_BLK)]
        p = _split_mixer_params({n: r[...] for n, r in zip(names, p_refs)})
        a_out, b_out, h_out = _mixer_chunk(us, vs, zs, xbcs, halos, dtblk, hps, p)
        for i in range(N_BLK):
            ab_ref[:, i * CH:(i + 1) * CH] = a_out[i].astype(bf16)
            ab_ref[:, D_MODEL + i * CH: D_MODEL + (i + 1) * CH] = b_out[i].astype(bf16)
            h_ref[i * CH:(i + 1) * CH, :] = h_out[i]

    def const(shape):
        return pl.BlockSpec(shape, lambda c: (0,) * len(shape))

    return pl.pallas_call(
        body, name="mixer_fwd", grid=(nc,),
        in_specs=[pl.BlockSpec((CH, NP_IN), lambda c: (c, 0)),
                  pl.BlockSpec((8, NP_IN), lambda c: (jnp.maximum(c * (CH // 8) - 1, 0), 0))]
                 + [const(shp) for _, shp in _MIXER_PARAM_SHAPES],
        out_specs=[pl.BlockSpec((CH, 2 * D_MODEL), lambda c: (c, 0)),
                   pl.BlockSpec((None, D_MODEL, CH), lambda c: (c, 0, 0))],
        out_shape=[jax.ShapeDtypeStruct((s, 2 * D_MODEL), bf16), jax.ShapeDtypeStruct((nc, D_MODEL, CH), f32)],
        scratch_shapes=[pltpu.VMEM((D_MODEL, CH), f32)],
        compiler_params=_cparams(("arbitrary",)),
    )(proj, proj, *[prm[n] for n in names])


def mixer_bwd(proj, hstates, dab, prm):
    s = proj.shape[0]
    nc = s // CH
    names = [n for n, _ in _MIXER_PARAM_SHAPES]
    npar = len(names)

    def body(proj_ref, halo_ref, hs_ref, dab_ref, *rest):
        p_refs = rest[:npar]
        dproj_ref = rest[npar]
        g_refs = rest[npar + 1: 2 * npar + 1]
        dh_ref, dhalo_ref = rest[2 * npar + 1:]
        i = pl.program_id(0)
        c = nc - 1 - i

        @pl.when(i == 0)
        def _():
            dh_ref[...] = jnp.zeros_like(dh_ref)
            dhalo_ref[...] = jnp.zeros_like(dhalo_ref)
            for r in g_refs:
                r[...] = jnp.zeros_like(r)

        keep = (c > 0).astype(f32)
        us, vs, zs, xbcs, halos, dtblk = _mixer_leaves(proj_ref, halo_ref, keep)
        hps = [hs_ref[j * CH:(j + 1) * CH, :] for j in range(N_BLK)]
        pvals = {n: r[...] for n, r in zip(names, p_refs)}

        def fn(us, vs, zs, xbcs, halos, dtblk, hps, pvals):
            return _mixer_chunk(us, vs, zs, xbcs, halos, dtblk, hps, _split_mixer_params(pvals))

        _, vjp = jax.vjp(fn, us, vs, zs, xbcs, halos, dtblk, hps, pvals)
        da = [dab_ref[:, j * CH:(j + 1) * CH].astype(f32) for j in range(N_BLK)]
        db = [dab_ref[:, D_MODEL + j * CH: D_MODEL + (j + 1) * CH].astype(f32) for j in range(N_BLK)]
        dh = [dh_ref[j * CH:(j + 1) * CH, :] for j in range(N_BLK)]
        dus, dvs, dzs, dxbcs, dhalos, ddt, dhps, dp = vjp((da, db, dh))

        for j in range(N_BLK):
            dproj_ref[:, OFF_U + j * CH: OFF_U + (j + 1) * CH] = dus[j].astype(bf16)
            dproj_ref[:, OFF_V + j * CH: OFF_V + (j + 1) * CH] = dvs[j].astype(bf16)
            dproj_ref[:, OFF_Z + j * CH: OFF_Z + (j + 1) * CH] = dzs[j].astype(bf16)
            dh_ref[j * CH:(j + 1) * CH, :] = dhps[j]
        zeros_top = jnp.zeros((CH - 8, CH), f32)
        for j in range(XBC_BLKS):
            late = jnp.concatenate([zeros_top, dhalo_ref[:, j * CH:(j + 1) * CH]], axis=0)
            dproj_ref[:, OFF_X + j * CH: OFF_X + (j + 1) * CH] = (dxbcs[j] + late).astype(bf16)
        for j in range(XBC_BLKS):
            dhalo_ref[:, j * CH:(j + 1) * CH] = dhalos[j] * keep
        lane = lax.broadcasted_iota(jnp.int32, (CH, CH), 1)
        dproj_ref[:, OFF_DT: OFF_DT + CH] = jnp.where(lane < SSM_HEADS, ddt, 0.0).astype(bf16)
        dproj_ref[:, OFF_DT + CH:] = jnp.zeros((CH, NP_IN - OFF_DT - CH), bf16)
        for n, r in zip(names, g_refs):
            r[...] += dp[n]

    def const(shape):
        return pl.BlockSpec(shape, lambda i: (0,) * len(shape))

    outs = pl.pallas_call(
        body, name="mixer_bwd", grid=(nc,),
        in_specs=[pl.BlockSpec((CH, NP_IN), lambda i: (nc - 1 - i, 0)),
                  pl.BlockSpec((8, NP_IN), lambda i: (jnp.maximum((nc - 1 - i) * (CH // 8) - 1, 0), 0)),
                  pl.BlockSpec((None, D_MODEL, CH), lambda i: (nc - 1 - i, 0, 0)),
                  pl.BlockSpec((CH, 2 * D_MODEL), lambda i: (nc - 1 - i, 0))]
                 + [const(shp) for _, shp in _MIXER_PARAM_SHAPES],
        out_specs=[pl.BlockSpec((CH, NP_IN), lambda i: (nc - 1 - i, 0))]
                  + [const(shp) for _, shp in _MIXER_PARAM_SHAPES],
        out_shape=[jax.ShapeDtypeStruct((s, NP_IN), bf16)]
                  + [jax.ShapeDtypeStruct(shp, f32) for _, shp in _MIXER_PARAM_SHAPES],
        scratch_shapes=[pltpu.VMEM((D_MODEL, CH), f32), pltpu.VMEM((8, 2048), f32)],
        compiler_params=_cparams(("arbitrary",)),
    )(proj, proj, hstates, dab, *[prm[n] for n in names])
    return outs[0], dict(zip(names, outs[1:]))


_K_BLK = D_MODEL // CH
_V_BLK = _K_BLK + 1


def _attn_specs(rev, nb):
    def blk(i):
        return nb - 1 - i if rev else i

    q_spec = pl.BlockSpec((CH, D_MODEL), lambda i: (blk(i), 0))
    kv = lambda col, prev: pl.BlockSpec(
        (CH, CH), lambda i: (jnp.maximum(blk(i) - 1, 0) if prev else blk(i), col))
    return q_spec, [kv(_K_BLK, True), kv(_K_BLK, False), kv(_V_BLK, True), kv(_V_BLK, False)]


def attn_fwd(qkv, sink_row):
    s = qkv.shape[0]
    nb = s // CH

    def body(q_ref, kp_ref, kc_ref, vp_ref, vc_ref, sink_ref, o_ref):
        qps = [q_ref[:, p * CH:(p + 1) * CH] for p in range(N_BLK)]
        outs = _attn_block(qps, kp_ref[...], kc_ref[...], vp_ref[...], vc_ref[...], sink_ref[...],
                           pl.program_id(0) == 0)
        for p in range(N_BLK):
            o_ref[:, p * CH:(p + 1) * CH] = outs[p].astype(bf16)

    q_spec, kv_specs = _attn_specs(False, nb)
    return pl.pallas_call(
        body, name="attn_fwd", grid=(nb,),
        in_specs=[q_spec] + kv_specs + [pl.BlockSpec((1, CH), lambda i: (0, 0))],
        out_specs=pl.BlockSpec((CH, D_MODEL), lambda i: (i, 0)),
        out_shape=jax.ShapeDtypeStruct((s, D_MODEL), bf16),
        compiler_params=_cparams(("parallel",)),
    )(qkv, qkv, qkv, qkv, qkv, sink_row)


def attn_bwd(qkv, sink_row, dout):
    s = qkv.shape[0]
    nb = s // CH

    def body(q_ref, kp_ref, kc_ref, vp_ref, vc_ref, sink_ref, do_ref, dqkv_ref, dsink_ref, carry_ref):
        i = pl.program_id(0)
        blk = nb - 1 - i

        @pl.when(i == 0)
        def _():
            dsink_ref[...] = jnp.zeros_like(dsink_ref)
            carry_ref[...] = jnp.zeros_like(carry_ref)

        qps = [q_ref[:, p * CH:(p + 1) * CH] for p in range(N_BLK)]
        first = blk == 0
        _, vjp = jax.vjp(lambda *a: _attn_block(*a, first), qps, kp_ref[...], kc_ref[...], vp_ref[...],
                         vc_ref[...], sink_ref[...])
        dos = [do_ref[:, p * CH:(p + 1) * CH].astype(f32) for p in range(N_BLK)]
        dqs, dkp, dkc, dvp, dvc, dsink = vjp(dos)
        for p in range(N_BLK):
            dqkv_ref[:, p * CH:(p + 1) * CH] = dqs[p].astype(bf16)
        dqkv_ref[:, D_MODEL: D_MODEL + CH] = (dkc + carry_ref[0]).astype(bf16)
        dqkv_ref[:, D_MODEL + CH:] = (dvc + carry_ref[1]).astype(bf16)
        keep = jnp.logical_not(first).astype(f32)
        carry_ref[0] = dkp * keep
        carry_ref[1] = dvp * keep
        dsink_ref[...] += dsink

    q_spec, kv_specs = _attn_specs(True, nb)
    return pl.pallas_call(
        body, name="attn_bwd", grid=(nb,),
        in_specs=[q_spec] + kv_specs + [pl.BlockSpec((1, CH), lambda i: (0, 0)),
                                        pl.BlockSpec((CH, D_MODEL), lambda i: (nb - 1 - i, 0))],
        out_specs=[pl.BlockSpec((CH, QKV_DIM), lambda i: (nb - 1 - i, 0)), pl.BlockSpec((1, CH), lambda i: (0, 0))],
        out_shape=[jax.ShapeDtypeStruct((s, QKV_DIM), bf16), jax.ShapeDtypeStruct((1, CH), f32)],
        scratch_shapes=[pltpu.VMEM((2, CH, CH), f32)],
        compiler_params=_cparams(("arbitrary",)),
    )(qkv, qkv, qkv, qkv, qkv, sink_row, dout)


def adamw(w, g, m, v, name):
    def body(w_ref, g_ref, m_ref, v_ref, d_ref, nm_ref, nv_ref):
        gv = g_ref[...]
        nm = ADAM_B1 * m_ref[...] + (1.0 - ADAM_B1) * gv
        nv = ADAM_B2 * v_ref[...] + (1.0 - ADAM_B2) * jnp.square(gv)
        m_hat = nm / (1.0 - ADAM_B1 ** ADAM_STEP)
        v_hat = nv / (1.0 - ADAM_B2 ** ADAM_STEP)
        d_ref[...] = -ADAM_LR * (m_hat / (jnp.sqrt(v_hat) + ADAM_EPS) + ADAM_WD * w_ref[...])
        nm_ref[...] = nm
        nv_ref[...] = nv

    out_shape = [jax.ShapeDtypeStruct(w.shape, f32)] * 3
    if w.ndim == 3 and w.shape[1] % 256 == 0:
        tile = pl.BlockSpec((None, 256, w.shape[2]), lambda l, i: (l, i, 0))
        return pl.pallas_call(
            body, name=name, grid=(w.shape[0], w.shape[1] // 256),
            in_specs=[tile] * 4, out_specs=[tile] * 3, out_shape=out_shape,
            compiler_params=_cparams(("parallel", "parallel")),
        )(w, g, m, v)
    return pl.pallas_call(body, name=name, in_specs=[_VMEM] * 4, out_specs=[_VMEM] * 3, out_shape=out_shape,
                          compiler_params=_cparams())(w, g, m, v)


_MESH = pl.DeviceIdType.MESH
_ANY = pl.BlockSpec(memory_space=pl.ANY)
_VMEM = pl.BlockSpec(memory_space=pltpu.VMEM)


def _place():
    x, y, c = lax.axis_index("x"), lax.axis_index("y"), lax.axis_index("c")
    chips = [(1 - x, y), (x, 1 - y), (1 - x, 1 - y)]
    return x, y, c, 2 * x + y, chips, [2 * cx + cy for cx, cy in chips]


def _half(c, rows):
    return pl.ds(pl.multiple_of(c * (rows // 2), 16), rows // 2)


def _step_rows(rows):
    return max(t for t in range(16, 641, 16) if rows % t == 0)


def place_shard(b, slot, name):
    r, c = b.shape
    tr = _step_rows(r)

    def body(slot_ref, b_ref, o_ref):
        o_ref[...] = b_ref[...]

    return pl.pallas_call(
        body, name=name,
        grid_spec=pltpu.PrefetchScalarGridSpec(
            num_scalar_prefetch=1, grid=(r // tr,),
            in_specs=[pl.BlockSpec((tr, c), lambda i, s: (i, 0))],
            out_specs=pl.BlockSpec((None, tr, c), lambda i, s: (s[0], i, 0))),
        out_shape=jax.ShapeDtypeStruct((N_CHIPS, r, c), b.dtype),
        compiler_params=_cparams(("parallel",)),
    )(slot, b)


_HBM = pl.BlockSpec(memory_space=pltpu.HBM)
_SEM = pl.BlockSpec(memory_space=pltpu.SEMAPHORE)
_EFFECT = pltpu.SideEffectType.DATAFLOW_SIDE_EFFECTING


def _gather_ici_copies(bufs, send_sems, recv_sems):
    x, y, c, me, chips, chip_idx = _place()
    return [pltpu.make_async_remote_copy(
        src_ref=buf.at[me, _half(c, buf.shape[1])], dst_ref=buf.at[chip_idx[j], _half(c, buf.shape[1])],
        send_sem=send_sems.at[3 * k + j], recv_sem=recv_sems.at[3 * k + j],
        device_id=(*chips[j], c), device_id_type=_MESH) for j in range(3) for k, buf in enumerate(bufs)]


def gather_start(groups):
    sizes = [len(g) for g in groups]
    flat = [b for g in groups for b in g]
    n = len(flat)

    def body(*refs):
        bufs, sems = refs[:n], refs[n:n + 2 * len(groups)]
        x, y, c, me, chips, chip_idx = _place()
        lo = 0
        for gi, size in enumerate(sizes):
            for j in range(3):
                for k, buf in enumerate(bufs[lo:lo + size]):
                    mine = buf.at[me, _half(c, buf.shape[1])]
                    pltpu.make_async_remote_copy(
                        src_ref=mine, dst_ref=mine, send_sem=sems[2 * gi].at[3 * k + j],
                        recv_sem=sems[2 * gi + 1].at[3 * k + j], device_id=(*chips[j], c),
                        device_id_type=_MESH).start()
            lo += size

    sem_shapes = [pltpu.SemaphoreType.DMA((3 * size,)) for size in sizes for _ in range(2)]
    outs = pl.pallas_call(
        body, name="gather_start",
        out_shape=(*sem_shapes, *[pltpu.HBM(b.shape, b.dtype) for b in flat]),
        in_specs=[_HBM] * n, out_specs=(*[_SEM] * len(sem_shapes), *[_HBM] * n),
        input_output_aliases={i: len(sem_shapes) + i for i in range(n)},
        compiler_params=pltpu.CompilerParams(has_side_effects=_EFFECT),
    )(*[pltpu.with_memory_space_constraint(b, pltpu.HBM) for b in flat])
    sems = [(outs[2 * gi], outs[2 * gi + 1]) for gi in range(len(groups))]
    thru, lo = [], len(sem_shapes)
    for size in sizes:
        thru.append(list(outs[lo:lo + size]))
        lo += size
    return sems, thru


def gather_wait(bufs, sems, after, tag):
    n = len(bufs)

    def body(*refs):
        for cp in _gather_ici_copies(refs[:n], refs[n], refs[n + 1]):
            cp.wait_send()
            cp.wait_recv()

    extra = [] if after is None else [after]
    return list(pl.pallas_call(
        body, name=f"gather_wait_{tag}",
        out_shape=[pltpu.HBM(b.shape, b.dtype) for b in bufs],
        in_specs=[_HBM] * n + [_SEM, _SEM] + [_ANY] * len(extra), out_specs=[_HBM] * n,
        input_output_aliases={i: i for i in range(n)},
        compiler_params=pltpu.CompilerParams(has_side_effects=_EFFECT),
    )(*bufs, *sems, *extra))


def gather_forward(bufs, tag):
    n = len(bufs)

    def body(*refs):
        out_refs = refs[n:2 * n]
        send_sems, recv_sems = refs[2 * n:]
        x, y, c, me, chips, chip_idx = _place()

        def copy(k, j, half):
            part = out_refs[k].at[chip_idx[j], _half(half, out_refs[k].shape[1])]
            return pltpu.make_async_remote_copy(
                src_ref=part, dst_ref=part, send_sem=send_sems.at[3 * k + j], recv_sem=recv_sems.at[3 * k + j],
                device_id=(x, y, 1 - c), device_id_type=_MESH)

        sends = [copy(k, j, c) for j in range(3) for k in range(n)]
        for cp in sends:
            cp.start()
        for j in range(3):
            for k in range(n):
                copy(k, j, 1 - c).wait_recv()
        for cp in sends:
            cp.wait_send()

    return list(pl.pallas_call(
        body, name=f"gather_forward_{tag}",
        out_shape=[jax.ShapeDtypeStruct(b.shape, b.dtype) for b in bufs],
        in_specs=[_ANY] * n, out_specs=[_ANY] * n, input_output_aliases={i: i for i in range(n)},
        scratch_shapes=[pltpu.SemaphoreType.DMA((3 * n,)), pltpu.SemaphoreType.DMA((3 * n,))],
    )(*bufs))


def exchange_halves(bufs, tag):
    n = len(bufs)

    def body(*refs):
        g_refs, out_refs = refs[:n], refs[n:2 * n]
        send_sems, recv_sems = refs[2 * n:]
        x, y, c, *_ = _place()
        cps = [pltpu.make_async_remote_copy(
            src_ref=g_refs[b].at[:, _half(1 - c, g_refs[b].shape[1])], dst_ref=out_refs[b],
            send_sem=send_sems.at[b], recv_sem=recv_sems.at[b], device_id=(x, y, 1 - c), device_id_type=_MESH)
            for b in range(n)]
        for cp in cps:
            cp.start()
        for cp in cps:
            cp.wait()

    return pl.pallas_call(
        body, name=f"exchange_halves_{tag}",
        out_shape=[jax.ShapeDtypeStruct((N_CHIPS, b.shape[1] // 2, b.shape[2]), b.dtype) for b in bufs],
        in_specs=[_ANY] * n, out_specs=[_ANY] * n,
        scratch_shapes=[pltpu.SemaphoreType.DMA((n,)), pltpu.SemaphoreType.DMA((n,))],
    )(*bufs)


def add_halves(g, got, c_idx, name):
    hr, cols = got.shape[1], got.shape[2]
    tr = _step_rows(hr)
    steps = hr // tr

    def body(c_ref, g_ref, got_ref, o_ref):
        o_ref[...] = (g_ref[...].astype(f32) + got_ref[...].astype(f32)).astype(bf16)

    return pl.pallas_call(
        body, name=name,
        grid_spec=pltpu.PrefetchScalarGridSpec(
            num_scalar_prefetch=1, grid=(N_CHIPS, steps),
            in_specs=[pl.BlockSpec((None, tr, cols), lambda s, i, c: (s, c[0] * steps + i, 0)),
                      pl.BlockSpec((None, tr, cols), lambda s, i, c: (s, i, 0))],
            out_specs=pl.BlockSpec((None, tr, cols), lambda s, i, c: (s, i, 0))),
        out_shape=jax.ShapeDtypeStruct(got.shape, bf16),
        compiler_params=_cparams(("parallel", "parallel")),
    )(c_idx, g, got)


def sum_chips(t, got, place_idx, name):
    hr, cols = t.shape[1], t.shape[2]
    tr = _step_rows(hr)
    steps = hr // tr

    def body(idx_ref, t_ref, got_ref, o_ref):
        acc = t_ref[...].astype(f32)
        for j in range(3):
            acc = acc + got_ref[j].astype(f32)
        o_ref[...] = acc

    return pl.pallas_call(
        body, name=name,
        grid_spec=pltpu.PrefetchScalarGridSpec(
            num_scalar_prefetch=1, grid=(steps,),
            in_specs=[pl.BlockSpec((None, tr, cols), lambda i, idx: (idx[0], i, 0)),
                      pl.BlockSpec((3, tr, cols), lambda i, idx: (0, i, 0))],
            out_specs=pl.BlockSpec((tr, cols), lambda i, idx: (idx[1] * steps + i, 0))),
        out_shape=jax.ShapeDtypeStruct((2 * hr, cols), f32),
        compiler_params=_cparams(("parallel",)),
    )(place_idx, t, got)


def share_halves(bufs, tag):
    n = len(bufs)

    def body(*refs):
        out_refs = refs[n:2 * n]
        send_sems, recv_sems = refs[2 * n:]
        x, y, c, *_ = _place()

        def copy(b, half):
            part = out_refs[b].at[_half(half, out_refs[b].shape[0])]
            return pltpu.make_async_remote_copy(
                src_ref=part, dst_ref=part, send_sem=send_sems.at[b], recv_sem=recv_sems.at[b],
                device_id=(x, y, 1 - c), device_id_type=_MESH)

        for b in range(n):
            copy(b, c).start()
        for b in range(n):
            copy(b, 1 - c).wait_recv()
        for b in range(n):
            copy(b, c).wait_send()

    return pl.pallas_call(
        body, name=f"share_halves_{tag}",
        out_shape=[jax.ShapeDtypeStruct(b.shape, b.dtype) for b in bufs],
        in_specs=[_ANY] * n, out_specs=[_ANY] * n, input_output_aliases={i: i for i in range(n)},
        scratch_shapes=[pltpu.SemaphoreType.DMA((n,)), pltpu.SemaphoreType.DMA((n,))],
    )(*bufs)


def _scatter_copies(t_refs, land_refs, send_sems, recv_sems):
    x, y, c, me, chips, chip_idx = _place()
    return [pltpu.make_async_remote_copy(
        src_ref=t_refs[b].at[chip_idx[j]], dst_ref=land_refs[b].at[j], send_sem=send_sems.at[3 * b + j],
        recv_sem=recv_sems.at[3 * b + j], device_id=(*chips[j], c), device_id_type=_MESH)
        for j in range(3) for b in range(len(t_refs))]


def scatter_start(ts, tag):
    n = len(ts)
    lands = [lax.empty((3,) + t.shape[1:], t.dtype) for t in ts]

    def body(*refs):
        for cp in _scatter_copies(refs[:n], refs[n:2 * n], refs[2 * n], refs[2 * n + 1]):
            cp.start()
        token = refs[-1]
        token[...] = jnp.zeros_like(token)

    hbm = [pltpu.HBM(a.shape, a.dtype) for a in (*ts, *lands)]
    outs = pl.pallas_call(
        body, name=f"scatter_start_{tag}",
        out_shape=(pltpu.SemaphoreType.DMA((3 * n,)), pltpu.SemaphoreType.DMA((3 * n,)), *hbm,
                   jax.ShapeDtypeStruct((8, 128), f32)),
        in_specs=[_HBM] * (2 * n), out_specs=(_SEM, _SEM, *[_HBM] * (2 * n), _VMEM),
        input_output_aliases={i: 2 + i for i in range(2 * n)},
        compiler_params=pltpu.CompilerParams(has_side_effects=_EFFECT),
    )(*[pltpu.with_memory_space_constraint(a, pltpu.HBM) for a in (*ts, *lands)])
    return outs[0], outs[1], list(outs[2:2 + n]), list(outs[2 + n:2 + 2 * n]), outs[-1]


def scatter_wait(send_sems, recv_sems, ts, lands, after, tag):
    n = len(ts)

    def body(*refs):
        for cp in _scatter_copies(refs[:n], refs[n:2 * n], refs[2 * n], refs[2 * n + 1]):
            cp.wait_send()
            cp.wait_recv()

    outs = pl.pallas_call(
        body, name=f"scatter_wait_{tag}",
        out_shape=[pltpu.HBM(a.shape, a.dtype) for a in (*ts, *lands)],
        in_specs=[_HBM] * (2 * n) + [_SEM, _SEM, _ANY], out_specs=[_HBM] * (2 * n),
        input_output_aliases={i: i for i in range(2 * n)},
        compiler_params=pltpu.CompilerParams(has_side_effects=_EFFECT),
    )(*ts, *lands, send_sems, recv_sems, after)
    return list(outs[:n]), list(outs[n:])


class GradReducer:
    def __init__(self, c_idx, place_idx):
        self.c_idx, self.place_idx = c_idx, place_idx

    def start(self, bufs, tag):
        got = exchange_halves(bufs, tag)
        ts = [add_halves(b, g, self.c_idx, f"add_halves_{tag}{i}") for i, (b, g) in enumerate(zip(bufs, got))]
        send_sems, recv_sems, ts, lands, token = scatter_start(ts, tag)
        return (send_sems, recv_sems, ts, lands), token

    def finish(self, state, after, tag):
        ts, lands = scatter_wait(*state, after, tag)
        sums = [sum_chips(t, l, self.place_idx, f"sum_chips_{tag}{i}") for i, (t, l) in enumerate(zip(ts, lands))]
        return share_halves(sums, tag)


def allreduce_small(sp):
    def body(s_ref, out_ref, gather_ref, send_sems, recv_sems):
        x, y, c, me, chips, chip_idx = _place()
        sibling = (x, y, 1 - c)

        def copy(k, chip, core, to, src=None):
            dst = gather_ref.at[2 * chip + core]
            return pltpu.make_async_remote_copy(
                src_ref=dst if src is None else src, dst_ref=dst, send_sem=send_sems.at[k],
                recv_sem=recv_sems.at[k], device_id=to, device_id_type=_MESH)

        first = [copy(0, me, c, sibling, src=s_ref)]
        first += [copy(1 + j, me, c, (*chips[j], c), src=s_ref) for j in range(3)]
        for cp in first:
            cp.start()
        gather_ref[2 * me + c] = s_ref[...]
        passed = [copy(4 + j, chip_idx[j], c, sibling) for j in range(3)]
        for j in range(3):
            copy(1 + j, chip_idx[j], c, sibling).wait_recv()
            passed[j].start()
        copy(0, me, 1 - c, sibling).wait_recv()
        for j in range(3):
            copy(4 + j, chip_idx[j], 1 - c, sibling).wait_recv()
        for cp in first + passed:
            cp.wait_send()
        acc = gather_ref[0]
        for d in range(1, 2 * N_CHIPS):
            acc = acc + gather_ref[d]
        out_ref[...] = acc

    return pl.pallas_call(
        body, name="allreduce_small",
        out_shape=jax.ShapeDtypeStruct(sp.shape, sp.dtype),
        in_specs=[_VMEM], out_specs=_VMEM,
        scratch_shapes=[pltpu.VMEM((2 * N_CHIPS,) + sp.shape, sp.dtype),
                        pltpu.SemaphoreType.DMA((7,)), pltpu.SemaphoreType.DMA((7,))],
        compiler_params=_cparams(),
    )(sp)


def _n_rows(shape):
    n = 1
    for d in shape:
        n *= d
    return 8 * (-(-n // 8192))


def _pack(arrays, total_rows):
    parts = []
    for a in arrays:
        flat = a.reshape(-1)
        parts.append(jnp.pad(flat, (0, 1024 * _n_rows(a.shape) - flat.shape[0])).reshape(-1, 1024))
    rows = jnp.concatenate(parts, axis=0)
    return jnp.pad(rows, ((0, total_rows - rows.shape[0]), (0, 0)))


def _unpack(packed, shapes):
    out, r = [], 0
    for shp in shapes:
        n = 1
        for d in shp:
            n *= d
        nr = _n_rows(shp)
        out.append(packed[r:r + nr].reshape(-1)[:n].reshape(shp))
        r += nr
    return out


IN_SHARD, IN_PAD = 1284, 1408
QKV_SHARD, QKV_PAD = 320, 384


def _lane_padded(a, cols):
    return jnp.pad(a, ((0, 0), (0, cols - a.shape[1])))


_SMALL_SHAPES = (
    ("norm_mix_g", (2, 1024)), ("norm_mlp_g", (2, 1024)), ("final_norm_g", (1024,)), ("gm_ln_g", (1, 1024)),
    ("gm_ln_b", (1, 1024)), ("gm_w_s", (1, 8, 128, 128)), ("gm_b_s", (1, 8, 128)), ("ssm_conv_b", (1, 2048)),
    ("ssm_dt_bias", (1, 16)), ("ssm_a_log", (1, 16)), ("ssm_d", (1, 16)), ("ssm_norm_g", (1, 1024)),
    ("attn_sinks", (1, 16)), ("ssm_conv_w", (1, 4, 2048)), ("b_qkv", (1, 1280)), ("b_o", (1, 1024)),
)
_N_REPLICATED = 13
_SHARDED_SMALL = (("ssm_conv_w", 2, 512), ("b_qkv", 1, 320), ("b_o", 1, 256))
_GATHER_ROWS = 24
_SHARD_PACK_ROWS = 24


def _cols_by_owner(a):
    return a.transpose(1, 0, 2).reshape(a.shape[1], -1)


class WeightGatherer:
    def __init__(self, w, chip_idx):
        rows = lambda *parts: jnp.concatenate(parts, axis=0).astype(bf16)
        shards = [
            ("in", _lane_padded(w["w_in_even"][0], IN_PAD).astype(bf16)),
            ("l0", rows(w["w_out_even"][0], w["w_up"][0], w["w_down"][0])),
            ("l1", rows(w["w_o"][0], w["w_up"][1], w["w_down"][1])),
            ("qkv", _lane_padded(w["w_qkv"][0], QKV_PAD).astype(bf16)),
        ]
        placed = [place_shard(b, chip_idx, f"place_shard_{tag}") for tag, b in shards]
        self.sems, self.bufs = gather_start([placed[:1], placed[1:2], placed[2:]])

    def _group(self, gi, after, tag):
        return gather_forward(gather_wait(self.bufs[gi], self.sems[gi], after, tag), tag)

    def mixer_in(self):
        (g,) = self._group(0, None, "in")
        return _lane_padded(_cols_by_owner(g[:, :, :IN_SHARD]), NP_IN)

    def layer0(self, after):
        (g,) = self._group(1, after, "l0")
        return g[:, :512].reshape(2048, 1024), _cols_by_owner(g[:, 512:1536]), g[:, 1536:].reshape(4096, 1024)

    def layer1(self, after):
        g, q = self._group(2, after, "l1")
        return (_cols_by_owner(q[:, :, :QKV_SHARD]), g[:, :256].reshape(1024, 1024), _cols_by_owner(g[:, 256:1280]),
                g[:, 1280:].reshape(4096, 1024))


def _row2(v):
    return v.reshape(1, -1)


def _lane_pad(v):
    return jnp.pad(v, ((0, 0), (0, CH - v.shape[1])))


def _mlp_fwd(h, g_row, w_up, w_down, tag):
    y = rmsnorm_fwd(h, g_row, f"mlp_norm{tag}")
    a = matmul(y, w_up, dims="nn", name=f"mlp_up{tag}", out_dtype=bf16, tn=1024)
    out = matmul(a, w_down, dims="nn", name=f"mlp_down{tag}", a_pro=_relu2, epi=_add, epi_args=(("tile", h),))
    return out, y, a


def _mlp_bwd(dh_out, h, g_row, y, a, w_up, w_down, tag, after=None):
    da = matmul(dh_out, w_down, dims="nt", name=f"mlp_da{tag}", out_dtype=bf16, tn=1024,
                epi=_times_relu2_grad, epi_args=(("tile", a),), after=after)
    dw_down = matmul(a, dh_out, dims="tn", name=f"mlp_dwdown{tag}", out_dtype=bf16, a_pro=_relu2)
    dw_up = matmul(y, da, dims="tn", name=f"mlp_dwup{tag}", out_dtype=bf16, tn=1024, out_by_col_tile=True)
    dy = matmul(da, w_up, dims="nt", name=f"mlp_dy{tag}")
    dh, dg = rmsnorm_bwd(h, g_row, dy, dh_out, f"mlp_dnorm{tag}")
    return dh, dg, dw_up, dw_down


def _by_owner(a):
    return a.reshape(N_CHIPS, a.shape[0] // N_CHIPS, a.shape[1])


def _col_shards(a, shard, padded):
    return jnp.stack([_lane_padded(a[:, shard * s: shard * (s + 1)], padded) for s in range(N_CHIPS)])


def _local_step(x, target, weights, sm, reducer):
    w_up, w_down = [None, None], [None, None]
    mix_g = [_row2(sm["norm_mix_g"][i]) for i in range(2)]
    mlp_g = [_row2(sm["norm_mlp_g"][i]) for i in range(2)]
    mixer_prm = {
        "ln_g": sm["gm_ln_g"], "ln_b": sm["gm_ln_b"], "wm": sm["gm_w_s"][0],
        "bs_t": jnp.pad(sm["gm_b_s"][0].T, ((0, 0), (0, CH - N_BLK))),
        "conv_w": jnp.pad(sm["ssm_conv_w"][0], ((0, 4), (0, 0))), "conv_b": sm["ssm_conv_b"],
        "dt_bias": _lane_pad(sm["ssm_dt_bias"]), "a_log": _lane_pad(sm["ssm_a_log"]),
        "d_heads": _lane_pad(sm["ssm_d"]), "norm_g": sm["ssm_norm_g"],
    }
    sink_row = _lane_pad(sm["attn_sinks"])

    y0 = rmsnorm_fwd(x, mix_g[0], "mix_norm0")
    w_in_p = weights.mixer_in()
    proj = matmul(y0, w_in_p, dims="nn", name="in_proj", tn=768)
    ab, hstates = mixer_fwd(proj, mixer_prm)
    w_out, w_up[0], w_down[0] = weights.layer0(ab)
    h1 = matmul(ab, w_out, dims="nn", name="out_proj", epi=_add, epi_args=(("tile", x),))
    h2, y1, a1 = _mlp_fwd(h1, mlp_g[0], w_up[0], w_down[0], 0)
    w_qkv, w_o, w_up[1], w_down[1] = weights.layer1(h2)
    y2 = rmsnorm_fwd(h2, mix_g[1], "mix_norm1")
    qkv = matmul(y2, w_qkv, dims="nn", name="qkv_proj", tn=QKV_DIM, epi=_add_bias, epi_args=(("row", sm["b_qkv"]),))
    att = attn_fwd(qkv, sink_row)
    h3 = matmul(att, w_o, dims="nn", name="o_proj", epi=_add_bias_res,
                epi_args=(("row", sm["b_o"]), ("tile", h2)))
    h4, y3, a3 = _mlp_fwd(h3, mlp_g[1], w_up[1], w_down[1], 1)
    loss, dh4, dg_final = final_loss(h4, _row2(sm["final_norm_g"]), target, "final_loss")

    dh3, dg_mlp1, dw_up1, dw_down1 = _mlp_bwd(dh4, h3, mlp_g[1], y3, a3, w_up[1], w_down[1], 1)
    db_o = colsum(dh3, "db_o")
    datt = matmul(dh3, w_o, dims="nt", name="attn_dout", out_dtype=bf16)
    dw_o = matmul(att, dh3, dims="tn", name="dw_o", out_dtype=bf16)
    dqkv, dsink = attn_bwd(qkv, sink_row, datt)
    db_qkv = colsum(dqkv, "db_qkv")
    dw_qkv = matmul(y2, dqkv, dims="tn", name="dw_qkv", out_dtype=bf16, tn=QKV_DIM)
    dy2 = matmul(dqkv, w_qkv, dims="nt", name="dy_qkv", tk=QKV_DIM)
    dh2, dg_mix1 = rmsnorm_bwd(h2, mix_g[1], dy2, dh3, "mix_dnorm1")
    layer1 = [jnp.concatenate([_by_owner(dw_o), dw_up1, _by_owner(dw_down1)], axis=1),
              _col_shards(dw_qkv, QKV_SHARD, QKV_PAD)]
    flight1, token1 = reducer.start(layer1, "l1")
    dh1, dg_mlp0, dw_up0, dw_down0 = _mlp_bwd(dh2, h1, mlp_g[0], y1, a1, w_up[0], w_down[0], 0, after=token1)
    r_l1, r_qkv = reducer.finish(flight1, dh1, "l1")
    flight0, token0 = reducer.start([jnp.concatenate([dw_up0, _by_owner(dw_down0)], axis=1)], "l0")
    dab = matmul(dh1, w_out, dims="nt", name="mixer_dout", tn=1024, after=token0)
    dw_out = matmul(ab, dh1, dims="tn", name="dw_out", out_dtype=bf16)
    dproj, dmix = mixer_bwd(proj, hstates, dab, mixer_prm)
    dw_in_p = matmul(y0, dproj, dims="tn", name="dw_in", out_dtype=bf16, tn=768)
    dy0 = matmul(dproj, w_in_p, dims="nt", name="dy_in")
    dx, dg_mix0 = rmsnorm_bwd(x, mix_g[0], dy0, dh1, "mix_dnorm0")
    (r_mlp0,) = reducer.finish(flight0, dx, "l0")
    flight_m, token_m = reducer.start([_by_owner(dw_out), _col_shards(dw_in_p, IN_SHARD, IN_PAD)], "mix")
    r_out, r_in = reducer.finish(flight_m, token_m, "mix")
    reduced = {
        "w_out_even": r_out[None], "w_in_even": r_in[None, :, :IN_SHARD], "w_qkv": r_qkv[None, :, :QKV_SHARD],
        "w_o": r_l1[None, :256], "w_up": jnp.stack([r_mlp0[:1024], r_l1[256:1280]]),
        "w_down": jnp.stack([r_mlp0[1024:], r_l1[1280:]]),
    }

    small_grads = {
        "norm_mix_g": jnp.concatenate([dg_mix0, dg_mix1], axis=0),
        "norm_mlp_g": jnp.concatenate([dg_mlp0, dg_mlp1], axis=0),
        "final_norm_g": dg_final[0], "gm_ln_g": dmix["ln_g"], "gm_ln_b": dmix["ln_b"],
        "gm_w_s": dmix["wm"][None], "gm_b_s": dmix["bs_t"][:, :N_BLK].T[None],
        "ssm_conv_b": dmix["conv_b"], "ssm_dt_bias": dmix["dt_bias"][:, :SSM_HEADS],
        "ssm_a_log": dmix["a_log"][:, :SSM_HEADS], "ssm_d": dmix["d_heads"][:, :SSM_HEADS],
        "ssm_norm_g": dmix["norm_g"], "attn_sinks": dsink[:, :SSM_HEADS],
        "ssm_conv_w": dmix["conv_w"][None, :4], "b_qkv": db_qkv, "b_o": db_o,
    }
    return loss, dx, reduced, small_grads


def kernel(x, norm_mix_g, norm_mlp_g, final_norm_g, w_in_even, w_out_even, gm_ln_g, gm_ln_b, gm_w_s, gm_b_s, ssm_conv_w, ssm_conv_b, ssm_dt_bias, ssm_a_log, ssm_d, ssm_norm_g, w_qkv, b_qkv, w_o, b_o, attn_sinks, w_up, w_down, loss_target, m_norm_mix_g, m_norm_mlp_g, m_final_norm_g, m_w_in_even, m_w_out_even, m_gm_ln_g, m_gm_ln_b, m_gm_w_s, m_gm_b_s, m_ssm_conv_w, m_ssm_conv_b, m_ssm_dt_bias, m_ssm_a_log, m_ssm_d, m_ssm_norm_g, m_w_qkv, m_b_qkv, m_w_o, m_b_o, m_attn_sinks, m_w_up, m_w_down, v_norm_mix_g, v_norm_mlp_g, v_final_norm_g, v_w_in_even, v_w_out_even, v_gm_ln_g, v_gm_ln_b, v_gm_w_s, v_gm_b_s, v_ssm_conv_w, v_ssm_conv_b, v_ssm_dt_bias, v_ssm_a_log, v_ssm_d, v_ssm_norm_g, v_w_qkv, v_b_qkv, v_w_o, v_b_o, v_attn_sinks, v_w_up, v_w_down):
    w = dict(norm_mix_g=norm_mix_g, norm_mlp_g=norm_mlp_g, final_norm_g=final_norm_g, w_in_even=w_in_even,
             w_out_even=w_out_even, gm_ln_g=gm_ln_g, gm_ln_b=gm_ln_b, gm_w_s=gm_w_s, gm_b_s=gm_b_s,
             ssm_conv_w=ssm_conv_w, ssm_conv_b=ssm_conv_b, ssm_dt_bias=ssm_dt_bias, ssm_a_log=ssm_a_log,
             ssm_d=ssm_d, ssm_norm_g=ssm_norm_g, w_qkv=w_qkv, b_qkv=b_qkv, w_o=w_o, b_o=b_o,
             attn_sinks=attn_sinks, w_up=w_up, w_down=w_down)
    m = dict(norm_mix_g=m_norm_mix_g, norm_mlp_g=m_norm_mlp_g, final_norm_g=m_final_norm_g,
             w_in_even=m_w_in_even, w_out_even=m_w_out_even, gm_ln_g=m_gm_ln_g, gm_ln_b=m_gm_ln_b,
             gm_w_s=m_gm_w_s, gm_b_s=m_gm_b_s, ssm_conv_w=m_ssm_conv_w, ssm_conv_b=m_ssm_conv_b,
             ssm_dt_bias=m_ssm_dt_bias, ssm_a_log=m_ssm_a_log, ssm_d=m_ssm_d, ssm_norm_g=m_ssm_norm_g,
             w_qkv=m_w_qkv, b_qkv=m_b_qkv, w_o=m_w_o, b_o=m_b_o, attn_sinks=m_attn_sinks, w_up=m_w_up,
             w_down=m_w_down)
    v = dict(norm_mix_g=v_norm_mix_g, norm_mlp_g=v_norm_mlp_g, final_norm_g=v_final_norm_g,
             w_in_even=v_w_in_even, w_out_even=v_w_out_even, gm_ln_g=v_gm_ln_g, gm_ln_b=v_gm_ln_b,
             gm_w_s=v_gm_w_s, gm_b_s=v_gm_b_s, ssm_conv_w=v_ssm_conv_w, ssm_conv_b=v_ssm_conv_b,
             ssm_dt_bias=v_ssm_dt_bias, ssm_a_log=v_ssm_a_log, ssm_d=v_ssm_d, ssm_norm_g=v_ssm_norm_g,
             w_qkv=v_w_qkv, b_qkv=v_b_qkv, w_o=v_w_o, b_o=v_b_o, attn_sinks=v_attn_sinks, w_up=v_w_up,
             w_down=v_w_down)
    names = ("norm_mix_g", "norm_mlp_g", "final_norm_g", "w_in_even", "w_out_even", "gm_ln_g", "gm_ln_b",
             "gm_w_s", "gm_b_s", "ssm_conv_w", "ssm_conv_b", "ssm_dt_bias", "ssm_a_log", "ssm_d", "ssm_norm_g",
             "w_qkv", "b_qkv", "w_o", "b_o", "attn_sinks", "w_up", "w_down")

    cx, cy, cc = lax.axis_index("x"), lax.axis_index("y"), lax.axis_index("c")
    chip = 2 * cx + cy
    c_idx = jnp.reshape(cc, (1,)).astype(jnp.int32)
    chip_idx = jnp.reshape(chip, (1,)).astype(jnp.int32)

    weights = WeightGatherer(w, chip_idx)

    small_shard_placed = []
    for name, axis, width in _SHARDED_SMALL:
        full_shape = dict(_SMALL_SHAPES)[name]
        placed = lax.dynamic_update_slice_in_dim(jnp.zeros(full_shape, f32), w[name], chip * width, axis)
        small_shard_placed.append(jnp.where(cc == 0, placed, 0.0))
    gathered_small = _unpack(allreduce_small(_pack(small_shard_placed, _GATHER_ROWS)),
                             [dict(_SMALL_SHAPES)[n] for n, _, _ in _SHARDED_SMALL])
    sm = {n: w[n] for n, _ in _SMALL_SHAPES[:_N_REPLICATED]}
    sm.update({n: g for (n, _, _), g in zip(_SHARDED_SMALL, gathered_small)})

    reducer = GradReducer(c_idx, jnp.concatenate([chip_idx, c_idx]))
    loss_part, dx, grads, small_grads = _local_step(x[0], loss_target[0], weights, sm, reducer)
    loss = lax.psum(loss_part[0, 0], ("x", "y", "c"))

    small_sum = allreduce_small(_pack([small_grads[n] for n, _ in _SMALL_SHAPES], SMALL_ROWS))
    small_full = dict(zip([n for n, _ in _SMALL_SHAPES], _unpack(small_sum, [s for _, s in _SMALL_SHAPES])))
    for n, _ in _SMALL_SHAPES[:_N_REPLICATED]:
        grads[n] = small_full[n]
    for n, axis, width in _SHARDED_SMALL:
        grads[n] = lax.dynamic_slice_in_dim(small_full[n], chip * width, width, axis)
    grads = {n: grads[n].reshape(w[n].shape) for n in names}

    delta, new_m, new_v = {}, {}, {}
    for n in names:
        shape = (1,) + w[n].shape if w[n].ndim == 1 else w[n].shape
        outs = adamw(*[d[n].reshape(shape) for d in (w, grads, m, v)], f"adamw_{n}")
        delta[n], new_m[n], new_v[n] = (o.reshape(w[n].shape) for o in outs)

    return (loss, dx[None], *[grads[n] for n in names], *[delta[n] for n in names],
            *[new_m[n] for n in names], *[new_v[n] for n in names])
```

```python
import functools

import jax
import jax.numpy as jnp
from jax import lax
from jax.experimental import pallas as pl
from jax.experimental.pallas import tpu as pltpu

f32 = jnp.float32
bf16 = jnp.bfloat16
MXU_DTYPE = bf16

RMS_EPS = 1e-5
LN_EPS = 1e-5
D_MODEL = 1024
D_FF = 4096
CH = 128
N_BLK = 8
SSM_HEADS = 16
IN_EVEN = 5136
NP_IN = 5376
OFF_U, OFF_V, OFF_Z, OFF_X, OFF_DT = 0, 1024, 2048, 3072, 5120
XBC_BLKS = 16
QKV_DIM = 1280
ATT_SCALE = 64 ** -0.5

ADAM_LR = 0.001
ADAM_B1 = 0.9
ADAM_B2 = 0.999
ADAM_EPS = 1e-08
ADAM_WD = 0.01
ADAM_STEP = 10

VMEM_LIMIT_BYTES = 48 * 1024 * 1024
N_CHIPS = 4
SMALL_ROWS = 256

NN = ((1,), (0,))
NT = ((1,), (1,))
TN = ((0,), (0,))


def _mm(a, b, dims):
    return lax.dot_general(a.astype(MXU_DTYPE), b.astype(MXU_DTYPE), (dims, ((), ())),
                           preferred_element_type=f32)


def _mm_exact(a, b):
    return jnp.dot(a, b, preferred_element_type=f32, precision=lax.Precision.HIGHEST)


def _cparams(sem=None):
    return pltpu.CompilerParams(dimension_semantics=sem, vmem_limit_bytes=VMEM_LIMIT_BYTES)


@jax.custom_vjp
def _swap64(x):
    return pltpu.roll(x, 64, axis=1)


_swap64.defvjp(lambda x: (pltpu.roll(x, 64, axis=1), None), lambda _, g: (pltpu.roll(g, 64, axis=1),))


def _make_delay(k):
    @jax.custom_vjp
    def delay(ext):
        return pltpu.roll(ext, k, axis=0)[8:, :]

    def fwd(ext):
        return delay(ext), None

    def bwd(_, g):
        gp = jnp.concatenate([jnp.zeros((8, g.shape[1]), g.dtype), g], axis=0)
        return (pltpu.roll(gp, gp.shape[0] - k, axis=0),)

    delay.defvjp(fwd, bwd)
    return delay


_DELAYS = {k: _make_delay(k) for k in (1, 2, 3)}


def _col(m, lane, h):
    return jnp.sum(jnp.where(lane == h, m, 0.0), axis=1, keepdims=True)


def _row(m, sub, h):
    return jnp.sum(jnp.where(sub == h, m, 0.0), axis=0, keepdims=True)


def _mixer_chunk(us, vs, zs, xbcs, halos, dtblk, hps, prm):
    lane = lax.broadcasted_iota(jnp.int32, (CH, CH), 1)
    sub = lax.broadcasted_iota(jnp.int32, (CH, CH), 0)
    left = lane < 64
    top = sub < 64
    causal = sub >= lane

    gus = [jax.nn.gelu(u) for u in us]
    gvs = [jax.nn.gelu(v) for v in vs]
    mu = sum(jnp.sum(g, axis=1, keepdims=True) for g in gvs) / D_MODEL
    cen = [g - mu for g in gvs]
    var = sum(jnp.sum(c * c, axis=1, keepdims=True) for c in cen) / D_MODEL
    rstd = lax.rsqrt(var + LN_EPS)
    a_out = []
    for g in range(N_BLK):
        vn = cen[g] * rstd * prm["ln_g"][g] + prm["ln_b"][g]
        w = jnp.where(causal, prm["wm"][g], 0.0)
        mixed = _mm(w, vn, NN) + _col(prm["bs_t"], lane, g)
        a_out.append(gus[g] * mixed)

    act = []
    for b in range(XBC_BLKS):
        w8 = prm["conv_w"][b]
        sub8 = lax.broadcasted_iota(jnp.int32, w8.shape, 0)
        ext = jnp.concatenate([halos[b], xbcs[b]], axis=0)
        conv = xbcs[b] * _row(w8, sub8, 3) + prm["conv_b"][b]
        for k in (1, 2, 3):
            conv = conv + _DELAYS[k](ext) * _row(w8, sub8, 3 - k)
        act.append(jax.nn.silu(conv))

    dt = jax.nn.softplus(dtblk + prm["dt_bias"])
    a_neg = -jnp.exp(prm["a_log"])
    tri = causal.astype(f32)
    acum = _mm_exact(tri, dt * a_neg)
    acum_t = acum.T
    dt_t = dt.T
    last = sub == CH - 1
    ys, h_out = [], []
    for grp in range(4):
        bm = act[8 + grp]
        cm = act[12 + grp]
        cb = _mm(cm, bm, NT)
        for p in (2 * grp, 2 * grp + 1):
            h0, h1 = 2 * p, 2 * p + 1
            xp = act[p]
            hp = hps[p]
            wis = []
            for h in (h0, h1):
                seg = _col(acum, lane, h) - _row(acum_t, sub, h)
                decay = jnp.exp(jnp.where(causal, seg, -jnp.inf))
                wis.append(cb * decay * _row(dt_t, sub, h))
            wcat = jnp.concatenate(wis, axis=1)
            xbd = jnp.concatenate([jnp.where(left, xp, 0.0), jnp.where(left, 0.0, xp)], axis=0)
            y_diag = _mm(wcat, xbd, NN)
            a_end = [jnp.sum(jnp.where(last & (lane == h), acum, 0.0), keepdims=True) for h in (h0, h1)]
            a_col = jnp.where(left, _col(acum, lane, h0), _col(acum, lane, h1))
            dt_col = jnp.where(left, _col(dt, lane, h0), _col(dt, lane, h1))
            to_end = jnp.exp(jnp.where(left, a_end[0], a_end[1]) - a_col) * dt_col
            states = _mm(xp * to_end, bm, TN)
            chunk_decay = jnp.where(top, jnp.exp(a_end[0]), jnp.exp(a_end[1]))
            h_out.append(chunk_decay * hp + states)
            y_off = jnp.exp(a_col) * _mm(cm, hp, NT)
            d_skip = jnp.where(left[:1], _col(prm["d_heads"], lane[:1], h0), _col(prm["d_heads"], lane[:1], h1))
            ys.append((y_diag + y_off + xp * d_skip) * jax.nn.silu(zs[p]))

    b_out = []
    for grp in range(4):
        pair = (ys[2 * grp], ys[2 * grp + 1])
        ms = sum(jnp.sum(y * y, axis=1, keepdims=True) for y in pair) / 256.0
        r = lax.rsqrt(ms + RMS_EPS)
        for j, y in enumerate(pair):
            b_out.append(y * r * prm["norm_g"][2 * grp + j])
    return a_out, b_out, h_out


def _attn_block(qps, kprev, kcur, vprev, vcur, sink_row, first):
    lane = lax.broadcasted_iota(jnp.int32, (CH, CH), 1)
    left = lane < 64
    kband = jnp.concatenate([kprev, kcur], axis=0)
    vband = jnp.concatenate([vprev, vcur], axis=0)
    left2 = lax.broadcasted_iota(jnp.int32, kband.shape, 1) < 64
    ksw, vsw = _swap64(kband), _swap64(vband)
    kdup = [jnp.where(left2, kband, ksw), jnp.where(left2, ksw, kband)]
    vdup = [jnp.where(left2, vband, vsw), jnp.where(left2, vsw, vband)]
    qi = lax.broadcasted_iota(jnp.int32, (CH, 2 * CH), 0)
    si = lax.broadcasted_iota(jnp.int32, (CH, 2 * CH), 1)
    rel = qi + CH - si
    valid = (rel >= 0) & (rel < CH) & (jnp.logical_not(first) | (si >= CH))
    outs = []
    for p in range(N_BLK):
        j = p // 4
        halves = []
        for side, h in ((0, 2 * p), (1, 2 * p + 1)):
            qh = jnp.where(left, qps[p], 0.0) if side == 0 else jnp.where(left, 0.0, qps[p])
            s = _mm(qh, kdup[j], NT) * ATT_SCALE
            s = jnp.where(valid, s, -jnp.inf)
            sink = _col(sink_row, lane[:1], h)
            m = lax.stop_gradient(jnp.maximum(jnp.max(s, axis=1, keepdims=True), sink))
            pexp = jnp.exp(s - m)
            denom = jnp.sum(pexp, axis=1, keepdims=True) + jnp.exp(sink - m)
            halves.append(_mm(pexp / denom, vdup[j], NN))
        outs.append(jnp.where(left, halves[0], halves[1]))
    return outs


def _rmsnorm(x, g):
    r = lax.rsqrt(jnp.mean(x * x, axis=-1, keepdims=True) + RMS_EPS)
    return x * r * g


def rmsnorm_fwd(x, g_row, name):
    s, d = x.shape
    tm = min(512, s)

    def body(x_ref, g_ref, y_ref):
        y_ref[...] = _rmsnorm(x_ref[...], g_ref[...]).astype(bf16)

    return pl.pallas_call(
        body, name=name, grid=(s // tm,),
        in_specs=[pl.BlockSpec((tm, d), lambda i: (i, 0)), pl.BlockSpec((1, d), lambda i: (0, 0))],
        out_specs=pl.BlockSpec((tm, d), lambda i: (i, 0)),
        out_shape=jax.ShapeDtypeStruct((s, d), bf16),
        compiler_params=_cparams(("parallel",)),
    )(x, g_row)


def rmsnorm_bwd(x, g_row, dy, res, name):
    s, d = x.shape
    tm = min(512, s)

    def body(x_ref, g_ref, dy_ref, res_ref, dx_ref, dg_ref):
        @pl.when(pl.program_id(0) == 0)
        def _():
            dg_ref[...] = jnp.zeros_like(dg_ref)

        _, vjp = jax.vjp(_rmsnorm, x_ref[...], g_ref[...])
        dx, dg = vjp(dy_ref[...])
        dx_ref[...] = res_ref[...] + dx
        dg_ref[...] += dg

    tile = pl.BlockSpec((tm, d), lambda i: (i, 0))
    row = pl.BlockSpec((1, d), lambda i: (0, 0))
    return pl.pallas_call(
        body, name=name, grid=(s // tm,),
        in_specs=[tile, row, tile, tile], out_specs=[tile, row],
        out_shape=[jax.ShapeDtypeStruct((s, d), f32), jax.ShapeDtypeStruct((1, d), f32)],
        compiler_params=_cparams(("arbitrary",)),
    )(x, g_row, dy, res)


def final_loss(h, g_row, target, name):
    s, d = h.shape
    tm = min(512, s)

    def body(h_ref, g_ref, t_ref, loss_ref, dh_ref, dg_ref):
        @pl.when(pl.program_id(0) == 0)
        def _():
            dg_ref[...] = jnp.zeros_like(dg_ref)
            loss_ref[...] = jnp.zeros_like(loss_ref)

        def f(hv, gv):
            err = jnp.square(_rmsnorm(hv, gv) - t_ref[...])
            return 0.5 * jnp.sum(jnp.mean(err, axis=-1, keepdims=True), axis=0, keepdims=True)

        loss, vjp = jax.vjp(f, h_ref[...], g_ref[...])
        dh, dg = vjp(jnp.ones_like(loss))
        dh_ref[...] = dh
        dg_ref[...] += dg
        loss_ref[...] += jnp.broadcast_to(loss, loss_ref.shape)

    tile = pl.BlockSpec((tm, d), lambda i: (i, 0))
    row = pl.BlockSpec((1, d), lambda i: (0, 0))
    return pl.pallas_call(
        body, name=name, grid=(s // tm,),
        in_specs=[tile, row, tile],
        out_specs=[pl.BlockSpec((1, 128), lambda i: (0, 0)), tile, row],
        out_shape=[jax.ShapeDtypeStruct((1, 128), f32), jax.ShapeDtypeStruct((s, d), f32),
                   jax.ShapeDtypeStruct((1, d), f32)],
        compiler_params=_cparams(("arbitrary",)),
    )(h, g_row, target)


def colsum(x, name):
    s, n = x.shape
    tm = min(512, s)

    def body(x_ref, o_ref):
        @pl.when(pl.program_id(0) == 0)
        def _():
            o_ref[...] = jnp.zeros_like(o_ref)

        o_ref[...] += jnp.sum(x_ref[...].astype(f32), axis=0, keepdims=True)

    return pl.pallas_call(
        body, name=name, grid=(s // tm,),
        in_specs=[pl.BlockSpec((tm, n), lambda i: (i, 0))],
        out_specs=pl.BlockSpec((1, n), lambda i: (0, 0)),
        out_shape=jax.ShapeDtypeStruct((1, n), f32),
        compiler_params=_cparams(("arbitrary",)),
    )(x)


def _fit(dim, want):
    if dim <= want:
        return dim
    t = want
    while dim % t:
        t -= 128
    return t


def matmul(a, b, *, dims, name, out_dtype=f32, tm=1024, tn=512, tk=8192, a_pro=None, epi=None, epi_args=(),
           out_by_col_tile=False, after=None):
    if dims == "nn":
        (m, k), n = a.shape, b.shape[1]
    elif dims == "nt":
        (m, k), n = a.shape, b.shape[0]
    else:
        (k, m), n = a.shape, b.shape[1]
    tm, tn, tk = _fit(m, tm), _fit(n, tn), _fit(k, tk)
    nk = k // tk
    if dims == "nn":
        a_spec = pl.BlockSpec((tm, tk), lambda i, j, kk: (i, kk))
        b_spec = pl.BlockSpec((tk, tn), lambda i, j, kk: (kk, j))
        dn = NN
    elif dims == "nt":
        a_spec = pl.BlockSpec((tm, tk), lambda i, j, kk: (i, kk))
        b_spec = pl.BlockSpec((tn, tk), lambda i, j, kk: (j, kk))
        dn = NT
    else:
        a_spec = pl.BlockSpec((tk, tm), lambda i, j, kk: (kk, i))
        b_spec = pl.BlockSpec((tk, tn), lambda i, j, kk: (kk, j))
        dn = TN
    e_specs = [pl.BlockSpec((tm, tn), lambda i, j, kk: (i, j)) if kind == "tile"
               else pl.BlockSpec((1, tn), lambda i, j, kk: (0, j)) for kind, _ in epi_args]
    n_epi = len(epi_args)
    order_specs = [] if after is None else [pl.BlockSpec((8, 128), lambda i, j, kk: (0, 0))]
    order_args = [] if after is None else [after]

    def body(*refs):
        a_ref, b_ref = refs[0], refs[1]
        e_refs = refs[2:2 + n_epi]
        n_in = 2 + n_epi + len(order_args)
        o_ref = refs[n_in]
        av = a_ref[...]
        if a_pro is not None:
            av = a_pro(av)
        part = _mm(av, b_ref[...], dn)

        def finish(acc):
            if epi is not None:
                acc = epi(acc, *[r[...] for r in e_refs])
            o_ref[...] = acc.astype(out_dtype)

        if nk == 1:
            finish(part)
        else:
            acc_ref = refs[n_in + 1]
            kk = pl.program_id(2)

            @pl.when(kk == 0)
            def _():
                acc_ref[...] = part

            @pl.when(kk > 0)
            def _():
                acc_ref[...] += part

            @pl.when(kk == nk - 1)
            def _():
                finish(acc_ref[...])

    if out_by_col_tile:
        out_spec = pl.BlockSpec((None, tm, tn), lambda i, j, kk: (j, i, 0))
        out_shape = jax.ShapeDtypeStruct((n // tn, m, tn), out_dtype)
    else:
        out_spec = pl.BlockSpec((tm, tn), lambda i, j, kk: (i, j))
        out_shape = jax.ShapeDtypeStruct((m, n), out_dtype)
    return pl.pallas_call(
        body, name=name, grid=(m // tm, n // tn, nk),
        in_specs=[a_spec, b_spec] + e_specs + order_specs,
        out_specs=out_spec,
        out_shape=out_shape,
        scratch_shapes=[pltpu.VMEM((tm, tn), f32)] if nk > 1 else [],
        compiler_params=_cparams(("parallel", "parallel", "arbitrary")),
    )(a, b, *[arr for _, arr in epi_args], *order_args)


def _relu2(a):
    r = jnp.maximum(a.astype(f32), 0.0)
    return r * r


def _add(acc, t):
    return acc + t


def _add_bias(acc, t):
    return acc + t


def _add_bias_res(acc, bias, res):
    return acc + bias + res


def _times_relu2_grad(acc, a):
    return acc * (2.0 * jnp.maximum(a.astype(f32), 0.0))


_MIXER_PARAM_SHAPES = (
    ("ln_g", (1, D_MODEL)), ("ln_b", (1, D_MODEL)), ("wm", (N_BLK, CH, CH)), ("bs_t", (CH, CH)),
    ("conv_w", (8, 2048)), ("conv_b", (1, 2048)), ("dt_bias", (1, CH)), ("a_log", (1, CH)),
    ("d_heads", (1, CH)), ("norm_g", (1, D_MODEL)),
)


def _blocks(v, n, off=0):
    return [v[:, off + i * CH: off + (i + 1) * CH] for i in range(n)]


def _split_mixer_params(vals):
    p = dict(vals)
    return {
        "ln_g": _blocks(p["ln_g"], N_BLK), "ln_b": _blocks(p["ln_b"], N_BLK),
        "wm": [p["wm"][g] for g in range(N_BLK)], "bs_t": p["bs_t"],
        "conv_w": _blocks(p["conv_w"], XBC_BLKS), "conv_b": _blocks(p["conv_b"], XBC_BLKS),
        "dt_bias": p["dt_bias"], "a_log": p["a_log"], "d_heads": p["d_heads"],
        "norm_g": _blocks(p["norm_g"], N_BLK),
    }


def _mixer_leaves(proj_ref, halo_ref, keep_halo):
    pv = proj_ref
    us = [pv[:, OFF_U + i * CH: OFF_U + (i + 1) * CH] for i in range(N_BLK)]
    vs = [pv[:, OFF_V + i * CH: OFF_V + (i + 1) * CH] for i in range(N_BLK)]
    zs = [pv[:, OFF_Z + i * CH: OFF_Z + (i + 1) * CH] for i in range(N_BLK)]
    xbcs = [pv[:, OFF_X + i * CH: OFF_X + (i + 1) * CH] for i in range(XBC_BLKS)]
    halos = [halo_ref[:, OFF_X + i * CH: OFF_X + (i + 1) * CH] * keep_halo for i in range(XBC_BLKS)]
    dtblk = pv[:, OFF_DT: OFF_DT + CH]
    return us, vs, zs, xbcs, halos, dtblk


def mixer_fwd(proj, prm):
    s = proj.shape[0]
    nc = s // CH
    names = [n for n, _ in _MIXER_PARAM_SHAPES]

    def body(proj_ref, halo_ref, *rest):
        p_refs = rest[:len(names)]
        ab_ref, hs_ref, h_ref = rest[len(names):]
        c = pl.program_id(0)

        @pl.when(c == 0)
        def _():
            h_ref[...] = jnp.zeros_like(h_ref)

        hs_ref[...] = h_ref[...]
        keep = (c > 0).astype(f32)
        us, vs, zs, xbcs, halos, dtblk = _mixer_leaves(proj_ref, halo_ref, keep)
        hps = [h_ref[i * CH:(i + 1) * CH, :] for i in range(N_BLK)]
        p = _split_mixer_params({n: r[...] for n, r in zip(names, p_refs)})
        a_out, b_out, h_out = _mixer_chunk(us, vs, zs, xbcs, halos, dtblk, hps, p)
        for i in range(N_BLK):
            ab_ref[:, i * CH:(i + 1) * CH] = a_out[i].astype(bf16)
            ab_ref[:, D_MODEL + i * CH: D_MODEL + (i + 1) * CH] = b_out[i].astype(bf16)
            h_ref[i * CH:(i + 1) * CH, :] = h_out[i]

    def const(shape):
        return pl.BlockSpec(shape, lambda c: (0,) * len(shape))

    return pl.pallas_call(
        body, name="mixer_fwd", grid=(nc,),
        in_specs=[pl.BlockSpec((CH, NP_IN), lambda c: (c, 0)),
                  pl.BlockSpec((8, NP_IN), lambda c: (jnp.maximum(c * (CH // 8) - 1, 0), 0))]
                 + [const(shp) for _, shp in _MIXER_PARAM_SHAPES],
        out_specs=[pl.BlockSpec((CH, 2 * D_MODEL), lambda c: (c, 0)),
                   pl.BlockSpec((None, D_MODEL, CH), lambda c: (c, 0, 0))],
        out_shape=[jax.ShapeDtypeStruct((s, 2 * D_MODEL), bf16), jax.ShapeDtypeStruct((nc, D_MODEL, CH), f32)],
        scratch_shapes=[pltpu.VMEM((D_MODEL, CH), f32)],
        compiler_params=_cparams(("arbitrary",)),
    )(proj, proj, *[prm[n] for n in names])


def mixer_bwd(proj, hstates, dab, prm):
    s = proj.shape[0]
    nc = s // CH
    names = [n for n, _ in _MIXER_PARAM_SHAPES]
    npar = len(names)

    def body(proj_ref, halo_ref, hs_ref, dab_ref, *rest):
        p_refs = rest[:npar]
        dproj_ref = rest[npar]
        g_refs = rest[npar + 1: 2 * npar + 1]
        dh_ref, dhalo_ref = rest[2 * npar + 1:]
        i = pl.program_id(0)
        c = nc - 1 - i

        @pl.when(i == 0)
        def _():
            dh_ref[...] = jnp.zeros_like(dh_ref)
            dhalo_ref[...] = jnp.zeros_like(dhalo_ref)
            for r in g_refs:
                r[...] = jnp.zeros_like(r)

        keep = (c > 0).astype(f32)
        us, vs, zs, xbcs, halos, dtblk = _mixer_leaves(proj_ref, halo_ref, keep)
        hps = [hs_ref[j * CH:(j + 1) * CH, :] for j in range(N_BLK)]
        pvals = {n: r[...] for n, r in zip(names, p_refs)}

        def fn(us, vs, zs, xbcs, halos, dtblk, hps, pvals):
            return _mixer_chunk(us, vs, zs, xbcs, halos, dtblk, hps, _split_mixer_params(pvals))

        _, vjp = jax.vjp(fn, us, vs, zs, xbcs, halos, dtblk, hps, pvals)
        da = [dab_ref[:, j * CH:(j + 1) * CH].astype(f32) for j in range(N_BLK)]
        db = [dab_ref[:, D_MODEL + j * CH: D_MODEL + (j + 1) * CH].astype(f32) for j in range(N_BLK)]
        dh = [dh_ref[j * CH:(j + 1) * CH, :] for j in range(N_BLK)]
        dus, dvs, dzs, dxbcs, dhalos, ddt, dhps, dp = vjp((da, db, dh))

        for j in range(N_BLK):
            dproj_ref[:, OFF_U + j * CH: OFF_U + (j + 1) * CH] = dus[j].astype(bf16)
            dproj_ref[:, OFF_V + j * CH: OFF_V + (j + 1) * CH] = dvs[j].astype(bf16)
            dproj_ref[:, OFF_Z + j * CH: OFF_Z + (j + 1) * CH] = dzs[j].astype(bf16)
            dh_ref[j * CH:(j + 1) * CH, :] = dhps[j]
        zeros_top = jnp.zeros((CH - 8, CH), f32)
        for j in range(XBC_BLKS):
            late = jnp.concatenate([zeros_top, dhalo_ref[:, j * CH:(j + 1) * CH]], axis=0)
            dproj_ref[:, OFF_X + j * CH: OFF_X + (j + 1) * CH] = (dxbcs[j] + late).astype(bf16)
        for j in range(XBC_BLKS):
            dhalo_ref[:, j * CH:(j + 1) * CH] = dhalos[j] * keep
        lane = lax.broadcasted_iota(jnp.int32, (CH, CH), 1)
        dproj_ref[:, OFF_DT: OFF_DT + CH] = jnp.where(lane < SSM_HEADS, ddt, 0.0).astype(bf16)
        dproj_ref[:, OFF_DT + CH:] = jnp.zeros((CH, NP_IN - OFF_DT - CH), bf16)
        for n, r in zip(names, g_refs):
            r[...] += dp[n]

    def const(shape):
        return pl.BlockSpec(shape, lambda i: (0,) * len(shape))

    outs = pl.pallas_call(
        body, name="mixer_bwd", grid=(nc,),
        in_specs=[pl.BlockSpec((CH, NP_IN), lambda i: (nc - 1 - i, 0)),
                  pl.BlockSpec((8, NP_IN), lambda i: (jnp.maximum((nc - 1 - i) * (CH // 8) - 1, 0), 0)),
                  pl.BlockSpec((None, D_MODEL, CH), lambda i: (nc - 1 - i, 0, 0)),
                  pl.BlockSpec((CH, 2 * D_MODEL), lambda i: (nc - 1 - i, 0))]
                 + [const(shp) for _, shp in _MIXER_PARAM_SHAPES],
        out_specs=[pl.BlockSpec((CH, NP_IN), lambda i: (nc - 1 - i, 0))]
                  + [const(shp) for _, shp in _MIXER_PARAM_SHAPES],
        out_shape=[jax.ShapeDtypeStruct((s, NP_IN), bf16)]
                  + [jax.ShapeDtypeStruct(shp, f32) for _, shp in _MIXER_PARAM_SHAPES],
        scratch_shapes=[pltpu.VMEM((D_MODEL, CH), f32), pltpu.VMEM((8, 2048), f32)],
        compiler_params=_cparams(("arbitrary",)),
    )(proj, proj, hstates, dab, *[prm[n] for n in names])
    return outs[0], dict(zip(names, outs[1:]))


_K_BLK = D_MODEL // CH
_V_BLK = _K_BLK + 1


def _attn_specs(rev, nb):
    def blk(i):
        return nb - 1 - i if rev else i

    q_spec = pl.BlockSpec((CH, D_MODEL), lambda i: (blk(i), 0))
    kv = lambda col, prev: pl.BlockSpec(
        (CH, CH), lambda i: (jnp.maximum(blk(i) - 1, 0) if prev else blk(i), col))
    return q_spec, [kv(_K_BLK, True), kv(_K_BLK, False), kv(_V_BLK, True), kv(_V_BLK, False)]


def attn_fwd(qkv, sink_row):
    s = qkv.shape[0]
    nb = s // CH

    def body(q_ref, kp_ref, kc_ref, vp_ref, vc_ref, sink_ref, o_ref):
        qps = [q_ref[:, p * CH:(p + 1) * CH] for p in range(N_BLK)]
        outs = _attn_block(qps, kp_ref[...], kc_ref[...], vp_ref[...], vc_ref[...], sink_ref[...],
                           pl.program_id(0) == 0)
        for p in range(N_BLK):
            o_ref[:, p * CH:(p + 1) * CH] = outs[p].astype(bf16)

    q_spec, kv_specs = _attn_specs(False, nb)
    return pl.pallas_call(
        body, name="attn_fwd", grid=(nb,),
        in_specs=[q_spec] + kv_specs + [pl.BlockSpec((1, CH), lambda i: (0, 0))],
        out_specs=pl.BlockSpec((CH, D_MODEL), lambda i: (i, 0)),
        out_shape=jax.ShapeDtypeStruct((s, D_MODEL), bf16),
        compiler_params=_cparams(("parallel",)),
    )(qkv, qkv, qkv, qkv, qkv, sink_row)


def attn_bwd(qkv, sink_row, dout):
    s = qkv.shape[0]
    nb = s // CH

    def body(q_ref, kp_ref, kc_ref, vp_ref, vc_ref, sink_ref, do_ref, dqkv_ref, dsink_ref, carry_ref):
        i = pl.program_id(0)
        blk = nb - 1 - i

        @pl.when(i == 0)
        def _():
            dsink_ref[...] = jnp.zeros_like(dsink_ref)
            carry_ref[...] = jnp.zeros_like(carry_ref)

        qps = [q_ref[:, p * CH:(p + 1) * CH] for p in range(N_BLK)]
        first = blk == 0
        _, vjp = jax.vjp(lambda *a: _attn_block(*a, first), qps, kp_ref[...], kc_ref[...], vp_ref[...],
                         vc_ref[...], sink_ref[...])
        dos = [do_ref[:, p * CH:(p + 1) * CH].astype(f32) for p in range(N_BLK)]
        dqs, dkp, dkc, dvp, dvc, dsink = vjp(dos)
        for p in range(N_BLK):
            dqkv_ref[:, p * CH:(p + 1) * CH] = dqs[p].astype(bf16)
        dqkv_ref[:, D_MODEL: D_MODEL + CH] = (dkc + carry_ref[0]).astype(bf16)
        dqkv_ref[:, D_MODEL + CH:] = (dvc + carry_ref[1]).astype(bf16)
        keep = jnp.logical_not(first).astype(f32)
        carry_ref[0] = dkp * keep
        carry_ref[1] = dvp * keep
        dsink_ref[...] += dsink

    q_spec, kv_specs = _attn_specs(True, nb)
    return pl.pallas_call(
        body, name="attn_bwd", grid=(nb,),
        in_specs=[q_spec] + kv_specs + [pl.BlockSpec((1, CH), lambda i: (0, 0)),
                                        pl.BlockSpec((CH, D_MODEL), lambda i: (nb - 1 - i, 0))],
        out_specs=[pl.BlockSpec((CH, QKV_DIM), lambda i: (nb - 1 - i, 0)), pl.BlockSpec((1, CH), lambda i: (0, 0))],
        out_shape=[jax.ShapeDtypeStruct((s, QKV_DIM), bf16), jax.ShapeDtypeStruct((1, CH), f32)],
        scratch_shapes=[pltpu.VMEM((2, CH, CH), f32)],
        compiler_params=_cparams(("arbitrary",)),
    )(qkv, qkv, qkv, qkv, qkv, sink_row, dout)


def adamw(w, g, m, v, name):
    def body(w_ref, g_ref, m_ref, v_ref, d_ref, nm_ref, nv_ref):
        gv = g_ref[...]
        nm = ADAM_B1 * m_ref[...] + (1.0 - ADAM_B1) * gv
        nv = ADAM_B2 * v_ref[...] + (1.0 - ADAM_B2) * jnp.square(gv)
        m_hat = nm / (1.0 - ADAM_B1 ** ADAM_STEP)
        v_hat = nv / (1.0 - ADAM_B2 ** ADAM_STEP)
        d_ref[...] = -ADAM_LR * (m_hat / (jnp.sqrt(v_hat) + ADAM_EPS) + ADAM_WD * w_ref[...])
        nm_ref[...] = nm
        nv_ref[...] = nv

    out_shape = [jax.ShapeDtypeStruct(w.shape, f32)] * 3
    if w.ndim == 3 and w.shape[1] % 256 == 0:
        tile = pl.BlockSpec((None, 256, w.shape[2]), lambda l, i: (l, i, 0))
        return pl.pallas_call(
            body, name=name, grid=(w.shape[0], w.shape[1] // 256),
            in_specs=[tile] * 4, out_specs=[tile] * 3, out_shape=out_shape,
            compiler_params=_cparams(("parallel", "parallel")),
        )(w, g, m, v)
    return pl.pallas_call(body, name=name, in_specs=[_VMEM] * 4, out_specs=[_VMEM] * 3, out_shape=out_shape,
                          compiler_params=_cparams())(w, g, m, v)


_MESH = pl.DeviceIdType.MESH
_ANY = pl.BlockSpec(memory_space=pl.ANY)
_VMEM = pl.BlockSpec(memory_space=pltpu.VMEM)


def _place():
    x, y, c = lax.axis_index("x"), lax.axis_index("y"), lax.axis_index("c")
    chips = [(1 - x, y), (x, 1 - y), (1 - x, 1 - y)]
    return x, y, c, 2 * x + y, chips, [2 * cx + cy for cx, cy in chips]


def _half(c, rows):
    return pl.ds(pl.multiple_of(c * (rows // 2), 16), rows // 2)


def _step_rows(rows):
    return max(t for t in range(16, 641, 16) if rows % t == 0)


def place_shard(b, slot, name):
    r, c = b.shape
    tr = _step_rows(r)

    def body(slot_ref, b_ref, o_ref):
        o_ref[...] = b_ref[...]

    return pl.pallas_call(
        body, name=name,
        grid_spec=pltpu.PrefetchScalarGridSpec(
            num_scalar_prefetch=1, grid=(r // tr,),
            in_specs=[pl.BlockSpec((tr, c), lambda i, s: (i, 0))],
            out_specs=pl.BlockSpec((None, tr, c), lambda i, s: (s[0], i, 0))),
        out_shape=jax.ShapeDtypeStruct((N_CHIPS, r, c), b.dtype),
        compiler_params=_cparams(("parallel",)),
    )(slot, b)


_HBM = pl.BlockSpec(memory_space=pltpu.HBM)
_SEM = pl.BlockSpec(memory_space=pltpu.SEMAPHORE)
_EFFECT = pltpu.SideEffectType.DATAFLOW_SIDE_EFFECTING


def _gather_ici_copies(bufs, send_sems, recv_sems):
    x, y, c, me, chips, chip_idx = _place()
    return [pltpu.make_async_remote_copy(
        src_ref=buf.at[me, _half(c, buf.shape[1])], dst_ref=buf.at[chip_idx[j], _half(c, buf.shape[1])],
        send_sem=send_sems.at[3 * k + j], recv_sem=recv_sems.at[3 * k + j],
        device_id=(*chips[j], c), device_id_type=_MESH) for j in range(3) for k, buf in enumerate(bufs)]


def gather_start(groups):
    sizes = [len(g) for g in groups]
    flat = [b for g in groups for b in g]
    n = len(flat)

    def body(*refs):
        bufs, sems = refs[:n], refs[n:n + 2 * len(groups)]
        x, y, c, me, chips, chip_idx = _place()
        lo = 0
        for gi, size in enumerate(sizes):
            for j in range(3):
                for k, buf in enumerate(bufs[lo:lo + size]):
                    mine = buf.at[me, _half(c, buf.shape[1])]
                    pltpu.make_async_remote_copy(
                        src_ref=mine, dst_ref=mine, send_sem=sems[2 * gi].at[3 * k + j],
                        recv_sem=sems[2 * gi + 1].at[3 * k + j], device_id=(*chips[j], c),
                        device_id_type=_MESH).start()
            lo += size

    sem_shapes = [pltpu.SemaphoreType.DMA((3 * size,)) for size in sizes for _ in range(2)]
    outs = pl.pallas_call(
        body, name="gather_start",
        out_shape=(*sem_shapes, *[pltpu.HBM(b.shape, b.dtype) for b in flat]),
        in_specs=[_HBM] * n, out_specs=(*[_SEM] * len(sem_shapes), *[_HBM] * n),
        input_output_aliases={i: len(sem_shapes) + i for i in range(n)},
        compiler_params=pltpu.CompilerParams(has_side_effects=_EFFECT),
    )(*[pltpu.with_memory_space_constraint(b, pltpu.HBM) for b in flat])
    sems = [(outs[2 * gi], outs[2 * gi + 1]) for gi in range(len(groups))]
    thru, lo = [], len(sem_shapes)
    for size in sizes:
        thru.append(list(outs[lo:lo + size]))
        lo += size
    return sems, thru


def gather_wait(bufs, sems, after, tag):
    n = len(bufs)

    def body(*refs):
        for cp in _gather_ici_copies(refs[:n], refs[n], refs[n + 1]):
            cp.wait_send()
            cp.wait_recv()

    extra = [] if after is None else [after]
    return list(pl.pallas_call(
        body, name=f"gather_wait_{tag}",
        out_shape=[pltpu.HBM(b.shape, b.dtype) for b in bufs],
        in_specs=[_HBM] * n + [_SEM, _SEM] + [_ANY] * len(extra), out_specs=[_HBM] * n,
        input_output_aliases={i: i for i in range(n)},
        compiler_params=pltpu.CompilerParams(has_side_effects=_EFFECT),
    )(*bufs, *sems, *extra))


def gather_forward(bufs, tag):
    n = len(bufs)

    def body(*refs):
        out_refs = refs[n:2 * n]
        send_sems, recv_sems = refs[2 * n:]
        x, y, c, me, chips, chip_idx = _place()

        def copy(k, j, half):
            part = out_refs[k].at[chip_idx[j], _half(half, out_refs[k].shape[1])]
            return pltpu.make_async_remote_copy(
                src_ref=part, dst_ref=part, send_sem=send_sems.at[3 * k + j], recv_sem=recv_sems.at[3 * k + j],
                device_id=(x, y, 1 - c), device_id_type=_MESH)

        sends = [copy(k, j, c) for j in range(3) for k in range(n)]
        for cp in sends:
            cp.start()
        for j in range(3):
            for k in range(n):
                copy(k, j, 1 - c).wait_recv()
        for cp in sends:
            cp.wait_send()

    return list(pl.pallas_call(
        body, name=f"gather_forward_{tag}",
        out_shape=[jax.ShapeDtypeStruct(b.shape, b.dtype) for b in bufs],
        in_specs=[_ANY] * n, out_specs=[_ANY] * n, input_output_aliases={i: i for i in range(n)},
        scratch_shapes=[pltpu.SemaphoreType.DMA((3 * n,)), pltpu.SemaphoreType.DMA((3 * n,))],
    )(*bufs))


def exchange_halves(bufs, tag):
    n = len(bufs)

    def body(*refs):
        g_refs, out_refs = refs[:n], refs[n:2 * n]
        send_sems, recv_sems = refs[2 * n:]
        x, y, c, *_ = _place()
        cps = [pltpu.make_async_remote_copy(
            src_ref=g_refs[b].at[:, _half(1 - c, g_refs[b].shape[1])], dst_ref=out_refs[b],
            send_sem=send_sems.at[b], recv_sem=recv_sems.at[b], device_id=(x, y, 1 - c), device_id_type=_MESH)
            for b in range(n)]
        for cp in cps:
            cp.start()
        for cp in cps:
            cp.wait()

    return pl.pallas_call(
        body, name=f"exchange_halves_{tag}",
        out_shape=[jax.ShapeDtypeStruct((N_CHIPS, b.shape[1] // 2, b.shape[2]), b.dtype) for b in bufs],
        in_specs=[_ANY] * n, out_specs=[_ANY] * n,
        scratch_shapes=[pltpu.SemaphoreType.DMA((n,)), pltpu.SemaphoreType.DMA((n,))],
    )(*bufs)


def add_halves(g, got, c_idx, name):
    hr, cols = got.shape[1], got.shape[2]
    tr = _step_rows(hr)
    steps = hr // tr

    def body(c_ref, g_ref, got_ref, o_ref):
        o_ref[...] = (g_ref[...].astype(f32) + got_ref[...].astype(f32)).astype(bf16)

    return pl.pallas_call(
        body, name=name,
        grid_spec=pltpu.PrefetchScalarGridSpec(
            num_scalar_prefetch=1, grid=(N_CHIPS, steps),
            in_specs=[pl.BlockSpec((None, tr, cols), lambda s, i, c: (s, c[0] * steps + i, 0)),
                      pl.BlockSpec((None, tr, cols), lambda s, i, c: (s, i, 0))],
            out_specs=pl.BlockSpec((None, tr, cols), lambda s, i, c: (s, i, 0))),
        out_shape=jax.ShapeDtypeStruct(got.shape, bf16),
        compiler_params=_cparams(("parallel", "parallel")),
    )(c_idx, g, got)


def sum_chips(t, got, place_idx, name):
    hr, cols = t.shape[1], t.shape[2]
    tr = _step_rows(hr)
    steps = hr // tr

    def body(idx_ref, t_ref, got_ref, o_ref):
        acc = t_ref[...].astype(f32)
        for j in range(3):
            acc = acc + got_ref[j].astype(f32)
        o_ref[...] = acc

    return pl.pallas_call(
        body, name=name,
        grid_spec=pltpu.PrefetchScalarGridSpec(
            num_scalar_prefetch=1, grid=(steps,),
            in_specs=[pl.BlockSpec((None, tr, cols), lambda i, idx: (idx[0], i, 0)),
                      pl.BlockSpec((3, tr, cols), lambda i, idx: (0, i, 0))],
            out_specs=pl.BlockSpec((tr, cols), lambda i, idx: (idx[1] * steps + i, 0))),
        out_shape=jax.ShapeDtypeStruct((2 * hr, cols), f32),
        compiler_params=_cparams(("parallel",)),
    )(place_idx, t, got)


def share_halves(bufs, tag):
    n = len(bufs)

    def body(*refs):
        out_refs = refs[n:2 * n]
        send_sems, recv_sems = refs[2 * n:]
        x, y, c, *_ = _place()

        def copy(b, half):
            part = out_refs[b].at[_half(half, out_refs[b].shape[0])]
            return pltpu.make_async_remote_copy(
                src_ref=part, dst_ref=part, send_sem=send_sems.at[b], recv_sem=recv_sems.at[b],
                device_id=(x, y, 1 - c), device_id_type=_MESH)

        for b in range(n):
            copy(b, c).start()
        for b in range(n):
            copy(b, 1 - c).wait_recv()
        for b in range(n):
            copy(b, c).wait_send()

    return pl.pallas_call(
        body, name=f"share_halves_{tag}",
        out_shape=[jax.ShapeDtypeStruct(b.shape, b.dtype) for b in bufs],
        in_specs=[_ANY] * n, out_specs=[_ANY] * n, input_output_aliases={i: i for i in range(n)},
        scratch_shapes=[pltpu.SemaphoreType.DMA((n,)), pltpu.SemaphoreType.DMA((n,))],
    )(*bufs)


def _scatter_copies(t_refs, land_refs, send_sems, recv_sems):
    x, y, c, me, chips, chip_idx = _place()
    return [pltpu.make_async_remote_copy(
        src_ref=t_refs[b].at[chip_idx[j]], dst_ref=land_refs[b].at[j], send_sem=send_sems.at[3 * b + j],
        recv_sem=recv_sems.at[3 * b + j], device_id=(*chips[j], c), device_id_type=_MESH)
        for j in range(3) for b in range(len(t_refs))]


def scatter_start(ts, tag):
    n = len(ts)
    lands = [lax.empty((3,) + t.shape[1:], t.dtype) for t in ts]

    def body(*refs):
        for cp in _scatter_copies(refs[:n], refs[n:2 * n], refs[2 * n], refs[2 * n + 1]):
            cp.start()
        token = refs[-1]
        token[...] = jnp.zeros_like(token)

    hbm = [pltpu.HBM(a.shape, a.dtype) for a in (*ts, *lands)]
    outs = pl.pallas_call(
        body, name=f"scatter_start_{tag}",
        out_shape=(pltpu.SemaphoreType.DMA((3 * n,)), pltpu.SemaphoreType.DMA((3 * n,)), *hbm,
                   jax.ShapeDtypeStruct((8, 128), f32)),
        in_specs=[_HBM] * (2 * n), out_specs=(_SEM, _SEM, *[_HBM] * (2 * n), _VMEM),
        input_output_aliases={i: 2 + i for i in range(2 * n)},
        compiler_params=pltpu.CompilerParams(has_side_effects=_EFFECT),
    )(*[pltpu.with_memory_space_constraint(a, pltpu.HBM) for a in (*ts, *lands)])
    return outs[0], outs[1], list(outs[2:2 + n]), list(outs[2 + n:2 + 2 * n]), outs[-1]


def scatter_wait(send_sems, recv_sems, ts, lands, after, tag):
    n = len(ts)

    def body(*refs):
        for cp in _scatter_copies(refs[:n], refs[n:2 * n], refs[2 * n], refs[2 * n + 1]):
            cp.wait_send()
            cp.wait_recv()

    outs = pl.pallas_call(
        body, name=f"scatter_wait_{tag}",
        out_shape=[pltpu.HBM(a.shape, a.dtype) for a in (*ts, *lands)],
        in_specs=[_HBM] * (2 * n) + [_SEM, _SEM, _ANY], out_specs=[_HBM] * (2 * n),
        input_output_aliases={i: i for i in range(2 * n)},
        compiler_params=pltpu.CompilerParams(has_side_effects=_EFFECT),
    )(*ts, *lands, send_sems, recv_sems, after)
    return list(outs[:n]), list(outs[n:])


class GradReducer:
    def __init__(self, c_idx, place_idx):
        self.c_idx, self.place_idx = c_idx, place_idx

    def start(self, bufs, tag):
        got = exchange_halves(bufs, tag)
        ts = [add_halves(b, g, self.c_idx, f"add_halves_{tag}{i}") for i, (b, g) in enumerate(zip(bufs, got))]
        send_sems, recv_sems, ts, lands, token = scatter_start(ts, tag)
        return (send_sems, recv_sems, ts, lands), token

    def finish(self, state, after, tag):
        ts, lands = scatter_wait(*state, after, tag)
        sums = [sum_chips(t, l, self.place_idx, f"sum_chips_{tag}{i}") for i, (t, l) in enumerate(zip(ts, lands))]
        return share_halves(sums, tag)


def allreduce_small(sp):
    def body(s_ref, out_ref, gather_ref, send_sems, recv_sems):
        x, y, c, me, chips, chip_idx = _place()
        sibling = (x, y, 1 - c)

        def copy(k, chip, core, to, src=None):
            dst = gather_ref.at[2 * chip + core]
            return pltpu.make_async_remote_copy(
                src_ref=dst if src is None else src, dst_ref=dst, send_sem=send_sems.at[k],
                recv_sem=recv_sems.at[k], device_id=to, device_id_type=_MESH)

        first = [copy(0, me, c, sibling, src=s_ref)]
        first += [copy(1 + j, me, c, (*chips[j], c), src=s_ref) for j in range(3)]
        for cp in first:
            cp.start()
        gather_ref[2 * me + c] = s_ref[...]
        passed = [copy(4 + j, chip_idx[j], c, sibling) for j in range(3)]
        for j in range(3):
            copy(1 + j, chip_idx[j], c, sibling).wait_recv()
            passed[j].start()
        copy(0, me, 1 - c, sibling).wait_recv()
        for j in range(3):
            copy(4 + j, chip_idx[j], 1 - c, sibling).wait_recv()
        for cp in first + passed:
            cp.wait_send()
        acc = gather_ref[0]
        for d in range(1, 2 * N_CHIPS):
            acc = acc + gather_ref[d]
        out_ref[...] = acc

    return pl.pallas_call(
        body, name="allreduce_small",
        out_shape=jax.ShapeDtypeStruct(sp.shape, sp.dtype),
        in_specs=[_VMEM], out_specs=_VMEM,
        scratch_shapes=[pltpu.VMEM((2 * N_CHIPS,) + sp.shape, sp.dtype),
                        pltpu.SemaphoreType.DMA((7,)), pltpu.SemaphoreType.DMA((7,))],
        compiler_params=_cparams(),
    )(sp)


def _n_rows(shape):
    n = 1
    for d in shape:
        n *= d
    return 8 * (-(-n // 8192))


def _pack(arrays, total_rows):
    parts = []
    for a in arrays:
        flat = a.reshape(-1)
        parts.append(jnp.pad(flat, (0, 1024 * _n_rows(a.shape) - flat.shape[0])).reshape(-1, 1024))
    rows = jnp.concatenate(parts, axis=0)
    return jnp.pad(rows, ((0, total_rows - rows.shape[0]), (0, 0)))


def _unpack(packed, shapes):
    out, r = [], 0
    for shp in shapes:
        n = 1
        for d in shp:
            n *= d
        nr = _n_rows(shp)
        out.append(packed[r:r + nr].reshape(-1)[:n].reshape(shp))
        r += nr
    return out


IN_SHARD, IN_PAD = 1284, 1408
QKV_SHARD, QKV_PAD = 320, 384


def _lane_padded(a, cols):
    return jnp.pad(a, ((0, 0), (0, cols - a.shape[1])))


_SMALL_SHAPES = (
    ("norm_mix_g", (2, 1024)), ("norm_mlp_g", (2, 1024)), ("final_norm_g", (1024,)), ("gm_ln_g", (1, 1024)),
    ("gm_ln_b", (1, 1024)), ("gm_w_s", (1, 8, 128, 128)), ("gm_b_s", (1, 8, 128)), ("ssm_conv_b", (1, 2048)),
    ("ssm_dt_bias", (1, 16)), ("ssm_a_log", (1, 16)), ("ssm_d", (1, 16)), ("ssm_norm_g", (1, 1024)),
    ("attn_sinks", (1, 16)), ("ssm_conv_w", (1, 4, 2048)), ("b_qkv", (1, 1280)), ("b_o", (1, 1024)),
)
_N_REPLICATED = 13
_SHARDED_SMALL = (("ssm_conv_w", 2, 512), ("b_qkv", 1, 320), ("b_o", 1, 256))
_SHARD_PACK_ROWS = 32


def _cols_by_owner(a):
    return a.transpose(1, 0, 2).reshape(a.shape[1], -1)


class WeightGatherer:
    def __init__(self, w, chip_idx):
        rows = lambda *parts: jnp.concatenate(parts, axis=0).astype(bf16)
        shards = [
            ("in", _lane_padded(w["w_in_even"][0], IN_PAD).astype(bf16)),
            ("l0", rows(w["w_out_even"][0], w["w_up"][0], w["w_down"][0])),
            ("l1", rows(w["w_o"][0], w["w_up"][1], w["w_down"][1])),
            ("qkv", _lane_padded(w["w_qkv"][0], QKV_PAD).astype(bf16)),
        ]
        shards.append(("small", _pack([w[n] for n, _, _ in _SHARDED_SMALL], _SHARD_PACK_ROWS)))
        placed = [place_shard(b, chip_idx, f"place_shard_{tag}") for tag, b in shards]
        self.sems, self.bufs = gather_start([[placed[0], placed[4]], placed[1:2], placed[2:4]])

    def _group(self, gi, after, tag):
        return gather_forward(gather_wait(self.bufs[gi], self.sems[gi], after, tag), tag)

    def mixer_in(self):
        g, small = self._group(0, None, "in")
        shard_shapes = [tuple(width if i == axis else d for i, d in enumerate(dict(_SMALL_SHAPES)[n]))
                        for n, axis, width in _SHARDED_SMALL]
        per_chip = [_unpack(small[s], shard_shapes) for s in range(N_CHIPS)]
        full = {n: jnp.concatenate([per_chip[s][i] for s in range(N_CHIPS)], axis=axis)
                for i, (n, axis, _) in enumerate(_SHARDED_SMALL)}
        return _lane_padded(_cols_by_owner(g[:, :, :IN_SHARD]), NP_IN), full

    def layer0(self, after):
        (g,) = self._group(1, after, "l0")
        return g[:, :512].reshape(2048, 1024), _cols_by_owner(g[:, 512:1536]), g[:, 1536:].reshape(4096, 1024)

    def layer1(self, after):
        g, q = self._group(2, after, "l1")
        return (_cols_by_owner(q[:, :, :QKV_SHARD]), g[:, :256].reshape(1024, 1024), _cols_by_owner(g[:, 256:1280]),
                g[:, 1280:].reshape(4096, 1024))


def _row2(v):
    return v.reshape(1, -1)


def _lane_pad(v):
    return jnp.pad(v, ((0, 0), (0, CH - v.shape[1])))


def _mlp_fwd(h, g_row, w_up, w_down, tag):
    y = rmsnorm_fwd(h, g_row, f"mlp_norm{tag}")
    a = matmul(y, w_up, dims="nn", name=f"mlp_up{tag}", out_dtype=bf16, tn=1024)
    out = matmul(a, w_down, dims="nn", name=f"mlp_down{tag}", a_pro=_relu2, epi=_add, epi_args=(("tile", h),))
    return out, y, a


def _mlp_bwd(dh_out, h, g_row, y, a, w_up, w_down, tag, after=None):
    da = matmul(dh_out, w_down, dims="nt", name=f"mlp_da{tag}", out_dtype=bf16, tn=1024,
                epi=_times_relu2_grad, epi_args=(("tile", a),), after=after)
    dw_down = matmul(a, dh_out, dims="tn", name=f"mlp_dwdown{tag}", out_dtype=bf16, a_pro=_relu2)
    dw_up = matmul(y, da, dims="tn", name=f"mlp_dwup{tag}", out_dtype=bf16, tn=1024, out_by_col_tile=True)
    dy = matmul(da, w_up, dims="nt", name=f"mlp_dy{tag}")
    dh, dg = rmsnorm_bwd(h, g_row, dy, dh_out, f"mlp_dnorm{tag}")
    return dh, dg, dw_up, dw_down


def _by_owner(a):
    return a.reshape(N_CHIPS, a.shape[0] // N_CHIPS, a.shape[1])


def _col_shards(a, shard, padded):
    return jnp.stack([_lane_padded(a[:, shard * s: shard * (s + 1)], padded) for s in range(N_CHIPS)])


def _local_step(x, target, weights, sm, reducer):
    w_up, w_down = [None, None], [None, None]
    w_in_p, sharded_small = weights.mixer_in()
    sm = {**sm, **sharded_small}
    mix_g = [_row2(sm["norm_mix_g"][i]) for i in range(2)]
    mlp_g = [_row2(sm["norm_mlp_g"][i]) for i in range(2)]
    mixer_prm = {
        "ln_g": sm["gm_ln_g"], "ln_b": sm["gm_ln_b"], "wm": sm["gm_w_s"][0],
        "bs_t": jnp.pad(sm["gm_b_s"][0].T, ((0, 0), (0, CH - N_BLK))),
        "conv_w": jnp.pad(sm["ssm_conv_w"][0], ((0, 4), (0, 0))), "conv_b": sm["ssm_conv_b"],
        "dt_bias": _lane_pad(sm["ssm_dt_bias"]), "a_log": _lane_pad(sm["ssm_a_log"]),
        "d_heads": _lane_pad(sm["ssm_d"]), "norm_g": sm["ssm_norm_g"],
    }
    sink_row = _lane_pad(sm["attn_sinks"])

    y0 = rmsnorm_fwd(x, mix_g[0], "mix_norm0")
    proj = matmul(y0, w_in_p, dims="nn", name="in_proj", tn=768)
    ab, hstates = mixer_fwd(proj, mixer_prm)
    w_out, w_up[0], w_down[0] = weights.layer0(ab)
    h1 = matmul(ab, w_out, dims="nn", name="out_proj", epi=_add, epi_args=(("tile", x),))
    h2, y1, a1 = _mlp_fwd(h1, mlp_g[0], w_up[0], w_down[0], 0)
    w_qkv, w_o, w_up[1], w_down[1] = weights.layer1(h2)
    y2 = rmsnorm_fwd(h2, mix_g[1], "mix_norm1")
    qkv = matmul(y2, w_qkv, dims="nn", name="qkv_proj", tn=QKV_DIM, epi=_add_bias, epi_args=(("row", sm["b_qkv"]),))
    att = attn_fwd(qkv, sink_row)
    h3 = matmul(att, w_o, dims="nn", name="o_proj", epi=_add_bias_res,
                epi_args=(("row", sm["b_o"]), ("tile", h2)))
    h4, y3, a3 = _mlp_fwd(h3, mlp_g[1], w_up[1], w_down[1], 1)
    loss, dh4, dg_final = final_loss(h4, _row2(sm["final_norm_g"]), target, "final_loss")

    dh3, dg_mlp1, dw_up1, dw_down1 = _mlp_bwd(dh4, h3, mlp_g[1], y3, a3, w_up[1], w_down[1], 1)
    db_o = colsum(dh3, "db_o")
    datt = matmul(dh3, w_o, dims="nt", name="attn_dout", out_dtype=bf16)
    dw_o = matmul(att, dh3, dims="tn", name="dw_o", out_dtype=bf16)
    dqkv, dsink = attn_bwd(qkv, sink_row, datt)
    db_qkv = colsum(dqkv, "db_qkv")
    dw_qkv = matmul(y2, dqkv, dims="tn", name="dw_qkv", out_dtype=bf16, tn=QKV_DIM)
    dy2 = matmul(dqkv, w_qkv, dims="nt", name="dy_qkv", tk=QKV_DIM)
    dh2, dg_mix1 = rmsnorm_bwd(h2, mix_g[1], dy2, dh3, "mix_dnorm1")
    layer1 = [jnp.concatenate([_by_owner(dw_o), dw_up1, _by_owner(dw_down1)], axis=1),
              _col_shards(dw_qkv, QKV_SHARD, QKV_PAD)]
    flight1, token1 = reducer.start(layer1, "l1")
    dh1, dg_mlp0, dw_up0, dw_down0 = _mlp_bwd(dh2, h1, mlp_g[0], y1, a1, w_up[0], w_down[0], 0, after=token1)
    r_l1, r_qkv = reducer.finish(flight1, dh1, "l1")
    flight0, token0 = reducer.start([jnp.concatenate([dw_up0, _by_owner(dw_down0)], axis=1)], "l0")
    dab = matmul(dh1, w_out, dims="nt", name="mixer_dout", tn=1024, after=token0)
    dw_out = matmul(ab, dh1, dims="tn", name="dw_out", out_dtype=bf16)
    dproj, dmix = mixer_bwd(proj, hstates, dab, mixer_prm)
    dw_in_p = matmul(y0, dproj, dims="tn", name="dw_in", out_dtype=bf16, tn=768)
    dy0 = matmul(dproj, w_in_p, dims="nt", name="dy_in")
    dx, dg_mix0 = rmsnorm_bwd(x, mix_g[0], dy0, dh1, "mix_dnorm0")
    (r_mlp0,) = reducer.finish(flight0, dx, "l0")
    flight_m, token_m = reducer.start([_by_owner(dw_out), _col_shards(dw_in_p, IN_SHARD, IN_PAD)], "mix")
    r_out, r_in = reducer.finish(flight_m, token_m, "mix")
    reduced = {
        "w_out_even": r_out[None], "w_in_even": r_in[None, :, :IN_SHARD], "w_qkv": r_qkv[None, :, :QKV_SHARD],
        "w_o": r_l1[None, :256], "w_up": jnp.stack([r_mlp0[:1024], r_l1[256:1280]]),
        "w_down": jnp.stack([r_mlp0[1024:], r_l1[1280:]]),
    }

    small_grads = {
        "norm_mix_g": jnp.concatenate([dg_mix0, dg_mix1], axis=0),
        "norm_mlp_g": jnp.concatenate([dg_mlp0, dg_mlp1], axis=0),
        "final_norm_g": dg_final[0], "gm_ln_g": dmix["ln_g"], "gm_ln_b": dmix["ln_b"],
        "gm_w_s": dmix["wm"][None], "gm_b_s": dmix["bs_t"][:, :N_BLK].T[None],
        "ssm_conv_b": dmix["conv_b"], "ssm_dt_bias": dmix["dt_bias"][:, :SSM_HEADS],
        "ssm_a_log": dmix["a_log"][:, :SSM_HEADS], "ssm_d": dmix["d_heads"][:, :SSM_HEADS],
        "ssm_norm_g": dmix["norm_g"], "attn_sinks": dsink[:, :SSM_HEADS],
        "ssm_conv_w": dmix["conv_w"][None, :4], "b_qkv": db_qkv, "b_o": db_o,
    }
    return loss, dx, reduced, small_grads


def kernel(x, norm_mix_g, norm_mlp_g, final_norm_g, w_in_even, w_out_even, gm_ln_g, gm_ln_b, gm_w_s, gm_b_s, ssm_conv_w, ssm_conv_b, ssm_dt_bias, ssm_a_log, ssm_d, ssm_norm_g, w_qkv, b_qkv, w_o, b_o, attn_sinks, w_up, w_down, loss_target, m_norm_mix_g, m_norm_mlp_g, m_final_norm_g, m_w_in_even, m_w_out_even, m_gm_ln_g, m_gm_ln_b, m_gm_w_s, m_gm_b_s, m_ssm_conv_w, m_ssm_conv_b, m_ssm_dt_bias, m_ssm_a_log, m_ssm_d, m_ssm_norm_g, m_w_qkv, m_b_qkv, m_w_o, m_b_o, m_attn_sinks, m_w_up, m_w_down, v_norm_mix_g, v_norm_mlp_g, v_final_norm_g, v_w_in_even, v_w_out_even, v_gm_ln_g, v_gm_ln_b, v_gm_w_s, v_gm_b_s, v_ssm_conv_w, v_ssm_conv_b, v_ssm_dt_bias, v_ssm_a_log, v_ssm_d, v_ssm_norm_g, v_w_qkv, v_b_qkv, v_w_o, v_b_o, v_attn_sinks, v_w_up, v_w_down):
    w = dict(norm_mix_g=norm_mix_g, norm_mlp_g=norm_mlp_g, final_norm_g=final_norm_g, w_in_even=w_in_even,
             w_out_even=w_out_even, gm_ln_g=gm_ln_g, gm_ln_b=gm_ln_b, gm_w_s=gm_w_s, gm_b_s=gm_b_s,
             ssm_conv_w=ssm_conv_w, ssm_conv_b=ssm_conv_b, ssm_dt_bias=ssm_dt_bias, ssm_a_log=ssm_a_log,
             ssm_d=ssm_d, ssm_norm_g=ssm_norm_g, w_qkv=w_qkv, b_qkv=b_qkv, w_o=w_o, b_o=b_o,
             attn_sinks=attn_sinks, w_up=w_up, w_down=w_down)
    m = dict(norm_mix_g=m_norm_mix_g, norm_mlp_g=m_norm_mlp_g, final_norm_g=m_final_norm_g,
             w_in_even=m_w_in_even, w_out_even=m_w_out_even, gm_ln_g=m_gm_ln_g, gm_ln_b=m_gm_ln_b,
             gm_w_s=m_gm_w_s, gm_b_s=m_gm_b_s, ssm_conv_w=m_ssm_conv_w, ssm_conv_b=m_ssm_conv_b,
             ssm_dt_bias=m_ssm_dt_bias, ssm_a_log=m_ssm_a_log, ssm_d=m_ssm_d, ssm_norm_g=m_ssm_norm_g,
             w_qkv=m_w_qkv, b_qkv=m_b_qkv, w_o=m_w_o, b_o=m_b_o, attn_sinks=m_attn_sinks, w_up=m_w_up,
             w_down=m_w_down)
    v = dict(norm_mix_g=v_norm_mix_g, norm_mlp_g=v_norm_mlp_g, final_norm_g=v_final_norm_g,
             w_in_even=v_w_in_even, w_out_even=v_w_out_even, gm_ln_g=v_gm_ln_g, gm_ln_b=v_gm_ln_b,
             gm_w_s=v_gm_w_s, gm_b_s=v_gm_b_s, ssm_conv_w=v_ssm_conv_w, ssm_conv_b=v_ssm_conv_b,
             ssm_dt_bias=v_ssm_dt_bias, ssm_a_log=v_ssm_a_log, ssm_d=v_ssm_d, ssm_norm_g=v_ssm_norm_g,
             w_qkv=v_w_qkv, b_qkv=v_b_qkv, w_o=v_w_o, b_o=v_b_o, attn_sinks=v_attn_sinks, w_up=v_w_up,
             w_down=v_w_down)
    names = ("norm_mix_g", "norm_mlp_g", "final_norm_g", "w_in_even", "w_out_even", "gm_ln_g", "gm_ln_b",
             "gm_w_s", "gm_b_s", "ssm_conv_w", "ssm_conv_b", "ssm_dt_bias", "ssm_a_log", "ssm_d", "ssm_norm_g",
             "w_qkv", "b_qkv", "w_o", "b_o", "attn_sinks", "w_up", "w_down")

    cx, cy, cc = lax.axis_index("x"), lax.axis_index("y"), lax.axis_index("c")
    chip = 2 * cx + cy
    c_idx = jnp.reshape(cc, (1,)).astype(jnp.int32)
    chip_idx = jnp.reshape(chip, (1,)).astype(jnp.int32)

    weights = WeightGatherer(w, chip_idx)
    sm = {n: w[n] for n, _ in _SMALL_SHAPES[:_N_REPLICATED]}

    reducer = GradReducer(c_idx, jnp.concatenate([chip_idx, c_idx]))
    loss_part, dx, grads, small_grads = _local_step(x[0], loss_target[0], weights, sm, reducer)
    loss = lax.psum(loss_part[0, 0], ("x", "y", "c"))

    small_sum = allreduce_small(_pack([small_grads[n] for n, _ in _SMALL_SHAPES], SMALL_ROWS))
    small_full = dict(zip([n for n, _ in _SMALL_SHAPES], _unpack(small_sum, [s for _, s in _SMALL_SHAPES])))
    for n, _ in _SMALL_SHAPES[:_N_REPLICATED]:
        grads[n] = small_full[n]
    for n, axis, width in _SHARDED_SMALL:
        grads[n] = lax.dynamic_slice_in_dim(small_full[n], chip * width, width, axis)
    grads = {n: grads[n].reshape(w[n].shape) for n in names}

    delta, new_m, new_v = {}, {}, {}
    for n in names:
        shape = (1,) + w[n].shape if w[n].ndim == 1 else w[n].shape
        outs = adamw(*[d[n].reshape(shape) for d in (w, grads, m, v)], f"adamw_{n}")
        delta[n], new_m[n], new_v[n] = (o.reshape(w[n].shape) for o in outs)

    return (loss, dx[None], *[grads[n] for n in names], *[delta[n] for n in names],
            *[new_m[n] for n in names], *[new_v[n] for n in names])
```

```python
import functools

import jax
import jax.numpy as jnp
from jax import lax
from jax.experimental import pallas as pl
from jax.experimental.pallas import tpu as pltpu

f32 = jnp.float32
bf16 = jnp.bfloat16
MXU_DTYPE = bf16

RMS_EPS = 1e-5
LN_EPS = 1e-5
D_MODEL = 1024
D_FF = 4096
CH = 128
N_BLK = 8
SSM_HEADS = 16
IN_EVEN = 5136
NP_IN = 5376
OFF_U, OFF_V, OFF_Z, OFF_X, OFF_DT = 0, 1024, 2048, 3072, 5120
XBC_BLKS = 16
QKV_DIM = 1280
ATT_SCALE = 64 ** -0.5

ADAM_LR = 0.001
ADAM_B1 = 0.9
ADAM_B2 = 0.999
ADAM_EPS = 1e-08
ADAM_WD = 0.01
ADAM_STEP = 10

VMEM_LIMIT_BYTES = 48 * 1024 * 1024
N_CHIPS = 4
SMALL_ROWS = 256

NN = ((1,), (0,))
NT = ((1,), (1,))
TN = ((0,), (0,))


def _mm(a, b, dims):
    return lax.dot_general(a.astype(MXU_DTYPE), b.astype(MXU_DTYPE), (dims, ((), ())),
                           preferred_element_type=f32)


def _mm_exact(a, b):
    return jnp.dot(a, b, preferred_element_type=f32, precision=lax.Precision.HIGHEST)


def _cparams(sem=None):
    return pltpu.CompilerParams(dimension_semantics=sem, vmem_limit_bytes=VMEM_LIMIT_BYTES)


@jax.custom_vjp
def _swap64(x):
    return pltpu.roll(x, 64, axis=1)


_swap64.defvjp(lambda x: (pltpu.roll(x, 64, axis=1), None), lambda _, g: (pltpu.roll(g, 64, axis=1),))


@jax.custom_vjp
def _top_rows(x):
    return x[:x.shape[0] // 2]


_top_rows.defvjp(lambda x: (x[:x.shape[0] // 2], None),
                 lambda _, g: (jnp.concatenate([g, jnp.zeros_like(g)], axis=0),))


@jax.custom_vjp
def _bottom_rows(x):
    return x[x.shape[0] // 2:]


_bottom_rows.defvjp(lambda x: (x[x.shape[0] // 2:], None),
                    lambda _, g: (jnp.concatenate([jnp.zeros_like(g), g], axis=0),))


def _make_delay(k):
    @jax.custom_vjp
    def delay(ext):
        return pltpu.roll(ext, k, axis=0)[8:, :]

    def fwd(ext):
        return delay(ext), None

    def bwd(_, g):
        gp = jnp.concatenate([jnp.zeros((8, g.shape[1]), g.dtype), g], axis=0)
        return (pltpu.roll(gp, gp.shape[0] - k, axis=0),)

    delay.defvjp(fwd, bwd)
    return delay


_DELAYS = {k: _make_delay(k) for k in (1, 2, 3)}


def _col(m, lane, h):
    return jnp.sum(jnp.where(lane == h, m, 0.0), axis=1, keepdims=True)


def _row(m, sub, h):
    return jnp.sum(jnp.where(sub == h, m, 0.0), axis=0, keepdims=True)


def _mixer_chunk(us, vs, zs, xbcs, halos, dtblk, hps, prm):
    lane = lax.broadcasted_iota(jnp.int32, (CH, CH), 1)
    sub = lax.broadcasted_iota(jnp.int32, (CH, CH), 0)
    left = lane < 64
    top = sub < 64
    causal = sub >= lane

    gus = [jax.nn.gelu(u) for u in us]
    gvs = [jax.nn.gelu(v) for v in vs]
    mu = sum(jnp.sum(g, axis=1, keepdims=True) for g in gvs) / D_MODEL
    cen = [g - mu for g in gvs]
    var = sum(jnp.sum(c * c, axis=1, keepdims=True) for c in cen) / D_MODEL
    rstd = lax.rsqrt(var + LN_EPS)
    a_out = []
    for g in range(N_BLK):
        vn = cen[g] * rstd * prm["ln_g"][g] + prm["ln_b"][g]
        w = jnp.where(causal, prm["wm"][g], 0.0)
        mixed = _mm(w, vn, NN) + _col(prm["bs_t"], lane, g)
        a_out.append(gus[g] * mixed)

    act = []
    for b in range(XBC_BLKS):
        w8 = prm["conv_w"][b]
        sub8 = lax.broadcasted_iota(jnp.int32, w8.shape, 0)
        ext = jnp.concatenate([halos[b], xbcs[b]], axis=0)
        conv = xbcs[b] * _row(w8, sub8, 3) + prm["conv_b"][b]
        for k in (1, 2, 3):
            conv = conv + _DELAYS[k](ext) * _row(w8, sub8, 3 - k)
        act.append(jax.nn.silu(conv))

    dt = jax.nn.softplus(dtblk + prm["dt_bias"])
    a_neg = -jnp.exp(prm["a_log"])
    tri = causal.astype(f32)
    acum = _mm_exact(tri, dt * a_neg)
    acum_t = acum.T
    dt_t = dt.T
    last = sub == CH - 1
    ys, h_out = [], []
    for grp in range(4):
        bm = act[8 + grp]
        cm = act[12 + grp]
        cb = _mm(cm, bm, NT)
        for p in (2 * grp, 2 * grp + 1):
            h0, h1 = 2 * p, 2 * p + 1
            xp = act[p]
            hp = hps[p]
            wis = []
            for h in (h0, h1):
                seg = _col(acum, lane, h) - _row(acum_t, sub, h)
                decay = jnp.exp(jnp.where(causal, seg, -jnp.inf))
                wis.append(cb * decay * _row(dt_t, sub, h))
            wcat = jnp.concatenate(wis, axis=1)
            xbd = jnp.concatenate([jnp.where(left, xp, 0.0), jnp.where(left, 0.0, xp)], axis=0)
            y_diag = _mm(wcat, xbd, NN)
            a_end = [jnp.sum(jnp.where(last & (lane == h), acum, 0.0), keepdims=True) for h in (h0, h1)]
            a_col = jnp.where(left, _col(acum, lane, h0), _col(acum, lane, h1))
            dt_col = jnp.where(left, _col(dt, lane, h0), _col(dt, lane, h1))
            to_end = jnp.exp(jnp.where(left, a_end[0], a_end[1]) - a_col) * dt_col
            states = _mm(xp * to_end, bm, TN)
            chunk_decay = jnp.where(top, jnp.exp(a_end[0]), jnp.exp(a_end[1]))
            h_out.append(chunk_decay * hp + states)
            y_off = jnp.exp(a_col) * _mm(cm, hp, NT)
            d_skip = jnp.where(left[:1], _col(prm["d_heads"], lane[:1], h0), _col(prm["d_heads"], lane[:1], h1))
            ys.append((y_diag + y_off + xp * d_skip) * jax.nn.silu(zs[p]))

    b_out = []
    for grp in range(4):
        pair = (ys[2 * grp], ys[2 * grp + 1])
        ms = sum(jnp.sum(y * y, axis=1, keepdims=True) for y in pair) / 256.0
        r = lax.rsqrt(ms + RMS_EPS)
        for j, y in enumerate(pair):
            b_out.append(y * r * prm["norm_g"][2 * grp + j])
    return a_out, b_out, h_out


def _attn_block(qps, kprev, kcur, vprev, vcur, sink_row, first):
    lane = lax.broadcasted_iota(jnp.int32, (CH, CH), 1)
    left = lane < 64
    row2 = lax.broadcasted_iota(jnp.int32, (2 * CH, CH), 0)
    key2 = lax.broadcasted_iota(jnp.int32, (2 * CH, CH), 1)
    upper = row2 < CH
    own = key2 <= jnp.where(upper, row2, row2 - CH)

    def both_halves(a):
        sw = _swap64(a)
        return [jnp.where(left, a, sw), jnp.where(left, sw, a)]

    kc, kp, vc, vp = both_halves(kcur), both_halves(kprev), both_halves(vcur), both_halves(vprev)
    outs = []
    for p in range(N_BLK):
        j = p // 4
        q2 = jnp.concatenate([jnp.where(left, qps[p], 0.0), jnp.where(left, 0.0, qps[p])], axis=0)
        s_prev = jnp.where(first, -jnp.inf, _mm(q2, kp[j], NT) * ATT_SCALE)
        s = jnp.where(own, _mm(q2, kc[j], NT) * ATT_SCALE, s_prev)
        sink = jnp.where(upper[:, :1], _col(sink_row, lane[:1], 2 * p), _col(sink_row, lane[:1], 2 * p + 1))
        m = lax.stop_gradient(jnp.maximum(jnp.max(s, axis=1, keepdims=True), sink))
        pexp = jnp.exp(s - m)
        probs = pexp / (jnp.sum(pexp, axis=1, keepdims=True) + jnp.exp(sink - m))
        o = _mm(jnp.where(own, probs, 0.0), vc[j], NN) + _mm(jnp.where(own, 0.0, probs), vp[j], NN)
        outs.append(jnp.where(left, _top_rows(o), _bottom_rows(o)))
    return outs


def _rmsnorm(x, g):
    r = lax.rsqrt(jnp.mean(x * x, axis=-1, keepdims=True) + RMS_EPS)
    return x * r * g


def rmsnorm_fwd(x, g_row, name):
    s, d = x.shape
    tm = min(512, s)

    def body(x_ref, g_ref, y_ref):
        y_ref[...] = _rmsnorm(x_ref[...], g_ref[...]).astype(bf16)

    return pl.pallas_call(
        body, name=name, grid=(s // tm,),
        in_specs=[pl.BlockSpec((tm, d), lambda i: (i, 0)), pl.BlockSpec((1, d), lambda i: (0, 0))],
        out_specs=pl.BlockSpec((tm, d), lambda i: (i, 0)),
        out_shape=jax.ShapeDtypeStruct((s, d), bf16),
        compiler_params=_cparams(("parallel",)),
    )(x, g_row)


def rmsnorm_bwd(x, g_row, dy, res, name):
    s, d = x.shape
    tm = min(512, s)

    def body(x_ref, g_ref, dy_ref, res_ref, dx_ref, dg_ref):
        @pl.when(pl.program_id(0) == 0)
        def _():
            dg_ref[...] = jnp.zeros_like(dg_ref)

        _, vjp = jax.vjp(_rmsnorm, x_ref[...], g_ref[...])
        dx, dg = vjp(dy_ref[...])
        dx_ref[...] = res_ref[...] + dx
        dg_ref[...] += dg

    tile = pl.BlockSpec((tm, d), lambda i: (i, 0))
    row = pl.BlockSpec((1, d), lambda i: (0, 0))
    return pl.pallas_call(
        body, name=name, grid=(s // tm,),
        in_specs=[tile, row, tile, tile], out_specs=[tile, row],
        out_shape=[jax.ShapeDtypeStruct((s, d), f32), jax.ShapeDtypeStruct((1, d), f32)],
        compiler_params=_cparams(("arbitrary",)),
    )(x, g_row, dy, res)


def final_loss(h, g_row, target, name):
    s, d = h.shape
    tm = min(512, s)

    def body(h_ref, g_ref, t_ref, loss_ref, dh_ref, dg_ref):
        @pl.when(pl.program_id(0) == 0)
        def _():
            dg_ref[...] = jnp.zeros_like(dg_ref)
            loss_ref[...] = jnp.zeros_like(loss_ref)

        def f(hv, gv):
            err = jnp.square(_rmsnorm(hv, gv) - t_ref[...])
            return 0.5 * jnp.sum(jnp.mean(err, axis=-1, keepdims=True), axis=0, keepdims=True)

        loss, vjp = jax.vjp(f, h_ref[...], g_ref[...])
        dh, dg = vjp(jnp.ones_like(loss))
        dh_ref[...] = dh
        dg_ref[...] += dg
        loss_ref[...] += jnp.broadcast_to(loss, loss_ref.shape)

    tile = pl.BlockSpec((tm, d), lambda i: (i, 0))
    row = pl.BlockSpec((1, d), lambda i: (0, 0))
    return pl.pallas_call(
        body, name=name, grid=(s // tm,),
        in_specs=[tile, row, tile],
        out_specs=[pl.BlockSpec((1, 128), lambda i: (0, 0)), tile, row],
        out_shape=[jax.ShapeDtypeStruct((1, 128), f32), jax.ShapeDtypeStruct((s, d), f32),
                   jax.ShapeDtypeStruct((1, d), f32)],
        compiler_params=_cparams(("arbitrary",)),
    )(h, g_row, target)


def colsum(x, name):
    s, n = x.shape
    tm = min(512, s)

    def body(x_ref, o_ref):
        @pl.when(pl.program_id(0) == 0)
        def _():
            o_ref[...] = jnp.zeros_like(o_ref)

        o_ref[...] += jnp.sum(x_ref[...].astype(f32), axis=0, keepdims=True)

    return pl.pallas_call(
        body, name=name, grid=(s // tm,),
        in_specs=[pl.BlockSpec((tm, n), lambda i: (i, 0))],
        out_specs=pl.BlockSpec((1, n), lambda i: (0, 0)),
        out_shape=jax.ShapeDtypeStruct((1, n), f32),
        compiler_params=_cparams(("arbitrary",)),
    )(x)


def _fit(dim, want):
    if dim <= want:
        return dim
    t = want
    while dim % t:
        t -= 128
    return t


def matmul(a, b, *, dims, name, out_dtype=f32, tm=1024, tn=512, tk=8192, a_pro=None, epi=None, epi_args=(),
           out_by_col_tile=False, after=None):
    if dims == "nn":
        (m, k), n = a.shape, b.shape[1]
    elif dims == "nt":
        (m, k), n = a.shape, b.shape[0]
    else:
        (k, m), n = a.shape, b.shape[1]
    tm, tn, tk = _fit(m, tm), _fit(n, tn), _fit(k, tk)
    nk = k // tk
    if dims == "nn":
        a_spec = pl.BlockSpec((tm, tk), lambda i, j, kk: (i, kk))
        b_spec = pl.BlockSpec((tk, tn), lambda i, j, kk: (kk, j))
        dn = NN
    elif dims == "nt":
        a_spec = pl.BlockSpec((tm, tk), lambda i, j, kk: (i, kk))
        b_spec = pl.BlockSpec((tn, tk), lambda i, j, kk: (j, kk))
        dn = NT
    else:
        a_spec = pl.BlockSpec((tk, tm), lambda i, j, kk: (kk, i))
        b_spec = pl.BlockSpec((tk, tn), lambda i, j, kk: (kk, j))
        dn = TN
    e_specs = [pl.BlockSpec((tm, tn), lambda i, j, kk: (i, j)) if kind == "tile"
               else pl.BlockSpec((1, tn), lambda i, j, kk: (0, j)) for kind, _ in epi_args]
    n_epi = len(epi_args)
    order_specs = [] if after is None else [pl.BlockSpec((8, 128), lambda i, j, kk: (0, 0))]
    order_args = [] if after is None else [after]

    def body(*refs):
        a_ref, b_ref = refs[0], refs[1]
        e_refs = refs[2:2 + n_epi]
        n_in = 2 + n_epi + len(order_args)
        o_ref = refs[n_in]
        av = a_ref[...]
        if a_pro is not None:
            av = a_pro(av)
        part = _mm(av, b_ref[...], dn)

        def finish(acc):
            if epi is not None:
                acc = epi(acc, *[r[...] for r in e_refs])
            o_ref[...] = acc.astype(out_dtype)

        if nk == 1:
            finish(part)
        else:
            acc_ref = refs[n_in + 1]
            kk = pl.program_id(2)

            @pl.when(kk == 0)
            def _():
                acc_ref[...] = part

            @pl.when(kk > 0)
            def _():
                acc_ref[...] += part

            @pl.when(kk == nk - 1)
            def _():
                finish(acc_ref[...])

    if out_by_col_tile:
        out_spec = pl.BlockSpec((None, tm, tn), lambda i, j, kk: (j, i, 0))
        out_shape = jax.ShapeDtypeStruct((n // tn, m, tn), out_dtype)
    else:
        out_spec = pl.BlockSpec((tm, tn), lambda i, j, kk: (i, j))
        out_shape = jax.ShapeDtypeStruct((m, n), out_dtype)
    return pl.pallas_call(
        body, name=name, grid=(m // tm, n // tn, nk),
        in_specs=[a_spec, b_spec] + e_specs + order_specs,
        out_specs=out_spec,
        out_shape=out_shape,
        scratch_shapes=[pltpu.VMEM((tm, tn), f32)] if nk > 1 else [],
        compiler_params=_cparams(("parallel", "parallel", "arbitrary")),
    )(a, b, *[arr for _, arr in epi_args], *order_args)


def _relu2(a):
    r = jnp.maximum(a.astype(f32), 0.0)
    return r * r


def _add(acc, t):
    return acc + t


def _add_bias(acc, t):
    return acc + t


def _add_bias_res(acc, bias, res):
    return acc + bias + res


def _times_relu2_grad(acc, a):
    return acc * (2.0 * jnp.maximum(a.astype(f32), 0.0))


_MIXER_PARAM_SHAPES = (
    ("ln_g", (1, D_MODEL)), ("ln_b", (1, D_MODEL)), ("wm", (N_BLK, CH, CH)), ("bs_t", (CH, CH)),
    ("conv_w", (8, 2048)), ("conv_b", (1, 2048)), ("dt_bias", (1, CH)), ("a_log", (1, CH)),
    ("d_heads", (1, CH)), ("norm_g", (1, D_MODEL)),
)


def _blocks(v, n, off=0):
    return [v[:, off + i * CH: off + (i + 1) * CH] for i in range(n)]


def _split_mixer_params(vals):
    p = dict(vals)
    return {
        "ln_g": _blocks(p["ln_g"], N_BLK), "ln_b": _blocks(p["ln_b"], N_BLK),
        "wm": [p["wm"][g] for g in range(N_BLK)], "bs_t": p["bs_t"],
        "conv_w": _blocks(p["conv_w"], XBC_BLKS), "conv_b": _blocks(p["conv_b"], XBC_BLKS),
        "dt_bias": p["dt_bias"], "a_log": p["a_log"], "d_heads": p["d_heads"],
        "norm_g": _blocks(p["norm_g"], N_BLK),
    }


def _mixer_leaves(proj_ref, halo_ref, keep_halo):
    pv = proj_ref
    us = [pv[:, OFF_U + i * CH: OFF_U + (i + 1) * CH] for i in range(N_BLK)]
    vs = [pv[:, OFF_V + i * CH: OFF_V + (i + 1) * CH] for i in range(N_BLK)]
    zs = [pv[:, OFF_Z + i * CH: OFF_Z + (i + 1) * CH] for i in range(N_BLK)]
    xbcs = [pv[:, OFF_X + i * CH: OFF_X + (i + 1) * CH] for i in range(XBC_BLKS)]
    halos = [halo_ref[:, OFF_X + i * CH: OFF_X + (i + 1) * CH] * keep_halo for i in range(XBC_BLKS)]
    dtblk = pv[:, OFF_DT: OFF_DT + CH]
    return us, vs, zs, xbcs, halos, dtblk


def mixer_fwd(proj, prm):
    s = proj.shape[0]
    nc = s // CH
    names = [n for n, _ in _MIXER_PARAM_SHAPES]

    def body(proj_ref, halo_ref, *rest):
        p_refs = rest[:len(names)]
        ab_ref, hs_ref, h_ref = rest[len(names):]
        c = pl.program_id(0)

        @pl.when(c == 0)
        def _():
            h_ref[...] = jnp.zeros_like(h_ref)

        hs_ref[...] = h_ref[...]
        keep = (c > 0).astype(f32)
        us, vs, zs, xbcs, halos, dtblk = _mixer_leaves(proj_ref, halo_ref, keep)
        hps = [h_ref[i * CH:(i + 1) * CH, :] for i in range(N_BLK)]
        p = _split_mixer_params({n: r[...] for n, r in zip(names, p_refs)})
        a_out, b_out, h_out = _mixer_chunk(us, vs, zs, xbcs, halos, dtblk, hps, p)
        for i in range(N_BLK):
            ab_ref[:, i * CH:(i + 1) * CH] = a_out[i].astype(bf16)
            ab_ref[:, D_MODEL + i * CH: D_MODEL + (i + 1) * CH] = b_out[i].astype(bf16)
            h_ref[i * CH:(i + 1) * CH, :] = h_out[i]

    def const(shape):
        return pl.BlockSpec(shape, lambda c: (0,) * len(shape))

    return pl.pallas_call(
        body, name="mixer_fwd", grid=(nc,),
        in_specs=[pl.BlockSpec((CH, NP_IN), lambda c: (c, 0)),
                  pl.BlockSpec((8, NP_IN), lambda c: (jnp.maximum(c * (CH // 8) - 1, 0), 0))]
                 + [const(shp) for _, shp in _MIXER_PARAM_SHAPES],
        out_specs=[pl.BlockSpec((CH, 2 * D_MODEL), lambda c: (c, 0)),
                   pl.BlockSpec((None, D_MODEL, CH), lambda c: (c, 0, 0))],
        out_shape=[jax.ShapeDtypeStruct((s, 2 * D_MODEL), bf16), jax.ShapeDtypeStruct((nc, D_MODEL, CH), f32)],
        scratch_shapes=[pltpu.VMEM((D_MODEL, CH), f32)],
        compiler_params=_cparams(("arbitrary",)),
    )(proj, proj, *[prm[n] for n in names])


def mixer_bwd(proj, hstates, dab, prm):
    s = proj.shape[0]
    nc = s // CH
    names = [n for n, _ in _MIXER_PARAM_SHAPES]
    npar = len(names)

    def body(proj_ref, halo_ref, hs_ref, dab_ref, *rest):
        p_refs = rest[:npar]
        dproj_ref = rest[npar]
        g_refs = rest[npar + 1: 2 * npar + 1]
        dh_ref, dhalo_ref = rest[2 * npar + 1:]
        i = pl.program_id(0)
        c = nc - 1 - i

        @pl.when(i == 0)
        def _():
            dh_ref[...] = jnp.zeros_like(dh_ref)
            dhalo_ref[...] = jnp.zeros_like(dhalo_ref)
            for r in g_refs:
                r[...] = jnp.zeros_like(r)

        keep = (c > 0).astype(f32)
        us, vs, zs, xbcs, halos, dtblk = _mixer_leaves(proj_ref, halo_ref, keep)
        hps = [hs_ref[j * CH:(j + 1) * CH, :] for j in range(N_BLK)]
        pvals = {n: r[...] for n, r in zip(names, p_refs)}

        def fn(us, vs, zs, xbcs, halos, dtblk, hps, pvals):
            return _mixer_chunk(us, vs, zs, xbcs, halos, dtblk, hps, _split_mixer_params(pvals))

        _, vjp = jax.vjp(fn, us, vs, zs, xbcs, halos, dtblk, hps, pvals)
        da = [dab_ref[:, j * CH:(j + 1) * CH].astype(f32) for j in range(N_BLK)]
        db = [dab_ref[:, D_MODEL + j * CH: D_MODEL + (j + 1) * CH].astype(f32) for j in range(N_BLK)]
        dh = [dh_ref[j * CH:(j + 1) * CH, :] for j in range(N_BLK)]
        dus, dvs, dzs, dxbcs, dhalos, ddt, dhps, dp = vjp((da, db, dh))

        for j in range(N_BLK):
            dproj_ref[:, OFF_U + j * CH: OFF_U + (j + 1) * CH] = dus[j].astype(bf16)
            dproj_ref[:, OFF_V + j * CH: OFF_V + (j + 1) * CH] = dvs[j].astype(bf16)
            dproj_ref[:, OFF_Z + j * CH: OFF_Z + (j + 1) * CH] = dzs[j].astype(bf16)
            dh_ref[j * CH:(j + 1) * CH, :] = dhps[j]
        zeros_top = jnp.zeros((CH - 8, CH), f32)
        for j in range(XBC_BLKS):
            late = jnp.concatenate([zeros_top, dhalo_ref[:, j * CH:(j + 1) * CH]], axis=0)
            dproj_ref[:, OFF_X + j * CH: OFF_X + (j + 1) * CH] = (dxbcs[j] + late).astype(bf16)
        for j in range(XBC_BLKS):
            dhalo_ref[:, j * CH:(j + 1) * CH] = dhalos[j] * keep
        lane = lax.broadcasted_iota(jnp.int32, (CH, CH), 1)
        dproj_ref[:, OFF_DT: OFF_DT + CH] = jnp.where(lane < SSM_HEADS, ddt, 0.0).astype(bf16)
        dproj_ref[:, OFF_DT + CH:] = jnp.zeros((CH, NP_IN - OFF_DT - CH), bf16)
        for n, r in zip(names, g_refs):
            r[...] += dp[n]

    def const(shape):
        return pl.BlockSpec(shape, lambda i: (0,) * len(shape))

    outs = pl.pallas_call(
        body, name="mixer_bwd", grid=(nc,),
        in_specs=[pl.BlockSpec((CH, NP_IN), lambda i: (nc - 1 - i, 0)),
                  pl.BlockSpec((8, NP_IN), lambda i: (jnp.maximum((nc - 1 - i) * (CH // 8) - 1, 0), 0)),
                  pl.BlockSpec((None, D_MODEL, CH), lambda i: (nc - 1 - i, 0, 0)),
                  pl.BlockSpec((CH, 2 * D_MODEL), lambda i: (nc - 1 - i, 0))]
                 + [const(shp) for _, shp in _MIXER_PARAM_SHAPES],
        out_specs=[pl.BlockSpec((CH, NP_IN), lambda i: (nc - 1 - i, 0))]
                  + [const(shp) for _, shp in _MIXER_PARAM_SHAPES],
        out_shape=[jax.ShapeDtypeStruct((s, NP_IN), bf16)]
                  + [jax.ShapeDtypeStruct(shp, f32) for _, shp in _MIXER_PARAM_SHAPES],
        scratch_shapes=[pltpu.VMEM((D_MODEL, CH), f32), pltpu.VMEM((8, 2048), f32)],
        compiler_params=_cparams(("arbitrary",)),
    )(proj, proj, hstates, dab, *[prm[n] for n in names])
    return outs[0], dict(zip(names, outs[1:]))


_K_BLK = D_MODEL // CH
_V_BLK = _K_BLK + 1


def _attn_specs(rev, nb):
    def blk(i):
        return nb - 1 - i if rev else i

    q_spec = pl.BlockSpec((CH, D_MODEL), lambda i: (blk(i), 0))
    kv = lambda col, prev: pl.BlockSpec(
        (CH, CH), lambda i: (jnp.maximum(blk(i) - 1, 0) if prev else blk(i), col))
    return q_spec, [kv(_K_BLK, True), kv(_K_BLK, False), kv(_V_BLK, True), kv(_V_BLK, False)]


def attn_fwd(qkv, sink_row):
    s = qkv.shape[0]
    nb = s // CH

    def body(q_ref, kp_ref, kc_ref, vp_ref, vc_ref, sink_ref, o_ref):
        qps = [q_ref[:, p * CH:(p + 1) * CH] for p in range(N_BLK)]
        outs = _attn_block(qps, kp_ref[...], kc_ref[...], vp_ref[...], vc_ref[...], sink_ref[...],
                           pl.program_id(0) == 0)
        for p in range(N_BLK):
            o_ref[:, p * CH:(p + 1) * CH] = outs[p].astype(bf16)

    q_spec, kv_specs = _attn_specs(False, nb)
    return pl.pallas_call(
        body, name="attn_fwd", grid=(nb,),
        in_specs=[q_spec] + kv_specs + [pl.BlockSpec((1, CH), lambda i: (0, 0))],
        out_specs=pl.BlockSpec((CH, D_MODEL), lambda i: (i, 0)),
        out_shape=jax.ShapeDtypeStruct((s, D_MODEL), bf16),
        compiler_params=_cparams(("parallel",)),
    )(qkv, qkv, qkv, qkv, qkv, sink_row)


def attn_bwd(qkv, sink_row, dout):
    s = qkv.shape[0]
    nb = s // CH

    def body(q_ref, kp_ref, kc_ref, vp_ref, vc_ref, sink_ref, do_ref, dqkv_ref, dsink_ref, carry_ref):
        i = pl.program_id(0)
        blk = nb - 1 - i

        @pl.when(i == 0)
        def _():
            dsink_ref[...] = jnp.zeros_like(dsink_ref)
            carry_ref[...] = jnp.zeros_like(carry_ref)

        qps = [q_ref[:, p * CH:(p + 1) * CH] for p in range(N_BLK)]
        first = blk == 0
        _, vjp = jax.vjp(lambda *a: _attn_block(*a, first), qps, kp_ref[...], kc_ref[...], vp_ref[...],
                         vc_ref[...], sink_ref[...])
        dos = [do_ref[:, p * CH:(p + 1) * CH].astype(f32) for p in range(N_BLK)]
        dqs, dkp, dkc, dvp, dvc, dsink = vjp(dos)
        for p in range(N_BLK):
            dqkv_ref[:, p * CH:(p + 1) * CH] = dqs[p].astype(bf16)
        dqkv_ref[:, D_MODEL: D_MODEL + CH] = (dkc + carry_ref[0]).astype(bf16)
        dqkv_ref[:, D_MODEL + CH:] = (dvc + carry_ref[1]).astype(bf16)
        keep = jnp.logical_not(first).astype(f32)
        carry_ref[0] = dkp * keep
        carry_ref[1] = dvp * keep
        dsink_ref[...] += dsink

    q_spec, kv_specs = _attn_specs(True, nb)
    return pl.pallas_call(
        body, name="attn_bwd", grid=(nb,),
        in_specs=[q_spec] + kv_specs + [pl.BlockSpec((1, CH), lambda i: (0, 0)),
                                        pl.BlockSpec((CH, D_MODEL), lambda i: (nb - 1 - i, 0))],
        out_specs=[pl.BlockSpec((CH, QKV_DIM), lambda i: (nb - 1 - i, 0)), pl.BlockSpec((1, CH), lambda i: (0, 0))],
        out_shape=[jax.ShapeDtypeStruct((s, QKV_DIM), bf16), jax.ShapeDtypeStruct((1, CH), f32)],
        scratch_shapes=[pltpu.VMEM((2, CH, CH), f32)],
        compiler_params=_cparams(("arbitrary",)),
    )(qkv, qkv, qkv, qkv, qkv, sink_row, dout)


def adamw(w, g, m, v, name):
    def body(w_ref, g_ref, m_ref, v_ref, d_ref, nm_ref, nv_ref):
        gv = g_ref[...]
        nm = ADAM_B1 * m_ref[...] + (1.0 - ADAM_B1) * gv
        nv = ADAM_B2 * v_ref[...] + (1.0 - ADAM_B2) * jnp.square(gv)
        m_hat = nm / (1.0 - ADAM_B1 ** ADAM_STEP)
        v_hat = nv / (1.0 - ADAM_B2 ** ADAM_STEP)
        d_ref[...] = -ADAM_LR * (m_hat / (jnp.sqrt(v_hat) + ADAM_EPS) + ADAM_WD * w_ref[...])
        nm_ref[...] = nm
        nv_ref[...] = nv

    out_shape = [jax.ShapeDtypeStruct(w.shape, f32)] * 3
    if w.ndim == 3 and w.shape[1] % 256 == 0:
        tile = pl.BlockSpec((None, 256, w.shape[2]), lambda l, i: (l, i, 0))
        return pl.pallas_call(
            body, name=name, grid=(w.shape[0], w.shape[1] // 256),
            in_specs=[tile] * 4, out_specs=[tile] * 3, out_shape=out_shape,
            compiler_params=_cparams(("parallel", "parallel")),
        )(w, g, m, v)
    return pl.pallas_call(body, name=name, in_specs=[_VMEM] * 4, out_specs=[_VMEM] * 3, out_shape=out_shape,
                          compiler_params=_cparams())(w, g, m, v)


_MESH = pl.DeviceIdType.MESH
_ANY = pl.BlockSpec(memory_space=pl.ANY)
_VMEM = pl.BlockSpec(memory_space=pltpu.VMEM)


def _place():
    x, y, c = lax.axis_index("x"), lax.axis_index("y"), lax.axis_index("c")
    chips = [(1 - x, y), (x, 1 - y), (1 - x, 1 - y)]
    return x, y, c, 2 * x + y, chips, [2 * cx + cy for cx, cy in chips]


def _half(c, rows):
    return pl.ds(pl.multiple_of(c * (rows // 2), 16), rows // 2)


def _step_rows(rows):
    return max(t for t in range(16, 641, 16) if rows % t == 0)


def place_shard(b, slot, name):
    r, c = b.shape
    tr = _step_rows(r)

    def body(slot_ref, b_ref, o_ref):
        o_ref[...] = b_ref[...]

    return pl.pallas_call(
        body, name=name,
        grid_spec=pltpu.PrefetchScalarGridSpec(
            num_scalar_prefetch=1, grid=(r // tr,),
            in_specs=[pl.BlockSpec((tr, c), lambda i, s: (i, 0))],
            out_specs=pl.BlockSpec((None, tr, c), lambda i, s: (s[0], i, 0))),
        out_shape=jax.ShapeDtypeStruct((N_CHIPS, r, c), b.dtype),
        compiler_params=_cparams(("parallel",)),
    )(slot, b)


_HBM = pl.BlockSpec(memory_space=pltpu.HBM)
_SEM = pl.BlockSpec(memory_space=pltpu.SEMAPHORE)
_EFFECT = pltpu.SideEffectType.DATAFLOW_SIDE_EFFECTING


def _gather_ici_copies(bufs, send_sems, recv_sems):
    x, y, c, me, chips, chip_idx = _place()
    return [pltpu.make_async_remote_copy(
        src_ref=buf.at[me, _half(c, buf.shape[1])], dst_ref=buf.at[chip_idx[j], _half(c, buf.shape[1])],
        send_sem=send_sems.at[3 * k + j], recv_sem=recv_sems.at[3 * k + j],
        device_id=(*chips[j], c), device_id_type=_MESH) for j in range(3) for k, buf in enumerate(bufs)]


def gather_start(groups):
    sizes = [len(g) for g in groups]
    flat = [b for g in groups for b in g]
    n = len(flat)

    def body(*refs):
        bufs, sems = refs[:n], refs[n:n + 2 * len(groups)]
        x, y, c, me, chips, chip_idx = _place()
        lo = 0
        for gi, size in enumerate(sizes):
            for j in range(3):
                for k, buf in enumerate(bufs[lo:lo + size]):
                    mine = buf.at[me, _half(c, buf.shape[1])]
                    pltpu.make_async_remote_copy(
                        src_ref=mine, dst_ref=mine, send_sem=sems[2 * gi].at[3 * k + j],
                        recv_sem=sems[2 * gi + 1].at[3 * k + j], device_id=(*chips[j], c),
                        device_id_type=_MESH).start()
            lo += size

    sem_shapes = [pltpu.SemaphoreType.DMA((3 * size,)) for size in sizes for _ in range(2)]
    outs = pl.pallas_call(
        body, name="gather_start",
        out_shape=(*sem_shapes, *[pltpu.HBM(b.shape, b.dtype) for b in flat]),
        in_specs=[_HBM] * n, out_specs=(*[_SEM] * len(sem_shapes), *[_HBM] * n),
        input_output_aliases={i: len(sem_shapes) + i for i in range(n)},
        compiler_params=pltpu.CompilerParams(has_side_effects=_EFFECT),
    )(*[pltpu.with_memory_space_constraint(b, pltpu.HBM) for b in flat])
    sems = [(outs[2 * gi], outs[2 * gi + 1]) for gi in range(len(groups))]
    thru, lo = [], len(sem_shapes)
    for size in sizes:
        thru.append(list(outs[lo:lo + size]))
        lo += size
    return sems, thru


def gather_wait(bufs, sems, after, tag):
    n = len(bufs)

    def body(*refs):
        for cp in _gather_ici_copies(refs[:n], refs[n], refs[n + 1]):
            cp.wait_send()
            cp.wait_recv()

    extra = [] if after is None else [after]
    return list(pl.pallas_call(
        body, name=f"gather_wait_{tag}",
        out_shape=[pltpu.HBM(b.shape, b.dtype) for b in bufs],
        in_specs=[_HBM] * n + [_SEM, _SEM] + [_ANY] * len(extra), out_specs=[_HBM] * n,
        input_output_aliases={i: i for i in range(n)},
        compiler_params=pltpu.CompilerParams(has_side_effects=_EFFECT),
    )(*bufs, *sems, *extra))


def gather_forward(bufs, tag):
    n = len(bufs)

    def body(*refs):
        out_refs = refs[n:2 * n]
        send_sems, recv_sems = refs[2 * n:]
        x, y, c, me, chips, chip_idx = _place()

        def copy(k, j, half):
            part = out_refs[k].at[chip_idx[j], _half(half, out_refs[k].shape[1])]
            return pltpu.make_async_remote_copy(
                src_ref=part, dst_ref=part, send_sem=send_sems.at[3 * k + j], recv_sem=recv_sems.at[3 * k + j],
                device_id=(x, y, 1 - c), device_id_type=_MESH)

        sends = [copy(k, j, c) for j in range(3) for k in range(n)]
        for cp in sends:
            cp.start()
        for j in range(3):
            for k in range(n):
                copy(k, j, 1 - c).wait_recv()
        for cp in sends:
            cp.wait_send()

    return list(pl.pallas_call(
        body, name=f"gather_forward_{tag}",
        out_shape=[jax.ShapeDtypeStruct(b.shape, b.dtype) for b in bufs],
        in_specs=[_ANY] * n, out_specs=[_ANY] * n, input_output_aliases={i: i for i in range(n)},
        scratch_shapes=[pltpu.SemaphoreType.DMA((3 * n,)), pltpu.SemaphoreType.DMA((3 * n,))],
    )(*bufs))


def exchange_halves(bufs, tag):
    n = len(bufs)

    def body(*refs):
        g_refs, out_refs = refs[:n], refs[n:2 * n]
        send_sems, recv_sems = refs[2 * n:]
        x, y, c, *_ = _place()
        cps = [pltpu.make_async_remote_copy(
            src_ref=g_refs[b].at[:, _half(1 - c, g_refs[b].shape[1])], dst_ref=out_refs[b],
            send_sem=send_sems.at[b], recv_sem=recv_sems.at[b], device_id=(x, y, 1 - c), device_id_type=_MESH)
            for b in range(n)]
        for cp in cps:
            cp.start()
        for cp in cps:
            cp.wait()

    return pl.pallas_call(
        body, name=f"exchange_halves_{tag}",
        out_shape=[jax.ShapeDtypeStruct((N_CHIPS, b.shape[1] // 2, b.shape[2]), b.dtype) for b in bufs],
        in_specs=[_ANY] * n, out_specs=[_ANY] * n,
        scratch_shapes=[pltpu.SemaphoreType.DMA((n,)), pltpu.SemaphoreType.DMA((n,))],
    )(*bufs)


def add_halves(g, got, c_idx, name):
    hr, cols = got.shape[1], got.shape[2]
    tr = _step_rows(hr)
    steps = hr // tr

    def body(c_ref, g_ref, got_ref, o_ref):
        o_ref[...] = (g_ref[...].astype(f32) + got_ref[...].astype(f32)).astype(bf16)

    return pl.pallas_call(
        body, name=name,
        grid_spec=pltpu.PrefetchScalarGridSpec(
            num_scalar_prefetch=1, grid=(N_CHIPS, steps),
            in_specs=[pl.BlockSpec((None, tr, cols), lambda s, i, c: (s, c[0] * steps + i, 0)),
                      pl.BlockSpec((None, tr, cols), lambda s, i, c: (s, i, 0))],
            out_specs=pl.BlockSpec((None, tr, cols), lambda s, i, c: (s, i, 0))),
        out_shape=jax.ShapeDtypeStruct(got.shape, bf16),
        compiler_params=_cparams(("parallel", "parallel")),
    )(c_idx, g, got)


def sum_chips(t, got, place_idx, name):
    hr, cols = t.shape[1], t.shape[2]
    tr = _step_rows(hr)
    steps = hr // tr

    def body(idx_ref, t_ref, got_ref, o_ref):
        acc = t_ref[...].astype(f32)
        for j in range(3):
            acc = acc + got_ref[j].astype(f32)
        o_ref[...] = acc

    return pl.pallas_call(
        body, name=name,
        grid_spec=pltpu.PrefetchScalarGridSpec(
            num_scalar_prefetch=1, grid=(steps,),
            in_specs=[pl.BlockSpec((None, tr, cols), lambda i, idx: (idx[0], i, 0)),
                      pl.BlockSpec((3, tr, cols), lambda i, idx: (0, i, 0))],
            out_specs=pl.BlockSpec((tr, cols), lambda i, idx: (idx[1] * steps + i, 0))),
        out_shape=jax.ShapeDtypeStruct((2 * hr, cols), f32),
        compiler_params=_cparams(("parallel",)),
    )(place_idx, t, got)


def share_halves(bufs, tag):
    n = len(bufs)

    def body(*refs):
        out_refs = refs[n:2 * n]
        send_sems, recv_sems = refs[2 * n:]
        x, y, c, *_ = _place()

        def copy(b, half):
            part = out_refs[b].at[_half(half, out_refs[b].shape[0])]
            return pltpu.make_async_remote_copy(
                src_ref=part, dst_ref=part, send_sem=send_sems.at[b], recv_sem=recv_sems.at[b],
                device_id=(x, y, 1 - c), device_id_type=_MESH)

        for b in range(n):
            copy(b, c).start()
        for b in range(n):
            copy(b, 1 - c).wait_recv()
        for b in range(n):
            copy(b, c).wait_send()

    return pl.pallas_call(
        body, name=f"share_halves_{tag}",
        out_shape=[jax.ShapeDtypeStruct(b.shape, b.dtype) for b in bufs],
        in_specs=[_ANY] * n, out_specs=[_ANY] * n, input_output_aliases={i: i for i in range(n)},
        scratch_shapes=[pltpu.SemaphoreType.DMA((n,)), pltpu.SemaphoreType.DMA((n,))],
    )(*bufs)


def _scatter_copies(t_refs, land_refs, send_sems, recv_sems):
    x, y, c, me, chips, chip_idx = _place()
    return [pltpu.make_async_remote_copy(
        src_ref=t_refs[b].at[chip_idx[j]], dst_ref=land_refs[b].at[j], send_sem=send_sems.at[3 * b + j],
        recv_sem=recv_sems.at[3 * b + j], device_id=(*chips[j], c), device_id_type=_MESH)
        for j in range(3) for b in range(len(t_refs))]


def scatter_start(ts, tag):
    n = len(ts)
    lands = [lax.empty((3,) + t.shape[1:], t.dtype) for t in ts]

    def body(*refs):
        for cp in _scatter_copies(refs[:n], refs[n:2 * n], refs[2 * n], refs[2 * n + 1]):
            cp.start()
        token = refs[-1]
        token[...] = jnp.zeros_like(token)

    hbm = [pltpu.HBM(a.shape, a.dtype) for a in (*ts, *lands)]
    outs = pl.pallas_call(
        body, name=f"scatter_start_{tag}",
        out_shape=(pltpu.SemaphoreType.DMA((3 * n,)), pltpu.SemaphoreType.DMA((3 * n,)), *hbm,
                   jax.ShapeDtypeStruct((8, 128), f32)),
        in_specs=[_HBM] * (2 * n), out_specs=(_SEM, _SEM, *[_HBM] * (2 * n), _VMEM),
        input_output_aliases={i: 2 + i for i in range(2 * n)},
        compiler_params=pltpu.CompilerParams(has_side_effects=_EFFECT),
    )(*[pltpu.with_memory_space_constraint(a, pltpu.HBM) for a in (*ts, *lands)])
    return outs[0], outs[1], list(outs[2:2 + n]), list(outs[2 + n:2 + 2 * n]), outs[-1]


def scatter_wait(send_sems, recv_sems, ts, lands, after, tag):
    n = len(ts)

    def body(*refs):
        for cp in _scatter_copies(refs[:n], refs[n:2 * n], refs[2 * n], refs[2 * n + 1]):
            cp.wait_send()
            cp.wait_recv()

    outs = pl.pallas_call(
        body, name=f"scatter_wait_{tag}",
        out_shape=[pltpu.HBM(a.shape, a.dtype) for a in (*ts, *lands)],
        in_specs=[_HBM] * (2 * n) + [_SEM, _SEM, _ANY], out_specs=[_HBM] * (2 * n),
        input_output_aliases={i: i for i in range(2 * n)},
        compiler_params=pltpu.CompilerParams(has_side_effects=_EFFECT),
    )(*ts, *lands, send_sems, recv_sems, after)
    return list(outs[:n]), list(outs[n:])


class GradReducer:
    def __init__(self, c_idx, place_idx):
        self.c_idx, self.place_idx = c_idx, place_idx

    def start(self, bufs, tag):
        got = exchange_halves(bufs, tag)
        ts = [add_halves(b, g, self.c_idx, f"add_halves_{tag}{i}") for i, (b, g) in enumerate(zip(bufs, got))]
        send_sems, recv_sems, ts, lands, token = scatter_start(ts, tag)
        return (send_sems, recv_sems, ts, lands), token

    def finish(self, state, after, tag):
        ts, lands = scatter_wait(*state, after, tag)
        sums = [sum_chips(t, l, self.place_idx, f"sum_chips_{tag}{i}") for i, (t, l) in enumerate(zip(ts, lands))]
        return share_halves(sums, tag)


def allreduce_small(sp):
    def body(s_ref, out_ref, gather_ref, send_sems, recv_sems):
        x, y, c, me, chips, chip_idx = _place()
        sibling = (x, y, 1 - c)

        def copy(k, chip, core, to, src=None):
            dst = gather_ref.at[2 * chip + core]
            return pltpu.make_async_remote_copy(
                src_ref=dst if src is None else src, dst_ref=dst, send_sem=send_sems.at[k],
                recv_sem=recv_sems.at[k], device_id=to, device_id_type=_MESH)

        first = [copy(0, me, c, sibling, src=s_ref)]
        first += [copy(1 + j, me, c, (*chips[j], c), src=s_ref) for j in range(3)]
        for cp in first:
            cp.start()
        gather_ref[2 * me + c] = s_ref[...]
        passed = [copy(4 + j, chip_idx[j], c, sibling) for j in range(3)]
        for j in range(3):
            copy(1 + j, chip_idx[j], c, sibling).wait_recv()
            passed[j].start()
        copy(0, me, 1 - c, sibling).wait_recv()
        for j in range(3):
            copy(4 + j, chip_idx[j], 1 - c, sibling).wait_recv()
        for cp in first + passed:
            cp.wait_send()
        acc = gather_ref[0]
        for d in range(1, 2 * N_CHIPS):
            acc = acc + gather_ref[d]
        out_ref[...] = acc

    return pl.pallas_call(
        body, name="allreduce_small",
        out_shape=jax.ShapeDtypeStruct(sp.shape, sp.dtype),
        in_specs=[_VMEM], out_specs=_VMEM,
        scratch_shapes=[pltpu.VMEM((2 * N_CHIPS,) + sp.shape, sp.dtype),
                        pltpu.SemaphoreType.DMA((7,)), pltpu.SemaphoreType.DMA((7,))],
        compiler_params=_cparams(),
    )(sp)


def _n_rows(shape):
    n = 1
    for d in shape:
        n *= d
    return 8 * (-(-n // 8192))


def _pack(arrays, total_rows):
    parts = []
    for a in arrays:
        flat = a.reshape(-1)
        parts.append(jnp.pad(flat, (0, 1024 * _n_rows(a.shape) - flat.shape[0])).reshape(-1, 1024))
    rows = jnp.concatenate(parts, axis=0)
    return jnp.pad(rows, ((0, total_rows - rows.shape[0]), (0, 0)))


def _unpack(packed, shapes):
    out, r = [], 0
    for shp in shapes:
        n = 1
        for d in shp:
            n *= d
        nr = _n_rows(shp)
        out.append(packed[r:r + nr].reshape(-1)[:n].reshape(shp))
        r += nr
    return out


IN_SHARD, IN_PAD = 1284, 1408
QKV_SHARD, QKV_PAD = 320, 384


def _lane_padded(a, cols):
    return jnp.pad(a, ((0, 0), (0, cols - a.shape[1])))


_SMALL_SHAPES = (
    ("norm_mix_g", (2, 1024)), ("norm_mlp_g", (2, 1024)), ("final_norm_g", (1024,)), ("gm_ln_g", (1, 1024)),
    ("gm_ln_b", (1, 1024)), ("gm_w_s", (1, 8, 128, 128)), ("gm_b_s", (1, 8, 128)), ("ssm_conv_b", (1, 2048)),
    ("ssm_dt_bias", (1, 16)), ("ssm_a_log", (1, 16)), ("ssm_d", (1, 16)), ("ssm_norm_g", (1, 1024)),
    ("attn_sinks", (1, 16)), ("ssm_conv_w", (1, 4, 2048)), ("b_qkv", (1, 1280)), ("b_o", (1, 1024)),
)
_N_REPLICATED = 13
_SHARDED_SMALL = (("ssm_conv_w", 2, 512), ("b_qkv", 1, 320), ("b_o", 1, 256))
_SHARD_PACK_ROWS = 32


def _cols_by_owner(a):
    return a.transpose(1, 0, 2).reshape(a.shape[1], -1)


class WeightGatherer:
    def __init__(self, w, chip_idx):
        rows = lambda *parts: jnp.concatenate(parts, axis=0).astype(bf16)
        shards = [
            ("in", _lane_padded(w["w_in_even"][0], IN_PAD).astype(bf16)),
            ("l0", rows(w["w_out_even"][0], w["w_up"][0], w["w_down"][0])),
            ("l1", rows(w["w_o"][0], w["w_up"][1], w["w_down"][1])),
            ("qkv", _lane_padded(w["w_qkv"][0], QKV_PAD).astype(bf16)),
        ]
        shards.append(("small", _pack([w[n] for n, _, _ in _SHARDED_SMALL], _SHARD_PACK_ROWS)))
        placed = [place_shard(b, chip_idx, f"place_shard_{tag}") for tag, b in shards]
        self.sems, self.bufs = gather_start([[placed[0], placed[4]], placed[1:2], placed[2:4]])

    def _group(self, gi, after, tag):
        return gather_forward(gather_wait(self.bufs[gi], self.sems[gi], after, tag), tag)

    def mixer_in(self):
        g, small = self._group(0, None, "in")
        shard_shapes = [tuple(width if i == axis else d for i, d in enumerate(dict(_SMALL_SHAPES)[n]))
                        for n, axis, width in _SHARDED_SMALL]
        per_chip = [_unpack(small[s], shard_shapes) for s in range(N_CHIPS)]
        full = {n: jnp.concatenate([per_chip[s][i] for s in range(N_CHIPS)], axis=axis)
                for i, (n, axis, _) in enumerate(_SHARDED_SMALL)}
        return _lane_padded(_cols_by_owner(g[:, :, :IN_SHARD]), NP_IN), full

    def layer0(self, after):
        (g,) = self._group(1, after, "l0")
        return g[:, :512].reshape(2048, 1024), _cols_by_owner(g[:, 512:1536]), g[:, 1536:].reshape(4096, 1024)

    def layer1(self, after):
        g, q = self._group(2, after, "l1")
        return (_cols_by_owner(q[:, :, :QKV_SHARD]), g[:, :256].reshape(1024, 1024), _cols_by_owner(g[:, 256:1280]),
                g[:, 1280:].reshape(4096, 1024))


def _row2(v):
    return v.reshape(1, -1)


def _lane_pad(v):
    return jnp.pad(v, ((0, 0), (0, CH - v.shape[1])))


def _mlp_fwd(h, g_row, w_up, w_down, tag):
    y = rmsnorm_fwd(h, g_row, f"mlp_norm{tag}")
    a = matmul(y, w_up, dims="nn", name=f"mlp_up{tag}", out_dtype=bf16, tn=1024)
    out = matmul(a, w_down, dims="nn", name=f"mlp_down{tag}", a_pro=_relu2, epi=_add, epi_args=(("tile", h),))
    return out, y, a


def _mlp_bwd(dh_out, h, g_row, y, a, w_up, w_down, tag, after=None):
    da = matmul(dh_out, w_down, dims="nt", name=f"mlp_da{tag}", out_dtype=bf16, tn=1024,
                epi=_times_relu2_grad, epi_args=(("tile", a),), after=after)
    dw_down = matmul(a, dh_out, dims="tn", name=f"mlp_dwdown{tag}", out_dtype=bf16, a_pro=_relu2)
    dw_up = matmul(y, da, dims="tn", name=f"mlp_dwup{tag}", out_dtype=bf16, tn=1024, out_by_col_tile=True)
    dy = matmul(da, w_up, dims="nt", name=f"mlp_dy{tag}")
    dh, dg = rmsnorm_bwd(h, g_row, dy, dh_out, f"mlp_dnorm{tag}")
    return dh, dg, dw_up, dw_down


def _by_owner(a):
    return a.reshape(N_CHIPS, a.shape[0] // N_CHIPS, a.shape[1])


def _col_shards(a, shard, padded):
    return jnp.stack([_lane_padded(a[:, shard * s: shard * (s + 1)], padded) for s in range(N_CHIPS)])


def _local_step(x, target, weights, sm, reducer):
    w_up, w_down = [None, None], [None, None]
    w_in_p, sharded_small = weights.mixer_in()
    sm = {**sm, **sharded_small}
    mix_g = [_row2(sm["norm_mix_g"][i]) for i in range(2)]
    mlp_g = [_row2(sm["norm_mlp_g"][i]) for i in range(2)]
    mixer_prm = {
        "ln_g": sm["gm_ln_g"], "ln_b": sm["gm_ln_b"], "wm": sm["gm_w_s"][0],
        "bs_t": jnp.pad(sm["gm_b_s"][0].T, ((0, 0), (0, CH - N_BLK))),
        "conv_w": jnp.pad(sm["ssm_conv_w"][0], ((0, 4), (0, 0))), "conv_b": sm["ssm_conv_b"],
        "dt_bias": _lane_pad(sm["ssm_dt_bias"]), "a_log": _lane_pad(sm["ssm_a_log"]),
        "d_heads": _lane_pad(sm["ssm_d"]), "norm_g": sm["ssm_norm_g"],
    }
    sink_row = _lane_pad(sm["attn_sinks"])

    y0 = rmsnorm_fwd(x, mix_g[0], "mix_norm0")
    proj = matmul(y0, w_in_p, dims="nn", name="in_proj", tn=768)
    ab, hstates = mixer_fwd(proj, mixer_prm)
    w_out, w_up[0], w_down[0] = weights.layer0(ab)
    h1 = matmul(ab, w_out, dims="nn", name="out_proj", epi=_add, epi_args=(("tile", x),))
    h2, y1, a1 = _mlp_fwd(h1, mlp_g[0], w_up[0], w_down[0], 0)
    w_qkv, w_o, w_up[1], w_down[1] = weights.layer1(h2)
    y2 = rmsnorm_fwd(h2, mix_g[1], "mix_norm1")
    qkv = matmul(y2, w_qkv, dims="nn", name="qkv_proj", tn=QKV_DIM, epi=_add_bias, epi_args=(("row", sm["b_qkv"]),))
    att = attn_fwd(qkv, sink_row)
    h3 = matmul(att, w_o, dims="nn", name="o_proj", epi=_add_bias_res,
                epi_args=(("row", sm["b_o"]), ("tile", h2)))
    h4, y3, a3 = _mlp_fwd(h3, mlp_g[1], w_up[1], w_down[1], 1)
    loss, dh4, dg_final = final_loss(h4, _row2(sm["final_norm_g"]), target, "final_loss")

    dh3, dg_mlp1, dw_up1, dw_down1 = _mlp_bwd(dh4, h3, mlp_g[1], y3, a3, w_up[1], w_down[1], 1)
    db_o = colsum(dh3, "db_o")
    datt = matmul(dh3, w_o, dims="nt", name="attn_dout", out_dtype=bf16)
    dw_o = matmul(att, dh3, dims="tn", name="dw_o", out_dtype=bf16)
    dqkv, dsink = attn_bwd(qkv, sink_row, datt)
    db_qkv = colsum(dqkv, "db_qkv")
    dw_qkv = matmul(y2, dqkv, dims="tn", name="dw_qkv", out_dtype=bf16, tn=QKV_DIM)
    dy2 = matmul(dqkv, w_qkv, dims="nt", name="dy_qkv", tk=QKV_DIM)
    dh2, dg_mix1 = rmsnorm_bwd(h2, mix_g[1], dy2, dh3, "mix_dnorm1")
    layer1 = [jnp.concatenate([_by_owner(dw_o), dw_up1, _by_owner(dw_down1)], axis=1),
              _col_shards(dw_qkv, QKV_SHARD, QKV_PAD)]
    flight1, token1 = reducer.start(layer1, "l1")
    dh1, dg_mlp0, dw_up0, dw_down0 = _mlp_bwd(dh2, h1, mlp_g[0], y1, a1, w_up[0], w_down[0], 0, after=token1)
    r_l1, r_qkv = reducer.finish(flight1, dh1, "l1")
    dw_out = matmul(ab, dh1, dims="tn", name="dw_out", out_dtype=bf16)
    flight0, token0 = reducer.start(
        [jnp.concatenate([dw_up0, _by_owner(dw_down0), _by_owner(dw_out)], axis=1)], "l0")
    dab = matmul(dh1, w_out, dims="nt", name="mixer_dout", tn=1024, after=token0)
    dproj, dmix = mixer_bwd(proj, hstates, dab, mixer_prm)
    dw_in_p = matmul(y0, dproj, dims="tn", name="dw_in", out_dtype=bf16, tn=768)
    (r_l0,) = reducer.finish(flight0, dw_in_p, "l0")
    flight_in, token_in = reducer.start([_col_shards(dw_in_p, IN_SHARD, IN_PAD)], "in")
    dy0 = matmul(dproj, w_in_p, dims="nt", name="dy_in", after=token_in)
    dx, dg_mix0 = rmsnorm_bwd(x, mix_g[0], dy0, dh1, "mix_dnorm0")
    (r_in,) = reducer.finish(flight_in, dx, "in")
    reduced = {
        "w_out_even": r_l0[None, 2048:], "w_in_even": r_in[None, :, :IN_SHARD], "w_qkv": r_qkv[None, :, :QKV_SHARD],
        "w_o": r_l1[None, :256], "w_up": jnp.stack([r_l0[:1024], r_l1[256:1280]]),
        "w_down": jnp.stack([r_l0[1024:2048], r_l1[1280:]]),
    }

    small_grads = {
        "norm_mix_g": jnp.concatenate([dg_mix0, dg_mix1], axis=0),
        "norm_mlp_g": jnp.concatenate([dg_mlp0, dg_mlp1], axis=0),
        "final_norm_g": dg_final[0], "gm_ln_g": dmix["ln_g"], "gm_ln_b": dmix["ln_b"],
        "gm_w_s": dmix["wm"][None], "gm_b_s": dmix["bs_t"][:, :N_BLK].T[None],
        "ssm_conv_b": dmix["conv_b"], "ssm_dt_bias": dmix["dt_bias"][:, :SSM_HEADS],
        "ssm_a_log": dmix["a_log"][:, :SSM_HEADS], "ssm_d": dmix["d_heads"][:, :SSM_HEADS],
        "ssm_norm_g": dmix["norm_g"], "attn_sinks": dsink[:, :SSM_HEADS],
        "ssm_conv_w": dmix["conv_w"][None, :4], "b_qkv": db_qkv, "b_o": db_o,
    }
    return loss, dx, reduced, small_grads


def kernel(x, norm_mix_g, norm_mlp_g, final_norm_g, w_in_even, w_out_even, gm_ln_g, gm_ln_b, gm_w_s, gm_b_s, ssm_conv_w, ssm_conv_b, ssm_dt_bias, ssm_a_log, ssm_d, ssm_norm_g, w_qkv, b_qkv, w_o, b_o, attn_sinks, w_up, w_down, loss_target, m_norm_mix_g, m_norm_mlp_g, m_final_norm_g, m_w_in_even, m_w_out_even, m_gm_ln_g, m_gm_ln_b, m_gm_w_s, m_gm_b_s, m_ssm_conv_w, m_ssm_conv_b, m_ssm_dt_bias, m_ssm_a_log, m_ssm_d, m_ssm_norm_g, m_w_qkv, m_b_qkv, m_w_o, m_b_o, m_attn_sinks, m_w_up, m_w_down, v_norm_mix_g, v_norm_mlp_g, v_final_norm_g, v_w_in_even, v_w_out_even, v_gm_ln_g, v_gm_ln_b, v_gm_w_s, v_gm_b_s, v_ssm_conv_w, v_ssm_conv_b, v_ssm_dt_bias, v_ssm_a_log, v_ssm_d, v_ssm_norm_g, v_w_qkv, v_b_qkv, v_w_o, v_b_o, v_attn_sinks, v_w_up, v_w_down):
    w = dict(norm_mix_g=norm_mix_g, norm_mlp_g=norm_mlp_g, final_norm_g=final_norm_g, w_in_even=w_in_even,
             w_out_even=w_out_even, gm_ln_g=gm_ln_g, gm_ln_b=gm_ln_b, gm_w_s=gm_w_s, gm_b_s=gm_b_s,
             ssm_conv_w=ssm_conv_w, ssm_conv_b=ssm_conv_b, ssm_dt_bias=ssm_dt_bias, ssm_a_log=ssm_a_log,
             ssm_d=ssm_d, ssm_norm_g=ssm_norm_g, w_qkv=w_qkv, b_qkv=b_qkv, w_o=w_o, b_o=b_o,
             attn_sinks=attn_sinks, w_up=w_up, w_down=w_down)
    m = dict(norm_mix_g=m_norm_mix_g, norm_mlp_g=m_norm_mlp_g, final_norm_g=m_final_norm_g,
             w_in_even=m_w_in_even, w_out_even=m_w_out_even, gm_ln_g=m_gm_ln_g, gm_ln_b=m_gm_ln_b,
             gm_w_s=m_gm_w_s, gm_b_s=m_gm_b_s, ssm_conv_w=m_ssm_conv_w, ssm_conv_b=m_ssm_conv_b,
             ssm_dt_bias=m_ssm_dt_bias, ssm_a_log=m_ssm_a_log, ssm_d=m_ssm_d, ssm_norm_g=m_ssm_norm_g,
             w_qkv=m_w_qkv, b_qkv=m_b_qkv, w_o=m_w_o, b_o=m_b_o, attn_sinks=m_attn_sinks, w_up=m_w_up,
             w_down=m_w_down)
    v = dict(norm_mix_g=v_norm_mix_g, norm_mlp_g=v_norm_mlp_g, final_norm_g=v_final_norm_g,
             w_in_even=v_w_in_even, w_out_even=v_w_out_even, gm_ln_g=v_gm_ln_g, gm_ln_b=v_gm_ln_b,
             gm_w_s=v_gm_w_s, gm_b_s=v_gm_b_s, ssm_conv_w=v_ssm_conv_w, ssm_conv_b=v_ssm_conv_b,
             ssm_dt_bias=v_ssm_dt_bias, ssm_a_log=v_ssm_a_log, ssm_d=v_ssm_d, ssm_norm_g=v_ssm_norm_g,
             w_qkv=v_w_qkv, b_qkv=v_b_qkv, w_o=v_w_o, b_o=v_b_o, attn_sinks=v_attn_sinks, w_up=v_w_up,
             w_down=v_w_down)
    names = ("norm_mix_g", "norm_mlp_g", "final_norm_g", "w_in_even", "w_out_even", "gm_ln_g", "gm_ln_b",
             "gm_w_s", "gm_b_s", "ssm_conv_w", "ssm_conv_b", "ssm_dt_bias", "ssm_a_log", "ssm_d", "ssm_norm_g",
             "w_qkv", "b_qkv", "w_o", "b_o", "attn_sinks", "w_up", "w_down")

    cx, cy, cc = lax.axis_index("x"), lax.axis_index("y"), lax.axis_index("c")
    chip = 2 * cx + cy
    c_idx = jnp.reshape(cc, (1,)).astype(jnp.int32)
    chip_idx = jnp.reshape(chip, (1,)).astype(jnp.int32)

    weights = WeightGatherer(w, chip_idx)
    sm = {n: w[n] for n, _ in _SMALL_SHAPES[:_N_REPLICATED]}

    reducer = GradReducer(c_idx, jnp.concatenate([chip_idx, c_idx]))
    loss_part, dx, grads, small_grads = _local_step(x[0], loss_target[0], weights, sm, reducer)
    loss = lax.psum(loss_part[0, 0], ("x", "y", "c"))

    small_sum = allreduce_small(_pack([small_grads[n] for n, _ in _SMALL_SHAPES], SMALL_ROWS))
    small_full = dict(zip([n for n, _ in _SMALL_SHAPES], _unpack(small_sum, [s for _, s in _SMALL_SHAPES])))
    for n, _ in _SMALL_SHAPES[:_N_REPLICATED]:
        grads[n] = small_full[n]
    for n, axis, width in _SHARDED_SMALL:
        grads[n] = lax.dynamic_slice_in_dim(small_full[n], chip * width, width, axis)
    grads = {n: grads[n].reshape(w[n].shape) for n in names}

    delta, new_m, new_v = {}, {}, {}
    for n in names:
        shape = (1,) + w[n].shape if w[n].ndim == 1 else w[n].shape
        outs = adamw(*[d[n].reshape(shape) for d in (w, grads, m, v)], f"adamw_{n}")
        delta[n], new_m[n], new_v[n] = (o.reshape(w[n].shape) for o in outs)

    return (loss, dx[None], *[grads[n] for n in names], *[delta[n] for n in names],
            *[new_m[n] for n in names], *[new_v[n] for n in names])
```

```python
import functools

import jax
import jax.numpy as jnp
from jax import lax
from jax.experimental import pallas as pl
from jax.experimental.pallas import tpu as pltpu

f32 = jnp.float32
bf16 = jnp.bfloat16
MXU_DTYPE = bf16

RMS_EPS = 1e-5
LN_EPS = 1e-5
D_MODEL = 1024
D_FF = 4096
CH = 128
N_BLK = 8
SSM_HEADS = 16
IN_EVEN = 5136
NP_IN = 5376
OFF_U, OFF_V, OFF_Z, OFF_X, OFF_DT = 0, 1024, 2048, 3072, 5120
XBC_BLKS = 16
QKV_DIM = 1280
ATT_SCALE = 64 ** -0.5

ADAM_LR = 0.001
ADAM_B1 = 0.9
ADAM_B2 = 0.999
ADAM_EPS = 1e-08
ADAM_WD = 0.01
ADAM_STEP = 10

VMEM_LIMIT_BYTES = 48 * 1024 * 1024
N_CHIPS = 4
SMALL_ROWS = 256

NN = ((1,), (0,))
NT = ((1,), (1,))
TN = ((0,), (0,))


def _mm(a, b, dims):
    return lax.dot_general(a.astype(MXU_DTYPE), b.astype(MXU_DTYPE), (dims, ((), ())),
                           preferred_element_type=f32)


def _mm_exact(a, b):
    return jnp.dot(a, b, preferred_element_type=f32, precision=lax.Precision.HIGHEST)


def _cparams(sem=None):
    return pltpu.CompilerParams(dimension_semantics=sem, vmem_limit_bytes=VMEM_LIMIT_BYTES)


@jax.custom_vjp
def _swap64(x):
    return pltpu.roll(x, 64, axis=1)


_swap64.defvjp(lambda x: (pltpu.roll(x, 64, axis=1), None), lambda _, g: (pltpu.roll(g, 64, axis=1),))


@jax.custom_vjp
def _top_rows(x):
    return x[:x.shape[0] // 2]


_top_rows.defvjp(lambda x: (x[:x.shape[0] // 2], None),
                 lambda _, g: (jnp.concatenate([g, jnp.zeros_like(g)], axis=0),))


@jax.custom_vjp
def _bottom_rows(x):
    return x[x.shape[0] // 2:]


_bottom_rows.defvjp(lambda x: (x[x.shape[0] // 2:], None),
                    lambda _, g: (jnp.concatenate([jnp.zeros_like(g), g], axis=0),))


def _make_delay(k):
    @jax.custom_vjp
    def delay(ext):
        return pltpu.roll(ext, k, axis=0)[8:, :]

    def fwd(ext):
        return delay(ext), None

    def bwd(_, g):
        gp = jnp.concatenate([jnp.zeros((8, g.shape[1]), g.dtype), g], axis=0)
        return (pltpu.roll(gp, gp.shape[0] - k, axis=0),)

    delay.defvjp(fwd, bwd)
    return delay


_DELAYS = {k: _make_delay(k) for k in (1, 2, 3)}


_GELU_C = 0.7978845608028654
_GELU_K = 0.044715


@jax.custom_vjp
def _gelu(x):
    return 0.5 * x * (1.0 + jnp.tanh(_GELU_C * (x + _GELU_K * (x * x * x))))


def _gelu_fwd(x):
    t = jnp.tanh(_GELU_C * (x + _GELU_K * (x * x * x)))
    return 0.5 * x * (1.0 + t), (x, t)


def _gelu_bwd(res, g):
    x, t = res
    dz = _GELU_C + (3.0 * _GELU_C * _GELU_K) * (x * x)
    return (g * (0.5 * (1.0 + t) + (0.5 * x) * (1.0 - t * t) * dz),)


_gelu.defvjp(_gelu_fwd, _gelu_bwd)


def _col(m, lane, h):
    return jnp.sum(jnp.where(lane == h, m, 0.0), axis=1, keepdims=True)


def _row(m, sub, h):
    return jnp.sum(jnp.where(sub == h, m, 0.0), axis=0, keepdims=True)


def _mixer_chunk(us, vs, zs, xbcs, halos, dtblk, hps, prm):
    lane = lax.broadcasted_iota(jnp.int32, (CH, CH), 1)
    sub = lax.broadcasted_iota(jnp.int32, (CH, CH), 0)
    left = lane < 64
    top = sub < 64
    causal = sub >= lane

    gus = [_gelu(u) for u in us]
    gvs = [_gelu(v) for v in vs]
    mu = sum(jnp.sum(g, axis=1, keepdims=True) for g in gvs) / D_MODEL
    cen = [g - mu for g in gvs]
    var = sum(jnp.sum(c * c, axis=1, keepdims=True) for c in cen) / D_MODEL
    rstd = lax.rsqrt(var + LN_EPS)
    a_out = []
    for g in range(N_BLK):
        vn = cen[g] * rstd * prm["ln_g"][g] + prm["ln_b"][g]
        w = jnp.where(causal, prm["wm"][g], 0.0)
        mixed = _mm(w, vn, NN) + _col(prm["bs_t"], lane, g)
        a_out.append(gus[g] * mixed)

    act = []
    for b in range(XBC_BLKS):
        w8 = prm["conv_w"][b]
        sub8 = lax.broadcasted_iota(jnp.int32, w8.shape, 0)
        ext = jnp.concatenate([halos[b], xbcs[b]], axis=0)
        conv = xbcs[b] * _row(w8, sub8, 3) + prm["conv_b"][b]
        for k in (1, 2, 3):
            conv = conv + _DELAYS[k](ext) * _row(w8, sub8, 3 - k)
        act.append(jax.nn.silu(conv))

    dt = jax.nn.softplus(dtblk + prm["dt_bias"])
    a_neg = -jnp.exp(prm["a_log"])
    tri = causal.astype(f32)
    acum = _mm_exact(tri, dt * a_neg)
    acum_t = acum.T
    dt_t = dt.T
    last = sub == CH - 1
    ys, h_out = [], []
    for grp in range(4):
        bm = act[8 + grp]
        cm = act[12 + grp]
        cb = _mm(cm, bm, NT)
        for p in (2 * grp, 2 * grp + 1):
            h0, h1 = 2 * p, 2 * p + 1
            xp = act[p]
            hp = hps[p]
            wis = []
            for h in (h0, h1):
                seg = _col(acum, lane, h) - _row(acum_t, sub, h)
                decay = jnp.exp(jnp.where(causal, seg, -jnp.inf))
                wis.append(cb * decay * _row(dt_t, sub, h))
            wcat = jnp.concatenate(wis, axis=1)
            xbd = jnp.concatenate([jnp.where(left, xp, 0.0), jnp.where(left, 0.0, xp)], axis=0)
            y_diag = _mm(wcat, xbd, NN)
            a_end = [jnp.sum(jnp.where(last & (lane == h), acum, 0.0), keepdims=True) for h in (h0, h1)]
            a_col = jnp.where(left, _col(acum, lane, h0), _col(acum, lane, h1))
            dt_col = jnp.where(left, _col(dt, lane, h0), _col(dt, lane, h1))
            to_end = jnp.exp(jnp.where(left, a_end[0], a_end[1]) - a_col) * dt_col
            states = _mm(xp * to_end, bm, TN)
            chunk_decay = jnp.where(top, jnp.exp(a_end[0]), jnp.exp(a_end[1]))
            h_out.append(chunk_decay * hp + states)
            y_off = jnp.exp(a_col) * _mm(cm, hp, NT)
            d_skip = jnp.where(left[:1], _col(prm["d_heads"], lane[:1], h0), _col(prm["d_heads"], lane[:1], h1))
            ys.append((y_diag + y_off + xp * d_skip) * jax.nn.silu(zs[p]))

    b_out = []
    for grp in range(4):
        pair = (ys[2 * grp], ys[2 * grp + 1])
        ms = sum(jnp.sum(y * y, axis=1, keepdims=True) for y in pair) / 256.0
        r = lax.rsqrt(ms + RMS_EPS)
        for j, y in enumerate(pair):
            b_out.append(y * r * prm["norm_g"][2 * grp + j])
    return a_out, b_out, h_out


def _attn_block(qps, kprev, kcur, vprev, vcur, sink_row, first):
    lane = lax.broadcasted_iota(jnp.int32, (CH, CH), 1)
    left = lane < 64
    row2 = lax.broadcasted_iota(jnp.int32, (2 * CH, CH), 0)
    key2 = lax.broadcasted_iota(jnp.int32, (2 * CH, CH), 1)
    upper = row2 < CH
    own = key2 <= jnp.where(upper, row2, row2 - CH)

    def both_halves(a):
        sw = _swap64(a)
        return [jnp.where(left, a, sw), jnp.where(left, sw, a)]

    kc, kp, vc, vp = both_halves(kcur), both_halves(kprev), both_halves(vcur), both_halves(vprev)
    outs = []
    for p in range(N_BLK):
        j = p // 4
        q2 = jnp.concatenate([jnp.where(left, qps[p], 0.0), jnp.where(left, 0.0, qps[p])], axis=0)
        s_prev = jnp.where(first, -jnp.inf, _mm(q2, kp[j], NT) * ATT_SCALE)
        s = jnp.where(own, _mm(q2, kc[j], NT) * ATT_SCALE, s_prev)
        sink = jnp.where(upper[:, :1], _col(sink_row, lane[:1], 2 * p), _col(sink_row, lane[:1], 2 * p + 1))
        m = lax.stop_gradient(jnp.maximum(jnp.max(s, axis=1, keepdims=True), sink))
        pexp = jnp.exp(s - m)
        probs = pexp / (jnp.sum(pexp, axis=1, keepdims=True) + jnp.exp(sink - m))
        o = _mm(jnp.where(own, probs, 0.0), vc[j], NN) + _mm(jnp.where(own, 0.0, probs), vp[j], NN)
        outs.append(jnp.where(left, _top_rows(o), _bottom_rows(o)))
    return outs


def _rmsnorm(x, g):
    r = lax.rsqrt(jnp.mean(x * x, axis=-1, keepdims=True) + RMS_EPS)
    return x * r * g


def rmsnorm_fwd(x, g_row, name):
    s, d = x.shape
    tm = min(512, s)

    def body(x_ref, g_ref, y_ref):
        y_ref[...] = _rmsnorm(x_ref[...], g_ref[...]).astype(bf16)

    return pl.pallas_call(
        body, name=name, grid=(s // tm,),
        in_specs=[pl.BlockSpec((tm, d), lambda i: (i, 0)), pl.BlockSpec((1, d), lambda i: (0, 0))],
        out_specs=pl.BlockSpec((tm, d), lambda i: (i, 0)),
        out_shape=jax.ShapeDtypeStruct((s, d), bf16),
        compiler_params=_cparams(("parallel",)),
    )(x, g_row)


def rmsnorm_bwd(x, g_row, dy, res, name):
    s, d = x.shape
    tm = min(512, s)

    def body(x_ref, g_ref, dy_ref, res_ref, dx_ref, dg_ref):
        @pl.when(pl.program_id(0) == 0)
        def _():
            dg_ref[...] = jnp.zeros_like(dg_ref)

        _, vjp = jax.vjp(_rmsnorm, x_ref[...], g_ref[...])
        dx, dg = vjp(dy_ref[...])
        dx_ref[...] = res_ref[...] + dx
        dg_ref[...] += dg

    tile = pl.BlockSpec((tm, d), lambda i: (i, 0))
    row = pl.BlockSpec((1, d), lambda i: (0, 0))
    return pl.pallas_call(
        body, name=name, grid=(s // tm,),
        in_specs=[tile, row, tile, tile], out_specs=[tile, row],
        out_shape=[jax.ShapeDtypeStruct((s, d), f32), jax.ShapeDtypeStruct((1, d), f32)],
        compiler_params=_cparams(("arbitrary",)),
    )(x, g_row, dy, res)


def final_loss(h, g_row, target, name):
    s, d = h.shape
    tm = min(512, s)

    def body(h_ref, g_ref, t_ref, loss_ref, dh_ref, dg_ref):
        @pl.when(pl.program_id(0) == 0)
        def _():
            dg_ref[...] = jnp.zeros_like(dg_ref)
            loss_ref[...] = jnp.zeros_like(loss_ref)

        def f(hv, gv):
            err = jnp.square(_rmsnorm(hv, gv) - t_ref[...])
            return 0.5 * jnp.sum(jnp.mean(err, axis=-1, keepdims=True), axis=0, keepdims=True)

        loss, vjp = jax.vjp(f, h_ref[...], g_ref[...])
        dh, dg = vjp(jnp.ones_like(loss))
        dh_ref[...] = dh
        dg_ref[...] += dg
        loss_ref[...] += jnp.broadcast_to(loss, loss_ref.shape)

    tile = pl.BlockSpec((tm, d), lambda i: (i, 0))
    row = pl.BlockSpec((1, d), lambda i: (0, 0))
    return pl.pallas_call(
        body, name=name, grid=(s // tm,),
        in_specs=[tile, row, tile],
        out_specs=[pl.BlockSpec((1, 128), lambda i: (0, 0)), tile, row],
        out_shape=[jax.ShapeDtypeStruct((1, 128), f32), jax.ShapeDtypeStruct((s, d), f32),
                   jax.ShapeDtypeStruct((1, d), f32)],
        compiler_params=_cparams(("arbitrary",)),
    )(h, g_row, target)


def colsum(x, name):
    s, n = x.shape
    tm = min(512, s)

    def body(x_ref, o_ref):
        @pl.when(pl.program_id(0) == 0)
        def _():
            o_ref[...] = jnp.zeros_like(o_ref)

        o_ref[...] += jnp.sum(x_ref[...].astype(f32), axis=0, keepdims=True)

    return pl.pallas_call(
        body, name=name, grid=(s // tm,),
        in_specs=[pl.BlockSpec((tm, n), lambda i: (i, 0))],
        out_specs=pl.BlockSpec((1, n), lambda i: (0, 0)),
        out_shape=jax.ShapeDtypeStruct((1, n), f32),
        compiler_params=_cparams(("arbitrary",)),
    )(x)


def _fit(dim, want):
    if dim <= want:
        return dim
    t = want
    while dim % t:
        t -= 128
    return t


def matmul(a, b, *, dims, name, out_dtype=f32, tm=1024, tn=512, tk=8192, a_pro=None, epi=None, epi_args=(),
           out_by_col_tile=False, after=None):
    if dims == "nn":
        (m, k), n = a.shape, b.shape[1]
    elif dims == "nt":
        (m, k), n = a.shape, b.shape[0]
    else:
        (k, m), n = a.shape, b.shape[1]
    tm, tn, tk = _fit(m, tm), _fit(n, tn), _fit(k, tk)
    nk = k // tk
    if dims == "nn":
        a_spec = pl.BlockSpec((tm, tk), lambda i, j, kk: (i, kk))
        b_spec = pl.BlockSpec((tk, tn), lambda i, j, kk: (kk, j))
        dn = NN
    elif dims == "nt":
        a_spec = pl.BlockSpec((tm, tk), lambda i, j, kk: (i, kk))
        b_spec = pl.BlockSpec((tn, tk), lambda i, j, kk: (j, kk))
        dn = NT
    else:
        a_spec = pl.BlockSpec((tk, tm), lambda i, j, kk: (kk, i))
        b_spec = pl.BlockSpec((tk, tn), lambda i, j, kk: (kk, j))
        dn = TN
    e_specs = [pl.BlockSpec((tm, tn), lambda i, j, kk: (i, j)) if kind == "tile"
               else pl.BlockSpec((1, tn), lambda i, j, kk: (0, j)) for kind, _ in epi_args]
    n_epi = len(epi_args)
    order_specs = [] if after is None else [pl.BlockSpec((8, 128), lambda i, j, kk: (0, 0))]
    order_args = [] if after is None else [after]

    def body(*refs):
        a_ref, b_ref = refs[0], refs[1]
        e_refs = refs[2:2 + n_epi]
        n_in = 2 + n_epi + len(order_args)
        o_ref = refs[n_in]
        av = a_ref[...]
        if a_pro is not None:
            av = a_pro(av)
        part = _mm(av, b_ref[...], dn)

        def finish(acc):
            if epi is not None:
                acc = epi(acc, *[r[...] for r in e_refs])
            o_ref[...] = acc.astype(out_dtype)

        if nk == 1:
            finish(part)
        else:
            acc_ref = refs[n_in + 1]
            kk = pl.program_id(2)

            @pl.when(kk == 0)
            def _():
                acc_ref[...] = part

            @pl.when(kk > 0)
            def _():
                acc_ref[...] += part

            @pl.when(kk == nk - 1)
            def _():
                finish(acc_ref[...])

    if out_by_col_tile:
        out_spec = pl.BlockSpec((None, tm, tn), lambda i, j, kk: (j, i, 0))
        out_shape = jax.ShapeDtypeStruct((n // tn, m, tn), out_dtype)
    else:
        out_spec = pl.BlockSpec((tm, tn), lambda i, j, kk: (i, j))
        out_shape = jax.ShapeDtypeStruct((m, n), out_dtype)
    return pl.pallas_call(
        body, name=name, grid=(m // tm, n // tn, nk),
        in_specs=[a_spec, b_spec] + e_specs + order_specs,
        out_specs=out_spec,
        out_shape=out_shape,
        scratch_shapes=[pltpu.VMEM((tm, tn), f32)] if nk > 1 else [],
        compiler_params=_cparams(("parallel", "parallel", "arbitrary")),
    )(a, b, *[arr for _, arr in epi_args], *order_args)


def _relu2(a):
    r = jnp.maximum(a.astype(f32), 0.0)
    return r * r


def _add(acc, t):
    return acc + t


def _add_bias(acc, t):
    return acc + t


def _add_bias_res(acc, bias, res):
    return acc + bias + res


def _times_relu2_grad(acc, a):
    return acc * (2.0 * jnp.maximum(a.astype(f32), 0.0))


_MIXER_PARAM_SHAPES = (
    ("ln_g", (1, D_MODEL)), ("ln_b", (1, D_MODEL)), ("wm", (N_BLK, CH, CH)), ("bs_t", (CH, CH)),
    ("conv_w", (8, 2048)), ("conv_b", (1, 2048)), ("dt_bias", (1, CH)), ("a_log", (1, CH)),
    ("d_heads", (1, CH)), ("norm_g", (1, D_MODEL)),
)


def _blocks(v, n, off=0):
    return [v[:, off + i * CH: off + (i + 1) * CH] for i in range(n)]


def _split_mixer_params(vals):
    p = dict(vals)
    return {
        "ln_g": _blocks(p["ln_g"], N_BLK), "ln_b": _blocks(p["ln_b"], N_BLK),
        "wm": [p["wm"][g] for g in range(N_BLK)], "bs_t": p["bs_t"],
        "conv_w": _blocks(p["conv_w"], XBC_BLKS), "conv_b": _blocks(p["conv_b"], XBC_BLKS),
        "dt_bias": p["dt_bias"], "a_log": p["a_log"], "d_heads": p["d_heads"],
        "norm_g": _blocks(p["norm_g"], N_BLK),
    }


def _mixer_leaves(proj_ref, halo_ref, keep_halo):
    pv = proj_ref
    us = [pv[:, OFF_U + i * CH: OFF_U + (i + 1) * CH] for i in range(N_BLK)]
    vs = [pv[:, OFF_V + i * CH: OFF_V + (i + 1) * CH] for i in range(N_BLK)]
    zs = [pv[:, OFF_Z + i * CH: OFF_Z + (i + 1) * CH] for i in range(N_BLK)]
    xbcs = [pv[:, OFF_X + i * CH: OFF_X + (i + 1) * CH] for i in range(XBC_BLKS)]
    halos = [halo_ref[:, OFF_X + i * CH: OFF_X + (i + 1) * CH] * keep_halo for i in range(XBC_BLKS)]
    dtblk = pv[:, OFF_DT: OFF_DT + CH]
    return us, vs, zs, xbcs, halos, dtblk


def mixer_fwd(proj, prm):
    s = proj.shape[0]
    nc = s // CH
    names = [n for n, _ in _MIXER_PARAM_SHAPES]

    def body(proj_ref, halo_ref, *rest):
        p_refs = rest[:len(names)]
        ab_ref, hs_ref, h_ref = rest[len(names):]
        c = pl.program_id(0)

        @pl.when(c == 0)
        def _():
            h_ref[...] = jnp.zeros_like(h_ref)

        hs_ref[...] = h_ref[...]
        keep = (c > 0).astype(f32)
        us, vs, zs, xbcs, halos, dtblk = _mixer_leaves(proj_ref, halo_ref, keep)
        hps = [h_ref[i * CH:(i + 1) * CH, :] for i in range(N_BLK)]
        p = _split_mixer_params({n: r[...] for n, r in zip(names, p_refs)})
        a_out, b_out, h_out = _mixer_chunk(us, vs, zs, xbcs, halos, dtblk, hps, p)
        for i in range(N_BLK):
            ab_ref[:, i * CH:(i + 1) * CH] = a_out[i].astype(bf16)
            ab_ref[:, D_MODEL + i * CH: D_MODEL + (i + 1) * CH] = b_out[i].astype(bf16)
            h_ref[i * CH:(i + 1) * CH, :] = h_out[i]

    def const(shape):
        return pl.BlockSpec(shape, lambda c: (0,) * len(shape))

    return pl.pallas_call(
        body, name="mixer_fwd", grid=(nc,),
        in_specs=[pl.BlockSpec((CH, NP_IN), lambda c: (c, 0)),
                  pl.BlockSpec((8, NP_IN), lambda c: (jnp.maximum(c * (CH // 8) - 1, 0), 0))]
                 + [const(shp) for _, shp in _MIXER_PARAM_SHAPES],
        out_specs=[pl.BlockSpec((CH, 2 * D_MODEL), lambda c: (c, 0)),
                   pl.BlockSpec((None, D_MODEL, CH), lambda c: (c, 0, 0))],
        out_shape=[jax.ShapeDtypeStruct((s, 2 * D_MODEL), bf16), jax.ShapeDtypeStruct((nc, D_MODEL, CH), f32)],
        scratch_shapes=[pltpu.VMEM((D_MODEL, CH), f32)],
        compiler_params=_cparams(("arbitrary",)),
    )(proj, proj, *[prm[n] for n in names])


def mixer_bwd(proj, hstates, dab, prm):
    s = proj.shape[0]
    nc = s // CH
    names = [n for n, _ in _MIXER_PARAM_SHAPES]
    npar = len(names)

    def body(proj_ref, halo_ref, hs_ref, dab_ref, *rest):
        p_refs = rest[:npar]
        dproj_ref = rest[npar]
        g_refs = rest[npar + 1: 2 * npar + 1]
        dh_ref, dhalo_ref = rest[2 * npar + 1:]
        i = pl.program_id(0)
        c = nc - 1 - i

        @pl.when(i == 0)
        def _():
            dh_ref[...] = jnp.zeros_like(dh_ref)
            dhalo_ref[...] = jnp.zeros_like(dhalo_ref)
            for r in g_refs:
                r[...] = jnp.zeros_like(r)

        keep = (c > 0).astype(f32)
        us, vs, zs, xbcs, halos, dtblk = _mixer_leaves(proj_ref, halo_ref, keep)
        hps = [hs_ref[j * CH:(j + 1) * CH, :] for j in range(N_BLK)]
        pvals = {n: r[...] for n, r in zip(names, p_refs)}

        def fn(us, vs, zs, xbcs, halos, dtblk, hps, pvals):
            return _mixer_chunk(us, vs, zs, xbcs, halos, dtblk, hps, _split_mixer_params(pvals))

        _, vjp = jax.vjp(fn, us, vs, zs, xbcs, halos, dtblk, hps, pvals)
        da = [dab_ref[:, j * CH:(j + 1) * CH].astype(f32) for j in range(N_BLK)]
        db = [dab_ref[:, D_MODEL + j * CH: D_MODEL + (j + 1) * CH].astype(f32) for j in range(N_BLK)]
        dh = [dh_ref[j * CH:(j + 1) * CH, :] for j in range(N_BLK)]
        dus, dvs, dzs, dxbcs, dhalos, ddt, dhps, dp = vjp((da, db, dh))

        for j in range(N_BLK):
            dproj_ref[:, OFF_U + j * CH: OFF_U + (j + 1) * CH] = dus[j].astype(bf16)
            dproj_ref[:, OFF_V + j * CH: OFF_V + (j + 1) * CH] = dvs[j].astype(bf16)
            dproj_ref[:, OFF_Z + j * CH: OFF_Z + (j + 1) * CH] = dzs[j].astype(bf16)
            dh_ref[j * CH:(j + 1) * CH, :] = dhps[j]
        zeros_top = jnp.zeros((CH - 8, CH), f32)
        for j in range(XBC_BLKS):
            late = jnp.concatenate([zeros_top, dhalo_ref[:, j * CH:(j + 1) * CH]], axis=0)
            dproj_ref[:, OFF_X + j * CH: OFF_X + (j + 1) * CH] = (dxbcs[j] + late).astype(bf16)
        for j in range(XBC_BLKS):
            dhalo_ref[:, j * CH:(j + 1) * CH] = dhalos[j] * keep
        lane = lax.broadcasted_iota(jnp.int32, (CH, CH), 1)
        dproj_ref[:, OFF_DT: OFF_DT + CH] = jnp.where(lane < SSM_HEADS, ddt, 0.0).astype(bf16)
        dproj_ref[:, OFF_DT + CH:] = jnp.zeros((CH, NP_IN - OFF_DT - CH), bf16)
        for n, r in zip(names, g_refs):
            r[...] += dp[n]

    def const(shape):
        return pl.BlockSpec(shape, lambda i: (0,) * len(shape))

    outs = pl.pallas_call(
        body, name="mixer_bwd", grid=(nc,),
        in_specs=[pl.BlockSpec((CH, NP_IN), lambda i: (nc - 1 - i, 0)),
                  pl.BlockSpec((8, NP_IN), lambda i: (jnp.maximum((nc - 1 - i) * (CH // 8) - 1, 0), 0)),
                  pl.BlockSpec((None, D_MODEL, CH), lambda i: (nc - 1 - i, 0, 0)),
                  pl.BlockSpec((CH, 2 * D_MODEL), lambda i: (nc - 1 - i, 0))]
                 + [const(shp) for _, shp in _MIXER_PARAM_SHAPES],
        out_specs=[pl.BlockSpec((CH, NP_IN), lambda i: (nc - 1 - i, 0))]
                  + [const(shp) for _, shp in _MIXER_PARAM_SHAPES],
        out_shape=[jax.ShapeDtypeStruct((s, NP_IN), bf16)]
                  + [jax.ShapeDtypeStruct(shp, f32) for _, shp in _MIXER_PARAM_SHAPES],
        scratch_shapes=[pltpu.VMEM((D_MODEL, CH), f32), pltpu.VMEM((8, 2048), f32)],
        compiler_params=_cparams(("arbitrary",)),
    )(proj, proj, hstates, dab, *[prm[n] for n in names])
    return outs[0], dict(zip(names, outs[1:]))


_K_BLK = D_MODEL // CH
_V_BLK = _K_BLK + 1


def _attn_specs(rev, nb):
    def blk(i):
        return nb - 1 - i if rev else i

    q_spec = pl.BlockSpec((CH, D_MODEL), lambda i: (blk(i), 0))
    kv = lambda col, prev: pl.BlockSpec(
        (CH, CH), lambda i: (jnp.maximum(blk(i) - 1, 0) if prev else blk(i), col))
    return q_spec, [kv(_K_BLK, True), kv(_K_BLK, False), kv(_V_BLK, True), kv(_V_BLK, False)]


def attn_fwd(qkv, sink_row):
    s = qkv.shape[0]
    nb = s // CH

    def body(q_ref, kp_ref, kc_ref, vp_ref, vc_ref, sink_ref, o_ref):
        qps = [q_ref[:, p * CH:(p + 1) * CH] for p in range(N_BLK)]
        outs = _attn_block(qps, kp_ref[...], kc_ref[...], vp_ref[...], vc_ref[...], sink_ref[...],
                           pl.program_id(0) == 0)
        for p in range(N_BLK):
            o_ref[:, p * CH:(p + 1) * CH] = outs[p].astype(bf16)

    q_spec, kv_specs = _attn_specs(False, nb)
    return pl.pallas_call(
        body, name="attn_fwd", grid=(nb,),
        in_specs=[q_spec] + kv_specs + [pl.BlockSpec((1, CH), lambda i: (0, 0))],
        out_specs=pl.BlockSpec((CH, D_MODEL), lambda i: (i, 0)),
        out_shape=jax.ShapeDtypeStruct((s, D_MODEL), bf16),
        compiler_params=_cparams(("parallel",)),
    )(qkv, qkv, qkv, qkv, qkv, sink_row)


def attn_bwd(qkv, sink_row, dout):
    s = qkv.shape[0]
    nb = s // CH

    def body(q_ref, kp_ref, kc_ref, vp_ref, vc_ref, sink_ref, do_ref, dqkv_ref, dsink_ref, carry_ref):
        i = pl.program_id(0)
        blk = nb - 1 - i

        @pl.when(i == 0)
        def _():
            dsink_ref[...] = jnp.zeros_like(dsink_ref)
            carry_ref[...] = jnp.zeros_like(carry_ref)

        qps = [q_ref[:, p * CH:(p + 1) * CH] for p in range(N_BLK)]
        first = blk == 0
        _, vjp = jax.vjp(lambda *a: _attn_block(*a, first), qps, kp_ref[...], kc_ref[...], vp_ref[...],
                         vc_ref[...], sink_ref[...])
        dos = [do_ref[:, p * CH:(p + 1) * CH].astype(f32) for p in range(N_BLK)]
        dqs, dkp, dkc, dvp, dvc, dsink = vjp(dos)
        for p in range(N_BLK):
            dqkv_ref[:, p * CH:(p + 1) * CH] = dqs[p].astype(bf16)
        dqkv_ref[:, D_MODEL: D_MODEL + CH] = (dkc + carry_ref[0]).astype(bf16)
        dqkv_ref[:, D_MODEL + CH:] = (dvc + carry_ref[1]).astype(bf16)
        keep = jnp.logical_not(first).astype(f32)
        carry_ref[0] = dkp * keep
        carry_ref[1] = dvp * keep
        dsink_ref[...] += dsink

    q_spec, kv_specs = _attn_specs(True, nb)
    return pl.pallas_call(
        body, name="attn_bwd", grid=(nb,),
        in_specs=[q_spec] + kv_specs + [pl.BlockSpec((1, CH), lambda i: (0, 0)),
                                        pl.BlockSpec((CH, D_MODEL), lambda i: (nb - 1 - i, 0))],
        out_specs=[pl.BlockSpec((CH, QKV_DIM), lambda i: (nb - 1 - i, 0)), pl.BlockSpec((1, CH), lambda i: (0, 0))],
        out_shape=[jax.ShapeDtypeStruct((s, QKV_DIM), bf16), jax.ShapeDtypeStruct((1, CH), f32)],
        scratch_shapes=[pltpu.VMEM((2, CH, CH), f32)],
        compiler_params=_cparams(("arbitrary",)),
    )(qkv, qkv, qkv, qkv, qkv, sink_row, dout)


def adamw(w, g, m, v, name):
    def body(w_ref, g_ref, m_ref, v_ref, d_ref, nm_ref, nv_ref):
        gv = g_ref[...]
        nm = ADAM_B1 * m_ref[...] + (1.0 - ADAM_B1) * gv
        nv = ADAM_B2 * v_ref[...] + (1.0 - ADAM_B2) * jnp.square(gv)
        m_hat = nm / (1.0 - ADAM_B1 ** ADAM_STEP)
        v_hat = nv / (1.0 - ADAM_B2 ** ADAM_STEP)
        d_ref[...] = -ADAM_LR * (m_hat / (jnp.sqrt(v_hat) + ADAM_EPS) + ADAM_WD * w_ref[...])
        nm_ref[...] = nm
        nv_ref[...] = nv

    out_shape = [jax.ShapeDtypeStruct(w.shape, f32)] * 3
    if w.ndim == 3 and w.shape[1] == 1:
        tr = max(t for t in range(1, 129) if w.shape[0] % t == 0)
        tile = pl.BlockSpec((tr, 1, w.shape[2]), lambda i: (i, 0, 0))
        return pl.pallas_call(
            body, name=name, grid=(w.shape[0] // tr,),
            in_specs=[tile] * 4, out_specs=[tile] * 3, out_shape=out_shape,
            compiler_params=_cparams(("parallel",)),
        )(w, g, m, v)
    if w.ndim == 3 and w.shape[1] % 256 == 0:
        tile = pl.BlockSpec((None, 256, w.shape[2]), lambda l, i: (l, i, 0))
        return pl.pallas_call(
            body, name=name, grid=(w.shape[0], w.shape[1] // 256),
            in_specs=[tile] * 4, out_specs=[tile] * 3, out_shape=out_shape,
            compiler_params=_cparams(("parallel", "parallel")),
        )(w, g, m, v)
    return pl.pallas_call(body, name=name, in_specs=[_VMEM] * 4, out_specs=[_VMEM] * 3, out_shape=out_shape,
                          compiler_params=_cparams())(w, g, m, v)


_MESH = pl.DeviceIdType.MESH
_ANY = pl.BlockSpec(memory_space=pl.ANY)
_VMEM = pl.BlockSpec(memory_space=pltpu.VMEM)


def _place():
    x, y, c = lax.axis_index("x"), lax.axis_index("y"), lax.axis_index("c")
    chips = [(1 - x, y), (x, 1 - y), (1 - x, 1 - y)]
    return x, y, c, 2 * x + y, chips, [2 * cx + cy for cx, cy in chips]


def _half(c, rows):
    return pl.ds(pl.multiple_of(c * (rows // 2), 16), rows // 2)


def _step_rows(rows):
    return max(t for t in range(16, 641, 16) if rows % t == 0)


def place_shard(b, slot, name):
    r, c = b.shape
    tr = _step_rows(r)

    def body(slot_ref, b_ref, o_ref):
        o_ref[...] = b_ref[...]

    return pl.pallas_call(
        body, name=name,
        grid_spec=pltpu.PrefetchScalarGridSpec(
            num_scalar_prefetch=1, grid=(r // tr,),
            in_specs=[pl.BlockSpec((tr, c), lambda i, s: (i, 0))],
            out_specs=pl.BlockSpec((None, tr, c), lambda i, s: (s[0], i, 0))),
        out_shape=jax.ShapeDtypeStruct((N_CHIPS, r, c), b.dtype),
        compiler_params=_cparams(("parallel",)),
    )(slot, b)


_HBM = pl.BlockSpec(memory_space=pltpu.HBM)
_SEM = pl.BlockSpec(memory_space=pltpu.SEMAPHORE)
_EFFECT = pltpu.SideEffectType.DATAFLOW_SIDE_EFFECTING


def _gather_ici_copies(bufs, send_sems, recv_sems):
    x, y, c, me, chips, chip_idx = _place()
    return [pltpu.make_async_remote_copy(
        src_ref=buf.at[me, _half(c, buf.shape[1])], dst_ref=buf.at[chip_idx[j], _half(c, buf.shape[1])],
        send_sem=send_sems.at[3 * k + j], recv_sem=recv_sems.at[3 * k + j],
        device_id=(*chips[j], c), device_id_type=_MESH) for j in range(3) for k, buf in enumerate(bufs)]


def gather_start(groups):
    sizes = [len(g) for g in groups]
    flat = [b for g in groups for b in g]
    n = len(flat)

    def body(*refs):
        bufs, sems = refs[:n], refs[n:n + 2 * len(groups)]
        x, y, c, me, chips, chip_idx = _place()
        lo = 0
        for gi, size in enumerate(sizes):
            for j in range(3):
                for k, buf in enumerate(bufs[lo:lo + size]):
                    mine = buf.at[me, _half(c, buf.shape[1])]
                    pltpu.make_async_remote_copy(
                        src_ref=mine, dst_ref=mine, send_sem=sems[2 * gi].at[3 * k + j],
                        recv_sem=sems[2 * gi + 1].at[3 * k + j], device_id=(*chips[j], c),
                        device_id_type=_MESH).start()
            lo += size

    sem_shapes = [pltpu.SemaphoreType.DMA((3 * size,)) for size in sizes for _ in range(2)]
    outs = pl.pallas_call(
        body, name="gather_start",
        out_shape=(*sem_shapes, *[pltpu.HBM(b.shape, b.dtype) for b in flat]),
        in_specs=[_HBM] * n, out_specs=(*[_SEM] * len(sem_shapes), *[_HBM] * n),
        input_output_aliases={i: len(sem_shapes) + i for i in range(n)},
        compiler_params=pltpu.CompilerParams(has_side_effects=_EFFECT),
    )(*[pltpu.with_memory_space_constraint(b, pltpu.HBM) for b in flat])
    sems = [(outs[2 * gi], outs[2 * gi + 1]) for gi in range(len(groups))]
    thru, lo = [], len(sem_shapes)
    for size in sizes:
        thru.append(list(outs[lo:lo + size]))
        lo += size
    return sems, thru


def gather_wait(bufs, sems, after, tag):
    n = len(bufs)

    def body(*refs):
        for cp in _gather_ici_copies(refs[:n], refs[n], refs[n + 1]):
            cp.wait_send()
            cp.wait_recv()

    extra = [] if after is None else [after]
    return list(pl.pallas_call(
        body, name=f"gather_wait_{tag}",
        out_shape=[pltpu.HBM(b.shape, b.dtype) for b in bufs],
        in_specs=[_HBM] * n + [_SEM, _SEM] + [_ANY] * len(extra), out_specs=[_HBM] * n,
        input_output_aliases={i: i for i in range(n)},
        compiler_params=pltpu.CompilerParams(has_side_effects=_EFFECT),
    )(*bufs, *sems, *extra))


def gather_forward(bufs, tag):
    n = len(bufs)

    def body(*refs):
        out_refs = refs[n:2 * n]
        send_sems, recv_sems = refs[2 * n:]
        x, y, c, me, chips, chip_idx = _place()

        def copy(k, j, half):
            part = out_refs[k].at[chip_idx[j], _half(half, out_refs[k].shape[1])]
            return pltpu.make_async_remote_copy(
                src_ref=part, dst_ref=part, send_sem=send_sems.at[3 * k + j], recv_sem=recv_sems.at[3 * k + j],
                device_id=(x, y, 1 - c), device_id_type=_MESH)

        sends = [copy(k, j, c) for j in range(3) for k in range(n)]
        for cp in sends:
            cp.start()
        for j in range(3):
            for k in range(n):
                copy(k, j, 1 - c).wait_recv()
        for cp in sends:
            cp.wait_send()

    return list(pl.pallas_call(
        body, name=f"gather_forward_{tag}",
        out_shape=[jax.ShapeDtypeStruct(b.shape, b.dtype) for b in bufs],
        in_specs=[_ANY] * n, out_specs=[_ANY] * n, input_output_aliases={i: i for i in range(n)},
        scratch_shapes=[pltpu.SemaphoreType.DMA((3 * n,)), pltpu.SemaphoreType.DMA((3 * n,))],
    )(*bufs))


def exchange_halves(bufs, tag):
    n = len(bufs)

    def body(*refs):
        g_refs, out_refs = refs[:n], refs[n:2 * n]
        send_sems, recv_sems = refs[2 * n:]
        x, y, c, *_ = _place()
        cps = [pltpu.make_async_remote_copy(
            src_ref=g_refs[b].at[:, _half(1 - c, g_refs[b].shape[1])], dst_ref=out_refs[b],
            send_sem=send_sems.at[b], recv_sem=recv_sems.at[b], device_id=(x, y, 1 - c), device_id_type=_MESH)
            for b in range(n)]
        for cp in cps:
            cp.start()
        for cp in cps:
            cp.wait()

    return pl.pallas_call(
        body, name=f"exchange_halves_{tag}",
        out_shape=[jax.ShapeDtypeStruct((N_CHIPS, b.shape[1] // 2, b.shape[2]), b.dtype) for b in bufs],
        in_specs=[_ANY] * n, out_specs=[_ANY] * n,
        scratch_shapes=[pltpu.SemaphoreType.DMA((n,)), pltpu.SemaphoreType.DMA((n,))],
    )(*bufs)


def add_halves(g, got, c_idx, name):
    hr, cols = got.shape[1], got.shape[2]
    tr = _step_rows(hr)
    steps = hr // tr

    def body(c_ref, g_ref, got_ref, o_ref):
        o_ref[...] = (g_ref[...].astype(f32) + got_ref[...].astype(f32)).astype(bf16)

    return pl.pallas_call(
        body, name=name,
        grid_spec=pltpu.PrefetchScalarGridSpec(
            num_scalar_prefetch=1, grid=(N_CHIPS, steps),
            in_specs=[pl.BlockSpec((None, tr, cols), lambda s, i, c: (s, c[0] * steps + i, 0)),
                      pl.BlockSpec((None, tr, cols), lambda s, i, c: (s, i, 0))],
            out_specs=pl.BlockSpec((None, tr, cols), lambda s, i, c: (s, i, 0))),
        out_shape=jax.ShapeDtypeStruct(got.shape, bf16),
        compiler_params=_cparams(("parallel", "parallel")),
    )(c_idx, g, got)


def sum_chips(t, got, place_idx, name):
    hr, cols = t.shape[1], t.shape[2]
    tr = _step_rows(hr)
    steps = hr // tr

    def body(idx_ref, t_ref, got_ref, o_ref):
        acc = t_ref[...].astype(f32)
        for j in range(3):
            acc = acc + got_ref[j].astype(f32)
        o_ref[...] = acc

    return pl.pallas_call(
        body, name=name,
        grid_spec=pltpu.PrefetchScalarGridSpec(
            num_scalar_prefetch=1, grid=(steps,),
            in_specs=[pl.BlockSpec((None, tr, cols), lambda i, idx: (idx[0], i, 0)),
                      pl.BlockSpec((3, tr, cols), lambda i, idx: (0, i, 0))],
            out_specs=pl.BlockSpec((tr, cols), lambda i, idx: (idx[1] * steps + i, 0))),
        out_shape=jax.ShapeDtypeStruct((2 * hr, cols), f32),
        compiler_params=_cparams(("parallel",)),
    )(place_idx, t, got)


def share_halves(bufs, tag):
    n = len(bufs)

    def body(*refs):
        out_refs = refs[n:2 * n]
        send_sems, recv_sems = refs[2 * n:]
        x, y, c, *_ = _place()

        def copy(b, half):
            part = out_refs[b].at[_half(half, out_refs[b].shape[0])]
            return pltpu.make_async_remote_copy(
                src_ref=part, dst_ref=part, send_sem=send_sems.at[b], recv_sem=recv_sems.at[b],
                device_id=(x, y, 1 - c), device_id_type=_MESH)

        for b in range(n):
            copy(b, c).start()
        for b in range(n):
            copy(b, 1 - c).wait_recv()
        for b in range(n):
            copy(b, c).wait_send()

    return pl.pallas_call(
        body, name=f"share_halves_{tag}",
        out_shape=[jax.ShapeDtypeStruct(b.shape, b.dtype) for b in bufs],
        in_specs=[_ANY] * n, out_specs=[_ANY] * n, input_output_aliases={i: i for i in range(n)},
        scratch_shapes=[pltpu.SemaphoreType.DMA((n,)), pltpu.SemaphoreType.DMA((n,))],
    )(*bufs)


def _scatter_copies(t_refs, land_refs, send_sems, recv_sems):
    x, y, c, me, chips, chip_idx = _place()
    return [pltpu.make_async_remote_copy(
        src_ref=t_refs[b].at[chip_idx[j]], dst_ref=land_refs[b].at[j], send_sem=send_sems.at[3 * b + j],
        recv_sem=recv_sems.at[3 * b + j], device_id=(*chips[j], c), device_id_type=_MESH)
        for j in range(3) for b in range(len(t_refs))]


def scatter_start(ts, tag):
    n = len(ts)
    lands = [lax.empty((3,) + t.shape[1:], t.dtype) for t in ts]

    def body(*refs):
        for cp in _scatter_copies(refs[:n], refs[n:2 * n], refs[2 * n], refs[2 * n + 1]):
            cp.start()
        token = refs[-1]
        token[...] = jnp.zeros_like(token)

    hbm = [pltpu.HBM(a.shape, a.dtype) for a in (*ts, *lands)]
    outs = pl.pallas_call(
        body, name=f"scatter_start_{tag}",
        out_shape=(pltpu.SemaphoreType.DMA((3 * n,)), pltpu.SemaphoreType.DMA((3 * n,)), *hbm,
                   jax.ShapeDtypeStruct((8, 128), f32)),
        in_specs=[_HBM] * (2 * n), out_specs=(_SEM, _SEM, *[_HBM] * (2 * n), _VMEM),
        input_output_aliases={i: 2 + i for i in range(2 * n)},
        compiler_params=pltpu.CompilerParams(has_side_effects=_EFFECT),
    )(*[pltpu.with_memory_space_constraint(a, pltpu.HBM) for a in (*ts, *lands)])
    return outs[0], outs[1], list(outs[2:2 + n]), list(outs[2 + n:2 + 2 * n]), outs[-1]


def scatter_wait(send_sems, recv_sems, ts, lands, after, tag):
    n = len(ts)

    def body(*refs):
        for cp in _scatter_copies(refs[:n], refs[n:2 * n], refs[2 * n], refs[2 * n + 1]):
            cp.wait_send()
            cp.wait_recv()

    outs = pl.pallas_call(
        body, name=f"scatter_wait_{tag}",
        out_shape=[pltpu.HBM(a.shape, a.dtype) for a in (*ts, *lands)],
        in_specs=[_HBM] * (2 * n) + [_SEM, _SEM, _ANY], out_specs=[_HBM] * (2 * n),
        input_output_aliases={i: i for i in range(2 * n)},
        compiler_params=pltpu.CompilerParams(has_side_effects=_EFFECT),
    )(*ts, *lands, send_sems, recv_sems, after)
    return list(outs[:n]), list(outs[n:])


class GradReducer:
    def __init__(self, c_idx, place_idx):
        self.c_idx, self.place_idx = c_idx, place_idx

    def start(self, bufs, tag):
        got = exchange_halves(bufs, tag)
        ts = [add_halves(b, g, self.c_idx, f"add_halves_{tag}{i}") for i, (b, g) in enumerate(zip(bufs, got))]
        send_sems, recv_sems, ts, lands, token = scatter_start(ts, tag)
        return (send_sems, recv_sems, ts, lands), token

    def finish(self, state, after, tag):
        ts, lands = scatter_wait(*state, after, tag)
        sums = [sum_chips(t, l, self.place_idx, f"sum_chips_{tag}{i}") for i, (t, l) in enumerate(zip(ts, lands))]
        return share_halves(sums, tag)


def allreduce_small(sp):
    def body(s_ref, out_ref, gather_ref, send_sems, recv_sems):
        x, y, c, me, chips, chip_idx = _place()
        sibling = (x, y, 1 - c)

        def copy(k, chip, core, to, src=None):
            dst = gather_ref.at[2 * chip + core]
            return pltpu.make_async_remote_copy(
                src_ref=dst if src is None else src, dst_ref=dst, send_sem=send_sems.at[k],
                recv_sem=recv_sems.at[k], device_id=to, device_id_type=_MESH)

        first = [copy(0, me, c, sibling, src=s_ref)]
        first += [copy(1 + j, me, c, (*chips[j], c), src=s_ref) for j in range(3)]
        for cp in first:
            cp.start()
        gather_ref[2 * me + c] = s_ref[...]
        passed = [copy(4 + j, chip_idx[j], c, sibling) for j in range(3)]
        for j in range(3):
            copy(1 + j, chip_idx[j], c, sibling).wait_recv()
            passed[j].start()
        copy(0, me, 1 - c, sibling).wait_recv()
        for j in range(3):
            copy(4 + j, chip_idx[j], 1 - c, sibling).wait_recv()
        for cp in first + passed:
            cp.wait_send()
        acc = gather_ref[0]
        for d in range(1, 2 * N_CHIPS):
            acc = acc + gather_ref[d]
        out_ref[...] = acc

    return pl.pallas_call(
        body, name="allreduce_small",
        out_shape=jax.ShapeDtypeStruct(sp.shape, sp.dtype),
        in_specs=[_VMEM], out_specs=_VMEM,
        scratch_shapes=[pltpu.VMEM((2 * N_CHIPS,) + sp.shape, sp.dtype),
                        pltpu.SemaphoreType.DMA((7,)), pltpu.SemaphoreType.DMA((7,))],
        compiler_params=_cparams(),
    )(sp)


def _n_rows(shape):
    n = 1
    for d in shape:
        n *= d
    return 8 * (-(-n // 8192))


def _pack(arrays, total_rows):
    parts = []
    for a in arrays:
        flat = a.reshape(-1)
        parts.append(jnp.pad(flat, (0, 1024 * _n_rows(a.shape) - flat.shape[0])).reshape(-1, 1024))
    rows = jnp.concatenate(parts, axis=0)
    return jnp.pad(rows, ((0, total_rows - rows.shape[0]), (0, 0)))


def _unpack(packed, shapes):
    out, r = [], 0
    for shp in shapes:
        n = 1
        for d in shp:
            n *= d
        nr = _n_rows(shp)
        out.append(packed[r:r + nr].reshape(-1)[:n].reshape(shp))
        r += nr
    return out


_COLUMN_SHARDED = ("w_in_even", "w_qkv")
IN_SHARD, IN_PAD = 1284, 1408
QKV_SHARD, QKV_PAD = 320, 384


def _lane_padded(a, cols):
    return jnp.pad(a, ((0, 0), (0, cols - a.shape[1])))


_SMALL_SHAPES = (
    ("norm_mix_g", (2, 1024)), ("norm_mlp_g", (2, 1024)), ("final_norm_g", (1024,)), ("gm_ln_g", (1, 1024)),
    ("gm_ln_b", (1, 1024)), ("gm_w_s", (1, 8, 128, 128)), ("gm_b_s", (1, 8, 128)), ("ssm_conv_b", (1, 2048)),
    ("ssm_dt_bias", (1, 16)), ("ssm_a_log", (1, 16)), ("ssm_d", (1, 16)), ("ssm_norm_g", (1, 1024)),
    ("attn_sinks", (1, 16)), ("ssm_conv_w", (1, 4, 2048)), ("b_qkv", (1, 1280)), ("b_o", (1, 1024)),
)
_N_REPLICATED = 13
_SHARDED_SMALL = (("ssm_conv_w", 2, 512), ("b_qkv", 1, 320), ("b_o", 1, 256))
_SHARD_PACK_ROWS = 32


def _cols_by_owner(a):
    return a.transpose(1, 0, 2).reshape(a.shape[1], -1)


class WeightGatherer:
    def __init__(self, w, chip_idx):
        rows = lambda *parts: jnp.concatenate(parts, axis=0).astype(bf16)
        shards = [
            ("in", _lane_padded(w["w_in_even"][0], IN_PAD).astype(bf16)),
            ("l0", rows(w["w_out_even"][0], w["w_up"][0], w["w_down"][0])),
            ("l1", rows(w["w_o"][0], w["w_up"][1], w["w_down"][1])),
            ("qkv", _lane_padded(w["w_qkv"][0], QKV_PAD).astype(bf16)),
        ]
        shards.append(("small", _pack([w[n] for n, _, _ in _SHARDED_SMALL], _SHARD_PACK_ROWS)))
        placed = [place_shard(b, chip_idx, f"place_shard_{tag}") for tag, b in shards]
        self.sems, self.bufs = gather_start([[placed[0], placed[4]], placed[1:2], placed[2:4]])

    def _group(self, gi, after, tag):
        return gather_forward(gather_wait(self.bufs[gi], self.sems[gi], after, tag), tag)

    def mixer_in(self):
        g, small = self._group(0, None, "in")
        shard_shapes = [tuple(width if i == axis else d for i, d in enumerate(dict(_SMALL_SHAPES)[n]))
                        for n, axis, width in _SHARDED_SMALL]
        per_chip = [_unpack(small[s], shard_shapes) for s in range(N_CHIPS)]
        full = {n: jnp.concatenate([per_chip[s][i] for s in range(N_CHIPS)], axis=axis)
                for i, (n, axis, _) in enumerate(_SHARDED_SMALL)}
        return _lane_padded(_cols_by_owner(g[:, :, :IN_SHARD]), NP_IN), full

    def layer0(self, after):
        (g,) = self._group(1, after, "l0")
        return g[:, :512].reshape(2048, 1024), _cols_by_owner(g[:, 512:1536]), g[:, 1536:].reshape(4096, 1024)

    def layer1(self, after):
        g, q = self._group(2, after, "l1")
        return (_cols_by_owner(q[:, :, :QKV_SHARD]), g[:, :256].reshape(1024, 1024), _cols_by_owner(g[:, 256:1280]),
                g[:, 1280:].reshape(4096, 1024))


def _row2(v):
    return v.reshape(1, -1)


def _lane_pad(v):
    return jnp.pad(v, ((0, 0), (0, CH - v.shape[1])))


def _mlp_fwd(h, g_row, w_up, w_down, tag):
    y = rmsnorm_fwd(h, g_row, f"mlp_norm{tag}")
    a = matmul(y, w_up, dims="nn", name=f"mlp_up{tag}", out_dtype=bf16, tn=1024)
    out = matmul(a, w_down, dims="nn", name=f"mlp_down{tag}", a_pro=_relu2, epi=_add, epi_args=(("tile", h),))
    return out, y, a


def _mlp_bwd(dh_out, h, g_row, y, a, w_up, w_down, tag, after=None):
    da = matmul(dh_out, w_down, dims="nt", name=f"mlp_da{tag}", out_dtype=bf16, tn=1024,
                epi=_times_relu2_grad, epi_args=(("tile", a),), after=after)
    dw_down = matmul(a, dh_out, dims="tn", name=f"mlp_dwdown{tag}", out_dtype=bf16, a_pro=_relu2)
    dw_up = matmul(y, da, dims="tn", name=f"mlp_dwup{tag}", out_dtype=bf16, tn=1024, out_by_col_tile=True)
    dy = matmul(da, w_up, dims="nt", name=f"mlp_dy{tag}")
    dh, dg = rmsnorm_bwd(h, g_row, dy, dh_out, f"mlp_dnorm{tag}")
    return dh, dg, dw_up, dw_down


def _by_owner(a):
    return a.reshape(N_CHIPS, a.shape[0] // N_CHIPS, a.shape[1])


def _col_shards(a, shard, padded):
    return jnp.stack([_lane_padded(a[:, shard * s: shard * (s + 1)], padded) for s in range(N_CHIPS)])


def _local_step(x, target, weights, sm, reducer):
    w_up, w_down = [None, None], [None, None]
    w_in_p, sharded_small = weights.mixer_in()
    sm = {**sm, **sharded_small}
    mix_g = [_row2(sm["norm_mix_g"][i]) for i in range(2)]
    mlp_g = [_row2(sm["norm_mlp_g"][i]) for i in range(2)]
    mixer_prm = {
        "ln_g": sm["gm_ln_g"], "ln_b": sm["gm_ln_b"], "wm": sm["gm_w_s"][0],
        "bs_t": jnp.pad(sm["gm_b_s"][0].T, ((0, 0), (0, CH - N_BLK))),
        "conv_w": jnp.pad(sm["ssm_conv_w"][0], ((0, 4), (0, 0))), "conv_b": sm["ssm_conv_b"],
        "dt_bias": _lane_pad(sm["ssm_dt_bias"]), "a_log": _lane_pad(sm["ssm_a_log"]),
        "d_heads": _lane_pad(sm["ssm_d"]), "norm_g": sm["ssm_norm_g"],
    }
    sink_row = _lane_pad(sm["attn_sinks"])

    y0 = rmsnorm_fwd(x, mix_g[0], "mix_norm0")
    proj = matmul(y0, w_in_p, dims="nn", name="in_proj", tn=768)
    ab, hstates = mixer_fwd(proj, mixer_prm)
    w_out, w_up[0], w_down[0] = weights.layer0(ab)
    h1 = matmul(ab, w_out, dims="nn", name="out_proj", epi=_add, epi_args=(("tile", x),))
    h2, y1, a1 = _mlp_fwd(h1, mlp_g[0], w_up[0], w_down[0], 0)
    w_qkv, w_o, w_up[1], w_down[1] = weights.layer1(h2)
    y2 = rmsnorm_fwd(h2, mix_g[1], "mix_norm1")
    qkv = matmul(y2, w_qkv, dims="nn", name="qkv_proj", tn=QKV_DIM, epi=_add_bias, epi_args=(("row", sm["b_qkv"]),))
    att = attn_fwd(qkv, sink_row)
    h3 = matmul(att, w_o, dims="nn", name="o_proj", epi=_add_bias_res,
                epi_args=(("row", sm["b_o"]), ("tile", h2)))
    h4, y3, a3 = _mlp_fwd(h3, mlp_g[1], w_up[1], w_down[1], 1)
    loss, dh4, dg_final = final_loss(h4, _row2(sm["final_norm_g"]), target, "final_loss")

    dh3, dg_mlp1, dw_up1, dw_down1 = _mlp_bwd(dh4, h3, mlp_g[1], y3, a3, w_up[1], w_down[1], 1)
    db_o = colsum(dh3, "db_o")
    datt = matmul(dh3, w_o, dims="nt", name="attn_dout", out_dtype=bf16)
    dw_o = matmul(att, dh3, dims="tn", name="dw_o", out_dtype=bf16)
    dqkv, dsink = attn_bwd(qkv, sink_row, datt)
    db_qkv = colsum(dqkv, "db_qkv")
    dw_qkv = matmul(y2, dqkv, dims="tn", name="dw_qkv", out_dtype=bf16, tn=QKV_DIM)
    dy2 = matmul(dqkv, w_qkv, dims="nt", name="dy_qkv", tk=QKV_DIM)
    dh2, dg_mix1 = rmsnorm_bwd(h2, mix_g[1], dy2, dh3, "mix_dnorm1")
    layer1 = [jnp.concatenate([_by_owner(dw_o), dw_up1, _by_owner(dw_down1)], axis=1),
              _col_shards(dw_qkv, QKV_SHARD, QKV_PAD)]
    flight1, token1 = reducer.start(layer1, "l1")
    dh1, dg_mlp0, dw_up0, dw_down0 = _mlp_bwd(dh2, h1, mlp_g[0], y1, a1, w_up[0], w_down[0], 0, after=token1)
    r_l1, r_qkv = reducer.finish(flight1, dh1, "l1")
    dw_out = matmul(ab, dh1, dims="tn", name="dw_out", out_dtype=bf16)
    flight0, token0 = reducer.start(
        [jnp.concatenate([dw_up0, _by_owner(dw_down0), _by_owner(dw_out)], axis=1)], "l0")
    dab = matmul(dh1, w_out, dims="nt", name="mixer_dout", tn=1024, after=token0)
    dproj, dmix = mixer_bwd(proj, hstates, dab, mixer_prm)
    dw_in_p = matmul(y0, dproj, dims="tn", name="dw_in", out_dtype=bf16, tn=768)
    (r_l0,) = reducer.finish(flight0, dw_in_p, "l0")
    flight_in, token_in = reducer.start([_col_shards(dw_in_p, IN_SHARD, IN_PAD)], "in")
    dy0 = matmul(dproj, w_in_p, dims="nt", name="dy_in", after=token_in)
    dx, dg_mix0 = rmsnorm_bwd(x, mix_g[0], dy0, dh1, "mix_dnorm0")
    (r_in,) = reducer.finish(flight_in, dx, "in")
    reduced = {
        "w_out_even": r_l0[None, 2048:], "w_in_even": r_in[None, :, :IN_SHARD], "w_qkv": r_qkv[None, :, :QKV_SHARD],
        "w_o": r_l1[None, :256], "w_up": jnp.stack([r_l0[:1024], r_l1[256:1280]]),
        "w_down": jnp.stack([r_l0[1024:2048], r_l1[1280:]]),
    }

    small_grads = {
        "norm_mix_g": jnp.concatenate([dg_mix0, dg_mix1], axis=0),
        "norm_mlp_g": jnp.concatenate([dg_mlp0, dg_mlp1], axis=0),
        "final_norm_g": dg_final[0], "gm_ln_g": dmix["ln_g"], "gm_ln_b": dmix["ln_b"],
        "gm_w_s": dmix["wm"][None], "gm_b_s": dmix["bs_t"][:, :N_BLK].T[None],
        "ssm_conv_b": dmix["conv_b"], "ssm_dt_bias": dmix["dt_bias"][:, :SSM_HEADS],
        "ssm_a_log": dmix["a_log"][:, :SSM_HEADS], "ssm_d": dmix["d_heads"][:, :SSM_HEADS],
        "ssm_norm_g": dmix["norm_g"], "attn_sinks": dsink[:, :SSM_HEADS],
        "ssm_conv_w": dmix["conv_w"][None, :4], "b_qkv": db_qkv, "b_o": db_o,
    }
    return loss, dx, reduced, small_grads


def kernel(x, norm_mix_g, norm_mlp_g, final_norm_g, w_in_even, w_out_even, gm_ln_g, gm_ln_b, gm_w_s, gm_b_s, ssm_conv_w, ssm_conv_b, ssm_dt_bias, ssm_a_log, ssm_d, ssm_norm_g, w_qkv, b_qkv, w_o, b_o, attn_sinks, w_up, w_down, loss_target, m_norm_mix_g, m_norm_mlp_g, m_final_norm_g, m_w_in_even, m_w_out_even, m_gm_ln_g, m_gm_ln_b, m_gm_w_s, m_gm_b_s, m_ssm_conv_w, m_ssm_conv_b, m_ssm_dt_bias, m_ssm_a_log, m_ssm_d, m_ssm_norm_g, m_w_qkv, m_b_qkv, m_w_o, m_b_o, m_attn_sinks, m_w_up, m_w_down, v_norm_mix_g, v_norm_mlp_g, v_final_norm_g, v_w_in_even, v_w_out_even, v_gm_ln_g, v_gm_ln_b, v_gm_w_s, v_gm_b_s, v_ssm_conv_w, v_ssm_conv_b, v_ssm_dt_bias, v_ssm_a_log, v_ssm_d, v_ssm_norm_g, v_w_qkv, v_b_qkv, v_w_o, v_b_o, v_attn_sinks, v_w_up, v_w_down):
    w = dict(norm_mix_g=norm_mix_g, norm_mlp_g=norm_mlp_g, final_norm_g=final_norm_g, w_in_even=w_in_even,
             w_out_even=w_out_even, gm_ln_g=gm_ln_g, gm_ln_b=gm_ln_b, gm_w_s=gm_w_s, gm_b_s=gm_b_s,
             ssm_conv_w=ssm_conv_w, ssm_conv_b=ssm_conv_b, ssm_dt_bias=ssm_dt_bias, ssm_a_log=ssm_a_log,
             ssm_d=ssm_d, ssm_norm_g=ssm_norm_g, w_qkv=w_qkv, b_qkv=b_qkv, w_o=w_o, b_o=b_o,
             attn_sinks=attn_sinks, w_up=w_up, w_down=w_down)
    m = dict(norm_mix_g=m_norm_mix_g, norm_mlp_g=m_norm_mlp_g, final_norm_g=m_final_norm_g,
             w_in_even=m_w_in_even, w_out_even=m_w_out_even, gm_ln_g=m_gm_ln_g, gm_ln_b=m_gm_ln_b,
             gm_w_s=m_gm_w_s, gm_b_s=m_gm_b_s, ssm_conv_w=m_ssm_conv_w, ssm_conv_b=m_ssm_conv_b,
             ssm_dt_bias=m_ssm_dt_bias, ssm_a_log=m_ssm_a_log, ssm_d=m_ssm_d, ssm_norm_g=m_ssm_norm_g,
             w_qkv=m_w_qkv, b_qkv=m_b_qkv, w_o=m_w_o, b_o=m_b_o, attn_sinks=m_attn_sinks, w_up=m_w_up,
             w_down=m_w_down)
    v = dict(norm_mix_g=v_norm_mix_g, norm_mlp_g=v_norm_mlp_g, final_norm_g=v_final_norm_g,
             w_in_even=v_w_in_even, w_out_even=v_w_out_even, gm_ln_g=v_gm_ln_g, gm_ln_b=v_gm_ln_b,
             gm_w_s=v_gm_w_s, gm_b_s=v_gm_b_s, ssm_conv_w=v_ssm_conv_w, ssm_conv_b=v_ssm_conv_b,
             ssm_dt_bias=v_ssm_dt_bias, ssm_a_log=v_ssm_a_log, ssm_d=v_ssm_d, ssm_norm_g=v_ssm_norm_g,
             w_qkv=v_w_qkv, b_qkv=v_b_qkv, w_o=v_w_o, b_o=v_b_o, attn_sinks=v_attn_sinks, w_up=v_w_up,
             w_down=v_w_down)
    names = ("norm_mix_g", "norm_mlp_g", "final_norm_g", "w_in_even", "w_out_even", "gm_ln_g", "gm_ln_b",
             "gm_w_s", "gm_b_s", "ssm_conv_w", "ssm_conv_b", "ssm_dt_bias", "ssm_a_log", "ssm_d", "ssm_norm_g",
             "w_qkv", "b_qkv", "w_o", "b_o", "attn_sinks", "w_up", "w_down")

    cx, cy, cc = lax.axis_index("x"), lax.axis_index("y"), lax.axis_index("c")
    chip = 2 * cx + cy
    c_idx = jnp.reshape(cc, (1,)).astype(jnp.int32)
    chip_idx = jnp.reshape(chip, (1,)).astype(jnp.int32)

    weights = WeightGatherer(w, chip_idx)
    sm = {n: w[n] for n, _ in _SMALL_SHAPES[:_N_REPLICATED]}

    reducer = GradReducer(c_idx, jnp.concatenate([chip_idx, c_idx]))
    loss_part, dx, grads, small_grads = _local_step(x[0], loss_target[0], weights, sm, reducer)
    loss = lax.psum(loss_part[0, 0], ("x", "y", "c"))

    small_sum = allreduce_small(_pack([small_grads[n] for n, _ in _SMALL_SHAPES], SMALL_ROWS))
    small_full = dict(zip([n for n, _ in _SMALL_SHAPES], _unpack(small_sum, [s for _, s in _SMALL_SHAPES])))
    for n, _ in _SMALL_SHAPES[:_N_REPLICATED]:
        grads[n] = small_full[n]
    for n, axis, width in _SHARDED_SMALL:
        grads[n] = lax.dynamic_slice_in_dim(small_full[n], chip * width, width, axis)
    grads = {n: grads[n].reshape(w[n].shape) for n in names}

    delta, new_m, new_v = {}, {}, {}
    for n in names:
        if n in _COLUMN_SHARDED:
            args = [jnp.transpose(d[n], (2, 0, 1)) for d in (w, grads, m, v)]
            grads[n] = jnp.transpose(args[1], (1, 2, 0))
            outs = adamw(*args, f"adamw_{n}")
            delta[n], new_m[n], new_v[n] = (jnp.transpose(o, (1, 2, 0)) for o in outs)
            continue
        shape = (1,) + w[n].shape if w[n].ndim == 1 else w[n].shape
        outs = adamw(*[d[n].reshape(shape) for d in (w, grads, m, v)], f"adamw_{n}")
        delta[n], new_m[n], new_v[n] = (o.reshape(w[n].shape) for o in outs)

    return (loss, dx[None], *[grads[n] for n in names], *[delta[n] for n in names],
            *[new_m[n] for n in names], *[new_v[n] for n in names])
```

```python
import functools

import jax
import jax.numpy as jnp
from jax import lax
from jax.experimental import pallas as pl
from jax.experimental.pallas import tpu as pltpu

f32 = jnp.float32
bf16 = jnp.bfloat16
MXU_DTYPE = bf16

RMS_EPS = 1e-5
LN_EPS = 1e-5
D_MODEL = 1024
D_FF = 4096
CH = 128
N_BLK = 8
SSM_HEADS = 16
IN_EVEN = 5136
NP_IN = 5376
OFF_U, OFF_V, OFF_Z, OFF_X, OFF_DT = 0, 1024, 2048, 3072, 5120
XBC_BLKS = 16
QKV_DIM = 1280
ATT_SCALE = 64 ** -0.5

ADAM_LR = 0.001
ADAM_B1 = 0.9
ADAM_B2 = 0.999
ADAM_EPS = 1e-08
ADAM_WD = 0.01
ADAM_STEP = 10

VMEM_LIMIT_BYTES = 48 * 1024 * 1024
N_CHIPS = 4
SMALL_ROWS = 256

NN = ((1,), (0,))
NT = ((1,), (1,))
TN = ((0,), (0,))


def _mm(a, b, dims):
    return lax.dot_general(a.astype(MXU_DTYPE), b.astype(MXU_DTYPE), (dims, ((), ())),
                           preferred_element_type=f32)


def _mm_exact(a, b):
    return jnp.dot(a, b, preferred_element_type=f32, precision=lax.Precision.HIGHEST)


def _cparams(sem=None):
    return pltpu.CompilerParams(dimension_semantics=sem, vmem_limit_bytes=VMEM_LIMIT_BYTES)


@jax.custom_vjp
def _swap64(x):
    return pltpu.roll(x, 64, axis=1)


_swap64.defvjp(lambda x: (pltpu.roll(x, 64, axis=1), None), lambda _, g: (pltpu.roll(g, 64, axis=1),))


@jax.custom_vjp
def _top_rows(x):
    return x[:x.shape[0] // 2]


_top_rows.defvjp(lambda x: (x[:x.shape[0] // 2], None),
                 lambda _, g: (jnp.concatenate([g, jnp.zeros_like(g)], axis=0),))


@jax.custom_vjp
def _bottom_rows(x):
    return x[x.shape[0] // 2:]


_bottom_rows.defvjp(lambda x: (x[x.shape[0] // 2:], None),
                    lambda _, g: (jnp.concatenate([jnp.zeros_like(g), g], axis=0),))


def _make_delay(k):
    @jax.custom_vjp
    def delay(ext):
        return pltpu.roll(ext, k, axis=0)[8:, :]

    def fwd(ext):
        return delay(ext), None

    def bwd(_, g):
        gp = jnp.concatenate([jnp.zeros((8, g.shape[1]), g.dtype), g], axis=0)
        return (pltpu.roll(gp, gp.shape[0] - k, axis=0),)

    delay.defvjp(fwd, bwd)
    return delay


_DELAYS = {k: _make_delay(k) for k in (1, 2, 3)}


_GELU_C = 0.7978845608028654
_GELU_K = 0.044715


@jax.custom_vjp
def _gelu(x):
    return 0.5 * x * (1.0 + jnp.tanh(_GELU_C * (x + _GELU_K * (x * x * x))))


def _gelu_fwd(x):
    t = jnp.tanh(_GELU_C * (x + _GELU_K * (x * x * x)))
    return 0.5 * x * (1.0 + t), (x, t)


def _gelu_bwd(res, g):
    x, t = res
    dz = _GELU_C + (3.0 * _GELU_C * _GELU_K) * (x * x)
    return (g * (0.5 * (1.0 + t) + (0.5 * x) * (1.0 - t * t) * dz),)


_gelu.defvjp(_gelu_fwd, _gelu_bwd)


def _col(m, lane, h):
    return jnp.sum(jnp.where(lane == h, m, 0.0), axis=1, keepdims=True)


def _row(m, sub, h):
    return jnp.sum(jnp.where(sub == h, m, 0.0), axis=0, keepdims=True)


def _mixer_chunk(us, vs, zs, xbcs, halos, dtblk, hps, prm):
    lane = lax.broadcasted_iota(jnp.int32, (CH, CH), 1)
    sub = lax.broadcasted_iota(jnp.int32, (CH, CH), 0)
    left = lane < 64
    top = sub < 64
    causal = sub >= lane

    gus = [_gelu(u) for u in us]
    gvs = [_gelu(v) for v in vs]
    mu = sum(jnp.sum(g, axis=1, keepdims=True) for g in gvs) / D_MODEL
    cen = [g - mu for g in gvs]
    var = sum(jnp.sum(c * c, axis=1, keepdims=True) for c in cen) / D_MODEL
    rstd = lax.rsqrt(var + LN_EPS)
    a_out = []
    for g in range(N_BLK):
        vn = cen[g] * rstd * prm["ln_g"][g] + prm["ln_b"][g]
        w = jnp.where(causal, prm["wm"][g], 0.0)
        mixed = _mm(w, vn, NN) + _col(prm["bs_t"], lane, g)
        a_out.append(gus[g] * mixed)

    act = []
    for b in range(XBC_BLKS):
        w8 = prm["conv_w"][b]
        sub8 = lax.broadcasted_iota(jnp.int32, w8.shape, 0)
        ext = jnp.concatenate([halos[b], xbcs[b]], axis=0)
        conv = xbcs[b] * _row(w8, sub8, 3) + prm["conv_b"][b]
        for k in (1, 2, 3):
            conv = conv + _DELAYS[k](ext) * _row(w8, sub8, 3 - k)
        act.append(jax.nn.silu(conv))

    dt = jax.nn.softplus(dtblk + prm["dt_bias"])
    a_neg = -jnp.exp(prm["a_log"])
    tri = causal.astype(f32)
    acum = _mm_exact(tri, dt * a_neg)
    acum_t = acum.T
    dt_t = dt.T
    last = sub == CH - 1
    ys, h_out = [], []
    for grp in range(4):
        bm = act[8 + grp]
        cm = act[12 + grp]
        cb = _mm(cm, bm, NT)
        for p in (2 * grp, 2 * grp + 1):
            h0, h1 = 2 * p, 2 * p + 1
            xp = act[p]
            hp = hps[p]
            wis = []
            for h in (h0, h1):
                seg = _col(acum, lane, h) - _row(acum_t, sub, h)
                decay = jnp.exp(jnp.where(causal, seg, -jnp.inf))
                wis.append(cb * decay * _row(dt_t, sub, h))
            wcat = jnp.concatenate(wis, axis=1)
            xbd = jnp.concatenate([jnp.where(left, xp, 0.0), jnp.where(left, 0.0, xp)], axis=0)
            y_diag = _mm(wcat, xbd, NN)
            a_end = [jnp.sum(jnp.where(last & (lane == h), acum, 0.0), keepdims=True) for h in (h0, h1)]
            a_col = jnp.where(left, _col(acum, lane, h0), _col(acum, lane, h1))
            dt_col = jnp.where(left, _col(dt, lane, h0), _col(dt, lane, h1))
            to_end = jnp.exp(jnp.where(left, a_end[0], a_end[1]) - a_col) * dt_col
            states = _mm(xp * to_end, bm, TN)
            chunk_decay = jnp.where(top, jnp.exp(a_end[0]), jnp.exp(a_end[1]))
            h_out.append(chunk_decay * hp + states)
            y_off = jnp.exp(a_col) * _mm(cm, hp, NT)
            d_skip = jnp.where(left[:1], _col(prm["d_heads"], lane[:1], h0), _col(prm["d_heads"], lane[:1], h1))
            ys.append((y_diag + y_off + xp * d_skip) * jax.nn.silu(zs[p]))

    b_out = []
    for grp in range(4):
        pair = (ys[2 * grp], ys[2 * grp + 1])
        ms = sum(jnp.sum(y * y, axis=1, keepdims=True) for y in pair) / 256.0
        r = lax.rsqrt(ms + RMS_EPS)
        for j, y in enumerate(pair):
            b_out.append(y * r * prm["norm_g"][2 * grp + j])
    return a_out, b_out, h_out


def _attn_block(qps, kprev, kcur, vprev, vcur, sink_row, first):
    lane = lax.broadcasted_iota(jnp.int32, (CH, CH), 1)
    left = lane < 64
    row2 = lax.broadcasted_iota(jnp.int32, (2 * CH, CH), 0)
    key2 = lax.broadcasted_iota(jnp.int32, (2 * CH, CH), 1)
    upper = row2 < CH
    own = key2 <= jnp.where(upper, row2, row2 - CH)

    def both_halves(a):
        sw = _swap64(a)
        return [jnp.where(left, a, sw), jnp.where(left, sw, a)]

    kc, kp, vc, vp = both_halves(kcur), both_halves(kprev), both_halves(vcur), both_halves(vprev)
    outs = []
    for p in range(N_BLK):
        j = p // 4
        q2 = jnp.concatenate([jnp.where(left, qps[p], 0.0), jnp.where(left, 0.0, qps[p])], axis=0)
        s_prev = jnp.where(first, -jnp.inf, _mm(q2, kp[j], NT) * ATT_SCALE)
        s = jnp.where(own, _mm(q2, kc[j], NT) * ATT_SCALE, s_prev)
        sink = jnp.where(upper[:, :1], _col(sink_row, lane[:1], 2 * p), _col(sink_row, lane[:1], 2 * p + 1))
        m = lax.stop_gradient(jnp.maximum(jnp.max(s, axis=1, keepdims=True), sink))
        pexp = jnp.exp(s - m)
        probs = pexp / (jnp.sum(pexp, axis=1, keepdims=True) + jnp.exp(sink - m))
        o = _mm(jnp.where(own, probs, 0.0), vc[j], NN) + _mm(jnp.where(own, 0.0, probs), vp[j], NN)
        outs.append(jnp.where(left, _top_rows(o), _bottom_rows(o)))
    return outs


def _rmsnorm(x, g):
    r = lax.rsqrt(jnp.mean(x * x, axis=-1, keepdims=True) + RMS_EPS)
    return x * r * g


def rmsnorm_fwd(x, g_row, name):
    s, d = x.shape
    tm = min(512, s)

    def body(x_ref, g_ref, y_ref):
        y_ref[...] = _rmsnorm(x_ref[...], g_ref[...]).astype(bf16)

    return pl.pallas_call(
        body, name=name, grid=(s // tm,),
        in_specs=[pl.BlockSpec((tm, d), lambda i: (i, 0)), pl.BlockSpec((1, d), lambda i: (0, 0))],
        out_specs=pl.BlockSpec((tm, d), lambda i: (i, 0)),
        out_shape=jax.ShapeDtypeStruct((s, d), bf16),
        compiler_params=_cparams(("parallel",)),
    )(x, g_row)


def rmsnorm_bwd(x, g_row, dy, res, name):
    s, d = x.shape
    tm = min(512, s)

    def body(x_ref, g_ref, dy_ref, res_ref, dx_ref, dg_ref):
        @pl.when(pl.program_id(0) == 0)
        def _():
            dg_ref[...] = jnp.zeros_like(dg_ref)

        _, vjp = jax.vjp(_rmsnorm, x_ref[...], g_ref[...])
        dx, dg = vjp(dy_ref[...])
        dx_ref[...] = res_ref[...] + dx
        dg_ref[...] += dg

    tile = pl.BlockSpec((tm, d), lambda i: (i, 0))
    row = pl.BlockSpec((1, d), lambda i: (0, 0))
    return pl.pallas_call(
        body, name=name, grid=(s // tm,),
        in_specs=[tile, row, tile, tile], out_specs=[tile, row],
        out_shape=[jax.ShapeDtypeStruct((s, d), f32), jax.ShapeDtypeStruct((1, d), f32)],
        compiler_params=_cparams(("arbitrary",)),
    )(x, g_row, dy, res)


def final_loss(h, g_row, target, name):
    s, d = h.shape
    tm = min(512, s)

    def body(h_ref, g_ref, t_ref, loss_ref, dh_ref, dg_ref):
        @pl.when(pl.program_id(0) == 0)
        def _():
            dg_ref[...] = jnp.zeros_like(dg_ref)
            loss_ref[...] = jnp.zeros_like(loss_ref)

        def f(hv, gv):
            err = jnp.square(_rmsnorm(hv, gv) - t_ref[...])
            return 0.5 * jnp.sum(jnp.mean(err, axis=-1, keepdims=True), axis=0, keepdims=True)

        loss, vjp = jax.vjp(f, h_ref[...], g_ref[...])
        dh, dg = vjp(jnp.ones_like(loss))
        dh_ref[...] = dh
        dg_ref[...] += dg
        loss_ref[...] += jnp.broadcast_to(loss, loss_ref.shape)

    tile = pl.BlockSpec((tm, d), lambda i: (i, 0))
    row = pl.BlockSpec((1, d), lambda i: (0, 0))
    return pl.pallas_call(
        body, name=name, grid=(s // tm,),
        in_specs=[tile, row, tile],
        out_specs=[pl.BlockSpec((1, 128), lambda i: (0, 0)), tile, row],
        out_shape=[jax.ShapeDtypeStruct((1, 128), f32), jax.ShapeDtypeStruct((s, d), f32),
                   jax.ShapeDtypeStruct((1, d), f32)],
        compiler_params=_cparams(("arbitrary",)),
    )(h, g_row, target)


def colsum(x, name):
    s, n = x.shape
    tm = min(512, s)

    def body(x_ref, o_ref):
        @pl.when(pl.program_id(0) == 0)
        def _():
            o_ref[...] = jnp.zeros_like(o_ref)

        o_ref[...] += jnp.sum(x_ref[...].astype(f32), axis=0, keepdims=True)

    return pl.pallas_call(
        body, name=name, grid=(s // tm,),
        in_specs=[pl.BlockSpec((tm, n), lambda i: (i, 0))],
        out_specs=pl.BlockSpec((1, n), lambda i: (0, 0)),
        out_shape=jax.ShapeDtypeStruct((1, n), f32),
        compiler_params=_cparams(("arbitrary",)),
    )(x)


def _fit(dim, want):
    if dim <= want:
        return dim
    t = want
    while dim % t:
        t -= 128
    return t


def matmul(a, b, *, dims, name, out_dtype=f32, tm=1024, tn=512, tk=8192, a_pro=None, epi=None, epi_args=(),
           out_by_col_tile=False, after=None):
    if dims == "nn":
        (m, k), n = a.shape, b.shape[1]
    elif dims == "nt":
        (m, k), n = a.shape, b.shape[0]
    else:
        (k, m), n = a.shape, b.shape[1]
    tm, tn, tk = _fit(m, tm), _fit(n, tn), _fit(k, tk)
    nk = k // tk
    if dims == "nn":
        a_spec = pl.BlockSpec((tm, tk), lambda i, j, kk: (i, kk))
        b_spec = pl.BlockSpec((tk, tn), lambda i, j, kk: (kk, j))
        dn = NN
    elif dims == "nt":
        a_spec = pl.BlockSpec((tm, tk), lambda i, j, kk: (i, kk))
        b_spec = pl.BlockSpec((tn, tk), lambda i, j, kk: (j, kk))
        dn = NT
    else:
        a_spec = pl.BlockSpec((tk, tm), lambda i, j, kk: (kk, i))
        b_spec = pl.BlockSpec((tk, tn), lambda i, j, kk: (kk, j))
        dn = TN
    e_specs = [pl.BlockSpec((tm, tn), lambda i, j, kk: (i, j)) if kind == "tile"
               else pl.BlockSpec((1, tn), lambda i, j, kk: (0, j)) for kind, _ in epi_args]
    n_epi = len(epi_args)
    order_specs = [] if after is None else [pl.BlockSpec((8, 128), lambda i, j, kk: (0, 0))]
    order_args = [] if after is None else [after]

    def body(*refs):
        a_ref, b_ref = refs[0], refs[1]
        e_refs = refs[2:2 + n_epi]
        n_in = 2 + n_epi + len(order_args)
        o_ref = refs[n_in]
        av = a_ref[...]
        if a_pro is not None:
            av = a_pro(av)
        part = _mm(av, b_ref[...], dn)

        def finish(acc):
            if epi is not None:
                acc = epi(acc, *[r[...] for r in e_refs])
            o_ref[...] = acc.astype(out_dtype)

        if nk == 1:
            finish(part)
        else:
            acc_ref = refs[n_in + 1]
            kk = pl.program_id(2)

            @pl.when(kk == 0)
            def _():
                acc_ref[...] = part

            @pl.when(kk > 0)
            def _():
                acc_ref[...] += part

            @pl.when(kk == nk - 1)
            def _():
                finish(acc_ref[...])

    if out_by_col_tile:
        out_spec = pl.BlockSpec((None, tm, tn), lambda i, j, kk: (j, i, 0))
        out_shape = jax.ShapeDtypeStruct((n // tn, m, tn), out_dtype)
    else:
        out_spec = pl.BlockSpec((tm, tn), lambda i, j, kk: (i, j))
        out_shape = jax.ShapeDtypeStruct((m, n), out_dtype)
    return pl.pallas_call(
        body, name=name, grid=(m // tm, n // tn, nk),
        in_specs=[a_spec, b_spec] + e_specs + order_specs,
        out_specs=out_spec,
        out_shape=out_shape,
        scratch_shapes=[pltpu.VMEM((tm, tn), f32)] if nk > 1 else [],
        compiler_params=_cparams(("parallel", "parallel", "arbitrary")),
    )(a, b, *[arr for _, arr in epi_args], *order_args)


def _relu2(a):
    r = jnp.maximum(a.astype(f32), 0.0)
    return r * r


def _add(acc, t):
    return acc + t


def _add_bias(acc, t):
    return acc + t


def _add_bias_res(acc, bias, res):
    return acc + bias + res


def _times_relu2_grad(acc, a):
    return acc * (2.0 * jnp.maximum(a.astype(f32), 0.0))


_MIXER_PARAM_SHAPES = (
    ("ln_g", (1, D_MODEL)), ("ln_b", (1, D_MODEL)), ("wm", (N_BLK, CH, CH)), ("bs_t", (CH, CH)),
    ("conv_w", (8, 2048)), ("conv_b", (1, 2048)), ("dt_bias", (1, CH)), ("a_log", (1, CH)),
    ("d_heads", (1, CH)), ("norm_g", (1, D_MODEL)),
)


def _blocks(v, n, off=0):
    return [v[:, off + i * CH: off + (i + 1) * CH] for i in range(n)]


def _split_mixer_params(vals):
    p = dict(vals)
    return {
        "ln_g": _blocks(p["ln_g"], N_BLK), "ln_b": _blocks(p["ln_b"], N_BLK),
        "wm": [p["wm"][g] for g in range(N_BLK)], "bs_t": p["bs_t"],
        "conv_w": _blocks(p["conv_w"], XBC_BLKS), "conv_b": _blocks(p["conv_b"], XBC_BLKS),
        "dt_bias": p["dt_bias"], "a_log": p["a_log"], "d_heads": p["d_heads"],
        "norm_g": _blocks(p["norm_g"], N_BLK),
    }


def _mixer_leaves(proj_ref, halo_ref, keep_halo):
    pv = proj_ref
    us = [pv[:, OFF_U + i * CH: OFF_U + (i + 1) * CH] for i in range(N_BLK)]
    vs = [pv[:, OFF_V + i * CH: OFF_V + (i + 1) * CH] for i in range(N_BLK)]
    zs = [pv[:, OFF_Z + i * CH: OFF_Z + (i + 1) * CH] for i in range(N_BLK)]
    xbcs = [pv[:, OFF_X + i * CH: OFF_X + (i + 1) * CH] for i in range(XBC_BLKS)]
    halos = [halo_ref[:, OFF_X + i * CH: OFF_X + (i + 1) * CH] * keep_halo for i in range(XBC_BLKS)]
    dtblk = pv[:, OFF_DT: OFF_DT + CH]
    return us, vs, zs, xbcs, halos, dtblk


def mixer_fwd(proj, prm):
    s = proj.shape[0]
    nc = s // CH
    names = [n for n, _ in _MIXER_PARAM_SHAPES]

    def body(proj_ref, halo_ref, *rest):
        p_refs = rest[:len(names)]
        ab_ref, hs_ref, h_ref = rest[len(names):]
        c = pl.program_id(0)

        @pl.when(c == 0)
        def _():
            h_ref[...] = jnp.zeros_like(h_ref)

        hs_ref[...] = h_ref[...]
        keep = (c > 0).astype(f32)
        us, vs, zs, xbcs, halos, dtblk = _mixer_leaves(proj_ref, halo_ref, keep)
        hps = [h_ref[i * CH:(i + 1) * CH, :] for i in range(N_BLK)]
        p = _split_mixer_params({n: r[...] for n, r in zip(names, p_refs)})
        a_out, b_out, h_out = _mixer_chunk(us, vs, zs, xbcs, halos, dtblk, hps, p)
        for i in range(N_BLK):
            ab_ref[:, i * CH:(i + 1) * CH] = a_out[i].astype(bf16)
            ab_ref[:, D_MODEL + i * CH: D_MODEL + (i + 1) * CH] = b_out[i].astype(bf16)
            h_ref[i * CH:(i + 1) * CH, :] = h_out[i]

    def const(shape):
        return pl.BlockSpec(shape, lambda c: (0,) * len(shape))

    return pl.pallas_call(
        body, name="mixer_fwd", grid=(nc,),
        in_specs=[pl.BlockSpec((CH, NP_IN), lambda c: (c, 0)),
                  pl.BlockSpec((8, NP_IN), lambda c: (jnp.maximum(c * (CH // 8) - 1, 0), 0))]
                 + [const(shp) for _, shp in _MIXER_PARAM_SHAPES],
        out_specs=[pl.BlockSpec((CH, 2 * D_MODEL), lambda c: (c, 0)),
                   pl.BlockSpec((None, D_MODEL, CH), lambda c: (c, 0, 0))],
        out_shape=[jax.ShapeDtypeStruct((s, 2 * D_MODEL), bf16), jax.ShapeDtypeStruct((nc, D_MODEL, CH), f32)],
        scratch_shapes=[pltpu.VMEM((D_MODEL, CH), f32)],
        compiler_params=_cparams(("arbitrary",)),
    )(proj, proj, *[prm[n] for n in names])


def mixer_bwd(proj, hstates, dab, prm):
    s = proj.shape[0]
    nc = s // CH
    names = [n for n, _ in _MIXER_PARAM_SHAPES]
    npar = len(names)

    def body(proj_ref, halo_ref, hs_ref, dab_ref, *rest):
        p_refs = rest[:npar]
        dproj_ref = rest[npar]
        g_refs = rest[npar + 1: 2 * npar + 1]
        dh_ref, dhalo_ref = rest[2 * npar + 1:]
        i = pl.program_id(0)
        c = nc - 1 - i

        @pl.when(i == 0)
        def _():
            dh_ref[...] = jnp.zeros_like(dh_ref)
            dhalo_ref[...] = jnp.zeros_like(dhalo_ref)
            for r in g_refs:
                r[...] = jnp.zeros_like(r)

        keep = (c > 0).astype(f32)
        us, vs, zs, xbcs, halos, dtblk = _mixer_leaves(proj_ref, halo_ref, keep)
        hps = [hs_ref[j * CH:(j + 1) * CH, :] for j in range(N_BLK)]
        pvals = {n: r[...] for n, r in zip(names, p_refs)}

        def fn(us, vs, zs, xbcs, halos, dtblk, hps, pvals):
            return _mixer_chunk(us, vs, zs, xbcs, halos, dtblk, hps, _split_mixer_params(pvals))

        _, vjp = jax.vjp(fn, us, vs, zs, xbcs, halos, dtblk, hps, pvals)
        da = [dab_ref[:, j * CH:(j + 1) * CH].astype(f32) for j in range(N_BLK)]
        db = [dab_ref[:, D_MODEL + j * CH: D_MODEL + (j + 1) * CH].astype(f32) for j in range(N_BLK)]
        dh = [dh_ref[j * CH:(j + 1) * CH, :] for j in range(N_BLK)]
        dus, dvs, dzs, dxbcs, dhalos, ddt, dhps, dp = vjp((da, db, dh))

        for j in range(N_BLK):
            dproj_ref[:, OFF_U + j * CH: OFF_U + (j + 1) * CH] = dus[j].astype(bf16)
            dproj_ref[:, OFF_V + j * CH: OFF_V + (j + 1) * CH] = dvs[j].astype(bf16)
            dproj_ref[:, OFF_Z + j * CH: OFF_Z + (j + 1) * CH] = dzs[j].astype(bf16)
            dh_ref[j * CH:(j + 1) * CH, :] = dhps[j]
        zeros_top = jnp.zeros((CH - 8, CH), f32)
        for j in range(XBC_BLKS):
            late = jnp.concatenate([zeros_top, dhalo_ref[:, j * CH:(j + 1) * CH]], axis=0)
            dproj_ref[:, OFF_X + j * CH: OFF_X + (j + 1) * CH] = (dxbcs[j] + late).astype(bf16)
        for j in range(XBC_BLKS):
            dhalo_ref[:, j * CH:(j + 1) * CH] = dhalos[j] * keep
        lane = lax.broadcasted_iota(jnp.int32, (CH, CH), 1)
        dproj_ref[:, OFF_DT: OFF_DT + CH] = jnp.where(lane < SSM_HEADS, ddt, 0.0).astype(bf16)
        dproj_ref[:, OFF_DT + CH:] = jnp.zeros((CH, NP_IN - OFF_DT - CH), bf16)
        for n, r in zip(names, g_refs):
            r[...] += dp[n]

    def const(shape):
        return pl.BlockSpec(shape, lambda i: (0,) * len(shape))

    outs = pl.pallas_call(
        body, name="mixer_bwd", grid=(nc,),
        in_specs=[pl.BlockSpec((CH, NP_IN), lambda i: (nc - 1 - i, 0)),
                  pl.BlockSpec((8, NP_IN), lambda i: (jnp.maximum((nc - 1 - i) * (CH // 8) - 1, 0), 0)),
                  pl.BlockSpec((None, D_MODEL, CH), lambda i: (nc - 1 - i, 0, 0)),
                  pl.BlockSpec((CH, 2 * D_MODEL), lambda i: (nc - 1 - i, 0))]
                 + [const(shp) for _, shp in _MIXER_PARAM_SHAPES],
        out_specs=[pl.BlockSpec((CH, NP_IN), lambda i: (nc - 1 - i, 0))]
                  + [const(shp) for _, shp in _MIXER_PARAM_SHAPES],
        out_shape=[jax.ShapeDtypeStruct((s, NP_IN), bf16)]
                  + [jax.ShapeDtypeStruct(shp, f32) for _, shp in _MIXER_PARAM_SHAPES],
        scratch_shapes=[pltpu.VMEM((D_MODEL, CH), f32), pltpu.VMEM((8, 2048), f32)],
        compiler_params=_cparams(("arbitrary",)),
    )(proj, proj, hstates, dab, *[prm[n] for n in names])
    return outs[0], dict(zip(names, outs[1:]))


_K_BLK = D_MODEL // CH
_V_BLK = _K_BLK + 1


def _attn_specs(rev, nb):
    def blk(i):
        return nb - 1 - i if rev else i

    q_spec = pl.BlockSpec((CH, D_MODEL), lambda i: (blk(i), 0))
    kv = lambda col, prev: pl.BlockSpec(
        (CH, CH), lambda i: (jnp.maximum(blk(i) - 1, 0) if prev else blk(i), col))
    return q_spec, [kv(_K_BLK, True), kv(_K_BLK, False), kv(_V_BLK, True), kv(_V_BLK, False)]


def attn_fwd(qkv, sink_row):
    s = qkv.shape[0]
    nb = s // CH

    def body(q_ref, kp_ref, kc_ref, vp_ref, vc_ref, sink_ref, o_ref):
        qps = [q_ref[:, p * CH:(p + 1) * CH] for p in range(N_BLK)]
        outs = _attn_block(qps, kp_ref[...], kc_ref[...], vp_ref[...], vc_ref[...], sink_ref[...],
                           pl.program_id(0) == 0)
        for p in range(N_BLK):
            o_ref[:, p * CH:(p + 1) * CH] = outs[p].astype(bf16)

    q_spec, kv_specs = _attn_specs(False, nb)
    return pl.pallas_call(
        body, name="attn_fwd", grid=(nb,),
        in_specs=[q_spec] + kv_specs + [pl.BlockSpec((1, CH), lambda i: (0, 0))],
        out_specs=pl.BlockSpec((CH, D_MODEL), lambda i: (i, 0)),
        out_shape=jax.ShapeDtypeStruct((s, D_MODEL), bf16),
        compiler_params=_cparams(("parallel",)),
    )(qkv, qkv, qkv, qkv, qkv, sink_row)


def attn_bwd(qkv, sink_row, dout):
    s = qkv.shape[0]
    nb = s // CH

    def body(q_ref, kp_ref, kc_ref, vp_ref, vc_ref, sink_ref, do_ref, dqkv_ref, dsink_ref, carry_ref):
        i = pl.program_id(0)
        blk = nb - 1 - i

        @pl.when(i == 0)
        def _():
            dsink_ref[...] = jnp.zeros_like(dsink_ref)
            carry_ref[...] = jnp.zeros_like(carry_ref)

        qps = [q_ref[:, p * CH:(p + 1) * CH] for p in range(N_BLK)]
        first = blk == 0
        _, vjp = jax.vjp(lambda *a: _attn_block(*a, first), qps, kp_ref[...], kc_ref[...], vp_ref[...],
                         vc_ref[...], sink_ref[...])
        dos = [do_ref[:, p * CH:(p + 1) * CH].astype(f32) for p in range(N_BLK)]
        dqs, dkp, dkc, dvp, dvc, dsink = vjp(dos)
        for p in range(N_BLK):
            dqkv_ref[:, p * CH:(p + 1) * CH] = dqs[p].astype(bf16)
        dqkv_ref[:, D_MODEL: D_MODEL + CH] = (dkc + carry_ref[0]).astype(bf16)
        dqkv_ref[:, D_MODEL + CH:] = (dvc + carry_ref[1]).astype(bf16)
        keep = jnp.logical_not(first).astype(f32)
        carry_ref[0] = dkp * keep
        carry_ref[1] = dvp * keep
        dsink_ref[...] += dsink

    q_spec, kv_specs = _attn_specs(True, nb)
    return pl.pallas_call(
        body, name="attn_bwd", grid=(nb,),
        in_specs=[q_spec] + kv_specs + [pl.BlockSpec((1, CH), lambda i: (0, 0)),
                                        pl.BlockSpec((CH, D_MODEL), lambda i: (nb - 1 - i, 0))],
        out_specs=[pl.BlockSpec((CH, QKV_DIM), lambda i: (nb - 1 - i, 0)), pl.BlockSpec((1, CH), lambda i: (0, 0))],
        out_shape=[jax.ShapeDtypeStruct((s, QKV_DIM), bf16), jax.ShapeDtypeStruct((1, CH), f32)],
        scratch_shapes=[pltpu.VMEM((2, CH, CH), f32)],
        compiler_params=_cparams(("arbitrary",)),
    )(qkv, qkv, qkv, qkv, qkv, sink_row, dout)


def adamw(w, g, m, v, name):
    def body(w_ref, g_ref, m_ref, v_ref, d_ref, nm_ref, nv_ref):
        gv = g_ref[...]
        nm = ADAM_B1 * m_ref[...] + (1.0 - ADAM_B1) * gv
        nv = ADAM_B2 * v_ref[...] + (1.0 - ADAM_B2) * jnp.square(gv)
        m_hat = nm / (1.0 - ADAM_B1 ** ADAM_STEP)
        v_hat = nv / (1.0 - ADAM_B2 ** ADAM_STEP)
        d_ref[...] = -ADAM_LR * (m_hat / (jnp.sqrt(v_hat) + ADAM_EPS) + ADAM_WD * w_ref[...])
        nm_ref[...] = nm
        nv_ref[...] = nv

    out_shape = [jax.ShapeDtypeStruct(w.shape, f32)] * 3
    if w.ndim == 3 and w.shape[1] == 1:
        tr = max(t for t in range(1, 129) if w.shape[0] % t == 0)
        tile = pl.BlockSpec((tr, 1, w.shape[2]), lambda i: (i, 0, 0))
        return pl.pallas_call(
            body, name=name, grid=(w.shape[0] // tr,),
            in_specs=[tile] * 4, out_specs=[tile] * 3, out_shape=out_shape,
            compiler_params=_cparams(("parallel",)),
        )(w, g, m, v)
    if w.ndim == 3 and w.shape[1] % 256 == 0:
        tile = pl.BlockSpec((None, 256, w.shape[2]), lambda l, i: (l, i, 0))
        return pl.pallas_call(
            body, name=name, grid=(w.shape[0], w.shape[1] // 256),
            in_specs=[tile] * 4, out_specs=[tile] * 3, out_shape=out_shape,
            compiler_params=_cparams(("parallel", "parallel")),
        )(w, g, m, v)
    return pl.pallas_call(body, name=name, in_specs=[_VMEM] * 4, out_specs=[_VMEM] * 3, out_shape=out_shape,
                          compiler_params=_cparams())(w, g, m, v)


_MESH = pl.DeviceIdType.MESH
_ANY = pl.BlockSpec(memory_space=pl.ANY)
_VMEM = pl.BlockSpec(memory_space=pltpu.VMEM)


def _place():
    x, y, c = lax.axis_index("x"), lax.axis_index("y"), lax.axis_index("c")
    chips = [(1 - x, y), (x, 1 - y), (1 - x, 1 - y)]
    return x, y, c, 2 * x + y, chips, [2 * cx + cy for cx, cy in chips]


def _half(c, rows):
    return pl.ds(pl.multiple_of(c * (rows // 2), 16), rows // 2)


def _step_rows(rows):
    return max(t for t in range(16, 641, 16) if rows % t == 0)


def place_shard(b, slot, name):
    r, c = b.shape
    tr = _step_rows(r)

    def body(slot_ref, b_ref, o_ref):
        o_ref[...] = b_ref[...]

    return pl.pallas_call(
        body, name=name,
        grid_spec=pltpu.PrefetchScalarGridSpec(
            num_scalar_prefetch=1, grid=(r // tr,),
            in_specs=[pl.BlockSpec((tr, c), lambda i, s: (i, 0))],
            out_specs=pl.BlockSpec((None, tr, c), lambda i, s: (s[0], i, 0))),
        out_shape=jax.ShapeDtypeStruct((N_CHIPS, r, c), b.dtype),
        compiler_params=_cparams(("parallel",)),
    )(slot, b)


_HBM = pl.BlockSpec(memory_space=pltpu.HBM)
_SEM = pl.BlockSpec(memory_space=pltpu.SEMAPHORE)
_EFFECT = pltpu.SideEffectType.DATAFLOW_SIDE_EFFECTING


def _gather_ici_copies(bufs, send_sems, recv_sems):
    x, y, c, me, chips, chip_idx = _place()
    return [pltpu.make_async_remote_copy(
        src_ref=buf.at[me, _half(c, buf.shape[1])], dst_ref=buf.at[chip_idx[j], _half(c, buf.shape[1])],
        send_sem=send_sems.at[3 * k + j], recv_sem=recv_sems.at[3 * k + j],
        device_id=(*chips[j], c), device_id_type=_MESH) for j in range(3) for k, buf in enumerate(bufs)]


def gather_start(groups):
    sizes = [len(g) for g in groups]
    flat = [b for g in groups for b in g]
    n = len(flat)

    def body(*refs):
        bufs, sems = refs[:n], refs[n:n + 2 * len(groups)]
        x, y, c, me, chips, chip_idx = _place()
        lo = 0
        for gi, size in enumerate(sizes):
            for j in range(3):
                for k, buf in enumerate(bufs[lo:lo + size]):
                    mine = buf.at[me, _half(c, buf.shape[1])]
                    pltpu.make_async_remote_copy(
                        src_ref=mine, dst_ref=mine, send_sem=sems[2 * gi].at[3 * k + j],
                        recv_sem=sems[2 * gi + 1].at[3 * k + j], device_id=(*chips[j], c),
                        device_id_type=_MESH).start()
            lo += size

    sem_shapes = [pltpu.SemaphoreType.DMA((3 * size,)) for size in sizes for _ in range(2)]
    outs = pl.pallas_call(
        body, name="gather_start",
        out_shape=(*sem_shapes, *[pltpu.HBM(b.shape, b.dtype) for b in flat]),
        in_specs=[_HBM] * n, out_specs=(*[_SEM] * len(sem_shapes), *[_HBM] * n),
        input_output_aliases={i: len(sem_shapes) + i for i in range(n)},
        compiler_params=pltpu.CompilerParams(has_side_effects=_EFFECT),
    )(*[pltpu.with_memory_space_constraint(b, pltpu.HBM) for b in flat])
    sems = [(outs[2 * gi], outs[2 * gi + 1]) for gi in range(len(groups))]
    thru, lo = [], len(sem_shapes)
    for size in sizes:
        thru.append(list(outs[lo:lo + size]))
        lo += size
    return sems, thru


def gather_wait(bufs, sems, after, tag):
    n = len(bufs)

    def body(*refs):
        for cp in _gather_ici_copies(refs[:n], refs[n], refs[n + 1]):
            cp.wait_send()
            cp.wait_recv()

    extra = [] if after is None else [after]
    return list(pl.pallas_call(
        body, name=f"gather_wait_{tag}",
        out_shape=[pltpu.HBM(b.shape, b.dtype) for b in bufs],
        in_specs=[_HBM] * n + [_SEM, _SEM] + [_ANY] * len(extra), out_specs=[_HBM] * n,
        input_output_aliases={i: i for i in range(n)},
        compiler_params=pltpu.CompilerParams(has_side_effects=_EFFECT),
    )(*bufs, *sems, *extra))


def gather_forward(bufs, tag):
    n = len(bufs)

    def body(*refs):
        out_refs = refs[n:2 * n]
        send_sems, recv_sems = refs[2 * n:]
        x, y, c, me, chips, chip_idx = _place()

        def copy(k, j, half):
            part = out_refs[k].at[chip_idx[j], _half(half, out_refs[k].shape[1])]
            return pltpu.make_async_remote_copy(
                src_ref=part, dst_ref=part, send_sem=send_sems.at[3 * k + j], recv_sem=recv_sems.at[3 * k + j],
                device_id=(x, y, 1 - c), device_id_type=_MESH)

        sends = [copy(k, j, c) for j in range(3) for k in range(n)]
        for cp in sends:
            cp.start()
        for j in range(3):
            for k in range(n):
                copy(k, j, 1 - c).wait_recv()
        for cp in sends:
            cp.wait_send()

    return list(pl.pallas_call(
        body, name=f"gather_forward_{tag}",
        out_shape=[jax.ShapeDtypeStruct(b.shape, b.dtype) for b in bufs],
        in_specs=[_ANY] * n, out_specs=[_ANY] * n, input_output_aliases={i: i for i in range(n)},
        scratch_shapes=[pltpu.SemaphoreType.DMA((3 * n,)), pltpu.SemaphoreType.DMA((3 * n,))],
    )(*bufs))


def exchange_halves(bufs, tag):
    n = len(bufs)

    def body(*refs):
        g_refs, out_refs = refs[:n], refs[n:2 * n]
        send_sems, recv_sems = refs[2 * n:]
        x, y, c, *_ = _place()
        cps = [pltpu.make_async_remote_copy(
            src_ref=g_refs[b].at[:, _half(1 - c, g_refs[b].shape[1])], dst_ref=out_refs[b],
            send_sem=send_sems.at[b], recv_sem=recv_sems.at[b], device_id=(x, y, 1 - c), device_id_type=_MESH)
            for b in range(n)]
        for cp in cps:
            cp.start()
        for cp in cps:
            cp.wait()

    return pl.pallas_call(
        body, name=f"exchange_halves_{tag}",
        out_shape=[jax.ShapeDtypeStruct((N_CHIPS, b.shape[1] // 2, b.shape[2]), b.dtype) for b in bufs],
        in_specs=[_ANY] * n, out_specs=[_ANY] * n,
        scratch_shapes=[pltpu.SemaphoreType.DMA((n,)), pltpu.SemaphoreType.DMA((n,))],
    )(*bufs)


def add_halves(g, got, c_idx, name):
    hr, cols = got.shape[1], got.shape[2]
    tr = _step_rows(hr)
    steps = hr // tr

    def body(c_ref, g_ref, got_ref, o_ref):
        o_ref[...] = (g_ref[...].astype(f32) + got_ref[...].astype(f32)).astype(bf16)

    return pl.pallas_call(
        body, name=name,
        grid_spec=pltpu.PrefetchScalarGridSpec(
            num_scalar_prefetch=1, grid=(N_CHIPS, steps),
            in_specs=[pl.BlockSpec((None, tr, cols), lambda s, i, c: (s, c[0] * steps + i, 0)),
                      pl.BlockSpec((None, tr, cols), lambda s, i, c: (s, i, 0))],
            out_specs=pl.BlockSpec((None, tr, cols), lambda s, i, c: (s, i, 0))),
        out_shape=jax.ShapeDtypeStruct(got.shape, bf16),
        compiler_params=_cparams(("parallel", "parallel")),
    )(c_idx, g, got)


def sum_chips(t, got, place_idx, name):
    hr, cols = t.shape[1], t.shape[2]
    tr = _step_rows(hr)
    steps = hr // tr

    def body(idx_ref, t_ref, got_ref, o_ref):
        acc = t_ref[...].astype(f32)
        for j in range(3):
            acc = acc + got_ref[j].astype(f32)
        o_ref[...] = acc

    return pl.pallas_call(
        body, name=name,
        grid_spec=pltpu.PrefetchScalarGridSpec(
            num_scalar_prefetch=1, grid=(steps,),
            in_specs=[pl.BlockSpec((None, tr, cols), lambda i, idx: (idx[0], i, 0)),
                      pl.BlockSpec((3, tr, cols), lambda i, idx: (0, i, 0))],
            out_specs=pl.BlockSpec((tr, cols), lambda i, idx: (idx[1] * steps + i, 0))),
        out_shape=jax.ShapeDtypeStruct((2 * hr, cols), f32),
        compiler_params=_cparams(("parallel",)),
    )(place_idx, t, got)


def share_halves(bufs, tag):
    n = len(bufs)

    def body(*refs):
        out_refs = refs[n:2 * n]
        send_sems, recv_sems = refs[2 * n:]
        x, y, c, *_ = _place()

        def copy(b, half):
            part = out_refs[b].at[_half(half, out_refs[b].shape[0])]
            return pltpu.make_async_remote_copy(
                src_ref=part, dst_ref=part, send_sem=send_sems.at[b], recv_sem=recv_sems.at[b],
                device_id=(x, y, 1 - c), device_id_type=_MESH)

        for b in range(n):
            copy(b, c).start()
        for b in range(n):
            copy(b, 1 - c).wait_recv()
        for b in range(n):
            copy(b, c).wait_send()

    return pl.pallas_call(
        body, name=f"share_halves_{tag}",
        out_shape=[jax.ShapeDtypeStruct(b.shape, b.dtype) for b in bufs],
        in_specs=[_ANY] * n, out_specs=[_ANY] * n, input_output_aliases={i: i for i in range(n)},
        scratch_shapes=[pltpu.SemaphoreType.DMA((n,)), pltpu.SemaphoreType.DMA((n,))],
    )(*bufs)


def _share_copies(refs, send_sems, recv_sems):
    x, y, c, *_ = _place()
    return [pltpu.make_async_remote_copy(
        src_ref=ref.at[_half(c, ref.shape[0])], dst_ref=ref.at[_half(c, ref.shape[0])], send_sem=send_sems.at[b],
        recv_sem=recv_sems.at[b], device_id=(x, y, 1 - c), device_id_type=_MESH) for b, ref in enumerate(refs)]


def share_start(bufs, tag):
    n = len(bufs)

    def body(*refs):
        for cp in _share_copies(refs[:n], refs[n], refs[n + 1]):
            cp.start()

    outs = pl.pallas_call(
        body, name=f"share_start_{tag}",
        out_shape=(pltpu.SemaphoreType.DMA((n,)), pltpu.SemaphoreType.DMA((n,)),
                   *[pltpu.HBM(b.shape, b.dtype) for b in bufs]),
        in_specs=[_HBM] * n, out_specs=(_SEM, _SEM, *[_HBM] * n),
        input_output_aliases={i: 2 + i for i in range(n)},
        compiler_params=pltpu.CompilerParams(has_side_effects=_EFFECT),
    )(*[pltpu.with_memory_space_constraint(b, pltpu.HBM) for b in bufs])
    return outs[0], outs[1], list(outs[2:])


def share_wait(send_sems, recv_sems, bufs, after, tag):
    n = len(bufs)

    def body(*refs):
        x, y, c, *_ = _place()
        for b, ref in enumerate(refs[:n]):
            cp = pltpu.make_async_remote_copy(
                src_ref=ref.at[_half(c, ref.shape[0])], dst_ref=ref.at[_half(1 - c, ref.shape[0])],
                send_sem=refs[n].at[b], recv_sem=refs[n + 1].at[b], device_id=(x, y, 1 - c), device_id_type=_MESH)
            cp.wait_send()
            cp.wait_recv()

    return list(pl.pallas_call(
        body, name=f"share_wait_{tag}",
        out_shape=[pltpu.HBM(b.shape, b.dtype) for b in bufs],
        in_specs=[_HBM] * n + [_SEM, _SEM, _ANY], out_specs=[_HBM] * n,
        input_output_aliases={i: i for i in range(n)},
        compiler_params=pltpu.CompilerParams(has_side_effects=_EFFECT),
    )(*bufs, send_sems, recv_sems, after))


def _scatter_copies(t_refs, land_refs, send_sems, recv_sems):
    x, y, c, me, chips, chip_idx = _place()
    return [pltpu.make_async_remote_copy(
        src_ref=t_refs[b].at[chip_idx[j]], dst_ref=land_refs[b].at[j], send_sem=send_sems.at[3 * b + j],
        recv_sem=recv_sems.at[3 * b + j], device_id=(*chips[j], c), device_id_type=_MESH)
        for j in range(3) for b in range(len(t_refs))]


def scatter_start(ts, tag):
    n = len(ts)
    lands = [lax.empty((3,) + t.shape[1:], t.dtype) for t in ts]

    def body(*refs):
        for cp in _scatter_copies(refs[:n], refs[n:2 * n], refs[2 * n], refs[2 * n + 1]):
            cp.start()
        token = refs[-1]
        token[...] = jnp.zeros_like(token)

    hbm = [pltpu.HBM(a.shape, a.dtype) for a in (*ts, *lands)]
    outs = pl.pallas_call(
        body, name=f"scatter_start_{tag}",
        out_shape=(pltpu.SemaphoreType.DMA((3 * n,)), pltpu.SemaphoreType.DMA((3 * n,)), *hbm,
                   jax.ShapeDtypeStruct((8, 128), f32)),
        in_specs=[_HBM] * (2 * n), out_specs=(_SEM, _SEM, *[_HBM] * (2 * n), _VMEM),
        input_output_aliases={i: 2 + i for i in range(2 * n)},
        compiler_params=pltpu.CompilerParams(has_side_effects=_EFFECT),
    )(*[pltpu.with_memory_space_constraint(a, pltpu.HBM) for a in (*ts, *lands)])
    return outs[0], outs[1], list(outs[2:2 + n]), list(outs[2 + n:2 + 2 * n]), outs[-1]


def scatter_wait(send_sems, recv_sems, ts, lands, after, tag):
    n = len(ts)

    def body(*refs):
        for cp in _scatter_copies(refs[:n], refs[n:2 * n], refs[2 * n], refs[2 * n + 1]):
            cp.wait_send()
            cp.wait_recv()

    outs = pl.pallas_call(
        body, name=f"scatter_wait_{tag}",
        out_shape=[pltpu.HBM(a.shape, a.dtype) for a in (*ts, *lands)],
        in_specs=[_HBM] * (2 * n) + [_SEM, _SEM, _ANY], out_specs=[_HBM] * (2 * n),
        input_output_aliases={i: i for i in range(2 * n)},
        compiler_params=pltpu.CompilerParams(has_side_effects=_EFFECT),
    )(*ts, *lands, send_sems, recv_sems, after)
    return list(outs[:n]), list(outs[n:])


N_SENDERS = 7


def _direct_copies(g_refs, land_refs, send_sems, recv_sems):
    x, y, c, me, chips, chip_idx = _place()
    cps = []
    for b, (g, land) in enumerate(zip(g_refs, land_refs)):
        rows, base = g.shape[1], N_SENDERS * b
        cps.append(pltpu.make_async_remote_copy(
            src_ref=g.at[me, _half(1 - c, rows)], dst_ref=land.at[0], send_sem=send_sems.at[base],
            recv_sem=recv_sems.at[base], device_id=(x, y, 1 - c), device_id_type=_MESH))
        for j in range(3):
            for core in range(2):
                cps.append(pltpu.make_async_remote_copy(
                    src_ref=g.at[chip_idx[j], _half(core, rows)], dst_ref=land.at[1 + 2 * j + c],
                    send_sem=send_sems.at[base + 1 + 2 * j + core], recv_sem=recv_sems.at[base + 1 + 2 * j + c],
                    device_id=(*chips[j], core), device_id_type=_MESH))
    return cps


def direct_start(gs, tag):
    n = len(gs)
    lands = [lax.empty((N_SENDERS, g.shape[1] // 2, g.shape[2]), g.dtype) for g in gs]

    def body(*refs):
        for cp in _direct_copies(refs[:n], refs[n:2 * n], refs[2 * n], refs[2 * n + 1]):
            cp.start()
        token = refs[-1]
        token[...] = jnp.zeros_like(token)

    hbm = [pltpu.HBM(a.shape, a.dtype) for a in (*gs, *lands)]
    outs = pl.pallas_call(
        body, name=f"direct_start_{tag}",
        out_shape=(pltpu.SemaphoreType.DMA((N_SENDERS * n,)), pltpu.SemaphoreType.DMA((N_SENDERS * n,)), *hbm,
                   jax.ShapeDtypeStruct((8, 128), f32)),
        in_specs=[_HBM] * (2 * n), out_specs=(_SEM, _SEM, *[_HBM] * (2 * n), _VMEM),
        input_output_aliases={i: 2 + i for i in range(2 * n)},
        compiler_params=pltpu.CompilerParams(has_side_effects=_EFFECT),
    )(*[pltpu.with_memory_space_constraint(a, pltpu.HBM) for a in (*gs, *lands)])
    return outs[0], outs[1], list(outs[2:2 + n]), list(outs[2 + n:2 + 2 * n]), outs[-1]


def direct_wait(send_sems, recv_sems, gs, lands, after, tag):
    n = len(gs)

    def body(*refs):
        g_refs, land_refs, sends, recvs = refs[:n], refs[n:2 * n], refs[2 * n], refs[2 * n + 1]
        for b in range(n):
            for k in range(N_SENDERS):
                cp = pltpu.make_async_remote_copy(
                    src_ref=g_refs[b].at[0, _half(0, g_refs[b].shape[1])], dst_ref=land_refs[b].at[k],
                    send_sem=sends.at[N_SENDERS * b + k], recv_sem=recvs.at[N_SENDERS * b + k],
                    device_id=_place()[:3], device_id_type=_MESH)
                cp.wait_send()
                cp.wait_recv()

    outs = pl.pallas_call(
        body, name=f"direct_wait_{tag}",
        out_shape=[pltpu.HBM(a.shape, a.dtype) for a in (*gs, *lands)],
        in_specs=[_HBM] * (2 * n) + [_SEM, _SEM, _ANY], out_specs=[_HBM] * (2 * n),
        input_output_aliases={i: i for i in range(2 * n)},
        compiler_params=pltpu.CompilerParams(has_side_effects=_EFFECT),
    )(*gs, *lands, send_sems, recv_sems, after)
    return list(outs[:n]), list(outs[n:])


def sum_senders(g, lands, place_idx, name):
    hr, cols = lands.shape[1], lands.shape[2]
    tr = _step_rows(hr)
    steps = hr // tr

    def body(idx_ref, g_ref, land_ref, o_ref):
        acc = g_ref[...].astype(f32)
        for k in range(N_SENDERS):
            acc = acc + land_ref[k].astype(f32)
        o_ref[...] = acc

    return pl.pallas_call(
        body, name=name,
        grid_spec=pltpu.PrefetchScalarGridSpec(
            num_scalar_prefetch=1, grid=(steps,),
            in_specs=[pl.BlockSpec((None, tr, cols), lambda i, idx: (idx[0], idx[1] * steps + i, 0)),
                      pl.BlockSpec((N_SENDERS, tr, cols), lambda i, idx: (0, i, 0))],
            out_specs=pl.BlockSpec((tr, cols), lambda i, idx: (idx[1] * steps + i, 0))),
        out_shape=jax.ShapeDtypeStruct((2 * hr, cols), f32),
        compiler_params=_cparams(("parallel",)),
    )(place_idx, g, lands)


class GradReducer:
    def __init__(self, c_idx, place_idx):
        self.c_idx, self.place_idx = c_idx, place_idx

    def start(self, bufs, tag, direct=False):
        if direct:
            send_sems, recv_sems, gs, lands, token = direct_start(bufs, tag)
            return (True, send_sems, recv_sems, gs, lands), token
        got = exchange_halves(bufs, tag)
        ts = [add_halves(b, g, self.c_idx, f"add_halves_{tag}{i}") for i, (b, g) in enumerate(zip(bufs, got))]
        send_sems, recv_sems, ts, lands, token = scatter_start(ts, tag)
        return (False, send_sems, recv_sems, ts, lands), token

    def finish(self, state, after, tag):
        direct, *flight = state
        if direct:
            gs, lands = direct_wait(*flight, after, tag)
            sums = [sum_senders(g, l, self.place_idx, f"sum_senders_{tag}{i}") for i, (g, l) in enumerate(zip(gs, lands))]
        else:
            ts, lands = scatter_wait(*flight, after, tag)
            sums = [sum_chips(t, l, self.place_idx, f"sum_chips_{tag}{i}") for i, (t, l) in enumerate(zip(ts, lands))]
        return share_start(sums, tag)

    def collect(self, pending, after, tag):
        return share_wait(*pending, after, tag)


def allreduce_small(sp):
    def body(s_ref, out_ref, gather_ref, send_sems, recv_sems):
        x, y, c, me, chips, chip_idx = _place()
        sibling = (x, y, 1 - c)

        def copy(k, chip, core, to, src=None):
            dst = gather_ref.at[2 * chip + core]
            return pltpu.make_async_remote_copy(
                src_ref=dst if src is None else src, dst_ref=dst, send_sem=send_sems.at[k],
                recv_sem=recv_sems.at[k], device_id=to, device_id_type=_MESH)

        first = [copy(0, me, c, sibling, src=s_ref)]
        first += [copy(1 + j, me, c, (*chips[j], c), src=s_ref) for j in range(3)]
        for cp in first:
            cp.start()
        gather_ref[2 * me + c] = s_ref[...]
        passed = [copy(4 + j, chip_idx[j], c, sibling) for j in range(3)]
        for j in range(3):
            copy(1 + j, chip_idx[j], c, sibling).wait_recv()
            passed[j].start()
        copy(0, me, 1 - c, sibling).wait_recv()
        for j in range(3):
            copy(4 + j, chip_idx[j], 1 - c, sibling).wait_recv()
        for cp in first + passed:
            cp.wait_send()
        acc = gather_ref[0]
        for d in range(1, 2 * N_CHIPS):
            acc = acc + gather_ref[d]
        out_ref[...] = acc

    return pl.pallas_call(
        body, name="allreduce_small",
        out_shape=jax.ShapeDtypeStruct(sp.shape, sp.dtype),
        in_specs=[_VMEM], out_specs=_VMEM,
        scratch_shapes=[pltpu.VMEM((2 * N_CHIPS,) + sp.shape, sp.dtype),
                        pltpu.SemaphoreType.DMA((7,)), pltpu.SemaphoreType.DMA((7,))],
        compiler_params=_cparams(),
    )(sp)


def _n_rows(shape):
    n = 1
    for d in shape:
        n *= d
    return 8 * (-(-n // 8192))


def _pack(arrays, total_rows):
    parts = []
    for a in arrays:
        flat = a.reshape(-1)
        parts.append(jnp.pad(flat, (0, 1024 * _n_rows(a.shape) - flat.shape[0])).reshape(-1, 1024))
    rows = jnp.concatenate(parts, axis=0)
    return jnp.pad(rows, ((0, total_rows - rows.shape[0]), (0, 0)))


def _unpack(packed, shapes):
    out, r = [], 0
    for shp in shapes:
        n = 1
        for d in shp:
            n *= d
        nr = _n_rows(shp)
        out.append(packed[r:r + nr].reshape(-1)[:n].reshape(shp))
        r += nr
    return out


_COLUMN_SHARDED = ("w_in_even", "w_qkv")
IN_SHARD, IN_PAD = 1284, 1408
QKV_SHARD, QKV_PAD = 320, 384


def _lane_padded(a, cols):
    return jnp.pad(a, ((0, 0), (0, cols - a.shape[1])))


_SMALL_SHAPES = (
    ("norm_mix_g", (2, 1024)), ("norm_mlp_g", (2, 1024)), ("final_norm_g", (1024,)), ("gm_ln_g", (1, 1024)),
    ("gm_ln_b", (1, 1024)), ("gm_w_s", (1, 8, 128, 128)), ("gm_b_s", (1, 8, 128)), ("ssm_conv_b", (1, 2048)),
    ("ssm_dt_bias", (1, 16)), ("ssm_a_log", (1, 16)), ("ssm_d", (1, 16)), ("ssm_norm_g", (1, 1024)),
    ("attn_sinks", (1, 16)), ("ssm_conv_w", (1, 4, 2048)), ("b_qkv", (1, 1280)), ("b_o", (1, 1024)),
)
_N_REPLICATED = 13
_SHARDED_SMALL = (("ssm_conv_w", 2, 512), ("b_qkv", 1, 320), ("b_o", 1, 256))
_SHARD_PACK_ROWS = 32


def _cols_by_owner(a):
    return a.transpose(1, 0, 2).reshape(a.shape[1], -1)


class WeightGatherer:
    def __init__(self, w, chip_idx):
        rows = lambda *parts: jnp.concatenate(parts, axis=0).astype(bf16)
        shards = [
            ("in", _lane_padded(w["w_in_even"][0], IN_PAD).astype(bf16)),
            ("l0", rows(w["w_out_even"][0], w["w_up"][0], w["w_down"][0])),
            ("l1", rows(w["w_o"][0], w["w_up"][1], w["w_down"][1])),
            ("qkv", _lane_padded(w["w_qkv"][0], QKV_PAD).astype(bf16)),
        ]
        shards.append(("small", _pack([w[n] for n, _, _ in _SHARDED_SMALL], _SHARD_PACK_ROWS)))
        placed = [place_shard(b, chip_idx, f"place_shard_{tag}") for tag, b in shards]
        self.sems, self.bufs = gather_start([[placed[0], placed[4]], placed[1:2], placed[2:4]])

    def _group(self, gi, after, tag):
        return gather_forward(gather_wait(self.bufs[gi], self.sems[gi], after, tag), tag)

    def mixer_in(self):
        g, small = self._group(0, None, "in")
        shard_shapes = [tuple(width if i == axis else d for i, d in enumerate(dict(_SMALL_SHAPES)[n]))
                        for n, axis, width in _SHARDED_SMALL]
        per_chip = [_unpack(small[s], shard_shapes) for s in range(N_CHIPS)]
        full = {n: jnp.concatenate([per_chip[s][i] for s in range(N_CHIPS)], axis=axis)
                for i, (n, axis, _) in enumerate(_SHARDED_SMALL)}
        return _lane_padded(_cols_by_owner(g[:, :, :IN_SHARD]), NP_IN), full

    def layer0(self, after):
        (g,) = self._group(1, after, "l0")
        return g[:, :512].reshape(2048, 1024), _cols_by_owner(g[:, 512:1536]), g[:, 1536:].reshape(4096, 1024)

    def layer1(self, after):
        g, q = self._group(2, after, "l1")
        return (_cols_by_owner(q[:, :, :QKV_SHARD]), g[:, :256].reshape(1024, 1024), _cols_by_owner(g[:, 256:1280]),
                g[:, 1280:].reshape(4096, 1024))


def _row2(v):
    return v.reshape(1, -1)


def _lane_pad(v):
    return jnp.pad(v, ((0, 0), (0, CH - v.shape[1])))


def _mlp_fwd(h, g_row, w_up, w_down, tag):
    y = rmsnorm_fwd(h, g_row, f"mlp_norm{tag}")
    a = matmul(y, w_up, dims="nn", name=f"mlp_up{tag}", out_dtype=bf16, tn=1024)
    out = matmul(a, w_down, dims="nn", name=f"mlp_down{tag}", a_pro=_relu2, epi=_add, epi_args=(("tile", h),))
    return out, y, a


def _mlp_bwd(dh_out, h, g_row, y, a, w_up, w_down, tag, after=None):
    da = matmul(dh_out, w_down, dims="nt", name=f"mlp_da{tag}", out_dtype=bf16, tn=1024,
                epi=_times_relu2_grad, epi_args=(("tile", a),), after=after)
    dw_down = matmul(a, dh_out, dims="tn", name=f"mlp_dwdown{tag}", out_dtype=bf16, a_pro=_relu2)
    dw_up = matmul(y, da, dims="tn", name=f"mlp_dwup{tag}", out_dtype=bf16, tn=1024, out_by_col_tile=True)
    dy = matmul(da, w_up, dims="nt", name=f"mlp_dy{tag}")
    dh, dg = rmsnorm_bwd(h, g_row, dy, dh_out, f"mlp_dnorm{tag}")
    return dh, dg, dw_up, dw_down


def _by_owner(a):
    return a.reshape(N_CHIPS, a.shape[0] // N_CHIPS, a.shape[1])


def _col_shards(a, shard, padded):
    return jnp.stack([_lane_padded(a[:, shard * s: shard * (s + 1)], padded) for s in range(N_CHIPS)])


def _local_step(x, target, weights, sm, reducer):
    w_up, w_down = [None, None], [None, None]
    w_in_p, sharded_small = weights.mixer_in()
    sm = {**sm, **sharded_small}
    mix_g = [_row2(sm["norm_mix_g"][i]) for i in range(2)]
    mlp_g = [_row2(sm["norm_mlp_g"][i]) for i in range(2)]
    mixer_prm = {
        "ln_g": sm["gm_ln_g"], "ln_b": sm["gm_ln_b"], "wm": sm["gm_w_s"][0],
        "bs_t": jnp.pad(sm["gm_b_s"][0].T, ((0, 0), (0, CH - N_BLK))),
        "conv_w": jnp.pad(sm["ssm_conv_w"][0], ((0, 4), (0, 0))), "conv_b": sm["ssm_conv_b"],
        "dt_bias": _lane_pad(sm["ssm_dt_bias"]), "a_log": _lane_pad(sm["ssm_a_log"]),
        "d_heads": _lane_pad(sm["ssm_d"]), "norm_g": sm["ssm_norm_g"],
    }
    sink_row = _lane_pad(sm["attn_sinks"])

    y0 = rmsnorm_fwd(x, mix_g[0], "mix_norm0")
    proj = matmul(y0, w_in_p, dims="nn", name="in_proj", tn=768)
    ab, hstates = mixer_fwd(proj, mixer_prm)
    w_out, w_up[0], w_down[0] = weights.layer0(ab)
    h1 = matmul(ab, w_out, dims="nn", name="out_proj", epi=_add, epi_args=(("tile", x),))
    h2, y1, a1 = _mlp_fwd(h1, mlp_g[0], w_up[0], w_down[0], 0)
    w_qkv, w_o, w_up[1], w_down[1] = weights.layer1(h2)
    y2 = rmsnorm_fwd(h2, mix_g[1], "mix_norm1")
    qkv = matmul(y2, w_qkv, dims="nn", name="qkv_proj", tn=QKV_DIM, epi=_add_bias, epi_args=(("row", sm["b_qkv"]),))
    att = attn_fwd(qkv, sink_row)
    h3 = matmul(att, w_o, dims="nn", name="o_proj", epi=_add_bias_res,
                epi_args=(("row", sm["b_o"]), ("tile", h2)))
    h4, y3, a3 = _mlp_fwd(h3, mlp_g[1], w_up[1], w_down[1], 1)
    loss, dh4, dg_final = final_loss(h4, _row2(sm["final_norm_g"]), target, "final_loss")

    dh3, dg_mlp1, dw_up1, dw_down1 = _mlp_bwd(dh4, h3, mlp_g[1], y3, a3, w_up[1], w_down[1], 1)
    db_o = colsum(dh3, "db_o")
    datt = matmul(dh3, w_o, dims="nt", name="attn_dout", out_dtype=bf16)
    dw_o = matmul(att, dh3, dims="tn", name="dw_o", out_dtype=bf16)
    dqkv, dsink = attn_bwd(qkv, sink_row, datt)
    db_qkv = colsum(dqkv, "db_qkv")
    dw_qkv = matmul(y2, dqkv, dims="tn", name="dw_qkv", out_dtype=bf16, tn=QKV_DIM)
    dy2 = matmul(dqkv, w_qkv, dims="nt", name="dy_qkv", tk=QKV_DIM)
    dh2, dg_mix1 = rmsnorm_bwd(h2, mix_g[1], dy2, dh3, "mix_dnorm1")
    layer1 = [jnp.concatenate([_by_owner(dw_o), dw_up1, _by_owner(dw_down1)], axis=1),
              _col_shards(dw_qkv, QKV_SHARD, QKV_PAD)]
    flight1, token1 = reducer.start(layer1, "l1", direct=True)
    dh1, dg_mlp0, dw_up0, dw_down0 = _mlp_bwd(dh2, h1, mlp_g[0], y1, a1, w_up[0], w_down[0], 0, after=token1)
    pending1 = reducer.finish(flight1, dh1, "l1")
    dw_out = matmul(ab, dh1, dims="tn", name="dw_out", out_dtype=bf16)
    flight0, token0 = reducer.start(
        [jnp.concatenate([dw_up0, _by_owner(dw_down0), _by_owner(dw_out)], axis=1)], "l0", direct=True)
    dab = matmul(dh1, w_out, dims="nt", name="mixer_dout", tn=1024, after=token0)
    dproj, dmix = mixer_bwd(proj, hstates, dab, mixer_prm)
    dw_in_p = matmul(y0, dproj, dims="tn", name="dw_in", out_dtype=bf16, tn=768)
    pending0 = reducer.finish(flight0, dw_in_p, "l0")
    flight_in, token_in = reducer.start([_col_shards(dw_in_p, IN_SHARD, IN_PAD)], "in")
    dy0 = matmul(dproj, w_in_p, dims="nt", name="dy_in", after=token_in)
    dx, dg_mix0 = rmsnorm_bwd(x, mix_g[0], dy0, dh1, "mix_dnorm0")
    pending_in = reducer.finish(flight_in, dx, "in")
    r_l1, r_qkv = reducer.collect(pending1, dx, "l1")
    (r_l0,) = reducer.collect(pending0, dx, "l0")
    (r_in,) = reducer.collect(pending_in, dx, "in")
    reduced = {
        "w_out_even": r_l0[None, 2048:], "w_in_even": r_in[None, :, :IN_SHARD], "w_qkv": r_qkv[None, :, :QKV_SHARD],
        "w_o": r_l1[None, :256], "w_up": jnp.stack([r_l0[:1024], r_l1[256:1280]]),
        "w_down": jnp.stack([r_l0[1024:2048], r_l1[1280:]]),
    }

    small_grads = {
        "norm_mix_g": jnp.concatenate([dg_mix0, dg_mix1], axis=0),
        "norm_mlp_g": jnp.concatenate([dg_mlp0, dg_mlp1], axis=0),
        "final_norm_g": dg_final[0], "gm_ln_g": dmix["ln_g"], "gm_ln_b": dmix["ln_b"],
        "gm_w_s": dmix["wm"][None], "gm_b_s": dmix["bs_t"][:, :N_BLK].T[None],
        "ssm_conv_b": dmix["conv_b"], "ssm_dt_bias": dmix["dt_bias"][:, :SSM_HEADS],
        "ssm_a_log": dmix["a_log"][:, :SSM_HEADS], "ssm_d": dmix["d_heads"][:, :SSM_HEADS],
        "ssm_norm_g": dmix["norm_g"], "attn_sinks": dsink[:, :SSM_HEADS],
        "ssm_conv_w": dmix["conv_w"][None, :4], "b_qkv": db_qkv, "b_o": db_o,
    }
    return loss, dx, reduced, small_grads


def kernel(x, norm_mix_g, norm_mlp_g, final_norm_g, w_in_even, w_out_even, gm_ln_g, gm_ln_b, gm_w_s, gm_b_s, ssm_conv_w, ssm_conv_b, ssm_dt_bias, ssm_a_log, ssm_d, ssm_norm_g, w_qkv, b_qkv, w_o, b_o, attn_sinks, w_up, w_down, loss_target, m_norm_mix_g, m_norm_mlp_g, m_final_norm_g, m_w_in_even, m_w_out_even, m_gm_ln_g, m_gm_ln_b, m_gm_w_s, m_gm_b_s, m_ssm_conv_w, m_ssm_conv_b, m_ssm_dt_bias, m_ssm_a_log, m_ssm_d, m_ssm_norm_g, m_w_qkv, m_b_qkv, m_w_o, m_b_o, m_attn_sinks, m_w_up, m_w_down, v_norm_mix_g, v_norm_mlp_g, v_final_norm_g, v_w_in_even, v_w_out_even, v_gm_ln_g, v_gm_ln_b, v_gm_w_s, v_gm_b_s, v_ssm_conv_w, v_ssm_conv_b, v_ssm_dt_bias, v_ssm_a_log, v_ssm_d, v_ssm_norm_g, v_w_qkv, v_b_qkv, v_w_o, v_b_o, v_attn_sinks, v_w_up, v_w_down):
    w = dict(norm_mix_g=norm_mix_g, norm_mlp_g=norm_mlp_g, final_norm_g=final_norm_g, w_in_even=w_in_even,
             w_out_even=w_out_even, gm_ln_g=gm_ln_g, gm_ln_b=gm_ln_b, gm_w_s=gm_w_s, gm_b_s=gm_b_s,
             ssm_conv_w=ssm_conv_w, ssm_conv_b=ssm_conv_b, ssm_dt_bias=ssm_dt_bias, ssm_a_log=ssm_a_log,
             ssm_d=ssm_d, ssm_norm_g=ssm_norm_g, w_qkv=w_qkv, b_qkv=b_qkv, w_o=w_o, b_o=b_o,
             attn_sinks=attn_sinks, w_up=w_up, w_down=w_down)
    m = dict(norm_mix_g=m_norm_mix_g, norm_mlp_g=m_norm_mlp_g, final_norm_g=m_final_norm_g,
             w_in_even=m_w_in_even, w_out_even=m_w_out_even, gm_ln_g=m_gm_ln_g, gm_ln_b=m_gm_ln_b,
             gm_w_s=m_gm_w_s, gm_b_s=m_gm_b_s, ssm_conv_w=m_ssm_conv_w, ssm_conv_b=m_ssm_conv_b,
             ssm_dt_bias=m_ssm_dt_bias, ssm_a_log=m_ssm_a_log, ssm_d=m_ssm_d, ssm_norm_g=m_ssm_norm_g,
             w_qkv=m_w_qkv, b_qkv=m_b_qkv, w_o=m_w_o, b_o=m_b_o, attn_sinks=m_attn_sinks, w_up=m_w_up,
             w_down=m_w_down)
    v = dict(norm_mix_g=v_norm_mix_g, norm_mlp_g=v_norm_mlp_g, final_norm_g=v_final_norm_g,
             w_in_even=v_w_in_even, w_out_even=v_w_out_even, gm_ln_g=v_gm_ln_g, gm_ln_b=v_gm_ln_b,
             gm_w_s=v_gm_w_s, gm_b_s=v_gm_b_s, ssm_conv_w=v_ssm_conv_w, ssm_conv_b=v_ssm_conv_b,
             ssm_dt_bias=v_ssm_dt_bias, ssm_a_log=v_ssm_a_log, ssm_d=v_ssm_d, ssm_norm_g=v_ssm_norm_g,
             w_qkv=v_w_qkv, b_qkv=v_b_qkv, w_o=v_w_o, b_o=v_b_o, attn_sinks=v_attn_sinks, w_up=v_w_up,
             w_down=v_w_down)
    names = ("norm_mix_g", "norm_mlp_g", "final_norm_g", "w_in_even", "w_out_even", "gm_ln_g", "gm_ln_b",
             "gm_w_s", "gm_b_s", "ssm_conv_w", "ssm_conv_b", "ssm_dt_bias", "ssm_a_log", "ssm_d", "ssm_norm_g",
             "w_qkv", "b_qkv", "w_o", "b_o", "attn_sinks", "w_up", "w_down")

    cx, cy, cc = lax.axis_index("x"), lax.axis_index("y"), lax.axis_index("c")
    chip = 2 * cx + cy
    c_idx = jnp.reshape(cc, (1,)).astype(jnp.int32)
    chip_idx = jnp.reshape(chip, (1,)).astype(jnp.int32)

    weights = WeightGatherer(w, chip_idx)
    sm = {n: w[n] for n, _ in _SMALL_SHAPES[:_N_REPLICATED]}

    reducer = GradReducer(c_idx, jnp.concatenate([chip_idx, c_idx]))
    loss_part, dx, grads, small_grads = _local_step(x[0], loss_target[0], weights, sm, reducer)
    loss = lax.psum(loss_part[0, 0], ("x", "y", "c"))

    small_sum = allreduce_small(_pack([small_grads[n] for n, _ in _SMALL_SHAPES], SMALL_ROWS))
    small_full = dict(zip([n for n, _ in _SMALL_SHAPES], _unpack(small_sum, [s for _, s in _SMALL_SHAPES])))
    for n, _ in _SMALL_SHAPES[:_N_REPLICATED]:
        grads[n] = small_full[n]
    for n, axis, width in _SHARDED_SMALL:
        grads[n] = lax.dynamic_slice_in_dim(small_full[n], chip * width, width, axis)
    grads = {n: grads[n].reshape(w[n].shape) for n in names}

    delta, new_m, new_v = {}, {}, {}
    for n in names:
        if n in _COLUMN_SHARDED:
            args = [jnp.transpose(d[n], (2, 0, 1)) for d in (w, grads, m, v)]
            grads[n] = jnp.transpose(args[1], (1, 2, 0))
            outs = adamw(*args, f"adamw_{n}")
            delta[n], new_m[n], new_v[n] = (jnp.transpose(o, (1, 2, 0)) for o in outs)
            continue
        shape = (1,) + w[n].shape if w[n].ndim == 1 else w[n].shape
        outs = adamw(*[d[n].reshape(shape) for d in (w, grads, m, v)], f"adamw_{n}")
        delta[n], new_m[n], new_v[n] = (o.reshape(w[n].shape) for o in outs)

    return (loss, dx[None], *[grads[n] for n in names], *[delta[n] for n in names],
            *[new_m[n] for n in names], *[new_v[n] for n in names])
```

```python
import functools

import jax
import jax.numpy as jnp
from jax import lax
from jax.experimental import pallas as pl
from jax.experimental.pallas import tpu as pltpu

f32 = jnp.float32
bf16 = jnp.bfloat16
MXU_DTYPE = bf16

RMS_EPS = 1e-5
LN_EPS = 1e-5
D_MODEL = 1024
D_FF = 4096
CH = 128
N_BLK = 8
SSM_HEADS = 16
IN_EVEN = 5136
NP_IN = 5376
OFF_U, OFF_V, OFF_Z, OFF_X, OFF_DT = 0, 1024, 2048, 3072, 5120
XBC_BLKS = 16
QKV_DIM = 1280
ATT_SCALE = 64 ** -0.5

ADAM_LR = 0.001
ADAM_B1 = 0.9
ADAM_B2 = 0.999
ADAM_EPS = 1e-08
ADAM_WD = 0.01
ADAM_STEP = 10

VMEM_LIMIT_BYTES = 48 * 1024 * 1024
N_CHIPS = 4
SMALL_ROWS = 256

NN = ((1,), (0,))
NT = ((1,), (1,))
TN = ((0,), (0,))


def _mm(a, b, dims):
    return lax.dot_general(a.astype(MXU_DTYPE), b.astype(MXU_DTYPE), (dims, ((), ())),
                           preferred_element_type=f32)


def _mm_exact(a, b):
    return jnp.dot(a, b, preferred_element_type=f32, precision=lax.Precision.HIGHEST)


def _cparams(sem=None):
    return pltpu.CompilerParams(dimension_semantics=sem, vmem_limit_bytes=VMEM_LIMIT_BYTES)


@jax.custom_vjp
def _swap64(x):
    return pltpu.roll(x, 64, axis=1)


_swap64.defvjp(lambda x: (pltpu.roll(x, 64, axis=1), None), lambda _, g: (pltpu.roll(g, 64, axis=1),))


@jax.custom_vjp
def _top_rows(x):
    return x[:x.shape[0] // 2]


_top_rows.defvjp(lambda x: (x[:x.shape[0] // 2], None),
                 lambda _, g: (jnp.concatenate([g, jnp.zeros_like(g)], axis=0),))


@jax.custom_vjp
def _bottom_rows(x):
    return x[x.shape[0] // 2:]


_bottom_rows.defvjp(lambda x: (x[x.shape[0] // 2:], None),
                    lambda _, g: (jnp.concatenate([jnp.zeros_like(g), g], axis=0),))


def _make_delay(k):
    @jax.custom_vjp
    def delay(ext):
        return pltpu.roll(ext, k, axis=0)[8:, :]

    def fwd(ext):
        return delay(ext), None

    def bwd(_, g):
        gp = jnp.concatenate([jnp.zeros((8, g.shape[1]), g.dtype), g], axis=0)
        return (pltpu.roll(gp, gp.shape[0] - k, axis=0),)

    delay.defvjp(fwd, bwd)
    return delay


_DELAYS = {k: _make_delay(k) for k in (1, 2, 3)}


_GELU_C = 0.7978845608028654
_GELU_K = 0.044715


@jax.custom_vjp
def _gelu(x):
    return 0.5 * x * (1.0 + jnp.tanh(_GELU_C * (x + _GELU_K * (x * x * x))))


def _gelu_fwd(x):
    t = jnp.tanh(_GELU_C * (x + _GELU_K * (x * x * x)))
    return 0.5 * x * (1.0 + t), (x, t)


def _gelu_bwd(res, g):
    x, t = res
    dz = _GELU_C + (3.0 * _GELU_C * _GELU_K) * (x * x)
    return (g * (0.5 * (1.0 + t) + (0.5 * x) * (1.0 - t * t) * dz),)


_gelu.defvjp(_gelu_fwd, _gelu_bwd)


def _col(m, lane, h):
    return jnp.sum(jnp.where(lane == h, m, 0.0), axis=1, keepdims=True)


def _row(m, sub, h):
    return jnp.sum(jnp.where(sub == h, m, 0.0), axis=0, keepdims=True)


def _mixer_chunk(us, vs, zs, xbcs, halos, dtblk, hps, prm):
    lane = lax.broadcasted_iota(jnp.int32, (CH, CH), 1)
    sub = lax.broadcasted_iota(jnp.int32, (CH, CH), 0)
    left = lane < 64
    top = sub < 64
    causal = sub >= lane

    gus = [_gelu(u) for u in us]
    gvs = [_gelu(v) for v in vs]
    mu = sum(jnp.sum(g, axis=1, keepdims=True) for g in gvs) / D_MODEL
    cen = [g - mu for g in gvs]
    var = sum(jnp.sum(c * c, axis=1, keepdims=True) for c in cen) / D_MODEL
    rstd = lax.rsqrt(var + LN_EPS)
    a_out = []
    for g in range(N_BLK):
        vn = cen[g] * rstd * prm["ln_g"][g] + prm["ln_b"][g]
        w = jnp.where(causal, prm["wm"][g], 0.0)
        mixed = _mm(w, vn, NN) + _col(prm["bs_t"], lane, g)
        a_out.append(gus[g] * mixed)

    act = []
    for b in range(XBC_BLKS):
        w8 = prm["conv_w"][b]
        sub8 = lax.broadcasted_iota(jnp.int32, w8.shape, 0)
        ext = jnp.concatenate([halos[b], xbcs[b]], axis=0)
        conv = xbcs[b] * _row(w8, sub8, 3) + prm["conv_b"][b]
        for k in (1, 2, 3):
            conv = conv + _DELAYS[k](ext) * _row(w8, sub8, 3 - k)
        act.append(jax.nn.silu(conv))

    dt = jax.nn.softplus(dtblk + prm["dt_bias"])
    a_neg = -jnp.exp(prm["a_log"])
    tri = causal.astype(f32)
    acum = _mm_exact(tri, dt * a_neg)
    acum_t = acum.T
    dt_t = dt.T
    last = sub == CH - 1
    ys, h_out = [], []
    for grp in range(4):
        bm = act[8 + grp]
        cm = act[12 + grp]
        cb = _mm(cm, bm, NT)
        for p in (2 * grp, 2 * grp + 1):
            h0, h1 = 2 * p, 2 * p + 1
            xp = act[p]
            hp = hps[p]
            wis = []
            for h in (h0, h1):
                seg = _col(acum, lane, h) - _row(acum_t, sub, h)
                decay = jnp.exp(jnp.where(causal, seg, -jnp.inf))
                wis.append(cb * decay * _row(dt_t, sub, h))
            wcat = jnp.concatenate(wis, axis=1)
            xbd = jnp.concatenate([jnp.where(left, xp, 0.0), jnp.where(left, 0.0, xp)], axis=0)
            y_diag = _mm(wcat, xbd, NN)
            a_end = [jnp.sum(jnp.where(last & (lane == h), acum, 0.0), keepdims=True) for h in (h0, h1)]
            a_col = jnp.where(left, _col(acum, lane, h0), _col(acum, lane, h1))
            dt_col = jnp.where(left, _col(dt, lane, h0), _col(dt, lane, h1))
            to_end = jnp.exp(jnp.where(left, a_end[0], a_end[1]) - a_col) * dt_col
            states = _mm(xp * to_end, bm, TN)
            chunk_decay = jnp.where(top, jnp.exp(a_end[0]), jnp.exp(a_end[1]))
            h_out.append(chunk_decay * hp + states)
            y_off = jnp.exp(a_col) * _mm(cm, hp, NT)
            d_skip = jnp.where(left[:1], _col(prm["d_heads"], lane[:1], h0), _col(prm["d_heads"], lane[:1], h1))
            ys.append((y_diag + y_off + xp * d_skip) * jax.nn.silu(zs[p]))

    b_out = []
    for grp in range(4):
        pair = (ys[2 * grp], ys[2 * grp + 1])
        ms = sum(jnp.sum(y * y, axis=1, keepdims=True) for y in pair) / 256.0
        r = lax.rsqrt(ms + RMS_EPS)
        for j, y in enumerate(pair):
            b_out.append(y * r * prm["norm_g"][2 * grp + j])
    return a_out, b_out, h_out


def _attn_block(qps, kprev, kcur, vprev, vcur, sink_row, first):
    lane = lax.broadcasted_iota(jnp.int32, (CH, CH), 1)
    left = lane < 64
    row2 = lax.broadcasted_iota(jnp.int32, (2 * CH, CH), 0)
    key2 = lax.broadcasted_iota(jnp.int32, (2 * CH, CH), 1)
    upper = row2 < CH
    own = key2 <= jnp.where(upper, row2, row2 - CH)

    def both_halves(a):
        sw = _swap64(a)
        return [jnp.where(left, a, sw), jnp.where(left, sw, a)]

    kc, kp, vc, vp = both_halves(kcur), both_halves(kprev), both_halves(vcur), both_halves(vprev)
    outs = []
    for p in range(N_BLK):
        j = p // 4
        q2 = jnp.concatenate([jnp.where(left, qps[p], 0.0), jnp.where(left, 0.0, qps[p])], axis=0)
        s_prev = jnp.where(first, -jnp.inf, _mm(q2, kp[j], NT) * ATT_SCALE)
        s = jnp.where(own, _mm(q2, kc[j], NT) * ATT_SCALE, s_prev)
        sink = jnp.where(upper[:, :1], _col(sink_row, lane[:1], 2 * p), _col(sink_row, lane[:1], 2 * p + 1))
        m = lax.stop_gradient(jnp.maximum(jnp.max(s, axis=1, keepdims=True), sink))
        pexp = jnp.exp(s - m)
        probs = pexp / (jnp.sum(pexp, axis=1, keepdims=True) + jnp.exp(sink - m))
        o = _mm(jnp.where(own, probs, 0.0), vc[j], NN) + _mm(jnp.where(own, 0.0, probs), vp[j], NN)
        outs.append(jnp.where(left, _top_rows(o), _bottom_rows(o)))
    return outs


def _rmsnorm(x, g):
    r = lax.rsqrt(jnp.mean(x * x, axis=-1, keepdims=True) + RMS_EPS)
    return x * r * g


def rmsnorm_fwd(x, g_row, name):
    s, d = x.shape
    tm = min(512, s)

    def body(x_ref, g_ref, y_ref):
        y_ref[...] = _rmsnorm(x_ref[...], g_ref[...]).astype(bf16)

    return pl.pallas_call(
        body, name=name, grid=(s // tm,),
        in_specs=[pl.BlockSpec((tm, d), lambda i: (i, 0)), pl.BlockSpec((1, d), lambda i: (0, 0))],
        out_specs=pl.BlockSpec((tm, d), lambda i: (i, 0)),
        out_shape=jax.ShapeDtypeStruct((s, d), bf16),
        compiler_params=_cparams(("parallel",)),
    )(x, g_row)


def rmsnorm_bwd(x, g_row, dy, res, name):
    s, d = x.shape
    tm = min(512, s)

    def body(x_ref, g_ref, dy_ref, res_ref, dx_ref, dg_ref):
        @pl.when(pl.program_id(0) == 0)
        def _():
            dg_ref[...] = jnp.zeros_like(dg_ref)

        _, vjp = jax.vjp(_rmsnorm, x_ref[...], g_ref[...])
        dx, dg = vjp(dy_ref[...])
        dx_ref[...] = res_ref[...] + dx
        dg_ref[...] += dg

    tile = pl.BlockSpec((tm, d), lambda i: (i, 0))
    row = pl.BlockSpec((1, d), lambda i: (0, 0))
    return pl.pallas_call(
        body, name=name, grid=(s // tm,),
        in_specs=[tile, row, tile, tile], out_specs=[tile, row],
        out_shape=[jax.ShapeDtypeStruct((s, d), f32), jax.ShapeDtypeStruct((1, d), f32)],
        compiler_params=_cparams(("arbitrary",)),
    )(x, g_row, dy, res)


def final_loss(h, g_row, target, name):
    s, d = h.shape
    tm = min(512, s)

    def body(h_ref, g_ref, t_ref, loss_ref, dh_ref, dg_ref):
        @pl.when(pl.program_id(0) == 0)
        def _():
            dg_ref[...] = jnp.zeros_like(dg_ref)
            loss_ref[...] = jnp.zeros_like(loss_ref)

        def f(hv, gv):
            err = jnp.square(_rmsnorm(hv, gv) - t_ref[...])
            return 0.5 * jnp.sum(jnp.mean(err, axis=-1, keepdims=True), axis=0, keepdims=True)

        loss, vjp = jax.vjp(f, h_ref[...], g_ref[...])
        dh, dg = vjp(jnp.ones_like(loss))
        dh_ref[...] = dh
        dg_ref[...] += dg
        loss_ref[...] += jnp.broadcast_to(loss, loss_ref.shape)

    tile = pl.BlockSpec((tm, d), lambda i: (i, 0))
    row = pl.BlockSpec((1, d), lambda i: (0, 0))
    return pl.pallas_call(
        body, name=name, grid=(s // tm,),
        in_specs=[tile, row, tile],
        out_specs=[pl.BlockSpec((1, 128), lambda i: (0, 0)), tile, row],
        out_shape=[jax.ShapeDtypeStruct((1, 128), f32), jax.ShapeDtypeStruct((s, d), f32),
                   jax.ShapeDtypeStruct((1, d), f32)],
        compiler_params=_cparams(("arbitrary",)),
    )(h, g_row, target)


def colsum(x, name):
    s, n = x.shape
    tm = min(512, s)

    def body(x_ref, o_ref):
        @pl.when(pl.program_id(0) == 0)
        def _():
            o_ref[...] = jnp.zeros_like(o_ref)

        o_ref[...] += jnp.sum(x_ref[...].astype(f32), axis=0, keepdims=True)

    return pl.pallas_call(
        body, name=name, grid=(s // tm,),
        in_specs=[pl.BlockSpec((tm, n), lambda i: (i, 0))],
        out_specs=pl.BlockSpec((1, n), lambda i: (0, 0)),
        out_shape=jax.ShapeDtypeStruct((1, n), f32),
        compiler_params=_cparams(("arbitrary",)),
    )(x)


def _fit(dim, want):
    if dim <= want:
        return dim
    t = want
    while dim % t:
        t -= 128
    return t


def matmul(a, b, *, dims, name, out_dtype=f32, tm=1024, tn=512, tk=8192, a_pro=None, epi=None, epi_args=(),
           out_by_col_tile=False, after=None):
    if dims == "nn":
        (m, k), n = a.shape, b.shape[1]
    elif dims == "nt":
        (m, k), n = a.shape, b.shape[0]
    else:
        (k, m), n = a.shape, b.shape[1]
    tm, tn, tk = _fit(m, tm), _fit(n, tn), _fit(k, tk)
    nk = k // tk
    if dims == "nn":
        a_spec = pl.BlockSpec((tm, tk), lambda i, j, kk: (i, kk))
        b_spec = pl.BlockSpec((tk, tn), lambda i, j, kk: (kk, j))
        dn = NN
    elif dims == "nt":
        a_spec = pl.BlockSpec((tm, tk), lambda i, j, kk: (i, kk))
        b_spec = pl.BlockSpec((tn, tk), lambda i, j, kk: (j, kk))
        dn = NT
    else:
        a_spec = pl.BlockSpec((tk, tm), lambda i, j, kk: (kk, i))
        b_spec = pl.BlockSpec((tk, tn), lambda i, j, kk: (kk, j))
        dn = TN
    e_specs = [pl.BlockSpec((tm, tn), lambda i, j, kk: (i, j)) if kind == "tile"
               else pl.BlockSpec((1, tn), lambda i, j, kk: (0, j)) for kind, _ in epi_args]
    n_epi = len(epi_args)
    order_specs = [] if after is None else [pl.BlockSpec((8, 128), lambda i, j, kk: (0, 0))]
    order_args = [] if after is None else [after]

    def body(*refs):
        a_ref, b_ref = refs[0], refs[1]
        e_refs = refs[2:2 + n_epi]
        n_in = 2 + n_epi + len(order_args)
        o_ref = refs[n_in]
        av = a_ref[...]
        if a_pro is not None:
            av = a_pro(av)
        part = _mm(av, b_ref[...], dn)

        def finish(acc):
            if epi is not None:
                acc = epi(acc, *[r[...] for r in e_refs])
            o_ref[...] = acc.astype(out_dtype)

        if nk == 1:
            finish(part)
        else:
            acc_ref = refs[n_in + 1]
            kk = pl.program_id(2)

            @pl.when(kk == 0)
            def _():
                acc_ref[...] = part

            @pl.when(kk > 0)
            def _():
                acc_ref[...] += part

            @pl.when(kk == nk - 1)
            def _():
                finish(acc_ref[...])

    if out_by_col_tile:
        out_spec = pl.BlockSpec((None, tm, tn), lambda i, j, kk: (j, i, 0))
        out_shape = jax.ShapeDtypeStruct((n // tn, m, tn), out_dtype)
    else:
        out_spec = pl.BlockSpec((tm, tn), lambda i, j, kk: (i, j))
        out_shape = jax.ShapeDtypeStruct((m, n), out_dtype)
    return pl.pallas_call(
        body, name=name, grid=(m // tm, n // tn, nk),
        in_specs=[a_spec, b_spec] + e_specs + order_specs,
        out_specs=out_spec,
        out_shape=out_shape,
        scratch_shapes=[pltpu.VMEM((tm, tn), f32)] if nk > 1 else [],
        compiler_params=_cparams(("parallel", "parallel", "arbitrary")),
    )(a, b, *[arr for _, arr in epi_args], *order_args)


def _relu2(a):
    r = jnp.maximum(a.astype(f32), 0.0)
    return r * r


def _add(acc, t):
    return acc + t


def _add_bias(acc, t):
    return acc + t


def _add_bias_res(acc, bias, res):
    return acc + bias + res


def _times_relu2_grad(acc, a):
    return acc * (2.0 * jnp.maximum(a.astype(f32), 0.0))


def matmul_rows(a, b, *, dims, name, epi, epi_args, outs, tm=512, a_pro=None, after=None):
    m, k = a.shape
    n = b.shape[1] if dims == "nn" else b.shape[0]
    tm = _fit(m, tm)
    dn = NN if dims == "nn" else NT
    e_specs = [pl.BlockSpec((tm, arr.shape[1]), lambda i: (i, 0)) if kind == "tile"
               else pl.BlockSpec((1, arr.shape[1]), lambda i: (0, 0)) for kind, arr in epi_args]
    order_specs = [] if after is None else [pl.BlockSpec((8, 128), lambda i: (0, 0))]
    order_args = [] if after is None else [after]
    n_in = 2 + len(epi_args) + len(order_args)

    def body(*refs):
        av = refs[0][...]
        if a_pro is not None:
            av = a_pro(av)
        vals = epi(_mm(av, refs[1][...], dn), *[r[...] for r in refs[2:2 + len(epi_args)]])
        for (kind, _), o_ref, val in zip(outs, refs[n_in:], vals):
            if kind == "tile":
                o_ref[...] = val.astype(o_ref.dtype)
            else:
                @pl.when(pl.program_id(0) == 0)
                def _():
                    o_ref[...] = jnp.zeros_like(o_ref)

                o_ref[...] += val

    out_specs = [pl.BlockSpec((tm, n), lambda i: (i, 0)) if kind == "tile" else pl.BlockSpec((1, arg), lambda i: (0, 0))
                 for kind, arg in outs]
    out_shape = [jax.ShapeDtypeStruct((m, n), arg) if kind == "tile" else jax.ShapeDtypeStruct((1, arg), f32)
                 for kind, arg in outs]
    return pl.pallas_call(
        body, name=name, grid=(m // tm,),
        in_specs=[pl.BlockSpec((tm, k), lambda i: (i, 0)), pl.BlockSpec(b.shape, lambda i: (0, 0))] + e_specs + order_specs,
        out_specs=out_specs, out_shape=out_shape,
        compiler_params=_cparams(("arbitrary",)),
    )(a, b, *[arr for _, arr in epi_args], *order_args)


def _res_norm(acc, res, g):
    h = acc + res
    return h, _rmsnorm(h, g)


def _bias_res_norm(acc, bias, res, g):
    h = acc + bias + res
    return h, _rmsnorm(h, g)


def _res_norm_loss(acc, res, g, target):
    def f(h, gv):
        err = jnp.square(_rmsnorm(h, gv) - target)
        return 0.5 * jnp.sum(jnp.mean(err, axis=-1, keepdims=True), axis=0, keepdims=True)

    loss, vjp = jax.vjp(f, acc + res, g)
    dh, dg = vjp(jnp.ones_like(loss))
    return dh, dg, jnp.broadcast_to(loss, (1, 128))


def _norm_bwd_res(dy, x, g, res):
    _, vjp = jax.vjp(_rmsnorm, x, g)
    dx, dg = vjp(dy)
    return res + dx, dg


_MIXER_PARAM_SHAPES = (
    ("ln_g", (1, D_MODEL)), ("ln_b", (1, D_MODEL)), ("wm", (N_BLK, CH, CH)), ("bs_t", (CH, CH)),
    ("conv_w", (8, 2048)), ("conv_b", (1, 2048)), ("dt_bias", (1, CH)), ("a_log", (1, CH)),
    ("d_heads", (1, CH)), ("norm_g", (1, D_MODEL)),
)


def _blocks(v, n, off=0):
    return [v[:, off + i * CH: off + (i + 1) * CH] for i in range(n)]


def _split_mixer_params(vals):
    p = dict(vals)
    return {
        "ln_g": _blocks(p["ln_g"], N_BLK), "ln_b": _blocks(p["ln_b"], N_BLK),
        "wm": [p["wm"][g] for g in range(N_BLK)], "bs_t": p["bs_t"],
        "conv_w": _blocks(p["conv_w"], XBC_BLKS), "conv_b": _blocks(p["conv_b"], XBC_BLKS),
        "dt_bias": p["dt_bias"], "a_log": p["a_log"], "d_heads": p["d_heads"],
        "norm_g": _blocks(p["norm_g"], N_BLK),
    }


def _mixer_leaves(proj_ref, halo_ref, keep_halo):
    pv = proj_ref
    us = [pv[:, OFF_U + i * CH: OFF_U + (i + 1) * CH] for i in range(N_BLK)]
    vs = [pv[:, OFF_V + i * CH: OFF_V + (i + 1) * CH] for i in range(N_BLK)]
    zs = [pv[:, OFF_Z + i * CH: OFF_Z + (i + 1) * CH] for i in range(N_BLK)]
    xbcs = [pv[:, OFF_X + i * CH: OFF_X + (i + 1) * CH] for i in range(XBC_BLKS)]
    halos = [halo_ref[:, OFF_X + i * CH: OFF_X + (i + 1) * CH] * keep_halo for i in range(XBC_BLKS)]
    dtblk = pv[:, OFF_DT: OFF_DT + CH]
    return us, vs, zs, xbcs, halos, dtblk


def mixer_fwd(proj, prm):
    s = proj.shape[0]
    nc = s // CH
    names = [n for n, _ in _MIXER_PARAM_SHAPES]

    def body(proj_ref, halo_ref, *rest):
        p_refs = rest[:len(names)]
        ab_ref, hs_ref, h_ref = rest[len(names):]
        c = pl.program_id(0)

        @pl.when(c == 0)
        def _():
            h_ref[...] = jnp.zeros_like(h_ref)

        hs_ref[...] = h_ref[...]
        keep = (c > 0).astype(f32)
        us, vs, zs, xbcs, halos, dtblk = _mixer_leaves(proj_ref, halo_ref, keep)
        hps = [h_ref[i * CH:(i + 1) * CH, :] for i in range(N_BLK)]
        p = _split_mixer_params({n: r[...] for n, r in zip(names, p_refs)})
        a_out, b_out, h_out = _mixer_chunk(us, vs, zs, xbcs, halos, dtblk, hps, p)
        for i in range(N_BLK):
            ab_ref[:, i * CH:(i + 1) * CH] = a_out[i].astype(bf16)
            ab_ref[:, D_MODEL + i * CH: D_MODEL + (i + 1) * CH] = b_out[i].astype(bf16)
            h_ref[i * CH:(i + 1) * CH, :] = h_out[i]

    def const(shape):
        return pl.BlockSpec(shape, lambda c: (0,) * len(shape))

    return pl.pallas_call(
        body, name="mixer_fwd", grid=(nc,),
        in_specs=[pl.BlockSpec((CH, NP_IN), lambda c: (c, 0)),
                  pl.BlockSpec((8, NP_IN), lambda c: (jnp.maximum(c * (CH // 8) - 1, 0), 0))]
                 + [const(shp) for _, shp in _MIXER_PARAM_SHAPES],
        out_specs=[pl.BlockSpec((CH, 2 * D_MODEL), lambda c: (c, 0)),
                   pl.BlockSpec((None, D_MODEL, CH), lambda c: (c, 0, 0))],
        out_shape=[jax.ShapeDtypeStruct((s, 2 * D_MODEL), bf16), jax.ShapeDtypeStruct((nc, D_MODEL, CH), f32)],
        scratch_shapes=[pltpu.VMEM((D_MODEL, CH), f32)],
        compiler_params=_cparams(("arbitrary",)),
    )(proj, proj, *[prm[n] for n in names])


def mixer_bwd(proj, hstates, dab, prm):
    s = proj.shape[0]
    nc = s // CH
    names = [n for n, _ in _MIXER_PARAM_SHAPES]
    npar = len(names)

    def body(proj_ref, halo_ref, hs_ref, dab_ref, *rest):
        p_refs = rest[:npar]
        dproj_ref = rest[npar]
        g_refs = rest[npar + 1: 2 * npar + 1]
        dh_ref, dhalo_ref = rest[2 * npar + 1:]
        i = pl.program_id(0)
        c = nc - 1 - i

        @pl.when(i == 0)
        def _():
            dh_ref[...] = jnp.zeros_like(dh_ref)
            dhalo_ref[...] = jnp.zeros_like(dhalo_ref)
            for r in g_refs:
                r[...] = jnp.zeros_like(r)

        keep = (c > 0).astype(f32)
        us, vs, zs, xbcs, halos, dtblk = _mixer_leaves(proj_ref, halo_ref, keep)
        hps = [hs_ref[j * CH:(j + 1) * CH, :] for j in range(N_BLK)]
        pvals = {n: r[...] for n, r in zip(names, p_refs)}

        def fn(us, vs, zs, xbcs, halos, dtblk, hps, pvals):
            return _mixer_chunk(us, vs, zs, xbcs, halos, dtblk, hps, _split_mixer_params(pvals))

        _, vjp = jax.vjp(fn, us, vs, zs, xbcs, halos, dtblk, hps, pvals)
        da = [dab_ref[:, j * CH:(j + 1) * CH].astype(f32) for j in range(N_BLK)]
        db = [dab_ref[:, D_MODEL + j * CH: D_MODEL + (j + 1) * CH].astype(f32) for j in range(N_BLK)]
        dh = [dh_ref[j * CH:(j + 1) * CH, :] for j in range(N_BLK)]
        dus, dvs, dzs, dxbcs, dhalos, ddt, dhps, dp = vjp((da, db, dh))

        for j in range(N_BLK):
            dproj_ref[:, OFF_U + j * CH: OFF_U + (j + 1) * CH] = dus[j].astype(bf16)
            dproj_ref[:, OFF_V + j * CH: OFF_V + (j + 1) * CH] = dvs[j].astype(bf16)
            dproj_ref[:, OFF_Z + j * CH: OFF_Z + (j + 1) * CH] = dzs[j].astype(bf16)
            dh_ref[j * CH:(j + 1) * CH, :] = dhps[j]
        zeros_top = jnp.zeros((CH - 8, CH), f32)
        for j in range(XBC_BLKS):
            late = jnp.concatenate([zeros_top, dhalo_ref[:, j * CH:(j + 1) * CH]], axis=0)
            dproj_ref[:, OFF_X + j * CH: OFF_X + (j + 1) * CH] = (dxbcs[j] + late).astype(bf16)
        for j in range(XBC_BLKS):
            dhalo_ref[:, j * CH:(j + 1) * CH] = dhalos[j] * keep
        lane = lax.broadcasted_iota(jnp.int32, (CH, CH), 1)
        dproj_ref[:, OFF_DT: OFF_DT + CH] = jnp.where(lane < SSM_HEADS, ddt, 0.0).astype(bf16)
        dproj_ref[:, OFF_DT + CH:] = jnp.zeros((CH, NP_IN - OFF_DT - CH), bf16)
        for n, r in zip(names, g_refs):
            r[...] += dp[n]

    def const(shape):
        return pl.BlockSpec(shape, lambda i: (0,) * len(shape))

    outs = pl.pallas_call(
        body, name="mixer_bwd", grid=(nc,),
        in_specs=[pl.BlockSpec((CH, NP_IN), lambda i: (nc - 1 - i, 0)),
                  pl.BlockSpec((8, NP_IN), lambda i: (jnp.maximum((nc - 1 - i) * (CH // 8) - 1, 0), 0)),
                  pl.BlockSpec((None, D_MODEL, CH), lambda i: (nc - 1 - i, 0, 0)),
                  pl.BlockSpec((CH, 2 * D_MODEL), lambda i: (nc - 1 - i, 0))]
                 + [const(shp) for _, shp in _MIXER_PARAM_SHAPES],
        out_specs=[pl.BlockSpec((CH, NP_IN), lambda i: (nc - 1 - i, 0))]
                  + [const(shp) for _, shp in _MIXER_PARAM_SHAPES],
        out_shape=[jax.ShapeDtypeStruct((s, NP_IN), bf16)]
                  + [jax.ShapeDtypeStruct(shp, f32) for _, shp in _MIXER_PARAM_SHAPES],
        scratch_shapes=[pltpu.VMEM((D_MODEL, CH), f32), pltpu.VMEM((8, 2048), f32)],
        compiler_params=_cparams(("arbitrary",)),
    )(proj, proj, hstates, dab, *[prm[n] for n in names])
    return outs[0], dict(zip(names, outs[1:]))


_K_BLK = D_MODEL // CH
_V_BLK = _K_BLK + 1


def _attn_specs(rev, nb):
    def blk(i):
        return nb - 1 - i if rev else i

    q_spec = pl.BlockSpec((CH, D_MODEL), lambda i: (blk(i), 0))
    kv = lambda col, prev: pl.BlockSpec(
        (CH, CH), lambda i: (jnp.maximum(blk(i) - 1, 0) if prev else blk(i), col))
    return q_spec, [kv(_K_BLK, True), kv(_K_BLK, False), kv(_V_BLK, True), kv(_V_BLK, False)]


def attn_fwd(qkv, sink_row):
    s = qkv.shape[0]
    nb = s // CH

    def body(q_ref, kp_ref, kc_ref, vp_ref, vc_ref, sink_ref, o_ref):
        qps = [q_ref[:, p * CH:(p + 1) * CH] for p in range(N_BLK)]
        outs = _attn_block(qps, kp_ref[...], kc_ref[...], vp_ref[...], vc_ref[...], sink_ref[...],
                           pl.program_id(0) == 0)
        for p in range(N_BLK):
            o_ref[:, p * CH:(p + 1) * CH] = outs[p].astype(bf16)

    q_spec, kv_specs = _attn_specs(False, nb)
    return pl.pallas_call(
        body, name="attn_fwd", grid=(nb,),
        in_specs=[q_spec] + kv_specs + [pl.BlockSpec((1, CH), lambda i: (0, 0))],
        out_specs=pl.BlockSpec((CH, D_MODEL), lambda i: (i, 0)),
        out_shape=jax.ShapeDtypeStruct((s, D_MODEL), bf16),
        compiler_params=_cparams(("parallel",)),
    )(qkv, qkv, qkv, qkv, qkv, sink_row)


def attn_bwd(qkv, sink_row, dout):
    s = qkv.shape[0]
    nb = s // CH

    def body(q_ref, kp_ref, kc_ref, vp_ref, vc_ref, sink_ref, do_ref, dqkv_ref, dsink_ref, carry_ref):
        i = pl.program_id(0)
        blk = nb - 1 - i

        @pl.when(i == 0)
        def _():
            dsink_ref[...] = jnp.zeros_like(dsink_ref)
            carry_ref[...] = jnp.zeros_like(carry_ref)

        qps = [q_ref[:, p * CH:(p + 1) * CH] for p in range(N_BLK)]
        first = blk == 0
        _, vjp = jax.vjp(lambda *a: _attn_block(*a, first), qps, kp_ref[...], kc_ref[...], vp_ref[...],
                         vc_ref[...], sink_ref[...])
        dos = [do_ref[:, p * CH:(p + 1) * CH].astype(f32) for p in range(N_BLK)]
        dqs, dkp, dkc, dvp, dvc, dsink = vjp(dos)
        for p in range(N_BLK):
            dqkv_ref[:, p * CH:(p + 1) * CH] = dqs[p].astype(bf16)
        dqkv_ref[:, D_MODEL: D_MODEL + CH] = (dkc + carry_ref[0]).astype(bf16)
        dqkv_ref[:, D_MODEL + CH:] = (dvc + carry_ref[1]).astype(bf16)
        keep = jnp.logical_not(first).astype(f32)
        carry_ref[0] = dkp * keep
        carry_ref[1] = dvp * keep
        dsink_ref[...] += dsink

    q_spec, kv_specs = _attn_specs(True, nb)
    return pl.pallas_call(
        body, name="attn_bwd", grid=(nb,),
        in_specs=[q_spec] + kv_specs + [pl.BlockSpec((1, CH), lambda i: (0, 0)),
                                        pl.BlockSpec((CH, D_MODEL), lambda i: (nb - 1 - i, 0))],
        out_specs=[pl.BlockSpec((CH, QKV_DIM), lambda i: (nb - 1 - i, 0)), pl.BlockSpec((1, CH), lambda i: (0, 0))],
        out_shape=[jax.ShapeDtypeStruct((s, QKV_DIM), bf16), jax.ShapeDtypeStruct((1, CH), f32)],
        scratch_shapes=[pltpu.VMEM((2, CH, CH), f32)],
        compiler_params=_cparams(("arbitrary",)),
    )(qkv, qkv, qkv, qkv, qkv, sink_row, dout)


def adamw(w, g, m, v, name):
    def body(w_ref, g_ref, m_ref, v_ref, d_ref, nm_ref, nv_ref):
        gv = g_ref[...]
        nm = ADAM_B1 * m_ref[...] + (1.0 - ADAM_B1) * gv
        nv = ADAM_B2 * v_ref[...] + (1.0 - ADAM_B2) * jnp.square(gv)
        m_hat = nm / (1.0 - ADAM_B1 ** ADAM_STEP)
        v_hat = nv / (1.0 - ADAM_B2 ** ADAM_STEP)
        d_ref[...] = -ADAM_LR * (m_hat / (jnp.sqrt(v_hat) + ADAM_EPS) + ADAM_WD * w_ref[...])
        nm_ref[...] = nm
        nv_ref[...] = nv

    out_shape = [jax.ShapeDtypeStruct(w.shape, f32)] * 3
    if w.ndim == 3 and w.shape[1] == 1:
        tr = max(t for t in range(1, 129) if w.shape[0] % t == 0)
        tile = pl.BlockSpec((tr, 1, w.shape[2]), lambda i: (i, 0, 0))
        return pl.pallas_call(
            body, name=name, grid=(w.shape[0] // tr,),
            in_specs=[tile] * 4, out_specs=[tile] * 3, out_shape=out_shape,
            compiler_params=_cparams(("parallel",)),
        )(w, g, m, v)
    if w.ndim == 3 and w.shape[1] % 256 == 0:
        tile = pl.BlockSpec((None, 256, w.shape[2]), lambda l, i: (l, i, 0))
        return pl.pallas_call(
            body, name=name, grid=(w.shape[0], w.shape[1] // 256),
            in_specs=[tile] * 4, out_specs=[tile] * 3, out_shape=out_shape,
            compiler_params=_cparams(("parallel", "parallel")),
        )(w, g, m, v)
    return pl.pallas_call(body, name=name, in_specs=[_VMEM] * 4, out_specs=[_VMEM] * 3, out_shape=out_shape,
                          compiler_params=_cparams())(w, g, m, v)


_MESH = pl.DeviceIdType.MESH
_ANY = pl.BlockSpec(memory_space=pl.ANY)
_VMEM = pl.BlockSpec(memory_space=pltpu.VMEM)


def _place():
    x, y, c = lax.axis_index("x"), lax.axis_index("y"), lax.axis_index("c")
    chips = [(1 - x, y), (x, 1 - y), (1 - x, 1 - y)]
    return x, y, c, 2 * x + y, chips, [2 * cx + cy for cx, cy in chips]


def _half(c, rows):
    return pl.ds(pl.multiple_of(c * (rows // 2), 16), rows // 2)


def _step_rows(rows):
    return max(t for t in range(16, 641, 16) if rows % t == 0)


def place_shard(b, slot, name):
    r, c = b.shape
    tr = _step_rows(r)

    def body(slot_ref, b_ref, o_ref):
        o_ref[...] = b_ref[...]

    return pl.pallas_call(
        body, name=name,
        grid_spec=pltpu.PrefetchScalarGridSpec(
            num_scalar_prefetch=1, grid=(r // tr,),
            in_specs=[pl.BlockSpec((tr, c), lambda i, s: (i, 0))],
            out_specs=pl.BlockSpec((None, tr, c), lambda i, s: (s[0], i, 0))),
        out_shape=jax.ShapeDtypeStruct((N_CHIPS, r, c), b.dtype),
        compiler_params=_cparams(("parallel",)),
    )(slot, b)


_HBM = pl.BlockSpec(memory_space=pltpu.HBM)
_SEM = pl.BlockSpec(memory_space=pltpu.SEMAPHORE)
_EFFECT = pltpu.SideEffectType.DATAFLOW_SIDE_EFFECTING


def _gather_ici_copies(bufs, send_sems, recv_sems):
    x, y, c, me, chips, chip_idx = _place()
    return [pltpu.make_async_remote_copy(
        src_ref=buf.at[me, _half(c, buf.shape[1])], dst_ref=buf.at[chip_idx[j], _half(c, buf.shape[1])],
        send_sem=send_sems.at[3 * k + j], recv_sem=recv_sems.at[3 * k + j],
        device_id=(*chips[j], c), device_id_type=_MESH) for j in range(3) for k, buf in enumerate(bufs)]


def gather_start(groups):
    sizes = [len(g) for g in groups]
    flat = [b for g in groups for b in g]
    n = len(flat)

    def body(*refs):
        bufs, sems = refs[:n], refs[n:n + 2 * len(groups)]
        x, y, c, me, chips, chip_idx = _place()
        lo = 0
        for gi, size in enumerate(sizes):
            for j in range(3):
                for k, buf in enumerate(bufs[lo:lo + size]):
                    mine = buf.at[me, _half(c, buf.shape[1])]
                    pltpu.make_async_remote_copy(
                        src_ref=mine, dst_ref=mine, send_sem=sems[2 * gi].at[3 * k + j],
                        recv_sem=sems[2 * gi + 1].at[3 * k + j], device_id=(*chips[j], c),
                        device_id_type=_MESH).start()
            lo += size

    sem_shapes = [pltpu.SemaphoreType.DMA((3 * size,)) for size in sizes for _ in range(2)]
    outs = pl.pallas_call(
        body, name="gather_start",
        out_shape=(*sem_shapes, *[pltpu.HBM(b.shape, b.dtype) for b in flat]),
        in_specs=[_HBM] * n, out_specs=(*[_SEM] * len(sem_shapes), *[_HBM] * n),
        input_output_aliases={i: len(sem_shapes) + i for i in range(n)},
        compiler_params=pltpu.CompilerParams(has_side_effects=_EFFECT),
    )(*[pltpu.with_memory_space_constraint(b, pltpu.HBM) for b in flat])
    sems = [(outs[2 * gi], outs[2 * gi + 1]) for gi in range(len(groups))]
    thru, lo = [], len(sem_shapes)
    for size in sizes:
        thru.append(list(outs[lo:lo + size]))
        lo += size
    return sems, thru


def gather_wait(bufs, sems, after, tag):
    n = len(bufs)

    def body(*refs):
        for cp in _gather_ici_copies(refs[:n], refs[n], refs[n + 1]):
            cp.wait_send()
            cp.wait_recv()

    extra = [] if after is None else [after]
    return list(pl.pallas_call(
        body, name=f"gather_wait_{tag}",
        out_shape=[pltpu.HBM(b.shape, b.dtype) for b in bufs],
        in_specs=[_HBM] * n + [_SEM, _SEM] + [_ANY] * len(extra), out_specs=[_HBM] * n,
        input_output_aliases={i: i for i in range(n)},
        compiler_params=pltpu.CompilerParams(has_side_effects=_EFFECT),
    )(*bufs, *sems, *extra))


def gather_forward(bufs, tag):
    n = len(bufs)

    def body(*refs):
        out_refs = refs[n:2 * n]
        send_sems, recv_sems = refs[2 * n:]
        x, y, c, me, chips, chip_idx = _place()

        def copy(k, j, half):
            part = out_refs[k].at[chip_idx[j], _half(half, out_refs[k].shape[1])]
            return pltpu.make_async_remote_copy(
                src_ref=part, dst_ref=part, send_sem=send_sems.at[3 * k + j], recv_sem=recv_sems.at[3 * k + j],
                device_id=(x, y, 1 - c), device_id_type=_MESH)

        sends = [copy(k, j, c) for j in range(3) for k in range(n)]
        for cp in sends:
            cp.start()
        for j in range(3):
            for k in range(n):
                copy(k, j, 1 - c).wait_recv()
        for cp in sends:
            cp.wait_send()

    return list(pl.pallas_call(
        body, name=f"gather_forward_{tag}",
        out_shape=[jax.ShapeDtypeStruct(b.shape, b.dtype) for b in bufs],
        in_specs=[_ANY] * n, out_specs=[_ANY] * n, input_output_aliases={i: i for i in range(n)},
        scratch_shapes=[pltpu.SemaphoreType.DMA((3 * n,)), pltpu.SemaphoreType.DMA((3 * n,))],
    )(*bufs))


def exchange_halves(bufs, tag):
    n = len(bufs)

    def body(*refs):
        g_refs, out_refs = refs[:n], refs[n:2 * n]
        send_sems, recv_sems = refs[2 * n:]
        x, y, c, *_ = _place()
        cps = [pltpu.make_async_remote_copy(
            src_ref=g_refs[b].at[:, _half(1 - c, g_refs[b].shape[1])], dst_ref=out_refs[b],
            send_sem=send_sems.at[b], recv_sem=recv_sems.at[b], device_id=(x, y, 1 - c), device_id_type=_MESH)
            for b in range(n)]
        for cp in cps:
            cp.start()
        for cp in cps:
            cp.wait()

    return pl.pallas_call(
        body, name=f"exchange_halves_{tag}",
        out_shape=[jax.ShapeDtypeStruct((N_CHIPS, b.shape[1] // 2, b.shape[2]), b.dtype) for b in bufs],
        in_specs=[_ANY] * n, out_specs=[_ANY] * n,
        scratch_shapes=[pltpu.SemaphoreType.DMA((n,)), pltpu.SemaphoreType.DMA((n,))],
    )(*bufs)


def add_halves(g, got, c_idx, name):
    hr, cols = got.shape[1], got.shape[2]
    tr = _step_rows(hr)
    steps = hr // tr

    def body(c_ref, g_ref, got_ref, o_ref):
        o_ref[...] = (g_ref[...].astype(f32) + got_ref[...].astype(f32)).astype(bf16)

    return pl.pallas_call(
        body, name=name,
        grid_spec=pltpu.PrefetchScalarGridSpec(
            num_scalar_prefetch=1, grid=(N_CHIPS, steps),
            in_specs=[pl.BlockSpec((None, tr, cols), lambda s, i, c: (s, c[0] * steps + i, 0)),
                      pl.BlockSpec((None, tr, cols), lambda s, i, c: (s, i, 0))],
            out_specs=pl.BlockSpec((None, tr, cols), lambda s, i, c: (s, i, 0))),
        out_shape=jax.ShapeDtypeStruct(got.shape, bf16),
        compiler_params=_cparams(("parallel", "parallel")),
    )(c_idx, g, got)


def sum_chips(t, got, place_idx, name):
    hr, cols = t.shape[1], t.shape[2]
    tr = _step_rows(hr)
    steps = hr // tr

    def body(idx_ref, t_ref, got_ref, o_ref):
        acc = t_ref[...].astype(f32)
        for j in range(3):
            acc = acc + got_ref[j].astype(f32)
        o_ref[...] = acc

    return pl.pallas_call(
        body, name=name,
        grid_spec=pltpu.PrefetchScalarGridSpec(
            num_scalar_prefetch=1, grid=(steps,),
            in_specs=[pl.BlockSpec((None, tr, cols), lambda i, idx: (idx[0], i, 0)),
                      pl.BlockSpec((3, tr, cols), lambda i, idx: (0, i, 0))],
            out_specs=pl.BlockSpec((tr, cols), lambda i, idx: (idx[1] * steps + i, 0))),
        out_shape=jax.ShapeDtypeStruct((2 * hr, cols), f32),
        compiler_params=_cparams(("parallel",)),
    )(place_idx, t, got)


def share_halves(bufs, tag):
    n = len(bufs)

    def body(*refs):
        out_refs = refs[n:2 * n]
        send_sems, recv_sems = refs[2 * n:]
        x, y, c, *_ = _place()

        def copy(b, half):
            part = out_refs[b].at[_half(half, out_refs[b].shape[0])]
            return pltpu.make_async_remote_copy(
                src_ref=part, dst_ref=part, send_sem=send_sems.at[b], recv_sem=recv_sems.at[b],
                device_id=(x, y, 1 - c), device_id_type=_MESH)

        for b in range(n):
            copy(b, c).start()
        for b in range(n):
            copy(b, 1 - c).wait_recv()
        for b in range(n):
            copy(b, c).wait_send()

    return pl.pallas_call(
        body, name=f"share_halves_{tag}",
        out_shape=[jax.ShapeDtypeStruct(b.shape, b.dtype) for b in bufs],
        in_specs=[_ANY] * n, out_specs=[_ANY] * n, input_output_aliases={i: i for i in range(n)},
        scratch_shapes=[pltpu.SemaphoreType.DMA((n,)), pltpu.SemaphoreType.DMA((n,))],
    )(*bufs)


def _share_copies(refs, send_sems, recv_sems):
    x, y, c, *_ = _place()
    return [pltpu.make_async_remote_copy(
        src_ref=ref.at[_half(c, ref.shape[0])], dst_ref=ref.at[_half(c, ref.shape[0])], send_sem=send_sems.at[b],
        recv_sem=recv_sems.at[b], device_id=(x, y, 1 - c), device_id_type=_MESH) for b, ref in enumerate(refs)]


def share_start(bufs, tag):
    n = len(bufs)

    def body(*refs):
        for cp in _share_copies(refs[:n], refs[n], refs[n + 1]):
            cp.start()

    outs = pl.pallas_call(
        body, name=f"share_start_{tag}",
        out_shape=(pltpu.SemaphoreType.DMA((n,)), pltpu.SemaphoreType.DMA((n,)),
                   *[pltpu.HBM(b.shape, b.dtype) for b in bufs]),
        in_specs=[_HBM] * n, out_specs=(_SEM, _SEM, *[_HBM] * n),
        input_output_aliases={i: 2 + i for i in range(n)},
        compiler_params=pltpu.CompilerParams(has_side_effects=_EFFECT),
    )(*[pltpu.with_memory_space_constraint(b, pltpu.HBM) for b in bufs])
    return outs[0], outs[1], list(outs[2:])


def share_wait(send_sems, recv_sems, bufs, after, tag):
    n = len(bufs)

    def body(*refs):
        x, y, c, *_ = _place()
        for b, ref in enumerate(refs[:n]):
            cp = pltpu.make_async_remote_copy(
                src_ref=ref.at[_half(c, ref.shape[0])], dst_ref=ref.at[_half(1 - c, ref.shape[0])],
                send_sem=refs[n].at[b], recv_sem=refs[n + 1].at[b], device_id=(x, y, 1 - c), device_id_type=_MESH)
            cp.wait_send()
            cp.wait_recv()

    return list(pl.pallas_call(
        body, name=f"share_wait_{tag}",
        out_shape=[pltpu.HBM(b.shape, b.dtype) for b in bufs],
        in_specs=[_HBM] * n + [_SEM, _SEM, _ANY], out_specs=[_HBM] * n,
        input_output_aliases={i: i for i in range(n)},
        compiler_params=pltpu.CompilerParams(has_side_effects=_EFFECT),
    )(*bufs, send_sems, recv_sems, after))


def _scatter_copies(t_refs, land_refs, send_sems, recv_sems):
    x, y, c, me, chips, chip_idx = _place()
    return [pltpu.make_async_remote_copy(
        src_ref=t_refs[b].at[chip_idx[j]], dst_ref=land_refs[b].at[j], send_sem=send_sems.at[3 * b + j],
        recv_sem=recv_sems.at[3 * b + j], device_id=(*chips[j], c), device_id_type=_MESH)
        for j in range(3) for b in range(len(t_refs))]


def scatter_start(ts, tag):
    n = len(ts)
    lands = [lax.empty((3,) + t.shape[1:], t.dtype) for t in ts]

    def body(*refs):
        for cp in _scatter_copies(refs[:n], refs[n:2 * n], refs[2 * n], refs[2 * n + 1]):
            cp.start()
        token = refs[-1]
        token[...] = jnp.zeros_like(token)

    hbm = [pltpu.HBM(a.shape, a.dtype) for a in (*ts, *lands)]
    outs = pl.pallas_call(
        body, name=f"scatter_start_{tag}",
        out_shape=(pltpu.SemaphoreType.DMA((3 * n,)), pltpu.SemaphoreType.DMA((3 * n,)), *hbm,
                   jax.ShapeDtypeStruct((8, 128), f32)),
        in_specs=[_HBM] * (2 * n), out_specs=(_SEM, _SEM, *[_HBM] * (2 * n), _VMEM),
        input_output_aliases={i: 2 + i for i in range(2 * n)},
        compiler_params=pltpu.CompilerParams(has_side_effects=_EFFECT),
    )(*[pltpu.with_memory_space_constraint(a, pltpu.HBM) for a in (*ts, *lands)])
    return outs[0], outs[1], list(outs[2:2 + n]), list(outs[2 + n:2 + 2 * n]), outs[-1]


def scatter_wait(send_sems, recv_sems, ts, lands, after, tag):
    n = len(ts)

    def body(*refs):
        for cp in _scatter_copies(refs[:n], refs[n:2 * n], refs[2 * n], refs[2 * n + 1]):
            cp.wait_send()
            cp.wait_recv()

    outs = pl.pallas_call(
        body, name=f"scatter_wait_{tag}",
        out_shape=[pltpu.HBM(a.shape, a.dtype) for a in (*ts, *lands)],
        in_specs=[_HBM] * (2 * n) + [_SEM, _SEM, _ANY], out_specs=[_HBM] * (2 * n),
        input_output_aliases={i: i for i in range(2 * n)},
        compiler_params=pltpu.CompilerParams(has_side_effects=_EFFECT),
    )(*ts, *lands, send_sems, recv_sems, after)
    return list(outs[:n]), list(outs[n:])


N_SENDERS = 7


def _direct_copies(g_refs, land_refs, send_sems, recv_sems):
    x, y, c, me, chips, chip_idx = _place()
    cps = []
    for b, (g, land) in enumerate(zip(g_refs, land_refs)):
        rows, base = g.shape[1], N_SENDERS * b
        cps.append(pltpu.make_async_remote_copy(
            src_ref=g.at[me, _half(1 - c, rows)], dst_ref=land.at[0], send_sem=send_sems.at[base],
            recv_sem=recv_sems.at[base], device_id=(x, y, 1 - c), device_id_type=_MESH))
        for j in range(3):
            for core in range(2):
                cps.append(pltpu.make_async_remote_copy(
                    src_ref=g.at[chip_idx[j], _half(core, rows)], dst_ref=land.at[1 + 2 * j + c],
                    send_sem=send_sems.at[base + 1 + 2 * j + core], recv_sem=recv_sems.at[base + 1 + 2 * j + c],
                    device_id=(*chips[j], core), device_id_type=_MESH))
    return cps


def direct_start(gs, tag):
    n = len(gs)
    lands = [lax.empty((N_SENDERS, g.shape[1] // 2, g.shape[2]), g.dtype) for g in gs]

    def body(*refs):
        for cp in _direct_copies(refs[:n], refs[n:2 * n], refs[2 * n], refs[2 * n + 1]):
            cp.start()
        token = refs[-1]
        token[...] = jnp.zeros_like(token)

    hbm = [pltpu.HBM(a.shape, a.dtype) for a in (*gs, *lands)]
    outs = pl.pallas_call(
        body, name=f"direct_start_{tag}",
        out_shape=(pltpu.SemaphoreType.DMA((N_SENDERS * n,)), pltpu.SemaphoreType.DMA((N_SENDERS * n,)), *hbm,
                   jax.ShapeDtypeStruct((8, 128), f32)),
        in_specs=[_HBM] * (2 * n), out_specs=(_SEM, _SEM, *[_HBM] * (2 * n), _VMEM),
        input_output_aliases={i: 2 + i for i in range(2 * n)},
        compiler_params=pltpu.CompilerParams(has_side_effects=_EFFECT),
    )(*[pltpu.with_memory_space_constraint(a, pltpu.HBM) for a in (*gs, *lands)])
    return outs[0], outs[1], list(outs[2:2 + n]), list(outs[2 + n:2 + 2 * n]), outs[-1]


def direct_wait(send_sems, recv_sems, gs, lands, after, tag):
    n = len(gs)

    def body(*refs):
        g_refs, land_refs, sends, recvs = refs[:n], refs[n:2 * n], refs[2 * n], refs[2 * n + 1]
        for b in range(n):
            for k in range(N_SENDERS):
                cp = pltpu.make_async_remote_copy(
                    src_ref=g_refs[b].at[0, _half(0, g_refs[b].shape[1])], dst_ref=land_refs[b].at[k],
                    send_sem=sends.at[N_SENDERS * b + k], recv_sem=recvs.at[N_SENDERS * b + k],
                    device_id=_place()[:3], device_id_type=_MESH)
                cp.wait_send()
                cp.wait_recv()

    outs = pl.pallas_call(
        body, name=f"direct_wait_{tag}",
        out_shape=[pltpu.HBM(a.shape, a.dtype) for a in (*gs, *lands)],
        in_specs=[_HBM] * (2 * n) + [_SEM, _SEM, _ANY], out_specs=[_HBM] * (2 * n),
        input_output_aliases={i: i for i in range(2 * n)},
        compiler_params=pltpu.CompilerParams(has_side_effects=_EFFECT),
    )(*gs, *lands, send_sems, recv_sems, after)
    return list(outs[:n]), list(outs[n:])


def sum_senders(g, lands, place_idx, name):
    hr, cols = lands.shape[1], lands.shape[2]
    tr = _step_rows(hr)
    steps = hr // tr

    def body(idx_ref, g_ref, land_ref, o_ref):
        acc = g_ref[...].astype(f32)
        for k in range(N_SENDERS):
            acc = acc + land_ref[k].astype(f32)
        o_ref[...] = acc

    return pl.pallas_call(
        body, name=name,
        grid_spec=pltpu.PrefetchScalarGridSpec(
            num_scalar_prefetch=1, grid=(steps,),
            in_specs=[pl.BlockSpec((None, tr, cols), lambda i, idx: (idx[0], idx[1] * steps + i, 0)),
                      pl.BlockSpec((N_SENDERS, tr, cols), lambda i, idx: (0, i, 0))],
            out_specs=pl.BlockSpec((tr, cols), lambda i, idx: (idx[1] * steps + i, 0))),
        out_shape=jax.ShapeDtypeStruct((2 * hr, cols), f32),
        compiler_params=_cparams(("parallel",)),
    )(place_idx, g, lands)


class GradReducer:
    def __init__(self, c_idx, place_idx):
        self.c_idx, self.place_idx = c_idx, place_idx

    def start(self, bufs, tag, direct=False):
        if direct:
            send_sems, recv_sems, gs, lands, token = direct_start(bufs, tag)
            return (True, send_sems, recv_sems, gs, lands), token
        got = exchange_halves(bufs, tag)
        ts = [add_halves(b, g, self.c_idx, f"add_halves_{tag}{i}") for i, (b, g) in enumerate(zip(bufs, got))]
        send_sems, recv_sems, ts, lands, token = scatter_start(ts, tag)
        return (False, send_sems, recv_sems, ts, lands), token

    def finish(self, state, after, tag):
        direct, *flight = state
        if direct:
            gs, lands = direct_wait(*flight, after, tag)
            sums = [sum_senders(g, l, self.place_idx, f"sum_senders_{tag}{i}") for i, (g, l) in enumerate(zip(gs, lands))]
        else:
            ts, lands = scatter_wait(*flight, after, tag)
            sums = [sum_chips(t, l, self.place_idx, f"sum_chips_{tag}{i}") for i, (t, l) in enumerate(zip(ts, lands))]
        return share_start(sums, tag)

    def collect(self, pending, after, tag):
        return share_wait(*pending, after, tag)


def allreduce_small(sp):
    def body(s_ref, out_ref, gather_ref, send_sems, recv_sems):
        x, y, c, me, chips, chip_idx = _place()
        sibling = (x, y, 1 - c)

        def copy(k, chip, core, to, src=None):
            dst = gather_ref.at[2 * chip + core]
            return pltpu.make_async_remote_copy(
                src_ref=dst if src is None else src, dst_ref=dst, send_sem=send_sems.at[k],
                recv_sem=recv_sems.at[k], device_id=to, device_id_type=_MESH)

        first = [copy(0, me, c, sibling, src=s_ref)]
        first += [copy(1 + j, me, c, (*chips[j], c), src=s_ref) for j in range(3)]
        for cp in first:
            cp.start()
        gather_ref[2 * me + c] = s_ref[...]
        passed = [copy(4 + j, chip_idx[j], c, sibling) for j in range(3)]
        for j in range(3):
            copy(1 + j, chip_idx[j], c, sibling).wait_recv()
            passed[j].start()
        copy(0, me, 1 - c, sibling).wait_recv()
        for j in range(3):
            copy(4 + j, chip_idx[j], 1 - c, sibling).wait_recv()
        for cp in first + passed:
            cp.wait_send()
        acc = gather_ref[0]
        for d in range(1, 2 * N_CHIPS):
            acc = acc + gather_ref[d]
        out_ref[...] = acc

    return pl.pallas_call(
        body, name="allreduce_small",
        out_shape=jax.ShapeDtypeStruct(sp.shape, sp.dtype),
        in_specs=[_VMEM], out_specs=_VMEM,
        scratch_shapes=[pltpu.VMEM((2 * N_CHIPS,) + sp.shape, sp.dtype),
                        pltpu.SemaphoreType.DMA((7,)), pltpu.SemaphoreType.DMA((7,))],
        compiler_params=_cparams(),
    )(sp)


def _n_rows(shape):
    n = 1
    for d in shape:
        n *= d
    return 8 * (-(-n // 8192))


def _pack(arrays, total_rows):
    parts = []
    for a in arrays:
        flat = a.reshape(-1)
        parts.append(jnp.pad(flat, (0, 1024 * _n_rows(a.shape) - flat.shape[0])).reshape(-1, 1024))
    rows = jnp.concatenate(parts, axis=0)
    return jnp.pad(rows, ((0, total_rows - rows.shape[0]), (0, 0)))


def _unpack(packed, shapes):
    out, r = [], 0
    for shp in shapes:
        n = 1
        for d in shp:
            n *= d
        nr = _n_rows(shp)
        out.append(packed[r:r + nr].reshape(-1)[:n].reshape(shp))
        r += nr
    return out


_COLUMN_SHARDED = ("w_in_even", "w_qkv")
IN_SHARD, IN_PAD = 1284, 1408
QKV_SHARD, QKV_PAD = 320, 384


def _lane_padded(a, cols):
    return jnp.pad(a, ((0, 0), (0, cols - a.shape[1])))


_SMALL_SHAPES = (
    ("norm_mix_g", (2, 1024)), ("norm_mlp_g", (2, 1024)), ("final_norm_g", (1024,)), ("gm_ln_g", (1, 1024)),
    ("gm_ln_b", (1, 1024)), ("gm_w_s", (1, 8, 128, 128)), ("gm_b_s", (1, 8, 128)), ("ssm_conv_b", (1, 2048)),
    ("ssm_dt_bias", (1, 16)), ("ssm_a_log", (1, 16)), ("ssm_d", (1, 16)), ("ssm_norm_g", (1, 1024)),
    ("attn_sinks", (1, 16)), ("ssm_conv_w", (1, 4, 2048)), ("b_qkv", (1, 1280)), ("b_o", (1, 1024)),
)
_N_REPLICATED = 13
_SHARDED_SMALL = (("ssm_conv_w", 2, 512), ("b_qkv", 1, 320), ("b_o", 1, 256))
_SHARD_PACK_ROWS = 32


def _cols_by_owner(a):
    return a.transpose(1, 0, 2).reshape(a.shape[1], -1)


class WeightGatherer:
    def __init__(self, w, chip_idx):
        rows = lambda *parts: jnp.concatenate(parts, axis=0).astype(bf16)
        shards = [
            ("in", _lane_padded(w["w_in_even"][0], IN_PAD).astype(bf16)),
            ("l0", rows(w["w_out_even"][0], w["w_up"][0], w["w_down"][0])),
            ("l1", rows(w["w_o"][0], w["w_up"][1], w["w_down"][1])),
            ("qkv", _lane_padded(w["w_qkv"][0], QKV_PAD).astype(bf16)),
        ]
        shards.append(("small", _pack([w[n] for n, _, _ in _SHARDED_SMALL], _SHARD_PACK_ROWS)))
        placed = [place_shard(b, chip_idx, f"place_shard_{tag}") for tag, b in shards]
        self.sems, self.bufs = gather_start([[placed[0], placed[4]], placed[1:2], placed[2:4]])

    def _group(self, gi, after, tag):
        return gather_forward(gather_wait(self.bufs[gi], self.sems[gi], after, tag), tag)

    def mixer_in(self):
        g, small = self._group(0, None, "in")
        shard_shapes = [tuple(width if i == axis else d for i, d in enumerate(dict(_SMALL_SHAPES)[n]))
                        for n, axis, width in _SHARDED_SMALL]
        per_chip = [_unpack(small[s], shard_shapes) for s in range(N_CHIPS)]
        full = {n: jnp.concatenate([per_chip[s][i] for s in range(N_CHIPS)], axis=axis)
                for i, (n, axis, _) in enumerate(_SHARDED_SMALL)}
        return _lane_padded(_cols_by_owner(g[:, :, :IN_SHARD]), NP_IN), full

    def layer0(self, after):
        (g,) = self._group(1, after, "l0")
        return g[:, :512].reshape(2048, 1024), _cols_by_owner(g[:, 512:1536]), g[:, 1536:].reshape(4096, 1024)

    def layer1(self, after):
        g, q = self._group(2, after, "l1")
        return (_cols_by_owner(q[:, :, :QKV_SHARD]), g[:, :256].reshape(1024, 1024), _cols_by_owner(g[:, 256:1280]),
                g[:, 1280:].reshape(4096, 1024))


def _row2(v):
    return v.reshape(1, -1)


def _lane_pad(v):
    return jnp.pad(v, ((0, 0), (0, CH - v.shape[1])))


_H_AND_NORM = (("tile", f32), ("tile", bf16))
_DX_AND_DG = (("tile", f32), ("sum", D_MODEL))


def _mlp_bwd(dh_out, h, g_row, y, a, w_up, w_down, tag, after=None):
    da = matmul(dh_out, w_down, dims="nt", name=f"mlp_da{tag}", out_dtype=bf16, tn=1024,
                epi=_times_relu2_grad, epi_args=(("tile", a),), after=after)
    dw_down = matmul(a, dh_out, dims="tn", name=f"mlp_dwdown{tag}", out_dtype=bf16, a_pro=_relu2)
    dw_up = matmul(y, da, dims="tn", name=f"mlp_dwup{tag}", out_dtype=bf16, tn=1024, out_by_col_tile=True)
    dh, dg = matmul_rows(da, w_up, dims="nt", name=f"mlp_dy{tag}", epi=_norm_bwd_res,
                         epi_args=(("tile", h), ("row", g_row), ("tile", dh_out)), outs=_DX_AND_DG)
    return dh, dg, dw_up, dw_down


def _by_owner(a):
    return a.reshape(N_CHIPS, a.shape[0] // N_CHIPS, a.shape[1])


def _col_shards(a, shard, padded):
    return jnp.stack([_lane_padded(a[:, shard * s: shard * (s + 1)], padded) for s in range(N_CHIPS)])


def _local_step(x, target, weights, sm, reducer):
    w_up, w_down = [None, None], [None, None]
    w_in_p, sharded_small = weights.mixer_in()
    sm = {**sm, **sharded_small}
    mix_g = [_row2(sm["norm_mix_g"][i]) for i in range(2)]
    mlp_g = [_row2(sm["norm_mlp_g"][i]) for i in range(2)]
    mixer_prm = {
        "ln_g": sm["gm_ln_g"], "ln_b": sm["gm_ln_b"], "wm": sm["gm_w_s"][0],
        "bs_t": jnp.pad(sm["gm_b_s"][0].T, ((0, 0), (0, CH - N_BLK))),
        "conv_w": jnp.pad(sm["ssm_conv_w"][0], ((0, 4), (0, 0))), "conv_b": sm["ssm_conv_b"],
        "dt_bias": _lane_pad(sm["ssm_dt_bias"]), "a_log": _lane_pad(sm["ssm_a_log"]),
        "d_heads": _lane_pad(sm["ssm_d"]), "norm_g": sm["ssm_norm_g"],
    }
    sink_row = _lane_pad(sm["attn_sinks"])

    y0 = rmsnorm_fwd(x, mix_g[0], "mix_norm0")
    proj = matmul(y0, w_in_p, dims="nn", name="in_proj", tn=768)
    ab, hstates = mixer_fwd(proj, mixer_prm)
    w_out, w_up[0], w_down[0] = weights.layer0(ab)
    h1, y1 = matmul_rows(ab, w_out, dims="nn", name="out_proj", epi=_res_norm,
                         epi_args=(("tile", x), ("row", mlp_g[0])), outs=_H_AND_NORM)
    a1 = matmul(y1, w_up[0], dims="nn", name="mlp_up0", out_dtype=bf16, tn=1024)
    w_qkv, w_o, w_up[1], w_down[1] = weights.layer1(a1)
    h2, y2 = matmul_rows(a1, w_down[0], dims="nn", name="mlp_down0", a_pro=_relu2, epi=_res_norm,
                         epi_args=(("tile", h1), ("row", mix_g[1])), outs=_H_AND_NORM)
    qkv = matmul(y2, w_qkv, dims="nn", name="qkv_proj", tn=QKV_DIM, epi=_add_bias, epi_args=(("row", sm["b_qkv"]),))
    att = attn_fwd(qkv, sink_row)
    h3, y3 = matmul_rows(att, w_o, dims="nn", name="o_proj", epi=_bias_res_norm,
                         epi_args=(("row", sm["b_o"]), ("tile", h2), ("row", mlp_g[1])), outs=_H_AND_NORM)
    a3 = matmul(y3, w_up[1], dims="nn", name="mlp_up1", out_dtype=bf16, tn=1024)
    dh4, dg_final, loss = matmul_rows(
        a3, w_down[1], dims="nn", name="mlp_down1", a_pro=_relu2, epi=_res_norm_loss,
        epi_args=(("tile", h3), ("row", _row2(sm["final_norm_g"])), ("tile", target)),
        outs=(("tile", f32), ("sum", D_MODEL), ("sum", 128)))

    dh3, dg_mlp1, dw_up1, dw_down1 = _mlp_bwd(dh4, h3, mlp_g[1], y3, a3, w_up[1], w_down[1], 1)
    db_o = colsum(dh3, "db_o")
    datt = matmul(dh3, w_o, dims="nt", name="attn_dout", out_dtype=bf16)
    dw_o = matmul(att, dh3, dims="tn", name="dw_o", out_dtype=bf16)
    dqkv, dsink = attn_bwd(qkv, sink_row, datt)
    db_qkv = colsum(dqkv, "db_qkv")
    dw_qkv = matmul(y2, dqkv, dims="tn", name="dw_qkv", out_dtype=bf16, tn=QKV_DIM)
    dh2, dg_mix1 = matmul_rows(dqkv, w_qkv, dims="nt", name="dy_qkv", epi=_norm_bwd_res,
                               epi_args=(("tile", h2), ("row", mix_g[1]), ("tile", dh3)), outs=_DX_AND_DG)
    layer1 = [jnp.concatenate([_by_owner(dw_o), dw_up1, _by_owner(dw_down1)], axis=1),
              _col_shards(dw_qkv, QKV_SHARD, QKV_PAD)]
    flight1, token1 = reducer.start(layer1, "l1", direct=True)
    dh1, dg_mlp0, dw_up0, dw_down0 = _mlp_bwd(dh2, h1, mlp_g[0], y1, a1, w_up[0], w_down[0], 0, after=token1)
    pending1 = reducer.finish(flight1, dh1, "l1")
    dw_out = matmul(ab, dh1, dims="tn", name="dw_out", out_dtype=bf16)
    flight0, token0 = reducer.start(
        [jnp.concatenate([dw_up0, _by_owner(dw_down0), _by_owner(dw_out)], axis=1)], "l0", direct=True)
    dab = matmul(dh1, w_out, dims="nt", name="mixer_dout", tn=1024, after=token0)
    dproj, dmix = mixer_bwd(proj, hstates, dab, mixer_prm)
    dw_in_p = matmul(y0, dproj, dims="tn", name="dw_in", out_dtype=bf16, tn=768)
    pending0 = reducer.finish(flight0, dw_in_p, "l0")
    flight_in, token_in = reducer.start([_col_shards(dw_in_p, IN_SHARD, IN_PAD)], "in")
    dx, dg_mix0 = matmul_rows(dproj, w_in_p, dims="nt", name="dy_in", tm=256, epi=_norm_bwd_res,
                              epi_args=(("tile", x), ("row", mix_g[0]), ("tile", dh1)), outs=_DX_AND_DG, after=token_in)
    pending_in = reducer.finish(flight_in, dx, "in")
    r_l1, r_qkv = reducer.collect(pending1, dx, "l1")
    (r_l0,) = reducer.collect(pending0, dx, "l0")
    (r_in,) = reducer.collect(pending_in, dx, "in")
    reduced = {
        "w_out_even": r_l0[None, 2048:], "w_in_even": r_in[None, :, :IN_SHARD], "w_qkv": r_qkv[None, :, :QKV_SHARD],
        "w_o": r_l1[None, :256], "w_up": jnp.stack([r_l0[:1024], r_l1[256:1280]]),
        "w_down": jnp.stack([r_l0[1024:2048], r_l1[1280:]]),
    }

    small_grads = {
        "norm_mix_g": jnp.concatenate([dg_mix0, dg_mix1], axis=0),
        "norm_mlp_g": jnp.concatenate([dg_mlp0, dg_mlp1], axis=0),
        "final_norm_g": dg_final[0], "gm_ln_g": dmix["ln_g"], "gm_ln_b": dmix["ln_b"],
        "gm_w_s": dmix["wm"][None], "gm_b_s": dmix["bs_t"][:, :N_BLK].T[None],
        "ssm_conv_b": dmix["conv_b"], "ssm_dt_bias": dmix["dt_bias"][:, :SSM_HEADS],
        "ssm_a_log": dmix["a_log"][:, :SSM_HEADS], "ssm_d": dmix["d_heads"][:, :SSM_HEADS],
        "ssm_norm_g": dmix["norm_g"], "attn_sinks": dsink[:, :SSM_HEADS],
        "ssm_conv_w": dmix["conv_w"][None, :4], "b_qkv": db_qkv, "b_o": db_o,
    }
    return loss, dx, reduced, small_grads


def kernel(x, norm_mix_g, norm_mlp_g, final_norm_g, w_in_even, w_out_even, gm_ln_g, gm_ln_b, gm_w_s, gm_b_s, ssm_conv_w, ssm_conv_b, ssm_dt_bias, ssm_a_log, ssm_d, ssm_norm_g, w_qkv, b_qkv, w_o, b_o, attn_sinks, w_up, w_down, loss_target, m_norm_mix_g, m_norm_mlp_g, m_final_norm_g, m_w_in_even, m_w_out_even, m_gm_ln_g, m_gm_ln_b, m_gm_w_s, m_gm_b_s, m_ssm_conv_w, m_ssm_conv_b, m_ssm_dt_bias, m_ssm_a_log, m_ssm_d, m_ssm_norm_g, m_w_qkv, m_b_qkv, m_w_o, m_b_o, m_attn_sinks, m_w_up, m_w_down, v_norm_mix_g, v_norm_mlp_g, v_final_norm_g, v_w_in_even, v_w_out_even, v_gm_ln_g, v_gm_ln_b, v_gm_w_s, v_gm_b_s, v_ssm_conv_w, v_ssm_conv_b, v_ssm_dt_bias, v_ssm_a_log, v_ssm_d, v_ssm_norm_g, v_w_qkv, v_b_qkv, v_w_o, v_b_o, v_attn_sinks, v_w_up, v_w_down):
    w = dict(norm_mix_g=norm_mix_g, norm_mlp_g=norm_mlp_g, final_norm_g=final_norm_g, w_in_even=w_in_even,
             w_out_even=w_out_even, gm_ln_g=gm_ln_g, gm_ln_b=gm_ln_b, gm_w_s=gm_w_s, gm_b_s=gm_b_s,
             ssm_conv_w=ssm_conv_w, ssm_conv_b=ssm_conv_b, ssm_dt_bias=ssm_dt_bias, ssm_a_log=ssm_a_log,
             ssm_d=ssm_d, ssm_norm_g=ssm_norm_g, w_qkv=w_qkv, b_qkv=b_qkv, w_o=w_o, b_o=b_o,
             attn_sinks=attn_sinks, w_up=w_up, w_down=w_down)
    m = dict(norm_mix_g=m_norm_mix_g, norm_mlp_g=m_norm_mlp_g, final_norm_g=m_final_norm_g,
             w_in_even=m_w_in_even, w_out_even=m_w_out_even, gm_ln_g=m_gm_ln_g, gm_ln_b=m_gm_ln_b,
             gm_w_s=m_gm_w_s, gm_b_s=m_gm_b_s, ssm_conv_w=m_ssm_conv_w, ssm_conv_b=m_ssm_conv_b,
             ssm_dt_bias=m_ssm_dt_bias, ssm_a_log=m_ssm_a_log, ssm_d=m_ssm_d, ssm_norm_g=m_ssm_norm_g,
             w_qkv=m_w_qkv, b_qkv=m_b_qkv, w_o=m_w_o, b_o=m_b_o, attn_sinks=m_attn_sinks, w_up=m_w_up,
             w_down=m_w_down)
    v = dict(norm_mix_g=v_norm_mix_g, norm_mlp_g=v_norm_mlp_g, final_norm_g=v_final_norm_g,
             w_in_even=v_w_in_even, w_out_even=v_w_out_even, gm_ln_g=v_gm_ln_g, gm_ln_b=v_gm_ln_b,
             gm_w_s=v_gm_w_s, gm_b_s=v_gm_b_s, ssm_conv_w=v_ssm_conv_w, ssm_conv_b=v_ssm_conv_b,
             ssm_dt_bias=v_ssm_dt_bias, ssm_a_log=v_ssm_a_log, ssm_d=v_ssm_d, ssm_norm_g=v_ssm_norm_g,
             w_qkv=v_w_qkv, b_qkv=v_b_qkv, w_o=v_w_o, b_o=v_b_o, attn_sinks=v_attn_sinks, w_up=v_w_up,
             w_down=v_w_down)
    names = ("norm_mix_g", "norm_mlp_g", "final_norm_g", "w_in_even", "w_out_even", "gm_ln_g", "gm_ln_b",
             "gm_w_s", "gm_b_s", "ssm_conv_w", "ssm_conv_b", "ssm_dt_bias", "ssm_a_log", "ssm_d", "ssm_norm_g",
             "w_qkv", "b_qkv", "w_o", "b_o", "attn_sinks", "w_up", "w_down")

    cx, cy, cc = lax.axis_index("x"), lax.axis_index("y"), lax.axis_index("c")
    chip = 2 * cx + cy
    c_idx = jnp.reshape(cc, (1,)).astype(jnp.int32)
    chip_idx = jnp.reshape(chip, (1,)).astype(jnp.int32)

    weights = WeightGatherer(w, chip_idx)
    sm = {n: w[n] for n, _ in _SMALL_SHAPES[:_N_REPLICATED]}

    reducer = GradReducer(c_idx, jnp.concatenate([chip_idx, c_idx]))
    loss_part, dx, grads, small_grads = _local_step(x[0], loss_target[0], weights, sm, reducer)
    loss = lax.psum(loss_part[0, 0], ("x", "y", "c"))

    small_sum = allreduce_small(_pack([small_grads[n] for n, _ in _SMALL_SHAPES], SMALL_ROWS))
    small_full = dict(zip([n for n, _ in _SMALL_SHAPES], _unpack(small_sum, [s for _, s in _SMALL_SHAPES])))
    for n, _ in _SMALL_SHAPES[:_N_REPLICATED]:
        grads[n] = small_full[n]
    for n, axis, width in _SHARDED_SMALL:
        grads[n] = lax.dynamic_slice_in_dim(small_full[n], chip * width, width, axis)
    grads = {n: grads[n].reshape(w[n].shape) for n in names}

    delta, new_m, new_v = {}, {}, {}
    for n in names:
        if n in _COLUMN_SHARDED:
            args = [jnp.transpose(d[n], (2, 0, 1)) for d in (w, grads, m, v)]
            grads[n] = jnp.transpose(args[1], (1, 2, 0))
            outs = adamw(*args, f"adamw_{n}")
            delta[n], new_m[n], new_v[n] = (jnp.transpose(o, (1, 2, 0)) for o in outs)
            continue
        shape = (1,) + w[n].shape if w[n].ndim == 1 else w[n].shape
        outs = adamw(*[d[n].reshape(shape) for d in (w, grads, m, v)], f"adamw_{n}")
        delta[n], new_m[n], new_v[n] = (o.reshape(w[n].shape) for o in outs)

    return (loss, dx[None], *[grads[n] for n in names], *[delta[n] for n in names],
            *[new_m[n] for n in names], *[new_v[n] for n in names])
```

```python
import functools

import jax
import jax.numpy as jnp
from jax import lax
from jax.experimental import pallas as pl
from jax.experimental.pallas import tpu as pltpu

f32 = jnp.float32
bf16 = jnp.bfloat16
MXU_DTYPE = bf16

RMS_EPS = 1e-5
LN_EPS = 1e-5
D_MODEL = 1024
D_FF = 4096
CH = 128
N_BLK = 8
SSM_HEADS = 16
IN_EVEN = 5136
NP_IN = 5376
OFF_U, OFF_V, OFF_Z, OFF_X, OFF_DT = 0, 1024, 2048, 3072, 5120
XBC_BLKS = 16
QKV_DIM = 1280
ATT_SCALE = 64 ** -0.5

ADAM_LR = 0.001
ADAM_B1 = 0.9
ADAM_B2 = 0.999
ADAM_EPS = 1e-08
ADAM_WD = 0.01
ADAM_STEP = 10

VMEM_LIMIT_BYTES = 48 * 1024 * 1024
N_CHIPS = 4
SMALL_ROWS = 256

NN = ((1,), (0,))
NT = ((1,), (1,))
TN = ((0,), (0,))


def _mm(a, b, dims):
    return lax.dot_general(a.astype(MXU_DTYPE), b.astype(MXU_DTYPE), (dims, ((), ())),
                           preferred_element_type=f32)


def _mm_exact(a, b):
    return jnp.dot(a, b, preferred_element_type=f32, precision=lax.Precision.HIGHEST)


def _cparams(sem=None):
    return pltpu.CompilerParams(dimension_semantics=sem, vmem_limit_bytes=VMEM_LIMIT_BYTES)


@jax.custom_vjp
def _swap64(x):
    return pltpu.roll(x, 64, axis=1)


_swap64.defvjp(lambda x: (pltpu.roll(x, 64, axis=1), None), lambda _, g: (pltpu.roll(g, 64, axis=1),))


@jax.custom_vjp
def _top_rows(x):
    return x[:x.shape[0] // 2]


_top_rows.defvjp(lambda x: (x[:x.shape[0] // 2], None),
                 lambda _, g: (jnp.concatenate([g, jnp.zeros_like(g)], axis=0),))


@jax.custom_vjp
def _bottom_rows(x):
    return x[x.shape[0] // 2:]


_bottom_rows.defvjp(lambda x: (x[x.shape[0] // 2:], None),
                    lambda _, g: (jnp.concatenate([jnp.zeros_like(g), g], axis=0),))


def _make_delay(k):
    @jax.custom_vjp
    def delay(ext):
        return pltpu.roll(ext, k, axis=0)[8:, :]

    def fwd(ext):
        return delay(ext), None

    def bwd(_, g):
        gp = jnp.concatenate([jnp.zeros((8, g.shape[1]), g.dtype), g], axis=0)
        return (pltpu.roll(gp, gp.shape[0] - k, axis=0),)

    delay.defvjp(fwd, bwd)
    return delay


_DELAYS = {k: _make_delay(k) for k in (1, 2, 3)}


_GELU_C = 0.7978845608028654
_GELU_K = 0.044715


@jax.custom_vjp
def _gelu(x):
    return 0.5 * x * (1.0 + jnp.tanh(_GELU_C * (x + _GELU_K * (x * x * x))))


def _gelu_fwd(x):
    t = jnp.tanh(_GELU_C * (x + _GELU_K * (x * x * x)))
    return 0.5 * x * (1.0 + t), (x, t)


def _gelu_bwd(res, g):
    x, t = res
    dz = _GELU_C + (3.0 * _GELU_C * _GELU_K) * (x * x)
    return (g * (0.5 * (1.0 + t) + (0.5 * x) * (1.0 - t * t) * dz),)


_gelu.defvjp(_gelu_fwd, _gelu_bwd)


def _col(m, lane, h):
    return jnp.sum(jnp.where(lane == h, m, 0.0), axis=1, keepdims=True)


def _row(m, sub, h):
    return jnp.sum(jnp.where(sub == h, m, 0.0), axis=0, keepdims=True)


def _mixer_chunk(us, vs, zs, xbcs, halos, dtblk, hps, prm):
    lane = lax.broadcasted_iota(jnp.int32, (CH, CH), 1)
    sub = lax.broadcasted_iota(jnp.int32, (CH, CH), 0)
    left = lane < 64
    top = sub < 64
    causal = sub >= lane

    gus = [_gelu(u) for u in us]
    gvs = [_gelu(v) for v in vs]
    mu = sum(jnp.sum(g, axis=1, keepdims=True) for g in gvs) / D_MODEL
    cen = [g - mu for g in gvs]
    var = sum(jnp.sum(c * c, axis=1, keepdims=True) for c in cen) / D_MODEL
    rstd = lax.rsqrt(var + LN_EPS)
    a_out = []
    for g in range(N_BLK):
        vn = cen[g] * rstd * prm["ln_g"][g] + prm["ln_b"][g]
        w = jnp.where(causal, prm["wm"][g], 0.0)
        mixed = _mm(w, vn, NN) + _col(prm["bs_t"], lane, g)
        a_out.append(gus[g] * mixed)

    act = []
    for b in range(XBC_BLKS):
        w8 = prm["conv_w"][b]
        sub8 = lax.broadcasted_iota(jnp.int32, w8.shape, 0)
        ext = jnp.concatenate([halos[b], xbcs[b]], axis=0)
        conv = xbcs[b] * _row(w8, sub8, 3) + prm["conv_b"][b]
        for k in (1, 2, 3):
            conv = conv + _DELAYS[k](ext) * _row(w8, sub8, 3 - k)
        act.append(jax.nn.silu(conv))

    dt = jax.nn.softplus(dtblk + prm["dt_bias"])
    a_neg = -jnp.exp(prm["a_log"])
    tri = causal.astype(f32)
    acum = _mm_exact(tri, dt * a_neg)
    acum_t = acum.T
    dt_t = dt.T
    last = sub == CH - 1
    ys, h_out = [], []
    for grp in range(4):
        bm = act[8 + grp]
        cm = act[12 + grp]
        cb = _mm(cm, bm, NT)
        for p in (2 * grp, 2 * grp + 1):
            h0, h1 = 2 * p, 2 * p + 1
            xp = act[p]
            hp = hps[p]
            wis = []
            for h in (h0, h1):
                seg = _col(acum, lane, h) - _row(acum_t, sub, h)
                decay = jnp.exp(jnp.where(causal, seg, -jnp.inf))
                wis.append(cb * decay * _row(dt_t, sub, h))
            wcat = jnp.concatenate(wis, axis=1)
            xbd = jnp.concatenate([jnp.where(left, xp, 0.0), jnp.where(left, 0.0, xp)], axis=0)
            y_diag = _mm(wcat, xbd, NN)
            a_end = [jnp.sum(jnp.where(last & (lane == h), acum, 0.0), keepdims=True) for h in (h0, h1)]
            a_col = jnp.where(left, _col(acum, lane, h0), _col(acum, lane, h1))
            dt_col = jnp.where(left, _col(dt, lane, h0), _col(dt, lane, h1))
            to_end = jnp.exp(jnp.where(left, a_end[0], a_end[1]) - a_col) * dt_col
            states = _mm(xp * to_end, bm, TN)
            chunk_decay = jnp.where(top, jnp.exp(a_end[0]), jnp.exp(a_end[1]))
            h_out.append(chunk_decay * hp + states)
            y_off = jnp.exp(a_col) * _mm(cm, hp, NT)
            d_skip = jnp.where(left[:1], _col(prm["d_heads"], lane[:1], h0), _col(prm["d_heads"], lane[:1], h1))
            ys.append((y_diag + y_off + xp * d_skip) * jax.nn.silu(zs[p]))

    b_out = []
    for grp in range(4):
        pair = (ys[2 * grp], ys[2 * grp + 1])
        ms = sum(jnp.sum(y * y, axis=1, keepdims=True) for y in pair) / 256.0
        r = lax.rsqrt(ms + RMS_EPS)
        for j, y in enumerate(pair):
            b_out.append(y * r * prm["norm_g"][2 * grp + j])
    return a_out, b_out, h_out


def _attn_block(qps, kprev, kcur, vprev, vcur, sink_row, first):
    lane = lax.broadcasted_iota(jnp.int32, (CH, CH), 1)
    left = lane < 64
    row2 = lax.broadcasted_iota(jnp.int32, (2 * CH, CH), 0)
    key2 = lax.broadcasted_iota(jnp.int32, (2 * CH, CH), 1)
    upper = row2 < CH
    own = key2 <= jnp.where(upper, row2, row2 - CH)

    def both_halves(a):
        sw = _swap64(a)
        return [jnp.where(left, a, sw), jnp.where(left, sw, a)]

    kc, kp, vc, vp = both_halves(kcur), both_halves(kprev), both_halves(vcur), both_halves(vprev)
    outs = []
    for p in range(N_BLK):
        j = p // 4
        q2 = jnp.concatenate([jnp.where(left, qps[p], 0.0), jnp.where(left, 0.0, qps[p])], axis=0)
        s_prev = jnp.where(first, -jnp.inf, _mm(q2, kp[j], NT) * ATT_SCALE)
        s = jnp.where(own, _mm(q2, kc[j], NT) * ATT_SCALE, s_prev)
        sink = jnp.where(upper[:, :1], _col(sink_row, lane[:1], 2 * p), _col(sink_row, lane[:1], 2 * p + 1))
        m = lax.stop_gradient(jnp.maximum(jnp.max(s, axis=1, keepdims=True), sink))
        pexp = jnp.exp(s - m)
        probs = pexp / (jnp.sum(pexp, axis=1, keepdims=True) + jnp.exp(sink - m))
        o = _mm(jnp.where(own, probs, 0.0), vc[j], NN) + _mm(jnp.where(own, 0.0, probs), vp[j], NN)
        outs.append(jnp.where(left, _top_rows(o), _bottom_rows(o)))
    return outs


def _rmsnorm(x, g):
    r = lax.rsqrt(jnp.mean(x * x, axis=-1, keepdims=True) + RMS_EPS)
    return x * r * g


def rmsnorm_fwd(x, g_row, name):
    s, d = x.shape
    tm = min(512, s)

    def body(x_ref, g_ref, y_ref):
        y_ref[...] = _rmsnorm(x_ref[...], g_ref[...]).astype(bf16)

    return pl.pallas_call(
        body, name=name, grid=(s // tm,),
        in_specs=[pl.BlockSpec((tm, d), lambda i: (i, 0)), pl.BlockSpec((1, d), lambda i: (0, 0))],
        out_specs=pl.BlockSpec((tm, d), lambda i: (i, 0)),
        out_shape=jax.ShapeDtypeStruct((s, d), bf16),
        compiler_params=_cparams(("parallel",)),
    )(x, g_row)


def colsum(x, name):
    s, n = x.shape
    tm = min(512, s)

    def body(x_ref, o_ref):
        @pl.when(pl.program_id(0) == 0)
        def _():
            o_ref[...] = jnp.zeros_like(o_ref)

        o_ref[...] += jnp.sum(x_ref[...].astype(f32), axis=0, keepdims=True)

    return pl.pallas_call(
        body, name=name, grid=(s // tm,),
        in_specs=[pl.BlockSpec((tm, n), lambda i: (i, 0))],
        out_specs=pl.BlockSpec((1, n), lambda i: (0, 0)),
        out_shape=jax.ShapeDtypeStruct((1, n), f32),
        compiler_params=_cparams(("arbitrary",)),
    )(x)


def _fit(dim, want):
    if dim <= want:
        return dim
    t = want
    while dim % t:
        t -= 128
    return t


def matmul(a, b, *, dims, name, out_dtype=f32, tm=1024, tn=512, tk=8192, a_pro=None, epi=None, epi_args=(),
           out_by_col_tile=False, after=None):
    if dims == "nn" and b.ndim == 3:
        (m, k), n, tn = a.shape, b.shape[0] * b.shape[2], b.shape[2]
    elif dims == "nn":
        (m, k), n = a.shape, b.shape[1]
    elif dims == "nt":
        (m, k), n = a.shape, b.shape[0]
    else:
        (k, m), n = a.shape, b.shape[1]
    tm, tn, tk = _fit(m, tm), _fit(n, tn), _fit(k, tk)
    nk = k // tk
    if dims == "nn":
        a_spec = pl.BlockSpec((tm, tk), lambda i, j, kk: (i, kk))
        b_spec = (pl.BlockSpec((None, tk, tn), lambda i, j, kk: (j, kk, 0)) if b.ndim == 3
                  else pl.BlockSpec((tk, tn), lambda i, j, kk: (kk, j)))
        dn = NN
    elif dims == "nt":
        a_spec = pl.BlockSpec((tm, tk), lambda i, j, kk: (i, kk))
        b_spec = pl.BlockSpec((tn, tk), lambda i, j, kk: (j, kk))
        dn = NT
    else:
        a_spec = pl.BlockSpec((tk, tm), lambda i, j, kk: (kk, i))
        b_spec = pl.BlockSpec((tk, tn), lambda i, j, kk: (kk, j))
        dn = TN
    e_specs = [pl.BlockSpec((tm, tn), lambda i, j, kk: (i, j)) if kind == "tile"
               else pl.BlockSpec((1, tn), lambda i, j, kk: (0, j)) for kind, _ in epi_args]
    n_epi = len(epi_args)
    order_specs = [] if after is None else [pl.BlockSpec((8, 128), lambda i, j, kk: (0, 0))]
    order_args = [] if after is None else [after]

    def body(*refs):
        a_ref, b_ref = refs[0], refs[1]
        e_refs = refs[2:2 + n_epi]
        n_in = 2 + n_epi + len(order_args)
        o_ref = refs[n_in]
        av = a_ref[...]
        if a_pro is not None:
            av = a_pro(av)
        part = _mm(av, b_ref[...], dn)

        def finish(acc):
            if epi is not None:
                acc = epi(acc, *[r[...] for r in e_refs])
            o_ref[...] = acc.astype(out_dtype)

        if nk == 1:
            finish(part)
        else:
            acc_ref = refs[n_in + 1]
            kk = pl.program_id(2)

            @pl.when(kk == 0)
            def _():
                acc_ref[...] = part

            @pl.when(kk > 0)
            def _():
                acc_ref[...] += part

            @pl.when(kk == nk - 1)
            def _():
                finish(acc_ref[...])

    if out_by_col_tile:
        out_spec = pl.BlockSpec((None, tm, tn), lambda i, j, kk: (j, i, 0))
        out_shape = jax.ShapeDtypeStruct((n // tn, m, tn), out_dtype)
    else:
        out_spec = pl.BlockSpec((tm, tn), lambda i, j, kk: (i, j))
        out_shape = jax.ShapeDtypeStruct((m, n), out_dtype)
    return pl.pallas_call(
        body, name=name, grid=(m // tm, n // tn, nk),
        in_specs=[a_spec, b_spec] + e_specs + order_specs,
        out_specs=out_spec,
        out_shape=out_shape,
        scratch_shapes=[pltpu.VMEM((tm, tn), f32)] if nk > 1 else [],
        compiler_params=_cparams(("parallel", "parallel", "arbitrary")),
    )(a, b, *[arr for _, arr in epi_args], *order_args)


def _relu2(a):
    r = jnp.maximum(a.astype(f32), 0.0)
    return r * r


def _add(acc, t):
    return acc + t


def _add_bias(acc, t):
    return acc + t


def _add_bias_res(acc, bias, res):
    return acc + bias + res


def _times_relu2_grad(acc, a):
    return acc * (2.0 * jnp.maximum(a.astype(f32), 0.0))


def matmul_rows(a, b, *, dims, name, epi, epi_args, outs, tm=512, a_pro=None, after=None):
    m, k = a.shape
    n = b.shape[-1] if dims == "nn" else b.shape[-2]
    tm = _fit(m, tm)
    dn = NN if dims == "nn" else NT
    e_specs = [pl.BlockSpec((tm, arr.shape[1]), lambda i: (i, 0)) if kind == "tile"
               else pl.BlockSpec((1, arr.shape[1]), lambda i: (0, 0)) for kind, arr in epi_args]
    order_specs = [] if after is None else [pl.BlockSpec((8, 128), lambda i: (0, 0))]
    order_args = [] if after is None else [after]
    n_in = 2 + len(epi_args) + len(order_args)

    def body(*refs):
        av = refs[0][...]
        if a_pro is not None:
            av = a_pro(av)
        if b.ndim == 3:
            kb = b.shape[2]
            acc = sum(_mm(av[:, s * kb:(s + 1) * kb], refs[1][s], dn) for s in range(b.shape[0]))
        else:
            acc = _mm(av, refs[1][...], dn)
        vals = epi(acc, *[r[...] for r in refs[2:2 + len(epi_args)]])
        for (kind, _), o_ref, val in zip(outs, refs[n_in:], vals):
            if kind == "tile":
                o_ref[...] = val.astype(o_ref.dtype)
            else:
                @pl.when(pl.program_id(0) == 0)
                def _():
                    o_ref[...] = jnp.zeros_like(o_ref)

                o_ref[...] += val

    out_specs = [pl.BlockSpec((tm, n), lambda i: (i, 0)) if kind == "tile" else pl.BlockSpec((1, arg), lambda i: (0, 0))
                 for kind, arg in outs]
    out_shape = [jax.ShapeDtypeStruct((m, n), arg) if kind == "tile" else jax.ShapeDtypeStruct((1, arg), f32)
                 for kind, arg in outs]
    return pl.pallas_call(
        body, name=name, grid=(m // tm,),
        in_specs=[pl.BlockSpec((tm, k), lambda i: (i, 0)), pl.BlockSpec(b.shape, lambda i: (0,) * b.ndim)]
                 + e_specs + order_specs,
        out_specs=out_specs, out_shape=out_shape,
        compiler_params=_cparams(("arbitrary",)),
    )(a, b, *[arr for _, arr in epi_args], *order_args)


def _res_norm(acc, res, g):
    h = acc + res
    return h, _rmsnorm(h, g)


def _bias_res_norm(acc, bias, res, g):
    h = acc + bias + res
    return h, _rmsnorm(h, g)


def _res_norm_loss(acc, res, g, target):
    def f(h, gv):
        err = jnp.square(_rmsnorm(h, gv) - target)
        return 0.5 * jnp.sum(jnp.mean(err, axis=-1, keepdims=True), axis=0, keepdims=True)

    loss, vjp = jax.vjp(f, acc + res, g)
    dh, dg = vjp(jnp.ones_like(loss))
    return dh, dg, jnp.broadcast_to(loss, (1, 128))


def _norm_bwd_res(dy, x, g, res):
    _, vjp = jax.vjp(_rmsnorm, x, g)
    dx, dg = vjp(dy)
    return res + dx, dg


_MIXER_PARAM_SHAPES = (
    ("ln_g", (1, D_MODEL)), ("ln_b", (1, D_MODEL)), ("wm", (N_BLK, CH, CH)), ("bs_t", (CH, CH)),
    ("conv_w", (8, 2048)), ("conv_b", (1, 2048)), ("dt_bias", (1, CH)), ("a_log", (1, CH)),
    ("d_heads", (1, CH)), ("norm_g", (1, D_MODEL)),
)


def _blocks(v, n, off=0):
    return [v[:, off + i * CH: off + (i + 1) * CH] for i in range(n)]


def _split_mixer_params(vals):
    p = dict(vals)
    return {
        "ln_g": _blocks(p["ln_g"], N_BLK), "ln_b": _blocks(p["ln_b"], N_BLK),
        "wm": [p["wm"][g] for g in range(N_BLK)], "bs_t": p["bs_t"],
        "conv_w": _blocks(p["conv_w"], XBC_BLKS), "conv_b": _blocks(p["conv_b"], XBC_BLKS),
        "dt_bias": p["dt_bias"], "a_log": p["a_log"], "d_heads": p["d_heads"],
        "norm_g": _blocks(p["norm_g"], N_BLK),
    }


def _mixer_leaves(proj_ref, halo_ref, keep_halo):
    pv = proj_ref
    us = [pv[:, OFF_U + i * CH: OFF_U + (i + 1) * CH] for i in range(N_BLK)]
    vs = [pv[:, OFF_V + i * CH: OFF_V + (i + 1) * CH] for i in range(N_BLK)]
    zs = [pv[:, OFF_Z + i * CH: OFF_Z + (i + 1) * CH] for i in range(N_BLK)]
    xbcs = [pv[:, OFF_X + i * CH: OFF_X + (i + 1) * CH] for i in range(XBC_BLKS)]
    halos = [halo_ref[:, OFF_X + i * CH: OFF_X + (i + 1) * CH] * keep_halo for i in range(XBC_BLKS)]
    dtblk = pv[:, OFF_DT: OFF_DT + CH]
    return us, vs, zs, xbcs, halos, dtblk


def mixer_fwd(proj, prm):
    s = proj.shape[0]
    nc = s // CH
    names = [n for n, _ in _MIXER_PARAM_SHAPES]

    def body(proj_ref, halo_ref, *rest):
        p_refs = rest[:len(names)]
        ab_ref, hs_ref, h_ref = rest[len(names):]
        c = pl.program_id(0)

        @pl.when(c == 0)
        def _():
            h_ref[...] = jnp.zeros_like(h_ref)

        hs_ref[...] = h_ref[...]
        keep = (c > 0).astype(f32)
        us, vs, zs, xbcs, halos, dtblk = _mixer_leaves(proj_ref, halo_ref, keep)
        hps = [h_ref[i * CH:(i + 1) * CH, :] for i in range(N_BLK)]
        p = _split_mixer_params({n: r[...] for n, r in zip(names, p_refs)})
        a_out, b_out, h_out = _mixer_chunk(us, vs, zs, xbcs, halos, dtblk, hps, p)
        for i in range(N_BLK):
            ab_ref[:, i * CH:(i + 1) * CH] = a_out[i].astype(bf16)
            ab_ref[:, D_MODEL + i * CH: D_MODEL + (i + 1) * CH] = b_out[i].astype(bf16)
            h_ref[i * CH:(i + 1) * CH, :] = h_out[i]

    def const(shape):
        return pl.BlockSpec(shape, lambda c: (0,) * len(shape))

    return pl.pallas_call(
        body, name="mixer_fwd", grid=(nc,),
        in_specs=[pl.BlockSpec((CH, NP_IN), lambda c: (c, 0)),
                  pl.BlockSpec((8, NP_IN), lambda c: (jnp.maximum(c * (CH // 8) - 1, 0), 0))]
                 + [const(shp) for _, shp in _MIXER_PARAM_SHAPES],
        out_specs=[pl.BlockSpec((CH, 2 * D_MODEL), lambda c: (c, 0)),
                   pl.BlockSpec((None, D_MODEL, CH), lambda c: (c, 0, 0))],
        out_shape=[jax.ShapeDtypeStruct((s, 2 * D_MODEL), bf16), jax.ShapeDtypeStruct((nc, D_MODEL, CH), f32)],
        scratch_shapes=[pltpu.VMEM((D_MODEL, CH), f32)],
        compiler_params=_cparams(("arbitrary",)),
    )(proj, proj, *[prm[n] for n in names])


def mixer_bwd(proj, hstates, dab, prm):
    s = proj.shape[0]
    nc = s // CH
    names = [n for n, _ in _MIXER_PARAM_SHAPES]
    npar = len(names)

    def body(proj_ref, halo_ref, hs_ref, dab_ref, *rest):
        p_refs = rest[:npar]
        dproj_ref = rest[npar]
        g_refs = rest[npar + 1: 2 * npar + 1]
        dh_ref, dhalo_ref = rest[2 * npar + 1:]
        i = pl.program_id(0)
        c = nc - 1 - i

        @pl.when(i == 0)
        def _():
            dh_ref[...] = jnp.zeros_like(dh_ref)
            dhalo_ref[...] = jnp.zeros_like(dhalo_ref)
            for r in g_refs:
                r[...] = jnp.zeros_like(r)

        keep = (c > 0).astype(f32)
        us, vs, zs, xbcs, halos, dtblk = _mixer_leaves(proj_ref, halo_ref, keep)
        hps = [hs_ref[j * CH:(j + 1) * CH, :] for j in range(N_BLK)]
        pvals = {n: r[...] for n, r in zip(names, p_refs)}

        def fn(us, vs, zs, xbcs, halos, dtblk, hps, pvals):
            return _mixer_chunk(us, vs, zs, xbcs, halos, dtblk, hps, _split_mixer_params(pvals))

        _, vjp = jax.vjp(fn, us, vs, zs, xbcs, halos, dtblk, hps, pvals)
        da = [dab_ref[:, j * CH:(j + 1) * CH].astype(f32) for j in range(N_BLK)]
        db = [dab_ref[:, D_MODEL + j * CH: D_MODEL + (j + 1) * CH].astype(f32) for j in range(N_BLK)]
        dh = [dh_ref[j * CH:(j + 1) * CH, :] for j in range(N_BLK)]
        dus, dvs, dzs, dxbcs, dhalos, ddt, dhps, dp = vjp((da, db, dh))

        for j in range(N_BLK):
            dproj_ref[:, OFF_U + j * CH: OFF_U + (j + 1) * CH] = dus[j].astype(bf16)
            dproj_ref[:, OFF_V + j * CH: OFF_V + (j + 1) * CH] = dvs[j].astype(bf16)
            dproj_ref[:, OFF_Z + j * CH: OFF_Z + (j + 1) * CH] = dzs[j].astype(bf16)
            dh_ref[j * CH:(j + 1) * CH, :] = dhps[j]
        zeros_top = jnp.zeros((CH - 8, CH), f32)
        for j in range(XBC_BLKS):
            late = jnp.concatenate([zeros_top, dhalo_ref[:, j * CH:(j + 1) * CH]], axis=0)
            dproj_ref[:, OFF_X + j * CH: OFF_X + (j + 1) * CH] = (dxbcs[j] + late).astype(bf16)
        for j in range(XBC_BLKS):
            dhalo_ref[:, j * CH:(j + 1) * CH] = dhalos[j] * keep
        lane = lax.broadcasted_iota(jnp.int32, (CH, CH), 1)
        dproj_ref[:, OFF_DT: OFF_DT + CH] = jnp.where(lane < SSM_HEADS, ddt, 0.0).astype(bf16)
        dproj_ref[:, OFF_DT + CH:] = jnp.zeros((CH, NP_IN - OFF_DT - CH), bf16)
        for n, r in zip(names, g_refs):
            r[...] += dp[n]

    def const(shape):
        return pl.BlockSpec(shape, lambda i: (0,) * len(shape))

    outs = pl.pallas_call(
        body, name="mixer_bwd", grid=(nc,),
        in_specs=[pl.BlockSpec((CH, NP_IN), lambda i: (nc - 1 - i, 0)),
                  pl.BlockSpec((8, NP_IN), lambda i: (jnp.maximum((nc - 1 - i) * (CH // 8) - 1, 0), 0)),
                  pl.BlockSpec((None, D_MODEL, CH), lambda i: (nc - 1 - i, 0, 0)),
                  pl.BlockSpec((CH, 2 * D_MODEL), lambda i: (nc - 1 - i, 0))]
                 + [const(shp) for _, shp in _MIXER_PARAM_SHAPES],
        out_specs=[pl.BlockSpec((CH, NP_IN), lambda i: (nc - 1 - i, 0))]
                  + [const(shp) for _, shp in _MIXER_PARAM_SHAPES],
        out_shape=[jax.ShapeDtypeStruct((s, NP_IN), bf16)]
                  + [jax.ShapeDtypeStruct(shp, f32) for _, shp in _MIXER_PARAM_SHAPES],
        scratch_shapes=[pltpu.VMEM((D_MODEL, CH), f32), pltpu.VMEM((8, 2048), f32)],
        compiler_params=_cparams(("arbitrary",)),
    )(proj, proj, hstates, dab, *[prm[n] for n in names])
    return outs[0], dict(zip(names, outs[1:]))


_K_BLK = D_MODEL // CH
_V_BLK = _K_BLK + 1


def _attn_specs(rev, nb):
    def blk(i):
        return nb - 1 - i if rev else i

    q_spec = pl.BlockSpec((CH, D_MODEL), lambda i: (blk(i), 0))
    kv = lambda col, prev: pl.BlockSpec(
        (CH, CH), lambda i: (jnp.maximum(blk(i) - 1, 0) if prev else blk(i), col))
    return q_spec, [kv(_K_BLK, True), kv(_K_BLK, False), kv(_V_BLK, True), kv(_V_BLK, False)]


def attn_fwd(qkv, sink_row):
    s = qkv.shape[0]
    nb = s // CH

    def body(q_ref, kp_ref, kc_ref, vp_ref, vc_ref, sink_ref, o_ref):
        qps = [q_ref[:, p * CH:(p + 1) * CH] for p in range(N_BLK)]
        outs = _attn_block(qps, kp_ref[...], kc_ref[...], vp_ref[...], vc_ref[...], sink_ref[...],
                           pl.program_id(0) == 0)
        for p in range(N_BLK):
            o_ref[:, p * CH:(p + 1) * CH] = outs[p].astype(bf16)

    q_spec, kv_specs = _attn_specs(False, nb)
    return pl.pallas_call(
        body, name="attn_fwd", grid=(nb,),
        in_specs=[q_spec] + kv_specs + [pl.BlockSpec((1, CH), lambda i: (0, 0))],
        out_specs=pl.BlockSpec((CH, D_MODEL), lambda i: (i, 0)),
        out_shape=jax.ShapeDtypeStruct((s, D_MODEL), bf16),
        compiler_params=_cparams(("parallel",)),
    )(qkv, qkv, qkv, qkv, qkv, sink_row)


def attn_bwd(qkv, sink_row, dout):
    s = qkv.shape[0]
    nb = s // CH

    def body(q_ref, kp_ref, kc_ref, vp_ref, vc_ref, sink_ref, do_ref, dqkv_ref, dsink_ref, carry_ref):
        i = pl.program_id(0)
        blk = nb - 1 - i

        @pl.when(i == 0)
        def _():
            dsink_ref[...] = jnp.zeros_like(dsink_ref)
            carry_ref[...] = jnp.zeros_like(carry_ref)

        qps = [q_ref[:, p * CH:(p + 1) * CH] for p in range(N_BLK)]
        first = blk == 0
        _, vjp = jax.vjp(lambda *a: _attn_block(*a, first), qps, kp_ref[...], kc_ref[...], vp_ref[...],
                         vc_ref[...], sink_ref[...])
        dos = [do_ref[:, p * CH:(p + 1) * CH].astype(f32) for p in range(N_BLK)]
        dqs, dkp, dkc, dvp, dvc, dsink = vjp(dos)
        for p in range(N_BLK):
            dqkv_ref[:, p * CH:(p + 1) * CH] = dqs[p].astype(bf16)
        dqkv_ref[:, D_MODEL: D_MODEL + CH] = (dkc + carry_ref[0]).astype(bf16)
        dqkv_ref[:, D_MODEL + CH:] = (dvc + carry_ref[1]).astype(bf16)
        keep = jnp.logical_not(first).astype(f32)
        carry_ref[0] = dkp * keep
        carry_ref[1] = dvp * keep
        dsink_ref[...] += dsink

    q_spec, kv_specs = _attn_specs(True, nb)
    return pl.pallas_call(
        body, name="attn_bwd", grid=(nb,),
        in_specs=[q_spec] + kv_specs + [pl.BlockSpec((1, CH), lambda i: (0, 0)),
                                        pl.BlockSpec((CH, D_MODEL), lambda i: (nb - 1 - i, 0))],
        out_specs=[pl.BlockSpec((CH, QKV_DIM), lambda i: (nb - 1 - i, 0)), pl.BlockSpec((1, CH), lambda i: (0, 0))],
        out_shape=[jax.ShapeDtypeStruct((s, QKV_DIM), bf16), jax.ShapeDtypeStruct((1, CH), f32)],
        scratch_shapes=[pltpu.VMEM((2, CH, CH), f32)],
        compiler_params=_cparams(("arbitrary",)),
    )(qkv, qkv, qkv, qkv, qkv, sink_row, dout)


def adamw(w, g, m, v, name):
    def body(w_ref, g_ref, m_ref, v_ref, d_ref, nm_ref, nv_ref):
        gv = g_ref[...]
        nm = ADAM_B1 * m_ref[...] + (1.0 - ADAM_B1) * gv
        nv = ADAM_B2 * v_ref[...] + (1.0 - ADAM_B2) * jnp.square(gv)
        m_hat = nm / (1.0 - ADAM_B1 ** ADAM_STEP)
        v_hat = nv / (1.0 - ADAM_B2 ** ADAM_STEP)
        d_ref[...] = -ADAM_LR * (m_hat / (jnp.sqrt(v_hat) + ADAM_EPS) + ADAM_WD * w_ref[...])
        nm_ref[...] = nm
        nv_ref[...] = nv

    out_shape = [jax.ShapeDtypeStruct(w.shape, f32)] * 3
    if w.ndim == 3 and w.shape[1] == 1:
        tr = max(t for t in range(1, 129) if w.shape[0] % t == 0)
        tile = pl.BlockSpec((tr, 1, w.shape[2]), lambda i: (i, 0, 0))
        return pl.pallas_call(
            body, name=name, grid=(w.shape[0] // tr,),
            in_specs=[tile] * 4, out_specs=[tile] * 3, out_shape=out_shape,
            compiler_params=_cparams(("parallel",)),
        )(w, g, m, v)
    if w.ndim == 3 and w.shape[1] % 256 == 0:
        tile = pl.BlockSpec((None, 256, w.shape[2]), lambda l, i: (l, i, 0))
        return pl.pallas_call(
            body, name=name, grid=(w.shape[0], w.shape[1] // 256),
            in_specs=[tile] * 4, out_specs=[tile] * 3, out_shape=out_shape,
            compiler_params=_cparams(("parallel", "parallel")),
        )(w, g, m, v)
    return pl.pallas_call(body, name=name, in_specs=[_VMEM] * 4, out_specs=[_VMEM] * 3, out_shape=out_shape,
                          compiler_params=_cparams())(w, g, m, v)


_MESH = pl.DeviceIdType.MESH
_ANY = pl.BlockSpec(memory_space=pl.ANY)
_VMEM = pl.BlockSpec(memory_space=pltpu.VMEM)


def _place():
    x, y, c = lax.axis_index("x"), lax.axis_index("y"), lax.axis_index("c")
    chips = [(1 - x, y), (x, 1 - y), (1 - x, 1 - y)]
    return x, y, c, 2 * x + y, chips, [2 * cx + cy for cx, cy in chips]


def _half(c, rows):
    return pl.ds(pl.multiple_of(c * (rows // 2), 16), rows // 2)


def _step_rows(rows):
    return max(t for t in range(16, 641, 16) if rows % t == 0)


def place_shard(b, slot, name, dtype=bf16):
    r, c = b.shape
    tr = _step_rows(r)

    def body(slot_ref, b_ref, o_ref):
        o_ref[...] = b_ref[...].astype(dtype)

    return pl.pallas_call(
        body, name=name,
        grid_spec=pltpu.PrefetchScalarGridSpec(
            num_scalar_prefetch=1, grid=(r // tr,),
            in_specs=[pl.BlockSpec((tr, c), lambda i, s: (i, 0))],
            out_specs=pl.BlockSpec((None, tr, c), lambda i, s: (s[0], i, 0))),
        out_shape=jax.ShapeDtypeStruct((N_CHIPS, r, c), dtype),
        compiler_params=_cparams(("parallel",)),
    )(slot, b)


_HBM = pl.BlockSpec(memory_space=pltpu.HBM)
_SEM = pl.BlockSpec(memory_space=pltpu.SEMAPHORE)
_EFFECT = pltpu.SideEffectType.DATAFLOW_SIDE_EFFECTING


def _gather_ici_copies(bufs, send_sems, recv_sems):
    x, y, c, me, chips, chip_idx = _place()
    return [pltpu.make_async_remote_copy(
        src_ref=buf.at[me, _half(c, buf.shape[1])], dst_ref=buf.at[chip_idx[j], _half(c, buf.shape[1])],
        send_sem=send_sems.at[3 * k + j], recv_sem=recv_sems.at[3 * k + j],
        device_id=(*chips[j], c), device_id_type=_MESH) for j in range(3) for k, buf in enumerate(bufs)]


def gather_start(groups):
    sizes = [len(g) for g in groups]
    flat = [b for g in groups for b in g]
    n = len(flat)

    def body(*refs):
        bufs, sems = refs[:n], refs[n:n + 2 * len(groups)]
        x, y, c, me, chips, chip_idx = _place()
        lo = 0
        for gi, size in enumerate(sizes):
            for j in range(3):
                for k, buf in enumerate(bufs[lo:lo + size]):
                    mine = buf.at[me, _half(c, buf.shape[1])]
                    pltpu.make_async_remote_copy(
                        src_ref=mine, dst_ref=mine, send_sem=sems[2 * gi].at[3 * k + j],
                        recv_sem=sems[2 * gi + 1].at[3 * k + j], device_id=(*chips[j], c),
                        device_id_type=_MESH).start()
            lo += size

    sem_shapes = [pltpu.SemaphoreType.DMA((3 * size,)) for size in sizes for _ in range(2)]
    outs = pl.pallas_call(
        body, name="gather_start",
        out_shape=(*sem_shapes, *[pltpu.HBM(b.shape, b.dtype) for b in flat]),
        in_specs=[_HBM] * n, out_specs=(*[_SEM] * len(sem_shapes), *[_HBM] * n),
        input_output_aliases={i: len(sem_shapes) + i for i in range(n)},
        compiler_params=pltpu.CompilerParams(has_side_effects=_EFFECT),
    )(*[pltpu.with_memory_space_constraint(b, pltpu.HBM) for b in flat])
    sems = [(outs[2 * gi], outs[2 * gi + 1]) for gi in range(len(groups))]
    thru, lo = [], len(sem_shapes)
    for size in sizes:
        thru.append(list(outs[lo:lo + size]))
        lo += size
    return sems, thru


def gather_wait(bufs, sems, after, tag):
    n = len(bufs)

    def body(*refs):
        for cp in _gather_ici_copies(refs[:n], refs[n], refs[n + 1]):
            cp.wait_send()
            cp.wait_recv()

    extra = [] if after is None else [after]
    return list(pl.pallas_call(
        body, name=f"gather_wait_{tag}",
        out_shape=[pltpu.HBM(b.shape, b.dtype) for b in bufs],
        in_specs=[_HBM] * n + [_SEM, _SEM] + [_ANY] * len(extra), out_specs=[_HBM] * n,
        input_output_aliases={i: i for i in range(n)},
        compiler_params=pltpu.CompilerParams(has_side_effects=_EFFECT),
    )(*bufs, *sems, *extra))


def gather_forward(bufs, tag):
    n = len(bufs)

    def body(*refs):
        out_refs = refs[n:2 * n]
        send_sems, recv_sems = refs[2 * n:]
        x, y, c, me, chips, chip_idx = _place()

        def copy(k, j, half):
            part = out_refs[k].at[chip_idx[j], _half(half, out_refs[k].shape[1])]
            return pltpu.make_async_remote_copy(
                src_ref=part, dst_ref=part, send_sem=send_sems.at[3 * k + j], recv_sem=recv_sems.at[3 * k + j],
                device_id=(x, y, 1 - c), device_id_type=_MESH)

        sends = [copy(k, j, c) for j in range(3) for k in range(n)]
        for cp in sends:
            cp.start()
        for j in range(3):
            for k in range(n):
                copy(k, j, 1 - c).wait_recv()
        for cp in sends:
            cp.wait_send()

    return list(pl.pallas_call(
        body, name=f"gather_forward_{tag}",
        out_shape=[jax.ShapeDtypeStruct(b.shape, b.dtype) for b in bufs],
        in_specs=[_ANY] * n, out_specs=[_ANY] * n, input_output_aliases={i: i for i in range(n)},
        scratch_shapes=[pltpu.SemaphoreType.DMA((3 * n,)), pltpu.SemaphoreType.DMA((3 * n,))],
    )(*bufs))


def exchange_halves(bufs, tag):
    n = len(bufs)

    def body(*refs):
        g_refs, out_refs = refs[:n], refs[n:2 * n]
        send_sems, recv_sems = refs[2 * n:]
        x, y, c, *_ = _place()
        cps = [pltpu.make_async_remote_copy(
            src_ref=g_refs[b].at[:, _half(1 - c, g_refs[b].shape[1])], dst_ref=out_refs[b],
            send_sem=send_sems.at[b], recv_sem=recv_sems.at[b], device_id=(x, y, 1 - c), device_id_type=_MESH)
            for b in range(n)]
        for cp in cps:
            cp.start()
        for cp in cps:
            cp.wait()

    return pl.pallas_call(
        body, name=f"exchange_halves_{tag}",
        out_shape=[jax.ShapeDtypeStruct((N_CHIPS, b.shape[1] // 2, b.shape[2]), b.dtype) for b in bufs],
        in_specs=[_ANY] * n, out_specs=[_ANY] * n,
        scratch_shapes=[pltpu.SemaphoreType.DMA((n,)), pltpu.SemaphoreType.DMA((n,))],
    )(*bufs)


def add_halves(g, got, c_idx, name):
    hr, cols = got.shape[1], got.shape[2]
    tr = _step_rows(hr)
    steps = hr // tr

    def body(c_ref, g_ref, got_ref, o_ref):
        o_ref[...] = (g_ref[...].astype(f32) + got_ref[...].astype(f32)).astype(bf16)

    return pl.pallas_call(
        body, name=name,
        grid_spec=pltpu.PrefetchScalarGridSpec(
            num_scalar_prefetch=1, grid=(N_CHIPS, steps),
            in_specs=[pl.BlockSpec((None, tr, cols), lambda s, i, c: (s, c[0] * steps + i, 0)),
                      pl.BlockSpec((None, tr, cols), lambda s, i, c: (s, i, 0))],
            out_specs=pl.BlockSpec((None, tr, cols), lambda s, i, c: (s, i, 0))),
        out_shape=jax.ShapeDtypeStruct(got.shape, bf16),
        compiler_params=_cparams(("parallel", "parallel")),
    )(c_idx, g, got)


def sum_chips(t, got, place_idx, name):
    hr, cols = t.shape[1], t.shape[2]
    tr = _step_rows(hr)
    steps = hr // tr

    def body(idx_ref, t_ref, got_ref, o_ref):
        acc = t_ref[...].astype(f32)
        for j in range(3):
            acc = acc + got_ref[j].astype(f32)
        o_ref[...] = acc

    return pl.pallas_call(
        body, name=name,
        grid_spec=pltpu.PrefetchScalarGridSpec(
            num_scalar_prefetch=1, grid=(steps,),
            in_specs=[pl.BlockSpec((None, tr, cols), lambda i, idx: (idx[0], i, 0)),
                      pl.BlockSpec((3, tr, cols), lambda i, idx: (0, i, 0))],
            out_specs=pl.BlockSpec((tr, cols), lambda i, idx: (idx[1] * steps + i, 0))),
        out_shape=jax.ShapeDtypeStruct((2 * hr, cols), f32),
        compiler_params=_cparams(("parallel",)),
    )(place_idx, t, got)


def _share_copies(refs, send_sems, recv_sems):
    x, y, c, *_ = _place()
    return [pltpu.make_async_remote_copy(
        src_ref=ref.at[_half(c, ref.shape[0])], dst_ref=ref.at[_half(c, ref.shape[0])], send_sem=send_sems.at[b],
        recv_sem=recv_sems.at[b], device_id=(x, y, 1 - c), device_id_type=_MESH) for b, ref in enumerate(refs)]


def share_start(bufs, tag):
    n = len(bufs)

    def body(*refs):
        for cp in _share_copies(refs[:n], refs[n], refs[n + 1]):
            cp.start()
        token = refs[-1]
        token[...] = jnp.zeros_like(token)

    outs = pl.pallas_call(
        body, name=f"share_start_{tag}",
        out_shape=(pltpu.SemaphoreType.DMA((n,)), pltpu.SemaphoreType.DMA((n,)),
                   *[pltpu.HBM(b.shape, b.dtype) for b in bufs], jax.ShapeDtypeStruct((8, 128), f32)),
        in_specs=[_HBM] * n, out_specs=(_SEM, _SEM, *[_HBM] * n, _VMEM),
        input_output_aliases={i: 2 + i for i in range(n)},
        compiler_params=pltpu.CompilerParams(has_side_effects=_EFFECT),
    )(*[pltpu.with_memory_space_constraint(b, pltpu.HBM) for b in bufs])
    return (outs[0], outs[1], list(outs[2:2 + n])), outs[-1]


def share_wait(send_sems, recv_sems, bufs, after, tag):
    n = len(bufs)

    def body(*refs):
        x, y, c, *_ = _place()
        for b, ref in enumerate(refs[:n]):
            cp = pltpu.make_async_remote_copy(
                src_ref=ref.at[_half(c, ref.shape[0])], dst_ref=ref.at[_half(1 - c, ref.shape[0])],
                send_sem=refs[n].at[b], recv_sem=refs[n + 1].at[b], device_id=(x, y, 1 - c), device_id_type=_MESH)
            cp.wait_send()
            cp.wait_recv()

    return list(pl.pallas_call(
        body, name=f"share_wait_{tag}",
        out_shape=[pltpu.HBM(b.shape, b.dtype) for b in bufs],
        in_specs=[_HBM] * n + [_SEM, _SEM, _ANY], out_specs=[_HBM] * n,
        input_output_aliases={i: i for i in range(n)},
        compiler_params=pltpu.CompilerParams(has_side_effects=_EFFECT),
    )(*bufs, send_sems, recv_sems, after))


def _scatter_copies(t_refs, land_refs, send_sems, recv_sems):
    x, y, c, me, chips, chip_idx = _place()
    return [pltpu.make_async_remote_copy(
        src_ref=t_refs[b].at[chip_idx[j]], dst_ref=land_refs[b].at[j], send_sem=send_sems.at[3 * b + j],
        recv_sem=recv_sems.at[3 * b + j], device_id=(*chips[j], c), device_id_type=_MESH)
        for j in range(3) for b in range(len(t_refs))]


def scatter_start(ts, tag):
    n = len(ts)
    lands = [lax.empty((3,) + t.shape[1:], t.dtype) for t in ts]

    def body(*refs):
        for cp in _scatter_copies(refs[:n], refs[n:2 * n], refs[2 * n], refs[2 * n + 1]):
            cp.start()
        token = refs[-1]
        token[...] = jnp.zeros_like(token)

    hbm = [pltpu.HBM(a.shape, a.dtype) for a in (*ts, *lands)]
    outs = pl.pallas_call(
        body, name=f"scatter_start_{tag}",
        out_shape=(pltpu.SemaphoreType.DMA((3 * n,)), pltpu.SemaphoreType.DMA((3 * n,)), *hbm,
                   jax.ShapeDtypeStruct((8, 128), f32)),
        in_specs=[_HBM] * (2 * n), out_specs=(_SEM, _SEM, *[_HBM] * (2 * n), _VMEM),
        input_output_aliases={i: 2 + i for i in range(2 * n)},
        compiler_params=pltpu.CompilerParams(has_side_effects=_EFFECT),
    )(*[pltpu.with_memory_space_constraint(a, pltpu.HBM) for a in (*ts, *lands)])
    return outs[0], outs[1], list(outs[2:2 + n]), list(outs[2 + n:2 + 2 * n]), outs[-1]


def scatter_wait(send_sems, recv_sems, ts, lands, after, tag):
    n = len(ts)

    def body(*refs):
        for cp in _scatter_copies(refs[:n], refs[n:2 * n], refs[2 * n], refs[2 * n + 1]):
            cp.wait_send()
            cp.wait_recv()

    outs = pl.pallas_call(
        body, name=f"scatter_wait_{tag}",
        out_shape=[pltpu.HBM(a.shape, a.dtype) for a in (*ts, *lands)],
        in_specs=[_HBM] * (2 * n) + [_SEM, _SEM, _ANY], out_specs=[_HBM] * (2 * n),
        input_output_aliases={i: i for i in range(2 * n)},
        compiler_params=pltpu.CompilerParams(has_side_effects=_EFFECT),
    )(*ts, *lands, send_sems, recv_sems, after)
    return list(outs[:n]), list(outs[n:])


N_SENDERS = 7


def _direct_copies(g_refs, land_refs, send_sems, recv_sems):
    x, y, c, me, chips, chip_idx = _place()
    cps = []
    for b, (g, land) in enumerate(zip(g_refs, land_refs)):
        rows, base = g.shape[1], N_SENDERS * b
        cps.append(pltpu.make_async_remote_copy(
            src_ref=g.at[me, _half(1 - c, rows)], dst_ref=land.at[0], send_sem=send_sems.at[base],
            recv_sem=recv_sems.at[base], device_id=(x, y, 1 - c), device_id_type=_MESH))
        for j in range(3):
            for core in range(2):
                cps.append(pltpu.make_async_remote_copy(
                    src_ref=g.at[chip_idx[j], _half(core, rows)], dst_ref=land.at[1 + 2 * j + c],
                    send_sem=send_sems.at[base + 1 + 2 * j + core], recv_sem=recv_sems.at[base + 1 + 2 * j + c],
                    device_id=(*chips[j], core), device_id_type=_MESH))
    return cps


def direct_start(gs, tag):
    n = len(gs)
    lands = [lax.empty((N_SENDERS, g.shape[1] // 2, g.shape[2]), g.dtype) for g in gs]

    def body(*refs):
        for cp in _direct_copies(refs[:n], refs[n:2 * n], refs[2 * n], refs[2 * n + 1]):
            cp.start()
        token = refs[-1]
        token[...] = jnp.zeros_like(token)

    hbm = [pltpu.HBM(a.shape, a.dtype) for a in (*gs, *lands)]
    outs = pl.pallas_call(
        body, name=f"direct_start_{tag}",
        out_shape=(pltpu.SemaphoreType.DMA((N_SENDERS * n,)), pltpu.SemaphoreType.DMA((N_SENDERS * n,)), *hbm,
                   jax.ShapeDtypeStruct((8, 128), f32)),
        in_specs=[_HBM] * (2 * n), out_specs=(_SEM, _SEM, *[_HBM] * (2 * n), _VMEM),
        input_output_aliases={i: 2 + i for i in range(2 * n)},
        compiler_params=pltpu.CompilerParams(has_side_effects=_EFFECT),
    )(*[pltpu.with_memory_space_constraint(a, pltpu.HBM) for a in (*gs, *lands)])
    return outs[0], outs[1], list(outs[2:2 + n]), list(outs[2 + n:2 + 2 * n]), outs[-1]


def direct_wait(send_sems, recv_sems, gs, lands, after, tag):
    n = len(gs)

    def body(*refs):
        g_refs, land_refs, sends, recvs = refs[:n], refs[n:2 * n], refs[2 * n], refs[2 * n + 1]
        for b in range(n):
            for k in range(N_SENDERS):
                cp = pltpu.make_async_remote_copy(
                    src_ref=g_refs[b].at[0, _half(0, g_refs[b].shape[1])], dst_ref=land_refs[b].at[k],
                    send_sem=sends.at[N_SENDERS * b + k], recv_sem=recvs.at[N_SENDERS * b + k],
                    device_id=_place()[:3], device_id_type=_MESH)
                cp.wait_send()
                cp.wait_recv()

    outs = pl.pallas_call(
        body, name=f"direct_wait_{tag}",
        out_shape=[pltpu.HBM(a.shape, a.dtype) for a in (*gs, *lands)],
        in_specs=[_HBM] * (2 * n) + [_SEM, _SEM, _ANY], out_specs=[_HBM] * (2 * n),
        input_output_aliases={i: i for i in range(2 * n)},
        compiler_params=pltpu.CompilerParams(has_side_effects=_EFFECT),
    )(*gs, *lands, send_sems, recv_sems, after)
    return list(outs[:n]), list(outs[n:])


def sum_senders(g, lands, place_idx, name):
    hr, cols = lands.shape[1], lands.shape[2]
    tr = _step_rows(hr)
    steps = hr // tr

    def body(idx_ref, g_ref, land_ref, o_ref):
        acc = g_ref[...].astype(f32)
        for k in range(N_SENDERS):
            acc = acc + land_ref[k].astype(f32)
        o_ref[...] = acc

    return pl.pallas_call(
        body, name=name,
        grid_spec=pltpu.PrefetchScalarGridSpec(
            num_scalar_prefetch=1, grid=(steps,),
            in_specs=[pl.BlockSpec((None, tr, cols), lambda i, idx: (idx[0], idx[1] * steps + i, 0)),
                      pl.BlockSpec((N_SENDERS, tr, cols), lambda i, idx: (0, i, 0))],
            out_specs=pl.BlockSpec((tr, cols), lambda i, idx: (idx[1] * steps + i, 0))),
        out_shape=jax.ShapeDtypeStruct((2 * hr, cols), f32),
        compiler_params=_cparams(("parallel",)),
    )(place_idx, g, lands)


class GradReducer:
    def __init__(self, c_idx, place_idx):
        self.c_idx, self.place_idx = c_idx, place_idx

    def start(self, bufs, tag, direct=False):
        if direct:
            send_sems, recv_sems, gs, lands, token = direct_start(bufs, tag)
            return (True, send_sems, recv_sems, gs, lands), token
        got = exchange_halves(bufs, tag)
        ts = [add_halves(b, g, self.c_idx, f"add_halves_{tag}{i}") for i, (b, g) in enumerate(zip(bufs, got))]
        send_sems, recv_sems, ts, lands, token = scatter_start(ts, tag)
        return (False, send_sems, recv_sems, ts, lands), token

    def finish(self, state, after, tag):
        direct, *flight = state
        if direct:
            gs, lands = direct_wait(*flight, after, tag)
            sums = [sum_senders(g, l, self.place_idx, f"sum_senders_{tag}{i}") for i, (g, l) in enumerate(zip(gs, lands))]
        else:
            ts, lands = scatter_wait(*flight, after, tag)
            sums = [sum_chips(t, l, self.place_idx, f"sum_chips_{tag}{i}") for i, (t, l) in enumerate(zip(ts, lands))]
        return share_start(sums, tag)

    def collect(self, pending, after, tag):
        return share_wait(*pending, after, tag)


def allreduce_small(sp):
    def body(s_ref, out_ref, gather_ref, send_sems, recv_sems):
        x, y, c, me, chips, chip_idx = _place()
        sibling = (x, y, 1 - c)

        def copy(k, chip, core, to, src=None):
            dst = gather_ref.at[2 * chip + core]
            return pltpu.make_async_remote_copy(
                src_ref=dst if src is None else src, dst_ref=dst, send_sem=send_sems.at[k],
                recv_sem=recv_sems.at[k], device_id=to, device_id_type=_MESH)

        first = [copy(0, me, c, sibling, src=s_ref)]
        first += [copy(1 + j, me, c, (*chips[j], c), src=s_ref) for j in range(3)]
        for cp in first:
            cp.start()
        gather_ref[2 * me + c] = s_ref[...]
        passed = [copy(4 + j, chip_idx[j], c, sibling) for j in range(3)]
        for j in range(3):
            copy(1 + j, chip_idx[j], c, sibling).wait_recv()
            passed[j].start()
        copy(0, me, 1 - c, sibling).wait_recv()
        for j in range(3):
            copy(4 + j, chip_idx[j], 1 - c, sibling).wait_recv()
        for cp in first + passed:
            cp.wait_send()
        acc = gather_ref[0]
        for d in range(1, 2 * N_CHIPS):
            acc = acc + gather_ref[d]
        out_ref[...] = acc

    return pl.pallas_call(
        body, name="allreduce_small",
        out_shape=jax.ShapeDtypeStruct(sp.shape, sp.dtype),
        in_specs=[_VMEM], out_specs=_VMEM,
        scratch_shapes=[pltpu.VMEM((2 * N_CHIPS,) + sp.shape, sp.dtype),
                        pltpu.SemaphoreType.DMA((7,)), pltpu.SemaphoreType.DMA((7,))],
        compiler_params=_cparams(),
    )(sp)


def _n_rows(shape):
    n = 1
    for d in shape:
        n *= d
    return 8 * (-(-n // 8192))


def _pack(arrays, total_rows):
    parts = []
    for a in arrays:
        flat = a.reshape(-1)
        parts.append(jnp.pad(flat, (0, 1024 * _n_rows(a.shape) - flat.shape[0])).reshape(-1, 1024))
    rows = jnp.concatenate(parts, axis=0)
    return jnp.pad(rows, ((0, total_rows - rows.shape[0]), (0, 0)))


def _unpack(packed, shapes):
    out, r = [], 0
    for shp in shapes:
        n = 1
        for d in shp:
            n *= d
        nr = _n_rows(shp)
        out.append(packed[r:r + nr].reshape(-1)[:n].reshape(shp))
        r += nr
    return out


_COLUMN_SHARDED = ("w_in_even", "w_qkv")
IN_SHARD, IN_PAD = 1284, 1408
QKV_SHARD, QKV_PAD = 320, 384


def _lane_padded(a, cols):
    return jnp.pad(a, ((0, 0), (0, cols - a.shape[1])))


_SMALL_SHAPES = (
    ("norm_mix_g", (2, 1024)), ("norm_mlp_g", (2, 1024)), ("final_norm_g", (1024,)), ("gm_ln_g", (1, 1024)),
    ("gm_ln_b", (1, 1024)), ("gm_w_s", (1, 8, 128, 128)), ("gm_b_s", (1, 8, 128)), ("ssm_conv_b", (1, 2048)),
    ("ssm_dt_bias", (1, 16)), ("ssm_a_log", (1, 16)), ("ssm_d", (1, 16)), ("ssm_norm_g", (1, 1024)),
    ("attn_sinks", (1, 16)), ("ssm_conv_w", (1, 4, 2048)), ("b_qkv", (1, 1280)), ("b_o", (1, 1024)),
)
_N_REPLICATED = 13
_SHARDED_SMALL = (("ssm_conv_w", 2, 512), ("b_qkv", 1, 320), ("b_o", 1, 256))
_SHARD_PACK_ROWS = 32


def _cols_by_owner(a):
    return a.transpose(1, 0, 2).reshape(a.shape[1], -1)


class WeightGatherer:
    def __init__(self, w, chip_idx):
        place = lambda tag, b, dtype=bf16: place_shard(b, chip_idx, f"place_shard_{tag}", dtype)
        self.sems, self.bufs = gather_start([
            [place("in", _lane_padded(w["w_in_even"][0], IN_PAD)),
             place("small", _pack([w[n] for n, _, _ in _SHARDED_SMALL], _SHARD_PACK_ROWS), f32)],
            [place("out", w["w_out_even"][0]), place("up0", w["w_up"][0]), place("down0", w["w_down"][0])],
            [place("qkv", _lane_padded(w["w_qkv"][0], QKV_PAD)), place("o", w["w_o"][0]),
             place("up1", w["w_up"][1]), place("down1", w["w_down"][1])],
        ])

    def _group(self, gi, after, tag):
        return gather_forward(gather_wait(self.bufs[gi], self.sems[gi], after, tag), tag)

    def mixer_in(self):
        g, small = self._group(0, None, "in")
        shard_shapes = [tuple(width if i == axis else d for i, d in enumerate(dict(_SMALL_SHAPES)[n]))
                        for n, axis, width in _SHARDED_SMALL]
        per_chip = [_unpack(small[s], shard_shapes) for s in range(N_CHIPS)]
        full = {n: jnp.concatenate([per_chip[s][i] for s in range(N_CHIPS)], axis=axis)
                for i, (n, axis, _) in enumerate(_SHARDED_SMALL)}
        w_in = jnp.concatenate([g[s, :, :IN_SHARD] for s in range(N_CHIPS)], axis=1)
        return _lane_padded(w_in, NP_IN), full

    def layer0(self, after):
        w_out, w_up, w_down = self._group(1, after, "l0")
        return w_out.reshape(2048, 1024), w_up, w_down.reshape(4096, 1024)

    def layer1(self, after):
        q, w_o, w_up, w_down = self._group(2, after, "l1")
        w_qkv = jnp.concatenate([q[s, :, :QKV_SHARD] for s in range(N_CHIPS)], axis=1)
        return w_qkv, w_o.reshape(1024, 1024), w_up, w_down.reshape(4096, 1024)


def _row2(v):
    return v.reshape(1, -1)


def _lane_pad(v):
    return jnp.pad(v, ((0, 0), (0, CH - v.shape[1])))


_H_AND_NORM = (("tile", f32), ("tile", bf16))
_DX_AND_DG = (("tile", f32), ("sum", D_MODEL))


def _mlp_bwd(dh_out, h, g_row, y, a, w_up, w_down, tag, after=None):
    da = matmul(dh_out, w_down, dims="nt", name=f"mlp_da{tag}", out_dtype=bf16, tn=1024,
                epi=_times_relu2_grad, epi_args=(("tile", a),), after=after)
    dw_down = matmul(a, dh_out, dims="tn", name=f"mlp_dwdown{tag}", out_dtype=bf16, a_pro=_relu2)
    dw_up = matmul(y, da, dims="tn", name=f"mlp_dwup{tag}", out_dtype=bf16, tn=1024, out_by_col_tile=True)
    dh, dg = matmul_rows(da, w_up, dims="nt", name=f"mlp_dy{tag}", epi=_norm_bwd_res,
                         epi_args=(("tile", h), ("row", g_row), ("tile", dh_out)), outs=_DX_AND_DG)
    return dh, dg, dw_up, dw_down


def _by_owner(a):
    return a.reshape(N_CHIPS, a.shape[0] // N_CHIPS, a.shape[1])


def _col_shards(a, shard, padded):
    return jnp.stack([_lane_padded(a[:, shard * s: shard * (s + 1)], padded) for s in range(N_CHIPS)])


def _local_step(x, target, weights, sm, reducer):
    w_up, w_down = [None, None], [None, None]
    w_in_p, sharded_small = weights.mixer_in()
    sm = {**sm, **sharded_small}
    mix_g = [_row2(sm["norm_mix_g"][i]) for i in range(2)]
    mlp_g = [_row2(sm["norm_mlp_g"][i]) for i in range(2)]
    mixer_prm = {
        "ln_g": sm["gm_ln_g"], "ln_b": sm["gm_ln_b"], "wm": sm["gm_w_s"][0],
        "bs_t": jnp.pad(sm["gm_b_s"][0].T, ((0, 0), (0, CH - N_BLK))),
        "conv_w": jnp.pad(sm["ssm_conv_w"][0], ((0, 4), (0, 0))), "conv_b": sm["ssm_conv_b"],
        "dt_bias": _lane_pad(sm["ssm_dt_bias"]), "a_log": _lane_pad(sm["ssm_a_log"]),
        "d_heads": _lane_pad(sm["ssm_d"]), "norm_g": sm["ssm_norm_g"],
    }
    sink_row = _lane_pad(sm["attn_sinks"])

    y0 = rmsnorm_fwd(x, mix_g[0], "mix_norm0")
    proj = matmul(y0, w_in_p, dims="nn", name="in_proj", tn=768)
    ab, hstates = mixer_fwd(proj, mixer_prm)
    w_out, w_up[0], w_down[0] = weights.layer0(ab)
    h1, y1 = matmul_rows(ab, w_out, dims="nn", name="out_proj", epi=_res_norm,
                         epi_args=(("tile", x), ("row", mlp_g[0])), outs=_H_AND_NORM)
    a1 = matmul(y1, w_up[0], dims="nn", name="mlp_up0", out_dtype=bf16, tn=1024)
    w_qkv, w_o, w_up[1], w_down[1] = weights.layer1(a1)
    h2, y2 = matmul_rows(a1, w_down[0], dims="nn", name="mlp_down0", a_pro=_relu2, epi=_res_norm,
                         epi_args=(("tile", h1), ("row", mix_g[1])), outs=_H_AND_NORM)
    qkv = matmul(y2, w_qkv, dims="nn", name="qkv_proj", tn=QKV_DIM, epi=_add_bias, epi_args=(("row", sm["b_qkv"]),))
    att = attn_fwd(qkv, sink_row)
    h3, y3 = matmul_rows(att, w_o, dims="nn", name="o_proj", epi=_bias_res_norm,
                         epi_args=(("row", sm["b_o"]), ("tile", h2), ("row", mlp_g[1])), outs=_H_AND_NORM)
    a3 = matmul(y3, w_up[1], dims="nn", name="mlp_up1", out_dtype=bf16, tn=1024)
    dh4, dg_final, loss = matmul_rows(
        a3, w_down[1], dims="nn", name="mlp_down1", a_pro=_relu2, epi=_res_norm_loss,
        epi_args=(("tile", h3), ("row", _row2(sm["final_norm_g"])), ("tile", target)),
        outs=(("tile", f32), ("sum", D_MODEL), ("sum", 128)))

    dh3, dg_mlp1, dw_up1, dw_down1 = _mlp_bwd(dh4, h3, mlp_g[1], y3, a3, w_up[1], w_down[1], 1)
    db_o = colsum(dh3, "db_o")
    datt = matmul(dh3, w_o, dims="nt", name="attn_dout", out_dtype=bf16)
    dw_o = matmul(att, dh3, dims="tn", name="dw_o", out_dtype=bf16)
    dqkv, dsink = attn_bwd(qkv, sink_row, datt)
    db_qkv = colsum(dqkv, "db_qkv")
    dw_qkv = matmul(y2, dqkv, dims="tn", name="dw_qkv", out_dtype=bf16, tn=QKV_DIM)
    dh2, dg_mix1 = matmul_rows(dqkv, w_qkv, dims="nt", name="dy_qkv", epi=_norm_bwd_res,
                               epi_args=(("tile", h2), ("row", mix_g[1]), ("tile", dh3)), outs=_DX_AND_DG)
    layer1 = [jnp.concatenate([_by_owner(dw_o), dw_up1, _by_owner(dw_down1)], axis=1),
              _col_shards(dw_qkv, QKV_SHARD, QKV_PAD)]
    flight1, token1 = reducer.start(layer1, "l1", direct=True)
    dh1, dg_mlp0, dw_up0, dw_down0 = _mlp_bwd(dh2, h1, mlp_g[0], y1, a1, w_up[0], w_down[0], 0, after=token1)
    pending1, shared1 = reducer.finish(flight1, dh1, "l1")
    dw_out = matmul(ab, dh1, dims="tn", name="dw_out", out_dtype=bf16, after=shared1)
    flight0, token0 = reducer.start(
        [jnp.concatenate([dw_up0, _by_owner(dw_down0), _by_owner(dw_out)], axis=1)], "l0", direct=True)
    dab = matmul(dh1, w_out, dims="nt", name="mixer_dout", tn=1024, after=token0)
    dproj, dmix = mixer_bwd(proj, hstates, dab, mixer_prm)
    dw_in_p = matmul(y0, dproj, dims="tn", name="dw_in", out_dtype=bf16, tn=768)
    pending0, shared0 = reducer.finish(flight0, dw_in_p, "l0")
    flight_in, token_in = reducer.start([_col_shards(dw_in_p, IN_SHARD, IN_PAD)], "in")
    dx, dg_mix0 = matmul_rows(dproj, w_in_p, dims="nt", name="dy_in", tm=256, epi=_norm_bwd_res,
                              epi_args=(("tile", x), ("row", mix_g[0]), ("tile", dh1)), outs=_DX_AND_DG,
                              after=token_in + shared0)
    pending_in, _ = reducer.finish(flight_in, dx, "in")
    r_l1, r_qkv = reducer.collect(pending1, dx, "l1")
    (r_l0,) = reducer.collect(pending0, dx, "l0")
    (r_in,) = reducer.collect(pending_in, dx, "in")
    reduced = {
        "w_out_even": r_l0[None, 2048:], "w_in_even": r_in[None, :, :IN_SHARD], "w_qkv": r_qkv[None, :, :QKV_SHARD],
        "w_o": r_l1[None, :256], "w_up": jnp.stack([r_l0[:1024], r_l1[256:1280]]),
        "w_down": jnp.stack([r_l0[1024:2048], r_l1[1280:]]),
    }

    small_grads = {
        "norm_mix_g": jnp.concatenate([dg_mix0, dg_mix1], axis=0),
        "norm_mlp_g": jnp.concatenate([dg_mlp0, dg_mlp1], axis=0),
        "final_norm_g": dg_final[0], "gm_ln_g": dmix["ln_g"], "gm_ln_b": dmix["ln_b"],
        "gm_w_s": dmix["wm"][None], "gm_b_s": dmix["bs_t"][:, :N_BLK].T[None],
        "ssm_conv_b": dmix["conv_b"], "ssm_dt_bias": dmix["dt_bias"][:, :SSM_HEADS],
        "ssm_a_log": dmix["a_log"][:, :SSM_HEADS], "ssm_d": dmix["d_heads"][:, :SSM_HEADS],
        "ssm_norm_g": dmix["norm_g"], "attn_sinks": dsink[:, :SSM_HEADS],
        "ssm_conv_w": dmix["conv_w"][None, :4], "b_qkv": db_qkv, "b_o": db_o,
    }
    return loss, dx, reduced, small_grads


def kernel(x, norm_mix_g, norm_mlp_g, final_norm_g, w_in_even, w_out_even, gm_ln_g, gm_ln_b, gm_w_s, gm_b_s, ssm_conv_w, ssm_conv_b, ssm_dt_bias, ssm_a_log, ssm_d, ssm_norm_g, w_qkv, b_qkv, w_o, b_o, attn_sinks, w_up, w_down, loss_target, m_norm_mix_g, m_norm_mlp_g, m_final_norm_g, m_w_in_even, m_w_out_even, m_gm_ln_g, m_gm_ln_b, m_gm_w_s, m_gm_b_s, m_ssm_conv_w, m_ssm_conv_b, m_ssm_dt_bias, m_ssm_a_log, m_ssm_d, m_ssm_norm_g, m_w_qkv, m_b_qkv, m_w_o, m_b_o, m_attn_sinks, m_w_up, m_w_down, v_norm_mix_g, v_norm_mlp_g, v_final_norm_g, v_w_in_even, v_w_out_even, v_gm_ln_g, v_gm_ln_b, v_gm_w_s, v_gm_b_s, v_ssm_conv_w, v_ssm_conv_b, v_ssm_dt_bias, v_ssm_a_log, v_ssm_d, v_ssm_norm_g, v_w_qkv, v_b_qkv, v_w_o, v_b_o, v_attn_sinks, v_w_up, v_w_down):
    w = dict(norm_mix_g=norm_mix_g, norm_mlp_g=norm_mlp_g, final_norm_g=final_norm_g, w_in_even=w_in_even,
             w_out_even=w_out_even, gm_ln_g=gm_ln_g, gm_ln_b=gm_ln_b, gm_w_s=gm_w_s, gm_b_s=gm_b_s,
             ssm_conv_w=ssm_conv_w, ssm_conv_b=ssm_conv_b, ssm_dt_bias=ssm_dt_bias, ssm_a_log=ssm_a_log,
             ssm_d=ssm_d, ssm_norm_g=ssm_norm_g, w_qkv=w_qkv, b_qkv=b_qkv, w_o=w_o, b_o=b_o,
             attn_sinks=attn_sinks, w_up=w_up, w_down=w_down)
    m = dict(norm_mix_g=m_norm_mix_g, norm_mlp_g=m_norm_mlp_g, final_norm_g=m_final_norm_g,
             w_in_even=m_w_in_even, w_out_even=m_w_out_even, gm_ln_g=m_gm_ln_g, gm_ln_b=m_gm_ln_b,
             gm_w_s=m_gm_w_s, gm_b_s=m_gm_b_s, ssm_conv_w=m_ssm_conv_w, ssm_conv_b=m_ssm_conv_b,
             ssm_dt_bias=m_ssm_dt_bias, ssm_a_log=m_ssm_a_log, ssm_d=m_ssm_d, ssm_norm_g=m_ssm_norm_g,
             w_qkv=m_w_qkv, b_qkv=m_b_qkv, w_o=m_w_o, b_o=m_b_o, attn_sinks=m_attn_sinks, w_up=m_w_up,
             w_down=m_w_down)
    v = dict(norm_mix_g=v_norm_mix_g, norm_mlp_g=v_norm_mlp_g, final_norm_g=v_final_norm_g,
             w_in_even=v_w_in_even, w_out_even=v_w_out_even, gm_ln_g=v_gm_ln_g, gm_ln_b=v_gm_ln_b,
             gm_w_s=v_gm_w_s, gm_b_s=v_gm_b_s, ssm_conv_w=v_ssm_conv_w, ssm_conv_b=v_ssm_conv_b,
             ssm_dt_bias=v_ssm_dt_bias, ssm_a_log=v_ssm_a_log, ssm_d=v_ssm_d, ssm_norm_g=v_ssm_norm_g,
             w_qkv=v_w_qkv, b_qkv=v_b_qkv, w_o=v_w_o, b_o=v_b_o, attn_sinks=v_attn_sinks, w_up=v_w_up,
             w_down=v_w_down)
    names = ("norm_mix_g", "norm_mlp_g", "final_norm_g", "w_in_even", "w_out_even", "gm_ln_g", "gm_ln_b",
             "gm_w_s", "gm_b_s", "ssm_conv_w", "ssm_conv_b", "ssm_dt_bias", "ssm_a_log", "ssm_d", "ssm_norm_g",
             "w_qkv", "b_qkv", "w_o", "b_o", "attn_sinks", "w_up", "w_down")

    cx, cy, cc = lax.axis_index("x"), lax.axis_index("y"), lax.axis_index("c")
    chip = 2 * cx + cy
    c_idx = jnp.reshape(cc, (1,)).astype(jnp.int32)
    chip_idx = jnp.reshape(chip, (1,)).astype(jnp.int32)

    weights = WeightGatherer(w, chip_idx)
    sm = {n: w[n] for n, _ in _SMALL_SHAPES[:_N_REPLICATED]}

    reducer = GradReducer(c_idx, jnp.concatenate([chip_idx, c_idx]))
    loss_part, dx, grads, small_grads = _local_step(x[0], loss_target[0], weights, sm, reducer)

    small_sum = allreduce_small(_pack([small_grads[n] for n, _ in _SMALL_SHAPES] + [loss_part], SMALL_ROWS))
    *small_list, loss_row = _unpack(small_sum, [s for _, s in _SMALL_SHAPES] + [loss_part.shape])
    loss = loss_row[0, 0]
    small_full = dict(zip([n for n, _ in _SMALL_SHAPES], small_list))
    for n, _ in _SMALL_SHAPES[:_N_REPLICATED]:
        grads[n] = small_full[n]
    for n, axis, width in _SHARDED_SMALL:
        grads[n] = lax.dynamic_slice_in_dim(small_full[n], chip * width, width, axis)
    grads = {n: grads[n].reshape(w[n].shape) for n in names}

    delta, new_m, new_v = {}, {}, {}
    for n in names:
        if n in _COLUMN_SHARDED:
            args = [jnp.transpose(d[n], (2, 0, 1)) for d in (w, grads, m, v)]
            grads[n] = jnp.transpose(args[1], (1, 2, 0))
            outs = adamw(*args, f"adamw_{n}")
            delta[n], new_m[n], new_v[n] = (jnp.transpose(o, (1, 2, 0)) for o in outs)
            continue
        shape = (1,) + w[n].shape if w[n].ndim == 1 else w[n].shape
        outs = adamw(*[d[n].reshape(shape) for d in (w, grads, m, v)], f"adamw_{n}")
        delta[n], new_m[n], new_v[n] = (o.reshape(w[n].shape) for o in outs)

    return (loss, dx[None], *[grads[n] for n in names], *[delta[n] for n in names],
            *[new_m[n] for n in names], *[new_v[n] for n in names])
```

```python
import functools

import jax
import jax.numpy as jnp
from jax import lax
from jax.experimental import pallas as pl
from jax.experimental.pallas import tpu as pltpu

f32 = jnp.float32
bf16 = jnp.bfloat16
MXU_DTYPE = bf16

RMS_EPS = 1e-5
LN_EPS = 1e-5
D_MODEL = 1024
D_FF = 4096
CH = 128
N_BLK = 8
SSM_HEADS = 16
IN_EVEN = 5136
NP_IN = 5376
OFF_U, OFF_V, OFF_Z, OFF_X, OFF_DT = 0, 1024, 2048, 3072, 5120
XBC_BLKS = 16
QKV_DIM = 1280
ATT_SCALE = 64 ** -0.5

ADAM_LR = 0.001
ADAM_B1 = 0.9
ADAM_B2 = 0.999
ADAM_EPS = 1e-08
ADAM_WD = 0.01
ADAM_STEP = 10

VMEM_LIMIT_BYTES = 48 * 1024 * 1024
N_CHIPS = 4
SMALL_ROWS = 256

NN = ((1,), (0,))
NT = ((1,), (1,))
TN = ((0,), (0,))


def _mm(a, b, dims):
    return lax.dot_general(a.astype(MXU_DTYPE), b.astype(MXU_DTYPE), (dims, ((), ())),
                           preferred_element_type=f32)


def _mm_exact(a, b):
    return jnp.dot(a, b, preferred_element_type=f32, precision=lax.Precision.HIGHEST)


def _cparams(sem=None):
    return pltpu.CompilerParams(dimension_semantics=sem, vmem_limit_bytes=VMEM_LIMIT_BYTES)


@jax.custom_vjp
def _swap64(x):
    return pltpu.roll(x, 64, axis=1)


_swap64.defvjp(lambda x: (pltpu.roll(x, 64, axis=1), None), lambda _, g: (pltpu.roll(g, 64, axis=1),))


def _row_blocks_of(x):
    return tuple(x[i:i + CH] for i in range(0, x.shape[0], CH))


@jax.custom_vjp
def _row_blocks(x):
    return _row_blocks_of(x)


_row_blocks.defvjp(lambda x: (_row_blocks_of(x), None), lambda _, gs: (jnp.concatenate(gs, axis=0),))


def _make_delay(k):
    @jax.custom_vjp
    def delay(ext):
        return pltpu.roll(ext, k, axis=0)[8:, :]

    def fwd(ext):
        return delay(ext), None

    def bwd(_, g):
        gp = jnp.concatenate([jnp.zeros((8, g.shape[1]), g.dtype), g], axis=0)
        return (pltpu.roll(gp, gp.shape[0] - k, axis=0),)

    delay.defvjp(fwd, bwd)
    return delay


_DELAYS = {k: _make_delay(k) for k in (1, 2, 3)}


_GELU_C = 0.7978845608028654
_GELU_K = 0.044715


@jax.custom_vjp
def _gelu(x):
    return 0.5 * x * (1.0 + jnp.tanh(_GELU_C * (x + _GELU_K * (x * x * x))))


def _gelu_fwd(x):
    t = jnp.tanh(_GELU_C * (x + _GELU_K * (x * x * x)))
    return 0.5 * x * (1.0 + t), (x, t)


def _gelu_bwd(res, g):
    x, t = res
    dz = _GELU_C + (3.0 * _GELU_C * _GELU_K) * (x * x)
    return (g * (0.5 * (1.0 + t) + (0.5 * x) * (1.0 - t * t) * dz),)


_gelu.defvjp(_gelu_fwd, _gelu_bwd)


def _col(m, lane, h):
    return jnp.sum(jnp.where(lane == h, m, 0.0), axis=1, keepdims=True)


def _row(m, sub, h):
    return jnp.sum(jnp.where(sub == h, m, 0.0), axis=0, keepdims=True)


def _mixer_chunk(us, vs, zs, xbcs, halos, dtblk, hps, prm):
    lane = lax.broadcasted_iota(jnp.int32, (CH, CH), 1)
    sub = lax.broadcasted_iota(jnp.int32, (CH, CH), 0)
    left = lane < 64
    top = sub < 64
    causal = sub >= lane

    gus = [_gelu(u) for u in us]
    gvs = [_gelu(v) for v in vs]
    mu = sum(jnp.sum(g, axis=1, keepdims=True) for g in gvs) / D_MODEL
    cen = [g - mu for g in gvs]
    var = sum(jnp.sum(c * c, axis=1, keepdims=True) for c in cen) / D_MODEL
    rstd = lax.rsqrt(var + LN_EPS)
    a_out = []
    for g in range(N_BLK):
        vn = cen[g] * rstd * prm["ln_g"][g] + prm["ln_b"][g]
        w = jnp.where(causal, prm["wm"][g], 0.0)
        mixed = _mm(w, vn, NN) + _col(prm["bs_t"], lane, g)
        a_out.append(gus[g] * mixed)

    act = []
    for b in range(XBC_BLKS):
        w8 = prm["conv_w"][b]
        sub8 = lax.broadcasted_iota(jnp.int32, w8.shape, 0)
        ext = jnp.concatenate([halos[b], xbcs[b]], axis=0)
        conv = xbcs[b] * _row(w8, sub8, 3) + prm["conv_b"][b]
        for k in (1, 2, 3):
            conv = conv + _DELAYS[k](ext) * _row(w8, sub8, 3 - k)
        act.append(jax.nn.silu(conv))

    dt = jax.nn.softplus(dtblk + prm["dt_bias"])
    a_neg = -jnp.exp(prm["a_log"])
    tri = causal.astype(f32)
    acum = _mm_exact(tri, dt * a_neg)
    acum_t = acum.T
    dt_t = dt.T
    last = sub == CH - 1
    ys, h_out = [], []
    for grp in range(4):
        bm = act[8 + grp]
        cm = act[12 + grp]
        cb = _mm(cm, bm, NT)
        for p in (2 * grp, 2 * grp + 1):
            h0, h1 = 2 * p, 2 * p + 1
            xp = act[p]
            hp = hps[p]
            wis = []
            for h in (h0, h1):
                seg = _col(acum, lane, h) - _row(acum_t, sub, h)
                decay = jnp.exp(jnp.where(causal, seg, -jnp.inf))
                wis.append(cb * decay * _row(dt_t, sub, h))
            wcat = jnp.concatenate(wis, axis=1)
            xbd = jnp.concatenate([jnp.where(left, xp, 0.0), jnp.where(left, 0.0, xp)], axis=0)
            y_diag = _mm(wcat, xbd, NN)
            a_end = [jnp.sum(jnp.where(last & (lane == h), acum, 0.0), keepdims=True) for h in (h0, h1)]
            a_col = jnp.where(left, _col(acum, lane, h0), _col(acum, lane, h1))
            dt_col = jnp.where(left, _col(dt, lane, h0), _col(dt, lane, h1))
            to_end = jnp.exp(jnp.where(left, a_end[0], a_end[1]) - a_col) * dt_col
            states = _mm(xp * to_end, bm, TN)
            chunk_decay = jnp.where(top, jnp.exp(a_end[0]), jnp.exp(a_end[1]))
            h_out.append(chunk_decay * hp + states)
            y_off = jnp.exp(a_col) * _mm(cm, hp, NT)
            d_skip = jnp.where(left[:1], _col(prm["d_heads"], lane[:1], h0), _col(prm["d_heads"], lane[:1], h1))
            ys.append((y_diag + y_off + xp * d_skip) * jax.nn.silu(zs[p]))

    b_out = []
    for grp in range(4):
        pair = (ys[2 * grp], ys[2 * grp + 1])
        ms = sum(jnp.sum(y * y, axis=1, keepdims=True) for y in pair) / 256.0
        r = lax.rsqrt(ms + RMS_EPS)
        for j, y in enumerate(pair):
            b_out.append(y * r * prm["norm_g"][2 * grp + j])
    return a_out, b_out, h_out


def _attn_block(qps, kprev, kcur, vprev, vcur, sink_row, first):
    lane = lax.broadcasted_iota(jnp.int32, (CH, CH), 1)
    left = lane < 64
    heads = N_BLK
    row = lax.broadcasted_iota(jnp.int32, (heads * CH, CH), 0)
    key = lax.broadcasted_iota(jnp.int32, (heads * CH, CH), 1)
    own = key <= (row & (CH - 1))

    def both_halves(a):
        sw = _swap64(a)
        return [jnp.where(left, a, sw), jnp.where(left, sw, a)]

    kc, kp, vc, vp = both_halves(kcur), both_halves(kprev), both_halves(vcur), both_halves(vprev)
    outs = []
    for j in range(2):
        pairs = range(4 * j, 4 * j + 4)
        q8 = jnp.concatenate([part for p in pairs
                              for part in (jnp.where(left, qps[p], 0.0), jnp.where(left, 0.0, qps[p]))], axis=0)
        sink = jnp.concatenate([jnp.broadcast_to(_col(sink_row, lane[:1], h), (CH, 1))
                                for h in range(8 * j, 8 * j + 8)], axis=0)
        s_prev = jnp.where(first, -jnp.inf, _mm(q8, kp[j], NT) * ATT_SCALE)
        s = jnp.where(own, _mm(q8, kc[j], NT) * ATT_SCALE, s_prev)
        m = lax.stop_gradient(jnp.maximum(jnp.max(s, axis=1, keepdims=True), sink))
        pexp = jnp.exp(s - m)
        probs = pexp / (jnp.sum(pexp, axis=1, keepdims=True) + jnp.exp(sink - m))
        o = _row_blocks(_mm(jnp.where(own, probs, 0.0), vc[j], NN) + _mm(jnp.where(own, 0.0, probs), vp[j], NN))
        for t in range(4):
            outs.append(jnp.where(left, o[2 * t], o[2 * t + 1]))
    return outs


def _rmsnorm(x, g):
    r = lax.rsqrt(jnp.mean(x * x, axis=-1, keepdims=True) + RMS_EPS)
    return x * r * g


def rmsnorm_fwd(x, g_row, name):
    s, d = x.shape
    tm = min(512, s)

    def body(x_ref, g_ref, y_ref):
        y_ref[...] = _rmsnorm(x_ref[...], g_ref[...]).astype(bf16)

    return pl.pallas_call(
        body, name=name, grid=(s // tm,),
        in_specs=[pl.BlockSpec((tm, d), lambda i: (i, 0)), pl.BlockSpec((1, d), lambda i: (0, 0))],
        out_specs=pl.BlockSpec((tm, d), lambda i: (i, 0)),
        out_shape=jax.ShapeDtypeStruct((s, d), bf16),
        compiler_params=_cparams(("parallel",)),
    )(x, g_row)


def colsum(x, name):
    s, n = x.shape
    tm = min(512, s)

    def body(x_ref, o_ref):
        @pl.when(pl.program_id(0) == 0)
        def _():
            o_ref[...] = jnp.zeros_like(o_ref)

        o_ref[...] += jnp.sum(x_ref[...].astype(f32), axis=0, keepdims=True)

    return pl.pallas_call(
        body, name=name, grid=(s // tm,),
        in_specs=[pl.BlockSpec((tm, n), lambda i: (i, 0))],
        out_specs=pl.BlockSpec((1, n), lambda i: (0, 0)),
        out_shape=jax.ShapeDtypeStruct((1, n), f32),
        compiler_params=_cparams(("arbitrary",)),
    )(x)


def _fit(dim, want):
    if dim <= want:
        return dim
    t = want
    while dim % t:
        t -= 128
    return t


def matmul(a, b, *, dims, name, out_dtype=f32, tm=1024, tn=512, tk=8192, a_pro=None, epi=None, epi_args=(),
           out_by_col_tile=False, after=None):
    if dims == "nn" and b.ndim == 3:
        (m, k), n, tn = a.shape, b.shape[0] * b.shape[2], b.shape[2]
    elif dims == "nn":
        (m, k), n = a.shape, b.shape[1]
    elif dims == "nt":
        (m, k), n = a.shape, b.shape[0]
    else:
        (k, m), n = a.shape, b.shape[1]
    tm, tn, tk = _fit(m, tm), _fit(n, tn), _fit(k, tk)
    nk = k // tk
    if dims == "nn":
        a_spec = pl.BlockSpec((tm, tk), lambda i, j, kk: (i, kk))
        b_spec = (pl.BlockSpec((None, tk, tn), lambda i, j, kk: (j, kk, 0)) if b.ndim == 3
                  else pl.BlockSpec((tk, tn), lambda i, j, kk: (kk, j)))
        dn = NN
    elif dims == "nt":
        a_spec = pl.BlockSpec((tm, tk), lambda i, j, kk: (i, kk))
        b_spec = pl.BlockSpec((tn, tk), lambda i, j, kk: (j, kk))
        dn = NT
    else:
        a_spec = pl.BlockSpec((tk, tm), lambda i, j, kk: (kk, i))
        b_spec = pl.BlockSpec((tk, tn), lambda i, j, kk: (kk, j))
        dn = TN
    e_specs = [pl.BlockSpec((tm, tn), lambda i, j, kk: (i, j)) if kind == "tile"
               else pl.BlockSpec((1, tn), lambda i, j, kk: (0, j)) for kind, _ in epi_args]
    n_epi = len(epi_args)
    order_specs = [] if after is None else [pl.BlockSpec((8, 128), lambda i, j, kk: (0, 0))]
    order_args = [] if after is None else [after]

    def body(*refs):
        a_ref, b_ref = refs[0], refs[1]
        e_refs = refs[2:2 + n_epi]
        n_in = 2 + n_epi + len(order_args)
        o_ref = refs[n_in]
        av = a_ref[...]
        if a_pro is not None:
            av = a_pro(av)
        part = _mm(av, b_ref[...], dn)

        def finish(acc):
            if epi is not None:
                acc = epi(acc, *[r[...] for r in e_refs])
            o_ref[...] = acc.astype(out_dtype)

        if nk == 1:
            finish(part)
        else:
            acc_ref = refs[n_in + 1]
            kk = pl.program_id(2)

            @pl.when(kk == 0)
            def _():
                acc_ref[...] = part

            @pl.when(kk > 0)
            def _():
                acc_ref[...] += part

            @pl.when(kk == nk - 1)
            def _():
                finish(acc_ref[...])

    if out_by_col_tile:
        out_spec = pl.BlockSpec((None, tm, tn), lambda i, j, kk: (j, i, 0))
        out_shape = jax.ShapeDtypeStruct((n // tn, m, tn), out_dtype)
    else:
        out_spec = pl.BlockSpec((tm, tn), lambda i, j, kk: (i, j))
        out_shape = jax.ShapeDtypeStruct((m, n), out_dtype)
    return pl.pallas_call(
        body, name=name, grid=(m // tm, n // tn, nk),
        in_specs=[a_spec, b_spec] + e_specs + order_specs,
        out_specs=out_spec,
        out_shape=out_shape,
        scratch_shapes=[pltpu.VMEM((tm, tn), f32)] if nk > 1 else [],
        compiler_params=_cparams(("parallel", "parallel", "arbitrary")),
    )(a, b, *[arr for _, arr in epi_args], *order_args)


def _relu2(a):
    r = jnp.maximum(a.astype(f32), 0.0)
    return r * r


def _add(acc, t):
    return acc + t


def _add_bias(acc, t):
    return acc + t


def _add_bias_res(acc, bias, res):
    return acc + bias + res


def _times_relu2_grad(acc, a):
    return acc * (2.0 * jnp.maximum(a.astype(f32), 0.0))


def matmul_rows(a, b, *, dims, name, epi, epi_args, outs, tm=512, a_pro=None, after=None):
    m, k = a.shape
    n = b.shape[-1] if dims == "nn" else b.shape[-2]
    tm = _fit(m, tm)
    dn = NN if dims == "nn" else NT
    e_specs = [pl.BlockSpec((tm, arr.shape[1]), lambda i: (i, 0)) if kind == "tile"
               else pl.BlockSpec((1, arr.shape[1]), lambda i: (0, 0)) for kind, arr in epi_args]
    order_specs = [] if after is None else [pl.BlockSpec((8, 128), lambda i: (0, 0))]
    order_args = [] if after is None else [after]
    n_in = 2 + len(epi_args) + len(order_args)

    def body(*refs):
        av = refs[0][...]
        if a_pro is not None:
            av = a_pro(av)
        if b.ndim == 3:
            kb = b.shape[2]
            acc = sum(_mm(av[:, s * kb:(s + 1) * kb], refs[1][s], dn) for s in range(b.shape[0]))
        else:
            acc = _mm(av, refs[1][...], dn)
        vals = epi(acc, *[r[...] for r in refs[2:2 + len(epi_args)]])
        for (kind, _), o_ref, val in zip(outs, refs[n_in:], vals):
            if kind == "tile":
                o_ref[...] = val.astype(o_ref.dtype)
            else:
                @pl.when(pl.program_id(0) == 0)
                def _():
                    o_ref[...] = jnp.zeros_like(o_ref)

                o_ref[...] += val

    out_specs = [pl.BlockSpec((tm, n), lambda i: (i, 0)) if kind == "tile" else pl.BlockSpec((1, arg), lambda i: (0, 0))
                 for kind, arg in outs]
    out_shape = [jax.ShapeDtypeStruct((m, n), arg) if kind == "tile" else jax.ShapeDtypeStruct((1, arg), f32)
                 for kind, arg in outs]
    return pl.pallas_call(
        body, name=name, grid=(m // tm,),
        in_specs=[pl.BlockSpec((tm, k), lambda i: (i, 0)), pl.BlockSpec(b.shape, lambda i: (0,) * b.ndim)]
                 + e_specs + order_specs,
        out_specs=out_specs, out_shape=out_shape,
        compiler_params=_cparams(("arbitrary",)),
    )(a, b, *[arr for _, arr in epi_args], *order_args)


def _res_norm(acc, res, g):
    h = acc + res
    return h, _rmsnorm(h, g)


def _bias_res_norm(acc, bias, res, g):
    h = acc + bias + res
    return h, _rmsnorm(h, g)


def _res_norm_loss(acc, res, g, target):
    def f(h, gv):
        err = jnp.square(_rmsnorm(h, gv) - target)
        return 0.5 * jnp.sum(jnp.mean(err, axis=-1, keepdims=True), axis=0, keepdims=True)

    loss, vjp = jax.vjp(f, acc + res, g)
    dh, dg = vjp(jnp.ones_like(loss))
    return dh, dg, jnp.broadcast_to(loss, (1, 128))


def _norm_bwd_res(dy, x, g, res):
    _, vjp = jax.vjp(_rmsnorm, x, g)
    dx, dg = vjp(dy)
    return res + dx, dg


_MIXER_PARAM_SHAPES = (
    ("ln_g", (1, D_MODEL)), ("ln_b", (1, D_MODEL)), ("wm", (N_BLK, CH, CH)), ("bs_t", (CH, CH)),
    ("conv_w", (8, 2048)), ("conv_b", (1, 2048)), ("dt_bias", (1, CH)), ("a_log", (1, CH)),
    ("d_heads", (1, CH)), ("norm_g", (1, D_MODEL)),
)


def _blocks(v, n, off=0):
    return [v[:, off + i * CH: off + (i + 1) * CH] for i in range(n)]


def _split_mixer_params(vals):
    p = dict(vals)
    return {
        "ln_g": _blocks(p["ln_g"], N_BLK), "ln_b": _blocks(p["ln_b"], N_BLK),
        "wm": [p["wm"][g] for g in range(N_BLK)], "bs_t": p["bs_t"],
        "conv_w": _blocks(p["conv_w"], XBC_BLKS), "conv_b": _blocks(p["conv_b"], XBC_BLKS),
        "dt_bias": p["dt_bias"], "a_log": p["a_log"], "d_heads": p["d_heads"],
        "norm_g": _blocks(p["norm_g"], N_BLK),
    }


def _mixer_leaves(proj_ref, halo_ref, keep_halo):
    pv = proj_ref
    us = [pv[:, OFF_U + i * CH: OFF_U + (i + 1) * CH] for i in range(N_BLK)]
    vs = [pv[:, OFF_V + i * CH: OFF_V + (i + 1) * CH] for i in range(N_BLK)]
    zs = [pv[:, OFF_Z + i * CH: OFF_Z + (i + 1) * CH] for i in range(N_BLK)]
    xbcs = [pv[:, OFF_X + i * CH: OFF_X + (i + 1) * CH] for i in range(XBC_BLKS)]
    halos = [halo_ref[:, OFF_X + i * CH: OFF_X + (i + 1) * CH] * keep_halo for i in range(XBC_BLKS)]
    dtblk = pv[:, OFF_DT: OFF_DT + CH]
    return us, vs, zs, xbcs, halos, dtblk


def mixer_fwd(proj, prm):
    s = proj.shape[0]
    nc = s // CH
    names = [n for n, _ in _MIXER_PARAM_SHAPES]

    def body(proj_ref, halo_ref, *rest):
        p_refs = rest[:len(names)]
        ab_ref, hs_ref, h_ref = rest[len(names):]
        c = pl.program_id(0)

        @pl.when(c == 0)
        def _():
            h_ref[...] = jnp.zeros_like(h_ref)

        hs_ref[...] = h_ref[...]
        keep = (c > 0).astype(f32)
        us, vs, zs, xbcs, halos, dtblk = _mixer_leaves(proj_ref, halo_ref, keep)
        hps = [h_ref[i * CH:(i + 1) * CH, :] for i in range(N_BLK)]
        p = _split_mixer_params({n: r[...] for n, r in zip(names, p_refs)})
        a_out, b_out, h_out = _mixer_chunk(us, vs, zs, xbcs, halos, dtblk, hps, p)
        for i in range(N_BLK):
            ab_ref[:, i * CH:(i + 1) * CH] = a_out[i].astype(bf16)
            ab_ref[:, D_MODEL + i * CH: D_MODEL + (i + 1) * CH] = b_out[i].astype(bf16)
            h_ref[i * CH:(i + 1) * CH, :] = h_out[i]

    def const(shape):
        return pl.BlockSpec(shape, lambda c: (0,) * len(shape))

    return pl.pallas_call(
        body, name="mixer_fwd", grid=(nc,),
        in_specs=[pl.BlockSpec((CH, NP_IN), lambda c: (c, 0)),
                  pl.BlockSpec((8, NP_IN), lambda c: (jnp.maximum(c * (CH // 8) - 1, 0), 0))]
                 + [const(shp) for _, shp in _MIXER_PARAM_SHAPES],
        out_specs=[pl.BlockSpec((CH, 2 * D_MODEL), lambda c: (c, 0)),
                   pl.BlockSpec((None, D_MODEL, CH), lambda c: (c, 0, 0))],
        out_shape=[jax.ShapeDtypeStruct((s, 2 * D_MODEL), bf16), jax.ShapeDtypeStruct((nc, D_MODEL, CH), f32)],
        scratch_shapes=[pltpu.VMEM((D_MODEL, CH), f32)],
        compiler_params=_cparams(("arbitrary",)),
    )(proj, proj, *[prm[n] for n in names])


def mixer_bwd(proj, hstates, dab, prm):
    s = proj.shape[0]
    nc = s // CH
    names = [n for n, _ in _MIXER_PARAM_SHAPES]
    npar = len(names)

    def body(proj_ref, halo_ref, hs_ref, dab_ref, *rest):
        p_refs = rest[:npar]
        dproj_ref = rest[npar]
        g_refs = rest[npar + 1: 2 * npar + 1]
        dh_ref, dhalo_ref = rest[2 * npar + 1:]
        i = pl.program_id(0)
        c = nc - 1 - i

        @pl.when(i == 0)
        def _():
            dh_ref[...] = jnp.zeros_like(dh_ref)
            dhalo_ref[...] = jnp.zeros_like(dhalo_ref)
            for r in g_refs:
                r[...] = jnp.zeros_like(r)

        keep = (c > 0).astype(f32)
        us, vs, zs, xbcs, halos, dtblk = _mixer_leaves(proj_ref, halo_ref, keep)
        hps = [hs_ref[j * CH:(j + 1) * CH, :] for j in range(N_BLK)]
        pvals = {n: r[...] for n, r in zip(names, p_refs)}

        def fn(us, vs, zs, xbcs, halos, dtblk, hps, pvals):
            return _mixer_chunk(us, vs, zs, xbcs, halos, dtblk, hps, _split_mixer_params(pvals))

        _, vjp = jax.vjp(fn, us, vs, zs, xbcs, halos, dtblk, hps, pvals)
        da = [dab_ref[:, j * CH:(j + 1) * CH].astype(f32) for j in range(N_BLK)]
        db = [dab_ref[:, D_MODEL + j * CH: D_MODEL + (j + 1) * CH].astype(f32) for j in range(N_BLK)]
        dh = [dh_ref[j * CH:(j + 1) * CH, :] for j in range(N_BLK)]
        dus, dvs, dzs, dxbcs, dhalos, ddt, dhps, dp = vjp((da, db, dh))

        for j in range(N_BLK):
            dproj_ref[:, OFF_U + j * CH: OFF_U + (j + 1) * CH] = dus[j].astype(bf16)
            dproj_ref[:, OFF_V + j * CH: OFF_V + (j + 1) * CH] = dvs[j].astype(bf16)
            dproj_ref[:, OFF_Z + j * CH: OFF_Z + (j + 1) * CH] = dzs[j].astype(bf16)
            dh_ref[j * CH:(j + 1) * CH, :] = dhps[j]
        zeros_top = jnp.zeros((CH - 8, CH), f32)
        for j in range(XBC_BLKS):
            late = jnp.concatenate([zeros_top, dhalo_ref[:, j * CH:(j + 1) * CH]], axis=0)
            dproj_ref[:, OFF_X + j * CH: OFF_X + (j + 1) * CH] = (dxbcs[j] + late).astype(bf16)
        for j in range(XBC_BLKS):
            dhalo_ref[:, j * CH:(j + 1) * CH] = dhalos[j] * keep
        lane = lax.broadcasted_iota(jnp.int32, (CH, CH), 1)
        dproj_ref[:, OFF_DT: OFF_DT + CH] = jnp.where(lane < SSM_HEADS, ddt, 0.0).astype(bf16)
        dproj_ref[:, OFF_DT + CH:] = jnp.zeros((CH, NP_IN - OFF_DT - CH), bf16)
        for n, r in zip(names, g_refs):
            r[...] += dp[n]

    def const(shape):
        return pl.BlockSpec(shape, lambda i: (0,) * len(shape))

    outs = pl.pallas_call(
        body, name="mixer_bwd", grid=(nc,),
        in_specs=[pl.BlockSpec((CH, NP_IN), lambda i: (nc - 1 - i, 0)),
                  pl.BlockSpec((8, NP_IN), lambda i: (jnp.maximum((nc - 1 - i) * (CH // 8) - 1, 0), 0)),
                  pl.BlockSpec((None, D_MODEL, CH), lambda i: (nc - 1 - i, 0, 0)),
                  pl.BlockSpec((CH, 2 * D_MODEL), lambda i: (nc - 1 - i, 0))]
                 + [const(shp) for _, shp in _MIXER_PARAM_SHAPES],
        out_specs=[pl.BlockSpec((CH, NP_IN), lambda i: (nc - 1 - i, 0))]
                  + [const(shp) for _, shp in _MIXER_PARAM_SHAPES],
        out_shape=[jax.ShapeDtypeStruct((s, NP_IN), bf16)]
                  + [jax.ShapeDtypeStruct(shp, f32) for _, shp in _MIXER_PARAM_SHAPES],
        scratch_shapes=[pltpu.VMEM((D_MODEL, CH), f32), pltpu.VMEM((8, 2048), f32)],
        compiler_params=_cparams(("arbitrary",)),
    )(proj, proj, hstates, dab, *[prm[n] for n in names])
    return outs[0], dict(zip(names, outs[1:]))


_K_BLK = D_MODEL // CH
_V_BLK = _K_BLK + 1


def _attn_specs(rev, nb):
    def blk(i):
        return nb - 1 - i if rev else i

    q_spec = pl.BlockSpec((CH, D_MODEL), lambda i: (blk(i), 0))
    kv = lambda col, prev: pl.BlockSpec(
        (CH, CH), lambda i: (jnp.maximum(blk(i) - 1, 0) if prev else blk(i), col))
    return q_spec, [kv(_K_BLK, True), kv(_K_BLK, False), kv(_V_BLK, True), kv(_V_BLK, False)]


def attn_fwd(qkv, sink_row):
    s = qkv.shape[0]
    nb = s // CH

    def body(q_ref, kp_ref, kc_ref, vp_ref, vc_ref, sink_ref, o_ref):
        qps = [q_ref[:, p * CH:(p + 1) * CH] for p in range(N_BLK)]
        outs = _attn_block(qps, kp_ref[...], kc_ref[...], vp_ref[...], vc_ref[...], sink_ref[...],
                           pl.program_id(0) == 0)
        for p in range(N_BLK):
            o_ref[:, p * CH:(p + 1) * CH] = outs[p].astype(bf16)

    q_spec, kv_specs = _attn_specs(False, nb)
    return pl.pallas_call(
        body, name="attn_fwd", grid=(nb,),
        in_specs=[q_spec] + kv_specs + [pl.BlockSpec((1, CH), lambda i: (0, 0))],
        out_specs=pl.BlockSpec((CH, D_MODEL), lambda i: (i, 0)),
        out_shape=jax.ShapeDtypeStruct((s, D_MODEL), bf16),
        compiler_params=_cparams(("parallel",)),
    )(qkv, qkv, qkv, qkv, qkv, sink_row)


def attn_bwd(qkv, sink_row, dout):
    s = qkv.shape[0]
    nb = s // CH

    def body(q_ref, kp_ref, kc_ref, vp_ref, vc_ref, sink_ref, do_ref, dqkv_ref, dsink_ref, carry_ref):
        i = pl.program_id(0)
        blk = nb - 1 - i

        @pl.when(i == 0)
        def _():
            dsink_ref[...] = jnp.zeros_like(dsink_ref)
            carry_ref[...] = jnp.zeros_like(carry_ref)

        qps = [q_ref[:, p * CH:(p + 1) * CH] for p in range(N_BLK)]
        first = blk == 0
        _, vjp = jax.vjp(lambda *a: _attn_block(*a, first), qps, kp_ref[...], kc_ref[...], vp_ref[...],
                         vc_ref[...], sink_ref[...])
        dos = [do_ref[:, p * CH:(p + 1) * CH].astype(f32) for p in range(N_BLK)]
        dqs, dkp, dkc, dvp, dvc, dsink = vjp(dos)
        for p in range(N_BLK):
            dqkv_ref[:, p * CH:(p + 1) * CH] = dqs[p].astype(bf16)
        dqkv_ref[:, D_MODEL: D_MODEL + CH] = (dkc + carry_ref[0]).astype(bf16)
        dqkv_ref[:, D_MODEL + CH:] = (dvc + carry_ref[1]).astype(bf16)
        keep = jnp.logical_not(first).astype(f32)
        carry_ref[0] = dkp * keep
        carry_ref[1] = dvp * keep
        dsink_ref[...] += dsink

    q_spec, kv_specs = _attn_specs(True, nb)
    return pl.pallas_call(
        body, name="attn_bwd", grid=(nb,),
        in_specs=[q_spec] + kv_specs + [pl.BlockSpec((1, CH), lambda i: (0, 0)),
                                        pl.BlockSpec((CH, D_MODEL), lambda i: (nb - 1 - i, 0))],
        out_specs=[pl.BlockSpec((CH, QKV_DIM), lambda i: (nb - 1 - i, 0)), pl.BlockSpec((1, CH), lambda i: (0, 0))],
        out_shape=[jax.ShapeDtypeStruct((s, QKV_DIM), bf16), jax.ShapeDtypeStruct((1, CH), f32)],
        scratch_shapes=[pltpu.VMEM((2, CH, CH), f32)],
        compiler_params=_cparams(("arbitrary",)),
    )(qkv, qkv, qkv, qkv, qkv, sink_row, dout)


def adamw(w, g, m, v, name):
    def body(w_ref, g_ref, m_ref, v_ref, d_ref, nm_ref, nv_ref):
        gv = g_ref[...]
        nm = ADAM_B1 * m_ref[...] + (1.0 - ADAM_B1) * gv
        nv = ADAM_B2 * v_ref[...] + (1.0 - ADAM_B2) * jnp.square(gv)
        m_hat = nm / (1.0 - ADAM_B1 ** ADAM_STEP)
        v_hat = nv / (1.0 - ADAM_B2 ** ADAM_STEP)
        d_ref[...] = -ADAM_LR * (m_hat / (jnp.sqrt(v_hat) + ADAM_EPS) + ADAM_WD * w_ref[...])
        nm_ref[...] = nm
        nv_ref[...] = nv

    out_shape = [jax.ShapeDtypeStruct(w.shape, f32)] * 3
    if w.ndim == 3 and w.shape[1] == 1:
        tr = max(t for t in range(1, 129) if w.shape[0] % t == 0)
        tile = pl.BlockSpec((tr, 1, w.shape[2]), lambda i: (i, 0, 0))
        return pl.pallas_call(
            body, name=name, grid=(w.shape[0] // tr,),
            in_specs=[tile] * 4, out_specs=[tile] * 3, out_shape=out_shape,
            compiler_params=_cparams(("parallel",)),
        )(w, g, m, v)
    if w.ndim == 3 and w.shape[1] % 256 == 0:
        tile = pl.BlockSpec((None, 256, w.shape[2]), lambda l, i: (l, i, 0))
        return pl.pallas_call(
            body, name=name, grid=(w.shape[0], w.shape[1] // 256),
            in_specs=[tile] * 4, out_specs=[tile] * 3, out_shape=out_shape,
            compiler_params=_cparams(("parallel", "parallel")),
        )(w, g, m, v)
    return pl.pallas_call(body, name=name, in_specs=[_VMEM] * 4, out_specs=[_VMEM] * 3, out_shape=out_shape,
                          compiler_params=_cparams())(w, g, m, v)


_MESH = pl.DeviceIdType.MESH
_ANY = pl.BlockSpec(memory_space=pl.ANY)
_VMEM = pl.BlockSpec(memory_space=pltpu.VMEM)


def _place():
    x, y, c = lax.axis_index("x"), lax.axis_index("y"), lax.axis_index("c")
    chips = [(1 - x, y), (x, 1 - y), (1 - x, 1 - y)]
    return x, y, c, 2 * x + y, chips, [2 * cx + cy for cx, cy in chips]


def _half(c, rows):
    return pl.ds(pl.multiple_of(c * (rows // 2), 16), rows // 2)


def _step_rows(rows):
    return max(t for t in range(16, 641, 16) if rows % t == 0)


def place_shard(b, slot, name, dtype=bf16, after=None):
    r, c = b.shape
    tr = _step_rows(r)

    def body(slot_ref, b_ref, *rest):
        rest[-1][...] = b_ref[...].astype(dtype)

    order_specs = [] if after is None else [pl.BlockSpec((8, 128), lambda i, s: (0, 0))]
    return pl.pallas_call(
        body, name=name,
        grid_spec=pltpu.PrefetchScalarGridSpec(
            num_scalar_prefetch=1, grid=(r // tr,),
            in_specs=[pl.BlockSpec((tr, c), lambda i, s: (i, 0))] + order_specs,
            out_specs=pl.BlockSpec((None, tr, c), lambda i, s: (s[0], i, 0))),
        out_shape=jax.ShapeDtypeStruct((N_CHIPS, r, c), dtype),
        compiler_params=_cparams(("parallel",)),
    )(slot, b, *([] if after is None else [after]))


_HBM = pl.BlockSpec(memory_space=pltpu.HBM)
_SEM = pl.BlockSpec(memory_space=pltpu.SEMAPHORE)
_EFFECT = pltpu.SideEffectType.DATAFLOW_SIDE_EFFECTING


def _gather_ici_copies(bufs, send_sems, recv_sems):
    x, y, c, me, chips, chip_idx = _place()
    return [pltpu.make_async_remote_copy(
        src_ref=buf.at[me, _half(c, buf.shape[1])], dst_ref=buf.at[chip_idx[j], _half(c, buf.shape[1])],
        send_sem=send_sems.at[3 * k + j], recv_sem=recv_sems.at[3 * k + j],
        device_id=(*chips[j], c), device_id_type=_MESH) for j in range(3) for k, buf in enumerate(bufs)]


def gather_start(groups, tag):
    sizes = [len(g) for g in groups]
    flat = [b for g in groups for b in g]
    n = len(flat)

    def body(*refs):
        bufs, sems = refs[:n], refs[n:n + 2 * len(groups)]
        refs[-1][...] = jnp.zeros_like(refs[-1])
        x, y, c, me, chips, chip_idx = _place()
        lo = 0
        for gi, size in enumerate(sizes):
            for j in range(3):
                for k, buf in enumerate(bufs[lo:lo + size]):
                    mine = buf.at[me, _half(c, buf.shape[1])]
                    pltpu.make_async_remote_copy(
                        src_ref=mine, dst_ref=mine, send_sem=sems[2 * gi].at[3 * k + j],
                        recv_sem=sems[2 * gi + 1].at[3 * k + j], device_id=(*chips[j], c),
                        device_id_type=_MESH).start()
            lo += size

    sem_shapes = [pltpu.SemaphoreType.DMA((3 * size,)) for size in sizes for _ in range(2)]
    outs = pl.pallas_call(
        body, name=f"gather_start_{tag}",
        out_shape=(*sem_shapes, *[pltpu.HBM(b.shape, b.dtype) for b in flat], jax.ShapeDtypeStruct((8, 128), f32)),
        in_specs=[_HBM] * n, out_specs=(*[_SEM] * len(sem_shapes), *[_HBM] * n, _VMEM),
        input_output_aliases={i: len(sem_shapes) + i for i in range(n)},
        compiler_params=pltpu.CompilerParams(has_side_effects=_EFFECT),
    )(*[pltpu.with_memory_space_constraint(b, pltpu.HBM) for b in flat])
    sems = [(outs[2 * gi], outs[2 * gi + 1]) for gi in range(len(groups))]
    thru, lo = [], len(sem_shapes)
    for size in sizes:
        thru.append(list(outs[lo:lo + size]))
        lo += size
    return sems, thru, outs[-1]


def gather_wait(bufs, sems, after, tag):
    n = len(bufs)

    def body(*refs):
        for cp in _gather_ici_copies(refs[:n], refs[n], refs[n + 1]):
            cp.wait_send()
            cp.wait_recv()

    extra = list(after)
    return list(pl.pallas_call(
        body, name=f"gather_wait_{tag}",
        out_shape=[pltpu.HBM(b.shape, b.dtype) for b in bufs],
        in_specs=[_HBM] * n + [_SEM, _SEM] + [_ANY] * len(extra), out_specs=[_HBM] * n,
        input_output_aliases={i: i for i in range(n)},
        compiler_params=pltpu.CompilerParams(has_side_effects=_EFFECT),
    )(*bufs, *sems, *extra))


def gather_forward(bufs, tag):
    n = len(bufs)

    def body(*refs):
        out_refs = refs[n:2 * n]
        send_sems, recv_sems = refs[2 * n:]
        x, y, c, me, chips, chip_idx = _place()

        def copy(k, j, half):
            part = out_refs[k].at[chip_idx[j], _half(half, out_refs[k].shape[1])]
            return pltpu.make_async_remote_copy(
                src_ref=part, dst_ref=part, send_sem=send_sems.at[3 * k + j], recv_sem=recv_sems.at[3 * k + j],
                device_id=(x, y, 1 - c), device_id_type=_MESH)

        sends = [copy(k, j, c) for j in range(3) for k in range(n)]
        for cp in sends:
            cp.start()
        for j in range(3):
            for k in range(n):
                copy(k, j, 1 - c).wait_recv()
        for cp in sends:
            cp.wait_send()

    return list(pl.pallas_call(
        body, name=f"gather_forward_{tag}",
        out_shape=[jax.ShapeDtypeStruct(b.shape, b.dtype) for b in bufs],
        in_specs=[_ANY] * n, out_specs=[_ANY] * n, input_output_aliases={i: i for i in range(n)},
        scratch_shapes=[pltpu.SemaphoreType.DMA((3 * n,)), pltpu.SemaphoreType.DMA((3 * n,))],
    )(*bufs))


def exchange_halves(bufs, tag):
    n = len(bufs)

    def body(*refs):
        g_refs, out_refs = refs[:n], refs[n:2 * n]
        send_sems, recv_sems = refs[2 * n:]
        x, y, c, *_ = _place()
        cps = [pltpu.make_async_remote_copy(
            src_ref=g_refs[b].at[:, _half(1 - c, g_refs[b].shape[1])], dst_ref=out_refs[b],
            send_sem=send_sems.at[b], recv_sem=recv_sems.at[b], device_id=(x, y, 1 - c), device_id_type=_MESH)
            for b in range(n)]
        for cp in cps:
            cp.start()
        for cp in cps:
            cp.wait()

    return pl.pallas_call(
        body, name=f"exchange_halves_{tag}",
        out_shape=[jax.ShapeDtypeStruct((N_CHIPS, b.shape[1] // 2, b.shape[2]), b.dtype) for b in bufs],
        in_specs=[_ANY] * n, out_specs=[_ANY] * n,
        scratch_shapes=[pltpu.SemaphoreType.DMA((n,)), pltpu.SemaphoreType.DMA((n,))],
    )(*bufs)


def add_halves(g, got, c_idx, name):
    hr, cols = got.shape[1], got.shape[2]
    tr = _step_rows(hr)
    steps = hr // tr

    def body(c_ref, g_ref, got_ref, o_ref):
        o_ref[...] = (g_ref[...].astype(f32) + got_ref[...].astype(f32)).astype(bf16)

    return pl.pallas_call(
        body, name=name,
        grid_spec=pltpu.PrefetchScalarGridSpec(
            num_scalar_prefetch=1, grid=(N_CHIPS, steps),
            in_specs=[pl.BlockSpec((None, tr, cols), lambda s, i, c: (s, c[0] * steps + i, 0)),
                      pl.BlockSpec((None, tr, cols), lambda s, i, c: (s, i, 0))],
            out_specs=pl.BlockSpec((None, tr, cols), lambda s, i, c: (s, i, 0))),
        out_shape=jax.ShapeDtypeStruct(got.shape, bf16),
        compiler_params=_cparams(("parallel", "parallel")),
    )(c_idx, g, got)


def sum_chips(t, got, place_idx, name):
    hr, cols = t.shape[1], t.shape[2]
    tr = _step_rows(hr)
    steps = hr // tr

    def body(idx_ref, t_ref, got_ref, o_ref):
        acc = t_ref[...].astype(f32)
        for j in range(3):
            acc = acc + got_ref[j].astype(f32)
        o_ref[...] = acc

    return pl.pallas_call(
        body, name=name,
        grid_spec=pltpu.PrefetchScalarGridSpec(
            num_scalar_prefetch=1, grid=(steps,),
            in_specs=[pl.BlockSpec((None, tr, cols), lambda i, idx: (idx[0], i, 0)),
                      pl.BlockSpec((3, tr, cols), lambda i, idx: (0, i, 0))],
            out_specs=pl.BlockSpec((tr, cols), lambda i, idx: (idx[1] * steps + i, 0))),
        out_shape=jax.ShapeDtypeStruct((2 * hr, cols), f32),
        compiler_params=_cparams(("parallel",)),
    )(place_idx, t, got)


def _share_copies(refs, send_sems, recv_sems):
    x, y, c, *_ = _place()
    return [pltpu.make_async_remote_copy(
        src_ref=ref.at[_half(c, ref.shape[0])], dst_ref=ref.at[_half(c, ref.shape[0])], send_sem=send_sems.at[b],
        recv_sem=recv_sems.at[b], device_id=(x, y, 1 - c), device_id_type=_MESH) for b, ref in enumerate(refs)]


def share_start(bufs, tag):
    n = len(bufs)

    def body(*refs):
        for cp in _share_copies(refs[:n], refs[n], refs[n + 1]):
            cp.start()
        token = refs[-1]
        token[...] = jnp.zeros_like(token)

    outs = pl.pallas_call(
        body, name=f"share_start_{tag}",
        out_shape=(pltpu.SemaphoreType.DMA((n,)), pltpu.SemaphoreType.DMA((n,)),
                   *[pltpu.HBM(b.shape, b.dtype) for b in bufs], jax.ShapeDtypeStruct((8, 128), f32)),
        in_specs=[_HBM] * n, out_specs=(_SEM, _SEM, *[_HBM] * n, _VMEM),
        input_output_aliases={i: 2 + i for i in range(n)},
        compiler_params=pltpu.CompilerParams(has_side_effects=_EFFECT),
    )(*[pltpu.with_memory_space_constraint(b, pltpu.HBM) for b in bufs])
    return (outs[0], outs[1], list(outs[2:2 + n])), outs[-1]


def share_wait(send_sems, recv_sems, bufs, after, tag):
    n = len(bufs)

    def body(*refs):
        x, y, c, *_ = _place()
        for b, ref in enumerate(refs[:n]):
            cp = pltpu.make_async_remote_copy(
                src_ref=ref.at[_half(c, ref.shape[0])], dst_ref=ref.at[_half(1 - c, ref.shape[0])],
                send_sem=refs[n].at[b], recv_sem=refs[n + 1].at[b], device_id=(x, y, 1 - c), device_id_type=_MESH)
            cp.wait_send()
            cp.wait_recv()

    return list(pl.pallas_call(
        body, name=f"share_wait_{tag}",
        out_shape=[pltpu.HBM(b.shape, b.dtype) for b in bufs],
        in_specs=[_HBM] * n + [_SEM, _SEM, _ANY], out_specs=[_HBM] * n,
        input_output_aliases={i: i for i in range(n)},
        compiler_params=pltpu.CompilerParams(has_side_effects=_EFFECT),
    )(*bufs, send_sems, recv_sems, after))


def _scatter_copies(t_refs, land_refs, send_sems, recv_sems):
    x, y, c, me, chips, chip_idx = _place()
    return [pltpu.make_async_remote_copy(
        src_ref=t_refs[b].at[chip_idx[j]], dst_ref=land_refs[b].at[j], send_sem=send_sems.at[3 * b + j],
        recv_sem=recv_sems.at[3 * b + j], device_id=(*chips[j], c), device_id_type=_MESH)
        for j in range(3) for b in range(len(t_refs))]


def scatter_start(ts, tag):
    n = len(ts)
    lands = [lax.empty((3,) + t.shape[1:], t.dtype) for t in ts]

    def body(*refs):
        for cp in _scatter_copies(refs[:n], refs[n:2 * n], refs[2 * n], refs[2 * n + 1]):
            cp.start()
        token = refs[-1]
        token[...] = jnp.zeros_like(token)

    hbm = [pltpu.HBM(a.shape, a.dtype) for a in (*ts, *lands)]
    outs = pl.pallas_call(
        body, name=f"scatter_start_{tag}",
        out_shape=(pltpu.SemaphoreType.DMA((3 * n,)), pltpu.SemaphoreType.DMA((3 * n,)), *hbm,
                   jax.ShapeDtypeStruct((8, 128), f32)),
        in_specs=[_HBM] * (2 * n), out_specs=(_SEM, _SEM, *[_HBM] * (2 * n), _VMEM),
        input_output_aliases={i: 2 + i for i in range(2 * n)},
        compiler_params=pltpu.CompilerParams(has_side_effects=_EFFECT),
    )(*[pltpu.with_memory_space_constraint(a, pltpu.HBM) for a in (*ts, *lands)])
    return outs[0], outs[1], list(outs[2:2 + n]), list(outs[2 + n:2 + 2 * n]), outs[-1]


def scatter_wait(send_sems, recv_sems, ts, lands, after, tag):
    n = len(ts)

    def body(*refs):
        for cp in _scatter_copies(refs[:n], refs[n:2 * n], refs[2 * n], refs[2 * n + 1]):
            cp.wait_send()
            cp.wait_recv()

    outs = pl.pallas_call(
        body, name=f"scatter_wait_{tag}",
        out_shape=[pltpu.HBM(a.shape, a.dtype) for a in (*ts, *lands)],
        in_specs=[_HBM] * (2 * n) + [_SEM, _SEM, _ANY], out_specs=[_HBM] * (2 * n),
        input_output_aliases={i: i for i in range(2 * n)},
        compiler_params=pltpu.CompilerParams(has_side_effects=_EFFECT),
    )(*ts, *lands, send_sems, recv_sems, after)
    return list(outs[:n]), list(outs[n:])


N_SENDERS = 7


def _direct_copies(g_refs, land_refs, send_sems, recv_sems):
    x, y, c, me, chips, chip_idx = _place()
    cps = []
    for b, (g, land) in enumerate(zip(g_refs, land_refs)):
        rows, base = g.shape[1], N_SENDERS * b
        cps.append(pltpu.make_async_remote_copy(
            src_ref=g.at[me, _half(1 - c, rows)], dst_ref=land.at[0], send_sem=send_sems.at[base],
            recv_sem=recv_sems.at[base], device_id=(x, y, 1 - c), device_id_type=_MESH))
        for j in range(3):
            for core in range(2):
                cps.append(pltpu.make_async_remote_copy(
                    src_ref=g.at[chip_idx[j], _half(core, rows)], dst_ref=land.at[1 + 2 * j + c],
                    send_sem=send_sems.at[base + 1 + 2 * j + core], recv_sem=recv_sems.at[base + 1 + 2 * j + c],
                    device_id=(*chips[j], core), device_id_type=_MESH))
    return cps


def direct_start(gs, tag):
    n = len(gs)
    lands = [lax.empty((N_SENDERS, g.shape[1] // 2, g.shape[2]), g.dtype) for g in gs]

    def body(*refs):
        for cp in _direct_copies(refs[:n], refs[n:2 * n], refs[2 * n], refs[2 * n + 1]):
            cp.start()
        token = refs[-1]
        token[...] = jnp.zeros_like(token)

    hbm = [pltpu.HBM(a.shape, a.dtype) for a in (*gs, *lands)]
    outs = pl.pallas_call(
        body, name=f"direct_start_{tag}",
        out_shape=(pltpu.SemaphoreType.DMA((N_SENDERS * n,)), pltpu.SemaphoreType.DMA((N_SENDERS * n,)), *hbm,
                   jax.ShapeDtypeStruct((8, 128), f32)),
        in_specs=[_HBM] * (2 * n), out_specs=(_SEM, _SEM, *[_HBM] * (2 * n), _VMEM),
        input_output_aliases={i: 2 + i for i in range(2 * n)},
        compiler_params=pltpu.CompilerParams(has_side_effects=_EFFECT),
    )(*[pltpu.with_memory_space_constraint(a, pltpu.HBM) for a in (*gs, *lands)])
    return outs[0], outs[1], list(outs[2:2 + n]), list(outs[2 + n:2 + 2 * n]), outs[-1]


def direct_wait(send_sems, recv_sems, gs, lands, after, tag):
    n = len(gs)

    def body(*refs):
        g_refs, land_refs, sends, recvs = refs[:n], refs[n:2 * n], refs[2 * n], refs[2 * n + 1]
        for b in range(n):
            for k in range(N_SENDERS):
                cp = pltpu.make_async_remote_copy(
                    src_ref=g_refs[b].at[0, _half(0, g_refs[b].shape[1])], dst_ref=land_refs[b].at[k],
                    send_sem=sends.at[N_SENDERS * b + k], recv_sem=recvs.at[N_SENDERS * b + k],
                    device_id=_place()[:3], device_id_type=_MESH)
                cp.wait_send()
                cp.wait_recv()

    outs = pl.pallas_call(
        body, name=f"direct_wait_{tag}",
        out_shape=[pltpu.HBM(a.shape, a.dtype) for a in (*gs, *lands)],
        in_specs=[_HBM] * (2 * n) + [_SEM, _SEM, _ANY], out_specs=[_HBM] * (2 * n),
        input_output_aliases={i: i for i in range(2 * n)},
        compiler_params=pltpu.CompilerParams(has_side_effects=_EFFECT),
    )(*gs, *lands, send_sems, recv_sems, after)
    return list(outs[:n]), list(outs[n:])


def sum_senders(g, lands, place_idx, name):
    hr, cols = lands.shape[1], lands.shape[2]
    tr = _step_rows(hr)
    steps = hr // tr

    def body(idx_ref, g_ref, land_ref, o_ref):
        acc = g_ref[...].astype(f32)
        for k in range(N_SENDERS):
            acc = acc + land_ref[k].astype(f32)
        o_ref[...] = acc

    return pl.pallas_call(
        body, name=name,
        grid_spec=pltpu.PrefetchScalarGridSpec(
            num_scalar_prefetch=1, grid=(steps,),
            in_specs=[pl.BlockSpec((None, tr, cols), lambda i, idx: (idx[0], idx[1] * steps + i, 0)),
                      pl.BlockSpec((N_SENDERS, tr, cols), lambda i, idx: (0, i, 0))],
            out_specs=pl.BlockSpec((tr, cols), lambda i, idx: (idx[1] * steps + i, 0))),
        out_shape=jax.ShapeDtypeStruct((2 * hr, cols), f32),
        compiler_params=_cparams(("parallel",)),
    )(place_idx, g, lands)


class GradReducer:
    def __init__(self, c_idx, place_idx):
        self.c_idx, self.place_idx = c_idx, place_idx

    def start(self, bufs, tag, direct=False):
        if direct:
            send_sems, recv_sems, gs, lands, token = direct_start(bufs, tag)
            return (True, send_sems, recv_sems, gs, lands), token
        got = exchange_halves(bufs, tag)
        ts = [add_halves(b, g, self.c_idx, f"add_halves_{tag}{i}") for i, (b, g) in enumerate(zip(bufs, got))]
        send_sems, recv_sems, ts, lands, token = scatter_start(ts, tag)
        return (False, send_sems, recv_sems, ts, lands), token

    def finish(self, state, after, tag):
        direct, *flight = state
        if direct:
            gs, lands = direct_wait(*flight, after, tag)
            sums = [sum_senders(g, l, self.place_idx, f"sum_senders_{tag}{i}") for i, (g, l) in enumerate(zip(gs, lands))]
        else:
            ts, lands = scatter_wait(*flight, after, tag)
            sums = [sum_chips(t, l, self.place_idx, f"sum_chips_{tag}{i}") for i, (t, l) in enumerate(zip(ts, lands))]
        return share_start(sums, tag)

    def collect(self, pending, after, tag):
        return share_wait(*pending, after, tag)


def allreduce_small(sp):
    rows = sp.shape[0]
    hr = rows // 2

    def body(s_ref, out_ref, sib_ref, chip_ref, four_ref, send_sems, recv_sems):
        x, y, c, me, chips, chip_idx = _place()
        sibling = (x, y, 1 - c)
        mine = pl.ds(pl.multiple_of(c * hr, 8), hr)
        other = pl.ds(pl.multiple_of((1 - c) * hr, 8), hr)

        swap = pltpu.make_async_remote_copy(src_ref=s_ref, dst_ref=sib_ref, send_sem=send_sems.at[0],
                                            recv_sem=recv_sems.at[0], device_id=sibling, device_id_type=_MESH)
        swap.start()
        swap.wait()
        is_core0 = c == 0
        chip_ref[...] = jnp.where(is_core0, s_ref[...], sib_ref[...]) + jnp.where(is_core0, sib_ref[...], s_ref[...])

        sends = [pltpu.make_async_remote_copy(
            src_ref=chip_ref.at[mine], dst_ref=four_ref.at[me], send_sem=send_sems.at[1 + j],
            recv_sem=recv_sems.at[1 + j], device_id=(*chips[j], c), device_id_type=_MESH) for j in range(3)]
        for cp in sends:
            cp.start()
        four_ref[me] = chip_ref[mine, :]
        for j in range(3):
            pltpu.make_async_remote_copy(
                src_ref=chip_ref.at[mine], dst_ref=four_ref.at[chip_idx[j]], send_sem=send_sems.at[1 + j],
                recv_sem=recv_sems.at[1 + j], device_id=(*chips[j], c), device_id_type=_MESH).wait_recv()
        for cp in sends:
            cp.wait_send()
        out_ref[mine, :] = (four_ref[0] + four_ref[1]) + (four_ref[2] + four_ref[3])

        share = pltpu.make_async_remote_copy(src_ref=out_ref.at[mine], dst_ref=out_ref.at[mine], send_sem=send_sems.at[4],
                                             recv_sem=recv_sems.at[4], device_id=sibling, device_id_type=_MESH)
        share.start()
        pltpu.make_async_remote_copy(src_ref=out_ref.at[mine], dst_ref=out_ref.at[other], send_sem=send_sems.at[4],
                                     recv_sem=recv_sems.at[4], device_id=sibling, device_id_type=_MESH).wait_recv()
        share.wait_send()

    return pl.pallas_call(
        body, name="allreduce_small",
        out_shape=jax.ShapeDtypeStruct(sp.shape, sp.dtype),
        in_specs=[_VMEM], out_specs=_VMEM,
        scratch_shapes=[pltpu.VMEM(sp.shape, sp.dtype), pltpu.VMEM(sp.shape, sp.dtype),
                        pltpu.VMEM((N_CHIPS, hr, sp.shape[1]), sp.dtype),
                        pltpu.SemaphoreType.DMA((5,)), pltpu.SemaphoreType.DMA((5,))],
        compiler_params=_cparams(),
    )(sp)


def _n_rows(shape):
    n = 1
    for d in shape:
        n *= d
    return 8 * (-(-n // 8192))


def _pack(arrays, total_rows):
    parts = []
    for a in arrays:
        flat = a.reshape(-1)
        parts.append(jnp.pad(flat, (0, 1024 * _n_rows(a.shape) - flat.shape[0])).reshape(-1, 1024))
    rows = jnp.concatenate(parts, axis=0)
    return jnp.pad(rows, ((0, total_rows - rows.shape[0]), (0, 0)))


def _unpack(packed, shapes):
    out, r = [], 0
    for shp in shapes:
        n = 1
        for d in shp:
            n *= d
        nr = _n_rows(shp)
        out.append(packed[r:r + nr].reshape(-1)[:n].reshape(shp))
        r += nr
    return out


_COLUMN_SHARDED = ("w_in_even", "w_qkv")
IN_SHARD, IN_PAD = 1284, 1408
QKV_SHARD, QKV_PAD = 320, 384


def _lane_padded(a, cols):
    return jnp.pad(a, ((0, 0), (0, cols - a.shape[1])))


_SMALL_SHAPES = (
    ("norm_mix_g", (2, 1024)), ("norm_mlp_g", (2, 1024)), ("final_norm_g", (1024,)), ("gm_ln_g", (1, 1024)),
    ("gm_ln_b", (1, 1024)), ("gm_w_s", (1, 8, 128, 128)), ("gm_b_s", (1, 8, 128)), ("ssm_conv_b", (1, 2048)),
    ("ssm_dt_bias", (1, 16)), ("ssm_a_log", (1, 16)), ("ssm_d", (1, 16)), ("ssm_norm_g", (1, 1024)),
    ("attn_sinks", (1, 16)), ("ssm_conv_w", (1, 4, 2048)), ("b_qkv", (1, 1280)), ("b_o", (1, 1024)),
)
_N_REPLICATED = 13
_SHARDED_SMALL = (("ssm_conv_w", 2, 512), ("b_qkv", 1, 320), ("b_o", 1, 256))
_SHARD_PACK_ROWS = 32


def _cols_by_owner(a):
    return a.transpose(1, 0, 2).reshape(a.shape[1], -1)


class WeightGatherer:
    def __init__(self, w, chip_idx):
        def place(tag, b, dtype=bf16, after=None):
            return place_shard(b, chip_idx, f"place_shard_{tag}", dtype, after)

        sems_in, bufs_in, self.started = gather_start([
            [place("in", _lane_padded(w["w_in_even"][0], IN_PAD)),
             place("small", _pack([w[n] for n, _, _ in _SHARDED_SMALL], _SHARD_PACK_ROWS), f32)]], "in")
        t = self.started
        sems, bufs, self.all_started = gather_start([
            [place("out", w["w_out_even"][0], after=t), place("up0", w["w_up"][0], after=t),
             place("down0", w["w_down"][0], after=t)],
            [place("qkv", _lane_padded(w["w_qkv"][0], QKV_PAD), after=t), place("o", w["w_o"][0], after=t),
             place("up1", w["w_up"][1], after=t), place("down1", w["w_down"][1], after=t)],
        ], "rest")
        self.sems, self.bufs = sems_in + sems, bufs_in + bufs

    def _group(self, gi, after, tag):
        return gather_forward(gather_wait(self.bufs[gi], self.sems[gi], after, tag), tag)

    def mixer_in(self, after):
        g, small = self._group(0, [after, self.all_started], "in")
        shard_shapes = [tuple(width if i == axis else d for i, d in enumerate(dict(_SMALL_SHAPES)[n]))
                        for n, axis, width in _SHARDED_SMALL]
        per_chip = [_unpack(small[s], shard_shapes) for s in range(N_CHIPS)]
        full = {n: jnp.concatenate([per_chip[s][i] for s in range(N_CHIPS)], axis=axis)
                for i, (n, axis, _) in enumerate(_SHARDED_SMALL)}
        w_in = jnp.concatenate([g[s, :, :IN_SHARD] for s in range(N_CHIPS)], axis=1)
        return _lane_padded(w_in, NP_IN), full

    def layer0(self, after):
        w_out, w_up, w_down = self._group(1, [after], "l0")
        return w_out.reshape(2048, 1024), w_up, w_down.reshape(4096, 1024)

    def layer1(self, after):
        q, w_o, w_up, w_down = self._group(2, [after], "l1")
        w_qkv = jnp.concatenate([q[s, :, :QKV_SHARD] for s in range(N_CHIPS)], axis=1)
        return w_qkv, w_o.reshape(1024, 1024), w_up, w_down.reshape(4096, 1024)


def _row2(v):
    return v.reshape(1, -1)


def _lane_pad(v):
    return jnp.pad(v, ((0, 0), (0, CH - v.shape[1])))


_H_AND_NORM = (("tile", f32), ("tile", bf16))
_DX_AND_DG = (("tile", f32), ("sum", D_MODEL))


def _mlp_bwd(dh_out, h, g_row, y, a, w_up, w_down, tag, after=None):
    da = matmul(dh_out, w_down, dims="nt", name=f"mlp_da{tag}", out_dtype=bf16, tn=1024,
                epi=_times_relu2_grad, epi_args=(("tile", a),), after=after)
    dw_down = matmul(a, dh_out, dims="tn", name=f"mlp_dwdown{tag}", out_dtype=bf16, a_pro=_relu2)
    dw_up = matmul(y, da, dims="tn", name=f"mlp_dwup{tag}", out_dtype=bf16, tn=1024, out_by_col_tile=True)
    dh, dg = matmul_rows(da, w_up, dims="nt", name=f"mlp_dy{tag}", epi=_norm_bwd_res,
                         epi_args=(("tile", h), ("row", g_row), ("tile", dh_out)), outs=_DX_AND_DG)
    return dh, dg, dw_up, dw_down


def _by_owner(a):
    return a.reshape(N_CHIPS, a.shape[0] // N_CHIPS, a.shape[1])


def _col_shards(a, shard, padded):
    return jnp.stack([_lane_padded(a[:, shard * s: shard * (s + 1)], padded) for s in range(N_CHIPS)])


def _local_step(x, target, weights, sm, reducer):
    w_up, w_down = [None, None], [None, None]
    mix_g = [_row2(sm["norm_mix_g"][i]) for i in range(2)]
    y0 = rmsnorm_fwd(x, mix_g[0] + weights.started[:1, :1], "mix_norm0")
    w_in_p, sharded_small = weights.mixer_in(y0)
    sm = {**sm, **sharded_small}
    mlp_g = [_row2(sm["norm_mlp_g"][i]) for i in range(2)]
    mixer_prm = {
        "ln_g": sm["gm_ln_g"], "ln_b": sm["gm_ln_b"], "wm": sm["gm_w_s"][0],
        "bs_t": jnp.pad(sm["gm_b_s"][0].T, ((0, 0), (0, CH - N_BLK))),
        "conv_w": jnp.pad(sm["ssm_conv_w"][0], ((0, 4), (0, 0))), "conv_b": sm["ssm_conv_b"],
        "dt_bias": _lane_pad(sm["ssm_dt_bias"]), "a_log": _lane_pad(sm["ssm_a_log"]),
        "d_heads": _lane_pad(sm["ssm_d"]), "norm_g": sm["ssm_norm_g"],
    }
    sink_row = _lane_pad(sm["attn_sinks"])

    proj = matmul(y0, w_in_p, dims="nn", name="in_proj", tn=768)
    ab, hstates = mixer_fwd(proj, mixer_prm)
    w_out, w_up[0], w_down[0] = weights.layer0(ab)
    h1, y1 = matmul_rows(ab, w_out, dims="nn", name="out_proj", epi=_res_norm,
                         epi_args=(("tile", x), ("row", mlp_g[0])), outs=_H_AND_NORM)
    a1 = matmul(y1, w_up[0], dims="nn", name="mlp_up0", out_dtype=bf16, tn=1024)
    w_qkv, w_o, w_up[1], w_down[1] = weights.layer1(a1)
    h2, y2 = matmul_rows(a1, w_down[0], dims="nn", name="mlp_down0", a_pro=_relu2, epi=_res_norm,
                         epi_args=(("tile", h1), ("row", mix_g[1])), outs=_H_AND_NORM)
    qkv = matmul(y2, w_qkv, dims="nn", name="qkv_proj", tn=QKV_DIM, epi=_add_bias, epi_args=(("row", sm["b_qkv"]),))
    att = attn_fwd(qkv, sink_row)
    h3, y3 = matmul_rows(att, w_o, dims="nn", name="o_proj", epi=_bias_res_norm,
                         epi_args=(("row", sm["b_o"]), ("tile", h2), ("row", mlp_g[1])), outs=_H_AND_NORM)
    a3 = matmul(y3, w_up[1], dims="nn", name="mlp_up1", out_dtype=bf16, tn=1024)
    dh4, dg_final, loss = matmul_rows(
        a3, w_down[1], dims="nn", name="mlp_down1", a_pro=_relu2, epi=_res_norm_loss,
        epi_args=(("tile", h3), ("row", _row2(sm["final_norm_g"])), ("tile", target)),
        outs=(("tile", f32), ("sum", D_MODEL), ("sum", 128)))

    dh3, dg_mlp1, dw_up1, dw_down1 = _mlp_bwd(dh4, h3, mlp_g[1], y3, a3, w_up[1], w_down[1], 1)
    db_o = colsum(dh3, "db_o")
    datt = matmul(dh3, w_o, dims="nt", name="attn_dout", out_dtype=bf16)
    dw_o = matmul(att, dh3, dims="tn", name="dw_o", out_dtype=bf16)
    dqkv, dsink = attn_bwd(qkv, sink_row, datt)
    db_qkv = colsum(dqkv, "db_qkv")
    dw_qkv = matmul(y2, dqkv, dims="tn", name="dw_qkv", out_dtype=bf16, tn=QKV_DIM)
    dh2, dg_mix1 = matmul_rows(dqkv, w_qkv, dims="nt", name="dy_qkv", epi=_norm_bwd_res,
                               epi_args=(("tile", h2), ("row", mix_g[1]), ("tile", dh3)), outs=_DX_AND_DG)
    layer1 = [jnp.concatenate([_by_owner(dw_o), dw_up1, _by_owner(dw_down1)], axis=1),
              _col_shards(dw_qkv, QKV_SHARD, QKV_PAD)]
    flight1, token1 = reducer.start(layer1, "l1", direct=True)
    dh1, dg_mlp0, dw_up0, dw_down0 = _mlp_bwd(dh2, h1, mlp_g[0], y1, a1, w_up[0], w_down[0], 0, after=token1)
    pending1, shared1 = reducer.finish(flight1, dh1, "l1")
    dw_out = matmul(ab, dh1, dims="tn", name="dw_out", out_dtype=bf16, after=shared1)
    flight0, token0 = reducer.start(
        [jnp.concatenate([dw_up0, _by_owner(dw_down0), _by_owner(dw_out)], axis=1)], "l0", direct=True)
    dab = matmul(dh1, w_out, dims="nt", name="mixer_dout", tn=1024, after=token0)
    dproj, dmix = mixer_bwd(proj, hstates, dab, mixer_prm)
    dw_in_p = matmul(y0, dproj, dims="tn", name="dw_in", out_dtype=bf16, tn=768)
    pending0, shared0 = reducer.finish(flight0, dw_in_p, "l0")
    flight_in, token_in = reducer.start([_col_shards(dw_in_p, IN_SHARD, IN_PAD)], "in")
    dx, dg_mix0 = matmul_rows(dproj, w_in_p, dims="nt", name="dy_in", tm=256, epi=_norm_bwd_res,
                              epi_args=(("tile", x), ("row", mix_g[0]), ("tile", dh1)), outs=_DX_AND_DG,
                              after=token_in + shared0)
    pending_in, _ = reducer.finish(flight_in, dx, "in")
    r_l1, r_qkv = reducer.collect(pending1, dx, "l1")
    (r_l0,) = reducer.collect(pending0, dx, "l0")
    (r_in,) = reducer.collect(pending_in, dx, "in")
    reduced = {
        "w_out_even": r_l0[None, 2048:], "w_in_even": r_in[None, :, :IN_SHARD], "w_qkv": r_qkv[None, :, :QKV_SHARD],
        "w_o": r_l1[None, :256], "w_up": jnp.stack([r_l0[:1024], r_l1[256:1280]]),
        "w_down": jnp.stack([r_l0[1024:2048], r_l1[1280:]]),
    }

    small_grads = {
        "norm_mix_g": jnp.concatenate([dg_mix0, dg_mix1], axis=0),
        "norm_mlp_g": jnp.concatenate([dg_mlp0, dg_mlp1], axis=0),
        "final_norm_g": dg_final[0], "gm_ln_g": dmix["ln_g"], "gm_ln_b": dmix["ln_b"],
        "gm_w_s": dmix["wm"][None], "gm_b_s": dmix["bs_t"][:, :N_BLK].T[None],
        "ssm_conv_b": dmix["conv_b"], "ssm_dt_bias": dmix["dt_bias"][:, :SSM_HEADS],
        "ssm_a_log": dmix["a_log"][:, :SSM_HEADS], "ssm_d": dmix["d_heads"][:, :SSM_HEADS],
        "ssm_norm_g": dmix["norm_g"], "attn_sinks": dsink[:, :SSM_HEADS],
        "ssm_conv_w": dmix["conv_w"][None, :4], "b_qkv": db_qkv, "b_o": db_o,
    }
    return loss, dx, reduced, small_grads


def kernel(x, norm_mix_g, norm_mlp_g, final_norm_g, w_in_even, w_out_even, gm_ln_g, gm_ln_b, gm_w_s, gm_b_s, ssm_conv_w, ssm_conv_b, ssm_dt_bias, ssm_a_log, ssm_d, ssm_norm_g, w_qkv, b_qkv, w_o, b_o, attn_sinks, w_up, w_down, loss_target, m_norm_mix_g, m_norm_mlp_g, m_final_norm_g, m_w_in_even, m_w_out_even, m_gm_ln_g, m_gm_ln_b, m_gm_w_s, m_gm_b_s, m_ssm_conv_w, m_ssm_conv_b, m_ssm_dt_bias, m_ssm_a_log, m_ssm_d, m_ssm_norm_g, m_w_qkv, m_b_qkv, m_w_o, m_b_o, m_attn_sinks, m_w_up, m_w_down, v_norm_mix_g, v_norm_mlp_g, v_final_norm_g, v_w_in_even, v_w_out_even, v_gm_ln_g, v_gm_ln_b, v_gm_w_s, v_gm_b_s, v_ssm_conv_w, v_ssm_conv_b, v_ssm_dt_bias, v_ssm_a_log, v_ssm_d, v_ssm_norm_g, v_w_qkv, v_b_qkv, v_w_o, v_b_o, v_attn_sinks, v_w_up, v_w_down):
    w = dict(norm_mix_g=norm_mix_g, norm_mlp_g=norm_mlp_g, final_norm_g=final_norm_g, w_in_even=w_in_even,
             w_out_even=w_out_even, gm_ln_g=gm_ln_g, gm_ln_b=gm_ln_b, gm_w_s=gm_w_s, gm_b_s=gm_b_s,
             ssm_conv_w=ssm_conv_w, ssm_conv_b=ssm_conv_b, ssm_dt_bias=ssm_dt_bias, ssm_a_log=ssm_a_log,
             ssm_d=ssm_d, ssm_norm_g=ssm_norm_g, w_qkv=w_qkv, b_qkv=b_qkv, w_o=w_o, b_o=b_o,
             attn_sinks=attn_sinks, w_up=w_up, w_down=w_down)
    m = dict(norm_mix_g=m_norm_mix_g, norm_mlp_g=m_norm_mlp_g, final_norm_g=m_final_norm_g,
             w_in_even=m_w_in_even, w_out_even=m_w_out_even, gm_ln_g=m_gm_ln_g, gm_ln_b=m_gm_ln_b,
             gm_w_s=m_gm_w_s, gm_b_s=m_gm_b_s, ssm_conv_w=m_ssm_conv_w, ssm_conv_b=m_ssm_conv_b,
             ssm_dt_bias=m_ssm_dt_bias, ssm_a_log=m_ssm_a_log, ssm_d=m_ssm_d, ssm_norm_g=m_ssm_norm_g,
             w_qkv=m_w_qkv, b_qkv=m_b_qkv, w_o=m_w_o, b_o=m_b_o, attn_sinks=m_attn_sinks, w_up=m_w_up,
             w_down=m_w_down)
    v = dict(norm_mix_g=v_norm_mix_g, norm_mlp_g=v_norm_mlp_g, final_norm_g=v_final_norm_g,
             w_in_even=v_w_in_even, w_out_even=v_w_out_even, gm_ln_g=v_gm_ln_g, gm_ln_b=v_gm_ln_b,
             gm_w_s=v_gm_w_s, gm_b_s=v_gm_b_s, ssm_conv_w=v_ssm_conv_w, ssm_conv_b=v_ssm_conv_b,
             ssm_dt_bias=v_ssm_dt_bias, ssm_a_log=v_ssm_a_log, ssm_d=v_ssm_d, ssm_norm_g=v_ssm_norm_g,
             w_qkv=v_w_qkv, b_qkv=v_b_qkv, w_o=v_w_o, b_o=v_b_o, attn_sinks=v_attn_sinks, w_up=v_w_up,
             w_down=v_w_down)
    names = ("norm_mix_g", "norm_mlp_g", "final_norm_g", "w_in_even", "w_out_even", "gm_ln_g", "gm_ln_b",
             "gm_w_s", "gm_b_s", "ssm_conv_w", "ssm_conv_b", "ssm_dt_bias", "ssm_a_log", "ssm_d", "ssm_norm_g",
             "w_qkv", "b_qkv", "w_o", "b_o", "attn_sinks", "w_up", "w_down")

    cx, cy, cc = lax.axis_index("x"), lax.axis_index("y"), lax.axis_index("c")
    chip = 2 * cx + cy
    c_idx = jnp.reshape(cc, (1,)).astype(jnp.int32)
    chip_idx = jnp.reshape(chip, (1,)).astype(jnp.int32)

    weights = WeightGatherer(w, chip_idx)
    sm = {n: w[n] for n, _ in _SMALL_SHAPES[:_N_REPLICATED]}

    reducer = GradReducer(c_idx, jnp.concatenate([chip_idx, c_idx]))
    loss_part, dx, grads, small_grads = _local_step(x[0], loss_target[0], weights, sm, reducer)

    small_sum = allreduce_small(_pack([small_grads[n] for n, _ in _SMALL_SHAPES] + [loss_part], SMALL_ROWS))
    *small_list, loss_row = _unpack(small_sum, [s for _, s in _SMALL_SHAPES] + [loss_part.shape])
    loss = loss_row[0, 0]
    small_full = dict(zip([n for n, _ in _SMALL_SHAPES], small_list))
    for n, _ in _SMALL_SHAPES[:_N_REPLICATED]:
        grads[n] = small_full[n]
    for n, axis, width in _SHARDED_SMALL:
        grads[n] = lax.dynamic_slice_in_dim(small_full[n], chip * width, width, axis)
    grads = {n: grads[n].reshape(w[n].shape) for n in names}

    delta, new_m, new_v = {}, {}, {}
    for n in names:
        if n in _COLUMN_SHARDED:
            args = [jnp.transpose(d[n], (2, 0, 1)) for d in (w, grads, m, v)]
            grads[n] = jnp.transpose(args[1], (1, 2, 0))
            outs = adamw(*args, f"adamw_{n}")
            delta[n], new_m[n], new_v[n] = (jnp.transpose(o, (1, 2, 0)) for o in outs)
            continue
        shape = (1,) + w[n].shape if w[n].ndim == 1 else w[n].shape
        outs = adamw(*[d[n].reshape(shape) for d in (w, grads, m, v)], f"adamw_{n}")
        delta[n], new_m[n], new_v[n] = (o.reshape(w[n].shape) for o in outs)

    return (loss, dx[None], *[grads[n] for n in names], *[delta[n] for n in names],
            *[new_m[n] for n in names], *[new_v[n] for n in names])
```

```python
import functools

import jax
import jax.numpy as jnp
from jax import lax
from jax.experimental import pallas as pl
from jax.experimental.pallas import tpu as pltpu

f32 = jnp.float32
bf16 = jnp.bfloat16
MXU_DTYPE = bf16

RMS_EPS = 1e-5
LN_EPS = 1e-5
D_MODEL = 1024
D_FF = 4096
CH = 128
N_BLK = 8
SSM_HEADS = 16
IN_EVEN = 5136
NP_IN = 5376
OFF_U, OFF_V, OFF_Z, OFF_X, OFF_DT = 0, 1024, 2048, 3072, 5120
XBC_BLKS = 16
QKV_DIM = 1280
ATT_SCALE = 64 ** -0.5

ADAM_LR = 0.001
ADAM_B1 = 0.9
ADAM_B2 = 0.999
ADAM_EPS = 1e-08
ADAM_WD = 0.01
ADAM_STEP = 10

VMEM_LIMIT_BYTES = 48 * 1024 * 1024
N_CHIPS = 4
SMALL_ROWS = 256

NN = ((1,), (0,))
NT = ((1,), (1,))
TN = ((0,), (0,))


def _mm(a, b, dims):
    return lax.dot_general(a.astype(MXU_DTYPE), b.astype(MXU_DTYPE), (dims, ((), ())),
                           preferred_element_type=f32)


def _mm_exact(a, b):
    return jnp.dot(a, b, preferred_element_type=f32, precision=lax.Precision.HIGHEST)


def _cparams(sem=None):
    return pltpu.CompilerParams(dimension_semantics=sem, vmem_limit_bytes=VMEM_LIMIT_BYTES)


@jax.custom_vjp
def _swap64(x):
    return pltpu.roll(x, 64, axis=1)


_swap64.defvjp(lambda x: (pltpu.roll(x, 64, axis=1), None), lambda _, g: (pltpu.roll(g, 64, axis=1),))


def _row_blocks_of(x):
    return tuple(x[i:i + CH] for i in range(0, x.shape[0], CH))


@jax.custom_vjp
def _row_blocks(x):
    return _row_blocks_of(x)


_row_blocks.defvjp(lambda x: (_row_blocks_of(x), None), lambda _, gs: (jnp.concatenate(gs, axis=0),))


def _make_delay(k):
    @jax.custom_vjp
    def delay(ext):
        return pltpu.roll(ext, k, axis=0)[8:, :]

    def fwd(ext):
        return delay(ext), None

    def bwd(_, g):
        gp = jnp.concatenate([jnp.zeros((8, g.shape[1]), g.dtype), g], axis=0)
        return (pltpu.roll(gp, gp.shape[0] - k, axis=0),)

    delay.defvjp(fwd, bwd)
    return delay


_DELAYS = {k: _make_delay(k) for k in (1, 2, 3)}


_GELU_C = 0.7978845608028654
_GELU_K = 0.044715


@jax.custom_vjp
def _gelu(x):
    return 0.5 * x * (1.0 + jnp.tanh(_GELU_C * (x + _GELU_K * (x * x * x))))


def _gelu_fwd(x):
    t = jnp.tanh(_GELU_C * (x + _GELU_K * (x * x * x)))
    return 0.5 * x * (1.0 + t), (x, t)


def _gelu_bwd(res, g):
    x, t = res
    dz = _GELU_C + (3.0 * _GELU_C * _GELU_K) * (x * x)
    return (g * (0.5 * (1.0 + t) + (0.5 * x) * (1.0 - t * t) * dz),)


_gelu.defvjp(_gelu_fwd, _gelu_bwd)


def _col(m, lane, h):
    return jnp.sum(jnp.where(lane == h, m, 0.0), axis=1, keepdims=True)


def _row(m, sub, h):
    return jnp.sum(jnp.where(sub == h, m, 0.0), axis=0, keepdims=True)


def _mixer_chunk(us, vs, zs, xbcs, halos, dtblk, hps, prm):
    lane = lax.broadcasted_iota(jnp.int32, (CH, CH), 1)
    sub = lax.broadcasted_iota(jnp.int32, (CH, CH), 0)
    left = lane < 64
    top = sub < 64
    causal = sub >= lane

    gus = [_gelu(u) for u in us]
    gvs = [_gelu(v) for v in vs]
    mu = sum(jnp.sum(g, axis=1, keepdims=True) for g in gvs) / D_MODEL
    cen = [g - mu for g in gvs]
    var = sum(jnp.sum(c * c, axis=1, keepdims=True) for c in cen) / D_MODEL
    rstd = lax.rsqrt(var + LN_EPS)
    a_out = []
    for g in range(N_BLK):
        vn = cen[g] * rstd * prm["ln_g"][g] + prm["ln_b"][g]
        w = jnp.where(causal, prm["wm"][g], 0.0)
        mixed = _mm(w, vn, NN) + _col(prm["bs_t"], lane, g)
        a_out.append(gus[g] * mixed)

    act = []
    for b in range(XBC_BLKS):
        w8 = prm["conv_w"][b]
        sub8 = lax.broadcasted_iota(jnp.int32, w8.shape, 0)
        ext = jnp.concatenate([halos[b], xbcs[b]], axis=0)
        conv = xbcs[b] * _row(w8, sub8, 3) + prm["conv_b"][b]
        for k in (1, 2, 3):
            conv = conv + _DELAYS[k](ext) * _row(w8, sub8, 3 - k)
        act.append(jax.nn.silu(conv))

    dt = jax.nn.softplus(dtblk + prm["dt_bias"])
    a_neg = -jnp.exp(prm["a_log"])
    tri = causal.astype(f32)
    acum = _mm_exact(tri, dt * a_neg)
    acum_t = acum.T
    dt_t = dt.T
    last = sub == CH - 1
    ys, h_out = [], []
    for grp in range(4):
        bm = act[8 + grp]
        cm = act[12 + grp]
        cb = _mm(cm, bm, NT)
        for p in (2 * grp, 2 * grp + 1):
            h0, h1 = 2 * p, 2 * p + 1
            xp = act[p]
            hp = hps[p]
            wis = []
            for h in (h0, h1):
                seg = _col(acum, lane, h) - _row(acum_t, sub, h)
                decay = jnp.exp(jnp.where(causal, seg, -jnp.inf))
                wis.append(cb * decay * _row(dt_t, sub, h))
            wcat = jnp.concatenate(wis, axis=1)
            xbd = jnp.concatenate([jnp.where(left, xp, 0.0), jnp.where(left, 0.0, xp)], axis=0)
            y_diag = _mm(wcat, xbd, NN)
            a_end = [jnp.sum(jnp.where(last & (lane == h), acum, 0.0), keepdims=True) for h in (h0, h1)]
            a_col = jnp.where(left, _col(acum, lane, h0), _col(acum, lane, h1))
            dt_col = jnp.where(left, _col(dt, lane, h0), _col(dt, lane, h1))
            to_end = jnp.exp(jnp.where(left, a_end[0], a_end[1]) - a_col) * dt_col
            states = _mm(xp * to_end, bm, TN)
            chunk_decay = jnp.where(top, jnp.exp(a_end[0]), jnp.exp(a_end[1]))
            h_out.append(chunk_decay * hp + states)
            y_off = jnp.exp(a_col) * _mm(cm, hp, NT)
            d_skip = jnp.where(left[:1], _col(prm["d_heads"], lane[:1], h0), _col(prm["d_heads"], lane[:1], h1))
            ys.append((y_diag + y_off + xp * d_skip) * jax.nn.silu(zs[p]))

    b_out = []
    for grp in range(4):
        pair = (ys[2 * grp], ys[2 * grp + 1])
        ms = sum(jnp.sum(y * y, axis=1, keepdims=True) for y in pair) / 256.0
        r = lax.rsqrt(ms + RMS_EPS)
        for j, y in enumerate(pair):
            b_out.append(y * r * prm["norm_g"][2 * grp + j])
    return a_out, b_out, h_out


def _attn_block(qps, kprev, kcur, vprev, vcur, sink_row, first):
    lane = lax.broadcasted_iota(jnp.int32, (CH, CH), 1)
    left = lane < 64
    row = lax.broadcasted_iota(jnp.int32, (2 * CH, CH), 0)
    key = lax.broadcasted_iota(jnp.int32, (2 * CH, CH), 1)
    own = key <= (row & (CH - 1))

    def both_halves(a):
        sw = _swap64(a)
        return [jnp.where(left, a, sw), jnp.where(left, sw, a)]

    kc, kp, vc, vp = both_halves(kcur), both_halves(kprev), both_halves(vcur), both_halves(vprev)
    outs = []
    for p in range(N_BLK):
        j = p // 4
        q2 = jnp.concatenate([jnp.where(left, qps[p], 0.0), jnp.where(left, 0.0, qps[p])], axis=0)
        sink = jnp.concatenate([jnp.broadcast_to(_col(sink_row, lane[:1], h), (CH, 1)) for h in (2 * p, 2 * p + 1)],
                               axis=0)
        s_prev = jnp.where(first, -jnp.inf, _mm(q2, kp[j], NT) * ATT_SCALE)
        s = jnp.where(own, _mm(q2, kc[j], NT) * ATT_SCALE, s_prev)
        m = lax.stop_gradient(jnp.maximum(jnp.max(s, axis=1, keepdims=True), sink))
        pexp = jnp.exp(s - m)
        probs = pexp / (jnp.sum(pexp, axis=1, keepdims=True) + jnp.exp(sink - m))
        o = _row_blocks(_mm(jnp.where(own, probs, 0.0), vc[j], NN) + _mm(jnp.where(own, 0.0, probs), vp[j], NN))
        outs.append(jnp.where(left, o[0], o[1]))
    return outs


def _rmsnorm(x, g):
    r = lax.rsqrt(jnp.mean(x * x, axis=-1, keepdims=True) + RMS_EPS)
    return x * r * g


def rmsnorm_fwd(x, g_row, name):
    s, d = x.shape
    tm = min(512, s)

    def body(x_ref, g_ref, y_ref):
        y_ref[...] = _rmsnorm(x_ref[...], g_ref[...]).astype(bf16)

    return pl.pallas_call(
        body, name=name, grid=(s // tm,),
        in_specs=[pl.BlockSpec((tm, d), lambda i: (i, 0)), pl.BlockSpec((1, d), lambda i: (0, 0))],
        out_specs=pl.BlockSpec((tm, d), lambda i: (i, 0)),
        out_shape=jax.ShapeDtypeStruct((s, d), bf16),
        compiler_params=_cparams(("parallel",)),
    )(x, g_row)


def colsum(x, name):
    s, n = x.shape
    tm = min(512, s)

    def body(x_ref, o_ref):
        @pl.when(pl.program_id(0) == 0)
        def _():
            o_ref[...] = jnp.zeros_like(o_ref)

        o_ref[...] += jnp.sum(x_ref[...].astype(f32), axis=0, keepdims=True)

    return pl.pallas_call(
        body, name=name, grid=(s // tm,),
        in_specs=[pl.BlockSpec((tm, n), lambda i: (i, 0))],
        out_specs=pl.BlockSpec((1, n), lambda i: (0, 0)),
        out_shape=jax.ShapeDtypeStruct((1, n), f32),
        compiler_params=_cparams(("arbitrary",)),
    )(x)


def _fit(dim, want):
    if dim <= want:
        return dim
    t = want
    while dim % t:
        t -= 128
    return t


def matmul(a, b, *, dims, name, out_dtype=f32, tm=1024, tn=512, tk=8192, a_pro=None, epi=None, epi_args=(),
           out_by_col_tile=False, after=None):
    if dims == "nn" and b.ndim == 3:
        (m, k), n, tn = a.shape, b.shape[0] * b.shape[2], b.shape[2]
    elif dims == "nn":
        (m, k), n = a.shape, b.shape[1]
    elif dims == "nt":
        (m, k), n = a.shape, b.shape[0]
    else:
        (k, m), n = a.shape, b.shape[1]
    tm, tn, tk = _fit(m, tm), _fit(n, tn), _fit(k, tk)
    nk = k // tk
    if dims == "nn":
        a_spec = pl.BlockSpec((tm, tk), lambda i, j, kk: (i, kk))
        b_spec = (pl.BlockSpec((None, tk, tn), lambda i, j, kk: (j, kk, 0)) if b.ndim == 3
                  else pl.BlockSpec((tk, tn), lambda i, j, kk: (kk, j)))
        dn = NN
    elif dims == "nt":
        a_spec = pl.BlockSpec((tm, tk), lambda i, j, kk: (i, kk))
        b_spec = pl.BlockSpec((tn, tk), lambda i, j, kk: (j, kk))
        dn = NT
    else:
        a_spec = pl.BlockSpec((tk, tm), lambda i, j, kk: (kk, i))
        b_spec = pl.BlockSpec((tk, tn), lambda i, j, kk: (kk, j))
        dn = TN
    e_specs = [pl.BlockSpec((tm, tn), lambda i, j, kk: (i, j)) if kind == "tile"
               else pl.BlockSpec((1, tn), lambda i, j, kk: (0, j)) for kind, _ in epi_args]
    n_epi = len(epi_args)
    order_specs = [] if after is None else [pl.BlockSpec((8, 128), lambda i, j, kk: (0, 0))]
    order_args = [] if after is None else [after]

    def body(*refs):
        a_ref, b_ref = refs[0], refs[1]
        e_refs = refs[2:2 + n_epi]
        n_in = 2 + n_epi + len(order_args)
        o_ref = refs[n_in]
        av = a_ref[...]
        if a_pro is not None:
            av = a_pro(av)
        part = _mm(av, b_ref[...], dn)

        def finish(acc):
            if epi is not None:
                acc = epi(acc, *[r[...] for r in e_refs])
            o_ref[...] = acc.astype(out_dtype)

        if nk == 1:
            finish(part)
        else:
            acc_ref = refs[n_in + 1]
            kk = pl.program_id(2)

            @pl.when(kk == 0)
            def _():
                acc_ref[...] = part

            @pl.when(kk > 0)
            def _():
                acc_ref[...] += part

            @pl.when(kk == nk - 1)
            def _():
                finish(acc_ref[...])

    if out_by_col_tile:
        out_spec = pl.BlockSpec((None, tm, tn), lambda i, j, kk: (j, i, 0))
        out_shape = jax.ShapeDtypeStruct((n // tn, m, tn), out_dtype)
    else:
        out_spec = pl.BlockSpec((tm, tn), lambda i, j, kk: (i, j))
        out_shape = jax.ShapeDtypeStruct((m, n), out_dtype)
    return pl.pallas_call(
        body, name=name, grid=(m // tm, n // tn, nk),
        in_specs=[a_spec, b_spec] + e_specs + order_specs,
        out_specs=out_spec,
        out_shape=out_shape,
        scratch_shapes=[pltpu.VMEM((tm, tn), f32)] if nk > 1 else [],
        compiler_params=_cparams(("parallel", "parallel", "arbitrary")),
    )(a, b, *[arr for _, arr in epi_args], *order_args)


def _relu2(a):
    r = jnp.maximum(a.astype(f32), 0.0)
    return r * r


def _add(acc, t):
    return acc + t


def _add_bias(acc, t):
    return acc + t


def _add_bias_res(acc, bias, res):
    return acc + bias + res


def _times_relu2_grad(acc, a):
    return acc * (2.0 * jnp.maximum(a.astype(f32), 0.0))


def matmul_rows(a, b, *, dims, name, epi, epi_args, outs, tm=512, a_pro=None, after=None):
    m, k = a.shape
    n = b.shape[-1] if dims == "nn" else b.shape[-2]
    tm = _fit(m, tm)
    dn = NN if dims == "nn" else NT
    e_specs = [pl.BlockSpec((tm, arr.shape[1]), lambda i: (i, 0)) if kind == "tile"
               else pl.BlockSpec((1, arr.shape[1]), lambda i: (0, 0)) for kind, arr in epi_args]
    order_specs = [] if after is None else [pl.BlockSpec((8, 128), lambda i: (0, 0))]
    order_args = [] if after is None else [after]
    n_in = 2 + len(epi_args) + len(order_args)

    def body(*refs):
        av = refs[0][...]
        if a_pro is not None:
            av = a_pro(av)
        if b.ndim == 3:
            kb = b.shape[2]
            acc = sum(_mm(av[:, s * kb:(s + 1) * kb], refs[1][s], dn) for s in range(b.shape[0]))
        else:
            acc = _mm(av, refs[1][...], dn)
        vals = epi(acc, *[r[...] for r in refs[2:2 + len(epi_args)]])
        for (kind, _), o_ref, val in zip(outs, refs[n_in:], vals):
            if kind == "tile":
                o_ref[...] = val.astype(o_ref.dtype)
            else:
                @pl.when(pl.program_id(0) == 0)
                def _():
                    o_ref[...] = jnp.zeros_like(o_ref)

                o_ref[...] += val

    out_specs = [pl.BlockSpec((tm, n), lambda i: (i, 0)) if kind == "tile" else pl.BlockSpec((1, arg), lambda i: (0, 0))
                 for kind, arg in outs]
    out_shape = [jax.ShapeDtypeStruct((m, n), arg) if kind == "tile" else jax.ShapeDtypeStruct((1, arg), f32)
                 for kind, arg in outs]
    return pl.pallas_call(
        body, name=name, grid=(m // tm,),
        in_specs=[pl.BlockSpec((tm, k), lambda i: (i, 0)), pl.BlockSpec(b.shape, lambda i: (0,) * b.ndim)]
                 + e_specs + order_specs,
        out_specs=out_specs, out_shape=out_shape,
        compiler_params=_cparams(("arbitrary",)),
    )(a, b, *[arr for _, arr in epi_args], *order_args)


def _res_norm(acc, res, g):
    h = acc + res
    return h, _rmsnorm(h, g)


def _bias_res_norm(acc, bias, res, g):
    h = acc + bias + res
    return h, _rmsnorm(h, g)


def _res_norm_loss(acc, res, g, target):
    def f(h, gv):
        err = jnp.square(_rmsnorm(h, gv) - target)
        return 0.5 * jnp.sum(jnp.mean(err, axis=-1, keepdims=True), axis=0, keepdims=True)

    loss, vjp = jax.vjp(f, acc + res, g)
    dh, dg = vjp(jnp.ones_like(loss))
    return dh, dg, jnp.broadcast_to(loss, (1, 128))


def _norm_bwd_res(dy, x, g, res):
    _, vjp = jax.vjp(_rmsnorm, x, g)
    dx, dg = vjp(dy)
    return res + dx, dg


_MIXER_PARAM_SHAPES = (
    ("ln_g", (1, D_MODEL)), ("ln_b", (1, D_MODEL)), ("wm", (N_BLK, CH, CH)), ("bs_t", (CH, CH)),
    ("conv_w", (8, 2048)), ("conv_b", (1, 2048)), ("dt_bias", (1, CH)), ("a_log", (1, CH)),
    ("d_heads", (1, CH)), ("norm_g", (1, D_MODEL)),
)


def _blocks(v, n, off=0):
    return [v[:, off + i * CH: off + (i + 1) * CH] for i in range(n)]


def _split_mixer_params(vals):
    p = dict(vals)
    return {
        "ln_g": _blocks(p["ln_g"], N_BLK), "ln_b": _blocks(p["ln_b"], N_BLK),
        "wm": [p["wm"][g] for g in range(N_BLK)], "bs_t": p["bs_t"],
        "conv_w": _blocks(p["conv_w"], XBC_BLKS), "conv_b": _blocks(p["conv_b"], XBC_BLKS),
        "dt_bias": p["dt_bias"], "a_log": p["a_log"], "d_heads": p["d_heads"],
        "norm_g": _blocks(p["norm_g"], N_BLK),
    }


def _mixer_leaves(proj_ref, halo_ref, keep_halo):
    pv = proj_ref
    us = [pv[:, OFF_U + i * CH: OFF_U + (i + 1) * CH] for i in range(N_BLK)]
    vs = [pv[:, OFF_V + i * CH: OFF_V + (i + 1) * CH] for i in range(N_BLK)]
    zs = [pv[:, OFF_Z + i * CH: OFF_Z + (i + 1) * CH] for i in range(N_BLK)]
    xbcs = [pv[:, OFF_X + i * CH: OFF_X + (i + 1) * CH] for i in range(XBC_BLKS)]
    halos = [halo_ref[:, OFF_X + i * CH: OFF_X + (i + 1) * CH] * keep_halo for i in range(XBC_BLKS)]
    dtblk = pv[:, OFF_DT: OFF_DT + CH]
    return us, vs, zs, xbcs, halos, dtblk


def mixer_fwd(proj, prm):
    s = proj.shape[0]
    nc = s // CH
    names = [n for n, _ in _MIXER_PARAM_SHAPES]

    def body(proj_ref, halo_ref, *rest):
        p_refs = rest[:len(names)]
        ab_ref, hs_ref, h_ref = rest[len(names):]
        c = pl.program_id(0)

        @pl.when(c == 0)
        def _():
            h_ref[...] = jnp.zeros_like(h_ref)

        hs_ref[...] = h_ref[...]
        keep = (c > 0).astype(f32)
        us, vs, zs, xbcs, halos, dtblk = _mixer_leaves(proj_ref, halo_ref, keep)
        hps = [h_ref[i * CH:(i + 1) * CH, :] for i in range(N_BLK)]
        p = _split_mixer_params({n: r[...] for n, r in zip(names, p_refs)})
        a_out, b_out, h_out = _mixer_chunk(us, vs, zs, xbcs, halos, dtblk, hps, p)
        for i in range(N_BLK):
            ab_ref[:, i * CH:(i + 1) * CH] = a_out[i].astype(bf16)
            ab_ref[:, D_MODEL + i * CH: D_MODEL + (i + 1) * CH] = b_out[i].astype(bf16)
            h_ref[i * CH:(i + 1) * CH, :] = h_out[i]

    def const(shape):
        return pl.BlockSpec(shape, lambda c: (0,) * len(shape))

    return pl.pallas_call(
        body, name="mixer_fwd", grid=(nc,),
        in_specs=[pl.BlockSpec((CH, NP_IN), lambda c: (c, 0)),
                  pl.BlockSpec((8, NP_IN), lambda c: (jnp.maximum(c * (CH // 8) - 1, 0), 0))]
                 + [const(shp) for _, shp in _MIXER_PARAM_SHAPES],
        out_specs=[pl.BlockSpec((CH, 2 * D_MODEL), lambda c: (c, 0)),
                   pl.BlockSpec((None, D_MODEL, CH), lambda c: (c, 0, 0))],
        out_shape=[jax.ShapeDtypeStruct((s, 2 * D_MODEL), bf16), jax.ShapeDtypeStruct((nc, D_MODEL, CH), f32)],
        scratch_shapes=[pltpu.VMEM((D_MODEL, CH), f32)],
        compiler_params=_cparams(("arbitrary",)),
    )(proj, proj, *[prm[n] for n in names])


def mixer_bwd(proj, hstates, dab, prm):
    s = proj.shape[0]
    nc = s // CH
    names = [n for n, _ in _MIXER_PARAM_SHAPES]
    npar = len(names)

    def body(proj_ref, halo_ref, hs_ref, dab_ref, *rest):
        p_refs = rest[:npar]
        dproj_ref = rest[npar]
        g_refs = rest[npar + 1: 2 * npar + 1]
        dh_ref, dhalo_ref = rest[2 * npar + 1:]
        i = pl.program_id(0)
        c = nc - 1 - i

        @pl.when(i == 0)
        def _():
            dh_ref[...] = jnp.zeros_like(dh_ref)
            dhalo_ref[...] = jnp.zeros_like(dhalo_ref)
            for r in g_refs:
                r[...] = jnp.zeros_like(r)

        keep = (c > 0).astype(f32)
        us, vs, zs, xbcs, halos, dtblk = _mixer_leaves(proj_ref, halo_ref, keep)
        hps = [hs_ref[j * CH:(j + 1) * CH, :] for j in range(N_BLK)]
        pvals = {n: r[...] for n, r in zip(names, p_refs)}

        def fn(us, vs, zs, xbcs, halos, dtblk, hps, pvals):
            return _mixer_chunk(us, vs, zs, xbcs, halos, dtblk, hps, _split_mixer_params(pvals))

        _, vjp = jax.vjp(fn, us, vs, zs, xbcs, halos, dtblk, hps, pvals)
        da = [dab_ref[:, j * CH:(j + 1) * CH].astype(f32) for j in range(N_BLK)]
        db = [dab_ref[:, D_MODEL + j * CH: D_MODEL + (j + 1) * CH].astype(f32) for j in range(N_BLK)]
        dh = [dh_ref[j * CH:(j + 1) * CH, :] for j in range(N_BLK)]
        dus, dvs, dzs, dxbcs, dhalos, ddt, dhps, dp = vjp((da, db, dh))

        for j in range(N_BLK):
            dproj_ref[:, OFF_U + j * CH: OFF_U + (j + 1) * CH] = dus[j].astype(bf16)
            dproj_ref[:, OFF_V + j * CH: OFF_V + (j + 1) * CH] = dvs[j].astype(bf16)
            dproj_ref[:, OFF_Z + j * CH: OFF_Z + (j + 1) * CH] = dzs[j].astype(bf16)
            dh_ref[j * CH:(j + 1) * CH, :] = dhps[j]
        zeros_top = jnp.zeros((CH - 8, CH), f32)
        for j in range(XBC_BLKS):
            late = jnp.concatenate([zeros_top, dhalo_ref[:, j * CH:(j + 1) * CH]], axis=0)
            dproj_ref[:, OFF_X + j * CH: OFF_X + (j + 1) * CH] = (dxbcs[j] + late).astype(bf16)
        for j in range(XBC_BLKS):
            dhalo_ref[:, j * CH:(j + 1) * CH] = dhalos[j] * keep
        lane = lax.broadcasted_iota(jnp.int32, (CH, CH), 1)
        dproj_ref[:, OFF_DT: OFF_DT + CH] = jnp.where(lane < SSM_HEADS, ddt, 0.0).astype(bf16)
        dproj_ref[:, OFF_DT + CH:] = jnp.zeros((CH, NP_IN - OFF_DT - CH), bf16)
        for n, r in zip(names, g_refs):
            r[...] += dp[n]

    def const(shape):
        return pl.BlockSpec(shape, lambda i: (0,) * len(shape))

    outs = pl.pallas_call(
        body, name="mixer_bwd", grid=(nc,),
        in_specs=[pl.BlockSpec((CH, NP_IN), lambda i: (nc - 1 - i, 0)),
                  pl.BlockSpec((8, NP_IN), lambda i: (jnp.maximum((nc - 1 - i) * (CH // 8) - 1, 0), 0)),
                  pl.BlockSpec((None, D_MODEL, CH), lambda i: (nc - 1 - i, 0, 0)),
                  pl.BlockSpec((CH, 2 * D_MODEL), lambda i: (nc - 1 - i, 0))]
                 + [const(shp) for _, shp in _MIXER_PARAM_SHAPES],
        out_specs=[pl.BlockSpec((CH, NP_IN), lambda i: (nc - 1 - i, 0))]
                  + [const(shp) for _, shp in _MIXER_PARAM_SHAPES],
        out_shape=[jax.ShapeDtypeStruct((s, NP_IN), bf16)]
                  + [jax.ShapeDtypeStruct(shp, f32) for _, shp in _MIXER_PARAM_SHAPES],
        scratch_shapes=[pltpu.VMEM((D_MODEL, CH), f32), pltpu.VMEM((8, 2048), f32)],
        compiler_params=_cparams(("arbitrary",)),
    )(proj, proj, hstates, dab, *[prm[n] for n in names])
    return outs[0], dict(zip(names, outs[1:]))


_K_BLK = D_MODEL // CH
_V_BLK = _K_BLK + 1


def _attn_specs(rev, nb):
    def blk(i):
        return nb - 1 - i if rev else i

    q_spec = pl.BlockSpec((CH, D_MODEL), lambda i: (blk(i), 0))
    kv = lambda col, prev: pl.BlockSpec(
        (CH, CH), lambda i: (jnp.maximum(blk(i) - 1, 0) if prev else blk(i), col))
    return q_spec, [kv(_K_BLK, True), kv(_K_BLK, False), kv(_V_BLK, True), kv(_V_BLK, False)]


def attn_fwd(qkv, sink_row):
    s = qkv.shape[0]
    nb = s // CH

    def body(q_ref, kp_ref, kc_ref, vp_ref, vc_ref, sink_ref, o_ref):
        qps = [q_ref[:, p * CH:(p + 1) * CH] for p in range(N_BLK)]
        outs = _attn_block(qps, kp_ref[...], kc_ref[...], vp_ref[...], vc_ref[...], sink_ref[...],
                           pl.program_id(0) == 0)
        for p in range(N_BLK):
            o_ref[:, p * CH:(p + 1) * CH] = outs[p].astype(bf16)

    q_spec, kv_specs = _attn_specs(False, nb)
    return pl.pallas_call(
        body, name="attn_fwd", grid=(nb,),
        in_specs=[q_spec] + kv_specs + [pl.BlockSpec((1, CH), lambda i: (0, 0))],
        out_specs=pl.BlockSpec((CH, D_MODEL), lambda i: (i, 0)),
        out_shape=jax.ShapeDtypeStruct((s, D_MODEL), bf16),
        compiler_params=_cparams(("parallel",)),
    )(qkv, qkv, qkv, qkv, qkv, sink_row)


def attn_bwd(qkv, sink_row, dout):
    s = qkv.shape[0]
    nb = s // CH

    def body(q_ref, kp_ref, kc_ref, vp_ref, vc_ref, sink_ref, do_ref, dqkv_ref, dsink_ref, carry_ref):
        i = pl.program_id(0)
        blk = nb - 1 - i

        @pl.when(i == 0)
        def _():
            dsink_ref[...] = jnp.zeros_like(dsink_ref)
            carry_ref[...] = jnp.zeros_like(carry_ref)

        qps = [q_ref[:, p * CH:(p + 1) * CH] for p in range(N_BLK)]
        first = blk == 0
        _, vjp = jax.vjp(lambda *a: _attn_block(*a, first), qps, kp_ref[...], kc_ref[...], vp_ref[...],
                         vc_ref[...], sink_ref[...])
        dos = [do_ref[:, p * CH:(p + 1) * CH].astype(f32) for p in range(N_BLK)]
        dqs, dkp, dkc, dvp, dvc, dsink = vjp(dos)
        for p in range(N_BLK):
            dqkv_ref[:, p * CH:(p + 1) * CH] = dqs[p].astype(bf16)
        dqkv_ref[:, D_MODEL: D_MODEL + CH] = (dkc + carry_ref[0]).astype(bf16)
        dqkv_ref[:, D_MODEL + CH:] = (dvc + carry_ref[1]).astype(bf16)
        keep = jnp.logical_not(first).astype(f32)
        carry_ref[0] = dkp * keep
        carry_ref[1] = dvp * keep
        dsink_ref[...] += dsink

    q_spec, kv_specs = _attn_specs(True, nb)
    return pl.pallas_call(
        body, name="attn_bwd", grid=(nb,),
        in_specs=[q_spec] + kv_specs + [pl.BlockSpec((1, CH), lambda i: (0, 0)),
                                        pl.BlockSpec((CH, D_MODEL), lambda i: (nb - 1 - i, 0))],
        out_specs=[pl.BlockSpec((CH, QKV_DIM), lambda i: (nb - 1 - i, 0)), pl.BlockSpec((1, CH), lambda i: (0, 0))],
        out_shape=[jax.ShapeDtypeStruct((s, QKV_DIM), bf16), jax.ShapeDtypeStruct((1, CH), f32)],
        scratch_shapes=[pltpu.VMEM((2, CH, CH), f32)],
        compiler_params=_cparams(("arbitrary",)),
    )(qkv, qkv, qkv, qkv, qkv, sink_row, dout)


def adamw(w, g, m, v, name):
    def body(w_ref, g_ref, m_ref, v_ref, d_ref, nm_ref, nv_ref):
        gv = g_ref[...]
        nm = ADAM_B1 * m_ref[...] + (1.0 - ADAM_B1) * gv
        nv = ADAM_B2 * v_ref[...] + (1.0 - ADAM_B2) * jnp.square(gv)
        m_hat = nm / (1.0 - ADAM_B1 ** ADAM_STEP)
        v_hat = nv / (1.0 - ADAM_B2 ** ADAM_STEP)
        d_ref[...] = -ADAM_LR * (m_hat / (jnp.sqrt(v_hat) + ADAM_EPS) + ADAM_WD * w_ref[...])
        nm_ref[...] = nm
        nv_ref[...] = nv

    out_shape = [jax.ShapeDtypeStruct(w.shape, f32)] * 3
    if w.ndim == 3 and w.shape[1] == 1:
        tr = max(t for t in range(1, 129) if w.shape[0] % t == 0)
        tile = pl.BlockSpec((tr, 1, w.shape[2]), lambda i: (i, 0, 0))
        return pl.pallas_call(
            body, name=name, grid=(w.shape[0] // tr,),
            in_specs=[tile] * 4, out_specs=[tile] * 3, out_shape=out_shape,
            compiler_params=_cparams(("parallel",)),
        )(w, g, m, v)
    if w.ndim == 3 and w.shape[1] % 256 == 0:
        tile = pl.BlockSpec((None, 256, w.shape[2]), lambda l, i: (l, i, 0))
        return pl.pallas_call(
            body, name=name, grid=(w.shape[0], w.shape[1] // 256),
            in_specs=[tile] * 4, out_specs=[tile] * 3, out_shape=out_shape,
            compiler_params=_cparams(("parallel", "parallel")),
        )(w, g, m, v)
    return pl.pallas_call(body, name=name, in_specs=[_VMEM] * 4, out_specs=[_VMEM] * 3, out_shape=out_shape,
                          compiler_params=_cparams())(w, g, m, v)


_MESH = pl.DeviceIdType.MESH
_ANY = pl.BlockSpec(memory_space=pl.ANY)
_VMEM = pl.BlockSpec(memory_space=pltpu.VMEM)


def _place():
    x, y, c = lax.axis_index("x"), lax.axis_index("y"), lax.axis_index("c")
    chips = [(1 - x, y), (x, 1 - y), (1 - x, 1 - y)]
    return x, y, c, 2 * x + y, chips, [2 * cx + cy for cx, cy in chips]


def _half(c, rows):
    return pl.ds(pl.multiple_of(c * (rows // 2), 16), rows // 2)


def _step_rows(rows):
    return max(t for t in range(16, 641, 16) if rows % t == 0)


def place_shard(b, slot, name, dtype=bf16, after=None):
    r, c = b.shape
    tr = _step_rows(r)

    def body(slot_ref, b_ref, *rest):
        rest[-1][...] = b_ref[...].astype(dtype)

    order_specs = [] if after is None else [pl.BlockSpec((8, 128), lambda i, s: (0, 0))]
    return pl.pallas_call(
        body, name=name,
        grid_spec=pltpu.PrefetchScalarGridSpec(
            num_scalar_prefetch=1, grid=(r // tr,),
            in_specs=[pl.BlockSpec((tr, c), lambda i, s: (i, 0))] + order_specs,
            out_specs=pl.BlockSpec((None, tr, c), lambda i, s: (s[0], i, 0))),
        out_shape=jax.ShapeDtypeStruct((N_CHIPS, r, c), dtype),
        compiler_params=_cparams(("parallel",)),
    )(slot, b, *([] if after is None else [after]))


_HBM = pl.BlockSpec(memory_space=pltpu.HBM)
_SEM = pl.BlockSpec(memory_space=pltpu.SEMAPHORE)
_EFFECT = pltpu.SideEffectType.DATAFLOW_SIDE_EFFECTING


def _gather_ici_copies(bufs, send_sems, recv_sems):
    x, y, c, me, chips, chip_idx = _place()
    return [pltpu.make_async_remote_copy(
        src_ref=buf.at[me, _half(c, buf.shape[1])], dst_ref=buf.at[chip_idx[j], _half(c, buf.shape[1])],
        send_sem=send_sems.at[3 * k + j], recv_sem=recv_sems.at[3 * k + j],
        device_id=(*chips[j], c), device_id_type=_MESH) for j in range(3) for k, buf in enumerate(bufs)]


def gather_start(groups, tag):
    sizes = [len(g) for g in groups]
    flat = [b for g in groups for b in g]
    n = len(flat)

    def body(*refs):
        bufs, sems = refs[:n], refs[n:n + 2 * len(groups)]
        refs[-1][...] = jnp.zeros_like(refs[-1])
        x, y, c, me, chips, chip_idx = _place()
        lo = 0
        for gi, size in enumerate(sizes):
            for j in range(3):
                for k, buf in enumerate(bufs[lo:lo + size]):
                    mine = buf.at[me, _half(c, buf.shape[1])]
                    pltpu.make_async_remote_copy(
                        src_ref=mine, dst_ref=mine, send_sem=sems[2 * gi].at[3 * k + j],
                        recv_sem=sems[2 * gi + 1].at[3 * k + j], device_id=(*chips[j], c),
                        device_id_type=_MESH).start()
            lo += size

    sem_shapes = [pltpu.SemaphoreType.DMA((3 * size,)) for size in sizes for _ in range(2)]
    outs = pl.pallas_call(
        body, name=f"gather_start_{tag}",
        out_shape=(*sem_shapes, *[pltpu.HBM(b.shape, b.dtype) for b in flat], jax.ShapeDtypeStruct((8, 128), f32)),
        in_specs=[_HBM] * n, out_specs=(*[_SEM] * len(sem_shapes), *[_HBM] * n, _VMEM),
        input_output_aliases={i: len(sem_shapes) + i for i in range(n)},
        compiler_params=pltpu.CompilerParams(has_side_effects=_EFFECT),
    )(*[pltpu.with_memory_space_constraint(b, pltpu.HBM) for b in flat])
    sems = [(outs[2 * gi], outs[2 * gi + 1]) for gi in range(len(groups))]
    thru, lo = [], len(sem_shapes)
    for size in sizes:
        thru.append(list(outs[lo:lo + size]))
        lo += size
    return sems, thru, outs[-1]


def gather_wait(bufs, sems, after, tag):
    n = len(bufs)

    def body(*refs):
        for cp in _gather_ici_copies(refs[:n], refs[n], refs[n + 1]):
            cp.wait_send()
            cp.wait_recv()

    extra = list(after)
    return list(pl.pallas_call(
        body, name=f"gather_wait_{tag}",
        out_shape=[pltpu.HBM(b.shape, b.dtype) for b in bufs],
        in_specs=[_HBM] * n + [_SEM, _SEM] + [_ANY] * len(extra), out_specs=[_HBM] * n,
        input_output_aliases={i: i for i in range(n)},
        compiler_params=pltpu.CompilerParams(has_side_effects=_EFFECT),
    )(*bufs, *sems, *extra))


def gather_forward(bufs, tag):
    n = len(bufs)

    def body(*refs):
        out_refs = refs[n:2 * n]
        send_sems, recv_sems = refs[2 * n:]
        x, y, c, me, chips, chip_idx = _place()

        def copy(k, j, half):
            part = out_refs[k].at[chip_idx[j], _half(half, out_refs[k].shape[1])]
            return pltpu.make_async_remote_copy(
                src_ref=part, dst_ref=part, send_sem=send_sems.at[3 * k + j], recv_sem=recv_sems.at[3 * k + j],
                device_id=(x, y, 1 - c), device_id_type=_MESH)

        sends = [copy(k, j, c) for j in range(3) for k in range(n)]
        for cp in sends:
            cp.start()
        for j in range(3):
            for k in range(n):
                copy(k, j, 1 - c).wait_recv()
        for cp in sends:
            cp.wait_send()

    return list(pl.pallas_call(
        body, name=f"gather_forward_{tag}",
        out_shape=[jax.ShapeDtypeStruct(b.shape, b.dtype) for b in bufs],
        in_specs=[_ANY] * n, out_specs=[_ANY] * n, input_output_aliases={i: i for i in range(n)},
        scratch_shapes=[pltpu.SemaphoreType.DMA((3 * n,)), pltpu.SemaphoreType.DMA((3 * n,))],
    )(*bufs))


def exchange_halves(bufs, tag):
    n = len(bufs)

    def body(*refs):
        g_refs, out_refs = refs[:n], refs[n:2 * n]
        send_sems, recv_sems = refs[2 * n:]
        x, y, c, *_ = _place()
        cps = [pltpu.make_async_remote_copy(
            src_ref=g_refs[b].at[:, _half(1 - c, g_refs[b].shape[1])], dst_ref=out_refs[b],
            send_sem=send_sems.at[b], recv_sem=recv_sems.at[b], device_id=(x, y, 1 - c), device_id_type=_MESH)
            for b in range(n)]
        for cp in cps:
            cp.start()
        for cp in cps:
            cp.wait()

    return pl.pallas_call(
        body, name=f"exchange_halves_{tag}",
        out_shape=[jax.ShapeDtypeStruct((N_CHIPS, b.shape[1] // 2, b.shape[2]), b.dtype) for b in bufs],
        in_specs=[_ANY] * n, out_specs=[_ANY] * n,
        scratch_shapes=[pltpu.SemaphoreType.DMA((n,)), pltpu.SemaphoreType.DMA((n,))],
    )(*bufs)


def add_halves(g, got, c_idx, name):
    hr, cols = got.shape[1], got.shape[2]
    tr = _step_rows(hr)
    steps = hr // tr

    def body(c_ref, g_ref, got_ref, o_ref):
        o_ref[...] = (g_ref[...].astype(f32) + got_ref[...].astype(f32)).astype(bf16)

    return pl.pallas_call(
        body, name=name,
        grid_spec=pltpu.PrefetchScalarGridSpec(
            num_scalar_prefetch=1, grid=(N_CHIPS, steps),
            in_specs=[pl.BlockSpec((None, tr, cols), lambda s, i, c: (s, c[0] * steps + i, 0)),
                      pl.BlockSpec((None, tr, cols), lambda s, i, c: (s, i, 0))],
            out_specs=pl.BlockSpec((None, tr, cols), lambda s, i, c: (s, i, 0))),
        out_shape=jax.ShapeDtypeStruct(got.shape, bf16),
        compiler_params=_cparams(("parallel", "parallel")),
    )(c_idx, g, got)


def sum_chips(t, got, place_idx, name):
    hr, cols = t.shape[1], t.shape[2]
    tr = _step_rows(hr)
    steps = hr // tr

    def body(idx_ref, t_ref, got_ref, o_ref):
        acc = t_ref[...].astype(f32)
        for j in range(3):
            acc = acc + got_ref[j].astype(f32)
        o_ref[...] = acc

    return pl.pallas_call(
        body, name=name,
        grid_spec=pltpu.PrefetchScalarGridSpec(
            num_scalar_prefetch=1, grid=(steps,),
            in_specs=[pl.BlockSpec((None, tr, cols), lambda i, idx: (idx[0], i, 0)),
                      pl.BlockSpec((3, tr, cols), lambda i, idx: (0, i, 0))],
            out_specs=pl.BlockSpec((tr, cols), lambda i, idx: (idx[1] * steps + i, 0))),
        out_shape=jax.ShapeDtypeStruct((2 * hr, cols), f32),
        compiler_params=_cparams(("parallel",)),
    )(place_idx, t, got)


def _share_copies(refs, send_sems, recv_sems):
    x, y, c, *_ = _place()
    return [pltpu.make_async_remote_copy(
        src_ref=ref.at[_half(c, ref.shape[0])], dst_ref=ref.at[_half(c, ref.shape[0])], send_sem=send_sems.at[b],
        recv_sem=recv_sems.at[b], device_id=(x, y, 1 - c), device_id_type=_MESH) for b, ref in enumerate(refs)]


def share_start(bufs, tag):
    n = len(bufs)

    def body(*refs):
        for cp in _share_copies(refs[:n], refs[n], refs[n + 1]):
            cp.start()
        token = refs[-1]
        token[...] = jnp.zeros_like(token)

    outs = pl.pallas_call(
        body, name=f"share_start_{tag}",
        out_shape=(pltpu.SemaphoreType.DMA((n,)), pltpu.SemaphoreType.DMA((n,)),
                   *[pltpu.HBM(b.shape, b.dtype) for b in bufs], jax.ShapeDtypeStruct((8, 128), f32)),
        in_specs=[_HBM] * n, out_specs=(_SEM, _SEM, *[_HBM] * n, _VMEM),
        input_output_aliases={i: 2 + i for i in range(n)},
        compiler_params=pltpu.CompilerParams(has_side_effects=_EFFECT),
    )(*[pltpu.with_memory_space_constraint(b, pltpu.HBM) for b in bufs])
    return (outs[0], outs[1], list(outs[2:2 + n])), outs[-1]


def share_wait(send_sems, recv_sems, bufs, after, tag):
    n = len(bufs)

    def body(*refs):
        x, y, c, *_ = _place()
        for b, ref in enumerate(refs[:n]):
            cp = pltpu.make_async_remote_copy(
                src_ref=ref.at[_half(c, ref.shape[0])], dst_ref=ref.at[_half(1 - c, ref.shape[0])],
                send_sem=refs[n].at[b], recv_sem=refs[n + 1].at[b], device_id=(x, y, 1 - c), device_id_type=_MESH)
            cp.wait_send()
            cp.wait_recv()

    return list(pl.pallas_call(
        body, name=f"share_wait_{tag}",
        out_shape=[pltpu.HBM(b.shape, b.dtype) for b in bufs],
        in_specs=[_HBM] * n + [_SEM, _SEM, _ANY], out_specs=[_HBM] * n,
        input_output_aliases={i: i for i in range(n)},
        compiler_params=pltpu.CompilerParams(has_side_effects=_EFFECT),
    )(*bufs, send_sems, recv_sems, after))


def _scatter_copies(t_refs, land_refs, send_sems, recv_sems):
    x, y, c, me, chips, chip_idx = _place()
    return [pltpu.make_async_remote_copy(
        src_ref=t_refs[b].at[chip_idx[j]], dst_ref=land_refs[b].at[j], send_sem=send_sems.at[3 * b + j],
        recv_sem=recv_sems.at[3 * b + j], device_id=(*chips[j], c), device_id_type=_MESH)
        for j in range(3) for b in range(len(t_refs))]


def scatter_start(ts, tag):
    n = len(ts)
    lands = [lax.empty((3,) + t.shape[1:], t.dtype) for t in ts]

    def body(*refs):
        for cp in _scatter_copies(refs[:n], refs[n:2 * n], refs[2 * n], refs[2 * n + 1]):
            cp.start()
        token = refs[-1]
        token[...] = jnp.zeros_like(token)

    hbm = [pltpu.HBM(a.shape, a.dtype) for a in (*ts, *lands)]
    outs = pl.pallas_call(
        body, name=f"scatter_start_{tag}",
        out_shape=(pltpu.SemaphoreType.DMA((3 * n,)), pltpu.SemaphoreType.DMA((3 * n,)), *hbm,
                   jax.ShapeDtypeStruct((8, 128), f32)),
        in_specs=[_HBM] * (2 * n), out_specs=(_SEM, _SEM, *[_HBM] * (2 * n), _VMEM),
        input_output_aliases={i: 2 + i for i in range(2 * n)},
        compiler_params=pltpu.CompilerParams(has_side_effects=_EFFECT),
    )(*[pltpu.with_memory_space_constraint(a, pltpu.HBM) for a in (*ts, *lands)])
    return outs[0], outs[1], list(outs[2:2 + n]), list(outs[2 + n:2 + 2 * n]), outs[-1]


def scatter_wait(send_sems, recv_sems, ts, lands, after, tag):
    n = len(ts)

    def body(*refs):
        for cp in _scatter_copies(refs[:n], refs[n:2 * n], refs[2 * n], refs[2 * n + 1]):
            cp.wait_send()
            cp.wait_recv()

    outs = pl.pallas_call(
        body, name=f"scatter_wait_{tag}",
        out_shape=[pltpu.HBM(a.shape, a.dtype) for a in (*ts, *lands)],
        in_specs=[_HBM] * (2 * n) + [_SEM, _SEM, _ANY], out_specs=[_HBM] * (2 * n),
        input_output_aliases={i: i for i in range(2 * n)},
        compiler_params=pltpu.CompilerParams(has_side_effects=_EFFECT),
    )(*ts, *lands, send_sems, recv_sems, after)
    return list(outs[:n]), list(outs[n:])


N_SENDERS = 7


def _direct_copies(g_refs, land_refs, send_sems, recv_sems):
    x, y, c, me, chips, chip_idx = _place()
    cps = []
    for b, (g, land) in enumerate(zip(g_refs, land_refs)):
        rows, base = g.shape[1], N_SENDERS * b
        cps.append(pltpu.make_async_remote_copy(
            src_ref=g.at[me, _half(1 - c, rows)], dst_ref=land.at[0], send_sem=send_sems.at[base],
            recv_sem=recv_sems.at[base], device_id=(x, y, 1 - c), device_id_type=_MESH))
        for j in range(3):
            for core in range(2):
                cps.append(pltpu.make_async_remote_copy(
                    src_ref=g.at[chip_idx[j], _half(core, rows)], dst_ref=land.at[1 + 2 * j + c],
                    send_sem=send_sems.at[base + 1 + 2 * j + core], recv_sem=recv_sems.at[base + 1 + 2 * j + c],
                    device_id=(*chips[j], core), device_id_type=_MESH))
    return cps


def direct_start(gs, tag):
    n = len(gs)
    lands = [lax.empty((N_SENDERS, g.shape[1] // 2, g.shape[2]), g.dtype) for g in gs]

    def body(*refs):
        for cp in _direct_copies(refs[:n], refs[n:2 * n], refs[2 * n], refs[2 * n + 1]):
            cp.start()
        token = refs[-1]
        token[...] = jnp.zeros_like(token)

    hbm = [pltpu.HBM(a.shape, a.dtype) for a in (*gs, *lands)]
    outs = pl.pallas_call(
        body, name=f"direct_start_{tag}",
        out_shape=(pltpu.SemaphoreType.DMA((N_SENDERS * n,)), pltpu.SemaphoreType.DMA((N_SENDERS * n,)), *hbm,
                   jax.ShapeDtypeStruct((8, 128), f32)),
        in_specs=[_HBM] * (2 * n), out_specs=(_SEM, _SEM, *[_HBM] * (2 * n), _VMEM),
        input_output_aliases={i: 2 + i for i in range(2 * n)},
        compiler_params=pltpu.CompilerParams(has_side_effects=_EFFECT),
    )(*[pltpu.with_memory_space_constraint(a, pltpu.HBM) for a in (*gs, *lands)])
    return outs[0], outs[1], list(outs[2:2 + n]), list(outs[2 + n:2 + 2 * n]), outs[-1]


def direct_wait(send_sems, recv_sems, gs, lands, after, tag):
    n = len(gs)

    def body(*refs):
        g_refs, land_refs, sends, recvs = refs[:n], refs[n:2 * n], refs[2 * n], refs[2 * n + 1]
        for b in range(n):
            for k in range(N_SENDERS):
                cp = pltpu.make_async_remote_copy(
                    src_ref=g_refs[b].at[0, _half(0, g_refs[b].shape[1])], dst_ref=land_refs[b].at[k],
                    send_sem=sends.at[N_SENDERS * b + k], recv_sem=recvs.at[N_SENDERS * b + k],
                    device_id=_place()[:3], device_id_type=_MESH)
                cp.wait_send()
                cp.wait_recv()

    outs = pl.pallas_call(
        body, name=f"direct_wait_{tag}",
        out_shape=[pltpu.HBM(a.shape, a.dtype) for a in (*gs, *lands)],
        in_specs=[_HBM] * (2 * n) + [_SEM, _SEM, _ANY], out_specs=[_HBM] * (2 * n),
        input_output_aliases={i: i for i in range(2 * n)},
        compiler_params=pltpu.CompilerParams(has_side_effects=_EFFECT),
    )(*gs, *lands, send_sems, recv_sems, after)
    return list(outs[:n]), list(outs[n:])


def sum_senders(g, lands, place_idx, name):
    hr, cols = lands.shape[1], lands.shape[2]
    tr = _step_rows(hr)
    steps = hr // tr

    def body(idx_ref, g_ref, land_ref, o_ref):
        acc = g_ref[...].astype(f32)
        for k in range(N_SENDERS):
            acc = acc + land_ref[k].astype(f32)
        o_ref[...] = acc

    return pl.pallas_call(
        body, name=name,
        grid_spec=pltpu.PrefetchScalarGridSpec(
            num_scalar_prefetch=1, grid=(steps,),
            in_specs=[pl.BlockSpec((None, tr, cols), lambda i, idx: (idx[0], idx[1] * steps + i, 0)),
                      pl.BlockSpec((N_SENDERS, tr, cols), lambda i, idx: (0, i, 0))],
            out_specs=pl.BlockSpec((tr, cols), lambda i, idx: (idx[1] * steps + i, 0))),
        out_shape=jax.ShapeDtypeStruct((2 * hr, cols), f32),
        compiler_params=_cparams(("parallel",)),
    )(place_idx, g, lands)


class GradReducer:
    def __init__(self, c_idx, place_idx):
        self.c_idx, self.place_idx = c_idx, place_idx

    def start(self, bufs, tag, direct=False):
        if direct:
            send_sems, recv_sems, gs, lands, token = direct_start(bufs, tag)
            return (True, send_sems, recv_sems, gs, lands), token
        got = exchange_halves(bufs, tag)
        ts = [add_halves(b, g, self.c_idx, f"add_halves_{tag}{i}") for i, (b, g) in enumerate(zip(bufs, got))]
        send_sems, recv_sems, ts, lands, token = scatter_start(ts, tag)
        return (False, send_sems, recv_sems, ts, lands), token

    def finish(self, state, after, tag):
        direct, *flight = state
        if direct:
            gs, lands = direct_wait(*flight, after, tag)
            sums = [sum_senders(g, l, self.place_idx, f"sum_senders_{tag}{i}") for i, (g, l) in enumerate(zip(gs, lands))]
        else:
            ts, lands = scatter_wait(*flight, after, tag)
            sums = [sum_chips(t, l, self.place_idx, f"sum_chips_{tag}{i}") for i, (t, l) in enumerate(zip(ts, lands))]
        return share_start(sums, tag)

    def collect(self, pending, after, tag):
        return share_wait(*pending, after, tag)


def allreduce_small(sp):
    rows = sp.shape[0]
    hr = rows // 2

    def body(s_ref, out_ref, sib_ref, chip_ref, four_ref, send_sems, recv_sems):
        x, y, c, me, chips, chip_idx = _place()
        sibling = (x, y, 1 - c)
        mine = pl.ds(pl.multiple_of(c * hr, 8), hr)
        other = pl.ds(pl.multiple_of((1 - c) * hr, 8), hr)

        swap = pltpu.make_async_remote_copy(src_ref=s_ref, dst_ref=sib_ref, send_sem=send_sems.at[0],
                                            recv_sem=recv_sems.at[0], device_id=sibling, device_id_type=_MESH)
        swap.start()
        swap.wait()
        is_core0 = c == 0
        chip_ref[...] = jnp.where(is_core0, s_ref[...], sib_ref[...]) + jnp.where(is_core0, sib_ref[...], s_ref[...])

        sends = [pltpu.make_async_remote_copy(
            src_ref=chip_ref.at[mine], dst_ref=four_ref.at[me], send_sem=send_sems.at[1 + j],
            recv_sem=recv_sems.at[1 + j], device_id=(*chips[j], c), device_id_type=_MESH) for j in range(3)]
        for cp in sends:
            cp.start()
        four_ref[me] = chip_ref[mine, :]
        for j in range(3):
            pltpu.make_async_remote_copy(
                src_ref=chip_ref.at[mine], dst_ref=four_ref.at[chip_idx[j]], send_sem=send_sems.at[1 + j],
                recv_sem=recv_sems.at[1 + j], device_id=(*chips[j], c), device_id_type=_MESH).wait_recv()
        for cp in sends:
            cp.wait_send()
        out_ref[mine, :] = (four_ref[0] + four_ref[1]) + (four_ref[2] + four_ref[3])

        share = pltpu.make_async_remote_copy(src_ref=out_ref.at[mine], dst_ref=out_ref.at[mine], send_sem=send_sems.at[4],
                                             recv_sem=recv_sems.at[4], device_id=sibling, device_id_type=_MESH)
        share.start()
        pltpu.make_async_remote_copy(src_ref=out_ref.at[mine], dst_ref=out_ref.at[other], send_sem=send_sems.at[4],
                                     recv_sem=recv_sems.at[4], device_id=sibling, device_id_type=_MESH).wait_recv()
        share.wait_send()

    return pl.pallas_call(
        body, name="allreduce_small",
        out_shape=jax.ShapeDtypeStruct(sp.shape, sp.dtype),
        in_specs=[_VMEM], out_specs=_VMEM,
        scratch_shapes=[pltpu.VMEM(sp.shape, sp.dtype), pltpu.VMEM(sp.shape, sp.dtype),
                        pltpu.VMEM((N_CHIPS, hr, sp.shape[1]), sp.dtype),
                        pltpu.SemaphoreType.DMA((5,)), pltpu.SemaphoreType.DMA((5,))],
        compiler_params=_cparams(),
    )(sp)


def _n_rows(shape):
    n = 1
    for d in shape:
        n *= d
    return 8 * (-(-n // 8192))


def _pack(arrays, total_rows):
    parts = []
    for a in arrays:
        flat = a.reshape(-1)
        parts.append(jnp.pad(flat, (0, 1024 * _n_rows(a.shape) - flat.shape[0])).reshape(-1, 1024))
    rows = jnp.concatenate(parts, axis=0)
    return jnp.pad(rows, ((0, total_rows - rows.shape[0]), (0, 0)))


def _unpack(packed, shapes):
    out, r = [], 0
    for shp in shapes:
        n = 1
        for d in shp:
            n *= d
        nr = _n_rows(shp)
        out.append(packed[r:r + nr].reshape(-1)[:n].reshape(shp))
        r += nr
    return out


_COLUMN_SHARDED = ("w_in_even", "w_qkv")
IN_SHARD, IN_PAD = 1284, 1408
QKV_SHARD, QKV_PAD = 320, 384


def _lane_padded(a, cols):
    return jnp.pad(a, ((0, 0), (0, cols - a.shape[1])))


_SMALL_SHAPES = (
    ("norm_mix_g", (2, 1024)), ("norm_mlp_g", (2, 1024)), ("final_norm_g", (1024,)), ("gm_ln_g", (1, 1024)),
    ("gm_ln_b", (1, 1024)), ("gm_w_s", (1, 8, 128, 128)), ("gm_b_s", (1, 8, 128)), ("ssm_conv_b", (1, 2048)),
    ("ssm_dt_bias", (1, 16)), ("ssm_a_log", (1, 16)), ("ssm_d", (1, 16)), ("ssm_norm_g", (1, 1024)),
    ("attn_sinks", (1, 16)), ("ssm_conv_w", (1, 4, 2048)), ("b_qkv", (1, 1280)), ("b_o", (1, 1024)),
)
_N_REPLICATED = 13
_SHARDED_SMALL = (("ssm_conv_w", 2, 512), ("b_qkv", 1, 320), ("b_o", 1, 256))
_SHARD_PACK_ROWS = 32


def _cols_by_owner(a):
    return a.transpose(1, 0, 2).reshape(a.shape[1], -1)


class WeightGatherer:
    def __init__(self, w, chip_idx):
        def place(tag, b, dtype=bf16, after=None):
            return place_shard(b, chip_idx, f"place_shard_{tag}", dtype, after)

        sems_in, bufs_in, self.started = gather_start([
            [place("in", _lane_padded(w["w_in_even"][0], IN_PAD)),
             place("small", _pack([w[n] for n, _, _ in _SHARDED_SMALL], _SHARD_PACK_ROWS), f32)]], "in")
        t = self.started
        sems, bufs, self.all_started = gather_start([
            [place("out", w["w_out_even"][0], after=t), place("up0", w["w_up"][0], after=t),
             place("down0", w["w_down"][0], after=t)],
            [place("qkv", _lane_padded(w["w_qkv"][0], QKV_PAD), after=t), place("o", w["w_o"][0], after=t),
             place("up1", w["w_up"][1], after=t), place("down1", w["w_down"][1], after=t)],
        ], "rest")
        self.sems, self.bufs = sems_in + sems, bufs_in + bufs

    def _group(self, gi, after, tag):
        return gather_forward(gather_wait(self.bufs[gi], self.sems[gi], after, tag), tag)

    def mixer_in(self, after):
        g, small = self._group(0, [after, self.all_started], "in")
        shard_shapes = [tuple(width if i == axis else d for i, d in enumerate(dict(_SMALL_SHAPES)[n]))
                        for n, axis, width in _SHARDED_SMALL]
        per_chip = [_unpack(small[s], shard_shapes) for s in range(N_CHIPS)]
        full = {n: jnp.concatenate([per_chip[s][i] for s in range(N_CHIPS)], axis=axis)
                for i, (n, axis, _) in enumerate(_SHARDED_SMALL)}
        w_in = jnp.concatenate([g[s, :, :IN_SHARD] for s in range(N_CHIPS)], axis=1)
        return _lane_padded(w_in, NP_IN), full

    def layer0(self, after):
        w_out, w_up, w_down = self._group(1, [after], "l0")
        return w_out.reshape(2048, 1024), w_up, w_down.reshape(4096, 1024)

    def layer1(self, after):
        q, w_o, w_up, w_down = self._group(2, [after], "l1")
        w_qkv = jnp.concatenate([q[s, :, :QKV_SHARD] for s in range(N_CHIPS)], axis=1)
        return w_qkv, w_o.reshape(1024, 1024), w_up, w_down.reshape(4096, 1024)


def _row2(v):
    return v.reshape(1, -1)


def _lane_pad(v):
    return jnp.pad(v, ((0, 0), (0, CH - v.shape[1])))


_H_AND_NORM = (("tile", f32), ("tile", bf16))
_DX_AND_DG = (("tile", f32), ("sum", D_MODEL))


def _mlp_bwd(dh_out, h, g_row, y, a, w_up, w_down, tag, after=None):
    da = matmul(dh_out, w_down, dims="nt", name=f"mlp_da{tag}", out_dtype=bf16, tn=1024,
                epi=_times_relu2_grad, epi_args=(("tile", a),), after=after)
    dw_down = matmul(a, dh_out, dims="tn", name=f"mlp_dwdown{tag}", out_dtype=bf16, a_pro=_relu2)
    dw_up = matmul(y, da, dims="tn", name=f"mlp_dwup{tag}", out_dtype=bf16, tn=1024, out_by_col_tile=True)
    dh, dg = matmul_rows(da, w_up, dims="nt", name=f"mlp_dy{tag}", epi=_norm_bwd_res,
                         epi_args=(("tile", h), ("row", g_row), ("tile", dh_out)), outs=_DX_AND_DG)
    return dh, dg, dw_up, dw_down


def _by_owner(a):
    return a.reshape(N_CHIPS, a.shape[0] // N_CHIPS, a.shape[1])


def _col_shards(a, shard, padded):
    return jnp.stack([_lane_padded(a[:, shard * s: shard * (s + 1)], padded) for s in range(N_CHIPS)])


def _local_step(x, target, weights, sm, reducer):
    w_up, w_down = [None, None], [None, None]
    mix_g = [_row2(sm["norm_mix_g"][i]) for i in range(2)]
    y0 = rmsnorm_fwd(x, mix_g[0] + weights.started[:1, :1], "mix_norm0")
    w_in_p, sharded_small = weights.mixer_in(y0)
    sm = {**sm, **sharded_small}
    mlp_g = [_row2(sm["norm_mlp_g"][i]) for i in range(2)]
    mixer_prm = {
        "ln_g": sm["gm_ln_g"], "ln_b": sm["gm_ln_b"], "wm": sm["gm_w_s"][0],
        "bs_t": jnp.pad(sm["gm_b_s"][0].T, ((0, 0), (0, CH - N_BLK))),
        "conv_w": jnp.pad(sm["ssm_conv_w"][0], ((0, 4), (0, 0))), "conv_b": sm["ssm_conv_b"],
        "dt_bias": _lane_pad(sm["ssm_dt_bias"]), "a_log": _lane_pad(sm["ssm_a_log"]),
        "d_heads": _lane_pad(sm["ssm_d"]), "norm_g": sm["ssm_norm_g"],
    }
    sink_row = _lane_pad(sm["attn_sinks"])

    proj = matmul(y0, w_in_p, dims="nn", name="in_proj", tn=768)
    ab, hstates = mixer_fwd(proj, mixer_prm)
    w_out, w_up[0], w_down[0] = weights.layer0(ab)
    h1, y1 = matmul_rows(ab, w_out, dims="nn", name="out_proj", epi=_res_norm,
                         epi_args=(("tile", x), ("row", mlp_g[0])), outs=_H_AND_NORM)
    a1 = matmul(y1, w_up[0], dims="nn", name="mlp_up0", out_dtype=bf16, tn=1024)
    w_qkv, w_o, w_up[1], w_down[1] = weights.layer1(a1)
    h2, y2 = matmul_rows(a1, w_down[0], dims="nn", name="mlp_down0", a_pro=_relu2, epi=_res_norm,
                         epi_args=(("tile", h1), ("row", mix_g[1])), outs=_H_AND_NORM)
    qkv = matmul(y2, w_qkv, dims="nn", name="qkv_proj", tn=QKV_DIM, epi=_add_bias, epi_args=(("row", sm["b_qkv"]),))
    att = attn_fwd(qkv, sink_row)
    h3, y3 = matmul_rows(att, w_o, dims="nn", name="o_proj", epi=_bias_res_norm,
                         epi_args=(("row", sm["b_o"]), ("tile", h2), ("row", mlp_g[1])), outs=_H_AND_NORM)
    a3 = matmul(y3, w_up[1], dims="nn", name="mlp_up1", out_dtype=bf16, tn=1024)
    dh4, dg_final, loss = matmul_rows(
        a3, w_down[1], dims="nn", name="mlp_down1", a_pro=_relu2, epi=_res_norm_loss,
        epi_args=(("tile", h3), ("row", _row2(sm["final_norm_g"])), ("tile", target)),
        outs=(("tile", f32), ("sum", D_MODEL), ("sum", 128)))

    dh3, dg_mlp1, dw_up1, dw_down1 = _mlp_bwd(dh4, h3, mlp_g[1], y3, a3, w_up[1], w_down[1], 1)
    db_o = colsum(dh3, "db_o")
    datt = matmul(dh3, w_o, dims="nt", name="attn_dout", out_dtype=bf16)
    dw_o = matmul(att, dh3, dims="tn", name="dw_o", out_dtype=bf16)
    dqkv, dsink = attn_bwd(qkv, sink_row, datt)
    db_qkv = colsum(dqkv, "db_qkv")
    dw_qkv = matmul(y2, dqkv, dims="tn", name="dw_qkv", out_dtype=bf16, tn=QKV_DIM)
    dh2, dg_mix1 = matmul_rows(dqkv, w_qkv, dims="nt", name="dy_qkv", epi=_norm_bwd_res,
                               epi_args=(("tile", h2), ("row", mix_g[1]), ("tile", dh3)), outs=_DX_AND_DG)
    layer1 = [jnp.concatenate([_by_owner(dw_o), dw_up1, _by_owner(dw_down1)], axis=1),
              _col_shards(dw_qkv, QKV_SHARD, QKV_PAD)]
    flight1, token1 = reducer.start(layer1, "l1", direct=True)
    dh1, dg_mlp0, dw_up0, dw_down0 = _mlp_bwd(dh2, h1, mlp_g[0], y1, a1, w_up[0], w_down[0], 0, after=token1)
    pending1, shared1 = reducer.finish(flight1, dh1, "l1")
    dw_out = matmul(ab, dh1, dims="tn", name="dw_out", out_dtype=bf16, after=shared1)
    flight0, token0 = reducer.start(
        [jnp.concatenate([dw_up0, _by_owner(dw_down0), _by_owner(dw_out)], axis=1)], "l0", direct=True)
    dab = matmul(dh1, w_out, dims="nt", name="mixer_dout", tn=1024, after=token0)
    dproj, dmix = mixer_bwd(proj, hstates, dab, mixer_prm)
    dw_in_p = matmul(y0, dproj, dims="tn", name="dw_in", out_dtype=bf16, tn=768)
    pending0, shared0 = reducer.finish(flight0, dw_in_p, "l0")
    flight_in, token_in = reducer.start([_col_shards(dw_in_p, IN_SHARD, IN_PAD)], "in")
    dx, dg_mix0 = matmul_rows(dproj, w_in_p, dims="nt", name="dy_in", tm=256, epi=_norm_bwd_res,
                              epi_args=(("tile", x), ("row", mix_g[0]), ("tile", dh1)), outs=_DX_AND_DG,
                              after=token_in + shared0)
    pending_in, _ = reducer.finish(flight_in, dx, "in")
    r_l1, r_qkv = reducer.collect(pending1, dx, "l1")
    (r_l0,) = reducer.collect(pending0, dx, "l0")
    (r_in,) = reducer.collect(pending_in, dx, "in")
    reduced = {
        "w_out_even": r_l0[None, 2048:], "w_in_even": r_in[None, :, :IN_SHARD], "w_qkv": r_qkv[None, :, :QKV_SHARD],
        "w_o": r_l1[None, :256], "w_up": jnp.stack([r_l0[:1024], r_l1[256:1280]]),
        "w_down": jnp.stack([r_l0[1024:2048], r_l1[1280:]]),
    }

    small_grads = {
        "norm_mix_g": jnp.concatenate([dg_mix0, dg_mix1], axis=0),
        "norm_mlp_g": jnp.concatenate([dg_mlp0, dg_mlp1], axis=0),
        "final_norm_g": dg_final[0], "gm_ln_g": dmix["ln_g"], "gm_ln_b": dmix["ln_b"],
        "gm_w_s": dmix["wm"][None], "gm_b_s": dmix["bs_t"][:, :N_BLK].T[None],
        "ssm_conv_b": dmix["conv_b"], "ssm_dt_bias": dmix["dt_bias"][:, :SSM_HEADS],
        "ssm_a_log": dmix["a_log"][:, :SSM_HEADS], "ssm_d": dmix["d_heads"][:, :SSM_HEADS],
        "ssm_norm_g": dmix["norm_g"], "attn_sinks": dsink[:, :SSM_HEADS],
        "ssm_conv_w": dmix["conv_w"][None, :4], "b_qkv": db_qkv, "b_o": db_o,
    }
    return loss, dx, reduced, small_grads


def kernel(x, norm_mix_g, norm_mlp_g, final_norm_g, w_in_even, w_out_even, gm_ln_g, gm_ln_b, gm_w_s, gm_b_s, ssm_conv_w, ssm_conv_b, ssm_dt_bias, ssm_a_log, ssm_d, ssm_norm_g, w_qkv, b_qkv, w_o, b_o, attn_sinks, w_up, w_down, loss_target, m_norm_mix_g, m_norm_mlp_g, m_final_norm_g, m_w_in_even, m_w_out_even, m_gm_ln_g, m_gm_ln_b, m_gm_w_s, m_gm_b_s, m_ssm_conv_w, m_ssm_conv_b, m_ssm_dt_bias, m_ssm_a_log, m_ssm_d, m_ssm_norm_g, m_w_qkv, m_b_qkv, m_w_o, m_b_o, m_attn_sinks, m_w_up, m_w_down, v_norm_mix_g, v_norm_mlp_g, v_final_norm_g, v_w_in_even, v_w_out_even, v_gm_ln_g, v_gm_ln_b, v_gm_w_s, v_gm_b_s, v_ssm_conv_w, v_ssm_conv_b, v_ssm_dt_bias, v_ssm_a_log, v_ssm_d, v_ssm_norm_g, v_w_qkv, v_b_qkv, v_w_o, v_b_o, v_attn_sinks, v_w_up, v_w_down):
    w = dict(norm_mix_g=norm_mix_g, norm_mlp_g=norm_mlp_g, final_norm_g=final_norm_g, w_in_even=w_in_even,
             w_out_even=w_out_even, gm_ln_g=gm_ln_g, gm_ln_b=gm_ln_b, gm_w_s=gm_w_s, gm_b_s=gm_b_s,
             ssm_conv_w=ssm_conv_w, ssm_conv_b=ssm_conv_b, ssm_dt_bias=ssm_dt_bias, ssm_a_log=ssm_a_log,
             ssm_d=ssm_d, ssm_norm_g=ssm_norm_g, w_qkv=w_qkv, b_qkv=b_qkv, w_o=w_o, b_o=b_o,
             attn_sinks=attn_sinks, w_up=w_up, w_down=w_down)
    m = dict(norm_mix_g=m_norm_mix_g, norm_mlp_g=m_norm_mlp_g, final_norm_g=m_final_norm_g,
             w_in_even=m_w_in_even, w_out_even=m_w_out_even, gm_ln_g=m_gm_ln_g, gm_ln_b=m_gm_ln_b,
             gm_w_s=m_gm_w_s, gm_b_s=m_gm_b_s, ssm_conv_w=m_ssm_conv_w, ssm_conv_b=m_ssm_conv_b,
             ssm_dt_bias=m_ssm_dt_bias, ssm_a_log=m_ssm_a_log, ssm_d=m_ssm_d, ssm_norm_g=m_ssm_norm_g,
             w_qkv=m_w_qkv, b_qkv=m_b_qkv, w_o=m_w_o, b_o=m_b_o, attn_sinks=m_attn_sinks, w_up=m_w_up,
             w_down=m_w_down)
    v = dict(norm_mix_g=v_norm_mix_g, norm_mlp_g=v_norm_mlp_g, final_norm_g=v_final_norm_g,
             w_in_even=v_w_in_even, w_out_even=v_w_out_even, gm_ln_g=v_gm_ln_g, gm_ln_b=v_gm_ln_b,
             gm_w_s=v_gm_w_s, gm_b_s=v_gm_b_s, ssm_conv_w=v_ssm_conv_w, ssm_conv_b=v_ssm_conv_b,
             ssm_dt_bias=v_ssm_dt_bias, ssm_a_log=v_ssm_a_log, ssm_d=v_ssm_d, ssm_norm_g=v_ssm_norm_g,
             w_qkv=v_w_qkv, b_qkv=v_b_qkv, w_o=v_w_o, b_o=v_b_o, attn_sinks=v_attn_sinks, w_up=v_w_up,
             w_down=v_w_down)
    names = ("norm_mix_g", "norm_mlp_g", "final_norm_g", "w_in_even", "w_out_even", "gm_ln_g", "gm_ln_b",
             "gm_w_s", "gm_b_s", "ssm_conv_w", "ssm_conv_b", "ssm_dt_bias", "ssm_a_log", "ssm_d", "ssm_norm_g",
             "w_qkv", "b_qkv", "w_o", "b_o", "attn_sinks", "w_up", "w_down")

    cx, cy, cc = lax.axis_index("x"), lax.axis_index("y"), lax.axis_index("c")
    chip = 2 * cx + cy
    c_idx = jnp.reshape(cc, (1,)).astype(jnp.int32)
    chip_idx = jnp.reshape(chip, (1,)).astype(jnp.int32)

    weights = WeightGatherer(w, chip_idx)
    sm = {n: w[n] for n, _ in _SMALL_SHAPES[:_N_REPLICATED]}

    reducer = GradReducer(c_idx, jnp.concatenate([chip_idx, c_idx]))
    loss_part, dx, grads, small_grads = _local_step(x[0], loss_target[0], weights, sm, reducer)

    small_sum = allreduce_small(_pack([small_grads[n] for n, _ in _SMALL_SHAPES] + [loss_part], SMALL_ROWS))
    *small_list, loss_row = _unpack(small_sum, [s for _, s in _SMALL_SHAPES] + [loss_part.shape])
    loss = loss_row[0, 0]
    small_full = dict(zip([n for n, _ in _SMALL_SHAPES], small_list))
    for n, _ in _SMALL_SHAPES[:_N_REPLICATED]:
        grads[n] = small_full[n]
    for n, axis, width in _SHARDED_SMALL:
        grads[n] = lax.dynamic_slice_in_dim(small_full[n], chip * width, width, axis)
    grads = {n: grads[n].reshape(w[n].shape) for n in names}

    delta, new_m, new_v = {}, {}, {}
    for n in names:
        if n in _COLUMN_SHARDED:
            args = [jnp.transpose(d[n], (2, 0, 1)) for d in (w, grads, m, v)]
            grads[n] = jnp.transpose(args[1], (1, 2, 0))
            outs = adamw(*args, f"adamw_{n}")
            delta[n], new_m[n], new_v[n] = (jnp.transpose(o, (1, 2, 0)) for o in outs)
            continue
        shape = (1,) + w[n].shape if w[n].ndim == 1 else w[n].shape
        outs = adamw(*[d[n].reshape(shape) for d in (w, grads, m, v)], f"adamw_{n}")
        delta[n], new_m[n], new_v[n] = (o.reshape(w[n].shape) for o in outs)

    return (loss, dx[None], *[grads[n] for n in names], *[delta[n] for n in names],
            *[new_m[n] for n in names], *[new_v[n] for n in names])
```

```python
import functools

import jax
import jax.numpy as jnp
from jax import lax
from jax.experimental import pallas as pl
from jax.experimental.pallas import tpu as pltpu

f32 = jnp.float32
bf16 = jnp.bfloat16
MXU_DTYPE = bf16

RMS_EPS = 1e-5
LN_EPS = 1e-5
D_MODEL = 1024
D_FF = 4096
CH = 128
N_BLK = 8
SSM_HEADS = 16
IN_EVEN = 5136
NP_IN = 5376
OFF_U, OFF_V, OFF_Z, OFF_X, OFF_DT = 0, 1024, 2048, 3072, 5120
XBC_BLKS = 16
QKV_DIM = 1280
ATT_SCALE = 64 ** -0.5

ADAM_LR = 0.001
ADAM_B1 = 0.9
ADAM_B2 = 0.999
ADAM_EPS = 1e-08
ADAM_WD = 0.01
ADAM_STEP = 10

VMEM_LIMIT_BYTES = 48 * 1024 * 1024
N_CHIPS = 4
SMALL_ROWS = 256

NN = ((1,), (0,))
NT = ((1,), (1,))
TN = ((0,), (0,))


def _mm(a, b, dims):
    return lax.dot_general(a.astype(MXU_DTYPE), b.astype(MXU_DTYPE), (dims, ((), ())),
                           preferred_element_type=f32)


def _mm_exact(a, b):
    return jnp.dot(a, b, preferred_element_type=f32, precision=lax.Precision.HIGHEST)


def _cparams(sem=None):
    return pltpu.CompilerParams(dimension_semantics=sem, vmem_limit_bytes=VMEM_LIMIT_BYTES)


@jax.custom_vjp
def _swap64(x):
    return pltpu.roll(x, 64, axis=1)


_swap64.defvjp(lambda x: (pltpu.roll(x, 64, axis=1), None), lambda _, g: (pltpu.roll(g, 64, axis=1),))


@jax.custom_vjp
def _top_rows(x):
    return x[:x.shape[0] // 2]


_top_rows.defvjp(lambda x: (x[:x.shape[0] // 2], None),
                 lambda _, g: (jnp.concatenate([g, jnp.zeros_like(g)], axis=0),))


@jax.custom_vjp
def _bottom_rows(x):
    return x[x.shape[0] // 2:]


_bottom_rows.defvjp(lambda x: (x[x.shape[0] // 2:], None),
                    lambda _, g: (jnp.concatenate([jnp.zeros_like(g), g], axis=0),))


def _make_delay(k):
    @jax.custom_vjp
    def delay(ext):
        return pltpu.roll(ext, k, axis=0)[8:, :]

    def fwd(ext):
        return delay(ext), None

    def bwd(_, g):
        gp = jnp.concatenate([jnp.zeros((8, g.shape[1]), g.dtype), g], axis=0)
        return (pltpu.roll(gp, gp.shape[0] - k, axis=0),)

    delay.defvjp(fwd, bwd)
    return delay


_DELAYS = {k: _make_delay(k) for k in (1, 2, 3)}


_GELU_C = 0.7978845608028654
_GELU_K = 0.044715


@jax.custom_vjp
def _gelu(x):
    return 0.5 * x * (1.0 + jnp.tanh(_GELU_C * (x + _GELU_K * (x * x * x))))


def _gelu_fwd(x):
    t = jnp.tanh(_GELU_C * (x + _GELU_K * (x * x * x)))
    return 0.5 * x * (1.0 + t), (x, t)


def _gelu_bwd(res, g):
    x, t = res
    dz = _GELU_C + (3.0 * _GELU_C * _GELU_K) * (x * x)
    return (g * (0.5 * (1.0 + t) + (0.5 * x) * (1.0 - t * t) * dz),)


_gelu.defvjp(_gelu_fwd, _gelu_bwd)


def _col(m, lane, h):
    return jnp.sum(jnp.where(lane == h, m, 0.0), axis=1, keepdims=True)


def _row(m, sub, h):
    return jnp.sum(jnp.where(sub == h, m, 0.0), axis=0, keepdims=True)


def _mixer_chunk(us, vs, zs, xbcs, halos, dtblk, hps, prm):
    lane = lax.broadcasted_iota(jnp.int32, (CH, CH), 1)
    sub = lax.broadcasted_iota(jnp.int32, (CH, CH), 0)
    left = lane < 64
    top = sub < 64
    causal = sub >= lane

    gus = [_gelu(u) for u in us]
    gvs = [_gelu(v) for v in vs]
    mu = sum(jnp.sum(g, axis=1, keepdims=True) for g in gvs) / D_MODEL
    cen = [g - mu for g in gvs]
    var = sum(jnp.sum(c * c, axis=1, keepdims=True) for c in cen) / D_MODEL
    rstd = lax.rsqrt(var + LN_EPS)
    a_out = []
    for g in range(N_BLK):
        vn = cen[g] * rstd * prm["ln_g"][g] + prm["ln_b"][g]
        w = jnp.where(causal, prm["wm"][g], 0.0)
        mixed = _mm(w, vn, NN) + _col(prm["bs_t"], lane, g)
        a_out.append(gus[g] * mixed)

    act = []
    for b in range(XBC_BLKS):
        w8 = prm["conv_w"][b]
        sub8 = lax.broadcasted_iota(jnp.int32, w8.shape, 0)
        ext = jnp.concatenate([halos[b], xbcs[b]], axis=0)
        conv = xbcs[b] * _row(w8, sub8, 3) + prm["conv_b"][b]
        for k in (1, 2, 3):
            conv = conv + _DELAYS[k](ext) * _row(w8, sub8, 3 - k)
        act.append(jax.nn.silu(conv))

    dt = jax.nn.softplus(dtblk + prm["dt_bias"])
    a_neg = -jnp.exp(prm["a_log"])
    tri = causal.astype(f32)
    acum = _mm_exact(tri, dt * a_neg)
    acum_t = acum.T
    dt_t = dt.T
    last = sub == CH - 1
    ys, h_out = [], []
    for grp in range(4):
        bm = act[8 + grp]
        cm = act[12 + grp]
        cb = _mm(cm, bm, NT)
        for p in (2 * grp, 2 * grp + 1):
            h0, h1 = 2 * p, 2 * p + 1
            xp = act[p]
            hp = hps[p]
            wis = []
            for h in (h0, h1):
                seg = _col(acum, lane, h) - _row(acum_t, sub, h)
                decay = jnp.exp(jnp.where(causal, seg, -jnp.inf))
                wis.append(cb * decay * _row(dt_t, sub, h))
            wcat = jnp.concatenate(wis, axis=1)
            xbd = jnp.concatenate([jnp.where(left, xp, 0.0), jnp.where(left, 0.0, xp)], axis=0)
            y_diag = _mm(wcat, xbd, NN)
            a_end = [jnp.sum(jnp.where(last & (lane == h), acum, 0.0), keepdims=True) for h in (h0, h1)]
            a_col = jnp.where(left, _col(acum, lane, h0), _col(acum, lane, h1))
            dt_col = jnp.where(left, _col(dt, lane, h0), _col(dt, lane, h1))
            to_end = jnp.exp(jnp.where(left, a_end[0], a_end[1]) - a_col) * dt_col
            states = _mm(xp * to_end, bm, TN)
            chunk_decay = jnp.where(top, jnp.exp(a_end[0]), jnp.exp(a_end[1]))
            h_out.append(chunk_decay * hp + states)
            y_off = jnp.exp(a_col) * _mm(cm, hp, NT)
            d_skip = jnp.where(left[:1], _col(prm["d_heads"], lane[:1], h0), _col(prm["d_heads"], lane[:1], h1))
            ys.append((y_diag + y_off + xp * d_skip) * jax.nn.silu(zs[p]))

    b_out = []
    for grp in range(4):
        pair = (ys[2 * grp], ys[2 * grp + 1])
        ms = sum(jnp.sum(y * y, axis=1, keepdims=True) for y in pair) / 256.0
        r = lax.rsqrt(ms + RMS_EPS)
        for j, y in enumerate(pair):
            b_out.append(y * r * prm["norm_g"][2 * grp + j])
    return a_out, b_out, h_out


def _attn_block(qps, kprev, kcur, vprev, vcur, sink_row, first):
    lane = lax.broadcasted_iota(jnp.int32, (CH, CH), 1)
    left = lane < 64
    row2 = lax.broadcasted_iota(jnp.int32, (2 * CH, CH), 0)
    key2 = lax.broadcasted_iota(jnp.int32, (2 * CH, CH), 1)
    upper = row2 < CH
    own = key2 <= jnp.where(upper, row2, row2 - CH)

    def both_halves(a):
        sw = _swap64(a)
        return [jnp.where(left, a, sw), jnp.where(left, sw, a)]

    kc, kp, vc, vp = both_halves(kcur), both_halves(kprev), both_halves(vcur), both_halves(vprev)
    outs = []
    for p in range(N_BLK):
        j = p // 4
        q2 = jnp.concatenate([jnp.where(left, qps[p], 0.0), jnp.where(left, 0.0, qps[p])], axis=0)
        s_prev = jnp.where(first, -jnp.inf, _mm(q2, kp[j], NT) * ATT_SCALE)
        s = jnp.where(own, _mm(q2, kc[j], NT) * ATT_SCALE, s_prev)
        sink = jnp.where(upper[:, :1], _col(sink_row, lane[:1], 2 * p), _col(sink_row, lane[:1], 2 * p + 1))
        m = lax.stop_gradient(jnp.maximum(jnp.max(s, axis=1, keepdims=True), sink))
        pexp = jnp.exp(s - m)
        probs = pexp / (jnp.sum(pexp, axis=1, keepdims=True) + jnp.exp(sink - m))
        o = _mm(jnp.where(own, probs, 0.0), vc[j], NN) + _mm(jnp.where(own, 0.0, probs), vp[j], NN)
        outs.append(jnp.where(left, _top_rows(o), _bottom_rows(o)))
    return outs


def _rmsnorm(x, g):
    r = lax.rsqrt(jnp.mean(x * x, axis=-1, keepdims=True) + RMS_EPS)
    return x * r * g


def rmsnorm_fwd(x, g_row, name):
    s, d = x.shape
    tm = min(512, s)

    def body(x_ref, g_ref, y_ref):
        y_ref[...] = _rmsnorm(x_ref[...], g_ref[...]).astype(bf16)

    return pl.pallas_call(
        body, name=name, grid=(s // tm,),
        in_specs=[pl.BlockSpec((tm, d), lambda i: (i, 0)), pl.BlockSpec((1, d), lambda i: (0, 0))],
        out_specs=pl.BlockSpec((tm, d), lambda i: (i, 0)),
        out_shape=jax.ShapeDtypeStruct((s, d), bf16),
        compiler_params=_cparams(("parallel",)),
    )(x, g_row)


def colsum(x, name):
    s, n = x.shape
    tm = min(512, s)

    def body(x_ref, o_ref):
        @pl.when(pl.program_id(0) == 0)
        def _():
            o_ref[...] = jnp.zeros_like(o_ref)

        o_ref[...] += jnp.sum(x_ref[...].astype(f32), axis=0, keepdims=True)

    return pl.pallas_call(
        body, name=name, grid=(s // tm,),
        in_specs=[pl.BlockSpec((tm, n), lambda i: (i, 0))],
        out_specs=pl.BlockSpec((1, n), lambda i: (0, 0)),
        out_shape=jax.ShapeDtypeStruct((1, n), f32),
        compiler_params=_cparams(("arbitrary",)),
    )(x)


def _fit(dim, want):
    if dim <= want:
        return dim
    t = want
    while dim % t:
        t -= 128
    return t


def matmul(a, b, *, dims, name, out_dtype=f32, tm=1024, tn=512, tk=8192, a_pro=None, epi=None, epi_args=(),
           out_by_col_tile=False, after=None):
    if dims == "nn" and b.ndim == 3:
        (m, k), n, tn = a.shape, b.shape[0] * b.shape[2], b.shape[2]
    elif dims == "nn":
        (m, k), n = a.shape, b.shape[1]
    elif dims == "nt":
        (m, k), n = a.shape, b.shape[0]
    else:
        (k, m), n = a.shape, b.shape[1]
    tm, tn, tk = _fit(m, tm), _fit(n, tn), _fit(k, tk)
    nk = k // tk
    if dims == "nn":
        a_spec = pl.BlockSpec((tm, tk), lambda i, j, kk: (i, kk))
        b_spec = (pl.BlockSpec((None, tk, tn), lambda i, j, kk: (j, kk, 0)) if b.ndim == 3
                  else pl.BlockSpec((tk, tn), lambda i, j, kk: (kk, j)))
        dn = NN
    elif dims == "nt":
        a_spec = pl.BlockSpec((tm, tk), lambda i, j, kk: (i, kk))
        b_spec = pl.BlockSpec((tn, tk), lambda i, j, kk: (j, kk))
        dn = NT
    else:
        a_spec = pl.BlockSpec((tk, tm), lambda i, j, kk: (kk, i))
        b_spec = pl.BlockSpec((tk, tn), lambda i, j, kk: (kk, j))
        dn = TN
    e_specs = [pl.BlockSpec((tm, tn), lambda i, j, kk: (i, j)) if kind == "tile"
               else pl.BlockSpec((1, tn), lambda i, j, kk: (0, j)) for kind, _ in epi_args]
    n_epi = len(epi_args)
    order_specs = [] if after is None else [pl.BlockSpec((8, 128), lambda i, j, kk: (0, 0))]
    order_args = [] if after is None else [after]

    def body(*refs):
        a_ref, b_ref = refs[0], refs[1]
        e_refs = refs[2:2 + n_epi]
        n_in = 2 + n_epi + len(order_args)
        o_ref = refs[n_in]
        av = a_ref[...]
        if a_pro is not None:
            av = a_pro(av)
        part = _mm(av, b_ref[...], dn)

        def finish(acc):
            if epi is not None:
                acc = epi(acc, *[r[...] for r in e_refs])
            o_ref[...] = acc.astype(out_dtype)

        if nk == 1:
            finish(part)
        else:
            acc_ref = refs[n_in + 1]
            kk = pl.program_id(2)

            @pl.when(kk == 0)
            def _():
                acc_ref[...] = part

            @pl.when(kk > 0)
            def _():
                acc_ref[...] += part

            @pl.when(kk == nk - 1)
            def _():
                finish(acc_ref[...])

    if out_by_col_tile:
        out_spec = pl.BlockSpec((None, tm, tn), lambda i, j, kk: (j, i, 0))
        out_shape = jax.ShapeDtypeStruct((n // tn, m, tn), out_dtype)
    else:
        out_spec = pl.BlockSpec((tm, tn), lambda i, j, kk: (i, j))
        out_shape = jax.ShapeDtypeStruct((m, n), out_dtype)
    return pl.pallas_call(
        body, name=name, grid=(m // tm, n // tn, nk),
        in_specs=[a_spec, b_spec] + e_specs + order_specs,
        out_specs=out_spec,
        out_shape=out_shape,
        scratch_shapes=[pltpu.VMEM((tm, tn), f32)] if nk > 1 else [],
        compiler_params=_cparams(("parallel", "parallel", "arbitrary")),
    )(a, b, *[arr for _, arr in epi_args], *order_args)


def _relu2(a):
    r = jnp.maximum(a.astype(f32), 0.0)
    return r * r


def _add(acc, t):
    return acc + t


def _add_bias(acc, t):
    return acc + t


def _add_bias_res(acc, bias, res):
    return acc + bias + res


def _times_relu2_grad(acc, a):
    return acc * (2.0 * jnp.maximum(a.astype(f32), 0.0))


def matmul_rows(a, b, *, dims, name, epi, epi_args, outs, tm=512, a_pro=None, after=None):
    m, k = a.shape
    n = b.shape[-1] if dims == "nn" else b.shape[-2]
    tm = _fit(m, tm)
    dn = NN if dims == "nn" else NT
    e_specs = [pl.BlockSpec((tm, arr.shape[1]), lambda i: (i, 0)) if kind == "tile"
               else pl.BlockSpec((1, arr.shape[1]), lambda i: (0, 0)) for kind, arr in epi_args]
    order_specs = [] if after is None else [pl.BlockSpec((8, 128), lambda i: (0, 0))]
    order_args = [] if after is None else [after]
    n_in = 2 + len(epi_args) + len(order_args)

    def body(*refs):
        av = refs[0][...]
        if a_pro is not None:
            av = a_pro(av)
        if b.ndim == 3:
            kb = b.shape[2]
            acc = sum(_mm(av[:, s * kb:(s + 1) * kb], refs[1][s], dn) for s in range(b.shape[0]))
        else:
            acc = _mm(av, refs[1][...], dn)
        vals = epi(acc, *[r[...] for r in refs[2:2 + len(epi_args)]])
        for (kind, _), o_ref, val in zip(outs, refs[n_in:], vals):
            if kind == "tile":
                o_ref[...] = val.astype(o_ref.dtype)
            else:
                @pl.when(pl.program_id(0) == 0)
                def _():
                    o_ref[...] = jnp.zeros_like(o_ref)

                o_ref[...] += val

    out_specs = [pl.BlockSpec((tm, n), lambda i: (i, 0)) if kind == "tile" else pl.BlockSpec((1, arg), lambda i: (0, 0))
                 for kind, arg in outs]
    out_shape = [jax.ShapeDtypeStruct((m, n), arg) if kind == "tile" else jax.ShapeDtypeStruct((1, arg), f32)
                 for kind, arg in outs]
    return pl.pallas_call(
        body, name=name, grid=(m // tm,),
        in_specs=[pl.BlockSpec((tm, k), lambda i: (i, 0)), pl.BlockSpec(b.shape, lambda i: (0,) * b.ndim)]
                 + e_specs + order_specs,
        out_specs=out_specs, out_shape=out_shape,
        compiler_params=_cparams(("arbitrary",)),
    )(a, b, *[arr for _, arr in epi_args], *order_args)


def _res_norm(acc, res, g):
    h = acc + res
    return h, _rmsnorm(h, g)


def _bias_res_norm(acc, bias, res, g):
    h = acc + bias + res
    return h, _rmsnorm(h, g)


def _res_norm_loss(acc, res, g, target):
    def f(h, gv):
        err = jnp.square(_rmsnorm(h, gv) - target)
        return 0.5 * jnp.sum(jnp.mean(err, axis=-1, keepdims=True), axis=0, keepdims=True)

    loss, vjp = jax.vjp(f, acc + res, g)
    dh, dg = vjp(jnp.ones_like(loss))
    return dh, dg, jnp.broadcast_to(loss, (1, 128))


def _norm_bwd_res(dy, x, g, res):
    _, vjp = jax.vjp(_rmsnorm, x, g)
    dx, dg = vjp(dy)
    return res + dx, dg


_MIXER_PARAM_SHAPES = (
    ("ln_g", (1, D_MODEL)), ("ln_b", (1, D_MODEL)), ("wm", (N_BLK, CH, CH)), ("bs_t", (CH, CH)),
    ("conv_w", (8, 2048)), ("conv_b", (1, 2048)), ("dt_bias", (1, CH)), ("a_log", (1, CH)),
    ("d_heads", (1, CH)), ("norm_g", (1, D_MODEL)),
)


def _blocks(v, n, off=0):
    return [v[:, off + i * CH: off + (i + 1) * CH] for i in range(n)]


def _split_mixer_params(vals):
    p = dict(vals)
    return {
        "ln_g": _blocks(p["ln_g"], N_BLK), "ln_b": _blocks(p["ln_b"], N_BLK),
        "wm": [p["wm"][g] for g in range(N_BLK)], "bs_t": p["bs_t"],
        "conv_w": _blocks(p["conv_w"], XBC_BLKS), "conv_b": _blocks(p["conv_b"], XBC_BLKS),
        "dt_bias": p["dt_bias"], "a_log": p["a_log"], "d_heads": p["d_heads"],
        "norm_g": _blocks(p["norm_g"], N_BLK),
    }


def _mixer_leaves(proj_ref, halo_ref, keep_halo):
    pv = proj_ref
    us = [pv[:, OFF_U + i * CH: OFF_U + (i + 1) * CH] for i in range(N_BLK)]
    vs = [pv[:, OFF_V + i * CH: OFF_V + (i + 1) * CH] for i in range(N_BLK)]
    zs = [pv[:, OFF_Z + i * CH: OFF_Z + (i + 1) * CH] for i in range(N_BLK)]
    xbcs = [pv[:, OFF_X + i * CH: OFF_X + (i + 1) * CH] for i in range(XBC_BLKS)]
    halos = [halo_ref[:, OFF_X + i * CH: OFF_X + (i + 1) * CH] * keep_halo for i in range(XBC_BLKS)]
    dtblk = pv[:, OFF_DT: OFF_DT + CH]
    return us, vs, zs, xbcs, halos, dtblk


def mixer_fwd(proj, prm):
    s = proj.shape[0]
    nc = s // CH
    names = [n for n, _ in _MIXER_PARAM_SHAPES]

    def body(proj_ref, halo_ref, *rest):
        p_refs = rest[:len(names)]
        ab_ref, hs_ref, h_ref = rest[len(names):]
        c = pl.program_id(0)

        @pl.when(c == 0)
        def _():
            h_ref[...] = jnp.zeros_like(h_ref)

        hs_ref[...] = h_ref[...]
        keep = (c > 0).astype(f32)
        us, vs, zs, xbcs, halos, dtblk = _mixer_leaves(proj_ref, halo_ref, keep)
        hps = [h_ref[i * CH:(i + 1) * CH, :] for i in range(N_BLK)]
        p = _split_mixer_params({n: r[...] for n, r in zip(names, p_refs)})
        a_out, b_out, h_out = _mixer_chunk(us, vs, zs, xbcs, halos, dtblk, hps, p)
        for i in range(N_BLK):
            ab_ref[:, i * CH:(i + 1) * CH] = a_out[i].astype(bf16)
            ab_ref[:, D_MODEL + i * CH: D_MODEL + (i + 1) * CH] = b_out[i].astype(bf16)
            h_ref[i * CH:(i + 1) * CH, :] = h_out[i]

    def const(shape):
        return pl.BlockSpec(shape, lambda c: (0,) * len(shape))

    return pl.pallas_call(
        body, name="mixer_fwd", grid=(nc,),
        in_specs=[pl.BlockSpec((CH, NP_IN), lambda c: (c, 0)),
                  pl.BlockSpec((8, NP_IN), lambda c: (jnp.maximum(c * (CH // 8) - 1, 0), 0))]
                 + [const(shp) for _, shp in _MIXER_PARAM_SHAPES],
        out_specs=[pl.BlockSpec((CH, 2 * D_MODEL), lambda c: (c, 0)),
                   pl.BlockSpec((None, D_MODEL, CH), lambda c: (c, 0, 0))],
        out_shape=[jax.ShapeDtypeStruct((s, 2 * D_MODEL), bf16), jax.ShapeDtypeStruct((nc, D_MODEL, CH), f32)],
        scratch_shapes=[pltpu.VMEM((D_MODEL, CH), f32)],
        compiler_params=_cparams(("arbitrary",)),
    )(proj, proj, *[prm[n] for n in names])


def mixer_bwd(proj, hstates, dab, prm):
    s = proj.shape[0]
    nc = s // CH
    names = [n for n, _ in _MIXER_PARAM_SHAPES]
    npar = len(names)

    def body(proj_ref, halo_ref, hs_ref, dab_ref, *rest):
        p_refs = rest[:npar]
        dproj_ref = rest[npar]
        g_refs = rest[npar + 1: 2 * npar + 1]
        dh_ref, dhalo_ref = rest[2 * npar + 1:]
        i = pl.program_id(0)
        c = nc - 1 - i

        @pl.when(i == 0)
        def _():
            dh_ref[...] = jnp.zeros_like(dh_ref)
            dhalo_ref[...] = jnp.zeros_like(dhalo_ref)
            for r in g_refs:
                r[...] = jnp.zeros_like(r)

        keep = (c > 0).astype(f32)
        us, vs, zs, xbcs, halos, dtblk = _mixer_leaves(proj_ref, halo_ref, keep)
        hps = [hs_ref[j * CH:(j + 1) * CH, :] for j in range(N_BLK)]
        pvals = {n: r[...] for n, r in zip(names, p_refs)}

        def fn(us, vs, zs, xbcs, halos, dtblk, hps, pvals):
            return _mixer_chunk(us, vs, zs, xbcs, halos, dtblk, hps, _split_mixer_params(pvals))

        _, vjp = jax.vjp(fn, us, vs, zs, xbcs, halos, dtblk, hps, pvals)
        da = [dab_ref[:, j * CH:(j + 1) * CH].astype(f32) for j in range(N_BLK)]
        db = [dab_ref[:, D_MODEL + j * CH: D_MODEL + (j + 1) * CH].astype(f32) for j in range(N_BLK)]
        dh = [dh_ref[j * CH:(j + 1) * CH, :] for j in range(N_BLK)]
        dus, dvs, dzs, dxbcs, dhalos, ddt, dhps, dp = vjp((da, db, dh))

        for j in range(N_BLK):
            dproj_ref[:, OFF_U + j * CH: OFF_U + (j + 1) * CH] = dus[j].astype(bf16)
            dproj_ref[:, OFF_V + j * CH: OFF_V + (j + 1) * CH] = dvs[j].astype(bf16)
            dproj_ref[:, OFF_Z + j * CH: OFF_Z + (j + 1) * CH] = dzs[j].astype(bf16)
            dh_ref[j * CH:(j + 1) * CH, :] = dhps[j]
        zeros_top = jnp.zeros((CH - 8, CH), f32)
        for j in range(XBC_BLKS):
            late = jnp.concatenate([zeros_top, dhalo_ref[:, j * CH:(j + 1) * CH]], axis=0)
            dproj_ref[:, OFF_X + j * CH: OFF_X + (j + 1) * CH] = (dxbcs[j] + late).astype(bf16)
        for j in range(XBC_BLKS):
            dhalo_ref[:, j * CH:(j + 1) * CH] = dhalos[j] * keep
        lane = lax.broadcasted_iota(jnp.int32, (CH, CH), 1)
        dproj_ref[:, OFF_DT: OFF_DT + CH] = jnp.where(lane < SSM_HEADS, ddt, 0.0).astype(bf16)
        dproj_ref[:, OFF_DT + CH:] = jnp.zeros((CH, NP_IN - OFF_DT - CH), bf16)
        for n, r in zip(names, g_refs):
            r[...] += dp[n]

    def const(shape):
        return pl.BlockSpec(shape, lambda i: (0,) * len(shape))

    outs = pl.pallas_call(
        body, name="mixer_bwd", grid=(nc,),
        in_specs=[pl.BlockSpec((CH, NP_IN), lambda i: (nc - 1 - i, 0)),
                  pl.BlockSpec((8, NP_IN), lambda i: (jnp.maximum((nc - 1 - i) * (CH // 8) - 1, 0), 0)),
                  pl.BlockSpec((None, D_MODEL, CH), lambda i: (nc - 1 - i, 0, 0)),
                  pl.BlockSpec((CH, 2 * D_MODEL), lambda i: (nc - 1 - i, 0))]
                 + [const(shp) for _, shp in _MIXER_PARAM_SHAPES],
        out_specs=[pl.BlockSpec((CH, NP_IN), lambda i: (nc - 1 - i, 0))]
                  + [const(shp) for _, shp in _MIXER_PARAM_SHAPES],
        out_shape=[jax.ShapeDtypeStruct((s, NP_IN), bf16)]
                  + [jax.ShapeDtypeStruct(shp, f32) for _, shp in _MIXER_PARAM_SHAPES],
        scratch_shapes=[pltpu.VMEM((D_MODEL, CH), f32), pltpu.VMEM((8, 2048), f32)],
        compiler_params=_cparams(("arbitrary",)),
    )(proj, proj, hstates, dab, *[prm[n] for n in names])
    return outs[0], dict(zip(names, outs[1:]))


_K_BLK = D_MODEL // CH
_V_BLK = _K_BLK + 1


def _attn_specs(rev, nb):
    def blk(i):
        return nb - 1 - i if rev else i

    q_spec = pl.BlockSpec((CH, D_MODEL), lambda i: (blk(i), 0))
    kv = lambda col, prev: pl.BlockSpec(
        (CH, CH), lambda i: (jnp.maximum(blk(i) - 1, 0) if prev else blk(i), col))
    return q_spec, [kv(_K_BLK, True), kv(_K_BLK, False), kv(_V_BLK, True), kv(_V_BLK, False)]


def attn_fwd(qkv, sink_row):
    s = qkv.shape[0]
    nb = s // CH

    def body(q_ref, kp_ref, kc_ref, vp_ref, vc_ref, sink_ref, o_ref):
        qps = [q_ref[:, p * CH:(p + 1) * CH] for p in range(N_BLK)]
        outs = _attn_block(qps, kp_ref[...], kc_ref[...], vp_ref[...], vc_ref[...], sink_ref[...],
                           pl.program_id(0) == 0)
        for p in range(N_BLK):
            o_ref[:, p * CH:(p + 1) * CH] = outs[p].astype(bf16)

    q_spec, kv_specs = _attn_specs(False, nb)
    return pl.pallas_call(
        body, name="attn_fwd", grid=(nb,),
        in_specs=[q_spec] + kv_specs + [pl.BlockSpec((1, CH), lambda i: (0, 0))],
        out_specs=pl.BlockSpec((CH, D_MODEL), lambda i: (i, 0)),
        out_shape=jax.ShapeDtypeStruct((s, D_MODEL), bf16),
        compiler_params=_cparams(("parallel",)),
    )(qkv, qkv, qkv, qkv, qkv, sink_row)


def attn_bwd(qkv, sink_row, dout):
    s = qkv.shape[0]
    nb = s // CH

    def body(q_ref, kp_ref, kc_ref, vp_ref, vc_ref, sink_ref, do_ref, dqkv_ref, dsink_ref, carry_ref):
        i = pl.program_id(0)
        blk = nb - 1 - i

        @pl.when(i == 0)
        def _():
            dsink_ref[...] = jnp.zeros_like(dsink_ref)
            carry_ref[...] = jnp.zeros_like(carry_ref)

        qps = [q_ref[:, p * CH:(p + 1) * CH] for p in range(N_BLK)]
        first = blk == 0
        _, vjp = jax.vjp(lambda *a: _attn_block(*a, first), qps, kp_ref[...], kc_ref[...], vp_ref[...],
                         vc_ref[...], sink_ref[...])
        dos = [do_ref[:, p * CH:(p + 1) * CH].astype(f32) for p in range(N_BLK)]
        dqs, dkp, dkc, dvp, dvc, dsink = vjp(dos)
        for p in range(N_BLK):
            dqkv_ref[:, p * CH:(p + 1) * CH] = dqs[p].astype(bf16)
        dqkv_ref[:, D_MODEL: D_MODEL + CH] = (dkc + carry_ref[0]).astype(bf16)
        dqkv_ref[:, D_MODEL + CH:] = (dvc + carry_ref[1]).astype(bf16)
        keep = jnp.logical_not(first).astype(f32)
        carry_ref[0] = dkp * keep
        carry_ref[1] = dvp * keep
        dsink_ref[...] += dsink

    q_spec, kv_specs = _attn_specs(True, nb)
    return pl.pallas_call(
        body, name="attn_bwd", grid=(nb,),
        in_specs=[q_spec] + kv_specs + [pl.BlockSpec((1, CH), lambda i: (0, 0)),
                                        pl.BlockSpec((CH, D_MODEL), lambda i: (nb - 1 - i, 0))],
        out_specs=[pl.BlockSpec((CH, QKV_DIM), lambda i: (nb - 1 - i, 0)), pl.BlockSpec((1, CH), lambda i: (0, 0))],
        out_shape=[jax.ShapeDtypeStruct((s, QKV_DIM), bf16), jax.ShapeDtypeStruct((1, CH), f32)],
        scratch_shapes=[pltpu.VMEM((2, CH, CH), f32)],
        compiler_params=_cparams(("arbitrary",)),
    )(qkv, qkv, qkv, qkv, qkv, sink_row, dout)


def adamw(w, g, m, v, name):
    def body(w_ref, g_ref, m_ref, v_ref, d_ref, nm_ref, nv_ref):
        gv = g_ref[...]
        nm = ADAM_B1 * m_ref[...] + (1.0 - ADAM_B1) * gv
        nv = ADAM_B2 * v_ref[...] + (1.0 - ADAM_B2) * jnp.square(gv)
        m_hat = nm / (1.0 - ADAM_B1 ** ADAM_STEP)
        v_hat = nv / (1.0 - ADAM_B2 ** ADAM_STEP)
        d_ref[...] = -ADAM_LR * (m_hat / (jnp.sqrt(v_hat) + ADAM_EPS) + ADAM_WD * w_ref[...])
        nm_ref[...] = nm
        nv_ref[...] = nv

    out_shape = [jax.ShapeDtypeStruct(w.shape, f32)] * 3
    if w.ndim == 3 and w.shape[1] == 1:
        tr = max(t for t in range(1, 129) if w.shape[0] % t == 0)
        tile = pl.BlockSpec((tr, 1, w.shape[2]), lambda i: (i, 0, 0))
        return pl.pallas_call(
            body, name=name, grid=(w.shape[0] // tr,),
            in_specs=[tile] * 4, out_specs=[tile] * 3, out_shape=out_shape,
            compiler_params=_cparams(("parallel",)),
        )(w, g, m, v)
    if w.ndim == 3 and w.shape[1] % 256 == 0:
        tile = pl.BlockSpec((None, 256, w.shape[2]), lambda l, i: (l, i, 0))
        return pl.pallas_call(
            body, name=name, grid=(w.shape[0], w.shape[1] // 256),
            in_specs=[tile] * 4, out_specs=[tile] * 3, out_shape=out_shape,
            compiler_params=_cparams(("parallel", "parallel")),
        )(w, g, m, v)
    return pl.pallas_call(body, name=name, in_specs=[_VMEM] * 4, out_specs=[_VMEM] * 3, out_shape=out_shape,
                          compiler_params=_cparams())(w, g, m, v)


_MESH = pl.DeviceIdType.MESH
_ANY = pl.BlockSpec(memory_space=pl.ANY)
_VMEM = pl.BlockSpec(memory_space=pltpu.VMEM)


def _place():
    x, y, c = lax.axis_index("x"), lax.axis_index("y"), lax.axis_index("c")
    chips = [(1 - x, y), (x, 1 - y), (1 - x, 1 - y)]
    return x, y, c, 2 * x + y, chips, [2 * cx + cy for cx, cy in chips]


def _half(c, rows):
    return pl.ds(pl.multiple_of(c * (rows // 2), 16), rows // 2)


def _step_rows(rows):
    return max(t for t in range(16, 641, 16) if rows % t == 0)


def place_shard(b, slot, name, dtype=bf16, after=None):
    r, c = b.shape
    tr = _step_rows(r)

    def body(slot_ref, b_ref, *rest):
        rest[-1][...] = b_ref[...].astype(dtype)

    order_specs = [] if after is None else [pl.BlockSpec((8, 128), lambda i, s: (0, 0))]
    return pl.pallas_call(
        body, name=name,
        grid_spec=pltpu.PrefetchScalarGridSpec(
            num_scalar_prefetch=1, grid=(r // tr,),
            in_specs=[pl.BlockSpec((tr, c), lambda i, s: (i, 0))] + order_specs,
            out_specs=pl.BlockSpec((None, tr, c), lambda i, s: (s[0], i, 0))),
        out_shape=jax.ShapeDtypeStruct((N_CHIPS, r, c), dtype),
        compiler_params=_cparams(("parallel",)),
    )(slot, b, *([] if after is None else [after]))


_HBM = pl.BlockSpec(memory_space=pltpu.HBM)
_SEM = pl.BlockSpec(memory_space=pltpu.SEMAPHORE)
_EFFECT = pltpu.SideEffectType.DATAFLOW_SIDE_EFFECTING


def _gather_ici_copies(bufs, send_sems, recv_sems):
    x, y, c, me, chips, chip_idx = _place()
    return [pltpu.make_async_remote_copy(
        src_ref=buf.at[me, _half(c, buf.shape[1])], dst_ref=buf.at[chip_idx[j], _half(c, buf.shape[1])],
        send_sem=send_sems.at[3 * k + j], recv_sem=recv_sems.at[3 * k + j],
        device_id=(*chips[j], c), device_id_type=_MESH) for j in range(3) for k, buf in enumerate(bufs)]


def gather_start(groups, tag):
    sizes = [len(g) for g in groups]
    flat = [b for g in groups for b in g]
    n = len(flat)

    def body(*refs):
        bufs, sems = refs[:n], refs[n:n + 2 * len(groups)]
        refs[-1][...] = jnp.zeros_like(refs[-1])
        x, y, c, me, chips, chip_idx = _place()
        lo = 0
        for gi, size in enumerate(sizes):
            for j in range(3):
                for k, buf in enumerate(bufs[lo:lo + size]):
                    mine = buf.at[me, _half(c, buf.shape[1])]
                    pltpu.make_async_remote_copy(
                        src_ref=mine, dst_ref=mine, send_sem=sems[2 * gi].at[3 * k + j],
                        recv_sem=sems[2 * gi + 1].at[3 * k + j], device_id=(*chips[j], c),
                        device_id_type=_MESH).start()
            lo += size

    sem_shapes = [pltpu.SemaphoreType.DMA((3 * size,)) for size in sizes for _ in range(2)]
    outs = pl.pallas_call(
        body, name=f"gather_start_{tag}",
        out_shape=(*sem_shapes, *[pltpu.HBM(b.shape, b.dtype) for b in flat], jax.ShapeDtypeStruct((8, 128), f32)),
        in_specs=[_HBM] * n, out_specs=(*[_SEM] * len(sem_shapes), *[_HBM] * n, _VMEM),
        input_output_aliases={i: len(sem_shapes) + i for i in range(n)},
        compiler_params=pltpu.CompilerParams(has_side_effects=_EFFECT),
    )(*[pltpu.with_memory_space_constraint(b, pltpu.HBM) for b in flat])
    sems = [(outs[2 * gi], outs[2 * gi + 1]) for gi in range(len(groups))]
    thru, lo = [], len(sem_shapes)
    for size in sizes:
        thru.append(list(outs[lo:lo + size]))
        lo += size
    return sems, thru, outs[-1]


def gather_wait(bufs, sems, after, tag):
    n = len(bufs)

    def body(*refs):
        for cp in _gather_ici_copies(refs[:n], refs[n], refs[n + 1]):
            cp.wait_send()
            cp.wait_recv()

    extra = list(after)
    return list(pl.pallas_call(
        body, name=f"gather_wait_{tag}",
        out_shape=[pltpu.HBM(b.shape, b.dtype) for b in bufs],
        in_specs=[_HBM] * n + [_SEM, _SEM] + [_ANY] * len(extra), out_specs=[_HBM] * n,
        input_output_aliases={i: i for i in range(n)},
        compiler_params=pltpu.CompilerParams(has_side_effects=_EFFECT),
    )(*bufs, *sems, *extra))


def gather_forward(bufs, tag):
    n = len(bufs)

    def body(*refs):
        out_refs = refs[n:2 * n]
        send_sems, recv_sems = refs[2 * n:]
        x, y, c, me, chips, chip_idx = _place()

        def copy(k, j, half):
            part = out_refs[k].at[chip_idx[j], _half(half, out_refs[k].shape[1])]
            return pltpu.make_async_remote_copy(
                src_ref=part, dst_ref=part, send_sem=send_sems.at[3 * k + j], recv_sem=recv_sems.at[3 * k + j],
                device_id=(x, y, 1 - c), device_id_type=_MESH)

        sends = [copy(k, j, c) for j in range(3) for k in range(n)]
        for cp in sends:
            cp.start()
        for j in range(3):
            for k in range(n):
                copy(k, j, 1 - c).wait_recv()
        for cp in sends:
            cp.wait_send()

    return list(pl.pallas_call(
        body, name=f"gather_forward_{tag}",
        out_shape=[jax.ShapeDtypeStruct(b.shape, b.dtype) for b in bufs],
        in_specs=[_ANY] * n, out_specs=[_ANY] * n, input_output_aliases={i: i for i in range(n)},
        scratch_shapes=[pltpu.SemaphoreType.DMA((3 * n,)), pltpu.SemaphoreType.DMA((3 * n,))],
    )(*bufs))


def exchange_halves(bufs, tag):
    n = len(bufs)

    def body(*refs):
        g_refs, out_refs = refs[:n], refs[n:2 * n]
        send_sems, recv_sems = refs[2 * n:]
        x, y, c, *_ = _place()
        cps = [pltpu.make_async_remote_copy(
            src_ref=g_refs[b].at[:, _half(1 - c, g_refs[b].shape[1])], dst_ref=out_refs[b],
            send_sem=send_sems.at[b], recv_sem=recv_sems.at[b], device_id=(x, y, 1 - c), device_id_type=_MESH)
            for b in range(n)]
        for cp in cps:
            cp.start()
        for cp in cps:
            cp.wait()

    return pl.pallas_call(
        body, name=f"exchange_halves_{tag}",
        out_shape=[jax.ShapeDtypeStruct((N_CHIPS, b.shape[1] // 2, b.shape[2]), b.dtype) for b in bufs],
        in_specs=[_ANY] * n, out_specs=[_ANY] * n,
        scratch_shapes=[pltpu.SemaphoreType.DMA((n,)), pltpu.SemaphoreType.DMA((n,))],
    )(*bufs)


def add_halves(g, got, c_idx, name):
    hr, cols = got.shape[1], got.shape[2]
    tr = _step_rows(hr)
    steps = hr // tr

    def body(c_ref, g_ref, got_ref, o_ref):
        o_ref[...] = (g_ref[...].astype(f32) + got_ref[...].astype(f32)).astype(bf16)

    return pl.pallas_call(
        body, name=name,
        grid_spec=pltpu.PrefetchScalarGridSpec(
            num_scalar_prefetch=1, grid=(N_CHIPS, steps),
            in_specs=[pl.BlockSpec((None, tr, cols), lambda s, i, c: (s, c[0] * steps + i, 0)),
                      pl.BlockSpec((None, tr, cols), lambda s, i, c: (s, i, 0))],
            out_specs=pl.BlockSpec((None, tr, cols), lambda s, i, c: (s, i, 0))),
        out_shape=jax.ShapeDtypeStruct(got.shape, bf16),
        compiler_params=_cparams(("parallel", "parallel")),
    )(c_idx, g, got)


def sum_chips(t, got, place_idx, name):
    hr, cols = t.shape[1], t.shape[2]
    tr = _step_rows(hr)
    steps = hr // tr

    def body(idx_ref, t_ref, got_ref, o_ref):
        acc = t_ref[...].astype(f32)
        for j in range(3):
            acc = acc + got_ref[j].astype(f32)
        o_ref[...] = acc

    return pl.pallas_call(
        body, name=name,
        grid_spec=pltpu.PrefetchScalarGridSpec(
            num_scalar_prefetch=1, grid=(steps,),
            in_specs=[pl.BlockSpec((None, tr, cols), lambda i, idx: (idx[0], i, 0)),
                      pl.BlockSpec((3, tr, cols), lambda i, idx: (0, i, 0))],
            out_specs=pl.BlockSpec((tr, cols), lambda i, idx: (idx[1] * steps + i, 0))),
        out_shape=jax.ShapeDtypeStruct((2 * hr, cols), f32),
        compiler_params=_cparams(("parallel",)),
    )(place_idx, t, got)


def _share_copies(refs, send_sems, recv_sems):
    x, y, c, *_ = _place()
    return [pltpu.make_async_remote_copy(
        src_ref=ref.at[_half(c, ref.shape[0])], dst_ref=ref.at[_half(c, ref.shape[0])], send_sem=send_sems.at[b],
        recv_sem=recv_sems.at[b], device_id=(x, y, 1 - c), device_id_type=_MESH) for b, ref in enumerate(refs)]


def share_start(bufs, tag):
    n = len(bufs)

    def body(*refs):
        for cp in _share_copies(refs[:n], refs[n], refs[n + 1]):
            cp.start()
        token = refs[-1]
        token[...] = jnp.zeros_like(token)

    outs = pl.pallas_call(
        body, name=f"share_start_{tag}",
        out_shape=(pltpu.SemaphoreType.DMA((n,)), pltpu.SemaphoreType.DMA((n,)),
                   *[pltpu.HBM(b.shape, b.dtype) for b in bufs], jax.ShapeDtypeStruct((8, 128), f32)),
        in_specs=[_HBM] * n, out_specs=(_SEM, _SEM, *[_HBM] * n, _VMEM),
        input_output_aliases={i: 2 + i for i in range(n)},
        compiler_params=pltpu.CompilerParams(has_side_effects=_EFFECT),
    )(*[pltpu.with_memory_space_constraint(b, pltpu.HBM) for b in bufs])
    return (outs[0], outs[1], list(outs[2:2 + n])), outs[-1]


def share_wait(send_sems, recv_sems, bufs, after, tag):
    n = len(bufs)

    def body(*refs):
        x, y, c, *_ = _place()
        for b, ref in enumerate(refs[:n]):
            cp = pltpu.make_async_remote_copy(
                src_ref=ref.at[_half(c, ref.shape[0])], dst_ref=ref.at[_half(1 - c, ref.shape[0])],
                send_sem=refs[n].at[b], recv_sem=refs[n + 1].at[b], device_id=(x, y, 1 - c), device_id_type=_MESH)
            cp.wait_send()
            cp.wait_recv()

    return list(pl.pallas_call(
        body, name=f"share_wait_{tag}",
        out_shape=[pltpu.HBM(b.shape, b.dtype) for b in bufs],
        in_specs=[_HBM] * n + [_SEM, _SEM, _ANY], out_specs=[_HBM] * n,
        input_output_aliases={i: i for i in range(n)},
        compiler_params=pltpu.CompilerParams(has_side_effects=_EFFECT),
    )(*bufs, send_sems, recv_sems, after))


def _scatter_copies(t_refs, land_refs, send_sems, recv_sems):
    x, y, c, me, chips, chip_idx = _place()
    return [pltpu.make_async_remote_copy(
        src_ref=t_refs[b].at[chip_idx[j]], dst_ref=land_refs[b].at[j], send_sem=send_sems.at[3 * b + j],
        recv_sem=recv_sems.at[3 * b + j], device_id=(*chips[j], c), device_id_type=_MESH)
        for j in range(3) for b in range(len(t_refs))]


def scatter_start(ts, tag):
    n = len(ts)
    lands = [lax.empty((3,) + t.shape[1:], t.dtype) for t in ts]

    def body(*refs):
        for cp in _scatter_copies(refs[:n], refs[n:2 * n], refs[2 * n], refs[2 * n + 1]):
            cp.start()
        token = refs[-1]
        token[...] = jnp.zeros_like(token)

    hbm = [pltpu.HBM(a.shape, a.dtype) for a in (*ts, *lands)]
    outs = pl.pallas_call(
        body, name=f"scatter_start_{tag}",
        out_shape=(pltpu.SemaphoreType.DMA((3 * n,)), pltpu.SemaphoreType.DMA((3 * n,)), *hbm,
                   jax.ShapeDtypeStruct((8, 128), f32)),
        in_specs=[_HBM] * (2 * n), out_specs=(_SEM, _SEM, *[_HBM] * (2 * n), _VMEM),
        input_output_aliases={i: 2 + i for i in range(2 * n)},
        compiler_params=pltpu.CompilerParams(has_side_effects=_EFFECT),
    )(*[pltpu.with_memory_space_constraint(a, pltpu.HBM) for a in (*ts, *lands)])
    return outs[0], outs[1], list(outs[2:2 + n]), list(outs[2 + n:2 + 2 * n]), outs[-1]


def scatter_wait(send_sems, recv_sems, ts, lands, after, tag):
    n = len(ts)

    def body(*refs):
        for cp in _scatter_copies(refs[:n], refs[n:2 * n], refs[2 * n], refs[2 * n + 1]):
            cp.wait_send()
            cp.wait_recv()

    outs = pl.pallas_call(
        body, name=f"scatter_wait_{tag}",
        out_shape=[pltpu.HBM(a.shape, a.dtype) for a in (*ts, *lands)],
        in_specs=[_HBM] * (2 * n) + [_SEM, _SEM, _ANY], out_specs=[_HBM] * (2 * n),
        input_output_aliases={i: i for i in range(2 * n)},
        compiler_params=pltpu.CompilerParams(has_side_effects=_EFFECT),
    )(*ts, *lands, send_sems, recv_sems, after)
    return list(outs[:n]), list(outs[n:])


N_SENDERS = 7


def _direct_copies(g_refs, land_refs, send_sems, recv_sems):
    x, y, c, me, chips, chip_idx = _place()
    cps = []
    for b, (g, land) in enumerate(zip(g_refs, land_refs)):
        rows, base = g.shape[1], N_SENDERS * b
        cps.append(pltpu.make_async_remote_copy(
            src_ref=g.at[me, _half(1 - c, rows)], dst_ref=land.at[0], send_sem=send_sems.at[base],
            recv_sem=recv_sems.at[base], device_id=(x, y, 1 - c), device_id_type=_MESH))
        for j in range(3):
            for core in range(2):
                cps.append(pltpu.make_async_remote_copy(
                    src_ref=g.at[chip_idx[j], _half(core, rows)], dst_ref=land.at[1 + 2 * j + c],
                    send_sem=send_sems.at[base + 1 + 2 * j + core], recv_sem=recv_sems.at[base + 1 + 2 * j + c],
                    device_id=(*chips[j], core), device_id_type=_MESH))
    return cps


def direct_start(gs, tag):
    n = len(gs)
    lands = [lax.empty((N_SENDERS, g.shape[1] // 2, g.shape[2]), g.dtype) for g in gs]

    def body(*refs):
        for cp in _direct_copies(refs[:n], refs[n:2 * n], refs[2 * n], refs[2 * n + 1]):
            cp.start()
        token = refs[-1]
        token[...] = jnp.zeros_like(token)

    hbm = [pltpu.HBM(a.shape, a.dtype) for a in (*gs, *lands)]
    outs = pl.pallas_call(
        body, name=f"direct_start_{tag}",
        out_shape=(pltpu.SemaphoreType.DMA((N_SENDERS * n,)), pltpu.SemaphoreType.DMA((N_SENDERS * n,)), *hbm,
                   jax.ShapeDtypeStruct((8, 128), f32)),
        in_specs=[_HBM] * (2 * n), out_specs=(_SEM, _SEM, *[_HBM] * (2 * n), _VMEM),
        input_output_aliases={i: 2 + i for i in range(2 * n)},
        compiler_params=pltpu.CompilerParams(has_side_effects=_EFFECT),
    )(*[pltpu.with_memory_space_constraint(a, pltpu.HBM) for a in (*gs, *lands)])
    return outs[0], outs[1], list(outs[2:2 + n]), list(outs[2 + n:2 + 2 * n]), outs[-1]


def direct_wait(send_sems, recv_sems, gs, lands, after, tag):
    n = len(gs)

    def body(*refs):
        g_refs, land_refs, sends, recvs = refs[:n], refs[n:2 * n], refs[2 * n], refs[2 * n + 1]
        for b in range(n):
            for k in range(N_SENDERS):
                cp = pltpu.make_async_remote_copy(
                    src_ref=g_refs[b].at[0, _half(0, g_refs[b].shape[1])], dst_ref=land_refs[b].at[k],
                    send_sem=sends.at[N_SENDERS * b + k], recv_sem=recvs.at[N_SENDERS * b + k],
                    device_id=_place()[:3], device_id_type=_MESH)
                cp.wait_send()
                cp.wait_recv()

    outs = pl.pallas_call(
        body, name=f"direct_wait_{tag}",
        out_shape=[pltpu.HBM(a.shape, a.dtype) for a in (*gs, *lands)],
        in_specs=[_HBM] * (2 * n) + [_SEM, _SEM, _ANY], out_specs=[_HBM] * (2 * n),
        input_output_aliases={i: i for i in range(2 * n)},
        compiler_params=pltpu.CompilerParams(has_side_effects=_EFFECT),
    )(*gs, *lands, send_sems, recv_sems, after)
    return list(outs[:n]), list(outs[n:])


def sum_senders(g, lands, place_idx, name):
    hr, cols = lands.shape[1], lands.shape[2]
    tr = _step_rows(hr)
    steps = hr // tr

    def body(idx_ref, g_ref, land_ref, o_ref):
        acc = g_ref[...].astype(f32)
        for k in range(N_SENDERS):
            acc = acc + land_ref[k].astype(f32)
        o_ref[...] = acc

    return pl.pallas_call(
        body, name=name,
        grid_spec=pltpu.PrefetchScalarGridSpec(
            num_scalar_prefetch=1, grid=(steps,),
            in_specs=[pl.BlockSpec((None, tr, cols), lambda i, idx: (idx[0], idx[1] * steps + i, 0)),
                      pl.BlockSpec((N_SENDERS, tr, cols), lambda i, idx: (0, i, 0))],
            out_specs=pl.BlockSpec((tr, cols), lambda i, idx: (idx[1] * steps + i, 0))),
        out_shape=jax.ShapeDtypeStruct((2 * hr, cols), f32),
        compiler_params=_cparams(("parallel",)),
    )(place_idx, g, lands)


class GradReducer:
    def __init__(self, c_idx, place_idx):
        self.c_idx, self.place_idx = c_idx, place_idx

    def start(self, bufs, tag, direct=False):
        if direct:
            send_sems, recv_sems, gs, lands, token = direct_start(bufs, tag)
            return (True, send_sems, recv_sems, gs, lands), token
        got = exchange_halves(bufs, tag)
        ts = [add_halves(b, g, self.c_idx, f"add_halves_{tag}{i}") for i, (b, g) in enumerate(zip(bufs, got))]
        send_sems, recv_sems, ts, lands, token = scatter_start(ts, tag)
        return (False, send_sems, recv_sems, ts, lands), token

    def finish(self, state, after, tag):
        direct, *flight = state
        if direct:
            gs, lands = direct_wait(*flight, after, tag)
            sums = [sum_senders(g, l, self.place_idx, f"sum_senders_{tag}{i}") for i, (g, l) in enumerate(zip(gs, lands))]
        else:
            ts, lands = scatter_wait(*flight, after, tag)
            sums = [sum_chips(t, l, self.place_idx, f"sum_chips_{tag}{i}") for i, (t, l) in enumerate(zip(ts, lands))]
        return share_start(sums, tag)

    def collect(self, pending, after, tag):
        return share_wait(*pending, after, tag)


def allreduce_small(sp):
    rows = sp.shape[0]
    hr = rows // 2

    def body(s_ref, out_ref, sib_ref, chip_ref, four_ref, send_sems, recv_sems):
        x, y, c, me, chips, chip_idx = _place()
        sibling = (x, y, 1 - c)
        mine = pl.ds(pl.multiple_of(c * hr, 8), hr)
        other = pl.ds(pl.multiple_of((1 - c) * hr, 8), hr)

        swap = pltpu.make_async_remote_copy(src_ref=s_ref, dst_ref=sib_ref, send_sem=send_sems.at[0],
                                            recv_sem=recv_sems.at[0], device_id=sibling, device_id_type=_MESH)
        swap.start()
        swap.wait()
        is_core0 = c == 0
        chip_ref[...] = jnp.where(is_core0, s_ref[...], sib_ref[...]) + jnp.where(is_core0, sib_ref[...], s_ref[...])

        sends = [pltpu.make_async_remote_copy(
            src_ref=chip_ref.at[mine], dst_ref=four_ref.at[me], send_sem=send_sems.at[1 + j],
            recv_sem=recv_sems.at[1 + j], device_id=(*chips[j], c), device_id_type=_MESH) for j in range(3)]
        for cp in sends:
            cp.start()
        four_ref[me] = chip_ref[mine, :]
        for j in range(3):
            pltpu.make_async_remote_copy(
                src_ref=chip_ref.at[mine], dst_ref=four_ref.at[chip_idx[j]], send_sem=send_sems.at[1 + j],
                recv_sem=recv_sems.at[1 + j], device_id=(*chips[j], c), device_id_type=_MESH).wait_recv()
        for cp in sends:
            cp.wait_send()
        out_ref[mine, :] = (four_ref[0] + four_ref[1]) + (four_ref[2] + four_ref[3])

        share = pltpu.make_async_remote_copy(src_ref=out_ref.at[mine], dst_ref=out_ref.at[mine], send_sem=send_sems.at[4],
                                             recv_sem=recv_sems.at[4], device_id=sibling, device_id_type=_MESH)
        share.start()
        pltpu.make_async_remote_copy(src_ref=out_ref.at[mine], dst_ref=out_ref.at[other], send_sem=send_sems.at[4],
                                     recv_sem=recv_sems.at[4], device_id=sibling, device_id_type=_MESH).wait_recv()
        share.wait_send()

    return pl.pallas_call(
        body, name="allreduce_small",
        out_shape=jax.ShapeDtypeStruct(sp.shape, sp.dtype),
        in_specs=[_VMEM], out_specs=_VMEM,
        scratch_shapes=[pltpu.VMEM(sp.shape, sp.dtype), pltpu.VMEM(sp.shape, sp.dtype),
                        pltpu.VMEM((N_CHIPS, hr, sp.shape[1]), sp.dtype),
                        pltpu.SemaphoreType.DMA((5,)), pltpu.SemaphoreType.DMA((5,))],
        compiler_params=_cparams(),
    )(sp)


def _n_rows(shape):
    n = 1
    for d in shape:
        n *= d
    return 8 * (-(-n // 8192))


def _pack(arrays, total_rows):
    parts = []
    for a in arrays:
        flat = a.reshape(-1)
        parts.append(jnp.pad(flat, (0, 1024 * _n_rows(a.shape) - flat.shape[0])).reshape(-1, 1024))
    rows = jnp.concatenate(parts, axis=0)
    return jnp.pad(rows, ((0, total_rows - rows.shape[0]), (0, 0)))


def _unpack(packed, shapes):
    out, r = [], 0
    for shp in shapes:
        n = 1
        for d in shp:
            n *= d
        nr = _n_rows(shp)
        out.append(packed[r:r + nr].reshape(-1)[:n].reshape(shp))
        r += nr
    return out


_COLUMN_SHARDED = ("w_in_even", "w_qkv")
IN_SHARD, IN_PAD = 1284, 1408
QKV_SHARD, QKV_PAD = 320, 384


def _lane_padded(a, cols):
    return jnp.pad(a, ((0, 0), (0, cols - a.shape[1])))


_SMALL_SHAPES = (
    ("norm_mix_g", (2, 1024)), ("norm_mlp_g", (2, 1024)), ("final_norm_g", (1024,)), ("gm_ln_g", (1, 1024)),
    ("gm_ln_b", (1, 1024)), ("gm_w_s", (1, 8, 128, 128)), ("gm_b_s", (1, 8, 128)), ("ssm_conv_b", (1, 2048)),
    ("ssm_dt_bias", (1, 16)), ("ssm_a_log", (1, 16)), ("ssm_d", (1, 16)), ("ssm_norm_g", (1, 1024)),
    ("attn_sinks", (1, 16)), ("ssm_conv_w", (1, 4, 2048)), ("b_qkv", (1, 1280)), ("b_o", (1, 1024)),
)
_N_REPLICATED = 13
_SHARDED_SMALL = (("ssm_conv_w", 2, 512), ("b_qkv", 1, 320), ("b_o", 1, 256))
_SHARD_PACK_ROWS = 32


def _cols_by_owner(a):
    return a.transpose(1, 0, 2).reshape(a.shape[1], -1)


class WeightGatherer:
    def __init__(self, w, chip_idx):
        def place(tag, b, dtype=bf16, after=None):
            return place_shard(b, chip_idx, f"place_shard_{tag}", dtype, after)

        sems_in, bufs_in, self.started = gather_start([
            [place("in", _lane_padded(w["w_in_even"][0], IN_PAD)),
             place("small", _pack([w[n] for n, _, _ in _SHARDED_SMALL], _SHARD_PACK_ROWS), f32)]], "in")
        t = self.started
        sems, bufs, self.all_started = gather_start([
            [place("out", w["w_out_even"][0], after=t), place("up0", w["w_up"][0], after=t),
             place("down0", w["w_down"][0], after=t)],
            [place("qkv", _lane_padded(w["w_qkv"][0], QKV_PAD), after=t), place("o", w["w_o"][0], after=t),
             place("up1", w["w_up"][1], after=t), place("down1", w["w_down"][1], after=t)],
        ], "rest")
        self.sems, self.bufs = sems_in + sems, bufs_in + bufs

    def _group(self, gi, after, tag):
        return gather_forward(gather_wait(self.bufs[gi], self.sems[gi], after, tag), tag)

    def mixer_in(self, after):
        g, small = self._group(0, [after, self.all_started], "in")
        shard_shapes = [tuple(width if i == axis else d for i, d in enumerate(dict(_SMALL_SHAPES)[n]))
                        for n, axis, width in _SHARDED_SMALL]
        per_chip = [_unpack(small[s], shard_shapes) for s in range(N_CHIPS)]
        full = {n: jnp.concatenate([per_chip[s][i] for s in range(N_CHIPS)], axis=axis)
                for i, (n, axis, _) in enumerate(_SHARDED_SMALL)}
        w_in = jnp.concatenate([g[s, :, :IN_SHARD] for s in range(N_CHIPS)], axis=1)
        return _lane_padded(w_in, NP_IN), full

    def layer0(self, after):
        w_out, w_up, w_down = self._group(1, [after], "l0")
        return w_out.reshape(2048, 1024), w_up, w_down.reshape(4096, 1024)

    def layer1(self, after):
        q, w_o, w_up, w_down = self._group(2, [after], "l1")
        w_qkv = jnp.concatenate([q[s, :, :QKV_SHARD] for s in range(N_CHIPS)], axis=1)
        return w_qkv, w_o.reshape(1024, 1024), w_up, w_down.reshape(4096, 1024)


def _row2(v):
    return v.reshape(1, -1)


def _lane_pad(v):
    return jnp.pad(v, ((0, 0), (0, CH - v.shape[1])))


_H_AND_NORM = (("tile", f32), ("tile", bf16))
_DX_AND_DG = (("tile", f32), ("sum", D_MODEL))


def _mlp_bwd(dh_out, h, g_row, y, a, w_up, w_down, tag, after=None):
    da = matmul(dh_out, w_down, dims="nt", name=f"mlp_da{tag}", out_dtype=bf16, tn=1024,
                epi=_times_relu2_grad, epi_args=(("tile", a),), after=after)
    dw_down = matmul(a, dh_out, dims="tn", name=f"mlp_dwdown{tag}", out_dtype=bf16, a_pro=_relu2)
    dw_up = matmul(y, da, dims="tn", name=f"mlp_dwup{tag}", out_dtype=bf16, tn=1024, out_by_col_tile=True)
    dh, dg = matmul_rows(da, w_up, dims="nt", name=f"mlp_dy{tag}", epi=_norm_bwd_res,
                         epi_args=(("tile", h), ("row", g_row), ("tile", dh_out)), outs=_DX_AND_DG)
    return dh, dg, dw_up, dw_down


def _by_owner(a):
    return a.reshape(N_CHIPS, a.shape[0] // N_CHIPS, a.shape[1])


def _col_shards(a, shard, padded):
    return jnp.stack([_lane_padded(a[:, shard * s: shard * (s + 1)], padded) for s in range(N_CHIPS)])


def _local_step(x, target, weights, sm, reducer):
    w_up, w_down = [None, None], [None, None]
    mix_g = [_row2(sm["norm_mix_g"][i]) for i in range(2)]
    y0 = rmsnorm_fwd(x, mix_g[0] + weights.started[:1, :1], "mix_norm0")
    w_in_p, sharded_small = weights.mixer_in(y0)
    sm = {**sm, **sharded_small}
    mlp_g = [_row2(sm["norm_mlp_g"][i]) for i in range(2)]
    mixer_prm = {
        "ln_g": sm["gm_ln_g"], "ln_b": sm["gm_ln_b"], "wm": sm["gm_w_s"][0],
        "bs_t": jnp.pad(sm["gm_b_s"][0].T, ((0, 0), (0, CH - N_BLK))),
        "conv_w": jnp.pad(sm["ssm_conv_w"][0], ((0, 4), (0, 0))), "conv_b": sm["ssm_conv_b"],
        "dt_bias": _lane_pad(sm["ssm_dt_bias"]), "a_log": _lane_pad(sm["ssm_a_log"]),
        "d_heads": _lane_pad(sm["ssm_d"]), "norm_g": sm["ssm_norm_g"],
    }
    sink_row = _lane_pad(sm["attn_sinks"])

    proj = matmul(y0, w_in_p, dims="nn", name="in_proj", tn=768)
    ab, hstates = mixer_fwd(proj, mixer_prm)
    w_out, w_up[0], w_down[0] = weights.layer0(ab)
    h1, y1 = matmul_rows(ab, w_out, dims="nn", name="out_proj", epi=_res_norm,
                         epi_args=(("tile", x), ("row", mlp_g[0])), outs=_H_AND_NORM)
    a1 = matmul(y1, w_up[0], dims="nn", name="mlp_up0", out_dtype=bf16, tn=1024)
    w_qkv, w_o, w_up[1], w_down[1] = weights.layer1(a1)
    h2, y2 = matmul_rows(a1, w_down[0], dims="nn", name="mlp_down0", a_pro=_relu2, epi=_res_norm,
                         epi_args=(("tile", h1), ("row", mix_g[1])), outs=_H_AND_NORM)
    qkv = matmul(y2, w_qkv, dims="nn", name="qkv_proj", tn=QKV_DIM, epi=_add_bias, epi_args=(("row", sm["b_qkv"]),))
    att = attn_fwd(qkv, sink_row)
    h3, y3 = matmul_rows(att, w_o, dims="nn", name="o_proj", epi=_bias_res_norm,
                         epi_args=(("row", sm["b_o"]), ("tile", h2), ("row", mlp_g[1])), outs=_H_AND_NORM)
    a3 = matmul(y3, w_up[1], dims="nn", name="mlp_up1", out_dtype=bf16, tn=1024)
    dh4, dg_final, loss = matmul_rows(
        a3, w_down[1], dims="nn", name="mlp_down1", a_pro=_relu2, epi=_res_norm_loss,
        epi_args=(("tile", h3), ("row", _row2(sm["final_norm_g"])), ("tile", target)),
        outs=(("tile", f32), ("sum", D_MODEL), ("sum", 128)))

    dh3, dg_mlp1, dw_up1, dw_down1 = _mlp_bwd(dh4, h3, mlp_g[1], y3, a3, w_up[1], w_down[1], 1)
    db_o = colsum(dh3, "db_o")
    datt = matmul(dh3, w_o, dims="nt", name="attn_dout", out_dtype=bf16)
    dw_o = matmul(att, dh3, dims="tn", name="dw_o", out_dtype=bf16)
    dqkv, dsink = attn_bwd(qkv, sink_row, datt)
    db_qkv = colsum(dqkv, "db_qkv")
    dw_qkv = matmul(y2, dqkv, dims="tn", name="dw_qkv", out_dtype=bf16, tn=QKV_DIM)
    dh2, dg_mix1 = matmul_rows(dqkv, w_qkv, dims="nt", name="dy_qkv", epi=_norm_bwd_res,
                               epi_args=(("tile", h2), ("row", mix_g[1]), ("tile", dh3)), outs=_DX_AND_DG)
    layer1 = [jnp.concatenate([_by_owner(dw_o), dw_up1, _by_owner(dw_down1)], axis=1),
              _col_shards(dw_qkv, QKV_SHARD, QKV_PAD)]
    flight1, token1 = reducer.start(layer1, "l1", direct=True)
    dh1, dg_mlp0, dw_up0, dw_down0 = _mlp_bwd(dh2, h1, mlp_g[0], y1, a1, w_up[0], w_down[0], 0, after=token1)
    pending1, shared1 = reducer.finish(flight1, dh1, "l1")
    dw_out = matmul(ab, dh1, dims="tn", name="dw_out", out_dtype=bf16, after=shared1)
    flight0, token0 = reducer.start(
        [jnp.concatenate([dw_up0, _by_owner(dw_down0), _by_owner(dw_out)], axis=1)], "l0", direct=True)
    dab = matmul(dh1, w_out, dims="nt", name="mixer_dout", tn=1024, after=token0)
    dproj, dmix = mixer_bwd(proj, hstates, dab, mixer_prm)
    dw_in_p = matmul(y0, dproj, dims="tn", name="dw_in", out_dtype=bf16, tn=768)
    pending0, shared0 = reducer.finish(flight0, dw_in_p, "l0")
    flight_in, token_in = reducer.start([_col_shards(dw_in_p, IN_SHARD, IN_PAD)], "in")
    dx, dg_mix0 = matmul_rows(dproj, w_in_p, dims="nt", name="dy_in", tm=256, epi=_norm_bwd_res,
                              epi_args=(("tile", x), ("row", mix_g[0]), ("tile", dh1)), outs=_DX_AND_DG,
                              after=token_in + shared0)
    pending_in, _ = reducer.finish(flight_in, dx, "in")
    r_l1, r_qkv = reducer.collect(pending1, dx, "l1")
    (r_l0,) = reducer.collect(pending0, dx, "l0")
    (r_in,) = reducer.collect(pending_in, dx, "in")
    reduced = {
        "w_out_even": r_l0[None, 2048:], "w_in_even": r_in[None, :, :IN_SHARD], "w_qkv": r_qkv[None, :, :QKV_SHARD],
        "w_o": r_l1[None, :256], "w_up": jnp.stack([r_l0[:1024], r_l1[256:1280]]),
        "w_down": jnp.stack([r_l0[1024:2048], r_l1[1280:]]),
    }

    small_grads = {
        "norm_mix_g": jnp.concatenate([dg_mix0, dg_mix1], axis=0),
        "norm_mlp_g": jnp.concatenate([dg_mlp0, dg_mlp1], axis=0),
        "final_norm_g": dg_final[0], "gm_ln_g": dmix["ln_g"], "gm_ln_b": dmix["ln_b"],
        "gm_w_s": dmix["wm"][None], "gm_b_s": dmix["bs_t"][:, :N_BLK].T[None],
        "ssm_conv_b": dmix["conv_b"], "ssm_dt_bias": dmix["dt_bias"][:, :SSM_HEADS],
        "ssm_a_log": dmix["a_log"][:, :SSM_HEADS], "ssm_d": dmix["d_heads"][:, :SSM_HEADS],
        "ssm_norm_g": dmix["norm_g"], "attn_sinks": dsink[:, :SSM_HEADS],
        "ssm_conv_w": dmix["conv_w"][None, :4], "b_qkv": db_qkv, "b_o": db_o,
    }
    return loss, dx, reduced, small_grads


def kernel(x, norm_mix_g, norm_mlp_g, final_norm_g, w_in_even, w_out_even, gm_ln_g, gm_ln_b, gm_w_s, gm_b_s, ssm_conv_w, ssm_conv_b, ssm_dt_bias, ssm_a_log, ssm_d, ssm_norm_g, w_qkv, b_qkv, w_o, b_o, attn_sinks, w_up, w_down, loss_target, m_norm_mix_g, m_norm_mlp_g, m_final_norm_g, m_w_in_even, m_w_out_even, m_gm_ln_g, m_gm_ln_b, m_gm_w_s, m_gm_b_s, m_ssm_conv_w, m_ssm_conv_b, m_ssm_dt_bias, m_ssm_a_log, m_ssm_d, m_ssm_norm_g, m_w_qkv, m_b_qkv, m_w_o, m_b_o, m_attn_sinks, m_w_up, m_w_down, v_norm_mix_g, v_norm_mlp_g, v_final_norm_g, v_w_in_even, v_w_out_even, v_gm_ln_g, v_gm_ln_b, v_gm_w_s, v_gm_b_s, v_ssm_conv_w, v_ssm_conv_b, v_ssm_dt_bias, v_ssm_a_log, v_ssm_d, v_ssm_norm_g, v_w_qkv, v_b_qkv, v_w_o, v_b_o, v_attn_sinks, v_w_up, v_w_down):
    w = dict(norm_mix_g=norm_mix_g, norm_mlp_g=norm_mlp_g, final_norm_g=final_norm_g, w_in_even=w_in_even,
             w_out_even=w_out_even, gm_ln_g=gm_ln_g, gm_ln_b=gm_ln_b, gm_w_s=gm_w_s, gm_b_s=gm_b_s,
             ssm_conv_w=ssm_conv_w, ssm_conv_b=ssm_conv_b, ssm_dt_bias=ssm_dt_bias, ssm_a_log=ssm_a_log,
             ssm_d=ssm_d, ssm_norm_g=ssm_norm_g, w_qkv=w_qkv, b_qkv=b_qkv, w_o=w_o, b_o=b_o,
             attn_sinks=attn_sinks, w_up=w_up, w_down=w_down)
    m = dict(norm_mix_g=m_norm_mix_g, norm_mlp_g=m_norm_mlp_g, final_norm_g=m_final_norm_g,
             w_in_even=m_w_in_even, w_out_even=m_w_out_even, gm_ln_g=m_gm_ln_g, gm_ln_b=m_gm_ln_b,
             gm_w_s=m_gm_w_s, gm_b_s=m_gm_b_s, ssm_conv_w=m_ssm_conv_w, ssm_conv_b=m_ssm_conv_b,
             ssm_dt_bias=m_ssm_dt_bias, ssm_a_log=m_ssm_a_log, ssm_d=m_ssm_d, ssm_norm_g=m_ssm_norm_g,
             w_qkv=m_w_qkv, b_qkv=m_b_qkv, w_o=m_w_o, b_o=m_b_o, attn_sinks=m_attn_sinks, w_up=m_w_up,
             w_down=m_w_down)
    v = dict(norm_mix_g=v_norm_mix_g, norm_mlp_g=v_norm_mlp_g, final_norm_g=v_final_norm_g,
             w_in_even=v_w_in_even, w_out_even=v_w_out_even, gm_ln_g=v_gm_ln_g, gm_ln_b=v_gm_ln_b,
             gm_w_s=v_gm_w_s, gm_b_s=v_gm_b_s, ssm_conv_w=v_ssm_conv_w, ssm_conv_b=v_ssm_conv_b,
             ssm_dt_bias=v_ssm_dt_bias, ssm_a_log=v_ssm_a_log, ssm_d=v_ssm_d, ssm_norm_g=v_ssm_norm_g,
             w_qkv=v_w_qkv, b_qkv=v_b_qkv, w_o=v_w_o, b_o=v_b_o, attn_sinks=v_attn_sinks, w_up=v_w_up,
             w_down=v_w_down)
    names = ("norm_mix_g", "norm_mlp_g", "final_norm_g", "w_in_even", "w_out_even", "gm_ln_g", "gm_ln_b",
             "gm_w_s", "gm_b_s", "ssm_conv_w", "ssm_conv_b", "ssm_dt_bias", "ssm_a_log", "ssm_d", "ssm_norm_g",
             "w_qkv", "b_qkv", "w_o", "b_o", "attn_sinks", "w_up", "w_down")

    cx, cy, cc = lax.axis_index("x"), lax.axis_index("y"), lax.axis_index("c")
    chip = 2 * cx + cy
    c_idx = jnp.reshape(cc, (1,)).astype(jnp.int32)
    chip_idx = jnp.reshape(chip, (1,)).astype(jnp.int32)

    weights = WeightGatherer(w, chip_idx)
    sm = {n: w[n] for n, _ in _SMALL_SHAPES[:_N_REPLICATED]}

    reducer = GradReducer(c_idx, jnp.concatenate([chip_idx, c_idx]))
    loss_part, dx, grads, small_grads = _local_step(x[0], loss_target[0], weights, sm, reducer)

    small_sum = allreduce_small(_pack([small_grads[n] for n, _ in _SMALL_SHAPES] + [loss_part], SMALL_ROWS))
    *small_list, loss_row = _unpack(small_sum, [s for _, s in _SMALL_SHAPES] + [loss_part.shape])
    loss = loss_row[0, 0]
    small_full = dict(zip([n for n, _ in _SMALL_SHAPES], small_list))
    for n, _ in _SMALL_SHAPES[:_N_REPLICATED]:
        grads[n] = small_full[n]
    for n, axis, width in _SHARDED_SMALL:
        grads[n] = lax.dynamic_slice_in_dim(small_full[n], chip * width, width, axis)
    grads = {n: grads[n].reshape(w[n].shape) for n in names}

    delta, new_m, new_v = {}, {}, {}
    for n in names:
        if n in _COLUMN_SHARDED:
            args = [jnp.transpose(d[n], (2, 0, 1)) for d in (w, grads, m, v)]
            grads[n] = jnp.transpose(args[1], (1, 2, 0))
            outs = adamw(*args, f"adamw_{n}")
            delta[n], new_m[n], new_v[n] = (jnp.transpose(o, (1, 2, 0)) for o in outs)
            continue
        shape = (1,) + w[n].shape if w[n].ndim == 1 else w[n].shape
        outs = adamw(*[d[n].reshape(shape) for d in (w, grads, m, v)], f"adamw_{n}")
        delta[n], new_m[n], new_v[n] = (o.reshape(w[n].shape) for o in outs)

    return (loss, dx[None], *[grads[n] for n in names], *[delta[n] for n in names],
            *[new_m[n] for n in names], *[new_v[n] for n in names])
```

```python
import functools

import jax
import jax.numpy as jnp
from jax import lax
from jax.experimental import pallas as pl
from jax.experimental.pallas import tpu as pltpu

f32 = jnp.float32
bf16 = jnp.bfloat16
MXU_DTYPE = bf16

RMS_EPS = 1e-5
LN_EPS = 1e-5
D_MODEL = 1024
D_FF = 4096
CH = 128
N_BLK = 8
SSM_HEADS = 16
IN_EVEN = 5136
NP_IN = 5376
OFF_U, OFF_V, OFF_Z, OFF_X, OFF_DT = 0, 1024, 2048, 3072, 5120
XBC_BLKS = 16
QKV_DIM = 1280
ATT_SCALE = 64 ** -0.5

ADAM_LR = 0.001
ADAM_B1 = 0.9
ADAM_B2 = 0.999
ADAM_EPS = 1e-08
ADAM_WD = 0.01
ADAM_STEP = 10

VMEM_LIMIT_BYTES = 48 * 1024 * 1024
N_CHIPS = 4
SMALL_ROWS = 256

NN = ((1,), (0,))
NT = ((1,), (1,))
TN = ((0,), (0,))


def _mm(a, b, dims):
    return lax.dot_general(a.astype(MXU_DTYPE), b.astype(MXU_DTYPE), (dims, ((), ())),
                           preferred_element_type=f32)


def _mm_exact(a, b):
    return jnp.dot(a, b, preferred_element_type=f32, precision=lax.Precision.HIGHEST)


def _cparams(sem=None):
    return pltpu.CompilerParams(dimension_semantics=sem, vmem_limit_bytes=VMEM_LIMIT_BYTES)


@jax.custom_vjp
def _swap64(x):
    return pltpu.roll(x, 64, axis=1)


_swap64.defvjp(lambda x: (pltpu.roll(x, 64, axis=1), None), lambda _, g: (pltpu.roll(g, 64, axis=1),))


def _row_blocks_of(x):
    return tuple(x[i:i + CH] for i in range(0, x.shape[0], CH))


@jax.custom_vjp
def _row_blocks(x):
    return _row_blocks_of(x)


_row_blocks.defvjp(lambda x: (_row_blocks_of(x), None), lambda _, gs: (jnp.concatenate(gs, axis=0),))


def _make_delay(k):
    @jax.custom_vjp
    def delay(ext):
        return pltpu.roll(ext, k, axis=0)[8:, :]

    def fwd(ext):
        return delay(ext), None

    def bwd(_, g):
        gp = jnp.concatenate([jnp.zeros((8, g.shape[1]), g.dtype), g], axis=0)
        return (pltpu.roll(gp, gp.shape[0] - k, axis=0),)

    delay.defvjp(fwd, bwd)
    return delay


_DELAYS = {k: _make_delay(k) for k in (1, 2, 3)}


_GELU_C = 0.7978845608028654
_GELU_K = 0.044715


@jax.custom_vjp
def _gelu(x):
    return 0.5 * x * (1.0 + jnp.tanh(_GELU_C * (x + _GELU_K * (x * x * x))))


def _gelu_fwd(x):
    t = jnp.tanh(_GELU_C * (x + _GELU_K * (x * x * x)))
    return 0.5 * x * (1.0 + t), (x, t)


def _gelu_bwd(res, g):
    x, t = res
    dz = _GELU_C + (3.0 * _GELU_C * _GELU_K) * (x * x)
    return (g * (0.5 * (1.0 + t) + (0.5 * x) * (1.0 - t * t) * dz),)


_gelu.defvjp(_gelu_fwd, _gelu_bwd)


def _col(m, lane, h):
    return jnp.sum(jnp.where(lane == h, m, 0.0), axis=1, keepdims=True)


def _row(m, sub, h):
    return jnp.sum(jnp.where(sub == h, m, 0.0), axis=0, keepdims=True)


def _mixer_chunk(us, vs, zs, xbcs, halos, dtblk, hps, prm):
    lane = lax.broadcasted_iota(jnp.int32, (CH, CH), 1)
    sub = lax.broadcasted_iota(jnp.int32, (CH, CH), 0)
    left = lane < 64
    top = sub < 64
    causal = sub >= lane

    gus = [_gelu(u) for u in us]
    gvs = [_gelu(v) for v in vs]
    mu = sum(jnp.sum(g, axis=1, keepdims=True) for g in gvs) / D_MODEL
    cen = [g - mu for g in gvs]
    var = sum(jnp.sum(c * c, axis=1, keepdims=True) for c in cen) / D_MODEL
    rstd = lax.rsqrt(var + LN_EPS)
    a_out = []
    for g in range(N_BLK):
        vn = cen[g] * rstd * prm["ln_g"][g] + prm["ln_b"][g]
        w = jnp.where(causal, prm["wm"][g], 0.0)
        mixed = _mm(w, vn, NN) + _col(prm["bs_t"], lane, g)
        a_out.append(gus[g] * mixed)

    act = []
    for b in range(XBC_BLKS):
        w8 = prm["conv_w"][b]
        sub8 = lax.broadcasted_iota(jnp.int32, w8.shape, 0)
        ext = jnp.concatenate([halos[b], xbcs[b]], axis=0)
        conv = xbcs[b] * _row(w8, sub8, 3) + prm["conv_b"][b]
        for k in (1, 2, 3):
            conv = conv + _DELAYS[k](ext) * _row(w8, sub8, 3 - k)
        act.append(jax.nn.silu(conv))

    dt = jax.nn.softplus(dtblk + prm["dt_bias"])
    a_neg = -jnp.exp(prm["a_log"])
    tri = causal.astype(f32)
    acum = _mm_exact(tri, dt * a_neg)
    acum_t = acum.T
    dt_t = dt.T
    last = sub == CH - 1
    ys, h_out = [], []
    for grp in range(4):
        bm = act[8 + grp]
        cm = act[12 + grp]
        cb = _mm(cm, bm, NT)
        for p in (2 * grp, 2 * grp + 1):
            h0, h1 = 2 * p, 2 * p + 1
            xp = act[p]
            hp = hps[p]
            wis = []
            for h in (h0, h1):
                seg = _col(acum, lane, h) - _row(acum_t, sub, h)
                decay = jnp.exp(jnp.where(causal, seg, -jnp.inf))
                wis.append(cb * decay * _row(dt_t, sub, h))
            wcat = jnp.concatenate(wis, axis=1)
            xbd = jnp.concatenate([jnp.where(left, xp, 0.0), jnp.where(left, 0.0, xp)], axis=0)
            y_diag = _mm(wcat, xbd, NN)
            a_end = [jnp.sum(jnp.where(last & (lane == h), acum, 0.0), keepdims=True) for h in (h0, h1)]
            a_col = jnp.where(left, _col(acum, lane, h0), _col(acum, lane, h1))
            dt_col = jnp.where(left, _col(dt, lane, h0), _col(dt, lane, h1))
            to_end = jnp.exp(jnp.where(left, a_end[0], a_end[1]) - a_col) * dt_col
            states = _mm(xp * to_end, bm, TN)
            chunk_decay = jnp.where(top, jnp.exp(a_end[0]), jnp.exp(a_end[1]))
            h_out.append(chunk_decay * hp + states)
            y_off = jnp.exp(a_col) * _mm(cm, hp, NT)
            d_skip = jnp.where(left[:1], _col(prm["d_heads"], lane[:1], h0), _col(prm["d_heads"], lane[:1], h1))
            ys.append((y_diag + y_off + xp * d_skip) * jax.nn.silu(zs[p]))

    b_out = []
    for grp in range(4):
        pair = (ys[2 * grp], ys[2 * grp + 1])
        ms = sum(jnp.sum(y * y, axis=1, keepdims=True) for y in pair) / 256.0
        r = lax.rsqrt(ms + RMS_EPS)
        for j, y in enumerate(pair):
            b_out.append(y * r * prm["norm_g"][2 * grp + j])
    return a_out, b_out, h_out


def _attn_block(qps, kprev, kcur, vprev, vcur, sink_row, first):
    lane = lax.broadcasted_iota(jnp.int32, (CH, CH), 1)
    left = lane < 64
    own = lane <= lax.broadcasted_iota(jnp.int32, (CH, CH), 0)
    own8 = jnp.concatenate([own] * N_BLK, axis=0)

    def both_halves(a):
        sw = _swap64(a)
        return [jnp.where(left, a, sw), jnp.where(left, sw, a)]

    kc, kp, vc, vp = both_halves(kcur), both_halves(kprev), both_halves(vcur), both_halves(vprev)
    outs = []
    for j in range(2):
        q8 = jnp.concatenate([part for p in range(4 * j, 4 * j + 4)
                              for part in (jnp.where(left, qps[p], 0.0), jnp.where(left, 0.0, qps[p]))], axis=0)
        s_cur = _row_blocks(_mm(q8, kc[j], NT))
        s_prev = _row_blocks(_mm(q8, kp[j], NT))
        probs = []
        for h in range(N_BLK):
            s = jnp.where(own, s_cur[h] * ATT_SCALE, jnp.where(first, -jnp.inf, s_prev[h] * ATT_SCALE))
            sink = _col(sink_row, lane[:1], N_BLK * j + h)
            m = lax.stop_gradient(jnp.maximum(jnp.max(s, axis=1, keepdims=True), sink))
            pexp = jnp.exp(s - m)
            probs.append(pexp / (jnp.sum(pexp, axis=1, keepdims=True) + jnp.exp(sink - m)))
        p8 = jnp.concatenate(probs, axis=0)
        o = _row_blocks(_mm(jnp.where(own8, p8, 0.0), vc[j], NN) + _mm(jnp.where(own8, 0.0, p8), vp[j], NN))
        for t in range(4):
            outs.append(jnp.where(left, o[2 * t], o[2 * t + 1]))
    return outs


def _rmsnorm(x, g):
    r = lax.rsqrt(jnp.mean(x * x, axis=-1, keepdims=True) + RMS_EPS)
    return x * r * g


def rmsnorm_fwd(x, g_row, name):
    s, d = x.shape
    tm = min(512, s)

    def body(x_ref, g_ref, y_ref):
        y_ref[...] = _rmsnorm(x_ref[...], g_ref[...]).astype(bf16)

    return pl.pallas_call(
        body, name=name, grid=(s // tm,),
        in_specs=[pl.BlockSpec((tm, d), lambda i: (i, 0)), pl.BlockSpec((1, d), lambda i: (0, 0))],
        out_specs=pl.BlockSpec((tm, d), lambda i: (i, 0)),
        out_shape=jax.ShapeDtypeStruct((s, d), bf16),
        compiler_params=_cparams(("parallel",)),
    )(x, g_row)


def _fit(dim, want):
    if dim <= want:
        return dim
    t = want
    while dim % t:
        t -= 128
    return t


def matmul(a, b, *, dims, name, out_dtype=f32, tm=1024, tn=512, tk=8192, a_pro=None, epi=None, epi_args=(),
           out_by_col_tile=False, after=None):
    if dims == "nn" and b.ndim == 3:
        (m, k), n, tn = a.shape, b.shape[0] * b.shape[2], b.shape[2]
    elif dims == "nn":
        (m, k), n = a.shape, b.shape[1]
    elif dims == "nt":
        (m, k), n = a.shape, b.shape[0]
    else:
        (k, m), n = a.shape, b.shape[1]
    tm, tn, tk = _fit(m, tm), _fit(n, tn), _fit(k, tk)
    nk = k // tk
    if dims == "nn":
        a_spec = pl.BlockSpec((tm, tk), lambda i, j, kk: (i, kk))
        b_spec = (pl.BlockSpec((None, tk, tn), lambda i, j, kk: (j, kk, 0)) if b.ndim == 3
                  else pl.BlockSpec((tk, tn), lambda i, j, kk: (kk, j)))
        dn = NN
    elif dims == "nt":
        a_spec = pl.BlockSpec((tm, tk), lambda i, j, kk: (i, kk))
        b_spec = pl.BlockSpec((tn, tk), lambda i, j, kk: (j, kk))
        dn = NT
    else:
        a_spec = pl.BlockSpec((tk, tm), lambda i, j, kk: (kk, i))
        b_spec = pl.BlockSpec((tk, tn), lambda i, j, kk: (kk, j))
        dn = TN
    e_specs = [pl.BlockSpec((tm, tn), lambda i, j, kk: (i, j)) if kind == "tile"
               else pl.BlockSpec((1, tn), lambda i, j, kk: (0, j)) for kind, _ in epi_args]
    n_epi = len(epi_args)
    order_specs = [] if after is None else [pl.BlockSpec((8, 128), lambda i, j, kk: (0, 0))]
    order_args = [] if after is None else [after]

    def body(*refs):
        a_ref, b_ref = refs[0], refs[1]
        e_refs = refs[2:2 + n_epi]
        n_in = 2 + n_epi + len(order_args)
        o_ref = refs[n_in]
        av = a_ref[...]
        if a_pro is not None:
            av = a_pro(av)
        part = _mm(av, b_ref[...], dn)

        def finish(acc):
            if epi is not None:
                acc = epi(acc, *[r[...] for r in e_refs])
            o_ref[...] = acc.astype(out_dtype)

        if nk == 1:
            finish(part)
        else:
            acc_ref = refs[n_in + 1]
            kk = pl.program_id(2)

            @pl.when(kk == 0)
            def _():
                acc_ref[...] = part

            @pl.when(kk > 0)
            def _():
                acc_ref[...] += part

            @pl.when(kk == nk - 1)
            def _():
                finish(acc_ref[...])

    if out_by_col_tile:
        out_spec = pl.BlockSpec((None, tm, tn), lambda i, j, kk: (j, i, 0))
        out_shape = jax.ShapeDtypeStruct((n // tn, m, tn), out_dtype)
    else:
        out_spec = pl.BlockSpec((tm, tn), lambda i, j, kk: (i, j))
        out_shape = jax.ShapeDtypeStruct((m, n), out_dtype)
    return pl.pallas_call(
        body, name=name, grid=(m // tm, n // tn, nk),
        in_specs=[a_spec, b_spec] + e_specs + order_specs,
        out_specs=out_spec,
        out_shape=out_shape,
        scratch_shapes=[pltpu.VMEM((tm, tn), f32)] if nk > 1 else [],
        compiler_params=_cparams(("parallel", "parallel", "arbitrary")),
    )(a, b, *[arr for _, arr in epi_args], *order_args)


def _relu2(a):
    r = jnp.maximum(a.astype(f32), 0.0)
    return r * r


def _add(acc, t):
    return acc + t


def _add_bias(acc, t):
    return acc + t


def _add_bias_res(acc, bias, res):
    return acc + bias + res


def _times_relu2_grad(acc, a):
    return acc * (2.0 * jnp.maximum(a.astype(f32), 0.0))


def matmul_rows(a, b, *, dims, name, epi, epi_args, outs, tm=512, a_pro=None, after=None):
    m, k = a.shape
    n = b.shape[-1] if dims == "nn" else b.shape[-2]
    tm = _fit(m, tm)
    dn = NN if dims == "nn" else NT
    e_specs = [pl.BlockSpec((tm, arr.shape[1]), lambda i: (i, 0)) if kind == "tile"
               else pl.BlockSpec((1, arr.shape[1]), lambda i: (0, 0)) for kind, arr in epi_args]
    order_specs = [] if after is None else [pl.BlockSpec((8, 128), lambda i: (0, 0))]
    order_args = [] if after is None else [after]
    n_in = 2 + len(epi_args) + len(order_args)

    def body(*refs):
        av = refs[0][...]
        if a_pro is not None:
            av = a_pro(av)
        if b.ndim == 3:
            kb = b.shape[2]
            acc = sum(_mm(av[:, s * kb:(s + 1) * kb], refs[1][s], dn) for s in range(b.shape[0]))
        else:
            acc = _mm(av, refs[1][...], dn)
        vals = epi(acc, *[r[...] for r in refs[2:2 + len(epi_args)]])
        for (kind, _), o_ref, val in zip(outs, refs[n_in:], vals):
            if kind == "tile":
                o_ref[...] = val.astype(o_ref.dtype)
            else:
                @pl.when(pl.program_id(0) == 0)
                def _():
                    o_ref[...] = jnp.zeros_like(o_ref)

                o_ref[...] += val

    out_specs = [pl.BlockSpec((tm, n), lambda i: (i, 0)) if kind == "tile" else pl.BlockSpec((1, arg), lambda i: (0, 0))
                 for kind, arg in outs]
    out_shape = [jax.ShapeDtypeStruct((m, n), arg) if kind == "tile" else jax.ShapeDtypeStruct((1, arg), f32)
                 for kind, arg in outs]
    return pl.pallas_call(
        body, name=name, grid=(m // tm,),
        in_specs=[pl.BlockSpec((tm, k), lambda i: (i, 0)), pl.BlockSpec(b.shape, lambda i: (0,) * b.ndim)]
                 + e_specs + order_specs,
        out_specs=out_specs, out_shape=out_shape,
        compiler_params=_cparams(("arbitrary",)),
    )(a, b, *[arr for _, arr in epi_args], *order_args)


def _res_norm(acc, res, g):
    h = acc + res
    return h, _rmsnorm(h, g)


def _bias_res_norm(acc, bias, res, g):
    h = acc + bias + res
    return h, _rmsnorm(h, g)


def _res_norm_loss(acc, res, g, target):
    def f(h, gv):
        err = jnp.square(_rmsnorm(h, gv) - target)
        return 0.5 * jnp.sum(jnp.mean(err, axis=-1, keepdims=True), axis=0, keepdims=True)

    loss, vjp = jax.vjp(f, acc + res, g)
    dh, dg = vjp(jnp.ones_like(loss))
    return dh, dg, jnp.broadcast_to(loss, (1, 128))


def _norm_bwd_res_colsum(dy, x, g, res):
    dx, dg = _norm_bwd_res(dy, x, g, res)
    return dx, dg, jnp.sum(dx, axis=0, keepdims=True)


def _norm_bwd_res(dy, x, g, res):
    _, vjp = jax.vjp(_rmsnorm, x, g)
    dx, dg = vjp(dy)
    return res + dx, dg


_MIXER_PARAM_SHAPES = (
    ("ln_g", (1, D_MODEL)), ("ln_b", (1, D_MODEL)), ("wm", (N_BLK, CH, CH)), ("bs_t", (CH, CH)),
    ("conv_w", (8, 2048)), ("conv_b", (1, 2048)), ("dt_bias", (1, CH)), ("a_log", (1, CH)),
    ("d_heads", (1, CH)), ("norm_g", (1, D_MODEL)),
)


def _blocks(v, n, off=0):
    return [v[:, off + i * CH: off + (i + 1) * CH] for i in range(n)]


def _split_mixer_params(vals):
    p = dict(vals)
    return {
        "ln_g": _blocks(p["ln_g"], N_BLK), "ln_b": _blocks(p["ln_b"], N_BLK),
        "wm": [p["wm"][g] for g in range(N_BLK)], "bs_t": p["bs_t"],
        "conv_w": _blocks(p["conv_w"], XBC_BLKS), "conv_b": _blocks(p["conv_b"], XBC_BLKS),
        "dt_bias": p["dt_bias"], "a_log": p["a_log"], "d_heads": p["d_heads"],
        "norm_g": _blocks(p["norm_g"], N_BLK),
    }


def _mixer_leaves(proj_ref, halo_ref, keep_halo):
    pv = proj_ref
    us = [pv[:, OFF_U + i * CH: OFF_U + (i + 1) * CH] for i in range(N_BLK)]
    vs = [pv[:, OFF_V + i * CH: OFF_V + (i + 1) * CH] for i in range(N_BLK)]
    zs = [pv[:, OFF_Z + i * CH: OFF_Z + (i + 1) * CH] for i in range(N_BLK)]
    xbcs = [pv[:, OFF_X + i * CH: OFF_X + (i + 1) * CH] for i in range(XBC_BLKS)]
    halos = [halo_ref[:, OFF_X + i * CH: OFF_X + (i + 1) * CH] * keep_halo for i in range(XBC_BLKS)]
    dtblk = pv[:, OFF_DT: OFF_DT + CH]
    return us, vs, zs, xbcs, halos, dtblk


def mixer_fwd(proj, prm):
    s = proj.shape[0]
    nc = s // CH
    names = [n for n, _ in _MIXER_PARAM_SHAPES]

    def body(proj_ref, halo_ref, *rest):
        p_refs = rest[:len(names)]
        ab_ref, hs_ref, h_ref = rest[len(names):]
        c = pl.program_id(0)

        @pl.when(c == 0)
        def _():
            h_ref[...] = jnp.zeros_like(h_ref)

        hs_ref[...] = h_ref[...]
        keep = (c > 0).astype(f32)
        us, vs, zs, xbcs, halos, dtblk = _mixer_leaves(proj_ref, halo_ref, keep)
        hps = [h_ref[i * CH:(i + 1) * CH, :] for i in range(N_BLK)]
        p = _split_mixer_params({n: r[...] for n, r in zip(names, p_refs)})
        a_out, b_out, h_out = _mixer_chunk(us, vs, zs, xbcs, halos, dtblk, hps, p)
        for i in range(N_BLK):
            ab_ref[:, i * CH:(i + 1) * CH] = a_out[i].astype(bf16)
            ab_ref[:, D_MODEL + i * CH: D_MODEL + (i + 1) * CH] = b_out[i].astype(bf16)
            h_ref[i * CH:(i + 1) * CH, :] = h_out[i]

    def const(shape):
        return pl.BlockSpec(shape, lambda c: (0,) * len(shape))

    return pl.pallas_call(
        body, name="mixer_fwd", grid=(nc,),
        in_specs=[pl.BlockSpec((CH, NP_IN), lambda c: (c, 0)),
                  pl.BlockSpec((8, NP_IN), lambda c: (jnp.maximum(c * (CH // 8) - 1, 0), 0))]
                 + [const(shp) for _, shp in _MIXER_PARAM_SHAPES],
        out_specs=[pl.BlockSpec((CH, 2 * D_MODEL), lambda c: (c, 0)),
                   pl.BlockSpec((None, D_MODEL, CH), lambda c: (c, 0, 0))],
        out_shape=[jax.ShapeDtypeStruct((s, 2 * D_MODEL), bf16), jax.ShapeDtypeStruct((nc, D_MODEL, CH), f32)],
        scratch_shapes=[pltpu.VMEM((D_MODEL, CH), f32)],
        compiler_params=_cparams(("arbitrary",)),
    )(proj, proj, *[prm[n] for n in names])


def mixer_bwd(proj, hstates, dab, prm):
    s = proj.shape[0]
    nc = s // CH
    names = [n for n, _ in _MIXER_PARAM_SHAPES]
    npar = len(names)

    def body(proj_ref, halo_ref, hs_ref, dab_ref, *rest):
        p_refs = rest[:npar]
        dproj_ref = rest[npar]
        g_refs = rest[npar + 1: 2 * npar + 1]
        dh_ref, dhalo_ref = rest[2 * npar + 1:]
        i = pl.program_id(0)
        c = nc - 1 - i

        @pl.when(i == 0)
        def _():
            dh_ref[...] = jnp.zeros_like(dh_ref)
            dhalo_ref[...] = jnp.zeros_like(dhalo_ref)
            for r in g_refs:
                r[...] = jnp.zeros_like(r)

        keep = (c > 0).astype(f32)
        us, vs, zs, xbcs, halos, dtblk = _mixer_leaves(proj_ref, halo_ref, keep)
        hps = [hs_ref[j * CH:(j + 1) * CH, :] for j in range(N_BLK)]
        pvals = {n: r[...] for n, r in zip(names, p_refs)}

        def fn(us, vs, zs, xbcs, halos, dtblk, hps, pvals):
            return _mixer_chunk(us, vs, zs, xbcs, halos, dtblk, hps, _split_mixer_params(pvals))

        _, vjp = jax.vjp(fn, us, vs, zs, xbcs, halos, dtblk, hps, pvals)
        da = [dab_ref[:, j * CH:(j + 1) * CH].astype(f32) for j in range(N_BLK)]
        db = [dab_ref[:, D_MODEL + j * CH: D_MODEL + (j + 1) * CH].astype(f32) for j in range(N_BLK)]
        dh = [dh_ref[j * CH:(j + 1) * CH, :] for j in range(N_BLK)]
        dus, dvs, dzs, dxbcs, dhalos, ddt, dhps, dp = vjp((da, db, dh))

        for j in range(N_BLK):
            dproj_ref[:, OFF_U + j * CH: OFF_U + (j + 1) * CH] = dus[j].astype(bf16)
            dproj_ref[:, OFF_V + j * CH: OFF_V + (j + 1) * CH] = dvs[j].astype(bf16)
            dproj_ref[:, OFF_Z + j * CH: OFF_Z + (j + 1) * CH] = dzs[j].astype(bf16)
            dh_ref[j * CH:(j + 1) * CH, :] = dhps[j]
        zeros_top = jnp.zeros((CH - 8, CH), f32)
        for j in range(XBC_BLKS):
            late = jnp.concatenate([zeros_top, dhalo_ref[:, j * CH:(j + 1) * CH]], axis=0)
            dproj_ref[:, OFF_X + j * CH: OFF_X + (j + 1) * CH] = (dxbcs[j] + late).astype(bf16)
        for j in range(XBC_BLKS):
            dhalo_ref[:, j * CH:(j + 1) * CH] = dhalos[j] * keep
        lane = lax.broadcasted_iota(jnp.int32, (CH, CH), 1)
        dproj_ref[:, OFF_DT: OFF_DT + CH] = jnp.where(lane < SSM_HEADS, ddt, 0.0).astype(bf16)
        dproj_ref[:, OFF_DT + CH:] = jnp.zeros((CH, NP_IN - OFF_DT - CH), bf16)
        for n, r in zip(names, g_refs):
            r[...] += dp[n]

    def const(shape):
        return pl.BlockSpec(shape, lambda i: (0,) * len(shape))

    outs = pl.pallas_call(
        body, name="mixer_bwd", grid=(nc,),
        in_specs=[pl.BlockSpec((CH, NP_IN), lambda i: (nc - 1 - i, 0)),
                  pl.BlockSpec((8, NP_IN), lambda i: (jnp.maximum((nc - 1 - i) * (CH // 8) - 1, 0), 0)),
                  pl.BlockSpec((None, D_MODEL, CH), lambda i: (nc - 1 - i, 0, 0)),
                  pl.BlockSpec((CH, 2 * D_MODEL), lambda i: (nc - 1 - i, 0))]
                 + [const(shp) for _, shp in _MIXER_PARAM_SHAPES],
        out_specs=[pl.BlockSpec((CH, NP_IN), lambda i: (nc - 1 - i, 0))]
                  + [const(shp) for _, shp in _MIXER_PARAM_SHAPES],
        out_shape=[jax.ShapeDtypeStruct((s, NP_IN), bf16)]
                  + [jax.ShapeDtypeStruct(shp, f32) for _, shp in _MIXER_PARAM_SHAPES],
        scratch_shapes=[pltpu.VMEM((D_MODEL, CH), f32), pltpu.VMEM((8, 2048), f32)],
        compiler_params=_cparams(("arbitrary",)),
    )(proj, proj, hstates, dab, *[prm[n] for n in names])
    return outs[0], dict(zip(names, outs[1:]))


_K_BLK = D_MODEL // CH
_V_BLK = _K_BLK + 1


def _attn_specs(rev, nb):
    def blk(i):
        return nb - 1 - i if rev else i

    q_spec = pl.BlockSpec((CH, D_MODEL), lambda i: (blk(i), 0))
    kv = lambda col, prev: pl.BlockSpec(
        (CH, CH), lambda i: (jnp.maximum(blk(i) - 1, 0) if prev else blk(i), col))
    return q_spec, [kv(_K_BLK, True), kv(_K_BLK, False), kv(_V_BLK, True), kv(_V_BLK, False)]


def attn_fwd(qkv, sink_row):
    s = qkv.shape[0]
    nb = s // CH

    def body(q_ref, kp_ref, kc_ref, vp_ref, vc_ref, sink_ref, o_ref):
        qps = [q_ref[:, p * CH:(p + 1) * CH] for p in range(N_BLK)]
        outs = _attn_block(qps, kp_ref[...], kc_ref[...], vp_ref[...], vc_ref[...], sink_ref[...],
                           pl.program_id(0) == 0)
        for p in range(N_BLK):
            o_ref[:, p * CH:(p + 1) * CH] = outs[p].astype(bf16)

    q_spec, kv_specs = _attn_specs(False, nb)
    return pl.pallas_call(
        body, name="attn_fwd", grid=(nb,),
        in_specs=[q_spec] + kv_specs + [pl.BlockSpec((1, CH), lambda i: (0, 0))],
        out_specs=pl.BlockSpec((CH, D_MODEL), lambda i: (i, 0)),
        out_shape=jax.ShapeDtypeStruct((s, D_MODEL), bf16),
        compiler_params=_cparams(("parallel",)),
    )(qkv, qkv, qkv, qkv, qkv, sink_row)


def attn_bwd(qkv, sink_row, dout):
    s = qkv.shape[0]
    nb = s // CH

    def body(q_ref, kp_ref, kc_ref, vp_ref, vc_ref, sink_ref, do_ref, dqkv_ref, dsink_ref, db_ref, carry_ref):
        i = pl.program_id(0)
        blk = nb - 1 - i

        @pl.when(i == 0)
        def _():
            dsink_ref[...] = jnp.zeros_like(dsink_ref)
            db_ref[...] = jnp.zeros_like(db_ref)
            carry_ref[...] = jnp.zeros_like(carry_ref)

        qps = [q_ref[:, p * CH:(p + 1) * CH] for p in range(N_BLK)]
        first = blk == 0
        _, vjp = jax.vjp(lambda *a: _attn_block(*a, first), qps, kp_ref[...], kc_ref[...], vp_ref[...],
                         vc_ref[...], sink_ref[...])
        dos = [do_ref[:, p * CH:(p + 1) * CH].astype(f32) for p in range(N_BLK)]
        dqs, dkp, dkc, dvp, dvc, dsink = vjp(dos)
        blocks = list(dqs) + [dkc + carry_ref[0], dvc + carry_ref[1]]
        for p, val in enumerate(blocks):
            dqkv_ref[:, p * CH:(p + 1) * CH] = val.astype(bf16)
            db_ref[:, p * CH:(p + 1) * CH] += jnp.sum(val, axis=0, keepdims=True)
        keep = jnp.logical_not(first).astype(f32)
        carry_ref[0] = dkp * keep
        carry_ref[1] = dvp * keep
        dsink_ref[...] += dsink

    q_spec, kv_specs = _attn_specs(True, nb)
    return pl.pallas_call(
        body, name="attn_bwd", grid=(nb,),
        in_specs=[q_spec] + kv_specs + [pl.BlockSpec((1, CH), lambda i: (0, 0)),
                                        pl.BlockSpec((CH, D_MODEL), lambda i: (nb - 1 - i, 0))],
        out_specs=[pl.BlockSpec((CH, QKV_DIM), lambda i: (nb - 1 - i, 0)), pl.BlockSpec((1, CH), lambda i: (0, 0)),
                   pl.BlockSpec((1, QKV_DIM), lambda i: (0, 0))],
        out_shape=[jax.ShapeDtypeStruct((s, QKV_DIM), bf16), jax.ShapeDtypeStruct((1, CH), f32),
                   jax.ShapeDtypeStruct((1, QKV_DIM), f32)],
        scratch_shapes=[pltpu.VMEM((2, CH, CH), f32)],
        compiler_params=_cparams(("arbitrary",)),
    )(qkv, qkv, qkv, qkv, qkv, sink_row, dout)


def adamw(w, g, m, v, name):
    def body(w_ref, g_ref, m_ref, v_ref, d_ref, nm_ref, nv_ref):
        gv = g_ref[...]
        nm = ADAM_B1 * m_ref[...] + (1.0 - ADAM_B1) * gv
        nv = ADAM_B2 * v_ref[...] + (1.0 - ADAM_B2) * jnp.square(gv)
        m_hat = nm / (1.0 - ADAM_B1 ** ADAM_STEP)
        v_hat = nv / (1.0 - ADAM_B2 ** ADAM_STEP)
        d_ref[...] = -ADAM_LR * (m_hat / (jnp.sqrt(v_hat) + ADAM_EPS) + ADAM_WD * w_ref[...])
        nm_ref[...] = nm
        nv_ref[...] = nv

    out_shape = [jax.ShapeDtypeStruct(w.shape, f32)] * 3
    if w.ndim == 3 and w.shape[1] == 1:
        tr = max(t for t in range(1, 129) if w.shape[0] % t == 0)
        tile = pl.BlockSpec((tr, 1, w.shape[2]), lambda i: (i, 0, 0))
        return pl.pallas_call(
            body, name=name, grid=(w.shape[0] // tr,),
            in_specs=[tile] * 4, out_specs=[tile] * 3, out_shape=out_shape,
            compiler_params=_cparams(("parallel",)),
        )(w, g, m, v)
    if w.ndim == 3 and w.shape[1] % 256 == 0:
        tile = pl.BlockSpec((None, 256, w.shape[2]), lambda l, i: (l, i, 0))
        return pl.pallas_call(
            body, name=name, grid=(w.shape[0], w.shape[1] // 256),
            in_specs=[tile] * 4, out_specs=[tile] * 3, out_shape=out_shape,
            compiler_params=_cparams(("parallel", "parallel")),
        )(w, g, m, v)
    return pl.pallas_call(body, name=name, in_specs=[_VMEM] * 4, out_specs=[_VMEM] * 3, out_shape=out_shape,
                          compiler_params=_cparams())(w, g, m, v)


_MESH = pl.DeviceIdType.MESH
_ANY = pl.BlockSpec(memory_space=pl.ANY)
_VMEM = pl.BlockSpec(memory_space=pltpu.VMEM)


def _place():
    x, y, c = lax.axis_index("x"), lax.axis_index("y"), lax.axis_index("c")
    chips = [(1 - x, y), (x, 1 - y), (1 - x, 1 - y)]
    return x, y, c, 2 * x + y, chips, [2 * cx + cy for cx, cy in chips]


def _half(c, rows):
    return pl.ds(pl.multiple_of(c * (rows // 2), 16), rows // 2)


def _step_rows(rows):
    return max(t for t in range(16, 641, 16) if rows % t == 0)


def place_shard(b, slot, name, dtype=bf16, after=None):
    r, c = b.shape
    tr = _step_rows(r)

    def body(slot_ref, b_ref, *rest):
        rest[-1][...] = b_ref[...].astype(dtype)

    order_specs = [] if after is None else [pl.BlockSpec((8, 128), lambda i, s: (0, 0))]
    return pl.pallas_call(
        body, name=name,
        grid_spec=pltpu.PrefetchScalarGridSpec(
            num_scalar_prefetch=1, grid=(r // tr,),
            in_specs=[pl.BlockSpec((tr, c), lambda i, s: (i, 0))] + order_specs,
            out_specs=pl.BlockSpec((None, tr, c), lambda i, s: (s[0], i, 0))),
        out_shape=jax.ShapeDtypeStruct((N_CHIPS, r, c), dtype),
        compiler_params=_cparams(("parallel",)),
    )(slot, b, *([] if after is None else [after]))


_HBM = pl.BlockSpec(memory_space=pltpu.HBM)
_SEM = pl.BlockSpec(memory_space=pltpu.SEMAPHORE)
_EFFECT = pltpu.SideEffectType.DATAFLOW_SIDE_EFFECTING


def _gather_ici_copies(bufs, send_sems, recv_sems):
    x, y, c, me, chips, chip_idx = _place()
    return [pltpu.make_async_remote_copy(
        src_ref=buf.at[me, _half(c, buf.shape[1])], dst_ref=buf.at[chip_idx[j], _half(c, buf.shape[1])],
        send_sem=send_sems.at[3 * k + j], recv_sem=recv_sems.at[3 * k + j],
        device_id=(*chips[j], c), device_id_type=_MESH) for j in range(3) for k, buf in enumerate(bufs)]


def gather_start(groups, tag):
    sizes = [len(g) for g in groups]
    flat = [b for g in groups for b in g]
    n = len(flat)

    def body(*refs):
        bufs, sems = refs[:n], refs[n:n + 2 * len(groups)]
        refs[-1][...] = jnp.zeros_like(refs[-1])
        x, y, c, me, chips, chip_idx = _place()
        lo = 0
        for gi, size in enumerate(sizes):
            for j in range(3):
                for k, buf in enumerate(bufs[lo:lo + size]):
                    mine = buf.at[me, _half(c, buf.shape[1])]
                    pltpu.make_async_remote_copy(
                        src_ref=mine, dst_ref=mine, send_sem=sems[2 * gi].at[3 * k + j],
                        recv_sem=sems[2 * gi + 1].at[3 * k + j], device_id=(*chips[j], c),
                        device_id_type=_MESH).start()
            lo += size

    sem_shapes = [pltpu.SemaphoreType.DMA((3 * size,)) for size in sizes for _ in range(2)]
    outs = pl.pallas_call(
        body, name=f"gather_start_{tag}",
        out_shape=(*sem_shapes, *[pltpu.HBM(b.shape, b.dtype) for b in flat], jax.ShapeDtypeStruct((8, 128), f32)),
        in_specs=[_HBM] * n, out_specs=(*[_SEM] * len(sem_shapes), *[_HBM] * n, _VMEM),
        input_output_aliases={i: len(sem_shapes) + i for i in range(n)},
        compiler_params=pltpu.CompilerParams(has_side_effects=_EFFECT),
    )(*[pltpu.with_memory_space_constraint(b, pltpu.HBM) for b in flat])
    sems = [(outs[2 * gi], outs[2 * gi + 1]) for gi in range(len(groups))]
    thru, lo = [], len(sem_shapes)
    for size in sizes:
        thru.append(list(outs[lo:lo + size]))
        lo += size
    return sems, thru, outs[-1]


def gather_wait(bufs, sems, after, tag):
    n = len(bufs)

    def body(*refs):
        for cp in _gather_ici_copies(refs[:n], refs[n], refs[n + 1]):
            cp.wait_send()
            cp.wait_recv()

    extra = list(after)
    return list(pl.pallas_call(
        body, name=f"gather_wait_{tag}",
        out_shape=[pltpu.HBM(b.shape, b.dtype) for b in bufs],
        in_specs=[_HBM] * n + [_SEM, _SEM] + [_ANY] * len(extra), out_specs=[_HBM] * n,
        input_output_aliases={i: i for i in range(n)},
        compiler_params=pltpu.CompilerParams(has_side_effects=_EFFECT),
    )(*bufs, *sems, *extra))


def gather_forward(bufs, tag):
    n = len(bufs)

    def body(*refs):
        out_refs = refs[n:2 * n]
        send_sems, recv_sems = refs[2 * n:]
        x, y, c, me, chips, chip_idx = _place()

        def copy(k, j, half):
            part = out_refs[k].at[chip_idx[j], _half(half, out_refs[k].shape[1])]
            return pltpu.make_async_remote_copy(
                src_ref=part, dst_ref=part, send_sem=send_sems.at[3 * k + j], recv_sem=recv_sems.at[3 * k + j],
                device_id=(x, y, 1 - c), device_id_type=_MESH)

        sends = [copy(k, j, c) for j in range(3) for k in range(n)]
        for cp in sends:
            cp.start()
        for j in range(3):
            for k in range(n):
                copy(k, j, 1 - c).wait_recv()
        for cp in sends:
            cp.wait_send()

    return list(pl.pallas_call(
        body, name=f"gather_forward_{tag}",
        out_shape=[jax.ShapeDtypeStruct(b.shape, b.dtype) for b in bufs],
        in_specs=[_ANY] * n, out_specs=[_ANY] * n, input_output_aliases={i: i for i in range(n)},
        scratch_shapes=[pltpu.SemaphoreType.DMA((3 * n,)), pltpu.SemaphoreType.DMA((3 * n,))],
    )(*bufs))


def exchange_halves(bufs, tag):
    n = len(bufs)

    def body(*refs):
        g_refs, out_refs = refs[:n], refs[n:2 * n]
        send_sems, recv_sems = refs[2 * n:]
        x, y, c, *_ = _place()
        cps = [pltpu.make_async_remote_copy(
            src_ref=g_refs[b].at[:, _half(1 - c, g_refs[b].shape[1])], dst_ref=out_refs[b],
            send_sem=send_sems.at[b], recv_sem=recv_sems.at[b], device_id=(x, y, 1 - c), device_id_type=_MESH)
            for b in range(n)]
        for cp in cps:
            cp.start()
        for cp in cps:
            cp.wait()

    return pl.pallas_call(
        body, name=f"exchange_halves_{tag}",
        out_shape=[jax.ShapeDtypeStruct((N_CHIPS, b.shape[1] // 2, b.shape[2]), b.dtype) for b in bufs],
        in_specs=[_ANY] * n, out_specs=[_ANY] * n,
        scratch_shapes=[pltpu.SemaphoreType.DMA((n,)), pltpu.SemaphoreType.DMA((n,))],
    )(*bufs)


def add_halves(g, got, c_idx, name):
    hr, cols = got.shape[1], got.shape[2]
    tr = _step_rows(hr)
    steps = hr // tr

    def body(c_ref, g_ref, got_ref, o_ref):
        o_ref[...] = (g_ref[...].astype(f32) + got_ref[...].astype(f32)).astype(bf16)

    return pl.pallas_call(
        body, name=name,
        grid_spec=pltpu.PrefetchScalarGridSpec(
            num_scalar_prefetch=1, grid=(N_CHIPS, steps),
            in_specs=[pl.BlockSpec((None, tr, cols), lambda s, i, c: (s, c[0] * steps + i, 0)),
                      pl.BlockSpec((None, tr, cols), lambda s, i, c: (s, i, 0))],
            out_specs=pl.BlockSpec((None, tr, cols), lambda s, i, c: (s, i, 0))),
        out_shape=jax.ShapeDtypeStruct(got.shape, bf16),
        compiler_params=_cparams(("parallel", "parallel")),
    )(c_idx, g, got)


def sum_chips(t, got, place_idx, name):
    hr, cols = t.shape[1], t.shape[2]
    tr = _step_rows(hr)
    steps = hr // tr

    def body(idx_ref, t_ref, got_ref, o_ref):
        acc = t_ref[...].astype(f32)
        for j in range(3):
            acc = acc + got_ref[j].astype(f32)
        o_ref[...] = acc

    return pl.pallas_call(
        body, name=name,
        grid_spec=pltpu.PrefetchScalarGridSpec(
            num_scalar_prefetch=1, grid=(steps,),
            in_specs=[pl.BlockSpec((None, tr, cols), lambda i, idx: (idx[0], i, 0)),
                      pl.BlockSpec((3, tr, cols), lambda i, idx: (0, i, 0))],
            out_specs=pl.BlockSpec((tr, cols), lambda i, idx: (idx[1] * steps + i, 0))),
        out_shape=jax.ShapeDtypeStruct((2 * hr, cols), f32),
        compiler_params=_cparams(("parallel",)),
    )(place_idx, t, got)


def _share_copies(refs, send_sems, recv_sems):
    x, y, c, *_ = _place()
    return [pltpu.make_async_remote_copy(
        src_ref=ref.at[_half(c, ref.shape[0])], dst_ref=ref.at[_half(c, ref.shape[0])], send_sem=send_sems.at[b],
        recv_sem=recv_sems.at[b], device_id=(x, y, 1 - c), device_id_type=_MESH) for b, ref in enumerate(refs)]


def share_start(bufs, tag):
    n = len(bufs)

    def body(*refs):
        for cp in _share_copies(refs[:n], refs[n], refs[n + 1]):
            cp.start()
        token = refs[-1]
        token[...] = jnp.zeros_like(token)

    outs = pl.pallas_call(
        body, name=f"share_start_{tag}",
        out_shape=(pltpu.SemaphoreType.DMA((n,)), pltpu.SemaphoreType.DMA((n,)),
                   *[pltpu.HBM(b.shape, b.dtype) for b in bufs], jax.ShapeDtypeStruct((8, 128), f32)),
        in_specs=[_HBM] * n, out_specs=(_SEM, _SEM, *[_HBM] * n, _VMEM),
        input_output_aliases={i: 2 + i for i in range(n)},
        compiler_params=pltpu.CompilerParams(has_side_effects=_EFFECT),
    )(*[pltpu.with_memory_space_constraint(b, pltpu.HBM) for b in bufs])
    return (outs[0], outs[1], list(outs[2:2 + n])), outs[-1]


def share_wait(send_sems, recv_sems, bufs, after, tag):
    n = len(bufs)

    def body(*refs):
        x, y, c, *_ = _place()
        for b, ref in enumerate(refs[:n]):
            cp = pltpu.make_async_remote_copy(
                src_ref=ref.at[_half(c, ref.shape[0])], dst_ref=ref.at[_half(1 - c, ref.shape[0])],
                send_sem=refs[n].at[b], recv_sem=refs[n + 1].at[b], device_id=(x, y, 1 - c), device_id_type=_MESH)
            cp.wait_send()
            cp.wait_recv()

    return list(pl.pallas_call(
        body, name=f"share_wait_{tag}",
        out_shape=[pltpu.HBM(b.shape, b.dtype) for b in bufs],
        in_specs=[_HBM] * n + [_SEM, _SEM, _ANY], out_specs=[_HBM] * n,
        input_output_aliases={i: i for i in range(n)},
        compiler_params=pltpu.CompilerParams(has_side_effects=_EFFECT),
    )(*bufs, send_sems, recv_sems, after))


def _scatter_copies(t_refs, land_refs, send_sems, recv_sems):
    x, y, c, me, chips, chip_idx = _place()
    return [pltpu.make_async_remote_copy(
        src_ref=t_refs[b].at[chip_idx[j]], dst_ref=land_refs[b].at[j], send_sem=send_sems.at[3 * b + j],
        recv_sem=recv_sems.at[3 * b + j], device_id=(*chips[j], c), device_id_type=_MESH)
        for j in range(3) for b in range(len(t_refs))]


def scatter_start(ts, tag):
    n = len(ts)
    lands = [lax.empty((3,) + t.shape[1:], t.dtype) for t in ts]

    def body(*refs):
        for cp in _scatter_copies(refs[:n], refs[n:2 * n], refs[2 * n], refs[2 * n + 1]):
            cp.start()
        token = refs[-1]
        token[...] = jnp.zeros_like(token)

    hbm = [pltpu.HBM(a.shape, a.dtype) for a in (*ts, *lands)]
    outs = pl.pallas_call(
        body, name=f"scatter_start_{tag}",
        out_shape=(pltpu.SemaphoreType.DMA((3 * n,)), pltpu.SemaphoreType.DMA((3 * n,)), *hbm,
                   jax.ShapeDtypeStruct((8, 128), f32)),
        in_specs=[_HBM] * (2 * n), out_specs=(_SEM, _SEM, *[_HBM] * (2 * n), _VMEM),
        input_output_aliases={i: 2 + i for i in range(2 * n)},
        compiler_params=pltpu.CompilerParams(has_side_effects=_EFFECT),
    )(*[pltpu.with_memory_space_constraint(a, pltpu.HBM) for a in (*ts, *lands)])
    return outs[0], outs[1], list(outs[2:2 + n]), list(outs[2 + n:2 + 2 * n]), outs[-1]


def scatter_wait(send_sems, recv_sems, ts, lands, after, tag):
    n = len(ts)

    def body(*refs):
        for cp in _scatter_copies(refs[:n], refs[n:2 * n], refs[2 * n], refs[2 * n + 1]):
            cp.wait_send()
            cp.wait_recv()

    outs = pl.pallas_call(
        body, name=f"scatter_wait_{tag}",
        out_shape=[pltpu.HBM(a.shape, a.dtype) for a in (*ts, *lands)],
        in_specs=[_HBM] * (2 * n) + [_SEM, _SEM, _ANY], out_specs=[_HBM] * (2 * n),
        input_output_aliases={i: i for i in range(2 * n)},
        compiler_params=pltpu.CompilerParams(has_side_effects=_EFFECT),
    )(*ts, *lands, send_sems, recv_sems, after)
    return list(outs[:n]), list(outs[n:])


N_SENDERS = 7


def _direct_copies(g_refs, land_refs, send_sems, recv_sems):
    x, y, c, me, chips, chip_idx = _place()
    cps = []
    for b, (g, land) in enumerate(zip(g_refs, land_refs)):
        rows, base = g.shape[1], N_SENDERS * b
        cps.append(pltpu.make_async_remote_copy(
            src_ref=g.at[me, _half(1 - c, rows)], dst_ref=land.at[0], send_sem=send_sems.at[base],
            recv_sem=recv_sems.at[base], device_id=(x, y, 1 - c), device_id_type=_MESH))
        for j in range(3):
            for core in range(2):
                cps.append(pltpu.make_async_remote_copy(
                    src_ref=g.at[chip_idx[j], _half(core, rows)], dst_ref=land.at[1 + 2 * j + c],
                    send_sem=send_sems.at[base + 1 + 2 * j + core], recv_sem=recv_sems.at[base + 1 + 2 * j + c],
                    device_id=(*chips[j], core), device_id_type=_MESH))
    return cps


def direct_start(gs, tag):
    n = len(gs)
    lands = [lax.empty((N_SENDERS, g.shape[1] // 2, g.shape[2]), g.dtype) for g in gs]

    def body(*refs):
        for cp in _direct_copies(refs[:n], refs[n:2 * n], refs[2 * n], refs[2 * n + 1]):
            cp.start()
        token = refs[-1]
        token[...] = jnp.zeros_like(token)

    hbm = [pltpu.HBM(a.shape, a.dtype) for a in (*gs, *lands)]
    outs = pl.pallas_call(
        body, name=f"direct_start_{tag}",
        out_shape=(pltpu.SemaphoreType.DMA((N_SENDERS * n,)), pltpu.SemaphoreType.DMA((N_SENDERS * n,)), *hbm,
                   jax.ShapeDtypeStruct((8, 128), f32)),
        in_specs=[_HBM] * (2 * n), out_specs=(_SEM, _SEM, *[_HBM] * (2 * n), _VMEM),
        input_output_aliases={i: 2 + i for i in range(2 * n)},
        compiler_params=pltpu.CompilerParams(has_side_effects=_EFFECT),
    )(*[pltpu.with_memory_space_constraint(a, pltpu.HBM) for a in (*gs, *lands)])
    return outs[0], outs[1], list(outs[2:2 + n]), list(outs[2 + n:2 + 2 * n]), outs[-1]


def direct_wait(send_sems, recv_sems, gs, lands, after, tag):
    n = len(gs)

    def body(*refs):
        g_refs, land_refs, sends, recvs = refs[:n], refs[n:2 * n], refs[2 * n], refs[2 * n + 1]
        for b in range(n):
            for k in range(N_SENDERS):
                cp = pltpu.make_async_remote_copy(
                    src_ref=g_refs[b].at[0, _half(0, g_refs[b].shape[1])], dst_ref=land_refs[b].at[k],
                    send_sem=sends.at[N_SENDERS * b + k], recv_sem=recvs.at[N_SENDERS * b + k],
                    device_id=_place()[:3], device_id_type=_MESH)
                cp.wait_send()
                cp.wait_recv()

    outs = pl.pallas_call(
        body, name=f"direct_wait_{tag}",
        out_shape=[pltpu.HBM(a.shape, a.dtype) for a in (*gs, *lands)],
        in_specs=[_HBM] * (2 * n) + [_SEM, _SEM, _ANY], out_specs=[_HBM] * (2 * n),
        input_output_aliases={i: i for i in range(2 * n)},
        compiler_params=pltpu.CompilerParams(has_side_effects=_EFFECT),
    )(*gs, *lands, send_sems, recv_sems, after)
    return list(outs[:n]), list(outs[n:])


def sum_senders(g, lands, place_idx, name):
    hr, cols = lands.shape[1], lands.shape[2]
    tr = _step_rows(hr)
    steps = hr // tr

    def body(idx_ref, g_ref, land_ref, o_ref):
        acc = g_ref[...].astype(f32)
        for k in range(N_SENDERS):
            acc = acc + land_ref[k].astype(f32)
        o_ref[...] = acc

    return pl.pallas_call(
        body, name=name,
        grid_spec=pltpu.PrefetchScalarGridSpec(
            num_scalar_prefetch=1, grid=(steps,),
            in_specs=[pl.BlockSpec((None, tr, cols), lambda i, idx: (idx[0], idx[1] * steps + i, 0)),
                      pl.BlockSpec((N_SENDERS, tr, cols), lambda i, idx: (0, i, 0))],
            out_specs=pl.BlockSpec((tr, cols), lambda i, idx: (idx[1] * steps + i, 0))),
        out_shape=jax.ShapeDtypeStruct((2 * hr, cols), f32),
        compiler_params=_cparams(("parallel",)),
    )(place_idx, g, lands)


class GradReducer:
    def __init__(self, c_idx, place_idx):
        self.c_idx, self.place_idx = c_idx, place_idx

    def start(self, bufs, tag, direct=False):
        if direct:
            send_sems, recv_sems, gs, lands, token = direct_start(bufs, tag)
            return (True, send_sems, recv_sems, gs, lands), token
        got = exchange_halves(bufs, tag)
        ts = [add_halves(b, g, self.c_idx, f"add_halves_{tag}{i}") for i, (b, g) in enumerate(zip(bufs, got))]
        send_sems, recv_sems, ts, lands, token = scatter_start(ts, tag)
        return (False, send_sems, recv_sems, ts, lands), token

    def finish(self, state, after, tag):
        direct, *flight = state
        if direct:
            gs, lands = direct_wait(*flight, after, tag)
            sums = [sum_senders(g, l, self.place_idx, f"sum_senders_{tag}{i}") for i, (g, l) in enumerate(zip(gs, lands))]
        else:
            ts, lands = scatter_wait(*flight, after, tag)
            sums = [sum_chips(t, l, self.place_idx, f"sum_chips_{tag}{i}") for i, (t, l) in enumerate(zip(ts, lands))]
        return share_start(sums, tag)

    def collect(self, pending, after, tag):
        return share_wait(*pending, after, tag)


def allreduce_small(sp):
    rows = sp.shape[0]
    hr = rows // 2

    def body(s_ref, out_ref, sib_ref, chip_ref, four_ref, send_sems, recv_sems):
        x, y, c, me, chips, chip_idx = _place()
        sibling = (x, y, 1 - c)
        mine = pl.ds(pl.multiple_of(c * hr, 8), hr)
        other = pl.ds(pl.multiple_of((1 - c) * hr, 8), hr)

        swap = pltpu.make_async_remote_copy(src_ref=s_ref, dst_ref=sib_ref, send_sem=send_sems.at[0],
                                            recv_sem=recv_sems.at[0], device_id=sibling, device_id_type=_MESH)
        swap.start()
        swap.wait()
        is_core0 = c == 0
        chip_ref[...] = jnp.where(is_core0, s_ref[...], sib_ref[...]) + jnp.where(is_core0, sib_ref[...], s_ref[...])

        sends = [pltpu.make_async_remote_copy(
            src_ref=chip_ref.at[mine], dst_ref=four_ref.at[me], send_sem=send_sems.at[1 + j],
            recv_sem=recv_sems.at[1 + j], device_id=(*chips[j], c), device_id_type=_MESH) for j in range(3)]
        for cp in sends:
            cp.start()
        four_ref[me] = chip_ref[mine, :]
        for j in range(3):
            pltpu.make_async_remote_copy(
                src_ref=chip_ref.at[mine], dst_ref=four_ref.at[chip_idx[j]], send_sem=send_sems.at[1 + j],
                recv_sem=recv_sems.at[1 + j], device_id=(*chips[j], c), device_id_type=_MESH).wait_recv()
        for cp in sends:
            cp.wait_send()
        out_ref[mine, :] = (four_ref[0] + four_ref[1]) + (four_ref[2] + four_ref[3])

        share = pltpu.make_async_remote_copy(src_ref=out_ref.at[mine], dst_ref=out_ref.at[mine], send_sem=send_sems.at[4],
                                             recv_sem=recv_sems.at[4], device_id=sibling, device_id_type=_MESH)
        share.start()
        pltpu.make_async_remote_copy(src_ref=out_ref.at[mine], dst_ref=out_ref.at[other], send_sem=send_sems.at[4],
                                     recv_sem=recv_sems.at[4], device_id=sibling, device_id_type=_MESH).wait_recv()
        share.wait_send()

    return pl.pallas_call(
        body, name="allreduce_small",
        out_shape=jax.ShapeDtypeStruct(sp.shape, sp.dtype),
        in_specs=[_VMEM], out_specs=_VMEM,
        scratch_shapes=[pltpu.VMEM(sp.shape, sp.dtype), pltpu.VMEM(sp.shape, sp.dtype),
                        pltpu.VMEM((N_CHIPS, hr, sp.shape[1]), sp.dtype),
                        pltpu.SemaphoreType.DMA((5,)), pltpu.SemaphoreType.DMA((5,))],
        compiler_params=_cparams(),
    )(sp)


def _n_rows(shape):
    n = 1
    for d in shape:
        n *= d
    return 8 * (-(-n // 8192))


def _pack(arrays, total_rows):
    parts = []
    for a in arrays:
        flat = a.reshape(-1)
        parts.append(jnp.pad(flat, (0, 1024 * _n_rows(a.shape) - flat.shape[0])).reshape(-1, 1024))
    rows = jnp.concatenate(parts, axis=0)
    return jnp.pad(rows, ((0, total_rows - rows.shape[0]), (0, 0)))


def _unpack(packed, shapes):
    out, r = [], 0
    for shp in shapes:
        n = 1
        for d in shp:
            n *= d
        nr = _n_rows(shp)
        out.append(packed[r:r + nr].reshape(-1)[:n].reshape(shp))
        r += nr
    return out


_COLUMN_SHARDED = ("w_in_even", "w_qkv")
IN_SHARD, IN_PAD = 1284, 1408
QKV_SHARD, QKV_PAD = 320, 384


def _lane_padded(a, cols):
    return jnp.pad(a, ((0, 0), (0, cols - a.shape[1])))


_SMALL_SHAPES = (
    ("norm_mix_g", (2, 1024)), ("norm_mlp_g", (2, 1024)), ("final_norm_g", (1024,)), ("gm_ln_g", (1, 1024)),
    ("gm_ln_b", (1, 1024)), ("gm_w_s", (1, 8, 128, 128)), ("gm_b_s", (1, 8, 128)), ("ssm_conv_b", (1, 2048)),
    ("ssm_dt_bias", (1, 16)), ("ssm_a_log", (1, 16)), ("ssm_d", (1, 16)), ("ssm_norm_g", (1, 1024)),
    ("attn_sinks", (1, 16)), ("ssm_conv_w", (1, 4, 2048)), ("b_qkv", (1, 1280)), ("b_o", (1, 1024)),
)
_N_REPLICATED = 13
_SHARDED_SMALL = (("ssm_conv_w", 2, 512), ("b_qkv", 1, 320), ("b_o", 1, 256))
_SHARD_PACK_ROWS = 32


def _cols_by_owner(a):
    return a.transpose(1, 0, 2).reshape(a.shape[1], -1)


class WeightGatherer:
    def __init__(self, w, chip_idx):
        def place(tag, b, dtype=bf16, after=None):
            return place_shard(b, chip_idx, f"place_shard_{tag}", dtype, after)

        sems_in, bufs_in, self.started = gather_start([
            [place("in", _lane_padded(w["w_in_even"][0], IN_PAD)),
             place("small", _pack([w[n] for n, _, _ in _SHARDED_SMALL], _SHARD_PACK_ROWS), f32)]], "in")
        t = self.started
        sems, bufs, self.all_started = gather_start([
            [place("out", w["w_out_even"][0], after=t), place("up0", w["w_up"][0], after=t),
             place("down0", w["w_down"][0], after=t)],
            [place("qkv", _lane_padded(w["w_qkv"][0], QKV_PAD), after=t), place("o", w["w_o"][0], after=t),
             place("up1", w["w_up"][1], after=t), place("down1", w["w_down"][1], after=t)],
        ], "rest")
        self.sems, self.bufs = sems_in + sems, bufs_in + bufs

    def _group(self, gi, after, tag):
        return gather_forward(gather_wait(self.bufs[gi], self.sems[gi], after, tag), tag)

    def mixer_in(self, after):
        g, small = self._group(0, [after, self.all_started], "in")
        shard_shapes = [tuple(width if i == axis else d for i, d in enumerate(dict(_SMALL_SHAPES)[n]))
                        for n, axis, width in _SHARDED_SMALL]
        per_chip = [_unpack(small[s], shard_shapes) for s in range(N_CHIPS)]
        full = {n: jnp.concatenate([per_chip[s][i] for s in range(N_CHIPS)], axis=axis)
                for i, (n, axis, _) in enumerate(_SHARDED_SMALL)}
        w_in = jnp.concatenate([g[s, :, :IN_SHARD] for s in range(N_CHIPS)], axis=1)
        return _lane_padded(w_in, NP_IN), full

    def layer0(self, after):
        w_out, w_up, w_down = self._group(1, [after], "l0")
        return w_out.reshape(2048, 1024), w_up, w_down.reshape(4096, 1024)

    def layer1(self, after):
        q, w_o, w_up, w_down = self._group(2, [after], "l1")
        w_qkv = jnp.concatenate([q[s, :, :QKV_SHARD] for s in range(N_CHIPS)], axis=1)
        return w_qkv, w_o.reshape(1024, 1024), w_up, w_down.reshape(4096, 1024)


def _row2(v):
    return v.reshape(1, -1)


def _lane_pad(v):
    return jnp.pad(v, ((0, 0), (0, CH - v.shape[1])))


_H_AND_NORM = (("tile", f32), ("tile", bf16))
_DX_AND_DG = (("tile", f32), ("sum", D_MODEL))


def _mlp_bwd(dh_out, h, g_row, y, a, w_up, w_down, tag, after=None):
    da = matmul(dh_out, w_down, dims="nt", name=f"mlp_da{tag}", out_dtype=bf16, tn=1024,
                epi=_times_relu2_grad, epi_args=(("tile", a),), after=after)
    dw_down = matmul(a, dh_out, dims="tn", name=f"mlp_dwdown{tag}", out_dtype=bf16, a_pro=_relu2)
    dw_up = matmul(y, da, dims="tn", name=f"mlp_dwup{tag}", out_dtype=bf16, tn=1024, out_by_col_tile=True)
    dh, dg, dh_colsum = matmul_rows(da, w_up, dims="nt", name=f"mlp_dy{tag}", epi=_norm_bwd_res_colsum,
                                    epi_args=(("tile", h), ("row", g_row), ("tile", dh_out)),
                                    outs=_DX_AND_DG + (("sum", D_MODEL),))
    return dh, dg, dw_up, dw_down, dh_colsum


def _by_owner(a):
    return a.reshape(N_CHIPS, a.shape[0] // N_CHIPS, a.shape[1])


def _row_shards(a, shard, padded):
    return jnp.stack([jnp.pad(a[shard * s: shard * (s + 1)], ((0, padded - shard), (0, 0))) for s in range(N_CHIPS)])


def _col_shards(a, shard, padded):
    return jnp.stack([_lane_padded(a[:, shard * s: shard * (s + 1)], padded) for s in range(N_CHIPS)])


def _local_step(x, target, weights, sm, reducer):
    w_up, w_down = [None, None], [None, None]
    mix_g = [_row2(sm["norm_mix_g"][i]) for i in range(2)]
    y0 = rmsnorm_fwd(x, mix_g[0] + weights.started[:1, :1], "mix_norm0")
    w_in_p, sharded_small = weights.mixer_in(y0)
    sm = {**sm, **sharded_small}
    mlp_g = [_row2(sm["norm_mlp_g"][i]) for i in range(2)]
    mixer_prm = {
        "ln_g": sm["gm_ln_g"], "ln_b": sm["gm_ln_b"], "wm": sm["gm_w_s"][0],
        "bs_t": jnp.pad(sm["gm_b_s"][0].T, ((0, 0), (0, CH - N_BLK))),
        "conv_w": jnp.pad(sm["ssm_conv_w"][0], ((0, 4), (0, 0))), "conv_b": sm["ssm_conv_b"],
        "dt_bias": _lane_pad(sm["ssm_dt_bias"]), "a_log": _lane_pad(sm["ssm_a_log"]),
        "d_heads": _lane_pad(sm["ssm_d"]), "norm_g": sm["ssm_norm_g"],
    }
    sink_row = _lane_pad(sm["attn_sinks"])

    proj = matmul(y0, w_in_p, dims="nn", name="in_proj", tn=768)
    ab, hstates = mixer_fwd(proj, mixer_prm)
    w_out, w_up[0], w_down[0] = weights.layer0(ab)
    h1, y1 = matmul_rows(ab, w_out, dims="nn", name="out_proj", epi=_res_norm,
                         epi_args=(("tile", x), ("row", mlp_g[0])), outs=_H_AND_NORM)
    a1 = matmul(y1, w_up[0], dims="nn", name="mlp_up0", out_dtype=bf16, tn=1024)
    w_qkv, w_o, w_up[1], w_down[1] = weights.layer1(a1)
    h2, y2 = matmul_rows(a1, w_down[0], dims="nn", name="mlp_down0", a_pro=_relu2, epi=_res_norm,
                         epi_args=(("tile", h1), ("row", mix_g[1])), outs=_H_AND_NORM)
    qkv = matmul(y2, w_qkv, dims="nn", name="qkv_proj", tn=QKV_DIM, epi=_add_bias, epi_args=(("row", sm["b_qkv"]),))
    att = attn_fwd(qkv, sink_row)
    h3, y3 = matmul_rows(att, w_o, dims="nn", name="o_proj", epi=_bias_res_norm,
                         epi_args=(("row", sm["b_o"]), ("tile", h2), ("row", mlp_g[1])), outs=_H_AND_NORM)
    a3 = matmul(y3, w_up[1], dims="nn", name="mlp_up1", out_dtype=bf16, tn=1024)
    dh4, dg_final, loss = matmul_rows(
        a3, w_down[1], dims="nn", name="mlp_down1", a_pro=_relu2, epi=_res_norm_loss,
        epi_args=(("tile", h3), ("row", _row2(sm["final_norm_g"])), ("tile", target)),
        outs=(("tile", f32), ("sum", D_MODEL), ("sum", 128)))

    dh3, dg_mlp1, dw_up1, dw_down1, db_o = _mlp_bwd(dh4, h3, mlp_g[1], y3, a3, w_up[1], w_down[1], 1)
    datt = matmul(dh3, w_o, dims="nt", name="attn_dout", out_dtype=bf16)
    dw_o = matmul(att, dh3, dims="tn", name="dw_o", out_dtype=bf16)
    dqkv, dsink, db_qkv = attn_bwd(qkv, sink_row, datt)
    dw_qkv = matmul(y2, dqkv, dims="tn", name="dw_qkv", out_dtype=bf16, tn=QKV_DIM)
    dh2, dg_mix1 = matmul_rows(dqkv, w_qkv, dims="nt", name="dy_qkv", epi=_norm_bwd_res,
                               epi_args=(("tile", h2), ("row", mix_g[1]), ("tile", dh3)), outs=_DX_AND_DG)
    layer1 = [jnp.concatenate([_by_owner(dw_o), dw_up1, _by_owner(dw_down1)], axis=1),
              _col_shards(dw_qkv, QKV_SHARD, QKV_PAD)]
    flight1, token1 = reducer.start(layer1, "l1", direct=True)
    dh1, dg_mlp0, dw_up0, dw_down0, _ = _mlp_bwd(dh2, h1, mlp_g[0], y1, a1, w_up[0], w_down[0], 0, after=token1)
    pending1, shared1 = reducer.finish(flight1, dh1, "l1")
    dw_out = matmul(ab, dh1, dims="tn", name="dw_out", out_dtype=bf16, after=shared1)
    flight0, token0 = reducer.start(
        [jnp.concatenate([dw_up0, _by_owner(dw_down0), _by_owner(dw_out)], axis=1)], "l0", direct=True)
    dab = matmul(dh1, w_out, dims="nt", name="mixer_dout", tn=1024, after=token0)
    dproj, dmix = mixer_bwd(proj, hstates, dab, mixer_prm)
    dw_in_t = matmul(dproj, y0, dims="tn", name="dw_in", out_dtype=bf16, tm=768, tn=1024)
    pending0, shared0 = reducer.finish(flight0, dw_in_t, "l0")
    flight_in, token_in = reducer.start([_row_shards(dw_in_t, IN_SHARD, IN_PAD)], "in")
    dx, dg_mix0 = matmul_rows(dproj, w_in_p, dims="nt", name="dy_in", tm=256, epi=_norm_bwd_res,
                              epi_args=(("tile", x), ("row", mix_g[0]), ("tile", dh1)), outs=_DX_AND_DG,
                              after=token_in + shared0)
    pending_in, _ = reducer.finish(flight_in, dx, "in")
    r_l1, r_qkv = reducer.collect(pending1, dx, "l1")
    (r_l0,) = reducer.collect(pending0, dx, "l0")
    (r_in,) = reducer.collect(pending_in, dx, "in")
    reduced = {
        "w_out_even": r_l0[None, 2048:], "w_in_even": r_in[:IN_SHARD].T[None], "w_qkv": r_qkv[None, :, :QKV_SHARD],
        "w_o": r_l1[None, :256], "w_up": jnp.stack([r_l0[:1024], r_l1[256:1280]]),
        "w_down": jnp.stack([r_l0[1024:2048], r_l1[1280:]]),
    }

    small_grads = {
        "norm_mix_g": jnp.concatenate([dg_mix0, dg_mix1], axis=0),
        "norm_mlp_g": jnp.concatenate([dg_mlp0, dg_mlp1], axis=0),
        "final_norm_g": dg_final[0], "gm_ln_g": dmix["ln_g"], "gm_ln_b": dmix["ln_b"],
        "gm_w_s": dmix["wm"][None], "gm_b_s": dmix["bs_t"][:, :N_BLK].T[None],
        "ssm_conv_b": dmix["conv_b"], "ssm_dt_bias": dmix["dt_bias"][:, :SSM_HEADS],
        "ssm_a_log": dmix["a_log"][:, :SSM_HEADS], "ssm_d": dmix["d_heads"][:, :SSM_HEADS],
        "ssm_norm_g": dmix["norm_g"], "attn_sinks": dsink[:, :SSM_HEADS],
        "ssm_conv_w": dmix["conv_w"][None, :4], "b_qkv": db_qkv, "b_o": db_o,
    }
    return loss, dx, reduced, small_grads


def kernel(x, norm_mix_g, norm_mlp_g, final_norm_g, w_in_even, w_out_even, gm_ln_g, gm_ln_b, gm_w_s, gm_b_s, ssm_conv_w, ssm_conv_b, ssm_dt_bias, ssm_a_log, ssm_d, ssm_norm_g, w_qkv, b_qkv, w_o, b_o, attn_sinks, w_up, w_down, loss_target, m_norm_mix_g, m_norm_mlp_g, m_final_norm_g, m_w_in_even, m_w_out_even, m_gm_ln_g, m_gm_ln_b, m_gm_w_s, m_gm_b_s, m_ssm_conv_w, m_ssm_conv_b, m_ssm_dt_bias, m_ssm_a_log, m_ssm_d, m_ssm_norm_g, m_w_qkv, m_b_qkv, m_w_o, m_b_o, m_attn_sinks, m_w_up, m_w_down, v_norm_mix_g, v_norm_mlp_g, v_final_norm_g, v_w_in_even, v_w_out_even, v_gm_ln_g, v_gm_ln_b, v_gm_w_s, v_gm_b_s, v_ssm_conv_w, v_ssm_conv_b, v_ssm_dt_bias, v_ssm_a_log, v_ssm_d, v_ssm_norm_g, v_w_qkv, v_b_qkv, v_w_o, v_b_o, v_attn_sinks, v_w_up, v_w_down):
    w = dict(norm_mix_g=norm_mix_g, norm_mlp_g=norm_mlp_g, final_norm_g=final_norm_g, w_in_even=w_in_even,
             w_out_even=w_out_even, gm_ln_g=gm_ln_g, gm_ln_b=gm_ln_b, gm_w_s=gm_w_s, gm_b_s=gm_b_s,
             ssm_conv_w=ssm_conv_w, ssm_conv_b=ssm_conv_b, ssm_dt_bias=ssm_dt_bias, ssm_a_log=ssm_a_log,
             ssm_d=ssm_d, ssm_norm_g=ssm_norm_g, w_qkv=w_qkv, b_qkv=b_qkv, w_o=w_o, b_o=b_o,
             attn_sinks=attn_sinks, w_up=w_up, w_down=w_down)
    m = dict(norm_mix_g=m_norm_mix_g, norm_mlp_g=m_norm_mlp_g, final_norm_g=m_final_norm_g,
             w_in_even=m_w_in_even, w_out_even=m_w_out_even, gm_ln_g=m_gm_ln_g, gm_ln_b=m_gm_ln_b,
             gm_w_s=m_gm_w_s, gm_b_s=m_gm_b_s, ssm_conv_w=m_ssm_conv_w, ssm_conv_b=m_ssm_conv_b,
             ssm_dt_bias=m_ssm_dt_bias, ssm_a_log=m_ssm_a_log, ssm_d=m_ssm_d, ssm_norm_g=m_ssm_norm_g,
             w_qkv=m_w_qkv, b_qkv=m_b_qkv, w_o=m_w_o, b_o=m_b_o, attn_sinks=m_attn_sinks, w_up=m_w_up,
             w_down=m_w_down)
    v = dict(norm_mix_g=v_norm_mix_g, norm_mlp_g=v_norm_mlp_g, final_norm_g=v_final_norm_g,
             w_in_even=v_w_in_even, w_out_even=v_w_out_even, gm_ln_g=v_gm_ln_g, gm_ln_b=v_gm_ln_b,
             gm_w_s=v_gm_w_s, gm_b_s=v_gm_b_s, ssm_conv_w=v_ssm_conv_w, ssm_conv_b=v_ssm_conv_b,
             ssm_dt_bias=v_ssm_dt_bias, ssm_a_log=v_ssm_a_log, ssm_d=v_ssm_d, ssm_norm_g=v_ssm_norm_g,
             w_qkv=v_w_qkv, b_qkv=v_b_qkv, w_o=v_w_o, b_o=v_b_o, attn_sinks=v_attn_sinks, w_up=v_w_up,
             w_down=v_w_down)
    names = ("norm_mix_g", "norm_mlp_g", "final_norm_g", "w_in_even", "w_out_even", "gm_ln_g", "gm_ln_b",
             "gm_w_s", "gm_b_s", "ssm_conv_w", "ssm_conv_b", "ssm_dt_bias", "ssm_a_log", "ssm_d", "ssm_norm_g",
             "w_qkv", "b_qkv", "w_o", "b_o", "attn_sinks", "w_up", "w_down")

    cx, cy, cc = lax.axis_index("x"), lax.axis_index("y"), lax.axis_index("c")
    chip = 2 * cx + cy
    c_idx = jnp.reshape(cc, (1,)).astype(jnp.int32)
    chip_idx = jnp.reshape(chip, (1,)).astype(jnp.int32)

    weights = WeightGatherer(w, chip_idx)
    sm = {n: w[n] for n, _ in _SMALL_SHAPES[:_N_REPLICATED]}

    reducer = GradReducer(c_idx, jnp.concatenate([chip_idx, c_idx]))
    loss_part, dx, grads, small_grads = _local_step(x[0], loss_target[0], weights, sm, reducer)

    small_sum = allreduce_small(_pack([small_grads[n] for n, _ in _SMALL_SHAPES] + [loss_part], SMALL_ROWS))
    *small_list, loss_row = _unpack(small_sum, [s for _, s in _SMALL_SHAPES] + [loss_part.shape])
    loss = loss_row[0, 0]
    small_full = dict(zip([n for n, _ in _SMALL_SHAPES], small_list))
    for n, _ in _SMALL_SHAPES[:_N_REPLICATED]:
        grads[n] = small_full[n]
    for n, axis, width in _SHARDED_SMALL:
        grads[n] = lax.dynamic_slice_in_dim(small_full[n], chip * width, width, axis)
    grads = {n: grads[n].reshape(w[n].shape) for n in names}

    delta, new_m, new_v = {}, {}, {}
    for n in names:
        if n in _COLUMN_SHARDED:
            args = [jnp.transpose(d[n], (2, 0, 1)) for d in (w, grads, m, v)]
            grads[n] = jnp.transpose(args[1], (1, 2, 0))
            outs = adamw(*args, f"adamw_{n}")
            delta[n], new_m[n], new_v[n] = (jnp.transpose(o, (1, 2, 0)) for o in outs)
            continue
        shape = (1,) + w[n].shape if w[n].ndim == 1 else w[n].shape
        outs = adamw(*[d[n].reshape(shape) for d in (w, grads, m, v)], f"adamw_{n}")
        delta[n], new_m[n], new_v[n] = (o.reshape(w[n].shape) for o in outs)

    return (loss, dx[None], *[grads[n] for n in names], *[delta[n] for n in names],
            *[new_m[n] for n in names], *[new_v[n] for n in names])
```

```python
import functools

import jax
import jax.numpy as jnp
from jax import lax
from jax.experimental import pallas as pl
from jax.experimental.pallas import tpu as pltpu

f32 = jnp.float32
bf16 = jnp.bfloat16
MXU_DTYPE = bf16

RMS_EPS = 1e-5
LN_EPS = 1e-5
D_MODEL = 1024
D_FF = 4096
CH = 128
N_BLK = 8
SSM_HEADS = 16
IN_EVEN = 5136
NP_IN = 5376
OFF_U, OFF_V, OFF_Z, OFF_X, OFF_DT = 0, 1024, 2048, 3072, 5120
XBC_BLKS = 16
QKV_DIM = 1280
ATT_SCALE = 64 ** -0.5

ADAM_LR = 0.001
ADAM_B1 = 0.9
ADAM_B2 = 0.999
ADAM_EPS = 1e-08
ADAM_WD = 0.01
ADAM_STEP = 10

VMEM_LIMIT_BYTES = 48 * 1024 * 1024
N_CHIPS = 4
SMALL_ROWS = 256

NN = ((1,), (0,))
NT = ((1,), (1,))
TN = ((0,), (0,))


def _mm(a, b, dims):
    return lax.dot_general(a.astype(MXU_DTYPE), b.astype(MXU_DTYPE), (dims, ((), ())),
                           preferred_element_type=f32)


def _mm_exact(a, b):
    return jnp.dot(a, b, preferred_element_type=f32, precision=lax.Precision.HIGHEST)


def _cparams(sem=None):
    return pltpu.CompilerParams(dimension_semantics=sem, vmem_limit_bytes=VMEM_LIMIT_BYTES)


@jax.custom_vjp
def _swap64(x):
    return pltpu.roll(x, 64, axis=1)


_swap64.defvjp(lambda x: (pltpu.roll(x, 64, axis=1), None), lambda _, g: (pltpu.roll(g, 64, axis=1),))


def _row_blocks_of(x):
    return tuple(x[i:i + CH] for i in range(0, x.shape[0], CH))


@jax.custom_vjp
def _row_blocks(x):
    return _row_blocks_of(x)


_row_blocks.defvjp(lambda x: (_row_blocks_of(x), None), lambda _, gs: (jnp.concatenate(gs, axis=0),))


def _make_delay(k):
    @jax.custom_vjp
    def delay(ext):
        return pltpu.roll(ext, k, axis=0)[8:, :]

    def fwd(ext):
        return delay(ext), None

    def bwd(_, g):
        gp = jnp.concatenate([jnp.zeros((8, g.shape[1]), g.dtype), g], axis=0)
        return (pltpu.roll(gp, gp.shape[0] - k, axis=0),)

    delay.defvjp(fwd, bwd)
    return delay


_DELAYS = {k: _make_delay(k) for k in (1, 2, 3)}


_GELU_C = 0.7978845608028654
_GELU_K = 0.044715


@jax.custom_vjp
def _gelu(x):
    return 0.5 * x * (1.0 + jnp.tanh(_GELU_C * (x + _GELU_K * (x * x * x))))


def _gelu_fwd(x):
    t = jnp.tanh(_GELU_C * (x + _GELU_K * (x * x * x)))
    return 0.5 * x * (1.0 + t), (x, t)


def _gelu_bwd(res, g):
    x, t = res
    dz = _GELU_C + (3.0 * _GELU_C * _GELU_K) * (x * x)
    return (g * (0.5 * (1.0 + t) + (0.5 * x) * (1.0 - t * t) * dz),)


_gelu.defvjp(_gelu_fwd, _gelu_bwd)


def _col(m, lane, h):
    return jnp.sum(jnp.where(lane == h, m, 0.0), axis=1, keepdims=True)


def _row(m, sub, h):
    return jnp.sum(jnp.where(sub == h, m, 0.0), axis=0, keepdims=True)


def _mixer_chunk(us, vs, zs, xbcs, halos, dtblk, hps, prm):
    lane = lax.broadcasted_iota(jnp.int32, (CH, CH), 1)
    sub = lax.broadcasted_iota(jnp.int32, (CH, CH), 0)
    left = lane < 64
    top = sub < 64
    causal = sub >= lane

    gus = [_gelu(u) for u in us]
    gvs = [_gelu(v) for v in vs]
    mu = sum(jnp.sum(g, axis=1, keepdims=True) for g in gvs) / D_MODEL
    cen = [g - mu for g in gvs]
    var = sum(jnp.sum(c * c, axis=1, keepdims=True) for c in cen) / D_MODEL
    rstd = lax.rsqrt(var + LN_EPS)
    a_out = []
    for g in range(N_BLK):
        vn = cen[g] * rstd * prm["ln_g"][g] + prm["ln_b"][g]
        w = jnp.where(causal, prm["wm"][g], 0.0)
        mixed = _mm(w, vn, NN) + _col(prm["bs_t"], lane, g)
        a_out.append(gus[g] * mixed)

    act = []
    for b in range(XBC_BLKS):
        w8 = prm["conv_w"][b]
        sub8 = lax.broadcasted_iota(jnp.int32, w8.shape, 0)
        ext = jnp.concatenate([halos[b], xbcs[b]], axis=0)
        conv = xbcs[b] * _row(w8, sub8, 3) + prm["conv_b"][b]
        for k in (1, 2, 3):
            conv = conv + _DELAYS[k](ext) * _row(w8, sub8, 3 - k)
        act.append(jax.nn.silu(conv))

    dt = jax.nn.softplus(dtblk + prm["dt_bias"])
    a_neg = -jnp.exp(prm["a_log"])
    tri = causal.astype(f32)
    acum = _mm_exact(tri, dt * a_neg)
    acum_t = acum.T
    dt_t = dt.T
    last = sub == CH - 1
    ys, h_out = [], []
    for grp in range(4):
        bm = act[8 + grp]
        cm = act[12 + grp]
        cb = _mm(cm, bm, NT)
        for p in (2 * grp, 2 * grp + 1):
            h0, h1 = 2 * p, 2 * p + 1
            xp = act[p]
            hp = hps[p]
            wis = []
            for h in (h0, h1):
                seg = _col(acum, lane, h) - _row(acum_t, sub, h)
                decay = jnp.exp(jnp.where(causal, seg, -jnp.inf))
                wis.append(cb * decay * _row(dt_t, sub, h))
            wcat = jnp.concatenate(wis, axis=1)
            xbd = jnp.concatenate([jnp.where(left, xp, 0.0), jnp.where(left, 0.0, xp)], axis=0)
            y_diag = _mm(wcat, xbd, NN)
            a_end = [jnp.sum(jnp.where(last & (lane == h), acum, 0.0), keepdims=True) for h in (h0, h1)]
            a_col = jnp.where(left, _col(acum, lane, h0), _col(acum, lane, h1))
            dt_col = jnp.where(left, _col(dt, lane, h0), _col(dt, lane, h1))
            to_end = jnp.exp(jnp.where(left, a_end[0], a_end[1]) - a_col) * dt_col
            states = _mm(xp * to_end, bm, TN)
            chunk_decay = jnp.where(top, jnp.exp(a_end[0]), jnp.exp(a_end[1]))
            h_out.append(chunk_decay * hp + states)
            y_off = jnp.exp(a_col) * _mm(cm, hp, NT)
            d_skip = jnp.where(left[:1], _col(prm["d_heads"], lane[:1], h0), _col(prm["d_heads"], lane[:1], h1))
            ys.append((y_diag + y_off + xp * d_skip) * jax.nn.silu(zs[p]))

    b_out = []
    for grp in range(4):
        pair = (ys[2 * grp], ys[2 * grp + 1])
        ms = sum(jnp.sum(y * y, axis=1, keepdims=True) for y in pair) / 256.0
        r = lax.rsqrt(ms + RMS_EPS)
        for j, y in enumerate(pair):
            b_out.append(y * r * prm["norm_g"][2 * grp + j])
    return a_out, b_out, h_out


def _attn_block(qps, kprev, kcur, vprev, vcur, sink_row, first):
    lane = lax.broadcasted_iota(jnp.int32, (CH, CH), 1)
    left = lane < 64
    own = lane <= lax.broadcasted_iota(jnp.int32, (CH, CH), 0)
    own8 = jnp.concatenate([own] * N_BLK, axis=0)

    def both_halves(a):
        sw = _swap64(a)
        return [jnp.where(left, a, sw), jnp.where(left, sw, a)]

    kc, kp, vc, vp = both_halves(kcur), both_halves(kprev), both_halves(vcur), both_halves(vprev)
    outs = []
    for j in range(2):
        q8 = jnp.concatenate([part for p in range(4 * j, 4 * j + 4)
                              for part in (jnp.where(left, qps[p], 0.0), jnp.where(left, 0.0, qps[p]))], axis=0)
        s_cur = _row_blocks(_mm(q8, kc[j], NT))
        s_prev = _row_blocks(_mm(q8, kp[j], NT))
        probs = []
        for h in range(N_BLK):
            s = jnp.where(own, s_cur[h] * ATT_SCALE, jnp.where(first, -jnp.inf, s_prev[h] * ATT_SCALE))
            sink = _col(sink_row, lane[:1], N_BLK * j + h)
            m = lax.stop_gradient(jnp.maximum(jnp.max(s, axis=1, keepdims=True), sink))
            pexp = jnp.exp(s - m)
            probs.append(pexp / (jnp.sum(pexp, axis=1, keepdims=True) + jnp.exp(sink - m)))
        p8 = jnp.concatenate(probs, axis=0)
        o = _row_blocks(_mm(jnp.where(own8, p8, 0.0), vc[j], NN) + _mm(jnp.where(own8, 0.0, p8), vp[j], NN))
        for t in range(4):
            outs.append(jnp.where(left, o[2 * t], o[2 * t + 1]))
    return outs


def _rmsnorm(x, g):
    r = lax.rsqrt(jnp.mean(x * x, axis=-1, keepdims=True) + RMS_EPS)
    return x * r * g


def rmsnorm_fwd(x, g_row, name):
    s, d = x.shape
    tm = min(512, s)

    def body(x_ref, g_ref, y_ref):
        y_ref[...] = _rmsnorm(x_ref[...], g_ref[...]).astype(bf16)

    return pl.pallas_call(
        body, name=name, grid=(s // tm,),
        in_specs=[pl.BlockSpec((tm, d), lambda i: (i, 0)), pl.BlockSpec((1, d), lambda i: (0, 0))],
        out_specs=pl.BlockSpec((tm, d), lambda i: (i, 0)),
        out_shape=jax.ShapeDtypeStruct((s, d), bf16),
        compiler_params=_cparams(("parallel",)),
    )(x, g_row)


def _fit(dim, want):
    if dim <= want:
        return dim
    t = want
    while dim % t:
        t -= 128
    return t


def matmul(a, b, *, dims, name, out_dtype=f32, tm=1024, tn=512, tk=8192, a_pro=None, epi=None, epi_args=(),
           out_by_col_tile=False, after=None):
    if dims == "nn" and b.ndim == 3:
        (m, k), n, tn = a.shape, b.shape[0] * b.shape[2], b.shape[2]
    elif dims == "nn":
        (m, k), n = a.shape, b.shape[1]
    elif dims == "nt":
        (m, k), n = a.shape, b.shape[0]
    else:
        (k, m), n = a.shape, b.shape[1]
    tm, tn, tk = _fit(m, tm), _fit(n, tn), _fit(k, tk)
    nk = k // tk
    if dims == "nn":
        a_spec = pl.BlockSpec((tm, tk), lambda i, j, kk: (i, kk))
        b_spec = (pl.BlockSpec((None, tk, tn), lambda i, j, kk: (j, kk, 0)) if b.ndim == 3
                  else pl.BlockSpec((tk, tn), lambda i, j, kk: (kk, j)))
        dn = NN
    elif dims == "nt":
        a_spec = pl.BlockSpec((tm, tk), lambda i, j, kk: (i, kk))
        b_spec = pl.BlockSpec((tn, tk), lambda i, j, kk: (j, kk))
        dn = NT
    else:
        a_spec = pl.BlockSpec((tk, tm), lambda i, j, kk: (kk, i))
        b_spec = pl.BlockSpec((tk, tn), lambda i, j, kk: (kk, j))
        dn = TN
    e_specs = [pl.BlockSpec((tm, tn), lambda i, j, kk: (i, j)) if kind == "tile"
               else pl.BlockSpec((1, tn), lambda i, j, kk: (0, j)) for kind, _ in epi_args]
    n_epi = len(epi_args)
    order_specs = [] if after is None else [pl.BlockSpec((8, 128), lambda i, j, kk: (0, 0))]
    order_args = [] if after is None else [after]

    def body(*refs):
        a_ref, b_ref = refs[0], refs[1]
        e_refs = refs[2:2 + n_epi]
        n_in = 2 + n_epi + len(order_args)
        o_ref = refs[n_in]
        av = a_ref[...]
        if a_pro is not None:
            av = a_pro(av)
        part = _mm(av, b_ref[...], dn)

        def finish(acc):
            if epi is not None:
                acc = epi(acc, *[r[...] for r in e_refs])
            o_ref[...] = acc.astype(out_dtype)

        if nk == 1:
            finish(part)
        else:
            acc_ref = refs[n_in + 1]
            kk = pl.program_id(2)

            @pl.when(kk == 0)
            def _():
                acc_ref[...] = part

            @pl.when(kk > 0)
            def _():
                acc_ref[...] += part

            @pl.when(kk == nk - 1)
            def _():
                finish(acc_ref[...])

    if out_by_col_tile:
        out_spec = pl.BlockSpec((None, tm, tn), lambda i, j, kk: (j, i, 0))
        out_shape = jax.ShapeDtypeStruct((n // tn, m, tn), out_dtype)
    else:
        out_spec = pl.BlockSpec((tm, tn), lambda i, j, kk: (i, j))
        out_shape = jax.ShapeDtypeStruct((m, n), out_dtype)
    return pl.pallas_call(
        body, name=name, grid=(m // tm, n // tn, nk),
        in_specs=[a_spec, b_spec] + e_specs + order_specs,
        out_specs=out_spec,
        out_shape=out_shape,
        scratch_shapes=[pltpu.VMEM((tm, tn), f32)] if nk > 1 else [],
        compiler_params=_cparams(("parallel", "parallel", "arbitrary")),
    )(a, b, *[arr for _, arr in epi_args], *order_args)


def _relu2(a):
    r = jnp.maximum(a.astype(f32), 0.0)
    return r * r


def _add(acc, t):
    return acc + t


def _add_bias(acc, t):
    return acc + t


def _add_bias_res(acc, bias, res):
    return acc + bias + res


def _times_relu2_grad(acc, a):
    return acc * (2.0 * jnp.maximum(a.astype(f32), 0.0))


def matmul_rows(a, b, *, dims, name, epi, epi_args, outs, tm=512, a_pro=None, after=None):
    m, k = a.shape
    n = b.shape[-1] if dims == "nn" else b.shape[-2]
    tm = _fit(m, tm)
    dn = NN if dims == "nn" else NT
    e_specs = [pl.BlockSpec((tm, arr.shape[1]), lambda i: (i, 0)) if kind == "tile"
               else pl.BlockSpec((1, arr.shape[1]), lambda i: (0, 0)) for kind, arr in epi_args]
    order_specs = [] if after is None else [pl.BlockSpec((8, 128), lambda i: (0, 0))]
    order_args = [] if after is None else [after]
    n_in = 2 + len(epi_args) + len(order_args)

    def body(*refs):
        av = refs[0][...]
        if a_pro is not None:
            av = a_pro(av)
        if b.ndim == 3:
            kb = b.shape[2]
            acc = sum(_mm(av[:, s * kb:(s + 1) * kb], refs[1][s], dn) for s in range(b.shape[0]))
        else:
            acc = _mm(av, refs[1][...], dn)
        vals = epi(acc, *[r[...] for r in refs[2:2 + len(epi_args)]])
        for (kind, _), o_ref, val in zip(outs, refs[n_in:], vals):
            if kind == "tile":
                o_ref[...] = val.astype(o_ref.dtype)
            else:
                @pl.when(pl.program_id(0) == 0)
                def _():
                    o_ref[...] = jnp.zeros_like(o_ref)

                o_ref[...] += val

    out_specs = [pl.BlockSpec((tm, n), lambda i: (i, 0)) if kind == "tile" else pl.BlockSpec((1, arg), lambda i: (0, 0))
                 for kind, arg in outs]
    out_shape = [jax.ShapeDtypeStruct((m, n), arg) if kind == "tile" else jax.ShapeDtypeStruct((1, arg), f32)
                 for kind, arg in outs]
    return pl.pallas_call(
        body, name=name, grid=(m // tm,),
        in_specs=[pl.BlockSpec((tm, k), lambda i: (i, 0)), pl.BlockSpec(b.shape, lambda i: (0,) * b.ndim)]
                 + e_specs + order_specs,
        out_specs=out_specs, out_shape=out_shape,
        compiler_params=_cparams(("arbitrary",)),
    )(a, b, *[arr for _, arr in epi_args], *order_args)


def _res_norm(acc, res, g):
    h = acc + res
    return h, _rmsnorm(h, g)


def _bias_res_norm(acc, bias, res, g):
    h = acc + bias + res
    return h, _rmsnorm(h, g)


def _res_norm_loss(acc, res, g, target):
    def f(h, gv):
        err = jnp.square(_rmsnorm(h, gv) - target)
        return 0.5 * jnp.sum(jnp.mean(err, axis=-1, keepdims=True), axis=0, keepdims=True)

    loss, vjp = jax.vjp(f, acc + res, g)
    dh, dg = vjp(jnp.ones_like(loss))
    return dh, dg, jnp.broadcast_to(loss, (1, 128))


def _norm_bwd_res_colsum(dy, x, g, res):
    dx, dg = _norm_bwd_res(dy, x, g, res)
    return dx, dg, jnp.sum(dx, axis=0, keepdims=True)


def _norm_bwd_res(dy, x, g, res):
    _, vjp = jax.vjp(_rmsnorm, x, g)
    dx, dg = vjp(dy)
    return res + dx, dg


_MIXER_PARAM_SHAPES = (
    ("ln_g", (1, D_MODEL)), ("ln_b", (1, D_MODEL)), ("wm", (N_BLK, CH, CH)), ("bs_t", (CH, CH)),
    ("conv_w", (8, 2048)), ("conv_b", (1, 2048)), ("dt_bias", (1, CH)), ("a_log", (1, CH)),
    ("d_heads", (1, CH)), ("norm_g", (1, D_MODEL)),
)


def _blocks(v, n, off=0):
    return [v[:, off + i * CH: off + (i + 1) * CH] for i in range(n)]


def _split_mixer_params(vals):
    p = dict(vals)
    return {
        "ln_g": _blocks(p["ln_g"], N_BLK), "ln_b": _blocks(p["ln_b"], N_BLK),
        "wm": [p["wm"][g] for g in range(N_BLK)], "bs_t": p["bs_t"],
        "conv_w": _blocks(p["conv_w"], XBC_BLKS), "conv_b": _blocks(p["conv_b"], XBC_BLKS),
        "dt_bias": p["dt_bias"], "a_log": p["a_log"], "d_heads": p["d_heads"],
        "norm_g": _blocks(p["norm_g"], N_BLK),
    }


def _mixer_leaves(proj_ref, halo_ref, keep_halo):
    pv = proj_ref
    us = [pv[:, OFF_U + i * CH: OFF_U + (i + 1) * CH] for i in range(N_BLK)]
    vs = [pv[:, OFF_V + i * CH: OFF_V + (i + 1) * CH] for i in range(N_BLK)]
    zs = [pv[:, OFF_Z + i * CH: OFF_Z + (i + 1) * CH] for i in range(N_BLK)]
    xbcs = [pv[:, OFF_X + i * CH: OFF_X + (i + 1) * CH] for i in range(XBC_BLKS)]
    halos = [halo_ref[:, OFF_X + i * CH: OFF_X + (i + 1) * CH] * keep_halo for i in range(XBC_BLKS)]
    dtblk = pv[:, OFF_DT: OFF_DT + CH]
    return us, vs, zs, xbcs, halos, dtblk


def mixer_fwd(proj, prm):
    s = proj.shape[0]
    nc = s // CH
    names = [n for n, _ in _MIXER_PARAM_SHAPES]

    def body(proj_ref, halo_ref, *rest):
        p_refs = rest[:len(names)]
        ab_ref, hs_ref, h_ref = rest[len(names):]
        c = pl.program_id(0)

        @pl.when(c == 0)
        def _():
            h_ref[...] = jnp.zeros_like(h_ref)

        hs_ref[...] = h_ref[...]
        keep = (c > 0).astype(f32)
        us, vs, zs, xbcs, halos, dtblk = _mixer_leaves(proj_ref, halo_ref, keep)
        hps = [h_ref[i * CH:(i + 1) * CH, :] for i in range(N_BLK)]
        p = _split_mixer_params({n: r[...] for n, r in zip(names, p_refs)})
        a_out, b_out, h_out = _mixer_chunk(us, vs, zs, xbcs, halos, dtblk, hps, p)
        for i in range(N_BLK):
            ab_ref[:, i * CH:(i + 1) * CH] = a_out[i].astype(bf16)
            ab_ref[:, D_MODEL + i * CH: D_MODEL + (i + 1) * CH] = b_out[i].astype(bf16)
            h_ref[i * CH:(i + 1) * CH, :] = h_out[i]

    def const(shape):
        return pl.BlockSpec(shape, lambda c: (0,) * len(shape))

    return pl.pallas_call(
        body, name="mixer_fwd", grid=(nc,),
        in_specs=[pl.BlockSpec((CH, NP_IN), lambda c: (c, 0)),
                  pl.BlockSpec((8, NP_IN), lambda c: (jnp.maximum(c * (CH // 8) - 1, 0), 0))]
                 + [const(shp) for _, shp in _MIXER_PARAM_SHAPES],
        out_specs=[pl.BlockSpec((CH, 2 * D_MODEL), lambda c: (c, 0)),
                   pl.BlockSpec((None, D_MODEL, CH), lambda c: (c, 0, 0))],
        out_shape=[jax.ShapeDtypeStruct((s, 2 * D_MODEL), bf16), jax.ShapeDtypeStruct((nc, D_MODEL, CH), f32)],
        scratch_shapes=[pltpu.VMEM((D_MODEL, CH), f32)],
        compiler_params=_cparams(("arbitrary",)),
    )(proj, proj, *[prm[n] for n in names])


def mixer_bwd(proj, hstates, dab, prm):
    s = proj.shape[0]
    nc = s // CH
    names = [n for n, _ in _MIXER_PARAM_SHAPES]
    npar = len(names)

    def body(proj_ref, halo_ref, hs_ref, dab_ref, *rest):
        p_refs = rest[:npar]
        dproj_ref = rest[npar]
        g_refs = rest[npar + 1: 2 * npar + 1]
        dh_ref, dhalo_ref = rest[2 * npar + 1:]
        i = pl.program_id(0)
        c = nc - 1 - i

        @pl.when(i == 0)
        def _():
            dh_ref[...] = jnp.zeros_like(dh_ref)
            dhalo_ref[...] = jnp.zeros_like(dhalo_ref)
            for r in g_refs:
                r[...] = jnp.zeros_like(r)

        keep = (c > 0).astype(f32)
        us, vs, zs, xbcs, halos, dtblk = _mixer_leaves(proj_ref, halo_ref, keep)
        hps = [hs_ref[j * CH:(j + 1) * CH, :] for j in range(N_BLK)]
        pvals = {n: r[...] for n, r in zip(names, p_refs)}

        def fn(us, vs, zs, xbcs, halos, dtblk, hps, pvals):
            return _mixer_chunk(us, vs, zs, xbcs, halos, dtblk, hps, _split_mixer_params(pvals))

        _, vjp = jax.vjp(fn, us, vs, zs, xbcs, halos, dtblk, hps, pvals)
        da = [dab_ref[:, j * CH:(j + 1) * CH].astype(f32) for j in range(N_BLK)]
        db = [dab_ref[:, D_MODEL + j * CH: D_MODEL + (j + 1) * CH].astype(f32) for j in range(N_BLK)]
        dh = [dh_ref[j * CH:(j + 1) * CH, :] for j in range(N_BLK)]
        dus, dvs, dzs, dxbcs, dhalos, ddt, dhps, dp = vjp((da, db, dh))

        for j in range(N_BLK):
            dproj_ref[:, OFF_U + j * CH: OFF_U + (j + 1) * CH] = dus[j].astype(bf16)
            dproj_ref[:, OFF_V + j * CH: OFF_V + (j + 1) * CH] = dvs[j].astype(bf16)
            dproj_ref[:, OFF_Z + j * CH: OFF_Z + (j + 1) * CH] = dzs[j].astype(bf16)
            dh_ref[j * CH:(j + 1) * CH, :] = dhps[j]
        zeros_top = jnp.zeros((CH - 8, CH), f32)
        for j in range(XBC_BLKS):
            late = jnp.concatenate([zeros_top, dhalo_ref[:, j * CH:(j + 1) * CH]], axis=0)
            dproj_ref[:, OFF_X + j * CH: OFF_X + (j + 1) * CH] = (dxbcs[j] + late).astype(bf16)
        for j in range(XBC_BLKS):
            dhalo_ref[:, j * CH:(j + 1) * CH] = dhalos[j] * keep
        lane = lax.broadcasted_iota(jnp.int32, (CH, CH), 1)
        dproj_ref[:, OFF_DT: OFF_DT + CH] = jnp.where(lane < SSM_HEADS, ddt, 0.0).astype(bf16)
        dproj_ref[:, OFF_DT + CH:] = jnp.zeros((CH, NP_IN - OFF_DT - CH), bf16)
        for n, r in zip(names, g_refs):
            r[...] += dp[n]

    def const(shape):
        return pl.BlockSpec(shape, lambda i: (0,) * len(shape))

    outs = pl.pallas_call(
        body, name="mixer_bwd", grid=(nc,),
        in_specs=[pl.BlockSpec((CH, NP_IN), lambda i: (nc - 1 - i, 0)),
                  pl.BlockSpec((8, NP_IN), lambda i: (jnp.maximum((nc - 1 - i) * (CH // 8) - 1, 0), 0)),
                  pl.BlockSpec((None, D_MODEL, CH), lambda i: (nc - 1 - i, 0, 0)),
                  pl.BlockSpec((CH, 2 * D_MODEL), lambda i: (nc - 1 - i, 0))]
                 + [const(shp) for _, shp in _MIXER_PARAM_SHAPES],
        out_specs=[pl.BlockSpec((CH, NP_IN), lambda i: (nc - 1 - i, 0))]
                  + [const(shp) for _, shp in _MIXER_PARAM_SHAPES],
        out_shape=[jax.ShapeDtypeStruct((s, NP_IN), bf16)]
                  + [jax.ShapeDtypeStruct(shp, f32) for _, shp in _MIXER_PARAM_SHAPES],
        scratch_shapes=[pltpu.VMEM((D_MODEL, CH), f32), pltpu.VMEM((8, 2048), f32)],
        compiler_params=_cparams(("arbitrary",)),
    )(proj, proj, hstates, dab, *[prm[n] for n in names])
    return outs[0], dict(zip(names, outs[1:]))


_K_BLK = D_MODEL // CH
_V_BLK = _K_BLK + 1


def _attn_specs(rev, nb):
    def blk(i):
        return nb - 1 - i if rev else i

    q_spec = pl.BlockSpec((CH, D_MODEL), lambda i: (blk(i), 0))
    kv = lambda col, prev: pl.BlockSpec(
        (CH, CH), lambda i: (jnp.maximum(blk(i) - 1, 0) if prev else blk(i), col))
    return q_spec, [kv(_K_BLK, True), kv(_K_BLK, False), kv(_V_BLK, True), kv(_V_BLK, False)]


def attn_fwd(qkv, sink_row):
    s = qkv.shape[0]
    nb = s // CH

    def body(q_ref, kp_ref, kc_ref, vp_ref, vc_ref, sink_ref, o_ref):
        qps = [q_ref[:, p * CH:(p + 1) * CH] for p in range(N_BLK)]
        outs = _attn_block(qps, kp_ref[...], kc_ref[...], vp_ref[...], vc_ref[...], sink_ref[...],
                           pl.program_id(0) == 0)
        for p in range(N_BLK):
            o_ref[:, p * CH:(p + 1) * CH] = outs[p].astype(bf16)

    q_spec, kv_specs = _attn_specs(False, nb)
    return pl.pallas_call(
        body, name="attn_fwd", grid=(nb,),
        in_specs=[q_spec] + kv_specs + [pl.BlockSpec((1, CH), lambda i: (0, 0))],
        out_specs=pl.BlockSpec((CH, D_MODEL), lambda i: (i, 0)),
        out_shape=jax.ShapeDtypeStruct((s, D_MODEL), bf16),
        compiler_params=_cparams(("parallel",)),
    )(qkv, qkv, qkv, qkv, qkv, sink_row)


def attn_bwd(qkv, sink_row, dout):
    s = qkv.shape[0]
    nb = s // CH

    def body(q_ref, kp_ref, kc_ref, vp_ref, vc_ref, sink_ref, do_ref, dqkv_ref, dsink_ref, db_ref, carry_ref):
        i = pl.program_id(0)
        blk = nb - 1 - i

        @pl.when(i == 0)
        def _():
            dsink_ref[...] = jnp.zeros_like(dsink_ref)
            db_ref[...] = jnp.zeros_like(db_ref)
            carry_ref[...] = jnp.zeros_like(carry_ref)

        qps = [q_ref[:, p * CH:(p + 1) * CH] for p in range(N_BLK)]
        first = blk == 0
        _, vjp = jax.vjp(lambda *a: _attn_block(*a, first), qps, kp_ref[...], kc_ref[...], vp_ref[...],
                         vc_ref[...], sink_ref[...])
        dos = [do_ref[:, p * CH:(p + 1) * CH].astype(f32) for p in range(N_BLK)]
        dqs, dkp, dkc, dvp, dvc, dsink = vjp(dos)
        blocks = list(dqs) + [dkc + carry_ref[0], dvc + carry_ref[1]]
        for p, val in enumerate(blocks):
            dqkv_ref[:, p * CH:(p + 1) * CH] = val.astype(bf16)
            db_ref[:, p * CH:(p + 1) * CH] += jnp.sum(val, axis=0, keepdims=True)
        keep = jnp.logical_not(first).astype(f32)
        carry_ref[0] = dkp * keep
        carry_ref[1] = dvp * keep
        dsink_ref[...] += dsink

    q_spec, kv_specs = _attn_specs(True, nb)
    return pl.pallas_call(
        body, name="attn_bwd", grid=(nb,),
        in_specs=[q_spec] + kv_specs + [pl.BlockSpec((1, CH), lambda i: (0, 0)),
                                        pl.BlockSpec((CH, D_MODEL), lambda i: (nb - 1 - i, 0))],
        out_specs=[pl.BlockSpec((CH, QKV_DIM), lambda i: (nb - 1 - i, 0)), pl.BlockSpec((1, CH), lambda i: (0, 0)),
                   pl.BlockSpec((1, QKV_DIM), lambda i: (0, 0))],
        out_shape=[jax.ShapeDtypeStruct((s, QKV_DIM), bf16), jax.ShapeDtypeStruct((1, CH), f32),
                   jax.ShapeDtypeStruct((1, QKV_DIM), f32)],
        scratch_shapes=[pltpu.VMEM((2, CH, CH), f32)],
        compiler_params=_cparams(("arbitrary",)),
    )(qkv, qkv, qkv, qkv, qkv, sink_row, dout)


def adamw(w, g, m, v, name):
    def body(w_ref, g_ref, m_ref, v_ref, d_ref, nm_ref, nv_ref):
        gv = g_ref[...]
        nm = ADAM_B1 * m_ref[...] + (1.0 - ADAM_B1) * gv
        nv = ADAM_B2 * v_ref[...] + (1.0 - ADAM_B2) * jnp.square(gv)
        m_hat = nm / (1.0 - ADAM_B1 ** ADAM_STEP)
        v_hat = nv / (1.0 - ADAM_B2 ** ADAM_STEP)
        d_ref[...] = -ADAM_LR * (m_hat / (jnp.sqrt(v_hat) + ADAM_EPS) + ADAM_WD * w_ref[...])
        nm_ref[...] = nm
        nv_ref[...] = nv

    out_shape = [jax.ShapeDtypeStruct(w.shape, f32)] * 3
    if w.ndim == 3 and w.shape[1] == 1:
        tr = max(t for t in range(1, 129) if w.shape[0] % t == 0)
        tile = pl.BlockSpec((tr, 1, w.shape[2]), lambda i: (i, 0, 0))
        return pl.pallas_call(
            body, name=name, grid=(w.shape[0] // tr,),
            in_specs=[tile] * 4, out_specs=[tile] * 3, out_shape=out_shape,
            compiler_params=_cparams(("parallel",)),
        )(w, g, m, v)
    if w.ndim == 3 and w.shape[1] % 256 == 0:
        tile = pl.BlockSpec((None, 256, w.shape[2]), lambda l, i: (l, i, 0))
        return pl.pallas_call(
            body, name=name, grid=(w.shape[0], w.shape[1] // 256),
            in_specs=[tile] * 4, out_specs=[tile] * 3, out_shape=out_shape,
            compiler_params=_cparams(("parallel", "parallel")),
        )(w, g, m, v)
    return pl.pallas_call(body, name=name, in_specs=[_VMEM] * 4, out_specs=[_VMEM] * 3, out_shape=out_shape,
                          compiler_params=_cparams())(w, g, m, v)


_MESH = pl.DeviceIdType.MESH
_ANY = pl.BlockSpec(memory_space=pl.ANY)
_VMEM = pl.BlockSpec(memory_space=pltpu.VMEM)


def _place():
    x, y, c = lax.axis_index("x"), lax.axis_index("y"), lax.axis_index("c")
    chips = [(1 - x, y), (x, 1 - y), (1 - x, 1 - y)]
    return x, y, c, 2 * x + y, chips, [2 * cx + cy for cx, cy in chips]


def _half(c, rows):
    return pl.ds(pl.multiple_of(c * (rows // 2), 16), rows // 2)


def _step_rows(rows):
    return max(t for t in range(16, 641, 16) if rows % t == 0)


def place_shard(b, slot, name, dtype=bf16, after=None):
    r, c = b.shape
    tr = _step_rows(r)

    def body(slot_ref, b_ref, *rest):
        rest[-1][...] = b_ref[...].astype(dtype)

    order_specs = [] if after is None else [pl.BlockSpec((8, 128), lambda i, s: (0, 0))]
    return pl.pallas_call(
        body, name=name,
        grid_spec=pltpu.PrefetchScalarGridSpec(
            num_scalar_prefetch=1, grid=(r // tr,),
            in_specs=[pl.BlockSpec((tr, c), lambda i, s: (i, 0))] + order_specs,
            out_specs=pl.BlockSpec((None, tr, c), lambda i, s: (s[0], i, 0))),
        out_shape=jax.ShapeDtypeStruct((N_CHIPS, r, c), dtype),
        compiler_params=_cparams(("parallel",)),
    )(slot, b, *([] if after is None else [after]))


_HBM = pl.BlockSpec(memory_space=pltpu.HBM)
_SEM = pl.BlockSpec(memory_space=pltpu.SEMAPHORE)
_EFFECT = pltpu.SideEffectType.DATAFLOW_SIDE_EFFECTING


def _gather_ici_copies(bufs, send_sems, recv_sems):
    x, y, c, me, chips, chip_idx = _place()
    return [pltpu.make_async_remote_copy(
        src_ref=buf.at[me, _half(c, buf.shape[1])], dst_ref=buf.at[chip_idx[j], _half(c, buf.shape[1])],
        send_sem=send_sems.at[3 * k + j], recv_sem=recv_sems.at[3 * k + j],
        device_id=(*chips[j], c), device_id_type=_MESH) for j in range(3) for k, buf in enumerate(bufs)]


def gather_start(groups, tag):
    sizes = [len(g) for g in groups]
    flat = [b for g in groups for b in g]
    n = len(flat)

    def body(*refs):
        bufs, sems = refs[:n], refs[n:n + 2 * len(groups)]
        refs[-1][...] = jnp.zeros_like(refs[-1])
        x, y, c, me, chips, chip_idx = _place()
        lo = 0
        for gi, size in enumerate(sizes):
            for j in range(3):
                for k, buf in enumerate(bufs[lo:lo + size]):
                    mine = buf.at[me, _half(c, buf.shape[1])]
                    pltpu.make_async_remote_copy(
                        src_ref=mine, dst_ref=mine, send_sem=sems[2 * gi].at[3 * k + j],
                        recv_sem=sems[2 * gi + 1].at[3 * k + j], device_id=(*chips[j], c),
                        device_id_type=_MESH).start()
            lo += size

    sem_shapes = [pltpu.SemaphoreType.DMA((3 * size,)) for size in sizes for _ in range(2)]
    outs = pl.pallas_call(
        body, name=f"gather_start_{tag}",
        out_shape=(*sem_shapes, *[pltpu.HBM(b.shape, b.dtype) for b in flat], jax.ShapeDtypeStruct((8, 128), f32)),
        in_specs=[_HBM] * n, out_specs=(*[_SEM] * len(sem_shapes), *[_HBM] * n, _VMEM),
        input_output_aliases={i: len(sem_shapes) + i for i in range(n)},
        compiler_params=pltpu.CompilerParams(has_side_effects=_EFFECT),
    )(*[pltpu.with_memory_space_constraint(b, pltpu.HBM) for b in flat])
    sems = [(outs[2 * gi], outs[2 * gi + 1]) for gi in range(len(groups))]
    thru, lo = [], len(sem_shapes)
    for size in sizes:
        thru.append(list(outs[lo:lo + size]))
        lo += size
    return sems, thru, outs[-1]


def gather_wait(bufs, sems, after, tag):
    n = len(bufs)

    def body(*refs):
        for cp in _gather_ici_copies(refs[:n], refs[n], refs[n + 1]):
            cp.wait_send()
            cp.wait_recv()

    extra = list(after)
    return list(pl.pallas_call(
        body, name=f"gather_wait_{tag}",
        out_shape=[pltpu.HBM(b.shape, b.dtype) for b in bufs],
        in_specs=[_HBM] * n + [_SEM, _SEM] + [_ANY] * len(extra), out_specs=[_HBM] * n,
        input_output_aliases={i: i for i in range(n)},
        compiler_params=pltpu.CompilerParams(has_side_effects=_EFFECT),
    )(*bufs, *sems, *extra))


def gather_forward(bufs, tag):
    n = len(bufs)

    def body(*refs):
        out_refs = refs[n:2 * n]
        send_sems, recv_sems = refs[2 * n:]
        x, y, c, me, chips, chip_idx = _place()

        def copy(k, j, half):
            part = out_refs[k].at[chip_idx[j], _half(half, out_refs[k].shape[1])]
            return pltpu.make_async_remote_copy(
                src_ref=part, dst_ref=part, send_sem=send_sems.at[3 * k + j], recv_sem=recv_sems.at[3 * k + j],
                device_id=(x, y, 1 - c), device_id_type=_MESH)

        sends = [copy(k, j, c) for j in range(3) for k in range(n)]
        for cp in sends:
            cp.start()
        for j in range(3):
            for k in range(n):
                copy(k, j, 1 - c).wait_recv()
        for cp in sends:
            cp.wait_send()

    return list(pl.pallas_call(
        body, name=f"gather_forward_{tag}",
        out_shape=[jax.ShapeDtypeStruct(b.shape, b.dtype) for b in bufs],
        in_specs=[_ANY] * n, out_specs=[_ANY] * n, input_output_aliases={i: i for i in range(n)},
        scratch_shapes=[pltpu.SemaphoreType.DMA((3 * n,)), pltpu.SemaphoreType.DMA((3 * n,))],
    )(*bufs))


def exchange_halves(bufs, tag):
    n = len(bufs)

    def body(*refs):
        g_refs, out_refs = refs[:n], refs[n:2 * n]
        send_sems, recv_sems = refs[2 * n:]
        x, y, c, *_ = _place()
        cps = [pltpu.make_async_remote_copy(
            src_ref=g_refs[b].at[:, _half(1 - c, g_refs[b].shape[1])], dst_ref=out_refs[b],
            send_sem=send_sems.at[b], recv_sem=recv_sems.at[b], device_id=(x, y, 1 - c), device_id_type=_MESH)
            for b in range(n)]
        for cp in cps:
            cp.start()
        for cp in cps:
            cp.wait()

    return pl.pallas_call(
        body, name=f"exchange_halves_{tag}",
        out_shape=[jax.ShapeDtypeStruct((N_CHIPS, b.shape[1] // 2, b.shape[2]), b.dtype) for b in bufs],
        in_specs=[_ANY] * n, out_specs=[_ANY] * n,
        scratch_shapes=[pltpu.SemaphoreType.DMA((n,)), pltpu.SemaphoreType.DMA((n,))],
    )(*bufs)


def add_halves(g, got, c_idx, name):
    hr, cols = got.shape[1], got.shape[2]
    tr = _step_rows(hr)
    steps = hr // tr

    def body(c_ref, g_ref, got_ref, o_ref):
        o_ref[...] = (g_ref[...].astype(f32) + got_ref[...].astype(f32)).astype(bf16)

    return pl.pallas_call(
        body, name=name,
        grid_spec=pltpu.PrefetchScalarGridSpec(
            num_scalar_prefetch=1, grid=(N_CHIPS, steps),
            in_specs=[pl.BlockSpec((None, tr, cols), lambda s, i, c: (s, c[0] * steps + i, 0)),
                      pl.BlockSpec((None, tr, cols), lambda s, i, c: (s, i, 0))],
            out_specs=pl.BlockSpec((None, tr, cols), lambda s, i, c: (s, i, 0))),
        out_shape=jax.ShapeDtypeStruct(got.shape, bf16),
        compiler_params=_cparams(("parallel", "parallel")),
    )(c_idx, g, got)


def sum_chips(t, got, place_idx, name):
    hr, cols = t.shape[1], t.shape[2]
    tr = _step_rows(hr)
    steps = hr // tr

    def body(idx_ref, t_ref, got_ref, o_ref):
        acc = t_ref[...].astype(f32)
        for j in range(3):
            acc = acc + got_ref[j].astype(f32)
        o_ref[...] = acc

    return pl.pallas_call(
        body, name=name,
        grid_spec=pltpu.PrefetchScalarGridSpec(
            num_scalar_prefetch=1, grid=(steps,),
            in_specs=[pl.BlockSpec((None, tr, cols), lambda i, idx: (idx[0], i, 0)),
                      pl.BlockSpec((3, tr, cols), lambda i, idx: (0, i, 0))],
            out_specs=pl.BlockSpec((tr, cols), lambda i, idx: (idx[1] * steps + i, 0))),
        out_shape=jax.ShapeDtypeStruct((2 * hr, cols), f32),
        compiler_params=_cparams(("parallel",)),
    )(place_idx, t, got)


def _share_copies(refs, send_sems, recv_sems):
    x, y, c, *_ = _place()
    return [pltpu.make_async_remote_copy(
        src_ref=ref.at[_half(c, ref.shape[0])], dst_ref=ref.at[_half(c, ref.shape[0])], send_sem=send_sems.at[b],
        recv_sem=recv_sems.at[b], device_id=(x, y, 1 - c), device_id_type=_MESH) for b, ref in enumerate(refs)]


def share_start(bufs, tag):
    n = len(bufs)

    def body(*refs):
        for cp in _share_copies(refs[:n], refs[n], refs[n + 1]):
            cp.start()
        token = refs[-1]
        token[...] = jnp.zeros_like(token)

    outs = pl.pallas_call(
        body, name=f"share_start_{tag}",
        out_shape=(pltpu.SemaphoreType.DMA((n,)), pltpu.SemaphoreType.DMA((n,)),
                   *[pltpu.HBM(b.shape, b.dtype) for b in bufs], jax.ShapeDtypeStruct((8, 128), f32)),
        in_specs=[_HBM] * n, out_specs=(_SEM, _SEM, *[_HBM] * n, _VMEM),
        input_output_aliases={i: 2 + i for i in range(n)},
        compiler_params=pltpu.CompilerParams(has_side_effects=_EFFECT),
    )(*[pltpu.with_memory_space_constraint(b, pltpu.HBM) for b in bufs])
    return (outs[0], outs[1], list(outs[2:2 + n])), outs[-1]


def share_wait(send_sems, recv_sems, bufs, after, tag):
    n = len(bufs)

    def body(*refs):
        x, y, c, *_ = _place()
        for b, ref in enumerate(refs[:n]):
            cp = pltpu.make_async_remote_copy(
                src_ref=ref.at[_half(c, ref.shape[0])], dst_ref=ref.at[_half(1 - c, ref.shape[0])],
                send_sem=refs[n].at[b], recv_sem=refs[n + 1].at[b], device_id=(x, y, 1 - c), device_id_type=_MESH)
            cp.wait_send()
            cp.wait_recv()

    return list(pl.pallas_call(
        body, name=f"share_wait_{tag}",
        out_shape=[pltpu.HBM(b.shape, b.dtype) for b in bufs],
        in_specs=[_HBM] * n + [_SEM, _SEM, _ANY], out_specs=[_HBM] * n,
        input_output_aliases={i: i for i in range(n)},
        compiler_params=pltpu.CompilerParams(has_side_effects=_EFFECT),
    )(*bufs, send_sems, recv_sems, after))


def _scatter_copies(t_refs, land_refs, send_sems, recv_sems):
    x, y, c, me, chips, chip_idx = _place()
    return [pltpu.make_async_remote_copy(
        src_ref=t_refs[b].at[chip_idx[j]], dst_ref=land_refs[b].at[j], send_sem=send_sems.at[3 * b + j],
        recv_sem=recv_sems.at[3 * b + j], device_id=(*chips[j], c), device_id_type=_MESH)
        for j in range(3) for b in range(len(t_refs))]


def scatter_start(ts, tag):
    n = len(ts)
    lands = [lax.empty((3,) + t.shape[1:], t.dtype) for t in ts]

    def body(*refs):
        for cp in _scatter_copies(refs[:n], refs[n:2 * n], refs[2 * n], refs[2 * n + 1]):
            cp.start()
        token = refs[-1]
        token[...] = jnp.zeros_like(token)

    hbm = [pltpu.HBM(a.shape, a.dtype) for a in (*ts, *lands)]
    outs = pl.pallas_call(
        body, name=f"scatter_start_{tag}",
        out_shape=(pltpu.SemaphoreType.DMA((3 * n,)), pltpu.SemaphoreType.DMA((3 * n,)), *hbm,
                   jax.ShapeDtypeStruct((8, 128), f32)),
        in_specs=[_HBM] * (2 * n), out_specs=(_SEM, _SEM, *[_HBM] * (2 * n), _VMEM),
        input_output_aliases={i: 2 + i for i in range(2 * n)},
        compiler_params=pltpu.CompilerParams(has_side_effects=_EFFECT),
    )(*[pltpu.with_memory_space_constraint(a, pltpu.HBM) for a in (*ts, *lands)])
    return outs[0], outs[1], list(outs[2:2 + n]), list(outs[2 + n:2 + 2 * n]), outs[-1]


def scatter_wait(send_sems, recv_sems, ts, lands, after, tag):
    n = len(ts)

    def body(*refs):
        for cp in _scatter_copies(refs[:n], refs[n:2 * n], refs[2 * n], refs[2 * n + 1]):
            cp.wait_send()
            cp.wait_recv()

    outs = pl.pallas_call(
        body, name=f"scatter_wait_{tag}",
        out_shape=[pltpu.HBM(a.shape, a.dtype) for a in (*ts, *lands)],
        in_specs=[_HBM] * (2 * n) + [_SEM, _SEM, _ANY], out_specs=[_HBM] * (2 * n),
        input_output_aliases={i: i for i in range(2 * n)},
        compiler_params=pltpu.CompilerParams(has_side_effects=_EFFECT),
    )(*ts, *lands, send_sems, recv_sems, after)
    return list(outs[:n]), list(outs[n:])


N_SENDERS = 7


def _direct_copies(g_refs, land_refs, send_sems, recv_sems):
    x, y, c, me, chips, chip_idx = _place()
    cps = []
    for b, (g, land) in enumerate(zip(g_refs, land_refs)):
        rows, base = g.shape[1], N_SENDERS * b
        cps.append(pltpu.make_async_remote_copy(
            src_ref=g.at[me, _half(1 - c, rows)], dst_ref=land.at[0], send_sem=send_sems.at[base],
            recv_sem=recv_sems.at[base], device_id=(x, y, 1 - c), device_id_type=_MESH))
        for j in range(3):
            for core in range(2):
                cps.append(pltpu.make_async_remote_copy(
                    src_ref=g.at[chip_idx[j], _half(core, rows)], dst_ref=land.at[1 + 2 * j + c],
                    send_sem=send_sems.at[base + 1 + 2 * j + core], recv_sem=recv_sems.at[base + 1 + 2 * j + c],
                    device_id=(*chips[j], core), device_id_type=_MESH))
    return cps


def direct_start(gs, tag):
    n = len(gs)
    lands = [lax.empty((N_SENDERS, g.shape[1] // 2, g.shape[2]), g.dtype) for g in gs]

    def body(*refs):
        for cp in _direct_copies(refs[:n], refs[n:2 * n], refs[2 * n], refs[2 * n + 1]):
            cp.start()
        token = refs[-1]
        token[...] = jnp.zeros_like(token)

    hbm = [pltpu.HBM(a.shape, a.dtype) for a in (*gs, *lands)]
    outs = pl.pallas_call(
        body, name=f"direct_start_{tag}",
        out_shape=(pltpu.SemaphoreType.DMA((N_SENDERS * n,)), pltpu.SemaphoreType.DMA((N_SENDERS * n,)), *hbm,
                   jax.ShapeDtypeStruct((8, 128), f32)),
        in_specs=[_HBM] * (2 * n), out_specs=(_SEM, _SEM, *[_HBM] * (2 * n), _VMEM),
        input_output_aliases={i: 2 + i for i in range(2 * n)},
        compiler_params=pltpu.CompilerParams(has_side_effects=_EFFECT),
    )(*[pltpu.with_memory_space_constraint(a, pltpu.HBM) for a in (*gs, *lands)])
    return outs[0], outs[1], list(outs[2:2 + n]), list(outs[2 + n:2 + 2 * n]), outs[-1]


def direct_wait(send_sems, recv_sems, gs, lands, after, tag):
    n = len(gs)

    def body(*refs):
        g_refs, land_refs, sends, recvs = refs[:n], refs[n:2 * n], refs[2 * n], refs[2 * n + 1]
        for b in range(n):
            for k in range(N_SENDERS):
                cp = pltpu.make_async_remote_copy(
                    src_ref=g_refs[b].at[0, _half(0, g_refs[b].shape[1])], dst_ref=land_refs[b].at[k],
                    send_sem=sends.at[N_SENDERS * b + k], recv_sem=recvs.at[N_SENDERS * b + k],
                    device_id=_place()[:3], device_id_type=_MESH)
                cp.wait_send()
                cp.wait_recv()

    outs = pl.pallas_call(
        body, name=f"direct_wait_{tag}",
        out_shape=[pltpu.HBM(a.shape, a.dtype) for a in (*gs, *lands)],
        in_specs=[_HBM] * (2 * n) + [_SEM, _SEM, _ANY], out_specs=[_HBM] * (2 * n),
        input_output_aliases={i: i for i in range(2 * n)},
        compiler_params=pltpu.CompilerParams(has_side_effects=_EFFECT),
    )(*gs, *lands, send_sems, recv_sems, after)
    return list(outs[:n]), list(outs[n:])


def sum_senders(g, lands, place_idx, name):
    hr, cols = lands.shape[1], lands.shape[2]
    tr = _step_rows(hr)
    steps = hr // tr

    def body(idx_ref, g_ref, land_ref, o_ref):
        acc = g_ref[...].astype(f32)
        for k in range(N_SENDERS):
            acc = acc + land_ref[k].astype(f32)
        o_ref[...] = acc

    return pl.pallas_call(
        body, name=name,
        grid_spec=pltpu.PrefetchScalarGridSpec(
            num_scalar_prefetch=1, grid=(steps,),
            in_specs=[pl.BlockSpec((None, tr, cols), lambda i, idx: (idx[0], idx[1] * steps + i, 0)),
                      pl.BlockSpec((N_SENDERS, tr, cols), lambda i, idx: (0, i, 0))],
            out_specs=pl.BlockSpec((tr, cols), lambda i, idx: (idx[1] * steps + i, 0))),
        out_shape=jax.ShapeDtypeStruct((2 * hr, cols), f32),
        compiler_params=_cparams(("parallel",)),
    )(place_idx, g, lands)


class GradReducer:
    def __init__(self, c_idx, place_idx):
        self.c_idx, self.place_idx = c_idx, place_idx

    def start(self, bufs, tag, direct=False):
        if direct:
            send_sems, recv_sems, gs, lands, token = direct_start(bufs, tag)
            return (True, send_sems, recv_sems, gs, lands), token
        got = exchange_halves(bufs, tag)
        ts = [add_halves(b, g, self.c_idx, f"add_halves_{tag}{i}") for i, (b, g) in enumerate(zip(bufs, got))]
        send_sems, recv_sems, ts, lands, token = scatter_start(ts, tag)
        return (False, send_sems, recv_sems, ts, lands), token

    def finish(self, state, after, tag):
        direct, *flight = state
        if direct:
            gs, lands = direct_wait(*flight, after, tag)
            sums = [sum_senders(g, l, self.place_idx, f"sum_senders_{tag}{i}") for i, (g, l) in enumerate(zip(gs, lands))]
        else:
            ts, lands = scatter_wait(*flight, after, tag)
            sums = [sum_chips(t, l, self.place_idx, f"sum_chips_{tag}{i}") for i, (t, l) in enumerate(zip(ts, lands))]
        return share_start(sums, tag)

    def collect(self, pending, after, tag):
        return share_wait(*pending, after, tag)


def allreduce_small(sp):
    rows = sp.shape[0]
    hr = rows // 2

    def body(s_ref, out_ref, sib_ref, chip_ref, four_ref, send_sems, recv_sems):
        x, y, c, me, chips, chip_idx = _place()
        sibling = (x, y, 1 - c)
        mine = pl.ds(pl.multiple_of(c * hr, 8), hr)
        other = pl.ds(pl.multiple_of((1 - c) * hr, 8), hr)

        swap = pltpu.make_async_remote_copy(src_ref=s_ref, dst_ref=sib_ref, send_sem=send_sems.at[0],
                                            recv_sem=recv_sems.at[0], device_id=sibling, device_id_type=_MESH)
        swap.start()
        swap.wait()
        is_core0 = c == 0
        chip_ref[...] = jnp.where(is_core0, s_ref[...], sib_ref[...]) + jnp.where(is_core0, sib_ref[...], s_ref[...])

        sends = [pltpu.make_async_remote_copy(
            src_ref=chip_ref.at[mine], dst_ref=four_ref.at[me], send_sem=send_sems.at[1 + j],
            recv_sem=recv_sems.at[1 + j], device_id=(*chips[j], c), device_id_type=_MESH) for j in range(3)]
        for cp in sends:
            cp.start()
        four_ref[me] = chip_ref[mine, :]
        for j in range(3):
            pltpu.make_async_remote_copy(
                src_ref=chip_ref.at[mine], dst_ref=four_ref.at[chip_idx[j]], send_sem=send_sems.at[1 + j],
                recv_sem=recv_sems.at[1 + j], device_id=(*chips[j], c), device_id_type=_MESH).wait_recv()
        for cp in sends:
            cp.wait_send()
        out_ref[mine, :] = (four_ref[0] + four_ref[1]) + (four_ref[2] + four_ref[3])

        share = pltpu.make_async_remote_copy(src_ref=out_ref.at[mine], dst_ref=out_ref.at[mine], send_sem=send_sems.at[4],
                                             recv_sem=recv_sems.at[4], device_id=sibling, device_id_type=_MESH)
        share.start()
        pltpu.make_async_remote_copy(src_ref=out_ref.at[mine], dst_ref=out_ref.at[other], send_sem=send_sems.at[4],
                                     recv_sem=recv_sems.at[4], device_id=sibling, device_id_type=_MESH).wait_recv()
        share.wait_send()

    return pl.pallas_call(
        body, name="allreduce_small",
        out_shape=jax.ShapeDtypeStruct(sp.shape, sp.dtype),
        in_specs=[_VMEM], out_specs=_VMEM,
        scratch_shapes=[pltpu.VMEM(sp.shape, sp.dtype), pltpu.VMEM(sp.shape, sp.dtype),
                        pltpu.VMEM((N_CHIPS, hr, sp.shape[1]), sp.dtype),
                        pltpu.SemaphoreType.DMA((5,)), pltpu.SemaphoreType.DMA((5,))],
        compiler_params=_cparams(),
    )(sp)


def _n_rows(shape):
    n = 1
    for d in shape:
        n *= d
    return 8 * (-(-n // 8192))


def _pack(arrays, total_rows):
    parts = []
    for a in arrays:
        flat = a.reshape(-1)
        parts.append(jnp.pad(flat, (0, 1024 * _n_rows(a.shape) - flat.shape[0])).reshape(-1, 1024))
    rows = jnp.concatenate(parts, axis=0)
    return jnp.pad(rows, ((0, total_rows - rows.shape[0]), (0, 0)))


def _unpack(packed, shapes):
    out, r = [], 0
    for shp in shapes:
        n = 1
        for d in shp:
            n *= d
        nr = _n_rows(shp)
        out.append(packed[r:r + nr].reshape(-1)[:n].reshape(shp))
        r += nr
    return out


_COLUMN_SHARDED = ("w_in_even", "w_qkv")
IN_SHARD, IN_PAD = 1284, 1408
QKV_SHARD, QKV_PAD = 320, 384


def _lane_padded(a, cols):
    return jnp.pad(a, ((0, 0), (0, cols - a.shape[1])))


_SMALL_SHAPES = (
    ("norm_mix_g", (2, 1024)), ("norm_mlp_g", (2, 1024)), ("final_norm_g", (1024,)), ("gm_ln_g", (1, 1024)),
    ("gm_ln_b", (1, 1024)), ("gm_w_s", (1, 8, 128, 128)), ("gm_b_s", (1, 8, 128)), ("ssm_conv_b", (1, 2048)),
    ("ssm_dt_bias", (1, 16)), ("ssm_a_log", (1, 16)), ("ssm_d", (1, 16)), ("ssm_norm_g", (1, 1024)),
    ("attn_sinks", (1, 16)), ("ssm_conv_w", (1, 4, 2048)), ("b_qkv", (1, 1280)), ("b_o", (1, 1024)),
)
_N_REPLICATED = 13
_SHARDED_SMALL = (("ssm_conv_w", 2, 512), ("b_qkv", 1, 320), ("b_o", 1, 256))
_SHARD_PACK_ROWS = 32


def _cols_by_owner(a):
    return a.transpose(1, 0, 2).reshape(a.shape[1], -1)


class WeightGatherer:
    def __init__(self, w, chip_idx):
        def place(tag, b, dtype=bf16, after=None):
            return place_shard(b, chip_idx, f"place_shard_{tag}", dtype, after)

        sems_in, bufs_in, self.started = gather_start([
            [place("in", _lane_padded(w["w_in_even"][0].astype(bf16), IN_PAD)),
             place("small", _pack([w[n] for n, _, _ in _SHARDED_SMALL], _SHARD_PACK_ROWS), f32)]], "in")
        t = self.started
        sems, bufs, self.all_started = gather_start([
            [place("out", w["w_out_even"][0], after=t), place("up0", w["w_up"][0], after=t),
             place("down0", w["w_down"][0], after=t)],
            [place("qkv", _lane_padded(w["w_qkv"][0], QKV_PAD), after=t), place("o", w["w_o"][0], after=t),
             place("up1", w["w_up"][1], after=t), place("down1", w["w_down"][1], after=t)],
        ], "rest")
        self.sems, self.bufs = sems_in + sems, bufs_in + bufs

    def _group(self, gi, after, tag):
        return gather_forward(gather_wait(self.bufs[gi], self.sems[gi], after, tag), tag)

    def mixer_in(self, after):
        g, small = self._group(0, [after, self.all_started], "in")
        shard_shapes = [tuple(width if i == axis else d for i, d in enumerate(dict(_SMALL_SHAPES)[n]))
                        for n, axis, width in _SHARDED_SMALL]
        per_chip = [_unpack(small[s], shard_shapes) for s in range(N_CHIPS)]
        full = {n: jnp.concatenate([per_chip[s][i] for s in range(N_CHIPS)], axis=axis)
                for i, (n, axis, _) in enumerate(_SHARDED_SMALL)}
        w_in_p = jnp.concatenate([g[s, :, :IN_SHARD] for s in range(N_CHIPS)]
                                 + [jnp.zeros((g.shape[1], NP_IN - IN_EVEN), g.dtype)], axis=1)
        return w_in_p, full

    def layer0(self, after):
        w_out, w_up, w_down = self._group(1, [after], "l0")
        return w_out.reshape(2048, 1024), w_up, w_down.reshape(4096, 1024)

    def layer1(self, after):
        q, w_o, w_up, w_down = self._group(2, [after], "l1")
        w_qkv = jnp.concatenate([q[s, :, :QKV_SHARD] for s in range(N_CHIPS)], axis=1)
        return w_qkv, w_o.reshape(1024, 1024), w_up, w_down.reshape(4096, 1024)


def _row2(v):
    return v.reshape(1, -1)


def _lane_pad(v):
    return jnp.pad(v, ((0, 0), (0, CH - v.shape[1])))


_H_AND_NORM = (("tile", f32), ("tile", bf16))
_DX_AND_DG = (("tile", f32), ("sum", D_MODEL))


def _mlp_bwd(dh_out, h, g_row, y, a, w_up, w_down, tag, after=None):
    da = matmul(dh_out, w_down, dims="nt", name=f"mlp_da{tag}", out_dtype=bf16, tm=2048, tn=1024,
                epi=_times_relu2_grad, epi_args=(("tile", a),), after=after)
    dw_down = matmul(a, dh_out, dims="tn", name=f"mlp_dwdown{tag}", out_dtype=bf16, a_pro=_relu2)
    dw_up = matmul(y, da, dims="tn", name=f"mlp_dwup{tag}", out_dtype=bf16, tn=1024, out_by_col_tile=True)
    dh, dg, dh_colsum = matmul_rows(da, w_up, dims="nt", name=f"mlp_dy{tag}", epi=_norm_bwd_res_colsum,
                                    epi_args=(("tile", h), ("row", g_row), ("tile", dh_out)),
                                    outs=_DX_AND_DG + (("sum", D_MODEL),))
    return dh, dg, dw_up, dw_down, dh_colsum


def _by_owner(a):
    return a.reshape(N_CHIPS, a.shape[0] // N_CHIPS, a.shape[1])


def _row_shards(a, shard, padded):
    return jnp.stack([jnp.pad(a[shard * s: shard * (s + 1)], ((0, padded - shard), (0, 0))) for s in range(N_CHIPS)])


def _col_shards(a, shard, padded):
    return jnp.stack([_lane_padded(a[:, shard * s: shard * (s + 1)], padded) for s in range(N_CHIPS)])


def _local_step(x, target, weights, sm, reducer):
    w_up, w_down = [None, None], [None, None]
    mix_g = [_row2(sm["norm_mix_g"][i]) for i in range(2)]
    y0 = rmsnorm_fwd(x, mix_g[0] + weights.started[:1, :1], "mix_norm0")
    w_in_p, sharded_small = weights.mixer_in(y0)
    sm = {**sm, **sharded_small}
    mlp_g = [_row2(sm["norm_mlp_g"][i]) for i in range(2)]
    mixer_prm = {
        "ln_g": sm["gm_ln_g"], "ln_b": sm["gm_ln_b"], "wm": sm["gm_w_s"][0],
        "bs_t": jnp.pad(sm["gm_b_s"][0].T, ((0, 0), (0, CH - N_BLK))),
        "conv_w": jnp.pad(sm["ssm_conv_w"][0], ((0, 4), (0, 0))), "conv_b": sm["ssm_conv_b"],
        "dt_bias": _lane_pad(sm["ssm_dt_bias"]), "a_log": _lane_pad(sm["ssm_a_log"]),
        "d_heads": _lane_pad(sm["ssm_d"]), "norm_g": sm["ssm_norm_g"],
    }
    sink_row = _lane_pad(sm["attn_sinks"])

    proj = matmul(y0, w_in_p, dims="nn", name="in_proj", tm=2048, tn=768)
    ab, hstates = mixer_fwd(proj, mixer_prm)
    w_out, w_up[0], w_down[0] = weights.layer0(ab)
    h1, y1 = matmul_rows(ab, w_out, dims="nn", name="out_proj", epi=_res_norm,
                         epi_args=(("tile", x), ("row", mlp_g[0])), outs=_H_AND_NORM)
    a1 = matmul(y1, w_up[0], dims="nn", name="mlp_up0", out_dtype=bf16, tm=2048, tn=1024)
    w_qkv, w_o, w_up[1], w_down[1] = weights.layer1(a1)
    h2, y2 = matmul_rows(a1, w_down[0], dims="nn", name="mlp_down0", a_pro=_relu2, epi=_res_norm,
                         epi_args=(("tile", h1), ("row", mix_g[1])), outs=_H_AND_NORM)
    qkv = matmul(y2, w_qkv, dims="nn", name="qkv_proj", tn=QKV_DIM, epi=_add_bias, epi_args=(("row", sm["b_qkv"]),))
    att = attn_fwd(qkv, sink_row)
    h3, y3 = matmul_rows(att, w_o, dims="nn", name="o_proj", epi=_bias_res_norm,
                         epi_args=(("row", sm["b_o"]), ("tile", h2), ("row", mlp_g[1])), outs=_H_AND_NORM)
    a3 = matmul(y3, w_up[1], dims="nn", name="mlp_up1", out_dtype=bf16, tm=2048, tn=1024)
    dh4, dg_final, loss = matmul_rows(
        a3, w_down[1], dims="nn", name="mlp_down1", a_pro=_relu2, epi=_res_norm_loss,
        epi_args=(("tile", h3), ("row", _row2(sm["final_norm_g"])), ("tile", target)),
        outs=(("tile", f32), ("sum", D_MODEL), ("sum", 128)))

    dh3, dg_mlp1, dw_up1, dw_down1, db_o = _mlp_bwd(dh4, h3, mlp_g[1], y3, a3, w_up[1], w_down[1], 1)
    datt = matmul(dh3, w_o, dims="nt", name="attn_dout", out_dtype=bf16)
    dw_o = matmul(att, dh3, dims="tn", name="dw_o", out_dtype=bf16)
    dqkv, dsink, db_qkv = attn_bwd(qkv, sink_row, datt)
    dw_qkv = matmul(y2, dqkv, dims="tn", name="dw_qkv", out_dtype=bf16, tn=QKV_DIM)
    dh2, dg_mix1 = matmul_rows(dqkv, w_qkv, dims="nt", name="dy_qkv", epi=_norm_bwd_res,
                               epi_args=(("tile", h2), ("row", mix_g[1]), ("tile", dh3)), outs=_DX_AND_DG)
    layer1 = [jnp.concatenate([_by_owner(dw_o), dw_up1, _by_owner(dw_down1)], axis=1),
              _col_shards(dw_qkv, QKV_SHARD, QKV_PAD)]
    flight1, token1 = reducer.start(layer1, "l1", direct=True)
    dh1, dg_mlp0, dw_up0, dw_down0, _ = _mlp_bwd(dh2, h1, mlp_g[0], y1, a1, w_up[0], w_down[0], 0, after=token1)
    pending1, shared1 = reducer.finish(flight1, dh1, "l1")
    dw_out = matmul(ab, dh1, dims="tn", name="dw_out", out_dtype=bf16, after=shared1)
    flight0, token0 = reducer.start(
        [jnp.concatenate([dw_up0, _by_owner(dw_down0), _by_owner(dw_out)], axis=1)], "l0", direct=True)
    dab = matmul(dh1, w_out, dims="nt", name="mixer_dout", tm=2048, tn=1024, after=token0)
    dproj, dmix = mixer_bwd(proj, hstates, dab, mixer_prm)
    dw_in_t = matmul(dproj, y0, dims="tn", name="dw_in", out_dtype=bf16, tm=768, tn=1024)
    pending0, shared0 = reducer.finish(flight0, dw_in_t, "l0")
    flight_in, token_in = reducer.start([_row_shards(dw_in_t, IN_SHARD, IN_PAD)], "in")
    dx, dg_mix0 = matmul_rows(dproj, w_in_p, dims="nt", name="dy_in", tm=256, epi=_norm_bwd_res,
                              epi_args=(("tile", x), ("row", mix_g[0]), ("tile", dh1)), outs=_DX_AND_DG,
                              after=token_in + shared0)
    pending_in, _ = reducer.finish(flight_in, dx, "in")
    r_l1, r_qkv = reducer.collect(pending1, dx, "l1")
    (r_l0,) = reducer.collect(pending0, dx, "l0")
    (r_in,) = reducer.collect(pending_in, dx, "in")
    reduced = {
        "w_out_even": r_l0[None, 2048:], "w_in_even": r_in[:IN_SHARD].T[None], "w_qkv": r_qkv[None, :, :QKV_SHARD],
        "w_o": r_l1[None, :256], "w_up": jnp.stack([r_l0[:1024], r_l1[256:1280]]),
        "w_down": jnp.stack([r_l0[1024:2048], r_l1[1280:]]),
    }

    small_grads = {
        "norm_mix_g": jnp.concatenate([dg_mix0, dg_mix1], axis=0),
        "norm_mlp_g": jnp.concatenate([dg_mlp0, dg_mlp1], axis=0),
        "final_norm_g": dg_final[0], "gm_ln_g": dmix["ln_g"], "gm_ln_b": dmix["ln_b"],
        "gm_w_s": dmix["wm"][None], "gm_b_s": dmix["bs_t"][:, :N_BLK].T[None],
        "ssm_conv_b": dmix["conv_b"], "ssm_dt_bias": dmix["dt_bias"][:, :SSM_HEADS],
        "ssm_a_log": dmix["a_log"][:, :SSM_HEADS], "ssm_d": dmix["d_heads"][:, :SSM_HEADS],
        "ssm_norm_g": dmix["norm_g"], "attn_sinks": dsink[:, :SSM_HEADS],
        "ssm_conv_w": dmix["conv_w"][None, :4], "b_qkv": db_qkv, "b_o": db_o,
    }
    return loss, dx, reduced, small_grads


def kernel(x, norm_mix_g, norm_mlp_g, final_norm_g, w_in_even, w_out_even, gm_ln_g, gm_ln_b, gm_w_s, gm_b_s, ssm_conv_w, ssm_conv_b, ssm_dt_bias, ssm_a_log, ssm_d, ssm_norm_g, w_qkv, b_qkv, w_o, b_o, attn_sinks, w_up, w_down, loss_target, m_norm_mix_g, m_norm_mlp_g, m_final_norm_g, m_w_in_even, m_w_out_even, m_gm_ln_g, m_gm_ln_b, m_gm_w_s, m_gm_b_s, m_ssm_conv_w, m_ssm_conv_b, m_ssm_dt_bias, m_ssm_a_log, m_ssm_d, m_ssm_norm_g, m_w_qkv, m_b_qkv, m_w_o, m_b_o, m_attn_sinks, m_w_up, m_w_down, v_norm_mix_g, v_norm_mlp_g, v_final_norm_g, v_w_in_even, v_w_out_even, v_gm_ln_g, v_gm_ln_b, v_gm_w_s, v_gm_b_s, v_ssm_conv_w, v_ssm_conv_b, v_ssm_dt_bias, v_ssm_a_log, v_ssm_d, v_ssm_norm_g, v_w_qkv, v_b_qkv, v_w_o, v_b_o, v_attn_sinks, v_w_up, v_w_down):
    w = dict(norm_mix_g=norm_mix_g, norm_mlp_g=norm_mlp_g, final_norm_g=final_norm_g, w_in_even=w_in_even,
             w_out_even=w_out_even, gm_ln_g=gm_ln_g, gm_ln_b=gm_ln_b, gm_w_s=gm_w_s, gm_b_s=gm_b_s,
             ssm_conv_w=ssm_conv_w, ssm_conv_b=ssm_conv_b, ssm_dt_bias=ssm_dt_bias, ssm_a_log=ssm_a_log,
             ssm_d=ssm_d, ssm_norm_g=ssm_norm_g, w_qkv=w_qkv, b_qkv=b_qkv, w_o=w_o, b_o=b_o,
             attn_sinks=attn_sinks, w_up=w_up, w_down=w_down)
    m = dict(norm_mix_g=m_norm_mix_g, norm_mlp_g=m_norm_mlp_g, final_norm_g=m_final_norm_g,
             w_in_even=m_w_in_even, w_out_even=m_w_out_even, gm_ln_g=m_gm_ln_g, gm_ln_b=m_gm_ln_b,
             gm_w_s=m_gm_w_s, gm_b_s=m_gm_b_s, ssm_conv_w=m_ssm_conv_w, ssm_conv_b=m_ssm_conv_b,
             ssm_dt_bias=m_ssm_dt_bias, ssm_a_log=m_ssm_a_log, ssm_d=m_ssm_d, ssm_norm_g=m_ssm_norm_g,
             w_qkv=m_w_qkv, b_qkv=m_b_qkv, w_o=m_w_o, b_o=m_b_o, attn_sinks=m_attn_sinks, w_up=m_w_up,
             w_down=m_w_down)
    v = dict(norm_mix_g=v_norm_mix_g, norm_mlp_g=v_norm_mlp_g, final_norm_g=v_final_norm_g,
             w_in_even=v_w_in_even, w_out_even=v_w_out_even, gm_ln_g=v_gm_ln_g, gm_ln_b=v_gm_ln_b,
             gm_w_s=v_gm_w_s, gm_b_s=v_gm_b_s, ssm_conv_w=v_ssm_conv_w, ssm_conv_b=v_ssm_conv_b,
             ssm_dt_bias=v_ssm_dt_bias, ssm_a_log=v_ssm_a_log, ssm_d=v_ssm_d, ssm_norm_g=v_ssm_norm_g,
             w_qkv=v_w_qkv, b_qkv=v_b_qkv, w_o=v_w_o, b_o=v_b_o, attn_sinks=v_attn_sinks, w_up=v_w_up,
             w_down=v_w_down)
    names = ("norm_mix_g", "norm_mlp_g", "final_norm_g", "w_in_even", "w_out_even", "gm_ln_g", "gm_ln_b",
             "gm_w_s", "gm_b_s", "ssm_conv_w", "ssm_conv_b", "ssm_dt_bias", "ssm_a_log", "ssm_d", "ssm_norm_g",
             "w_qkv", "b_qkv", "w_o", "b_o", "attn_sinks", "w_up", "w_down")

    cx, cy, cc = lax.axis_index("x"), lax.axis_index("y"), lax.axis_index("c")
    chip = 2 * cx + cy
    c_idx = jnp.reshape(cc, (1,)).astype(jnp.int32)
    chip_idx = jnp.reshape(chip, (1,)).astype(jnp.int32)

    weights = WeightGatherer(w, chip_idx)
    sm = {n: w[n] for n, _ in _SMALL_SHAPES[:_N_REPLICATED]}

    reducer = GradReducer(c_idx, jnp.concatenate([chip_idx, c_idx]))
    loss_part, dx, grads, small_grads = _local_step(x[0], loss_target[0], weights, sm, reducer)

    small_sum = allreduce_small(_pack([small_grads[n] for n, _ in _SMALL_SHAPES] + [loss_part], SMALL_ROWS))
    *small_list, loss_row = _unpack(small_sum, [s for _, s in _SMALL_SHAPES] + [loss_part.shape])
    loss = loss_row[0, 0]
    small_full = dict(zip([n for n, _ in _SMALL_SHAPES], small_list))
    for n, _ in _SMALL_SHAPES[:_N_REPLICATED]:
        grads[n] = small_full[n]
    for n, axis, width in _SHARDED_SMALL:
        grads[n] = lax.dynamic_slice_in_dim(small_full[n], chip * width, width, axis)
    grads = {n: grads[n].reshape(w[n].shape) for n in names}

    delta, new_m, new_v = {}, {}, {}
    for n in names:
        if n in _COLUMN_SHARDED:
            args = [jnp.transpose(d[n], (2, 0, 1)) for d in (w, grads, m, v)]
            grads[n] = jnp.transpose(args[1], (1, 2, 0))
            outs = adamw(*args, f"adamw_{n}")
            delta[n], new_m[n], new_v[n] = (jnp.transpose(o, (1, 2, 0)) for o in outs)
            continue
        shape = (1,) + w[n].shape if w[n].ndim == 1 else w[n].shape
        outs = adamw(*[d[n].reshape(shape) for d in (w, grads, m, v)], f"adamw_{n}")
        delta[n], new_m[n], new_v[n] = (o.reshape(w[n].shape) for o in outs)

    return (loss, dx[None], *[grads[n] for n in names], *[delta[n] for n in names],
            *[new_m[n] for n in names], *[new_v[n] for n in names])
```

```python
import functools

import jax
import jax.numpy as jnp
from jax import lax
from jax.experimental import pallas as pl
from jax.experimental.pallas import tpu as pltpu

f32 = jnp.float32
bf16 = jnp.bfloat16
MXU_DTYPE = bf16

RMS_EPS = 1e-5
LN_EPS = 1e-5
D_MODEL = 1024
D_FF = 4096
CH = 128
N_BLK = 8
SSM_HEADS = 16
IN_EVEN = 5136
NP_IN = 5376
OFF_U, OFF_V, OFF_Z, OFF_X, OFF_DT = 0, 1024, 2048, 3072, 5120
XBC_BLKS = 16
QKV_DIM = 1280
ATT_SCALE = 64 ** -0.5

ADAM_LR = 0.001
ADAM_B1 = 0.9
ADAM_B2 = 0.999
ADAM_EPS = 1e-08
ADAM_WD = 0.01
ADAM_STEP = 10

VMEM_LIMIT_BYTES = 48 * 1024 * 1024
N_CHIPS = 4
SMALL_ROWS = 256

NN = ((1,), (0,))
NT = ((1,), (1,))
TN = ((0,), (0,))


def _mm(a, b, dims):
    return lax.dot_general(a.astype(MXU_DTYPE), b.astype(MXU_DTYPE), (dims, ((), ())),
                           preferred_element_type=f32)


def _mm_exact(a, b):
    return jnp.dot(a, b, preferred_element_type=f32, precision=lax.Precision.HIGHEST)


def _cparams(sem=None):
    return pltpu.CompilerParams(dimension_semantics=sem, vmem_limit_bytes=VMEM_LIMIT_BYTES)


@jax.custom_vjp
def _swap64(x):
    return pltpu.roll(x, 64, axis=1)


_swap64.defvjp(lambda x: (pltpu.roll(x, 64, axis=1), None), lambda _, g: (pltpu.roll(g, 64, axis=1),))


def _row_blocks_of(x):
    return tuple(x[i:i + CH] for i in range(0, x.shape[0], CH))


@jax.custom_vjp
def _row_blocks(x):
    return _row_blocks_of(x)


_row_blocks.defvjp(lambda x: (_row_blocks_of(x), None), lambda _, gs: (jnp.concatenate(gs, axis=0),))


def _make_delay(k):
    @jax.custom_vjp
    def delay(ext):
        return pltpu.roll(ext, k, axis=0)[8:, :]

    def fwd(ext):
        return delay(ext), None

    def bwd(_, g):
        gp = jnp.concatenate([jnp.zeros((8, g.shape[1]), g.dtype), g], axis=0)
        return (pltpu.roll(gp, gp.shape[0] - k, axis=0),)

    delay.defvjp(fwd, bwd)
    return delay


_DELAYS = {k: _make_delay(k) for k in (1, 2, 3)}


_GELU_C = 0.7978845608028654
_GELU_K = 0.044715


@jax.custom_vjp
def _gelu(x):
    return 0.5 * x * (1.0 + jnp.tanh(_GELU_C * (x + _GELU_K * (x * x * x))))


def _gelu_fwd(x):
    t = jnp.tanh(_GELU_C * (x + _GELU_K * (x * x * x)))
    return 0.5 * x * (1.0 + t), (x, t)


def _gelu_bwd(res, g):
    x, t = res
    dz = _GELU_C + (3.0 * _GELU_C * _GELU_K) * (x * x)
    return (g * (0.5 * (1.0 + t) + (0.5 * x) * (1.0 - t * t) * dz),)


_gelu.defvjp(_gelu_fwd, _gelu_bwd)


def _col(m, lane, h):
    return jnp.sum(jnp.where(lane == h, m, 0.0), axis=1, keepdims=True)


@functools.lru_cache(maxsize=None)
def _row_picker(h, shape):
    @jax.custom_vjp
    def pick(m):
        return m[h:h + 1, :]

    def bwd(_, g):
        return (jnp.where(lax.broadcasted_iota(jnp.int32, shape, 0) == h, g, 0.0),)

    pick.defvjp(lambda m: (m[h:h + 1, :], None), bwd)
    return pick


def _row(m, sub, h):
    return _row_picker(h, m.shape)(m)


def _mixer_chunk(us, vs, zs, xbcs, halos, dtblk, hps, prm):
    lane = lax.broadcasted_iota(jnp.int32, (CH, CH), 1)
    sub = lax.broadcasted_iota(jnp.int32, (CH, CH), 0)
    left = lane < 64
    top = sub < 64
    causal = sub >= lane

    gus = [_gelu(u) for u in us]
    gvs = [_gelu(v) for v in vs]
    mu = sum(jnp.sum(g, axis=1, keepdims=True) for g in gvs) / D_MODEL
    cen = [g - mu for g in gvs]
    var = sum(jnp.sum(c * c, axis=1, keepdims=True) for c in cen) / D_MODEL
    rstd = lax.rsqrt(var + LN_EPS)
    a_out = []
    for g in range(N_BLK):
        vn = cen[g] * rstd * prm["ln_g"][g] + prm["ln_b"][g]
        w = jnp.where(causal, prm["wm"][g], 0.0)
        mixed = _mm(w, vn, NN) + _col(prm["bs_t"], lane, g)
        a_out.append(gus[g] * mixed)

    act = []
    for b in range(XBC_BLKS):
        w8 = prm["conv_w"][b]
        sub8 = lax.broadcasted_iota(jnp.int32, w8.shape, 0)
        ext = jnp.concatenate([halos[b], xbcs[b]], axis=0)
        conv = xbcs[b] * _row(w8, sub8, 3) + prm["conv_b"][b]
        for k in (1, 2, 3):
            conv = conv + _DELAYS[k](ext) * _row(w8, sub8, 3 - k)
        act.append(jax.nn.silu(conv))

    dt = jax.nn.softplus(dtblk + prm["dt_bias"])
    a_neg = -jnp.exp(prm["a_log"])
    tri = causal.astype(f32)
    acum = _mm_exact(tri, dt * a_neg)
    acum_t = acum.T
    dt_t = dt.T
    last = sub == CH - 1
    ys, h_out = [], []
    for grp in range(4):
        bm = act[8 + grp]
        cm = act[12 + grp]
        cb = _mm(cm, bm, NT)
        for p in (2 * grp, 2 * grp + 1):
            h0, h1 = 2 * p, 2 * p + 1
            xp = act[p]
            hp = hps[p]
            wis = []
            for h in (h0, h1):
                seg = _col(acum, lane, h) - _row(acum_t, sub, h)
                decay = jnp.exp(jnp.where(causal, seg, -jnp.inf))
                wis.append(cb * decay * _row(dt_t, sub, h))
            wcat = jnp.concatenate(wis, axis=1)
            xbd = jnp.concatenate([jnp.where(left, xp, 0.0), jnp.where(left, 0.0, xp)], axis=0)
            y_diag = _mm(wcat, xbd, NN)
            a_end = [jnp.sum(jnp.where(last & (lane == h), acum, 0.0), keepdims=True) for h in (h0, h1)]
            a_col = jnp.where(left, _col(acum, lane, h0), _col(acum, lane, h1))
            dt_col = jnp.where(left, _col(dt, lane, h0), _col(dt, lane, h1))
            to_end = jnp.exp(jnp.where(left, a_end[0], a_end[1]) - a_col) * dt_col
            states = _mm(xp * to_end, bm, TN)
            chunk_decay = jnp.where(top, jnp.exp(a_end[0]), jnp.exp(a_end[1]))
            h_out.append(chunk_decay * hp + states)
            y_off = jnp.exp(a_col) * _mm(cm, hp, NT)
            d_skip = jnp.where(left[:1], _col(prm["d_heads"], lane[:1], h0), _col(prm["d_heads"], lane[:1], h1))
            ys.append((y_diag + y_off + xp * d_skip) * jax.nn.silu(zs[p]))

    b_out = []
    for grp in range(4):
        pair = (ys[2 * grp], ys[2 * grp + 1])
        ms = sum(jnp.sum(y * y, axis=1, keepdims=True) for y in pair) / 256.0
        r = lax.rsqrt(ms + RMS_EPS)
        for j, y in enumerate(pair):
            b_out.append(y * r * prm["norm_g"][2 * grp + j])
    return a_out, b_out, h_out


def _attn_block(qps, kprev, kcur, vprev, vcur, sink_row, first):
    lane = lax.broadcasted_iota(jnp.int32, (CH, CH), 1)
    left = lane < 64
    own = lane <= lax.broadcasted_iota(jnp.int32, (CH, CH), 0)
    own8 = jnp.concatenate([own] * N_BLK, axis=0)

    def both_halves(a):
        sw = _swap64(a)
        return [jnp.where(left, a, sw), jnp.where(left, sw, a)]

    kc, kp, vc, vp = both_halves(kcur), both_halves(kprev), both_halves(vcur), both_halves(vprev)
    outs = []
    for j in range(2):
        q8 = jnp.concatenate([part for p in range(4 * j, 4 * j + 4)
                              for part in (jnp.where(left, qps[p], 0.0), jnp.where(left, 0.0, qps[p]))], axis=0)
        s_cur = _row_blocks(_mm(q8, kc[j], NT))
        s_prev = _row_blocks(_mm(q8, kp[j], NT))
        probs = []
        for h in range(N_BLK):
            s = jnp.where(own, s_cur[h] * ATT_SCALE, jnp.where(first, -jnp.inf, s_prev[h] * ATT_SCALE))
            sink = _col(sink_row, lane[:1], N_BLK * j + h)
            m = lax.stop_gradient(jnp.maximum(jnp.max(s, axis=1, keepdims=True), sink))
            pexp = jnp.exp(s - m)
            probs.append(pexp / (jnp.sum(pexp, axis=1, keepdims=True) + jnp.exp(sink - m)))
        p8 = jnp.concatenate(probs, axis=0)
        o = _row_blocks(_mm(jnp.where(own8, p8, 0.0), vc[j], NN) + _mm(jnp.where(own8, 0.0, p8), vp[j], NN))
        for t in range(4):
            outs.append(jnp.where(left, o[2 * t], o[2 * t + 1]))
    return outs


def _rmsnorm(x, g):
    r = lax.rsqrt(jnp.mean(x * x, axis=-1, keepdims=True) + RMS_EPS)
    return x * r * g


def rmsnorm_fwd(x, g_row, name):
    s, d = x.shape
    tm = min(512, s)

    def body(x_ref, g_ref, y_ref):
        y_ref[...] = _rmsnorm(x_ref[...], g_ref[...]).astype(bf16)

    return pl.pallas_call(
        body, name=name, grid=(s // tm,),
        in_specs=[pl.BlockSpec((tm, d), lambda i: (i, 0)), pl.BlockSpec((1, d), lambda i: (0, 0))],
        out_specs=pl.BlockSpec((tm, d), lambda i: (i, 0)),
        out_shape=jax.ShapeDtypeStruct((s, d), bf16),
        compiler_params=_cparams(("parallel",)),
    )(x, g_row)


def _fit(dim, want):
    if dim <= want:
        return dim
    t = want
    while dim % t:
        t -= 128
    return t


def matmul(a, b, *, dims, name, out_dtype=f32, tm=1024, tn=512, tk=8192, a_pro=None, epi=None, epi_args=(),
           out_by_col_tile=False, after=None):
    if dims == "nn" and b.ndim == 3:
        (m, k), n, tn = a.shape, b.shape[0] * b.shape[2], b.shape[2]
    elif dims == "nn":
        (m, k), n = a.shape, b.shape[1]
    elif dims == "nt":
        (m, k), n = a.shape, b.shape[0]
    else:
        (k, m), n = a.shape, b.shape[1]
    tm, tn, tk = _fit(m, tm), _fit(n, tn), _fit(k, tk)
    nk = k // tk
    if dims == "nn":
        a_spec = pl.BlockSpec((tm, tk), lambda i, j, kk: (i, kk))
        b_spec = (pl.BlockSpec((None, tk, tn), lambda i, j, kk: (j, kk, 0)) if b.ndim == 3
                  else pl.BlockSpec((tk, tn), lambda i, j, kk: (kk, j)))
        dn = NN
    elif dims == "nt":
        a_spec = pl.BlockSpec((tm, tk), lambda i, j, kk: (i, kk))
        b_spec = pl.BlockSpec((tn, tk), lambda i, j, kk: (j, kk))
        dn = NT
    else:
        a_spec = pl.BlockSpec((tk, tm), lambda i, j, kk: (kk, i))
        b_spec = pl.BlockSpec((tk, tn), lambda i, j, kk: (kk, j))
        dn = TN
    e_specs = [pl.BlockSpec((tm, tn), lambda i, j, kk: (i, j)) if kind == "tile"
               else pl.BlockSpec((1, tn), lambda i, j, kk: (0, j)) for kind, _ in epi_args]
    n_epi = len(epi_args)
    order_specs = [] if after is None else [pl.BlockSpec((8, 128), lambda i, j, kk: (0, 0))]
    order_args = [] if after is None else [after]

    def body(*refs):
        a_ref, b_ref = refs[0], refs[1]
        e_refs = refs[2:2 + n_epi]
        n_in = 2 + n_epi + len(order_args)
        o_ref = refs[n_in]
        av = a_ref[...]
        if a_pro is not None:
            av = a_pro(av)
        part = _mm(av, b_ref[...], dn)

        def finish(acc):
            if epi is not None:
                acc = epi(acc, *[r[...] for r in e_refs])
            o_ref[...] = acc.astype(out_dtype)

        if nk == 1:
            finish(part)
        else:
            acc_ref = refs[n_in + 1]
            kk = pl.program_id(2)

            @pl.when(kk == 0)
            def _():
                acc_ref[...] = part

            @pl.when(kk > 0)
            def _():
                acc_ref[...] += part

            @pl.when(kk == nk - 1)
            def _():
                finish(acc_ref[...])

    if out_by_col_tile:
        out_spec = pl.BlockSpec((None, tm, tn), lambda i, j, kk: (j, i, 0))
        out_shape = jax.ShapeDtypeStruct((n // tn, m, tn), out_dtype)
    else:
        out_spec = pl.BlockSpec((tm, tn), lambda i, j, kk: (i, j))
        out_shape = jax.ShapeDtypeStruct((m, n), out_dtype)
    return pl.pallas_call(
        body, name=name, grid=(m // tm, n // tn, nk),
        in_specs=[a_spec, b_spec] + e_specs + order_specs,
        out_specs=out_spec,
        out_shape=out_shape,
        scratch_shapes=[pltpu.VMEM((tm, tn), f32)] if nk > 1 else [],
        compiler_params=_cparams(("parallel", "parallel", "arbitrary")),
    )(a, b, *[arr for _, arr in epi_args], *order_args)


def _relu2(a):
    r = jnp.maximum(a.astype(f32), 0.0)
    return r * r


def _add(acc, t):
    return acc + t


def _add_bias(acc, t):
    return acc + t


def _add_bias_res(acc, bias, res):
    return acc + bias + res


def _times_relu2_grad(acc, a):
    return acc * (2.0 * jnp.maximum(a.astype(f32), 0.0))


def matmul_rows(a, b, *, dims, name, epi, epi_args, outs, tm=512, a_pro=None, after=None):
    m, k = a.shape
    n = b.shape[-1] if dims == "nn" else b.shape[-2]
    tm = _fit(m, tm)
    dn = NN if dims == "nn" else NT
    e_specs = [pl.BlockSpec((tm, arr.shape[1]), lambda i: (i, 0)) if kind == "tile"
               else pl.BlockSpec((1, arr.shape[1]), lambda i: (0, 0)) for kind, arr in epi_args]
    order_specs = [] if after is None else [pl.BlockSpec((8, 128), lambda i: (0, 0))]
    order_args = [] if after is None else [after]
    n_in = 2 + len(epi_args) + len(order_args)

    def body(*refs):
        av = refs[0][...]
        if a_pro is not None:
            av = a_pro(av)
        if b.ndim == 3:
            kb = b.shape[2]
            acc = sum(_mm(av[:, s * kb:(s + 1) * kb], refs[1][s], dn) for s in range(b.shape[0]))
        else:
            acc = _mm(av, refs[1][...], dn)
        vals = epi(acc, *[r[...] for r in refs[2:2 + len(epi_args)]])
        for (kind, _), o_ref, val in zip(outs, refs[n_in:], vals):
            if kind == "tile":
                o_ref[...] = val.astype(o_ref.dtype)
            else:
                @pl.when(pl.program_id(0) == 0)
                def _():
                    o_ref[...] = jnp.zeros_like(o_ref)

                o_ref[...] += val

    out_specs = [pl.BlockSpec((tm, n), lambda i: (i, 0)) if kind == "tile" else pl.BlockSpec((1, arg), lambda i: (0, 0))
                 for kind, arg in outs]
    out_shape = [jax.ShapeDtypeStruct((m, n), arg) if kind == "tile" else jax.ShapeDtypeStruct((1, arg), f32)
                 for kind, arg in outs]
    return pl.pallas_call(
        body, name=name, grid=(m // tm,),
        in_specs=[pl.BlockSpec((tm, k), lambda i: (i, 0)), pl.BlockSpec(b.shape, lambda i: (0,) * b.ndim)]
                 + e_specs + order_specs,
        out_specs=out_specs, out_shape=out_shape,
        compiler_params=_cparams(("arbitrary",)),
    )(a, b, *[arr for _, arr in epi_args], *order_args)


def _res_norm(acc, res, g):
    h = acc + res
    return h, _rmsnorm(h, g)


def _bias_res_norm(acc, bias, res, g):
    h = acc + bias + res
    return h, _rmsnorm(h, g)


def _res_norm_loss(acc, res, g, target):
    def f(h, gv):
        err = jnp.square(_rmsnorm(h, gv) - target)
        return 0.5 * jnp.sum(jnp.mean(err, axis=-1, keepdims=True), axis=0, keepdims=True)

    loss, vjp = jax.vjp(f, acc + res, g)
    dh, dg = vjp(jnp.ones_like(loss))
    return dh, dg, jnp.broadcast_to(loss, (1, 128))


def _norm_bwd_res_colsum(dy, x, g, res):
    dx, dg = _norm_bwd_res(dy, x, g, res)
    return dx, dg, jnp.sum(dx, axis=0, keepdims=True)


def _norm_bwd_res(dy, x, g, res):
    _, vjp = jax.vjp(_rmsnorm, x, g)
    dx, dg = vjp(dy)
    return res + dx, dg


_MIXER_PARAM_SHAPES = (
    ("ln_g", (1, D_MODEL)), ("ln_b", (1, D_MODEL)), ("wm", (N_BLK, CH, CH)), ("bs_t", (CH, CH)),
    ("conv_w", (8, 2048)), ("conv_b", (1, 2048)), ("dt_bias", (1, CH)), ("a_log", (1, CH)),
    ("d_heads", (1, CH)), ("norm_g", (1, D_MODEL)),
)


def _blocks(v, n, off=0):
    return [v[:, off + i * CH: off + (i + 1) * CH] for i in range(n)]


def _split_mixer_params(vals):
    p = dict(vals)
    return {
        "ln_g": _blocks(p["ln_g"], N_BLK), "ln_b": _blocks(p["ln_b"], N_BLK),
        "wm": [p["wm"][g] for g in range(N_BLK)], "bs_t": p["bs_t"],
        "conv_w": _blocks(p["conv_w"], XBC_BLKS), "conv_b": _blocks(p["conv_b"], XBC_BLKS),
        "dt_bias": p["dt_bias"], "a_log": p["a_log"], "d_heads": p["d_heads"],
        "norm_g": _blocks(p["norm_g"], N_BLK),
    }


def _mixer_leaves(proj_ref, halo_ref, keep_halo):
    pv = proj_ref
    us = [pv[:, OFF_U + i * CH: OFF_U + (i + 1) * CH] for i in range(N_BLK)]
    vs = [pv[:, OFF_V + i * CH: OFF_V + (i + 1) * CH] for i in range(N_BLK)]
    zs = [pv[:, OFF_Z + i * CH: OFF_Z + (i + 1) * CH] for i in range(N_BLK)]
    xbcs = [pv[:, OFF_X + i * CH: OFF_X + (i + 1) * CH] for i in range(XBC_BLKS)]
    halos = [halo_ref[:, OFF_X + i * CH: OFF_X + (i + 1) * CH] * keep_halo for i in range(XBC_BLKS)]
    dtblk = pv[:, OFF_DT: OFF_DT + CH]
    return us, vs, zs, xbcs, halos, dtblk


def mixer_fwd(proj, prm):
    s = proj.shape[0]
    nc = s // CH
    names = [n for n, _ in _MIXER_PARAM_SHAPES]

    def body(proj_ref, halo_ref, *rest):
        p_refs = rest[:len(names)]
        ab_ref, hs_ref, h_ref = rest[len(names):]
        c = pl.program_id(0)

        @pl.when(c == 0)
        def _():
            h_ref[...] = jnp.zeros_like(h_ref)

        hs_ref[...] = h_ref[...]
        keep = (c > 0).astype(f32)
        us, vs, zs, xbcs, halos, dtblk = _mixer_leaves(proj_ref, halo_ref, keep)
        hps = [h_ref[i * CH:(i + 1) * CH, :] for i in range(N_BLK)]
        p = _split_mixer_params({n: r[...] for n, r in zip(names, p_refs)})
        a_out, b_out, h_out = _mixer_chunk(us, vs, zs, xbcs, halos, dtblk, hps, p)
        for i in range(N_BLK):
            ab_ref[:, i * CH:(i + 1) * CH] = a_out[i].astype(bf16)
            ab_ref[:, D_MODEL + i * CH: D_MODEL + (i + 1) * CH] = b_out[i].astype(bf16)
            h_ref[i * CH:(i + 1) * CH, :] = h_out[i]

    def const(shape):
        return pl.BlockSpec(shape, lambda c: (0,) * len(shape))

    return pl.pallas_call(
        body, name="mixer_fwd", grid=(nc,),
        in_specs=[pl.BlockSpec((CH, NP_IN), lambda c: (c, 0)),
                  pl.BlockSpec((8, NP_IN), lambda c: (jnp.maximum(c * (CH // 8) - 1, 0), 0))]
                 + [const(shp) for _, shp in _MIXER_PARAM_SHAPES],
        out_specs=[pl.BlockSpec((CH, 2 * D_MODEL), lambda c: (c, 0)),
                   pl.BlockSpec((None, D_MODEL, CH), lambda c: (c, 0, 0))],
        out_shape=[jax.ShapeDtypeStruct((s, 2 * D_MODEL), bf16), jax.ShapeDtypeStruct((nc, D_MODEL, CH), f32)],
        scratch_shapes=[pltpu.VMEM((D_MODEL, CH), f32)],
        compiler_params=_cparams(("arbitrary",)),
    )(proj, proj, *[prm[n] for n in names])


def mixer_bwd(proj, hstates, dab, prm):
    s = proj.shape[0]
    nc = s // CH
    names = [n for n, _ in _MIXER_PARAM_SHAPES]
    npar = len(names)

    def body(proj_ref, halo_ref, hs_ref, dab_ref, *rest):
        p_refs = rest[:npar]
        dproj_ref = rest[npar]
        g_refs = rest[npar + 1: 2 * npar + 1]
        dh_ref, dhalo_ref = rest[2 * npar + 1:]
        i = pl.program_id(0)
        c = nc - 1 - i

        @pl.when(i == 0)
        def _():
            dh_ref[...] = jnp.zeros_like(dh_ref)
            dhalo_ref[...] = jnp.zeros_like(dhalo_ref)
            for r in g_refs:
                r[...] = jnp.zeros_like(r)

        keep = (c > 0).astype(f32)
        us, vs, zs, xbcs, halos, dtblk = _mixer_leaves(proj_ref, halo_ref, keep)
        hps = [hs_ref[j * CH:(j + 1) * CH, :] for j in range(N_BLK)]
        pvals = {n: r[...] for n, r in zip(names, p_refs)}

        def fn(us, vs, zs, xbcs, halos, dtblk, hps, pvals):
            return _mixer_chunk(us, vs, zs, xbcs, halos, dtblk, hps, _split_mixer_params(pvals))

        _, vjp = jax.vjp(fn, us, vs, zs, xbcs, halos, dtblk, hps, pvals)
        da = [dab_ref[:, j * CH:(j + 1) * CH].astype(f32) for j in range(N_BLK)]
        db = [dab_ref[:, D_MODEL + j * CH: D_MODEL + (j + 1) * CH].astype(f32) for j in range(N_BLK)]
        dh = [dh_ref[j * CH:(j + 1) * CH, :] for j in range(N_BLK)]
        dus, dvs, dzs, dxbcs, dhalos, ddt, dhps, dp = vjp((da, db, dh))

        for j in range(N_BLK):
            dproj_ref[:, OFF_U + j * CH: OFF_U + (j + 1) * CH] = dus[j].astype(bf16)
            dproj_ref[:, OFF_V + j * CH: OFF_V + (j + 1) * CH] = dvs[j].astype(bf16)
            dproj_ref[:, OFF_Z + j * CH: OFF_Z + (j + 1) * CH] = dzs[j].astype(bf16)
            dh_ref[j * CH:(j + 1) * CH, :] = dhps[j]
        zeros_top = jnp.zeros((CH - 8, CH), f32)
        for j in range(XBC_BLKS):
            late = jnp.concatenate([zeros_top, dhalo_ref[:, j * CH:(j + 1) * CH]], axis=0)
            dproj_ref[:, OFF_X + j * CH: OFF_X + (j + 1) * CH] = (dxbcs[j] + late).astype(bf16)
        for j in range(XBC_BLKS):
            dhalo_ref[:, j * CH:(j + 1) * CH] = dhalos[j] * keep
        lane = lax.broadcasted_iota(jnp.int32, (CH, CH), 1)
        dproj_ref[:, OFF_DT: OFF_DT + CH] = jnp.where(lane < SSM_HEADS, ddt, 0.0).astype(bf16)
        dproj_ref[:, OFF_DT + CH:] = jnp.zeros((CH, NP_IN - OFF_DT - CH), bf16)
        for n, r in zip(names, g_refs):
            r[...] += dp[n]

    def const(shape):
        return pl.BlockSpec(shape, lambda i: (0,) * len(shape))

    outs = pl.pallas_call(
        body, name="mixer_bwd", grid=(nc,),
        in_specs=[pl.BlockSpec((CH, NP_IN), lambda i: (nc - 1 - i, 0)),
                  pl.BlockSpec((8, NP_IN), lambda i: (jnp.maximum((nc - 1 - i) * (CH // 8) - 1, 0), 0)),
                  pl.BlockSpec((None, D_MODEL, CH), lambda i: (nc - 1 - i, 0, 0)),
                  pl.BlockSpec((CH, 2 * D_MODEL), lambda i: (nc - 1 - i, 0))]
                 + [const(shp) for _, shp in _MIXER_PARAM_SHAPES],
        out_specs=[pl.BlockSpec((CH, NP_IN), lambda i: (nc - 1 - i, 0))]
                  + [const(shp) for _, shp in _MIXER_PARAM_SHAPES],
        out_shape=[jax.ShapeDtypeStruct((s, NP_IN), bf16)]
                  + [jax.ShapeDtypeStruct(shp, f32) for _, shp in _MIXER_PARAM_SHAPES],
        scratch_shapes=[pltpu.VMEM((D_MODEL, CH), f32), pltpu.VMEM((8, 2048), f32)],
        compiler_params=_cparams(("arbitrary",)),
    )(proj, proj, hstates, dab, *[prm[n] for n in names])
    return outs[0], dict(zip(names, outs[1:]))


_K_BLK = D_MODEL // CH
_V_BLK = _K_BLK + 1


def _attn_specs(rev, nb):
    def blk(i):
        return nb - 1 - i if rev else i

    q_spec = pl.BlockSpec((CH, D_MODEL), lambda i: (blk(i), 0))
    kv = lambda col, prev: pl.BlockSpec(
        (CH, CH), lambda i: (jnp.maximum(blk(i) - 1, 0) if prev else blk(i), col))
    return q_spec, [kv(_K_BLK, True), kv(_K_BLK, False), kv(_V_BLK, True), kv(_V_BLK, False)]


def attn_fwd(qkv, sink_row):
    s = qkv.shape[0]
    nb = s // CH

    def body(q_ref, kp_ref, kc_ref, vp_ref, vc_ref, sink_ref, o_ref):
        qps = [q_ref[:, p * CH:(p + 1) * CH] for p in range(N_BLK)]
        outs = _attn_block(qps, kp_ref[...], kc_ref[...], vp_ref[...], vc_ref[...], sink_ref[...],
                           pl.program_id(0) == 0)
        for p in range(N_BLK):
            o_ref[:, p * CH:(p + 1) * CH] = outs[p].astype(bf16)

    q_spec, kv_specs = _attn_specs(False, nb)
    return pl.pallas_call(
        body, name="attn_fwd", grid=(nb,),
        in_specs=[q_spec] + kv_specs + [pl.BlockSpec((1, CH), lambda i: (0, 0))],
        out_specs=pl.BlockSpec((CH, D_MODEL), lambda i: (i, 0)),
        out_shape=jax.ShapeDtypeStruct((s, D_MODEL), bf16),
        compiler_params=_cparams(("parallel",)),
    )(qkv, qkv, qkv, qkv, qkv, sink_row)


def attn_bwd(qkv, sink_row, dout):
    s = qkv.shape[0]
    nb = s // CH

    def body(q_ref, kp_ref, kc_ref, vp_ref, vc_ref, sink_ref, do_ref, dqkv_ref, dsink_ref, db_ref, carry_ref):
        i = pl.program_id(0)
        blk = nb - 1 - i

        @pl.when(i == 0)
        def _():
            dsink_ref[...] = jnp.zeros_like(dsink_ref)
            db_ref[...] = jnp.zeros_like(db_ref)
            carry_ref[...] = jnp.zeros_like(carry_ref)

        qps = [q_ref[:, p * CH:(p + 1) * CH] for p in range(N_BLK)]
        first = blk == 0
        _, vjp = jax.vjp(lambda *a: _attn_block(*a, first), qps, kp_ref[...], kc_ref[...], vp_ref[...],
                         vc_ref[...], sink_ref[...])
        dos = [do_ref[:, p * CH:(p + 1) * CH].astype(f32) for p in range(N_BLK)]
        dqs, dkp, dkc, dvp, dvc, dsink = vjp(dos)
        blocks = list(dqs) + [dkc + carry_ref[0], dvc + carry_ref[1]]
        for p, val in enumerate(blocks):
            dqkv_ref[:, p * CH:(p + 1) * CH] = val.astype(bf16)
            db_ref[:, p * CH:(p + 1) * CH] += jnp.sum(val, axis=0, keepdims=True)
        keep = jnp.logical_not(first).astype(f32)
        carry_ref[0] = dkp * keep
        carry_ref[1] = dvp * keep
        dsink_ref[...] += dsink

    q_spec, kv_specs = _attn_specs(True, nb)
    return pl.pallas_call(
        body, name="attn_bwd", grid=(nb,),
        in_specs=[q_spec] + kv_specs + [pl.BlockSpec((1, CH), lambda i: (0, 0)),
                                        pl.BlockSpec((CH, D_MODEL), lambda i: (nb - 1 - i, 0))],
        out_specs=[pl.BlockSpec((CH, QKV_DIM), lambda i: (nb - 1 - i, 0)), pl.BlockSpec((1, CH), lambda i: (0, 0)),
                   pl.BlockSpec((1, QKV_DIM), lambda i: (0, 0))],
        out_shape=[jax.ShapeDtypeStruct((s, QKV_DIM), bf16), jax.ShapeDtypeStruct((1, CH), f32),
                   jax.ShapeDtypeStruct((1, QKV_DIM), f32)],
        scratch_shapes=[pltpu.VMEM((2, CH, CH), f32)],
        compiler_params=_cparams(("arbitrary",)),
    )(qkv, qkv, qkv, qkv, qkv, sink_row, dout)


def adamw(w, g, m, v, name):
    def body(w_ref, g_ref, m_ref, v_ref, d_ref, nm_ref, nv_ref):
        gv = g_ref[...]
        nm = ADAM_B1 * m_ref[...] + (1.0 - ADAM_B1) * gv
        nv = ADAM_B2 * v_ref[...] + (1.0 - ADAM_B2) * jnp.square(gv)
        m_hat = nm / (1.0 - ADAM_B1 ** ADAM_STEP)
        v_hat = nv / (1.0 - ADAM_B2 ** ADAM_STEP)
        d_ref[...] = -ADAM_LR * (m_hat / (jnp.sqrt(v_hat) + ADAM_EPS) + ADAM_WD * w_ref[...])
        nm_ref[...] = nm
        nv_ref[...] = nv

    out_shape = [jax.ShapeDtypeStruct(w.shape, f32)] * 3
    if w.ndim == 3 and w.shape[1] == 1:
        tr = max(t for t in range(1, 129) if w.shape[0] % t == 0)
        tile = pl.BlockSpec((tr, 1, w.shape[2]), lambda i: (i, 0, 0))
        return pl.pallas_call(
            body, name=name, grid=(w.shape[0] // tr,),
            in_specs=[tile] * 4, out_specs=[tile] * 3, out_shape=out_shape,
            compiler_params=_cparams(("parallel",)),
        )(w, g, m, v)
    if w.ndim == 3 and w.shape[1] % 256 == 0:
        tile = pl.BlockSpec((None, 256, w.shape[2]), lambda l, i: (l, i, 0))
        return pl.pallas_call(
            body, name=name, grid=(w.shape[0], w.shape[1] // 256),
            in_specs=[tile] * 4, out_specs=[tile] * 3, out_shape=out_shape,
            compiler_params=_cparams(("parallel", "parallel")),
        )(w, g, m, v)
    return pl.pallas_call(body, name=name, in_specs=[_VMEM] * 4, out_specs=[_VMEM] * 3, out_shape=out_shape,
                          compiler_params=_cparams())(w, g, m, v)


_MESH = pl.DeviceIdType.MESH
_ANY = pl.BlockSpec(memory_space=pl.ANY)
_VMEM = pl.BlockSpec(memory_space=pltpu.VMEM)


def _place():
    x, y, c = lax.axis_index("x"), lax.axis_index("y"), lax.axis_index("c")
    chips = [(1 - x, y), (x, 1 - y), (1 - x, 1 - y)]
    return x, y, c, 2 * x + y, chips, [2 * cx + cy for cx, cy in chips]


def _half(c, rows):
    return pl.ds(pl.multiple_of(c * (rows // 2), 16), rows // 2)


def _step_rows(rows):
    return max(t for t in range(16, 641, 16) if rows % t == 0)


def place_shard(b, slot, name, dtype=bf16, after=None):
    r, c = b.shape
    tr = _step_rows(r)

    def body(slot_ref, b_ref, *rest):
        rest[-1][...] = b_ref[...].astype(dtype)

    order_specs = [] if after is None else [pl.BlockSpec((8, 128), lambda i, s: (0, 0))]
    return pl.pallas_call(
        body, name=name,
        grid_spec=pltpu.PrefetchScalarGridSpec(
            num_scalar_prefetch=1, grid=(r // tr,),
            in_specs=[pl.BlockSpec((tr, c), lambda i, s: (i, 0))] + order_specs,
            out_specs=pl.BlockSpec((None, tr, c), lambda i, s: (s[0], i, 0))),
        out_shape=jax.ShapeDtypeStruct((N_CHIPS, r, c), dtype),
        compiler_params=_cparams(("parallel",)),
    )(slot, b, *([] if after is None else [after]))


_HBM = pl.BlockSpec(memory_space=pltpu.HBM)
_SEM = pl.BlockSpec(memory_space=pltpu.SEMAPHORE)
_EFFECT = pltpu.SideEffectType.DATAFLOW_SIDE_EFFECTING


def _gather_ici_copies(bufs, send_sems, recv_sems):
    x, y, c, me, chips, chip_idx = _place()
    return [pltpu.make_async_remote_copy(
        src_ref=buf.at[me, _half(c, buf.shape[1])], dst_ref=buf.at[chip_idx[j], _half(c, buf.shape[1])],
        send_sem=send_sems.at[3 * k + j], recv_sem=recv_sems.at[3 * k + j],
        device_id=(*chips[j], c), device_id_type=_MESH) for j in range(3) for k, buf in enumerate(bufs)]


def gather_start(groups, tag):
    sizes = [len(g) for g in groups]
    flat = [b for g in groups for b in g]
    n = len(flat)

    def body(*refs):
        bufs, sems = refs[:n], refs[n:n + 2 * len(groups)]
        refs[-1][...] = jnp.zeros_like(refs[-1])
        x, y, c, me, chips, chip_idx = _place()
        lo = 0
        for gi, size in enumerate(sizes):
            for j in range(3):
                for k, buf in enumerate(bufs[lo:lo + size]):
                    mine = buf.at[me, _half(c, buf.shape[1])]
                    pltpu.make_async_remote_copy(
                        src_ref=mine, dst_ref=mine, send_sem=sems[2 * gi].at[3 * k + j],
                        recv_sem=sems[2 * gi + 1].at[3 * k + j], device_id=(*chips[j], c),
                        device_id_type=_MESH).start()
            lo += size

    sem_shapes = [pltpu.SemaphoreType.DMA((3 * size,)) for size in sizes for _ in range(2)]
    outs = pl.pallas_call(
        body, name=f"gather_start_{tag}",
        out_shape=(*sem_shapes, *[pltpu.HBM(b.shape, b.dtype) for b in flat], jax.ShapeDtypeStruct((8, 128), f32)),
        in_specs=[_HBM] * n, out_specs=(*[_SEM] * len(sem_shapes), *[_HBM] * n, _VMEM),
        input_output_aliases={i: len(sem_shapes) + i for i in range(n)},
        compiler_params=pltpu.CompilerParams(has_side_effects=_EFFECT),
    )(*[pltpu.with_memory_space_constraint(b, pltpu.HBM) for b in flat])
    sems = [(outs[2 * gi], outs[2 * gi + 1]) for gi in range(len(groups))]
    thru, lo = [], len(sem_shapes)
    for size in sizes:
        thru.append(list(outs[lo:lo + size]))
        lo += size
    return sems, thru, outs[-1]


def gather_wait(bufs, sems, after, tag):
    n = len(bufs)

    def body(*refs):
        for cp in _gather_ici_copies(refs[:n], refs[n], refs[n + 1]):
            cp.wait_send()
            cp.wait_recv()

    extra = list(after)
    return list(pl.pallas_call(
        body, name=f"gather_wait_{tag}",
        out_shape=[pltpu.HBM(b.shape, b.dtype) for b in bufs],
        in_specs=[_HBM] * n + [_SEM, _SEM] + [_ANY] * len(extra), out_specs=[_HBM] * n,
        input_output_aliases={i: i for i in range(n)},
        compiler_params=pltpu.CompilerParams(has_side_effects=_EFFECT),
    )(*bufs, *sems, *extra))


def gather_forward(bufs, tag):
    n = len(bufs)

    def body(*refs):
        out_refs = refs[n:2 * n]
        send_sems, recv_sems = refs[2 * n:]
        x, y, c, me, chips, chip_idx = _place()

        def copy(k, j, half):
            part = out_refs[k].at[chip_idx[j], _half(half, out_refs[k].shape[1])]
            return pltpu.make_async_remote_copy(
                src_ref=part, dst_ref=part, send_sem=send_sems.at[3 * k + j], recv_sem=recv_sems.at[3 * k + j],
                device_id=(x, y, 1 - c), device_id_type=_MESH)

        sends = [copy(k, j, c) for j in range(3) for k in range(n)]
        for cp in sends:
            cp.start()
        for j in range(3):
            for k in range(n):
                copy(k, j, 1 - c).wait_recv()
        for cp in sends:
            cp.wait_send()

    return list(pl.pallas_call(
        body, name=f"gather_forward_{tag}",
        out_shape=[jax.ShapeDtypeStruct(b.shape, b.dtype) for b in bufs],
        in_specs=[_ANY] * n, out_specs=[_ANY] * n, input_output_aliases={i: i for i in range(n)},
        scratch_shapes=[pltpu.SemaphoreType.DMA((3 * n,)), pltpu.SemaphoreType.DMA((3 * n,))],
    )(*bufs))


def _forward_copy(ref, k, j, half, send_sems, recv_sems):
    x, y, c, me, chips, chip_idx = _place()
    part = ref.at[chip_idx[j], _half(half, ref.shape[1])]
    return pltpu.make_async_remote_copy(
        src_ref=part, dst_ref=part, send_sem=send_sems.at[3 * k + j], recv_sem=recv_sems.at[3 * k + j],
        device_id=(x, y, 1 - c), device_id_type=_MESH)


def forward_start(bufs, tag):
    n = len(bufs)

    def body(*refs):
        c = _place()[2]
        for j in range(3):
            for k in range(n):
                _forward_copy(refs[k], k, j, c, refs[n], refs[n + 1]).start()
        refs[-1][...] = jnp.zeros_like(refs[-1])

    outs = pl.pallas_call(
        body, name=f"forward_start_{tag}",
        out_shape=(pltpu.SemaphoreType.DMA((3 * n,)), pltpu.SemaphoreType.DMA((3 * n,)),
                   *[pltpu.HBM(b.shape, b.dtype) for b in bufs], jax.ShapeDtypeStruct((8, 128), f32)),
        in_specs=[_HBM] * n, out_specs=(_SEM, _SEM, *[_HBM] * n, _VMEM),
        input_output_aliases={i: 2 + i for i in range(n)},
        compiler_params=pltpu.CompilerParams(has_side_effects=_EFFECT),
    )(*[pltpu.with_memory_space_constraint(b, pltpu.HBM) for b in bufs])
    return (outs[0], outs[1], list(outs[2:2 + n])), outs[-1]


def forward_wait(send_sems, recv_sems, bufs, after, tag):
    n = len(bufs)

    def body(*refs):
        c = _place()[2]
        for j in range(3):
            for k in range(n):
                _forward_copy(refs[k], k, j, c, refs[n], refs[n + 1]).wait_send()
                _forward_copy(refs[k], k, j, 1 - c, refs[n], refs[n + 1]).wait_recv()

    return list(pl.pallas_call(
        body, name=f"forward_wait_{tag}",
        out_shape=[pltpu.HBM(b.shape, b.dtype) for b in bufs],
        in_specs=[_HBM] * n + [_SEM, _SEM, _ANY], out_specs=[_HBM] * n,
        input_output_aliases={i: i for i in range(n)},
        compiler_params=pltpu.CompilerParams(has_side_effects=_EFFECT),
    )(*bufs, send_sems, recv_sems, after))


def exchange_halves(bufs, tag):
    n = len(bufs)

    def body(*refs):
        g_refs, out_refs = refs[:n], refs[n:2 * n]
        send_sems, recv_sems = refs[2 * n:]
        x, y, c, *_ = _place()
        cps = [pltpu.make_async_remote_copy(
            src_ref=g_refs[b].at[:, _half(1 - c, g_refs[b].shape[1])], dst_ref=out_refs[b],
            send_sem=send_sems.at[b], recv_sem=recv_sems.at[b], device_id=(x, y, 1 - c), device_id_type=_MESH)
            for b in range(n)]
        for cp in cps:
            cp.start()
        for cp in cps:
            cp.wait()

    return pl.pallas_call(
        body, name=f"exchange_halves_{tag}",
        out_shape=[jax.ShapeDtypeStruct((N_CHIPS, b.shape[1] // 2, b.shape[2]), b.dtype) for b in bufs],
        in_specs=[_ANY] * n, out_specs=[_ANY] * n,
        scratch_shapes=[pltpu.SemaphoreType.DMA((n,)), pltpu.SemaphoreType.DMA((n,))],
    )(*bufs)


def add_halves(g, got, c_idx, name):
    hr, cols = got.shape[1], got.shape[2]
    tr = _step_rows(hr)
    steps = hr // tr

    def body(c_ref, g_ref, got_ref, o_ref):
        o_ref[...] = (g_ref[...].astype(f32) + got_ref[...].astype(f32)).astype(bf16)

    return pl.pallas_call(
        body, name=name,
        grid_spec=pltpu.PrefetchScalarGridSpec(
            num_scalar_prefetch=1, grid=(N_CHIPS, steps),
            in_specs=[pl.BlockSpec((None, tr, cols), lambda s, i, c: (s, c[0] * steps + i, 0)),
                      pl.BlockSpec((None, tr, cols), lambda s, i, c: (s, i, 0))],
            out_specs=pl.BlockSpec((None, tr, cols), lambda s, i, c: (s, i, 0))),
        out_shape=jax.ShapeDtypeStruct(got.shape, bf16),
        compiler_params=_cparams(("parallel", "parallel")),
    )(c_idx, g, got)


def sum_chips(t, got, place_idx, name):
    hr, cols = t.shape[1], t.shape[2]
    tr = _step_rows(hr)
    steps = hr // tr

    def body(idx_ref, t_ref, got_ref, o_ref):
        acc = t_ref[...].astype(f32)
        for j in range(3):
            acc = acc + got_ref[j].astype(f32)
        o_ref[...] = acc

    return pl.pallas_call(
        body, name=name,
        grid_spec=pltpu.PrefetchScalarGridSpec(
            num_scalar_prefetch=1, grid=(steps,),
            in_specs=[pl.BlockSpec((None, tr, cols), lambda i, idx: (idx[0], i, 0)),
                      pl.BlockSpec((3, tr, cols), lambda i, idx: (0, i, 0))],
            out_specs=pl.BlockSpec((tr, cols), lambda i, idx: (idx[1] * steps + i, 0))),
        out_shape=jax.ShapeDtypeStruct((2 * hr, cols), f32),
        compiler_params=_cparams(("parallel",)),
    )(place_idx, t, got)


def _share_copies(refs, send_sems, recv_sems):
    x, y, c, *_ = _place()
    return [pltpu.make_async_remote_copy(
        src_ref=ref.at[_half(c, ref.shape[0])], dst_ref=ref.at[_half(c, ref.shape[0])], send_sem=send_sems.at[b],
        recv_sem=recv_sems.at[b], device_id=(x, y, 1 - c), device_id_type=_MESH) for b, ref in enumerate(refs)]


def share_start(bufs, tag):
    n = len(bufs)

    def body(*refs):
        for cp in _share_copies(refs[:n], refs[n], refs[n + 1]):
            cp.start()
        token = refs[-1]
        token[...] = jnp.zeros_like(token)

    outs = pl.pallas_call(
        body, name=f"share_start_{tag}",
        out_shape=(pltpu.SemaphoreType.DMA((n,)), pltpu.SemaphoreType.DMA((n,)),
                   *[pltpu.HBM(b.shape, b.dtype) for b in bufs], jax.ShapeDtypeStruct((8, 128), f32)),
        in_specs=[_HBM] * n, out_specs=(_SEM, _SEM, *[_HBM] * n, _VMEM),
        input_output_aliases={i: 2 + i for i in range(n)},
        compiler_params=pltpu.CompilerParams(has_side_effects=_EFFECT),
    )(*[pltpu.with_memory_space_constraint(b, pltpu.HBM) for b in bufs])
    return (outs[0], outs[1], list(outs[2:2 + n])), outs[-1]


def share_wait(send_sems, recv_sems, bufs, after, tag):
    n = len(bufs)

    def body(*refs):
        x, y, c, *_ = _place()
        for b, ref in enumerate(refs[:n]):
            cp = pltpu.make_async_remote_copy(
                src_ref=ref.at[_half(c, ref.shape[0])], dst_ref=ref.at[_half(1 - c, ref.shape[0])],
                send_sem=refs[n].at[b], recv_sem=refs[n + 1].at[b], device_id=(x, y, 1 - c), device_id_type=_MESH)
            cp.wait_send()
            cp.wait_recv()

    return list(pl.pallas_call(
        body, name=f"share_wait_{tag}",
        out_shape=[pltpu.HBM(b.shape, b.dtype) for b in bufs],
        in_specs=[_HBM] * n + [_SEM, _SEM, _ANY], out_specs=[_HBM] * n,
        input_output_aliases={i: i for i in range(n)},
        compiler_params=pltpu.CompilerParams(has_side_effects=_EFFECT),
    )(*bufs, send_sems, recv_sems, after))


def _scatter_copies(t_refs, land_refs, send_sems, recv_sems):
    x, y, c, me, chips, chip_idx = _place()
    return [pltpu.make_async_remote_copy(
        src_ref=t_refs[b].at[chip_idx[j]], dst_ref=land_refs[b].at[j], send_sem=send_sems.at[3 * b + j],
        recv_sem=recv_sems.at[3 * b + j], device_id=(*chips[j], c), device_id_type=_MESH)
        for j in range(3) for b in range(len(t_refs))]


def scatter_start(ts, tag):
    n = len(ts)
    lands = [lax.empty((3,) + t.shape[1:], t.dtype) for t in ts]

    def body(*refs):
        for cp in _scatter_copies(refs[:n], refs[n:2 * n], refs[2 * n], refs[2 * n + 1]):
            cp.start()
        token = refs[-1]
        token[...] = jnp.zeros_like(token)

    hbm = [pltpu.HBM(a.shape, a.dtype) for a in (*ts, *lands)]
    outs = pl.pallas_call(
        body, name=f"scatter_start_{tag}",
        out_shape=(pltpu.SemaphoreType.DMA((3 * n,)), pltpu.SemaphoreType.DMA((3 * n,)), *hbm,
                   jax.ShapeDtypeStruct((8, 128), f32)),
        in_specs=[_HBM] * (2 * n), out_specs=(_SEM, _SEM, *[_HBM] * (2 * n), _VMEM),
        input_output_aliases={i: 2 + i for i in range(2 * n)},
        compiler_params=pltpu.CompilerParams(has_side_effects=_EFFECT),
    )(*[pltpu.with_memory_space_constraint(a, pltpu.HBM) for a in (*ts, *lands)])
    return outs[0], outs[1], list(outs[2:2 + n]), list(outs[2 + n:2 + 2 * n]), outs[-1]


def scatter_wait(send_sems, recv_sems, ts, lands, after, tag):
    n = len(ts)

    def body(*refs):
        for cp in _scatter_copies(refs[:n], refs[n:2 * n], refs[2 * n], refs[2 * n + 1]):
            cp.wait_send()
            cp.wait_recv()

    outs = pl.pallas_call(
        body, name=f"scatter_wait_{tag}",
        out_shape=[pltpu.HBM(a.shape, a.dtype) for a in (*ts, *lands)],
        in_specs=[_HBM] * (2 * n) + [_SEM, _SEM, _ANY], out_specs=[_HBM] * (2 * n),
        input_output_aliases={i: i for i in range(2 * n)},
        compiler_params=pltpu.CompilerParams(has_side_effects=_EFFECT),
    )(*ts, *lands, send_sems, recv_sems, after)
    return list(outs[:n]), list(outs[n:])


N_SENDERS = 7


def _direct_copies(g_refs, land_refs, send_sems, recv_sems):
    x, y, c, me, chips, chip_idx = _place()
    cps = []
    for b, (g, land) in enumerate(zip(g_refs, land_refs)):
        rows, base = g.shape[1], N_SENDERS * b
        cps.append(pltpu.make_async_remote_copy(
            src_ref=g.at[me, _half(1 - c, rows)], dst_ref=land.at[0], send_sem=send_sems.at[base],
            recv_sem=recv_sems.at[base], device_id=(x, y, 1 - c), device_id_type=_MESH))
        for j in range(3):
            for core in range(2):
                cps.append(pltpu.make_async_remote_copy(
                    src_ref=g.at[chip_idx[j], _half(core, rows)], dst_ref=land.at[1 + 2 * j + c],
                    send_sem=send_sems.at[base + 1 + 2 * j + core], recv_sem=recv_sems.at[base + 1 + 2 * j + c],
                    device_id=(*chips[j], core), device_id_type=_MESH))
    return cps


def direct_start(gs, tag):
    n = len(gs)
    lands = [lax.empty((N_SENDERS, g.shape[1] // 2, g.shape[2]), g.dtype) for g in gs]

    def body(*refs):
        for cp in _direct_copies(refs[:n], refs[n:2 * n], refs[2 * n], refs[2 * n + 1]):
            cp.start()
        token = refs[-1]
        token[...] = jnp.zeros_like(token)

    hbm = [pltpu.HBM(a.shape, a.dtype) for a in (*gs, *lands)]
    outs = pl.pallas_call(
        body, name=f"direct_start_{tag}",
        out_shape=(pltpu.SemaphoreType.DMA((N_SENDERS * n,)), pltpu.SemaphoreType.DMA((N_SENDERS * n,)), *hbm,
                   jax.ShapeDtypeStruct((8, 128), f32)),
        in_specs=[_HBM] * (2 * n), out_specs=(_SEM, _SEM, *[_HBM] * (2 * n), _VMEM),
        input_output_aliases={i: 2 + i for i in range(2 * n)},
        compiler_params=pltpu.CompilerParams(has_side_effects=_EFFECT),
    )(*[pltpu.with_memory_space_constraint(a, pltpu.HBM) for a in (*gs, *lands)])
    return outs[0], outs[1], list(outs[2:2 + n]), list(outs[2 + n:2 + 2 * n]), outs[-1]


def direct_wait(send_sems, recv_sems, gs, lands, after, tag):
    n = len(gs)

    def body(*refs):
        g_refs, land_refs, sends, recvs = refs[:n], refs[n:2 * n], refs[2 * n], refs[2 * n + 1]
        for b in range(n):
            for k in range(N_SENDERS):
                cp = pltpu.make_async_remote_copy(
                    src_ref=g_refs[b].at[0, _half(0, g_refs[b].shape[1])], dst_ref=land_refs[b].at[k],
                    send_sem=sends.at[N_SENDERS * b + k], recv_sem=recvs.at[N_SENDERS * b + k],
                    device_id=_place()[:3], device_id_type=_MESH)
                cp.wait_send()
                cp.wait_recv()

    outs = pl.pallas_call(
        body, name=f"direct_wait_{tag}",
        out_shape=[pltpu.HBM(a.shape, a.dtype) for a in (*gs, *lands)],
        in_specs=[_HBM] * (2 * n) + [_SEM, _SEM, _ANY], out_specs=[_HBM] * (2 * n),
        input_output_aliases={i: i for i in range(2 * n)},
        compiler_params=pltpu.CompilerParams(has_side_effects=_EFFECT),
    )(*gs, *lands, send_sems, recv_sems, after)
    return list(outs[:n]), list(outs[n:])


def sum_senders(g, lands, place_idx, name):
    hr, cols = lands.shape[1], lands.shape[2]
    tr = _step_rows(hr)
    steps = hr // tr

    def body(idx_ref, g_ref, land_ref, o_ref):
        acc = g_ref[...].astype(f32)
        for k in range(N_SENDERS):
            acc = acc + land_ref[k].astype(f32)
        o_ref[...] = acc

    return pl.pallas_call(
        body, name=name,
        grid_spec=pltpu.PrefetchScalarGridSpec(
            num_scalar_prefetch=1, grid=(steps,),
            in_specs=[pl.BlockSpec((None, tr, cols), lambda i, idx: (idx[0], idx[1] * steps + i, 0)),
                      pl.BlockSpec((N_SENDERS, tr, cols), lambda i, idx: (0, i, 0))],
            out_specs=pl.BlockSpec((tr, cols), lambda i, idx: (idx[1] * steps + i, 0))),
        out_shape=jax.ShapeDtypeStruct((2 * hr, cols), f32),
        compiler_params=_cparams(("parallel",)),
    )(place_idx, g, lands)


class GradReducer:
    def __init__(self, c_idx, place_idx):
        self.c_idx, self.place_idx = c_idx, place_idx

    def start(self, bufs, tag, direct=False):
        if direct:
            send_sems, recv_sems, gs, lands, token = direct_start(bufs, tag)
            return (True, send_sems, recv_sems, gs, lands), token
        got = exchange_halves(bufs, tag)
        ts = [add_halves(b, g, self.c_idx, f"add_halves_{tag}{i}") for i, (b, g) in enumerate(zip(bufs, got))]
        send_sems, recv_sems, ts, lands, token = scatter_start(ts, tag)
        return (False, send_sems, recv_sems, ts, lands), token

    def finish(self, state, after, tag):
        direct, *flight = state
        if direct:
            gs, lands = direct_wait(*flight, after, tag)
            sums = [sum_senders(g, l, self.place_idx, f"sum_senders_{tag}{i}") for i, (g, l) in enumerate(zip(gs, lands))]
        else:
            ts, lands = scatter_wait(*flight, after, tag)
            sums = [sum_chips(t, l, self.place_idx, f"sum_chips_{tag}{i}") for i, (t, l) in enumerate(zip(ts, lands))]
        return share_start(sums, tag)

    def collect(self, pending, after, tag):
        return share_wait(*pending, after, tag)


def allreduce_small(sp):
    rows = sp.shape[0]
    hr = rows // 2

    def body(s_ref, out_ref, sib_ref, chip_ref, four_ref, send_sems, recv_sems):
        x, y, c, me, chips, chip_idx = _place()
        sibling = (x, y, 1 - c)
        mine = pl.ds(pl.multiple_of(c * hr, 8), hr)
        other = pl.ds(pl.multiple_of((1 - c) * hr, 8), hr)

        swap = pltpu.make_async_remote_copy(src_ref=s_ref, dst_ref=sib_ref, send_sem=send_sems.at[0],
                                            recv_sem=recv_sems.at[0], device_id=sibling, device_id_type=_MESH)
        swap.start()
        swap.wait()
        is_core0 = c == 0
        chip_ref[...] = jnp.where(is_core0, s_ref[...], sib_ref[...]) + jnp.where(is_core0, sib_ref[...], s_ref[...])

        sends = [pltpu.make_async_remote_copy(
            src_ref=chip_ref.at[mine], dst_ref=four_ref.at[me], send_sem=send_sems.at[1 + j],
            recv_sem=recv_sems.at[1 + j], device_id=(*chips[j], c), device_id_type=_MESH) for j in range(3)]
        for cp in sends:
            cp.start()
        four_ref[me] = chip_ref[mine, :]
        for j in range(3):
            pltpu.make_async_remote_copy(
                src_ref=chip_ref.at[mine], dst_ref=four_ref.at[chip_idx[j]], send_sem=send_sems.at[1 + j],
                recv_sem=recv_sems.at[1 + j], device_id=(*chips[j], c), device_id_type=_MESH).wait_recv()
        for cp in sends:
            cp.wait_send()
        out_ref[mine, :] = (four_ref[0] + four_ref[1]) + (four_ref[2] + four_ref[3])

        share = pltpu.make_async_remote_copy(src_ref=out_ref.at[mine], dst_ref=out_ref.at[mine], send_sem=send_sems.at[4],
                                             recv_sem=recv_sems.at[4], device_id=sibling, device_id_type=_MESH)
        share.start()
        pltpu.make_async_remote_copy(src_ref=out_ref.at[mine], dst_ref=out_ref.at[other], send_sem=send_sems.at[4],
                                     recv_sem=recv_sems.at[4], device_id=sibling, device_id_type=_MESH).wait_recv()
        share.wait_send()

    return pl.pallas_call(
        body, name="allreduce_small",
        out_shape=jax.ShapeDtypeStruct(sp.shape, sp.dtype),
        in_specs=[_VMEM], out_specs=_VMEM,
        scratch_shapes=[pltpu.VMEM(sp.shape, sp.dtype), pltpu.VMEM(sp.shape, sp.dtype),
                        pltpu.VMEM((N_CHIPS, hr, sp.shape[1]), sp.dtype),
                        pltpu.SemaphoreType.DMA((5,)), pltpu.SemaphoreType.DMA((5,))],
        compiler_params=_cparams(),
    )(sp)


def _n_rows(shape):
    n = 1
    for d in shape:
        n *= d
    return 8 * (-(-n // 8192))


def _pack(arrays, total_rows):
    parts = []
    for a in arrays:
        flat = a.reshape(-1)
        parts.append(jnp.pad(flat, (0, 1024 * _n_rows(a.shape) - flat.shape[0])).reshape(-1, 1024))
    rows = jnp.concatenate(parts, axis=0)
    return jnp.pad(rows, ((0, total_rows - rows.shape[0]), (0, 0)))


def _unpack(packed, shapes):
    out, r = [], 0
    for shp in shapes:
        n = 1
        for d in shp:
            n *= d
        nr = _n_rows(shp)
        out.append(packed[r:r + nr].reshape(-1)[:n].reshape(shp))
        r += nr
    return out


_COLUMN_SHARDED = ("w_in_even", "w_qkv")
IN_SHARD, IN_PAD = 1284, 1408
QKV_SHARD, QKV_PAD = 320, 384


def _lane_padded(a, cols):
    return jnp.pad(a, ((0, 0), (0, cols - a.shape[1])))


_SMALL_SHAPES = (
    ("norm_mix_g", (2, 1024)), ("norm_mlp_g", (2, 1024)), ("final_norm_g", (1024,)), ("gm_ln_g", (1, 1024)),
    ("gm_ln_b", (1, 1024)), ("gm_w_s", (1, 8, 128, 128)), ("gm_b_s", (1, 8, 128)), ("ssm_conv_b", (1, 2048)),
    ("ssm_dt_bias", (1, 16)), ("ssm_a_log", (1, 16)), ("ssm_d", (1, 16)), ("ssm_norm_g", (1, 1024)),
    ("attn_sinks", (1, 16)), ("ssm_conv_w", (1, 4, 2048)), ("b_qkv", (1, 1280)), ("b_o", (1, 1024)),
)
_N_REPLICATED = 13
_SHARDED_SMALL = (("ssm_conv_w", 2, 512), ("b_qkv", 1, 320), ("b_o", 1, 256))
_SHARD_PACK_ROWS = 32


def _cols_by_owner(a):
    return a.transpose(1, 0, 2).reshape(a.shape[1], -1)


class WeightGatherer:
    def __init__(self, w, chip_idx):
        def place(tag, b, dtype=bf16, after=None):
            return place_shard(b, chip_idx, f"place_shard_{tag}", dtype, after)

        sems_in, bufs_in, self.started = gather_start([
            [place("in", _lane_padded(w["w_in_even"][0].astype(bf16), IN_PAD)),
             place("small", _pack([w[n] for n, _, _ in _SHARDED_SMALL], _SHARD_PACK_ROWS), f32)]], "in")
        t = self.started
        sems, bufs, self.all_started = gather_start([
            [place("out", w["w_out_even"][0], after=t), place("up0", w["w_up"][0], after=t),
             place("down0", w["w_down"][0], after=t)],
            [place("qkv", _lane_padded(w["w_qkv"][0], QKV_PAD), after=t), place("o", w["w_o"][0], after=t),
             place("up1", w["w_up"][1], after=t), place("down1", w["w_down"][1], after=t)],
        ], "rest")
        self.sems, self.bufs = sems_in + sems, bufs_in + bufs

    def _group(self, gi, after, tag):
        return gather_forward(gather_wait(self.bufs[gi], self.sems[gi], after, tag), tag)

    def mixer_in(self, after):
        g, small = self._group(0, [after, self.all_started], "in")
        shard_shapes = [tuple(width if i == axis else d for i, d in enumerate(dict(_SMALL_SHAPES)[n]))
                        for n, axis, width in _SHARDED_SMALL]
        per_chip = [_unpack(small[s], shard_shapes) for s in range(N_CHIPS)]
        full = {n: jnp.concatenate([per_chip[s][i] for s in range(N_CHIPS)], axis=axis)
                for i, (n, axis, _) in enumerate(_SHARDED_SMALL)}
        w_in_p = jnp.concatenate([g[s, :, :IN_SHARD] for s in range(N_CHIPS)]
                                 + [jnp.zeros((g.shape[1], NP_IN - IN_EVEN), g.dtype)], axis=1)
        return w_in_p, full

    def layer0(self, after):
        w_out, w_up, w_down = self._group(1, [after], "l0")
        return w_out.reshape(2048, 1024), w_up, w_down.reshape(4096, 1024)

    def layer1_start(self, after):
        return forward_start(gather_wait(self.bufs[2], self.sems[2], [after], "l1"), "l1")

    def layer1(self, pending, after):
        q, w_o, w_up, w_down = forward_wait(*pending, after, "l1")
        w_qkv = jnp.concatenate([q[s, :, :QKV_SHARD] for s in range(N_CHIPS)], axis=1)
        return w_qkv, w_o.reshape(1024, 1024), w_up, w_down.reshape(4096, 1024)


def _row2(v):
    return v.reshape(1, -1)


def _lane_pad(v):
    return jnp.pad(v, ((0, 0), (0, CH - v.shape[1])))


_H_AND_NORM = (("tile", f32), ("tile", bf16))
_DX_AND_DG = (("tile", f32), ("sum", D_MODEL))


def _mlp_bwd(dh_out, h, g_row, y, a, w_up, w_down, tag, after=None):
    da = matmul(dh_out, w_down, dims="nt", name=f"mlp_da{tag}", out_dtype=bf16, tm=2048, tn=1024,
                epi=_times_relu2_grad, epi_args=(("tile", a),), after=after)
    dw_down = matmul(a, dh_out, dims="tn", name=f"mlp_dwdown{tag}", out_dtype=bf16, a_pro=_relu2)
    dw_up = matmul(y, da, dims="tn", name=f"mlp_dwup{tag}", out_dtype=bf16, tn=1024, out_by_col_tile=True)
    dh, dg, dh_colsum = matmul_rows(da, w_up, dims="nt", name=f"mlp_dy{tag}", epi=_norm_bwd_res_colsum,
                                    epi_args=(("tile", h), ("row", g_row), ("tile", dh_out)),
                                    outs=_DX_AND_DG + (("sum", D_MODEL),))
    return dh, dg, dw_up, dw_down, dh_colsum


def _by_owner(a):
    return a.reshape(N_CHIPS, a.shape[0] // N_CHIPS, a.shape[1])


def _row_shards(a, shard, padded):
    return jnp.stack([jnp.pad(a[shard * s: shard * (s + 1)], ((0, padded - shard), (0, 0))) for s in range(N_CHIPS)])


def _col_shards(a, shard, padded):
    return jnp.stack([_lane_padded(a[:, shard * s: shard * (s + 1)], padded) for s in range(N_CHIPS)])


def _local_step(x, target, weights, sm, reducer):
    w_up, w_down = [None, None], [None, None]
    mix_g = [_row2(sm["norm_mix_g"][i]) for i in range(2)]
    y0 = rmsnorm_fwd(x, mix_g[0] + weights.started[:1, :1], "mix_norm0")
    w_in_p, sharded_small = weights.mixer_in(y0)
    sm = {**sm, **sharded_small}
    mlp_g = [_row2(sm["norm_mlp_g"][i]) for i in range(2)]
    mixer_prm = {
        "ln_g": sm["gm_ln_g"], "ln_b": sm["gm_ln_b"], "wm": sm["gm_w_s"][0],
        "bs_t": jnp.pad(sm["gm_b_s"][0].T, ((0, 0), (0, CH - N_BLK))),
        "conv_w": jnp.pad(sm["ssm_conv_w"][0], ((0, 4), (0, 0))), "conv_b": sm["ssm_conv_b"],
        "dt_bias": _lane_pad(sm["ssm_dt_bias"]), "a_log": _lane_pad(sm["ssm_a_log"]),
        "d_heads": _lane_pad(sm["ssm_d"]), "norm_g": sm["ssm_norm_g"],
    }
    sink_row = _lane_pad(sm["attn_sinks"])

    proj = matmul(y0, w_in_p, dims="nn", name="in_proj", tm=2048, tn=768)
    ab, hstates = mixer_fwd(proj, mixer_prm)
    w_out, w_up[0], w_down[0] = weights.layer0(ab)
    h1, y1 = matmul_rows(ab, w_out, dims="nn", name="out_proj", epi=_res_norm,
                         epi_args=(("tile", x), ("row", mlp_g[0])), outs=_H_AND_NORM)
    a1 = matmul(y1, w_up[0], dims="nn", name="mlp_up0", out_dtype=bf16, tm=2048, tn=1024)
    pending_l1, forwarding_l1 = weights.layer1_start(a1)
    h2, y2 = matmul_rows(a1, w_down[0], dims="nn", name="mlp_down0", a_pro=_relu2, epi=_res_norm,
                         epi_args=(("tile", h1), ("row", mix_g[1])), outs=_H_AND_NORM, after=forwarding_l1)
    w_qkv, w_o, w_up[1], w_down[1] = weights.layer1(pending_l1, h2)
    qkv = matmul(y2, w_qkv, dims="nn", name="qkv_proj", tn=QKV_DIM, epi=_add_bias, epi_args=(("row", sm["b_qkv"]),))
    att = attn_fwd(qkv, sink_row)
    h3, y3 = matmul_rows(att, w_o, dims="nn", name="o_proj", epi=_bias_res_norm,
                         epi_args=(("row", sm["b_o"]), ("tile", h2), ("row", mlp_g[1])), outs=_H_AND_NORM)
    a3 = matmul(y3, w_up[1], dims="nn", name="mlp_up1", out_dtype=bf16, tm=2048, tn=1024)
    dh4, dg_final, loss = matmul_rows(
        a3, w_down[1], dims="nn", name="mlp_down1", a_pro=_relu2, epi=_res_norm_loss,
        epi_args=(("tile", h3), ("row", _row2(sm["final_norm_g"])), ("tile", target)),
        outs=(("tile", f32), ("sum", D_MODEL), ("sum", 128)))

    dh3, dg_mlp1, dw_up1, dw_down1, db_o = _mlp_bwd(dh4, h3, mlp_g[1], y3, a3, w_up[1], w_down[1], 1)
    datt = matmul(dh3, w_o, dims="nt", name="attn_dout", out_dtype=bf16)
    dw_o = matmul(att, dh3, dims="tn", name="dw_o", out_dtype=bf16)
    dqkv, dsink, db_qkv = attn_bwd(qkv, sink_row, datt)
    dw_qkv = matmul(y2, dqkv, dims="tn", name="dw_qkv", out_dtype=bf16, tn=QKV_DIM)
    dh2, dg_mix1 = matmul_rows(dqkv, w_qkv, dims="nt", name="dy_qkv", epi=_norm_bwd_res,
                               epi_args=(("tile", h2), ("row", mix_g[1]), ("tile", dh3)), outs=_DX_AND_DG)
    layer1 = [jnp.concatenate([_by_owner(dw_o), dw_up1, _by_owner(dw_down1)], axis=1),
              _col_shards(dw_qkv, QKV_SHARD, QKV_PAD)]
    flight1, token1 = reducer.start(layer1, "l1", direct=True)
    dh1, dg_mlp0, dw_up0, dw_down0, _ = _mlp_bwd(dh2, h1, mlp_g[0], y1, a1, w_up[0], w_down[0], 0, after=token1)
    pending1, shared1 = reducer.finish(flight1, dh1, "l1")
    dw_out = matmul(ab, dh1, dims="tn", name="dw_out", out_dtype=bf16, after=shared1)
    flight0, token0 = reducer.start(
        [jnp.concatenate([dw_up0, _by_owner(dw_down0), _by_owner(dw_out)], axis=1)], "l0", direct=True)
    dab = matmul(dh1, w_out, dims="nt", name="mixer_dout", tm=2048, tn=1024, after=token0)
    dproj, dmix = mixer_bwd(proj, hstates, dab, mixer_prm)
    dw_in_t = matmul(dproj, y0, dims="tn", name="dw_in", out_dtype=bf16, tm=768, tn=1024)
    pending0, shared0 = reducer.finish(flight0, dw_in_t, "l0")
    flight_in, token_in = reducer.start([_row_shards(dw_in_t, IN_SHARD, IN_PAD)], "in")
    dx, dg_mix0 = matmul_rows(dproj, w_in_p, dims="nt", name="dy_in", tm=256, epi=_norm_bwd_res,
                              epi_args=(("tile", x), ("row", mix_g[0]), ("tile", dh1)), outs=_DX_AND_DG,
                              after=token_in + shared0)
    pending_in, _ = reducer.finish(flight_in, dx, "in")
    r_l1, r_qkv = reducer.collect(pending1, dx, "l1")
    (r_l0,) = reducer.collect(pending0, dx, "l0")
    (r_in,) = reducer.collect(pending_in, dx, "in")
    reduced = {
        "w_out_even": r_l0[None, 2048:], "w_in_even": r_in[:IN_SHARD].T[None], "w_qkv": r_qkv[None, :, :QKV_SHARD],
        "w_o": r_l1[None, :256], "w_up": jnp.stack([r_l0[:1024], r_l1[256:1280]]),
        "w_down": jnp.stack([r_l0[1024:2048], r_l1[1280:]]),
    }

    small_grads = {
        "norm_mix_g": jnp.concatenate([dg_mix0, dg_mix1], axis=0),
        "norm_mlp_g": jnp.concatenate([dg_mlp0, dg_mlp1], axis=0),
        "final_norm_g": dg_final[0], "gm_ln_g": dmix["ln_g"], "gm_ln_b": dmix["ln_b"],
        "gm_w_s": dmix["wm"][None], "gm_b_s": dmix["bs_t"][:, :N_BLK].T[None],
        "ssm_conv_b": dmix["conv_b"], "ssm_dt_bias": dmix["dt_bias"][:, :SSM_HEADS],
        "ssm_a_log": dmix["a_log"][:, :SSM_HEADS], "ssm_d": dmix["d_heads"][:, :SSM_HEADS],
        "ssm_norm_g": dmix["norm_g"], "attn_sinks": dsink[:, :SSM_HEADS],
        "ssm_conv_w": dmix["conv_w"][None, :4], "b_qkv": db_qkv, "b_o": db_o,
    }
    return loss, dx, reduced, small_grads


def kernel(x, norm_mix_g, norm_mlp_g, final_norm_g, w_in_even, w_out_even, gm_ln_g, gm_ln_b, gm_w_s, gm_b_s, ssm_conv_w, ssm_conv_b, ssm_dt_bias, ssm_a_log, ssm_d, ssm_norm_g, w_qkv, b_qkv, w_o, b_o, attn_sinks, w_up, w_down, loss_target, m_norm_mix_g, m_norm_mlp_g, m_final_norm_g, m_w_in_even, m_w_out_even, m_gm_ln_g, m_gm_ln_b, m_gm_w_s, m_gm_b_s, m_ssm_conv_w, m_ssm_conv_b, m_ssm_dt_bias, m_ssm_a_log, m_ssm_d, m_ssm_norm_g, m_w_qkv, m_b_qkv, m_w_o, m_b_o, m_attn_sinks, m_w_up, m_w_down, v_norm_mix_g, v_norm_mlp_g, v_final_norm_g, v_w_in_even, v_w_out_even, v_gm_ln_g, v_gm_ln_b, v_gm_w_s, v_gm_b_s, v_ssm_conv_w, v_ssm_conv_b, v_ssm_dt_bias, v_ssm_a_log, v_ssm_d, v_ssm_norm_g, v_w_qkv, v_b_qkv, v_w_o, v_b_o, v_attn_sinks, v_w_up, v_w_down):
    w = dict(norm_mix_g=norm_mix_g, norm_mlp_g=norm_mlp_g, final_norm_g=final_norm_g, w_in_even=w_in_even,
             w_out_even=w_out_even, gm_ln_g=gm_ln_g, gm_ln_b=gm_ln_b, gm_w_s=gm_w_s, gm_b_s=gm_b_s,
             ssm_conv_w=ssm_conv_w, ssm_conv_b=ssm_conv_b, ssm_dt_bias=ssm_dt_bias, ssm_a_log=ssm_a_log,
             ssm_d=ssm_d, ssm_norm_g=ssm_norm_g, w_qkv=w_qkv, b_qkv=b_qkv, w_o=w_o, b_o=b_o,
             attn_sinks=attn_sinks, w_up=w_up, w_down=w_down)
    m = dict(norm_mix_g=m_norm_mix_g, norm_mlp_g=m_norm_mlp_g, final_norm_g=m_final_norm_g,
             w_in_even=m_w_in_even, w_out_even=m_w_out_even, gm_ln_g=m_gm_ln_g, gm_ln_b=m_gm_ln_b,
             gm_w_s=m_gm_w_s, gm_b_s=m_gm_b_s, ssm_conv_w=m_ssm_conv_w, ssm_conv_b=m_ssm_conv_b,
             ssm_dt_bias=m_ssm_dt_bias, ssm_a_log=m_ssm_a_log, ssm_d=m_ssm_d, ssm_norm_g=m_ssm_norm_g,
             w_qkv=m_w_qkv, b_qkv=m_b_qkv, w_o=m_w_o, b_o=m_b_o, attn_sinks=m_attn_sinks, w_up=m_w_up,
             w_down=m_w_down)
    v = dict(norm_mix_g=v_norm_mix_g, norm_mlp_g=v_norm_mlp_g, final_norm_g=v_final_norm_g,
             w_in_even=v_w_in_even, w_out_even=v_w_out_even, gm_ln_g=v_gm_ln_g, gm_ln_b=v_gm_ln_b,
             gm_w_s=v_gm_w_s, gm_b_s=v_gm_b_s, ssm_conv_w=v_ssm_conv_w, ssm_conv_b=v_ssm_conv_b,
             ssm_dt_bias=v_ssm_dt_bias, ssm_a_log=v_ssm_a_log, ssm_d=v_ssm_d, ssm_norm_g=v_ssm_norm_g,
             w_qkv=v_w_qkv, b_qkv=v_b_qkv, w_o=v_w_o, b_o=v_b_o, attn_sinks=v_attn_sinks, w_up=v_w_up,
             w_down=v_w_down)
    names = ("norm_mix_g", "norm_mlp_g", "final_norm_g", "w_in_even", "w_out_even", "gm_ln_g", "gm_ln_b",
             "gm_w_s", "gm_b_s", "ssm_conv_w", "ssm_conv_b", "ssm_dt_bias", "ssm_a_log", "ssm_d", "ssm_norm_g",
             "w_qkv", "b_qkv", "w_o", "b_o", "attn_sinks", "w_up", "w_down")

    cx, cy, cc = lax.axis_index("x"), lax.axis_index("y"), lax.axis_index("c")
    chip = 2 * cx + cy
    c_idx = jnp.reshape(cc, (1,)).astype(jnp.int32)
    chip_idx = jnp.reshape(chip, (1,)).astype(jnp.int32)

    weights = WeightGatherer(w, chip_idx)
    sm = {n: w[n] for n, _ in _SMALL_SHAPES[:_N_REPLICATED]}

    reducer = GradReducer(c_idx, jnp.concatenate([chip_idx, c_idx]))
    loss_part, dx, grads, small_grads = _local_step(x[0], loss_target[0], weights, sm, reducer)

    small_sum = allreduce_small(_pack([small_grads[n] for n, _ in _SMALL_SHAPES] + [loss_part], SMALL_ROWS))
    *small_list, loss_row = _unpack(small_sum, [s for _, s in _SMALL_SHAPES] + [loss_part.shape])
    loss = loss_row[0, 0]
    small_full = dict(zip([n for n, _ in _SMALL_SHAPES], small_list))
    for n, _ in _SMALL_SHAPES[:_N_REPLICATED]:
        grads[n] = small_full[n]
    for n, axis, width in _SHARDED_SMALL:
        grads[n] = lax.dynamic_slice_in_dim(small_full[n], chip * width, width, axis)
    grads = {n: grads[n].reshape(w[n].shape) for n in names}

    delta, new_m, new_v = {}, {}, {}
    for n in names:
        if n in _COLUMN_SHARDED:
            args = [jnp.transpose(d[n], (2, 0, 1)) for d in (w, grads, m, v)]
            grads[n] = jnp.transpose(args[1], (1, 2, 0))
            outs = adamw(*args, f"adamw_{n}")
            delta[n], new_m[n], new_v[n] = (jnp.transpose(o, (1, 2, 0)) for o in outs)
            continue
        shape = (1,) + w[n].shape if w[n].ndim == 1 else w[n].shape
        outs = adamw(*[d[n].reshape(shape) for d in (w, grads, m, v)], f"adamw_{n}")
        delta[n], new_m[n], new_v[n] = (o.reshape(w[n].shape) for o in outs)

    return (loss, dx[None], *[grads[n] for n in names], *[delta[n] for n in names],
            *[new_m[n] for n in names], *[new_v[n] for n in names])
```

```python
import functools

import jax
import jax.numpy as jnp
from jax import lax
from jax.experimental import pallas as pl
from jax.experimental.pallas import tpu as pltpu

f32 = jnp.float32
bf16 = jnp.bfloat16
MXU_DTYPE = bf16

RMS_EPS = 1e-5
LN_EPS = 1e-5
D_MODEL = 1024
D_FF = 4096
CH = 128
N_BLK = 8
SSM_HEADS = 16
IN_EVEN = 5136
NP_IN = 5376
OFF_U, OFF_V, OFF_Z, OFF_X, OFF_DT = 0, 1024, 2048, 3072, 5120
XBC_BLKS = 16
QKV_DIM = 1280
ATT_SCALE = 64 ** -0.5

ADAM_LR = 0.001
ADAM_B1 = 0.9
ADAM_B2 = 0.999
ADAM_EPS = 1e-08
ADAM_WD = 0.01
ADAM_STEP = 10

VMEM_LIMIT_BYTES = 48 * 1024 * 1024
N_CHIPS = 4
SMALL_ROWS = 256

NN = ((1,), (0,))
NT = ((1,), (1,))
TN = ((0,), (0,))


def _mm(a, b, dims):
    return lax.dot_general(a.astype(MXU_DTYPE), b.astype(MXU_DTYPE), (dims, ((), ())),
                           preferred_element_type=f32)


def _mm_exact(a, b):
    return jnp.dot(a, b, preferred_element_type=f32, precision=lax.Precision.HIGHEST)


def _cparams(sem=None):
    return pltpu.CompilerParams(dimension_semantics=sem, vmem_limit_bytes=VMEM_LIMIT_BYTES)


@jax.custom_vjp
def _swap64(x):
    return pltpu.roll(x, 64, axis=1)


_swap64.defvjp(lambda x: (pltpu.roll(x, 64, axis=1), None), lambda _, g: (pltpu.roll(g, 64, axis=1),))


def _row_blocks_of(x):
    return tuple(x[i:i + CH] for i in range(0, x.shape[0], CH))


@jax.custom_vjp
def _row_blocks(x):
    return _row_blocks_of(x)


_row_blocks.defvjp(lambda x: (_row_blocks_of(x), None), lambda _, gs: (jnp.concatenate(gs, axis=0),))


def _make_delay(k):
    @jax.custom_vjp
    def delay(ext):
        return pltpu.roll(ext, k, axis=0)[8:, :]

    def fwd(ext):
        return delay(ext), None

    def bwd(_, g):
        gp = jnp.concatenate([jnp.zeros((8, g.shape[1]), g.dtype), g], axis=0)
        return (pltpu.roll(gp, gp.shape[0] - k, axis=0),)

    delay.defvjp(fwd, bwd)
    return delay


_DELAYS = {k: _make_delay(k) for k in (1, 2, 3)}


_GELU_C = 0.7978845608028654
_GELU_K = 0.044715


@jax.custom_vjp
def _gelu(x):
    return 0.5 * x * (1.0 + jnp.tanh(_GELU_C * (x + _GELU_K * (x * x * x))))


def _gelu_fwd(x):
    t = jnp.tanh(_GELU_C * (x + _GELU_K * (x * x * x)))
    return 0.5 * x * (1.0 + t), (x, t)


def _gelu_bwd(res, g):
    x, t = res
    dz = _GELU_C + (3.0 * _GELU_C * _GELU_K) * (x * x)
    return (g * (0.5 * (1.0 + t) + (0.5 * x) * (1.0 - t * t) * dz),)


_gelu.defvjp(_gelu_fwd, _gelu_bwd)


def _col(m, lane, h):
    return jnp.sum(jnp.where(lane == h, m, 0.0), axis=1, keepdims=True)


@functools.lru_cache(maxsize=None)
def _row_picker(h, shape):
    @jax.custom_vjp
    def pick(m):
        return m[h:h + 1, :]

    def bwd(_, g):
        return (jnp.where(lax.broadcasted_iota(jnp.int32, shape, 0) == h, g, 0.0),)

    pick.defvjp(lambda m: (m[h:h + 1, :], None), bwd)
    return pick


def _row(m, sub, h):
    return _row_picker(h, m.shape)(m)


def _mixer_chunk(us, vs, zs, xbcs, halos, dtblk, hps, prm):
    lane = lax.broadcasted_iota(jnp.int32, (CH, CH), 1)
    sub = lax.broadcasted_iota(jnp.int32, (CH, CH), 0)
    left = lane < 64
    top = sub < 64
    causal = sub >= lane

    gus = [_gelu(u) for u in us]
    gvs = [_gelu(v) for v in vs]
    mu = sum(jnp.sum(g, axis=1, keepdims=True) for g in gvs) / D_MODEL
    cen = [g - mu for g in gvs]
    var = sum(jnp.sum(c * c, axis=1, keepdims=True) for c in cen) / D_MODEL
    rstd = lax.rsqrt(var + LN_EPS)
    a_out = []
    for g in range(N_BLK):
        vn = cen[g] * rstd * prm["ln_g"][g] + prm["ln_b"][g]
        w = jnp.where(causal, prm["wm"][g], 0.0)
        mixed = _mm(w, vn, NN) + _col(prm["bs_t"], lane, g)
        a_out.append(gus[g] * mixed)

    act = []
    for b in range(XBC_BLKS):
        w8 = prm["conv_w"][b]
        sub8 = lax.broadcasted_iota(jnp.int32, w8.shape, 0)
        ext = jnp.concatenate([halos[b], xbcs[b]], axis=0)
        conv = xbcs[b] * _row(w8, sub8, 3) + prm["conv_b"][b]
        for k in (1, 2, 3):
            conv = conv + _DELAYS[k](ext) * _row(w8, sub8, 3 - k)
        act.append(jax.nn.silu(conv))

    dt = jax.nn.softplus(dtblk + prm["dt_bias"])
    a_neg = -jnp.exp(prm["a_log"])
    tri = causal.astype(f32)
    acum = _mm_exact(tri, dt * a_neg)
    acum_t = acum.T
    dt_t = dt.T
    last = sub == CH - 1
    ys, h_out = [], []
    for grp in range(4):
        bm = act[8 + grp]
        cm = act[12 + grp]
        cb = _mm(cm, bm, NT)
        for p in (2 * grp, 2 * grp + 1):
            h0, h1 = 2 * p, 2 * p + 1
            xp = act[p]
            hp = hps[p]
            wis = []
            for h in (h0, h1):
                seg = _col(acum, lane, h) - _row(acum_t, sub, h)
                decay = jnp.exp(jnp.where(causal, seg, -jnp.inf))
                wis.append(cb * decay * _row(dt_t, sub, h))
            wcat = jnp.concatenate(wis, axis=1)
            xbd = jnp.concatenate([jnp.where(left, xp, 0.0), jnp.where(left, 0.0, xp)], axis=0)
            y_diag = _mm(wcat, xbd, NN)
            a_end = [jnp.sum(jnp.where(last & (lane == h), acum, 0.0), keepdims=True) for h in (h0, h1)]
            a_col = jnp.where(left, _col(acum, lane, h0), _col(acum, lane, h1))
            dt_col = jnp.where(left, _col(dt, lane, h0), _col(dt, lane, h1))
            to_end = jnp.exp(jnp.where(left, a_end[0], a_end[1]) - a_col) * dt_col
            states = _mm(xp * to_end, bm, TN)
            chunk_decay = jnp.where(top, jnp.exp(a_end[0]), jnp.exp(a_end[1]))
            h_out.append(chunk_decay * hp + states)
            y_off = jnp.exp(a_col) * _mm(cm, hp, NT)
            d_skip = jnp.where(left[:1], _col(prm["d_heads"], lane[:1], h0), _col(prm["d_heads"], lane[:1], h1))
            ys.append((y_diag + y_off + xp * d_skip) * jax.nn.silu(zs[p]))

    b_out = []
    for grp in range(4):
        pair = (ys[2 * grp], ys[2 * grp + 1])
        ms = sum(jnp.sum(y * y, axis=1, keepdims=True) for y in pair) / 256.0
        r = lax.rsqrt(ms + RMS_EPS)
        for j, y in enumerate(pair):
            b_out.append(y * r * prm["norm_g"][2 * grp + j])
    return a_out, b_out, h_out


def _attn_block(qps, kprev, kcur, vprev, vcur, sink_row, first):
    lane = lax.broadcasted_iota(jnp.int32, (CH, CH), 1)
    left = lane < 64
    own = lane <= lax.broadcasted_iota(jnp.int32, (CH, CH), 0)
    own8 = jnp.concatenate([own] * N_BLK, axis=0)

    def both_halves(a):
        sw = _swap64(a)
        return [jnp.where(left, a, sw), jnp.where(left, sw, a)]

    kc, kp, vc, vp = both_halves(kcur), both_halves(kprev), both_halves(vcur), both_halves(vprev)
    outs = []
    for j in range(2):
        q8 = jnp.concatenate([part for p in range(4 * j, 4 * j + 4)
                              for part in (jnp.where(left, qps[p], 0.0), jnp.where(left, 0.0, qps[p]))], axis=0)
        s_cur = _row_blocks(_mm(q8, kc[j], NT))
        s_prev = _row_blocks(_mm(q8, kp[j], NT))
        probs = []
        for h in range(N_BLK):
            s = jnp.where(own, s_cur[h] * ATT_SCALE, jnp.where(first, -jnp.inf, s_prev[h] * ATT_SCALE))
            sink = _col(sink_row, lane[:1], N_BLK * j + h)
            m = lax.stop_gradient(jnp.maximum(jnp.max(s, axis=1, keepdims=True), sink))
            pexp = jnp.exp(s - m)
            probs.append(pexp / (jnp.sum(pexp, axis=1, keepdims=True) + jnp.exp(sink - m)))
        p8 = jnp.concatenate(probs, axis=0)
        o = _row_blocks(_mm(jnp.where(own8, p8, 0.0), vc[j], NN) + _mm(jnp.where(own8, 0.0, p8), vp[j], NN))
        for t in range(4):
            outs.append(jnp.where(left, o[2 * t], o[2 * t + 1]))
    return outs


def _rmsnorm(x, g):
    r = lax.rsqrt(jnp.mean(x * x, axis=-1, keepdims=True) + RMS_EPS)
    return x * r * g


def rmsnorm_fwd(x, g_row, name):
    s, d = x.shape
    tm = min(512, s)

    def body(x_ref, g_ref, y_ref):
        y_ref[...] = _rmsnorm(x_ref[...], g_ref[...]).astype(bf16)

    return pl.pallas_call(
        body, name=name, grid=(s // tm,),
        in_specs=[pl.BlockSpec((tm, d), lambda i: (i, 0)), pl.BlockSpec((1, d), lambda i: (0, 0))],
        out_specs=pl.BlockSpec((tm, d), lambda i: (i, 0)),
        out_shape=jax.ShapeDtypeStruct((s, d), bf16),
        compiler_params=_cparams(("parallel",)),
    )(x, g_row)


def _fit(dim, want):
    if dim <= want:
        return dim
    t = want
    while dim % t:
        t -= 128
    return t


def matmul(a, b, *, dims, name, out_dtype=f32, tm=1024, tn=512, tk=8192, a_pro=None, epi=None, epi_args=(),
           out_by_col_tile=False, after=None):
    if dims == "nn" and b.ndim == 3:
        (m, k), n, tn = a.shape, b.shape[0] * b.shape[2], b.shape[2]
    elif dims == "nn":
        (m, k), n = a.shape, b.shape[1]
    elif dims == "nt":
        (m, k), n = a.shape, b.shape[0]
    else:
        (k, m), n = a.shape, b.shape[1]
    tm, tn, tk = _fit(m, tm), _fit(n, tn), _fit(k, tk)
    nk = k // tk
    if dims == "nn":
        a_spec = pl.BlockSpec((tm, tk), lambda i, j, kk: (i, kk))
        b_spec = (pl.BlockSpec((None, tk, tn), lambda i, j, kk: (j, kk, 0)) if b.ndim == 3
                  else pl.BlockSpec((tk, tn), lambda i, j, kk: (kk, j)))
        dn = NN
    elif dims == "nt":
        a_spec = pl.BlockSpec((tm, tk), lambda i, j, kk: (i, kk))
        b_spec = pl.BlockSpec((tn, tk), lambda i, j, kk: (j, kk))
        dn = NT
    else:
        a_spec = pl.BlockSpec((tk, tm), lambda i, j, kk: (kk, i))
        b_spec = pl.BlockSpec((tk, tn), lambda i, j, kk: (kk, j))
        dn = TN
    e_specs = [pl.BlockSpec((tm, tn), lambda i, j, kk: (i, j)) if kind == "tile"
               else pl.BlockSpec((1, tn), lambda i, j, kk: (0, j)) for kind, _ in epi_args]
    n_epi = len(epi_args)
    order_specs = [] if after is None else [pl.BlockSpec((8, 128), lambda i, j, kk: (0, 0))]
    order_args = [] if after is None else [after]

    def body(*refs):
        a_ref, b_ref = refs[0], refs[1]
        e_refs = refs[2:2 + n_epi]
        n_in = 2 + n_epi + len(order_args)
        o_ref = refs[n_in]
        av = a_ref[...]
        if a_pro is not None:
            av = a_pro(av)
        part = _mm(av, b_ref[...], dn)

        def finish(acc):
            if epi is not None:
                acc = epi(acc, *[r[...] for r in e_refs])
            o_ref[...] = acc.astype(out_dtype)

        if nk == 1:
            finish(part)
        else:
            acc_ref = refs[n_in + 1]
            kk = pl.program_id(2)

            @pl.when(kk == 0)
            def _():
                acc_ref[...] = part

            @pl.when(kk > 0)
            def _():
                acc_ref[...] += part

            @pl.when(kk == nk - 1)
            def _():
                finish(acc_ref[...])

    if out_by_col_tile:
        out_spec = pl.BlockSpec((None, tm, tn), lambda i, j, kk: (j, i, 0))
        out_shape = jax.ShapeDtypeStruct((n // tn, m, tn), out_dtype)
    else:
        out_spec = pl.BlockSpec((tm, tn), lambda i, j, kk: (i, j))
        out_shape = jax.ShapeDtypeStruct((m, n), out_dtype)
    return pl.pallas_call(
        body, name=name, grid=(m // tm, n // tn, nk),
        in_specs=[a_spec, b_spec] + e_specs + order_specs,
        out_specs=out_spec,
        out_shape=out_shape,
        scratch_shapes=[pltpu.VMEM((tm, tn), f32)] if nk > 1 else [],
        compiler_params=_cparams(("parallel", "parallel", "arbitrary")),
    )(a, b, *[arr for _, arr in epi_args], *order_args)


def _relu2(a):
    r = jnp.maximum(a.astype(f32), 0.0)
    return r * r


def _add(acc, t):
    return acc + t


def _add_bias(acc, t):
    return acc + t


def _add_bias_res(acc, bias, res):
    return acc + bias + res


def _times_relu2_grad(acc, a):
    return acc * (2.0 * jnp.maximum(a.astype(f32), 0.0))


def matmul_rows(a, b, *, dims, name, epi, epi_args, outs, tm=512, a_pro=None, after=None):
    m, k = a.shape
    n = b.shape[-1] if dims == "nn" else b.shape[-2]
    tm = _fit(m, tm)
    dn = NN if dims == "nn" else NT
    e_specs = [pl.BlockSpec((tm, arr.shape[1]), lambda i: (i, 0)) if kind == "tile"
               else pl.BlockSpec((1, arr.shape[1]), lambda i: (0, 0)) for kind, arr in epi_args]
    order_specs = [] if after is None else [pl.BlockSpec((8, 128), lambda i: (0, 0))]
    order_args = [] if after is None else [after]
    n_in = 2 + len(epi_args) + len(order_args)

    def body(*refs):
        av = refs[0][...]
        if a_pro is not None:
            av = a_pro(av)
        if b.ndim == 3:
            kb = b.shape[2]
            acc = sum(_mm(av[:, s * kb:(s + 1) * kb], refs[1][s], dn) for s in range(b.shape[0]))
        else:
            acc = _mm(av, refs[1][...], dn)
        vals = epi(acc, *[r[...] for r in refs[2:2 + len(epi_args)]])
        for (kind, _), o_ref, val in zip(outs, refs[n_in:], vals):
            if kind == "tile":
                o_ref[...] = val.astype(o_ref.dtype)
            else:
                @pl.when(pl.program_id(0) == 0)
                def _():
                    o_ref[...] = jnp.zeros_like(o_ref)

                o_ref[...] += val

    out_specs = [pl.BlockSpec((tm, n), lambda i: (i, 0)) if kind == "tile" else pl.BlockSpec((1, arg), lambda i: (0, 0))
                 for kind, arg in outs]
    out_shape = [jax.ShapeDtypeStruct((m, n), arg) if kind == "tile" else jax.ShapeDtypeStruct((1, arg), f32)
                 for kind, arg in outs]
    return pl.pallas_call(
        body, name=name, grid=(m // tm,),
        in_specs=[pl.BlockSpec((tm, k), lambda i: (i, 0)), pl.BlockSpec(b.shape, lambda i: (0,) * b.ndim)]
                 + e_specs + order_specs,
        out_specs=out_specs, out_shape=out_shape,
        compiler_params=_cparams(("arbitrary",)),
    )(a, b, *[arr for _, arr in epi_args], *order_args)


def _res_norm(acc, res, g):
    h = acc + res
    return h, _rmsnorm(h, g)


def _bias_res_norm(acc, bias, res, g):
    h = acc + bias + res
    return h, _rmsnorm(h, g)


def _res_norm_loss(acc, res, g, target):
    def f(h, gv):
        err = jnp.square(_rmsnorm(h, gv) - target)
        return 0.5 * jnp.sum(jnp.mean(err, axis=-1, keepdims=True), axis=0, keepdims=True)

    loss, vjp = jax.vjp(f, acc + res, g)
    dh, dg = vjp(jnp.ones_like(loss))
    return dh, dg, jnp.broadcast_to(loss, (1, 128))


def _norm_bwd_res_colsum(dy, x, g, res):
    dx, dg = _norm_bwd_res(dy, x, g, res)
    return dx, dg, jnp.sum(dx, axis=0, keepdims=True)


def _norm_bwd_res(dy, x, g, res):
    _, vjp = jax.vjp(_rmsnorm, x, g)
    dx, dg = vjp(dy)
    return res + dx, dg


_MIXER_PARAM_SHAPES = (
    ("ln_g", (1, D_MODEL)), ("ln_b", (1, D_MODEL)), ("wm", (N_BLK, CH, CH)), ("bs_t", (CH, CH)),
    ("conv_w", (8, 2048)), ("conv_b", (1, 2048)), ("dt_bias", (1, CH)), ("a_log", (1, CH)),
    ("d_heads", (1, CH)), ("norm_g", (1, D_MODEL)),
)


def _blocks(v, n, off=0):
    return [v[:, off + i * CH: off + (i + 1) * CH] for i in range(n)]


def _split_mixer_params(vals):
    p = dict(vals)
    return {
        "ln_g": _blocks(p["ln_g"], N_BLK), "ln_b": _blocks(p["ln_b"], N_BLK),
        "wm": [p["wm"][g] for g in range(N_BLK)], "bs_t": p["bs_t"],
        "conv_w": _blocks(p["conv_w"], XBC_BLKS), "conv_b": _blocks(p["conv_b"], XBC_BLKS),
        "dt_bias": p["dt_bias"], "a_log": p["a_log"], "d_heads": p["d_heads"],
        "norm_g": _blocks(p["norm_g"], N_BLK),
    }


def _mixer_leaves(proj_ref, halo_ref, keep_halo):
    pv = proj_ref
    us = [pv[:, OFF_U + i * CH: OFF_U + (i + 1) * CH] for i in range(N_BLK)]
    vs = [pv[:, OFF_V + i * CH: OFF_V + (i + 1) * CH] for i in range(N_BLK)]
    zs = [pv[:, OFF_Z + i * CH: OFF_Z + (i + 1) * CH] for i in range(N_BLK)]
    xbcs = [pv[:, OFF_X + i * CH: OFF_X + (i + 1) * CH] for i in range(XBC_BLKS)]
    halos = [halo_ref[:, OFF_X + i * CH: OFF_X + (i + 1) * CH] * keep_halo for i in range(XBC_BLKS)]
    dtblk = pv[:, OFF_DT: OFF_DT + CH]
    return us, vs, zs, xbcs, halos, dtblk


def mixer_fwd(proj, prm):
    s = proj.shape[0]
    nc = s // CH
    names = [n for n, _ in _MIXER_PARAM_SHAPES]

    def body(proj_ref, halo_ref, *rest):
        p_refs = rest[:len(names)]
        ab_ref, hs_ref, h_ref = rest[len(names):]
        c = pl.program_id(0)

        @pl.when(c == 0)
        def _():
            h_ref[...] = jnp.zeros_like(h_ref)

        hs_ref[...] = h_ref[...]
        keep = (c > 0).astype(f32)
        us, vs, zs, xbcs, halos, dtblk = _mixer_leaves(proj_ref, halo_ref, keep)
        hps = [h_ref[i * CH:(i + 1) * CH, :] for i in range(N_BLK)]
        p = _split_mixer_params({n: r[...] for n, r in zip(names, p_refs)})
        a_out, b_out, h_out = _mixer_chunk(us, vs, zs, xbcs, halos, dtblk, hps, p)
        for i in range(N_BLK):
            ab_ref[:, i * CH:(i + 1) * CH] = a_out[i].astype(bf16)
            ab_ref[:, D_MODEL + i * CH: D_MODEL + (i + 1) * CH] = b_out[i].astype(bf16)
            h_ref[i * CH:(i + 1) * CH, :] = h_out[i]

    def const(shape):
        return pl.BlockSpec(shape, lambda c: (0,) * len(shape))

    return pl.pallas_call(
        body, name="mixer_fwd", grid=(nc,),
        in_specs=[pl.BlockSpec((CH, NP_IN), lambda c: (c, 0)),
                  pl.BlockSpec((8, NP_IN), lambda c: (jnp.maximum(c * (CH // 8) - 1, 0), 0))]
                 + [const(shp) for _, shp in _MIXER_PARAM_SHAPES],
        out_specs=[pl.BlockSpec((CH, 2 * D_MODEL), lambda c: (c, 0)),
                   pl.BlockSpec((None, D_MODEL, CH), lambda c: (c, 0, 0))],
        out_shape=[jax.ShapeDtypeStruct((s, 2 * D_MODEL), bf16), jax.ShapeDtypeStruct((nc, D_MODEL, CH), f32)],
        scratch_shapes=[pltpu.VMEM((D_MODEL, CH), f32)],
        compiler_params=_cparams(("arbitrary",)),
    )(proj, proj, *[prm[n] for n in names])


def mixer_bwd(proj, hstates, dab, prm):
    s = proj.shape[0]
    nc = s // CH
    names = [n for n, _ in _MIXER_PARAM_SHAPES]
    npar = len(names)

    def body(proj_ref, halo_ref, hs_ref, dab_ref, *rest):
        p_refs = rest[:npar]
        dproj_ref = rest[npar]
        g_refs = rest[npar + 1: 2 * npar + 1]
        dh_ref, dhalo_ref = rest[2 * npar + 1:]
        i = pl.program_id(0)
        c = nc - 1 - i

        @pl.when(i == 0)
        def _():
            dh_ref[...] = jnp.zeros_like(dh_ref)
            dhalo_ref[...] = jnp.zeros_like(dhalo_ref)
            for r in g_refs:
                r[...] = jnp.zeros_like(r)

        keep = (c > 0).astype(f32)
        us, vs, zs, xbcs, halos, dtblk = _mixer_leaves(proj_ref, halo_ref, keep)
        hps = [hs_ref[j * CH:(j + 1) * CH, :] for j in range(N_BLK)]
        pvals = {n: r[...] for n, r in zip(names, p_refs)}

        def fn(us, vs, zs, xbcs, halos, dtblk, hps, pvals):
            return _mixer_chunk(us, vs, zs, xbcs, halos, dtblk, hps, _split_mixer_params(pvals))

        _, vjp = jax.vjp(fn, us, vs, zs, xbcs, halos, dtblk, hps, pvals)
        da = [dab_ref[:, j * CH:(j + 1) * CH].astype(f32) for j in range(N_BLK)]
        db = [dab_ref[:, D_MODEL + j * CH: D_MODEL + (j + 1) * CH].astype(f32) for j in range(N_BLK)]
        dh = [dh_ref[j * CH:(j + 1) * CH, :] for j in range(N_BLK)]
        dus, dvs, dzs, dxbcs, dhalos, ddt, dhps, dp = vjp((da, db, dh))

        for j in range(N_BLK):
            dproj_ref[:, OFF_U + j * CH: OFF_U + (j + 1) * CH] = dus[j].astype(bf16)
            dproj_ref[:, OFF_V + j * CH: OFF_V + (j + 1) * CH] = dvs[j].astype(bf16)
            dproj_ref[:, OFF_Z + j * CH: OFF_Z + (j + 1) * CH] = dzs[j].astype(bf16)
            dh_ref[j * CH:(j + 1) * CH, :] = dhps[j]
        zeros_top = jnp.zeros((CH - 8, CH), f32)
        for j in range(XBC_BLKS):
            late = jnp.concatenate([zeros_top, dhalo_ref[:, j * CH:(j + 1) * CH]], axis=0)
            dproj_ref[:, OFF_X + j * CH: OFF_X + (j + 1) * CH] = (dxbcs[j] + late).astype(bf16)
        for j in range(XBC_BLKS):
            dhalo_ref[:, j * CH:(j + 1) * CH] = dhalos[j] * keep
        lane = lax.broadcasted_iota(jnp.int32, (CH, CH), 1)
        dproj_ref[:, OFF_DT: OFF_DT + CH] = jnp.where(lane < SSM_HEADS, ddt, 0.0).astype(bf16)
        dproj_ref[:, OFF_DT + CH:] = jnp.zeros((CH, NP_IN - OFF_DT - CH), bf16)
        for n, r in zip(names, g_refs):
            r[...] += dp[n]

    def const(shape):
        return pl.BlockSpec(shape, lambda i: (0,) * len(shape))

    outs = pl.pallas_call(
        body, name="mixer_bwd", grid=(nc,),
        in_specs=[pl.BlockSpec((CH, NP_IN), lambda i: (nc - 1 - i, 0)),
                  pl.BlockSpec((8, NP_IN), lambda i: (jnp.maximum((nc - 1 - i) * (CH // 8) - 1, 0), 0)),
                  pl.BlockSpec((None, D_MODEL, CH), lambda i: (nc - 1 - i, 0, 0)),
                  pl.BlockSpec((CH, 2 * D_MODEL), lambda i: (nc - 1 - i, 0))]
                 + [const(shp) for _, shp in _MIXER_PARAM_SHAPES],
        out_specs=[pl.BlockSpec((CH, NP_IN), lambda i: (nc - 1 - i, 0))]
                  + [const(shp) for _, shp in _MIXER_PARAM_SHAPES],
        out_shape=[jax.ShapeDtypeStruct((s, NP_IN), bf16)]
                  + [jax.ShapeDtypeStruct(shp, f32) for _, shp in _MIXER_PARAM_SHAPES],
        scratch_shapes=[pltpu.VMEM((D_MODEL, CH), f32), pltpu.VMEM((8, 2048), f32)],
        compiler_params=_cparams(("arbitrary",)),
    )(proj, proj, hstates, dab, *[prm[n] for n in names])
    return outs[0], dict(zip(names, outs[1:]))


_K_BLK = D_MODEL // CH
_V_BLK = _K_BLK + 1


def _attn_specs(rev, nb):
    def blk(i):
        return nb - 1 - i if rev else i

    q_spec = pl.BlockSpec((CH, D_MODEL), lambda i: (blk(i), 0))
    kv = lambda col, prev: pl.BlockSpec(
        (CH, CH), lambda i: (jnp.maximum(blk(i) - 1, 0) if prev else blk(i), col))
    return q_spec, [kv(_K_BLK, True), kv(_K_BLK, False), kv(_V_BLK, True), kv(_V_BLK, False)]


def attn_fwd(qkv, sink_row):
    s = qkv.shape[0]
    nb = s // CH

    def body(q_ref, kp_ref, kc_ref, vp_ref, vc_ref, sink_ref, o_ref):
        qps = [q_ref[:, p * CH:(p + 1) * CH] for p in range(N_BLK)]
        outs = _attn_block(qps, kp_ref[...], kc_ref[...], vp_ref[...], vc_ref[...], sink_ref[...],
                           pl.program_id(0) == 0)
        for p in range(N_BLK):
            o_ref[:, p * CH:(p + 1) * CH] = outs[p].astype(bf16)

    q_spec, kv_specs = _attn_specs(False, nb)
    return pl.pallas_call(
        body, name="attn_fwd", grid=(nb,),
        in_specs=[q_spec] + kv_specs + [pl.BlockSpec((1, CH), lambda i: (0, 0))],
        out_specs=pl.BlockSpec((CH, D_MODEL), lambda i: (i, 0)),
        out_shape=jax.ShapeDtypeStruct((s, D_MODEL), bf16),
        compiler_params=_cparams(("parallel",)),
    )(qkv, qkv, qkv, qkv, qkv, sink_row)


def attn_bwd(qkv, sink_row, dout):
    s = qkv.shape[0]
    nb = s // CH

    def body(q_ref, kp_ref, kc_ref, vp_ref, vc_ref, sink_ref, do_ref, dqkv_ref, dsink_ref, db_ref, carry_ref):
        i = pl.program_id(0)
        blk = nb - 1 - i

        @pl.when(i == 0)
        def _():
            dsink_ref[...] = jnp.zeros_like(dsink_ref)
            db_ref[...] = jnp.zeros_like(db_ref)
            carry_ref[...] = jnp.zeros_like(carry_ref)

        qps = [q_ref[:, p * CH:(p + 1) * CH] for p in range(N_BLK)]
        first = blk == 0
        _, vjp = jax.vjp(lambda *a: _attn_block(*a, first), qps, kp_ref[...], kc_ref[...], vp_ref[...],
                         vc_ref[...], sink_ref[...])
        dos = [do_ref[:, p * CH:(p + 1) * CH].astype(f32) for p in range(N_BLK)]
        dqs, dkp, dkc, dvp, dvc, dsink = vjp(dos)
        blocks = list(dqs) + [dkc + carry_ref[0], dvc + carry_ref[1]]
        for p, val in enumerate(blocks):
            dqkv_ref[:, p * CH:(p + 1) * CH] = val.astype(bf16)
            db_ref[:, p * CH:(p + 1) * CH] += jnp.sum(val, axis=0, keepdims=True)
        keep = jnp.logical_not(first).astype(f32)
        carry_ref[0] = dkp * keep
        carry_ref[1] = dvp * keep
        dsink_ref[...] += dsink

    q_spec, kv_specs = _attn_specs(True, nb)
    return pl.pallas_call(
        body, name="attn_bwd", grid=(nb,),
        in_specs=[q_spec] + kv_specs + [pl.BlockSpec((1, CH), lambda i: (0, 0)),
                                        pl.BlockSpec((CH, D_MODEL), lambda i: (nb - 1 - i, 0))],
        out_specs=[pl.BlockSpec((CH, QKV_DIM), lambda i: (nb - 1 - i, 0)), pl.BlockSpec((1, CH), lambda i: (0, 0)),
                   pl.BlockSpec((1, QKV_DIM), lambda i: (0, 0))],
        out_shape=[jax.ShapeDtypeStruct((s, QKV_DIM), bf16), jax.ShapeDtypeStruct((1, CH), f32),
                   jax.ShapeDtypeStruct((1, QKV_DIM), f32)],
        scratch_shapes=[pltpu.VMEM((2, CH, CH), f32)],
        compiler_params=_cparams(("arbitrary",)),
    )(qkv, qkv, qkv, qkv, qkv, sink_row, dout)


def _adamw_update(w, g, m, v):
    nm = ADAM_B1 * m + (1.0 - ADAM_B1) * g
    nv = ADAM_B2 * v + (1.0 - ADAM_B2) * jnp.square(g)
    m_hat = nm / (1.0 - ADAM_B1 ** ADAM_STEP)
    v_hat = nv / (1.0 - ADAM_B2 ** ADAM_STEP)
    return -ADAM_LR * (m_hat / (jnp.sqrt(v_hat) + ADAM_EPS) + ADAM_WD * w), nm, nv


def adamw_rows(w, r, m, v, layer, row_off, name, into=None):
    rows, cols = w.shape[1], w.shape[2]
    tr = 256
    assert rows % tr == 0 and row_off % tr == 0

    def body(w_ref, r_ref, m_ref, v_ref, *rest):
        g_ref, d_ref, nm_ref, nv_ref = rest[-4:]
        g = r_ref[...]
        g_ref[...] = g
        d_ref[...], nm_ref[...], nv_ref[...] = _adamw_update(w_ref[...], g, m_ref[...], v_ref[...])

    tile = pl.BlockSpec((None, tr, cols), lambda i: (layer, i, 0))
    extra = [] if into is None else list(into)
    return pl.pallas_call(
        body, name=name, grid=(rows // tr,),
        in_specs=[tile, pl.BlockSpec((tr, cols), lambda i: (row_off // tr + i, 0)), tile, tile] + [_ANY] * len(extra),
        out_specs=[tile] * 4, out_shape=[jax.ShapeDtypeStruct(w.shape, f32)] * 4,
        input_output_aliases={4 + k: k for k in range(len(extra))},
        compiler_params=_cparams(("parallel",)),
    )(w, r, m, v, *extra)


def adamw(w, g, m, v, name):
    def body(w_ref, g_ref, m_ref, v_ref, d_ref, nm_ref, nv_ref):
        d_ref[...], nm_ref[...], nv_ref[...] = _adamw_update(w_ref[...], g_ref[...], m_ref[...], v_ref[...])

    out_shape = [jax.ShapeDtypeStruct(w.shape, f32)] * 3
    if w.ndim == 3 and w.shape[1] == 1:
        tr = max(t for t in range(1, 129) if w.shape[0] % t == 0)
        tile = pl.BlockSpec((tr, 1, w.shape[2]), lambda i: (i, 0, 0))
        return pl.pallas_call(
            body, name=name, grid=(w.shape[0] // tr,),
            in_specs=[tile] * 4, out_specs=[tile] * 3, out_shape=out_shape,
            compiler_params=_cparams(("parallel",)),
        )(w, g, m, v)
    if w.ndim == 3 and w.shape[1] % 256 == 0:
        tile = pl.BlockSpec((None, 256, w.shape[2]), lambda l, i: (l, i, 0))
        return pl.pallas_call(
            body, name=name, grid=(w.shape[0], w.shape[1] // 256),
            in_specs=[tile] * 4, out_specs=[tile] * 3, out_shape=out_shape,
            compiler_params=_cparams(("parallel", "parallel")),
        )(w, g, m, v)
    return pl.pallas_call(body, name=name, in_specs=[_VMEM] * 4, out_specs=[_VMEM] * 3, out_shape=out_shape,
                          compiler_params=_cparams())(w, g, m, v)


_MESH = pl.DeviceIdType.MESH
_ANY = pl.BlockSpec(memory_space=pl.ANY)
_VMEM = pl.BlockSpec(memory_space=pltpu.VMEM)


def _place():
    x, y, c = lax.axis_index("x"), lax.axis_index("y"), lax.axis_index("c")
    chips = [(1 - x, y), (x, 1 - y), (1 - x, 1 - y)]
    return x, y, c, 2 * x + y, chips, [2 * cx + cy for cx, cy in chips]


def _half(c, rows):
    return pl.ds(pl.multiple_of(c * (rows // 2), 16), rows // 2)


def _step_rows(rows):
    return max(t for t in range(16, 641, 16) if rows % t == 0)


def place_shard(b, slot, name, dtype=bf16, after=None):
    r, c = b.shape
    tr = _step_rows(r)

    def body(slot_ref, b_ref, *rest):
        rest[-1][...] = b_ref[...].astype(dtype)

    order_specs = [] if after is None else [pl.BlockSpec((8, 128), lambda i, s: (0, 0))]
    return pl.pallas_call(
        body, name=name,
        grid_spec=pltpu.PrefetchScalarGridSpec(
            num_scalar_prefetch=1, grid=(r // tr,),
            in_specs=[pl.BlockSpec((tr, c), lambda i, s: (i, 0))] + order_specs,
            out_specs=pl.BlockSpec((None, tr, c), lambda i, s: (s[0], i, 0))),
        out_shape=jax.ShapeDtypeStruct((N_CHIPS, r, c), dtype),
        compiler_params=_cparams(("parallel",)),
    )(slot, b, *([] if after is None else [after]))


_HBM = pl.BlockSpec(memory_space=pltpu.HBM)
_SEM = pl.BlockSpec(memory_space=pltpu.SEMAPHORE)
_EFFECT = pltpu.SideEffectType.DATAFLOW_SIDE_EFFECTING


def _gather_ici_copies(bufs, send_sems, recv_sems):
    x, y, c, me, chips, chip_idx = _place()
    return [pltpu.make_async_remote_copy(
        src_ref=buf.at[me, _half(c, buf.shape[1])], dst_ref=buf.at[chip_idx[j], _half(c, buf.shape[1])],
        send_sem=send_sems.at[3 * k + j], recv_sem=recv_sems.at[3 * k + j],
        device_id=(*chips[j], c), device_id_type=_MESH) for j in range(3) for k, buf in enumerate(bufs)]


def gather_start(groups, tag):
    sizes = [len(g) for g in groups]
    flat = [b for g in groups for b in g]
    n = len(flat)

    def body(*refs):
        bufs, sems = refs[:n], refs[n:n + 2 * len(groups)]
        refs[-1][...] = jnp.zeros_like(refs[-1])
        x, y, c, me, chips, chip_idx = _place()
        lo = 0
        for gi, size in enumerate(sizes):
            for j in range(3):
                for k, buf in enumerate(bufs[lo:lo + size]):
                    mine = buf.at[me, _half(c, buf.shape[1])]
                    pltpu.make_async_remote_copy(
                        src_ref=mine, dst_ref=mine, send_sem=sems[2 * gi].at[3 * k + j],
                        recv_sem=sems[2 * gi + 1].at[3 * k + j], device_id=(*chips[j], c),
                        device_id_type=_MESH).start()
            lo += size

    sem_shapes = [pltpu.SemaphoreType.DMA((3 * size,)) for size in sizes for _ in range(2)]
    outs = pl.pallas_call(
        body, name=f"gather_start_{tag}",
        out_shape=(*sem_shapes, *[pltpu.HBM(b.shape, b.dtype) for b in flat], jax.ShapeDtypeStruct((8, 128), f32)),
        in_specs=[_HBM] * n, out_specs=(*[_SEM] * len(sem_shapes), *[_HBM] * n, _VMEM),
        input_output_aliases={i: len(sem_shapes) + i for i in range(n)},
        compiler_params=pltpu.CompilerParams(has_side_effects=_EFFECT),
    )(*[pltpu.with_memory_space_constraint(b, pltpu.HBM) for b in flat])
    sems = [(outs[2 * gi], outs[2 * gi + 1]) for gi in range(len(groups))]
    thru, lo = [], len(sem_shapes)
    for size in sizes:
        thru.append(list(outs[lo:lo + size]))
        lo += size
    return sems, thru, outs[-1]


def gather_wait(bufs, sems, after, tag):
    n = len(bufs)

    def body(*refs):
        for cp in _gather_ici_copies(refs[:n], refs[n], refs[n + 1]):
            cp.wait_send()
            cp.wait_recv()

    extra = list(after)
    return list(pl.pallas_call(
        body, name=f"gather_wait_{tag}",
        out_shape=[pltpu.HBM(b.shape, b.dtype) for b in bufs],
        in_specs=[_HBM] * n + [_SEM, _SEM] + [_ANY] * len(extra), out_specs=[_HBM] * n,
        input_output_aliases={i: i for i in range(n)},
        compiler_params=pltpu.CompilerParams(has_side_effects=_EFFECT),
    )(*bufs, *sems, *extra))


def gather_forward(bufs, tag):
    n = len(bufs)

    def body(*refs):
        out_refs = refs[n:2 * n]
        send_sems, recv_sems = refs[2 * n:]
        x, y, c, me, chips, chip_idx = _place()

        def copy(k, j, half):
            part = out_refs[k].at[chip_idx[j], _half(half, out_refs[k].shape[1])]
            return pltpu.make_async_remote_copy(
                src_ref=part, dst_ref=part, send_sem=send_sems.at[3 * k + j], recv_sem=recv_sems.at[3 * k + j],
                device_id=(x, y, 1 - c), device_id_type=_MESH)

        sends = [copy(k, j, c) for j in range(3) for k in range(n)]
        for cp in sends:
            cp.start()
        for j in range(3):
            for k in range(n):
                copy(k, j, 1 - c).wait_recv()
        for cp in sends:
            cp.wait_send()

    return list(pl.pallas_call(
        body, name=f"gather_forward_{tag}",
        out_shape=[jax.ShapeDtypeStruct(b.shape, b.dtype) for b in bufs],
        in_specs=[_ANY] * n, out_specs=[_ANY] * n, input_output_aliases={i: i for i in range(n)},
        scratch_shapes=[pltpu.SemaphoreType.DMA((3 * n,)), pltpu.SemaphoreType.DMA((3 * n,))],
    )(*bufs))


def _forward_copy(ref, k, j, half, send_sems, recv_sems):
    x, y, c, me, chips, chip_idx = _place()
    part = ref.at[chip_idx[j], _half(half, ref.shape[1])]
    return pltpu.make_async_remote_copy(
        src_ref=part, dst_ref=part, send_sem=send_sems.at[3 * k + j], recv_sem=recv_sems.at[3 * k + j],
        device_id=(x, y, 1 - c), device_id_type=_MESH)


def forward_start(bufs, tag):
    n = len(bufs)

    def body(*refs):
        c = _place()[2]
        for j in range(3):
            for k in range(n):
                _forward_copy(refs[k], k, j, c, refs[n], refs[n + 1]).start()
        refs[-1][...] = jnp.zeros_like(refs[-1])

    outs = pl.pallas_call(
        body, name=f"forward_start_{tag}",
        out_shape=(pltpu.SemaphoreType.DMA((3 * n,)), pltpu.SemaphoreType.DMA((3 * n,)),
                   *[pltpu.HBM(b.shape, b.dtype) for b in bufs], jax.ShapeDtypeStruct((8, 128), f32)),
        in_specs=[_HBM] * n, out_specs=(_SEM, _SEM, *[_HBM] * n, _VMEM),
        input_output_aliases={i: 2 + i for i in range(n)},
        compiler_params=pltpu.CompilerParams(has_side_effects=_EFFECT),
    )(*[pltpu.with_memory_space_constraint(b, pltpu.HBM) for b in bufs])
    return (outs[0], outs[1], list(outs[2:2 + n])), outs[-1]


def forward_wait(send_sems, recv_sems, bufs, after, tag):
    n = len(bufs)

    def body(*refs):
        c = _place()[2]
        for j in range(3):
            for k in range(n):
                _forward_copy(refs[k], k, j, c, refs[n], refs[n + 1]).wait_send()
                _forward_copy(refs[k], k, j, 1 - c, refs[n], refs[n + 1]).wait_recv()

    return list(pl.pallas_call(
        body, name=f"forward_wait_{tag}",
        out_shape=[pltpu.HBM(b.shape, b.dtype) for b in bufs],
        in_specs=[_HBM] * n + [_SEM, _SEM, _ANY], out_specs=[_HBM] * n,
        input_output_aliases={i: i for i in range(n)},
        compiler_params=pltpu.CompilerParams(has_side_effects=_EFFECT),
    )(*bufs, send_sems, recv_sems, after))


def exchange_halves(bufs, tag):
    n = len(bufs)

    def body(*refs):
        g_refs, out_refs = refs[:n], refs[n:2 * n]
        send_sems, recv_sems = refs[2 * n:]
        x, y, c, *_ = _place()
        cps = [pltpu.make_async_remote_copy(
            src_ref=g_refs[b].at[:, _half(1 - c, g_refs[b].shape[1])], dst_ref=out_refs[b],
            send_sem=send_sems.at[b], recv_sem=recv_sems.at[b], device_id=(x, y, 1 - c), device_id_type=_MESH)
            for b in range(n)]
        for cp in cps:
            cp.start()
        for cp in cps:
            cp.wait()

    return pl.pallas_call(
        body, name=f"exchange_halves_{tag}",
        out_shape=[jax.ShapeDtypeStruct((N_CHIPS, b.shape[1] // 2, b.shape[2]), b.dtype) for b in bufs],
        in_specs=[_ANY] * n, out_specs=[_ANY] * n,
        scratch_shapes=[pltpu.SemaphoreType.DMA((n,)), pltpu.SemaphoreType.DMA((n,))],
    )(*bufs)


def add_halves(g, got, c_idx, name):
    hr, cols = got.shape[1], got.shape[2]
    tr = _step_rows(hr)
    steps = hr // tr

    def body(c_ref, g_ref, got_ref, o_ref):
        o_ref[...] = (g_ref[...].astype(f32) + got_ref[...].astype(f32)).astype(bf16)

    return pl.pallas_call(
        body, name=name,
        grid_spec=pltpu.PrefetchScalarGridSpec(
            num_scalar_prefetch=1, grid=(N_CHIPS, steps),
            in_specs=[pl.BlockSpec((None, tr, cols), lambda s, i, c: (s, c[0] * steps + i, 0)),
                      pl.BlockSpec((None, tr, cols), lambda s, i, c: (s, i, 0))],
            out_specs=pl.BlockSpec((None, tr, cols), lambda s, i, c: (s, i, 0))),
        out_shape=jax.ShapeDtypeStruct(got.shape, bf16),
        compiler_params=_cparams(("parallel", "parallel")),
    )(c_idx, g, got)


def sum_chips(t, got, place_idx, name):
    hr, cols = t.shape[1], t.shape[2]
    tr = _step_rows(hr)
    steps = hr // tr

    def body(idx_ref, t_ref, got_ref, o_ref):
        acc = t_ref[...].astype(f32)
        for j in range(3):
            acc = acc + got_ref[j].astype(f32)
        o_ref[...] = acc

    return pl.pallas_call(
        body, name=name,
        grid_spec=pltpu.PrefetchScalarGridSpec(
            num_scalar_prefetch=1, grid=(steps,),
            in_specs=[pl.BlockSpec((None, tr, cols), lambda i, idx: (idx[0], i, 0)),
                      pl.BlockSpec((3, tr, cols), lambda i, idx: (0, i, 0))],
            out_specs=pl.BlockSpec((tr, cols), lambda i, idx: (idx[1] * steps + i, 0))),
        out_shape=jax.ShapeDtypeStruct((2 * hr, cols), f32),
        compiler_params=_cparams(("parallel",)),
    )(place_idx, t, got)


def _share_copies(refs, send_sems, recv_sems):
    x, y, c, *_ = _place()
    return [pltpu.make_async_remote_copy(
        src_ref=ref.at[_half(c, ref.shape[0])], dst_ref=ref.at[_half(c, ref.shape[0])], send_sem=send_sems.at[b],
        recv_sem=recv_sems.at[b], device_id=(x, y, 1 - c), device_id_type=_MESH) for b, ref in enumerate(refs)]


def share_start(bufs, tag):
    n = len(bufs)

    def body(*refs):
        for cp in _share_copies(refs[:n], refs[n], refs[n + 1]):
            cp.start()
        token = refs[-1]
        token[...] = jnp.zeros_like(token)

    outs = pl.pallas_call(
        body, name=f"share_start_{tag}",
        out_shape=(pltpu.SemaphoreType.DMA((n,)), pltpu.SemaphoreType.DMA((n,)),
                   *[pltpu.HBM(b.shape, b.dtype) for b in bufs], jax.ShapeDtypeStruct((8, 128), f32)),
        in_specs=[_HBM] * n, out_specs=(_SEM, _SEM, *[_HBM] * n, _VMEM),
        input_output_aliases={i: 2 + i for i in range(n)},
        compiler_params=pltpu.CompilerParams(has_side_effects=_EFFECT),
    )(*[pltpu.with_memory_space_constraint(b, pltpu.HBM) for b in bufs])
    return (outs[0], outs[1], list(outs[2:2 + n])), outs[-1]


def share_wait(send_sems, recv_sems, bufs, after, tag):
    n = len(bufs)

    def body(*refs):
        x, y, c, *_ = _place()
        for b, ref in enumerate(refs[:n]):
            cp = pltpu.make_async_remote_copy(
                src_ref=ref.at[_half(c, ref.shape[0])], dst_ref=ref.at[_half(1 - c, ref.shape[0])],
                send_sem=refs[n].at[b], recv_sem=refs[n + 1].at[b], device_id=(x, y, 1 - c), device_id_type=_MESH)
            cp.wait_send()
            cp.wait_recv()

    return list(pl.pallas_call(
        body, name=f"share_wait_{tag}",
        out_shape=[pltpu.HBM(b.shape, b.dtype) for b in bufs],
        in_specs=[_HBM] * n + [_SEM, _SEM, _ANY], out_specs=[_HBM] * n,
        input_output_aliases={i: i for i in range(n)},
        compiler_params=pltpu.CompilerParams(has_side_effects=_EFFECT),
    )(*bufs, send_sems, recv_sems, after))


def _scatter_copies(t_refs, land_refs, send_sems, recv_sems):
    x, y, c, me, chips, chip_idx = _place()
    return [pltpu.make_async_remote_copy(
        src_ref=t_refs[b].at[chip_idx[j]], dst_ref=land_refs[b].at[j], send_sem=send_sems.at[3 * b + j],
        recv_sem=recv_sems.at[3 * b + j], device_id=(*chips[j], c), device_id_type=_MESH)
        for j in range(3) for b in range(len(t_refs))]


def scatter_start(ts, tag):
    n = len(ts)
    lands = [lax.empty((3,) + t.shape[1:], t.dtype) for t in ts]

    def body(*refs):
        for cp in _scatter_copies(refs[:n], refs[n:2 * n], refs[2 * n], refs[2 * n + 1]):
            cp.start()
        token = refs[-1]
        token[...] = jnp.zeros_like(token)

    hbm = [pltpu.HBM(a.shape, a.dtype) for a in (*ts, *lands)]
    outs = pl.pallas_call(
        body, name=f"scatter_start_{tag}",
        out_shape=(pltpu.SemaphoreType.DMA((3 * n,)), pltpu.SemaphoreType.DMA((3 * n,)), *hbm,
                   jax.ShapeDtypeStruct((8, 128), f32)),
        in_specs=[_HBM] * (2 * n), out_specs=(_SEM, _SEM, *[_HBM] * (2 * n), _VMEM),
        input_output_aliases={i: 2 + i for i in range(2 * n)},
        compiler_params=pltpu.CompilerParams(has_side_effects=_EFFECT),
    )(*[pltpu.with_memory_space_constraint(a, pltpu.HBM) for a in (*ts, *lands)])
    return outs[0], outs[1], list(outs[2:2 + n]), list(outs[2 + n:2 + 2 * n]), outs[-1]


def scatter_wait(send_sems, recv_sems, ts, lands, after, tag):
    n = len(ts)

    def body(*refs):
        for cp in _scatter_copies(refs[:n], refs[n:2 * n], refs[2 * n], refs[2 * n + 1]):
            cp.wait_send()
            cp.wait_recv()

    outs = pl.pallas_call(
        body, name=f"scatter_wait_{tag}",
        out_shape=[pltpu.HBM(a.shape, a.dtype) for a in (*ts, *lands)],
        in_specs=[_HBM] * (2 * n) + [_SEM, _SEM, _ANY], out_specs=[_HBM] * (2 * n),
        input_output_aliases={i: i for i in range(2 * n)},
        compiler_params=pltpu.CompilerParams(has_side_effects=_EFFECT),
    )(*ts, *lands, send_sems, recv_sems, after)
    return list(outs[:n]), list(outs[n:])


N_SENDERS = 7


def _direct_copies(g_refs, land_refs, send_sems, recv_sems):
    x, y, c, me, chips, chip_idx = _place()
    cps = []
    for b, (g, land) in enumerate(zip(g_refs, land_refs)):
        rows, base = g.shape[1], N_SENDERS * b
        cps.append(pltpu.make_async_remote_copy(
            src_ref=g.at[me, _half(1 - c, rows)], dst_ref=land.at[0], send_sem=send_sems.at[base],
            recv_sem=recv_sems.at[base], device_id=(x, y, 1 - c), device_id_type=_MESH))
        for j in range(3):
            for core in range(2):
                cps.append(pltpu.make_async_remote_copy(
                    src_ref=g.at[chip_idx[j], _half(core, rows)], dst_ref=land.at[1 + 2 * j + c],
                    send_sem=send_sems.at[base + 1 + 2 * j + core], recv_sem=recv_sems.at[base + 1 + 2 * j + c],
                    device_id=(*chips[j], core), device_id_type=_MESH))
    return cps


def direct_start(gs, tag):
    n = len(gs)
    lands = [lax.empty((N_SENDERS, g.shape[1] // 2, g.shape[2]), g.dtype) for g in gs]

    def body(*refs):
        for cp in _direct_copies(refs[:n], refs[n:2 * n], refs[2 * n], refs[2 * n + 1]):
            cp.start()
        token = refs[-1]
        token[...] = jnp.zeros_like(token)

    hbm = [pltpu.HBM(a.shape, a.dtype) for a in (*gs, *lands)]
    outs = pl.pallas_call(
        body, name=f"direct_start_{tag}",
        out_shape=(pltpu.SemaphoreType.DMA((N_SENDERS * n,)), pltpu.SemaphoreType.DMA((N_SENDERS * n,)), *hbm,
                   jax.ShapeDtypeStruct((8, 128), f32)),
        in_specs=[_HBM] * (2 * n), out_specs=(_SEM, _SEM, *[_HBM] * (2 * n), _VMEM),
        input_output_aliases={i: 2 + i for i in range(2 * n)},
        compiler_params=pltpu.CompilerParams(has_side_effects=_EFFECT),
    )(*[pltpu.with_memory_space_constraint(a, pltpu.HBM) for a in (*gs, *lands)])
    return outs[0], outs[1], list(outs[2:2 + n]), list(outs[2 + n:2 + 2 * n]), outs[-1]


def direct_wait(send_sems, recv_sems, gs, lands, after, tag):
    n = len(gs)

    def body(*refs):
        g_refs, land_refs, sends, recvs = refs[:n], refs[n:2 * n], refs[2 * n], refs[2 * n + 1]
        for b in range(n):
            for k in range(N_SENDERS):
                cp = pltpu.make_async_remote_copy(
                    src_ref=g_refs[b].at[0, _half(0, g_refs[b].shape[1])], dst_ref=land_refs[b].at[k],
                    send_sem=sends.at[N_SENDERS * b + k], recv_sem=recvs.at[N_SENDERS * b + k],
                    device_id=_place()[:3], device_id_type=_MESH)
                cp.wait_send()
                cp.wait_recv()

    outs = pl.pallas_call(
        body, name=f"direct_wait_{tag}",
        out_shape=[pltpu.HBM(a.shape, a.dtype) for a in (*gs, *lands)],
        in_specs=[_HBM] * (2 * n) + [_SEM, _SEM, _ANY], out_specs=[_HBM] * (2 * n),
        input_output_aliases={i: i for i in range(2 * n)},
        compiler_params=pltpu.CompilerParams(has_side_effects=_EFFECT),
    )(*gs, *lands, send_sems, recv_sems, after)
    return list(outs[:n]), list(outs[n:])


def sum_senders(g, lands, place_idx, name):
    hr, cols = lands.shape[1], lands.shape[2]
    tr = _step_rows(hr)
    steps = hr // tr

    def body(idx_ref, g_ref, land_ref, o_ref):
        acc = g_ref[...].astype(f32)
        for k in range(N_SENDERS):
            acc = acc + land_ref[k].astype(f32)
        o_ref[...] = acc

    return pl.pallas_call(
        body, name=name,
        grid_spec=pltpu.PrefetchScalarGridSpec(
            num_scalar_prefetch=1, grid=(steps,),
            in_specs=[pl.BlockSpec((None, tr, cols), lambda i, idx: (idx[0], idx[1] * steps + i, 0)),
                      pl.BlockSpec((N_SENDERS, tr, cols), lambda i, idx: (0, i, 0))],
            out_specs=pl.BlockSpec((tr, cols), lambda i, idx: (idx[1] * steps + i, 0))),
        out_shape=jax.ShapeDtypeStruct((2 * hr, cols), f32),
        compiler_params=_cparams(("parallel",)),
    )(place_idx, g, lands)


class GradReducer:
    def __init__(self, c_idx, place_idx):
        self.c_idx, self.place_idx = c_idx, place_idx

    def start(self, bufs, tag, direct=False):
        if direct:
            send_sems, recv_sems, gs, lands, token = direct_start(bufs, tag)
            return (True, send_sems, recv_sems, gs, lands), token
        got = exchange_halves(bufs, tag)
        ts = [add_halves(b, g, self.c_idx, f"add_halves_{tag}{i}") for i, (b, g) in enumerate(zip(bufs, got))]
        send_sems, recv_sems, ts, lands, token = scatter_start(ts, tag)
        return (False, send_sems, recv_sems, ts, lands), token

    def finish(self, state, after, tag):
        direct, *flight = state
        if direct:
            gs, lands = direct_wait(*flight, after, tag)
            sums = [sum_senders(g, l, self.place_idx, f"sum_senders_{tag}{i}") for i, (g, l) in enumerate(zip(gs, lands))]
        else:
            ts, lands = scatter_wait(*flight, after, tag)
            sums = [sum_chips(t, l, self.place_idx, f"sum_chips_{tag}{i}") for i, (t, l) in enumerate(zip(ts, lands))]
        return share_start(sums, tag)

    def collect(self, pending, after, tag):
        return share_wait(*pending, after, tag)


def allreduce_small(sp):
    rows = sp.shape[0]
    hr = rows // 2

    def body(s_ref, out_ref, sib_ref, chip_ref, four_ref, send_sems, recv_sems):
        x, y, c, me, chips, chip_idx = _place()
        sibling = (x, y, 1 - c)
        mine = pl.ds(pl.multiple_of(c * hr, 8), hr)
        other = pl.ds(pl.multiple_of((1 - c) * hr, 8), hr)

        swap = pltpu.make_async_remote_copy(src_ref=s_ref, dst_ref=sib_ref, send_sem=send_sems.at[0],
                                            recv_sem=recv_sems.at[0], device_id=sibling, device_id_type=_MESH)
        swap.start()
        swap.wait()
        is_core0 = c == 0
        chip_ref[...] = jnp.where(is_core0, s_ref[...], sib_ref[...]) + jnp.where(is_core0, sib_ref[...], s_ref[...])

        sends = [pltpu.make_async_remote_copy(
            src_ref=chip_ref.at[mine], dst_ref=four_ref.at[me], send_sem=send_sems.at[1 + j],
            recv_sem=recv_sems.at[1 + j], device_id=(*chips[j], c), device_id_type=_MESH) for j in range(3)]
        for cp in sends:
            cp.start()
        four_ref[me] = chip_ref[mine, :]
        for j in range(3):
            pltpu.make_async_remote_copy(
                src_ref=chip_ref.at[mine], dst_ref=four_ref.at[chip_idx[j]], send_sem=send_sems.at[1 + j],
                recv_sem=recv_sems.at[1 + j], device_id=(*chips[j], c), device_id_type=_MESH).wait_recv()
        for cp in sends:
            cp.wait_send()
        out_ref[mine, :] = (four_ref[0] + four_ref[1]) + (four_ref[2] + four_ref[3])

        share = pltpu.make_async_remote_copy(src_ref=out_ref.at[mine], dst_ref=out_ref.at[mine], send_sem=send_sems.at[4],
                                             recv_sem=recv_sems.at[4], device_id=sibling, device_id_type=_MESH)
        share.start()
        pltpu.make_async_remote_copy(src_ref=out_ref.at[mine], dst_ref=out_ref.at[other], send_sem=send_sems.at[4],
                                     recv_sem=recv_sems.at[4], device_id=sibling, device_id_type=_MESH).wait_recv()
        share.wait_send()

    return pl.pallas_call(
        body, name="allreduce_small",
        out_shape=jax.ShapeDtypeStruct(sp.shape, sp.dtype),
        in_specs=[_VMEM], out_specs=_VMEM,
        scratch_shapes=[pltpu.VMEM(sp.shape, sp.dtype), pltpu.VMEM(sp.shape, sp.dtype),
                        pltpu.VMEM((N_CHIPS, hr, sp.shape[1]), sp.dtype),
                        pltpu.SemaphoreType.DMA((5,)), pltpu.SemaphoreType.DMA((5,))],
        compiler_params=_cparams(),
    )(sp)


def _n_rows(shape):
    n = 1
    for d in shape:
        n *= d
    return 8 * (-(-n // 8192))


def _pack(arrays, total_rows):
    parts = []
    for a in arrays:
        flat = a.reshape(-1)
        parts.append(jnp.pad(flat, (0, 1024 * _n_rows(a.shape) - flat.shape[0])).reshape(-1, 1024))
    rows = jnp.concatenate(parts, axis=0)
    return jnp.pad(rows, ((0, total_rows - rows.shape[0]), (0, 0)))


def _unpack(packed, shapes):
    out, r = [], 0
    for shp in shapes:
        n = 1
        for d in shp:
            n *= d
        nr = _n_rows(shp)
        out.append(packed[r:r + nr].reshape(-1)[:n].reshape(shp))
        r += nr
    return out


_COLUMN_SHARDED = ("w_in_even", "w_qkv")
IN_SHARD, IN_PAD = 1284, 1408
QKV_SHARD, QKV_PAD = 320, 384


def _lane_padded(a, cols):
    return jnp.pad(a, ((0, 0), (0, cols - a.shape[1])))


_SMALL_SHAPES = (
    ("norm_mix_g", (2, 1024)), ("norm_mlp_g", (2, 1024)), ("final_norm_g", (1024,)), ("gm_ln_g", (1, 1024)),
    ("gm_ln_b", (1, 1024)), ("gm_w_s", (1, 8, 128, 128)), ("gm_b_s", (1, 8, 128)), ("ssm_conv_b", (1, 2048)),
    ("ssm_dt_bias", (1, 16)), ("ssm_a_log", (1, 16)), ("ssm_d", (1, 16)), ("ssm_norm_g", (1, 1024)),
    ("attn_sinks", (1, 16)), ("ssm_conv_w", (1, 4, 2048)), ("b_qkv", (1, 1280)), ("b_o", (1, 1024)),
)
_N_REPLICATED = 13
_SHARDED_SMALL = (("ssm_conv_w", 2, 512), ("b_qkv", 1, 320), ("b_o", 1, 256))
_SHARD_PACK_ROWS = 32


def _cols_by_owner(a):
    return a.transpose(1, 0, 2).reshape(a.shape[1], -1)


class WeightGatherer:
    def __init__(self, w, chip_idx):
        def place(tag, b, dtype=bf16, after=None):
            return place_shard(b, chip_idx, f"place_shard_{tag}", dtype, after)

        sems_in, bufs_in, self.started = gather_start([
            [place("in", _lane_padded(w["w_in_even"][0].astype(bf16), IN_PAD)),
             place("small", _pack([w[n] for n, _, _ in _SHARDED_SMALL], _SHARD_PACK_ROWS), f32)]], "in")
        t = self.started
        sems, bufs, self.all_started = gather_start([
            [place("out", w["w_out_even"][0], after=t), place("up0", w["w_up"][0], after=t),
             place("down0", w["w_down"][0], after=t)],
            [place("qkv", _lane_padded(w["w_qkv"][0], QKV_PAD), after=t), place("o", w["w_o"][0], after=t),
             place("up1", w["w_up"][1], after=t), place("down1", w["w_down"][1], after=t)],
        ], "rest")
        self.sems, self.bufs = sems_in + sems, bufs_in + bufs

    def _group(self, gi, after, tag):
        return gather_forward(gather_wait(self.bufs[gi], self.sems[gi], after, tag), tag)

    def mixer_in(self, after):
        g, small = self._group(0, [after, self.all_started], "in")
        shard_shapes = [tuple(width if i == axis else d for i, d in enumerate(dict(_SMALL_SHAPES)[n]))
                        for n, axis, width in _SHARDED_SMALL]
        per_chip = [_unpack(small[s], shard_shapes) for s in range(N_CHIPS)]
        full = {n: jnp.concatenate([per_chip[s][i] for s in range(N_CHIPS)], axis=axis)
                for i, (n, axis, _) in enumerate(_SHARDED_SMALL)}
        w_in_p = jnp.concatenate([g[s, :, :IN_SHARD] for s in range(N_CHIPS)]
                                 + [jnp.zeros((g.shape[1], NP_IN - IN_EVEN), g.dtype)], axis=1)
        return w_in_p, full

    def layer0(self, after):
        w_out, w_up, w_down = self._group(1, [after], "l0")
        return w_out.reshape(2048, 1024), w_up, w_down.reshape(4096, 1024)

    def layer1_start(self, after):
        return forward_start(gather_wait(self.bufs[2], self.sems[2], [after], "l1"), "l1")

    def layer1(self, pending, after):
        q, w_o, w_up, w_down = forward_wait(*pending, after, "l1")
        w_qkv = jnp.concatenate([q[s, :, :QKV_SHARD] for s in range(N_CHIPS)], axis=1)
        return w_qkv, w_o.reshape(1024, 1024), w_up, w_down.reshape(4096, 1024)


def _row2(v):
    return v.reshape(1, -1)


def _lane_pad(v):
    return jnp.pad(v, ((0, 0), (0, CH - v.shape[1])))


_H_AND_NORM = (("tile", f32), ("tile", bf16))
_DX_AND_DG = (("tile", f32), ("sum", D_MODEL))


def _mlp_bwd(dh_out, h, g_row, y, a, w_up, w_down, tag, after=None):
    da = matmul(dh_out, w_down, dims="nt", name=f"mlp_da{tag}", out_dtype=bf16, tm=2048, tn=1024,
                epi=_times_relu2_grad, epi_args=(("tile", a),), after=after)
    dw_down = matmul(a, dh_out, dims="tn", name=f"mlp_dwdown{tag}", out_dtype=bf16, a_pro=_relu2)
    dw_up = matmul(y, da, dims="tn", name=f"mlp_dwup{tag}", out_dtype=bf16, tn=1024, out_by_col_tile=True)
    dh, dg, dh_colsum = matmul_rows(da, w_up, dims="nt", name=f"mlp_dy{tag}", epi=_norm_bwd_res_colsum,
                                    epi_args=(("tile", h), ("row", g_row), ("tile", dh_out)),
                                    outs=_DX_AND_DG + (("sum", D_MODEL),))
    return dh, dg, dw_up, dw_down, dh_colsum


def _by_owner(a):
    return a.reshape(N_CHIPS, a.shape[0] // N_CHIPS, a.shape[1])


def _row_shards(a, shard, padded):
    return jnp.stack([jnp.pad(a[shard * s: shard * (s + 1)], ((0, padded - shard), (0, 0))) for s in range(N_CHIPS)])


def _col_shards(a, shard, padded):
    return jnp.stack([_lane_padded(a[:, shard * s: shard * (s + 1)], padded) for s in range(N_CHIPS)])


def _local_step(x, target, weights, sm, reducer):
    w_up, w_down = [None, None], [None, None]
    mix_g = [_row2(sm["norm_mix_g"][i]) for i in range(2)]
    y0 = rmsnorm_fwd(x, mix_g[0] + weights.started[:1, :1], "mix_norm0")
    w_in_p, sharded_small = weights.mixer_in(y0)
    sm = {**sm, **sharded_small}
    mlp_g = [_row2(sm["norm_mlp_g"][i]) for i in range(2)]
    mixer_prm = {
        "ln_g": sm["gm_ln_g"], "ln_b": sm["gm_ln_b"], "wm": sm["gm_w_s"][0],
        "bs_t": jnp.pad(sm["gm_b_s"][0].T, ((0, 0), (0, CH - N_BLK))),
        "conv_w": jnp.pad(sm["ssm_conv_w"][0], ((0, 4), (0, 0))), "conv_b": sm["ssm_conv_b"],
        "dt_bias": _lane_pad(sm["ssm_dt_bias"]), "a_log": _lane_pad(sm["ssm_a_log"]),
        "d_heads": _lane_pad(sm["ssm_d"]), "norm_g": sm["ssm_norm_g"],
    }
    sink_row = _lane_pad(sm["attn_sinks"])

    proj = matmul(y0, w_in_p, dims="nn", name="in_proj", tm=2048, tn=768)
    ab, hstates = mixer_fwd(proj, mixer_prm)
    w_out, w_up[0], w_down[0] = weights.layer0(ab)
    h1, y1 = matmul_rows(ab, w_out, dims="nn", name="out_proj", epi=_res_norm,
                         epi_args=(("tile", x), ("row", mlp_g[0])), outs=_H_AND_NORM)
    a1 = matmul(y1, w_up[0], dims="nn", name="mlp_up0", out_dtype=bf16, tm=2048, tn=1024)
    pending_l1, forwarding_l1 = weights.layer1_start(a1)
    h2, y2 = matmul_rows(a1, w_down[0], dims="nn", name="mlp_down0", a_pro=_relu2, epi=_res_norm,
                         epi_args=(("tile", h1), ("row", mix_g[1])), outs=_H_AND_NORM, after=forwarding_l1)
    w_qkv, w_o, w_up[1], w_down[1] = weights.layer1(pending_l1, h2)
    qkv = matmul(y2, w_qkv, dims="nn", name="qkv_proj", tn=QKV_DIM, epi=_add_bias, epi_args=(("row", sm["b_qkv"]),))
    att = attn_fwd(qkv, sink_row)
    h3, y3 = matmul_rows(att, w_o, dims="nn", name="o_proj", epi=_bias_res_norm,
                         epi_args=(("row", sm["b_o"]), ("tile", h2), ("row", mlp_g[1])), outs=_H_AND_NORM)
    a3 = matmul(y3, w_up[1], dims="nn", name="mlp_up1", out_dtype=bf16, tm=2048, tn=1024)
    dh4, dg_final, loss = matmul_rows(
        a3, w_down[1], dims="nn", name="mlp_down1", a_pro=_relu2, epi=_res_norm_loss,
        epi_args=(("tile", h3), ("row", _row2(sm["final_norm_g"])), ("tile", target)),
        outs=(("tile", f32), ("sum", D_MODEL), ("sum", 128)))

    dh3, dg_mlp1, dw_up1, dw_down1, db_o = _mlp_bwd(dh4, h3, mlp_g[1], y3, a3, w_up[1], w_down[1], 1)
    datt = matmul(dh3, w_o, dims="nt", name="attn_dout", out_dtype=bf16)
    dw_o = matmul(att, dh3, dims="tn", name="dw_o", out_dtype=bf16)
    dqkv, dsink, db_qkv = attn_bwd(qkv, sink_row, datt)
    dw_qkv = matmul(y2, dqkv, dims="tn", name="dw_qkv", out_dtype=bf16, tn=QKV_DIM)
    dh2, dg_mix1 = matmul_rows(dqkv, w_qkv, dims="nt", name="dy_qkv", epi=_norm_bwd_res,
                               epi_args=(("tile", h2), ("row", mix_g[1]), ("tile", dh3)), outs=_DX_AND_DG)
    layer1 = [jnp.concatenate([_by_owner(dw_o), dw_up1, _by_owner(dw_down1)], axis=1),
              _col_shards(dw_qkv, QKV_SHARD, QKV_PAD)]
    flight1, token1 = reducer.start(layer1, "l1", direct=True)
    dh1, dg_mlp0, dw_up0, dw_down0, _ = _mlp_bwd(dh2, h1, mlp_g[0], y1, a1, w_up[0], w_down[0], 0, after=token1)
    pending1, shared1 = reducer.finish(flight1, dh1, "l1")
    dw_out = matmul(ab, dh1, dims="tn", name="dw_out", out_dtype=bf16, after=shared1)
    flight0, token0 = reducer.start(
        [jnp.concatenate([dw_up0, _by_owner(dw_down0), _by_owner(dw_out)], axis=1)], "l0", direct=True)
    dab = matmul(dh1, w_out, dims="nt", name="mixer_dout", tm=2048, tn=1024, after=token0)
    dproj, dmix = mixer_bwd(proj, hstates, dab, mixer_prm)
    dw_in_t = matmul(dproj, y0, dims="tn", name="dw_in", out_dtype=bf16, tm=768, tn=1024)
    pending0, shared0 = reducer.finish(flight0, dw_in_t, "l0")
    flight_in, token_in = reducer.start([_row_shards(dw_in_t, IN_SHARD, IN_PAD)], "in")
    dx, dg_mix0 = matmul_rows(dproj, w_in_p, dims="nt", name="dy_in", tm=256, epi=_norm_bwd_res,
                              epi_args=(("tile", x), ("row", mix_g[0]), ("tile", dh1)), outs=_DX_AND_DG,
                              after=token_in + shared0)
    pending_in, _ = reducer.finish(flight_in, dx, "in")
    r_l1, r_qkv = reducer.collect(pending1, dx, "l1")
    (r_l0,) = reducer.collect(pending0, dx, "l0")
    (r_in,) = reducer.collect(pending_in, dx, "in")
    reduced = {
        "w_out_even": [(r_l0, 2048)], "w_o": [(r_l1, 0)], "w_up": [(r_l0, 0), (r_l1, 256)],
        "w_down": [(r_l0, 1024), (r_l1, 1280)],
        "w_in_even": r_in[:IN_SHARD].T[None], "w_qkv": r_qkv[None, :, :QKV_SHARD],
    }

    small_grads = {
        "norm_mix_g": jnp.concatenate([dg_mix0, dg_mix1], axis=0),
        "norm_mlp_g": jnp.concatenate([dg_mlp0, dg_mlp1], axis=0),
        "final_norm_g": dg_final[0], "gm_ln_g": dmix["ln_g"], "gm_ln_b": dmix["ln_b"],
        "gm_w_s": dmix["wm"][None], "gm_b_s": dmix["bs_t"][:, :N_BLK].T[None],
        "ssm_conv_b": dmix["conv_b"], "ssm_dt_bias": dmix["dt_bias"][:, :SSM_HEADS],
        "ssm_a_log": dmix["a_log"][:, :SSM_HEADS], "ssm_d": dmix["d_heads"][:, :SSM_HEADS],
        "ssm_norm_g": dmix["norm_g"], "attn_sinks": dsink[:, :SSM_HEADS],
        "ssm_conv_w": dmix["conv_w"][None, :4], "b_qkv": db_qkv, "b_o": db_o,
    }
    return loss, dx, reduced, small_grads


def kernel(x, norm_mix_g, norm_mlp_g, final_norm_g, w_in_even, w_out_even, gm_ln_g, gm_ln_b, gm_w_s, gm_b_s, ssm_conv_w, ssm_conv_b, ssm_dt_bias, ssm_a_log, ssm_d, ssm_norm_g, w_qkv, b_qkv, w_o, b_o, attn_sinks, w_up, w_down, loss_target, m_norm_mix_g, m_norm_mlp_g, m_final_norm_g, m_w_in_even, m_w_out_even, m_gm_ln_g, m_gm_ln_b, m_gm_w_s, m_gm_b_s, m_ssm_conv_w, m_ssm_conv_b, m_ssm_dt_bias, m_ssm_a_log, m_ssm_d, m_ssm_norm_g, m_w_qkv, m_b_qkv, m_w_o, m_b_o, m_attn_sinks, m_w_up, m_w_down, v_norm_mix_g, v_norm_mlp_g, v_final_norm_g, v_w_in_even, v_w_out_even, v_gm_ln_g, v_gm_ln_b, v_gm_w_s, v_gm_b_s, v_ssm_conv_w, v_ssm_conv_b, v_ssm_dt_bias, v_ssm_a_log, v_ssm_d, v_ssm_norm_g, v_w_qkv, v_b_qkv, v_w_o, v_b_o, v_attn_sinks, v_w_up, v_w_down):
    w = dict(norm_mix_g=norm_mix_g, norm_mlp_g=norm_mlp_g, final_norm_g=final_norm_g, w_in_even=w_in_even,
             w_out_even=w_out_even, gm_ln_g=gm_ln_g, gm_ln_b=gm_ln_b, gm_w_s=gm_w_s, gm_b_s=gm_b_s,
             ssm_conv_w=ssm_conv_w, ssm_conv_b=ssm_conv_b, ssm_dt_bias=ssm_dt_bias, ssm_a_log=ssm_a_log,
             ssm_d=ssm_d, ssm_norm_g=ssm_norm_g, w_qkv=w_qkv, b_qkv=b_qkv, w_o=w_o, b_o=b_o,
             attn_sinks=attn_sinks, w_up=w_up, w_down=w_down)
    m = dict(norm_mix_g=m_norm_mix_g, norm_mlp_g=m_norm_mlp_g, final_norm_g=m_final_norm_g,
             w_in_even=m_w_in_even, w_out_even=m_w_out_even, gm_ln_g=m_gm_ln_g, gm_ln_b=m_gm_ln_b,
             gm_w_s=m_gm_w_s, gm_b_s=m_gm_b_s, ssm_conv_w=m_ssm_conv_w, ssm_conv_b=m_ssm_conv_b,
             ssm_dt_bias=m_ssm_dt_bias, ssm_a_log=m_ssm_a_log, ssm_d=m_ssm_d, ssm_norm_g=m_ssm_norm_g,
             w_qkv=m_w_qkv, b_qkv=m_b_qkv, w_o=m_w_o, b_o=m_b_o, attn_sinks=m_attn_sinks, w_up=m_w_up,
             w_down=m_w_down)
    v = dict(norm_mix_g=v_norm_mix_g, norm_mlp_g=v_norm_mlp_g, final_norm_g=v_final_norm_g,
             w_in_even=v_w_in_even, w_out_even=v_w_out_even, gm_ln_g=v_gm_ln_g, gm_ln_b=v_gm_ln_b,
             gm_w_s=v_gm_w_s, gm_b_s=v_gm_b_s, ssm_conv_w=v_ssm_conv_w, ssm_conv_b=v_ssm_conv_b,
             ssm_dt_bias=v_ssm_dt_bias, ssm_a_log=v_ssm_a_log, ssm_d=v_ssm_d, ssm_norm_g=v_ssm_norm_g,
             w_qkv=v_w_qkv, b_qkv=v_b_qkv, w_o=v_w_o, b_o=v_b_o, attn_sinks=v_attn_sinks, w_up=v_w_up,
             w_down=v_w_down)
    names = ("norm_mix_g", "norm_mlp_g", "final_norm_g", "w_in_even", "w_out_even", "gm_ln_g", "gm_ln_b",
             "gm_w_s", "gm_b_s", "ssm_conv_w", "ssm_conv_b", "ssm_dt_bias", "ssm_a_log", "ssm_d", "ssm_norm_g",
             "w_qkv", "b_qkv", "w_o", "b_o", "attn_sinks", "w_up", "w_down")

    cx, cy, cc = lax.axis_index("x"), lax.axis_index("y"), lax.axis_index("c")
    chip = 2 * cx + cy
    c_idx = jnp.reshape(cc, (1,)).astype(jnp.int32)
    chip_idx = jnp.reshape(chip, (1,)).astype(jnp.int32)

    weights = WeightGatherer(w, chip_idx)
    sm = {n: w[n] for n, _ in _SMALL_SHAPES[:_N_REPLICATED]}

    reducer = GradReducer(c_idx, jnp.concatenate([chip_idx, c_idx]))
    loss_part, dx, grads, small_grads = _local_step(x[0], loss_target[0], weights, sm, reducer)

    small_sum = allreduce_small(_pack([small_grads[n] for n, _ in _SMALL_SHAPES] + [loss_part], SMALL_ROWS))
    *small_list, loss_row = _unpack(small_sum, [s for _, s in _SMALL_SHAPES] + [loss_part.shape])
    loss = loss_row[0, 0]
    small_full = dict(zip([n for n, _ in _SMALL_SHAPES], small_list))
    for n, _ in _SMALL_SHAPES[:_N_REPLICATED]:
        grads[n] = small_full[n]
    for n, axis, width in _SHARDED_SMALL:
        grads[n] = lax.dynamic_slice_in_dim(small_full[n], chip * width, width, axis)

    delta, new_m, new_v = {}, {}, {}
    for n in names:
        if isinstance(grads[n], list):
            outs = None
            for layer, (buf, row_off) in enumerate(grads[n]):
                outs = adamw_rows(w[n], buf, m[n], v[n], layer, row_off, f"adamw_{n}{layer}", into=outs)
            grads[n], delta[n], new_m[n], new_v[n] = outs
            continue
        grads[n] = grads[n].reshape(w[n].shape)
        if n in _COLUMN_SHARDED:
            args = [jnp.transpose(d[n], (2, 0, 1)) for d in (w, grads, m, v)]
            grads[n] = jnp.transpose(args[1], (1, 2, 0))
            outs = adamw(*args, f"adamw_{n}")
            delta[n], new_m[n], new_v[n] = (jnp.transpose(o, (1, 2, 0)) for o in outs)
            continue
        shape = (1,) + w[n].shape if w[n].ndim == 1 else w[n].shape
        outs = adamw(*[d[n].reshape(shape) for d in (w, grads, m, v)], f"adamw_{n}")
        delta[n], new_m[n], new_v[n] = (o.reshape(w[n].shape) for o in outs)

    return (loss, dx[None], *[grads[n] for n in names], *[delta[n] for n in names],
            *[new_m[n] for n in names], *[new_v[n] for n in names])
```

```python
import functools

import jax
import jax.numpy as jnp
from jax import lax
from jax.experimental import pallas as pl
from jax.experimental.pallas import tpu as pltpu

f32 = jnp.float32
bf16 = jnp.bfloat16
MXU_DTYPE = bf16

RMS_EPS = 1e-5
LN_EPS = 1e-5
D_MODEL = 1024
D_FF = 4096
CH = 128
N_BLK = 8
SSM_HEADS = 16
IN_EVEN = 5136
NP_IN = 5376
OFF_U, OFF_V, OFF_Z, OFF_X, OFF_DT = 0, 1024, 2048, 3072, 5120
XBC_BLKS = 16
QKV_DIM = 1280
ATT_SCALE = 64 ** -0.5

ADAM_LR = 0.001
ADAM_B1 = 0.9
ADAM_B2 = 0.999
ADAM_EPS = 1e-08
ADAM_WD = 0.01
ADAM_STEP = 10

VMEM_LIMIT_BYTES = 48 * 1024 * 1024
N_CHIPS = 4
SMALL_ROWS = 256

NN = ((1,), (0,))
NT = ((1,), (1,))
TN = ((0,), (0,))


def _mm(a, b, dims):
    return lax.dot_general(a.astype(MXU_DTYPE), b.astype(MXU_DTYPE), (dims, ((), ())),
                           preferred_element_type=f32)


def _mm_exact(a, b):
    return jnp.dot(a, b, preferred_element_type=f32, precision=lax.Precision.HIGHEST)


def _cparams(sem=None):
    return pltpu.CompilerParams(dimension_semantics=sem, vmem_limit_bytes=VMEM_LIMIT_BYTES)


@jax.custom_vjp
def _swap64(x):
    return pltpu.roll(x, 64, axis=1)


_swap64.defvjp(lambda x: (pltpu.roll(x, 64, axis=1), None), lambda _, g: (pltpu.roll(g, 64, axis=1),))


def _row_blocks_of(x):
    return tuple(x[i:i + CH] for i in range(0, x.shape[0], CH))


@jax.custom_vjp
def _row_blocks(x):
    return _row_blocks_of(x)


_row_blocks.defvjp(lambda x: (_row_blocks_of(x), None), lambda _, gs: (jnp.concatenate(gs, axis=0),))


def _make_delay(k):
    @jax.custom_vjp
    def delay(ext):
        return pltpu.roll(ext, k, axis=0)[8:, :]

    def fwd(ext):
        return delay(ext), None

    def bwd(_, g):
        gp = jnp.concatenate([jnp.zeros((8, g.shape[1]), g.dtype), g], axis=0)
        return (pltpu.roll(gp, gp.shape[0] - k, axis=0),)

    delay.defvjp(fwd, bwd)
    return delay


_DELAYS = {k: _make_delay(k) for k in (1, 2, 3)}


_GELU_C = 0.7978845608028654
_GELU_K = 0.044715


@jax.custom_vjp
def _gelu(x):
    return 0.5 * x * (1.0 + jnp.tanh(_GELU_C * (x + _GELU_K * (x * x * x))))


def _gelu_fwd(x):
    t = jnp.tanh(_GELU_C * (x + _GELU_K * (x * x * x)))
    return 0.5 * x * (1.0 + t), (x, t)


def _gelu_bwd(res, g):
    x, t = res
    dz = _GELU_C + (3.0 * _GELU_C * _GELU_K) * (x * x)
    return (g * (0.5 * (1.0 + t) + (0.5 * x) * (1.0 - t * t) * dz),)


_gelu.defvjp(_gelu_fwd, _gelu_bwd)


def _col(m, lane, h):
    return jnp.sum(jnp.where(lane == h, m, 0.0), axis=1, keepdims=True)


@functools.lru_cache(maxsize=None)
def _row_picker(h, shape):
    @jax.custom_vjp
    def pick(m):
        return m[h:h + 1, :]

    def bwd(_, g):
        return (jnp.where(lax.broadcasted_iota(jnp.int32, shape, 0) == h, g, 0.0),)

    pick.defvjp(lambda m: (m[h:h + 1, :], None), bwd)
    return pick


def _row(m, sub, h):
    return _row_picker(h, m.shape)(m)


def _mixer_chunk(us, vs, zs, xbcs, halos, dtblk, hps, prm):
    lane = lax.broadcasted_iota(jnp.int32, (CH, CH), 1)
    sub = lax.broadcasted_iota(jnp.int32, (CH, CH), 0)
    left = lane < 64
    top = sub < 64
    causal = sub >= lane

    gus = [_gelu(u) for u in us]
    gvs = [_gelu(v) for v in vs]
    mu = sum(jnp.sum(g, axis=1, keepdims=True) for g in gvs) / D_MODEL
    cen = [g - mu for g in gvs]
    var = sum(jnp.sum(c * c, axis=1, keepdims=True) for c in cen) / D_MODEL
    rstd = lax.rsqrt(var + LN_EPS)
    a_out = []
    for g in range(N_BLK):
        vn = cen[g] * rstd * prm["ln_g"][g] + prm["ln_b"][g]
        w = jnp.where(causal, prm["wm"][g], 0.0)
        mixed = _mm(w, vn, NN) + _col(prm["bs_t"], lane, g)
        a_out.append(gus[g] * mixed)

    act = []
    for b in range(XBC_BLKS):
        w8 = prm["conv_w"][b]
        sub8 = lax.broadcasted_iota(jnp.int32, w8.shape, 0)
        ext = jnp.concatenate([halos[b], xbcs[b]], axis=0)
        conv = xbcs[b] * _row(w8, sub8, 3) + prm["conv_b"][b]
        for k in (1, 2, 3):
            conv = conv + _DELAYS[k](ext) * _row(w8, sub8, 3 - k)
        act.append(jax.nn.silu(conv))

    dt = jax.nn.softplus(dtblk + prm["dt_bias"])
    a_neg = -jnp.exp(prm["a_log"])
    tri = causal.astype(f32)
    acum = _mm_exact(tri, dt * a_neg)
    acum_t = acum.T
    dt_t = dt.T
    last = sub == CH - 1
    ys, h_out = [], []
    for grp in range(4):
        bm = act[8 + grp]
        cm = act[12 + grp]
        cb = _mm(cm, bm, NT)
        for p in (2 * grp, 2 * grp + 1):
            h0, h1 = 2 * p, 2 * p + 1
            xp = act[p]
            hp = hps[p]
            wis = []
            for h in (h0, h1):
                seg = _col(acum, lane, h) - _row(acum_t, sub, h)
                decay = jnp.exp(jnp.where(causal, seg, -jnp.inf))
                wis.append(cb * decay * _row(dt_t, sub, h))
            wcat = jnp.concatenate(wis, axis=1)
            xbd = jnp.concatenate([jnp.where(left, xp, 0.0), jnp.where(left, 0.0, xp)], axis=0)
            y_diag = _mm(wcat, xbd, NN)
            a_end = [jnp.sum(jnp.where(last & (lane == h), acum, 0.0), keepdims=True) for h in (h0, h1)]
            a_col = jnp.where(left, _col(acum, lane, h0), _col(acum, lane, h1))
            dt_col = jnp.where(left, _col(dt, lane, h0), _col(dt, lane, h1))
            to_end = jnp.exp(jnp.where(left, a_end[0], a_end[1]) - a_col) * dt_col
            states = _mm(xp * to_end, bm, TN)
            chunk_decay = jnp.where(top, jnp.exp(a_end[0]), jnp.exp(a_end[1]))
            h_out.append(chunk_decay * hp + states)
            y_off = jnp.exp(a_col) * _mm(cm, hp, NT)
            d_skip = jnp.where(left[:1], _col(prm["d_heads"], lane[:1], h0), _col(prm["d_heads"], lane[:1], h1))
            ys.append((y_diag + y_off + xp * d_skip) * jax.nn.silu(zs[p]))

    b_out = []
    for grp in range(4):
        pair = (ys[2 * grp], ys[2 * grp + 1])
        ms = sum(jnp.sum(y * y, axis=1, keepdims=True) for y in pair) / 256.0
        r = lax.rsqrt(ms + RMS_EPS)
        for j, y in enumerate(pair):
            b_out.append(y * r * prm["norm_g"][2 * grp + j])
    return a_out, b_out, h_out


def _attn_block(qps, kprev, kcur, vprev, vcur, sink_row, first):
    lane = lax.broadcasted_iota(jnp.int32, (CH, CH), 1)
    left = lane < 64
    own = lane <= lax.broadcasted_iota(jnp.int32, (CH, CH), 0)
    own8 = jnp.concatenate([own] * N_BLK, axis=0)

    def both_halves(a):
        sw = _swap64(a)
        return [jnp.where(left, a, sw), jnp.where(left, sw, a)]

    kc, kp, vc, vp = both_halves(kcur), both_halves(kprev), both_halves(vcur), both_halves(vprev)
    outs = []
    for j in range(2):
        q8 = jnp.concatenate([part for p in range(4 * j, 4 * j + 4)
                              for part in (jnp.where(left, qps[p], 0.0), jnp.where(left, 0.0, qps[p]))], axis=0)
        s_cur = _row_blocks(_mm(q8, kc[j], NT))
        s_prev = _row_blocks(_mm(q8, kp[j], NT))
        probs = []
        for h in range(N_BLK):
            s = jnp.where(own, s_cur[h] * ATT_SCALE, jnp.where(first, -jnp.inf, s_prev[h] * ATT_SCALE))
            sink = _col(sink_row, lane[:1], N_BLK * j + h)
            m = lax.stop_gradient(jnp.maximum(jnp.max(s, axis=1, keepdims=True), sink))
            pexp = jnp.exp(s - m)
            probs.append(pexp / (jnp.sum(pexp, axis=1, keepdims=True) + jnp.exp(sink - m)))
        p8 = jnp.concatenate(probs, axis=0)
        o = _row_blocks(_mm(jnp.where(own8, p8, 0.0), vc[j], NN) + _mm(jnp.where(own8, 0.0, p8), vp[j], NN))
        for t in range(4):
            outs.append(jnp.where(left, o[2 * t], o[2 * t + 1]))
    return outs


def _rmsnorm(x, g):
    r = lax.rsqrt(jnp.mean(x * x, axis=-1, keepdims=True) + RMS_EPS)
    return x * r * g


def rmsnorm_fwd(x, g_row, name):
    s, d = x.shape
    tm = min(512, s)

    def body(x_ref, g_ref, y_ref):
        y_ref[...] = _rmsnorm(x_ref[...], g_ref[...]).astype(bf16)

    return pl.pallas_call(
        body, name=name, grid=(s // tm,),
        in_specs=[pl.BlockSpec((tm, d), lambda i: (i, 0)), pl.BlockSpec((1, d), lambda i: (0, 0))],
        out_specs=pl.BlockSpec((tm, d), lambda i: (i, 0)),
        out_shape=jax.ShapeDtypeStruct((s, d), bf16),
        compiler_params=_cparams(("parallel",)),
    )(x, g_row)


def _fit(dim, want):
    if dim <= want:
        return dim
    t = want
    while dim % t:
        t -= 128
    return t


def matmul(a, b, *, dims, name, out_dtype=f32, tm=1024, tn=512, tk=8192, a_pro=None, epi=None, epi_args=(),
           out_by_col_tile=False, after=None):
    if dims == "nn" and b.ndim == 3:
        (m, k), n, tn = a.shape, b.shape[0] * b.shape[2], b.shape[2]
    elif dims == "nn":
        (m, k), n = a.shape, b.shape[1]
    elif dims == "nt":
        (m, k), n = a.shape, b.shape[0]
    else:
        (k, m), n = a.shape, b.shape[1]
    tm, tn, tk = _fit(m, tm), _fit(n, tn), _fit(k, tk)
    nk = k // tk
    if dims == "nn":
        a_spec = pl.BlockSpec((tm, tk), lambda i, j, kk: (i, kk))
        b_spec = (pl.BlockSpec((None, tk, tn), lambda i, j, kk: (j, kk, 0)) if b.ndim == 3
                  else pl.BlockSpec((tk, tn), lambda i, j, kk: (kk, j)))
        dn = NN
    elif dims == "nt":
        a_spec = pl.BlockSpec((tm, tk), lambda i, j, kk: (i, kk))
        b_spec = pl.BlockSpec((tn, tk), lambda i, j, kk: (j, kk))
        dn = NT
    else:
        a_spec = pl.BlockSpec((tk, tm), lambda i, j, kk: (kk, i))
        b_spec = pl.BlockSpec((tk, tn), lambda i, j, kk: (kk, j))
        dn = TN
    e_specs = [pl.BlockSpec((tm, tn), lambda i, j, kk: (i, j)) if kind == "tile"
               else pl.BlockSpec((1, tn), lambda i, j, kk: (0, j)) for kind, _ in epi_args]
    n_epi = len(epi_args)
    order_specs = [] if after is None else [pl.BlockSpec((8, 128), lambda i, j, kk: (0, 0))]
    order_args = [] if after is None else [after]

    def body(*refs):
        a_ref, b_ref = refs[0], refs[1]
        e_refs = refs[2:2 + n_epi]
        n_in = 2 + n_epi + len(order_args)
        o_ref = refs[n_in]
        av = a_ref[...]
        if a_pro is not None:
            av = a_pro(av)
        part = _mm(av, b_ref[...], dn)

        def finish(acc):
            if epi is not None:
                acc = epi(acc, *[r[...] for r in e_refs])
            o_ref[...] = acc.astype(out_dtype)

        if nk == 1:
            finish(part)
        else:
            acc_ref = refs[n_in + 1]
            kk = pl.program_id(2)

            @pl.when(kk == 0)
            def _():
                acc_ref[...] = part

            @pl.when(kk > 0)
            def _():
                acc_ref[...] += part

            @pl.when(kk == nk - 1)
            def _():
                finish(acc_ref[...])

    if out_by_col_tile:
        out_spec = pl.BlockSpec((None, tm, tn), lambda i, j, kk: (j, i, 0))
        out_shape = jax.ShapeDtypeStruct((n // tn, m, tn), out_dtype)
    else:
        out_spec = pl.BlockSpec((tm, tn), lambda i, j, kk: (i, j))
        out_shape = jax.ShapeDtypeStruct((m, n), out_dtype)
    return pl.pallas_call(
        body, name=name, grid=(m // tm, n // tn, nk),
        in_specs=[a_spec, b_spec] + e_specs + order_specs,
        out_specs=out_spec,
        out_shape=out_shape,
        scratch_shapes=[pltpu.VMEM((tm, tn), f32)] if nk > 1 else [],
        compiler_params=_cparams(("parallel", "parallel", "arbitrary")),
    )(a, b, *[arr for _, arr in epi_args], *order_args)


def _relu2(a):
    r = jnp.maximum(a.astype(f32), 0.0)
    return r * r


def _add(acc, t):
    return acc + t


def _add_bias(acc, t):
    return acc + t


def _add_bias_res(acc, bias, res):
    return acc + bias + res


def _times_relu2_grad(acc, a):
    return acc * (2.0 * jnp.maximum(a.astype(f32), 0.0))


def matmul_rows(a, b, *, dims, name, epi, epi_args, outs, tm=512, a_pro=None, after=None):
    m, k = a.shape
    n = b.shape[-1] if dims == "nn" else b.shape[-2]
    tm = _fit(m, tm)
    dn = NN if dims == "nn" else NT
    e_specs = [pl.BlockSpec((tm, arr.shape[1]), lambda i: (i, 0)) if kind == "tile"
               else pl.BlockSpec((1, arr.shape[1]), lambda i: (0, 0)) for kind, arr in epi_args]
    order_specs = [] if after is None else [pl.BlockSpec((8, 128), lambda i: (0, 0))]
    order_args = [] if after is None else [after]
    n_in = 2 + len(epi_args) + len(order_args)

    def body(*refs):
        av = refs[0][...]
        if a_pro is not None:
            av = a_pro(av)
        if b.ndim == 3:
            kb = b.shape[2]
            acc = sum(_mm(av[:, s * kb:(s + 1) * kb], refs[1][s], dn) for s in range(b.shape[0]))
        else:
            acc = _mm(av, refs[1][...], dn)
        vals = epi(acc, *[r[...] for r in refs[2:2 + len(epi_args)]])
        for (kind, _), o_ref, val in zip(outs, refs[n_in:], vals):
            if kind == "tile":
                o_ref[...] = val.astype(o_ref.dtype)
            else:
                @pl.when(pl.program_id(0) == 0)
                def _():
                    o_ref[...] = jnp.zeros_like(o_ref)

                o_ref[...] += val

    out_specs = [pl.BlockSpec((tm, n), lambda i: (i, 0)) if kind == "tile" else pl.BlockSpec((1, arg), lambda i: (0, 0))
                 for kind, arg in outs]
    out_shape = [jax.ShapeDtypeStruct((m, n), arg) if kind == "tile" else jax.ShapeDtypeStruct((1, arg), f32)
                 for kind, arg in outs]
    return pl.pallas_call(
        body, name=name, grid=(m // tm,),
        in_specs=[pl.BlockSpec((tm, k), lambda i: (i, 0)), pl.BlockSpec(b.shape, lambda i: (0,) * b.ndim)]
                 + e_specs + order_specs,
        out_specs=out_specs, out_shape=out_shape,
        compiler_params=_cparams(("arbitrary",)),
    )(a, b, *[arr for _, arr in epi_args], *order_args)


def _res_norm(acc, res, g):
    h = acc + res
    return h, _rmsnorm(h, g)


def _bias_res_norm(acc, bias, res, g):
    h = acc + bias + res
    return h, _rmsnorm(h, g)


def _res_norm_loss(acc, res, g, target):
    def f(h, gv):
        err = jnp.square(_rmsnorm(h, gv) - target)
        return 0.5 * jnp.sum(jnp.mean(err, axis=-1, keepdims=True), axis=0, keepdims=True)

    loss, vjp = jax.vjp(f, acc + res, g)
    dh, dg = vjp(jnp.ones_like(loss))
    return dh, dg, jnp.broadcast_to(loss, (1, 128))


def _norm_bwd_res_colsum(dy, x, g, res):
    dx, dg = _norm_bwd_res(dy, x, g, res)
    return dx, dg, jnp.sum(dx, axis=0, keepdims=True)


def _norm_bwd_res(dy, x, g, res):
    _, vjp = jax.vjp(_rmsnorm, x, g)
    dx, dg = vjp(dy)
    return res + dx, dg


_MIXER_PARAM_SHAPES = (
    ("ln_g", (1, D_MODEL)), ("ln_b", (1, D_MODEL)), ("wm", (N_BLK, CH, CH)), ("bs_t", (CH, CH)),
    ("conv_w", (8, 2048)), ("conv_b", (1, 2048)), ("dt_bias", (1, CH)), ("a_log", (1, CH)),
    ("d_heads", (1, CH)), ("norm_g", (1, D_MODEL)),
)


def _blocks(v, n, off=0):
    return [v[:, off + i * CH: off + (i + 1) * CH] for i in range(n)]


def _split_mixer_params(vals):
    p = dict(vals)
    return {
        "ln_g": _blocks(p["ln_g"], N_BLK), "ln_b": _blocks(p["ln_b"], N_BLK),
        "wm": [p["wm"][g] for g in range(N_BLK)], "bs_t": p["bs_t"],
        "conv_w": _blocks(p["conv_w"], XBC_BLKS), "conv_b": _blocks(p["conv_b"], XBC_BLKS),
        "dt_bias": p["dt_bias"], "a_log": p["a_log"], "d_heads": p["d_heads"],
        "norm_g": _blocks(p["norm_g"], N_BLK),
    }


def _mixer_leaves(proj_ref, halo_ref, keep_halo):
    pv = proj_ref
    us = [pv[:, OFF_U + i * CH: OFF_U + (i + 1) * CH] for i in range(N_BLK)]
    vs = [pv[:, OFF_V + i * CH: OFF_V + (i + 1) * CH] for i in range(N_BLK)]
    zs = [pv[:, OFF_Z + i * CH: OFF_Z + (i + 1) * CH] for i in range(N_BLK)]
    xbcs = [pv[:, OFF_X + i * CH: OFF_X + (i + 1) * CH] for i in range(XBC_BLKS)]
    halos = [halo_ref[:, OFF_X + i * CH: OFF_X + (i + 1) * CH] * keep_halo for i in range(XBC_BLKS)]
    dtblk = pv[:, OFF_DT: OFF_DT + CH]
    return us, vs, zs, xbcs, halos, dtblk


def mixer_fwd(proj, prm):
    s = proj.shape[0]
    nc = s // CH
    names = [n for n, _ in _MIXER_PARAM_SHAPES]

    def body(proj_ref, halo_ref, *rest):
        p_refs = rest[:len(names)]
        ab_ref, hs_ref, h_ref = rest[len(names):]
        c = pl.program_id(0)

        @pl.when(c == 0)
        def _():
            h_ref[...] = jnp.zeros_like(h_ref)

        hs_ref[...] = h_ref[...]
        keep = (c > 0).astype(f32)
        us, vs, zs, xbcs, halos, dtblk = _mixer_leaves(proj_ref, halo_ref, keep)
        hps = [h_ref[i * CH:(i + 1) * CH, :] for i in range(N_BLK)]
        p = _split_mixer_params({n: r[...] for n, r in zip(names, p_refs)})
        a_out, b_out, h_out = _mixer_chunk(us, vs, zs, xbcs, halos, dtblk, hps, p)
        for i in range(N_BLK):
            ab_ref[:, i * CH:(i + 1) * CH] = a_out[i].astype(bf16)
            ab_ref[:, D_MODEL + i * CH: D_MODEL + (i + 1) * CH] = b_out[i].astype(bf16)
            h_ref[i * CH:(i + 1) * CH, :] = h_out[i]

    def const(shape):
        return pl.BlockSpec(shape, lambda c: (0,) * len(shape))

    return pl.pallas_call(
        body, name="mixer_fwd", grid=(nc,),
        in_specs=[pl.BlockSpec((CH, NP_IN), lambda c: (c, 0)),
                  pl.BlockSpec((8, NP_IN), lambda c: (jnp.maximum(c * (CH // 8) - 1, 0), 0))]
                 + [const(shp) for _, shp in _MIXER_PARAM_SHAPES],
        out_specs=[pl.BlockSpec((CH, 2 * D_MODEL), lambda c: (c, 0)),
                   pl.BlockSpec((None, D_MODEL, CH), lambda c: (c, 0, 0))],
        out_shape=[jax.ShapeDtypeStruct((s, 2 * D_MODEL), bf16), jax.ShapeDtypeStruct((nc, D_MODEL, CH), f32)],
        scratch_shapes=[pltpu.VMEM((D_MODEL, CH), f32)],
        compiler_params=_cparams(("arbitrary",)),
    )(proj, proj, *[prm[n] for n in names])


def mixer_bwd(proj, hstates, dab, prm):
    s = proj.shape[0]
    nc = s // CH
    names = [n for n, _ in _MIXER_PARAM_SHAPES]
    npar = len(names)

    def body(proj_ref, halo_ref, hs_ref, dab_ref, *rest):
        p_refs = rest[:npar]
        dproj_ref = rest[npar]
        g_refs = rest[npar + 1: 2 * npar + 1]
        dh_ref, dhalo_ref = rest[2 * npar + 1:]
        i = pl.program_id(0)
        c = nc - 1 - i

        @pl.when(i == 0)
        def _():
            dh_ref[...] = jnp.zeros_like(dh_ref)
            dhalo_ref[...] = jnp.zeros_like(dhalo_ref)
            for r in g_refs:
                r[...] = jnp.zeros_like(r)

        keep = (c > 0).astype(f32)
        us, vs, zs, xbcs, halos, dtblk = _mixer_leaves(proj_ref, halo_ref, keep)
        hps = [hs_ref[j * CH:(j + 1) * CH, :] for j in range(N_BLK)]
        pvals = {n: r[...] for n, r in zip(names, p_refs)}

        def fn(us, vs, zs, xbcs, halos, dtblk, hps, pvals):
            return _mixer_chunk(us, vs, zs, xbcs, halos, dtblk, hps, _split_mixer_params(pvals))

        _, vjp = jax.vjp(fn, us, vs, zs, xbcs, halos, dtblk, hps, pvals)
        da = [dab_ref[:, j * CH:(j + 1) * CH].astype(f32) for j in range(N_BLK)]
        db = [dab_ref[:, D_MODEL + j * CH: D_MODEL + (j + 1) * CH].astype(f32) for j in range(N_BLK)]
        dh = [dh_ref[j * CH:(j + 1) * CH, :] for j in range(N_BLK)]
        dus, dvs, dzs, dxbcs, dhalos, ddt, dhps, dp = vjp((da, db, dh))

        for j in range(N_BLK):
            dproj_ref[:, OFF_U + j * CH: OFF_U + (j + 1) * CH] = dus[j].astype(bf16)
            dproj_ref[:, OFF_V + j * CH: OFF_V + (j + 1) * CH] = dvs[j].astype(bf16)
            dproj_ref[:, OFF_Z + j * CH: OFF_Z + (j + 1) * CH] = dzs[j].astype(bf16)
            dh_ref[j * CH:(j + 1) * CH, :] = dhps[j]
        zeros_top = jnp.zeros((CH - 8, CH), f32)
        for j in range(XBC_BLKS):
            late = jnp.concatenate([zeros_top, dhalo_ref[:, j * CH:(j + 1) * CH]], axis=0)
            dproj_ref[:, OFF_X + j * CH: OFF_X + (j + 1) * CH] = (dxbcs[j] + late).astype(bf16)
        for j in range(XBC_BLKS):
            dhalo_ref[:, j * CH:(j + 1) * CH] = dhalos[j] * keep
        lane = lax.broadcasted_iota(jnp.int32, (CH, CH), 1)
        dproj_ref[:, OFF_DT: OFF_DT + CH] = jnp.where(lane < SSM_HEADS, ddt, 0.0).astype(bf16)
        dproj_ref[:, OFF_DT + CH:] = jnp.zeros((CH, NP_IN - OFF_DT - CH), bf16)
        for n, r in zip(names, g_refs):
            r[...] += dp[n]

    def const(shape):
        return pl.BlockSpec(shape, lambda i: (0,) * len(shape))

    outs = pl.pallas_call(
        body, name="mixer_bwd", grid=(nc,),
        in_specs=[pl.BlockSpec((CH, NP_IN), lambda i: (nc - 1 - i, 0)),
                  pl.BlockSpec((8, NP_IN), lambda i: (jnp.maximum((nc - 1 - i) * (CH // 8) - 1, 0), 0)),
                  pl.BlockSpec((None, D_MODEL, CH), lambda i: (nc - 1 - i, 0, 0)),
                  pl.BlockSpec((CH, 2 * D_MODEL), lambda i: (nc - 1 - i, 0))]
                 + [const(shp) for _, shp in _MIXER_PARAM_SHAPES],
        out_specs=[pl.BlockSpec((CH, NP_IN), lambda i: (nc - 1 - i, 0))]
                  + [const(shp) for _, shp in _MIXER_PARAM_SHAPES],
        out_shape=[jax.ShapeDtypeStruct((s, NP_IN), bf16)]
                  + [jax.ShapeDtypeStruct(shp, f32) for _, shp in _MIXER_PARAM_SHAPES],
        scratch_shapes=[pltpu.VMEM((D_MODEL, CH), f32), pltpu.VMEM((8, 2048), f32)],
        compiler_params=_cparams(("arbitrary",)),
    )(proj, proj, hstates, dab, *[prm[n] for n in names])
    return outs[0], dict(zip(names, outs[1:]))


_K_BLK = D_MODEL // CH
_V_BLK = _K_BLK + 1


def _attn_specs(rev, nb):
    def blk(i):
        return nb - 1 - i if rev else i

    q_spec = pl.BlockSpec((CH, D_MODEL), lambda i: (blk(i), 0))
    kv = lambda col, prev: pl.BlockSpec(
        (CH, CH), lambda i: (jnp.maximum(blk(i) - 1, 0) if prev else blk(i), col))
    return q_spec, [kv(_K_BLK, True), kv(_K_BLK, False), kv(_V_BLK, True), kv(_V_BLK, False)]


def attn_fwd(qkv, sink_row):
    s = qkv.shape[0]
    nb = s // CH

    def body(q_ref, kp_ref, kc_ref, vp_ref, vc_ref, sink_ref, o_ref):
        qps = [q_ref[:, p * CH:(p + 1) * CH] for p in range(N_BLK)]
        outs = _attn_block(qps, kp_ref[...], kc_ref[...], vp_ref[...], vc_ref[...], sink_ref[...],
                           pl.program_id(0) == 0)
        for p in range(N_BLK):
            o_ref[:, p * CH:(p + 1) * CH] = outs[p].astype(bf16)

    q_spec, kv_specs = _attn_specs(False, nb)
    return pl.pallas_call(
        body, name="attn_fwd", grid=(nb,),
        in_specs=[q_spec] + kv_specs + [pl.BlockSpec((1, CH), lambda i: (0, 0))],
        out_specs=pl.BlockSpec((CH, D_MODEL), lambda i: (i, 0)),
        out_shape=jax.ShapeDtypeStruct((s, D_MODEL), bf16),
        compiler_params=_cparams(("parallel",)),
    )(qkv, qkv, qkv, qkv, qkv, sink_row)


def attn_bwd(qkv, sink_row, dout):
    s = qkv.shape[0]
    nb = s // CH

    def body(q_ref, kp_ref, kc_ref, vp_ref, vc_ref, sink_ref, do_ref, dqkv_ref, dsink_ref, db_ref, carry_ref):
        i = pl.program_id(0)
        blk = nb - 1 - i

        @pl.when(i == 0)
        def _():
            dsink_ref[...] = jnp.zeros_like(dsink_ref)
            db_ref[...] = jnp.zeros_like(db_ref)
            carry_ref[...] = jnp.zeros_like(carry_ref)

        qps = [q_ref[:, p * CH:(p + 1) * CH] for p in range(N_BLK)]
        first = blk == 0
        _, vjp = jax.vjp(lambda *a: _attn_block(*a, first), qps, kp_ref[...], kc_ref[...], vp_ref[...],
                         vc_ref[...], sink_ref[...])
        dos = [do_ref[:, p * CH:(p + 1) * CH].astype(f32) for p in range(N_BLK)]
        dqs, dkp, dkc, dvp, dvc, dsink = vjp(dos)
        blocks = list(dqs) + [dkc + carry_ref[0], dvc + carry_ref[1]]
        for p, val in enumerate(blocks):
            dqkv_ref[:, p * CH:(p + 1) * CH] = val.astype(bf16)
            db_ref[:, p * CH:(p + 1) * CH] += jnp.sum(val, axis=0, keepdims=True)
        keep = jnp.logical_not(first).astype(f32)
        carry_ref[0] = dkp * keep
        carry_ref[1] = dvp * keep
        dsink_ref[...] += dsink

    q_spec, kv_specs = _attn_specs(True, nb)
    return pl.pallas_call(
        body, name="attn_bwd", grid=(nb,),
        in_specs=[q_spec] + kv_specs + [pl.BlockSpec((1, CH), lambda i: (0, 0)),
                                        pl.BlockSpec((CH, D_MODEL), lambda i: (nb - 1 - i, 0))],
        out_specs=[pl.BlockSpec((CH, QKV_DIM), lambda i: (nb - 1 - i, 0)), pl.BlockSpec((1, CH), lambda i: (0, 0)),
                   pl.BlockSpec((1, QKV_DIM), lambda i: (0, 0))],
        out_shape=[jax.ShapeDtypeStruct((s, QKV_DIM), bf16), jax.ShapeDtypeStruct((1, CH), f32),
                   jax.ShapeDtypeStruct((1, QKV_DIM), f32)],
        scratch_shapes=[pltpu.VMEM((2, CH, CH), f32)],
        compiler_params=_cparams(("arbitrary",)),
    )(qkv, qkv, qkv, qkv, qkv, sink_row, dout)


def _adamw_update(w, g, m, v):
    nm = ADAM_B1 * m + (1.0 - ADAM_B1) * g
    nv = ADAM_B2 * v + (1.0 - ADAM_B2) * jnp.square(g)
    m_hat = nm / (1.0 - ADAM_B1 ** ADAM_STEP)
    v_hat = nv / (1.0 - ADAM_B2 ** ADAM_STEP)
    return -ADAM_LR * (m_hat / (jnp.sqrt(v_hat) + ADAM_EPS) + ADAM_WD * w), nm, nv


def adamw_rows(w, r, m, v, layer, row_off, name, into=None):
    rows, cols = w.shape[1], w.shape[2]
    tr = 256
    assert rows % tr == 0 and row_off % tr == 0

    def body(w_ref, r_ref, m_ref, v_ref, *rest):
        g_ref, d_ref, nm_ref, nv_ref = rest[-4:]
        g = r_ref[...]
        g_ref[...] = g
        d_ref[...], nm_ref[...], nv_ref[...] = _adamw_update(w_ref[...], g, m_ref[...], v_ref[...])

    tile = pl.BlockSpec((None, tr, cols), lambda i: (layer, i, 0))
    extra = [] if into is None else list(into)
    return pl.pallas_call(
        body, name=name, grid=(rows // tr,),
        in_specs=[tile, pl.BlockSpec((tr, cols), lambda i: (row_off // tr + i, 0)), tile, tile] + [_ANY] * len(extra),
        out_specs=[tile] * 4, out_shape=[jax.ShapeDtypeStruct(w.shape, f32)] * 4,
        input_output_aliases={4 + k: k for k in range(len(extra))},
        compiler_params=_cparams(("parallel",)),
    )(w, r, m, v, *extra)


def adamw(w, g, m, v, name):
    def body(w_ref, g_ref, m_ref, v_ref, d_ref, nm_ref, nv_ref):
        d_ref[...], nm_ref[...], nv_ref[...] = _adamw_update(w_ref[...], g_ref[...], m_ref[...], v_ref[...])

    out_shape = [jax.ShapeDtypeStruct(w.shape, f32)] * 3
    if w.ndim == 3 and w.shape[1] == 1:
        tr = max(t for t in range(1, 129) if w.shape[0] % t == 0)
        tile = pl.BlockSpec((tr, 1, w.shape[2]), lambda i: (i, 0, 0))
        return pl.pallas_call(
            body, name=name, grid=(w.shape[0] // tr,),
            in_specs=[tile] * 4, out_specs=[tile] * 3, out_shape=out_shape,
            compiler_params=_cparams(("parallel",)),
        )(w, g, m, v)
    if w.ndim == 3 and w.shape[1] % 256 == 0:
        tile = pl.BlockSpec((None, 256, w.shape[2]), lambda l, i: (l, i, 0))
        return pl.pallas_call(
            body, name=name, grid=(w.shape[0], w.shape[1] // 256),
            in_specs=[tile] * 4, out_specs=[tile] * 3, out_shape=out_shape,
            compiler_params=_cparams(("parallel", "parallel")),
        )(w, g, m, v)
    return pl.pallas_call(body, name=name, in_specs=[_VMEM] * 4, out_specs=[_VMEM] * 3, out_shape=out_shape,
                          compiler_params=_cparams())(w, g, m, v)


_MESH = pl.DeviceIdType.MESH
_ANY = pl.BlockSpec(memory_space=pl.ANY)
_VMEM = pl.BlockSpec(memory_space=pltpu.VMEM)


def _place():
    x, y, c = lax.axis_index("x"), lax.axis_index("y"), lax.axis_index("c")
    chips = [(1 - x, y), (x, 1 - y), (1 - x, 1 - y)]
    return x, y, c, 2 * x + y, chips, [2 * cx + cy for cx, cy in chips]


def _half(c, rows):
    return pl.ds(pl.multiple_of(c * (rows // 2), 16), rows // 2)


def _step_rows(rows):
    return max(t for t in range(16, 641, 16) if rows % t == 0)


def place_shard(b, slot, name, dtype=bf16, after=None, cols=None):
    r, c_in = b.shape
    c = c_in if cols is None else cols
    tr = _step_rows(r)

    def body(slot_ref, b_ref, *rest):
        o_ref = rest[-1]
        if c > c_in:
            whole = (c_in // 128) * 128
            o_ref[:, whole:] = jnp.zeros((tr, c - whole), dtype)
        o_ref[:, :c_in] = b_ref[...].astype(dtype)

    order_specs = [] if after is None else [pl.BlockSpec((8, 128), lambda i, s: (0, 0))]
    return pl.pallas_call(
        body, name=name,
        grid_spec=pltpu.PrefetchScalarGridSpec(
            num_scalar_prefetch=1, grid=(r // tr,),
            in_specs=[pl.BlockSpec((tr, c_in), lambda i, s: (i, 0))] + order_specs,
            out_specs=pl.BlockSpec((None, tr, c), lambda i, s: (s[0], i, 0))),
        out_shape=jax.ShapeDtypeStruct((N_CHIPS, r, c), dtype),
        compiler_params=_cparams(("parallel",)),
    )(slot, b, *([] if after is None else [after]))


_HBM = pl.BlockSpec(memory_space=pltpu.HBM)
_SEM = pl.BlockSpec(memory_space=pltpu.SEMAPHORE)
_EFFECT = pltpu.SideEffectType.DATAFLOW_SIDE_EFFECTING


def _gather_ici_copies(bufs, send_sems, recv_sems):
    x, y, c, me, chips, chip_idx = _place()
    return [pltpu.make_async_remote_copy(
        src_ref=buf.at[me, _half(c, buf.shape[1])], dst_ref=buf.at[chip_idx[j], _half(c, buf.shape[1])],
        send_sem=send_sems.at[3 * k + j], recv_sem=recv_sems.at[3 * k + j],
        device_id=(*chips[j], c), device_id_type=_MESH) for j in range(3) for k, buf in enumerate(bufs)]


def gather_start(groups, tag):
    sizes = [len(g) for g in groups]
    flat = [b for g in groups for b in g]
    n = len(flat)

    def body(*refs):
        bufs, sems = refs[:n], refs[n:n + 2 * len(groups)]
        refs[-1][...] = jnp.zeros_like(refs[-1])
        x, y, c, me, chips, chip_idx = _place()
        lo = 0
        for gi, size in enumerate(sizes):
            for j in range(3):
                for k, buf in enumerate(bufs[lo:lo + size]):
                    mine = buf.at[me, _half(c, buf.shape[1])]
                    pltpu.make_async_remote_copy(
                        src_ref=mine, dst_ref=mine, send_sem=sems[2 * gi].at[3 * k + j],
                        recv_sem=sems[2 * gi + 1].at[3 * k + j], device_id=(*chips[j], c),
                        device_id_type=_MESH).start()
            lo += size

    sem_shapes = [pltpu.SemaphoreType.DMA((3 * size,)) for size in sizes for _ in range(2)]
    outs = pl.pallas_call(
        body, name=f"gather_start_{tag}",
        out_shape=(*sem_shapes, *[pltpu.HBM(b.shape, b.dtype) for b in flat], jax.ShapeDtypeStruct((8, 128), f32)),
        in_specs=[_HBM] * n, out_specs=(*[_SEM] * len(sem_shapes), *[_HBM] * n, _VMEM),
        input_output_aliases={i: len(sem_shapes) + i for i in range(n)},
        compiler_params=pltpu.CompilerParams(has_side_effects=_EFFECT),
    )(*[pltpu.with_memory_space_constraint(b, pltpu.HBM) for b in flat])
    sems = [(outs[2 * gi], outs[2 * gi + 1]) for gi in range(len(groups))]
    thru, lo = [], len(sem_shapes)
    for size in sizes:
        thru.append(list(outs[lo:lo + size]))
        lo += size
    return sems, thru, outs[-1]


def gather_wait(bufs, sems, after, tag):
    n = len(bufs)

    def body(*refs):
        for cp in _gather_ici_copies(refs[:n], refs[n], refs[n + 1]):
            cp.wait_send()
            cp.wait_recv()

    extra = list(after)
    return list(pl.pallas_call(
        body, name=f"gather_wait_{tag}",
        out_shape=[pltpu.HBM(b.shape, b.dtype) for b in bufs],
        in_specs=[_HBM] * n + [_SEM, _SEM] + [_ANY] * len(extra), out_specs=[_HBM] * n,
        input_output_aliases={i: i for i in range(n)},
        compiler_params=pltpu.CompilerParams(has_side_effects=_EFFECT),
    )(*bufs, *sems, *extra))


def gather_forward(bufs, tag):
    n = len(bufs)

    def body(*refs):
        out_refs = refs[n:2 * n]
        send_sems, recv_sems = refs[2 * n:]
        x, y, c, me, chips, chip_idx = _place()

        def copy(k, j, half):
            part = out_refs[k].at[chip_idx[j], _half(half, out_refs[k].shape[1])]
            return pltpu.make_async_remote_copy(
                src_ref=part, dst_ref=part, send_sem=send_sems.at[3 * k + j], recv_sem=recv_sems.at[3 * k + j],
                device_id=(x, y, 1 - c), device_id_type=_MESH)

        sends = [copy(k, j, c) for j in range(3) for k in range(n)]
        for cp in sends:
            cp.start()
        for j in range(3):
            for k in range(n):
                copy(k, j, 1 - c).wait_recv()
        for cp in sends:
            cp.wait_send()

    return list(pl.pallas_call(
        body, name=f"gather_forward_{tag}",
        out_shape=[jax.ShapeDtypeStruct(b.shape, b.dtype) for b in bufs],
        in_specs=[_ANY] * n, out_specs=[_ANY] * n, input_output_aliases={i: i for i in range(n)},
        scratch_shapes=[pltpu.SemaphoreType.DMA((3 * n,)), pltpu.SemaphoreType.DMA((3 * n,))],
    )(*bufs))


def _forward_copy(ref, k, j, half, send_sems, recv_sems):
    x, y, c, me, chips, chip_idx = _place()
    part = ref.at[chip_idx[j], _half(half, ref.shape[1])]
    return pltpu.make_async_remote_copy(
        src_ref=part, dst_ref=part, send_sem=send_sems.at[3 * k + j], recv_sem=recv_sems.at[3 * k + j],
        device_id=(x, y, 1 - c), device_id_type=_MESH)


def forward_start(bufs, tag):
    n = len(bufs)

    def body(*refs):
        c = _place()[2]
        for j in range(3):
            for k in range(n):
                _forward_copy(refs[k], k, j, c, refs[n], refs[n + 1]).start()
        refs[-1][...] = jnp.zeros_like(refs[-1])

    outs = pl.pallas_call(
        body, name=f"forward_start_{tag}",
        out_shape=(pltpu.SemaphoreType.DMA((3 * n,)), pltpu.SemaphoreType.DMA((3 * n,)),
                   *[pltpu.HBM(b.shape, b.dtype) for b in bufs], jax.ShapeDtypeStruct((8, 128), f32)),
        in_specs=[_HBM] * n, out_specs=(_SEM, _SEM, *[_HBM] * n, _VMEM),
        input_output_aliases={i: 2 + i for i in range(n)},
        compiler_params=pltpu.CompilerParams(has_side_effects=_EFFECT),
    )(*[pltpu.with_memory_space_constraint(b, pltpu.HBM) for b in bufs])
    return (outs[0], outs[1], list(outs[2:2 + n])), outs[-1]


def forward_wait(send_sems, recv_sems, bufs, after, tag):
    n = len(bufs)

    def body(*refs):
        c = _place()[2]
        for j in range(3):
            for k in range(n):
                _forward_copy(refs[k], k, j, c, refs[n], refs[n + 1]).wait_send()
                _forward_copy(refs[k], k, j, 1 - c, refs[n], refs[n + 1]).wait_recv()

    return list(pl.pallas_call(
        body, name=f"forward_wait_{tag}",
        out_shape=[pltpu.HBM(b.shape, b.dtype) for b in bufs],
        in_specs=[_HBM] * n + [_SEM, _SEM, _ANY], out_specs=[_HBM] * n,
        input_output_aliases={i: i for i in range(n)},
        compiler_params=pltpu.CompilerParams(has_side_effects=_EFFECT),
    )(*bufs, send_sems, recv_sems, after))


def exchange_halves(bufs, tag):
    n = len(bufs)

    def body(*refs):
        g_refs, out_refs = refs[:n], refs[n:2 * n]
        send_sems, recv_sems = refs[2 * n:]
        x, y, c, *_ = _place()
        cps = [pltpu.make_async_remote_copy(
            src_ref=g_refs[b].at[:, _half(1 - c, g_refs[b].shape[1])], dst_ref=out_refs[b],
            send_sem=send_sems.at[b], recv_sem=recv_sems.at[b], device_id=(x, y, 1 - c), device_id_type=_MESH)
            for b in range(n)]
        for cp in cps:
            cp.start()
        for cp in cps:
            cp.wait()

    return pl.pallas_call(
        body, name=f"exchange_halves_{tag}",
        out_shape=[jax.ShapeDtypeStruct((N_CHIPS, b.shape[1] // 2, b.shape[2]), b.dtype) for b in bufs],
        in_specs=[_ANY] * n, out_specs=[_ANY] * n,
        scratch_shapes=[pltpu.SemaphoreType.DMA((n,)), pltpu.SemaphoreType.DMA((n,))],
    )(*bufs)


def add_halves(g, got, c_idx, name):
    hr, cols = got.shape[1], got.shape[2]
    tr = _step_rows(hr)
    steps = hr // tr

    def body(c_ref, g_ref, got_ref, o_ref):
        o_ref[...] = (g_ref[...].astype(f32) + got_ref[...].astype(f32)).astype(bf16)

    return pl.pallas_call(
        body, name=name,
        grid_spec=pltpu.PrefetchScalarGridSpec(
            num_scalar_prefetch=1, grid=(N_CHIPS, steps),
            in_specs=[pl.BlockSpec((None, tr, cols), lambda s, i, c: (s, c[0] * steps + i, 0)),
                      pl.BlockSpec((None, tr, cols), lambda s, i, c: (s, i, 0))],
            out_specs=pl.BlockSpec((None, tr, cols), lambda s, i, c: (s, i, 0))),
        out_shape=jax.ShapeDtypeStruct(got.shape, bf16),
        compiler_params=_cparams(("parallel", "parallel")),
    )(c_idx, g, got)


def sum_chips(t, got, place_idx, name):
    hr, cols = t.shape[1], t.shape[2]
    tr = _step_rows(hr)
    steps = hr // tr

    def body(idx_ref, t_ref, got_ref, o_ref):
        acc = t_ref[...].astype(f32)
        for j in range(3):
            acc = acc + got_ref[j].astype(f32)
        o_ref[...] = acc

    return pl.pallas_call(
        body, name=name,
        grid_spec=pltpu.PrefetchScalarGridSpec(
            num_scalar_prefetch=1, grid=(steps,),
            in_specs=[pl.BlockSpec((None, tr, cols), lambda i, idx: (idx[0], i, 0)),
                      pl.BlockSpec((3, tr, cols), lambda i, idx: (0, i, 0))],
            out_specs=pl.BlockSpec((tr, cols), lambda i, idx: (idx[1] * steps + i, 0))),
        out_shape=jax.ShapeDtypeStruct((2 * hr, cols), f32),
        compiler_params=_cparams(("parallel",)),
    )(place_idx, t, got)


def _share_copies(refs, send_sems, recv_sems):
    x, y, c, *_ = _place()
    return [pltpu.make_async_remote_copy(
        src_ref=ref.at[_half(c, ref.shape[0])], dst_ref=ref.at[_half(c, ref.shape[0])], send_sem=send_sems.at[b],
        recv_sem=recv_sems.at[b], device_id=(x, y, 1 - c), device_id_type=_MESH) for b, ref in enumerate(refs)]


def share_start(bufs, tag):
    n = len(bufs)

    def body(*refs):
        for cp in _share_copies(refs[:n], refs[n], refs[n + 1]):
            cp.start()
        token = refs[-1]
        token[...] = jnp.zeros_like(token)

    outs = pl.pallas_call(
        body, name=f"share_start_{tag}",
        out_shape=(pltpu.SemaphoreType.DMA((n,)), pltpu.SemaphoreType.DMA((n,)),
                   *[pltpu.HBM(b.shape, b.dtype) for b in bufs], jax.ShapeDtypeStruct((8, 128), f32)),
        in_specs=[_HBM] * n, out_specs=(_SEM, _SEM, *[_HBM] * n, _VMEM),
        input_output_aliases={i: 2 + i for i in range(n)},
        compiler_params=pltpu.CompilerParams(has_side_effects=_EFFECT),
    )(*[pltpu.with_memory_space_constraint(b, pltpu.HBM) for b in bufs])
    return (outs[0], outs[1], list(outs[2:2 + n])), outs[-1]


def share_wait(send_sems, recv_sems, bufs, after, tag):
    n = len(bufs)

    def body(*refs):
        x, y, c, *_ = _place()
        for b, ref in enumerate(refs[:n]):
            cp = pltpu.make_async_remote_copy(
                src_ref=ref.at[_half(c, ref.shape[0])], dst_ref=ref.at[_half(1 - c, ref.shape[0])],
                send_sem=refs[n].at[b], recv_sem=refs[n + 1].at[b], device_id=(x, y, 1 - c), device_id_type=_MESH)
            cp.wait_send()
            cp.wait_recv()

    return list(pl.pallas_call(
        body, name=f"share_wait_{tag}",
        out_shape=[pltpu.HBM(b.shape, b.dtype) for b in bufs],
        in_specs=[_HBM] * n + [_SEM, _SEM, _ANY], out_specs=[_HBM] * n,
        input_output_aliases={i: i for i in range(n)},
        compiler_params=pltpu.CompilerParams(has_side_effects=_EFFECT),
    )(*bufs, send_sems, recv_sems, after))


def _scatter_copies(t_refs, land_refs, send_sems, recv_sems):
    x, y, c, me, chips, chip_idx = _place()
    return [pltpu.make_async_remote_copy(
        src_ref=t_refs[b].at[chip_idx[j]], dst_ref=land_refs[b].at[j], send_sem=send_sems.at[3 * b + j],
        recv_sem=recv_sems.at[3 * b + j], device_id=(*chips[j], c), device_id_type=_MESH)
        for j in range(3) for b in range(len(t_refs))]


def scatter_start(ts, tag):
    n = len(ts)
    lands = [lax.empty((3,) + t.shape[1:], t.dtype) for t in ts]

    def body(*refs):
        for cp in _scatter_copies(refs[:n], refs[n:2 * n], refs[2 * n], refs[2 * n + 1]):
            cp.start()
        token = refs[-1]
        token[...] = jnp.zeros_like(token)

    hbm = [pltpu.HBM(a.shape, a.dtype) for a in (*ts, *lands)]
    outs = pl.pallas_call(
        body, name=f"scatter_start_{tag}",
        out_shape=(pltpu.SemaphoreType.DMA((3 * n,)), pltpu.SemaphoreType.DMA((3 * n,)), *hbm,
                   jax.ShapeDtypeStruct((8, 128), f32)),
        in_specs=[_HBM] * (2 * n), out_specs=(_SEM, _SEM, *[_HBM] * (2 * n), _VMEM),
        input_output_aliases={i: 2 + i for i in range(2 * n)},
        compiler_params=pltpu.CompilerParams(has_side_effects=_EFFECT),
    )(*[pltpu.with_memory_space_constraint(a, pltpu.HBM) for a in (*ts, *lands)])
    return outs[0], outs[1], list(outs[2:2 + n]), list(outs[2 + n:2 + 2 * n]), outs[-1]


def scatter_wait(send_sems, recv_sems, ts, lands, after, tag):
    n = len(ts)

    def body(*refs):
        for cp in _scatter_copies(refs[:n], refs[n:2 * n], refs[2 * n], refs[2 * n + 1]):
            cp.wait_send()
            cp.wait_recv()

    outs = pl.pallas_call(
        body, name=f"scatter_wait_{tag}",
        out_shape=[pltpu.HBM(a.shape, a.dtype) for a in (*ts, *lands)],
        in_specs=[_HBM] * (2 * n) + [_SEM, _SEM, _ANY], out_specs=[_HBM] * (2 * n),
        input_output_aliases={i: i for i in range(2 * n)},
        compiler_params=pltpu.CompilerParams(has_side_effects=_EFFECT),
    )(*ts, *lands, send_sems, recv_sems, after)
    return list(outs[:n]), list(outs[n:])


N_SENDERS = 7


def _direct_copies(g_refs, land_refs, send_sems, recv_sems):
    x, y, c, me, chips, chip_idx = _place()
    cps = []
    for b, (g, land) in enumerate(zip(g_refs, land_refs)):
        rows, base = g.shape[1], N_SENDERS * b
        cps.append(pltpu.make_async_remote_copy(
            src_ref=g.at[me, _half(1 - c, rows)], dst_ref=land.at[0], send_sem=send_sems.at[base],
            recv_sem=recv_sems.at[base], device_id=(x, y, 1 - c), device_id_type=_MESH))
        for j in range(3):
            for core in range(2):
                cps.append(pltpu.make_async_remote_copy(
                    src_ref=g.at[chip_idx[j], _half(core, rows)], dst_ref=land.at[1 + 2 * j + c],
                    send_sem=send_sems.at[base + 1 + 2 * j + core], recv_sem=recv_sems.at[base + 1 + 2 * j + c],
                    device_id=(*chips[j], core), device_id_type=_MESH))
    return cps


def direct_start(gs, tag):
    n = len(gs)
    lands = [lax.empty((N_SENDERS, g.shape[1] // 2, g.shape[2]), g.dtype) for g in gs]

    def body(*refs):
        for cp in _direct_copies(refs[:n], refs[n:2 * n], refs[2 * n], refs[2 * n + 1]):
            cp.start()
        token = refs[-1]
        token[...] = jnp.zeros_like(token)

    hbm = [pltpu.HBM(a.shape, a.dtype) for a in (*gs, *lands)]
    outs = pl.pallas_call(
        body, name=f"direct_start_{tag}",
        out_shape=(pltpu.SemaphoreType.DMA((N_SENDERS * n,)), pltpu.SemaphoreType.DMA((N_SENDERS * n,)), *hbm,
                   jax.ShapeDtypeStruct((8, 128), f32)),
        in_specs=[_HBM] * (2 * n), out_specs=(_SEM, _SEM, *[_HBM] * (2 * n), _VMEM),
        input_output_aliases={i: 2 + i for i in range(2 * n)},
        compiler_params=pltpu.CompilerParams(has_side_effects=_EFFECT),
    )(*[pltpu.with_memory_space_constraint(a, pltpu.HBM) for a in (*gs, *lands)])
    return outs[0], outs[1], list(outs[2:2 + n]), list(outs[2 + n:2 + 2 * n]), outs[-1]


def direct_wait(send_sems, recv_sems, gs, lands, after, tag):
    n = len(gs)

    def body(*refs):
        g_refs, land_refs, sends, recvs = refs[:n], refs[n:2 * n], refs[2 * n], refs[2 * n + 1]
        for b in range(n):
            for k in range(N_SENDERS):
                cp = pltpu.make_async_remote_copy(
                    src_ref=g_refs[b].at[0, _half(0, g_refs[b].shape[1])], dst_ref=land_refs[b].at[k],
                    send_sem=sends.at[N_SENDERS * b + k], recv_sem=recvs.at[N_SENDERS * b + k],
                    device_id=_place()[:3], device_id_type=_MESH)
                cp.wait_send()
                cp.wait_recv()

    outs = pl.pallas_call(
        body, name=f"direct_wait_{tag}",
        out_shape=[pltpu.HBM(a.shape, a.dtype) for a in (*gs, *lands)],
        in_specs=[_HBM] * (2 * n) + [_SEM, _SEM, _ANY], out_specs=[_HBM] * (2 * n),
        input_output_aliases={i: i for i in range(2 * n)},
        compiler_params=pltpu.CompilerParams(has_side_effects=_EFFECT),
    )(*gs, *lands, send_sems, recv_sems, after)
    return list(outs[:n]), list(outs[n:])


def sum_senders(g, lands, place_idx, name):
    hr, cols = lands.shape[1], lands.shape[2]
    tr = _step_rows(hr)
    steps = hr // tr

    def body(idx_ref, g_ref, land_ref, o_ref):
        acc = g_ref[...].astype(f32)
        for k in range(N_SENDERS):
            acc = acc + land_ref[k].astype(f32)
        o_ref[...] = acc

    return pl.pallas_call(
        body, name=name,
        grid_spec=pltpu.PrefetchScalarGridSpec(
            num_scalar_prefetch=1, grid=(steps,),
            in_specs=[pl.BlockSpec((None, tr, cols), lambda i, idx: (idx[0], idx[1] * steps + i, 0)),
                      pl.BlockSpec((N_SENDERS, tr, cols), lambda i, idx: (0, i, 0))],
            out_specs=pl.BlockSpec((tr, cols), lambda i, idx: (idx[1] * steps + i, 0))),
        out_shape=jax.ShapeDtypeStruct((2 * hr, cols), f32),
        compiler_params=_cparams(("parallel",)),
    )(place_idx, g, lands)


class GradReducer:
    def __init__(self, c_idx, place_idx):
        self.c_idx, self.place_idx = c_idx, place_idx

    def start(self, bufs, tag, direct=False):
        if direct:
            send_sems, recv_sems, gs, lands, token = direct_start(bufs, tag)
            return (True, send_sems, recv_sems, gs, lands), token
        got = exchange_halves(bufs, tag)
        ts = [add_halves(b, g, self.c_idx, f"add_halves_{tag}{i}") for i, (b, g) in enumerate(zip(bufs, got))]
        send_sems, recv_sems, ts, lands, token = scatter_start(ts, tag)
        return (False, send_sems, recv_sems, ts, lands), token

    def finish(self, state, after, tag):
        direct, *flight = state
        if direct:
            gs, lands = direct_wait(*flight, after, tag)
            sums = [sum_senders(g, l, self.place_idx, f"sum_senders_{tag}{i}") for i, (g, l) in enumerate(zip(gs, lands))]
        else:
            ts, lands = scatter_wait(*flight, after, tag)
            sums = [sum_chips(t, l, self.place_idx, f"sum_chips_{tag}{i}") for i, (t, l) in enumerate(zip(ts, lands))]
        return share_start(sums, tag)

    def collect(self, pending, after, tag):
        return share_wait(*pending, after, tag)


def allreduce_small(sp):
    rows = sp.shape[0]
    hr = rows // 2

    def body(s_ref, out_ref, sib_ref, chip_ref, four_ref, send_sems, recv_sems):
        x, y, c, me, chips, chip_idx = _place()
        sibling = (x, y, 1 - c)
        mine = pl.ds(pl.multiple_of(c * hr, 8), hr)
        other = pl.ds(pl.multiple_of((1 - c) * hr, 8), hr)

        swap = pltpu.make_async_remote_copy(src_ref=s_ref, dst_ref=sib_ref, send_sem=send_sems.at[0],
                                            recv_sem=recv_sems.at[0], device_id=sibling, device_id_type=_MESH)
        swap.start()
        swap.wait()
        is_core0 = c == 0
        chip_ref[...] = jnp.where(is_core0, s_ref[...], sib_ref[...]) + jnp.where(is_core0, sib_ref[...], s_ref[...])

        sends = [pltpu.make_async_remote_copy(
            src_ref=chip_ref.at[mine], dst_ref=four_ref.at[me], send_sem=send_sems.at[1 + j],
            recv_sem=recv_sems.at[1 + j], device_id=(*chips[j], c), device_id_type=_MESH) for j in range(3)]
        for cp in sends:
            cp.start()
        four_ref[me] = chip_ref[mine, :]
        for j in range(3):
            pltpu.make_async_remote_copy(
                src_ref=chip_ref.at[mine], dst_ref=four_ref.at[chip_idx[j]], send_sem=send_sems.at[1 + j],
                recv_sem=recv_sems.at[1 + j], device_id=(*chips[j], c), device_id_type=_MESH).wait_recv()
        for cp in sends:
            cp.wait_send()
        out_ref[mine, :] = (four_ref[0] + four_ref[1]) + (four_ref[2] + four_ref[3])

        share = pltpu.make_async_remote_copy(src_ref=out_ref.at[mine], dst_ref=out_ref.at[mine], send_sem=send_sems.at[4],
                                             recv_sem=recv_sems.at[4], device_id=sibling, device_id_type=_MESH)
        share.start()
        pltpu.make_async_remote_copy(src_ref=out_ref.at[mine], dst_ref=out_ref.at[other], send_sem=send_sems.at[4],
                                     recv_sem=recv_sems.at[4], device_id=sibling, device_id_type=_MESH).wait_recv()
        share.wait_send()

    return pl.pallas_call(
        body, name="allreduce_small",
        out_shape=jax.ShapeDtypeStruct(sp.shape, sp.dtype),
        in_specs=[_VMEM], out_specs=_VMEM,
        scratch_shapes=[pltpu.VMEM(sp.shape, sp.dtype), pltpu.VMEM(sp.shape, sp.dtype),
                        pltpu.VMEM((N_CHIPS, hr, sp.shape[1]), sp.dtype),
                        pltpu.SemaphoreType.DMA((5,)), pltpu.SemaphoreType.DMA((5,))],
        compiler_params=_cparams(),
    )(sp)


def _n_rows(shape):
    n = 1
    for d in shape:
        n *= d
    return 8 * (-(-n // 8192))


def _pack(arrays, total_rows):
    parts = []
    for a in arrays:
        flat = a.reshape(-1)
        parts.append(jnp.pad(flat, (0, 1024 * _n_rows(a.shape) - flat.shape[0])).reshape(-1, 1024))
    rows = jnp.concatenate(parts, axis=0)
    return jnp.pad(rows, ((0, total_rows - rows.shape[0]), (0, 0)))


def _unpack(packed, shapes):
    out, r = [], 0
    for shp in shapes:
        n = 1
        for d in shp:
            n *= d
        nr = _n_rows(shp)
        out.append(packed[r:r + nr].reshape(-1)[:n].reshape(shp))
        r += nr
    return out


_COLUMN_SHARDED = ("w_in_even", "w_qkv")
IN_SHARD, IN_PAD = 1284, 1408
QKV_SHARD, QKV_PAD = 320, 384


def _lane_padded(a, cols):
    return jnp.pad(a, ((0, 0), (0, cols - a.shape[1])))


_SMALL_SHAPES = (
    ("norm_mix_g", (2, 1024)), ("norm_mlp_g", (2, 1024)), ("final_norm_g", (1024,)), ("gm_ln_g", (1, 1024)),
    ("gm_ln_b", (1, 1024)), ("gm_w_s", (1, 8, 128, 128)), ("gm_b_s", (1, 8, 128)), ("ssm_conv_b", (1, 2048)),
    ("ssm_dt_bias", (1, 16)), ("ssm_a_log", (1, 16)), ("ssm_d", (1, 16)), ("ssm_norm_g", (1, 1024)),
    ("attn_sinks", (1, 16)), ("ssm_conv_w", (1, 4, 2048)), ("b_qkv", (1, 1280)), ("b_o", (1, 1024)),
)
_N_REPLICATED = 13
_SHARDED_SMALL = (("ssm_conv_w", 2, 512), ("b_qkv", 1, 320), ("b_o", 1, 256))
_SHARD_PACK_ROWS = 32


def _cols_by_owner(a):
    return a.transpose(1, 0, 2).reshape(a.shape[1], -1)


class WeightGatherer:
    def __init__(self, w, chip_idx):
        def place(tag, b, dtype=bf16, after=None, cols=None):
            return place_shard(b, chip_idx, f"place_shard_{tag}", dtype, after, cols)

        sems_in, bufs_in, self.started = gather_start([
            [place("in", w["w_in_even"][0].astype(bf16), cols=IN_PAD),
             place("small", _pack([w[n] for n, _, _ in _SHARDED_SMALL], _SHARD_PACK_ROWS), f32)]], "in")
        t = self.started
        sems, bufs, self.all_started = gather_start([
            [place("out", w["w_out_even"][0], after=t), place("up0", w["w_up"][0], after=t),
             place("down0", w["w_down"][0], after=t)],
            [place("qkv", w["w_qkv"][0], after=t, cols=QKV_PAD), place("o", w["w_o"][0], after=t),
             place("up1", w["w_up"][1], after=t), place("down1", w["w_down"][1], after=t)],
        ], "rest")
        self.sems, self.bufs = sems_in + sems, bufs_in + bufs

    def _group(self, gi, after, tag):
        return gather_forward(gather_wait(self.bufs[gi], self.sems[gi], after, tag), tag)

    def mixer_in(self, after):
        g, small = self._group(0, [after, self.all_started], "in")
        shard_shapes = [tuple(width if i == axis else d for i, d in enumerate(dict(_SMALL_SHAPES)[n]))
                        for n, axis, width in _SHARDED_SMALL]
        per_chip = [_unpack(small[s], shard_shapes) for s in range(N_CHIPS)]
        full = {n: jnp.concatenate([per_chip[s][i] for s in range(N_CHIPS)], axis=axis)
                for i, (n, axis, _) in enumerate(_SHARDED_SMALL)}
        w_in_p = jnp.concatenate([g[s, :, :IN_SHARD] for s in range(N_CHIPS)]
                                 + [jnp.zeros((g.shape[1], NP_IN - IN_EVEN), g.dtype)], axis=1)
        return w_in_p, full

    def layer0(self, after):
        w_out, w_up, w_down = self._group(1, [after], "l0")
        return w_out.reshape(2048, 1024), w_up, w_down.reshape(4096, 1024)

    def layer1_start(self, after):
        return forward_start(gather_wait(self.bufs[2], self.sems[2], [after], "l1"), "l1")

    def layer1(self, pending, after):
        q, w_o, w_up, w_down = forward_wait(*pending, after, "l1")
        w_qkv = jnp.concatenate([q[s, :, :QKV_SHARD] for s in range(N_CHIPS)], axis=1)
        return w_qkv, w_o.reshape(1024, 1024), w_up, w_down.reshape(4096, 1024)


def _row2(v):
    return v.reshape(1, -1)


def _lane_pad(v):
    return jnp.pad(v, ((0, 0), (0, CH - v.shape[1])))


_H_AND_NORM = (("tile", f32), ("tile", bf16))
_DX_AND_DG = (("tile", f32), ("sum", D_MODEL))


def _mlp_bwd(dh_out, h, g_row, y, a, w_up, w_down, tag, after=None):
    da = matmul(dh_out, w_down, dims="nt", name=f"mlp_da{tag}", out_dtype=bf16, tm=2048, tn=1024,
                epi=_times_relu2_grad, epi_args=(("tile", a),), after=after)
    dw_down = matmul(a, dh_out, dims="tn", name=f"mlp_dwdown{tag}", out_dtype=bf16, a_pro=_relu2)
    dw_up = matmul(y, da, dims="tn", name=f"mlp_dwup{tag}", out_dtype=bf16, tn=1024, out_by_col_tile=True)
    dh, dg, dh_colsum = matmul_rows(da, w_up, dims="nt", name=f"mlp_dy{tag}", epi=_norm_bwd_res_colsum,
                                    epi_args=(("tile", h), ("row", g_row), ("tile", dh_out)),
                                    outs=_DX_AND_DG + (("sum", D_MODEL),))
    return dh, dg, dw_up, dw_down, dh_colsum


def _by_owner(a):
    return a.reshape(N_CHIPS, a.shape[0] // N_CHIPS, a.shape[1])


def _row_shards(a, shard, padded):
    return jnp.stack([jnp.pad(a[shard * s: shard * (s + 1)], ((0, padded - shard), (0, 0))) for s in range(N_CHIPS)])


def _col_shards(a, shard, padded):
    return jnp.stack([_lane_padded(a[:, shard * s: shard * (s + 1)], padded) for s in range(N_CHIPS)])


def _local_step(x, target, weights, sm, reducer):
    w_up, w_down = [None, None], [None, None]
    mix_g = [_row2(sm["norm_mix_g"][i]) for i in range(2)]
    y0 = rmsnorm_fwd(x, mix_g[0] + weights.started[:1, :1], "mix_norm0")
    w_in_p, sharded_small = weights.mixer_in(y0)
    sm = {**sm, **sharded_small}
    mlp_g = [_row2(sm["norm_mlp_g"][i]) for i in range(2)]
    mixer_prm = {
        "ln_g": sm["gm_ln_g"], "ln_b": sm["gm_ln_b"], "wm": sm["gm_w_s"][0],
        "bs_t": jnp.pad(sm["gm_b_s"][0].T, ((0, 0), (0, CH - N_BLK))),
        "conv_w": jnp.pad(sm["ssm_conv_w"][0], ((0, 4), (0, 0))), "conv_b": sm["ssm_conv_b"],
        "dt_bias": _lane_pad(sm["ssm_dt_bias"]), "a_log": _lane_pad(sm["ssm_a_log"]),
        "d_heads": _lane_pad(sm["ssm_d"]), "norm_g": sm["ssm_norm_g"],
    }
    sink_row = _lane_pad(sm["attn_sinks"])

    proj = matmul(y0, w_in_p, dims="nn", name="in_proj", tm=2048, tn=768)
    ab, hstates = mixer_fwd(proj, mixer_prm)
    w_out, w_up[0], w_down[0] = weights.layer0(ab)
    h1, y1 = matmul_rows(ab, w_out, dims="nn", name="out_proj", epi=_res_norm,
                         epi_args=(("tile", x), ("row", mlp_g[0])), outs=_H_AND_NORM)
    a1 = matmul(y1, w_up[0], dims="nn", name="mlp_up0", out_dtype=bf16, tm=2048, tn=1024)
    pending_l1, forwarding_l1 = weights.layer1_start(a1)
    h2, y2 = matmul_rows(a1, w_down[0], dims="nn", name="mlp_down0", a_pro=_relu2, epi=_res_norm,
                         epi_args=(("tile", h1), ("row", mix_g[1])), outs=_H_AND_NORM, after=forwarding_l1)
    w_qkv, w_o, w_up[1], w_down[1] = weights.layer1(pending_l1, h2)
    qkv = matmul(y2, w_qkv, dims="nn", name="qkv_proj", tn=QKV_DIM, epi=_add_bias, epi_args=(("row", sm["b_qkv"]),))
    att = attn_fwd(qkv, sink_row)
    h3, y3 = matmul_rows(att, w_o, dims="nn", name="o_proj", epi=_bias_res_norm,
                         epi_args=(("row", sm["b_o"]), ("tile", h2), ("row", mlp_g[1])), outs=_H_AND_NORM)
    a3 = matmul(y3, w_up[1], dims="nn", name="mlp_up1", out_dtype=bf16, tm=2048, tn=1024)
    dh4, dg_final, loss = matmul_rows(
        a3, w_down[1], dims="nn", name="mlp_down1", a_pro=_relu2, epi=_res_norm_loss,
        epi_args=(("tile", h3), ("row", _row2(sm["final_norm_g"])), ("tile", target)),
        outs=(("tile", f32), ("sum", D_MODEL), ("sum", 128)))

    dh3, dg_mlp1, dw_up1, dw_down1, db_o = _mlp_bwd(dh4, h3, mlp_g[1], y3, a3, w_up[1], w_down[1], 1)
    datt = matmul(dh3, w_o, dims="nt", name="attn_dout", out_dtype=bf16, tm=2048)
    dw_o = matmul(att, dh3, dims="tn", name="dw_o", out_dtype=bf16)
    dqkv, dsink, db_qkv = attn_bwd(qkv, sink_row, datt)
    dw_qkv = matmul(y2, dqkv, dims="tn", name="dw_qkv", out_dtype=bf16, tn=QKV_DIM)
    dh2, dg_mix1 = matmul_rows(dqkv, w_qkv, dims="nt", name="dy_qkv", epi=_norm_bwd_res,
                               epi_args=(("tile", h2), ("row", mix_g[1]), ("tile", dh3)), outs=_DX_AND_DG)
    layer1 = [jnp.concatenate([_by_owner(dw_o), dw_up1, _by_owner(dw_down1)], axis=1),
              _col_shards(dw_qkv, QKV_SHARD, QKV_PAD)]
    flight1, token1 = reducer.start(layer1, "l1", direct=True)
    dh1, dg_mlp0, dw_up0, dw_down0, _ = _mlp_bwd(dh2, h1, mlp_g[0], y1, a1, w_up[0], w_down[0], 0, after=token1)
    pending1, shared1 = reducer.finish(flight1, dh1, "l1")
    dw_out = matmul(ab, dh1, dims="tn", name="dw_out", out_dtype=bf16, after=shared1)
    flight0, token0 = reducer.start(
        [jnp.concatenate([dw_up0, _by_owner(dw_down0), _by_owner(dw_out)], axis=1)], "l0", direct=True)
    dab = matmul(dh1, w_out, dims="nt", name="mixer_dout", tm=2048, tn=1024, after=token0)
    dproj, dmix = mixer_bwd(proj, hstates, dab, mixer_prm)
    dw_in_t = matmul(dproj, y0, dims="tn", name="dw_in", out_dtype=bf16, tm=768, tn=1024)
    pending0, shared0 = reducer.finish(flight0, dw_in_t, "l0")
    flight_in, token_in = reducer.start([_row_shards(dw_in_t, IN_SHARD, IN_PAD)], "in")
    dx, dg_mix0 = matmul_rows(dproj, w_in_p, dims="nt", name="dy_in", tm=256, epi=_norm_bwd_res,
                              epi_args=(("tile", x), ("row", mix_g[0]), ("tile", dh1)), outs=_DX_AND_DG,
                              after=token_in + shared0)
    pending_in, _ = reducer.finish(flight_in, dx, "in")
    r_l1, r_qkv = reducer.collect(pending1, dx, "l1")
    (r_l0,) = reducer.collect(pending0, dx, "l0")
    (r_in,) = reducer.collect(pending_in, dx, "in")
    reduced = {
        "w_out_even": [(r_l0, 2048)], "w_o": [(r_l1, 0)], "w_up": [(r_l0, 0), (r_l1, 256)],
        "w_down": [(r_l0, 1024), (r_l1, 1280)],
        "w_in_even": r_in[:IN_SHARD].T[None], "w_qkv": r_qkv[None, :, :QKV_SHARD],
    }

    small_grads = {
        "norm_mix_g": jnp.concatenate([dg_mix0, dg_mix1], axis=0),
        "norm_mlp_g": jnp.concatenate([dg_mlp0, dg_mlp1], axis=0),
        "final_norm_g": dg_final[0], "gm_ln_g": dmix["ln_g"], "gm_ln_b": dmix["ln_b"],
        "gm_w_s": dmix["wm"][None], "gm_b_s": dmix["bs_t"][:, :N_BLK].T[None],
        "ssm_conv_b": dmix["conv_b"], "ssm_dt_bias": dmix["dt_bias"][:, :SSM_HEADS],
        "ssm_a_log": dmix["a_log"][:, :SSM_HEADS], "ssm_d": dmix["d_heads"][:, :SSM_HEADS],
        "ssm_norm_g": dmix["norm_g"], "attn_sinks": dsink[:, :SSM_HEADS],
        "ssm_conv_w": dmix["conv_w"][None, :4], "b_qkv": db_qkv, "b_o": db_o,
    }
    return loss, dx, reduced, small_grads


def kernel(x, norm_mix_g, norm_mlp_g, final_norm_g, w_in_even, w_out_even, gm_ln_g, gm_ln_b, gm_w_s, gm_b_s, ssm_conv_w, ssm_conv_b, ssm_dt_bias, ssm_a_log, ssm_d, ssm_norm_g, w_qkv, b_qkv, w_o, b_o, attn_sinks, w_up, w_down, loss_target, m_norm_mix_g, m_norm_mlp_g, m_final_norm_g, m_w_in_even, m_w_out_even, m_gm_ln_g, m_gm_ln_b, m_gm_w_s, m_gm_b_s, m_ssm_conv_w, m_ssm_conv_b, m_ssm_dt_bias, m_ssm_a_log, m_ssm_d, m_ssm_norm_g, m_w_qkv, m_b_qkv, m_w_o, m_b_o, m_attn_sinks, m_w_up, m_w_down, v_norm_mix_g, v_norm_mlp_g, v_final_norm_g, v_w_in_even, v_w_out_even, v_gm_ln_g, v_gm_ln_b, v_gm_w_s, v_gm_b_s, v_ssm_conv_w, v_ssm_conv_b, v_ssm_dt_bias, v_ssm_a_log, v_ssm_d, v_ssm_norm_g, v_w_qkv, v_b_qkv, v_w_o, v_b_o, v_attn_sinks, v_w_up, v_w_down):
    w = dict(norm_mix_g=norm_mix_g, norm_mlp_g=norm_mlp_g, final_norm_g=final_norm_g, w_in_even=w_in_even,
             w_out_even=w_out_even, gm_ln_g=gm_ln_g, gm_ln_b=gm_ln_b, gm_w_s=gm_w_s, gm_b_s=gm_b_s,
             ssm_conv_w=ssm_conv_w, ssm_conv_b=ssm_conv_b, ssm_dt_bias=ssm_dt_bias, ssm_a_log=ssm_a_log,
             ssm_d=ssm_d, ssm_norm_g=ssm_norm_g, w_qkv=w_qkv, b_qkv=b_qkv, w_o=w_o, b_o=b_o,
             attn_sinks=attn_sinks, w_up=w_up, w_down=w_down)
    m = dict(norm_mix_g=m_norm_mix_g, norm_mlp_g=m_norm_mlp_g, final_norm_g=m_final_norm_g,
             w_in_even=m_w_in_even, w_out_even=m_w_out_even, gm_ln_g=m_gm_ln_g, gm_ln_b=m_gm_ln_b,
             gm_w_s=m_gm_w_s, gm_b_s=m_gm_b_s, ssm_conv_w=m_ssm_conv_w, ssm_conv_b=m_ssm_conv_b,
             ssm_dt_bias=m_ssm_dt_bias, ssm_a_log=m_ssm_a_log, ssm_d=m_ssm_d, ssm_norm_g=m_ssm_norm_g,
             w_qkv=m_w_qkv, b_qkv=m_b_qkv, w_o=m_w_o, b_o=m_b_o, attn_sinks=m_attn_sinks, w_up=m_w_up,
             w_down=m_w_down)
    v = dict(norm_mix_g=v_norm_mix_g, norm_mlp_g=v_norm_mlp_g, final_norm_g=v_final_norm_g,
             w_in_even=v_w_in_even, w_out_even=v_w_out_even, gm_ln_g=v_gm_ln_g, gm_ln_b=v_gm_ln_b,
             gm_w_s=v_gm_w_s, gm_b_s=v_gm_b_s, ssm_conv_w=v_ssm_conv_w, ssm_conv_b=v_ssm_conv_b,
             ssm_dt_bias=v_ssm_dt_bias, ssm_a_log=v_ssm_a_log, ssm_d=v_ssm_d, ssm_norm_g=v_ssm_norm_g,
             w_qkv=v_w_qkv, b_qkv=v_b_qkv, w_o=v_w_o, b_o=v_b_o, attn_sinks=v_attn_sinks, w_up=v_w_up,
             w_down=v_w_down)
    names = ("norm_mix_g", "norm_mlp_g", "final_norm_g", "w_in_even", "w_out_even", "gm_ln_g", "gm_ln_b",
             "gm_w_s", "gm_b_s", "ssm_conv_w", "ssm_conv_b", "ssm_dt_bias", "ssm_a_log", "ssm_d", "ssm_norm_g",
             "w_qkv", "b_qkv", "w_o", "b_o", "attn_sinks", "w_up", "w_down")

    cx, cy, cc = lax.axis_index("x"), lax.axis_index("y"), lax.axis_index("c")
    chip = 2 * cx + cy
    c_idx = jnp.reshape(cc, (1,)).astype(jnp.int32)
    chip_idx = jnp.reshape(chip, (1,)).astype(jnp.int32)

    weights = WeightGatherer(w, chip_idx)
    sm = {n: w[n] for n, _ in _SMALL_SHAPES[:_N_REPLICATED]}

    reducer = GradReducer(c_idx, jnp.concatenate([chip_idx, c_idx]))
    loss_part, dx, grads, small_grads = _local_step(x[0], loss_target[0], weights, sm, reducer)

    small_sum = allreduce_small(_pack([small_grads[n] for n, _ in _SMALL_SHAPES] + [loss_part], SMALL_ROWS))
    *small_list, loss_row = _unpack(small_sum, [s for _, s in _SMALL_SHAPES] + [loss_part.shape])
    loss = loss_row[0, 0]
    small_full = dict(zip([n for n, _ in _SMALL_SHAPES], small_list))
    for n, _ in _SMALL_SHAPES[:_N_REPLICATED]:
        grads[n] = small_full[n]
    for n, axis, width in _SHARDED_SMALL:
        grads[n] = lax.dynamic_slice_in_dim(small_full[n], chip * width, width, axis)

    delta, new_m, new_v = {}, {}, {}
    for n in names:
        if isinstance(grads[n], list):
            outs = None
            for layer, (buf, row_off) in enumerate(grads[n]):
                outs = adamw_rows(w[n], buf, m[n], v[n], layer, row_off, f"adamw_{n}{layer}", into=outs)
            grads[n], delta[n], new_m[n], new_v[n] = outs
            continue
        grads[n] = grads[n].reshape(w[n].shape)
        if n in _COLUMN_SHARDED:
            args = [jnp.transpose(d[n], (2, 0, 1)) for d in (w, grads, m, v)]
            grads[n] = jnp.transpose(args[1], (1, 2, 0))
            outs = adamw(*args, f"adamw_{n}")
            delta[n], new_m[n], new_v[n] = (jnp.transpose(o, (1, 2, 0)) for o in outs)
            continue
        shape = (1,) + w[n].shape if w[n].ndim == 1 else w[n].shape
        outs = adamw(*[d[n].reshape(shape) for d in (w, grads, m, v)], f"adamw_{n}")
        delta[n], new_m[n], new_v[n] = (o.reshape(w[n].shape) for o in outs)

    return (loss, dx[None], *[grads[n] for n in names], *[delta[n] for n in names],
            *[new_m[n] for n in names], *[new_v[n] for n in names])
```

```python
import functools

import jax
import jax.numpy as jnp
from jax import lax
from jax.experimental import pallas as pl
from jax.experimental.pallas import tpu as pltpu

f32 = jnp.float32
bf16 = jnp.bfloat16
MXU_DTYPE = bf16

RMS_EPS = 1e-5
LN_EPS = 1e-5
D_MODEL = 1024
D_FF = 4096
CH = 128
N_BLK = 8
SSM_HEADS = 16
IN_EVEN = 5136
NP_IN = 5376
OFF_U, OFF_V, OFF_Z, OFF_X, OFF_DT = 0, 1024, 2048, 3072, 5120
XBC_BLKS = 16
QKV_DIM = 1280
ATT_SCALE = 64 ** -0.5

ADAM_LR = 0.001
ADAM_B1 = 0.9
ADAM_B2 = 0.999
ADAM_EPS = 1e-08
ADAM_WD = 0.01
ADAM_STEP = 10

VMEM_LIMIT_BYTES = 48 * 1024 * 1024
N_CHIPS = 4
SMALL_ROWS = 256

NN = ((1,), (0,))
NT = ((1,), (1,))
TN = ((0,), (0,))


def _mm(a, b, dims):
    return lax.dot_general(a.astype(MXU_DTYPE), b.astype(MXU_DTYPE), (dims, ((), ())),
                           preferred_element_type=f32)


def _mm_exact(a, b):
    return jnp.dot(a, b, preferred_element_type=f32, precision=lax.Precision.HIGHEST)


def _cparams(sem=None):
    return pltpu.CompilerParams(dimension_semantics=sem, vmem_limit_bytes=VMEM_LIMIT_BYTES)


@jax.custom_vjp
def _swap64(x):
    return pltpu.roll(x, 64, axis=1)


_swap64.defvjp(lambda x: (pltpu.roll(x, 64, axis=1), None), lambda _, g: (pltpu.roll(g, 64, axis=1),))


def _row_blocks_of(x):
    return tuple(x[i:i + CH] for i in range(0, x.shape[0], CH))


@jax.custom_vjp
def _row_blocks(x):
    return _row_blocks_of(x)


_row_blocks.defvjp(lambda x: (_row_blocks_of(x), None), lambda _, gs: (jnp.concatenate(gs, axis=0),))


def _make_delay(k):
    @jax.custom_vjp
    def delay(ext):
        return pltpu.roll(ext, k, axis=0)[8:, :]

    def fwd(ext):
        return delay(ext), None

    def bwd(_, g):
        gp = jnp.concatenate([jnp.zeros((8, g.shape[1]), g.dtype), g], axis=0)
        return (pltpu.roll(gp, gp.shape[0] - k, axis=0),)

    delay.defvjp(fwd, bwd)
    return delay


_DELAYS = {k: _make_delay(k) for k in (1, 2, 3)}


_GELU_C = 0.7978845608028654
_GELU_K = 0.044715


@jax.custom_vjp
def _gelu(x):
    return 0.5 * x * (1.0 + jnp.tanh(_GELU_C * (x + _GELU_K * (x * x * x))))


def _gelu_fwd(x):
    t = jnp.tanh(_GELU_C * (x + _GELU_K * (x * x * x)))
    return 0.5 * x * (1.0 + t), (x, t)


def _gelu_bwd(res, g):
    x, t = res
    dz = _GELU_C + (3.0 * _GELU_C * _GELU_K) * (x * x)
    return (g * (0.5 * (1.0 + t) + (0.5 * x) * (1.0 - t * t) * dz),)


_gelu.defvjp(_gelu_fwd, _gelu_bwd)


def _col(m, lane, h):
    return jnp.sum(jnp.where(lane == h, m, 0.0), axis=1, keepdims=True)


@functools.lru_cache(maxsize=None)
def _row_picker(h, shape):
    @jax.custom_vjp
    def pick(m):
        return m[h:h + 1, :]

    def bwd(_, g):
        return (jnp.where(lax.broadcasted_iota(jnp.int32, shape, 0) == h, g, 0.0),)

    pick.defvjp(lambda m: (m[h:h + 1, :], None), bwd)
    return pick


def _row(m, sub, h):
    return _row_picker(h, m.shape)(m)


def _mixer_chunk(us, vs, zs, xbcs, halos, dtblk, hps, prm):
    lane = lax.broadcasted_iota(jnp.int32, (CH, CH), 1)
    sub = lax.broadcasted_iota(jnp.int32, (CH, CH), 0)
    left = lane < 64
    top = sub < 64
    causal = sub >= lane

    gus = [_gelu(u) for u in us]
    gvs = [_gelu(v) for v in vs]
    mu = sum(jnp.sum(g, axis=1, keepdims=True) for g in gvs) / D_MODEL
    cen = [g - mu for g in gvs]
    var = sum(jnp.sum(c * c, axis=1, keepdims=True) for c in cen) / D_MODEL
    rstd = lax.rsqrt(var + LN_EPS)
    a_out = []
    for g in range(N_BLK):
        vn = cen[g] * rstd * prm["ln_g"][g] + prm["ln_b"][g]
        w = jnp.where(causal, prm["wm"][g], 0.0)
        mixed = _mm(w, vn, NN) + _col(prm["bs_t"], lane, g)
        a_out.append(gus[g] * mixed)

    act = []
    for b in range(XBC_BLKS):
        w8 = prm["conv_w"][b]
        sub8 = lax.broadcasted_iota(jnp.int32, w8.shape, 0)
        ext = jnp.concatenate([halos[b], xbcs[b]], axis=0)
        conv = xbcs[b] * _row(w8, sub8, 3) + prm["conv_b"][b]
        for k in (1, 2, 3):
            conv = conv + _DELAYS[k](ext) * _row(w8, sub8, 3 - k)
        act.append(jax.nn.silu(conv))

    dt = jax.nn.softplus(dtblk + prm["dt_bias"])
    a_neg = -jnp.exp(prm["a_log"])
    tri = causal.astype(f32)
    acum = _mm_exact(tri, dt * a_neg)
    acum_t = acum.T
    dt_t = dt.T
    last = sub == CH - 1
    ys, h_out = [], []
    for grp in range(4):
        bm = act[8 + grp]
        cm = act[12 + grp]
        cb = _mm(cm, bm, NT)
        for p in (2 * grp, 2 * grp + 1):
            h0, h1 = 2 * p, 2 * p + 1
            xp = act[p]
            hp = hps[p]
            wis = []
            for h in (h0, h1):
                seg = _col(acum, lane, h) - _row(acum_t, sub, h)
                decay = jnp.exp(jnp.where(causal, seg, -jnp.inf))
                wis.append(cb * decay * _row(dt_t, sub, h))
            wcat = jnp.concatenate(wis, axis=1)
            xbd = jnp.concatenate([jnp.where(left, xp, 0.0), jnp.where(left, 0.0, xp)], axis=0)
            y_diag = _mm(wcat, xbd, NN)
            a_end = [jnp.sum(jnp.where(last & (lane == h), acum, 0.0), keepdims=True) for h in (h0, h1)]
            a_col = jnp.where(left, _col(acum, lane, h0), _col(acum, lane, h1))
            dt_col = jnp.where(left, _col(dt, lane, h0), _col(dt, lane, h1))
            to_end = jnp.exp(jnp.where(left, a_end[0], a_end[1]) - a_col) * dt_col
            states = _mm(xp * to_end, bm, TN)
            chunk_decay = jnp.where(top, jnp.exp(a_end[0]), jnp.exp(a_end[1]))
            h_out.append(chunk_decay * hp + states)
            y_off = jnp.exp(a_col) * _mm(cm, hp, NT)
            d_skip = jnp.where(left[:1], _col(prm["d_heads"], lane[:1], h0), _col(prm["d_heads"], lane[:1], h1))
            ys.append((y_diag + y_off + xp * d_skip) * jax.nn.silu(zs[p]))

    b_out = []
    for grp in range(4):
        pair = (ys[2 * grp], ys[2 * grp + 1])
        ms = sum(jnp.sum(y * y, axis=1, keepdims=True) for y in pair) / 256.0
        r = lax.rsqrt(ms + RMS_EPS)
        for j, y in enumerate(pair):
            b_out.append(y * r * prm["norm_g"][2 * grp + j])
    return a_out, b_out, h_out


def _attn_block(qps, kprev, kcur, vprev, vcur, sink_row, first):
    lane = lax.broadcasted_iota(jnp.int32, (CH, CH), 1)
    left = lane < 64
    own = lane <= lax.broadcasted_iota(jnp.int32, (CH, CH), 0)
    own8 = jnp.concatenate([own] * N_BLK, axis=0)

    def both_halves(a):
        sw = _swap64(a)
        return [jnp.where(left, a, sw), jnp.where(left, sw, a)]

    kc, kp, vc, vp = both_halves(kcur), both_halves(kprev), both_halves(vcur), both_halves(vprev)
    outs = []
    for j in range(2):
        q8 = jnp.concatenate([part for p in range(4 * j, 4 * j + 4)
                              for part in (jnp.where(left, qps[p], 0.0), jnp.where(left, 0.0, qps[p]))], axis=0)
        s_cur = _row_blocks(_mm(q8, kc[j], NT))
        s_prev = _row_blocks(_mm(q8, kp[j], NT))
        probs = []
        for h in range(N_BLK):
            s = jnp.where(own, s_cur[h] * ATT_SCALE, jnp.where(first, -jnp.inf, s_prev[h] * ATT_SCALE))
            sink = _col(sink_row, lane[:1], N_BLK * j + h)
            m = lax.stop_gradient(jnp.maximum(jnp.max(s, axis=1, keepdims=True), sink))
            pexp = jnp.exp(s - m)
            probs.append(pexp / (jnp.sum(pexp, axis=1, keepdims=True) + jnp.exp(sink - m)))
        p8 = jnp.concatenate(probs, axis=0)
        o = _row_blocks(_mm(jnp.where(own8, p8, 0.0), vc[j], NN) + _mm(jnp.where(own8, 0.0, p8), vp[j], NN))
        for t in range(4):
            outs.append(jnp.where(left, o[2 * t], o[2 * t + 1]))
    return outs


def _rmsnorm(x, g):
    r = lax.rsqrt(jnp.mean(x * x, axis=-1, keepdims=True) + RMS_EPS)
    return x * r * g


def rmsnorm_fwd(x, g_row, name):
    s, d = x.shape
    tm = min(512, s)

    def body(x_ref, g_ref, y_ref):
        y_ref[...] = _rmsnorm(x_ref[...], g_ref[...]).astype(bf16)

    return pl.pallas_call(
        body, name=name, grid=(s // tm,),
        in_specs=[pl.BlockSpec((tm, d), lambda i: (i, 0)), pl.BlockSpec((1, d), lambda i: (0, 0))],
        out_specs=pl.BlockSpec((tm, d), lambda i: (i, 0)),
        out_shape=jax.ShapeDtypeStruct((s, d), bf16),
        compiler_params=_cparams(("parallel",)),
    )(x, g_row)


def _fit(dim, want):
    if dim <= want:
        return dim
    t = want
    while dim % t:
        t -= 128
    return t


def matmul(a, b, *, dims, name, out_dtype=f32, tm=1024, tn=512, tk=8192, a_pro=None, epi=None, epi_args=(),
           out_by_col_tile=False, after=None):
    if dims == "nn" and b.ndim == 3:
        (m, k), n, tn = a.shape, b.shape[0] * b.shape[2], b.shape[2]
    elif dims == "nn":
        (m, k), n = a.shape, b.shape[1]
    elif dims == "nt":
        (m, k), n = a.shape, b.shape[0]
    else:
        (k, m), n = a.shape, b.shape[1]
    tm, tn, tk = _fit(m, tm), _fit(n, tn), _fit(k, tk)
    nk = k // tk
    if dims == "nn":
        a_spec = pl.BlockSpec((tm, tk), lambda i, j, kk: (i, kk))
        b_spec = (pl.BlockSpec((None, tk, tn), lambda i, j, kk: (j, kk, 0)) if b.ndim == 3
                  else pl.BlockSpec((tk, tn), lambda i, j, kk: (kk, j)))
        dn = NN
    elif dims == "nt":
        a_spec = pl.BlockSpec((tm, tk), lambda i, j, kk: (i, kk))
        b_spec = pl.BlockSpec((tn, tk), lambda i, j, kk: (j, kk))
        dn = NT
    else:
        a_spec = pl.BlockSpec((tk, tm), lambda i, j, kk: (kk, i))
        b_spec = pl.BlockSpec((tk, tn), lambda i, j, kk: (kk, j))
        dn = TN
    e_specs = [pl.BlockSpec((tm, tn), lambda i, j, kk: (i, j)) if kind == "tile"
               else pl.BlockSpec((1, tn), lambda i, j, kk: (0, j)) for kind, _ in epi_args]
    n_epi = len(epi_args)
    order_specs = [] if after is None else [pl.BlockSpec((8, 128), lambda i, j, kk: (0, 0))]
    order_args = [] if after is None else [after]

    def body(*refs):
        a_ref, b_ref = refs[0], refs[1]
        e_refs = refs[2:2 + n_epi]
        n_in = 2 + n_epi + len(order_args)
        o_ref = refs[n_in]
        av = a_ref[...]
        if a_pro is not None:
            av = a_pro(av)
        part = _mm(av, b_ref[...], dn)

        def finish(acc):
            if epi is not None:
                acc = epi(acc, *[r[...] for r in e_refs])
            o_ref[...] = acc.astype(out_dtype)

        if nk == 1:
            finish(part)
        else:
            acc_ref = refs[n_in + 1]
            kk = pl.program_id(2)

            @pl.when(kk == 0)
            def _():
                acc_ref[...] = part

            @pl.when(kk > 0)
            def _():
                acc_ref[...] += part

            @pl.when(kk == nk - 1)
            def _():
                finish(acc_ref[...])

    if out_by_col_tile:
        out_spec = pl.BlockSpec((None, tm, tn), lambda i, j, kk: (j, i, 0))
        out_shape = jax.ShapeDtypeStruct((n // tn, m, tn), out_dtype)
    else:
        out_spec = pl.BlockSpec((tm, tn), lambda i, j, kk: (i, j))
        out_shape = jax.ShapeDtypeStruct((m, n), out_dtype)
    return pl.pallas_call(
        body, name=name, grid=(m // tm, n // tn, nk),
        in_specs=[a_spec, b_spec] + e_specs + order_specs,
        out_specs=out_spec,
        out_shape=out_shape,
        scratch_shapes=[pltpu.VMEM((tm, tn), f32)] if nk > 1 else [],
        compiler_params=_cparams(("parallel", "parallel", "arbitrary")),
    )(a, b, *[arr for _, arr in epi_args], *order_args)


def _relu2(a):
    r = jnp.maximum(a.astype(f32), 0.0)
    return r * r


def _add(acc, t):
    return acc + t


def _add_bias(acc, t):
    return acc + t


def _add_bias_res(acc, bias, res):
    return acc + bias + res


def _times_relu2_grad(acc, a):
    return acc * (2.0 * jnp.maximum(a.astype(f32), 0.0))


def matmul_rows(a, b, *, dims, name, epi, epi_args, outs, tm=512, a_pro=None, after=None):
    m, k = a.shape
    n = b.shape[-1] if dims == "nn" else b.shape[-2]
    tm = _fit(m, tm)
    dn = NN if dims == "nn" else NT
    e_specs = [pl.BlockSpec((tm, arr.shape[1]), lambda i: (i, 0)) if kind == "tile"
               else pl.BlockSpec((1, arr.shape[1]), lambda i: (0, 0)) for kind, arr in epi_args]
    order_specs = [] if after is None else [pl.BlockSpec((8, 128), lambda i: (0, 0))]
    order_args = [] if after is None else [after]
    n_in = 2 + len(epi_args) + len(order_args)

    def body(*refs):
        av = refs[0][...]
        if a_pro is not None:
            av = a_pro(av)
        if b.ndim == 3:
            kb = b.shape[2]
            acc = sum(_mm(av[:, s * kb:(s + 1) * kb], refs[1][s], dn) for s in range(b.shape[0]))
        else:
            acc = _mm(av, refs[1][...], dn)
        vals = epi(acc, *[r[...] for r in refs[2:2 + len(epi_args)]])
        for (kind, _), o_ref, val in zip(outs, refs[n_in:], vals):
            if kind == "tile":
                o_ref[...] = val.astype(o_ref.dtype)
            else:
                @pl.when(pl.program_id(0) == 0)
                def _():
                    o_ref[...] = jnp.zeros_like(o_ref)

                o_ref[...] += val

    out_specs = [pl.BlockSpec((tm, n), lambda i: (i, 0)) if kind == "tile" else pl.BlockSpec((1, arg), lambda i: (0, 0))
                 for kind, arg in outs]
    out_shape = [jax.ShapeDtypeStruct((m, n), arg) if kind == "tile" else jax.ShapeDtypeStruct((1, arg), f32)
                 for kind, arg in outs]
    return pl.pallas_call(
        body, name=name, grid=(m // tm,),
        in_specs=[pl.BlockSpec((tm, k), lambda i: (i, 0)), pl.BlockSpec(b.shape, lambda i: (0,) * b.ndim)]
                 + e_specs + order_specs,
        out_specs=out_specs, out_shape=out_shape,
        compiler_params=_cparams(("arbitrary",)),
    )(a, b, *[arr for _, arr in epi_args], *order_args)


def _res_norm(acc, res, g):
    h = acc + res
    return h, _rmsnorm(h, g)


def _bias_res_norm(acc, bias, res, g):
    h = acc + bias + res
    return h, _rmsnorm(h, g)


def _res_norm_loss(acc, res, g, target):
    def f(h, gv):
        err = jnp.square(_rmsnorm(h, gv) - target)
        return 0.5 * jnp.sum(jnp.mean(err, axis=-1, keepdims=True), axis=0, keepdims=True)

    loss, vjp = jax.vjp(f, acc + res, g)
    dh, dg = vjp(jnp.ones_like(loss))
    return dh, dg, jnp.broadcast_to(loss, (1, 128))


def _norm_bwd_res_colsum(dy, x, g, res):
    dx, dg = _norm_bwd_res(dy, x, g, res)
    return dx, dg, jnp.sum(dx, axis=0, keepdims=True)


def _norm_bwd_res(dy, x, g, res):
    _, vjp = jax.vjp(_rmsnorm, x, g)
    dx, dg = vjp(dy)
    return res + dx, dg


_MIXER_PARAM_SHAPES = (
    ("ln_g", (1, D_MODEL)), ("ln_b", (1, D_MODEL)), ("wm", (N_BLK, CH, CH)), ("bs_t", (CH, CH)),
    ("conv_w", (8, 2048)), ("conv_b", (1, 2048)), ("dt_bias", (1, CH)), ("a_log", (1, CH)),
    ("d_heads", (1, CH)), ("norm_g", (1, D_MODEL)),
)


def _blocks(v, n, off=0):
    return [v[:, off + i * CH: off + (i + 1) * CH] for i in range(n)]


def _split_mixer_params(vals):
    p = dict(vals)
    return {
        "ln_g": _blocks(p["ln_g"], N_BLK), "ln_b": _blocks(p["ln_b"], N_BLK),
        "wm": [p["wm"][g] for g in range(N_BLK)], "bs_t": p["bs_t"],
        "conv_w": _blocks(p["conv_w"], XBC_BLKS), "conv_b": _blocks(p["conv_b"], XBC_BLKS),
        "dt_bias": p["dt_bias"], "a_log": p["a_log"], "d_heads": p["d_heads"],
        "norm_g": _blocks(p["norm_g"], N_BLK),
    }


def _mixer_leaves(proj_ref, halo_ref, keep_halo):
    pv = proj_ref
    us = [pv[:, OFF_U + i * CH: OFF_U + (i + 1) * CH] for i in range(N_BLK)]
    vs = [pv[:, OFF_V + i * CH: OFF_V + (i + 1) * CH] for i in range(N_BLK)]
    zs = [pv[:, OFF_Z + i * CH: OFF_Z + (i + 1) * CH] for i in range(N_BLK)]
    xbcs = [pv[:, OFF_X + i * CH: OFF_X + (i + 1) * CH] for i in range(XBC_BLKS)]
    halos = [halo_ref[:, OFF_X + i * CH: OFF_X + (i + 1) * CH] * keep_halo for i in range(XBC_BLKS)]
    dtblk = pv[:, OFF_DT: OFF_DT + CH]
    return us, vs, zs, xbcs, halos, dtblk


def mixer_fwd(proj, prm):
    s = proj.shape[0]
    nc = s // CH
    names = [n for n, _ in _MIXER_PARAM_SHAPES]

    def body(proj_ref, halo_ref, *rest):
        p_refs = rest[:len(names)]
        ab_ref, hs_ref, h_ref = rest[len(names):]
        c = pl.program_id(0)

        @pl.when(c == 0)
        def _():
            h_ref[...] = jnp.zeros_like(h_ref)

        hs_ref[...] = h_ref[...]
        keep = (c > 0).astype(f32)
        us, vs, zs, xbcs, halos, dtblk = _mixer_leaves(proj_ref, halo_ref, keep)
        hps = [h_ref[i * CH:(i + 1) * CH, :] for i in range(N_BLK)]
        p = _split_mixer_params({n: r[...] for n, r in zip(names, p_refs)})
        a_out, b_out, h_out = _mixer_chunk(us, vs, zs, xbcs, halos, dtblk, hps, p)
        for i in range(N_BLK):
            ab_ref[:, i * CH:(i + 1) * CH] = a_out[i].astype(bf16)
            ab_ref[:, D_MODEL + i * CH: D_MODEL + (i + 1) * CH] = b_out[i].astype(bf16)
            h_ref[i * CH:(i + 1) * CH, :] = h_out[i]

    def const(shape):
        return pl.BlockSpec(shape, lambda c: (0,) * len(shape))

    return pl.pallas_call(
        body, name="mixer_fwd", grid=(nc,),
        in_specs=[pl.BlockSpec((CH, NP_IN), lambda c: (c, 0)),
                  pl.BlockSpec((8, NP_IN), lambda c: (jnp.maximum(c * (CH // 8) - 1, 0), 0))]
                 + [const(shp) for _, shp in _MIXER_PARAM_SHAPES],
        out_specs=[pl.BlockSpec((CH, 2 * D_MODEL), lambda c: (c, 0)),
                   pl.BlockSpec((None, D_MODEL, CH), lambda c: (c, 0, 0))],
        out_shape=[jax.ShapeDtypeStruct((s, 2 * D_MODEL), bf16), jax.ShapeDtypeStruct((nc, D_MODEL, CH), f32)],
        scratch_shapes=[pltpu.VMEM((D_MODEL, CH), f32)],
        compiler_params=_cparams(("arbitrary",)),
    )(proj, proj, *[prm[n] for n in names])


def mixer_bwd(proj, hstates, dab, prm):
    s = proj.shape[0]
    nc = s // CH
    names = [n for n, _ in _MIXER_PARAM_SHAPES]
    npar = len(names)

    def body(proj_ref, halo_ref, hs_ref, dab_ref, *rest):
        p_refs = rest[:npar]
        dproj_ref = rest[npar]
        g_refs = rest[npar + 1: 2 * npar + 1]
        dh_ref, dhalo_ref = rest[2 * npar + 1:]
        i = pl.program_id(0)
        c = nc - 1 - i

        @pl.when(i == 0)
        def _():
            dh_ref[...] = jnp.zeros_like(dh_ref)
            dhalo_ref[...] = jnp.zeros_like(dhalo_ref)
            for r in g_refs:
                r[...] = jnp.zeros_like(r)

        keep = (c > 0).astype(f32)
        us, vs, zs, xbcs, halos, dtblk = _mixer_leaves(proj_ref, halo_ref, keep)
        hps = [hs_ref[j * CH:(j + 1) * CH, :] for j in range(N_BLK)]
        pvals = {n: r[...] for n, r in zip(names, p_refs)}

        def fn(us, vs, zs, xbcs, halos, dtblk, hps, pvals):
            return _mixer_chunk(us, vs, zs, xbcs, halos, dtblk, hps, _split_mixer_params(pvals))

        _, vjp = jax.vjp(fn, us, vs, zs, xbcs, halos, dtblk, hps, pvals)
        da = [dab_ref[:, j * CH:(j + 1) * CH].astype(f32) for j in range(N_BLK)]
        db = [dab_ref[:, D_MODEL + j * CH: D_MODEL + (j + 1) * CH].astype(f32) for j in range(N_BLK)]
        dh = [dh_ref[j * CH:(j + 1) * CH, :] for j in range(N_BLK)]
        dus, dvs, dzs, dxbcs, dhalos, ddt, dhps, dp = vjp((da, db, dh))

        for j in range(N_BLK):
            dproj_ref[:, OFF_U + j * CH: OFF_U + (j + 1) * CH] = dus[j].astype(bf16)
            dproj_ref[:, OFF_V + j * CH: OFF_V + (j + 1) * CH] = dvs[j].astype(bf16)
            dproj_ref[:, OFF_Z + j * CH: OFF_Z + (j + 1) * CH] = dzs[j].astype(bf16)
            dh_ref[j * CH:(j + 1) * CH, :] = dhps[j]
        zeros_top = jnp.zeros((CH - 8, CH), f32)
        for j in range(XBC_BLKS):
            late = jnp.concatenate([zeros_top, dhalo_ref[:, j * CH:(j + 1) * CH]], axis=0)
            dproj_ref[:, OFF_X + j * CH: OFF_X + (j + 1) * CH] = (dxbcs[j] + late).astype(bf16)
        for j in range(XBC_BLKS):
            dhalo_ref[:, j * CH:(j + 1) * CH] = dhalos[j] * keep
        lane = lax.broadcasted_iota(jnp.int32, (CH, CH), 1)
        dproj_ref[:, OFF_DT: OFF_DT + CH] = jnp.where(lane < SSM_HEADS, ddt, 0.0).astype(bf16)
        dproj_ref[:, OFF_DT + CH:] = jnp.zeros((CH, NP_IN - OFF_DT - CH), bf16)
        for n, r in zip(names, g_refs):
            r[...] += dp[n]

    def const(shape):
        return pl.BlockSpec(shape, lambda i: (0,) * len(shape))

    outs = pl.pallas_call(
        body, name="mixer_bwd", grid=(nc,),
        in_specs=[pl.BlockSpec((CH, NP_IN), lambda i: (nc - 1 - i, 0)),
                  pl.BlockSpec((8, NP_IN), lambda i: (jnp.maximum((nc - 1 - i) * (CH // 8) - 1, 0), 0)),
                  pl.BlockSpec((None, D_MODEL, CH), lambda i: (nc - 1 - i, 0, 0)),
                  pl.BlockSpec((CH, 2 * D_MODEL), lambda i: (nc - 1 - i, 0))]
                 + [const(shp) for _, shp in _MIXER_PARAM_SHAPES],
        out_specs=[pl.BlockSpec((CH, NP_IN), lambda i: (nc - 1 - i, 0))]
                  + [const(shp) for _, shp in _MIXER_PARAM_SHAPES],
        out_shape=[jax.ShapeDtypeStruct((s, NP_IN), bf16)]
                  + [jax.ShapeDtypeStruct(shp, f32) for _, shp in _MIXER_PARAM_SHAPES],
        scratch_shapes=[pltpu.VMEM((D_MODEL, CH), f32), pltpu.VMEM((8, 2048), f32)],
        compiler_params=_cparams(("arbitrary",)),
    )(proj, proj, hstates, dab, *[prm[n] for n in names])
    return outs[0], dict(zip(names, outs[1:]))


_K_BLK = D_MODEL // CH
_V_BLK = _K_BLK + 1


def _attn_specs(rev, nb):
    def blk(i):
        return nb - 1 - i if rev else i

    q_spec = pl.BlockSpec((CH, D_MODEL), lambda i: (blk(i), 0))
    kv = lambda col, prev: pl.BlockSpec(
        (CH, CH), lambda i: (jnp.maximum(blk(i) - 1, 0) if prev else blk(i), col))
    return q_spec, [kv(_K_BLK, True), kv(_K_BLK, False), kv(_V_BLK, True), kv(_V_BLK, False)]


def attn_fwd(qkv, sink_row):
    s = qkv.shape[0]
    nb = s // CH

    def body(q_ref, kp_ref, kc_ref, vp_ref, vc_ref, sink_ref, o_ref):
        qps = [q_ref[:, p * CH:(p + 1) * CH] for p in range(N_BLK)]
        outs = _attn_block(qps, kp_ref[...], kc_ref[...], vp_ref[...], vc_ref[...], sink_ref[...],
                           pl.program_id(0) == 0)
        for p in range(N_BLK):
            o_ref[:, p * CH:(p + 1) * CH] = outs[p].astype(bf16)

    q_spec, kv_specs = _attn_specs(False, nb)
    return pl.pallas_call(
        body, name="attn_fwd", grid=(nb,),
        in_specs=[q_spec] + kv_specs + [pl.BlockSpec((1, CH), lambda i: (0, 0))],
        out_specs=pl.BlockSpec((CH, D_MODEL), lambda i: (i, 0)),
        out_shape=jax.ShapeDtypeStruct((s, D_MODEL), bf16),
        compiler_params=_cparams(("parallel",)),
    )(qkv, qkv, qkv, qkv, qkv, sink_row)


def attn_bwd(qkv, sink_row, dout):
    s = qkv.shape[0]
    nb = s // CH

    def body(q_ref, kp_ref, kc_ref, vp_ref, vc_ref, sink_ref, do_ref, dqkv_ref, dsink_ref, db_ref, carry_ref):
        i = pl.program_id(0)
        blk = nb - 1 - i

        @pl.when(i == 0)
        def _():
            dsink_ref[...] = jnp.zeros_like(dsink_ref)
            db_ref[...] = jnp.zeros_like(db_ref)
            carry_ref[...] = jnp.zeros_like(carry_ref)

        qps = [q_ref[:, p * CH:(p + 1) * CH] for p in range(N_BLK)]
        first = blk == 0
        _, vjp = jax.vjp(lambda *a: _attn_block(*a, first), qps, kp_ref[...], kc_ref[...], vp_ref[...],
                         vc_ref[...], sink_ref[...])
        dos = [do_ref[:, p * CH:(p + 1) * CH].astype(f32) for p in range(N_BLK)]
        dqs, dkp, dkc, dvp, dvc, dsink = vjp(dos)
        blocks = list(dqs) + [dkc + carry_ref[0], dvc + carry_ref[1]]
        for p, val in enumerate(blocks):
            dqkv_ref[:, p * CH:(p + 1) * CH] = val.astype(bf16)
            db_ref[:, p * CH:(p + 1) * CH] += jnp.sum(val, axis=0, keepdims=True)
        keep = jnp.logical_not(first).astype(f32)
        carry_ref[0] = dkp * keep
        carry_ref[1] = dvp * keep
        dsink_ref[...] += dsink

    q_spec, kv_specs = _attn_specs(True, nb)
    return pl.pallas_call(
        body, name="attn_bwd", grid=(nb,),
        in_specs=[q_spec] + kv_specs + [pl.BlockSpec((1, CH), lambda i: (0, 0)),
                                        pl.BlockSpec((CH, D_MODEL), lambda i: (nb - 1 - i, 0))],
        out_specs=[pl.BlockSpec((CH, QKV_DIM), lambda i: (nb - 1 - i, 0)), pl.BlockSpec((1, CH), lambda i: (0, 0)),
                   pl.BlockSpec((1, QKV_DIM), lambda i: (0, 0))],
        out_shape=[jax.ShapeDtypeStruct((s, QKV_DIM), bf16), jax.ShapeDtypeStruct((1, CH), f32),
                   jax.ShapeDtypeStruct((1, QKV_DIM), f32)],
        scratch_shapes=[pltpu.VMEM((2, CH, CH), f32)],
        compiler_params=_cparams(("arbitrary",)),
    )(qkv, qkv, qkv, qkv, qkv, sink_row, dout)


def _adamw_update(w, g, m, v):
    nm = ADAM_B1 * m + (1.0 - ADAM_B1) * g
    nv = ADAM_B2 * v + (1.0 - ADAM_B2) * jnp.square(g)
    m_hat = nm / (1.0 - ADAM_B1 ** ADAM_STEP)
    v_hat = nv / (1.0 - ADAM_B2 ** ADAM_STEP)
    return -ADAM_LR * (m_hat / (jnp.sqrt(v_hat) + ADAM_EPS) + ADAM_WD * w), nm, nv


def adamw_rows(w, r, m, v, layer, row_off, name, into=None):
    rows, cols = w.shape[1], w.shape[2]
    tr = 256
    assert rows % tr == 0 and row_off % tr == 0

    def body(w_ref, r_ref, m_ref, v_ref, *rest):
        g_ref, d_ref, nm_ref, nv_ref = rest[-4:]
        g = r_ref[...]
        g_ref[...] = g
        d_ref[...], nm_ref[...], nv_ref[...] = _adamw_update(w_ref[...], g, m_ref[...], v_ref[...])

    tile = pl.BlockSpec((None, tr, cols), lambda i: (layer, i, 0))
    extra = [] if into is None else list(into)
    return pl.pallas_call(
        body, name=name, grid=(rows // tr,),
        in_specs=[tile, pl.BlockSpec((tr, cols), lambda i: (row_off // tr + i, 0)), tile, tile] + [_ANY] * len(extra),
        out_specs=[tile] * 4, out_shape=[jax.ShapeDtypeStruct(w.shape, f32)] * 4,
        input_output_aliases={4 + k: k for k in range(len(extra))},
        compiler_params=_cparams(("parallel",)),
    )(w, r, m, v, *extra)


def adamw(w, g, m, v, name):
    def body(w_ref, g_ref, m_ref, v_ref, d_ref, nm_ref, nv_ref):
        d_ref[...], nm_ref[...], nv_ref[...] = _adamw_update(w_ref[...], g_ref[...], m_ref[...], v_ref[...])

    out_shape = [jax.ShapeDtypeStruct(w.shape, f32)] * 3
    if w.ndim == 3 and w.shape[1] == 1:
        tr = max(t for t in range(1, 129) if w.shape[0] % t == 0)
        tile = pl.BlockSpec((tr, 1, w.shape[2]), lambda i: (i, 0, 0))
        return pl.pallas_call(
            body, name=name, grid=(w.shape[0] // tr,),
            in_specs=[tile] * 4, out_specs=[tile] * 3, out_shape=out_shape,
            compiler_params=_cparams(("parallel",)),
        )(w, g, m, v)
    if w.ndim == 3 and w.shape[1] % 256 == 0:
        tile = pl.BlockSpec((None, 256, w.shape[2]), lambda l, i: (l, i, 0))
        return pl.pallas_call(
            body, name=name, grid=(w.shape[0], w.shape[1] // 256),
            in_specs=[tile] * 4, out_specs=[tile] * 3, out_shape=out_shape,
            compiler_params=_cparams(("parallel", "parallel")),
        )(w, g, m, v)
    return pl.pallas_call(body, name=name, in_specs=[_VMEM] * 4, out_specs=[_VMEM] * 3, out_shape=out_shape,
                          compiler_params=_cparams())(w, g, m, v)


_MESH = pl.DeviceIdType.MESH
_ANY = pl.BlockSpec(memory_space=pl.ANY)
_VMEM = pl.BlockSpec(memory_space=pltpu.VMEM)


def _place():
    x, y, c = lax.axis_index("x"), lax.axis_index("y"), lax.axis_index("c")
    chips = [(1 - x, y), (x, 1 - y), (1 - x, 1 - y)]
    return x, y, c, 2 * x + y, chips, [2 * cx + cy for cx, cy in chips]


def _half(c, rows):
    return pl.ds(pl.multiple_of(c * (rows // 2), 16), rows // 2)


def _step_rows(rows):
    return max(t for t in range(16, 641, 16) if rows % t == 0)


def place_shard(b, slot, name, dtype=bf16, after=None, cols=None):
    r, c_in = b.shape
    c = c_in if cols is None else cols
    tr = _step_rows(r)

    def body(slot_ref, b_ref, *rest):
        o_ref = rest[-1]
        if c > c_in:
            whole = (c_in // 128) * 128
            o_ref[:, whole:] = jnp.zeros((tr, c - whole), dtype)
        o_ref[:, :c_in] = b_ref[...].astype(dtype)

    order_specs = [] if after is None else [pl.BlockSpec((8, 128), lambda i, s: (0, 0))]
    return pl.pallas_call(
        body, name=name,
        grid_spec=pltpu.PrefetchScalarGridSpec(
            num_scalar_prefetch=1, grid=(r // tr,),
            in_specs=[pl.BlockSpec((tr, c_in), lambda i, s: (i, 0))] + order_specs,
            out_specs=pl.BlockSpec((None, tr, c), lambda i, s: (s[0], i, 0))),
        out_shape=jax.ShapeDtypeStruct((N_CHIPS, r, c), dtype),
        compiler_params=_cparams(("parallel",)),
    )(slot, b, *([] if after is None else [after]))


_HBM = pl.BlockSpec(memory_space=pltpu.HBM)
_SEM = pl.BlockSpec(memory_space=pltpu.SEMAPHORE)
_EFFECT = pltpu.SideEffectType.DATAFLOW_SIDE_EFFECTING


def _gather_ici_copies(bufs, send_sems, recv_sems):
    x, y, c, me, chips, chip_idx = _place()
    return [pltpu.make_async_remote_copy(
        src_ref=buf.at[me, _half(c, buf.shape[1])], dst_ref=buf.at[chip_idx[j], _half(c, buf.shape[1])],
        send_sem=send_sems.at[3 * k + j], recv_sem=recv_sems.at[3 * k + j],
        device_id=(*chips[j], c), device_id_type=_MESH) for j in range(3) for k, buf in enumerate(bufs)]


def gather_start(groups, tag):
    sizes = [len(g) for g in groups]
    flat = [b for g in groups for b in g]
    n = len(flat)

    def body(*refs):
        bufs, sems = refs[:n], refs[n:n + 2 * len(groups)]
        refs[-1][...] = jnp.zeros_like(refs[-1])
        x, y, c, me, chips, chip_idx = _place()
        lo = 0
        for gi, size in enumerate(sizes):
            for j in range(3):
                for k, buf in enumerate(bufs[lo:lo + size]):
                    mine = buf.at[me, _half(c, buf.shape[1])]
                    pltpu.make_async_remote_copy(
                        src_ref=mine, dst_ref=mine, send_sem=sems[2 * gi].at[3 * k + j],
                        recv_sem=sems[2 * gi + 1].at[3 * k + j], device_id=(*chips[j], c),
                        device_id_type=_MESH).start()
            lo += size

    sem_shapes = [pltpu.SemaphoreType.DMA((3 * size,)) for size in sizes for _ in range(2)]
    outs = pl.pallas_call(
        body, name=f"gather_start_{tag}",
        out_shape=(*sem_shapes, *[pltpu.HBM(b.shape, b.dtype) for b in flat], jax.ShapeDtypeStruct((8, 128), f32)),
        in_specs=[_HBM] * n, out_specs=(*[_SEM] * len(sem_shapes), *[_HBM] * n, _VMEM),
        input_output_aliases={i: len(sem_shapes) + i for i in range(n)},
        compiler_params=pltpu.CompilerParams(has_side_effects=_EFFECT),
    )(*[pltpu.with_memory_space_constraint(b, pltpu.HBM) for b in flat])
    sems = [(outs[2 * gi], outs[2 * gi + 1]) for gi in range(len(groups))]
    thru, lo = [], len(sem_shapes)
    for size in sizes:
        thru.append(list(outs[lo:lo + size]))
        lo += size
    return sems, thru, outs[-1]


def gather_wait(bufs, sems, after, tag):
    n = len(bufs)

    def body(*refs):
        for cp in _gather_ici_copies(refs[:n], refs[n], refs[n + 1]):
            cp.wait_send()
            cp.wait_recv()

    extra = list(after)
    return list(pl.pallas_call(
        body, name=f"gather_wait_{tag}",
        out_shape=[pltpu.HBM(b.shape, b.dtype) for b in bufs],
        in_specs=[_HBM] * n + [_SEM, _SEM] + [_ANY] * len(extra), out_specs=[_HBM] * n,
        input_output_aliases={i: i for i in range(n)},
        compiler_params=pltpu.CompilerParams(has_side_effects=_EFFECT),
    )(*bufs, *sems, *extra))


def gather_forward(bufs, tag):
    n = len(bufs)

    def body(*refs):
        out_refs = refs[n:2 * n]
        send_sems, recv_sems = refs[2 * n:]
        x, y, c, me, chips, chip_idx = _place()

        def copy(k, j, half):
            part = out_refs[k].at[chip_idx[j], _half(half, out_refs[k].shape[1])]
            return pltpu.make_async_remote_copy(
                src_ref=part, dst_ref=part, send_sem=send_sems.at[3 * k + j], recv_sem=recv_sems.at[3 * k + j],
                device_id=(x, y, 1 - c), device_id_type=_MESH)

        sends = [copy(k, j, c) for j in range(3) for k in range(n)]
        for cp in sends:
            cp.start()
        for j in range(3):
            for k in range(n):
                copy(k, j, 1 - c).wait_recv()
        for cp in sends:
            cp.wait_send()

    return list(pl.pallas_call(
        body, name=f"gather_forward_{tag}",
        out_shape=[jax.ShapeDtypeStruct(b.shape, b.dtype) for b in bufs],
        in_specs=[_ANY] * n, out_specs=[_ANY] * n, input_output_aliases={i: i for i in range(n)},
        scratch_shapes=[pltpu.SemaphoreType.DMA((3 * n,)), pltpu.SemaphoreType.DMA((3 * n,))],
    )(*bufs))


def _forward_copy(ref, k, j, half, send_sems, recv_sems):
    x, y, c, me, chips, chip_idx = _place()
    part = ref.at[chip_idx[j], _half(half, ref.shape[1])]
    return pltpu.make_async_remote_copy(
        src_ref=part, dst_ref=part, send_sem=send_sems.at[3 * k + j], recv_sem=recv_sems.at[3 * k + j],
        device_id=(x, y, 1 - c), device_id_type=_MESH)


def forward_start(bufs, tag):
    n = len(bufs)

    def body(*refs):
        c = _place()[2]
        for j in range(3):
            for k in range(n):
                _forward_copy(refs[k], k, j, c, refs[n], refs[n + 1]).start()
        refs[-1][...] = jnp.zeros_like(refs[-1])

    outs = pl.pallas_call(
        body, name=f"forward_start_{tag}",
        out_shape=(pltpu.SemaphoreType.DMA((3 * n,)), pltpu.SemaphoreType.DMA((3 * n,)),
                   *[pltpu.HBM(b.shape, b.dtype) for b in bufs], jax.ShapeDtypeStruct((8, 128), f32)),
        in_specs=[_HBM] * n, out_specs=(_SEM, _SEM, *[_HBM] * n, _VMEM),
        input_output_aliases={i: 2 + i for i in range(n)},
        compiler_params=pltpu.CompilerParams(has_side_effects=_EFFECT),
    )(*[pltpu.with_memory_space_constraint(b, pltpu.HBM) for b in bufs])
    return (outs[0], outs[1], list(outs[2:2 + n])), outs[-1]


def forward_wait(send_sems, recv_sems, bufs, after, tag):
    n = len(bufs)

    def body(*refs):
        c = _place()[2]
        for j in range(3):
            for k in range(n):
                _forward_copy(refs[k], k, j, c, refs[n], refs[n + 1]).wait_send()
                _forward_copy(refs[k], k, j, 1 - c, refs[n], refs[n + 1]).wait_recv()

    return list(pl.pallas_call(
        body, name=f"forward_wait_{tag}",
        out_shape=[pltpu.HBM(b.shape, b.dtype) for b in bufs],
        in_specs=[_HBM] * n + [_SEM, _SEM, _ANY], out_specs=[_HBM] * n,
        input_output_aliases={i: i for i in range(n)},
        compiler_params=pltpu.CompilerParams(has_side_effects=_EFFECT),
    )(*bufs, send_sems, recv_sems, after))


def exchange_halves(bufs, tag):
    n = len(bufs)

    def body(*refs):
        g_refs, out_refs = refs[:n], refs[n:2 * n]
        send_sems, recv_sems = refs[2 * n:]
        x, y, c, *_ = _place()
        cps = [pltpu.make_async_remote_copy(
            src_ref=g_refs[b].at[:, _half(1 - c, g_refs[b].shape[1])], dst_ref=out_refs[b],
            send_sem=send_sems.at[b], recv_sem=recv_sems.at[b], device_id=(x, y, 1 - c), device_id_type=_MESH)
            for b in range(n)]
        for cp in cps:
            cp.start()
        for cp in cps:
            cp.wait()

    return pl.pallas_call(
        body, name=f"exchange_halves_{tag}",
        out_shape=[jax.ShapeDtypeStruct((N_CHIPS, b.shape[1] // 2, b.shape[2]), b.dtype) for b in bufs],
        in_specs=[_ANY] * n, out_specs=[_ANY] * n,
        scratch_shapes=[pltpu.SemaphoreType.DMA((n,)), pltpu.SemaphoreType.DMA((n,))],
    )(*bufs)


def add_halves(g, got, c_idx, name):
    hr, cols = got.shape[1], got.shape[2]
    tr = _step_rows(hr)
    steps = hr // tr

    def body(c_ref, g_ref, got_ref, o_ref):
        o_ref[...] = (g_ref[...].astype(f32) + got_ref[...].astype(f32)).astype(bf16)

    return pl.pallas_call(
        body, name=name,
        grid_spec=pltpu.PrefetchScalarGridSpec(
            num_scalar_prefetch=1, grid=(N_CHIPS, steps),
            in_specs=[pl.BlockSpec((None, tr, cols), lambda s, i, c: (s, c[0] * steps + i, 0)),
                      pl.BlockSpec((None, tr, cols), lambda s, i, c: (s, i, 0))],
            out_specs=pl.BlockSpec((None, tr, cols), lambda s, i, c: (s, i, 0))),
        out_shape=jax.ShapeDtypeStruct(got.shape, bf16),
        compiler_params=_cparams(("parallel", "parallel")),
    )(c_idx, g, got)


def sum_chips(t, got, place_idx, name):
    hr, cols = t.shape[1], t.shape[2]
    tr = _step_rows(hr)
    steps = hr // tr

    def body(idx_ref, t_ref, got_ref, o_ref):
        acc = t_ref[...].astype(f32)
        for j in range(3):
            acc = acc + got_ref[j].astype(f32)
        o_ref[...] = acc

    return pl.pallas_call(
        body, name=name,
        grid_spec=pltpu.PrefetchScalarGridSpec(
            num_scalar_prefetch=1, grid=(steps,),
            in_specs=[pl.BlockSpec((None, tr, cols), lambda i, idx: (idx[0], i, 0)),
                      pl.BlockSpec((3, tr, cols), lambda i, idx: (0, i, 0))],
            out_specs=pl.BlockSpec((tr, cols), lambda i, idx: (idx[1] * steps + i, 0))),
        out_shape=jax.ShapeDtypeStruct((2 * hr, cols), f32),
        compiler_params=_cparams(("parallel",)),
    )(place_idx, t, got)


def _share_copies(refs, send_sems, recv_sems):
    x, y, c, *_ = _place()
    return [pltpu.make_async_remote_copy(
        src_ref=ref.at[_half(c, ref.shape[0])], dst_ref=ref.at[_half(c, ref.shape[0])], send_sem=send_sems.at[b],
        recv_sem=recv_sems.at[b], device_id=(x, y, 1 - c), device_id_type=_MESH) for b, ref in enumerate(refs)]


def share_start(bufs, tag):
    n = len(bufs)

    def body(*refs):
        for cp in _share_copies(refs[:n], refs[n], refs[n + 1]):
            cp.start()
        token = refs[-1]
        token[...] = jnp.zeros_like(token)

    outs = pl.pallas_call(
        body, name=f"share_start_{tag}",
        out_shape=(pltpu.SemaphoreType.DMA((n,)), pltpu.SemaphoreType.DMA((n,)),
                   *[pltpu.HBM(b.shape, b.dtype) for b in bufs], jax.ShapeDtypeStruct((8, 128), f32)),
        in_specs=[_HBM] * n, out_specs=(_SEM, _SEM, *[_HBM] * n, _VMEM),
        input_output_aliases={i: 2 + i for i in range(n)},
        compiler_params=pltpu.CompilerParams(has_side_effects=_EFFECT),
    )(*[pltpu.with_memory_space_constraint(b, pltpu.HBM) for b in bufs])
    return (outs[0], outs[1], list(outs[2:2 + n])), outs[-1]


def share_wait(send_sems, recv_sems, bufs, after, tag):
    n = len(bufs)

    def body(*refs):
        x, y, c, *_ = _place()
        for b, ref in enumerate(refs[:n]):
            cp = pltpu.make_async_remote_copy(
                src_ref=ref.at[_half(c, ref.shape[0])], dst_ref=ref.at[_half(1 - c, ref.shape[0])],
                send_sem=refs[n].at[b], recv_sem=refs[n + 1].at[b], device_id=(x, y, 1 - c), device_id_type=_MESH)
            cp.wait_send()
            cp.wait_recv()

    return list(pl.pallas_call(
        body, name=f"share_wait_{tag}",
        out_shape=[pltpu.HBM(b.shape, b.dtype) for b in bufs],
        in_specs=[_HBM] * n + [_SEM, _SEM, _ANY], out_specs=[_HBM] * n,
        input_output_aliases={i: i for i in range(n)},
        compiler_params=pltpu.CompilerParams(has_side_effects=_EFFECT),
    )(*bufs, send_sems, recv_sems, after))


def _scatter_copies(t_refs, land_refs, send_sems, recv_sems):
    x, y, c, me, chips, chip_idx = _place()
    return [pltpu.make_async_remote_copy(
        src_ref=t_refs[b].at[chip_idx[j]], dst_ref=land_refs[b].at[j], send_sem=send_sems.at[3 * b + j],
        recv_sem=recv_sems.at[3 * b + j], device_id=(*chips[j], c), device_id_type=_MESH)
        for j in range(3) for b in range(len(t_refs))]


def scatter_start(ts, tag):
    n = len(ts)
    lands = [lax.empty((3,) + t.shape[1:], t.dtype) for t in ts]

    def body(*refs):
        for cp in _scatter_copies(refs[:n], refs[n:2 * n], refs[2 * n], refs[2 * n + 1]):
            cp.start()
        token = refs[-1]
        token[...] = jnp.zeros_like(token)

    hbm = [pltpu.HBM(a.shape, a.dtype) for a in (*ts, *lands)]
    outs = pl.pallas_call(
        body, name=f"scatter_start_{tag}",
        out_shape=(pltpu.SemaphoreType.DMA((3 * n,)), pltpu.SemaphoreType.DMA((3 * n,)), *hbm,
                   jax.ShapeDtypeStruct((8, 128), f32)),
        in_specs=[_HBM] * (2 * n), out_specs=(_SEM, _SEM, *[_HBM] * (2 * n), _VMEM),
        input_output_aliases={i: 2 + i for i in range(2 * n)},
        compiler_params=pltpu.CompilerParams(has_side_effects=_EFFECT),
    )(*[pltpu.with_memory_space_constraint(a, pltpu.HBM) for a in (*ts, *lands)])
    return outs[0], outs[1], list(outs[2:2 + n]), list(outs[2 + n:2 + 2 * n]), outs[-1]


def scatter_wait(send_sems, recv_sems, ts, lands, after, tag):
    n = len(ts)

    def body(*refs):
        for cp in _scatter_copies(refs[:n], refs[n:2 * n], refs[2 * n], refs[2 * n + 1]):
            cp.wait_send()
            cp.wait_recv()

    outs = pl.pallas_call(
        body, name=f"scatter_wait_{tag}",
        out_shape=[pltpu.HBM(a.shape, a.dtype) for a in (*ts, *lands)],
        in_specs=[_HBM] * (2 * n) + [_SEM, _SEM, _ANY], out_specs=[_HBM] * (2 * n),
        input_output_aliases={i: i for i in range(2 * n)},
        compiler_params=pltpu.CompilerParams(has_side_effects=_EFFECT),
    )(*ts, *lands, send_sems, recv_sems, after)
    return list(outs[:n]), list(outs[n:])


N_SENDERS = 7


def _direct_copies(g_refs, land_refs, send_sems, recv_sems):
    x, y, c, me, chips, chip_idx = _place()
    cps = []
    for b, (g, land) in enumerate(zip(g_refs, land_refs)):
        rows, base = g.shape[1], N_SENDERS * b
        cps.append(pltpu.make_async_remote_copy(
            src_ref=g.at[me, _half(1 - c, rows)], dst_ref=land.at[0], send_sem=send_sems.at[base],
            recv_sem=recv_sems.at[base], device_id=(x, y, 1 - c), device_id_type=_MESH))
        for j in range(3):
            for core in range(2):
                cps.append(pltpu.make_async_remote_copy(
                    src_ref=g.at[chip_idx[j], _half(core, rows)], dst_ref=land.at[1 + 2 * j + c],
                    send_sem=send_sems.at[base + 1 + 2 * j + core], recv_sem=recv_sems.at[base + 1 + 2 * j + c],
                    device_id=(*chips[j], core), device_id_type=_MESH))
    return cps


def direct_start(gs, tag):
    n = len(gs)
    lands = [lax.empty((N_SENDERS, g.shape[1] // 2, g.shape[2]), g.dtype) for g in gs]

    def body(*refs):
        for cp in _direct_copies(refs[:n], refs[n:2 * n], refs[2 * n], refs[2 * n + 1]):
            cp.start()
        token = refs[-1]
        token[...] = jnp.zeros_like(token)

    hbm = [pltpu.HBM(a.shape, a.dtype) for a in (*gs, *lands)]
    outs = pl.pallas_call(
        body, name=f"direct_start_{tag}",
        out_shape=(pltpu.SemaphoreType.DMA((N_SENDERS * n,)), pltpu.SemaphoreType.DMA((N_SENDERS * n,)), *hbm,
                   jax.ShapeDtypeStruct((8, 128), f32)),
        in_specs=[_HBM] * (2 * n), out_specs=(_SEM, _SEM, *[_HBM] * (2 * n), _VMEM),
        input_output_aliases={i: 2 + i for i in range(2 * n)},
        compiler_params=pltpu.CompilerParams(has_side_effects=_EFFECT),
    )(*[pltpu.with_memory_space_constraint(a, pltpu.HBM) for a in (*gs, *lands)])
    return outs[0], outs[1], list(outs[2:2 + n]), list(outs[2 + n:2 + 2 * n]), outs[-1]


def direct_wait(send_sems, recv_sems, gs, lands, after, tag):
    n = len(gs)

    def body(*refs):
        g_refs, land_refs, sends, recvs = refs[:n], refs[n:2 * n], refs[2 * n], refs[2 * n + 1]
        for b in range(n):
            for k in range(N_SENDERS):
                cp = pltpu.make_async_remote_copy(
                    src_ref=g_refs[b].at[0, _half(0, g_refs[b].shape[1])], dst_ref=land_refs[b].at[k],
                    send_sem=sends.at[N_SENDERS * b + k], recv_sem=recvs.at[N_SENDERS * b + k],
                    device_id=_place()[:3], device_id_type=_MESH)
                cp.wait_send()
                cp.wait_recv()

    outs = pl.pallas_call(
        body, name=f"direct_wait_{tag}",
        out_shape=[pltpu.HBM(a.shape, a.dtype) for a in (*gs, *lands)],
        in_specs=[_HBM] * (2 * n) + [_SEM, _SEM, _ANY], out_specs=[_HBM] * (2 * n),
        input_output_aliases={i: i for i in range(2 * n)},
        compiler_params=pltpu.CompilerParams(has_side_effects=_EFFECT),
    )(*gs, *lands, send_sems, recv_sems, after)
    return list(outs[:n]), list(outs[n:])


def sum_senders(g, lands, place_idx, name):
    hr, cols = lands.shape[1], lands.shape[2]
    tr = _step_rows(hr)
    steps = hr // tr

    def body(idx_ref, g_ref, land_ref, o_ref):
        acc = g_ref[...].astype(f32)
        for k in range(N_SENDERS):
            acc = acc + land_ref[k].astype(f32)
        o_ref[...] = acc

    return pl.pallas_call(
        body, name=name,
        grid_spec=pltpu.PrefetchScalarGridSpec(
            num_scalar_prefetch=1, grid=(steps,),
            in_specs=[pl.BlockSpec((None, tr, cols), lambda i, idx: (idx[0], idx[1] * steps + i, 0)),
                      pl.BlockSpec((N_SENDERS, tr, cols), lambda i, idx: (0, i, 0))],
            out_specs=pl.BlockSpec((tr, cols), lambda i, idx: (idx[1] * steps + i, 0))),
        out_shape=jax.ShapeDtypeStruct((2 * hr, cols), f32),
        compiler_params=_cparams(("parallel",)),
    )(place_idx, g, lands)


class GradReducer:
    def __init__(self, c_idx, place_idx):
        self.c_idx, self.place_idx = c_idx, place_idx

    def start(self, bufs, tag, direct=False):
        if direct:
            send_sems, recv_sems, gs, lands, token = direct_start(bufs, tag)
            return (True, send_sems, recv_sems, gs, lands), token
        got = exchange_halves(bufs, tag)
        ts = [add_halves(b, g, self.c_idx, f"add_halves_{tag}{i}") for i, (b, g) in enumerate(zip(bufs, got))]
        send_sems, recv_sems, ts, lands, token = scatter_start(ts, tag)
        return (False, send_sems, recv_sems, ts, lands), token

    def finish(self, state, after, tag):
        direct, *flight = state
        if direct:
            gs, lands = direct_wait(*flight, after, tag)
            sums = [sum_senders(g, l, self.place_idx, f"sum_senders_{tag}{i}") for i, (g, l) in enumerate(zip(gs, lands))]
        else:
            ts, lands = scatter_wait(*flight, after, tag)
            sums = [sum_chips(t, l, self.place_idx, f"sum_chips_{tag}{i}") for i, (t, l) in enumerate(zip(ts, lands))]
        return share_start(sums, tag)

    def collect(self, pending, after, tag):
        return share_wait(*pending, after, tag)


def allreduce_small(sp):
    rows = sp.shape[0]
    hr = rows // 2

    def body(s_ref, out_ref, sib_ref, chip_ref, four_ref, send_sems, recv_sems):
        x, y, c, me, chips, chip_idx = _place()
        sibling = (x, y, 1 - c)
        mine = pl.ds(pl.multiple_of(c * hr, 8), hr)
        other = pl.ds(pl.multiple_of((1 - c) * hr, 8), hr)

        swap = pltpu.make_async_remote_copy(src_ref=s_ref, dst_ref=sib_ref, send_sem=send_sems.at[0],
                                            recv_sem=recv_sems.at[0], device_id=sibling, device_id_type=_MESH)
        swap.start()
        swap.wait()
        is_core0 = c == 0
        chip_ref[...] = jnp.where(is_core0, s_ref[...], sib_ref[...]) + jnp.where(is_core0, sib_ref[...], s_ref[...])

        sends = [pltpu.make_async_remote_copy(
            src_ref=chip_ref.at[mine], dst_ref=four_ref.at[me], send_sem=send_sems.at[1 + j],
            recv_sem=recv_sems.at[1 + j], device_id=(*chips[j], c), device_id_type=_MESH) for j in range(3)]
        for cp in sends:
            cp.start()
        four_ref[me] = chip_ref[mine, :]
        for j in range(3):
            pltpu.make_async_remote_copy(
                src_ref=chip_ref.at[mine], dst_ref=four_ref.at[chip_idx[j]], send_sem=send_sems.at[1 + j],
                recv_sem=recv_sems.at[1 + j], device_id=(*chips[j], c), device_id_type=_MESH).wait_recv()
        for cp in sends:
            cp.wait_send()
        out_ref[mine, :] = (four_ref[0] + four_ref[1]) + (four_ref[2] + four_ref[3])

        share = pltpu.make_async_remote_copy(src_ref=out_ref.at[mine], dst_ref=out_ref.at[mine], send_sem=send_sems.at[4],
                                             recv_sem=recv_sems.at[4], device_id=sibling, device_id_type=_MESH)
        share.start()
        pltpu.make_async_remote_copy(src_ref=out_ref.at[mine], dst_ref=out_ref.at[other], send_sem=send_sems.at[4],
                                     recv_sem=recv_sems.at[4], device_id=sibling, device_id_type=_MESH).wait_recv()
        share.wait_send()

    return pl.pallas_call(
        body, name="allreduce_small",
        out_shape=jax.ShapeDtypeStruct(sp.shape, sp.dtype),
        in_specs=[_VMEM], out_specs=_VMEM,
        scratch_shapes=[pltpu.VMEM(sp.shape, sp.dtype), pltpu.VMEM(sp.shape, sp.dtype),
                        pltpu.VMEM((N_CHIPS, hr, sp.shape[1]), sp.dtype),
                        pltpu.SemaphoreType.DMA((5,)), pltpu.SemaphoreType.DMA((5,))],
        compiler_params=_cparams(),
    )(sp)


def _n_rows(shape):
    n = 1
    for d in shape:
        n *= d
    return 8 * (-(-n // 8192))


def _pack(arrays, total_rows):
    parts = []
    for a in arrays:
        flat = a.reshape(-1)
        parts.append(jnp.pad(flat, (0, 1024 * _n_rows(a.shape) - flat.shape[0])).reshape(-1, 1024))
    rows = jnp.concatenate(parts, axis=0)
    return jnp.pad(rows, ((0, total_rows - rows.shape[0]), (0, 0)))


def _unpack(packed, shapes):
    out, r = [], 0
    for shp in shapes:
        n = 1
        for d in shp:
            n *= d
        nr = _n_rows(shp)
        out.append(packed[r:r + nr].reshape(-1)[:n].reshape(shp))
        r += nr
    return out


_COLUMN_SHARDED = ("w_in_even", "w_qkv")
IN_SHARD, IN_PAD = 1284, 1408
QKV_SHARD, QKV_PAD = 320, 384


def _lane_padded(a, cols):
    return jnp.pad(a, ((0, 0), (0, cols - a.shape[1])))


_SMALL_SHAPES = (
    ("norm_mix_g", (2, 1024)), ("norm_mlp_g", (2, 1024)), ("final_norm_g", (1024,)), ("gm_ln_g", (1, 1024)),
    ("gm_ln_b", (1, 1024)), ("gm_w_s", (1, 8, 128, 128)), ("gm_b_s", (1, 8, 128)), ("ssm_conv_b", (1, 2048)),
    ("ssm_dt_bias", (1, 16)), ("ssm_a_log", (1, 16)), ("ssm_d", (1, 16)), ("ssm_norm_g", (1, 1024)),
    ("attn_sinks", (1, 16)), ("ssm_conv_w", (1, 4, 2048)), ("b_qkv", (1, 1280)), ("b_o", (1, 1024)),
)
_N_REPLICATED = 13
_SHARDED_SMALL = (("ssm_conv_w", 2, 512), ("b_qkv", 1, 320), ("b_o", 1, 256))
_SHARD_PACK_ROWS = 32


def _cols_by_owner(a):
    return a.transpose(1, 0, 2).reshape(a.shape[1], -1)


class WeightGatherer:
    def __init__(self, w, chip_idx):
        def place(tag, b, dtype=bf16, after=None, cols=None):
            return place_shard(b, chip_idx, f"place_shard_{tag}", dtype, after, cols)

        sems_in, bufs_in, self.started = gather_start([
            [place("in", w["w_in_even"][0].astype(bf16), cols=IN_PAD),
             place("small", _pack([w[n] for n, _, _ in _SHARDED_SMALL], _SHARD_PACK_ROWS), f32)]], "in")
        t = self.started
        sems, bufs, self.all_started = gather_start([
            [place("out", w["w_out_even"][0], after=t), place("up0", w["w_up"][0], after=t),
             place("down0", w["w_down"][0], after=t)],
            [place("qkv", w["w_qkv"][0], after=t, cols=QKV_PAD), place("o", w["w_o"][0], after=t),
             place("up1", w["w_up"][1], after=t), place("down1", w["w_down"][1], after=t)],
        ], "rest")
        self.sems, self.bufs = sems_in + sems, bufs_in + bufs

    def _group(self, gi, after, tag):
        return gather_forward(gather_wait(self.bufs[gi], self.sems[gi], after, tag), tag)

    def mixer_in(self, after):
        g, small = self._group(0, [after, self.all_started], "in")
        shard_shapes = [tuple(width if i == axis else d for i, d in enumerate(dict(_SMALL_SHAPES)[n]))
                        for n, axis, width in _SHARDED_SMALL]
        per_chip = [_unpack(small[s], shard_shapes) for s in range(N_CHIPS)]
        full = {n: jnp.concatenate([per_chip[s][i] for s in range(N_CHIPS)], axis=axis)
                for i, (n, axis, _) in enumerate(_SHARDED_SMALL)}
        w_in_p = jnp.concatenate([g[s, :, :IN_SHARD] for s in range(N_CHIPS)]
                                 + [jnp.zeros((g.shape[1], NP_IN - IN_EVEN), g.dtype)], axis=1)
        return w_in_p, full

    def layer0(self, after):
        w_out, w_up, w_down = self._group(1, [after], "l0")
        return w_out.reshape(2048, 1024), w_up, w_down.reshape(4096, 1024)

    def layer1_start(self, after):
        return forward_start(gather_wait(self.bufs[2], self.sems[2], [after], "l1"), "l1")

    def layer1(self, pending, after):
        q, w_o, w_up, w_down = forward_wait(*pending, after, "l1")
        w_qkv = jnp.concatenate([q[s, :, :QKV_SHARD] for s in range(N_CHIPS)], axis=1)
        return w_qkv, w_o.reshape(1024, 1024), w_up, w_down.reshape(4096, 1024)


def _row2(v):
    return v.reshape(1, -1)


def _lane_pad(v):
    return jnp.pad(v, ((0, 0), (0, CH - v.shape[1])))


_H_AND_NORM = (("tile", f32), ("tile", bf16))
_DX_AND_DG = (("tile", f32), ("sum", D_MODEL))


def _mlp_bwd(dh_out, h, g_row, y, a, w_up, w_down, tag, after=None):
    da = matmul(dh_out, w_down, dims="nt", name=f"mlp_da{tag}", out_dtype=bf16, tm=2048, tn=1024,
                epi=_times_relu2_grad, epi_args=(("tile", a),), after=after)
    dw_down = matmul(a, dh_out, dims="tn", name=f"mlp_dwdown{tag}", out_dtype=bf16, a_pro=_relu2)
    dw_up = matmul(y, da, dims="tn", name=f"mlp_dwup{tag}", out_dtype=bf16, tn=1024, out_by_col_tile=True)
    dh, dg, dh_colsum = matmul_rows(da, w_up, dims="nt", name=f"mlp_dy{tag}", epi=_norm_bwd_res_colsum,
                                    epi_args=(("tile", h), ("row", g_row), ("tile", dh_out)),
                                    outs=_DX_AND_DG + (("sum", D_MODEL),))
    return dh, dg, dw_up, dw_down, dh_colsum


def _by_owner(a):
    return a.reshape(N_CHIPS, a.shape[0] // N_CHIPS, a.shape[1])


def _row_shards(a, shard, padded):
    return jnp.stack([jnp.pad(a[shard * s: shard * (s + 1)], ((0, padded - shard), (0, 0))) for s in range(N_CHIPS)])


def _col_shards(a, shard, padded):
    return jnp.stack([_lane_padded(a[:, shard * s: shard * (s + 1)], padded) for s in range(N_CHIPS)])


def _local_step(x, target, weights, sm, reducer):
    w_up, w_down = [None, None], [None, None]
    mix_g = [_row2(sm["norm_mix_g"][i]) for i in range(2)]
    y0 = rmsnorm_fwd(x, mix_g[0] + weights.started[:1, :1], "mix_norm0")
    w_in_p, sharded_small = weights.mixer_in(y0)
    sm = {**sm, **sharded_small}
    mlp_g = [_row2(sm["norm_mlp_g"][i]) for i in range(2)]
    mixer_prm = {
        "ln_g": sm["gm_ln_g"], "ln_b": sm["gm_ln_b"], "wm": sm["gm_w_s"][0],
        "bs_t": jnp.pad(sm["gm_b_s"][0].T, ((0, 0), (0, CH - N_BLK))),
        "conv_w": jnp.pad(sm["ssm_conv_w"][0], ((0, 4), (0, 0))), "conv_b": sm["ssm_conv_b"],
        "dt_bias": _lane_pad(sm["ssm_dt_bias"]), "a_log": _lane_pad(sm["ssm_a_log"]),
        "d_heads": _lane_pad(sm["ssm_d"]), "norm_g": sm["ssm_norm_g"],
    }
    sink_row = _lane_pad(sm["attn_sinks"])

    proj = matmul(y0, w_in_p, dims="nn", name="in_proj", tm=2048, tn=768)
    ab, hstates = mixer_fwd(proj, mixer_prm)
    w_out, w_up[0], w_down[0] = weights.layer0(ab)
    h1, y1 = matmul_rows(ab, w_out, dims="nn", name="out_proj", epi=_res_norm,
                         epi_args=(("tile", x), ("row", mlp_g[0])), outs=_H_AND_NORM)
    a1 = matmul(y1, w_up[0], dims="nn", name="mlp_up0", out_dtype=bf16, tm=2048, tn=1024)
    pending_l1, forwarding_l1 = weights.layer1_start(a1)
    h2, y2 = matmul_rows(a1, w_down[0], dims="nn", name="mlp_down0", a_pro=_relu2, epi=_res_norm,
                         epi_args=(("tile", h1), ("row", mix_g[1])), outs=_H_AND_NORM, after=forwarding_l1)
    w_qkv, w_o, w_up[1], w_down[1] = weights.layer1(pending_l1, h2)
    qkv = matmul(y2, w_qkv, dims="nn", name="qkv_proj", tn=QKV_DIM, epi=_add_bias, epi_args=(("row", sm["b_qkv"]),))
    att = attn_fwd(qkv, sink_row)
    h3, y3 = matmul_rows(att, w_o, dims="nn", name="o_proj", epi=_bias_res_norm,
                         epi_args=(("row", sm["b_o"]), ("tile", h2), ("row", mlp_g[1])), outs=_H_AND_NORM)
    a3 = matmul(y3, w_up[1], dims="nn", name="mlp_up1", out_dtype=bf16, tm=2048, tn=1024)
    dh4, dg_final, loss = matmul_rows(
        a3, w_down[1], dims="nn", name="mlp_down1", a_pro=_relu2, epi=_res_norm_loss,
        epi_args=(("tile", h3), ("row", _row2(sm["final_norm_g"])), ("tile", target)),
        outs=(("tile", f32), ("sum", D_MODEL), ("sum", 128)))

    dh3, dg_mlp1, dw_up1, dw_down1, db_o = _mlp_bwd(dh4, h3, mlp_g[1], y3, a3, w_up[1], w_down[1], 1)
    datt = matmul(dh3, w_o, dims="nt", name="attn_dout", out_dtype=bf16, tm=2048)
    dw_o = matmul(att, dh3, dims="tn", name="dw_o", out_dtype=bf16)
    dqkv, dsink, db_qkv = attn_bwd(qkv, sink_row, datt)
    dw_qkv = matmul(y2, dqkv, dims="tn", name="dw_qkv", out_dtype=bf16, tn=QKV_DIM)
    dh2, dg_mix1 = matmul_rows(dqkv, w_qkv, dims="nt", name="dy_qkv", epi=_norm_bwd_res,
                               epi_args=(("tile", h2), ("row", mix_g[1]), ("tile", dh3)), outs=_DX_AND_DG)
    layer1 = [_by_owner(dw_o), dw_up1, _by_owner(dw_down1), _col_shards(dw_qkv, QKV_SHARD, QKV_PAD)]
    flight1, token1 = reducer.start(layer1, "l1", direct=True)
    dh1, dg_mlp0, dw_up0, dw_down0, _ = _mlp_bwd(dh2, h1, mlp_g[0], y1, a1, w_up[0], w_down[0], 0, after=token1)
    pending1, shared1 = reducer.finish(flight1, dh1, "l1")
    dw_out = matmul(ab, dh1, dims="tn", name="dw_out", out_dtype=bf16, after=shared1)
    flight0, token0 = reducer.start([dw_up0, _by_owner(dw_down0), _by_owner(dw_out)], "l0", direct=True)
    dab = matmul(dh1, w_out, dims="nt", name="mixer_dout", tm=2048, tn=1024, after=token0)
    dproj, dmix = mixer_bwd(proj, hstates, dab, mixer_prm)
    dw_in_t = matmul(dproj, y0, dims="tn", name="dw_in", out_dtype=bf16, tm=768, tn=1024)
    pending0, shared0 = reducer.finish(flight0, dw_in_t, "l0")
    flight_in, token_in = reducer.start([_row_shards(dw_in_t, IN_SHARD, IN_PAD)], "in")
    dx, dg_mix0 = matmul_rows(dproj, w_in_p, dims="nt", name="dy_in", tm=256, epi=_norm_bwd_res,
                              epi_args=(("tile", x), ("row", mix_g[0]), ("tile", dh1)), outs=_DX_AND_DG,
                              after=token_in + shared0)
    pending_in, _ = reducer.finish(flight_in, dx, "in")
    r_o, r_up1, r_down1, r_qkv = reducer.collect(pending1, dx, "l1")
    r_up0, r_down0, r_out = reducer.collect(pending0, dx, "l0")
    (r_in,) = reducer.collect(pending_in, dx, "in")
    reduced = {
        "w_out_even": [(r_out, 0)], "w_o": [(r_o, 0)], "w_up": [(r_up0, 0), (r_up1, 0)],
        "w_down": [(r_down0, 0), (r_down1, 0)],
        "w_in_even": r_in[:IN_SHARD].T[None], "w_qkv": r_qkv[None, :, :QKV_SHARD],
    }

    small_grads = {
        "norm_mix_g": jnp.concatenate([dg_mix0, dg_mix1], axis=0),
        "norm_mlp_g": jnp.concatenate([dg_mlp0, dg_mlp1], axis=0),
        "final_norm_g": dg_final[0], "gm_ln_g": dmix["ln_g"], "gm_ln_b": dmix["ln_b"],
        "gm_w_s": dmix["wm"][None], "gm_b_s": dmix["bs_t"][:, :N_BLK].T[None],
        "ssm_conv_b": dmix["conv_b"], "ssm_dt_bias": dmix["dt_bias"][:, :SSM_HEADS],
        "ssm_a_log": dmix["a_log"][:, :SSM_HEADS], "ssm_d": dmix["d_heads"][:, :SSM_HEADS],
        "ssm_norm_g": dmix["norm_g"], "attn_sinks": dsink[:, :SSM_HEADS],
        "ssm_conv_w": dmix["conv_w"][None, :4], "b_qkv": db_qkv, "b_o": db_o,
    }
    return loss, dx, reduced, small_grads


def kernel(x, norm_mix_g, norm_mlp_g, final_norm_g, w_in_even, w_out_even, gm_ln_g, gm_ln_b, gm_w_s, gm_b_s, ssm_conv_w, ssm_conv_b, ssm_dt_bias, ssm_a_log, ssm_d, ssm_norm_g, w_qkv, b_qkv, w_o, b_o, attn_sinks, w_up, w_down, loss_target, m_norm_mix_g, m_norm_mlp_g, m_final_norm_g, m_w_in_even, m_w_out_even, m_gm_ln_g, m_gm_ln_b, m_gm_w_s, m_gm_b_s, m_ssm_conv_w, m_ssm_conv_b, m_ssm_dt_bias, m_ssm_a_log, m_ssm_d, m_ssm_norm_g, m_w_qkv, m_b_qkv, m_w_o, m_b_o, m_attn_sinks, m_w_up, m_w_down, v_norm_mix_g, v_norm_mlp_g, v_final_norm_g, v_w_in_even, v_w_out_even, v_gm_ln_g, v_gm_ln_b, v_gm_w_s, v_gm_b_s, v_ssm_conv_w, v_ssm_conv_b, v_ssm_dt_bias, v_ssm_a_log, v_ssm_d, v_ssm_norm_g, v_w_qkv, v_b_qkv, v_w_o, v_b_o, v_attn_sinks, v_w_up, v_w_down):
    w = dict(norm_mix_g=norm_mix_g, norm_mlp_g=norm_mlp_g, final_norm_g=final_norm_g, w_in_even=w_in_even,
             w_out_even=w_out_even, gm_ln_g=gm_ln_g, gm_ln_b=gm_ln_b, gm_w_s=gm_w_s, gm_b_s=gm_b_s,
             ssm_conv_w=ssm_conv_w, ssm_conv_b=ssm_conv_b, ssm_dt_bias=ssm_dt_bias, ssm_a_log=ssm_a_log,
             ssm_d=ssm_d, ssm_norm_g=ssm_norm_g, w_qkv=w_qkv, b_qkv=b_qkv, w_o=w_o, b_o=b_o,
             attn_sinks=attn_sinks, w_up=w_up, w_down=w_down)
    m = dict(norm_mix_g=m_norm_mix_g, norm_mlp_g=m_norm_mlp_g, final_norm_g=m_final_norm_g,
             w_in_even=m_w_in_even, w_out_even=m_w_out_even, gm_ln_g=m_gm_ln_g, gm_ln_b=m_gm_ln_b,
             gm_w_s=m_gm_w_s, gm_b_s=m_gm_b_s, ssm_conv_w=m_ssm_conv_w, ssm_conv_b=m_ssm_conv_b,
             ssm_dt_bias=m_ssm_dt_bias, ssm_a_log=m_ssm_a_log, ssm_d=m_ssm_d, ssm_norm_g=m_ssm_norm_g,
             w_qkv=m_w_qkv, b_qkv=m_b_qkv, w_o=m_w_o, b_o=m_b_o, attn_sinks=m_attn_sinks, w_up=m_w_up,
             w_down=m_w_down)
    v = dict(norm_mix_g=v_norm_mix_g, norm_mlp_g=v_norm_mlp_g, final_norm_g=v_final_norm_g,
             w_in_even=v_w_in_even, w_out_even=v_w_out_even, gm_ln_g=v_gm_ln_g, gm_ln_b=v_gm_ln_b,
             gm_w_s=v_gm_w_s, gm_b_s=v_gm_b_s, ssm_conv_w=v_ssm_conv_w, ssm_conv_b=v_ssm_conv_b,
             ssm_dt_bias=v_ssm_dt_bias, ssm_a_log=v_ssm_a_log, ssm_d=v_ssm_d, ssm_norm_g=v_ssm_norm_g,
             w_qkv=v_w_qkv, b_qkv=v_b_qkv, w_o=v_w_o, b_o=v_b_o, attn_sinks=v_attn_sinks, w_up=v_w_up,
             w_down=v_w_down)
    names = ("norm_mix_g", "norm_mlp_g", "final_norm_g", "w_in_even", "w_out_even", "gm_ln_g", "gm_ln_b",
             "gm_w_s", "gm_b_s", "ssm_conv_w", "ssm_conv_b", "ssm_dt_bias", "ssm_a_log", "ssm_d", "ssm_norm_g",
             "w_qkv", "b_qkv", "w_o", "b_o", "attn_sinks", "w_up", "w_down")

    cx, cy, cc = lax.axis_index("x"), lax.axis_index("y"), lax.axis_index("c")
    chip = 2 * cx + cy
    c_idx = jnp.reshape(cc, (1,)).astype(jnp.int32)
    chip_idx = jnp.reshape(chip, (1,)).astype(jnp.int32)

    weights = WeightGatherer(w, chip_idx)
    sm = {n: w[n] for n, _ in _SMALL_SHAPES[:_N_REPLICATED]}

    reducer = GradReducer(c_idx, jnp.concatenate([chip_idx, c_idx]))
    loss_part, dx, grads, small_grads = _local_step(x[0], loss_target[0], weights, sm, reducer)

    small_sum = allreduce_small(_pack([small_grads[n] for n, _ in _SMALL_SHAPES] + [loss_part], SMALL_ROWS))
    *small_list, loss_row = _unpack(small_sum, [s for _, s in _SMALL_SHAPES] + [loss_part.shape])
    loss = loss_row[0, 0]
    small_full = dict(zip([n for n, _ in _SMALL_SHAPES], small_list))
    for n, _ in _SMALL_SHAPES[:_N_REPLICATED]:
        grads[n] = small_full[n]
    for n, axis, width in _SHARDED_SMALL:
        grads[n] = lax.dynamic_slice_in_dim(small_full[n], chip * width, width, axis)

    delta, new_m, new_v = {}, {}, {}
    for n in names:
        if isinstance(grads[n], list):
            outs = None
            for layer, (buf, row_off) in enumerate(grads[n]):
                outs = adamw_rows(w[n], buf, m[n], v[n], layer, row_off, f"adamw_{n}{layer}", into=outs)
            grads[n], delta[n], new_m[n], new_v[n] = outs
            continue
        grads[n] = grads[n].reshape(w[n].shape)
        if n in _COLUMN_SHARDED:
            args = [jnp.transpose(d[n], (2, 0, 1)) for d in (w, grads, m, v)]
            grads[n] = jnp.transpose(args[1], (1, 2, 0))
            outs = adamw(*args, f"adamw_{n}")
            delta[n], new_m[n], new_v[n] = (jnp.transpose(o, (1, 2, 0)) for o in outs)
            continue
        shape = (1,) + w[n].shape if w[n].ndim == 1 else w[n].shape
        outs = adamw(*[d[n].reshape(shape) for d in (w, grads, m, v)], f"adamw_{n}")
        delta[n], new_m[n], new_v[n] = (o.reshape(w[n].shape) for o in outs)

    return (loss, dx[None], *[grads[n] for n in names], *[delta[n] for n in names],
            *[new_m[n] for n in names], *[new_v[n] for n in names])
```

```python
import functools

import jax
import jax.numpy as jnp
from jax import lax
from jax.experimental import pallas as pl
from jax.experimental.pallas import tpu as pltpu

f32 = jnp.float32
bf16 = jnp.bfloat16
MXU_DTYPE = bf16

RMS_EPS = 1e-5
LN_EPS = 1e-5
D_MODEL = 1024
D_FF = 4096
CH = 128
N_BLK = 8
SSM_HEADS = 16
IN_EVEN = 5136
NP_IN = 5376
OFF_U, OFF_V, OFF_Z, OFF_X, OFF_DT = 0, 1024, 2048, 3072, 5120
XBC_BLKS = 16
QKV_DIM = 1280
ATT_SCALE = 64 ** -0.5

ADAM_LR = 0.001
ADAM_B1 = 0.9
ADAM_B2 = 0.999
ADAM_EPS = 1e-08
ADAM_WD = 0.01
ADAM_STEP = 10

VMEM_LIMIT_BYTES = 48 * 1024 * 1024
N_CHIPS = 4
SMALL_ROWS = 256

NN = ((1,), (0,))
NT = ((1,), (1,))
TN = ((0,), (0,))


def _mm(a, b, dims):
    return lax.dot_general(a.astype(MXU_DTYPE), b.astype(MXU_DTYPE), (dims, ((), ())),
                           preferred_element_type=f32)


def _mm_exact(a, b):
    return jnp.dot(a, b, preferred_element_type=f32, precision=lax.Precision.HIGHEST)


def _cparams(sem=None):
    return pltpu.CompilerParams(dimension_semantics=sem, vmem_limit_bytes=VMEM_LIMIT_BYTES)


@jax.custom_vjp
def _swap64(x):
    return pltpu.roll(x, 64, axis=1)


_swap64.defvjp(lambda x: (pltpu.roll(x, 64, axis=1), None), lambda _, g: (pltpu.roll(g, 64, axis=1),))


def _row_blocks_of(x):
    return tuple(x[i:i + CH] for i in range(0, x.shape[0], CH))


@jax.custom_vjp
def _row_blocks(x):
    return _row_blocks_of(x)


_row_blocks.defvjp(lambda x: (_row_blocks_of(x), None), lambda _, gs: (jnp.concatenate(gs, axis=0),))


def _make_delay(k):
    @jax.custom_vjp
    def delay(ext):
        return pltpu.roll(ext, k, axis=0)[8:, :]

    def fwd(ext):
        return delay(ext), None

    def bwd(_, g):
        gp = jnp.concatenate([jnp.zeros((8, g.shape[1]), g.dtype), g], axis=0)
        return (pltpu.roll(gp, gp.shape[0] - k, axis=0),)

    delay.defvjp(fwd, bwd)
    return delay


_DELAYS = {k: _make_delay(k) for k in (1, 2, 3)}


_GELU_C = 0.7978845608028654
_GELU_K = 0.044715


@jax.custom_vjp
def _gelu(x):
    return 0.5 * x * (1.0 + jnp.tanh(_GELU_C * (x + _GELU_K * (x * x * x))))


def _gelu_fwd(x):
    t = jnp.tanh(_GELU_C * (x + _GELU_K * (x * x * x)))
    return 0.5 * x * (1.0 + t), (x, t)


def _gelu_bwd(res, g):
    x, t = res
    dz = _GELU_C + (3.0 * _GELU_C * _GELU_K) * (x * x)
    return (g * (0.5 * (1.0 + t) + (0.5 * x) * (1.0 - t * t) * dz),)


_gelu.defvjp(_gelu_fwd, _gelu_bwd)


def _col(m, lane, h):
    return jnp.sum(jnp.where(lane == h, m, 0.0), axis=1, keepdims=True)


@functools.lru_cache(maxsize=None)
def _row_picker(h, shape):
    @jax.custom_vjp
    def pick(m):
        return m[h:h + 1, :]

    def bwd(_, g):
        return (jnp.where(lax.broadcasted_iota(jnp.int32, shape, 0) == h, g, 0.0),)

    pick.defvjp(lambda m: (m[h:h + 1, :], None), bwd)
    return pick


def _row(m, sub, h):
    return _row_picker(h, m.shape)(m)


def _mixer_chunk(us, vs, zs, xbcs, halos, dtblk, hps, prm):
    lane = lax.broadcasted_iota(jnp.int32, (CH, CH), 1)
    sub = lax.broadcasted_iota(jnp.int32, (CH, CH), 0)
    left = lane < 64
    top = sub < 64
    causal = sub >= lane

    gus = [_gelu(u) for u in us]
    gvs = [_gelu(v) for v in vs]
    mu = sum(jnp.sum(g, axis=1, keepdims=True) for g in gvs) / D_MODEL
    cen = [g - mu for g in gvs]
    var = sum(jnp.sum(c * c, axis=1, keepdims=True) for c in cen) / D_MODEL
    rstd = lax.rsqrt(var + LN_EPS)
    a_out = []
    for g in range(N_BLK):
        vn = cen[g] * rstd * prm["ln_g"][g] + prm["ln_b"][g]
        w = jnp.where(causal, prm["wm"][g], 0.0)
        mixed = _mm(w, vn, NN) + _col(prm["bs_t"], lane, g)
        a_out.append(gus[g] * mixed)

    act = []
    for b in range(XBC_BLKS):
        w8 = prm["conv_w"][b]
        sub8 = lax.broadcasted_iota(jnp.int32, w8.shape, 0)
        ext = jnp.concatenate([halos[b], xbcs[b]], axis=0)
        conv = xbcs[b] * _row(w8, sub8, 3) + prm["conv_b"][b]
        for k in (1, 2, 3):
            conv = conv + _DELAYS[k](ext) * _row(w8, sub8, 3 - k)
        act.append(jax.nn.silu(conv))

    dt = jax.nn.softplus(dtblk + prm["dt_bias"])
    a_neg = -jnp.exp(prm["a_log"])
    tri = causal.astype(f32)
    acum = _mm_exact(tri, dt * a_neg)
    acum_t = acum.T
    dt_t = dt.T
    last = sub == CH - 1
    ys, h_out = [], []
    for grp in range(4):
        bm = act[8 + grp]
        cm = act[12 + grp]
        cb = _mm(cm, bm, NT)
        for p in (2 * grp, 2 * grp + 1):
            h0, h1 = 2 * p, 2 * p + 1
            xp = act[p]
            hp = hps[p]
            wis = []
            for h in (h0, h1):
                seg = _col(acum, lane, h) - _row(acum_t, sub, h)
                decay = jnp.exp(jnp.where(causal, seg, -jnp.inf))
                wis.append(cb * decay * _row(dt_t, sub, h))
            wcat = jnp.concatenate(wis, axis=1)
            xbd = jnp.concatenate([jnp.where(left, xp, 0.0), jnp.where(left, 0.0, xp)], axis=0)
            y_diag = _mm(wcat, xbd, NN)
            a_end = [jnp.sum(jnp.where(last & (lane == h), acum, 0.0), keepdims=True) for h in (h0, h1)]
            a_col = jnp.where(left, _col(acum, lane, h0), _col(acum, lane, h1))
            dt_col = jnp.where(left, _col(dt, lane, h0), _col(dt, lane, h1))
            to_end = jnp.exp(jnp.where(left, a_end[0], a_end[1]) - a_col) * dt_col
            states = _mm(xp * to_end, bm, TN)
            chunk_decay = jnp.where(top, jnp.exp(a_end[0]), jnp.exp(a_end[1]))
            h_out.append(chunk_decay * hp + states)
            y_off = jnp.exp(a_col) * _mm(cm, hp, NT)
            d_skip = jnp.where(left[:1], _col(prm["d_heads"], lane[:1], h0), _col(prm["d_heads"], lane[:1], h1))
            ys.append((y_diag + y_off + xp * d_skip) * jax.nn.silu(zs[p]))

    b_out = []
    for grp in range(4):
        pair = (ys[2 * grp], ys[2 * grp + 1])
        ms = sum(jnp.sum(y * y, axis=1, keepdims=True) for y in pair) / 256.0
        r = lax.rsqrt(ms + RMS_EPS)
        for j, y in enumerate(pair):
            b_out.append(y * r * prm["norm_g"][2 * grp + j])
    return a_out, b_out, h_out


def _attn_block(qps, kprev, kcur, vprev, vcur, sink_row, first):
    lane = lax.broadcasted_iota(jnp.int32, (CH, CH), 1)
    left = lane < 64
    own = lane <= lax.broadcasted_iota(jnp.int32, (CH, CH), 0)
    own8 = jnp.concatenate([own] * N_BLK, axis=0)

    def both_halves(a):
        sw = _swap64(a)
        return [jnp.where(left, a, sw), jnp.where(left, sw, a)]

    kc, kp, vc, vp = both_halves(kcur), both_halves(kprev), both_halves(vcur), both_halves(vprev)
    outs = []
    for j in range(2):
        q8 = jnp.concatenate([part for p in range(4 * j, 4 * j + 4)
                              for part in (jnp.where(left, qps[p], 0.0), jnp.where(left, 0.0, qps[p]))], axis=0)
        s_cur = _row_blocks(_mm(q8, kc[j], NT))
        s_prev = _row_blocks(_mm(q8, kp[j], NT))
        probs = []
        for h in range(N_BLK):
            s = jnp.where(own, s_cur[h] * ATT_SCALE, jnp.where(first, -jnp.inf, s_prev[h] * ATT_SCALE))
            sink = _col(sink_row, lane[:1], N_BLK * j + h)
            m = lax.stop_gradient(jnp.maximum(jnp.max(s, axis=1, keepdims=True), sink))
            pexp = jnp.exp(s - m)
            probs.append(pexp / (jnp.sum(pexp, axis=1, keepdims=True) + jnp.exp(sink - m)))
        p8 = jnp.concatenate(probs, axis=0)
        o = _row_blocks(_mm(jnp.where(own8, p8, 0.0), vc[j], NN) + _mm(jnp.where(own8, 0.0, p8), vp[j], NN))
        for t in range(4):
            outs.append(jnp.where(left, o[2 * t], o[2 * t + 1]))
    return outs


def _rmsnorm(x, g):
    r = lax.rsqrt(jnp.mean(x * x, axis=-1, keepdims=True) + RMS_EPS)
    return x * r * g


def rmsnorm_fwd(x, g_row, name):
    s, d = x.shape
    tm = min(512, s)

    def body(x_ref, g_ref, y_ref):
        y_ref[...] = _rmsnorm(x_ref[...], g_ref[...]).astype(bf16)

    return pl.pallas_call(
        body, name=name, grid=(s // tm,),
        in_specs=[pl.BlockSpec((tm, d), lambda i: (i, 0)), pl.BlockSpec((1, d), lambda i: (0, 0))],
        out_specs=pl.BlockSpec((tm, d), lambda i: (i, 0)),
        out_shape=jax.ShapeDtypeStruct((s, d), bf16),
        compiler_params=_cparams(("parallel",)),
    )(x, g_row)


def _fit(dim, want):
    if dim <= want:
        return dim
    t = want
    while dim % t:
        t -= 128
    return t


def matmul(a, b, *, dims, name, out_dtype=f32, tm=1024, tn=512, tk=8192, a_pro=None, epi=None, epi_args=(),
           out_by_col_tile=False, after=None, dest=None):
    if dims == "nn" and b.ndim == 3:
        (m, k), n, tn = a.shape, b.shape[0] * b.shape[2], b.shape[2]
    elif dims == "nn":
        (m, k), n = a.shape, b.shape[1]
    elif dims == "nt":
        (m, k), n = a.shape, b.shape[0]
    else:
        (k, m), n = a.shape, b.shape[1]
    tm, tn, tk = _fit(m, tm), _fit(n, tn), _fit(k, tk)
    nk = k // tk
    if dims == "nn":
        a_spec = pl.BlockSpec((tm, tk), lambda i, j, kk: (i, kk))
        b_spec = (pl.BlockSpec((None, tk, tn), lambda i, j, kk: (j, kk, 0)) if b.ndim == 3
                  else pl.BlockSpec((tk, tn), lambda i, j, kk: (kk, j)))
        dn = NN
    elif dims == "nt":
        a_spec = pl.BlockSpec((tm, tk), lambda i, j, kk: (i, kk))
        b_spec = pl.BlockSpec((tn, tk), lambda i, j, kk: (j, kk))
        dn = NT
    else:
        a_spec = pl.BlockSpec((tk, tm), lambda i, j, kk: (kk, i))
        b_spec = pl.BlockSpec((tk, tn), lambda i, j, kk: (kk, j))
        dn = TN
    e_specs = [pl.BlockSpec((tm, tn), lambda i, j, kk: (i, j)) if kind == "tile"
               else pl.BlockSpec((1, tn), lambda i, j, kk: (0, j)) for kind, _ in epi_args]
    n_epi = len(epi_args)
    order_specs = [] if after is None else [pl.BlockSpec((8, 128), lambda i, j, kk: (0, 0))]
    order_args = [] if after is None else [after]
    into = None if dest is None else dest[3]
    if into is not None:
        order_specs, order_args = order_specs + [_ANY], order_args + [into]

    def body(*refs):
        a_ref, b_ref = refs[0], refs[1]
        e_refs = refs[2:2 + n_epi]
        n_in = 2 + n_epi + len(order_args)
        o_ref = refs[n_in]
        av = a_ref[...]
        if a_pro is not None:
            av = a_pro(av)
        part = _mm(av, b_ref[...], dn)

        def finish(acc):
            if epi is not None:
                acc = epi(acc, *[r[...] for r in e_refs])
            o_ref[...] = acc.astype(out_dtype)

        if nk == 1:
            finish(part)
        else:
            acc_ref = refs[n_in + 1]
            kk = pl.program_id(2)

            @pl.when(kk == 0)
            def _():
                acc_ref[...] = part

            @pl.when(kk > 0)
            def _():
                acc_ref[...] += part

            @pl.when(kk == nk - 1)
            def _():
                finish(acc_ref[...])

    if dest is not None:
        out_spec = pl.BlockSpec(dest[1], lambda i, j, kk: dest[2](i, j))
        out_shape = jax.ShapeDtypeStruct(dest[0], out_dtype)
    elif out_by_col_tile:
        out_spec = pl.BlockSpec((None, tm, tn), lambda i, j, kk: (j, i, 0))
        out_shape = jax.ShapeDtypeStruct((n // tn, m, tn), out_dtype)
    else:
        out_spec = pl.BlockSpec((tm, tn), lambda i, j, kk: (i, j))
        out_shape = jax.ShapeDtypeStruct((m, n), out_dtype)
    return pl.pallas_call(
        body, name=name, grid=(m // tm, n // tn, nk),
        in_specs=[a_spec, b_spec] + e_specs + order_specs,
        out_specs=out_spec,
        out_shape=out_shape,
        scratch_shapes=[pltpu.VMEM((tm, tn), f32)] if nk > 1 else [],
        input_output_aliases={} if into is None else {2 + n_epi + len(order_args) - 1: 0},
        compiler_params=_cparams(("parallel", "parallel", "arbitrary")),
    )(a, b, *[arr for _, arr in epi_args], *order_args)


def _relu2(a):
    r = jnp.maximum(a.astype(f32), 0.0)
    return r * r


def _add(acc, t):
    return acc + t


def _add_bias(acc, t):
    return acc + t


def _add_bias_res(acc, bias, res):
    return acc + bias + res


def _times_relu2_grad(acc, a):
    return acc * (2.0 * jnp.maximum(a.astype(f32), 0.0))


def matmul_rows(a, b, *, dims, name, epi, epi_args, outs, tm=512, a_pro=None, after=None):
    m, k = a.shape
    n = b.shape[-1] if dims == "nn" else b.shape[-2]
    tm = _fit(m, tm)
    dn = NN if dims == "nn" else NT
    e_specs = [pl.BlockSpec((tm, arr.shape[1]), lambda i: (i, 0)) if kind == "tile"
               else pl.BlockSpec((1, arr.shape[1]), lambda i: (0, 0)) for kind, arr in epi_args]
    order_specs = [] if after is None else [pl.BlockSpec((8, 128), lambda i: (0, 0))]
    order_args = [] if after is None else [after]
    n_in = 2 + len(epi_args) + len(order_args)

    def body(*refs):
        av = refs[0][...]
        if a_pro is not None:
            av = a_pro(av)
        if b.ndim == 3:
            kb = b.shape[2]
            acc = sum(_mm(av[:, s * kb:(s + 1) * kb], refs[1][s], dn) for s in range(b.shape[0]))
        else:
            acc = _mm(av, refs[1][...], dn)
        vals = epi(acc, *[r[...] for r in refs[2:2 + len(epi_args)]])
        for (kind, _), o_ref, val in zip(outs, refs[n_in:], vals):
            if kind == "tile":
                o_ref[...] = val.astype(o_ref.dtype)
            else:
                @pl.when(pl.program_id(0) == 0)
                def _():
                    o_ref[...] = jnp.zeros_like(o_ref)

                o_ref[...] += val

    out_specs = [pl.BlockSpec((tm, n), lambda i: (i, 0)) if kind == "tile" else pl.BlockSpec((1, arg), lambda i: (0, 0))
                 for kind, arg in outs]
    out_shape = [jax.ShapeDtypeStruct((m, n), arg) if kind == "tile" else jax.ShapeDtypeStruct((1, arg), f32)
                 for kind, arg in outs]
    return pl.pallas_call(
        body, name=name, grid=(m // tm,),
        in_specs=[pl.BlockSpec((tm, k), lambda i: (i, 0)), pl.BlockSpec(b.shape, lambda i: (0,) * b.ndim)]
                 + e_specs + order_specs,
        out_specs=out_specs, out_shape=out_shape,
        compiler_params=_cparams(("arbitrary",)),
    )(a, b, *[arr for _, arr in epi_args], *order_args)


def _res_norm(acc, res, g):
    h = acc + res
    return h, _rmsnorm(h, g)


def _bias_res_norm(acc, bias, res, g):
    h = acc + bias + res
    return h, _rmsnorm(h, g)


def _res_norm_loss(acc, res, g, target):
    def f(h, gv):
        err = jnp.square(_rmsnorm(h, gv) - target)
        return 0.5 * jnp.sum(jnp.mean(err, axis=-1, keepdims=True), axis=0, keepdims=True)

    loss, vjp = jax.vjp(f, acc + res, g)
    dh, dg = vjp(jnp.ones_like(loss))
    return dh, dg, jnp.broadcast_to(loss, (1, 128))


def _norm_bwd_res_colsum(dy, x, g, res):
    dx, dg = _norm_bwd_res(dy, x, g, res)
    return dx, dg, jnp.sum(dx, axis=0, keepdims=True)


def _norm_bwd_res(dy, x, g, res):
    _, vjp = jax.vjp(_rmsnorm, x, g)
    dx, dg = vjp(dy)
    return res + dx, dg


_MIXER_PARAM_SHAPES = (
    ("ln_g", (1, D_MODEL)), ("ln_b", (1, D_MODEL)), ("wm", (N_BLK, CH, CH)), ("bs_t", (CH, CH)),
    ("conv_w", (8, 2048)), ("conv_b", (1, 2048)), ("dt_bias", (1, CH)), ("a_log", (1, CH)),
    ("d_heads", (1, CH)), ("norm_g", (1, D_MODEL)),
)


def _blocks(v, n, off=0):
    return [v[:, off + i * CH: off + (i + 1) * CH] for i in range(n)]


def _split_mixer_params(vals):
    p = dict(vals)
    return {
        "ln_g": _blocks(p["ln_g"], N_BLK), "ln_b": _blocks(p["ln_b"], N_BLK),
        "wm": [p["wm"][g] for g in range(N_BLK)], "bs_t": p["bs_t"],
        "conv_w": _blocks(p["conv_w"], XBC_BLKS), "conv_b": _blocks(p["conv_b"], XBC_BLKS),
        "dt_bias": p["dt_bias"], "a_log": p["a_log"], "d_heads": p["d_heads"],
        "norm_g": _blocks(p["norm_g"], N_BLK),
    }


def _mixer_leaves(proj_ref, halo_ref, keep_halo):
    pv = proj_ref
    us = [pv[:, OFF_U + i * CH: OFF_U + (i + 1) * CH] for i in range(N_BLK)]
    vs = [pv[:, OFF_V + i * CH: OFF_V + (i + 1) * CH] for i in range(N_BLK)]
    zs = [pv[:, OFF_Z + i * CH: OFF_Z + (i + 1) * CH] for i in range(N_BLK)]
    xbcs = [pv[:, OFF_X + i * CH: OFF_X + (i + 1) * CH] for i in range(XBC_BLKS)]
    halos = [halo_ref[:, OFF_X + i * CH: OFF_X + (i + 1) * CH] * keep_halo for i in range(XBC_BLKS)]
    dtblk = pv[:, OFF_DT: OFF_DT + CH]
    return us, vs, zs, xbcs, halos, dtblk


def mixer_fwd(proj, prm):
    s = proj.shape[0]
    nc = s // CH
    names = [n for n, _ in _MIXER_PARAM_SHAPES]

    def body(proj_ref, halo_ref, *rest):
        p_refs = rest[:len(names)]
        ab_ref, hs_ref, h_ref = rest[len(names):]
        c = pl.program_id(0)

        @pl.when(c == 0)
        def _():
            h_ref[...] = jnp.zeros_like(h_ref)

        hs_ref[...] = h_ref[...]
        keep = (c > 0).astype(f32)
        us, vs, zs, xbcs, halos, dtblk = _mixer_leaves(proj_ref, halo_ref, keep)
        hps = [h_ref[i * CH:(i + 1) * CH, :] for i in range(N_BLK)]
        p = _split_mixer_params({n: r[...] for n, r in zip(names, p_refs)})
        a_out, b_out, h_out = _mixer_chunk(us, vs, zs, xbcs, halos, dtblk, hps, p)
        for i in range(N_BLK):
            ab_ref[:, i * CH:(i + 1) * CH] = a_out[i].astype(bf16)
            ab_ref[:, D_MODEL + i * CH: D_MODEL + (i + 1) * CH] = b_out[i].astype(bf16)
            h_ref[i * CH:(i + 1) * CH, :] = h_out[i]

    def const(shape):
        return pl.BlockSpec(shape, lambda c: (0,) * len(shape))

    return pl.pallas_call(
        body, name="mixer_fwd", grid=(nc,),
        in_specs=[pl.BlockSpec((CH, NP_IN), lambda c: (c, 0)),
                  pl.BlockSpec((8, NP_IN), lambda c: (jnp.maximum(c * (CH // 8) - 1, 0), 0))]
                 + [const(shp) for _, shp in _MIXER_PARAM_SHAPES],
        out_specs=[pl.BlockSpec((CH, 2 * D_MODEL), lambda c: (c, 0)),
                   pl.BlockSpec((None, D_MODEL, CH), lambda c: (c, 0, 0))],
        out_shape=[jax.ShapeDtypeStruct((s, 2 * D_MODEL), bf16), jax.ShapeDtypeStruct((nc, D_MODEL, CH), f32)],
        scratch_shapes=[pltpu.VMEM((D_MODEL, CH), f32)],
        compiler_params=_cparams(("arbitrary",)),
    )(proj, proj, *[prm[n] for n in names])


def mixer_bwd(proj, hstates, dab, prm):
    s = proj.shape[0]
    nc = s // CH
    names = [n for n, _ in _MIXER_PARAM_SHAPES]
    npar = len(names)

    def body(proj_ref, halo_ref, hs_ref, dab_ref, *rest):
        p_refs = rest[:npar]
        dproj_ref = rest[npar]
        g_refs = rest[npar + 1: 2 * npar + 1]
        dh_ref, dhalo_ref = rest[2 * npar + 1:]
        i = pl.program_id(0)
        c = nc - 1 - i

        @pl.when(i == 0)
        def _():
            dh_ref[...] = jnp.zeros_like(dh_ref)
            dhalo_ref[...] = jnp.zeros_like(dhalo_ref)
            for r in g_refs:
                r[...] = jnp.zeros_like(r)

        keep = (c > 0).astype(f32)
        us, vs, zs, xbcs, halos, dtblk = _mixer_leaves(proj_ref, halo_ref, keep)
        hps = [hs_ref[j * CH:(j + 1) * CH, :] for j in range(N_BLK)]
        pvals = {n: r[...] for n, r in zip(names, p_refs)}

        def fn(us, vs, zs, xbcs, halos, dtblk, hps, pvals):
            return _mixer_chunk(us, vs, zs, xbcs, halos, dtblk, hps, _split_mixer_params(pvals))

        _, vjp = jax.vjp(fn, us, vs, zs, xbcs, halos, dtblk, hps, pvals)
        da = [dab_ref[:, j * CH:(j + 1) * CH].astype(f32) for j in range(N_BLK)]
        db = [dab_ref[:, D_MODEL + j * CH: D_MODEL + (j + 1) * CH].astype(f32) for j in range(N_BLK)]
        dh = [dh_ref[j * CH:(j + 1) * CH, :] for j in range(N_BLK)]
        dus, dvs, dzs, dxbcs, dhalos, ddt, dhps, dp = vjp((da, db, dh))

        for j in range(N_BLK):
            dproj_ref[:, OFF_U + j * CH: OFF_U + (j + 1) * CH] = dus[j].astype(bf16)
            dproj_ref[:, OFF_V + j * CH: OFF_V + (j + 1) * CH] = dvs[j].astype(bf16)
            dproj_ref[:, OFF_Z + j * CH: OFF_Z + (j + 1) * CH] = dzs[j].astype(bf16)
            dh_ref[j * CH:(j + 1) * CH, :] = dhps[j]
        zeros_top = jnp.zeros((CH - 8, CH), f32)
        for j in range(XBC_BLKS):
            late = jnp.concatenate([zeros_top, dhalo_ref[:, j * CH:(j + 1) * CH]], axis=0)
            dproj_ref[:, OFF_X + j * CH: OFF_X + (j + 1) * CH] = (dxbcs[j] + late).astype(bf16)
        for j in range(XBC_BLKS):
            dhalo_ref[:, j * CH:(j + 1) * CH] = dhalos[j] * keep
        lane = lax.broadcasted_iota(jnp.int32, (CH, CH), 1)
        dproj_ref[:, OFF_DT: OFF_DT + CH] = jnp.where(lane < SSM_HEADS, ddt, 0.0).astype(bf16)
        dproj_ref[:, OFF_DT + CH:] = jnp.zeros((CH, NP_IN - OFF_DT - CH), bf16)
        for n, r in zip(names, g_refs):
            r[...] += dp[n]

    def const(shape):
        return pl.BlockSpec(shape, lambda i: (0,) * len(shape))

    outs = pl.pallas_call(
        body, name="mixer_bwd", grid=(nc,),
        in_specs=[pl.BlockSpec((CH, NP_IN), lambda i: (nc - 1 - i, 0)),
                  pl.BlockSpec((8, NP_IN), lambda i: (jnp.maximum((nc - 1 - i) * (CH // 8) - 1, 0), 0)),
                  pl.BlockSpec((None, D_MODEL, CH), lambda i: (nc - 1 - i, 0, 0)),
                  pl.BlockSpec((CH, 2 * D_MODEL), lambda i: (nc - 1 - i, 0))]
                 + [const(shp) for _, shp in _MIXER_PARAM_SHAPES],
        out_specs=[pl.BlockSpec((CH, NP_IN), lambda i: (nc - 1 - i, 0))]
                  + [const(shp) for _, shp in _MIXER_PARAM_SHAPES],
        out_shape=[jax.ShapeDtypeStruct((s, NP_IN), bf16)]
                  + [jax.ShapeDtypeStruct(shp, f32) for _, shp in _MIXER_PARAM_SHAPES],
        scratch_shapes=[pltpu.VMEM((D_MODEL, CH), f32), pltpu.VMEM((8, 2048), f32)],
        compiler_params=_cparams(("arbitrary",)),
    )(proj, proj, hstates, dab, *[prm[n] for n in names])
    return outs[0], dict(zip(names, outs[1:]))


_K_BLK = D_MODEL // CH
_V_BLK = _K_BLK + 1


def _attn_specs(rev, nb):
    def blk(i):
        return nb - 1 - i if rev else i

    q_spec = pl.BlockSpec((CH, D_MODEL), lambda i: (blk(i), 0))
    kv = lambda col, prev: pl.BlockSpec(
        (CH, CH), lambda i: (jnp.maximum(blk(i) - 1, 0) if prev else blk(i), col))
    return q_spec, [kv(_K_BLK, True), kv(_K_BLK, False), kv(_V_BLK, True), kv(_V_BLK, False)]


def attn_fwd(qkv, sink_row):
    s = qkv.shape[0]
    nb = s // CH

    def body(q_ref, kp_ref, kc_ref, vp_ref, vc_ref, sink_ref, o_ref):
        qps = [q_ref[:, p * CH:(p + 1) * CH] for p in range(N_BLK)]
        outs = _attn_block(qps, kp_ref[...], kc_ref[...], vp_ref[...], vc_ref[...], sink_ref[...],
                           pl.program_id(0) == 0)
        for p in range(N_BLK):
            o_ref[:, p * CH:(p + 1) * CH] = outs[p].astype(bf16)

    q_spec, kv_specs = _attn_specs(False, nb)
    return pl.pallas_call(
        body, name="attn_fwd", grid=(nb,),
        in_specs=[q_spec] + kv_specs + [pl.BlockSpec((1, CH), lambda i: (0, 0))],
        out_specs=pl.BlockSpec((CH, D_MODEL), lambda i: (i, 0)),
        out_shape=jax.ShapeDtypeStruct((s, D_MODEL), bf16),
        compiler_params=_cparams(("parallel",)),
    )(qkv, qkv, qkv, qkv, qkv, sink_row)


def attn_bwd(qkv, sink_row, dout):
    s = qkv.shape[0]
    nb = s // CH

    def body(q_ref, kp_ref, kc_ref, vp_ref, vc_ref, sink_ref, do_ref, dqkv_ref, dsink_ref, db_ref, carry_ref):
        i = pl.program_id(0)
        blk = nb - 1 - i

        @pl.when(i == 0)
        def _():
            dsink_ref[...] = jnp.zeros_like(dsink_ref)
            db_ref[...] = jnp.zeros_like(db_ref)
            carry_ref[...] = jnp.zeros_like(carry_ref)

        qps = [q_ref[:, p * CH:(p + 1) * CH] for p in range(N_BLK)]
        first = blk == 0
        _, vjp = jax.vjp(lambda *a: _attn_block(*a, first), qps, kp_ref[...], kc_ref[...], vp_ref[...],
                         vc_ref[...], sink_ref[...])
        dos = [do_ref[:, p * CH:(p + 1) * CH].astype(f32) for p in range(N_BLK)]
        dqs, dkp, dkc, dvp, dvc, dsink = vjp(dos)
        blocks = list(dqs) + [dkc + carry_ref[0], dvc + carry_ref[1]]
        for p, val in enumerate(blocks):
            dqkv_ref[:, p * CH:(p + 1) * CH] = val.astype(bf16)
            db_ref[:, p * CH:(p + 1) * CH] += jnp.sum(val, axis=0, keepdims=True)
        keep = jnp.logical_not(first).astype(f32)
        carry_ref[0] = dkp * keep
        carry_ref[1] = dvp * keep
        dsink_ref[...] += dsink

    q_spec, kv_specs = _attn_specs(True, nb)
    return pl.pallas_call(
        body, name="attn_bwd", grid=(nb,),
        in_specs=[q_spec] + kv_specs + [pl.BlockSpec((1, CH), lambda i: (0, 0)),
                                        pl.BlockSpec((CH, D_MODEL), lambda i: (nb - 1 - i, 0))],
        out_specs=[pl.BlockSpec((CH, QKV_DIM), lambda i: (nb - 1 - i, 0)), pl.BlockSpec((1, CH), lambda i: (0, 0)),
                   pl.BlockSpec((1, QKV_DIM), lambda i: (0, 0))],
        out_shape=[jax.ShapeDtypeStruct((s, QKV_DIM), bf16), jax.ShapeDtypeStruct((1, CH), f32),
                   jax.ShapeDtypeStruct((1, QKV_DIM), f32)],
        scratch_shapes=[pltpu.VMEM((2, CH, CH), f32)],
        compiler_params=_cparams(("arbitrary",)),
    )(qkv, qkv, qkv, qkv, qkv, sink_row, dout)


def _adamw_update(w, g, m, v):
    nm = ADAM_B1 * m + (1.0 - ADAM_B1) * g
    nv = ADAM_B2 * v + (1.0 - ADAM_B2) * jnp.square(g)
    m_hat = nm / (1.0 - ADAM_B1 ** ADAM_STEP)
    v_hat = nv / (1.0 - ADAM_B2 ** ADAM_STEP)
    return -ADAM_LR * (m_hat / (jnp.sqrt(v_hat) + ADAM_EPS) + ADAM_WD * w), nm, nv


def adamw_rows(w, r, m, v, layer, row_off, name, into=None):
    rows, cols = w.shape[1], w.shape[2]
    tr = 256
    assert rows % tr == 0 and row_off % tr == 0

    def body(w_ref, r_ref, m_ref, v_ref, *rest):
        g_ref, d_ref, nm_ref, nv_ref = rest[-4:]
        g = r_ref[...]
        g_ref[...] = g
        d_ref[...], nm_ref[...], nv_ref[...] = _adamw_update(w_ref[...], g, m_ref[...], v_ref[...])

    tile = pl.BlockSpec((None, tr, cols), lambda i: (layer, i, 0))
    extra = [] if into is None else list(into)
    return pl.pallas_call(
        body, name=name, grid=(rows // tr,),
        in_specs=[tile, pl.BlockSpec((tr, cols), lambda i: (row_off // tr + i, 0)), tile, tile] + [_ANY] * len(extra),
        out_specs=[tile] * 4, out_shape=[jax.ShapeDtypeStruct(w.shape, f32)] * 4,
        input_output_aliases={4 + k: k for k in range(len(extra))},
        compiler_params=_cparams(("parallel",)),
    )(w, r, m, v, *extra)


def adamw(w, g, m, v, name):
    def body(w_ref, g_ref, m_ref, v_ref, d_ref, nm_ref, nv_ref):
        d_ref[...], nm_ref[...], nv_ref[...] = _adamw_update(w_ref[...], g_ref[...], m_ref[...], v_ref[...])

    out_shape = [jax.ShapeDtypeStruct(w.shape, f32)] * 3
    if w.ndim == 3 and w.shape[1] == 1:
        tr = max(t for t in range(1, 129) if w.shape[0] % t == 0)
        tile = pl.BlockSpec((tr, 1, w.shape[2]), lambda i: (i, 0, 0))
        return pl.pallas_call(
            body, name=name, grid=(w.shape[0] // tr,),
            in_specs=[tile] * 4, out_specs=[tile] * 3, out_shape=out_shape,
            compiler_params=_cparams(("parallel",)),
        )(w, g, m, v)
    if w.ndim == 3 and w.shape[1] % 256 == 0:
        tile = pl.BlockSpec((None, 256, w.shape[2]), lambda l, i: (l, i, 0))
        return pl.pallas_call(
            body, name=name, grid=(w.shape[0], w.shape[1] // 256),
            in_specs=[tile] * 4, out_specs=[tile] * 3, out_shape=out_shape,
            compiler_params=_cparams(("parallel", "parallel")),
        )(w, g, m, v)
    return pl.pallas_call(body, name=name, in_specs=[_VMEM] * 4, out_specs=[_VMEM] * 3, out_shape=out_shape,
                          compiler_params=_cparams())(w, g, m, v)


_MESH = pl.DeviceIdType.MESH
_ANY = pl.BlockSpec(memory_space=pl.ANY)
_VMEM = pl.BlockSpec(memory_space=pltpu.VMEM)


def _place():
    x, y, c = lax.axis_index("x"), lax.axis_index("y"), lax.axis_index("c")
    chips = [(1 - x, y), (x, 1 - y), (1 - x, 1 - y)]
    return x, y, c, 2 * x + y, chips, [2 * cx + cy for cx, cy in chips]


def _half(c, rows):
    return pl.ds(pl.multiple_of(c * (rows // 2), 16), rows // 2)


def _step_rows(rows):
    return max(t for t in range(16, 641, 16) if rows % t == 0)


def place_shard(b, slot, name, dtype=bf16, after=None, cols=None):
    r, c_in = b.shape
    c = c_in if cols is None else cols
    tr = _step_rows(r)

    def body(slot_ref, b_ref, *rest):
        o_ref = rest[-1]
        if c > c_in:
            whole = (c_in // 128) * 128
            o_ref[:, whole:] = jnp.zeros((tr, c - whole), dtype)
        o_ref[:, :c_in] = b_ref[...].astype(dtype)

    order_specs = [] if after is None else [pl.BlockSpec((8, 128), lambda i, s: (0, 0))]
    return pl.pallas_call(
        body, name=name,
        grid_spec=pltpu.PrefetchScalarGridSpec(
            num_scalar_prefetch=1, grid=(r // tr,),
            in_specs=[pl.BlockSpec((tr, c_in), lambda i, s: (i, 0))] + order_specs,
            out_specs=pl.BlockSpec((None, tr, c), lambda i, s: (s[0], i, 0))),
        out_shape=jax.ShapeDtypeStruct((N_CHIPS, r, c), dtype),
        compiler_params=_cparams(("parallel",)),
    )(slot, b, *([] if after is None else [after]))


_HBM = pl.BlockSpec(memory_space=pltpu.HBM)
_SEM = pl.BlockSpec(memory_space=pltpu.SEMAPHORE)
_EFFECT = pltpu.SideEffectType.DATAFLOW_SIDE_EFFECTING


def _gather_ici_copies(bufs, send_sems, recv_sems):
    x, y, c, me, chips, chip_idx = _place()
    return [pltpu.make_async_remote_copy(
        src_ref=buf.at[me, _half(c, buf.shape[1])], dst_ref=buf.at[chip_idx[j], _half(c, buf.shape[1])],
        send_sem=send_sems.at[3 * k + j], recv_sem=recv_sems.at[3 * k + j],
        device_id=(*chips[j], c), device_id_type=_MESH) for j in range(3) for k, buf in enumerate(bufs)]


def gather_start(groups, tag):
    sizes = [len(g) for g in groups]
    flat = [b for g in groups for b in g]
    n = len(flat)

    def body(*refs):
        bufs, sems = refs[:n], refs[n:n + 2 * len(groups)]
        refs[-1][...] = jnp.zeros_like(refs[-1])
        x, y, c, me, chips, chip_idx = _place()
        lo = 0
        for gi, size in enumerate(sizes):
            for j in range(3):
                for k, buf in enumerate(bufs[lo:lo + size]):
                    mine = buf.at[me, _half(c, buf.shape[1])]
                    pltpu.make_async_remote_copy(
                        src_ref=mine, dst_ref=mine, send_sem=sems[2 * gi].at[3 * k + j],
                        recv_sem=sems[2 * gi + 1].at[3 * k + j], device_id=(*chips[j], c),
                        device_id_type=_MESH).start()
            lo += size

    sem_shapes = [pltpu.SemaphoreType.DMA((3 * size,)) for size in sizes for _ in range(2)]
    outs = pl.pallas_call(
        body, name=f"gather_start_{tag}",
        out_shape=(*sem_shapes, *[pltpu.HBM(b.shape, b.dtype) for b in flat], jax.ShapeDtypeStruct((8, 128), f32)),
        in_specs=[_HBM] * n, out_specs=(*[_SEM] * len(sem_shapes), *[_HBM] * n, _VMEM),
        input_output_aliases={i: len(sem_shapes) + i for i in range(n)},
        compiler_params=pltpu.CompilerParams(has_side_effects=_EFFECT),
    )(*[pltpu.with_memory_space_constraint(b, pltpu.HBM) for b in flat])
    sems = [(outs[2 * gi], outs[2 * gi + 1]) for gi in range(len(groups))]
    thru, lo = [], len(sem_shapes)
    for size in sizes:
        thru.append(list(outs[lo:lo + size]))
        lo += size
    return sems, thru, outs[-1]


def gather_wait(bufs, sems, after, tag):
    n = len(bufs)

    def body(*refs):
        for cp in _gather_ici_copies(refs[:n], refs[n], refs[n + 1]):
            cp.wait_send()
            cp.wait_recv()

    extra = list(after)
    return list(pl.pallas_call(
        body, name=f"gather_wait_{tag}",
        out_shape=[pltpu.HBM(b.shape, b.dtype) for b in bufs],
        in_specs=[_HBM] * n + [_SEM, _SEM] + [_ANY] * len(extra), out_specs=[_HBM] * n,
        input_output_aliases={i: i for i in range(n)},
        compiler_params=pltpu.CompilerParams(has_side_effects=_EFFECT),
    )(*bufs, *sems, *extra))


def gather_forward(bufs, tag):
    n = len(bufs)

    def body(*refs):
        out_refs = refs[n:2 * n]
        send_sems, recv_sems = refs[2 * n:]
        x, y, c, me, chips, chip_idx = _place()

        def copy(k, j, half):
            part = out_refs[k].at[chip_idx[j], _half(half, out_refs[k].shape[1])]
            return pltpu.make_async_remote_copy(
                src_ref=part, dst_ref=part, send_sem=send_sems.at[3 * k + j], recv_sem=recv_sems.at[3 * k + j],
                device_id=(x, y, 1 - c), device_id_type=_MESH)

        sends = [copy(k, j, c) for j in range(3) for k in range(n)]
        for cp in sends:
            cp.start()
        for j in range(3):
            for k in range(n):
                copy(k, j, 1 - c).wait_recv()
        for cp in sends:
            cp.wait_send()

    return list(pl.pallas_call(
        body, name=f"gather_forward_{tag}",
        out_shape=[jax.ShapeDtypeStruct(b.shape, b.dtype) for b in bufs],
        in_specs=[_ANY] * n, out_specs=[_ANY] * n, input_output_aliases={i: i for i in range(n)},
        scratch_shapes=[pltpu.SemaphoreType.DMA((3 * n,)), pltpu.SemaphoreType.DMA((3 * n,))],
    )(*bufs))


def _forward_copy(ref, k, j, half, send_sems, recv_sems):
    x, y, c, me, chips, chip_idx = _place()
    part = ref.at[chip_idx[j], _half(half, ref.shape[1])]
    return pltpu.make_async_remote_copy(
        src_ref=part, dst_ref=part, send_sem=send_sems.at[3 * k + j], recv_sem=recv_sems.at[3 * k + j],
        device_id=(x, y, 1 - c), device_id_type=_MESH)


def forward_start(bufs, tag):
    n = len(bufs)

    def body(*refs):
        c = _place()[2]
        for j in range(3):
            for k in range(n):
                _forward_copy(refs[k], k, j, c, refs[n], refs[n + 1]).start()
        refs[-1][...] = jnp.zeros_like(refs[-1])

    outs = pl.pallas_call(
        body, name=f"forward_start_{tag}",
        out_shape=(pltpu.SemaphoreType.DMA((3 * n,)), pltpu.SemaphoreType.DMA((3 * n,)),
                   *[pltpu.HBM(b.shape, b.dtype) for b in bufs], jax.ShapeDtypeStruct((8, 128), f32)),
        in_specs=[_HBM] * n, out_specs=(_SEM, _SEM, *[_HBM] * n, _VMEM),
        input_output_aliases={i: 2 + i for i in range(n)},
        compiler_params=pltpu.CompilerParams(has_side_effects=_EFFECT),
    )(*[pltpu.with_memory_space_constraint(b, pltpu.HBM) for b in bufs])
    return (outs[0], outs[1], list(outs[2:2 + n])), outs[-1]


def forward_wait(send_sems, recv_sems, bufs, after, tag):
    n = len(bufs)

    def body(*refs):
        c = _place()[2]
        for j in range(3):
            for k in range(n):
                _forward_copy(refs[k], k, j, c, refs[n], refs[n + 1]).wait_send()
                _forward_copy(refs[k], k, j, 1 - c, refs[n], refs[n + 1]).wait_recv()

    return list(pl.pallas_call(
        body, name=f"forward_wait_{tag}",
        out_shape=[pltpu.HBM(b.shape, b.dtype) for b in bufs],
        in_specs=[_HBM] * n + [_SEM, _SEM, _ANY], out_specs=[_HBM] * n,
        input_output_aliases={i: i for i in range(n)},
        compiler_params=pltpu.CompilerParams(has_side_effects=_EFFECT),
    )(*bufs, send_sems, recv_sems, after))


def exchange_halves(bufs, tag):
    n = len(bufs)

    def body(*refs):
        g_refs, out_refs = refs[:n], refs[n:2 * n]
        send_sems, recv_sems = refs[2 * n:]
        x, y, c, *_ = _place()
        cps = [pltpu.make_async_remote_copy(
            src_ref=g_refs[b].at[:, _half(1 - c, g_refs[b].shape[1])], dst_ref=out_refs[b],
            send_sem=send_sems.at[b], recv_sem=recv_sems.at[b], device_id=(x, y, 1 - c), device_id_type=_MESH)
            for b in range(n)]
        for cp in cps:
            cp.start()
        for cp in cps:
            cp.wait()

    return pl.pallas_call(
        body, name=f"exchange_halves_{tag}",
        out_shape=[jax.ShapeDtypeStruct((N_CHIPS, b.shape[1] // 2, b.shape[2]), b.dtype) for b in bufs],
        in_specs=[_ANY] * n, out_specs=[_ANY] * n,
        scratch_shapes=[pltpu.SemaphoreType.DMA((n,)), pltpu.SemaphoreType.DMA((n,))],
    )(*bufs)


def add_halves(g, got, c_idx, name):
    hr, cols = got.shape[1], got.shape[2]
    tr = _step_rows(hr)
    steps = hr // tr

    def body(c_ref, g_ref, got_ref, o_ref):
        o_ref[...] = (g_ref[...].astype(f32) + got_ref[...].astype(f32)).astype(bf16)

    return pl.pallas_call(
        body, name=name,
        grid_spec=pltpu.PrefetchScalarGridSpec(
            num_scalar_prefetch=1, grid=(N_CHIPS, steps),
            in_specs=[pl.BlockSpec((None, tr, cols), lambda s, i, c: (s, c[0] * steps + i, 0)),
                      pl.BlockSpec((None, tr, cols), lambda s, i, c: (s, i, 0))],
            out_specs=pl.BlockSpec((None, tr, cols), lambda s, i, c: (s, i, 0))),
        out_shape=jax.ShapeDtypeStruct(got.shape, bf16),
        compiler_params=_cparams(("parallel", "parallel")),
    )(c_idx, g, got)


def sum_chips(t, got, place_idx, name):
    hr, cols = t.shape[1], t.shape[2]
    tr = _step_rows(hr)
    steps = hr // tr

    def body(idx_ref, t_ref, got_ref, o_ref):
        acc = t_ref[...].astype(f32)
        for j in range(3):
            acc = acc + got_ref[j].astype(f32)
        o_ref[...] = acc

    return pl.pallas_call(
        body, name=name,
        grid_spec=pltpu.PrefetchScalarGridSpec(
            num_scalar_prefetch=1, grid=(steps,),
            in_specs=[pl.BlockSpec((None, tr, cols), lambda i, idx: (idx[0], i, 0)),
                      pl.BlockSpec((3, tr, cols), lambda i, idx: (0, i, 0))],
            out_specs=pl.BlockSpec((tr, cols), lambda i, idx: (idx[1] * steps + i, 0))),
        out_shape=jax.ShapeDtypeStruct((2 * hr, cols), f32),
        compiler_params=_cparams(("parallel",)),
    )(place_idx, t, got)


def _share_copies(refs, send_sems, recv_sems):
    x, y, c, *_ = _place()
    return [pltpu.make_async_remote_copy(
        src_ref=ref.at[_half(c, ref.shape[0])], dst_ref=ref.at[_half(c, ref.shape[0])], send_sem=send_sems.at[b],
        recv_sem=recv_sems.at[b], device_id=(x, y, 1 - c), device_id_type=_MESH) for b, ref in enumerate(refs)]


def share_start(bufs, tag):
    n = len(bufs)

    def body(*refs):
        for cp in _share_copies(refs[:n], refs[n], refs[n + 1]):
            cp.start()
        token = refs[-1]
        token[...] = jnp.zeros_like(token)

    outs = pl.pallas_call(
        body, name=f"share_start_{tag}",
        out_shape=(pltpu.SemaphoreType.DMA((n,)), pltpu.SemaphoreType.DMA((n,)),
                   *[pltpu.HBM(b.shape, b.dtype) for b in bufs], jax.ShapeDtypeStruct((8, 128), f32)),
        in_specs=[_HBM] * n, out_specs=(_SEM, _SEM, *[_HBM] * n, _VMEM),
        input_output_aliases={i: 2 + i for i in range(n)},
        compiler_params=pltpu.CompilerParams(has_side_effects=_EFFECT),
    )(*[pltpu.with_memory_space_constraint(b, pltpu.HBM) for b in bufs])
    return (outs[0], outs[1], list(outs[2:2 + n])), outs[-1]


def share_wait(send_sems, recv_sems, bufs, after, tag):
    n = len(bufs)

    def body(*refs):
        x, y, c, *_ = _place()
        for b, ref in enumerate(refs[:n]):
            cp = pltpu.make_async_remote_copy(
                src_ref=ref.at[_half(c, ref.shape[0])], dst_ref=ref.at[_half(1 - c, ref.shape[0])],
                send_sem=refs[n].at[b], recv_sem=refs[n + 1].at[b], device_id=(x, y, 1 - c), device_id_type=_MESH)
            cp.wait_send()
            cp.wait_recv()

    return list(pl.pallas_call(
        body, name=f"share_wait_{tag}",
        out_shape=[pltpu.HBM(b.shape, b.dtype) for b in bufs],
        in_specs=[_HBM] * n + [_SEM, _SEM, _ANY], out_specs=[_HBM] * n,
        input_output_aliases={i: i for i in range(n)},
        compiler_params=pltpu.CompilerParams(has_side_effects=_EFFECT),
    )(*bufs, send_sems, recv_sems, after))


def _scatter_copies(t_refs, land_refs, send_sems, recv_sems):
    x, y, c, me, chips, chip_idx = _place()
    return [pltpu.make_async_remote_copy(
        src_ref=t_refs[b].at[chip_idx[j]], dst_ref=land_refs[b].at[j], send_sem=send_sems.at[3 * b + j],
        recv_sem=recv_sems.at[3 * b + j], device_id=(*chips[j], c), device_id_type=_MESH)
        for j in range(3) for b in range(len(t_refs))]


def scatter_start(ts, tag):
    n = len(ts)
    lands = [lax.empty((3,) + t.shape[1:], t.dtype) for t in ts]

    def body(*refs):
        for cp in _scatter_copies(refs[:n], refs[n:2 * n], refs[2 * n], refs[2 * n + 1]):
            cp.start()
        token = refs[-1]
        token[...] = jnp.zeros_like(token)

    hbm = [pltpu.HBM(a.shape, a.dtype) for a in (*ts, *lands)]
    outs = pl.pallas_call(
        body, name=f"scatter_start_{tag}",
        out_shape=(pltpu.SemaphoreType.DMA((3 * n,)), pltpu.SemaphoreType.DMA((3 * n,)), *hbm,
                   jax.ShapeDtypeStruct((8, 128), f32)),
        in_specs=[_HBM] * (2 * n), out_specs=(_SEM, _SEM, *[_HBM] * (2 * n), _VMEM),
        input_output_aliases={i: 2 + i for i in range(2 * n)},
        compiler_params=pltpu.CompilerParams(has_side_effects=_EFFECT),
    )(*[pltpu.with_memory_space_constraint(a, pltpu.HBM) for a in (*ts, *lands)])
    return outs[0], outs[1], list(outs[2:2 + n]), list(outs[2 + n:2 + 2 * n]), outs[-1]


def scatter_wait(send_sems, recv_sems, ts, lands, after, tag):
    n = len(ts)

    def body(*refs):
        for cp in _scatter_copies(refs[:n], refs[n:2 * n], refs[2 * n], refs[2 * n + 1]):
            cp.wait_send()
            cp.wait_recv()

    outs = pl.pallas_call(
        body, name=f"scatter_wait_{tag}",
        out_shape=[pltpu.HBM(a.shape, a.dtype) for a in (*ts, *lands)],
        in_specs=[_HBM] * (2 * n) + [_SEM, _SEM, _ANY], out_specs=[_HBM] * (2 * n),
        input_output_aliases={i: i for i in range(2 * n)},
        compiler_params=pltpu.CompilerParams(has_side_effects=_EFFECT),
    )(*ts, *lands, send_sems, recv_sems, after)
    return list(outs[:n]), list(outs[n:])


N_SENDERS = 7


def _direct_copies(g_refs, land_refs, send_sems, recv_sems):
    x, y, c, me, chips, chip_idx = _place()
    cps = []
    for b, (g, land) in enumerate(zip(g_refs, land_refs)):
        rows, base = g.shape[1], N_SENDERS * b
        cps.append(pltpu.make_async_remote_copy(
            src_ref=g.at[me, _half(1 - c, rows)], dst_ref=land.at[0], send_sem=send_sems.at[base],
            recv_sem=recv_sems.at[base], device_id=(x, y, 1 - c), device_id_type=_MESH))
        for j in range(3):
            for core in range(2):
                cps.append(pltpu.make_async_remote_copy(
                    src_ref=g.at[chip_idx[j], _half(core, rows)], dst_ref=land.at[1 + 2 * j + c],
                    send_sem=send_sems.at[base + 1 + 2 * j + core], recv_sem=recv_sems.at[base + 1 + 2 * j + c],
                    device_id=(*chips[j], core), device_id_type=_MESH))
    return cps


def direct_start(gs, tag):
    n = len(gs)
    lands = [lax.empty((N_SENDERS, g.shape[1] // 2, g.shape[2]), g.dtype) for g in gs]

    def body(*refs):
        for cp in _direct_copies(refs[:n], refs[n:2 * n], refs[2 * n], refs[2 * n + 1]):
            cp.start()
        token = refs[-1]
        token[...] = jnp.zeros_like(token)

    hbm = [pltpu.HBM(a.shape, a.dtype) for a in (*gs, *lands)]
    outs = pl.pallas_call(
        body, name=f"direct_start_{tag}",
        out_shape=(pltpu.SemaphoreType.DMA((N_SENDERS * n,)), pltpu.SemaphoreType.DMA((N_SENDERS * n,)), *hbm,
                   jax.ShapeDtypeStruct((8, 128), f32)),
        in_specs=[_HBM] * (2 * n), out_specs=(_SEM, _SEM, *[_HBM] * (2 * n), _VMEM),
        input_output_aliases={i: 2 + i for i in range(2 * n)},
        compiler_params=pltpu.CompilerParams(has_side_effects=_EFFECT),
    )(*[pltpu.with_memory_space_constraint(a, pltpu.HBM) for a in (*gs, *lands)])
    return outs[0], outs[1], list(outs[2:2 + n]), list(outs[2 + n:2 + 2 * n]), outs[-1]


def direct_wait(send_sems, recv_sems, gs, lands, after, tag):
    n = len(gs)

    def body(*refs):
        g_refs, land_refs, sends, recvs = refs[:n], refs[n:2 * n], refs[2 * n], refs[2 * n + 1]
        for b in range(n):
            for k in range(N_SENDERS):
                cp = pltpu.make_async_remote_copy(
                    src_ref=g_refs[b].at[0, _half(0, g_refs[b].shape[1])], dst_ref=land_refs[b].at[k],
                    send_sem=sends.at[N_SENDERS * b + k], recv_sem=recvs.at[N_SENDERS * b + k],
                    device_id=_place()[:3], device_id_type=_MESH)
                cp.wait_send()
                cp.wait_recv()

    outs = pl.pallas_call(
        body, name=f"direct_wait_{tag}",
        out_shape=[pltpu.HBM(a.shape, a.dtype) for a in (*gs, *lands)],
        in_specs=[_HBM] * (2 * n) + [_SEM, _SEM, _ANY], out_specs=[_HBM] * (2 * n),
        input_output_aliases={i: i for i in range(2 * n)},
        compiler_params=pltpu.CompilerParams(has_side_effects=_EFFECT),
    )(*gs, *lands, send_sems, recv_sems, after)
    return list(outs[:n]), list(outs[n:])


def sum_senders(g, lands, place_idx, name):
    hr, cols = lands.shape[1], lands.shape[2]
    tr = _step_rows(hr)
    steps = hr // tr

    def body(idx_ref, g_ref, land_ref, o_ref):
        acc = g_ref[...].astype(f32)
        for k in range(N_SENDERS):
            acc = acc + land_ref[k].astype(f32)
        o_ref[...] = acc

    return pl.pallas_call(
        body, name=name,
        grid_spec=pltpu.PrefetchScalarGridSpec(
            num_scalar_prefetch=1, grid=(steps,),
            in_specs=[pl.BlockSpec((None, tr, cols), lambda i, idx: (idx[0], idx[1] * steps + i, 0)),
                      pl.BlockSpec((N_SENDERS, tr, cols), lambda i, idx: (0, i, 0))],
            out_specs=pl.BlockSpec((tr, cols), lambda i, idx: (idx[1] * steps + i, 0))),
        out_shape=jax.ShapeDtypeStruct((2 * hr, cols), f32),
        compiler_params=_cparams(("parallel",)),
    )(place_idx, g, lands)


class GradReducer:
    def __init__(self, c_idx, place_idx):
        self.c_idx, self.place_idx = c_idx, place_idx

    def start(self, bufs, tag, direct=False):
        if direct:
            send_sems, recv_sems, gs, lands, token = direct_start(bufs, tag)
            return (True, send_sems, recv_sems, gs, lands), token
        got = exchange_halves(bufs, tag)
        ts = [add_halves(b, g, self.c_idx, f"add_halves_{tag}{i}") for i, (b, g) in enumerate(zip(bufs, got))]
        send_sems, recv_sems, ts, lands, token = scatter_start(ts, tag)
        return (False, send_sems, recv_sems, ts, lands), token

    def finish(self, state, after, tag):
        direct, *flight = state
        if direct:
            gs, lands = direct_wait(*flight, after, tag)
            sums = [sum_senders(g, l, self.place_idx, f"sum_senders_{tag}{i}") for i, (g, l) in enumerate(zip(gs, lands))]
        else:
            ts, lands = scatter_wait(*flight, after, tag)
            sums = [sum_chips(t, l, self.place_idx, f"sum_chips_{tag}{i}") for i, (t, l) in enumerate(zip(ts, lands))]
        return share_start(sums, tag)

    def collect(self, pending, after, tag):
        return share_wait(*pending, after, tag)


def allreduce_small(sp):
    rows = sp.shape[0]
    hr = rows // 2

    def body(s_ref, out_ref, sib_ref, chip_ref, four_ref, send_sems, recv_sems):
        x, y, c, me, chips, chip_idx = _place()
        sibling = (x, y, 1 - c)
        mine = pl.ds(pl.multiple_of(c * hr, 8), hr)
        other = pl.ds(pl.multiple_of((1 - c) * hr, 8), hr)

        swap = pltpu.make_async_remote_copy(src_ref=s_ref, dst_ref=sib_ref, send_sem=send_sems.at[0],
                                            recv_sem=recv_sems.at[0], device_id=sibling, device_id_type=_MESH)
        swap.start()
        swap.wait()
        is_core0 = c == 0
        chip_ref[...] = jnp.where(is_core0, s_ref[...], sib_ref[...]) + jnp.where(is_core0, sib_ref[...], s_ref[...])

        sends = [pltpu.make_async_remote_copy(
            src_ref=chip_ref.at[mine], dst_ref=four_ref.at[me], send_sem=send_sems.at[1 + j],
            recv_sem=recv_sems.at[1 + j], device_id=(*chips[j], c), device_id_type=_MESH) for j in range(3)]
        for cp in sends:
            cp.start()
        four_ref[me] = chip_ref[mine, :]
        for j in range(3):
            pltpu.make_async_remote_copy(
                src_ref=chip_ref.at[mine], dst_ref=four_ref.at[chip_idx[j]], send_sem=send_sems.at[1 + j],
                recv_sem=recv_sems.at[1 + j], device_id=(*chips[j], c), device_id_type=_MESH).wait_recv()
        for cp in sends:
            cp.wait_send()
        out_ref[mine, :] = (four_ref[0] + four_ref[1]) + (four_ref[2] + four_ref[3])

        share = pltpu.make_async_remote_copy(src_ref=out_ref.at[mine], dst_ref=out_ref.at[mine], send_sem=send_sems.at[4],
                                             recv_sem=recv_sems.at[4], device_id=sibling, device_id_type=_MESH)
        share.start()
        pltpu.make_async_remote_copy(src_ref=out_ref.at[mine], dst_ref=out_ref.at[other], send_sem=send_sems.at[4],
                                     recv_sem=recv_sems.at[4], device_id=sibling, device_id_type=_MESH).wait_recv()
        share.wait_send()

    return pl.pallas_call(
        body, name="allreduce_small",
        out_shape=jax.ShapeDtypeStruct(sp.shape, sp.dtype),
        in_specs=[_VMEM], out_specs=_VMEM,
        scratch_shapes=[pltpu.VMEM(sp.shape, sp.dtype), pltpu.VMEM(sp.shape, sp.dtype),
                        pltpu.VMEM((N_CHIPS, hr, sp.shape[1]), sp.dtype),
                        pltpu.SemaphoreType.DMA((5,)), pltpu.SemaphoreType.DMA((5,))],
        compiler_params=_cparams(),
    )(sp)


def _n_rows(shape):
    n = 1
    for d in shape:
        n *= d
    return 8 * (-(-n // 8192))


def _pack(arrays, total_rows):
    parts = []
    for a in arrays:
        flat = a.reshape(-1)
        parts.append(jnp.pad(flat, (0, 1024 * _n_rows(a.shape) - flat.shape[0])).reshape(-1, 1024))
    rows = jnp.concatenate(parts, axis=0)
    return jnp.pad(rows, ((0, total_rows - rows.shape[0]), (0, 0)))


def _unpack(packed, shapes):
    out, r = [], 0
    for shp in shapes:
        n = 1
        for d in shp:
            n *= d
        nr = _n_rows(shp)
        out.append(packed[r:r + nr].reshape(-1)[:n].reshape(shp))
        r += nr
    return out


_COLUMN_SHARDED = ("w_in_even", "w_qkv")
IN_SHARD, IN_PAD = 1284, 1408
QKV_SHARD, QKV_PAD = 320, 384


def _lane_padded(a, cols):
    return jnp.pad(a, ((0, 0), (0, cols - a.shape[1])))


_SMALL_SHAPES = (
    ("norm_mix_g", (2, 1024)), ("norm_mlp_g", (2, 1024)), ("final_norm_g", (1024,)), ("gm_ln_g", (1, 1024)),
    ("gm_ln_b", (1, 1024)), ("gm_w_s", (1, 8, 128, 128)), ("gm_b_s", (1, 8, 128)), ("ssm_conv_b", (1, 2048)),
    ("ssm_dt_bias", (1, 16)), ("ssm_a_log", (1, 16)), ("ssm_d", (1, 16)), ("ssm_norm_g", (1, 1024)),
    ("attn_sinks", (1, 16)), ("ssm_conv_w", (1, 4, 2048)), ("b_qkv", (1, 1280)), ("b_o", (1, 1024)),
)
_N_REPLICATED = 13
_SHARDED_SMALL = (("ssm_conv_w", 2, 512), ("b_qkv", 1, 320), ("b_o", 1, 256))
_SHARD_PACK_ROWS = 32


def _cols_by_owner(a):
    return a.transpose(1, 0, 2).reshape(a.shape[1], -1)


class WeightGatherer:
    def __init__(self, w, chip_idx):
        def place(tag, b, dtype=bf16, after=None, cols=None):
            return place_shard(b, chip_idx, f"place_shard_{tag}", dtype, after, cols)

        sems_in, bufs_in, self.started = gather_start([
            [place("in", w["w_in_even"][0].astype(bf16), cols=IN_PAD),
             place("small", _pack([w[n] for n, _, _ in _SHARDED_SMALL], _SHARD_PACK_ROWS), f32)]], "in")
        t = self.started
        sems, bufs, self.all_started = gather_start([
            [place("out", w["w_out_even"][0], after=t), place("up0", w["w_up"][0], after=t),
             place("down0", w["w_down"][0], after=t)],
            [place("qkv", w["w_qkv"][0], after=t, cols=QKV_PAD), place("o", w["w_o"][0], after=t),
             place("up1", w["w_up"][1], after=t), place("down1", w["w_down"][1], after=t)],
        ], "rest")
        self.sems, self.bufs = sems_in + sems, bufs_in + bufs

    def _group(self, gi, after, tag):
        return gather_forward(gather_wait(self.bufs[gi], self.sems[gi], after, tag), tag)

    def mixer_in(self, after):
        g, small = self._group(0, [after, self.all_started], "in")
        shard_shapes = [tuple(width if i == axis else d for i, d in enumerate(dict(_SMALL_SHAPES)[n]))
                        for n, axis, width in _SHARDED_SMALL]
        per_chip = [_unpack(small[s], shard_shapes) for s in range(N_CHIPS)]
        full = {n: jnp.concatenate([per_chip[s][i] for s in range(N_CHIPS)], axis=axis)
                for i, (n, axis, _) in enumerate(_SHARDED_SMALL)}
        w_in_p = jnp.concatenate([g[s, :, :IN_SHARD] for s in range(N_CHIPS)]
                                 + [jnp.zeros((g.shape[1], NP_IN - IN_EVEN), g.dtype)], axis=1)
        return w_in_p, full

    def layer0(self, after):
        w_out, w_up, w_down = self._group(1, [after], "l0")
        return w_out.reshape(2048, 1024), w_up, w_down.reshape(4096, 1024)

    def layer1_start(self, after):
        return forward_start(gather_wait(self.bufs[2], self.sems[2], [after], "l1"), "l1")

    def layer1(self, pending, after):
        q, w_o, w_up, w_down = forward_wait(*pending, after, "l1")
        w_qkv = jnp.concatenate([q[s, :, :QKV_SHARD] for s in range(N_CHIPS)], axis=1)
        return w_qkv, w_o.reshape(1024, 1024), w_up, w_down.reshape(4096, 1024)


def _row2(v):
    return v.reshape(1, -1)


def _lane_pad(v):
    return jnp.pad(v, ((0, 0), (0, CH - v.shape[1])))


L1_ROWS = 2048 + 256
L0_ROWS = 2048 + 512
_H_AND_NORM = (("tile", f32), ("tile", bf16))
_DX_AND_DG = (("tile", f32), ("sum", D_MODEL))


def _mlp_bwd(dh_out, h, g_row, y, a, w_up, w_down, tag, group_rows, after=None):
    shape = (N_CHIPS, group_rows, D_MODEL)
    da = matmul(dh_out, w_down, dims="nt", name=f"mlp_da{tag}", out_dtype=bf16, tm=2048, tn=1024,
                epi=_times_relu2_grad, epi_args=(("tile", a),), after=after)
    buf = matmul(a, dh_out, dims="tn", name=f"mlp_dwdown{tag}", out_dtype=bf16, a_pro=_relu2,
                 dest=(shape, (None, 1024, 512), lambda i, j: (i, 1, j), None))
    buf = matmul(y, da, dims="tn", name=f"mlp_dwup{tag}", out_dtype=bf16, tn=1024,
                 dest=(shape, (None, 1024, 1024), lambda i, j: (j, 0, 0), buf))
    dh, dg, dh_colsum = matmul_rows(da, w_up, dims="nt", name=f"mlp_dy{tag}", epi=_norm_bwd_res_colsum,
                                    epi_args=(("tile", h), ("row", g_row), ("tile", dh_out)),
                                    outs=_DX_AND_DG + (("sum", D_MODEL),))
    return dh, dg, buf, dh_colsum


def _by_owner(a):
    return a.reshape(N_CHIPS, a.shape[0] // N_CHIPS, a.shape[1])


def _row_shards(a, shard, padded):
    return jnp.stack([jnp.pad(a[shard * s: shard * (s + 1)], ((0, padded - shard), (0, 0))) for s in range(N_CHIPS)])


def _col_shards(a, shard, padded):
    return jnp.stack([_lane_padded(a[:, shard * s: shard * (s + 1)], padded) for s in range(N_CHIPS)])


def _local_step(x, target, weights, sm, reducer):
    w_up, w_down = [None, None], [None, None]
    mix_g = [_row2(sm["norm_mix_g"][i]) for i in range(2)]
    y0 = rmsnorm_fwd(x, mix_g[0] + weights.started[:1, :1], "mix_norm0")
    w_in_p, sharded_small = weights.mixer_in(y0)
    sm = {**sm, **sharded_small}
    mlp_g = [_row2(sm["norm_mlp_g"][i]) for i in range(2)]
    mixer_prm = {
        "ln_g": sm["gm_ln_g"], "ln_b": sm["gm_ln_b"], "wm": sm["gm_w_s"][0],
        "bs_t": jnp.pad(sm["gm_b_s"][0].T, ((0, 0), (0, CH - N_BLK))),
        "conv_w": jnp.pad(sm["ssm_conv_w"][0], ((0, 4), (0, 0))), "conv_b": sm["ssm_conv_b"],
        "dt_bias": _lane_pad(sm["ssm_dt_bias"]), "a_log": _lane_pad(sm["ssm_a_log"]),
        "d_heads": _lane_pad(sm["ssm_d"]), "norm_g": sm["ssm_norm_g"],
    }
    sink_row = _lane_pad(sm["attn_sinks"])

    proj = matmul(y0, w_in_p, dims="nn", name="in_proj", tm=2048, tn=768)
    ab, hstates = mixer_fwd(proj, mixer_prm)
    w_out, w_up[0], w_down[0] = weights.layer0(ab)
    h1, y1 = matmul_rows(ab, w_out, dims="nn", name="out_proj", epi=_res_norm,
                         epi_args=(("tile", x), ("row", mlp_g[0])), outs=_H_AND_NORM)
    a1 = matmul(y1, w_up[0], dims="nn", name="mlp_up0", out_dtype=bf16, tm=2048, tn=1024)
    pending_l1, forwarding_l1 = weights.layer1_start(a1)
    h2, y2 = matmul_rows(a1, w_down[0], dims="nn", name="mlp_down0", a_pro=_relu2, epi=_res_norm,
                         epi_args=(("tile", h1), ("row", mix_g[1])), outs=_H_AND_NORM, after=forwarding_l1)
    w_qkv, w_o, w_up[1], w_down[1] = weights.layer1(pending_l1, h2)
    qkv = matmul(y2, w_qkv, dims="nn", name="qkv_proj", tn=QKV_DIM, epi=_add_bias, epi_args=(("row", sm["b_qkv"]),))
    att = attn_fwd(qkv, sink_row)
    h3, y3 = matmul_rows(att, w_o, dims="nn", name="o_proj", epi=_bias_res_norm,
                         epi_args=(("row", sm["b_o"]), ("tile", h2), ("row", mlp_g[1])), outs=_H_AND_NORM)
    a3 = matmul(y3, w_up[1], dims="nn", name="mlp_up1", out_dtype=bf16, tm=2048, tn=1024)
    dh4, dg_final, loss = matmul_rows(
        a3, w_down[1], dims="nn", name="mlp_down1", a_pro=_relu2, epi=_res_norm_loss,
        epi_args=(("tile", h3), ("row", _row2(sm["final_norm_g"])), ("tile", target)),
        outs=(("tile", f32), ("sum", D_MODEL), ("sum", 128)))

    dh3, dg_mlp1, l1_buf, db_o = _mlp_bwd(dh4, h3, mlp_g[1], y3, a3, w_up[1], w_down[1], 1, L1_ROWS)
    datt = matmul(dh3, w_o, dims="nt", name="attn_dout", out_dtype=bf16, tm=2048)
    l1_buf = matmul(att, dh3, dims="tn", name="dw_o", out_dtype=bf16, tm=256,
                    dest=(l1_buf.shape, (None, 256, 512), lambda i, j: (i, 2048 // 256, j), l1_buf))
    dqkv, dsink, db_qkv = attn_bwd(qkv, sink_row, datt)
    dw_qkv = matmul(y2, dqkv, dims="tn", name="dw_qkv", out_dtype=bf16, tn=QKV_DIM)
    dh2, dg_mix1 = matmul_rows(dqkv, w_qkv, dims="nt", name="dy_qkv", epi=_norm_bwd_res,
                               epi_args=(("tile", h2), ("row", mix_g[1]), ("tile", dh3)), outs=_DX_AND_DG)
    flight1, token1 = reducer.start([l1_buf, _col_shards(dw_qkv, QKV_SHARD, QKV_PAD)], "l1", direct=True)
    dh1, dg_mlp0, l0_buf, _ = _mlp_bwd(dh2, h1, mlp_g[0], y1, a1, w_up[0], w_down[0], 0, L0_ROWS, after=token1)
    pending1, shared1 = reducer.finish(flight1, dh1, "l1")
    l0_buf = matmul(ab, dh1, dims="tn", name="dw_out", out_dtype=bf16, tm=512, after=shared1,
                    dest=(l0_buf.shape, (None, 512, 512), lambda i, j: (i, 2048 // 512, j), l0_buf))
    flight0, token0 = reducer.start([l0_buf], "l0", direct=True)
    dab = matmul(dh1, w_out, dims="nt", name="mixer_dout", tm=2048, tn=1024, after=token0)
    dproj, dmix = mixer_bwd(proj, hstates, dab, mixer_prm)
    dw_in_t = matmul(dproj, y0, dims="tn", name="dw_in", out_dtype=bf16, tm=768, tn=1024)
    pending0, shared0 = reducer.finish(flight0, dw_in_t, "l0")
    flight_in, token_in = reducer.start([_row_shards(dw_in_t, IN_SHARD, IN_PAD)], "in")
    dx, dg_mix0 = matmul_rows(dproj, w_in_p, dims="nt", name="dy_in", tm=256, epi=_norm_bwd_res,
                              epi_args=(("tile", x), ("row", mix_g[0]), ("tile", dh1)), outs=_DX_AND_DG,
                              after=token_in + shared0)
    pending_in, _ = reducer.finish(flight_in, dx, "in")
    r_l1, r_qkv = reducer.collect(pending1, dx, "l1")
    (r_l0,) = reducer.collect(pending0, dx, "l0")
    (r_in,) = reducer.collect(pending_in, dx, "in")
    reduced = {
        "w_out_even": [(r_l0, 2048)], "w_o": [(r_l1, 2048)], "w_up": [(r_l0, 0), (r_l1, 0)],
        "w_down": [(r_l0, 1024), (r_l1, 1024)],
        "w_in_even": r_in[:IN_SHARD].T[None], "w_qkv": r_qkv[None, :, :QKV_SHARD],
    }

    small_grads = {
        "norm_mix_g": jnp.concatenate([dg_mix0, dg_mix1], axis=0),
        "norm_mlp_g": jnp.concatenate([dg_mlp0, dg_mlp1], axis=0),
        "final_norm_g": dg_final[0], "gm_ln_g": dmix["ln_g"], "gm_ln_b": dmix["ln_b"],
        "gm_w_s": dmix["wm"][None], "gm_b_s": dmix["bs_t"][:, :N_BLK].T[None],
        "ssm_conv_b": dmix["conv_b"], "ssm_dt_bias": dmix["dt_bias"][:, :SSM_HEADS],
        "ssm_a_log": dmix["a_log"][:, :SSM_HEADS], "ssm_d": dmix["d_heads"][:, :SSM_HEADS],
        "ssm_norm_g": dmix["norm_g"], "attn_sinks": dsink[:, :SSM_HEADS],
        "ssm_conv_w": dmix["conv_w"][None, :4], "b_qkv": db_qkv, "b_o": db_o,
    }
    return loss, dx, reduced, small_grads


def kernel(x, norm_mix_g, norm_mlp_g, final_norm_g, w_in_even, w_out_even, gm_ln_g, gm_ln_b, gm_w_s, gm_b_s, ssm_conv_w, ssm_conv_b, ssm_dt_bias, ssm_a_log, ssm_d, ssm_norm_g, w_qkv, b_qkv, w_o, b_o, attn_sinks, w_up, w_down, loss_target, m_norm_mix_g, m_norm_mlp_g, m_final_norm_g, m_w_in_even, m_w_out_even, m_gm_ln_g, m_gm_ln_b, m_gm_w_s, m_gm_b_s, m_ssm_conv_w, m_ssm_conv_b, m_ssm_dt_bias, m_ssm_a_log, m_ssm_d, m_ssm_norm_g, m_w_qkv, m_b_qkv, m_w_o, m_b_o, m_attn_sinks, m_w_up, m_w_down, v_norm_mix_g, v_norm_mlp_g, v_final_norm_g, v_w_in_even, v_w_out_even, v_gm_ln_g, v_gm_ln_b, v_gm_w_s, v_gm_b_s, v_ssm_conv_w, v_ssm_conv_b, v_ssm_dt_bias, v_ssm_a_log, v_ssm_d, v_ssm_norm_g, v_w_qkv, v_b_qkv, v_w_o, v_b_o, v_attn_sinks, v_w_up, v_w_down):
    w = dict(norm_mix_g=norm_mix_g, norm_mlp_g=norm_mlp_g, final_norm_g=final_norm_g, w_in_even=w_in_even,
             w_out_even=w_out_even, gm_ln_g=gm_ln_g, gm_ln_b=gm_ln_b, gm_w_s=gm_w_s, gm_b_s=gm_b_s,
             ssm_conv_w=ssm_conv_w, ssm_conv_b=ssm_conv_b, ssm_dt_bias=ssm_dt_bias, ssm_a_log=ssm_a_log,
             ssm_d=ssm_d, ssm_norm_g=ssm_norm_g, w_qkv=w_qkv, b_qkv=b_qkv, w_o=w_o, b_o=b_o,
             attn_sinks=attn_sinks, w_up=w_up, w_down=w_down)
    m = dict(norm_mix_g=m_norm_mix_g, norm_mlp_g=m_norm_mlp_g, final_norm_g=m_final_norm_g,
             w_in_even=m_w_in_even, w_out_even=m_w_out_even, gm_ln_g=m_gm_ln_g, gm_ln_b=m_gm_ln_b,
             gm_w_s=m_gm_w_s, gm_b_s=m_gm_b_s, ssm_conv_w=m_ssm_conv_w, ssm_conv_b=m_ssm_conv_b,
             ssm_dt_bias=m_ssm_dt_bias, ssm_a_log=m_ssm_a_log, ssm_d=m_ssm_d, ssm_norm_g=m_ssm_norm_g,
             w_qkv=m_w_qkv, b_qkv=m_b_qkv, w_o=m_w_o, b_o=m_b_o, attn_sinks=m_attn_sinks, w_up=m_w_up,
             w_down=m_w_down)
    v = dict(norm_mix_g=v_norm_mix_g, norm_mlp_g=v_norm_mlp_g, final_norm_g=v_final_norm_g,
             w_in_even=v_w_in_even, w_out_even=v_w_out_even, gm_ln_g=v_gm_ln_g, gm_ln_b=v_gm_ln_b,
             gm_w_s=v_gm_w_s, gm_b_s=v_gm_b_s, ssm_conv_w=v_ssm_conv_w, ssm_conv_b=v_ssm_conv_b,
             ssm_dt_bias=v_ssm_dt_bias, ssm_a_log=v_ssm_a_log, ssm_d=v_ssm_d, ssm_norm_g=v_ssm_norm_g,
             w_qkv=v_w_qkv, b_qkv=v_b_qkv, w_o=v_w_o, b_o=v_b_o, attn_sinks=v_attn_sinks, w_up=v_w_up,
             w_down=v_w_down)
    names = ("norm_mix_g", "norm_mlp_g", "final_norm_g", "w_in_even", "w_out_even", "gm_ln_g", "gm_ln_b",
             "gm_w_s", "gm_b_s", "ssm_conv_w", "ssm_conv_b", "ssm_dt_bias", "ssm_a_log", "ssm_d", "ssm_norm_g",
             "w_qkv", "b_qkv", "w_o", "b_o", "attn_sinks", "w_up", "w_down")

    cx, cy, cc = lax.axis_index("x"), lax.axis_index("y"), lax.axis_index("c")
    chip = 2 * cx + cy
    c_idx = jnp.reshape(cc, (1,)).astype(jnp.int32)
    chip_idx = jnp.reshape(chip, (1,)).astype(jnp.int32)

    weights = WeightGatherer(w, chip_idx)
    sm = {n: w[n] for n, _ in _SMALL_SHAPES[:_N_REPLICATED]}

    reducer = GradReducer(c_idx, jnp.concatenate([chip_idx, c_idx]))
    loss_part, dx, grads, small_grads = _local_step(x[0], loss_target[0], weights, sm, reducer)

    small_sum = allreduce_small(_pack([small_grads[n] for n, _ in _SMALL_SHAPES] + [loss_part], SMALL_ROWS))
    *small_list, loss_row = _unpack(small_sum, [s for _, s in _SMALL_SHAPES] + [loss_part.shape])
    loss = loss_row[0, 0]
    small_full = dict(zip([n for n, _ in _SMALL_SHAPES], small_list))
    for n, _ in _SMALL_SHAPES[:_N_REPLICATED]:
        grads[n] = small_full[n]
    for n, axis, width in _SHARDED_SMALL:
        grads[n] = lax.dynamic_slice_in_dim(small_full[n], chip * width, width, axis)

    delta, new_m, new_v = {}, {}, {}
    for n in names:
        if isinstance(grads[n], list):
            outs = None
            for layer, (buf, row_off) in enumerate(grads[n]):
                outs = adamw_rows(w[n], buf, m[n], v[n], layer, row_off, f"adamw_{n}{layer}", into=outs)
            grads[n], delta[n], new_m[n], new_v[n] = outs
            continue
        grads[n] = grads[n].reshape(w[n].shape)
        if n in _COLUMN_SHARDED:
            args = [jnp.transpose(d[n], (2, 0, 1)) for d in (w, grads, m, v)]
            grads[n] = jnp.transpose(args[1], (1, 2, 0))
            outs = adamw(*args, f"adamw_{n}")
            delta[n], new_m[n], new_v[n] = (jnp.transpose(o, (1, 2, 0)) for o in outs)
            continue
        shape = (1,) + w[n].shape if w[n].ndim == 1 else w[n].shape
        outs = adamw(*[d[n].reshape(shape) for d in (w, grads, m, v)], f"adamw_{n}")
        delta[n], new_m[n], new_v[n] = (o.reshape(w[n].shape) for o in outs)

    return (loss, dx[None], *[grads[n] for n in names], *[delta[n] for n in names],
            *[new_m[n] for n in names], *[new_v[n] for n in names])
```

```python
import functools

import jax
import jax.numpy as jnp
from jax import lax
from jax.experimental import pallas as pl
from jax.experimental.pallas import tpu as pltpu

f32 = jnp.float32
bf16 = jnp.bfloat16
MXU_DTYPE = bf16

RMS_EPS = 1e-5
LN_EPS = 1e-5
D_MODEL = 1024
D_FF = 4096
CH = 128
N_BLK = 8
SSM_HEADS = 16
IN_EVEN = 5136
NP_IN = 5376
OFF_U, OFF_V, OFF_Z, OFF_X, OFF_DT = 0, 1024, 2048, 3072, 5120
XBC_BLKS = 16
QKV_DIM = 1280
ATT_SCALE = 64 ** -0.5

ADAM_LR = 0.001
ADAM_B1 = 0.9
ADAM_B2 = 0.999
ADAM_EPS = 1e-08
ADAM_WD = 0.01
ADAM_STEP = 10

VMEM_LIMIT_BYTES = 48 * 1024 * 1024
N_CHIPS = 4
SMALL_ROWS = 256

NN = ((1,), (0,))
NT = ((1,), (1,))
TN = ((0,), (0,))


def _mm(a, b, dims):
    return lax.dot_general(a.astype(MXU_DTYPE), b.astype(MXU_DTYPE), (dims, ((), ())),
                           preferred_element_type=f32)


def _mm_exact(a, b):
    return jnp.dot(a, b, preferred_element_type=f32, precision=lax.Precision.HIGHEST)


def _cparams(sem=None):
    return pltpu.CompilerParams(dimension_semantics=sem, vmem_limit_bytes=VMEM_LIMIT_BYTES)


@jax.custom_vjp
def _swap64(x):
    return pltpu.roll(x, 64, axis=1)


_swap64.defvjp(lambda x: (pltpu.roll(x, 64, axis=1), None), lambda _, g: (pltpu.roll(g, 64, axis=1),))


def _row_blocks_of(x):
    return tuple(x[i:i + CH] for i in range(0, x.shape[0], CH))


@jax.custom_vjp
def _row_blocks(x):
    return _row_blocks_of(x)


_row_blocks.defvjp(lambda x: (_row_blocks_of(x), None), lambda _, gs: (jnp.concatenate(gs, axis=0),))


def _make_delay(k):
    @jax.custom_vjp
    def delay(ext):
        return pltpu.roll(ext, k, axis=0)[8:, :]

    def fwd(ext):
        return delay(ext), None

    def bwd(_, g):
        gp = jnp.concatenate([jnp.zeros((8, g.shape[1]), g.dtype), g], axis=0)
        return (pltpu.roll(gp, gp.shape[0] - k, axis=0),)

    delay.defvjp(fwd, bwd)
    return delay


_DELAYS = {k: _make_delay(k) for k in (1, 2, 3)}


_GELU_C = 0.7978845608028654
_GELU_K = 0.044715


@jax.custom_vjp
def _gelu(x):
    return 0.5 * x * (1.0 + jnp.tanh(_GELU_C * (x + _GELU_K * (x * x * x))))


def _gelu_fwd(x):
    t = jnp.tanh(_GELU_C * (x + _GELU_K * (x * x * x)))
    return 0.5 * x * (1.0 + t), (x, t)


def _gelu_bwd(res, g):
    x, t = res
    dz = _GELU_C + (3.0 * _GELU_C * _GELU_K) * (x * x)
    return (g * (0.5 * (1.0 + t) + (0.5 * x) * (1.0 - t * t) * dz),)


_gelu.defvjp(_gelu_fwd, _gelu_bwd)


def _col(m, lane, h):
    return jnp.sum(jnp.where(lane == h, m, 0.0), axis=1, keepdims=True)


@functools.lru_cache(maxsize=None)
def _row_picker(h, shape):
    @jax.custom_vjp
    def pick(m):
        return m[h:h + 1, :]

    def bwd(_, g):
        return (jnp.where(lax.broadcasted_iota(jnp.int32, shape, 0) == h, g, 0.0),)

    pick.defvjp(lambda m: (m[h:h + 1, :], None), bwd)
    return pick


def _row(m, sub, h):
    return _row_picker(h, m.shape)(m)


def _mixer_chunk(us, vs, zs, xbcs, halos, dtblk, hps, prm):
    lane = lax.broadcasted_iota(jnp.int32, (CH, CH), 1)
    sub = lax.broadcasted_iota(jnp.int32, (CH, CH), 0)
    left = lane < 64
    top = sub < 64
    causal = sub >= lane

    gus = [_gelu(u) for u in us]
    gvs = [_gelu(v) for v in vs]
    mu = sum(jnp.sum(g, axis=1, keepdims=True) for g in gvs) / D_MODEL
    cen = [g - mu for g in gvs]
    var = sum(jnp.sum(c * c, axis=1, keepdims=True) for c in cen) / D_MODEL
    rstd = lax.rsqrt(var + LN_EPS)
    a_out = []
    for g in range(N_BLK):
        vn = cen[g] * rstd * prm["ln_g"][g] + prm["ln_b"][g]
        w = jnp.where(causal, prm["wm"][g], 0.0)
        mixed = _mm(w, vn, NN) + _col(prm["bs_t"], lane, g)
        a_out.append(gus[g] * mixed)

    act = []
    for b in range(XBC_BLKS):
        w8 = prm["conv_w"][b]
        sub8 = lax.broadcasted_iota(jnp.int32, w8.shape, 0)
        ext = jnp.concatenate([halos[b], xbcs[b]], axis=0)
        conv = xbcs[b] * _row(w8, sub8, 3) + prm["conv_b"][b]
        for k in (1, 2, 3):
            conv = conv + _DELAYS[k](ext) * _row(w8, sub8, 3 - k)
        act.append(jax.nn.silu(conv))

    dt = jax.nn.softplus(dtblk + prm["dt_bias"])
    a_neg = -jnp.exp(prm["a_log"])
    tri = causal.astype(f32)
    acum = _mm_exact(tri, dt * a_neg)
    acum_t = acum.T
    dt_t = dt.T
    last = sub == CH - 1
    ys, h_out = [], []
    for grp in range(4):
        bm = act[8 + grp]
        cm = act[12 + grp]
        cb = _mm(cm, bm, NT)
        for p in (2 * grp, 2 * grp + 1):
            h0, h1 = 2 * p, 2 * p + 1
            xp = act[p]
            hp = hps[p]
            wis = []
            for h in (h0, h1):
                seg = _col(acum, lane, h) - _row(acum_t, sub, h)
                decay = jnp.exp(jnp.where(causal, seg, -jnp.inf))
                wis.append(cb * decay * _row(dt_t, sub, h))
            wcat = jnp.concatenate(wis, axis=1)
            xbd = jnp.concatenate([jnp.where(left, xp, 0.0), jnp.where(left, 0.0, xp)], axis=0)
            y_diag = _mm(wcat, xbd, NN)
            a_end = [jnp.sum(jnp.where(last & (lane == h), acum, 0.0), keepdims=True) for h in (h0, h1)]
            a_col = jnp.where(left, _col(acum, lane, h0), _col(acum, lane, h1))
            dt_col = jnp.where(left, _col(dt, lane, h0), _col(dt, lane, h1))
            to_end = jnp.exp(jnp.where(left, a_end[0], a_end[1]) - a_col) * dt_col
            states = _mm(xp * to_end, bm, TN)
            chunk_decay = jnp.where(top, jnp.exp(a_end[0]), jnp.exp(a_end[1]))
            h_out.append(chunk_decay * hp + states)
            y_off = jnp.exp(a_col) * _mm(cm, hp, NT)
            d_skip = jnp.where(left[:1], _col(prm["d_heads"], lane[:1], h0), _col(prm["d_heads"], lane[:1], h1))
            ys.append((y_diag + y_off + xp * d_skip) * jax.nn.silu(zs[p]))

    b_out = []
    for grp in range(4):
        pair = (ys[2 * grp], ys[2 * grp + 1])
        ms = sum(jnp.sum(y * y, axis=1, keepdims=True) for y in pair) / 256.0
        r = lax.rsqrt(ms + RMS_EPS)
        for j, y in enumerate(pair):
            b_out.append(y * r * prm["norm_g"][2 * grp + j])
    return a_out, b_out, h_out


def _attn_block(qps, kprev, kcur, vprev, vcur, sink_row, first):
    lane = lax.broadcasted_iota(jnp.int32, (CH, CH), 1)
    left = lane < 64
    own = lane <= lax.broadcasted_iota(jnp.int32, (CH, CH), 0)
    own8 = jnp.concatenate([own] * N_BLK, axis=0)

    def both_halves(a):
        sw = _swap64(a)
        return [jnp.where(left, a, sw), jnp.where(left, sw, a)]

    kc, kp, vc, vp = both_halves(kcur), both_halves(kprev), both_halves(vcur), both_halves(vprev)
    outs = []
    for j in range(2):
        q8 = jnp.concatenate([part for p in range(4 * j, 4 * j + 4)
                              for part in (jnp.where(left, qps[p], 0.0), jnp.where(left, 0.0, qps[p]))], axis=0)
        s_cur = _row_blocks(_mm(q8, kc[j], NT))
        s_prev = _row_blocks(_mm(q8, kp[j], NT))
        probs = []
        for h in range(N_BLK):
            s = jnp.where(own, s_cur[h] * ATT_SCALE, jnp.where(first, -jnp.inf, s_prev[h] * ATT_SCALE))
            sink = _col(sink_row, lane[:1], N_BLK * j + h)
            m = lax.stop_gradient(jnp.maximum(jnp.max(s, axis=1, keepdims=True), sink))
            pexp = jnp.exp(s - m)
            probs.append(pexp / (jnp.sum(pexp, axis=1, keepdims=True) + jnp.exp(sink - m)))
        p8 = jnp.concatenate(probs, axis=0)
        o = _row_blocks(_mm(jnp.where(own8, p8, 0.0), vc[j], NN) + _mm(jnp.where(own8, 0.0, p8), vp[j], NN))
        for t in range(4):
            outs.append(jnp.where(left, o[2 * t], o[2 * t + 1]))
    return outs


def _rmsnorm(x, g):
    r = lax.rsqrt(jnp.mean(x * x, axis=-1, keepdims=True) + RMS_EPS)
    return x * r * g


def rmsnorm_fwd(x, g_row, name):
    s, d = x.shape
    tm = min(512, s)

    def body(x_ref, g_ref, y_ref):
        y_ref[...] = _rmsnorm(x_ref[...], g_ref[...]).astype(bf16)

    return pl.pallas_call(
        body, name=name, grid=(s // tm,),
        in_specs=[pl.BlockSpec((tm, d), lambda i: (i, 0)), pl.BlockSpec((1, d), lambda i: (0, 0))],
        out_specs=pl.BlockSpec((tm, d), lambda i: (i, 0)),
        out_shape=jax.ShapeDtypeStruct((s, d), bf16),
        compiler_params=_cparams(("parallel",)),
    )(x, g_row)


def _fit(dim, want):
    if dim <= want:
        return dim
    t = want
    while dim % t:
        t -= 128
    return t


def matmul(a, b, *, dims, name, out_dtype=f32, tm=1024, tn=512, tk=8192, a_pro=None, epi=None, epi_args=(),
           out_by_col_tile=False, after=None):
    if dims == "nn" and b.ndim == 3:
        (m, k), n, tn = a.shape, b.shape[0] * b.shape[2], b.shape[2]
    elif dims == "nn":
        (m, k), n = a.shape, b.shape[1]
    elif dims == "nt":
        (m, k), n = a.shape, b.shape[0]
    else:
        (k, m), n = a.shape, b.shape[1]
    tm, tn, tk = _fit(m, tm), _fit(n, tn), _fit(k, tk)
    nk = k // tk
    if dims == "nn":
        a_spec = pl.BlockSpec((tm, tk), lambda i, j, kk: (i, kk))
        b_spec = (pl.BlockSpec((None, tk, tn), lambda i, j, kk: (j, kk, 0)) if b.ndim == 3
                  else pl.BlockSpec((tk, tn), lambda i, j, kk: (kk, j)))
        dn = NN
    elif dims == "nt":
        a_spec = pl.BlockSpec((tm, tk), lambda i, j, kk: (i, kk))
        b_spec = pl.BlockSpec((tn, tk), lambda i, j, kk: (j, kk))
        dn = NT
    else:
        a_spec = pl.BlockSpec((tk, tm), lambda i, j, kk: (kk, i))
        b_spec = pl.BlockSpec((tk, tn), lambda i, j, kk: (kk, j))
        dn = TN
    e_specs = [pl.BlockSpec((tm, tn), lambda i, j, kk: (i, j)) if kind == "tile"
               else pl.BlockSpec((1, tn), lambda i, j, kk: (0, j)) for kind, _ in epi_args]
    n_epi = len(epi_args)
    order_specs = [] if after is None else [pl.BlockSpec((8, 128), lambda i, j, kk: (0, 0))]
    order_args = [] if after is None else [after]

    def body(*refs):
        a_ref, b_ref = refs[0], refs[1]
        e_refs = refs[2:2 + n_epi]
        n_in = 2 + n_epi + len(order_args)
        o_ref = refs[n_in]
        av = a_ref[...]
        if a_pro is not None:
            av = a_pro(av)
        part = _mm(av, b_ref[...], dn)

        def finish(acc):
            if epi is not None:
                acc = epi(acc, *[r[...] for r in e_refs])
            o_ref[...] = acc.astype(out_dtype)

        if nk == 1:
            finish(part)
        else:
            acc_ref = refs[n_in + 1]
            kk = pl.program_id(2)

            @pl.when(kk == 0)
            def _():
                acc_ref[...] = part

            @pl.when(kk > 0)
            def _():
                acc_ref[...] += part

            @pl.when(kk == nk - 1)
            def _():
                finish(acc_ref[...])

    if out_by_col_tile:
        out_spec = pl.BlockSpec((None, tm, tn), lambda i, j, kk: (j, i, 0))
        out_shape = jax.ShapeDtypeStruct((n // tn, m, tn), out_dtype)
    else:
        out_spec = pl.BlockSpec((tm, tn), lambda i, j, kk: (i, j))
        out_shape = jax.ShapeDtypeStruct((m, n), out_dtype)
    return pl.pallas_call(
        body, name=name, grid=(m // tm, n // tn, nk),
        in_specs=[a_spec, b_spec] + e_specs + order_specs,
        out_specs=out_spec,
        out_shape=out_shape,
        scratch_shapes=[pltpu.VMEM((tm, tn), f32)] if nk > 1 else [],
        compiler_params=_cparams(("parallel", "parallel", "arbitrary")),
    )(a, b, *[arr for _, arr in epi_args], *order_args)


def _relu2(a):
    r = jnp.maximum(a.astype(f32), 0.0)
    return r * r


def _add(acc, t):
    return acc + t


def _add_bias(acc, t):
    return acc + t


def _add_bias_res(acc, bias, res):
    return acc + bias + res


def _times_relu2_grad(acc, a):
    return acc * (2.0 * jnp.maximum(a.astype(f32), 0.0))


def matmul_rows(a, b, *, dims, name, epi, epi_args, outs, tm=512, a_pro=None, after=None):
    m, k = a.shape
    n = b.shape[-1] if dims == "nn" else b.shape[-2]
    tm = _fit(m, tm)
    dn = NN if dims == "nn" else NT
    e_specs = [pl.BlockSpec((tm, arr.shape[1]), lambda i: (i, 0)) if kind == "tile"
               else pl.BlockSpec((1, arr.shape[1]), lambda i: (0, 0)) for kind, arr in epi_args]
    order_specs = [] if after is None else [pl.BlockSpec((8, 128), lambda i: (0, 0))]
    order_args = [] if after is None else [after]
    n_in = 2 + len(epi_args) + len(order_args)

    def body(*refs):
        av = refs[0][...]
        if a_pro is not None:
            av = a_pro(av)
        if b.ndim == 3:
            kb = b.shape[2]
            acc = sum(_mm(av[:, s * kb:(s + 1) * kb], refs[1][s], dn) for s in range(b.shape[0]))
        else:
            acc = _mm(av, refs[1][...], dn)
        vals = epi(acc, *[r[...] for r in refs[2:2 + len(epi_args)]])
        for (kind, _), o_ref, val in zip(outs, refs[n_in:], vals):
            if kind == "tile":
                o_ref[...] = val.astype(o_ref.dtype)
            else:
                @pl.when(pl.program_id(0) == 0)
                def _():
                    o_ref[...] = jnp.zeros_like(o_ref)

                o_ref[...] += val

    out_specs = [pl.BlockSpec((tm, n), lambda i: (i, 0)) if kind == "tile" else pl.BlockSpec((1, arg), lambda i: (0, 0))
                 for kind, arg in outs]
    out_shape = [jax.ShapeDtypeStruct((m, n), arg) if kind == "tile" else jax.ShapeDtypeStruct((1, arg), f32)
                 for kind, arg in outs]
    return pl.pallas_call(
        body, name=name, grid=(m // tm,),
        in_specs=[pl.BlockSpec((tm, k), lambda i: (i, 0)), pl.BlockSpec(b.shape, lambda i: (0,) * b.ndim)]
                 + e_specs + order_specs,
        out_specs=out_specs, out_shape=out_shape,
        compiler_params=_cparams(("arbitrary",)),
    )(a, b, *[arr for _, arr in epi_args], *order_args)


def _res_norm(acc, res, g):
    h = acc + res
    return h, _rmsnorm(h, g)


def _bias_res_norm(acc, bias, res, g):
    h = acc + bias + res
    return h, _rmsnorm(h, g)


def _res_norm_loss(acc, res, g, target):
    def f(h, gv):
        err = jnp.square(_rmsnorm(h, gv) - target)
        return 0.5 * jnp.sum(jnp.mean(err, axis=-1, keepdims=True), axis=0, keepdims=True)

    loss, vjp = jax.vjp(f, acc + res, g)
    dh, dg = vjp(jnp.ones_like(loss))
    return dh, dg, jnp.broadcast_to(loss, (1, 128))


def _norm_bwd_res_colsum(dy, x, g, res):
    dx, dg = _norm_bwd_res(dy, x, g, res)
    return dx, dg, jnp.sum(dx, axis=0, keepdims=True)


def _norm_bwd_res(dy, x, g, res):
    _, vjp = jax.vjp(_rmsnorm, x, g)
    dx, dg = vjp(dy)
    return res + dx, dg


_MIXER_PARAM_SHAPES = (
    ("ln_g", (1, D_MODEL)), ("ln_b", (1, D_MODEL)), ("wm", (N_BLK, CH, CH)), ("bs_t", (CH, CH)),
    ("conv_w", (8, 2048)), ("conv_b", (1, 2048)), ("dt_bias", (1, CH)), ("a_log", (1, CH)),
    ("d_heads", (1, CH)), ("norm_g", (1, D_MODEL)),
)


def _blocks(v, n, off=0):
    return [v[:, off + i * CH: off + (i + 1) * CH] for i in range(n)]


def _split_mixer_params(vals):
    p = dict(vals)
    return {
        "ln_g": _blocks(p["ln_g"], N_BLK), "ln_b": _blocks(p["ln_b"], N_BLK),
        "wm": [p["wm"][g] for g in range(N_BLK)], "bs_t": p["bs_t"],
        "conv_w": _blocks(p["conv_w"], XBC_BLKS), "conv_b": _blocks(p["conv_b"], XBC_BLKS),
        "dt_bias": p["dt_bias"], "a_log": p["a_log"], "d_heads": p["d_heads"],
        "norm_g": _blocks(p["norm_g"], N_BLK),
    }


def _mixer_leaves(proj_ref, halo_ref, keep_halo):
    pv = proj_ref
    us = [pv[:, OFF_U + i * CH: OFF_U + (i + 1) * CH] for i in range(N_BLK)]
    vs = [pv[:, OFF_V + i * CH: OFF_V + (i + 1) * CH] for i in range(N_BLK)]
    zs = [pv[:, OFF_Z + i * CH: OFF_Z + (i + 1) * CH] for i in range(N_BLK)]
    xbcs = [pv[:, OFF_X + i * CH: OFF_X + (i + 1) * CH] for i in range(XBC_BLKS)]
    halos = [halo_ref[:, OFF_X + i * CH: OFF_X + (i + 1) * CH] * keep_halo for i in range(XBC_BLKS)]
    dtblk = pv[:, OFF_DT: OFF_DT + CH]
    return us, vs, zs, xbcs, halos, dtblk


def mixer_fwd(proj, prm):
    s = proj.shape[0]
    nc = s // CH
    names = [n for n, _ in _MIXER_PARAM_SHAPES]

    def body(proj_ref, halo_ref, *rest):
        p_refs = rest[:len(names)]
        ab_ref, hs_ref, h_ref = rest[len(names):]
        c = pl.program_id(0)

        @pl.when(c == 0)
        def _():
            h_ref[...] = jnp.zeros_like(h_ref)

        hs_ref[...] = h_ref[...]
        keep = (c > 0).astype(f32)
        us, vs, zs, xbcs, halos, dtblk = _mixer_leaves(proj_ref, halo_ref, keep)
        hps = [h_ref[i * CH:(i + 1) * CH, :] for i in range(N_BLK)]
        p = _split_mixer_params({n: r[...] for n, r in zip(names, p_refs)})
        a_out, b_out, h_out = _mixer_chunk(us, vs, zs, xbcs, halos, dtblk, hps, p)
        for i in range(N_BLK):
            ab_ref[:, i * CH:(i + 1) * CH] = a_out[i].astype(bf16)
            ab_ref[:, D_MODEL + i * CH: D_MODEL + (i + 1) * CH] = b_out[i].astype(bf16)
            h_ref[i * CH:(i + 1) * CH, :] = h_out[i]

    def const(shape):
        return pl.BlockSpec(shape, lambda c: (0,) * len(shape))

    return pl.pallas_call(
        body, name="mixer_fwd", grid=(nc,),
        in_specs=[pl.BlockSpec((CH, NP_IN), lambda c: (c, 0)),
                  pl.BlockSpec((8, NP_IN), lambda c: (jnp.maximum(c * (CH // 8) - 1, 0), 0))]
                 + [const(shp) for _, shp in _MIXER_PARAM_SHAPES],
        out_specs=[pl.BlockSpec((CH, 2 * D_MODEL), lambda c: (c, 0)),
                   pl.BlockSpec((None, D_MODEL, CH), lambda c: (c, 0, 0))],
        out_shape=[jax.ShapeDtypeStruct((s, 2 * D_MODEL), bf16), jax.ShapeDtypeStruct((nc, D_MODEL, CH), f32)],
        scratch_shapes=[pltpu.VMEM((D_MODEL, CH), f32)],
        compiler_params=_cparams(("arbitrary",)),
    )(proj, proj, *[prm[n] for n in names])


def mixer_bwd(proj, hstates, dab, prm):
    s = proj.shape[0]
    nc = s // CH
    names = [n for n, _ in _MIXER_PARAM_SHAPES]
    npar = len(names)

    def body(proj_ref, halo_ref, hs_ref, dab_ref, *rest):
        p_refs = rest[:npar]
        dproj_ref = rest[npar]
        g_refs = rest[npar + 1: 2 * npar + 1]
        dh_ref, dhalo_ref = rest[2 * npar + 1:]
        i = pl.program_id(0)
        c = nc - 1 - i

        @pl.when(i == 0)
        def _():
            dh_ref[...] = jnp.zeros_like(dh_ref)
            dhalo_ref[...] = jnp.zeros_like(dhalo_ref)
            for r in g_refs:
                r[...] = jnp.zeros_like(r)

        keep = (c > 0).astype(f32)
        us, vs, zs, xbcs, halos, dtblk = _mixer_leaves(proj_ref, halo_ref, keep)
        hps = [hs_ref[j * CH:(j + 1) * CH, :] for j in range(N_BLK)]
        pvals = {n: r[...] for n, r in zip(names, p_refs)}

        def fn(us, vs, zs, xbcs, halos, dtblk, hps, pvals):
            return _mixer_chunk(us, vs, zs, xbcs, halos, dtblk, hps, _split_mixer_params(pvals))

        _, vjp = jax.vjp(fn, us, vs, zs, xbcs, halos, dtblk, hps, pvals)
        da = [dab_ref[:, j * CH:(j + 1) * CH].astype(f32) for j in range(N_BLK)]
        db = [dab_ref[:, D_MODEL + j * CH: D_MODEL + (j + 1) * CH].astype(f32) for j in range(N_BLK)]
        dh = [dh_ref[j * CH:(j + 1) * CH, :] for j in range(N_BLK)]
        dus, dvs, dzs, dxbcs, dhalos, ddt, dhps, dp = vjp((da, db, dh))

        for j in range(N_BLK):
            dproj_ref[:, OFF_U + j * CH: OFF_U + (j + 1) * CH] = dus[j].astype(bf16)
            dproj_ref[:, OFF_V + j * CH: OFF_V + (j + 1) * CH] = dvs[j].astype(bf16)
            dproj_ref[:, OFF_Z + j * CH: OFF_Z + (j + 1) * CH] = dzs[j].astype(bf16)
            dh_ref[j * CH:(j + 1) * CH, :] = dhps[j]
        zeros_top = jnp.zeros((CH - 8, CH), f32)
        for j in range(XBC_BLKS):
            late = jnp.concatenate([zeros_top, dhalo_ref[:, j * CH:(j + 1) * CH]], axis=0)
            dproj_ref[:, OFF_X + j * CH: OFF_X + (j + 1) * CH] = (dxbcs[j] + late).astype(bf16)
        for j in range(XBC_BLKS):
            dhalo_ref[:, j * CH:(j + 1) * CH] = dhalos[j] * keep
        lane = lax.broadcasted_iota(jnp.int32, (CH, CH), 1)
        dproj_ref[:, OFF_DT: OFF_DT + CH] = jnp.where(lane < SSM_HEADS, ddt, 0.0).astype(bf16)
        dproj_ref[:, OFF_DT + CH:] = jnp.zeros((CH, NP_IN - OFF_DT - CH), bf16)
        for n, r in zip(names, g_refs):
            r[...] += dp[n]

    def const(shape):
        return pl.BlockSpec(shape, lambda i: (0,) * len(shape))

    outs = pl.pallas_call(
        body, name="mixer_bwd", grid=(nc,),
        in_specs=[pl.BlockSpec((CH, NP_IN), lambda i: (nc - 1 - i, 0)),
                  pl.BlockSpec((8, NP_IN), lambda i: (jnp.maximum((nc - 1 - i) * (CH // 8) - 1, 0), 0)),
                  pl.BlockSpec((None, D_MODEL, CH), lambda i: (nc - 1 - i, 0, 0)),
                  pl.BlockSpec((CH, 2 * D_MODEL), lambda i: (nc - 1 - i, 0))]
                 + [const(shp) for _, shp in _MIXER_PARAM_SHAPES],
        out_specs=[pl.BlockSpec((CH, NP_IN), lambda i: (nc - 1 - i, 0))]
                  + [const(shp) for _, shp in _MIXER_PARAM_SHAPES],
        out_shape=[jax.ShapeDtypeStruct((s, NP_IN), bf16)]
                  + [jax.ShapeDtypeStruct(shp, f32) for _, shp in _MIXER_PARAM_SHAPES],
        scratch_shapes=[pltpu.VMEM((D_MODEL, CH), f32), pltpu.VMEM((8, 2048), f32)],
        compiler_params=_cparams(("arbitrary",)),
    )(proj, proj, hstates, dab, *[prm[n] for n in names])
    return outs[0], dict(zip(names, outs[1:]))


_K_BLK = D_MODEL // CH
_V_BLK = _K_BLK + 1


def _attn_specs(rev, nb):
    def blk(i):
        return nb - 1 - i if rev else i

    q_spec = pl.BlockSpec((CH, D_MODEL), lambda i: (blk(i), 0))
    kv = lambda col, prev: pl.BlockSpec(
        (CH, CH), lambda i: (jnp.maximum(blk(i) - 1, 0) if prev else blk(i), col))
    return q_spec, [kv(_K_BLK, True), kv(_K_BLK, False), kv(_V_BLK, True), kv(_V_BLK, False)]


def attn_fwd(qkv, sink_row):
    s = qkv.shape[0]
    nb = s // CH

    def body(q_ref, kp_ref, kc_ref, vp_ref, vc_ref, sink_ref, o_ref):
        qps = [q_ref[:, p * CH:(p + 1) * CH] for p in range(N_BLK)]
        outs = _attn_block(qps, kp_ref[...], kc_ref[...], vp_ref[...], vc_ref[...], sink_ref[...],
                           pl.program_id(0) == 0)
        for p in range(N_BLK):
            o_ref[:, p * CH:(p + 1) * CH] = outs[p].astype(bf16)

    q_spec, kv_specs = _attn_specs(False, nb)
    return pl.pallas_call(
        body, name="attn_fwd", grid=(nb,),
        in_specs=[q_spec] + kv_specs + [pl.BlockSpec((1, CH), lambda i: (0, 0))],
        out_specs=pl.BlockSpec((CH, D_MODEL), lambda i: (i, 0)),
        out_shape=jax.ShapeDtypeStruct((s, D_MODEL), bf16),
        compiler_params=_cparams(("parallel",)),
    )(qkv, qkv, qkv, qkv, qkv, sink_row)


def attn_bwd(qkv, sink_row, dout):
    s = qkv.shape[0]
    nb = s // CH

    def body(q_ref, kp_ref, kc_ref, vp_ref, vc_ref, sink_ref, do_ref, dqkv_ref, dsink_ref, db_ref, carry_ref):
        i = pl.program_id(0)
        blk = nb - 1 - i

        @pl.when(i == 0)
        def _():
            dsink_ref[...] = jnp.zeros_like(dsink_ref)
            db_ref[...] = jnp.zeros_like(db_ref)
            carry_ref[...] = jnp.zeros_like(carry_ref)

        qps = [q_ref[:, p * CH:(p + 1) * CH] for p in range(N_BLK)]
        first = blk == 0
        _, vjp = jax.vjp(lambda *a: _attn_block(*a, first), qps, kp_ref[...], kc_ref[...], vp_ref[...],
                         vc_ref[...], sink_ref[...])
        dos = [do_ref[:, p * CH:(p + 1) * CH].astype(f32) for p in range(N_BLK)]
        dqs, dkp, dkc, dvp, dvc, dsink = vjp(dos)
        blocks = list(dqs) + [dkc + carry_ref[0], dvc + carry_ref[1]]
        for p, val in enumerate(blocks):
            dqkv_ref[:, p * CH:(p + 1) * CH] = val.astype(bf16)
            db_ref[:, p * CH:(p + 1) * CH] += jnp.sum(val, axis=0, keepdims=True)
        keep = jnp.logical_not(first).astype(f32)
        carry_ref[0] = dkp * keep
        carry_ref[1] = dvp * keep
        dsink_ref[...] += dsink

    q_spec, kv_specs = _attn_specs(True, nb)
    return pl.pallas_call(
        body, name="attn_bwd", grid=(nb,),
        in_specs=[q_spec] + kv_specs + [pl.BlockSpec((1, CH), lambda i: (0, 0)),
                                        pl.BlockSpec((CH, D_MODEL), lambda i: (nb - 1 - i, 0))],
        out_specs=[pl.BlockSpec((CH, QKV_DIM), lambda i: (nb - 1 - i, 0)), pl.BlockSpec((1, CH), lambda i: (0, 0)),
                   pl.BlockSpec((1, QKV_DIM), lambda i: (0, 0))],
        out_shape=[jax.ShapeDtypeStruct((s, QKV_DIM), bf16), jax.ShapeDtypeStruct((1, CH), f32),
                   jax.ShapeDtypeStruct((1, QKV_DIM), f32)],
        scratch_shapes=[pltpu.VMEM((2, CH, CH), f32)],
        compiler_params=_cparams(("arbitrary",)),
    )(qkv, qkv, qkv, qkv, qkv, sink_row, dout)


def _adamw_update(w, g, m, v):
    nm = ADAM_B1 * m + (1.0 - ADAM_B1) * g
    nv = ADAM_B2 * v + (1.0 - ADAM_B2) * jnp.square(g)
    m_hat = nm / (1.0 - ADAM_B1 ** ADAM_STEP)
    v_hat = nv / (1.0 - ADAM_B2 ** ADAM_STEP)
    return -ADAM_LR * (m_hat / (jnp.sqrt(v_hat) + ADAM_EPS) + ADAM_WD * w), nm, nv


def adamw_rows(w, r, m, v, layer, row_off, name, into=None):
    rows, cols = w.shape[1], w.shape[2]
    tr = 256
    assert rows % tr == 0 and row_off % tr == 0

    def body(w_ref, r_ref, m_ref, v_ref, *rest):
        g_ref, d_ref, nm_ref, nv_ref = rest[-4:]
        g = r_ref[...]
        g_ref[...] = g
        d_ref[...], nm_ref[...], nv_ref[...] = _adamw_update(w_ref[...], g, m_ref[...], v_ref[...])

    tile = pl.BlockSpec((None, tr, cols), lambda i: (layer, i, 0))
    extra = [] if into is None else list(into)
    return pl.pallas_call(
        body, name=name, grid=(rows // tr,),
        in_specs=[tile, pl.BlockSpec((tr, cols), lambda i: (row_off // tr + i, 0)), tile, tile] + [_ANY] * len(extra),
        out_specs=[tile] * 4, out_shape=[jax.ShapeDtypeStruct(w.shape, f32)] * 4,
        input_output_aliases={4 + k: k for k in range(len(extra))},
        compiler_params=_cparams(("parallel",)),
    )(w, r, m, v, *extra)


def adamw(w, g, m, v, name):
    def body(w_ref, g_ref, m_ref, v_ref, d_ref, nm_ref, nv_ref):
        d_ref[...], nm_ref[...], nv_ref[...] = _adamw_update(w_ref[...], g_ref[...], m_ref[...], v_ref[...])

    out_shape = [jax.ShapeDtypeStruct(w.shape, f32)] * 3
    if w.ndim == 3 and w.shape[1] == 1:
        tr = max(t for t in range(1, 129) if w.shape[0] % t == 0)
        tile = pl.BlockSpec((tr, 1, w.shape[2]), lambda i: (i, 0, 0))
        return pl.pallas_call(
            body, name=name, grid=(w.shape[0] // tr,),
            in_specs=[tile] * 4, out_specs=[tile] * 3, out_shape=out_shape,
            compiler_params=_cparams(("parallel",)),
        )(w, g, m, v)
    if w.ndim == 3 and w.shape[1] % 256 == 0:
        tile = pl.BlockSpec((None, 256, w.shape[2]), lambda l, i: (l, i, 0))
        return pl.pallas_call(
            body, name=name, grid=(w.shape[0], w.shape[1] // 256),
            in_specs=[tile] * 4, out_specs=[tile] * 3, out_shape=out_shape,
            compiler_params=_cparams(("parallel", "parallel")),
        )(w, g, m, v)
    return pl.pallas_call(body, name=name, in_specs=[_VMEM] * 4, out_specs=[_VMEM] * 3, out_shape=out_shape,
                          compiler_params=_cparams())(w, g, m, v)


_MESH = pl.DeviceIdType.MESH
_ANY = pl.BlockSpec(memory_space=pl.ANY)
_VMEM = pl.BlockSpec(memory_space=pltpu.VMEM)


def _place():
    x, y, c = lax.axis_index("x"), lax.axis_index("y"), lax.axis_index("c")
    chips = [(1 - x, y), (x, 1 - y), (1 - x, 1 - y)]
    return x, y, c, 2 * x + y, chips, [2 * cx + cy for cx, cy in chips]


def _half(c, rows):
    return pl.ds(pl.multiple_of(c * (rows // 2), 16), rows // 2)


def _step_rows(rows):
    return max(t for t in range(16, 641, 16) if rows % t == 0)


def place_shard(b, slot, name, dtype=bf16, after=None, cols=None):
    r, c_in = b.shape
    c = c_in if cols is None else cols
    tr = _step_rows(r)

    def body(slot_ref, b_ref, *rest):
        o_ref = rest[-1]
        if c > c_in:
            whole = (c_in // 128) * 128
            o_ref[:, whole:] = jnp.zeros((tr, c - whole), dtype)
        o_ref[:, :c_in] = b_ref[...].astype(dtype)

    order_specs = [] if after is None else [pl.BlockSpec((8, 128), lambda i, s: (0, 0))]
    return pl.pallas_call(
        body, name=name,
        grid_spec=pltpu.PrefetchScalarGridSpec(
            num_scalar_prefetch=1, grid=(r // tr,),
            in_specs=[pl.BlockSpec((tr, c_in), lambda i, s: (i, 0))] + order_specs,
            out_specs=pl.BlockSpec((None, tr, c), lambda i, s: (s[0], i, 0))),
        out_shape=jax.ShapeDtypeStruct((N_CHIPS, r, c), dtype),
        compiler_params=_cparams(("parallel",)),
    )(slot, b, *([] if after is None else [after]))


_HBM = pl.BlockSpec(memory_space=pltpu.HBM)
_SEM = pl.BlockSpec(memory_space=pltpu.SEMAPHORE)
_EFFECT = pltpu.SideEffectType.DATAFLOW_SIDE_EFFECTING


def _gather_ici_copies(bufs, send_sems, recv_sems):
    x, y, c, me, chips, chip_idx = _place()
    return [pltpu.make_async_remote_copy(
        src_ref=buf.at[me, _half(c, buf.shape[1])], dst_ref=buf.at[chip_idx[j], _half(c, buf.shape[1])],
        send_sem=send_sems.at[3 * k + j], recv_sem=recv_sems.at[3 * k + j],
        device_id=(*chips[j], c), device_id_type=_MESH) for j in range(3) for k, buf in enumerate(bufs)]


def gather_start(groups, tag):
    sizes = [len(g) for g in groups]
    flat = [b for g in groups for b in g]
    n = len(flat)

    def body(*refs):
        bufs, sems = refs[:n], refs[n:n + 2 * len(groups)]
        refs[-1][...] = jnp.zeros_like(refs[-1])
        x, y, c, me, chips, chip_idx = _place()
        lo = 0
        for gi, size in enumerate(sizes):
            for j in range(3):
                for k, buf in enumerate(bufs[lo:lo + size]):
                    mine = buf.at[me, _half(c, buf.shape[1])]
                    pltpu.make_async_remote_copy(
                        src_ref=mine, dst_ref=mine, send_sem=sems[2 * gi].at[3 * k + j],
                        recv_sem=sems[2 * gi + 1].at[3 * k + j], device_id=(*chips[j], c),
                        device_id_type=_MESH).start()
            lo += size

    sem_shapes = [pltpu.SemaphoreType.DMA((3 * size,)) for size in sizes for _ in range(2)]
    outs = pl.pallas_call(
        body, name=f"gather_start_{tag}",
        out_shape=(*sem_shapes, *[pltpu.HBM(b.shape, b.dtype) for b in flat], jax.ShapeDtypeStruct((8, 128), f32)),
        in_specs=[_HBM] * n, out_specs=(*[_SEM] * len(sem_shapes), *[_HBM] * n, _VMEM),
        input_output_aliases={i: len(sem_shapes) + i for i in range(n)},
        compiler_params=pltpu.CompilerParams(has_side_effects=_EFFECT),
    )(*[pltpu.with_memory_space_constraint(b, pltpu.HBM) for b in flat])
    sems = [(outs[2 * gi], outs[2 * gi + 1]) for gi in range(len(groups))]
    thru, lo = [], len(sem_shapes)
    for size in sizes:
        thru.append(list(outs[lo:lo + size]))
        lo += size
    return sems, thru, outs[-1]


def gather_wait(bufs, sems, after, tag):
    n = len(bufs)

    def body(*refs):
        for cp in _gather_ici_copies(refs[:n], refs[n], refs[n + 1]):
            cp.wait_send()
            cp.wait_recv()

    extra = list(after)
    return list(pl.pallas_call(
        body, name=f"gather_wait_{tag}",
        out_shape=[pltpu.HBM(b.shape, b.dtype) for b in bufs],
        in_specs=[_HBM] * n + [_SEM, _SEM] + [_ANY] * len(extra), out_specs=[_HBM] * n,
        input_output_aliases={i: i for i in range(n)},
        compiler_params=pltpu.CompilerParams(has_side_effects=_EFFECT),
    )(*bufs, *sems, *extra))


def gather_forward(bufs, tag):
    n = len(bufs)

    def body(*refs):
        out_refs = refs[n:2 * n]
        send_sems, recv_sems = refs[2 * n:]
        x, y, c, me, chips, chip_idx = _place()

        def copy(k, j, half):
            part = out_refs[k].at[chip_idx[j], _half(half, out_refs[k].shape[1])]
            return pltpu.make_async_remote_copy(
                src_ref=part, dst_ref=part, send_sem=send_sems.at[3 * k + j], recv_sem=recv_sems.at[3 * k + j],
                device_id=(x, y, 1 - c), device_id_type=_MESH)

        sends = [copy(k, j, c) for j in range(3) for k in range(n)]
        for cp in sends:
            cp.start()
        for j in range(3):
            for k in range(n):
                copy(k, j, 1 - c).wait_recv()
        for cp in sends:
            cp.wait_send()

    return list(pl.pallas_call(
        body, name=f"gather_forward_{tag}",
        out_shape=[jax.ShapeDtypeStruct(b.shape, b.dtype) for b in bufs],
        in_specs=[_ANY] * n, out_specs=[_ANY] * n, input_output_aliases={i: i for i in range(n)},
        scratch_shapes=[pltpu.SemaphoreType.DMA((3 * n,)), pltpu.SemaphoreType.DMA((3 * n,))],
    )(*bufs))


def _forward_copy(ref, k, j, half, send_sems, recv_sems):
    x, y, c, me, chips, chip_idx = _place()
    part = ref.at[chip_idx[j], _half(half, ref.shape[1])]
    return pltpu.make_async_remote_copy(
        src_ref=part, dst_ref=part, send_sem=send_sems.at[3 * k + j], recv_sem=recv_sems.at[3 * k + j],
        device_id=(x, y, 1 - c), device_id_type=_MESH)


def forward_start(bufs, tag):
    n = len(bufs)

    def body(*refs):
        c = _place()[2]
        for j in range(3):
            for k in range(n):
                _forward_copy(refs[k], k, j, c, refs[n], refs[n + 1]).start()
        refs[-1][...] = jnp.zeros_like(refs[-1])

    outs = pl.pallas_call(
        body, name=f"forward_start_{tag}",
        out_shape=(pltpu.SemaphoreType.DMA((3 * n,)), pltpu.SemaphoreType.DMA((3 * n,)),
                   *[pltpu.HBM(b.shape, b.dtype) for b in bufs], jax.ShapeDtypeStruct((8, 128), f32)),
        in_specs=[_HBM] * n, out_specs=(_SEM, _SEM, *[_HBM] * n, _VMEM),
        input_output_aliases={i: 2 + i for i in range(n)},
        compiler_params=pltpu.CompilerParams(has_side_effects=_EFFECT),
    )(*[pltpu.with_memory_space_constraint(b, pltpu.HBM) for b in bufs])
    return (outs[0], outs[1], list(outs[2:2 + n])), outs[-1]


def forward_wait(send_sems, recv_sems, bufs, after, tag):
    n = len(bufs)

    def body(*refs):
        c = _place()[2]
        for j in range(3):
            for k in range(n):
                _forward_copy(refs[k], k, j, c, refs[n], refs[n + 1]).wait_send()
                _forward_copy(refs[k], k, j, 1 - c, refs[n], refs[n + 1]).wait_recv()

    return list(pl.pallas_call(
        body, name=f"forward_wait_{tag}",
        out_shape=[pltpu.HBM(b.shape, b.dtype) for b in bufs],
        in_specs=[_HBM] * n + [_SEM, _SEM, _ANY], out_specs=[_HBM] * n,
        input_output_aliases={i: i for i in range(n)},
        compiler_params=pltpu.CompilerParams(has_side_effects=_EFFECT),
    )(*bufs, send_sems, recv_sems, after))


def exchange_halves(bufs, tag):
    n = len(bufs)

    def body(*refs):
        g_refs, out_refs = refs[:n], refs[n:2 * n]
        send_sems, recv_sems = refs[2 * n:]
        x, y, c, *_ = _place()
        cps = [pltpu.make_async_remote_copy(
            src_ref=g_refs[b].at[:, _half(1 - c, g_refs[b].shape[1])], dst_ref=out_refs[b],
            send_sem=send_sems.at[b], recv_sem=recv_sems.at[b], device_id=(x, y, 1 - c), device_id_type=_MESH)
            for b in range(n)]
        for cp in cps:
            cp.start()
        for cp in cps:
            cp.wait()

    return pl.pallas_call(
        body, name=f"exchange_halves_{tag}",
        out_shape=[jax.ShapeDtypeStruct((N_CHIPS, b.shape[1] // 2, b.shape[2]), b.dtype) for b in bufs],
        in_specs=[_ANY] * n, out_specs=[_ANY] * n,
        scratch_shapes=[pltpu.SemaphoreType.DMA((n,)), pltpu.SemaphoreType.DMA((n,))],
    )(*bufs)


def add_halves(g, got, c_idx, name):
    hr, cols = got.shape[1], got.shape[2]
    tr = _step_rows(hr)
    steps = hr // tr

    def body(c_ref, g_ref, got_ref, o_ref):
        o_ref[...] = (g_ref[...].astype(f32) + got_ref[...].astype(f32)).astype(bf16)

    return pl.pallas_call(
        body, name=name,
        grid_spec=pltpu.PrefetchScalarGridSpec(
            num_scalar_prefetch=1, grid=(N_CHIPS, steps),
            in_specs=[pl.BlockSpec((None, tr, cols), lambda s, i, c: (s, c[0] * steps + i, 0)),
                      pl.BlockSpec((None, tr, cols), lambda s, i, c: (s, i, 0))],
            out_specs=pl.BlockSpec((None, tr, cols), lambda s, i, c: (s, i, 0))),
        out_shape=jax.ShapeDtypeStruct(got.shape, bf16),
        compiler_params=_cparams(("parallel", "parallel")),
    )(c_idx, g, got)


def sum_chips(t, got, place_idx, name):
    hr, cols = t.shape[1], t.shape[2]
    tr = _step_rows(hr)
    steps = hr // tr

    def body(idx_ref, t_ref, got_ref, o_ref):
        acc = t_ref[...].astype(f32)
        for j in range(3):
            acc = acc + got_ref[j].astype(f32)
        o_ref[...] = acc

    return pl.pallas_call(
        body, name=name,
        grid_spec=pltpu.PrefetchScalarGridSpec(
            num_scalar_prefetch=1, grid=(steps,),
            in_specs=[pl.BlockSpec((None, tr, cols), lambda i, idx: (idx[0], i, 0)),
                      pl.BlockSpec((3, tr, cols), lambda i, idx: (0, i, 0))],
            out_specs=pl.BlockSpec((tr, cols), lambda i, idx: (idx[1] * steps + i, 0))),
        out_shape=jax.ShapeDtypeStruct((2 * hr, cols), f32),
        compiler_params=_cparams(("parallel",)),
    )(place_idx, t, got)


def _share_copies(refs, send_sems, recv_sems):
    x, y, c, *_ = _place()
    return [pltpu.make_async_remote_copy(
        src_ref=ref.at[_half(c, ref.shape[0])], dst_ref=ref.at[_half(c, ref.shape[0])], send_sem=send_sems.at[b],
        recv_sem=recv_sems.at[b], device_id=(x, y, 1 - c), device_id_type=_MESH) for b, ref in enumerate(refs)]


def share_start(bufs, tag):
    n = len(bufs)

    def body(*refs):
        for cp in _share_copies(refs[:n], refs[n], refs[n + 1]):
            cp.start()
        token = refs[-1]
        token[...] = jnp.zeros_like(token)

    outs = pl.pallas_call(
        body, name=f"share_start_{tag}",
        out_shape=(pltpu.SemaphoreType.DMA((n,)), pltpu.SemaphoreType.DMA((n,)),
                   *[pltpu.HBM(b.shape, b.dtype) for b in bufs], jax.ShapeDtypeStruct((8, 128), f32)),
        in_specs=[_HBM] * n, out_specs=(_SEM, _SEM, *[_HBM] * n, _VMEM),
        input_output_aliases={i: 2 + i for i in range(n)},
        compiler_params=pltpu.CompilerParams(has_side_effects=_EFFECT),
    )(*[pltpu.with_memory_space_constraint(b, pltpu.HBM) for b in bufs])
    return (outs[0], outs[1], list(outs[2:2 + n])), outs[-1]


def share_wait(send_sems, recv_sems, bufs, after, tag):
    n = len(bufs)

    def body(*refs):
        x, y, c, *_ = _place()
        for b, ref in enumerate(refs[:n]):
            cp = pltpu.make_async_remote_copy(
                src_ref=ref.at[_half(c, ref.shape[0])], dst_ref=ref.at[_half(1 - c, ref.shape[0])],
                send_sem=refs[n].at[b], recv_sem=refs[n + 1].at[b], device_id=(x, y, 1 - c), device_id_type=_MESH)
            cp.wait_send()
            cp.wait_recv()

    return list(pl.pallas_call(
        body, name=f"share_wait_{tag}",
        out_shape=[pltpu.HBM(b.shape, b.dtype) for b in bufs],
        in_specs=[_HBM] * n + [_SEM, _SEM, _ANY], out_specs=[_HBM] * n,
        input_output_aliases={i: i for i in range(n)},
        compiler_params=pltpu.CompilerParams(has_side_effects=_EFFECT),
    )(*bufs, send_sems, recv_sems, after))


def _scatter_copies(t_refs, land_refs, send_sems, recv_sems):
    x, y, c, me, chips, chip_idx = _place()
    return [pltpu.make_async_remote_copy(
        src_ref=t_refs[b].at[chip_idx[j]], dst_ref=land_refs[b].at[j], send_sem=send_sems.at[3 * b + j],
        recv_sem=recv_sems.at[3 * b + j], device_id=(*chips[j], c), device_id_type=_MESH)
        for j in range(3) for b in range(len(t_refs))]


def scatter_start(ts, tag):
    n = len(ts)
    lands = [lax.empty((3,) + t.shape[1:], t.dtype) for t in ts]

    def body(*refs):
        for cp in _scatter_copies(refs[:n], refs[n:2 * n], refs[2 * n], refs[2 * n + 1]):
            cp.start()
        token = refs[-1]
        token[...] = jnp.zeros_like(token)

    hbm = [pltpu.HBM(a.shape, a.dtype) for a in (*ts, *lands)]
    outs = pl.pallas_call(
        body, name=f"scatter_start_{tag}",
        out_shape=(pltpu.SemaphoreType.DMA((3 * n,)), pltpu.SemaphoreType.DMA((3 * n,)), *hbm,
                   jax.ShapeDtypeStruct((8, 128), f32)),
        in_specs=[_HBM] * (2 * n), out_specs=(_SEM, _SEM, *[_HBM] * (2 * n), _VMEM),
        input_output_aliases={i: 2 + i for i in range(2 * n)},
        compiler_params=pltpu.CompilerParams(has_side_effects=_EFFECT),
    )(*[pltpu.with_memory_space_constraint(a, pltpu.HBM) for a in (*ts, *lands)])
    return outs[0], outs[1], list(outs[2:2 + n]), list(outs[2 + n:2 + 2 * n]), outs[-1]


def scatter_wait(send_sems, recv_sems, ts, lands, after, tag):
    n = len(ts)

    def body(*refs):
        for cp in _scatter_copies(refs[:n], refs[n:2 * n], refs[2 * n], refs[2 * n + 1]):
            cp.wait_send()
            cp.wait_recv()

    outs = pl.pallas_call(
        body, name=f"scatter_wait_{tag}",
        out_shape=[pltpu.HBM(a.shape, a.dtype) for a in (*ts, *lands)],
        in_specs=[_HBM] * (2 * n) + [_SEM, _SEM, _ANY], out_specs=[_HBM] * (2 * n),
        input_output_aliases={i: i for i in range(2 * n)},
        compiler_params=pltpu.CompilerParams(has_side_effects=_EFFECT),
    )(*ts, *lands, send_sems, recv_sems, after)
    return list(outs[:n]), list(outs[n:])


N_SENDERS = 7


def _direct_copies(g_refs, land_refs, send_sems, recv_sems):
    x, y, c, me, chips, chip_idx = _place()
    cps = []
    for b, (g, land) in enumerate(zip(g_refs, land_refs)):
        rows, base = g.shape[1], N_SENDERS * b
        cps.append(pltpu.make_async_remote_copy(
            src_ref=g.at[me, _half(1 - c, rows)], dst_ref=land.at[0], send_sem=send_sems.at[base],
            recv_sem=recv_sems.at[base], device_id=(x, y, 1 - c), device_id_type=_MESH))
        for j in range(3):
            for core in range(2):
                cps.append(pltpu.make_async_remote_copy(
                    src_ref=g.at[chip_idx[j], _half(core, rows)], dst_ref=land.at[1 + 2 * j + c],
                    send_sem=send_sems.at[base + 1 + 2 * j + core], recv_sem=recv_sems.at[base + 1 + 2 * j + c],
                    device_id=(*chips[j], core), device_id_type=_MESH))
    return cps


def direct_start(gs, tag):
    n = len(gs)
    lands = [lax.empty((N_SENDERS, g.shape[1] // 2, g.shape[2]), g.dtype) for g in gs]

    def body(*refs):
        for cp in _direct_copies(refs[:n], refs[n:2 * n], refs[2 * n], refs[2 * n + 1]):
            cp.start()
        token = refs[-1]
        token[...] = jnp.zeros_like(token)

    hbm = [pltpu.HBM(a.shape, a.dtype) for a in (*gs, *lands)]
    outs = pl.pallas_call(
        body, name=f"direct_start_{tag}",
        out_shape=(pltpu.SemaphoreType.DMA((N_SENDERS * n,)), pltpu.SemaphoreType.DMA((N_SENDERS * n,)), *hbm,
                   jax.ShapeDtypeStruct((8, 128), f32)),
        in_specs=[_HBM] * (2 * n), out_specs=(_SEM, _SEM, *[_HBM] * (2 * n), _VMEM),
        input_output_aliases={i: 2 + i for i in range(2 * n)},
        compiler_params=pltpu.CompilerParams(has_side_effects=_EFFECT),
    )(*[pltpu.with_memory_space_constraint(a, pltpu.HBM) for a in (*gs, *lands)])
    return outs[0], outs[1], list(outs[2:2 + n]), list(outs[2 + n:2 + 2 * n]), outs[-1]


def direct_wait(send_sems, recv_sems, gs, lands, after, tag):
    n = len(gs)

    def body(*refs):
        g_refs, land_refs, sends, recvs = refs[:n], refs[n:2 * n], refs[2 * n], refs[2 * n + 1]
        for b in range(n):
            for k in range(N_SENDERS):
                cp = pltpu.make_async_remote_copy(
                    src_ref=g_refs[b].at[0, _half(0, g_refs[b].shape[1])], dst_ref=land_refs[b].at[k],
                    send_sem=sends.at[N_SENDERS * b + k], recv_sem=recvs.at[N_SENDERS * b + k],
                    device_id=_place()[:3], device_id_type=_MESH)
                cp.wait_send()
                cp.wait_recv()

    outs = pl.pallas_call(
        body, name=f"direct_wait_{tag}",
        out_shape=[pltpu.HBM(a.shape, a.dtype) for a in (*gs, *lands)],
        in_specs=[_HBM] * (2 * n) + [_SEM, _SEM, _ANY], out_specs=[_HBM] * (2 * n),
        input_output_aliases={i: i for i in range(2 * n)},
        compiler_params=pltpu.CompilerParams(has_side_effects=_EFFECT),
    )(*gs, *lands, send_sems, recv_sems, after)
    return list(outs[:n]), list(outs[n:])


def sum_senders(g, lands, place_idx, name):
    hr, cols = lands.shape[1], lands.shape[2]
    tr = _step_rows(hr)
    steps = hr // tr

    def body(idx_ref, g_ref, land_ref, o_ref):
        acc = g_ref[...].astype(f32)
        for k in range(N_SENDERS):
            acc = acc + land_ref[k].astype(f32)
        o_ref[...] = acc

    return pl.pallas_call(
        body, name=name,
        grid_spec=pltpu.PrefetchScalarGridSpec(
            num_scalar_prefetch=1, grid=(steps,),
            in_specs=[pl.BlockSpec((None, tr, cols), lambda i, idx: (idx[0], idx[1] * steps + i, 0)),
                      pl.BlockSpec((N_SENDERS, tr, cols), lambda i, idx: (0, i, 0))],
            out_specs=pl.BlockSpec((tr, cols), lambda i, idx: (idx[1] * steps + i, 0))),
        out_shape=jax.ShapeDtypeStruct((2 * hr, cols), f32),
        compiler_params=_cparams(("parallel",)),
    )(place_idx, g, lands)


class GradReducer:
    def __init__(self, c_idx, place_idx):
        self.c_idx, self.place_idx = c_idx, place_idx

    def start(self, bufs, tag, direct=False):
        if direct:
            send_sems, recv_sems, gs, lands, token = direct_start(bufs, tag)
            return (True, send_sems, recv_sems, gs, lands), token
        got = exchange_halves(bufs, tag)
        ts = [add_halves(b, g, self.c_idx, f"add_halves_{tag}{i}") for i, (b, g) in enumerate(zip(bufs, got))]
        send_sems, recv_sems, ts, lands, token = scatter_start(ts, tag)
        return (False, send_sems, recv_sems, ts, lands), token

    def finish(self, state, after, tag):
        direct, *flight = state
        if direct:
            gs, lands = direct_wait(*flight, after, tag)
            sums = [sum_senders(g, l, self.place_idx, f"sum_senders_{tag}{i}") for i, (g, l) in enumerate(zip(gs, lands))]
        else:
            ts, lands = scatter_wait(*flight, after, tag)
            sums = [sum_chips(t, l, self.place_idx, f"sum_chips_{tag}{i}") for i, (t, l) in enumerate(zip(ts, lands))]
        return share_start(sums, tag)

    def collect(self, pending, after, tag):
        return share_wait(*pending, after, tag)


def allreduce_small(sp):
    rows = sp.shape[0]
    hr = rows // 2

    def body(s_ref, out_ref, sib_ref, chip_ref, four_ref, send_sems, recv_sems):
        x, y, c, me, chips, chip_idx = _place()
        sibling = (x, y, 1 - c)
        mine = pl.ds(pl.multiple_of(c * hr, 8), hr)
        other = pl.ds(pl.multiple_of((1 - c) * hr, 8), hr)

        swap = pltpu.make_async_remote_copy(src_ref=s_ref, dst_ref=sib_ref, send_sem=send_sems.at[0],
                                            recv_sem=recv_sems.at[0], device_id=sibling, device_id_type=_MESH)
        swap.start()
        swap.wait()
        is_core0 = c == 0
        chip_ref[...] = jnp.where(is_core0, s_ref[...], sib_ref[...]) + jnp.where(is_core0, sib_ref[...], s_ref[...])

        sends = [pltpu.make_async_remote_copy(
            src_ref=chip_ref.at[mine], dst_ref=four_ref.at[me], send_sem=send_sems.at[1 + j],
            recv_sem=recv_sems.at[1 + j], device_id=(*chips[j], c), device_id_type=_MESH) for j in range(3)]
        for cp in sends:
            cp.start()
        four_ref[me] = chip_ref[mine, :]
        for j in range(3):
            pltpu.make_async_remote_copy(
                src_ref=chip_ref.at[mine], dst_ref=four_ref.at[chip_idx[j]], send_sem=send_sems.at[1 + j],
                recv_sem=recv_sems.at[1 + j], device_id=(*chips[j], c), device_id_type=_MESH).wait_recv()
        for cp in sends:
            cp.wait_send()
        out_ref[mine, :] = (four_ref[0] + four_ref[1]) + (four_ref[2] + four_ref[3])

        share = pltpu.make_async_remote_copy(src_ref=out_ref.at[mine], dst_ref=out_ref.at[mine], send_sem=send_sems.at[4],
                                             recv_sem=recv_sems.at[4], device_id=sibling, device_id_type=_MESH)
        share.start()
        pltpu.make_async_remote_copy(src_ref=out_ref.at[mine], dst_ref=out_ref.at[other], send_sem=send_sems.at[4],
                                     recv_sem=recv_sems.at[4], device_id=sibling, device_id_type=_MESH).wait_recv()
        share.wait_send()

    return pl.pallas_call(
        body, name="allreduce_small",
        out_shape=jax.ShapeDtypeStruct(sp.shape, sp.dtype),
        in_specs=[_VMEM], out_specs=_VMEM,
        scratch_shapes=[pltpu.VMEM(sp.shape, sp.dtype), pltpu.VMEM(sp.shape, sp.dtype),
                        pltpu.VMEM((N_CHIPS, hr, sp.shape[1]), sp.dtype),
                        pltpu.SemaphoreType.DMA((5,)), pltpu.SemaphoreType.DMA((5,))],
        compiler_params=_cparams(),
    )(sp)


def _n_rows(shape):
    n = 1
    for d in shape:
        n *= d
    return 8 * (-(-n // 8192))


def _pack(arrays, total_rows):
    parts = []
    for a in arrays:
        flat = a.reshape(-1)
        parts.append(jnp.pad(flat, (0, 1024 * _n_rows(a.shape) - flat.shape[0])).reshape(-1, 1024))
    rows = jnp.concatenate(parts, axis=0)
    return jnp.pad(rows, ((0, total_rows - rows.shape[0]), (0, 0)))


def _unpack(packed, shapes):
    out, r = [], 0
    for shp in shapes:
        n = 1
        for d in shp:
            n *= d
        nr = _n_rows(shp)
        out.append(packed[r:r + nr].reshape(-1)[:n].reshape(shp))
        r += nr
    return out


_COLUMN_SHARDED = ("w_in_even", "w_qkv")
IN_SHARD, IN_PAD = 1284, 1408
QKV_SHARD, QKV_PAD = 320, 384


def _lane_padded(a, cols):
    return jnp.pad(a, ((0, 0), (0, cols - a.shape[1])))


_SMALL_SHAPES = (
    ("norm_mix_g", (2, 1024)), ("norm_mlp_g", (2, 1024)), ("final_norm_g", (1024,)), ("gm_ln_g", (1, 1024)),
    ("gm_ln_b", (1, 1024)), ("gm_w_s", (1, 8, 128, 128)), ("gm_b_s", (1, 8, 128)), ("ssm_conv_b", (1, 2048)),
    ("ssm_dt_bias", (1, 16)), ("ssm_a_log", (1, 16)), ("ssm_d", (1, 16)), ("ssm_norm_g", (1, 1024)),
    ("attn_sinks", (1, 16)), ("ssm_conv_w", (1, 4, 2048)), ("b_qkv", (1, 1280)), ("b_o", (1, 1024)),
)
_N_REPLICATED = 13
_SHARDED_SMALL = (("ssm_conv_w", 2, 512), ("b_qkv", 1, 320), ("b_o", 1, 256))
_SHARD_PACK_ROWS = 32


def _cols_by_owner(a):
    return a.transpose(1, 0, 2).reshape(a.shape[1], -1)


class WeightGatherer:
    def __init__(self, w, chip_idx):
        def place(tag, b, dtype=bf16, after=None, cols=None):
            return place_shard(b, chip_idx, f"place_shard_{tag}", dtype, after, cols)

        sems_in, bufs_in, self.started = gather_start([
            [place("in", w["w_in_even"][0].astype(bf16), cols=IN_PAD),
             place("small", _pack([w[n] for n, _, _ in _SHARDED_SMALL], _SHARD_PACK_ROWS), f32)]], "in")
        t = self.started
        sems, bufs, self.all_started = gather_start([
            [place("out", w["w_out_even"][0], after=t), place("up0", w["w_up"][0], after=t),
             place("down0", w["w_down"][0], after=t)],
            [place("qkv", w["w_qkv"][0], after=t, cols=QKV_PAD), place("o", w["w_o"][0], after=t),
             place("up1", w["w_up"][1], after=t), place("down1", w["w_down"][1], after=t)],
        ], "rest")
        self.sems, self.bufs = sems_in + sems, bufs_in + bufs

    def _group(self, gi, after, tag):
        return gather_forward(gather_wait(self.bufs[gi], self.sems[gi], after, tag), tag)

    def mixer_in(self, after):
        g, small = self._group(0, [after, self.all_started], "in")
        shard_shapes = [tuple(width if i == axis else d for i, d in enumerate(dict(_SMALL_SHAPES)[n]))
                        for n, axis, width in _SHARDED_SMALL]
        per_chip = [_unpack(small[s], shard_shapes) for s in range(N_CHIPS)]
        full = {n: jnp.concatenate([per_chip[s][i] for s in range(N_CHIPS)], axis=axis)
                for i, (n, axis, _) in enumerate(_SHARDED_SMALL)}
        w_in_p = jnp.concatenate([g[s, :, :IN_SHARD] for s in range(N_CHIPS)]
                                 + [jnp.zeros((g.shape[1], NP_IN - IN_EVEN), g.dtype)], axis=1)
        return w_in_p, full

    def layer0(self, after):
        w_out, w_up, w_down = self._group(1, [after], "l0")
        return w_out.reshape(2048, 1024), w_up, w_down.reshape(4096, 1024)

    def layer1_start(self, after):
        return forward_start(gather_wait(self.bufs[2], self.sems[2], [after], "l1"), "l1")

    def layer1(self, pending, after):
        q, w_o, w_up, w_down = forward_wait(*pending, after, "l1")
        w_qkv = jnp.concatenate([q[s, :, :QKV_SHARD] for s in range(N_CHIPS)], axis=1)
        return w_qkv, w_o.reshape(1024, 1024), w_up, w_down.reshape(4096, 1024)


def _row2(v):
    return v.reshape(1, -1)


def _lane_pad(v):
    return jnp.pad(v, ((0, 0), (0, CH - v.shape[1])))


_H_AND_NORM = (("tile", f32), ("tile", bf16))
_DX_AND_DG = (("tile", f32), ("sum", D_MODEL))


def _mlp_bwd(dh_out, h, g_row, y, a, w_up, w_down, tag, after=None):
    da = matmul(dh_out, w_down, dims="nt", name=f"mlp_da{tag}", out_dtype=bf16, tm=2048, tn=1024,
                epi=_times_relu2_grad, epi_args=(("tile", a),), after=after)
    dw_down = matmul(a, dh_out, dims="tn", name=f"mlp_dwdown{tag}", out_dtype=bf16, a_pro=_relu2)
    dw_up = matmul(y, da, dims="tn", name=f"mlp_dwup{tag}", out_dtype=bf16, tn=1024, out_by_col_tile=True)
    dh, dg, dh_colsum = matmul_rows(da, w_up, dims="nt", name=f"mlp_dy{tag}", epi=_norm_bwd_res_colsum,
                                    epi_args=(("tile", h), ("row", g_row), ("tile", dh_out)),
                                    outs=_DX_AND_DG + (("sum", D_MODEL),))
    return dh, dg, dw_up, dw_down, dh_colsum


def _by_owner(a):
    return a.reshape(N_CHIPS, a.shape[0] // N_CHIPS, a.shape[1])


def _row_shards(a, shard, padded):
    return jnp.stack([jnp.pad(a[shard * s: shard * (s + 1)], ((0, padded - shard), (0, 0))) for s in range(N_CHIPS)])


def _col_shards(a, shard, padded):
    return jnp.stack([_lane_padded(a[:, shard * s: shard * (s + 1)], padded) for s in range(N_CHIPS)])


def _local_step(x, target, weights, sm, reducer):
    w_up, w_down = [None, None], [None, None]
    mix_g = [_row2(sm["norm_mix_g"][i]) for i in range(2)]
    y0 = rmsnorm_fwd(x, mix_g[0] + weights.started[:1, :1], "mix_norm0")
    w_in_p, sharded_small = weights.mixer_in(y0)
    sm = {**sm, **sharded_small}
    mlp_g = [_row2(sm["norm_mlp_g"][i]) for i in range(2)]
    mixer_prm = {
        "ln_g": sm["gm_ln_g"], "ln_b": sm["gm_ln_b"], "wm": sm["gm_w_s"][0],
        "bs_t": jnp.pad(sm["gm_b_s"][0].T, ((0, 0), (0, CH - N_BLK))),
        "conv_w": jnp.pad(sm["ssm_conv_w"][0], ((0, 4), (0, 0))), "conv_b": sm["ssm_conv_b"],
        "dt_bias": _lane_pad(sm["ssm_dt_bias"]), "a_log": _lane_pad(sm["ssm_a_log"]),
        "d_heads": _lane_pad(sm["ssm_d"]), "norm_g": sm["ssm_norm_g"],
    }
    sink_row = _lane_pad(sm["attn_sinks"])

    proj = matmul(y0, w_in_p, dims="nn", name="in_proj", tm=2048, tn=768)
    ab, hstates = mixer_fwd(proj, mixer_prm)
    w_out, w_up[0], w_down[0] = weights.layer0(ab)
    h1, y1 = matmul_rows(ab, w_out, dims="nn", name="out_proj", epi=_res_norm,
                         epi_args=(("tile", x), ("row", mlp_g[0])), outs=_H_AND_NORM)
    a1 = matmul(y1, w_up[0], dims="nn", name="mlp_up0", out_dtype=bf16, tm=2048, tn=1024)
    pending_l1, forwarding_l1 = weights.layer1_start(a1)
    h2, y2 = matmul_rows(a1, w_down[0], dims="nn", name="mlp_down0", a_pro=_relu2, epi=_res_norm,
                         epi_args=(("tile", h1), ("row", mix_g[1])), outs=_H_AND_NORM, after=forwarding_l1)
    w_qkv, w_o, w_up[1], w_down[1] = weights.layer1(pending_l1, h2)
    qkv = matmul(y2, w_qkv, dims="nn", name="qkv_proj", tn=QKV_DIM, epi=_add_bias, epi_args=(("row", sm["b_qkv"]),))
    att = attn_fwd(qkv, sink_row)
    h3, y3 = matmul_rows(att, w_o, dims="nn", name="o_proj", epi=_bias_res_norm,
                         epi_args=(("row", sm["b_o"]), ("tile", h2), ("row", mlp_g[1])), outs=_H_AND_NORM)
    a3 = matmul(y3, w_up[1], dims="nn", name="mlp_up1", out_dtype=bf16, tm=2048, tn=1024)
    dh4, dg_final, loss = matmul_rows(
        a3, w_down[1], dims="nn", name="mlp_down1", a_pro=_relu2, epi=_res_norm_loss,
        epi_args=(("tile", h3), ("row", _row2(sm["final_norm_g"])), ("tile", target)),
        outs=(("tile", f32), ("sum", D_MODEL), ("sum", 128)))

    dh3, dg_mlp1, dw_up1, dw_down1, db_o = _mlp_bwd(dh4, h3, mlp_g[1], y3, a3, w_up[1], w_down[1], 1)
    datt = matmul(dh3, w_o, dims="nt", name="attn_dout", out_dtype=bf16, tm=2048)
    dw_o = matmul(att, dh3, dims="tn", name="dw_o", out_dtype=bf16)
    dqkv, dsink, db_qkv = attn_bwd(qkv, sink_row, datt)
    dw_qkv = matmul(y2, dqkv, dims="tn", name="dw_qkv", out_dtype=bf16, tn=QKV_DIM)
    dh2, dg_mix1 = matmul_rows(dqkv, w_qkv, dims="nt", name="dy_qkv", epi=_norm_bwd_res,
                               epi_args=(("tile", h2), ("row", mix_g[1]), ("tile", dh3)), outs=_DX_AND_DG)
    layer1 = [jnp.concatenate([dw_up1, _by_owner(dw_down1)], axis=1), _col_shards(dw_qkv, QKV_SHARD, QKV_PAD)]
    flight1, token1 = reducer.start(layer1, "l1", direct=True)
    dh1, dg_mlp0, dw_up0, dw_down0, _ = _mlp_bwd(dh2, h1, mlp_g[0], y1, a1, w_up[0], w_down[0], 0, after=token1)
    pending1, shared1 = reducer.finish(flight1, dh1, "l1")
    dw_out = matmul(ab, dh1, dims="tn", name="dw_out", out_dtype=bf16, after=shared1)
    flight0, token0 = reducer.start(
        [jnp.concatenate([dw_up0, _by_owner(dw_down0), _by_owner(dw_out), _by_owner(dw_o)], axis=1)], "l0", direct=True)
    dab = matmul(dh1, w_out, dims="nt", name="mixer_dout", tm=2048, tn=1024, after=token0)
    dproj, dmix = mixer_bwd(proj, hstates, dab, mixer_prm)
    dw_in_t = matmul(dproj, y0, dims="tn", name="dw_in", out_dtype=bf16, tm=768, tn=1024)
    pending0, shared0 = reducer.finish(flight0, dw_in_t, "l0")
    flight_in, token_in = reducer.start([_row_shards(dw_in_t, IN_SHARD, IN_PAD)], "in")
    dx, dg_mix0 = matmul_rows(dproj, w_in_p, dims="nt", name="dy_in", tm=256, epi=_norm_bwd_res,
                              epi_args=(("tile", x), ("row", mix_g[0]), ("tile", dh1)), outs=_DX_AND_DG,
                              after=token_in + shared0)
    pending_in, _ = reducer.finish(flight_in, dx, "in")
    r_l1, r_qkv = reducer.collect(pending1, dx, "l1")
    (r_l0,) = reducer.collect(pending0, dx, "l0")
    (r_in,) = reducer.collect(pending_in, dx, "in")
    reduced = {
        "w_out_even": [(r_l0, 2048)], "w_o": [(r_l0, 2560)], "w_up": [(r_l0, 0), (r_l1, 0)],
        "w_down": [(r_l0, 1024), (r_l1, 1024)],
        "w_in_even": r_in[:IN_SHARD].T[None], "w_qkv": r_qkv[None, :, :QKV_SHARD],
    }

    small_grads = {
        "norm_mix_g": jnp.concatenate([dg_mix0, dg_mix1], axis=0),
        "norm_mlp_g": jnp.concatenate([dg_mlp0, dg_mlp1], axis=0),
        "final_norm_g": dg_final[0], "gm_ln_g": dmix["ln_g"], "gm_ln_b": dmix["ln_b"],
        "gm_w_s": dmix["wm"][None], "gm_b_s": dmix["bs_t"][:, :N_BLK].T[None],
        "ssm_conv_b": dmix["conv_b"], "ssm_dt_bias": dmix["dt_bias"][:, :SSM_HEADS],
        "ssm_a_log": dmix["a_log"][:, :SSM_HEADS], "ssm_d": dmix["d_heads"][:, :SSM_HEADS],
        "ssm_norm_g": dmix["norm_g"], "attn_sinks": dsink[:, :SSM_HEADS],
        "ssm_conv_w": dmix["conv_w"][None, :4], "b_qkv": db_qkv, "b_o": db_o,
    }
    return loss, dx, reduced, small_grads


def kernel(x, norm_mix_g, norm_mlp_g, final_norm_g, w_in_even, w_out_even, gm_ln_g, gm_ln_b, gm_w_s, gm_b_s, ssm_conv_w, ssm_conv_b, ssm_dt_bias, ssm_a_log, ssm_d, ssm_norm_g, w_qkv, b_qkv, w_o, b_o, attn_sinks, w_up, w_down, loss_target, m_norm_mix_g, m_norm_mlp_g, m_final_norm_g, m_w_in_even, m_w_out_even, m_gm_ln_g, m_gm_ln_b, m_gm_w_s, m_gm_b_s, m_ssm_conv_w, m_ssm_conv_b, m_ssm_dt_bias, m_ssm_a_log, m_ssm_d, m_ssm_norm_g, m_w_qkv, m_b_qkv, m_w_o, m_b_o, m_attn_sinks, m_w_up, m_w_down, v_norm_mix_g, v_norm_mlp_g, v_final_norm_g, v_w_in_even, v_w_out_even, v_gm_ln_g, v_gm_ln_b, v_gm_w_s, v_gm_b_s, v_ssm_conv_w, v_ssm_conv_b, v_ssm_dt_bias, v_ssm_a_log, v_ssm_d, v_ssm_norm_g, v_w_qkv, v_b_qkv, v_w_o, v_b_o, v_attn_sinks, v_w_up, v_w_down):
    w = dict(norm_mix_g=norm_mix_g, norm_mlp_g=norm_mlp_g, final_norm_g=final_norm_g, w_in_even=w_in_even,
             w_out_even=w_out_even, gm_ln_g=gm_ln_g, gm_ln_b=gm_ln_b, gm_w_s=gm_w_s, gm_b_s=gm_b_s,
             ssm_conv_w=ssm_conv_w, ssm_conv_b=ssm_conv_b, ssm_dt_bias=ssm_dt_bias, ssm_a_log=ssm_a_log,
             ssm_d=ssm_d, ssm_norm_g=ssm_norm_g, w_qkv=w_qkv, b_qkv=b_qkv, w_o=w_o, b_o=b_o,
             attn_sinks=attn_sinks, w_up=w_up, w_down=w_down)
    m = dict(norm_mix_g=m_norm_mix_g, norm_mlp_g=m_norm_mlp_g, final_norm_g=m_final_norm_g,
             w_in_even=m_w_in_even, w_out_even=m_w_out_even, gm_ln_g=m_gm_ln_g, gm_ln_b=m_gm_ln_b,
             gm_w_s=m_gm_w_s, gm_b_s=m_gm_b_s, ssm_conv_w=m_ssm_conv_w, ssm_conv_b=m_ssm_conv_b,
             ssm_dt_bias=m_ssm_dt_bias, ssm_a_log=m_ssm_a_log, ssm_d=m_ssm_d, ssm_norm_g=m_ssm_norm_g,
             w_qkv=m_w_qkv, b_qkv=m_b_qkv, w_o=m_w_o, b_o=m_b_o, attn_sinks=m_attn_sinks, w_up=m_w_up,
             w_down=m_w_down)
    v = dict(norm_mix_g=v_norm_mix_g, norm_mlp_g=v_norm_mlp_g, final_norm_g=v_final_norm_g,
             w_in_even=v_w_in_even, w_out_even=v_w_out_even, gm_ln_g=v_gm_ln_g, gm_ln_b=v_gm_ln_b,
             gm_w_s=v_gm_w_s, gm_b_s=v_gm_b_s, ssm_conv_w=v_ssm_conv_w, ssm_conv_b=v_ssm_conv_b,
             ssm_dt_bias=v_ssm_dt_bias, ssm_a_log=v_ssm_a_log, ssm_d=v_ssm_d, ssm_norm_g=v_ssm_norm_g,
             w_qkv=v_w_qkv, b_qkv=v_b_qkv, w_o=v_w_o, b_o=v_b_o, attn_sinks=v_attn_sinks, w_up=v_w_up,
             w_down=v_w_down)
    names = ("norm_mix_g", "norm_mlp_g", "final_norm_g", "w_in_even", "w_out_even", "gm_ln_g", "gm_ln_b",
             "gm_w_s", "gm_b_s", "ssm_conv_w", "ssm_conv_b", "ssm_dt_bias", "ssm_a_log", "ssm_d", "ssm_norm_g",
             "w_qkv", "b_qkv", "w_o", "b_o", "attn_sinks", "w_up", "w_down")

    cx, cy, cc = lax.axis_index("x"), lax.axis_index("y"), lax.axis_index("c")
    chip = 2 * cx + cy
    c_idx = jnp.reshape(cc, (1,)).astype(jnp.int32)
    chip_idx = jnp.reshape(chip, (1,)).astype(jnp.int32)

    weights = WeightGatherer(w, chip_idx)
    sm = {n: w[n] for n, _ in _SMALL_SHAPES[:_N_REPLICATED]}

    reducer = GradReducer(c_idx, jnp.concatenate([chip_idx, c_idx]))
    loss_part, dx, grads, small_grads = _local_step(x[0], loss_target[0], weights, sm, reducer)

    small_sum = allreduce_small(_pack([small_grads[n] for n, _ in _SMALL_SHAPES] + [loss_part], SMALL_ROWS))
    *small_list, loss_row = _unpack(small_sum, [s for _, s in _SMALL_SHAPES] + [loss_part.shape])
    loss = loss_row[0, 0]
    small_full = dict(zip([n for n, _ in _SMALL_SHAPES], small_list))
    for n, _ in _SMALL_SHAPES[:_N_REPLICATED]:
        grads[n] = small_full[n]
    for n, axis, width in _SHARDED_SMALL:
        grads[n] = lax.dynamic_slice_in_dim(small_full[n], chip * width, width, axis)

    delta, new_m, new_v = {}, {}, {}
    for n in names:
        if isinstance(grads[n], list):
            outs = None
            for layer, (buf, row_off) in enumerate(grads[n]):
                outs = adamw_rows(w[n], buf, m[n], v[n], layer, row_off, f"adamw_{n}{layer}", into=outs)
            grads[n], delta[n], new_m[n], new_v[n] = outs
            continue
        grads[n] = grads[n].reshape(w[n].shape)
        if n in _COLUMN_SHARDED:
            args = [jnp.transpose(d[n], (2, 0, 1)) for d in (w, grads, m, v)]
            grads[n] = jnp.transpose(args[1], (1, 2, 0))
            outs = adamw(*args, f"adamw_{n}")
            delta[n], new_m[n], new_v[n] = (jnp.transpose(o, (1, 2, 0)) for o in outs)
            continue
        shape = (1,) + w[n].shape if w[n].ndim == 1 else w[n].shape
        outs = adamw(*[d[n].reshape(shape) for d in (w, grads, m, v)], f"adamw_{n}")
        delta[n], new_m[n], new_v[n] = (o.reshape(w[n].shape) for o in outs)

    return (loss, dx[None], *[grads[n] for n in names], *[delta[n] for n in names],
            *[new_m[n] for n in names], *[new_v[n] for n in names])
```
